```python
import jax, jax.numpy as jnp
from jax import lax
import numpy as np

D_MODEL = 2048
BATCH = 8
SEQ = 4096
DEPTH = 1

N_Q_HEADS = 32
N_KV_HEADS = 8
HEAD_DIM = 64
Q_PER_KV = N_Q_HEADS // N_KV_HEADS
WINDOW = 128
ATTN_BLOCK = 128
ROT_DIM = HEAD_DIM // 4
ROPE_THETA = 500000.0
SSD_HEADS = 32
SSD_HEAD_DIM = 64
SSD_INNER = SSD_HEADS * SSD_HEAD_DIM
SSD_GROUPS = 8
SSD_STATE = 128
SSD_CONV = 4
SSD_CHUNK = 128
HEADS_PER_GROUP = SSD_HEADS // SSD_GROUPS
ATTN_WIDTH = N_Q_HEADS * HEAD_DIM
KV_WIDTH = N_KV_HEADS * HEAD_DIM
MIX_WIDTH = ATTN_WIDTH + SSD_INNER
BC_WIDTH = SSD_GROUPS * SSD_STATE
CONV_CH = SSD_INNER + 2 * BC_WIDTH
IN_PROJ_WIDTH = ATTN_WIDTH + 2 * KV_WIDTH + SSD_INNER + CONV_CH + SSD_HEADS
D_FF = 5632
FFN_CONV = 3
EPS = 1e-6

kernel_name = 'hymba_swa_sink_ssd_convffn'


def _rmsnorm(x, g):
    xf = x.astype(jnp.float32)
    y = xf * lax.rsqrt(jnp.mean(xf * xf, axis=-1, keepdims=True) + EPS)
    return (y * g.astype(jnp.float32)).astype(x.dtype)


def _causal_dwconv(x, w, b):
    k_taps = w.shape[0]
    s = x.shape[1]
    xp = jnp.pad(x, ((0, 0), (k_taps - 1, 0), (0, 0)))
    out = b
    for t in range(k_taps):
        out = out + xp[:, t:t + s] * w[t]
    return out


def _partial_rope(x, pos):
    half = ROT_DIM // 2
    inv = 1.0 / (ROPE_THETA ** (jnp.arange(0, ROT_DIM, 2, dtype=jnp.float32) / ROT_DIM))
    ang = pos[:, None] * inv[None, :]
    cos = jnp.cos(ang)[None, :, None, :]
    sin = jnp.sin(ang)[None, :, None, :]
    xf = x.astype(jnp.float32)
    x1 = xf[..., :half]
    x2 = xf[..., half:ROT_DIM]
    out = jnp.concatenate([x1 * cos - x2 * sin, x2 * cos + x1 * sin, xf[..., ROT_DIM:]], axis=-1)
    return out.astype(x.dtype)


def _band_blocks(t):
    b, s, h, d = t.shape
    nb = s // ATTN_BLOCK
    tp = jnp.pad(t, ((0, 0), (ATTN_BLOCK, 0), (0, 0), (0, 0)))
    prev = tp[:, :s].reshape(b, nb, ATTN_BLOCK, h, d)
    cur = t.reshape(b, nb, ATTN_BLOCK, h, d)
    return jnp.concatenate([prev, cur], axis=2)


def _sliding_window_attention(q, k, v, sinks):
    b, s, _, d = q.shape
    nb = s // ATTN_BLOCK
    qb = q.reshape(b, nb, ATTN_BLOCK, N_KV_HEADS, Q_PER_KV, d)
    kb = _band_blocks(k)
    vb = _band_blocks(v)
    scores = jnp.einsum('bnqhgd,bnkhd->bnhgqk', qb, kb, preferred_element_type=jnp.float32) * (d ** -0.5)
    qi = jnp.arange(ATTN_BLOCK)[:, None]
    kj = jnp.arange(2 * ATTN_BLOCK)[None, :]
    rel = qi + ATTN_BLOCK - kj
    band = (rel >= 0) & (rel < WINDOW)
    blk = jnp.arange(nb)[:, None, None]
    valid = band[None] & ((blk > 0) | (kj >= ATTN_BLOCK)[None])
    scores = jnp.where(valid[None, :, None, None], scores, -jnp.inf)
    sink = sinks.astype(jnp.float32).reshape(N_KV_HEADS, Q_PER_KV)[None, None, :, :, None, None]
    m = jnp.maximum(jnp.max(scores, axis=-1, keepdims=True), sink)
    p = jnp.exp(scores - m)
    probs = p / (jnp.sum(p, axis=-1, keepdims=True) + jnp.exp(sink - m))
    out = jnp.einsum('bnhgqk,bnkhd->bnqhgd', probs.astype(v.dtype), vb)
    return out.reshape(b, s, N_Q_HEADS * d)


def _ssd_chunked(xh, dt, a, bm, cm):
    b, s, _, p = xh.shape
    nc = s // SSD_CHUNK
    g, r, n = SSD_GROUPS, HEADS_PER_GROUP, SSD_STATE
    x_c = (xh * dt[..., None]).reshape(b, nc, SSD_CHUNK, g, r, p)
    a_cs = jnp.cumsum((dt * a).reshape(b, nc, SSD_CHUNK, g, r), axis=2)
    b_c = bm.reshape(b, nc, SSD_CHUNK, g, n)
    c_c = cm.reshape(b, nc, SSD_CHUNK, g, n)
    seg = a_cs[:, :, :, None] - a_cs[:, :, None, :]
    causal = jnp.tril(jnp.ones((SSD_CHUNK, SSD_CHUNK), dtype=bool))[None, None, :, :, None, None]
    l_mat = jnp.exp(jnp.where(causal, seg, -jnp.inf))
    cb = jnp.einsum('bclgn,bcsgn->bclsg', c_c, b_c)
    y_diag = jnp.einsum('bclsgr,bcsgrp->bclgrp', cb[..., None] * l_mat, x_c)
    decay_s = jnp.exp(a_cs[:, :, -1:] - a_cs)
    states = jnp.einsum('bcsgn,bcsgrp->bcgrpn', b_c, x_c * decay_s[..., None])
    chunk_decay = jnp.exp(a_cs[:, :, -1])

    def step(h, inp):
        s_c, d_c = inp
        return h * d_c[..., None, None] + s_c, h

    h0 = jnp.zeros((b, g, r, p, n), dtype=jnp.float32)
    _, prev = lax.scan(step, h0, (jnp.moveaxis(states, 1, 0), jnp.moveaxis(chunk_decay, 1, 0)))
    prev = jnp.moveaxis(prev, 0, 1)
    y_off = jnp.einsum('bclgn,bcgrpn->bclgrp', c_c, prev) * jnp.exp(a_cs)[..., None]
    return (y_diag + y_off).reshape(b, s, SSD_HEADS, p)


def _gated_group_rmsnorm(y, z, g):
    b, s, _ = y.shape
    yg = y * jax.nn.silu(z.astype(jnp.float32))
    yr = yg.reshape(b, s, SSD_GROUPS, -1)
    yr = yr * lax.rsqrt(jnp.mean(yr * yr, axis=-1, keepdims=True) + EPS)
    return yr.reshape(b, s, SSD_INNER) * g.astype(jnp.float32)


def _hybrid_mixer(xn, w_in, sinks, attn_out_norm, ssd_conv_w, ssd_conv_b, dt_bias, a_log, ssd_d, ssd_norm, w_out):
    b, s, _ = xn.shape
    proj = xn @ w_in
    o = np.cumsum([0, ATTN_WIDTH, KV_WIDTH, KV_WIDTH, SSD_INNER, CONV_CH, SSD_HEADS])
    q = proj[..., o[0]:o[1]].reshape(b, s, N_Q_HEADS, HEAD_DIM)
    k = proj[..., o[1]:o[2]].reshape(b, s, N_KV_HEADS, HEAD_DIM)
    v = proj[..., o[2]:o[3]].reshape(b, s, N_KV_HEADS, HEAD_DIM)
    z = proj[..., o[3]:o[4]]
    xbc = proj[..., o[4]:o[5]]
    dt_raw = proj[..., o[5]:o[6]]
    pos = jnp.arange(s, dtype=jnp.float32)
    q = _partial_rope(q, pos)
    k = _partial_rope(k, pos)
    attn = _rmsnorm(_sliding_window_attention(q, k, v, sinks), attn_out_norm)
    xbc = jax.nn.silu(_causal_dwconv(xbc, ssd_conv_w, ssd_conv_b)).astype(jnp.float32)
    xs = xbc[..., :SSD_INNER].reshape(b, s, SSD_HEADS, SSD_HEAD_DIM)
    bm = xbc[..., SSD_INNER:SSD_INNER + BC_WIDTH].reshape(b, s, SSD_GROUPS, SSD_STATE)
    cm = xbc[..., SSD_INNER + BC_WIDTH:].reshape(b, s, SSD_GROUPS, SSD_STATE)
    dt = jax.nn.softplus(dt_raw.astype(jnp.float32) + dt_bias.astype(jnp.float32))
    a = -jnp.exp(a_log.astype(jnp.float32))
    y = _ssd_chunked(xs, dt, a, bm, cm) + ssd_d.astype(jnp.float32)[:, None] * xs
    y = _gated_group_rmsnorm(y.reshape(b, s, SSD_INNER), z, ssd_norm).astype(xn.dtype)
    return jnp.concatenate([attn, y], axis=-1) @ w_out


def _conv_ffn(hn, w_up, ffn_conv_w, ffn_conv_b, w_down):
    u = _causal_dwconv(hn @ w_up, ffn_conv_w, ffn_conv_b)
    gate, val = u[..., :D_FF], u[..., D_FF:]
    return (jax.nn.silu(gate) * val) @ w_down


def _fwd_setup_inputs(seed: int = 0) -> dict:
    key = jax.random.key(seed)
    ks = jax.random.split(key, 20)
    f32 = jnp.float32
    nrm = lambda k, shape, scale: jax.random.normal(k, shape, f32) * scale
    x = jax.random.normal(ks[0], (BATCH, SEQ, D_MODEL), f32)
    dt0 = jnp.exp(jax.random.uniform(ks[8], (DEPTH, SSD_HEADS), f32, np.log(1e-3), np.log(1e-1)))
    return {
        'x': x,
        'norm_mix': 1.0 + nrm(ks[1], (DEPTH, D_MODEL), 0.01),
        'w_in': nrm(ks[2], (DEPTH, D_MODEL, IN_PROJ_WIDTH), D_MODEL ** -0.5),
        'sinks': nrm(ks[3], (DEPTH, N_Q_HEADS), 0.5),
        'attn_out_norm': 1.0 + nrm(ks[4], (DEPTH, ATTN_WIDTH), 0.01),
        'ssd_conv_w': nrm(ks[5], (DEPTH, SSD_CONV, CONV_CH), SSD_CONV ** -0.5),
        'ssd_conv_b': nrm(ks[6], (DEPTH, CONV_CH), 0.01),
        'dt_bias': dt0 + jnp.log(-jnp.expm1(-dt0)),
        'a_log': jnp.log(jax.random.uniform(ks[9], (DEPTH, SSD_HEADS), f32, 1.0, 16.0)),
        'ssd_d': 1.0 + nrm(ks[10], (DEPTH, SSD_HEADS), 0.01),
        'ssd_norm': 1.0 + nrm(ks[11], (DEPTH, SSD_INNER), 0.01),
        'w_out': nrm(ks[12], (DEPTH, MIX_WIDTH, D_MODEL), MIX_WIDTH ** -0.5),
        'norm_ffn': 1.0 + nrm(ks[13], (DEPTH, D_MODEL), 0.01),
        'w_up': nrm(ks[14], (DEPTH, D_MODEL, 2 * D_FF), D_MODEL ** -0.5),
        'ffn_conv_w': nrm(ks[15], (DEPTH, FFN_CONV, 2 * D_FF), FFN_CONV ** -0.5),
        'ffn_conv_b': nrm(ks[16], (DEPTH, 2 * D_FF), 0.01),
        'w_down': nrm(ks[17], (DEPTH, D_FF, D_MODEL), D_FF ** -0.5),
        'norm_final': 1.0 + nrm(ks[18], (D_MODEL,), 0.01),
    }


def _fwd_reference(x, norm_mix, w_in, sinks, attn_out_norm, ssd_conv_w, ssd_conv_b, dt_bias, a_log, ssd_d, ssd_norm, w_out, norm_ffn, w_up, ffn_conv_w, ffn_conv_b, w_down, norm_final):
    h = x
    for l in range(DEPTH):
        h = h + _hybrid_mixer(_rmsnorm(h, norm_mix[l]), w_in[l], sinks[l], attn_out_norm[l], ssd_conv_w[l], ssd_conv_b[l], dt_bias[l], a_log[l], ssd_d[l], ssd_norm[l], w_out[l])
        h = h + _conv_ffn(_rmsnorm(h, norm_ffn[l]), w_up[l], ffn_conv_w[l], ffn_conv_b[l], w_down[l])
    return _rmsnorm(h, norm_final)


import jax as _jax
import jax.numpy as _jnp

TWIN_FORMAT = 'train_step'
FWD_PARAMS = ['x', 'norm_mix', 'w_in', 'sinks', 'attn_out_norm', 'ssd_conv_w', 'ssd_conv_b', 'dt_bias', 'a_log', 'ssd_d', 'ssd_norm', 'w_out', 'norm_ffn', 'w_up', 'ffn_conv_w', 'ffn_conv_b', 'w_down', 'norm_final']
TWIN_WEIGHTS = ['norm_mix', 'w_in', 'sinks', 'attn_out_norm', 'ssd_conv_w', 'ssd_conv_b', 'dt_bias', 'a_log', 'ssd_d', 'ssd_norm', 'w_out', 'norm_ffn', 'w_up', 'ffn_conv_w', 'ffn_conv_b', 'w_down', 'norm_final']
TWIN_DIFF_INPUT = 'x'
TWIN_INPUTS = ['x', 'norm_mix', 'w_in', 'sinks', 'attn_out_norm', 'ssd_conv_w', 'ssd_conv_b', 'dt_bias', 'a_log', 'ssd_d', 'ssd_norm', 'w_out', 'norm_ffn', 'w_up', 'ffn_conv_w', 'ffn_conv_b', 'w_down', 'norm_final', 'loss_target', 'm_norm_mix', 'm_w_in', 'm_sinks', 'm_attn_out_norm', 'm_ssd_conv_w', 'm_ssd_conv_b', 'm_dt_bias', 'm_a_log', 'm_ssd_d', 'm_ssd_norm', 'm_w_out', 'm_norm_ffn', 'm_w_up', 'm_ffn_conv_w', 'm_ffn_conv_b', 'm_w_down', 'm_norm_final', 'v_norm_mix', 'v_w_in', 'v_sinks', 'v_attn_out_norm', 'v_ssd_conv_w', 'v_ssd_conv_b', 'v_dt_bias', 'v_a_log', 'v_ssd_d', 'v_ssd_norm', 'v_w_out', 'v_norm_ffn', 'v_w_up', 'v_ffn_conv_w', 'v_ffn_conv_b', 'v_w_down', 'v_norm_final']
TWIN_OUTPUTS = ['loss', 'grad_x', 'grad_norm_mix', 'grad_w_in', 'grad_sinks', 'grad_attn_out_norm', 'grad_ssd_conv_w', 'grad_ssd_conv_b', 'grad_dt_bias', 'grad_a_log', 'grad_ssd_d', 'grad_ssd_norm', 'grad_w_out', 'grad_norm_ffn', 'grad_w_up', 'grad_ffn_conv_w', 'grad_ffn_conv_b', 'grad_w_down', 'grad_norm_final', 'delta_norm_mix', 'delta_w_in', 'delta_sinks', 'delta_attn_out_norm', 'delta_ssd_conv_w', 'delta_ssd_conv_b', 'delta_dt_bias', 'delta_a_log', 'delta_ssd_d', 'delta_ssd_norm', 'delta_w_out', 'delta_norm_ffn', 'delta_w_up', 'delta_ffn_conv_w', 'delta_ffn_conv_b', 'delta_w_down', 'delta_norm_final', 'new_m_norm_mix', 'new_m_w_in', 'new_m_sinks', 'new_m_attn_out_norm', 'new_m_ssd_conv_w', 'new_m_ssd_conv_b', 'new_m_dt_bias', 'new_m_a_log', 'new_m_ssd_d', 'new_m_ssd_norm', 'new_m_w_out', 'new_m_norm_ffn', 'new_m_w_up', 'new_m_ffn_conv_w', 'new_m_ffn_conv_b', 'new_m_w_down', 'new_m_norm_final', 'new_v_norm_mix', 'new_v_w_in', 'new_v_sinks', 'new_v_attn_out_norm', 'new_v_ssd_conv_w', 'new_v_ssd_conv_b', 'new_v_dt_bias', 'new_v_a_log', 'new_v_ssd_d', 'new_v_ssd_norm', 'new_v_w_out', 'new_v_norm_ffn', 'new_v_w_up', 'new_v_ffn_conv_w', 'new_v_ffn_conv_b', 'new_v_w_down', 'new_v_norm_final']
TWIN_LEAF_KINDS = {'loss': 'loss', 'grad_x': 'grad_x', 'grad_norm_mix': 'grad_w', 'grad_w_in': 'grad_w', 'grad_sinks': 'grad_w', 'grad_attn_out_norm': 'grad_w', 'grad_ssd_conv_w': 'grad_w', 'grad_ssd_conv_b': 'grad_w', 'grad_dt_bias': 'grad_w', 'grad_a_log': 'grad_w', 'grad_ssd_d': 'grad_w', 'grad_ssd_norm': 'grad_w', 'grad_w_out': 'grad_w', 'grad_norm_ffn': 'grad_w', 'grad_w_up': 'grad_w', 'grad_ffn_conv_w': 'grad_w', 'grad_ffn_conv_b': 'grad_w', 'grad_w_down': 'grad_w', 'grad_norm_final': 'grad_w', 'delta_norm_mix': 'delta_w', 'delta_w_in': 'delta_w', 'delta_sinks': 'delta_w', 'delta_attn_out_norm': 'delta_w', 'delta_ssd_conv_w': 'delta_w', 'delta_ssd_conv_b': 'delta_w', 'delta_dt_bias': 'delta_w', 'delta_a_log': 'delta_w', 'delta_ssd_d': 'delta_w', 'delta_ssd_norm': 'delta_w', 'delta_w_out': 'delta_w', 'delta_norm_ffn': 'delta_w', 'delta_w_up': 'delta_w', 'delta_ffn_conv_w': 'delta_w', 'delta_ffn_conv_b': 'delta_w', 'delta_w_down': 'delta_w', 'delta_norm_final': 'delta_w', 'new_m_norm_mix': 'new_m', 'new_m_w_in': 'new_m', 'new_m_sinks': 'new_m', 'new_m_attn_out_norm': 'new_m', 'new_m_ssd_conv_w': 'new_m', 'new_m_ssd_conv_b': 'new_m', 'new_m_dt_bias': 'new_m', 'new_m_a_log': 'new_m', 'new_m_ssd_d': 'new_m', 'new_m_ssd_norm': 'new_m', 'new_m_w_out': 'new_m', 'new_m_norm_ffn': 'new_m', 'new_m_w_up': 'new_m', 'new_m_ffn_conv_w': 'new_m', 'new_m_ffn_conv_b': 'new_m', 'new_m_w_down': 'new_m', 'new_m_norm_final': 'new_m', 'new_v_norm_mix': 'new_v', 'new_v_w_in': 'new_v', 'new_v_sinks': 'new_v', 'new_v_attn_out_norm': 'new_v', 'new_v_ssd_conv_w': 'new_v', 'new_v_ssd_conv_b': 'new_v', 'new_v_dt_bias': 'new_v', 'new_v_a_log': 'new_v', 'new_v_ssd_d': 'new_v', 'new_v_ssd_norm': 'new_v', 'new_v_w_out': 'new_v', 'new_v_norm_ffn': 'new_v', 'new_v_w_up': 'new_v', 'new_v_ffn_conv_w': 'new_v', 'new_v_ffn_conv_b': 'new_v', 'new_v_w_down': 'new_v', 'new_v_norm_final': 'new_v'}


def _forward(args):
    return _fwd_reference(*[args[k] for k in FWD_PARAMS])


def _output_shape():
    def fwd():
        inp = _fwd_setup_inputs(0)
        return _fwd_reference(*[inp[k] for k in FWD_PARAMS])
    out = _jax.eval_shape(fwd)
    return out.shape, out.dtype

N_MICROBATCH = 1
ADAM_LR = 0.001
ADAM_B1 = 0.9
ADAM_B2 = 0.999
ADAM_EPS = 1e-08
ADAM_WD = 0.01
ADAM_STEP = 10
PER_EXAMPLE_BATCH_AXIS = {'x': 0, 'loss_target': 0}
SHARED_INPUTS = []
_WEIGHT_DTYPES = {'norm_mix': _jnp.float32, 'w_in': _jnp.float32, 'sinks': _jnp.float32, 'attn_out_norm': _jnp.float32, 'ssd_conv_w': _jnp.float32, 'ssd_conv_b': _jnp.float32, 'dt_bias': _jnp.float32, 'a_log': _jnp.float32, 'ssd_d': _jnp.float32, 'ssd_norm': _jnp.float32, 'w_out': _jnp.float32, 'norm_ffn': _jnp.float32, 'w_up': _jnp.float32, 'ffn_conv_w': _jnp.float32, 'ffn_conv_b': _jnp.float32, 'w_down': _jnp.float32, 'norm_final': _jnp.float32}
MOMENT_SCALE = {'norm_mix': 1.079681e-01, 'w_in': 5.054323e-02, 'sinks': 1.447173e-02, 'attn_out_norm': 4.763281e-02, 'ssd_conv_w': 3.578089e-02, 'ssd_conv_b': 5.097666e-02, 'dt_bias': 6.638017e-02, 'a_log': 1.244030e-01, 'ssd_d': 2.091625e-01, 'ssd_norm': 4.832653e-02, 'w_out': 6.774148e-02, 'norm_ffn': 5.075749e-02, 'w_up': 2.169980e-02, 'ffn_conv_w': 2.153450e-02, 'ffn_conv_b': 2.099085e-02, 'w_down': 3.542835e-02, 'norm_final': 1.599462e+01}


def _to_microbatches(a, axis):
    t = _jnp.moveaxis(a, axis, 0)
    t = t.reshape((N_MICROBATCH, t.shape[0] // N_MICROBATCH) + t.shape[1:])
    return _jnp.moveaxis(t, 1, axis + 1)


def setup_inputs(seed: int = 0) -> dict:
    inp = _fwd_setup_inputs(seed)
    key = _jax.random.fold_in(_jax.random.key(seed), 7919)
    shape, _ = _output_shape()
    out = dict(inp)
    out["loss_target"] = _jax.random.normal(_jax.random.fold_in(key, 0), shape, _jnp.float32)
    for i, name in enumerate(TWIN_WEIGHTS):
        w = inp[name].astype(_jnp.float32)
        if MOMENT_SCALE is None:
            s = _jnp.sqrt(_jnp.mean(_jnp.square(w)) + 1e-30)
        else:
            s = MOMENT_SCALE[name]
        km, kv = _jax.random.split(_jax.random.fold_in(key, i + 1))
        out[name] = w
        out["m_" + name] = s * _jax.random.normal(km, w.shape, _jnp.float32)
        out["v_" + name] = (s * s) * _jax.random.uniform(kv, w.shape, _jnp.float32, 0.5, 1.5)
    if N_MICROBATCH > 1:
        for name, axis in PER_EXAMPLE_BATCH_AXIS.items():
            out[name] = _to_microbatches(out[name], axis)
    return {'x': out['x'], 'norm_mix': out['norm_mix'], 'w_in': out['w_in'], 'sinks': out['sinks'], 'attn_out_norm': out['attn_out_norm'], 'ssd_conv_w': out['ssd_conv_w'], 'ssd_conv_b': out['ssd_conv_b'], 'dt_bias': out['dt_bias'], 'a_log': out['a_log'], 'ssd_d': out['ssd_d'], 'ssd_norm': out['ssd_norm'], 'w_out': out['w_out'], 'norm_ffn': out['norm_ffn'], 'w_up': out['w_up'], 'ffn_conv_w': out['ffn_conv_w'], 'ffn_conv_b': out['ffn_conv_b'], 'w_down': out['w_down'], 'norm_final': out['norm_final'], 'loss_target': out['loss_target'], 'm_norm_mix': out['m_norm_mix'], 'm_w_in': out['m_w_in'], 'm_sinks': out['m_sinks'], 'm_attn_out_norm': out['m_attn_out_norm'], 'm_ssd_conv_w': out['m_ssd_conv_w'], 'm_ssd_conv_b': out['m_ssd_conv_b'], 'm_dt_bias': out['m_dt_bias'], 'm_a_log': out['m_a_log'], 'm_ssd_d': out['m_ssd_d'], 'm_ssd_norm': out['m_ssd_norm'], 'm_w_out': out['m_w_out'], 'm_norm_ffn': out['m_norm_ffn'], 'm_w_up': out['m_w_up'], 'm_ffn_conv_w': out['m_ffn_conv_w'], 'm_ffn_conv_b': out['m_ffn_conv_b'], 'm_w_down': out['m_w_down'], 'm_norm_final': out['m_norm_final'], 'v_norm_mix': out['v_norm_mix'], 'v_w_in': out['v_w_in'], 'v_sinks': out['v_sinks'], 'v_attn_out_norm': out['v_attn_out_norm'], 'v_ssd_conv_w': out['v_ssd_conv_w'], 'v_ssd_conv_b': out['v_ssd_conv_b'], 'v_dt_bias': out['v_dt_bias'], 'v_a_log': out['v_a_log'], 'v_ssd_d': out['v_ssd_d'], 'v_ssd_norm': out['v_ssd_norm'], 'v_w_out': out['v_w_out'], 'v_norm_ffn': out['v_norm_ffn'], 'v_w_up': out['v_w_up'], 'v_ffn_conv_w': out['v_ffn_conv_w'], 'v_ffn_conv_b': out['v_ffn_conv_b'], 'v_w_down': out['v_w_down'], 'v_norm_final': out['v_norm_final']}


def _loss(weights, diff, rest, loss_target):
    with _jax.named_scope("forward"):
        args = {**rest, TWIN_DIFF_INPUT: diff, **{k: w.astype(_WEIGHT_DTYPES[k]) for k, w in weights.items()}}
        y = _forward(args)
    with _jax.named_scope("loss_head"):
        err = _jnp.square(y.astype(_jnp.float32) - loss_target)
        return 0.5 * _jnp.sum(_jnp.mean(err, axis=-1)) if err.ndim else 0.5 * err


def _adamw(w, g, m, v):
    m = ADAM_B1 * m + (1.0 - ADAM_B1) * g
    v = ADAM_B2 * v + (1.0 - ADAM_B2) * _jnp.square(g)
    m_hat = m / (1.0 - ADAM_B1 ** ADAM_STEP)
    v_hat = v / (1.0 - ADAM_B2 ** ADAM_STEP)
    delta = -ADAM_LR * (m_hat / (_jnp.sqrt(v_hat) + ADAM_EPS) + ADAM_WD * w)
    return delta, m, v


def reference(x, norm_mix, w_in, sinks, attn_out_norm, ssd_conv_w, ssd_conv_b, dt_bias, a_log, ssd_d, ssd_norm, w_out, norm_ffn, w_up, ffn_conv_w, ffn_conv_b, w_down, norm_final, loss_target, m_norm_mix, m_w_in, m_sinks, m_attn_out_norm, m_ssd_conv_w, m_ssd_conv_b, m_dt_bias, m_a_log, m_ssd_d, m_ssd_norm, m_w_out, m_norm_ffn, m_w_up, m_ffn_conv_w, m_ffn_conv_b, m_w_down, m_norm_final, v_norm_mix, v_w_in, v_sinks, v_attn_out_norm, v_ssd_conv_w, v_ssd_conv_b, v_dt_bias, v_a_log, v_ssd_d, v_ssd_norm, v_w_out, v_norm_ffn, v_w_up, v_ffn_conv_w, v_ffn_conv_b, v_w_down, v_norm_final):
    given = dict(x=x, norm_mix=norm_mix, w_in=w_in, sinks=sinks, attn_out_norm=attn_out_norm, ssd_conv_w=ssd_conv_w, ssd_conv_b=ssd_conv_b, dt_bias=dt_bias, a_log=a_log, ssd_d=ssd_d, ssd_norm=ssd_norm, w_out=w_out, norm_ffn=norm_ffn, w_up=w_up, ffn_conv_w=ffn_conv_w, ffn_conv_b=ffn_conv_b, w_down=w_down, norm_final=norm_final, loss_target=loss_target, m_norm_mix=m_norm_mix, m_w_in=m_w_in, m_sinks=m_sinks, m_attn_out_norm=m_attn_out_norm, m_ssd_conv_w=m_ssd_conv_w, m_ssd_conv_b=m_ssd_conv_b, m_dt_bias=m_dt_bias, m_a_log=m_a_log, m_ssd_d=m_ssd_d, m_ssd_norm=m_ssd_norm, m_w_out=m_w_out, m_norm_ffn=m_norm_ffn, m_w_up=m_w_up, m_ffn_conv_w=m_ffn_conv_w, m_ffn_conv_b=m_ffn_conv_b, m_w_down=m_w_down, m_norm_final=m_norm_final, v_norm_mix=v_norm_mix, v_w_in=v_w_in, v_sinks=v_sinks, v_attn_out_norm=v_attn_out_norm, v_ssd_conv_w=v_ssd_conv_w, v_ssd_conv_b=v_ssd_conv_b, v_dt_bias=v_dt_bias, v_a_log=v_a_log, v_ssd_d=v_ssd_d, v_ssd_norm=v_ssd_norm, v_w_out=v_w_out, v_norm_ffn=v_norm_ffn, v_w_up=v_w_up, v_ffn_conv_w=v_ffn_conv_w, v_ffn_conv_b=v_ffn_conv_b, v_w_down=v_w_down, v_norm_final=v_norm_final)
    weights = {n: given[n] for n in TWIN_WEIGHTS}
    shared = {n: given[n] for n in SHARED_INPUTS}
    per_example = {n: given[n] for n in ['x']}
    grad_fn = _jax.value_and_grad(_loss, argnums=(0, 1))

    def one_microbatch(ex, loss_target):
        ex = dict(ex)
        diff = ex.pop(TWIN_DIFF_INPUT)
        return grad_fn(weights, diff, {**shared, **ex}, loss_target)

    if N_MICROBATCH == 1:
        loss, (grad_w, grad_x) = one_microbatch(per_example, given["loss_target"])
    else:
        def body(carry, xs):
            loss_sum, grad_sum = carry
            l_k, (gw_k, gx_k) = one_microbatch(xs[0], xs[1])
            with _jax.named_scope("update"):
                return (loss_sum + l_k, _jax.tree.map(_jnp.add, grad_sum, gw_k)), gx_k

        init = (_jnp.zeros((), _jnp.float32), _jax.tree.map(_jnp.zeros_like, weights))
        (loss, grad_w), grad_x = _jax.lax.scan(body, init, (per_example, given["loss_target"]))
    with _jax.named_scope("update"):
        delta_w, new_m, new_v = {}, {}, {}
        for n in TWIN_WEIGHTS:
            delta_w[n], new_m[n], new_v[n] = _adamw(weights[n], grad_w[n], given["m_" + n], given["v_" + n])
    return (loss, grad_x, *[grad_w[n] for n in TWIN_WEIGHTS], *[delta_w[n] for n in TWIN_WEIGHTS],
            *[new_m[n] for n in TWIN_WEIGHTS], *[new_v[n] for n in TWIN_WEIGHTS])
```

```python
import functools

import jax
import jax.numpy as jnp
from jax import lax
from jax.experimental import pallas as pl
from jax.experimental.pallas import tpu as pltpu

F32 = jnp.float32
BF16 = jnp.bfloat16

D_MODEL = 2048
N_Q_HEADS = 32
N_KV_HEADS = 8
HEAD_DIM = 64
WINDOW = 128
ATTN_BLOCK = 128
ROT_DIM = 16
ROPE_THETA = 500000.0
SSD_HEADS = 32
SSD_HEAD_DIM = 64
SSD_INNER = 2048
SSD_GROUPS = 8
SSD_STATE = 128
SSD_CONV = 4
SSD_CHUNK = 128
ATTN_WIDTH = 2048
KV_WIDTH = 512
BC_WIDTH = 1024
CONV_CH = 4096
IN_PROJ_WIDTH = 9248
MAIN_WIDTH = 9216
D_FF = 5632
FFN_CONV = 3
EPS = 1e-6
O_Q, O_K, O_V, O_Z, O_XBC, O_DT = 0, 2048, 2560, 3072, 5120, 9216

ADAM_LR = 0.001
ADAM_B1 = 0.9
ADAM_B2 = 0.999
ADAM_EPS = 1e-08
ADAM_WD = 0.01
ADAM_STEP = 10

N_CHIPS = 4
NEG = -1e30
LANES = 128
VMEM_LIMIT = 48 * 1024 * 1024
MESH = pl.DeviceIdType.MESH
HBM_SPEC = pl.BlockSpec(memory_space=pltpu.HBM)

WEIGHTS = ['norm_mix', 'w_in', 'sinks', 'attn_out_norm', 'ssd_conv_w', 'ssd_conv_b', 'dt_bias', 'a_log', 'ssd_d',
           'ssd_norm', 'w_out', 'norm_ffn', 'w_up', 'ffn_conv_w', 'ffn_conv_b', 'w_down', 'norm_final']
BIG = ['w_in', 'w_out', 'w_up', 'w_down']


def _cp(sem=None, vmem=VMEM_LIMIT):
    kw = {'vmem_limit_bytes': vmem}
    if sem is not None:
        kw['dimension_semantics'] = sem
    return pltpu.CompilerParams(**kw)


def _tile(n, pref):
    if n <= pref:
        return n
    t = (pref // LANES) * LANES
    while t > LANES and n % t:
        t -= LANES
    assert n % t == 0, (n, pref)
    return t


def _rows(n, pref):
    t = min(n, pref)
    while n % t:
        t -= 8
    return t


def _iota(shape, dim):
    return lax.broadcasted_iota(jnp.int32, shape, dim)


def _dot(a, b, mode='nn'):
    dn = {'nn': (((1,), (0,)), ((), ())), 'nt': (((1,), (1,)), ((), ())), 'tn': (((0,), (0,)), ((), ()))}[mode]
    return lax.dot_general(a.astype(BF16), b.astype(BF16), dn, preferred_element_type=F32)


def _dot_exact(a, b):
    return lax.dot_general(a, b, (((1,), (0,)), ((), ())), precision=lax.Precision.HIGHEST,
                           preferred_element_type=F32)


def _sigmoid(x):
    return 1.0 / (1.0 + jnp.exp(-x))


def _softplus(x):
    return jnp.maximum(x, 0.0) + jnp.log(1.0 + jnp.exp(-jnp.abs(x)))


def _matmul(a, b, *, mode, name, out_dtype=F32, add=None, tm=1024, tn=1024, tk=512):
    if mode == 'nn':
        (m, k), (k2, n) = a.shape, b.shape
    elif mode == 'nt':
        (m, k), (n, k2) = a.shape, b.shape
    else:
        (k, m), (k2, n) = a.shape, b.shape
    assert k == k2, (a.shape, b.shape, mode)
    tm, tn, tk = _tile(m, tm), _tile(n, tn), _tile(k, tk)
    nk = k // tk
    has_add = add is not None

    def body(*refs):
        if has_add:
            a_ref, b_ref, add_ref, o_ref, acc = refs
        else:
            a_ref, b_ref, o_ref, acc = refs
        kk = pl.program_id(2)

        @pl.when(kk == 0)
        def _():
            acc[...] = jnp.zeros_like(acc)

        acc[...] += _dot(a_ref[...], b_ref[...], mode)

        @pl.when(kk == nk - 1)
        def _():
            r = acc[...]
            if has_add:
                r = r + add_ref[...].astype(F32)
            o_ref[...] = r.astype(out_dtype)

    if mode == 'tn':
        a_spec = pl.BlockSpec((tk, tm), lambda i, j, kk: (kk, i))
    else:
        a_spec = pl.BlockSpec((tm, tk), lambda i, j, kk: (i, kk))
    if mode == 'nt':
        b_spec = pl.BlockSpec((tn, tk), lambda i, j, kk: (j, kk))
    else:
        b_spec = pl.BlockSpec((tk, tn), lambda i, j, kk: (kk, j))
    o_spec = pl.BlockSpec((tm, tn), lambda i, j, kk: (i, j))
    in_specs = [a_spec, b_spec] + ([o_spec] if has_add else [])
    args = (a, b) + ((add,) if has_add else ())
    return pl.pallas_call(
        body, name=name, grid=(m // tm, n // tn, nk), in_specs=in_specs, out_specs=o_spec,
        out_shape=jax.ShapeDtypeStruct((m, n), out_dtype), scratch_shapes=[pltpu.VMEM((tm, tn), F32)],
        compiler_params=_cp(("parallel", "parallel", "arbitrary")))(*args)


def _rmsnorm_fwd(x, g, name):
    t, d = x.shape
    tb = _rows(t, 256)

    def body(x_ref, g_ref, o_ref):
        xv = x_ref[...]
        r = lax.rsqrt(jnp.mean(xv * xv, axis=-1, keepdims=True) + EPS)
        o_ref[...] = (xv * r * g_ref[...]).astype(BF16)

    return pl.pallas_call(
        body, name=name, grid=(t // tb,),
        in_specs=[pl.BlockSpec((tb, d), lambda i: (i, 0)), pl.BlockSpec((1, d), lambda i: (0, 0))],
        out_specs=pl.BlockSpec((tb, d), lambda i: (i, 0)), out_shape=jax.ShapeDtypeStruct((t, d), BF16),
        compiler_params=_cp(("parallel",)))(x, g)


def _rmsnorm_bwd(x, g, dy, res, name):
    t, d = x.shape
    tb = _rows(t, 256)

    def body(x_ref, g_ref, dy_ref, res_ref, dx_ref, dg_ref):
        i = pl.program_id(0)
        xv = x_ref[...]
        dyv = dy_ref[...].astype(F32)
        r = lax.rsqrt(jnp.mean(xv * xv, axis=-1, keepdims=True) + EPS)
        u = dyv * g_ref[...]
        dx = r * u - xv * (r * r * r * jnp.mean(u * xv, axis=-1, keepdims=True))
        dx_ref[...] = dx + res_ref[...]
        part = jnp.sum(dyv * xv * r, axis=0, keepdims=True)

        @pl.when(i == 0)
        def _():
            dg_ref[...] = part

        @pl.when(i > 0)
        def _():
            dg_ref[...] += part

    row = pl.BlockSpec((tb, d), lambda i: (i, 0))
    vec = pl.BlockSpec((1, d), lambda i: (0, 0))
    return pl.pallas_call(
        body, name=name, grid=(t // tb,), in_specs=[row, vec, row, row], out_specs=[row, vec],
        out_shape=[jax.ShapeDtypeStruct((t, d), F32), jax.ShapeDtypeStruct((1, d), F32)],
        compiler_params=_cp(("arbitrary",)))(x, g, dy, res)


def _final_loss(h, g, tgt):
    t, d = h.shape
    tb = _rows(t, 256)

    def body(h_ref, g_ref, t_ref, loss_ref, dh_ref, dg_ref):
        i = pl.program_id(0)
        hv = h_ref[...]
        gv = g_ref[...]
        r = lax.rsqrt(jnp.mean(hv * hv, axis=-1, keepdims=True) + EPS)
        y = hv * r * gv
        diff = y - t_ref[...]
        lpart = jnp.sum(jnp.sum(diff * diff, axis=1, keepdims=True), axis=0, keepdims=True) * (0.5 / d)
        dy = diff * (1.0 / d)
        u = dy * gv
        dh_ref[...] = r * u - hv * (r * r * r * jnp.mean(u * hv, axis=-1, keepdims=True))
        gpart = jnp.sum(dy * hv * r, axis=0, keepdims=True)
        lrow = jnp.broadcast_to(lpart, (1, LANES))

        @pl.when(i == 0)
        def _():
            loss_ref[...] = lrow
            dg_ref[...] = gpart

        @pl.when(i > 0)
        def _():
            loss_ref[...] += lrow
            dg_ref[...] += gpart

    row = pl.BlockSpec((tb, d), lambda i: (i, 0))
    vec = pl.BlockSpec((1, d), lambda i: (0, 0))
    return pl.pallas_call(
        body, name="final_loss", grid=(t // tb,), in_specs=[row, vec, row],
        out_specs=[pl.BlockSpec((1, LANES), lambda i: (0, 0)), row, vec],
        out_shape=[jax.ShapeDtypeStruct((1, LANES), F32), jax.ShapeDtypeStruct((t, d), F32),
                   jax.ShapeDtypeStruct((1, d), F32)],
        compiler_params=_cp(("arbitrary",)))(h, g, tgt)


def _rope_tables(t):
    pos = jnp.arange(t, dtype=F32)
    inv = 1.0 / (ROPE_THETA ** (jnp.arange(0, ROT_DIM, 2, dtype=F32) / ROT_DIM))
    ang = pos[:, None] * inv[None, :]
    cos, sin = jnp.cos(ang), jnp.sin(ang)
    half = ROT_DIM // 2
    rest = HEAD_DIM - ROT_DIM
    c = jnp.concatenate([cos, cos, jnp.ones((t, rest), F32)], axis=1)
    s1 = jnp.concatenate([-sin, jnp.zeros((t, half + rest), F32)], axis=1)
    s2 = jnp.concatenate([jnp.zeros((t, half), F32), sin, jnp.zeros((t, rest), F32)], axis=1)
    return tuple(jnp.tile(v, (1, LANES // HEAD_DIM)) for v in (c, s1, s2))


def _rope(x, c, s1, s2):
    half = ROT_DIM // 2
    return x * c + pltpu.roll(x, LANES - half, 1) * s1 + pltpu.roll(x, half, 1) * s2


def _rope_t(g, c, s1, s2):
    half = ROT_DIM // 2
    return g * c + pltpu.roll(g * s1, half, 1) + pltpu.roll(g * s2, LANES - half, 1)


def _attn_mask(i):
    qi = _iota((ATTN_BLOCK, 2 * ATTN_BLOCK), 0)
    kj = _iota((ATTN_BLOCK, 2 * ATTN_BLOCK), 1)
    rel = qi + ATTN_BLOCK - kj
    first_key = jnp.where(i > 0, 0, ATTN_BLOCK)
    return (rel >= 0) & (rel < WINDOW) & (kj >= first_key)


def _half_masks():
    lane = _iota((1, LANES), 1)
    return [(lane < HEAD_DIM).astype(F32), (lane >= HEAD_DIM).astype(F32)]


def _attn_specs(nb_clamp):
    blk = ATTN_BLOCK
    kb, vb = O_K // LANES, O_V // LANES

    def cur(i):
        return jnp.minimum(i, nb_clamp)

    def prev(i):
        return jnp.maximum(jnp.minimum(i, nb_clamp + 1) - 1, 0)

    q = pl.BlockSpec((blk, 512), lambda p, i: (cur(i), p))
    kc = pl.BlockSpec((blk, LANES), lambda p, i: (cur(i), kb + p))
    kp = pl.BlockSpec((blk, LANES), lambda p, i: (prev(i), kb + p))
    vc = pl.BlockSpec((blk, LANES), lambda p, i: (cur(i), vb + p))
    vp = pl.BlockSpec((blk, LANES), lambda p, i: (prev(i), vb + p))
    tc = pl.BlockSpec((blk, LANES), lambda p, i: (cur(i), 0))
    tp = pl.BlockSpec((blk, LANES), lambda p, i: (prev(i), 0))
    return q, kc, kp, vc, vp, tc, tp


def _attn_fwd(proj, sinks, tables):
    t = proj.shape[0]
    nb = t // ATTN_BLOCK
    scale = HEAD_DIM ** -0.5

    def body(sink_ref, q_ref, kc_ref, kp_ref, vc_ref, vp_ref, cc_ref, s1c_ref, s2c_ref, cp_ref, s1p_ref, s2p_ref,
             o_ref):
        p = pl.program_id(0)
        i = pl.program_id(1)
        cc, s1c, s2c = cc_ref[...], s1c_ref[...], s2c_ref[...]
        kband = jnp.concatenate([_rope(kp_ref[...], cp_ref[...], s1p_ref[...], s2p_ref[...]),
                                 _rope(kc_ref[...], cc, s1c, s2c)], axis=0).astype(BF16)
        vband = jnp.concatenate([vp_ref[...], vc_ref[...]], axis=0)
        hm = _half_masks()
        vsel = [(vband * hm[j]).astype(BF16) for j in range(2)]
        valid = _attn_mask(i)
        for qb in range(4):
            qr = _rope(q_ref[:, qb * LANES:(qb + 1) * LANES], cc, s1c, s2c)
            acc = jnp.zeros((ATTN_BLOCK, LANES), F32)
            for half in range(2):
                hh = qb * 2 + half
                j = hh // 4
                qs = qr * hm[half]
                if half != j:
                    qs = pltpu.roll(qs, HEAD_DIM, 1)
                s = jnp.where(valid, _dot(qs, kband, 'nt') * scale, NEG)
                sink = sink_ref[p * 8 + hh]
                m = jnp.maximum(jnp.max(s, axis=1, keepdims=True), sink)
                pe = jnp.exp(s - m)
                den = jnp.sum(pe, axis=1, keepdims=True) + jnp.exp(sink - m)
                o = _dot(pe / den, vsel[j])
                if half != j:
                    o = pltpu.roll(o, HEAD_DIM, 1)
                acc = acc + o
            o_ref[:, qb * LANES:(qb + 1) * LANES] = acc

    q, kc, kp, vc, vp, tc, tp = _attn_specs(nb - 1)
    smem = pl.BlockSpec(memory_space=pltpu.SMEM)
    return pl.pallas_call(
        body, name="attn_fwd", grid=(4, nb),
        in_specs=[smem, q, kc, kp, vc, vp, tc, tc, tc, tp, tp, tp],
        out_specs=pl.BlockSpec((ATTN_BLOCK, 512), lambda p, i: (i, p)),
        out_shape=jax.ShapeDtypeStruct((t, ATTN_WIDTH), F32),
        compiler_params=_cp(("parallel", "arbitrary")))(sinks, proj, proj, proj, proj, proj, *tables, *tables)


def _attn_bwd(proj, sinks, tables, dout):
    t = proj.shape[0]
    nb = t // ATTN_BLOCK
    scale = HEAD_DIM ** -0.5

    def body(sink_ref, q_ref, kc_ref, kp_ref, vc_ref, vp_ref, cc_ref, s1c_ref, s2c_ref, cp_ref, s1p_ref, s2p_ref,
             do_ref, dq_ref, dk_ref, dv_ref, ds_ref, carry_k, carry_v):
        p = pl.program_id(0)
        i = pl.program_id(1)
        ptab = (cp_ref[...], s1p_ref[...], s2p_ref[...])

        @pl.when(i == 0)
        def _():
            carry_k[...] = jnp.zeros_like(carry_k)
            carry_v[...] = jnp.zeros_like(carry_v)
            ds_ref[...] = jnp.zeros_like(ds_ref)

        @pl.when(i < nb)
        def _():
            cc, s1c, s2c = cc_ref[...], s1c_ref[...], s2c_ref[...]
            kband = jnp.concatenate([_rope(kp_ref[...], *ptab), _rope(kc_ref[...], cc, s1c, s2c)], axis=0)
            vband = jnp.concatenate([vp_ref[...], vc_ref[...]], axis=0)
            hm = _half_masks()
            kband16 = kband.astype(BF16)
            ksel = [(kband * hm[j]).astype(BF16) for j in range(2)]
            vband16 = vband.astype(BF16)
            vsel = [(vband * hm[j]).astype(BF16) for j in range(2)]
            valid = _attn_mask(i)
            dkb = jnp.zeros((2 * ATTN_BLOCK, LANES), F32)
            dvb = jnp.zeros((2 * ATTN_BLOCK, LANES), F32)
            row8 = _iota((8, LANES), 0)
            dsink = jnp.zeros((8, LANES), F32)
            for qb in range(4):
                qr = _rope(q_ref[:, qb * LANES:(qb + 1) * LANES], cc, s1c, s2c)
                dob = do_ref[:, qb * LANES:(qb + 1) * LANES]
                dqb = jnp.zeros((ATTN_BLOCK, LANES), F32)
                for half in range(2):
                    hh = qb * 2 + half
                    j = hh // 4
                    qs = qr * hm[half]
                    dos = dob * hm[half]
                    if half != j:
                        qs = pltpu.roll(qs, HEAD_DIM, 1)
                        dos = pltpu.roll(dos, HEAD_DIM, 1)
                    qs16 = qs.astype(BF16)
                    dos16 = dos.astype(BF16)
                    s = jnp.where(valid, _dot(qs16, kband16, 'nt') * scale, NEG)
                    sink = sink_ref[p * 8 + hh]
                    m = jnp.maximum(jnp.max(s, axis=1, keepdims=True), sink)
                    pe = jnp.exp(s - m)
                    psink = jnp.exp(sink - m)
                    den = jnp.sum(pe, axis=1, keepdims=True) + psink
                    pr = pe / den
                    dvb = dvb + _dot(pr.T, dos16)
                    dp = _dot(dos16, vband16, 'nt')
                    delta = jnp.sum(pr * dp, axis=1, keepdims=True)
                    dsc = pr * (dp - delta) * scale
                    dsink = dsink + jnp.where(row8 == hh, -jnp.sum(psink / den * delta), 0.0)
                    dqh = _dot(dsc, ksel[j])
                    if half != j:
                        dqh = pltpu.roll(dqh, HEAD_DIM, 1)
                    dqb = dqb + dqh
                    dkb = dkb + _dot(dsc.T, qs16)
                dq_ref[:, qb * LANES:(qb + 1) * LANES] = _rope_t(dqb, cc, s1c, s2c).astype(BF16)
            ds_ref[0] += dsink
            dk_ref[...] = _rope_t(carry_k[...] + dkb[:ATTN_BLOCK], *ptab).astype(BF16)
            dv_ref[...] = (carry_v[...] + dvb[:ATTN_BLOCK]).astype(BF16)
            carry_k[...] = dkb[ATTN_BLOCK:]
            carry_v[...] = dvb[ATTN_BLOCK:]

        @pl.when(i == nb)
        def _():
            dk_ref[...] = _rope_t(carry_k[...], *ptab).astype(BF16)
            dv_ref[...] = carry_v[...].astype(BF16)

    q, kc, kp, vc, vp, tc, tp = _attn_specs(nb - 1)
    smem = pl.BlockSpec(memory_space=pltpu.SMEM)
    qblk = pl.BlockSpec((ATTN_BLOCK, 512), lambda p, i: (jnp.minimum(i, nb - 1), p))
    kvout = pl.BlockSpec((ATTN_BLOCK, LANES), lambda p, i: (jnp.maximum(i - 1, 0), p))
    return pl.pallas_call(
        body, name="attn_bwd", grid=(4, nb + 1),
        in_specs=[smem, q, kc, kp, vc, vp, tc, tc, tc, tp, tp, tp, qblk],
        out_specs=[qblk, kvout, kvout, pl.BlockSpec((1, 8, LANES), lambda p, i: (p, 0, 0))],
        out_shape=[jax.ShapeDtypeStruct((t, ATTN_WIDTH), BF16), jax.ShapeDtypeStruct((t, KV_WIDTH), BF16),
                   jax.ShapeDtypeStruct((t, KV_WIDTH), BF16), jax.ShapeDtypeStruct((4, 8, LANES), F32)],
        scratch_shapes=[pltpu.VMEM((ATTN_BLOCK, LANES), F32), pltpu.VMEM((ATTN_BLOCK, LANES), F32)],
        compiler_params=_cp(("parallel", "arbitrary")))(sinks, proj, proj, proj, proj, proj, *tables, *tables, dout)


def _shift_rows(x, prev8, j):
    r = pltpu.roll(x, j, 0)
    head = jnp.where(_iota((8, 1), 0) < j, pltpu.roll(prev8, j, 0), r[:8])
    return jnp.concatenate([head, r[8:]], axis=0)


def _shift_rows_up(x, next8, j):
    n = x.shape[0]
    r = pltpu.roll(x, n - j, 0)
    tail = jnp.where(_iota((8, 1), 0) >= 8 - j, pltpu.roll(next8, 8 - j, 0), r[n - 8:])
    return jnp.concatenate([r[:n - 8], tail], axis=0)


def _conv_apply(x, prev8, w, b, taps):
    u = b + x * w[taps - 1:taps]
    for j in range(1, taps):
        u = u + _shift_rows(x, prev8, j) * w[taps - 1 - j:taps - j]
    return u


def _conv_specs(tb, tc, col0, t):
    c0 = col0 // tc
    cur = pl.BlockSpec((tb, tc), lambda j, i: (i, c0 + j))
    prev = pl.BlockSpec((8, tc), lambda j, i: (jnp.maximum(i * (tb // 8) - 1, 0), c0 + j))
    nxt = pl.BlockSpec((8, tc), lambda j, i: (jnp.minimum((i + 1) * (tb // 8), t // 8 - 1), c0 + j))
    return cur, prev, nxt


def _conv_fwd(x, w, b, *, col0, width, act, name):
    t = x.shape[0]
    taps = w.shape[0]
    tb, tc = _rows(t, 512), _tile(width, 1024)
    assert col0 % tc == 0

    def body(x_ref, xp_ref, w_ref, b_ref, o_ref):
        i = pl.program_id(1)
        prev8 = jnp.where(i > 0, xp_ref[...], 0.0)
        u = _conv_apply(x_ref[...], prev8, w_ref[...], b_ref[...], taps)
        if act:
            u = u * _sigmoid(u)
        o_ref[...] = u

    cur, prev, _ = _conv_specs(tb, tc, col0, t)
    par = pl.BlockSpec((taps, tc), lambda j, i: (0, j))
    bias = pl.BlockSpec((1, tc), lambda j, i: (0, j))
    return pl.pallas_call(
        body, name=name, grid=(width // tc, t // tb), in_specs=[cur, prev, par, bias],
        out_specs=pl.BlockSpec((tb, tc), lambda j, i: (i, j)), out_shape=jax.ShapeDtypeStruct((t, width), F32),
        compiler_params=_cp(("parallel", "parallel")))(x, x, w, b)


def _conv_silu_dact(x, w, b, dout, *, col0, width, name):
    t = x.shape[0]
    taps = w.shape[0]
    tb, tc = _rows(t, 512), _tile(width, 1024)

    def body(x_ref, xp_ref, w_ref, b_ref, d_ref, o_ref):
        i = pl.program_id(1)
        prev8 = jnp.where(i > 0, xp_ref[...], 0.0)
        u = _conv_apply(x_ref[...], prev8, w_ref[...], b_ref[...], taps)
        sg = _sigmoid(u)
        o_ref[...] = d_ref[...] * (sg * (1.0 + u * (1.0 - sg)))

    cur, prev, _ = _conv_specs(tb, tc, col0, t)
    par = pl.BlockSpec((taps, tc), lambda j, i: (0, j))
    bias = pl.BlockSpec((1, tc), lambda j, i: (0, j))
    out = pl.BlockSpec((tb, tc), lambda j, i: (i, j))
    return pl.pallas_call(
        body, name=name, grid=(width // tc, t // tb), in_specs=[cur, prev, par, bias, out],
        out_specs=out, out_shape=jax.ShapeDtypeStruct((t, width), F32),
        compiler_params=_cp(("parallel", "parallel")))(x, x, w, b, dout)


def _conv_bwd(x, w, du, *, col0, width, name):
    t = x.shape[0]
    taps = w.shape[0]
    tb, tc = _rows(t, 512), _tile(width, 1024)
    nrow = t // tb

    def body(x_ref, xp_ref, w_ref, du_ref, dun_ref, dx_ref, dw_ref, db_ref):
        i = pl.program_id(1)
        xv = x_ref[...]
        prev8 = jnp.where(i > 0, xp_ref[...], 0.0)
        duv = du_ref[...]
        next8 = jnp.where(i < nrow - 1, dun_ref[...], 0.0)
        wv = w_ref[...]
        dx = duv * wv[taps - 1:taps]
        parts = [jnp.sum(duv * xv, axis=0, keepdims=True)]
        for j in range(1, taps):
            dx = dx + _shift_rows_up(duv, next8, j) * wv[taps - 1 - j:taps - j]
            parts.append(jnp.sum(duv * _shift_rows(xv, prev8, j), axis=0, keepdims=True))
        dx_ref[...] = dx.astype(BF16)
        rowk = _iota((taps, 1), 0)
        dwv = jnp.zeros((taps, tc), F32)
        for j in range(taps):
            dwv = dwv + jnp.where(rowk == taps - 1 - j, parts[j], 0.0)
        dbv = jnp.sum(duv, axis=0, keepdims=True)

        @pl.when(i == 0)
        def _():
            dw_ref[...] = dwv
            db_ref[...] = dbv

        @pl.when(i > 0)
        def _():
            dw_ref[...] += dwv
            db_ref[...] += dbv

    cur, prev, _ = _conv_specs(tb, tc, col0, t)
    dcur, _, dnxt = _conv_specs(tb, tc, 0, t)
    par = pl.BlockSpec((taps, tc), lambda j, i: (0, j))
    bias = pl.BlockSpec((1, tc), lambda j, i: (0, j))
    return pl.pallas_call(
        body, name=name, grid=(width // tc, nrow), in_specs=[cur, prev, par, dcur, dnxt],
        out_specs=[dcur, par, bias],
        out_shape=[jax.ShapeDtypeStruct((t, width), BF16), jax.ShapeDtypeStruct((taps, width), F32),
                   jax.ShapeDtypeStruct((1, width), F32)],
        compiler_params=_cp(("parallel", "arbitrary")))(x, x, w, du, du)


def _swiglu_fwd(u):
    t = u.shape[0]
    tb, tc = _rows(t, 512), _tile(D_FF, 1408)
    nc = D_FF // tc

    def body(g_ref, v_ref, o_ref):
        g = g_ref[...]
        o_ref[...] = (g * _sigmoid(g) * v_ref[...]).astype(BF16)

    return pl.pallas_call(
        body, name="swiglu_fwd", grid=(t // tb, nc),
        in_specs=[pl.BlockSpec((tb, tc), lambda i, j: (i, j)), pl.BlockSpec((tb, tc), lambda i, j: (i, nc + j))],
        out_specs=pl.BlockSpec((tb, tc), lambda i, j: (i, j)), out_shape=jax.ShapeDtypeStruct((t, D_FF), BF16),
        compiler_params=_cp(("parallel", "parallel")))(u, u)


def _swiglu_bwd(u, da):
    t = u.shape[0]
    tb, tc = _rows(t, 512), _tile(D_FF, 1408)
    nc = D_FF // tc

    def body(g_ref, v_ref, da_ref, dg_ref, dv_ref):
        g = g_ref[...]
        dav = da_ref[...].astype(F32)
        sg = _sigmoid(g)
        dg_ref[...] = dav * v_ref[...] * (sg * (1.0 + g * (1.0 - sg)))
        dv_ref[...] = dav * g * sg

    lo = pl.BlockSpec((tb, tc), lambda i, j: (i, j))
    hi = pl.BlockSpec((tb, tc), lambda i, j: (i, nc + j))
    dg, dv = pl.pallas_call(
        body, name="swiglu_bwd", grid=(t // tb, nc), in_specs=[lo, hi, lo], out_specs=[lo, lo],
        out_shape=[jax.ShapeDtypeStruct((t, D_FF), F32), jax.ShapeDtypeStruct((t, D_FF), F32)],
        compiler_params=_cp(("parallel", "parallel")))(u, u, da)
    return dg, dv


def _head_masks():
    lane = _iota((1, 4 * SSD_HEAD_DIM), 1)
    return [((lane >= r * SSD_HEAD_DIM) & (lane < (r + 1) * SSD_HEAD_DIM)).astype(F32) for r in range(4)]


def _expand4(v4, masks):
    lane4 = _iota((1, 4), 1)
    out = 0.0
    for r in range(4):
        out = out + jnp.sum(jnp.where(lane4 == r, v4, 0.0), axis=1, keepdims=True) * masks[r]
    return out


def _collapse4(ve, masks):
    lane4 = _iota((1, 4), 1)
    out = 0.0
    for r in range(4):
        col = jnp.sum(ve * masks[r], axis=1, keepdims=True) * (1.0 / SSD_HEAD_DIM)
        out = out + jnp.where(lane4 == r, col, 0.0)
    return out


def _segsum(v, masks):
    out = 0.0
    for r in range(4):
        out = out + jnp.sum(v * masks[r], axis=1, keepdims=True) * masks[r]
    return out


def _ssd_common(raw4, prow, rawr4, bcol, acol, masks):
    n = SSD_CHUNK
    dt4 = _softplus(raw4 + prow[0:1, :])
    a4 = -jnp.exp(prow[1:2, :])
    dt_e = _expand4(dt4, masks)
    a_e = _expand4(a4, masks)
    d_e = _expand4(prow[2:3, :], masks)
    tril = (_iota((n, n), 0) >= _iota((n, n), 1)).astype(F32)
    acs_e = _dot_exact(tril, dt_e * a_e)
    last_e = acs_e[n - 1:n, :]
    dtr4 = _softplus(rawr4 + bcol)
    triu = (_iota((n, n), 0) <= _iota((n, n), 1)).astype(F32)
    acs_r4 = _dot_exact(dtr4 * (-jnp.exp(acol)), triu)
    return dt_e, a_e, d_e, acs_e, last_e, acs_r4


def _decay_matrix(acs_e, acs_r4, r, masks):
    n = SSD_CHUNK
    col = jnp.sum(acs_e * masks[r], axis=1, keepdims=True) * (1.0 / SSD_HEAD_DIM)
    seg = col - acs_r4[r:r + 1, :]
    causal = _iota((n, n), 0) >= _iota((n, n), 1)
    return jnp.exp(jnp.where(causal, seg, NEG))


def _ssd_specs(t, rev):
    nc = t // SSD_CHUNK
    xb, bb, cb = 0, SSD_INNER // SSD_STATE, (SSD_INNER + BC_WIDTH) // SSD_STATE

    def ch(c):
        return (nc - 1 - c) if rev else c

    x = pl.BlockSpec((SSD_CHUNK, 256), lambda g, c: (ch(c), xb + g))
    bm = pl.BlockSpec((SSD_CHUNK, SSD_STATE), lambda g, c: (ch(c), bb + g))
    cm = pl.BlockSpec((SSD_CHUNK, SSD_STATE), lambda g, c: (ch(c), cb + g))
    dtc = pl.BlockSpec((1, SSD_CHUNK, 4), lambda g, c: (g, ch(c), 0))
    dtr = pl.BlockSpec((1, 4, SSD_CHUNK), lambda g, c: (g, 0, ch(c)))
    prow = pl.BlockSpec((1, 3, 4), lambda g, c: (g, 0, 0))
    pcol = pl.BlockSpec((1, 4, 1), lambda g, c: (g, 0, 0))
    st = pl.BlockSpec((1, 1, SSD_STATE, 256), lambda g, c: (g, ch(c), 0, 0))
    return x, bm, cm, dtc, dtr, prow, pcol, st, ch


def _ssd_params(dt_raw, dt_bias, a_log, ssd_d):
    t = dt_raw.shape[0]
    dtc = dt_raw.reshape(t, SSD_GROUPS, 4).transpose(1, 0, 2)
    dtr = dt_raw.reshape(t, SSD_GROUPS, 4).transpose(1, 2, 0)
    prow = jnp.stack([dt_bias.reshape(SSD_GROUPS, 4), a_log.reshape(SSD_GROUPS, 4),
                      ssd_d.reshape(SSD_GROUPS, 4)], axis=1)
    bcol = dt_bias.reshape(SSD_GROUPS, 4, 1)
    acol = a_log.reshape(SSD_GROUPS, 4, 1)
    return dtc, dtr, prow, bcol, acol


def _ssd_fwd(xbc, params):
    t = xbc.shape[0]
    nc = t // SSD_CHUNK
    dtc, dtr, prow, bcol, acol = params

    def body(x_ref, b_ref, c_ref, dtc_ref, dtr_ref, prow_ref, bcol_ref, acol_ref, y_ref, st_ref, s_scr):
        c = pl.program_id(1)

        @pl.when(c == 0)
        def _():
            s_scr[...] = jnp.zeros_like(s_scr)

        masks = _head_masks()
        dt_e, a_e, d_e, acs_e, last_e, acs_r4 = _ssd_common(
            dtc_ref[0], prow_ref[0], dtr_ref[0], bcol_ref[0], acol_ref[0], masks)
        xv = x_ref[...]
        bm, cm = b_ref[...], c_ref[...]
        s = s_scr[...]
        st_ref[0, 0] = s
        xdt = xv * dt_e
        cb = _dot(cm, bm, 'nt')
        y = _dot(cm, s) * jnp.exp(acs_e) + xv * d_e
        for r in range(4):
            mr = cb * _decay_matrix(acs_e, acs_r4, r, masks)
            y = y + _dot(mr, xdt * masks[r])
        y_ref[...] = y
        w = xdt * jnp.exp(last_e - acs_e)
        s_scr[...] = s * jnp.exp(last_e) + _dot(bm.T, w)

    x, bm, cm, dtcs, dtrs, prs, pcs, st, _ = _ssd_specs(t, False)
    return pl.pallas_call(
        body, name="ssd_fwd", grid=(SSD_GROUPS, nc), in_specs=[x, bm, cm, dtcs, dtrs, prs, pcs, pcs],
        out_specs=[pl.BlockSpec((SSD_CHUNK, 256), lambda g, c: (c, g)), st],
        out_shape=[jax.ShapeDtypeStruct((t, SSD_INNER), F32),
                   jax.ShapeDtypeStruct((SSD_GROUPS, nc, SSD_STATE, 256), F32)],
        scratch_shapes=[pltpu.VMEM((SSD_STATE, 256), F32)],
        compiler_params=_cp(("parallel", "arbitrary")))(xbc, xbc, xbc, dtc, dtr, prow, bcol, acol)


def _ssd_bwd(xbc, params, states, dy):
    t = xbc.shape[0]
    nc = t // SSD_CHUNK
    n = SSD_CHUNK
    dtc, dtr, prow, bcol, acol = params

    def body(x_ref, b_ref, c_ref, dtc_ref, dtr_ref, prow_ref, bcol_ref, acol_ref, st_ref, dy_ref,
             dx_ref, db_ref, dc_ref, ddt_ref, dp_ref, ds_scr):
        c = pl.program_id(1)

        @pl.when(c == 0)
        def _():
            ds_scr[...] = jnp.zeros_like(ds_scr)
            dp_ref[...] = jnp.zeros_like(dp_ref)

        masks = _head_masks()
        raw4 = dtc_ref[0]
        prw = prow_ref[0]
        dt_e, a_e, d_e, acs_e, last_e, acs_r4 = _ssd_common(raw4, prw, dtr_ref[0], bcol_ref[0], acol_ref[0], masks)
        xv = x_ref[...]
        bm, cm = b_ref[...], c_ref[...]
        s = st_ref[0, 0]
        ds = ds_scr[...]
        dyv = dy_ref[...]
        e_e = jnp.exp(acs_e)
        dec_e = jnp.exp(last_e - acs_e)
        cd_e = jnp.exp(last_e)
        xdt = xv * dt_e
        w = xdt * dec_e
        b16, c16, s16, ds16 = bm.astype(BF16), cm.astype(BF16), s.astype(BF16), ds.astype(BF16)
        cb = _dot(c16, b16, 'nt')
        yoff_raw = _dot(c16, s16)
        dye = dyv * e_e
        dye16 = dye.astype(BF16)
        dcm = _dot(dye16, s16, 'nt')
        ds_scr[...] = ds * cd_e + _dot(cm.T, dye16)
        dacs_e = _segsum(dyv * yoff_raw, masks) * e_e
        dw = _dot(b16, ds16)
        dbm = _dot(w, ds16, 'nt')
        tdec = _segsum(dw * xdt, masks) * dec_e
        dacs_e = dacs_e - tdec
        dlast_e = jnp.sum(tdec, axis=0, keepdims=True)
        dxdt = dw * dec_e
        dlast_e = dlast_e + _segsum(jnp.sum(ds * s, axis=0, keepdims=True), masks) * cd_e
        dcb = jnp.zeros((n, n), F32)
        for r in range(4):
            lm = _decay_matrix(acs_e, acs_r4, r, masks)
            mr = cb * lm
            dyr16 = (dyv * masks[r]).astype(BF16)
            dm = _dot(dyr16, xdt * masks[r], 'nt')
            dcb = dcb + dm * lm
            dseg = dm * mr
            dcol = jnp.sum(dseg, axis=1, keepdims=True) - jnp.sum(dseg.T, axis=1, keepdims=True)
            dacs_e = dacs_e + dcol * masks[r]
            dxdt = dxdt + _dot(mr.T, dyr16)
        dcm = dcm + _dot(dcb, b16)
        dbm = dbm + _dot(dcb.T, c16)
        dacs_e = dacs_e + jnp.where(_iota((n, 1), 0) == n - 1, dlast_e, 0.0)
        triu = (_iota((n, n), 0) <= _iota((n, n), 1)).astype(F32)
        ddta_e = _dot_exact(triu, dacs_e)
        ddt_e = ddta_e * a_e + _segsum(dxdt * xv, masks)
        dx_ref[...] = dxdt * dt_e + dyv * d_e
        db_ref[...] = dbm
        dc_ref[...] = dcm
        draw_e = ddt_e * _sigmoid(_expand4(raw4 + prw[0:1, :], masks))
        ddt_ref[0] = _collapse4(draw_e, masks)
        dbias = _collapse4(jnp.sum(draw_e, axis=0, keepdims=True), masks)
        dalog = _collapse4(jnp.sum(ddta_e * dt_e, axis=0, keepdims=True) * a_e, masks)
        dd = _collapse4(jnp.sum(_segsum(dyv * xv, masks), axis=0, keepdims=True), masks)
        row3 = _iota((3, 1), 0)
        dp_ref[0] += (jnp.where(row3 == 0, dbias, 0.0) + jnp.where(row3 == 1, dalog, 0.0)
                      + jnp.where(row3 == 2, dd, 0.0))

    x, bm, cm, dtcs, dtrs, prs, pcs, st, ch = _ssd_specs(t, True)
    yblk = pl.BlockSpec((SSD_CHUNK, 256), lambda g, c: (ch(c), g))
    nblk = pl.BlockSpec((SSD_CHUNK, SSD_STATE), lambda g, c: (ch(c), g))
    return pl.pallas_call(
        body, name="ssd_bwd", grid=(SSD_GROUPS, nc),
        in_specs=[x, bm, cm, dtcs, dtrs, prs, pcs, pcs, st, yblk],
        out_specs=[yblk, nblk, nblk, dtcs, prs],
        out_shape=[jax.ShapeDtypeStruct((t, SSD_INNER), F32), jax.ShapeDtypeStruct((t, BC_WIDTH), F32),
                   jax.ShapeDtypeStruct((t, BC_WIDTH), F32), jax.ShapeDtypeStruct((SSD_GROUPS, t, 4), F32),
                   jax.ShapeDtypeStruct((SSD_GROUPS, 3, 4), F32)],
        scratch_shapes=[pltpu.VMEM((SSD_STATE, 256), F32)],
        compiler_params=_cp(("parallel", "arbitrary")))(xbc, xbc, xbc, dtc, dtr, prow, bcol, acol, states, dy)


GROUP_W = SSD_INNER // SSD_GROUPS


def _mix_specs(tb):
    row = pl.BlockSpec((tb, 2048), lambda i: (i, 0))
    zlo = pl.BlockSpec((tb, 1024), lambda i: (i, O_Z // 1024))
    zhi = pl.BlockSpec((tb, 1024), lambda i: (i, O_Z // 1024 + 1))
    vec = pl.BlockSpec((1, 2048), lambda i: (0, 0))
    return row, zlo, zhi, vec


def _mix_fwd(attn, y, proj, g_attn, g_ssd):
    t = attn.shape[0]
    tb = _rows(t, 256)

    def body(a_ref, y_ref, zlo_ref, zhi_ref, ga_ref, gs_ref, o_ref):
        av = a_ref[...]
        r = lax.rsqrt(jnp.mean(av * av, axis=-1, keepdims=True) + EPS)
        o_ref[:, :ATTN_WIDTH] = (av * r * ga_ref[...]).astype(BF16)
        for g in range(SSD_GROUPS):
            lo, hi = g * GROUP_W, (g + 1) * GROUP_W
            zref = zlo_ref if g < 4 else zhi_ref
            z = zref[:, lo % 1024:lo % 1024 + GROUP_W]
            yg = y_ref[:, lo:hi] * (z * _sigmoid(z))
            rg = lax.rsqrt(jnp.mean(yg * yg, axis=-1, keepdims=True) + EPS)
            o_ref[:, ATTN_WIDTH + lo:ATTN_WIDTH + hi] = (yg * rg * gs_ref[:, lo:hi]).astype(BF16)

    row, zlo, zhi, vec = _mix_specs(tb)
    return pl.pallas_call(
        body, name="mix_fwd", grid=(t // tb,), in_specs=[row, row, zlo, zhi, vec, vec],
        out_specs=pl.BlockSpec((tb, 4096), lambda i: (i, 0)), out_shape=jax.ShapeDtypeStruct((t, 4096), BF16),
        compiler_params=_cp(("parallel",)))(attn, y, proj, proj, g_attn, g_ssd)


def _mix_bwd(dmix, attn, y, proj, g_attn, g_ssd):
    t = attn.shape[0]
    tb = _rows(t, 256)

    def body(dm_ref, a_ref, y_ref, zlo_ref, zhi_ref, ga_ref, gs_ref, da_ref, dy_ref, dz_ref, dga_ref, dgs_ref):
        i = pl.program_id(0)
        av = a_ref[...]
        dn = dm_ref[:, :ATTN_WIDTH].astype(F32)
        r = lax.rsqrt(jnp.mean(av * av, axis=-1, keepdims=True) + EPS)
        u = dn * ga_ref[...]
        da_ref[...] = r * u - av * (r * r * r * jnp.mean(u * av, axis=-1, keepdims=True))
        dga = jnp.sum(dn * av * r, axis=0, keepdims=True)

        @pl.when(i == 0)
        def _():
            dga_ref[...] = dga

        @pl.when(i > 0)
        def _():
            dga_ref[...] += dga

        for g in range(SSD_GROUPS):
            lo, hi = g * GROUP_W, (g + 1) * GROUP_W
            zref = zlo_ref if g < 4 else zhi_ref
            z = zref[:, lo % 1024:lo % 1024 + GROUP_W]
            yv = y_ref[:, lo:hi]
            sg = _sigmoid(z)
            sz = z * sg
            yg = yv * sz
            rg = lax.rsqrt(jnp.mean(yg * yg, axis=-1, keepdims=True) + EPS)
            do = dm_ref[:, ATTN_WIDTH + lo:ATTN_WIDTH + hi].astype(F32)
            ug = do * gs_ref[:, lo:hi]
            dyg = rg * ug - yg * (rg * rg * rg * jnp.mean(ug * yg, axis=-1, keepdims=True))
            dy_ref[:, lo:hi] = dyg * sz
            dz_ref[:, lo:hi] = (dyg * yv * (sg * (1.0 + z * (1.0 - sg)))).astype(BF16)
            dgs = jnp.sum(do * yg * rg, axis=0, keepdims=True)

            @pl.when(i == 0)
            def _():
                dgs_ref[:, lo:hi] = dgs

            @pl.when(i > 0)
            def _():
                dgs_ref[:, lo:hi] += dgs

    row, zlo, zhi, vec = _mix_specs(tb)
    return pl.pallas_call(
        body, name="mix_bwd", grid=(t // tb,),
        in_specs=[pl.BlockSpec((tb, 4096), lambda i: (i, 0)), row, row, zlo, zhi, vec, vec],
        out_specs=[row, row, row, vec, vec],
        out_shape=[jax.ShapeDtypeStruct((t, 2048), F32), jax.ShapeDtypeStruct((t, 2048), F32),
                   jax.ShapeDtypeStruct((t, 2048), BF16), jax.ShapeDtypeStruct((1, 2048), F32),
                   jax.ShapeDtypeStruct((1, 2048), F32)],
        compiler_params=_cp(("arbitrary",)))(dmix, attn, y, proj, proj, g_attn, g_ssd)


def _adamw(w, g, m, v, name):
    r, c = w.shape
    tb = _rows(r, 256)
    c1 = 1.0 - ADAM_B1 ** ADAM_STEP
    c2 = 1.0 - ADAM_B2 ** ADAM_STEP

    def body(w_ref, g_ref, m_ref, v_ref, d_ref, m2_ref, v2_ref):
        gv = g_ref[...]
        m2 = ADAM_B1 * m_ref[...] + (1.0 - ADAM_B1) * gv
        v2 = ADAM_B2 * v_ref[...] + (1.0 - ADAM_B2) * (gv * gv)
        d_ref[...] = -ADAM_LR * ((m2 / c1) / (jnp.sqrt(v2 / c2) + ADAM_EPS) + ADAM_WD * w_ref[...])
        m2_ref[...] = m2
        v2_ref[...] = v2

    blk = pl.BlockSpec((tb, c), lambda i: (i, 0))
    shp = jax.ShapeDtypeStruct((r, c), F32)
    return pl.pallas_call(body, name=name, grid=(r // tb,), in_specs=[blk] * 4, out_specs=[blk] * 3,
                          out_shape=[shp] * 3, compiler_params=_cp(("parallel",)))(w, g, m, v)


def _sum_own_half(g4, recv, cidx, name):
    _, r, c = g4.shape
    h = r // 2
    tb = _rows(h, 128)
    nh = h // tb

    def body(c_ref, a_ref, b_ref, o_ref):
        o_ref[...] = a_ref[...] + b_ref[...]

    grid_spec = pltpu.PrefetchScalarGridSpec(
        num_scalar_prefetch=1, grid=(N_CHIPS, nh),
        in_specs=[pl.BlockSpec((1, tb, c), lambda j, i, cref: (j, cref[0] * nh + i, 0)),
                  pl.BlockSpec((1, tb, c), lambda j, i, cref: (j, i, 0))],
        out_specs=pl.BlockSpec((1, tb, c), lambda j, i, cref: (j, i, 0)))
    return pl.pallas_call(body, name=name, grid_spec=grid_spec, out_shape=jax.ShapeDtypeStruct((N_CHIPS, h, c), F32),
                          compiler_params=_cp(("parallel", "parallel")))(cidx, g4, recv)


def _sum_four(parts, name):
    _, h, c = parts.shape
    tb = _rows(h, 128)

    def body(p_ref, o_ref):
        o_ref[...] = ((p_ref[3] + p_ref[0]) + p_ref[1]) + p_ref[2]

    return pl.pallas_call(body, name=name, grid=(h // tb,),
                          in_specs=[pl.BlockSpec((N_CHIPS, tb, c), lambda i: (0, i, 0))],
                          out_specs=pl.BlockSpec((tb, c), lambda i: (i, 0)),
                          out_shape=jax.ShapeDtypeStruct((h, c), F32), compiler_params=_cp(("parallel",)))(parts)


def _me():
    return lax.axis_index("x"), lax.axis_index("y"), lax.axis_index("c")


def _flip(v, bit):
    return (1 - v) if bit else v


CHIP_FLIPS = [(1, 0), (0, 1), (1, 1)]


def _allgather_weights(shards):
    n = len(shards)

    def body(*refs):
        ins, outs = refs[:n], refs[n:2 * n]
        send_sems, recv_sems, loc_sems = refs[2 * n:]
        x, y, c = _me()
        chip = 2 * x + y
        sib = (x, y, 1 - c)

        def remote(src, dst, k, to):
            return pltpu.make_async_remote_copy(src_ref=src, dst_ref=dst, send_sem=send_sems.at[k],
                                                recv_sem=recv_sems.at[k], device_id=to, device_id_type=MESH)

        locs = []
        for t in range(n):
            lc = pltpu.make_async_copy(ins[t], outs[t].at[chip], loc_sems.at[t])
            lc.start()
            locs.append(lc)
        sends = []
        for t in range(n):
            h = ins[t].shape[0] // 2
            mine = pl.ds(c * h, h)
            for k, (fx, fy) in enumerate(CHIP_FLIPS):
                cp = remote(ins[t].at[mine], outs[t].at[chip, mine], 6 * t + k, (_flip(x, fx), _flip(y, fy), c))
                cp.start()
                sends.append(cp)
        for t in range(n):
            h = ins[t].shape[0] // 2
            mine = pl.ds(c * h, h)
            for k, (fx, fy) in enumerate(CHIP_FLIPS):
                src_chip = 2 * _flip(x, fx) + _flip(y, fy)
                landed = outs[t].at[src_chip, mine]
                remote(landed, landed, 6 * t + k, (x, y, c)).wait_recv()
                fw = remote(landed, landed, 6 * t + 3 + k, sib)
                fw.start()
                sends.append(fw)
        for t in range(n):
            h = ins[t].shape[0] // 2
            other = pl.ds((1 - c) * h, h)
            for k, (fx, fy) in enumerate(CHIP_FLIPS):
                src_chip = 2 * _flip(x, fx) + _flip(y, fy)
                got = outs[t].at[src_chip, other]
                remote(got, got, 6 * t + 3 + k, (x, y, c)).wait_recv()
        for cp in sends:
            cp.wait_send()
        for lc in locs:
            lc.wait()

    return pl.pallas_call(
        body, name="allgather_weights", in_specs=[HBM_SPEC] * n, out_specs=[HBM_SPEC] * n,
        out_shape=[jax.ShapeDtypeStruct((N_CHIPS,) + s.shape, s.dtype) for s in shards],
        scratch_shapes=[pltpu.SemaphoreType.DMA((6 * n,)), pltpu.SemaphoreType.DMA((6 * n,)),
                        pltpu.SemaphoreType.DMA((n,))],
        compiler_params=pltpu.CompilerParams(has_side_effects=True))(*shards)


def _exchange_halves(g4s):
    n = len(g4s)

    def body(*refs):
        ins, outs = refs[:n], refs[n:2 * n]
        send_sems, recv_sems = refs[2 * n:]
        x, y, c = _me()
        cps = []
        for t in range(n):
            h = ins[t].shape[1] // 2
            cp = pltpu.make_async_remote_copy(
                src_ref=ins[t].at[:, pl.ds((1 - c) * h, h)], dst_ref=outs[t], send_sem=send_sems.at[t],
                recv_sem=recv_sems.at[t], device_id=(x, y, 1 - c), device_id_type=MESH)
            cp.start()
            cps.append(cp)
        for cp in cps:
            cp.wait()

    return pl.pallas_call(
        body, name="grad_exchange_halves", in_specs=[HBM_SPEC] * n, out_specs=[HBM_SPEC] * n,
        out_shape=[jax.ShapeDtypeStruct((N_CHIPS, g.shape[1] // 2, g.shape[2]), g.dtype) for g in g4s],
        scratch_shapes=[pltpu.SemaphoreType.DMA((n,)), pltpu.SemaphoreType.DMA((n,))],
        compiler_params=pltpu.CompilerParams(has_side_effects=True))(*g4s)


def _scatter_to_chips(ps):
    n = len(ps)

    def body(*refs):
        ins, outs = refs[:n], refs[n:2 * n]
        send_sems, recv_sems, loc_sems = refs[2 * n:]
        x, y, c = _me()
        chip = 2 * x + y
        cps, locs = [], []
        for t in range(n):
            lc = pltpu.make_async_copy(ins[t].at[chip], outs[t].at[3], loc_sems.at[t])
            lc.start()
            locs.append(lc)
            for k, (fx, fy) in enumerate(CHIP_FLIPS):
                to_chip = 2 * _flip(x, fx) + _flip(y, fy)
                cp = pltpu.make_async_remote_copy(
                    src_ref=ins[t].at[to_chip], dst_ref=outs[t].at[k], send_sem=send_sems.at[3 * t + k],
                    recv_sem=recv_sems.at[3 * t + k], device_id=(_flip(x, fx), _flip(y, fy), c),
                    device_id_type=MESH)
                cp.start()
                cps.append(cp)
        for cp in cps:
            cp.wait()
        for lc in locs:
            lc.wait()

    return pl.pallas_call(
        body, name="grad_scatter_to_chips", in_specs=[HBM_SPEC] * n, out_specs=[HBM_SPEC] * n,
        out_shape=[jax.ShapeDtypeStruct(p.shape, p.dtype) for p in ps],
        scratch_shapes=[pltpu.SemaphoreType.DMA((3 * n,)), pltpu.SemaphoreType.DMA((3 * n,)),
                        pltpu.SemaphoreType.DMA((n,))],
        compiler_params=pltpu.CompilerParams(has_side_effects=True))(*ps)


def _share_halves(ghs):
    n = len(ghs)

    def body(*refs):
        ins, outs = refs[:n], refs[n:2 * n]
        send_sems, recv_sems, loc_sems = refs[2 * n:]
        x, y, c = _me()
        cps, locs = [], []
        for t in range(n):
            h = ins[t].shape[0]
            mine = pl.ds(c * h, h)
            lc = pltpu.make_async_copy(ins[t], outs[t].at[mine], loc_sems.at[t])
            lc.start()
            locs.append(lc)
            cp = pltpu.make_async_remote_copy(
                src_ref=ins[t], dst_ref=outs[t].at[mine], send_sem=send_sems.at[t], recv_sem=recv_sems.at[t],
                device_id=(x, y, 1 - c), device_id_type=MESH)
            cp.start()
            cps.append(cp)
        for t in range(n):
            h = ins[t].shape[0]
            theirs = outs[t].at[pl.ds((1 - c) * h, h)]
            pltpu.make_async_remote_copy(
                src_ref=ins[t], dst_ref=theirs, send_sem=send_sems.at[t], recv_sem=recv_sems.at[t],
                device_id=(x, y, 1 - c), device_id_type=MESH).wait_recv()
        for cp in cps:
            cp.wait_send()
        for lc in locs:
            lc.wait()

    return pl.pallas_call(
        body, name="grad_share_halves", in_specs=[HBM_SPEC] * n, out_specs=[HBM_SPEC] * n,
        out_shape=[jax.ShapeDtypeStruct((2 * g.shape[0], g.shape[1]), g.dtype) for g in ghs],
        scratch_shapes=[pltpu.SemaphoreType.DMA((n,)), pltpu.SemaphoreType.DMA((n,)),
                        pltpu.SemaphoreType.DMA((n,))],
        compiler_params=pltpu.CompilerParams(has_side_effects=True))(*ghs)


def _allreduce_small(v):
    r = v.shape[0]

    def body(v_ref, o_ref, buf, send_sems, recv_sems):
        x, y, c = _me()
        me = 4 * x + 2 * y + c
        buf[0] = v_ref[...]
        cps = []
        for k in range(1, 8):
            kx, ky, kc = (k >> 2) & 1, (k >> 1) & 1, k & 1
            cp = pltpu.make_async_remote_copy(
                src_ref=v_ref, dst_ref=buf.at[k], send_sem=send_sems.at[k - 1], recv_sem=recv_sems.at[k - 1],
                device_id=(_flip(x, kx), _flip(y, ky), _flip(c, kc)), device_id_type=MESH)
            cp.start()
            cps.append(cp)
        for cp in cps:
            cp.wait()
        acc = buf[me]
        for d in range(1, 8):
            acc = acc + buf[jnp.bitwise_xor(me, d)]
        o_ref[...] = acc

    vm = pl.BlockSpec(memory_space=pltpu.VMEM)
    return pl.pallas_call(
        body, name="allreduce_small", in_specs=[vm], out_specs=vm, out_shape=jax.ShapeDtypeStruct(v.shape, F32),
        scratch_shapes=[pltpu.VMEM((8, r, LANES), F32), pltpu.SemaphoreType.DMA((7,)),
                        pltpu.SemaphoreType.DMA((7,))],
        compiler_params=pltpu.CompilerParams(has_side_effects=True, vmem_limit_bytes=VMEM_LIMIT))(v)


def _reduce_scatter_big(g4s, cidx):
    recvs = _exchange_halves(g4s)
    ps = [_sum_own_half(g, r, cidx, name="grad_sum_pair_%d" % t) for t, (g, r) in enumerate(zip(g4s, recvs))]
    parts = _scatter_to_chips(ps)
    ghs = [_sum_four(p, name="grad_sum_chips_%d" % t) for t, p in enumerate(parts)]
    return _share_halves(ghs)


def _local_step(x, tgt, p, w_in_main, w_in_dt, w_out, w_up, w_down):
    t = x.shape[0]
    tables = _rope_tables(t)
    sinks = p['sinks'].reshape(N_Q_HEADS)

    xn = _rmsnorm_fwd(x, p['norm_mix'], "norm_mix_fwd")
    proj = _matmul(xn, w_in_main, mode='nn', name="in_proj")
    dt_raw = _matmul(xn, w_in_dt, mode='nn', name="in_proj_dt")[:, :SSD_HEADS]
    attn = _attn_fwd(proj, sinks, tables)
    conv_b = p['ssd_conv_b']
    xbc = _conv_fwd(proj, p['ssd_conv_w'], conv_b, col0=O_XBC, width=CONV_CH, act=True, name="ssd_conv_fwd")
    sp = _ssd_params(dt_raw, p['dt_bias'].reshape(-1), p['a_log'].reshape(-1), p['ssd_d'].reshape(-1))
    y, states = _ssd_fwd(xbc, sp)
    mix = _mix_fwd(attn, y, proj, p['attn_out_norm'], p['ssd_norm'])
    h1 = _matmul(mix, w_out, mode='nn', name="out_proj", add=x)
    hn = _rmsnorm_fwd(h1, p['norm_ffn'], "norm_ffn_fwd")
    u0 = _matmul(hn, w_up, mode='nn', name="ffn_up")
    u = _conv_fwd(u0, p['ffn_conv_w'], p['ffn_conv_b'], col0=0, width=2 * D_FF, act=False, name="ffn_conv_fwd")
    a = _swiglu_fwd(u)
    h2 = _matmul(a, w_down, mode='nn', name="ffn_down", add=h1)
    loss, dh2, g_norm_final = _final_loss(h2, p['norm_final'].reshape(1, D_MODEL), tgt)

    g = {}
    da = _matmul(dh2, w_down, mode='nt', name="ffn_down_dx", out_dtype=BF16, tn=1408)
    g['w_down'] = _matmul(a, dh2, mode='tn', name="ffn_down_dw", tm=1408)
    dug, duv = _swiglu_bwd(u, da)
    du = jnp.concatenate([dug, duv], axis=1)
    du0, g['ffn_conv_w'], g['ffn_conv_b'] = _conv_bwd(u0, p['ffn_conv_w'], du, col0=0, width=2 * D_FF,
                                                      name="ffn_conv_bwd")
    dhn = _matmul(du0, w_up, mode='nt', name="ffn_up_dx", out_dtype=BF16)
    g['w_up'] = _matmul(hn, du0, mode='tn', name="ffn_up_dw")
    dh1, g['norm_ffn'] = _rmsnorm_bwd(h1, p['norm_ffn'], dhn, dh2, "norm_ffn_bwd")

    dmix = _matmul(dh1, w_out, mode='nt', name="out_proj_dx", out_dtype=BF16)
    g['w_out'] = _matmul(mix, dh1, mode='tn', name="out_proj_dw")
    dattn, dy, dz, g['attn_out_norm'], g['ssd_norm'] = _mix_bwd(dmix, attn, y, proj, p['attn_out_norm'],
                                                                p['ssd_norm'])
    dq, dk, dv, dsink = _attn_bwd(proj, sinks, tables, dattn)
    g['sinks'] = dsink[:, :, 0].reshape(1, N_Q_HEADS)
    dxs, dbm, dcm, ddt8, dpar = _ssd_bwd(xbc, sp, states, dy)
    g['dt_bias'] = dpar[:, 0, :].reshape(1, SSD_HEADS)
    g['a_log'] = dpar[:, 1, :].reshape(1, SSD_HEADS)
    g['ssd_d'] = dpar[:, 2, :].reshape(1, SSD_HEADS)
    dxbc_act = jnp.concatenate([dxs, dbm, dcm], axis=1)
    dconv = _conv_silu_dact(proj, p['ssd_conv_w'], conv_b, dxbc_act, col0=O_XBC, width=CONV_CH,
                            name="ssd_conv_dact")
    dxbc, g['ssd_conv_w'], g['ssd_conv_b'] = _conv_bwd(proj, p['ssd_conv_w'], dconv, col0=O_XBC, width=CONV_CH,
                                                       name="ssd_conv_bwd")
    dproj = jnp.concatenate([dq, dk, dv, dz, dxbc], axis=1)
    ddt = ddt8.transpose(1, 0, 2).reshape(t, SSD_HEADS)
    ddt_pad = jnp.pad(ddt, ((0, 0), (0, LANES - SSD_HEADS))).astype(BF16)
    dxn_dt = _matmul(ddt_pad, w_in_dt, mode='nt', name="in_proj_dt_dx")
    dxn = _matmul(dproj, w_in_main, mode='nt', name="in_proj_dx", out_dtype=BF16, add=dxn_dt)
    g['w_in'] = (_matmul(xn, dproj, mode='tn', name="in_proj_dw"),
                 _matmul(xn, ddt_pad, mode='tn', name="in_proj_dt_dw"))
    dx, g['norm_mix'] = _rmsnorm_bwd(x, p['norm_mix'], dxn, dh1, "norm_mix_bwd")
    g['norm_final'] = g_norm_final
    return loss, dx, g


def _pack(arrs):
    flat = jnp.concatenate([a.reshape(-1) for a in arrs])
    n = flat.shape[0]
    rows = -(-n // LANES)
    rows = -(-rows // 8) * 8
    return jnp.pad(flat, (0, rows * LANES - n)).reshape(rows, LANES)


def _unpack(packed, shapes):
    flat = packed.reshape(-1)
    out, off = [], 0
    for s in shapes:
        n = 1
        for d in s:
            n *= d
        out.append(flat[off:off + n].reshape(s))
        off += n
    return out


def _gather_full_weights(w_in, w_out, w_up, w_down):
    shards = [w_in[0].astype(BF16), w_out[0].astype(BF16), w_up[0].astype(BF16), w_down[0].astype(BF16)]
    g_in, g_out, g_up, g_down = _allgather_weights(shards)
    full_in = jnp.concatenate([g_in[j] for j in range(N_CHIPS)], axis=1)
    w_in_main = full_in[:, :MAIN_WIDTH]
    w_in_dt = jnp.pad(full_in[:, MAIN_WIDTH:], ((0, 0), (0, LANES - SSD_HEADS)))
    full_up = jnp.concatenate([g_up[j] for j in range(N_CHIPS)], axis=1)
    full_out = g_out.reshape(-1, D_MODEL)
    full_down = g_down.reshape(-1, D_MODEL)
    return w_in_main, w_in_dt, full_out, full_up, full_down


def _owner_major(gfull, axis):
    if axis == 0:
        return gfull.reshape(N_CHIPS, gfull.shape[0] // N_CHIPS, gfull.shape[1])
    cw = gfull.shape[1] // N_CHIPS
    return jnp.stack([gfull[:, j * cw:(j + 1) * cw] for j in range(N_CHIPS)], axis=0)


def kernel(x, norm_mix, w_in, sinks, attn_out_norm, ssd_conv_w, ssd_conv_b, dt_bias, a_log, ssd_d, ssd_norm, w_out, norm_ffn, w_up, ffn_conv_w, ffn_conv_b, w_down, norm_final, loss_target, m_norm_mix, m_w_in, m_sinks, m_attn_out_norm, m_ssd_conv_w, m_ssd_conv_b, m_dt_bias, m_a_log, m_ssd_d, m_ssd_norm, m_w_out, m_norm_ffn, m_w_up, m_ffn_conv_w, m_ffn_conv_b, m_w_down, m_norm_final, v_norm_mix, v_w_in, v_sinks, v_attn_out_norm, v_ssd_conv_w, v_ssd_conv_b, v_dt_bias, v_a_log, v_ssd_d, v_ssd_norm, v_w_out, v_norm_ffn, v_w_up, v_ffn_conv_w, v_ffn_conv_b, v_w_down, v_norm_final):
    args = dict(locals())
    w = {n: args[n] for n in WEIGHTS}
    m = {n: args['m_' + n] for n in WEIGHTS}
    v = {n: args['v_' + n] for n in WEIGHTS}
    xi, yi, ci = _me()
    chip = 2 * xi + yi
    cidx = jnp.reshape(ci, (1,)).astype(jnp.int32)

    def place(shard, full_cols):
        z = jnp.zeros((shard.shape[0], full_cols), F32)
        return lax.dynamic_update_slice(z, shard * 0.5, (0, chip * shard.shape[1]))

    conv_pack = _pack([place(ssd_conv_w[0], CONV_CH), place(ffn_conv_w[0], 2 * D_FF)])
    conv_full = _allreduce_small(conv_pack)
    ssd_conv_w_full, ffn_conv_w_full = _unpack(conv_full, [(SSD_CONV, CONV_CH), (FFN_CONV, 2 * D_FF)])

    w_in_main, w_in_dt, full_out, full_up, full_down = _gather_full_weights(w_in, w_out, w_up, w_down)
    small = {
        'norm_mix': norm_mix, 'sinks': sinks, 'attn_out_norm': attn_out_norm, 'ssd_conv_w': ssd_conv_w_full,
        'ssd_conv_b': ssd_conv_b, 'dt_bias': dt_bias, 'a_log': a_log, 'ssd_d': ssd_d, 'ssd_norm': ssd_norm,
        'norm_ffn': norm_ffn, 'ffn_conv_w': ffn_conv_w_full, 'ffn_conv_b': ffn_conv_b, 'norm_final': norm_final,
    }
    loss, dx, g = _local_step(x[0], loss_target[0], small, w_in_main, w_in_dt, full_out, full_up, full_down)

    g_in_main, g_in_dt = g['w_in']
    g_in_full = jnp.concatenate([g_in_main, g_in_dt[:, :SSD_HEADS]], axis=1)
    g4s = [_owner_major(g_in_full, 1), _owner_major(g['w_out'], 0), _owner_major(g['w_up'], 1),
           _owner_major(g['w_down'], 0)]
    gbig = dict(zip(BIG, _reduce_scatter_big(g4s, cidx)))

    small_names = [n for n in WEIGHTS if n not in BIG]
    small_g = [loss[:, :1]] + [g[n] for n in small_names]
    small_shapes = [(1, 1)] + [tuple(a.shape) for a in small_g[1:]]
    red = _unpack(_allreduce_small(_pack(small_g)), small_shapes)
    loss_out = red[0].reshape(())
    gsm = dict(zip(small_names, red[1:]))
    gsm['ssd_conv_w'] = lax.dynamic_slice(gsm['ssd_conv_w'], (0, chip * ssd_conv_w.shape[2]),
                                          (SSD_CONV, ssd_conv_w.shape[2]))
    gsm['ffn_conv_w'] = lax.dynamic_slice(gsm['ffn_conv_w'], (0, chip * ffn_conv_w.shape[2]),
                                          (FFN_CONV, ffn_conv_w.shape[2]))

    grads, deltas, new_m, new_v = {}, {}, {}, {}
    for n in BIG:
        grads[n] = gbig[n][None]
        d, m2, v2 = _adamw(w[n][0], gbig[n], m[n][0], v[n][0], name="adamw_" + n)
        deltas[n], new_m[n], new_v[n] = d[None], m2[None], v2[None]
    shapes = [tuple(w[n].shape) for n in small_names]
    gp = _pack([gsm[n] for n in small_names])
    d, m2, v2 = _adamw(_pack([w[n] for n in small_names]), gp, _pack([m[n] for n in small_names]),
                       _pack([v[n] for n in small_names]), name="adamw_small")
    for n, gg, dd, mm, vv in zip(small_names, _unpack(gp, shapes), _unpack(d, shapes), _unpack(m2, shapes),
                                 _unpack(v2, shapes)):
        grads[n], deltas[n], new_m[n], new_v[n] = gg, dd, mm, vv

    return (loss_out, dx[None], *[grads[n] for n in WEIGHTS], *[deltas[n] for n in WEIGHTS],
            *[new_m[n] for n in WEIGHTS], *[new_v[n] for n in WEIGHTS])
```

```python
import functools

import jax
import jax.numpy as jnp
from jax import lax
from jax.experimental import pallas as pl
from jax.experimental.pallas import tpu as pltpu

F32 = jnp.float32
BF16 = jnp.bfloat16

D_MODEL = 2048
N_Q_HEADS = 32
N_KV_HEADS = 8
HEAD_DIM = 64
WINDOW = 128
ATTN_BLOCK = 128
ROT_DIM = 16
ROPE_THETA = 500000.0
SSD_HEADS = 32
SSD_HEAD_DIM = 64
SSD_INNER = 2048
SSD_GROUPS = 8
SSD_STATE = 128
SSD_CONV = 4
SSD_CHUNK = 128
ATTN_WIDTH = 2048
KV_WIDTH = 512
BC_WIDTH = 1024
CONV_CH = 4096
IN_PROJ_WIDTH = 9248
MAIN_WIDTH = 9216
D_FF = 5632
FFN_CONV = 3
EPS = 1e-6
O_Q, O_K, O_V, O_Z, O_XBC, O_DT = 0, 2048, 2560, 3072, 5120, 9216

ADAM_LR = 0.001
ADAM_B1 = 0.9
ADAM_B2 = 0.999
ADAM_EPS = 1e-08
ADAM_WD = 0.01
ADAM_STEP = 10

N_CHIPS = 4
NEG = -1e30
LANES = 128
VMEM_LIMIT = 48 * 1024 * 1024
MESH = pl.DeviceIdType.MESH
HBM_SPEC = pl.BlockSpec(memory_space=pltpu.HBM)

WEIGHTS = ['norm_mix', 'w_in', 'sinks', 'attn_out_norm', 'ssd_conv_w', 'ssd_conv_b', 'dt_bias', 'a_log', 'ssd_d',
           'ssd_norm', 'w_out', 'norm_ffn', 'w_up', 'ffn_conv_w', 'ffn_conv_b', 'w_down', 'norm_final']
BIG = ['w_in', 'w_out', 'w_up', 'w_down']


def _cp(sem=None, vmem=VMEM_LIMIT):
    kw = {'vmem_limit_bytes': vmem}
    if sem is not None:
        kw['dimension_semantics'] = sem
    return pltpu.CompilerParams(**kw)


def _tile(n, pref):
    if n <= pref:
        return n
    t = (pref // LANES) * LANES
    while t > LANES and n % t:
        t -= LANES
    assert n % t == 0, (n, pref)
    return t


def _rows(n, pref):
    t = min(n, pref)
    while n % t:
        t -= 8
    return t


def _iota(shape, dim):
    return lax.broadcasted_iota(jnp.int32, shape, dim)


def _dot(a, b, mode='nn'):
    dn = {'nn': (((1,), (0,)), ((), ())), 'nt': (((1,), (1,)), ((), ())), 'tn': (((0,), (0,)), ((), ()))}[mode]
    return lax.dot_general(a.astype(BF16), b.astype(BF16), dn, preferred_element_type=F32)


def _dot_exact(a, b):
    return lax.dot_general(a, b, (((1,), (0,)), ((), ())), precision=lax.Precision.HIGHEST,
                           preferred_element_type=F32)


def _sigmoid(x):
    return 1.0 / (1.0 + jnp.exp(-x))


def _softplus(x):
    return jnp.maximum(x, 0.0) + jnp.log(1.0 + jnp.exp(-jnp.abs(x)))


def _matmul(a, b, *, mode, name, out_dtype=F32, add=None, tm=1024, tn=1024, tk=512):
    if mode == 'nn':
        (m, k), (k2, n) = a.shape, b.shape
    elif mode == 'nt':
        (m, k), (n, k2) = a.shape, b.shape
    else:
        (k, m), (k2, n) = a.shape, b.shape
    assert k == k2, (a.shape, b.shape, mode)
    tm, tn, tk = _tile(m, tm), _tile(n, tn), _tile(k, tk)
    nk = k // tk
    has_add = add is not None

    def body(*refs):
        if has_add:
            a_ref, b_ref, add_ref, o_ref, acc = refs
        else:
            a_ref, b_ref, o_ref, acc = refs
        kk = pl.program_id(2)

        @pl.when(kk == 0)
        def _():
            acc[...] = jnp.zeros_like(acc)

        acc[...] += _dot(a_ref[...], b_ref[...], mode)

        @pl.when(kk == nk - 1)
        def _():
            r = acc[...]
            if has_add:
                r = r + add_ref[...].astype(F32)
            o_ref[...] = r.astype(out_dtype)

    if mode == 'tn':
        a_spec = pl.BlockSpec((tk, tm), lambda i, j, kk: (kk, i))
    else:
        a_spec = pl.BlockSpec((tm, tk), lambda i, j, kk: (i, kk))
    if mode == 'nt':
        b_spec = pl.BlockSpec((tn, tk), lambda i, j, kk: (j, kk))
    else:
        b_spec = pl.BlockSpec((tk, tn), lambda i, j, kk: (kk, j))
    o_spec = pl.BlockSpec((tm, tn), lambda i, j, kk: (i, j))
    in_specs = [a_spec, b_spec] + ([o_spec] if has_add else [])
    args = (a, b) + ((add,) if has_add else ())
    return pl.pallas_call(
        body, name=name, grid=(m // tm, n // tn, nk), in_specs=in_specs, out_specs=o_spec,
        out_shape=jax.ShapeDtypeStruct((m, n), out_dtype), scratch_shapes=[pltpu.VMEM((tm, tn), F32)],
        compiler_params=_cp(("parallel", "parallel", "arbitrary")))(*args)


def _rmsnorm_fwd(x, g, name):
    t, d = x.shape
    tb = _rows(t, 256)

    def body(x_ref, g_ref, o_ref):
        xv = x_ref[...]
        r = lax.rsqrt(jnp.mean(xv * xv, axis=-1, keepdims=True) + EPS)
        o_ref[...] = (xv * r * g_ref[...]).astype(BF16)

    return pl.pallas_call(
        body, name=name, grid=(t // tb,),
        in_specs=[pl.BlockSpec((tb, d), lambda i: (i, 0)), pl.BlockSpec((1, d), lambda i: (0, 0))],
        out_specs=pl.BlockSpec((tb, d), lambda i: (i, 0)), out_shape=jax.ShapeDtypeStruct((t, d), BF16),
        compiler_params=_cp(("parallel",)))(x, g)


def _rmsnorm_bwd(x, g, dy, res, name):
    t, d = x.shape
    tb = _rows(t, 256)

    def body(x_ref, g_ref, dy_ref, res_ref, dx_ref, dg_ref):
        i = pl.program_id(0)
        xv = x_ref[...]
        dyv = dy_ref[...].astype(F32)
        r = lax.rsqrt(jnp.mean(xv * xv, axis=-1, keepdims=True) + EPS)
        u = dyv * g_ref[...]
        dx = r * u - xv * (r * r * r * jnp.mean(u * xv, axis=-1, keepdims=True))
        dx_ref[...] = dx + res_ref[...]
        part = jnp.sum(dyv * xv * r, axis=0, keepdims=True)

        @pl.when(i == 0)
        def _():
            dg_ref[...] = part

        @pl.when(i > 0)
        def _():
            dg_ref[...] += part

    row = pl.BlockSpec((tb, d), lambda i: (i, 0))
    vec = pl.BlockSpec((1, d), lambda i: (0, 0))
    return pl.pallas_call(
        body, name=name, grid=(t // tb,), in_specs=[row, vec, row, row], out_specs=[row, vec],
        out_shape=[jax.ShapeDtypeStruct((t, d), F32), jax.ShapeDtypeStruct((1, d), F32)],
        compiler_params=_cp(("arbitrary",)))(x, g, dy, res)


def _final_loss(h, g, tgt):
    t, d = h.shape
    tb = _rows(t, 256)

    def body(h_ref, g_ref, t_ref, loss_ref, dh_ref, dg_ref):
        i = pl.program_id(0)
        hv = h_ref[...]
        gv = g_ref[...]
        r = lax.rsqrt(jnp.mean(hv * hv, axis=-1, keepdims=True) + EPS)
        y = hv * r * gv
        diff = y - t_ref[...]
        lpart = jnp.sum(jnp.sum(diff * diff, axis=1, keepdims=True), axis=0, keepdims=True) * (0.5 / d)
        dy = diff * (1.0 / d)
        u = dy * gv
        dh_ref[...] = r * u - hv * (r * r * r * jnp.mean(u * hv, axis=-1, keepdims=True))
        gpart = jnp.sum(dy * hv * r, axis=0, keepdims=True)
        lrow = jnp.broadcast_to(lpart, (1, LANES))

        @pl.when(i == 0)
        def _():
            loss_ref[...] = lrow
            dg_ref[...] = gpart

        @pl.when(i > 0)
        def _():
            loss_ref[...] += lrow
            dg_ref[...] += gpart

    row = pl.BlockSpec((tb, d), lambda i: (i, 0))
    vec = pl.BlockSpec((1, d), lambda i: (0, 0))
    return pl.pallas_call(
        body, name="final_loss", grid=(t // tb,), in_specs=[row, vec, row],
        out_specs=[pl.BlockSpec((1, LANES), lambda i: (0, 0)), row, vec],
        out_shape=[jax.ShapeDtypeStruct((1, LANES), F32), jax.ShapeDtypeStruct((t, d), F32),
                   jax.ShapeDtypeStruct((1, d), F32)],
        compiler_params=_cp(("arbitrary",)))(h, g, tgt)


def _rope_tables(t):
    pos = jnp.arange(t, dtype=F32)
    inv = 1.0 / (ROPE_THETA ** (jnp.arange(0, ROT_DIM, 2, dtype=F32) / ROT_DIM))
    ang = pos[:, None] * inv[None, :]
    cos, sin = jnp.cos(ang), jnp.sin(ang)
    half = ROT_DIM // 2
    rest = HEAD_DIM - ROT_DIM
    c = jnp.concatenate([cos, cos, jnp.ones((t, rest), F32)], axis=1)
    s1 = jnp.concatenate([-sin, jnp.zeros((t, half + rest), F32)], axis=1)
    s2 = jnp.concatenate([jnp.zeros((t, half), F32), sin, jnp.zeros((t, rest), F32)], axis=1)
    return tuple(jnp.tile(v, (1, LANES // HEAD_DIM)) for v in (c, s1, s2))


def _rope(x, c, s1, s2):
    half = ROT_DIM // 2
    return x * c + pltpu.roll(x, LANES - half, 1) * s1 + pltpu.roll(x, half, 1) * s2


def _rope_t(g, c, s1, s2):
    half = ROT_DIM // 2
    return g * c + pltpu.roll(g * s1, half, 1) + pltpu.roll(g * s2, LANES - half, 1)


def _attn_mask(i):
    qi = _iota((ATTN_BLOCK, 2 * ATTN_BLOCK), 0)
    kj = _iota((ATTN_BLOCK, 2 * ATTN_BLOCK), 1)
    rel = qi + ATTN_BLOCK - kj
    first_key = jnp.where(i > 0, 0, ATTN_BLOCK)
    return (rel >= 0) & (rel < WINDOW) & (kj >= first_key)


def _half_masks():
    lane = _iota((1, LANES), 1)
    return [(lane < HEAD_DIM).astype(F32), (lane >= HEAD_DIM).astype(F32)]


def _attn_specs(nb_clamp):
    blk = ATTN_BLOCK
    kb, vb = O_K // LANES, O_V // LANES

    def cur(i):
        return jnp.minimum(i, nb_clamp)

    def prev(i):
        return jnp.maximum(jnp.minimum(i, nb_clamp + 1) - 1, 0)

    q = pl.BlockSpec((blk, 512), lambda p, i: (cur(i), p))
    kc = pl.BlockSpec((blk, LANES), lambda p, i: (cur(i), kb + p))
    kp = pl.BlockSpec((blk, LANES), lambda p, i: (prev(i), kb + p))
    vc = pl.BlockSpec((blk, LANES), lambda p, i: (cur(i), vb + p))
    vp = pl.BlockSpec((blk, LANES), lambda p, i: (prev(i), vb + p))
    tc = pl.BlockSpec((blk, LANES), lambda p, i: (cur(i), 0))
    tp = pl.BlockSpec((blk, LANES), lambda p, i: (prev(i), 0))
    return q, kc, kp, vc, vp, tc, tp


def _attn_fwd(proj, sinks, tables):
    t = proj.shape[0]
    nb = t // ATTN_BLOCK
    scale = HEAD_DIM ** -0.5

    def body(sink_ref, q_ref, kc_ref, kp_ref, vc_ref, vp_ref, cc_ref, s1c_ref, s2c_ref, cp_ref, s1p_ref, s2p_ref,
             o_ref):
        p = pl.program_id(0)
        i = pl.program_id(1)
        cc, s1c, s2c = cc_ref[...], s1c_ref[...], s2c_ref[...]
        kband = jnp.concatenate([_rope(kp_ref[...], cp_ref[...], s1p_ref[...], s2p_ref[...]),
                                 _rope(kc_ref[...], cc, s1c, s2c)], axis=0).astype(BF16)
        vband = jnp.concatenate([vp_ref[...], vc_ref[...]], axis=0)
        hm = _half_masks()
        vsel = [(vband * hm[j]).astype(BF16) for j in range(2)]
        valid = _attn_mask(i)
        for qb in range(4):
            qr = _rope(q_ref[:, qb * LANES:(qb + 1) * LANES], cc, s1c, s2c)
            acc = jnp.zeros((ATTN_BLOCK, LANES), F32)
            for half in range(2):
                hh = qb * 2 + half
                j = hh // 4
                qs = qr * hm[half]
                if half != j:
                    qs = pltpu.roll(qs, HEAD_DIM, 1)
                s = jnp.where(valid, _dot(qs, kband, 'nt') * scale, NEG)
                sink = sink_ref[p * 8 + hh]
                m = jnp.maximum(jnp.max(s, axis=1, keepdims=True), sink)
                pe = jnp.exp(s - m)
                den = jnp.sum(pe, axis=1, keepdims=True) + jnp.exp(sink - m)
                o = _dot(pe / den, vsel[j])
                if half != j:
                    o = pltpu.roll(o, HEAD_DIM, 1)
                acc = acc + o
            o_ref[:, qb * LANES:(qb + 1) * LANES] = acc

    q, kc, kp, vc, vp, tc, tp = _attn_specs(nb - 1)
    smem = pl.BlockSpec(memory_space=pltpu.SMEM)
    return pl.pallas_call(
        body, name="attn_fwd", grid=(4, nb),
        in_specs=[smem, q, kc, kp, vc, vp, tc, tc, tc, tp, tp, tp],
        out_specs=pl.BlockSpec((ATTN_BLOCK, 512), lambda p, i: (i, p)),
        out_shape=jax.ShapeDtypeStruct((t, ATTN_WIDTH), F32),
        compiler_params=_cp(("parallel", "arbitrary")))(sinks, proj, proj, proj, proj, proj, *tables, *tables)


def _attn_bwd(proj, sinks, tables, dout):
    t = proj.shape[0]
    nb = t // ATTN_BLOCK
    scale = HEAD_DIM ** -0.5

    def body(sink_ref, q_ref, kc_ref, kp_ref, vc_ref, vp_ref, cc_ref, s1c_ref, s2c_ref, cp_ref, s1p_ref, s2p_ref,
             do_ref, dq_ref, dk_ref, dv_ref, ds_ref, carry_k, carry_v):
        p = pl.program_id(0)
        i = pl.program_id(1)
        ptab = (cp_ref[...], s1p_ref[...], s2p_ref[...])

        @pl.when(i == 0)
        def _():
            carry_k[...] = jnp.zeros_like(carry_k)
            carry_v[...] = jnp.zeros_like(carry_v)
            ds_ref[...] = jnp.zeros_like(ds_ref)

        @pl.when(i < nb)
        def _():
            cc, s1c, s2c = cc_ref[...], s1c_ref[...], s2c_ref[...]
            kband = jnp.concatenate([_rope(kp_ref[...], *ptab), _rope(kc_ref[...], cc, s1c, s2c)], axis=0)
            vband = jnp.concatenate([vp_ref[...], vc_ref[...]], axis=0)
            hm = _half_masks()
            kband16 = kband.astype(BF16)
            ksel = [(kband * hm[j]).astype(BF16) for j in range(2)]
            vband16 = vband.astype(BF16)
            vsel = [(vband * hm[j]).astype(BF16) for j in range(2)]
            valid = _attn_mask(i)
            dkb = jnp.zeros((2 * ATTN_BLOCK, LANES), F32)
            dvb = jnp.zeros((2 * ATTN_BLOCK, LANES), F32)
            row8 = _iota((8, LANES), 0)
            dsink = jnp.zeros((8, LANES), F32)
            for qb in range(4):
                qr = _rope(q_ref[:, qb * LANES:(qb + 1) * LANES], cc, s1c, s2c)
                dob = do_ref[:, qb * LANES:(qb + 1) * LANES]
                dqb = jnp.zeros((ATTN_BLOCK, LANES), F32)
                for half in range(2):
                    hh = qb * 2 + half
                    j = hh // 4
                    qs = qr * hm[half]
                    dos = dob * hm[half]
                    if half != j:
                        qs = pltpu.roll(qs, HEAD_DIM, 1)
                        dos = pltpu.roll(dos, HEAD_DIM, 1)
                    qs16 = qs.astype(BF16)
                    dos16 = dos.astype(BF16)
                    s = jnp.where(valid, _dot(qs16, kband16, 'nt') * scale, NEG)
                    sink = sink_ref[p * 8 + hh]
                    m = jnp.maximum(jnp.max(s, axis=1, keepdims=True), sink)
                    pe = jnp.exp(s - m)
                    psink = jnp.exp(sink - m)
                    den = jnp.sum(pe, axis=1, keepdims=True) + psink
                    pr = pe / den
                    dvb = dvb + _dot(pr.T, dos16)
                    dp = _dot(dos16, vband16, 'nt')
                    delta = jnp.sum(pr * dp, axis=1, keepdims=True)
                    dsc = pr * (dp - delta) * scale
                    dsink = dsink + jnp.where(row8 == hh, -jnp.sum(psink / den * delta), 0.0)
                    dqh = _dot(dsc, ksel[j])
                    if half != j:
                        dqh = pltpu.roll(dqh, HEAD_DIM, 1)
                    dqb = dqb + dqh
                    dkb = dkb + _dot(dsc.T, qs16)
                dq_ref[:, qb * LANES:(qb + 1) * LANES] = _rope_t(dqb, cc, s1c, s2c).astype(BF16)
            ds_ref[0] += dsink
            dk_ref[...] = _rope_t(carry_k[...] + dkb[:ATTN_BLOCK], *ptab).astype(BF16)
            dv_ref[...] = (carry_v[...] + dvb[:ATTN_BLOCK]).astype(BF16)
            carry_k[...] = dkb[ATTN_BLOCK:]
            carry_v[...] = dvb[ATTN_BLOCK:]

        @pl.when(i == nb)
        def _():
            dk_ref[...] = _rope_t(carry_k[...], *ptab).astype(BF16)
            dv_ref[...] = carry_v[...].astype(BF16)

    q, kc, kp, vc, vp, tc, tp = _attn_specs(nb - 1)
    smem = pl.BlockSpec(memory_space=pltpu.SMEM)
    qblk = pl.BlockSpec((ATTN_BLOCK, 512), lambda p, i: (jnp.minimum(i, nb - 1), p))
    kvout = pl.BlockSpec((ATTN_BLOCK, LANES), lambda p, i: (jnp.maximum(i - 1, 0), p))
    return pl.pallas_call(
        body, name="attn_bwd", grid=(4, nb + 1),
        in_specs=[smem, q, kc, kp, vc, vp, tc, tc, tc, tp, tp, tp, qblk],
        out_specs=[qblk, kvout, kvout, pl.BlockSpec((1, 8, LANES), lambda p, i: (p, 0, 0))],
        out_shape=[jax.ShapeDtypeStruct((t, ATTN_WIDTH), BF16), jax.ShapeDtypeStruct((t, KV_WIDTH), BF16),
                   jax.ShapeDtypeStruct((t, KV_WIDTH), BF16), jax.ShapeDtypeStruct((4, 8, LANES), F32)],
        scratch_shapes=[pltpu.VMEM((ATTN_BLOCK, LANES), F32), pltpu.VMEM((ATTN_BLOCK, LANES), F32)],
        compiler_params=_cp(("parallel", "arbitrary")))(sinks, proj, proj, proj, proj, proj, *tables, *tables, dout)


def _shift_rows(x, prev8, j):
    r = pltpu.roll(x, j, 0)
    head = jnp.where(_iota((8, 1), 0) < j, pltpu.roll(prev8, j, 0), r[:8])
    return jnp.concatenate([head, r[8:]], axis=0)


def _shift_rows_up(x, next8, j):
    n = x.shape[0]
    r = pltpu.roll(x, n - j, 0)
    tail = jnp.where(_iota((8, 1), 0) >= 8 - j, pltpu.roll(next8, 8 - j, 0), r[n - 8:])
    return jnp.concatenate([r[:n - 8], tail], axis=0)


def _conv_apply(x, prev8, w, b, taps):
    u = b + x * w[taps - 1:taps]
    for j in range(1, taps):
        u = u + _shift_rows(x, prev8, j) * w[taps - 1 - j:taps - j]
    return u


def _conv_specs(tb, tc, col0, t):
    c0 = col0 // tc
    cur = pl.BlockSpec((tb, tc), lambda j, i: (i, c0 + j))
    prev = pl.BlockSpec((8, tc), lambda j, i: (jnp.maximum(i * (tb // 8) - 1, 0), c0 + j))
    nxt = pl.BlockSpec((8, tc), lambda j, i: (jnp.minimum((i + 1) * (tb // 8), t // 8 - 1), c0 + j))
    return cur, prev, nxt


def _conv_fwd(x, w, b, *, col0, width, act, name):
    t = x.shape[0]
    taps = w.shape[0]
    tb, tc = _rows(t, 512), _tile(width, 1024)
    assert col0 % tc == 0

    def body(x_ref, xp_ref, w_ref, b_ref, o_ref):
        i = pl.program_id(1)
        prev8 = jnp.where(i > 0, xp_ref[...], 0.0)
        u = _conv_apply(x_ref[...], prev8, w_ref[...], b_ref[...], taps)
        if act:
            u = u * _sigmoid(u)
        o_ref[...] = u

    cur, prev, _ = _conv_specs(tb, tc, col0, t)
    par = pl.BlockSpec((taps, tc), lambda j, i: (0, j))
    bias = pl.BlockSpec((1, tc), lambda j, i: (0, j))
    return pl.pallas_call(
        body, name=name, grid=(width // tc, t // tb), in_specs=[cur, prev, par, bias],
        out_specs=pl.BlockSpec((tb, tc), lambda j, i: (i, j)), out_shape=jax.ShapeDtypeStruct((t, width), F32),
        compiler_params=_cp(("parallel", "parallel")))(x, x, w, b)


def _conv_silu_dact(x, w, b, dout, *, col0, width, name):
    t = x.shape[0]
    taps = w.shape[0]
    tb, tc = _rows(t, 512), _tile(width, 1024)

    def body(x_ref, xp_ref, w_ref, b_ref, d_ref, o_ref):
        i = pl.program_id(1)
        prev8 = jnp.where(i > 0, xp_ref[...], 0.0)
        u = _conv_apply(x_ref[...], prev8, w_ref[...], b_ref[...], taps)
        sg = _sigmoid(u)
        o_ref[...] = d_ref[...] * (sg * (1.0 + u * (1.0 - sg)))

    cur, prev, _ = _conv_specs(tb, tc, col0, t)
    par = pl.BlockSpec((taps, tc), lambda j, i: (0, j))
    bias = pl.BlockSpec((1, tc), lambda j, i: (0, j))
    out = pl.BlockSpec((tb, tc), lambda j, i: (i, j))
    return pl.pallas_call(
        body, name=name, grid=(width // tc, t // tb), in_specs=[cur, prev, par, bias, out],
        out_specs=out, out_shape=jax.ShapeDtypeStruct((t, width), F32),
        compiler_params=_cp(("parallel", "parallel")))(x, x, w, b, dout)


def _conv_bwd(x, w, du, *, col0, width, name):
    t = x.shape[0]
    taps = w.shape[0]
    tb, tc = _rows(t, 512), _tile(width, 1024)
    nrow = t // tb

    def body(x_ref, xp_ref, w_ref, du_ref, dun_ref, dx_ref, dw_ref, db_ref):
        i = pl.program_id(1)
        xv = x_ref[...]
        prev8 = jnp.where(i > 0, xp_ref[...], 0.0)
        duv = du_ref[...]
        next8 = jnp.where(i < nrow - 1, dun_ref[...], 0.0)
        wv = w_ref[...]
        dx = duv * wv[taps - 1:taps]
        parts = [jnp.sum(duv * xv, axis=0, keepdims=True)]
        for j in range(1, taps):
            dx = dx + _shift_rows_up(duv, next8, j) * wv[taps - 1 - j:taps - j]
            parts.append(jnp.sum(duv * _shift_rows(xv, prev8, j), axis=0, keepdims=True))
        dx_ref[...] = dx.astype(BF16)
        rowk = _iota((taps, 1), 0)
        dwv = jnp.zeros((taps, tc), F32)
        for j in range(taps):
            dwv = dwv + jnp.where(rowk == taps - 1 - j, parts[j], 0.0)
        dbv = jnp.sum(duv, axis=0, keepdims=True)

        @pl.when(i == 0)
        def _():
            dw_ref[...] = dwv
            db_ref[...] = dbv

        @pl.when(i > 0)
        def _():
            dw_ref[...] += dwv
            db_ref[...] += dbv

    cur, prev, _ = _conv_specs(tb, tc, col0, t)
    dcur, _, dnxt = _conv_specs(tb, tc, 0, t)
    par = pl.BlockSpec((taps, tc), lambda j, i: (0, j))
    bias = pl.BlockSpec((1, tc), lambda j, i: (0, j))
    return pl.pallas_call(
        body, name=name, grid=(width // tc, nrow), in_specs=[cur, prev, par, dcur, dnxt],
        out_specs=[dcur, par, bias],
        out_shape=[jax.ShapeDtypeStruct((t, width), BF16), jax.ShapeDtypeStruct((taps, width), F32),
                   jax.ShapeDtypeStruct((1, width), F32)],
        compiler_params=_cp(("parallel", "arbitrary")))(x, x, w, du, du)


def _swiglu_fwd(u):
    t = u.shape[0]
    tb, tc = _rows(t, 512), _tile(D_FF, 1408)
    nc = D_FF // tc

    def body(g_ref, v_ref, o_ref):
        g = g_ref[...]
        o_ref[...] = (g * _sigmoid(g) * v_ref[...]).astype(BF16)

    return pl.pallas_call(
        body, name="swiglu_fwd", grid=(t // tb, nc),
        in_specs=[pl.BlockSpec((tb, tc), lambda i, j: (i, j)), pl.BlockSpec((tb, tc), lambda i, j: (i, nc + j))],
        out_specs=pl.BlockSpec((tb, tc), lambda i, j: (i, j)), out_shape=jax.ShapeDtypeStruct((t, D_FF), BF16),
        compiler_params=_cp(("parallel", "parallel")))(u, u)


def _swiglu_bwd(u, da):
    t = u.shape[0]
    tb, tc = _rows(t, 512), _tile(D_FF, 1408)
    nc = D_FF // tc

    def body(g_ref, v_ref, da_ref, dg_ref, dv_ref):
        g = g_ref[...]
        dav = da_ref[...].astype(F32)
        sg = _sigmoid(g)
        dg_ref[...] = dav * v_ref[...] * (sg * (1.0 + g * (1.0 - sg)))
        dv_ref[...] = dav * g * sg

    lo = pl.BlockSpec((tb, tc), lambda i, j: (i, j))
    hi = pl.BlockSpec((tb, tc), lambda i, j: (i, nc + j))
    dg, dv = pl.pallas_call(
        body, name="swiglu_bwd", grid=(t // tb, nc), in_specs=[lo, hi, lo], out_specs=[lo, lo],
        out_shape=[jax.ShapeDtypeStruct((t, D_FF), F32), jax.ShapeDtypeStruct((t, D_FF), F32)],
        compiler_params=_cp(("parallel", "parallel")))(u, u, da)
    return dg, dv


def _head_masks():
    lane = _iota((1, 4 * SSD_HEAD_DIM), 1)
    return [((lane >= r * SSD_HEAD_DIM) & (lane < (r + 1) * SSD_HEAD_DIM)).astype(F32) for r in range(4)]


def _expand4(v4, masks):
    lane4 = _iota((1, 4), 1)
    out = 0.0
    for r in range(4):
        out = out + jnp.sum(jnp.where(lane4 == r, v4, 0.0), axis=1, keepdims=True) * masks[r]
    return out


def _collapse4(ve, masks):
    lane4 = _iota((1, 4), 1)
    out = 0.0
    for r in range(4):
        col = jnp.sum(ve * masks[r], axis=1, keepdims=True) * (1.0 / SSD_HEAD_DIM)
        out = out + jnp.where(lane4 == r, col, 0.0)
    return out


def _segsum(v, masks):
    out = 0.0
    for r in range(4):
        out = out + jnp.sum(v * masks[r], axis=1, keepdims=True) * masks[r]
    return out


def _ssd_common(raw4, prow, rawr4, bcol, acol, masks):
    n = SSD_CHUNK
    dt4 = _softplus(raw4 + prow[0:1, :])
    a4 = -jnp.exp(prow[1:2, :])
    dt_e = _expand4(dt4, masks)
    a_e = _expand4(a4, masks)
    d_e = _expand4(prow[2:3, :], masks)
    tril = (_iota((n, n), 0) >= _iota((n, n), 1)).astype(F32)
    acs_e = _dot_exact(tril, dt_e * a_e)
    last_e = acs_e[n - 1:n, :]
    dtr4 = _softplus(rawr4 + bcol)
    triu = (_iota((n, n), 0) <= _iota((n, n), 1)).astype(F32)
    acs_r4 = _dot_exact(dtr4 * (-jnp.exp(acol)), triu)
    return dt_e, a_e, d_e, acs_e, last_e, acs_r4


def _decay_matrix(acs_e, acs_r4, r, masks):
    n = SSD_CHUNK
    col = jnp.sum(acs_e * masks[r], axis=1, keepdims=True) * (1.0 / SSD_HEAD_DIM)
    seg = col - acs_r4[r:r + 1, :]
    causal = _iota((n, n), 0) >= _iota((n, n), 1)
    return jnp.exp(jnp.where(causal, seg, NEG))


def _ssd_specs(t, rev):
    nc = t // SSD_CHUNK
    xb, bb, cb = 0, SSD_INNER // SSD_STATE, (SSD_INNER + BC_WIDTH) // SSD_STATE

    def ch(c):
        return (nc - 1 - c) if rev else c

    x = pl.BlockSpec((SSD_CHUNK, 256), lambda g, c: (ch(c), xb + g))
    bm = pl.BlockSpec((SSD_CHUNK, SSD_STATE), lambda g, c: (ch(c), bb + g))
    cm = pl.BlockSpec((SSD_CHUNK, SSD_STATE), lambda g, c: (ch(c), cb + g))
    dtc = pl.BlockSpec((1, SSD_CHUNK, 4), lambda g, c: (g, ch(c), 0))
    dtr = pl.BlockSpec((1, 4, SSD_CHUNK), lambda g, c: (g, 0, ch(c)))
    prow = pl.BlockSpec((1, 3, 4), lambda g, c: (g, 0, 0))
    pcol = pl.BlockSpec((1, 4, 1), lambda g, c: (g, 0, 0))
    st = pl.BlockSpec((1, 1, SSD_STATE, 256), lambda g, c: (g, ch(c), 0, 0))
    return x, bm, cm, dtc, dtr, prow, pcol, st, ch


def _ssd_params(dt_raw, dt_bias, a_log, ssd_d):
    t = dt_raw.shape[0]
    dtc = dt_raw.reshape(t, SSD_GROUPS, 4).transpose(1, 0, 2)
    dtr = dt_raw.reshape(t, SSD_GROUPS, 4).transpose(1, 2, 0)
    prow = jnp.stack([dt_bias.reshape(SSD_GROUPS, 4), a_log.reshape(SSD_GROUPS, 4),
                      ssd_d.reshape(SSD_GROUPS, 4)], axis=1)
    bcol = dt_bias.reshape(SSD_GROUPS, 4, 1)
    acol = a_log.reshape(SSD_GROUPS, 4, 1)
    return dtc, dtr, prow, bcol, acol


def _ssd_fwd(xbc, params):
    t = xbc.shape[0]
    nc = t // SSD_CHUNK
    dtc, dtr, prow, bcol, acol = params

    def body(x_ref, b_ref, c_ref, dtc_ref, dtr_ref, prow_ref, bcol_ref, acol_ref, y_ref, st_ref, s_scr):
        c = pl.program_id(1)

        @pl.when(c == 0)
        def _():
            s_scr[...] = jnp.zeros_like(s_scr)

        masks = _head_masks()
        dt_e, a_e, d_e, acs_e, last_e, acs_r4 = _ssd_common(
            dtc_ref[0], prow_ref[0], dtr_ref[0], bcol_ref[0], acol_ref[0], masks)
        xv = x_ref[...]
        bm, cm = b_ref[...], c_ref[...]
        s = s_scr[...]
        st_ref[0, 0] = s
        xdt = xv * dt_e
        cb = _dot(cm, bm, 'nt')
        y = _dot(cm, s) * jnp.exp(acs_e) + xv * d_e
        for r in range(4):
            mr = cb * _decay_matrix(acs_e, acs_r4, r, masks)
            y = y + _dot(mr, xdt * masks[r])
        y_ref[...] = y
        w = xdt * jnp.exp(last_e - acs_e)
        s_scr[...] = s * jnp.exp(last_e) + _dot(bm.T, w)

    x, bm, cm, dtcs, dtrs, prs, pcs, st, _ = _ssd_specs(t, False)
    return pl.pallas_call(
        body, name="ssd_fwd", grid=(SSD_GROUPS, nc), in_specs=[x, bm, cm, dtcs, dtrs, prs, pcs, pcs],
        out_specs=[pl.BlockSpec((SSD_CHUNK, 256), lambda g, c: (c, g)), st],
        out_shape=[jax.ShapeDtypeStruct((t, SSD_INNER), F32),
                   jax.ShapeDtypeStruct((SSD_GROUPS, nc, SSD_STATE, 256), F32)],
        scratch_shapes=[pltpu.VMEM((SSD_STATE, 256), F32)],
        compiler_params=_cp(("parallel", "arbitrary")))(xbc, xbc, xbc, dtc, dtr, prow, bcol, acol)


def _ssd_bwd(xbc, params, states, dy):
    t = xbc.shape[0]
    nc = t // SSD_CHUNK
    n = SSD_CHUNK
    dtc, dtr, prow, bcol, acol = params

    def body(x_ref, b_ref, c_ref, dtc_ref, dtr_ref, prow_ref, bcol_ref, acol_ref, st_ref, dy_ref,
             dx_ref, db_ref, dc_ref, ddt_ref, dp_ref, ds_scr):
        c = pl.program_id(1)

        @pl.when(c == 0)
        def _():
            ds_scr[...] = jnp.zeros_like(ds_scr)
            dp_ref[...] = jnp.zeros_like(dp_ref)

        masks = _head_masks()
        raw4 = dtc_ref[0]
        prw = prow_ref[0]
        dt_e, a_e, d_e, acs_e, last_e, acs_r4 = _ssd_common(raw4, prw, dtr_ref[0], bcol_ref[0], acol_ref[0], masks)
        xv = x_ref[...]
        bm, cm = b_ref[...], c_ref[...]
        s = st_ref[0, 0]
        ds = ds_scr[...]
        dyv = dy_ref[...]
        e_e = jnp.exp(acs_e)
        dec_e = jnp.exp(last_e - acs_e)
        cd_e = jnp.exp(last_e)
        xdt = xv * dt_e
        w = xdt * dec_e
        b16, c16, s16, ds16 = bm.astype(BF16), cm.astype(BF16), s.astype(BF16), ds.astype(BF16)
        cb = _dot(c16, b16, 'nt')
        yoff_raw = _dot(c16, s16)
        dye = dyv * e_e
        dye16 = dye.astype(BF16)
        dcm = _dot(dye16, s16, 'nt')
        ds_scr[...] = ds * cd_e + _dot(cm.T, dye16)
        dacs_e = _segsum(dyv * yoff_raw, masks) * e_e
        dw = _dot(b16, ds16)
        dbm = _dot(w, ds16, 'nt')
        tdec = _segsum(dw * xdt, masks) * dec_e
        dacs_e = dacs_e - tdec
        dlast_e = jnp.sum(tdec, axis=0, keepdims=True)
        dxdt = dw * dec_e
        dlast_e = dlast_e + _segsum(jnp.sum(ds * s, axis=0, keepdims=True), masks) * cd_e
        dcb = jnp.zeros((n, n), F32)
        for r in range(4):
            lm = _decay_matrix(acs_e, acs_r4, r, masks)
            mr = cb * lm
            dyr16 = (dyv * masks[r]).astype(BF16)
            dm = _dot(dyr16, xdt * masks[r], 'nt')
            dcb = dcb + dm * lm
            dseg = dm * mr
            dcol = jnp.sum(dseg, axis=1, keepdims=True) - jnp.sum(dseg.T, axis=1, keepdims=True)
            dacs_e = dacs_e + dcol * masks[r]
            dxdt = dxdt + _dot(mr.T, dyr16)
        dcm = dcm + _dot(dcb, b16)
        dbm = dbm + _dot(dcb.T, c16)
        dacs_e = dacs_e + jnp.where(_iota((n, 1), 0) == n - 1, dlast_e, 0.0)
        triu = (_iota((n, n), 0) <= _iota((n, n), 1)).astype(F32)
        ddta_e = _dot_exact(triu, dacs_e)
        ddt_e = ddta_e * a_e + _segsum(dxdt * xv, masks)
        dx_ref[...] = dxdt * dt_e + dyv * d_e
        db_ref[...] = dbm
        dc_ref[...] = dcm
        draw_e = ddt_e * _sigmoid(_expand4(raw4 + prw[0:1, :], masks))
        ddt_ref[0] = _collapse4(draw_e, masks)
        dbias = _collapse4(jnp.sum(draw_e, axis=0, keepdims=True), masks)
        dalog = _collapse4(jnp.sum(ddta_e * dt_e, axis=0, keepdims=True) * a_e, masks)
        dd = _collapse4(jnp.sum(_segsum(dyv * xv, masks), axis=0, keepdims=True), masks)
        row3 = _iota((3, 1), 0)
        dp_ref[0] += (jnp.where(row3 == 0, dbias, 0.0) + jnp.where(row3 == 1, dalog, 0.0)
                      + jnp.where(row3 == 2, dd, 0.0))

    x, bm, cm, dtcs, dtrs, prs, pcs, st, ch = _ssd_specs(t, True)
    yblk = pl.BlockSpec((SSD_CHUNK, 256), lambda g, c: (ch(c), g))
    nblk = pl.BlockSpec((SSD_CHUNK, SSD_STATE), lambda g, c: (ch(c), g))
    return pl.pallas_call(
        body, name="ssd_bwd", grid=(SSD_GROUPS, nc),
        in_specs=[x, bm, cm, dtcs, dtrs, prs, pcs, pcs, st, yblk],
        out_specs=[yblk, nblk, nblk, dtcs, prs],
        out_shape=[jax.ShapeDtypeStruct((t, SSD_INNER), F32), jax.ShapeDtypeStruct((t, BC_WIDTH), F32),
                   jax.ShapeDtypeStruct((t, BC_WIDTH), F32), jax.ShapeDtypeStruct((SSD_GROUPS, t, 4), F32),
                   jax.ShapeDtypeStruct((SSD_GROUPS, 3, 4), F32)],
        scratch_shapes=[pltpu.VMEM((SSD_STATE, 256), F32)],
        compiler_params=_cp(("parallel", "arbitrary")))(xbc, xbc, xbc, dtc, dtr, prow, bcol, acol, states, dy)


GROUP_W = SSD_INNER // SSD_GROUPS


def _mix_specs(tb):
    row = pl.BlockSpec((tb, 2048), lambda i: (i, 0))
    zlo = pl.BlockSpec((tb, 1024), lambda i: (i, O_Z // 1024))
    zhi = pl.BlockSpec((tb, 1024), lambda i: (i, O_Z // 1024 + 1))
    vec = pl.BlockSpec((1, 2048), lambda i: (0, 0))
    return row, zlo, zhi, vec


def _mix_fwd(attn, y, proj, g_attn, g_ssd):
    t = attn.shape[0]
    tb = _rows(t, 256)

    def body(a_ref, y_ref, zlo_ref, zhi_ref, ga_ref, gs_ref, o_ref):
        av = a_ref[...]
        r = lax.rsqrt(jnp.mean(av * av, axis=-1, keepdims=True) + EPS)
        o_ref[:, :ATTN_WIDTH] = (av * r * ga_ref[...]).astype(BF16)
        for g in range(SSD_GROUPS):
            lo, hi = g * GROUP_W, (g + 1) * GROUP_W
            zref = zlo_ref if g < 4 else zhi_ref
            z = zref[:, lo % 1024:lo % 1024 + GROUP_W]
            yg = y_ref[:, lo:hi] * (z * _sigmoid(z))
            rg = lax.rsqrt(jnp.mean(yg * yg, axis=-1, keepdims=True) + EPS)
            o_ref[:, ATTN_WIDTH + lo:ATTN_WIDTH + hi] = (yg * rg * gs_ref[:, lo:hi]).astype(BF16)

    row, zlo, zhi, vec = _mix_specs(tb)
    return pl.pallas_call(
        body, name="mix_fwd", grid=(t // tb,), in_specs=[row, row, zlo, zhi, vec, vec],
        out_specs=pl.BlockSpec((tb, 4096), lambda i: (i, 0)), out_shape=jax.ShapeDtypeStruct((t, 4096), BF16),
        compiler_params=_cp(("parallel",)))(attn, y, proj, proj, g_attn, g_ssd)


def _mix_bwd(dmix, attn, y, proj, g_attn, g_ssd):
    t = attn.shape[0]
    tb = _rows(t, 256)

    def body(dm_ref, a_ref, y_ref, zlo_ref, zhi_ref, ga_ref, gs_ref, da_ref, dy_ref, dz_ref, dga_ref, dgs_ref):
        i = pl.program_id(0)
        av = a_ref[...]
        dn = dm_ref[:, :ATTN_WIDTH].astype(F32)
        r = lax.rsqrt(jnp.mean(av * av, axis=-1, keepdims=True) + EPS)
        u = dn * ga_ref[...]
        da_ref[...] = r * u - av * (r * r * r * jnp.mean(u * av, axis=-1, keepdims=True))
        dga = jnp.sum(dn * av * r, axis=0, keepdims=True)

        @pl.when(i == 0)
        def _():
            dga_ref[...] = dga

        @pl.when(i > 0)
        def _():
            dga_ref[...] += dga

        for g in range(SSD_GROUPS):
            lo, hi = g * GROUP_W, (g + 1) * GROUP_W
            zref = zlo_ref if g < 4 else zhi_ref
            z = zref[:, lo % 1024:lo % 1024 + GROUP_W]
            yv = y_ref[:, lo:hi]
            sg = _sigmoid(z)
            sz = z * sg
            yg = yv * sz
            rg = lax.rsqrt(jnp.mean(yg * yg, axis=-1, keepdims=True) + EPS)
            do = dm_ref[:, ATTN_WIDTH + lo:ATTN_WIDTH + hi].astype(F32)
            ug = do * gs_ref[:, lo:hi]
            dyg = rg * ug - yg * (rg * rg * rg * jnp.mean(ug * yg, axis=-1, keepdims=True))
            dy_ref[:, lo:hi] = dyg * sz
            dz_ref[:, lo:hi] = (dyg * yv * (sg * (1.0 + z * (1.0 - sg)))).astype(BF16)
            dgs = jnp.sum(do * yg * rg, axis=0, keepdims=True)

            @pl.when(i == 0)
            def _():
                dgs_ref[:, lo:hi] = dgs

            @pl.when(i > 0)
            def _():
                dgs_ref[:, lo:hi] += dgs

    row, zlo, zhi, vec = _mix_specs(tb)
    return pl.pallas_call(
        body, name="mix_bwd", grid=(t // tb,),
        in_specs=[pl.BlockSpec((tb, 4096), lambda i: (i, 0)), row, row, zlo, zhi, vec, vec],
        out_specs=[row, row, row, vec, vec],
        out_shape=[jax.ShapeDtypeStruct((t, 2048), F32), jax.ShapeDtypeStruct((t, 2048), F32),
                   jax.ShapeDtypeStruct((t, 2048), BF16), jax.ShapeDtypeStruct((1, 2048), F32),
                   jax.ShapeDtypeStruct((1, 2048), F32)],
        compiler_params=_cp(("arbitrary",)))(dmix, attn, y, proj, proj, g_attn, g_ssd)


def _adamw(w, g, m, v, name):
    r, c = w.shape
    tb = _rows(r, 256)
    c1 = 1.0 - ADAM_B1 ** ADAM_STEP
    c2 = 1.0 - ADAM_B2 ** ADAM_STEP

    def body(w_ref, g_ref, m_ref, v_ref, d_ref, m2_ref, v2_ref):
        gv = g_ref[...]
        m2 = ADAM_B1 * m_ref[...] + (1.0 - ADAM_B1) * gv
        v2 = ADAM_B2 * v_ref[...] + (1.0 - ADAM_B2) * (gv * gv)
        d_ref[...] = -ADAM_LR * ((m2 / c1) / (jnp.sqrt(v2 / c2) + ADAM_EPS) + ADAM_WD * w_ref[...])
        m2_ref[...] = m2
        v2_ref[...] = v2

    blk = pl.BlockSpec((tb, c), lambda i: (i, 0))
    shp = jax.ShapeDtypeStruct((r, c), F32)
    return pl.pallas_call(body, name=name, grid=(r // tb,), in_specs=[blk] * 4, out_specs=[blk] * 3,
                          out_shape=[shp] * 3, compiler_params=_cp(("parallel",)))(w, g, m, v)


def _sum_own_half(g4, recv, pos, name):
    _, r, c = g4.shape
    h = r // 2
    tb = _rows(h, 128)
    nh = h // tb

    def body(pos_ref, a_ref, b_ref, o_ref):
        o_ref[...] = (a_ref[...] + b_ref[...]).astype(BF16)

    grid_spec = pltpu.PrefetchScalarGridSpec(
        num_scalar_prefetch=1, grid=(N_CHIPS, nh),
        in_specs=[pl.BlockSpec((1, tb, c), lambda j, i, pref: (j, pref[0] * nh + i, 0)),
                  pl.BlockSpec((1, tb, c), lambda j, i, pref: (j, i, 0))],
        out_specs=pl.BlockSpec((1, tb, c), lambda j, i, pref: (j, i, 0)))
    return pl.pallas_call(body, name=name, grid_spec=grid_spec,
                          out_shape=jax.ShapeDtypeStruct((N_CHIPS, h, c), BF16),
                          compiler_params=_cp(("parallel", "parallel")))(pos, g4, recv)


def _sum_chips(g4, recv, parts, pos, name):
    _, r, c = g4.shape
    h = r // 2
    tb = _rows(h, 128)
    nh = h // tb

    def body(pos_ref, a_ref, b_ref, p_ref, o_ref):
        own = a_ref[0] + b_ref[0]
        o_ref[...] = ((own + p_ref[0].astype(F32)) + p_ref[1].astype(F32)) + p_ref[2].astype(F32)

    grid_spec = pltpu.PrefetchScalarGridSpec(
        num_scalar_prefetch=1, grid=(nh,),
        in_specs=[pl.BlockSpec((1, tb, c), lambda i, pref: (pref[1], pref[0] * nh + i, 0)),
                  pl.BlockSpec((1, tb, c), lambda i, pref: (pref[1], i, 0)),
                  pl.BlockSpec((3, tb, c), lambda i, pref: (0, i, 0))],
        out_specs=pl.BlockSpec((tb, c), lambda i, pref: (i, 0)))
    return pl.pallas_call(body, name=name, grid_spec=grid_spec, out_shape=jax.ShapeDtypeStruct((h, c), F32),
                          compiler_params=_cp(("parallel",)))(pos, g4, recv, parts)


def _me():
    return lax.axis_index("x"), lax.axis_index("y"), lax.axis_index("c")


def _flip(v, bit):
    return (1 - v) if bit else v


CHIP_FLIPS = [(1, 0), (0, 1), (1, 1)]


def _allgather_weights(shards):
    n = len(shards)

    def body(*refs):
        ins, outs = refs[:n], refs[n:2 * n]
        send_sems, recv_sems = refs[2 * n:]
        x, y, c = _me()
        chip = 2 * x + y
        sib = (x, y, 1 - c)

        def remote(src, dst, k, to):
            return pltpu.make_async_remote_copy(src_ref=src, dst_ref=dst, send_sem=send_sems.at[k],
                                                recv_sem=recv_sems.at[k], device_id=to, device_id_type=MESH)

        sends = []
        for t in range(n):
            h = ins[t].shape[0] // 2
            mine = pl.ds(c * h, h)
            for k, (fx, fy) in enumerate(CHIP_FLIPS):
                cp = remote(ins[t].at[mine], outs[t].at[chip, mine], 6 * t + k, (_flip(x, fx), _flip(y, fy), c))
                cp.start()
                sends.append(cp)
        for t in range(n):
            h = ins[t].shape[0] // 2
            mine = pl.ds(c * h, h)
            for k, (fx, fy) in enumerate(CHIP_FLIPS):
                src_chip = 2 * _flip(x, fx) + _flip(y, fy)
                landed = outs[t].at[src_chip, mine]
                remote(landed, landed, 6 * t + k, (x, y, c)).wait_recv()
                fw = remote(landed, landed, 6 * t + 3 + k, sib)
                fw.start()
                sends.append(fw)
        for t in range(n):
            h = ins[t].shape[0] // 2
            other = pl.ds((1 - c) * h, h)
            for k, (fx, fy) in enumerate(CHIP_FLIPS):
                src_chip = 2 * _flip(x, fx) + _flip(y, fy)
                got = outs[t].at[src_chip, other]
                remote(got, got, 6 * t + 3 + k, (x, y, c)).wait_recv()
        for cp in sends:
            cp.wait_send()

    return pl.pallas_call(
        body, name="allgather_weights", in_specs=[HBM_SPEC] * n, out_specs=[HBM_SPEC] * n,
        out_shape=[jax.ShapeDtypeStruct((N_CHIPS,) + s.shape, s.dtype) for s in shards],
        scratch_shapes=[pltpu.SemaphoreType.DMA((6 * n,)), pltpu.SemaphoreType.DMA((6 * n,))],
        compiler_params=pltpu.CompilerParams(has_side_effects=True))(*shards)


def _exchange_halves(g4s):
    n = len(g4s)

    def body(*refs):
        ins, outs = refs[:n], refs[n:2 * n]
        send_sems, recv_sems = refs[2 * n:]
        x, y, c = _me()
        cps = []
        for t in range(n):
            h = ins[t].shape[1] // 2
            cp = pltpu.make_async_remote_copy(
                src_ref=ins[t].at[:, pl.ds((1 - c) * h, h)], dst_ref=outs[t], send_sem=send_sems.at[t],
                recv_sem=recv_sems.at[t], device_id=(x, y, 1 - c), device_id_type=MESH)
            cp.start()
            cps.append(cp)
        for cp in cps:
            cp.wait()

    return pl.pallas_call(
        body, name="grad_exchange_halves", in_specs=[HBM_SPEC] * n, out_specs=[HBM_SPEC] * n,
        out_shape=[jax.ShapeDtypeStruct((N_CHIPS, g.shape[1] // 2, g.shape[2]), g.dtype) for g in g4s],
        scratch_shapes=[pltpu.SemaphoreType.DMA((n,)), pltpu.SemaphoreType.DMA((n,))],
        compiler_params=pltpu.CompilerParams(has_side_effects=True))(*g4s)


def _scatter_to_chips(ps):
    n = len(ps)

    def body(*refs):
        ins, outs = refs[:n], refs[n:2 * n]
        send_sems, recv_sems = refs[2 * n:]
        x, y, c = _me()
        cps = []
        for t in range(n):
            for k, (fx, fy) in enumerate(CHIP_FLIPS):
                to_chip = 2 * _flip(x, fx) + _flip(y, fy)
                cp = pltpu.make_async_remote_copy(
                    src_ref=ins[t].at[to_chip], dst_ref=outs[t].at[k], send_sem=send_sems.at[3 * t + k],
                    recv_sem=recv_sems.at[3 * t + k], device_id=(_flip(x, fx), _flip(y, fy), c),
                    device_id_type=MESH)
                cp.start()
                cps.append(cp)
        for cp in cps:
            cp.wait()

    return pl.pallas_call(
        body, name="grad_scatter_to_chips", in_specs=[HBM_SPEC] * n, out_specs=[HBM_SPEC] * n,
        out_shape=[jax.ShapeDtypeStruct((3,) + p.shape[1:], p.dtype) for p in ps],
        scratch_shapes=[pltpu.SemaphoreType.DMA((3 * n,)), pltpu.SemaphoreType.DMA((3 * n,))],
        compiler_params=pltpu.CompilerParams(has_side_effects=True))(*ps)


def _share_halves(ghs):
    n = len(ghs)

    def body(*refs):
        ins, outs = refs[:n], refs[n:2 * n]
        send_sems, recv_sems = refs[2 * n:]
        x, y, c = _me()
        cps = []
        for t in range(n):
            cp = pltpu.make_async_remote_copy(
                src_ref=ins[t], dst_ref=outs[t], send_sem=send_sems.at[t], recv_sem=recv_sems.at[t],
                device_id=(x, y, 1 - c), device_id_type=MESH)
            cp.start()
            cps.append(cp)
        for cp in cps:
            cp.wait()

    return pl.pallas_call(
        body, name="grad_share_halves", in_specs=[HBM_SPEC] * n, out_specs=[HBM_SPEC] * n,
        out_shape=[jax.ShapeDtypeStruct(g.shape, g.dtype) for g in ghs],
        scratch_shapes=[pltpu.SemaphoreType.DMA((n,)), pltpu.SemaphoreType.DMA((n,))],
        compiler_params=pltpu.CompilerParams(has_side_effects=True))(*ghs)


def _allreduce_small(v):
    r = v.shape[0]

    def body(v_ref, o_ref, buf, send_sems, recv_sems):
        x, y, c = _me()
        me = 4 * x + 2 * y + c
        buf[0] = v_ref[...]
        cps = []
        for k in range(1, 8):
            kx, ky, kc = (k >> 2) & 1, (k >> 1) & 1, k & 1
            cp = pltpu.make_async_remote_copy(
                src_ref=v_ref, dst_ref=buf.at[k], send_sem=send_sems.at[k - 1], recv_sem=recv_sems.at[k - 1],
                device_id=(_flip(x, kx), _flip(y, ky), _flip(c, kc)), device_id_type=MESH)
            cp.start()
            cps.append(cp)
        for cp in cps:
            cp.wait()
        acc = buf[me]
        for d in range(1, 8):
            acc = acc + buf[jnp.bitwise_xor(me, d)]
        o_ref[...] = acc

    vm = pl.BlockSpec(memory_space=pltpu.VMEM)
    return pl.pallas_call(
        body, name="allreduce_small", in_specs=[vm], out_specs=vm, out_shape=jax.ShapeDtypeStruct(v.shape, F32),
        scratch_shapes=[pltpu.VMEM((8, r, LANES), F32), pltpu.SemaphoreType.DMA((7,)),
                        pltpu.SemaphoreType.DMA((7,))],
        compiler_params=pltpu.CompilerParams(has_side_effects=True, vmem_limit_bytes=VMEM_LIMIT))(v)


def _reduce_scatter_big(g4s, pos):
    recvs = _exchange_halves(g4s)
    ps = [_sum_own_half(g, r, pos, name="grad_sum_pair_%d" % t) for t, (g, r) in enumerate(zip(g4s, recvs))]
    parts = _scatter_to_chips(ps)
    ghs = [_sum_chips(g, r, p, pos, name="grad_sum_chips_%d" % t)
           for t, (g, r, p) in enumerate(zip(g4s, recvs, parts))]
    theirs = _share_halves(ghs)
    core = pos[0]
    return [jnp.where(core == 0, jnp.concatenate([a, b], axis=0), jnp.concatenate([b, a], axis=0))
            for a, b in zip(ghs, theirs)]


def _local_step(x, tgt, p, w_in_main, w_in_dt, w_out, w_up, w_down):
    t = x.shape[0]
    tables = _rope_tables(t)
    sinks = p['sinks'].reshape(N_Q_HEADS)

    xn = _rmsnorm_fwd(x, p['norm_mix'], "norm_mix_fwd")
    proj = _matmul(xn, w_in_main, mode='nn', name="in_proj")
    dt_raw = _matmul(xn, w_in_dt, mode='nn', name="in_proj_dt")[:, :SSD_HEADS]
    attn = _attn_fwd(proj, sinks, tables)
    conv_b = p['ssd_conv_b']
    xbc = _conv_fwd(proj, p['ssd_conv_w'], conv_b, col0=O_XBC, width=CONV_CH, act=True, name="ssd_conv_fwd")
    sp = _ssd_params(dt_raw, p['dt_bias'].reshape(-1), p['a_log'].reshape(-1), p['ssd_d'].reshape(-1))
    y, states = _ssd_fwd(xbc, sp)
    mix = _mix_fwd(attn, y, proj, p['attn_out_norm'], p['ssd_norm'])
    h1 = _matmul(mix, w_out, mode='nn', name="out_proj", add=x)
    hn = _rmsnorm_fwd(h1, p['norm_ffn'], "norm_ffn_fwd")
    u0 = _matmul(hn, w_up, mode='nn', name="ffn_up")
    u = _conv_fwd(u0, p['ffn_conv_w'], p['ffn_conv_b'], col0=0, width=2 * D_FF, act=False, name="ffn_conv_fwd")
    a = _swiglu_fwd(u)
    h2 = _matmul(a, w_down, mode='nn', name="ffn_down", add=h1)
    loss, dh2, g_norm_final = _final_loss(h2, p['norm_final'].reshape(1, D_MODEL), tgt)

    g = {}
    da = _matmul(dh2, w_down, mode='nt', name="ffn_down_dx", out_dtype=BF16, tn=1408)
    g['w_down'] = _matmul(a, dh2, mode='tn', name="ffn_down_dw", tm=1408)
    dug, duv = _swiglu_bwd(u, da)
    du = jnp.concatenate([dug, duv], axis=1)
    du0, g['ffn_conv_w'], g['ffn_conv_b'] = _conv_bwd(u0, p['ffn_conv_w'], du, col0=0, width=2 * D_FF,
                                                      name="ffn_conv_bwd")
    dhn = _matmul(du0, w_up, mode='nt', name="ffn_up_dx", out_dtype=BF16)
    g['w_up'] = _matmul(hn, du0, mode='tn', name="ffn_up_dw")
    dh1, g['norm_ffn'] = _rmsnorm_bwd(h1, p['norm_ffn'], dhn, dh2, "norm_ffn_bwd")

    dmix = _matmul(dh1, w_out, mode='nt', name="out_proj_dx", out_dtype=BF16)
    g['w_out'] = _matmul(mix, dh1, mode='tn', name="out_proj_dw")
    dattn, dy, dz, g['attn_out_norm'], g['ssd_norm'] = _mix_bwd(dmix, attn, y, proj, p['attn_out_norm'],
                                                                p['ssd_norm'])
    dq, dk, dv, dsink = _attn_bwd(proj, sinks, tables, dattn)
    g['sinks'] = dsink[:, :, 0].reshape(1, N_Q_HEADS)
    dxs, dbm, dcm, ddt8, dpar = _ssd_bwd(xbc, sp, states, dy)
    g['dt_bias'] = dpar[:, 0, :].reshape(1, SSD_HEADS)
    g['a_log'] = dpar[:, 1, :].reshape(1, SSD_HEADS)
    g['ssd_d'] = dpar[:, 2, :].reshape(1, SSD_HEADS)
    dxbc_act = jnp.concatenate([dxs, dbm, dcm], axis=1)
    dconv = _conv_silu_dact(proj, p['ssd_conv_w'], conv_b, dxbc_act, col0=O_XBC, width=CONV_CH,
                            name="ssd_conv_dact")
    dxbc, g['ssd_conv_w'], g['ssd_conv_b'] = _conv_bwd(proj, p['ssd_conv_w'], dconv, col0=O_XBC, width=CONV_CH,
                                                       name="ssd_conv_bwd")
    dproj = jnp.concatenate([dq, dk, dv, dz, dxbc], axis=1)
    ddt = ddt8.transpose(1, 0, 2).reshape(t, SSD_HEADS)
    ddt_pad = jnp.pad(ddt, ((0, 0), (0, LANES - SSD_HEADS))).astype(BF16)
    dxn_dt = _matmul(ddt_pad, w_in_dt, mode='nt', name="in_proj_dt_dx")
    dxn = _matmul(dproj, w_in_main, mode='nt', name="in_proj_dx", out_dtype=BF16, add=dxn_dt)
    g['w_in'] = (_matmul(xn, dproj, mode='tn', name="in_proj_dw"),
                 _matmul(xn, ddt_pad, mode='tn', name="in_proj_dt_dw"))
    dx, g['norm_mix'] = _rmsnorm_bwd(x, p['norm_mix'], dxn, dh1, "norm_mix_bwd")
    g['norm_final'] = g_norm_final
    return loss, dx, g


def _pack(arrs):
    flat = jnp.concatenate([a.reshape(-1) for a in arrs])
    n = flat.shape[0]
    rows = -(-n // LANES)
    rows = -(-rows // 8) * 8
    return jnp.pad(flat, (0, rows * LANES - n)).reshape(rows, LANES)


def _unpack(packed, shapes):
    flat = packed.reshape(-1)
    out, off = [], 0
    for s in shapes:
        n = 1
        for d in s:
            n *= d
        out.append(flat[off:off + n].reshape(s))
        off += n
    return out


def _gather_full_weights(w_in, w_out, w_up, w_down, chip):
    shards = [w_in[0].astype(BF16), w_out[0].astype(BF16), w_up[0].astype(BF16), w_down[0].astype(BF16)]
    gathered = _allgather_weights(shards)

    def cols(g, own):
        return jnp.concatenate([jnp.where(chip == j, own, g[j]) for j in range(N_CHIPS)], axis=1)

    def rows(g, own):
        return lax.dynamic_update_slice(g, own[None], (chip, 0, 0)).reshape(-1, D_MODEL)

    full_in = cols(gathered[0], shards[0])
    w_in_main = full_in[:, :MAIN_WIDTH]
    w_in_dt = jnp.pad(full_in[:, MAIN_WIDTH:], ((0, 0), (0, LANES - SSD_HEADS)))
    full_up = cols(gathered[2], shards[2])
    full_out = rows(gathered[1], shards[1])
    full_down = rows(gathered[3], shards[3])
    return w_in_main, w_in_dt, full_out, full_up, full_down


def _owner_major(gfull, axis):
    if axis == 0:
        return gfull.reshape(N_CHIPS, gfull.shape[0] // N_CHIPS, gfull.shape[1])
    cw = gfull.shape[1] // N_CHIPS
    return jnp.stack([gfull[:, j * cw:(j + 1) * cw] for j in range(N_CHIPS)], axis=0)


def kernel(x, norm_mix, w_in, sinks, attn_out_norm, ssd_conv_w, ssd_conv_b, dt_bias, a_log, ssd_d, ssd_norm, w_out, norm_ffn, w_up, ffn_conv_w, ffn_conv_b, w_down, norm_final, loss_target, m_norm_mix, m_w_in, m_sinks, m_attn_out_norm, m_ssd_conv_w, m_ssd_conv_b, m_dt_bias, m_a_log, m_ssd_d, m_ssd_norm, m_w_out, m_norm_ffn, m_w_up, m_ffn_conv_w, m_ffn_conv_b, m_w_down, m_norm_final, v_norm_mix, v_w_in, v_sinks, v_attn_out_norm, v_ssd_conv_w, v_ssd_conv_b, v_dt_bias, v_a_log, v_ssd_d, v_ssd_norm, v_w_out, v_norm_ffn, v_w_up, v_ffn_conv_w, v_ffn_conv_b, v_w_down, v_norm_final):
    args = dict(locals())
    w = {n: args[n] for n in WEIGHTS}
    m = {n: args['m_' + n] for n in WEIGHTS}
    v = {n: args['v_' + n] for n in WEIGHTS}
    xi, yi, ci = _me()
    chip = 2 * xi + yi
    pos = jnp.stack([ci, chip]).astype(jnp.int32)

    def place(shard, full_cols):
        z = jnp.zeros((shard.shape[0], full_cols), F32)
        return lax.dynamic_update_slice(z, shard * 0.5, (0, chip * shard.shape[1]))

    conv_pack = _pack([place(ssd_conv_w[0], CONV_CH), place(ffn_conv_w[0], 2 * D_FF)])
    conv_full = _allreduce_small(conv_pack)
    ssd_conv_w_full, ffn_conv_w_full = _unpack(conv_full, [(SSD_CONV, CONV_CH), (FFN_CONV, 2 * D_FF)])

    w_in_main, w_in_dt, full_out, full_up, full_down = _gather_full_weights(w_in, w_out, w_up, w_down, chip)
    small = {
        'norm_mix': norm_mix, 'sinks': sinks, 'attn_out_norm': attn_out_norm, 'ssd_conv_w': ssd_conv_w_full,
        'ssd_conv_b': ssd_conv_b, 'dt_bias': dt_bias, 'a_log': a_log, 'ssd_d': ssd_d, 'ssd_norm': ssd_norm,
        'norm_ffn': norm_ffn, 'ffn_conv_w': ffn_conv_w_full, 'ffn_conv_b': ffn_conv_b, 'norm_final': norm_final,
    }
    loss, dx, g = _local_step(x[0], loss_target[0], small, w_in_main, w_in_dt, full_out, full_up, full_down)

    g_in_main, g_in_dt = g['w_in']
    g_in_full = jnp.concatenate([g_in_main, g_in_dt[:, :SSD_HEADS]], axis=1)
    g4s = [_owner_major(g_in_full, 1), _owner_major(g['w_out'], 0), _owner_major(g['w_up'], 1),
           _owner_major(g['w_down'], 0)]
    gbig = dict(zip(BIG, _reduce_scatter_big(g4s, pos)))

    small_names = [n for n in WEIGHTS if n not in BIG]
    small_g = [loss[:, :1]] + [g[n] for n in small_names]
    small_shapes = [(1, 1)] + [tuple(a.shape) for a in small_g[1:]]
    red = _unpack(_allreduce_small(_pack(small_g)), small_shapes)
    loss_out = red[0].reshape(())
    gsm = dict(zip(small_names, red[1:]))
    gsm['ssd_conv_w'] = lax.dynamic_slice(gsm['ssd_conv_w'], (0, chip * ssd_conv_w.shape[2]),
                                          (SSD_CONV, ssd_conv_w.shape[2]))
    gsm['ffn_conv_w'] = lax.dynamic_slice(gsm['ffn_conv_w'], (0, chip * ffn_conv_w.shape[2]),
                                          (FFN_CONV, ffn_conv_w.shape[2]))

    grads, deltas, new_m, new_v = {}, {}, {}, {}
    for n in BIG:
        grads[n] = gbig[n][None]
        d, m2, v2 = _adamw(w[n][0], gbig[n], m[n][0], v[n][0], name="adamw_" + n)
        deltas[n], new_m[n], new_v[n] = d[None], m2[None], v2[None]
    shapes = [tuple(w[n].shape) for n in small_names]
    gp = _pack([gsm[n] for n in small_names])
    d, m2, v2 = _adamw(_pack([w[n] for n in small_names]), gp, _pack([m[n] for n in small_names]),
                       _pack([v[n] for n in small_names]), name="adamw_small")
    for n, gg, dd, mm, vv in zip(small_names, _unpack(gp, shapes), _unpack(d, shapes), _unpack(m2, shapes),
                                 _unpack(v2, shapes)):
        grads[n], deltas[n], new_m[n], new_v[n] = gg, dd, mm, vv

    return (loss_out, dx[None], *[grads[n] for n in WEIGHTS], *[deltas[n] for n in WEIGHTS],
            *[new_m[n] for n in WEIGHTS], *[new_v[n] for n in WEIGHTS])
```

```python
import functools

import jax
import jax.numpy as jnp
from jax import lax
from jax.experimental import pallas as pl
from jax.experimental.pallas import tpu as pltpu

F32 = jnp.float32
BF16 = jnp.bfloat16

D_MODEL = 2048
N_Q_HEADS = 32
N_KV_HEADS = 8
HEAD_DIM = 64
WINDOW = 128
ATTN_BLOCK = 128
ROT_DIM = 16
ROPE_THETA = 500000.0
SSD_HEADS = 32
SSD_HEAD_DIM = 64
SSD_INNER = 2048
SSD_GROUPS = 8
SSD_STATE = 128
SSD_CONV = 4
SSD_CHUNK = 128
ATTN_WIDTH = 2048
KV_WIDTH = 512
BC_WIDTH = 1024
CONV_CH = 4096
IN_PROJ_WIDTH = 9248
MAIN_WIDTH = 9216
D_FF = 5632
FFN_CONV = 3
EPS = 1e-6
O_Q, O_K, O_V, O_Z, O_XBC, O_DT = 0, 2048, 2560, 3072, 5120, 9216

ADAM_LR = 0.001
ADAM_B1 = 0.9
ADAM_B2 = 0.999
ADAM_EPS = 1e-08
ADAM_WD = 0.01
ADAM_STEP = 10

N_CHIPS = 4
NEG = -1e30
LANES = 128
VMEM_LIMIT = 48 * 1024 * 1024
MESH = pl.DeviceIdType.MESH
HBM_SPEC = pl.BlockSpec(memory_space=pltpu.HBM)

WEIGHTS = ['norm_mix', 'w_in', 'sinks', 'attn_out_norm', 'ssd_conv_w', 'ssd_conv_b', 'dt_bias', 'a_log', 'ssd_d',
           'ssd_norm', 'w_out', 'norm_ffn', 'w_up', 'ffn_conv_w', 'ffn_conv_b', 'w_down', 'norm_final']
BIG = ['w_in', 'w_out', 'w_up', 'w_down']


def _cp(sem=None, vmem=VMEM_LIMIT):
    kw = {'vmem_limit_bytes': vmem}
    if sem is not None:
        kw['dimension_semantics'] = sem
    return pltpu.CompilerParams(**kw)


def _tile(n, pref):
    if n <= pref:
        return n
    t = (pref // LANES) * LANES
    while t > LANES and n % t:
        t -= LANES
    assert n % t == 0, (n, pref)
    return t


def _rows(n, pref):
    t = min(n, pref)
    while n % t:
        t -= 8
    return t


def _iota(shape, dim):
    return lax.broadcasted_iota(jnp.int32, shape, dim)


def _dot(a, b, mode='nn'):
    dn = {'nn': (((1,), (0,)), ((), ())), 'nt': (((1,), (1,)), ((), ())), 'tn': (((0,), (0,)), ((), ()))}[mode]
    return lax.dot_general(a.astype(BF16), b.astype(BF16), dn, preferred_element_type=F32)


def _dot_exact(a, b):
    return lax.dot_general(a, b, (((1,), (0,)), ((), ())), precision=lax.Precision.HIGHEST,
                           preferred_element_type=F32)


def _sigmoid(x):
    return 1.0 / (1.0 + jnp.exp(-x))


def _softplus(x):
    return jnp.maximum(x, 0.0) + jnp.log(1.0 + jnp.exp(-jnp.abs(x)))


def _matmul(a, b, *, mode, name, out_dtype=F32, add=None, deps=(), tm=1024, tn=1024, tk=512):
    if mode == 'nn':
        (m, k), (k2, n) = a.shape, b.shape
    elif mode == 'nt':
        (m, k), (n, k2) = a.shape, b.shape
    else:
        (k, m), (k2, n) = a.shape, b.shape
    assert k == k2, (a.shape, b.shape, mode)
    tm, tn, tk = _tile(m, tm), _tile(n, tn), _tile(k, tk)
    nk = k // tk
    has_add = add is not None

    def body(*refs):
        a_ref, b_ref = refs[:2]
        add_ref = refs[2] if has_add else None
        o_ref, acc = refs[-2:]
        kk = pl.program_id(2)

        @pl.when(kk == 0)
        def _():
            acc[...] = jnp.zeros_like(acc)

        acc[...] += _dot(a_ref[...], b_ref[...], mode)

        @pl.when(kk == nk - 1)
        def _():
            r = acc[...]
            if has_add:
                r = r + add_ref[...].astype(F32)
            o_ref[...] = r.astype(out_dtype)

    if mode == 'tn':
        a_spec = pl.BlockSpec((tk, tm), lambda i, j, kk: (kk, i))
    else:
        a_spec = pl.BlockSpec((tm, tk), lambda i, j, kk: (i, kk))
    if mode == 'nt':
        b_spec = pl.BlockSpec((tn, tk), lambda i, j, kk: (j, kk))
    else:
        b_spec = pl.BlockSpec((tk, tn), lambda i, j, kk: (kk, j))
    o_spec = pl.BlockSpec((tm, tn), lambda i, j, kk: (i, j))
    dep_spec = pl.BlockSpec((8, LANES), lambda i, j, kk: (0, 0))
    in_specs = [a_spec, b_spec] + ([o_spec] if has_add else []) + [dep_spec] * len(deps)
    args = (a, b) + ((add,) if has_add else ()) + tuple(deps)
    return pl.pallas_call(
        body, name=name, grid=(m // tm, n // tn, nk), in_specs=in_specs, out_specs=o_spec,
        out_shape=jax.ShapeDtypeStruct((m, n), out_dtype), scratch_shapes=[pltpu.VMEM((tm, tn), F32)],
        compiler_params=_cp(("parallel", "parallel", "arbitrary")))(*args)


def _rmsnorm_fwd(x, g, name):
    t, d = x.shape
    tb = _rows(t, 256)

    def body(x_ref, g_ref, o_ref):
        xv = x_ref[...]
        r = lax.rsqrt(jnp.mean(xv * xv, axis=-1, keepdims=True) + EPS)
        o_ref[...] = (xv * r * g_ref[...]).astype(BF16)

    return pl.pallas_call(
        body, name=name, grid=(t // tb,),
        in_specs=[pl.BlockSpec((tb, d), lambda i: (i, 0)), pl.BlockSpec((1, d), lambda i: (0, 0))],
        out_specs=pl.BlockSpec((tb, d), lambda i: (i, 0)), out_shape=jax.ShapeDtypeStruct((t, d), BF16),
        compiler_params=_cp(("parallel",)))(x, g)


def _rmsnorm_bwd(x, g, dy, res, name):
    t, d = x.shape
    tb = _rows(t, 256)

    def body(x_ref, g_ref, dy_ref, res_ref, dx_ref, dg_ref):
        i = pl.program_id(0)
        xv = x_ref[...]
        dyv = dy_ref[...].astype(F32)
        r = lax.rsqrt(jnp.mean(xv * xv, axis=-1, keepdims=True) + EPS)
        u = dyv * g_ref[...]
        dx = r * u - xv * (r * r * r * jnp.mean(u * xv, axis=-1, keepdims=True))
        dx_ref[...] = dx + res_ref[...]
        part = jnp.sum(dyv * xv * r, axis=0, keepdims=True)

        @pl.when(i == 0)
        def _():
            dg_ref[...] = part

        @pl.when(i > 0)
        def _():
            dg_ref[...] += part

    row = pl.BlockSpec((tb, d), lambda i: (i, 0))
    vec = pl.BlockSpec((1, d), lambda i: (0, 0))
    return pl.pallas_call(
        body, name=name, grid=(t // tb,), in_specs=[row, vec, row, row], out_specs=[row, vec],
        out_shape=[jax.ShapeDtypeStruct((t, d), F32), jax.ShapeDtypeStruct((1, d), F32)],
        compiler_params=_cp(("arbitrary",)))(x, g, dy, res)


def _final_loss(h, g, tgt):
    t, d = h.shape
    tb = _rows(t, 256)

    def body(h_ref, g_ref, t_ref, loss_ref, dh_ref, dg_ref):
        i = pl.program_id(0)
        hv = h_ref[...]
        gv = g_ref[...]
        r = lax.rsqrt(jnp.mean(hv * hv, axis=-1, keepdims=True) + EPS)
        y = hv * r * gv
        diff = y - t_ref[...]
        lpart = jnp.sum(jnp.sum(diff * diff, axis=1, keepdims=True), axis=0, keepdims=True) * (0.5 / d)
        dy = diff * (1.0 / d)
        u = dy * gv
        dh_ref[...] = r * u - hv * (r * r * r * jnp.mean(u * hv, axis=-1, keepdims=True))
        gpart = jnp.sum(dy * hv * r, axis=0, keepdims=True)
        lrow = jnp.broadcast_to(lpart, (1, LANES))

        @pl.when(i == 0)
        def _():
            loss_ref[...] = lrow
            dg_ref[...] = gpart

        @pl.when(i > 0)
        def _():
            loss_ref[...] += lrow
            dg_ref[...] += gpart

    row = pl.BlockSpec((tb, d), lambda i: (i, 0))
    vec = pl.BlockSpec((1, d), lambda i: (0, 0))
    return pl.pallas_call(
        body, name="final_loss", grid=(t // tb,), in_specs=[row, vec, row],
        out_specs=[pl.BlockSpec((1, LANES), lambda i: (0, 0)), row, vec],
        out_shape=[jax.ShapeDtypeStruct((1, LANES), F32), jax.ShapeDtypeStruct((t, d), F32),
                   jax.ShapeDtypeStruct((1, d), F32)],
        compiler_params=_cp(("arbitrary",)))(h, g, tgt)


def _rope_tables(t):
    pos = jnp.arange(t, dtype=F32)
    inv = 1.0 / (ROPE_THETA ** (jnp.arange(0, ROT_DIM, 2, dtype=F32) / ROT_DIM))
    ang = pos[:, None] * inv[None, :]
    cos, sin = jnp.cos(ang), jnp.sin(ang)
    half = ROT_DIM // 2
    rest = HEAD_DIM - ROT_DIM
    c = jnp.concatenate([cos, cos, jnp.ones((t, rest), F32)], axis=1)
    s1 = jnp.concatenate([-sin, jnp.zeros((t, half + rest), F32)], axis=1)
    s2 = jnp.concatenate([jnp.zeros((t, half), F32), sin, jnp.zeros((t, rest), F32)], axis=1)
    return tuple(jnp.tile(v, (1, LANES // HEAD_DIM)) for v in (c, s1, s2))


def _rope(x, c, s1, s2):
    half = ROT_DIM // 2
    return x * c + pltpu.roll(x, LANES - half, 1) * s1 + pltpu.roll(x, half, 1) * s2


def _rope_t(g, c, s1, s2):
    half = ROT_DIM // 2
    return g * c + pltpu.roll(g * s1, half, 1) + pltpu.roll(g * s2, LANES - half, 1)


def _attn_mask(i):
    qi = _iota((ATTN_BLOCK, 2 * ATTN_BLOCK), 0)
    kj = _iota((ATTN_BLOCK, 2 * ATTN_BLOCK), 1)
    rel = qi + ATTN_BLOCK - kj
    first_key = jnp.where(i > 0, 0, ATTN_BLOCK)
    return (rel >= 0) & (rel < WINDOW) & (kj >= first_key)


def _half_masks():
    lane = _iota((1, LANES), 1)
    return [(lane < HEAD_DIM).astype(F32), (lane >= HEAD_DIM).astype(F32)]


def _attn_specs(nb_clamp):
    blk = ATTN_BLOCK
    kb, vb = O_K // LANES, O_V // LANES

    def cur(i):
        return jnp.minimum(i, nb_clamp)

    def prev(i):
        return jnp.maximum(jnp.minimum(i, nb_clamp + 1) - 1, 0)

    q = pl.BlockSpec((blk, 512), lambda p, i: (cur(i), p))
    kc = pl.BlockSpec((blk, LANES), lambda p, i: (cur(i), kb + p))
    kp = pl.BlockSpec((blk, LANES), lambda p, i: (prev(i), kb + p))
    vc = pl.BlockSpec((blk, LANES), lambda p, i: (cur(i), vb + p))
    vp = pl.BlockSpec((blk, LANES), lambda p, i: (prev(i), vb + p))
    tc = pl.BlockSpec((blk, LANES), lambda p, i: (cur(i), 0))
    tp = pl.BlockSpec((blk, LANES), lambda p, i: (prev(i), 0))
    return q, kc, kp, vc, vp, tc, tp


def _attn_fwd(proj, sinks, tables):
    t = proj.shape[0]
    nb = t // ATTN_BLOCK
    scale = HEAD_DIM ** -0.5

    def body(sink_ref, q_ref, kc_ref, kp_ref, vc_ref, vp_ref, cc_ref, s1c_ref, s2c_ref, cp_ref, s1p_ref, s2p_ref,
             o_ref):
        p = pl.program_id(0)
        i = pl.program_id(1)
        cc, s1c, s2c = cc_ref[...], s1c_ref[...], s2c_ref[...]
        kband = jnp.concatenate([_rope(kp_ref[...], cp_ref[...], s1p_ref[...], s2p_ref[...]),
                                 _rope(kc_ref[...], cc, s1c, s2c)], axis=0).astype(BF16)
        vband = jnp.concatenate([vp_ref[...], vc_ref[...]], axis=0)
        hm = _half_masks()
        vsel = [(vband * hm[j]).astype(BF16) for j in range(2)]
        valid = _attn_mask(i)
        for qb in range(4):
            qr = _rope(q_ref[:, qb * LANES:(qb + 1) * LANES], cc, s1c, s2c)
            acc = jnp.zeros((ATTN_BLOCK, LANES), F32)
            for half in range(2):
                hh = qb * 2 + half
                j = hh // 4
                qs = qr * hm[half]
                if half != j:
                    qs = pltpu.roll(qs, HEAD_DIM, 1)
                s = jnp.where(valid, _dot(qs, kband, 'nt') * scale, NEG)
                sink = sink_ref[p * 8 + hh]
                m = jnp.maximum(jnp.max(s, axis=1, keepdims=True), sink)
                pe = jnp.exp(s - m)
                den = jnp.sum(pe, axis=1, keepdims=True) + jnp.exp(sink - m)
                o = _dot(pe / den, vsel[j])
                if half != j:
                    o = pltpu.roll(o, HEAD_DIM, 1)
                acc = acc + o
            o_ref[:, qb * LANES:(qb + 1) * LANES] = acc

    q, kc, kp, vc, vp, tc, tp = _attn_specs(nb - 1)
    smem = pl.BlockSpec(memory_space=pltpu.SMEM)
    return pl.pallas_call(
        body, name="attn_fwd", grid=(4, nb),
        in_specs=[smem, q, kc, kp, vc, vp, tc, tc, tc, tp, tp, tp],
        out_specs=pl.BlockSpec((ATTN_BLOCK, 512), lambda p, i: (i, p)),
        out_shape=jax.ShapeDtypeStruct((t, ATTN_WIDTH), F32),
        compiler_params=_cp(("parallel", "arbitrary")))(sinks, proj, proj, proj, proj, proj, *tables, *tables)


def _attn_bwd(proj, sinks, tables, dout):
    t = proj.shape[0]
    nb = t // ATTN_BLOCK
    scale = HEAD_DIM ** -0.5

    def body(sink_ref, q_ref, kc_ref, kp_ref, vc_ref, vp_ref, cc_ref, s1c_ref, s2c_ref, cp_ref, s1p_ref, s2p_ref,
             do_ref, dq_ref, dk_ref, dv_ref, ds_ref, carry_k, carry_v):
        p = pl.program_id(0)
        i = pl.program_id(1)
        ptab = (cp_ref[...], s1p_ref[...], s2p_ref[...])

        @pl.when(i == 0)
        def _():
            carry_k[...] = jnp.zeros_like(carry_k)
            carry_v[...] = jnp.zeros_like(carry_v)
            ds_ref[...] = jnp.zeros_like(ds_ref)

        @pl.when(i < nb)
        def _():
            cc, s1c, s2c = cc_ref[...], s1c_ref[...], s2c_ref[...]
            kband = jnp.concatenate([_rope(kp_ref[...], *ptab), _rope(kc_ref[...], cc, s1c, s2c)], axis=0)
            vband = jnp.concatenate([vp_ref[...], vc_ref[...]], axis=0)
            hm = _half_masks()
            kband16 = kband.astype(BF16)
            ksel = [(kband * hm[j]).astype(BF16) for j in range(2)]
            vband16 = vband.astype(BF16)
            vsel = [(vband * hm[j]).astype(BF16) for j in range(2)]
            valid = _attn_mask(i)
            dkb = jnp.zeros((2 * ATTN_BLOCK, LANES), F32)
            dvb = jnp.zeros((2 * ATTN_BLOCK, LANES), F32)
            row8 = _iota((8, LANES), 0)
            dsink = jnp.zeros((8, LANES), F32)
            for qb in range(4):
                qr = _rope(q_ref[:, qb * LANES:(qb + 1) * LANES], cc, s1c, s2c)
                dob = do_ref[:, qb * LANES:(qb + 1) * LANES]
                dqb = jnp.zeros((ATTN_BLOCK, LANES), F32)
                for half in range(2):
                    hh = qb * 2 + half
                    j = hh // 4
                    qs = qr * hm[half]
                    dos = dob * hm[half]
                    if half != j:
                        qs = pltpu.roll(qs, HEAD_DIM, 1)
                        dos = pltpu.roll(dos, HEAD_DIM, 1)
                    qs16 = qs.astype(BF16)
                    dos16 = dos.astype(BF16)
                    s = jnp.where(valid, _dot(qs16, kband16, 'nt') * scale, NEG)
                    sink = sink_ref[p * 8 + hh]
                    m = jnp.maximum(jnp.max(s, axis=1, keepdims=True), sink)
                    pe = jnp.exp(s - m)
                    psink = jnp.exp(sink - m)
                    den = jnp.sum(pe, axis=1, keepdims=True) + psink
                    pr = pe / den
                    dvb = dvb + _dot(pr.T, dos16)
                    dp = _dot(dos16, vband16, 'nt')
                    delta = jnp.sum(pr * dp, axis=1, keepdims=True)
                    dsc = pr * (dp - delta) * scale
                    dsink = dsink + jnp.where(row8 == hh, -jnp.sum(psink / den * delta), 0.0)
                    dqh = _dot(dsc, ksel[j])
                    if half != j:
                        dqh = pltpu.roll(dqh, HEAD_DIM, 1)
                    dqb = dqb + dqh
                    dkb = dkb + _dot(dsc.T, qs16)
                dq_ref[:, qb * LANES:(qb + 1) * LANES] = _rope_t(dqb, cc, s1c, s2c).astype(BF16)
            ds_ref[0] += dsink
            dk_ref[...] = _rope_t(carry_k[...] + dkb[:ATTN_BLOCK], *ptab).astype(BF16)
            dv_ref[...] = (carry_v[...] + dvb[:ATTN_BLOCK]).astype(BF16)
            carry_k[...] = dkb[ATTN_BLOCK:]
            carry_v[...] = dvb[ATTN_BLOCK:]

        @pl.when(i == nb)
        def _():
            dk_ref[...] = _rope_t(carry_k[...], *ptab).astype(BF16)
            dv_ref[...] = carry_v[...].astype(BF16)

    q, kc, kp, vc, vp, tc, tp = _attn_specs(nb - 1)
    smem = pl.BlockSpec(memory_space=pltpu.SMEM)
    qblk = pl.BlockSpec((ATTN_BLOCK, 512), lambda p, i: (jnp.minimum(i, nb - 1), p))
    kvout = pl.BlockSpec((ATTN_BLOCK, LANES), lambda p, i: (jnp.maximum(i - 1, 0), p))
    return pl.pallas_call(
        body, name="attn_bwd", grid=(4, nb + 1),
        in_specs=[smem, q, kc, kp, vc, vp, tc, tc, tc, tp, tp, tp, qblk],
        out_specs=[qblk, kvout, kvout, pl.BlockSpec((1, 8, LANES), lambda p, i: (p, 0, 0))],
        out_shape=[jax.ShapeDtypeStruct((t, ATTN_WIDTH), BF16), jax.ShapeDtypeStruct((t, KV_WIDTH), BF16),
                   jax.ShapeDtypeStruct((t, KV_WIDTH), BF16), jax.ShapeDtypeStruct((4, 8, LANES), F32)],
        scratch_shapes=[pltpu.VMEM((ATTN_BLOCK, LANES), F32), pltpu.VMEM((ATTN_BLOCK, LANES), F32)],
        compiler_params=_cp(("parallel", "arbitrary")))(sinks, proj, proj, proj, proj, proj, *tables, *tables, dout)


def _shift_rows(x, prev8, j):
    r = pltpu.roll(x, j, 0)
    head = jnp.where(_iota((8, 1), 0) < j, pltpu.roll(prev8, j, 0), r[:8])
    return jnp.concatenate([head, r[8:]], axis=0)


def _shift_rows_up(x, next8, j):
    n = x.shape[0]
    r = pltpu.roll(x, n - j, 0)
    tail = jnp.where(_iota((8, 1), 0) >= 8 - j, pltpu.roll(next8, 8 - j, 0), r[n - 8:])
    return jnp.concatenate([r[:n - 8], tail], axis=0)


def _conv_apply(x, prev8, w, b, taps):
    u = b + x * w[taps - 1:taps]
    for j in range(1, taps):
        u = u + _shift_rows(x, prev8, j) * w[taps - 1 - j:taps - j]
    return u


def _conv_specs(tb, tc, col0, t):
    c0 = col0 // tc
    cur = pl.BlockSpec((tb, tc), lambda j, i: (i, c0 + j))
    prev = pl.BlockSpec((8, tc), lambda j, i: (jnp.maximum(i * (tb // 8) - 1, 0), c0 + j))
    nxt = pl.BlockSpec((8, tc), lambda j, i: (jnp.minimum((i + 1) * (tb // 8), t // 8 - 1), c0 + j))
    return cur, prev, nxt


def _conv_fwd(x, w, b, *, col0, width, act, name):
    t = x.shape[0]
    taps = w.shape[0]
    tb, tc = _rows(t, 512), _tile(width, 1024)
    assert col0 % tc == 0

    def body(x_ref, xp_ref, w_ref, b_ref, o_ref):
        i = pl.program_id(1)
        prev8 = jnp.where(i > 0, xp_ref[...], 0.0)
        u = _conv_apply(x_ref[...], prev8, w_ref[...], b_ref[...], taps)
        if act:
            u = u * _sigmoid(u)
        o_ref[...] = u

    cur, prev, _ = _conv_specs(tb, tc, col0, t)
    par = pl.BlockSpec((taps, tc), lambda j, i: (0, j))
    bias = pl.BlockSpec((1, tc), lambda j, i: (0, j))
    return pl.pallas_call(
        body, name=name, grid=(width // tc, t // tb), in_specs=[cur, prev, par, bias],
        out_specs=pl.BlockSpec((tb, tc), lambda j, i: (i, j)), out_shape=jax.ShapeDtypeStruct((t, width), F32),
        compiler_params=_cp(("parallel", "parallel")))(x, x, w, b)


def _conv_silu_dact(x, w, b, dout, *, col0, width, name):
    t = x.shape[0]
    taps = w.shape[0]
    tb, tc = _rows(t, 512), _tile(width, 1024)

    def body(x_ref, xp_ref, w_ref, b_ref, d_ref, o_ref):
        i = pl.program_id(1)
        prev8 = jnp.where(i > 0, xp_ref[...], 0.0)
        u = _conv_apply(x_ref[...], prev8, w_ref[...], b_ref[...], taps)
        sg = _sigmoid(u)
        o_ref[...] = d_ref[...] * (sg * (1.0 + u * (1.0 - sg)))

    cur, prev, _ = _conv_specs(tb, tc, col0, t)
    par = pl.BlockSpec((taps, tc), lambda j, i: (0, j))
    bias = pl.BlockSpec((1, tc), lambda j, i: (0, j))
    out = pl.BlockSpec((tb, tc), lambda j, i: (i, j))
    return pl.pallas_call(
        body, name=name, grid=(width // tc, t // tb), in_specs=[cur, prev, par, bias, out],
        out_specs=out, out_shape=jax.ShapeDtypeStruct((t, width), F32),
        compiler_params=_cp(("parallel", "parallel")))(x, x, w, b, dout)


def _conv_bwd(x, w, du, *, col0, width, name):
    t = x.shape[0]
    taps = w.shape[0]
    tb, tc = _rows(t, 512), _tile(width, 1024)
    nrow = t // tb

    def body(x_ref, xp_ref, w_ref, du_ref, dun_ref, dx_ref, dw_ref, db_ref):
        i = pl.program_id(1)
        xv = x_ref[...]
        prev8 = jnp.where(i > 0, xp_ref[...], 0.0)
        duv = du_ref[...]
        next8 = jnp.where(i < nrow - 1, dun_ref[...], 0.0)
        wv = w_ref[...]
        dx = duv * wv[taps - 1:taps]
        parts = [jnp.sum(duv * xv, axis=0, keepdims=True)]
        for j in range(1, taps):
            dx = dx + _shift_rows_up(duv, next8, j) * wv[taps - 1 - j:taps - j]
            parts.append(jnp.sum(duv * _shift_rows(xv, prev8, j), axis=0, keepdims=True))
        dx_ref[...] = dx.astype(BF16)
        rowk = _iota((taps, 1), 0)
        dwv = jnp.zeros((taps, tc), F32)
        for j in range(taps):
            dwv = dwv + jnp.where(rowk == taps - 1 - j, parts[j], 0.0)
        dbv = jnp.sum(duv, axis=0, keepdims=True)

        @pl.when(i == 0)
        def _():
            dw_ref[...] = dwv
            db_ref[...] = dbv

        @pl.when(i > 0)
        def _():
            dw_ref[...] += dwv
            db_ref[...] += dbv

    cur, prev, _ = _conv_specs(tb, tc, col0, t)
    dcur, _, dnxt = _conv_specs(tb, tc, 0, t)
    par = pl.BlockSpec((taps, tc), lambda j, i: (0, j))
    bias = pl.BlockSpec((1, tc), lambda j, i: (0, j))
    return pl.pallas_call(
        body, name=name, grid=(width // tc, nrow), in_specs=[cur, prev, par, dcur, dnxt],
        out_specs=[dcur, par, bias],
        out_shape=[jax.ShapeDtypeStruct((t, width), BF16), jax.ShapeDtypeStruct((taps, width), F32),
                   jax.ShapeDtypeStruct((1, width), F32)],
        compiler_params=_cp(("parallel", "arbitrary")))(x, x, w, du, du)


def _swiglu_fwd(u):
    t = u.shape[0]
    tb, tc = _rows(t, 512), _tile(D_FF, 1408)
    nc = D_FF // tc

    def body(g_ref, v_ref, o_ref):
        g = g_ref[...]
        o_ref[...] = (g * _sigmoid(g) * v_ref[...]).astype(BF16)

    return pl.pallas_call(
        body, name="swiglu_fwd", grid=(t // tb, nc),
        in_specs=[pl.BlockSpec((tb, tc), lambda i, j: (i, j)), pl.BlockSpec((tb, tc), lambda i, j: (i, nc + j))],
        out_specs=pl.BlockSpec((tb, tc), lambda i, j: (i, j)), out_shape=jax.ShapeDtypeStruct((t, D_FF), BF16),
        compiler_params=_cp(("parallel", "parallel")))(u, u)


def _swiglu_bwd(u, da):
    t = u.shape[0]
    tb, tc = _rows(t, 512), _tile(D_FF, 1408)
    nc = D_FF // tc

    def body(g_ref, v_ref, da_ref, dg_ref, dv_ref):
        g = g_ref[...]
        dav = da_ref[...].astype(F32)
        sg = _sigmoid(g)
        dg_ref[...] = dav * v_ref[...] * (sg * (1.0 + g * (1.0 - sg)))
        dv_ref[...] = dav * g * sg

    lo = pl.BlockSpec((tb, tc), lambda i, j: (i, j))
    hi = pl.BlockSpec((tb, tc), lambda i, j: (i, nc + j))
    dg, dv = pl.pallas_call(
        body, name="swiglu_bwd", grid=(t // tb, nc), in_specs=[lo, hi, lo], out_specs=[lo, lo],
        out_shape=[jax.ShapeDtypeStruct((t, D_FF), F32), jax.ShapeDtypeStruct((t, D_FF), F32)],
        compiler_params=_cp(("parallel", "parallel")))(u, u, da)
    return dg, dv


def _head_masks():
    lane = _iota((1, 4 * SSD_HEAD_DIM), 1)
    return [((lane >= r * SSD_HEAD_DIM) & (lane < (r + 1) * SSD_HEAD_DIM)).astype(F32) for r in range(4)]


def _expand4(v4, masks):
    lane4 = _iota((1, 4), 1)
    out = 0.0
    for r in range(4):
        out = out + jnp.sum(jnp.where(lane4 == r, v4, 0.0), axis=1, keepdims=True) * masks[r]
    return out


def _collapse4(ve, masks):
    lane4 = _iota((1, 4), 1)
    out = 0.0
    for r in range(4):
        col = jnp.sum(ve * masks[r], axis=1, keepdims=True) * (1.0 / SSD_HEAD_DIM)
        out = out + jnp.where(lane4 == r, col, 0.0)
    return out


def _segsum(v, masks):
    out = 0.0
    for r in range(4):
        out = out + jnp.sum(v * masks[r], axis=1, keepdims=True) * masks[r]
    return out


def _ssd_common(raw4, prow, rawr4, bcol, acol, masks):
    n = SSD_CHUNK
    dt4 = _softplus(raw4 + prow[0:1, :])
    a4 = -jnp.exp(prow[1:2, :])
    dt_e = _expand4(dt4, masks)
    a_e = _expand4(a4, masks)
    d_e = _expand4(prow[2:3, :], masks)
    tril = (_iota((n, n), 0) >= _iota((n, n), 1)).astype(F32)
    acs_e = _dot_exact(tril, dt_e * a_e)
    last_e = acs_e[n - 1:n, :]
    dtr4 = _softplus(rawr4 + bcol)
    triu = (_iota((n, n), 0) <= _iota((n, n), 1)).astype(F32)
    acs_r4 = _dot_exact(dtr4 * (-jnp.exp(acol)), triu)
    return dt_e, a_e, d_e, acs_e, last_e, acs_r4


def _decay_matrix(acs_e, acs_r4, r, masks):
    n = SSD_CHUNK
    col = jnp.sum(acs_e * masks[r], axis=1, keepdims=True) * (1.0 / SSD_HEAD_DIM)
    seg = col - acs_r4[r:r + 1, :]
    causal = _iota((n, n), 0) >= _iota((n, n), 1)
    return jnp.exp(jnp.where(causal, seg, NEG))


def _ssd_specs(t, rev):
    nc = t // SSD_CHUNK
    xb, bb, cb = 0, SSD_INNER // SSD_STATE, (SSD_INNER + BC_WIDTH) // SSD_STATE

    def ch(c):
        return (nc - 1 - c) if rev else c

    x = pl.BlockSpec((SSD_CHUNK, 256), lambda g, c: (ch(c), xb + g))
    bm = pl.BlockSpec((SSD_CHUNK, SSD_STATE), lambda g, c: (ch(c), bb + g))
    cm = pl.BlockSpec((SSD_CHUNK, SSD_STATE), lambda g, c: (ch(c), cb + g))
    dtc = pl.BlockSpec((1, SSD_CHUNK, 4), lambda g, c: (g, ch(c), 0))
    dtr = pl.BlockSpec((1, 4, SSD_CHUNK), lambda g, c: (g, 0, ch(c)))
    prow = pl.BlockSpec((1, 3, 4), lambda g, c: (g, 0, 0))
    pcol = pl.BlockSpec((1, 4, 1), lambda g, c: (g, 0, 0))
    st = pl.BlockSpec((1, 1, SSD_STATE, 256), lambda g, c: (g, ch(c), 0, 0))
    return x, bm, cm, dtc, dtr, prow, pcol, st, ch


def _ssd_params(dt_raw, dt_bias, a_log, ssd_d):
    t = dt_raw.shape[0]
    dtc = dt_raw.reshape(t, SSD_GROUPS, 4).transpose(1, 0, 2)
    dtr = dt_raw.reshape(t, SSD_GROUPS, 4).transpose(1, 2, 0)
    prow = jnp.stack([dt_bias.reshape(SSD_GROUPS, 4), a_log.reshape(SSD_GROUPS, 4),
                      ssd_d.reshape(SSD_GROUPS, 4)], axis=1)
    bcol = dt_bias.reshape(SSD_GROUPS, 4, 1)
    acol = a_log.reshape(SSD_GROUPS, 4, 1)
    return dtc, dtr, prow, bcol, acol


def _ssd_fwd(xbc, params):
    t = xbc.shape[0]
    nc = t // SSD_CHUNK
    dtc, dtr, prow, bcol, acol = params

    def body(x_ref, b_ref, c_ref, dtc_ref, dtr_ref, prow_ref, bcol_ref, acol_ref, y_ref, st_ref, s_scr):
        c = pl.program_id(1)

        @pl.when(c == 0)
        def _():
            s_scr[...] = jnp.zeros_like(s_scr)

        masks = _head_masks()
        dt_e, a_e, d_e, acs_e, last_e, acs_r4 = _ssd_common(
            dtc_ref[0], prow_ref[0], dtr_ref[0], bcol_ref[0], acol_ref[0], masks)
        xv = x_ref[...]
        bm, cm = b_ref[...], c_ref[...]
        s = s_scr[...]
        st_ref[0, 0] = s
        xdt = xv * dt_e
        cb = _dot(cm, bm, 'nt')
        y = _dot(cm, s) * jnp.exp(acs_e) + xv * d_e
        for r in range(4):
            mr = cb * _decay_matrix(acs_e, acs_r4, r, masks)
            y = y + _dot(mr, xdt * masks[r])
        y_ref[...] = y
        w = xdt * jnp.exp(last_e - acs_e)
        s_scr[...] = s * jnp.exp(last_e) + _dot(bm.T, w)

    x, bm, cm, dtcs, dtrs, prs, pcs, st, _ = _ssd_specs(t, False)
    return pl.pallas_call(
        body, name="ssd_fwd", grid=(SSD_GROUPS, nc), in_specs=[x, bm, cm, dtcs, dtrs, prs, pcs, pcs],
        out_specs=[pl.BlockSpec((SSD_CHUNK, 256), lambda g, c: (c, g)), st],
        out_shape=[jax.ShapeDtypeStruct((t, SSD_INNER), F32),
                   jax.ShapeDtypeStruct((SSD_GROUPS, nc, SSD_STATE, 256), F32)],
        scratch_shapes=[pltpu.VMEM((SSD_STATE, 256), F32)],
        compiler_params=_cp(("parallel", "arbitrary")))(xbc, xbc, xbc, dtc, dtr, prow, bcol, acol)


def _ssd_bwd(xbc, params, states, dy):
    t = xbc.shape[0]
    nc = t // SSD_CHUNK
    n = SSD_CHUNK
    dtc, dtr, prow, bcol, acol = params

    def body(x_ref, b_ref, c_ref, dtc_ref, dtr_ref, prow_ref, bcol_ref, acol_ref, st_ref, dy_ref,
             dx_ref, db_ref, dc_ref, ddt_ref, dp_ref, ds_scr):
        c = pl.program_id(1)

        @pl.when(c == 0)
        def _():
            ds_scr[...] = jnp.zeros_like(ds_scr)
            dp_ref[...] = jnp.zeros_like(dp_ref)

        masks = _head_masks()
        raw4 = dtc_ref[0]
        prw = prow_ref[0]
        dt_e, a_e, d_e, acs_e, last_e, acs_r4 = _ssd_common(raw4, prw, dtr_ref[0], bcol_ref[0], acol_ref[0], masks)
        xv = x_ref[...]
        bm, cm = b_ref[...], c_ref[...]
        s = st_ref[0, 0]
        ds = ds_scr[...]
        dyv = dy_ref[...]
        e_e = jnp.exp(acs_e)
        dec_e = jnp.exp(last_e - acs_e)
        cd_e = jnp.exp(last_e)
        xdt = xv * dt_e
        w = xdt * dec_e
        b16, c16, s16, ds16 = bm.astype(BF16), cm.astype(BF16), s.astype(BF16), ds.astype(BF16)
        cb = _dot(c16, b16, 'nt')
        yoff_raw = _dot(c16, s16)
        dye = dyv * e_e
        dye16 = dye.astype(BF16)
        dcm = _dot(dye16, s16, 'nt')
        ds_scr[...] = ds * cd_e + _dot(cm.T, dye16)
        dacs_e = _segsum(dyv * yoff_raw, masks) * e_e
        dw = _dot(b16, ds16)
        dbm = _dot(w, ds16, 'nt')
        tdec = _segsum(dw * xdt, masks) * dec_e
        dacs_e = dacs_e - tdec
        dlast_e = jnp.sum(tdec, axis=0, keepdims=True)
        dxdt = dw * dec_e
        dlast_e = dlast_e + _segsum(jnp.sum(ds * s, axis=0, keepdims=True), masks) * cd_e
        dcb = jnp.zeros((n, n), F32)
        for r in range(4):
            lm = _decay_matrix(acs_e, acs_r4, r, masks)
            mr = cb * lm
            dyr16 = (dyv * masks[r]).astype(BF16)
            dm = _dot(dyr16, xdt * masks[r], 'nt')
            dcb = dcb + dm * lm
            dseg = dm * mr
            dcol = jnp.sum(dseg, axis=1, keepdims=True) - jnp.sum(dseg.T, axis=1, keepdims=True)
            dacs_e = dacs_e + dcol * masks[r]
            dxdt = dxdt + _dot(mr.T, dyr16)
        dcm = dcm + _dot(dcb, b16)
        dbm = dbm + _dot(dcb.T, c16)
        dacs_e = dacs_e + jnp.where(_iota((n, 1), 0) == n - 1, dlast_e, 0.0)
        triu = (_iota((n, n), 0) <= _iota((n, n), 1)).astype(F32)
        ddta_e = _dot_exact(triu, dacs_e)
        ddt_e = ddta_e * a_e + _segsum(dxdt * xv, masks)
        dx_ref[...] = dxdt * dt_e + dyv * d_e
        db_ref[...] = dbm
        dc_ref[...] = dcm
        draw_e = ddt_e * _sigmoid(_expand4(raw4 + prw[0:1, :], masks))
        ddt_ref[0] = _collapse4(draw_e, masks)
        dbias = _collapse4(jnp.sum(draw_e, axis=0, keepdims=True), masks)
        dalog = _collapse4(jnp.sum(ddta_e * dt_e, axis=0, keepdims=True) * a_e, masks)
        dd = _collapse4(jnp.sum(_segsum(dyv * xv, masks), axis=0, keepdims=True), masks)
        row3 = _iota((3, 1), 0)
        dp_ref[0] += (jnp.where(row3 == 0, dbias, 0.0) + jnp.where(row3 == 1, dalog, 0.0)
                      + jnp.where(row3 == 2, dd, 0.0))

    x, bm, cm, dtcs, dtrs, prs, pcs, st, ch = _ssd_specs(t, True)
    yblk = pl.BlockSpec((SSD_CHUNK, 256), lambda g, c: (ch(c), g))
    nblk = pl.BlockSpec((SSD_CHUNK, SSD_STATE), lambda g, c: (ch(c), g))
    return pl.pallas_call(
        body, name="ssd_bwd", grid=(SSD_GROUPS, nc),
        in_specs=[x, bm, cm, dtcs, dtrs, prs, pcs, pcs, st, yblk],
        out_specs=[yblk, nblk, nblk, dtcs, prs],
        out_shape=[jax.ShapeDtypeStruct((t, SSD_INNER), F32), jax.ShapeDtypeStruct((t, BC_WIDTH), F32),
                   jax.ShapeDtypeStruct((t, BC_WIDTH), F32), jax.ShapeDtypeStruct((SSD_GROUPS, t, 4), F32),
                   jax.ShapeDtypeStruct((SSD_GROUPS, 3, 4), F32)],
        scratch_shapes=[pltpu.VMEM((SSD_STATE, 256), F32)],
        compiler_params=_cp(("parallel", "arbitrary")))(xbc, xbc, xbc, dtc, dtr, prow, bcol, acol, states, dy)


GROUP_W = SSD_INNER // SSD_GROUPS


def _mix_specs(tb):
    row = pl.BlockSpec((tb, 2048), lambda i: (i, 0))
    zlo = pl.BlockSpec((tb, 1024), lambda i: (i, O_Z // 1024))
    zhi = pl.BlockSpec((tb, 1024), lambda i: (i, O_Z // 1024 + 1))
    vec = pl.BlockSpec((1, 2048), lambda i: (0, 0))
    return row, zlo, zhi, vec


def _mix_fwd(attn, y, proj, g_attn, g_ssd):
    t = attn.shape[0]
    tb = _rows(t, 256)

    def body(a_ref, y_ref, zlo_ref, zhi_ref, ga_ref, gs_ref, o_ref):
        av = a_ref[...]
        r = lax.rsqrt(jnp.mean(av * av, axis=-1, keepdims=True) + EPS)
        o_ref[:, :ATTN_WIDTH] = (av * r * ga_ref[...]).astype(BF16)
        for g in range(SSD_GROUPS):
            lo, hi = g * GROUP_W, (g + 1) * GROUP_W
            zref = zlo_ref if g < 4 else zhi_ref
            z = zref[:, lo % 1024:lo % 1024 + GROUP_W]
            yg = y_ref[:, lo:hi] * (z * _sigmoid(z))
            rg = lax.rsqrt(jnp.mean(yg * yg, axis=-1, keepdims=True) + EPS)
            o_ref[:, ATTN_WIDTH + lo:ATTN_WIDTH + hi] = (yg * rg * gs_ref[:, lo:hi]).astype(BF16)

    row, zlo, zhi, vec = _mix_specs(tb)
    return pl.pallas_call(
        body, name="mix_fwd", grid=(t // tb,), in_specs=[row, row, zlo, zhi, vec, vec],
        out_specs=pl.BlockSpec((tb, 4096), lambda i: (i, 0)), out_shape=jax.ShapeDtypeStruct((t, 4096), BF16),
        compiler_params=_cp(("parallel",)))(attn, y, proj, proj, g_attn, g_ssd)


def _mix_bwd(dmix, attn, y, proj, g_attn, g_ssd):
    t = attn.shape[0]
    tb = _rows(t, 256)

    def body(dm_ref, a_ref, y_ref, zlo_ref, zhi_ref, ga_ref, gs_ref, da_ref, dy_ref, dz_ref, dga_ref, dgs_ref):
        i = pl.program_id(0)
        av = a_ref[...]
        dn = dm_ref[:, :ATTN_WIDTH].astype(F32)
        r = lax.rsqrt(jnp.mean(av * av, axis=-1, keepdims=True) + EPS)
        u = dn * ga_ref[...]
        da_ref[...] = r * u - av * (r * r * r * jnp.mean(u * av, axis=-1, keepdims=True))
        dga = jnp.sum(dn * av * r, axis=0, keepdims=True)

        @pl.when(i == 0)
        def _():
            dga_ref[...] = dga

        @pl.when(i > 0)
        def _():
            dga_ref[...] += dga

        for g in range(SSD_GROUPS):
            lo, hi = g * GROUP_W, (g + 1) * GROUP_W
            zref = zlo_ref if g < 4 else zhi_ref
            z = zref[:, lo % 1024:lo % 1024 + GROUP_W]
            yv = y_ref[:, lo:hi]
            sg = _sigmoid(z)
            sz = z * sg
            yg = yv * sz
            rg = lax.rsqrt(jnp.mean(yg * yg, axis=-1, keepdims=True) + EPS)
            do = dm_ref[:, ATTN_WIDTH + lo:ATTN_WIDTH + hi].astype(F32)
            ug = do * gs_ref[:, lo:hi]
            dyg = rg * ug - yg * (rg * rg * rg * jnp.mean(ug * yg, axis=-1, keepdims=True))
            dy_ref[:, lo:hi] = dyg * sz
            dz_ref[:, lo:hi] = (dyg * yv * (sg * (1.0 + z * (1.0 - sg)))).astype(BF16)
            dgs = jnp.sum(do * yg * rg, axis=0, keepdims=True)

            @pl.when(i == 0)
            def _():
                dgs_ref[:, lo:hi] = dgs

            @pl.when(i > 0)
            def _():
                dgs_ref[:, lo:hi] += dgs

    row, zlo, zhi, vec = _mix_specs(tb)
    return pl.pallas_call(
        body, name="mix_bwd", grid=(t // tb,),
        in_specs=[pl.BlockSpec((tb, 4096), lambda i: (i, 0)), row, row, zlo, zhi, vec, vec],
        out_specs=[row, row, row, vec, vec],
        out_shape=[jax.ShapeDtypeStruct((t, 2048), F32), jax.ShapeDtypeStruct((t, 2048), F32),
                   jax.ShapeDtypeStruct((t, 2048), BF16), jax.ShapeDtypeStruct((1, 2048), F32),
                   jax.ShapeDtypeStruct((1, 2048), F32)],
        compiler_params=_cp(("arbitrary",)))(dmix, attn, y, proj, proj, g_attn, g_ssd)


def _adamw(w, g, m, v, name):
    r, c = w.shape
    tb = _rows(r, 256)
    c1 = 1.0 - ADAM_B1 ** ADAM_STEP
    c2 = 1.0 - ADAM_B2 ** ADAM_STEP

    def body(w_ref, g_ref, m_ref, v_ref, d_ref, m2_ref, v2_ref):
        gv = g_ref[...]
        m2 = ADAM_B1 * m_ref[...] + (1.0 - ADAM_B1) * gv
        v2 = ADAM_B2 * v_ref[...] + (1.0 - ADAM_B2) * (gv * gv)
        d_ref[...] = -ADAM_LR * ((m2 / c1) / (jnp.sqrt(v2 / c2) + ADAM_EPS) + ADAM_WD * w_ref[...])
        m2_ref[...] = m2
        v2_ref[...] = v2

    blk = pl.BlockSpec((tb, c), lambda i: (i, 0))
    shp = jax.ShapeDtypeStruct((r, c), F32)
    return pl.pallas_call(body, name=name, grid=(r // tb,), in_specs=[blk] * 4, out_specs=[blk] * 3,
                          out_shape=[shp] * 3, compiler_params=_cp(("parallel",)))(w, g, m, v)


def _sum_own_half(g4, recv, pos, name):
    _, r, c = g4.shape
    h = r // 2
    tb = _rows(h, 128)
    nh = h // tb

    def body(pos_ref, a_ref, b_ref, o_ref):
        o_ref[...] = (a_ref[...] + b_ref[...]).astype(BF16)

    grid_spec = pltpu.PrefetchScalarGridSpec(
        num_scalar_prefetch=1, grid=(N_CHIPS, nh),
        in_specs=[pl.BlockSpec((1, tb, c), lambda j, i, pref: (j, pref[0] * nh + i, 0)),
                  pl.BlockSpec((1, tb, c), lambda j, i, pref: (j, i, 0))],
        out_specs=pl.BlockSpec((1, tb, c), lambda j, i, pref: (j, i, 0)))
    return pl.pallas_call(body, name=name, grid_spec=grid_spec,
                          out_shape=jax.ShapeDtypeStruct((N_CHIPS, h, c), BF16),
                          compiler_params=_cp(("parallel", "parallel")))(pos, g4, recv)


def _sum_chips(g4, recv, parts, pos, name):
    _, r, c = g4.shape
    h = r // 2
    tb = _rows(h, 128)
    nh = h // tb

    def body(pos_ref, a_ref, b_ref, p_ref, o_ref):
        own = a_ref[0] + b_ref[0]
        o_ref[...] = ((own + p_ref[0].astype(F32)) + p_ref[1].astype(F32)) + p_ref[2].astype(F32)

    grid_spec = pltpu.PrefetchScalarGridSpec(
        num_scalar_prefetch=1, grid=(nh,),
        in_specs=[pl.BlockSpec((1, tb, c), lambda i, pref: (pref[1], pref[0] * nh + i, 0)),
                  pl.BlockSpec((1, tb, c), lambda i, pref: (pref[1], i, 0)),
                  pl.BlockSpec((3, tb, c), lambda i, pref: (0, i, 0))],
        out_specs=pl.BlockSpec((tb, c), lambda i, pref: (i, 0)))
    return pl.pallas_call(body, name=name, grid_spec=grid_spec, out_shape=jax.ShapeDtypeStruct((h, c), F32),
                          compiler_params=_cp(("parallel",)))(pos, g4, recv, parts)


def _me():
    return lax.axis_index("x"), lax.axis_index("y"), lax.axis_index("c")


def _flip(v, bit):
    return (1 - v) if bit else v


CHIP_FLIPS = [(1, 0), (0, 1), (1, 1)]


def _allgather_weights(shards):
    n = len(shards)

    def body(*refs):
        ins, outs = refs[:n], refs[n:2 * n]
        send_sems, recv_sems = refs[2 * n:]
        x, y, c = _me()
        chip = 2 * x + y
        sib = (x, y, 1 - c)

        def remote(src, dst, k, to):
            return pltpu.make_async_remote_copy(src_ref=src, dst_ref=dst, send_sem=send_sems.at[k],
                                                recv_sem=recv_sems.at[k], device_id=to, device_id_type=MESH)

        sends = []
        for t in range(n):
            h = ins[t].shape[0] // 2
            mine = pl.ds(c * h, h)
            for k, (fx, fy) in enumerate(CHIP_FLIPS):
                cp = remote(ins[t].at[mine], outs[t].at[chip, mine], 6 * t + k, (_flip(x, fx), _flip(y, fy), c))
                cp.start()
                sends.append(cp)
        for t in range(n):
            h = ins[t].shape[0] // 2
            mine = pl.ds(c * h, h)
            for k, (fx, fy) in enumerate(CHIP_FLIPS):
                src_chip = 2 * _flip(x, fx) + _flip(y, fy)
                landed = outs[t].at[src_chip, mine]
                remote(landed, landed, 6 * t + k, (x, y, c)).wait_recv()
                fw = remote(landed, landed, 6 * t + 3 + k, sib)
                fw.start()
                sends.append(fw)
        for t in range(n):
            h = ins[t].shape[0] // 2
            other = pl.ds((1 - c) * h, h)
            for k, (fx, fy) in enumerate(CHIP_FLIPS):
                src_chip = 2 * _flip(x, fx) + _flip(y, fy)
                got = outs[t].at[src_chip, other]
                remote(got, got, 6 * t + 3 + k, (x, y, c)).wait_recv()
        for cp in sends:
            cp.wait_send()

    return pl.pallas_call(
        body, name="allgather_weights", in_specs=[HBM_SPEC] * n, out_specs=[HBM_SPEC] * n,
        out_shape=[jax.ShapeDtypeStruct((N_CHIPS,) + s.shape, s.dtype) for s in shards],
        scratch_shapes=[pltpu.SemaphoreType.DMA((6 * n,)), pltpu.SemaphoreType.DMA((6 * n,))],
        compiler_params=pltpu.CompilerParams(has_side_effects=True))(*shards)


def _exchange_halves(g4s, name):
    n = len(g4s)

    def body(*refs):
        ins, outs = refs[:n], refs[n:2 * n]
        send_sems, recv_sems = refs[2 * n:]
        x, y, c = _me()
        cps = []
        for t in range(n):
            h = ins[t].shape[1] // 2
            cp = pltpu.make_async_remote_copy(
                src_ref=ins[t].at[:, pl.ds((1 - c) * h, h)], dst_ref=outs[t], send_sem=send_sems.at[t],
                recv_sem=recv_sems.at[t], device_id=(x, y, 1 - c), device_id_type=MESH)
            cp.start()
            cps.append(cp)
        for cp in cps:
            cp.wait()

    return pl.pallas_call(
        body, name=name, in_specs=[HBM_SPEC] * n, out_specs=[HBM_SPEC] * n,
        out_shape=[jax.ShapeDtypeStruct((N_CHIPS, g.shape[1] // 2, g.shape[2]), g.dtype) for g in g4s],
        scratch_shapes=[pltpu.SemaphoreType.DMA((n,)), pltpu.SemaphoreType.DMA((n,))],
        compiler_params=pltpu.CompilerParams(has_side_effects=True))(*g4s)


def _share_halves(ghs, name):
    n = len(ghs)

    def body(*refs):
        ins, outs = refs[:n], refs[n:2 * n]
        send_sems, recv_sems = refs[2 * n:]
        x, y, c = _me()
        cps = []
        for t in range(n):
            cp = pltpu.make_async_remote_copy(
                src_ref=ins[t], dst_ref=outs[t], send_sem=send_sems.at[t], recv_sem=recv_sems.at[t],
                device_id=(x, y, 1 - c), device_id_type=MESH)
            cp.start()
            cps.append(cp)
        for cp in cps:
            cp.wait()

    return pl.pallas_call(
        body, name=name, in_specs=[HBM_SPEC] * n, out_specs=[HBM_SPEC] * n,
        out_shape=[jax.ShapeDtypeStruct(g.shape, g.dtype) for g in ghs],
        scratch_shapes=[pltpu.SemaphoreType.DMA((n,)), pltpu.SemaphoreType.DMA((n,))],
        compiler_params=pltpu.CompilerParams(has_side_effects=True))(*ghs)


SEM_SPEC = pl.BlockSpec(memory_space=pltpu.SEMAPHORE)
ANY_SPEC = pl.BlockSpec(memory_space=pl.ANY)
DATAFLOW = pltpu.SideEffectType.DATAFLOW_SIDE_EFFECTING
TOKEN = jax.ShapeDtypeStruct((8, LANES), F32)


def _in_hbm(a):
    return pltpu.with_memory_space_constraint(a, pltpu.HBM)


def _push_start(srcs, land_shapes, route, name):
    n = len(srcs)
    lands = [lax.empty(shp, s.dtype) for shp, s in zip(land_shapes, srcs)]

    def body(*refs):
        ins, lnd = refs[:n], refs[n:2 * n]
        send_sems, recv_sems = refs[2 * n], refs[2 * n + 1]
        token = refs[-1]
        x, y, c = _me()
        for t in range(n):
            for k, (fx, fy) in enumerate(CHIP_FLIPS):
                src, dst = route(ins[t], lnd[t], k, x, y)
                pltpu.make_async_remote_copy(
                    src_ref=src, dst_ref=dst, send_sem=send_sems.at[3 * t + k], recv_sem=recv_sems.at[3 * t + k],
                    device_id=(_flip(x, fx), _flip(y, fy), c), device_id_type=MESH).start()
        token[...] = jnp.zeros_like(token)

    bufs = [_in_hbm(a) for a in list(srcs) + lands]
    outs = pl.pallas_call(
        body, name=name,
        out_shape=(pltpu.SemaphoreType.DMA((3 * n,)), pltpu.SemaphoreType.DMA((3 * n,)),
                   *[pltpu.HBM(b.shape, b.dtype) for b in bufs], TOKEN),
        in_specs=[HBM_SPEC] * (2 * n),
        out_specs=(SEM_SPEC, SEM_SPEC, *[HBM_SPEC] * (2 * n), pl.BlockSpec(memory_space=pltpu.VMEM)),
        input_output_aliases={i: 2 + i for i in range(2 * n)},
        compiler_params=pltpu.CompilerParams(has_side_effects=DATAFLOW))(*bufs)
    return outs[0], outs[1], list(outs[2:2 + n]), list(outs[2 + n:2 + 2 * n]), outs[-1]


def _push_wait(send_sems, recv_sems, srcs, lands, after, route, name):
    n = len(srcs)

    def body(*refs):
        ins, lnd = refs[:n], refs[n:2 * n]
        ssem, rsem = refs[2 * n], refs[2 * n + 1]
        x, y, c = _me()
        for t in range(n):
            for k, (fx, fy) in enumerate(CHIP_FLIPS):
                src, dst = route(ins[t], lnd[t], k, x, y)
                cp = pltpu.make_async_remote_copy(
                    src_ref=src, dst_ref=dst, send_sem=ssem.at[3 * t + k], recv_sem=rsem.at[3 * t + k],
                    device_id=(_flip(x, fx), _flip(y, fy), c), device_id_type=MESH)
                cp.wait_send()
                cp.wait_recv()

    bufs = list(srcs) + list(lands)
    outs = pl.pallas_call(
        body, name=name, out_shape=tuple(pltpu.HBM(b.shape, b.dtype) for b in bufs),
        in_specs=[HBM_SPEC] * (2 * n) + [SEM_SPEC, SEM_SPEC, ANY_SPEC], out_specs=tuple([HBM_SPEC] * (2 * n)),
        input_output_aliases={i: i for i in range(2 * n)},
        compiler_params=pltpu.CompilerParams(has_side_effects=DATAFLOW))(*bufs, send_sems, recv_sems, after)
    return list(outs[:n]), list(outs[n:])


def _route_gather(src, land, k, x, y):
    return src, land.at[2 * x + y]


def _route_gather_wait(src, land, k, x, y):
    fx, fy = CHIP_FLIPS[k]
    return src, land.at[2 * _flip(x, fx) + _flip(y, fy)]


def _route_scatter(src, land, k, x, y):
    fx, fy = CHIP_FLIPS[k]
    return src.at[2 * _flip(x, fx) + _flip(y, fy)], land.at[k]


def _allreduce_small(v):
    r = v.shape[0]

    def body(v_ref, o_ref, buf, send_sems, recv_sems):
        x, y, c = _me()
        me = 4 * x + 2 * y + c
        buf[0] = v_ref[...]
        cps = []
        for k in range(1, 8):
            kx, ky, kc = (k >> 2) & 1, (k >> 1) & 1, k & 1
            cp = pltpu.make_async_remote_copy(
                src_ref=v_ref, dst_ref=buf.at[k], send_sem=send_sems.at[k - 1], recv_sem=recv_sems.at[k - 1],
                device_id=(_flip(x, kx), _flip(y, ky), _flip(c, kc)), device_id_type=MESH)
            cp.start()
            cps.append(cp)
        for cp in cps:
            cp.wait()
        acc = buf[me]
        for d in range(1, 8):
            acc = acc + buf[jnp.bitwise_xor(me, d)]
        o_ref[...] = acc

    vm = pl.BlockSpec(memory_space=pltpu.VMEM)
    return pl.pallas_call(
        body, name="allreduce_small", in_specs=[vm], out_specs=vm, out_shape=jax.ShapeDtypeStruct(v.shape, F32),
        scratch_shapes=[pltpu.VMEM((8, r, LANES), F32), pltpu.SemaphoreType.DMA((7,)),
                        pltpu.SemaphoreType.DMA((7,))],
        compiler_params=pltpu.CompilerParams(has_side_effects=True, vmem_limit_bytes=VMEM_LIMIT))(v)


def _grad_reduce_begin(g4, pos, tag):
    recv = _exchange_halves([g4], name="grad_exchange_halves_" + tag)[0]
    p16 = _sum_own_half(g4, recv, pos, name="grad_sum_pair_" + tag)
    send_sems, recv_sems, srcs, lands, token = _push_start(
        [p16], [(3,) + p16.shape[1:]], _route_scatter, name="grad_scatter_start_" + tag)
    return (g4, recv, send_sems, recv_sems, srcs, lands, tag), token


def _grad_reduce_finish(state, pos, after):
    g4, recv, send_sems, recv_sems, srcs, lands, tag = state
    parts = _push_wait(send_sems, recv_sems, srcs, lands, after, _route_scatter,
                       name="grad_scatter_wait_" + tag)[1][0]
    mine = _sum_chips(g4, recv, parts, pos, name="grad_sum_chips_" + tag)
    theirs = _share_halves([mine], name="grad_share_halves_" + tag)[0]
    return jnp.where(pos[0] == 0, jnp.concatenate([mine, theirs], axis=0), jnp.concatenate([theirs, mine], axis=0))


def _local_step(x, tgt, p, w_in_main, w_in_dt, hooks):
    t = x.shape[0]
    tables = _rope_tables(t)
    sinks = p['sinks'].reshape(N_Q_HEADS)

    def told(name, value):
        tok = hooks.grad_ready(name, value)
        return () if tok is None else (tok,)

    xn = _rmsnorm_fwd(x, p['norm_mix'], "norm_mix_fwd")
    proj = _matmul(xn, w_in_main, mode='nn', name="in_proj", deps=hooks.first_deps)
    dt_raw = _matmul(xn, w_in_dt, mode='nn', name="in_proj_dt")[:, :SSD_HEADS]
    attn = _attn_fwd(proj, sinks, tables)
    conv_b = p['ssd_conv_b']
    xbc = _conv_fwd(proj, p['ssd_conv_w'], conv_b, col0=O_XBC, width=CONV_CH, act=True, name="ssd_conv_fwd")
    sp = _ssd_params(dt_raw, p['dt_bias'].reshape(-1), p['a_log'].reshape(-1), p['ssd_d'].reshape(-1))
    y, states = _ssd_fwd(xbc, sp)
    mix = _mix_fwd(attn, y, proj, p['attn_out_norm'], p['ssd_norm'])
    w_out, w_up, w_down = hooks.rest_weights(mix)
    h1 = _matmul(mix, w_out, mode='nn', name="out_proj", add=x)
    hn = _rmsnorm_fwd(h1, p['norm_ffn'], "norm_ffn_fwd")
    u0 = _matmul(hn, w_up, mode='nn', name="ffn_up")
    u = _conv_fwd(u0, p['ffn_conv_w'], p['ffn_conv_b'], col0=0, width=2 * D_FF, act=False, name="ffn_conv_fwd")
    a = _swiglu_fwd(u)
    h2 = _matmul(a, w_down, mode='nn', name="ffn_down", add=h1)
    loss, dh2, g_norm_final = _final_loss(h2, p['norm_final'].reshape(1, D_MODEL), tgt)

    g = {}
    da = _matmul(dh2, w_down, mode='nt', name="ffn_down_dx", out_dtype=BF16, tn=1408)
    g['w_down'] = _matmul(a, dh2, mode='tn', name="ffn_down_dw", tm=1408)
    dep = told('w_down', g['w_down'])
    dug, duv = _swiglu_bwd(u, da)
    du = jnp.concatenate([dug, duv], axis=1)
    du0, g['ffn_conv_w'], g['ffn_conv_b'] = _conv_bwd(u0, p['ffn_conv_w'], du, col0=0, width=2 * D_FF,
                                                      name="ffn_conv_bwd")
    g['w_up'] = _matmul(hn, du0, mode='tn', name="ffn_up_dw", deps=dep)
    dep = told('w_up', g['w_up'])
    dhn = _matmul(du0, w_up, mode='nt', name="ffn_up_dx", out_dtype=BF16, deps=dep)
    dh1, g['norm_ffn'] = _rmsnorm_bwd(h1, p['norm_ffn'], dhn, dh2, "norm_ffn_bwd")

    g['w_out'] = _matmul(mix, dh1, mode='tn', name="out_proj_dw")
    dep = told('w_out', g['w_out'])
    dmix = _matmul(dh1, w_out, mode='nt', name="out_proj_dx", out_dtype=BF16, deps=dep)
    dattn, dy, dz, g['attn_out_norm'], g['ssd_norm'] = _mix_bwd(dmix, attn, y, proj, p['attn_out_norm'],
                                                                p['ssd_norm'])
    dq, dk, dv, dsink = _attn_bwd(proj, sinks, tables, dattn)
    g['sinks'] = dsink[:, :, 0].reshape(1, N_Q_HEADS)
    dxs, dbm, dcm, ddt8, dpar = _ssd_bwd(xbc, sp, states, dy)
    g['dt_bias'] = dpar[:, 0, :].reshape(1, SSD_HEADS)
    g['a_log'] = dpar[:, 1, :].reshape(1, SSD_HEADS)
    g['ssd_d'] = dpar[:, 2, :].reshape(1, SSD_HEADS)
    dxbc_act = jnp.concatenate([dxs, dbm, dcm], axis=1)
    dconv = _conv_silu_dact(proj, p['ssd_conv_w'], conv_b, dxbc_act, col0=O_XBC, width=CONV_CH,
                            name="ssd_conv_dact")
    dxbc, g['ssd_conv_w'], g['ssd_conv_b'] = _conv_bwd(proj, p['ssd_conv_w'], dconv, col0=O_XBC, width=CONV_CH,
                                                       name="ssd_conv_bwd")
    dproj = jnp.concatenate([dq, dk, dv, dz, dxbc], axis=1)
    ddt = ddt8.transpose(1, 0, 2).reshape(t, SSD_HEADS)
    ddt_pad = jnp.pad(ddt, ((0, 0), (0, LANES - SSD_HEADS))).astype(BF16)
    g['w_in'] = (_matmul(xn, dproj, mode='tn', name="in_proj_dw"),
                 _matmul(xn, ddt_pad, mode='tn', name="in_proj_dt_dw"))
    dep = told('w_in', g['w_in'])
    dxn_dt = _matmul(ddt_pad, w_in_dt, mode='nt', name="in_proj_dt_dx", deps=dep)
    dxn = _matmul(dproj, w_in_main, mode='nt', name="in_proj_dx", out_dtype=BF16, add=dxn_dt)
    dx, g['norm_mix'] = _rmsnorm_bwd(x, p['norm_mix'], dxn, dh1, "norm_mix_bwd")
    g['norm_final'] = g_norm_final
    return loss, dx, g


def _pack(arrs):
    flat = jnp.concatenate([a.reshape(-1) for a in arrs])
    n = flat.shape[0]
    rows = -(-n // LANES)
    rows = -(-rows // 8) * 8
    return jnp.pad(flat, (0, rows * LANES - n)).reshape(rows, LANES)


def _unpack(packed, shapes):
    flat = packed.reshape(-1)
    out, off = [], 0
    for s in shapes:
        n = 1
        for d in s:
            n *= d
        out.append(flat[off:off + n].reshape(s))
        off += n
    return out


def _whole_from_cols(gathered, own, chip):
    return jnp.concatenate([jnp.where(chip == j, own, gathered[j]) for j in range(N_CHIPS)], axis=1)


def _whole_from_rows(gathered, own, chip):
    return lax.dynamic_update_slice(gathered, own[None], (chip, 0, 0)).reshape(-1, own.shape[1])


class _StepHooks:
    def __init__(self, first_deps, rest_weights, grad_ready):
        self.first_deps = first_deps
        self.rest_weights = rest_weights
        self.grad_ready = grad_ready


def _owner_major(gfull, axis):
    if axis == 0:
        return gfull.reshape(N_CHIPS, gfull.shape[0] // N_CHIPS, gfull.shape[1])
    cw = gfull.shape[1] // N_CHIPS
    return jnp.stack([gfull[:, j * cw:(j + 1) * cw] for j in range(N_CHIPS)], axis=0)


def kernel(x, norm_mix, w_in, sinks, attn_out_norm, ssd_conv_w, ssd_conv_b, dt_bias, a_log, ssd_d, ssd_norm, w_out, norm_ffn, w_up, ffn_conv_w, ffn_conv_b, w_down, norm_final, loss_target, m_norm_mix, m_w_in, m_sinks, m_attn_out_norm, m_ssd_conv_w, m_ssd_conv_b, m_dt_bias, m_a_log, m_ssd_d, m_ssd_norm, m_w_out, m_norm_ffn, m_w_up, m_ffn_conv_w, m_ffn_conv_b, m_w_down, m_norm_final, v_norm_mix, v_w_in, v_sinks, v_attn_out_norm, v_ssd_conv_w, v_ssd_conv_b, v_dt_bias, v_a_log, v_ssd_d, v_ssd_norm, v_w_out, v_norm_ffn, v_w_up, v_ffn_conv_w, v_ffn_conv_b, v_w_down, v_norm_final):
    args = dict(locals())
    w = {n: args[n] for n in WEIGHTS}
    m = {n: args['m_' + n] for n in WEIGHTS}
    v = {n: args['v_' + n] for n in WEIGHTS}
    xi, yi, ci = _me()
    chip = 2 * xi + yi
    pos = jnp.stack([ci, chip]).astype(jnp.int32)

    def place(shard, full_cols):
        z = jnp.zeros((shard.shape[0], full_cols), F32)
        return lax.dynamic_update_slice(z, shard * 0.5, (0, chip * shard.shape[1]))

    conv_pack = _pack([place(ssd_conv_w[0], CONV_CH), place(ffn_conv_w[0], 2 * D_FF)])
    conv_full = _allreduce_small(conv_pack)
    ssd_conv_w_full, ffn_conv_w_full = _unpack(conv_full, [(SSD_CONV, CONV_CH), (FFN_CONV, 2 * D_FF)])

    in_shard = w_in[0].astype(BF16)
    full_in = _whole_from_cols(_allgather_weights([in_shard])[0], in_shard, chip)
    w_in_main = full_in[:, :MAIN_WIDTH]
    w_in_dt = jnp.pad(full_in[:, MAIN_WIDTH:], ((0, 0), (0, LANES - SSD_HEADS)))
    rest = [w_out[0].astype(BF16), w_up[0].astype(BF16), w_down[0].astype(BF16)]
    rest_send, rest_recv, rest_srcs, rest_lands, rest_token = _push_start(
        rest, [(N_CHIPS,) + s.shape for s in rest], _route_gather, name="gather_rest_start")

    def rest_weights(after):
        own, got = _push_wait(rest_send, rest_recv, rest_srcs, rest_lands, after, _route_gather_wait,
                              name="gather_rest_wait")
        return (_whole_from_rows(got[0], own[0], chip), _whole_from_cols(got[1], own[1], chip),
                _whole_from_rows(got[2], own[2], chip))

    reductions = {}

    def grad_ready(name, value):
        if name == 'w_in':
            value = jnp.concatenate([value[0], value[1][:, :SSD_HEADS]], axis=1)
        g4 = _owner_major(value, 1 if name in ('w_in', 'w_up') else 0)
        reductions[name], token = _grad_reduce_begin(g4, pos, name)
        return token

    small = {
        'norm_mix': norm_mix, 'sinks': sinks, 'attn_out_norm': attn_out_norm, 'ssd_conv_w': ssd_conv_w_full,
        'ssd_conv_b': ssd_conv_b, 'dt_bias': dt_bias, 'a_log': a_log, 'ssd_d': ssd_d, 'ssd_norm': ssd_norm,
        'norm_ffn': norm_ffn, 'ffn_conv_w': ffn_conv_w_full, 'ffn_conv_b': ffn_conv_b, 'norm_final': norm_final,
    }
    loss, dx, g = _local_step(x[0], loss_target[0], small, w_in_main, w_in_dt,
                              _StepHooks((rest_token,), rest_weights, grad_ready))
    gbig = {n: _grad_reduce_finish(reductions[n], pos, dx) for n in ('w_down', 'w_up', 'w_out', 'w_in')}

    small_names = [n for n in WEIGHTS if n not in BIG]
    small_g = [loss[:, :1]] + [g[n] for n in small_names]
    small_shapes = [(1, 1)] + [tuple(a.shape) for a in small_g[1:]]
    red = _unpack(_allreduce_small(_pack(small_g)), small_shapes)
    loss_out = red[0].reshape(())
    gsm = dict(zip(small_names, red[1:]))
    gsm['ssd_conv_w'] = lax.dynamic_slice(gsm['ssd_conv_w'], (0, chip * ssd_conv_w.shape[2]),
                                          (SSD_CONV, ssd_conv_w.shape[2]))
    gsm['ffn_conv_w'] = lax.dynamic_slice(gsm['ffn_conv_w'], (0, chip * ffn_conv_w.shape[2]),
                                          (FFN_CONV, ffn_conv_w.shape[2]))

    grads, deltas, new_m, new_v = {}, {}, {}, {}
    for n in BIG:
        grads[n] = gbig[n][None]
        d, m2, v2 = _adamw(w[n][0], gbig[n], m[n][0], v[n][0], name="adamw_" + n)
        deltas[n], new_m[n], new_v[n] = d[None], m2[None], v2[None]
    shapes = [tuple(w[n].shape) for n in small_names]
    gp = _pack([gsm[n] for n in small_names])
    d, m2, v2 = _adamw(_pack([w[n] for n in small_names]), gp, _pack([m[n] for n in small_names]),
                       _pack([v[n] for n in small_names]), name="adamw_small")
    for n, gg, dd, mm, vv in zip(small_names, _unpack(gp, shapes), _unpack(d, shapes), _unpack(m2, shapes),
                                 _unpack(v2, shapes)):
        grads[n], deltas[n], new_m[n], new_v[n] = gg, dd, mm, vv

    return (loss_out, dx[None], *[grads[n] for n in WEIGHTS], *[deltas[n] for n in WEIGHTS],
            *[new_m[n] for n in WEIGHTS], *[new_v[n] for n in WEIGHTS])
```

```python
import functools

import jax
import jax.numpy as jnp
from jax import lax
from jax.experimental import pallas as pl
from jax.experimental.pallas import tpu as pltpu

F32 = jnp.float32
BF16 = jnp.bfloat16

D_MODEL = 2048
N_Q_HEADS = 32
N_KV_HEADS = 8
HEAD_DIM = 64
WINDOW = 128
ATTN_BLOCK = 128
ROT_DIM = 16
ROPE_THETA = 500000.0
SSD_HEADS = 32
SSD_HEAD_DIM = 64
SSD_INNER = 2048
SSD_GROUPS = 8
SSD_STATE = 128
SSD_CONV = 4
SSD_CHUNK = 128
ATTN_WIDTH = 2048
KV_WIDTH = 512
BC_WIDTH = 1024
CONV_CH = 4096
IN_PROJ_WIDTH = 9248
MAIN_WIDTH = 9216
D_FF = 5632
FFN_CONV = 3
EPS = 1e-6
O_Q, O_K, O_V, O_Z, O_XBC, O_DT = 0, 2048, 2560, 3072, 5120, 9216

ADAM_LR = 0.001
ADAM_B1 = 0.9
ADAM_B2 = 0.999
ADAM_EPS = 1e-08
ADAM_WD = 0.01
ADAM_STEP = 10

N_CHIPS = 4
NEG = -1e30
LANES = 128
VMEM_LIMIT = 48 * 1024 * 1024
MESH = pl.DeviceIdType.MESH
HBM_SPEC = pl.BlockSpec(memory_space=pltpu.HBM)

WEIGHTS = ['norm_mix', 'w_in', 'sinks', 'attn_out_norm', 'ssd_conv_w', 'ssd_conv_b', 'dt_bias', 'a_log', 'ssd_d',
           'ssd_norm', 'w_out', 'norm_ffn', 'w_up', 'ffn_conv_w', 'ffn_conv_b', 'w_down', 'norm_final']
BIG = ['w_in', 'w_out', 'w_up', 'w_down']


def _cp(sem=None, vmem=VMEM_LIMIT):
    kw = {'vmem_limit_bytes': vmem}
    if sem is not None:
        kw['dimension_semantics'] = sem
    return pltpu.CompilerParams(**kw)


def _tile(n, pref):
    if n <= pref:
        return n
    t = (pref // LANES) * LANES
    while t > LANES and n % t:
        t -= LANES
    assert n % t == 0, (n, pref)
    return t


def _rows(n, pref):
    t = min(n, pref)
    while n % t:
        t -= 8
    return t


def _iota(shape, dim):
    return lax.broadcasted_iota(jnp.int32, shape, dim)


def _dot(a, b, mode='nn'):
    dn = {'nn': (((1,), (0,)), ((), ())), 'nt': (((1,), (1,)), ((), ())), 'tn': (((0,), (0,)), ((), ()))}[mode]
    return lax.dot_general(a.astype(BF16), b.astype(BF16), dn, preferred_element_type=F32)


def _dot_exact(a, b):
    return lax.dot_general(a, b, (((1,), (0,)), ((), ())), precision=lax.Precision.HIGHEST,
                           preferred_element_type=F32)


def _sigmoid(x):
    return 1.0 / (1.0 + jnp.exp(-x))


def _softplus(x):
    return jnp.maximum(x, 0.0) + jnp.log(1.0 + jnp.exp(-jnp.abs(x)))


def _matmul(a, b, *, mode, name, out_dtype=F32, add=None, deps=(), tm=1024, tn=1024, tk=2048,
            a_halves=False, b_halves=False, owner_major=False):
    ash, bsh = (a.shape[1:] if a_halves else a.shape), (b.shape[1:] if b_halves else b.shape)
    if mode == 'nn':
        (m, k), (k2, n) = ash, bsh
    elif mode == 'nt':
        (m, k), (n, k2) = ash, bsh
    else:
        (k, m), (k2, n) = ash, bsh
    if a_halves:
        assert mode == 'nt'
        k = 2 * k
    if b_halves:
        assert mode == 'tn'
        n = 2 * n
    assert k == k2, (a.shape, b.shape, mode)
    tm = _tile(m, tm)
    tn = _tile(n // 4 if owner_major else (n // 2 if b_halves else n), tn)
    tk = _tile(k // 2 if a_halves else k, tk)
    nk = k // tk
    has_add = add is not None
    assert not (has_add and owner_major)

    def body(*refs):
        a_ref, b_ref = refs[:2]
        add_ref = refs[2] if has_add else None

        def finish(r, o_ref):
            if has_add:
                r = r + add_ref[...].astype(F32)
            o_ref[...] = r.astype(out_dtype)

        if nk == 1:
            finish(_dot(a_ref[...], b_ref[...], mode), refs[-1])
            return
        o_ref, acc = refs[-2:]
        kk = pl.program_id(2)

        @pl.when(kk == 0)
        def _():
            acc[...] = _dot(a_ref[...], b_ref[...], mode)

        @pl.when((kk > 0) & (kk < nk - 1))
        def _():
            acc[...] += _dot(a_ref[...], b_ref[...], mode)

        @pl.when(kk == nk - 1)
        def _():
            finish(acc[...] + _dot(a_ref[...], b_ref[...], mode), o_ref)

    if mode == 'tn':
        a_spec = pl.BlockSpec((tk, tm), lambda i, j, kk: (kk, i))
    elif a_halves:
        nkh = nk // 2
        a_spec = pl.BlockSpec((None, tm, tk), lambda i, j, kk: (kk // nkh, i, kk % nkh))
    else:
        a_spec = pl.BlockSpec((tm, tk), lambda i, j, kk: (i, kk))
    if mode == 'nt':
        b_spec = pl.BlockSpec((tn, tk), lambda i, j, kk: (j, kk))
    elif b_halves:
        njh = (n // 2) // tn
        b_spec = pl.BlockSpec((None, tk, tn), lambda i, j, kk: (j // njh, kk, j % njh))
    else:
        b_spec = pl.BlockSpec((tk, tn), lambda i, j, kk: (kk, j))
    if owner_major:
        njo = (n // 4) // tn
        o_spec = pl.BlockSpec((None, tm, tn), lambda i, j, kk: (j // njo, i, j % njo))
        out_shape = jax.ShapeDtypeStruct((N_CHIPS, m, n // 4), out_dtype)
    else:
        o_spec = pl.BlockSpec((tm, tn), lambda i, j, kk: (i, j))
        out_shape = jax.ShapeDtypeStruct((m, n), out_dtype)
    dep_spec = pl.BlockSpec((8, LANES), lambda i, j, kk: (0, 0))
    in_specs = [a_spec, b_spec] + ([pl.BlockSpec((tm, tn), lambda i, j, kk: (i, j))] if has_add else [])
    in_specs += [dep_spec] * len(deps)
    args = (a, b) + ((add,) if has_add else ()) + tuple(deps)
    return pl.pallas_call(
        body, name=name, grid=(m // tm, n // tn, nk), in_specs=in_specs, out_specs=o_spec, out_shape=out_shape,
        scratch_shapes=[pltpu.VMEM((tm, tn), F32)] if nk > 1 else [],
        compiler_params=_cp(("parallel", "parallel", "arbitrary")))(*args)


def _rmsnorm_fwd(x, g, name):
    t, d = x.shape
    tb = _rows(t, 256)

    def body(x_ref, g_ref, o_ref):
        xv = x_ref[...]
        r = lax.rsqrt(jnp.mean(xv * xv, axis=-1, keepdims=True) + EPS)
        o_ref[...] = (xv * r * g_ref[...]).astype(BF16)

    return pl.pallas_call(
        body, name=name, grid=(t // tb,),
        in_specs=[pl.BlockSpec((tb, d), lambda i: (i, 0)), pl.BlockSpec((1, d), lambda i: (0, 0))],
        out_specs=pl.BlockSpec((tb, d), lambda i: (i, 0)), out_shape=jax.ShapeDtypeStruct((t, d), BF16),
        compiler_params=_cp(("parallel",)))(x, g)


def _rmsnorm_bwd(x, g, dy, res, name):
    t, d = x.shape
    tb = _rows(t, 256)

    def body(x_ref, g_ref, dy_ref, res_ref, dx_ref, dg_ref):
        i = pl.program_id(0)
        xv = x_ref[...]
        dyv = dy_ref[...].astype(F32)
        r = lax.rsqrt(jnp.mean(xv * xv, axis=-1, keepdims=True) + EPS)
        u = dyv * g_ref[...]
        dx = r * u - xv * (r * r * r * jnp.mean(u * xv, axis=-1, keepdims=True))
        dx_ref[...] = dx + res_ref[...]
        part = jnp.sum(dyv * xv * r, axis=0, keepdims=True)

        @pl.when(i == 0)
        def _():
            dg_ref[...] = part

        @pl.when(i > 0)
        def _():
            dg_ref[...] += part

    row = pl.BlockSpec((tb, d), lambda i: (i, 0))
    vec = pl.BlockSpec((1, d), lambda i: (0, 0))
    return pl.pallas_call(
        body, name=name, grid=(t // tb,), in_specs=[row, vec, row, row], out_specs=[row, vec],
        out_shape=[jax.ShapeDtypeStruct((t, d), F32), jax.ShapeDtypeStruct((1, d), F32)],
        compiler_params=_cp(("arbitrary",)))(x, g, dy, res)


def _final_loss(h, g, tgt):
    t, d = h.shape
    tb = _rows(t, 256)

    def body(h_ref, g_ref, t_ref, loss_ref, dh_ref, dg_ref):
        i = pl.program_id(0)
        hv = h_ref[...]
        gv = g_ref[...]
        r = lax.rsqrt(jnp.mean(hv * hv, axis=-1, keepdims=True) + EPS)
        y = hv * r * gv
        diff = y - t_ref[...]
        lpart = jnp.sum(jnp.sum(diff * diff, axis=1, keepdims=True), axis=0, keepdims=True) * (0.5 / d)
        dy = diff * (1.0 / d)
        u = dy * gv
        dh_ref[...] = r * u - hv * (r * r * r * jnp.mean(u * hv, axis=-1, keepdims=True))
        gpart = jnp.sum(dy * hv * r, axis=0, keepdims=True)
        lrow = jnp.broadcast_to(lpart, (1, LANES))

        @pl.when(i == 0)
        def _():
            loss_ref[...] = lrow
            dg_ref[...] = gpart

        @pl.when(i > 0)
        def _():
            loss_ref[...] += lrow
            dg_ref[...] += gpart

    row = pl.BlockSpec((tb, d), lambda i: (i, 0))
    vec = pl.BlockSpec((1, d), lambda i: (0, 0))
    return pl.pallas_call(
        body, name="final_loss", grid=(t // tb,), in_specs=[row, vec, row],
        out_specs=[pl.BlockSpec((1, LANES), lambda i: (0, 0)), row, vec],
        out_shape=[jax.ShapeDtypeStruct((1, LANES), F32), jax.ShapeDtypeStruct((t, d), F32),
                   jax.ShapeDtypeStruct((1, d), F32)],
        compiler_params=_cp(("arbitrary",)))(h, g, tgt)


def _rope_tables(t):
    pos = jnp.arange(t, dtype=F32)
    inv = 1.0 / (ROPE_THETA ** (jnp.arange(0, ROT_DIM, 2, dtype=F32) / ROT_DIM))
    ang = pos[:, None] * inv[None, :]
    cos, sin = jnp.cos(ang), jnp.sin(ang)
    half = ROT_DIM // 2
    rest = HEAD_DIM - ROT_DIM
    c = jnp.concatenate([cos, cos, jnp.ones((t, rest), F32)], axis=1)
    s1 = jnp.concatenate([-sin, jnp.zeros((t, half + rest), F32)], axis=1)
    s2 = jnp.concatenate([jnp.zeros((t, half), F32), sin, jnp.zeros((t, rest), F32)], axis=1)
    return tuple(jnp.tile(v, (1, LANES // HEAD_DIM)) for v in (c, s1, s2))


def _rope(x, c, s1, s2):
    half = ROT_DIM // 2
    return x * c + pltpu.roll(x, LANES - half, 1) * s1 + pltpu.roll(x, half, 1) * s2


def _rope_t(g, c, s1, s2):
    half = ROT_DIM // 2
    return g * c + pltpu.roll(g * s1, half, 1) + pltpu.roll(g * s2, LANES - half, 1)


def _attn_mask(i):
    qi = _iota((ATTN_BLOCK, 2 * ATTN_BLOCK), 0)
    kj = _iota((ATTN_BLOCK, 2 * ATTN_BLOCK), 1)
    rel = qi + ATTN_BLOCK - kj
    first_key = jnp.where(i > 0, 0, ATTN_BLOCK)
    return (rel >= 0) & (rel < WINDOW) & (kj >= first_key)


def _half_masks():
    lane = _iota((1, LANES), 1)
    return [(lane < HEAD_DIM).astype(F32), (lane >= HEAD_DIM).astype(F32)]


def _attn_specs(nb_clamp):
    blk = ATTN_BLOCK
    kb, vb = O_K // LANES, O_V // LANES

    def cur(i):
        return jnp.minimum(i, nb_clamp)

    def prev(i):
        return jnp.maximum(jnp.minimum(i, nb_clamp + 1) - 1, 0)

    q = pl.BlockSpec((blk, 512), lambda p, i: (cur(i), p))
    kc = pl.BlockSpec((blk, LANES), lambda p, i: (cur(i), kb + p))
    kp = pl.BlockSpec((blk, LANES), lambda p, i: (prev(i), kb + p))
    vc = pl.BlockSpec((blk, LANES), lambda p, i: (cur(i), vb + p))
    vp = pl.BlockSpec((blk, LANES), lambda p, i: (prev(i), vb + p))
    tc = pl.BlockSpec((blk, LANES), lambda p, i: (cur(i), 0))
    tp = pl.BlockSpec((blk, LANES), lambda p, i: (prev(i), 0))
    return q, kc, kp, vc, vp, tc, tp


def _attn_fwd(proj, sinks, tables):
    t = proj.shape[0]
    nb = t // ATTN_BLOCK
    scale = HEAD_DIM ** -0.5

    def body(sink_ref, q_ref, kc_ref, kp_ref, vc_ref, vp_ref, cc_ref, s1c_ref, s2c_ref, cp_ref, s1p_ref, s2p_ref,
             o_ref):
        p = pl.program_id(0)
        i = pl.program_id(1)
        cc, s1c, s2c = cc_ref[...], s1c_ref[...], s2c_ref[...]
        kband = jnp.concatenate([_rope(kp_ref[...], cp_ref[...], s1p_ref[...], s2p_ref[...]),
                                 _rope(kc_ref[...], cc, s1c, s2c)], axis=0).astype(BF16)
        vband = jnp.concatenate([vp_ref[...], vc_ref[...]], axis=0)
        hm = _half_masks()
        vsel = [(vband * hm[j]).astype(BF16) for j in range(2)]
        valid = _attn_mask(i)
        for qb in range(4):
            qr = _rope(q_ref[:, qb * LANES:(qb + 1) * LANES], cc, s1c, s2c)
            acc = jnp.zeros((ATTN_BLOCK, LANES), F32)
            for half in range(2):
                hh = qb * 2 + half
                j = hh // 4
                qs = qr * hm[half]
                if half != j:
                    qs = pltpu.roll(qs, HEAD_DIM, 1)
                s = jnp.where(valid, _dot(qs, kband, 'nt') * scale, NEG)
                sink = sink_ref[p * 8 + hh]
                m = jnp.maximum(jnp.max(s, axis=1, keepdims=True), sink)
                pe = jnp.exp(s - m)
                den = jnp.sum(pe, axis=1, keepdims=True) + jnp.exp(sink - m)
                o = _dot(pe / den, vsel[j])
                if half != j:
                    o = pltpu.roll(o, HEAD_DIM, 1)
                acc = acc + o
            o_ref[:, qb * LANES:(qb + 1) * LANES] = acc

    q, kc, kp, vc, vp, tc, tp = _attn_specs(nb - 1)
    smem = pl.BlockSpec(memory_space=pltpu.SMEM)
    return pl.pallas_call(
        body, name="attn_fwd", grid=(4, nb),
        in_specs=[smem, q, kc, kp, vc, vp, tc, tc, tc, tp, tp, tp],
        out_specs=pl.BlockSpec((ATTN_BLOCK, 512), lambda p, i: (i, p)),
        out_shape=jax.ShapeDtypeStruct((t, ATTN_WIDTH), F32),
        compiler_params=_cp(("parallel", "arbitrary")))(sinks, proj, proj, proj, proj, proj, *tables, *tables)


def _attn_bwd(proj, sinks, tables, dout):
    t = proj.shape[0]
    nb = t // ATTN_BLOCK
    scale = HEAD_DIM ** -0.5

    def body(sink_ref, q_ref, kc_ref, kp_ref, vc_ref, vp_ref, cc_ref, s1c_ref, s2c_ref, cp_ref, s1p_ref, s2p_ref,
             do_ref, dq_ref, dk_ref, dv_ref, ds_ref, carry_k, carry_v):
        p = pl.program_id(0)
        i = pl.program_id(1)
        ptab = (cp_ref[...], s1p_ref[...], s2p_ref[...])

        @pl.when(i == 0)
        def _():
            carry_k[...] = jnp.zeros_like(carry_k)
            carry_v[...] = jnp.zeros_like(carry_v)
            ds_ref[...] = jnp.zeros_like(ds_ref)

        @pl.when(i < nb)
        def _():
            cc, s1c, s2c = cc_ref[...], s1c_ref[...], s2c_ref[...]
            kband = jnp.concatenate([_rope(kp_ref[...], *ptab), _rope(kc_ref[...], cc, s1c, s2c)], axis=0)
            vband = jnp.concatenate([vp_ref[...], vc_ref[...]], axis=0)
            hm = _half_masks()
            kband16 = kband.astype(BF16)
            ksel = [(kband * hm[j]).astype(BF16) for j in range(2)]
            vband16 = vband.astype(BF16)
            vsel = [(vband * hm[j]).astype(BF16) for j in range(2)]
            valid = _attn_mask(i)
            dkb = jnp.zeros((2 * ATTN_BLOCK, LANES), F32)
            dvb = jnp.zeros((2 * ATTN_BLOCK, LANES), F32)
            row8 = _iota((8, LANES), 0)
            dsink = jnp.zeros((8, LANES), F32)
            for qb in range(4):
                qr = _rope(q_ref[:, qb * LANES:(qb + 1) * LANES], cc, s1c, s2c)
                dob = do_ref[:, qb * LANES:(qb + 1) * LANES]
                dqb = jnp.zeros((ATTN_BLOCK, LANES), F32)
                for half in range(2):
                    hh = qb * 2 + half
                    j = hh // 4
                    qs = qr * hm[half]
                    dos = dob * hm[half]
                    if half != j:
                        qs = pltpu.roll(qs, HEAD_DIM, 1)
                        dos = pltpu.roll(dos, HEAD_DIM, 1)
                    qs16 = qs.astype(BF16)
                    dos16 = dos.astype(BF16)
                    s = jnp.where(valid, _dot(qs16, kband16, 'nt') * scale, NEG)
                    sink = sink_ref[p * 8 + hh]
                    m = jnp.maximum(jnp.max(s, axis=1, keepdims=True), sink)
                    pe = jnp.exp(s - m)
                    psink = jnp.exp(sink - m)
                    den = jnp.sum(pe, axis=1, keepdims=True) + psink
                    pr = pe / den
                    dvb = dvb + _dot(pr.T, dos16)
                    dp = _dot(dos16, vband16, 'nt')
                    delta = jnp.sum(pr * dp, axis=1, keepdims=True)
                    dsc = pr * (dp - delta) * scale
                    dsink = dsink + jnp.where(row8 == hh, -jnp.sum(psink / den * delta), 0.0)
                    dqh = _dot(dsc, ksel[j])
                    if half != j:
                        dqh = pltpu.roll(dqh, HEAD_DIM, 1)
                    dqb = dqb + dqh
                    dkb = dkb + _dot(dsc.T, qs16)
                dq_ref[:, qb * LANES:(qb + 1) * LANES] = _rope_t(dqb, cc, s1c, s2c).astype(BF16)
            ds_ref[0] += dsink
            dk_ref[...] = _rope_t(carry_k[...] + dkb[:ATTN_BLOCK], *ptab).astype(BF16)
            dv_ref[...] = (carry_v[...] + dvb[:ATTN_BLOCK]).astype(BF16)
            carry_k[...] = dkb[ATTN_BLOCK:]
            carry_v[...] = dvb[ATTN_BLOCK:]

        @pl.when(i == nb)
        def _():
            dk_ref[...] = _rope_t(carry_k[...], *ptab).astype(BF16)
            dv_ref[...] = carry_v[...].astype(BF16)

    q, kc, kp, vc, vp, tc, tp = _attn_specs(nb - 1)
    smem = pl.BlockSpec(memory_space=pltpu.SMEM)
    qblk = pl.BlockSpec((ATTN_BLOCK, 512), lambda p, i: (jnp.minimum(i, nb - 1), p))
    kvout = pl.BlockSpec((ATTN_BLOCK, LANES), lambda p, i: (jnp.maximum(i - 1, 0), p))
    return pl.pallas_call(
        body, name="attn_bwd", grid=(4, nb + 1),
        in_specs=[smem, q, kc, kp, vc, vp, tc, tc, tc, tp, tp, tp, qblk],
        out_specs=[qblk, kvout, kvout, pl.BlockSpec((1, 8, LANES), lambda p, i: (p, 0, 0))],
        out_shape=[jax.ShapeDtypeStruct((t, ATTN_WIDTH), BF16), jax.ShapeDtypeStruct((t, KV_WIDTH), BF16),
                   jax.ShapeDtypeStruct((t, KV_WIDTH), BF16), jax.ShapeDtypeStruct((4, 8, LANES), F32)],
        scratch_shapes=[pltpu.VMEM((ATTN_BLOCK, LANES), F32), pltpu.VMEM((ATTN_BLOCK, LANES), F32)],
        compiler_params=_cp(("parallel", "arbitrary")))(sinks, proj, proj, proj, proj, proj, *tables, *tables, dout)


def _shift_rows(x, prev8, j):
    r = pltpu.roll(x, j, 0)
    head = jnp.where(_iota((8, 1), 0) < j, pltpu.roll(prev8, j, 0), r[:8])
    if x.shape[0] == 8:
        return head
    return jnp.concatenate([head, r[8:]], axis=0)


def _shift_rows_up(x, next8, j):
    n = x.shape[0]
    r = pltpu.roll(x, n - j, 0)
    tail = jnp.where(_iota((8, 1), 0) >= 8 - j, pltpu.roll(next8, 8 - j, 0), r[n - 8:])
    return jnp.concatenate([r[:n - 8], tail], axis=0)


def _conv_apply(x, prev8, w, b, taps):
    u = b + x * w[taps - 1:taps]
    for j in range(1, taps):
        u = u + _shift_rows(x, prev8, j) * w[taps - 1 - j:taps - j]
    return u


def _conv_grads(du, du_next8, x, x_prev8, w, taps):
    dx = du * w[taps - 1:taps]
    rowk = _iota((taps, 1), 0)
    dw = jnp.where(rowk == taps - 1, jnp.sum(du * x, axis=0, keepdims=True), 0.0)
    for j in range(1, taps):
        dx = dx + _shift_rows_up(du, du_next8, j) * w[taps - 1 - j:taps - j]
        part = jnp.sum(du * _shift_rows(x, x_prev8, j), axis=0, keepdims=True)
        dw = dw + jnp.where(rowk == taps - 1 - j, part, 0.0)
    return dx, dw, jnp.sum(du, axis=0, keepdims=True)


def _conv_specs(tb, tc, col0, t):
    c0 = col0 // tc
    cur = pl.BlockSpec((tb, tc), lambda j, i: (i, c0 + j))
    prev = pl.BlockSpec((8, tc), lambda j, i: (jnp.maximum(i * (tb // 8) - 1, 0), c0 + j))
    nxt = pl.BlockSpec((8, tc), lambda j, i: (jnp.minimum((i + 1) * (tb // 8), t // 8 - 1), c0 + j))
    return cur, prev, nxt


def _conv_fwd(x, w, b, *, col0, width, act, name):
    t = x.shape[0]
    taps = w.shape[0]
    tb, tc = _rows(t, 512), _tile(width, 1024)
    assert col0 % tc == 0

    def body(x_ref, xp_ref, w_ref, b_ref, o_ref):
        i = pl.program_id(1)
        prev8 = jnp.where(i > 0, xp_ref[...], 0.0)
        u = _conv_apply(x_ref[...], prev8, w_ref[...], b_ref[...], taps)
        if act:
            u = u * _sigmoid(u)
        o_ref[...] = u

    cur, prev, _ = _conv_specs(tb, tc, col0, t)
    par = pl.BlockSpec((taps, tc), lambda j, i: (0, j))
    bias = pl.BlockSpec((1, tc), lambda j, i: (0, j))
    return pl.pallas_call(
        body, name=name, grid=(width // tc, t // tb), in_specs=[cur, prev, par, bias],
        out_specs=pl.BlockSpec((tb, tc), lambda j, i: (i, j)), out_shape=jax.ShapeDtypeStruct((t, width), F32),
        compiler_params=_cp(("parallel", "parallel")))(x, x, w, b)


def _conv_silu_dact(x, w, b, dout, *, col0, width, name):
    t = x.shape[0]
    taps = w.shape[0]
    tb, tc = _rows(t, 512), _tile(width, 1024)

    def body(x_ref, xp_ref, w_ref, b_ref, d_ref, o_ref):
        i = pl.program_id(1)
        prev8 = jnp.where(i > 0, xp_ref[...], 0.0)
        u = _conv_apply(x_ref[...], prev8, w_ref[...], b_ref[...], taps)
        sg = _sigmoid(u)
        o_ref[...] = d_ref[...] * (sg * (1.0 + u * (1.0 - sg)))

    cur, prev, _ = _conv_specs(tb, tc, col0, t)
    par = pl.BlockSpec((taps, tc), lambda j, i: (0, j))
    bias = pl.BlockSpec((1, tc), lambda j, i: (0, j))
    out = pl.BlockSpec((tb, tc), lambda j, i: (i, j))
    return pl.pallas_call(
        body, name=name, grid=(width // tc, t // tb), in_specs=[cur, prev, par, bias, out],
        out_specs=out, out_shape=jax.ShapeDtypeStruct((t, width), F32),
        compiler_params=_cp(("parallel", "parallel")))(x, x, w, b, dout)


def _conv_bwd(x, w, du, *, col0, width, name):
    t = x.shape[0]
    taps = w.shape[0]
    tb, tc = _rows(t, 512), _tile(width, 1024)
    nrow = t // tb

    def body(x_ref, xp_ref, w_ref, du_ref, dun_ref, dx_ref, dw_ref, db_ref):
        i = pl.program_id(1)
        xv = x_ref[...]
        prev8 = jnp.where(i > 0, xp_ref[...], 0.0)
        next8 = jnp.where(i < nrow - 1, dun_ref[...], 0.0)
        dx, dwv, dbv = _conv_grads(du_ref[...], next8, xv, prev8, w_ref[...], taps)
        dx_ref[...] = dx.astype(BF16)

        @pl.when(i == 0)
        def _():
            dw_ref[...] = dwv
            db_ref[...] = dbv

        @pl.when(i > 0)
        def _():
            dw_ref[...] += dwv
            db_ref[...] += dbv

    cur, prev, _ = _conv_specs(tb, tc, col0, t)
    dcur, _, dnxt = _conv_specs(tb, tc, 0, t)
    par = pl.BlockSpec((taps, tc), lambda j, i: (0, j))
    bias = pl.BlockSpec((1, tc), lambda j, i: (0, j))
    return pl.pallas_call(
        body, name=name, grid=(width // tc, nrow), in_specs=[cur, prev, par, dcur, dnxt],
        out_specs=[dcur, par, bias],
        out_shape=[jax.ShapeDtypeStruct((t, width), BF16), jax.ShapeDtypeStruct((taps, width), F32),
                   jax.ShapeDtypeStruct((1, width), F32)],
        compiler_params=_cp(("parallel", "arbitrary")))(x, x, w, du, du)


def _ffn_specs(tb, tc, t):
    nc = D_FF // tc

    def cur(half):
        return pl.BlockSpec((tb, tc), lambda j, i: (i, half * nc + j))

    def prev(half):
        return pl.BlockSpec((8, tc), lambda j, i: (jnp.maximum(i * (tb // 8) - 1, 0), half * nc + j))

    def nxt(half):
        return pl.BlockSpec((8, tc), lambda j, i: (jnp.minimum((i + 1) * (tb // 8), t // 8 - 1), half * nc + j))

    def par(rows, half):
        return pl.BlockSpec((rows, tc), lambda j, i: (0, half * nc + j))

    return cur, prev, nxt, par


def _ffn_act_fwd(u0, w, b):
    t = u0.shape[0]
    tb, tc = _rows(t, 512), _tile(D_FF, 1408)
    cur, prev, _, par = _ffn_specs(tb, tc, t)

    def body(g_ref, gp_ref, v_ref, vp_ref, wg_ref, wv_ref, bg_ref, bv_ref, o_ref):
        i = pl.program_id(1)
        ug = _conv_apply(g_ref[...], jnp.where(i > 0, gp_ref[...], 0.0), wg_ref[...], bg_ref[...], FFN_CONV)
        uv = _conv_apply(v_ref[...], jnp.where(i > 0, vp_ref[...], 0.0), wv_ref[...], bv_ref[...], FFN_CONV)
        o_ref[...] = (ug * _sigmoid(ug) * uv).astype(BF16)

    return pl.pallas_call(
        body, name="ffn_act_fwd", grid=(D_FF // tc, t // tb),
        in_specs=[cur(0), prev(0), cur(1), prev(1), par(FFN_CONV, 0), par(FFN_CONV, 1), par(1, 0), par(1, 1)],
        out_specs=pl.BlockSpec((tb, tc), lambda j, i: (i, j)), out_shape=jax.ShapeDtypeStruct((t, D_FF), BF16),
        compiler_params=_cp(("parallel", "parallel")))(u0, u0, u0, u0, w, w, b, b)


def _ffn_act_bwd(u0, w, b, da):
    t = u0.shape[0]
    tb, tc = _rows(t, 256), _tile(D_FF, 1408)
    nrow = t // tb
    taps = FFN_CONV
    cur, prev, nxt, par = _ffn_specs(tb, tc, t)

    def dact(ug, uv, dav):
        sg = _sigmoid(ug)
        return dav * uv * (sg * (1.0 + ug * (1.0 - sg))), dav * ug * sg

    def body(g_ref, gp_ref, gn_ref, v_ref, vp_ref, vn_ref, wg_ref, wv_ref, bg_ref, bv_ref, da_ref, dan_ref,
             dx_ref, dw_ref, db_ref):
        i = pl.program_id(1)
        xg, xv = g_ref[...], v_ref[...]
        gp = jnp.where(i > 0, gp_ref[...], 0.0)
        vp = jnp.where(i > 0, vp_ref[...], 0.0)
        wg, wv, bg, bv = wg_ref[...], wv_ref[...], bg_ref[...], bv_ref[...]
        dug, duv = dact(_conv_apply(xg, gp, wg, bg, taps), _conv_apply(xv, vp, wv, bv, taps),
                        da_ref[...].astype(F32))
        dan = jnp.where(i < nrow - 1, dan_ref[...].astype(F32)[:8], 0.0)
        dugn, duvn = dact(_conv_apply(gn_ref[...], xg[tb - 8:], wg, bg, taps),
                          _conv_apply(vn_ref[...], xv[tb - 8:], wv, bv, taps), dan)
        dxg, dwg, dbg = _conv_grads(dug, dugn, xg, gp, wg, taps)
        dxv, dwv, dbv = _conv_grads(duv, duvn, xv, vp, wv, taps)
        dx_ref[0] = dxg.astype(BF16)
        dx_ref[1] = dxv.astype(BF16)

        @pl.when(i == 0)
        def _():
            dw_ref[0] = dwg
            dw_ref[1] = dwv
            db_ref[0] = dbg
            db_ref[1] = dbv

        @pl.when(i > 0)
        def _():
            dw_ref[0] += dwg
            dw_ref[1] += dwv
            db_ref[0] += dbg
            db_ref[1] += dbv

    da_cur = pl.BlockSpec((tb, tc), lambda j, i: (i, j))
    da_nxt = pl.BlockSpec((16, tc), lambda j, i: (jnp.minimum((i + 1) * (tb // 16), t // 16 - 1), j))
    return pl.pallas_call(
        body, name="ffn_act_bwd", grid=(D_FF // tc, nrow),
        in_specs=[cur(0), prev(0), nxt(0), cur(1), prev(1), nxt(1), par(taps, 0), par(taps, 1), par(1, 0),
                  par(1, 1), da_cur, da_nxt],
        out_specs=[pl.BlockSpec((2, tb, tc), lambda j, i: (0, i, j)),
                   pl.BlockSpec((2, taps, tc), lambda j, i: (0, 0, j)),
                   pl.BlockSpec((2, 1, tc), lambda j, i: (0, 0, j))],
        out_shape=[jax.ShapeDtypeStruct((2, t, D_FF), BF16), jax.ShapeDtypeStruct((2, taps, D_FF), F32),
                   jax.ShapeDtypeStruct((2, 1, D_FF), F32)],
        compiler_params=_cp(("parallel", "arbitrary")))(u0, u0, u0, u0, u0, u0, w, w, b, b, da, da)


def _head_masks():
    lane = _iota((1, 4 * SSD_HEAD_DIM), 1)
    return [((lane >= r * SSD_HEAD_DIM) & (lane < (r + 1) * SSD_HEAD_DIM)).astype(F32) for r in range(4)]


def _expand4(v4, masks):
    lane4 = _iota((1, 4), 1)
    out = 0.0
    for r in range(4):
        out = out + jnp.sum(jnp.where(lane4 == r, v4, 0.0), axis=1, keepdims=True) * masks[r]
    return out


def _collapse4(ve, masks):
    lane4 = _iota((1, 4), 1)
    out = 0.0
    for r in range(4):
        col = jnp.sum(ve * masks[r], axis=1, keepdims=True) * (1.0 / SSD_HEAD_DIM)
        out = out + jnp.where(lane4 == r, col, 0.0)
    return out


def _segsum(v, masks):
    out = 0.0
    for r in range(4):
        out = out + jnp.sum(v * masks[r], axis=1, keepdims=True) * masks[r]
    return out


def _ssd_common(raw4, prow, rawr4, bcol, acol, masks):
    n = SSD_CHUNK
    dt4 = _softplus(raw4 + prow[0:1, :])
    a4 = -jnp.exp(prow[1:2, :])
    dt_e = _expand4(dt4, masks)
    a_e = _expand4(a4, masks)
    d_e = _expand4(prow[2:3, :], masks)
    tril = (_iota((n, n), 0) >= _iota((n, n), 1)).astype(F32)
    acs_e = _dot_exact(tril, dt_e * a_e)
    last_e = acs_e[n - 1:n, :]
    dtr4 = _softplus(rawr4 + bcol)
    triu = (_iota((n, n), 0) <= _iota((n, n), 1)).astype(F32)
    acs_r4 = _dot_exact(dtr4 * (-jnp.exp(acol)), triu)
    return dt_e, a_e, d_e, acs_e, last_e, acs_r4


def _decay_matrix(acs_e, acs_r4, r, masks):
    n = SSD_CHUNK
    col = jnp.sum(acs_e * masks[r], axis=1, keepdims=True) * (1.0 / SSD_HEAD_DIM)
    seg = col - acs_r4[r:r + 1, :]
    causal = _iota((n, n), 0) >= _iota((n, n), 1)
    return jnp.exp(jnp.where(causal, seg, NEG))


def _ssd_specs(t, rev):
    nc = t // SSD_CHUNK
    xb, bb, cb = 0, SSD_INNER // SSD_STATE, (SSD_INNER + BC_WIDTH) // SSD_STATE

    def ch(c):
        return (nc - 1 - c) if rev else c

    x = pl.BlockSpec((SSD_CHUNK, 256), lambda g, c: (ch(c), xb + g))
    bm = pl.BlockSpec((SSD_CHUNK, SSD_STATE), lambda g, c: (ch(c), bb + g))
    cm = pl.BlockSpec((SSD_CHUNK, SSD_STATE), lambda g, c: (ch(c), cb + g))
    dtc = pl.BlockSpec((1, SSD_CHUNK, 4), lambda g, c: (g, ch(c), 0))
    dtr = pl.BlockSpec((1, 4, SSD_CHUNK), lambda g, c: (g, 0, ch(c)))
    prow = pl.BlockSpec((1, 3, 4), lambda g, c: (g, 0, 0))
    pcol = pl.BlockSpec((1, 4, 1), lambda g, c: (g, 0, 0))
    st = pl.BlockSpec((1, 1, SSD_STATE, 256), lambda g, c: (g, ch(c), 0, 0))
    return x, bm, cm, dtc, dtr, prow, pcol, st, ch


def _ssd_params(dt_raw, dt_bias, a_log, ssd_d):
    t = dt_raw.shape[0]
    dtc = dt_raw.reshape(t, SSD_GROUPS, 4).transpose(1, 0, 2)
    dtr = dt_raw.reshape(t, SSD_GROUPS, 4).transpose(1, 2, 0)
    prow = jnp.stack([dt_bias.reshape(SSD_GROUPS, 4), a_log.reshape(SSD_GROUPS, 4),
                      ssd_d.reshape(SSD_GROUPS, 4)], axis=1)
    bcol = dt_bias.reshape(SSD_GROUPS, 4, 1)
    acol = a_log.reshape(SSD_GROUPS, 4, 1)
    return dtc, dtr, prow, bcol, acol


def _ssd_fwd(xbc, params):
    t = xbc.shape[0]
    nc = t // SSD_CHUNK
    dtc, dtr, prow, bcol, acol = params

    def body(x_ref, b_ref, c_ref, dtc_ref, dtr_ref, prow_ref, bcol_ref, acol_ref, y_ref, st_ref, s_scr):
        c = pl.program_id(1)

        @pl.when(c == 0)
        def _():
            s_scr[...] = jnp.zeros_like(s_scr)

        masks = _head_masks()
        dt_e, a_e, d_e, acs_e, last_e, acs_r4 = _ssd_common(
            dtc_ref[0], prow_ref[0], dtr_ref[0], bcol_ref[0], acol_ref[0], masks)
        xv = x_ref[...]
        bm, cm = b_ref[...], c_ref[...]
        s = s_scr[...]
        st_ref[0, 0] = s
        xdt = xv * dt_e
        cb = _dot(cm, bm, 'nt')
        y = _dot(cm, s) * jnp.exp(acs_e) + xv * d_e
        for r in range(4):
            mr = cb * _decay_matrix(acs_e, acs_r4, r, masks)
            y = y + _dot(mr, xdt * masks[r])
        y_ref[...] = y
        w = xdt * jnp.exp(last_e - acs_e)
        s_scr[...] = s * jnp.exp(last_e) + _dot(bm.T, w)

    x, bm, cm, dtcs, dtrs, prs, pcs, st, _ = _ssd_specs(t, False)
    return pl.pallas_call(
        body, name="ssd_fwd", grid=(SSD_GROUPS, nc), in_specs=[x, bm, cm, dtcs, dtrs, prs, pcs, pcs],
        out_specs=[pl.BlockSpec((SSD_CHUNK, 256), lambda g, c: (c, g)), st],
        out_shape=[jax.ShapeDtypeStruct((t, SSD_INNER), F32),
                   jax.ShapeDtypeStruct((SSD_GROUPS, nc, SSD_STATE, 256), F32)],
        scratch_shapes=[pltpu.VMEM((SSD_STATE, 256), F32)],
        compiler_params=_cp(("parallel", "arbitrary")))(xbc, xbc, xbc, dtc, dtr, prow, bcol, acol)


def _ssd_bwd(xbc, params, states, dy):
    t = xbc.shape[0]
    nc = t // SSD_CHUNK
    n = SSD_CHUNK
    dtc, dtr, prow, bcol, acol = params

    def body(x_ref, b_ref, c_ref, dtc_ref, dtr_ref, prow_ref, bcol_ref, acol_ref, st_ref, dy_ref,
             dx_ref, db_ref, dc_ref, ddt_ref, dp_ref, ds_scr):
        c = pl.program_id(1)

        @pl.when(c == 0)
        def _():
            ds_scr[...] = jnp.zeros_like(ds_scr)
            dp_ref[...] = jnp.zeros_like(dp_ref)

        masks = _head_masks()
        raw4 = dtc_ref[0]
        prw = prow_ref[0]
        dt_e, a_e, d_e, acs_e, last_e, acs_r4 = _ssd_common(raw4, prw, dtr_ref[0], bcol_ref[0], acol_ref[0], masks)
        xv = x_ref[...]
        bm, cm = b_ref[...], c_ref[...]
        s = st_ref[0, 0]
        ds = ds_scr[...]
        dyv = dy_ref[...]
        e_e = jnp.exp(acs_e)
        dec_e = jnp.exp(last_e - acs_e)
        cd_e = jnp.exp(last_e)
        xdt = xv * dt_e
        w = xdt * dec_e
        b16, c16, s16, ds16 = bm.astype(BF16), cm.astype(BF16), s.astype(BF16), ds.astype(BF16)
        cb = _dot(c16, b16, 'nt')
        yoff_raw = _dot(c16, s16)
        dye = dyv * e_e
        dye16 = dye.astype(BF16)
        dcm = _dot(dye16, s16, 'nt')
        ds_scr[...] = ds * cd_e + _dot(cm.T, dye16)
        dacs_e = _segsum(dyv * yoff_raw, masks) * e_e
        dw = _dot(b16, ds16)
        dbm = _dot(w, ds16, 'nt')
        tdec = _segsum(dw * xdt, masks) * dec_e
        dacs_e = dacs_e - tdec
        dlast_e = jnp.sum(tdec, axis=0, keepdims=True)
        dxdt = dw * dec_e
        dlast_e = dlast_e + _segsum(jnp.sum(ds * s, axis=0, keepdims=True), masks) * cd_e
        dcb = jnp.zeros((n, n), F32)
        for r in range(4):
            lm = _decay_matrix(acs_e, acs_r4, r, masks)
            mr = cb * lm
            dyr16 = (dyv * masks[r]).astype(BF16)
            dm = _dot(dyr16, xdt * masks[r], 'nt')
            dcb = dcb + dm * lm
            dseg = dm * mr
            dcol = jnp.sum(dseg, axis=1, keepdims=True) - jnp.sum(dseg.T, axis=1, keepdims=True)
            dacs_e = dacs_e + dcol * masks[r]
            dxdt = dxdt + _dot(mr.T, dyr16)
        dcm = dcm + _dot(dcb, b16)
        dbm = dbm + _dot(dcb.T, c16)
        dacs_e = dacs_e + jnp.where(_iota((n, 1), 0) == n - 1, dlast_e, 0.0)
        triu = (_iota((n, n), 0) <= _iota((n, n), 1)).astype(F32)
        ddta_e = _dot_exact(triu, dacs_e)
        ddt_e = ddta_e * a_e + _segsum(dxdt * xv, masks)
        dx_ref[...] = dxdt * dt_e + dyv * d_e
        db_ref[...] = dbm
        dc_ref[...] = dcm
        draw_e = ddt_e * _sigmoid(_expand4(raw4 + prw[0:1, :], masks))
        ddt_ref[0] = _collapse4(draw_e, masks)
        dbias = _collapse4(jnp.sum(draw_e, axis=0, keepdims=True), masks)
        dalog = _collapse4(jnp.sum(ddta_e * dt_e, axis=0, keepdims=True) * a_e, masks)
        dd = _collapse4(jnp.sum(_segsum(dyv * xv, masks), axis=0, keepdims=True), masks)
        row3 = _iota((3, 1), 0)
        dp_ref[0] += (jnp.where(row3 == 0, dbias, 0.0) + jnp.where(row3 == 1, dalog, 0.0)
                      + jnp.where(row3 == 2, dd, 0.0))

    x, bm, cm, dtcs, dtrs, prs, pcs, st, ch = _ssd_specs(t, True)
    yblk = pl.BlockSpec((SSD_CHUNK, 256), lambda g, c: (ch(c), g))
    nblk = pl.BlockSpec((SSD_CHUNK, SSD_STATE), lambda g, c: (ch(c), g))
    return pl.pallas_call(
        body, name="ssd_bwd", grid=(SSD_GROUPS, nc),
        in_specs=[x, bm, cm, dtcs, dtrs, prs, pcs, pcs, st, yblk],
        out_specs=[yblk, nblk, nblk, dtcs, prs],
        out_shape=[jax.ShapeDtypeStruct((t, SSD_INNER), F32), jax.ShapeDtypeStruct((t, BC_WIDTH), F32),
                   jax.ShapeDtypeStruct((t, BC_WIDTH), F32), jax.ShapeDtypeStruct((SSD_GROUPS, t, 4), F32),
                   jax.ShapeDtypeStruct((SSD_GROUPS, 3, 4), F32)],
        scratch_shapes=[pltpu.VMEM((SSD_STATE, 256), F32)],
        compiler_params=_cp(("parallel", "arbitrary")))(xbc, xbc, xbc, dtc, dtr, prow, bcol, acol, states, dy)


GROUP_W = SSD_INNER // SSD_GROUPS


def _mix_specs(tb):
    row = pl.BlockSpec((tb, 2048), lambda i: (i, 0))
    zlo = pl.BlockSpec((tb, 1024), lambda i: (i, O_Z // 1024))
    zhi = pl.BlockSpec((tb, 1024), lambda i: (i, O_Z // 1024 + 1))
    vec = pl.BlockSpec((1, 2048), lambda i: (0, 0))
    return row, zlo, zhi, vec


def _mix_fwd(attn, y, proj, g_attn, g_ssd):
    t = attn.shape[0]
    tb = _rows(t, 256)

    def body(a_ref, y_ref, zlo_ref, zhi_ref, ga_ref, gs_ref, o_ref):
        av = a_ref[...]
        r = lax.rsqrt(jnp.mean(av * av, axis=-1, keepdims=True) + EPS)
        o_ref[:, :ATTN_WIDTH] = (av * r * ga_ref[...]).astype(BF16)
        for g in range(SSD_GROUPS):
            lo, hi = g * GROUP_W, (g + 1) * GROUP_W
            zref = zlo_ref if g < 4 else zhi_ref
            z = zref[:, lo % 1024:lo % 1024 + GROUP_W]
            yg = y_ref[:, lo:hi] * (z * _sigmoid(z))
            rg = lax.rsqrt(jnp.mean(yg * yg, axis=-1, keepdims=True) + EPS)
            o_ref[:, ATTN_WIDTH + lo:ATTN_WIDTH + hi] = (yg * rg * gs_ref[:, lo:hi]).astype(BF16)

    row, zlo, zhi, vec = _mix_specs(tb)
    return pl.pallas_call(
        body, name="mix_fwd", grid=(t // tb,), in_specs=[row, row, zlo, zhi, vec, vec],
        out_specs=pl.BlockSpec((tb, 4096), lambda i: (i, 0)), out_shape=jax.ShapeDtypeStruct((t, 4096), BF16),
        compiler_params=_cp(("parallel",)))(attn, y, proj, proj, g_attn, g_ssd)


def _mix_bwd(dmix, attn, y, proj, g_attn, g_ssd):
    t = attn.shape[0]
    tb = _rows(t, 256)

    def body(dm_ref, a_ref, y_ref, zlo_ref, zhi_ref, ga_ref, gs_ref, da_ref, dy_ref, dz_ref, dga_ref, dgs_ref):
        i = pl.program_id(0)
        av = a_ref[...]
        dn = dm_ref[:, :ATTN_WIDTH].astype(F32)
        r = lax.rsqrt(jnp.mean(av * av, axis=-1, keepdims=True) + EPS)
        u = dn * ga_ref[...]
        da_ref[...] = r * u - av * (r * r * r * jnp.mean(u * av, axis=-1, keepdims=True))
        dga = jnp.sum(dn * av * r, axis=0, keepdims=True)

        @pl.when(i == 0)
        def _():
            dga_ref[...] = dga

        @pl.when(i > 0)
        def _():
            dga_ref[...] += dga

        for g in range(SSD_GROUPS):
            lo, hi = g * GROUP_W, (g + 1) * GROUP_W
            zref = zlo_ref if g < 4 else zhi_ref
            z = zref[:, lo % 1024:lo % 1024 + GROUP_W]
            yv = y_ref[:, lo:hi]
            sg = _sigmoid(z)
            sz = z * sg
            yg = yv * sz
            rg = lax.rsqrt(jnp.mean(yg * yg, axis=-1, keepdims=True) + EPS)
            do = dm_ref[:, ATTN_WIDTH + lo:ATTN_WIDTH + hi].astype(F32)
            ug = do * gs_ref[:, lo:hi]
            dyg = rg * ug - yg * (rg * rg * rg * jnp.mean(ug * yg, axis=-1, keepdims=True))
            dy_ref[:, lo:hi] = dyg * sz
            dz_ref[:, lo:hi] = (dyg * yv * (sg * (1.0 + z * (1.0 - sg)))).astype(BF16)
            dgs = jnp.sum(do * yg * rg, axis=0, keepdims=True)

            @pl.when(i == 0)
            def _():
                dgs_ref[:, lo:hi] = dgs

            @pl.when(i > 0)
            def _():
                dgs_ref[:, lo:hi] += dgs

    row, zlo, zhi, vec = _mix_specs(tb)
    return pl.pallas_call(
        body, name="mix_bwd", grid=(t // tb,),
        in_specs=[pl.BlockSpec((tb, 4096), lambda i: (i, 0)), row, row, zlo, zhi, vec, vec],
        out_specs=[row, row, row, vec, vec],
        out_shape=[jax.ShapeDtypeStruct((t, 2048), F32), jax.ShapeDtypeStruct((t, 2048), F32),
                   jax.ShapeDtypeStruct((t, 2048), BF16), jax.ShapeDtypeStruct((1, 2048), F32),
                   jax.ShapeDtypeStruct((1, 2048), F32)],
        compiler_params=_cp(("arbitrary",)))(dmix, attn, y, proj, proj, g_attn, g_ssd)


def _adamw(w, g, m, v, name):
    r, c = w.shape
    tb = _rows(r, 256)
    c1 = 1.0 - ADAM_B1 ** ADAM_STEP
    c2 = 1.0 - ADAM_B2 ** ADAM_STEP

    def body(w_ref, g_ref, m_ref, v_ref, d_ref, m2_ref, v2_ref):
        gv = g_ref[...]
        m2 = ADAM_B1 * m_ref[...] + (1.0 - ADAM_B1) * gv
        v2 = ADAM_B2 * v_ref[...] + (1.0 - ADAM_B2) * (gv * gv)
        d_ref[...] = -ADAM_LR * ((m2 / c1) / (jnp.sqrt(v2 / c2) + ADAM_EPS) + ADAM_WD * w_ref[...])
        m2_ref[...] = m2
        v2_ref[...] = v2

    blk = pl.BlockSpec((tb, c), lambda i: (i, 0))
    shp = jax.ShapeDtypeStruct((r, c), F32)
    return pl.pallas_call(body, name=name, grid=(r // tb,), in_specs=[blk] * 4, out_specs=[blk] * 3,
                          out_shape=[shp] * 3, compiler_params=_cp(("parallel",)))(w, g, m, v)


def _adamw_halves(w, mine, theirs, m, v, pos, name):
    r, c = w.shape
    h = r // 2
    tb = _rows(h, 128)
    nh = h // tb
    c1 = 1.0 - ADAM_B1 ** ADAM_STEP
    c2 = 1.0 - ADAM_B2 ** ADAM_STEP

    def body(pos_ref, w_ref, a_ref, b_ref, m_ref, v_ref, g_ref, d_ref, m2_ref, v2_ref):
        own_rows = (pl.program_id(0) // nh) == pos_ref[0]
        gv = jnp.where(own_rows, a_ref[...], b_ref[...])
        m2 = ADAM_B1 * m_ref[...] + (1.0 - ADAM_B1) * gv
        v2 = ADAM_B2 * v_ref[...] + (1.0 - ADAM_B2) * (gv * gv)
        g_ref[...] = gv
        d_ref[...] = -ADAM_LR * ((m2 / c1) / (jnp.sqrt(v2 / c2) + ADAM_EPS) + ADAM_WD * w_ref[...])
        m2_ref[...] = m2
        v2_ref[...] = v2

    full = pl.BlockSpec((tb, c), lambda i, pref: (i, 0))
    half = pl.BlockSpec((tb, c), lambda i, pref: (i % nh, 0))
    shp = jax.ShapeDtypeStruct((r, c), F32)
    grid_spec = pltpu.PrefetchScalarGridSpec(num_scalar_prefetch=1, grid=(r // tb,),
                                             in_specs=[full, half, half, full, full], out_specs=[full] * 4)
    return pl.pallas_call(body, name=name, grid_spec=grid_spec, out_shape=[shp] * 4,
                          compiler_params=_cp(("parallel",)))(pos, w, mine, theirs, m, v)


def _sum_own_half(g4, recv, pos, name):
    _, r, c = g4.shape
    h = r // 2
    tb = _rows(h, 128)
    nh = h // tb

    def body(pos_ref, a_ref, b_ref, o_ref):
        o_ref[...] = (a_ref[...] + b_ref[...]).astype(BF16)

    grid_spec = pltpu.PrefetchScalarGridSpec(
        num_scalar_prefetch=1, grid=(N_CHIPS, nh),
        in_specs=[pl.BlockSpec((1, tb, c), lambda j, i, pref: (j, pref[0] * nh + i, 0)),
                  pl.BlockSpec((1, tb, c), lambda j, i, pref: (j, i, 0))],
        out_specs=pl.BlockSpec((1, tb, c), lambda j, i, pref: (j, i, 0)))
    return pl.pallas_call(body, name=name, grid_spec=grid_spec,
                          out_shape=jax.ShapeDtypeStruct((N_CHIPS, h, c), BF16),
                          compiler_params=_cp(("parallel", "parallel")))(pos, g4, recv)


def _sum_chips(g4, recv, parts, pos, name):
    _, r, c = g4.shape
    h = r // 2
    tb = _rows(h, 128)
    nh = h // tb

    def body(pos_ref, a_ref, b_ref, p_ref, o_ref):
        own = a_ref[0] + b_ref[0]
        o_ref[...] = ((own + p_ref[0].astype(F32)) + p_ref[1].astype(F32)) + p_ref[2].astype(F32)

    grid_spec = pltpu.PrefetchScalarGridSpec(
        num_scalar_prefetch=1, grid=(nh,),
        in_specs=[pl.BlockSpec((1, tb, c), lambda i, pref: (pref[1], pref[0] * nh + i, 0)),
                  pl.BlockSpec((1, tb, c), lambda i, pref: (pref[1], i, 0)),
                  pl.BlockSpec((3, tb, c), lambda i, pref: (0, i, 0))],
        out_specs=pl.BlockSpec((tb, c), lambda i, pref: (i, 0)))
    return pl.pallas_call(body, name=name, grid_spec=grid_spec, out_shape=jax.ShapeDtypeStruct((h, c), F32),
                          compiler_params=_cp(("parallel",)))(pos, g4, recv, parts)


def _me():
    return lax.axis_index("x"), lax.axis_index("y"), lax.axis_index("c")


def _flip(v, bit):
    return (1 - v) if bit else v


CHIP_FLIPS = [(1, 0), (0, 1), (1, 1)]


def _allgather_weights(shards):
    n = len(shards)

    def body(*refs):
        ins, outs = refs[:n], refs[n:2 * n]
        send_sems, recv_sems = refs[2 * n:]
        x, y, c = _me()
        chip = 2 * x + y
        sib = (x, y, 1 - c)

        def remote(src, dst, k, to):
            return pltpu.make_async_remote_copy(src_ref=src, dst_ref=dst, send_sem=send_sems.at[k],
                                                recv_sem=recv_sems.at[k], device_id=to, device_id_type=MESH)

        sends = []
        for t in range(n):
            h = ins[t].shape[0] // 2
            mine = pl.ds(c * h, h)
            for k, (fx, fy) in enumerate(CHIP_FLIPS):
                cp = remote(ins[t].at[mine], outs[t].at[chip, mine], 6 * t + k, (_flip(x, fx), _flip(y, fy), c))
                cp.start()
                sends.append(cp)
        for t in range(n):
            h = ins[t].shape[0] // 2
            mine = pl.ds(c * h, h)
            for k, (fx, fy) in enumerate(CHIP_FLIPS):
                src_chip = 2 * _flip(x, fx) + _flip(y, fy)
                landed = outs[t].at[src_chip, mine]
                remote(landed, landed, 6 * t + k, (x, y, c)).wait_recv()
                fw = remote(landed, landed, 6 * t + 3 + k, sib)
                fw.start()
                sends.append(fw)
        for t in range(n):
            h = ins[t].shape[0] // 2
            other = pl.ds((1 - c) * h, h)
            for k, (fx, fy) in enumerate(CHIP_FLIPS):
                src_chip = 2 * _flip(x, fx) + _flip(y, fy)
                got = outs[t].at[src_chip, other]
                remote(got, got, 6 * t + 3 + k, (x, y, c)).wait_recv()
        for cp in sends:
            cp.wait_send()

    return pl.pallas_call(
        body, name="allgather_weights", in_specs=[HBM_SPEC] * n, out_specs=[HBM_SPEC] * n,
        out_shape=[jax.ShapeDtypeStruct((N_CHIPS,) + s.shape, s.dtype) for s in shards],
        scratch_shapes=[pltpu.SemaphoreType.DMA((6 * n,)), pltpu.SemaphoreType.DMA((6 * n,))],
        compiler_params=pltpu.CompilerParams(has_side_effects=True))(*shards)


def _exchange_halves(g4s, name):
    n = len(g4s)

    def body(*refs):
        ins, outs = refs[:n], refs[n:2 * n]
        send_sems, recv_sems = refs[2 * n:]
        x, y, c = _me()
        cps = []
        for t in range(n):
            h = ins[t].shape[1] // 2
            cp = pltpu.make_async_remote_copy(
                src_ref=ins[t].at[:, pl.ds((1 - c) * h, h)], dst_ref=outs[t], send_sem=send_sems.at[t],
                recv_sem=recv_sems.at[t], device_id=(x, y, 1 - c), device_id_type=MESH)
            cp.start()
            cps.append(cp)
        for cp in cps:
            cp.wait()

    return pl.pallas_call(
        body, name=name, in_specs=[HBM_SPEC] * n, out_specs=[HBM_SPEC] * n,
        out_shape=[jax.ShapeDtypeStruct((N_CHIPS, g.shape[1] // 2, g.shape[2]), g.dtype) for g in g4s],
        scratch_shapes=[pltpu.SemaphoreType.DMA((n,)), pltpu.SemaphoreType.DMA((n,))],
        compiler_params=pltpu.CompilerParams(has_side_effects=True))(*g4s)


def _share_halves(ghs, name):
    n = len(ghs)

    def body(*refs):
        ins, outs = refs[:n], refs[n:2 * n]
        send_sems, recv_sems = refs[2 * n:]
        x, y, c = _me()
        cps = []
        for t in range(n):
            cp = pltpu.make_async_remote_copy(
                src_ref=ins[t], dst_ref=outs[t], send_sem=send_sems.at[t], recv_sem=recv_sems.at[t],
                device_id=(x, y, 1 - c), device_id_type=MESH)
            cp.start()
            cps.append(cp)
        for cp in cps:
            cp.wait()

    return pl.pallas_call(
        body, name=name, in_specs=[HBM_SPEC] * n, out_specs=[HBM_SPEC] * n,
        out_shape=[jax.ShapeDtypeStruct(g.shape, g.dtype) for g in ghs],
        scratch_shapes=[pltpu.SemaphoreType.DMA((n,)), pltpu.SemaphoreType.DMA((n,))],
        compiler_params=pltpu.CompilerParams(has_side_effects=True))(*ghs)


SEM_SPEC = pl.BlockSpec(memory_space=pltpu.SEMAPHORE)
ANY_SPEC = pl.BlockSpec(memory_space=pl.ANY)
DATAFLOW = pltpu.SideEffectType.DATAFLOW_SIDE_EFFECTING
TOKEN = jax.ShapeDtypeStruct((8, LANES), F32)


def _in_hbm(a):
    return pltpu.with_memory_space_constraint(a, pltpu.HBM)


def _push_start(srcs, land_shapes, route, name):
    n = len(srcs)
    lands = [lax.empty(shp, s.dtype) for shp, s in zip(land_shapes, srcs)]

    def body(*refs):
        ins, lnd = refs[:n], refs[n:2 * n]
        send_sems, recv_sems = refs[2 * n], refs[2 * n + 1]
        token = refs[-1]
        x, y, c = _me()
        for t in range(n):
            for k, (fx, fy) in enumerate(CHIP_FLIPS):
                src, dst = route(ins[t], lnd[t], k, x, y)
                pltpu.make_async_remote_copy(
                    src_ref=src, dst_ref=dst, send_sem=send_sems.at[3 * t + k], recv_sem=recv_sems.at[3 * t + k],
                    device_id=(_flip(x, fx), _flip(y, fy), c), device_id_type=MESH).start()
        token[...] = jnp.zeros_like(token)

    bufs = [_in_hbm(a) for a in list(srcs) + lands]
    outs = pl.pallas_call(
        body, name=name,
        out_shape=(pltpu.SemaphoreType.DMA((3 * n,)), pltpu.SemaphoreType.DMA((3 * n,)),
                   *[pltpu.HBM(b.shape, b.dtype) for b in bufs], TOKEN),
        in_specs=[HBM_SPEC] * (2 * n),
        out_specs=(SEM_SPEC, SEM_SPEC, *[HBM_SPEC] * (2 * n), pl.BlockSpec(memory_space=pltpu.VMEM)),
        input_output_aliases={i: 2 + i for i in range(2 * n)},
        compiler_params=pltpu.CompilerParams(has_side_effects=DATAFLOW))(*bufs)
    return outs[0], outs[1], list(outs[2:2 + n]), list(outs[2 + n:2 + 2 * n]), outs[-1]


def _push_wait(send_sems, recv_sems, srcs, lands, after, route, name):
    n = len(srcs)

    def body(*refs):
        ins, lnd = refs[:n], refs[n:2 * n]
        ssem, rsem = refs[2 * n], refs[2 * n + 1]
        x, y, c = _me()
        for t in range(n):
            for k, (fx, fy) in enumerate(CHIP_FLIPS):
                src, dst = route(ins[t], lnd[t], k, x, y)
                cp = pltpu.make_async_remote_copy(
                    src_ref=src, dst_ref=dst, send_sem=ssem.at[3 * t + k], recv_sem=rsem.at[3 * t + k],
                    device_id=(_flip(x, fx), _flip(y, fy), c), device_id_type=MESH)
                cp.wait_send()
                cp.wait_recv()

    bufs = list(srcs) + list(lands)
    outs = pl.pallas_call(
        body, name=name, out_shape=tuple(pltpu.HBM(b.shape, b.dtype) for b in bufs),
        in_specs=[HBM_SPEC] * (2 * n) + [SEM_SPEC, SEM_SPEC, ANY_SPEC], out_specs=tuple([HBM_SPEC] * (2 * n)),
        input_output_aliases={i: i for i in range(2 * n)},
        compiler_params=pltpu.CompilerParams(has_side_effects=DATAFLOW))(*bufs, send_sems, recv_sems, after)
    return list(outs[:n]), list(outs[n:])


def _route_gather(src, land, k, x, y):
    return src, land.at[2 * x + y]


def _route_gather_wait(src, land, k, x, y):
    fx, fy = CHIP_FLIPS[k]
    return src, land.at[2 * _flip(x, fx) + _flip(y, fy)]


def _route_scatter(src, land, k, x, y):
    fx, fy = CHIP_FLIPS[k]
    return src.at[2 * _flip(x, fx) + _flip(y, fy)], land.at[k]


def _allreduce_small(v):
    r = v.shape[0]

    def body(v_ref, o_ref, buf, send_sems, recv_sems):
        x, y, c = _me()
        me = 4 * x + 2 * y + c
        buf[0] = v_ref[...]
        cps = []
        for k in range(1, 8):
            kx, ky, kc = (k >> 2) & 1, (k >> 1) & 1, k & 1
            cp = pltpu.make_async_remote_copy(
                src_ref=v_ref, dst_ref=buf.at[k], send_sem=send_sems.at[k - 1], recv_sem=recv_sems.at[k - 1],
                device_id=(_flip(x, kx), _flip(y, ky), _flip(c, kc)), device_id_type=MESH)
            cp.start()
            cps.append(cp)
        for cp in cps:
            cp.wait()
        acc = buf[me]
        for d in range(1, 8):
            acc = acc + buf[jnp.bitwise_xor(me, d)]
        o_ref[...] = acc

    vm = pl.BlockSpec(memory_space=pltpu.VMEM)
    return pl.pallas_call(
        body, name="allreduce_small", in_specs=[vm], out_specs=vm, out_shape=jax.ShapeDtypeStruct(v.shape, F32),
        scratch_shapes=[pltpu.VMEM((8, r, LANES), F32), pltpu.SemaphoreType.DMA((7,)),
                        pltpu.SemaphoreType.DMA((7,))],
        compiler_params=pltpu.CompilerParams(has_side_effects=True, vmem_limit_bytes=VMEM_LIMIT))(v)


def _grad_reduce_begin(g4, pos, tag):
    recv = _exchange_halves([g4], name="grad_exchange_halves_" + tag)[0]
    p16 = _sum_own_half(g4, recv, pos, name="grad_sum_pair_" + tag)
    send_sems, recv_sems, srcs, lands, token = _push_start(
        [p16], [(3,) + p16.shape[1:]], _route_scatter, name="grad_scatter_start_" + tag)
    return (g4, recv, send_sems, recv_sems, srcs, lands, tag), token


def _grad_reduce_finish(state, pos, after):
    g4, recv, send_sems, recv_sems, srcs, lands, tag = state
    parts = _push_wait(send_sems, recv_sems, srcs, lands, after, _route_scatter,
                       name="grad_scatter_wait_" + tag)[1][0]
    mine = _sum_chips(g4, recv, parts, pos, name="grad_sum_chips_" + tag)
    return mine, _share_halves([mine], name="grad_share_halves_" + tag)[0]


def _local_step(x, tgt, p, w_in_main, w_in_dt, hooks):
    t = x.shape[0]
    tables = _rope_tables(t)
    sinks = p['sinks'].reshape(N_Q_HEADS)

    def told(name, value):
        tok = hooks.grad_ready(name, value)
        return () if tok is None else (tok,)

    xn = _rmsnorm_fwd(x, p['norm_mix'], "norm_mix_fwd")
    proj = _matmul(xn, w_in_main, mode='nn', name="in_proj", deps=hooks.first_deps)
    dt_raw = _matmul(xn, w_in_dt, mode='nn', name="in_proj_dt")[:, :SSD_HEADS]
    attn = _attn_fwd(proj, sinks, tables)
    conv_b = p['ssd_conv_b']
    xbc = _conv_fwd(proj, p['ssd_conv_w'], conv_b, col0=O_XBC, width=CONV_CH, act=True, name="ssd_conv_fwd")
    sp = _ssd_params(dt_raw, p['dt_bias'].reshape(-1), p['a_log'].reshape(-1), p['ssd_d'].reshape(-1))
    y, states = _ssd_fwd(xbc, sp)
    mix = _mix_fwd(attn, y, proj, p['attn_out_norm'], p['ssd_norm'])
    w_out, w_up, w_down = hooks.rest_weights(mix)
    h1 = _matmul(mix, w_out, mode='nn', name="out_proj", add=x)
    hn = _rmsnorm_fwd(h1, p['norm_ffn'], "norm_ffn_fwd")
    u0 = _matmul(hn, w_up, mode='nn', name="ffn_up")
    a = _ffn_act_fwd(u0, p['ffn_conv_w'], p['ffn_conv_b'])
    h2 = _matmul(a, w_down, mode='nn', name="ffn_down", add=h1, tk=1408)
    loss, dh2, g_norm_final = _final_loss(h2, p['norm_final'].reshape(1, D_MODEL), tgt)

    g = {}
    da = _matmul(dh2, w_down, mode='nt', name="ffn_down_dx", out_dtype=BF16, tn=1408)
    g['w_down'] = _matmul(a, dh2, mode='tn', name="ffn_down_dw", tm=1408, tk=1024)
    dep = told('w_down', g['w_down'])
    du0, dcw, dcb = _ffn_act_bwd(u0, p['ffn_conv_w'], p['ffn_conv_b'], da)
    g['ffn_conv_w'] = dcw.transpose(1, 0, 2).reshape(FFN_CONV, 2 * D_FF)
    g['ffn_conv_b'] = dcb.transpose(1, 0, 2).reshape(1, 2 * D_FF)
    g['w_up'] = _matmul(hn, du0, mode='tn', name="ffn_up_dw", deps=dep, b_halves=True, owner_major=True,
                        tn=1408)
    dep = told('w_up', g['w_up'])
    dhn = _matmul(du0, w_up, mode='nt', name="ffn_up_dx", out_dtype=BF16, deps=dep, a_halves=True, tk=2816)
    dh1, g['norm_ffn'] = _rmsnorm_bwd(h1, p['norm_ffn'], dhn, dh2, "norm_ffn_bwd")

    g['w_out'] = _matmul(mix, dh1, mode='tn', name="out_proj_dw")
    dep = told('w_out', g['w_out'])
    dmix = _matmul(dh1, w_out, mode='nt', name="out_proj_dx", out_dtype=BF16, deps=dep)
    dattn, dy, dz, g['attn_out_norm'], g['ssd_norm'] = _mix_bwd(dmix, attn, y, proj, p['attn_out_norm'],
                                                                p['ssd_norm'])
    dq, dk, dv, dsink = _attn_bwd(proj, sinks, tables, dattn)
    g['sinks'] = dsink[:, :, 0].reshape(1, N_Q_HEADS)
    dxs, dbm, dcm, ddt8, dpar = _ssd_bwd(xbc, sp, states, dy)
    g['dt_bias'] = dpar[:, 0, :].reshape(1, SSD_HEADS)
    g['a_log'] = dpar[:, 1, :].reshape(1, SSD_HEADS)
    g['ssd_d'] = dpar[:, 2, :].reshape(1, SSD_HEADS)
    dxbc_act = jnp.concatenate([dxs, dbm, dcm], axis=1)
    dconv = _conv_silu_dact(proj, p['ssd_conv_w'], conv_b, dxbc_act, col0=O_XBC, width=CONV_CH,
                            name="ssd_conv_dact")
    dxbc, g['ssd_conv_w'], g['ssd_conv_b'] = _conv_bwd(proj, p['ssd_conv_w'], dconv, col0=O_XBC, width=CONV_CH,
                                                       name="ssd_conv_bwd")
    dproj = jnp.concatenate([dq, dk, dv, dz, dxbc], axis=1)
    ddt = ddt8.transpose(1, 0, 2).reshape(t, SSD_HEADS)
    ddt_pad = jnp.pad(ddt, ((0, 0), (0, LANES - SSD_HEADS))).astype(BF16)
    g['w_in'] = (_matmul(xn, dproj, mode='tn', name="in_proj_dw"),
                 _matmul(xn, ddt_pad, mode='tn', name="in_proj_dt_dw"))
    dep = told('w_in', g['w_in'])
    dxn_dt = _matmul(ddt_pad, w_in_dt, mode='nt', name="in_proj_dt_dx", deps=dep)
    dxn = _matmul(dproj, w_in_main, mode='nt', name="in_proj_dx", out_dtype=BF16, add=dxn_dt, tk=2304)
    dx, g['norm_mix'] = _rmsnorm_bwd(x, p['norm_mix'], dxn, dh1, "norm_mix_bwd")
    g['norm_final'] = g_norm_final
    return loss, dx, g


def _pack(arrs):
    flat = jnp.concatenate([a.reshape(-1) for a in arrs])
    n = flat.shape[0]
    rows = -(-n // LANES)
    rows = -(-rows // 8) * 8
    return jnp.pad(flat, (0, rows * LANES - n)).reshape(rows, LANES)


def _unpack(packed, shapes):
    flat = packed.reshape(-1)
    out, off = [], 0
    for s in shapes:
        n = 1
        for d in s:
            n *= d
        out.append(flat[off:off + n].reshape(s))
        off += n
    return out


def _whole_from_cols(gathered, own, chip):
    return jnp.concatenate([jnp.where(chip == j, own, gathered[j]) for j in range(N_CHIPS)], axis=1)


def _whole_from_rows(gathered, own, chip):
    return lax.dynamic_update_slice(gathered, own[None], (chip, 0, 0)).reshape(-1, own.shape[1])


class _StepHooks:
    def __init__(self, first_deps, rest_weights, grad_ready):
        self.first_deps = first_deps
        self.rest_weights = rest_weights
        self.grad_ready = grad_ready


def _owner_major(gfull, axis):
    if axis == 0:
        return gfull.reshape(N_CHIPS, gfull.shape[0] // N_CHIPS, gfull.shape[1])
    cw = gfull.shape[1] // N_CHIPS
    return jnp.stack([gfull[:, j * cw:(j + 1) * cw] for j in range(N_CHIPS)], axis=0)


def kernel(x, norm_mix, w_in, sinks, attn_out_norm, ssd_conv_w, ssd_conv_b, dt_bias, a_log, ssd_d, ssd_norm, w_out, norm_ffn, w_up, ffn_conv_w, ffn_conv_b, w_down, norm_final, loss_target, m_norm_mix, m_w_in, m_sinks, m_attn_out_norm, m_ssd_conv_w, m_ssd_conv_b, m_dt_bias, m_a_log, m_ssd_d, m_ssd_norm, m_w_out, m_norm_ffn, m_w_up, m_ffn_conv_w, m_ffn_conv_b, m_w_down, m_norm_final, v_norm_mix, v_w_in, v_sinks, v_attn_out_norm, v_ssd_conv_w, v_ssd_conv_b, v_dt_bias, v_a_log, v_ssd_d, v_ssd_norm, v_w_out, v_norm_ffn, v_w_up, v_ffn_conv_w, v_ffn_conv_b, v_w_down, v_norm_final):
    args = dict(locals())
    w = {n: args[n] for n in WEIGHTS}
    m = {n: args['m_' + n] for n in WEIGHTS}
    v = {n: args['v_' + n] for n in WEIGHTS}
    xi, yi, ci = _me()
    chip = 2 * xi + yi
    pos = jnp.stack([ci, chip]).astype(jnp.int32)

    def place(shard, full_cols):
        z = jnp.zeros((shard.shape[0], full_cols), F32)
        return lax.dynamic_update_slice(z, shard * 0.5, (0, chip * shard.shape[1]))

    conv_pack = _pack([place(ssd_conv_w[0], CONV_CH), place(ffn_conv_w[0], 2 * D_FF)])
    conv_full = _allreduce_small(conv_pack)
    ssd_conv_w_full, ffn_conv_w_full = _unpack(conv_full, [(SSD_CONV, CONV_CH), (FFN_CONV, 2 * D_FF)])

    in_shard = w_in[0].astype(BF16)
    full_in = _whole_from_cols(_allgather_weights([in_shard])[0], in_shard, chip)
    w_in_main = full_in[:, :MAIN_WIDTH]
    w_in_dt = jnp.pad(full_in[:, MAIN_WIDTH:], ((0, 0), (0, LANES - SSD_HEADS)))
    rest = [w_out[0].astype(BF16), w_up[0].astype(BF16), w_down[0].astype(BF16)]
    rest_send, rest_recv, rest_srcs, rest_lands, rest_token = _push_start(
        rest, [(N_CHIPS,) + s.shape for s in rest], _route_gather, name="gather_rest_start")

    def rest_weights(after):
        own, got = _push_wait(rest_send, rest_recv, rest_srcs, rest_lands, after, _route_gather_wait,
                              name="gather_rest_wait")
        return (_whole_from_rows(got[0], own[0], chip), _whole_from_cols(got[1], own[1], chip),
                _whole_from_rows(got[2], own[2], chip))

    reductions = {}

    def grad_ready(name, value):
        if name == 'w_in':
            value = jnp.concatenate([value[0], value[1][:, :SSD_HEADS]], axis=1)
        g4 = value if value.ndim == 3 else _owner_major(value, 1 if name == 'w_in' else 0)
        reductions[name], token = _grad_reduce_begin(g4, pos, name)
        return token

    small = {
        'norm_mix': norm_mix, 'sinks': sinks, 'attn_out_norm': attn_out_norm, 'ssd_conv_w': ssd_conv_w_full,
        'ssd_conv_b': ssd_conv_b, 'dt_bias': dt_bias, 'a_log': a_log, 'ssd_d': ssd_d, 'ssd_norm': ssd_norm,
        'norm_ffn': norm_ffn, 'ffn_conv_w': ffn_conv_w_full, 'ffn_conv_b': ffn_conv_b, 'norm_final': norm_final,
    }
    loss, dx, g = _local_step(x[0], loss_target[0], small, w_in_main, w_in_dt,
                              _StepHooks((rest_token,), rest_weights, grad_ready))
    gbig = {n: _grad_reduce_finish(reductions[n], pos, dx) for n in ('w_down', 'w_up', 'w_out', 'w_in')}

    small_names = [n for n in WEIGHTS if n not in BIG]
    small_g = [loss[:, :1]] + [g[n] for n in small_names]
    small_shapes = [(1, 1)] + [tuple(a.shape) for a in small_g[1:]]
    red = _unpack(_allreduce_small(_pack(small_g)), small_shapes)
    loss_out = red[0].reshape(())
    gsm = dict(zip(small_names, red[1:]))
    gsm['ssd_conv_w'] = lax.dynamic_slice(gsm['ssd_conv_w'], (0, chip * ssd_conv_w.shape[2]),
                                          (SSD_CONV, ssd_conv_w.shape[2]))
    gsm['ffn_conv_w'] = lax.dynamic_slice(gsm['ffn_conv_w'], (0, chip * ffn_conv_w.shape[2]),
                                          (FFN_CONV, ffn_conv_w.shape[2]))

    grads, deltas, new_m, new_v = {}, {}, {}, {}
    for n in BIG:
        mine, theirs = gbig[n]
        gg, d, m2, v2 = _adamw_halves(w[n][0], mine, theirs, m[n][0], v[n][0], pos, name="adamw_" + n)
        grads[n], deltas[n], new_m[n], new_v[n] = gg[None], d[None], m2[None], v2[None]
    shapes = [tuple(w[n].shape) for n in small_names]
    gp = _pack([gsm[n] for n in small_names])
    d, m2, v2 = _adamw(_pack([w[n] for n in small_names]), gp, _pack([m[n] for n in small_names]),
                       _pack([v[n] for n in small_names]), name="adamw_small")
    for n, gg, dd, mm, vv in zip(small_names, _unpack(gp, shapes), _unpack(d, shapes), _unpack(m2, shapes),
                                 _unpack(v2, shapes)):
        grads[n], deltas[n], new_m[n], new_v[n] = gg, dd, mm, vv

    return (loss_out, dx[None], *[grads[n] for n in WEIGHTS], *[deltas[n] for n in WEIGHTS],
            *[new_m[n] for n in WEIGHTS], *[new_v[n] for n in WEIGHTS])
```

```python
import functools

import jax
import jax.numpy as jnp
from jax import lax
from jax.experimental import pallas as pl
from jax.experimental.pallas import tpu as pltpu

F32 = jnp.float32
BF16 = jnp.bfloat16

D_MODEL = 2048
N_Q_HEADS = 32
N_KV_HEADS = 8
HEAD_DIM = 64
WINDOW = 128
ATTN_BLOCK = 128
ROT_DIM = 16
ROPE_THETA = 500000.0
SSD_HEADS = 32
SSD_HEAD_DIM = 64
SSD_INNER = 2048
SSD_GROUPS = 8
SSD_STATE = 128
SSD_CONV = 4
SSD_CHUNK = 128
ATTN_WIDTH = 2048
KV_WIDTH = 512
BC_WIDTH = 1024
CONV_CH = 4096
IN_PROJ_WIDTH = 9248
MAIN_WIDTH = 9216
D_FF = 5632
FFN_CONV = 3
EPS = 1e-6
O_Q, O_K, O_V, O_Z, O_XBC, O_DT = 0, 2048, 2560, 3072, 5120, 9216

ADAM_LR = 0.001
ADAM_B1 = 0.9
ADAM_B2 = 0.999
ADAM_EPS = 1e-08
ADAM_WD = 0.01
ADAM_STEP = 10

N_CHIPS = 4
NEG = -1e30
LANES = 128
VMEM_LIMIT = 48 * 1024 * 1024
MESH = pl.DeviceIdType.MESH
HBM_SPEC = pl.BlockSpec(memory_space=pltpu.HBM)

WEIGHTS = ['norm_mix', 'w_in', 'sinks', 'attn_out_norm', 'ssd_conv_w', 'ssd_conv_b', 'dt_bias', 'a_log', 'ssd_d',
           'ssd_norm', 'w_out', 'norm_ffn', 'w_up', 'ffn_conv_w', 'ffn_conv_b', 'w_down', 'norm_final']
BIG = ['w_in', 'w_out', 'w_up', 'w_down']


def _cp(sem=None, vmem=VMEM_LIMIT):
    kw = {'vmem_limit_bytes': vmem}
    if sem is not None:
        kw['dimension_semantics'] = sem
    return pltpu.CompilerParams(**kw)


def _tile(n, pref):
    if n <= pref:
        return n
    t = (pref // LANES) * LANES
    while t > LANES and n % t:
        t -= LANES
    assert n % t == 0, (n, pref)
    return t


def _rows(n, pref):
    t = min(n, pref)
    while n % t:
        t -= 8
    return t


def _iota(shape, dim):
    return lax.broadcasted_iota(jnp.int32, shape, dim)


def _dot(a, b, mode='nn'):
    dn = {'nn': (((1,), (0,)), ((), ())), 'nt': (((1,), (1,)), ((), ())), 'tn': (((0,), (0,)), ((), ()))}[mode]
    return lax.dot_general(a.astype(BF16), b.astype(BF16), dn, preferred_element_type=F32)


def _dot_exact(a, b):
    return lax.dot_general(a, b, (((1,), (0,)), ((), ())), precision=lax.Precision.HIGHEST,
                           preferred_element_type=F32)


def _sigmoid(x):
    return 1.0 / (1.0 + jnp.exp(-x))


def _softplus(x):
    return jnp.maximum(x, 0.0) + jnp.log(1.0 + jnp.exp(-jnp.abs(x)))


def _matmul(a, b, *, mode, name, out_dtype=F32, add=None, deps=(), tm=1024, tn=1024, tk=2048,
            a_halves=False, b_halves=False, b_owner=False, owner_major=False):
    ash, bsh = (a.shape[1:] if a_halves else a.shape), (b.shape[1:] if (b_halves or b_owner) else b.shape)
    if mode == 'nn':
        (m, k), (k2, n) = ash, bsh
    elif mode == 'nt':
        (m, k), (n, k2) = ash, bsh
    else:
        (k, m), (k2, n) = ash, bsh
    if a_halves:
        assert mode == 'nt'
        k = 2 * k
    if b_halves:
        assert mode == 'tn'
        n = 2 * n
    if b_owner:
        assert mode in ('nn', 'nt')
        if mode == 'nn':
            n = 4 * n
        else:
            k2 = 4 * k2
    assert k == k2, (a.shape, b.shape, mode)
    tm = _tile(m, tm)
    tn = _tile(n // 4 if (owner_major or (b_owner and mode == 'nn')) else (n // 2 if b_halves else n), tn)
    tk = _tile(k // 4 if (b_owner and mode == 'nt') else (k // 2 if a_halves else k), tk)
    nk = k // tk
    has_add = add is not None
    assert not (has_add and owner_major)

    def body(*refs):
        a_ref, b_ref = refs[:2]
        add_ref = refs[2] if has_add else None

        def finish(r, o_ref):
            if has_add:
                r = r + add_ref[...].astype(F32)
            o_ref[...] = r.astype(out_dtype)

        if nk == 1:
            finish(_dot(a_ref[...], b_ref[...], mode), refs[-1])
            return
        o_ref, acc = refs[-2:]
        kk = pl.program_id(2)

        @pl.when(kk == 0)
        def _():
            acc[...] = _dot(a_ref[...], b_ref[...], mode)

        @pl.when((kk > 0) & (kk < nk - 1))
        def _():
            acc[...] += _dot(a_ref[...], b_ref[...], mode)

        @pl.when(kk == nk - 1)
        def _():
            finish(acc[...] + _dot(a_ref[...], b_ref[...], mode), o_ref)

    if mode == 'tn':
        a_spec = pl.BlockSpec((tk, tm), lambda i, j, kk: (kk, i))
    elif a_halves:
        nkh = nk // 2
        a_spec = pl.BlockSpec((None, tm, tk), lambda i, j, kk: (kk // nkh, i, kk % nkh))
    else:
        a_spec = pl.BlockSpec((tm, tk), lambda i, j, kk: (i, kk))
    if mode == 'nt' and b_owner:
        nkq = nk // 4
        b_spec = pl.BlockSpec((None, tn, tk), lambda i, j, kk: (kk // nkq, j, kk % nkq))
    elif mode == 'nt':
        b_spec = pl.BlockSpec((tn, tk), lambda i, j, kk: (j, kk))
    elif b_owner:
        njq = (n // 4) // tn
        b_spec = pl.BlockSpec((None, tk, tn), lambda i, j, kk: (j // njq, kk, j % njq))
    elif b_halves:
        njh = (n // 2) // tn
        b_spec = pl.BlockSpec((None, tk, tn), lambda i, j, kk: (j // njh, kk, j % njh))
    else:
        b_spec = pl.BlockSpec((tk, tn), lambda i, j, kk: (kk, j))
    if owner_major:
        njo = (n // 4) // tn
        o_spec = pl.BlockSpec((None, tm, tn), lambda i, j, kk: (j // njo, i, j % njo))
        out_shape = jax.ShapeDtypeStruct((N_CHIPS, m, n // 4), out_dtype)
    else:
        o_spec = pl.BlockSpec((tm, tn), lambda i, j, kk: (i, j))
        out_shape = jax.ShapeDtypeStruct((m, n), out_dtype)
    dep_spec = pl.BlockSpec((8, LANES), lambda i, j, kk: (0, 0))
    in_specs = [a_spec, b_spec] + ([pl.BlockSpec((tm, tn), lambda i, j, kk: (i, j))] if has_add else [])
    in_specs += [dep_spec] * len(deps)
    args = (a, b) + ((add,) if has_add else ()) + tuple(deps)
    return pl.pallas_call(
        body, name=name, grid=(m // tm, n // tn, nk), in_specs=in_specs, out_specs=o_spec, out_shape=out_shape,
        scratch_shapes=[pltpu.VMEM((tm, tn), F32)] if nk > 1 else [],
        compiler_params=_cp(("parallel", "parallel", "arbitrary")))(*args)


def _rmsnorm_fwd(x, g, name):
    t, d = x.shape
    tb = _rows(t, 256)

    def body(x_ref, g_ref, o_ref):
        xv = x_ref[...]
        r = lax.rsqrt(jnp.mean(xv * xv, axis=-1, keepdims=True) + EPS)
        o_ref[...] = (xv * r * g_ref[...]).astype(BF16)

    return pl.pallas_call(
        body, name=name, grid=(t // tb,),
        in_specs=[pl.BlockSpec((tb, d), lambda i: (i, 0)), pl.BlockSpec((1, d), lambda i: (0, 0))],
        out_specs=pl.BlockSpec((tb, d), lambda i: (i, 0)), out_shape=jax.ShapeDtypeStruct((t, d), BF16),
        compiler_params=_cp(("parallel",)))(x, g)


def _rmsnorm_bwd(x, g, dy, res, name):
    t, d = x.shape
    tb = _rows(t, 256)

    def body(x_ref, g_ref, dy_ref, res_ref, dx_ref, dg_ref):
        i = pl.program_id(0)
        xv = x_ref[...]
        dyv = dy_ref[...].astype(F32)
        r = lax.rsqrt(jnp.mean(xv * xv, axis=-1, keepdims=True) + EPS)
        u = dyv * g_ref[...]
        dx = r * u - xv * (r * r * r * jnp.mean(u * xv, axis=-1, keepdims=True))
        dx_ref[...] = dx + res_ref[...]
        part = jnp.sum(dyv * xv * r, axis=0, keepdims=True)

        @pl.when(i == 0)
        def _():
            dg_ref[...] = part

        @pl.when(i > 0)
        def _():
            dg_ref[...] += part

    row = pl.BlockSpec((tb, d), lambda i: (i, 0))
    vec = pl.BlockSpec((1, d), lambda i: (0, 0))
    return pl.pallas_call(
        body, name=name, grid=(t // tb,), in_specs=[row, vec, row, row], out_specs=[row, vec],
        out_shape=[jax.ShapeDtypeStruct((t, d), F32), jax.ShapeDtypeStruct((1, d), F32)],
        compiler_params=_cp(("arbitrary",)))(x, g, dy, res)


def _final_loss(h, g, tgt):
    t, d = h.shape
    tb = _rows(t, 256)

    def body(h_ref, g_ref, t_ref, loss_ref, dh_ref, dg_ref):
        i = pl.program_id(0)
        hv = h_ref[...]
        gv = g_ref[...]
        r = lax.rsqrt(jnp.mean(hv * hv, axis=-1, keepdims=True) + EPS)
        y = hv * r * gv
        diff = y - t_ref[...]
        lpart = jnp.sum(jnp.sum(diff * diff, axis=1, keepdims=True), axis=0, keepdims=True) * (0.5 / d)
        dy = diff * (1.0 / d)
        u = dy * gv
        dh_ref[...] = r * u - hv * (r * r * r * jnp.mean(u * hv, axis=-1, keepdims=True))
        gpart = jnp.sum(dy * hv * r, axis=0, keepdims=True)
        lrow = jnp.broadcast_to(lpart, (1, LANES))

        @pl.when(i == 0)
        def _():
            loss_ref[...] = lrow
            dg_ref[...] = gpart

        @pl.when(i > 0)
        def _():
            loss_ref[...] += lrow
            dg_ref[...] += gpart

    row = pl.BlockSpec((tb, d), lambda i: (i, 0))
    vec = pl.BlockSpec((1, d), lambda i: (0, 0))
    return pl.pallas_call(
        body, name="final_loss", grid=(t // tb,), in_specs=[row, vec, row],
        out_specs=[pl.BlockSpec((1, LANES), lambda i: (0, 0)), row, vec],
        out_shape=[jax.ShapeDtypeStruct((1, LANES), F32), jax.ShapeDtypeStruct((t, d), F32),
                   jax.ShapeDtypeStruct((1, d), F32)],
        compiler_params=_cp(("arbitrary",)))(h, g, tgt)


def _rope_tables(t):
    pos = jnp.arange(t, dtype=F32)
    inv = 1.0 / (ROPE_THETA ** (jnp.arange(0, ROT_DIM, 2, dtype=F32) / ROT_DIM))
    ang = pos[:, None] * inv[None, :]
    cos, sin = jnp.cos(ang), jnp.sin(ang)
    half = ROT_DIM // 2
    rest = HEAD_DIM - ROT_DIM
    c = jnp.concatenate([cos, cos, jnp.ones((t, rest), F32)], axis=1)
    s1 = jnp.concatenate([-sin, jnp.zeros((t, half + rest), F32)], axis=1)
    s2 = jnp.concatenate([jnp.zeros((t, half), F32), sin, jnp.zeros((t, rest), F32)], axis=1)
    return tuple(jnp.tile(v, (1, LANES // HEAD_DIM)) for v in (c, s1, s2))


def _rope(x, c, s1, s2):
    half = ROT_DIM // 2
    return x * c + pltpu.roll(x, LANES - half, 1) * s1 + pltpu.roll(x, half, 1) * s2


def _rope_t(g, c, s1, s2):
    half = ROT_DIM // 2
    return g * c + pltpu.roll(g * s1, half, 1) + pltpu.roll(g * s2, LANES - half, 1)


def _attn_mask(i):
    qi = _iota((ATTN_BLOCK, 2 * ATTN_BLOCK), 0)
    kj = _iota((ATTN_BLOCK, 2 * ATTN_BLOCK), 1)
    rel = qi + ATTN_BLOCK - kj
    first_key = jnp.where(i > 0, 0, ATTN_BLOCK)
    return (rel >= 0) & (rel < WINDOW) & (kj >= first_key)


def _attn_mask4(i):
    n = 4 * ATTN_BLOCK
    qi = jnp.bitwise_and(_iota((n, 2 * ATTN_BLOCK), 0), ATTN_BLOCK - 1)
    kj = _iota((n, 2 * ATTN_BLOCK), 1)
    rel = qi + ATTN_BLOCK - kj
    first_key = jnp.where(i > 0, 0, ATTN_BLOCK)
    return (rel >= 0) & (rel < WINDOW) & (kj >= first_key)


def _half_masks():
    lane = _iota((1, LANES), 1)
    return [(lane < HEAD_DIM).astype(F32), (lane >= HEAD_DIM).astype(F32)]


def _stack_heads(blocks, hm, j):
    pieces = []
    for r in range(4):
        qb, half = (4 * j + r) // 2, (4 * j + r) % 2
        piece = blocks[qb] * hm[half]
        if half != j:
            piece = pltpu.roll(piece, HEAD_DIM, 1)
        pieces.append(piece)
    return jnp.concatenate(pieces, axis=0)


def _unstack_heads(stacked, j):
    out = []
    for qb in (2 * j, 2 * j + 1):
        acc = None
        for half in range(2):
            r = 2 * qb + half - 4 * j
            piece = stacked[r * ATTN_BLOCK:(r + 1) * ATTN_BLOCK]
            if half != j:
                piece = pltpu.roll(piece, HEAD_DIM, 1)
            acc = piece if acc is None else acc + piece
        out.append((qb, acc))
    return out


def _sink_column(sink_ref, base):
    return jnp.concatenate([jnp.full((ATTN_BLOCK, 1), sink_ref[base + r], F32) for r in range(4)], axis=0)


def _attn_specs(nb_clamp):
    blk = ATTN_BLOCK
    kb, vb = O_K // LANES, O_V // LANES

    def cur(i):
        return jnp.minimum(i, nb_clamp)

    def prev(i):
        return jnp.maximum(jnp.minimum(i, nb_clamp + 1) - 1, 0)

    q = pl.BlockSpec((blk, 512), lambda p, i: (cur(i), p))
    kc = pl.BlockSpec((blk, LANES), lambda p, i: (cur(i), kb + p))
    kp = pl.BlockSpec((blk, LANES), lambda p, i: (prev(i), kb + p))
    vc = pl.BlockSpec((blk, LANES), lambda p, i: (cur(i), vb + p))
    vp = pl.BlockSpec((blk, LANES), lambda p, i: (prev(i), vb + p))
    tc = pl.BlockSpec((blk, LANES), lambda p, i: (cur(i), 0))
    tp = pl.BlockSpec((blk, LANES), lambda p, i: (prev(i), 0))
    return q, kc, kp, vc, vp, tc, tp


def _attn_fwd(proj, sinks, tables):
    t = proj.shape[0]
    nb = t // ATTN_BLOCK
    scale = HEAD_DIM ** -0.5

    def body(sink_ref, q_ref, kc_ref, kp_ref, vc_ref, vp_ref, cc_ref, s1c_ref, s2c_ref, cp_ref, s1p_ref, s2p_ref,
             o_ref):
        p = pl.program_id(0)
        i = pl.program_id(1)
        cc, s1c, s2c = cc_ref[...], s1c_ref[...], s2c_ref[...]
        kband = jnp.concatenate([_rope(kp_ref[...], cp_ref[...], s1p_ref[...], s2p_ref[...]),
                                 _rope(kc_ref[...], cc, s1c, s2c)], axis=0).astype(BF16)
        vband = jnp.concatenate([vp_ref[...], vc_ref[...]], axis=0)
        hm = _half_masks()
        vsel = [(vband * hm[j]).astype(BF16) for j in range(2)]
        valid = _attn_mask(i)
        for qb in range(4):
            qr = _rope(q_ref[:, qb * LANES:(qb + 1) * LANES], cc, s1c, s2c)
            acc = jnp.zeros((ATTN_BLOCK, LANES), F32)
            for half in range(2):
                hh = qb * 2 + half
                j = hh // 4
                qs = qr * hm[half]
                if half != j:
                    qs = pltpu.roll(qs, HEAD_DIM, 1)
                s = jnp.where(valid, _dot(qs, kband, 'nt') * scale, NEG)
                sink = sink_ref[p * 8 + hh]
                m = jnp.maximum(jnp.max(s, axis=1, keepdims=True), sink)
                pe = jnp.exp(s - m)
                den = jnp.sum(pe, axis=1, keepdims=True) + jnp.exp(sink - m)
                o = _dot(pe / den, vsel[j])
                if half != j:
                    o = pltpu.roll(o, HEAD_DIM, 1)
                acc = acc + o
            o_ref[:, qb * LANES:(qb + 1) * LANES] = acc

    q, kc, kp, vc, vp, tc, tp = _attn_specs(nb - 1)
    smem = pl.BlockSpec(memory_space=pltpu.SMEM)
    return pl.pallas_call(
        body, name="attn_fwd", grid=(4, nb),
        in_specs=[smem, q, kc, kp, vc, vp, tc, tc, tc, tp, tp, tp],
        out_specs=pl.BlockSpec((ATTN_BLOCK, 512), lambda p, i: (i, p)),
        out_shape=jax.ShapeDtypeStruct((t, ATTN_WIDTH), F32),
        compiler_params=_cp(("parallel", "arbitrary")))(sinks, proj, proj, proj, proj, proj, *tables, *tables)


def _attn_bwd(proj, sinks, tables, dout):
    t = proj.shape[0]
    nb = t // ATTN_BLOCK
    scale = HEAD_DIM ** -0.5

    def body(sink_ref, q_ref, kc_ref, kp_ref, vc_ref, vp_ref, cc_ref, s1c_ref, s2c_ref, cp_ref, s1p_ref, s2p_ref,
             do_ref, dq_ref, dk_ref, dv_ref, ds_ref, carry_k, carry_v):
        p = pl.program_id(0)
        i = pl.program_id(1)
        ptab = (cp_ref[...], s1p_ref[...], s2p_ref[...])

        @pl.when(i == 0)
        def _():
            carry_k[...] = jnp.zeros_like(carry_k)
            carry_v[...] = jnp.zeros_like(carry_v)
            ds_ref[...] = jnp.zeros_like(ds_ref)

        @pl.when(i < nb)
        def _():
            cc, s1c, s2c = cc_ref[...], s1c_ref[...], s2c_ref[...]
            kband = jnp.concatenate([_rope(kp_ref[...], *ptab), _rope(kc_ref[...], cc, s1c, s2c)], axis=0)
            vband = jnp.concatenate([vp_ref[...], vc_ref[...]], axis=0)
            hm = _half_masks()
            kband16 = kband.astype(BF16)
            vband16 = vband.astype(BF16)
            valid = _attn_mask4(i)
            dkb = jnp.zeros((2 * ATTN_BLOCK, LANES), F32)
            dvb = jnp.zeros((2 * ATTN_BLOCK, LANES), F32)
            row8 = _iota((8, LANES), 0)
            dsink = jnp.zeros((8, LANES), F32)
            qr = [_rope(q_ref[:, qb * LANES:(qb + 1) * LANES], cc, s1c, s2c) for qb in range(4)]
            dob = [do_ref[:, qb * LANES:(qb + 1) * LANES] for qb in range(4)]
            for j in range(2):
                qst = _stack_heads(qr, hm, j).astype(BF16)
                dost = _stack_heads(dob, hm, j).astype(BF16)
                s = jnp.where(valid, _dot(qst, kband16, 'nt') * scale, NEG)
                sink = _sink_column(sink_ref, p * 8 + 4 * j)
                m = jnp.maximum(jnp.max(s, axis=1, keepdims=True), sink)
                pe = jnp.exp(s - m)
                psink = jnp.exp(sink - m)
                den = jnp.sum(pe, axis=1, keepdims=True) + psink
                pr = pe / den
                dvb = dvb + _dot(pr.T, dost)
                dp = _dot(dost, vband16, 'nt')
                delta = jnp.sum(pr * dp, axis=1, keepdims=True)
                dsc = pr * (dp - delta) * scale
                dsk = psink / den * delta
                for r in range(4):
                    part = jnp.sum(dsk[r * ATTN_BLOCK:(r + 1) * ATTN_BLOCK])
                    dsink = dsink + jnp.where(row8 == 4 * j + r, -part, 0.0)
                for qb, dqb in _unstack_heads(_dot(dsc, kband * hm[j]), j):
                    dq_ref[:, qb * LANES:(qb + 1) * LANES] = _rope_t(dqb, cc, s1c, s2c).astype(BF16)
                dkb = dkb + _dot(dsc.T, qst)
            ds_ref[0] += dsink
            dk_ref[...] = _rope_t(carry_k[...] + dkb[:ATTN_BLOCK], *ptab).astype(BF16)
            dv_ref[...] = (carry_v[...] + dvb[:ATTN_BLOCK]).astype(BF16)
            carry_k[...] = dkb[ATTN_BLOCK:]
            carry_v[...] = dvb[ATTN_BLOCK:]

        @pl.when(i == nb)
        def _():
            dk_ref[...] = _rope_t(carry_k[...], *ptab).astype(BF16)
            dv_ref[...] = carry_v[...].astype(BF16)

    q, kc, kp, vc, vp, tc, tp = _attn_specs(nb - 1)
    smem = pl.BlockSpec(memory_space=pltpu.SMEM)
    qblk = pl.BlockSpec((ATTN_BLOCK, 512), lambda p, i: (jnp.minimum(i, nb - 1), p))
    kvout = pl.BlockSpec((ATTN_BLOCK, LANES), lambda p, i: (jnp.maximum(i - 1, 0), p))
    return pl.pallas_call(
        body, name="attn_bwd", grid=(4, nb + 1),
        in_specs=[smem, q, kc, kp, vc, vp, tc, tc, tc, tp, tp, tp, qblk],
        out_specs=[qblk, kvout, kvout, pl.BlockSpec((1, 8, LANES), lambda p, i: (p, 0, 0))],
        out_shape=[jax.ShapeDtypeStruct((t, ATTN_WIDTH), BF16), jax.ShapeDtypeStruct((t, KV_WIDTH), BF16),
                   jax.ShapeDtypeStruct((t, KV_WIDTH), BF16), jax.ShapeDtypeStruct((4, 8, LANES), F32)],
        scratch_shapes=[pltpu.VMEM((ATTN_BLOCK, LANES), F32), pltpu.VMEM((ATTN_BLOCK, LANES), F32)],
        compiler_params=_cp(("parallel", "arbitrary")))(sinks, proj, proj, proj, proj, proj, *tables, *tables, dout)


def _shift_rows(x, prev8, j):
    r = pltpu.roll(x, j, 0)
    head = jnp.where(_iota((8, 1), 0) < j, pltpu.roll(prev8, j, 0), r[:8])
    if x.shape[0] == 8:
        return head
    return jnp.concatenate([head, r[8:]], axis=0)


def _shift_rows_up(x, next8, j):
    n = x.shape[0]
    r = pltpu.roll(x, n - j, 0)
    tail = jnp.where(_iota((8, 1), 0) >= 8 - j, pltpu.roll(next8, 8 - j, 0), r[n - 8:])
    return jnp.concatenate([r[:n - 8], tail], axis=0)


def _conv_apply(x, prev8, w, b, taps):
    u = b + x * w[taps - 1:taps]
    for j in range(1, taps):
        u = u + _shift_rows(x, prev8, j) * w[taps - 1 - j:taps - j]
    return u


def _conv_grads(du, du_next8, x, x_prev8, w, taps):
    dx = du * w[taps - 1:taps]
    rowk = _iota((taps, 1), 0)
    dw = jnp.where(rowk == taps - 1, jnp.sum(du * x, axis=0, keepdims=True), 0.0)
    for j in range(1, taps):
        dx = dx + _shift_rows_up(du, du_next8, j) * w[taps - 1 - j:taps - j]
        part = jnp.sum(du * _shift_rows(x, x_prev8, j), axis=0, keepdims=True)
        dw = dw + jnp.where(rowk == taps - 1 - j, part, 0.0)
    return dx, dw, jnp.sum(du, axis=0, keepdims=True)


def _conv_specs(tb, tc, col0, t):
    c0 = col0 // tc
    cur = pl.BlockSpec((tb, tc), lambda j, i: (i, c0 + j))
    prev = pl.BlockSpec((8, tc), lambda j, i: (jnp.maximum(i * (tb // 8) - 1, 0), c0 + j))
    nxt = pl.BlockSpec((8, tc), lambda j, i: (jnp.minimum((i + 1) * (tb // 8), t // 8 - 1), c0 + j))
    return cur, prev, nxt


def _conv_fwd(x, w, b, *, col0, width, act, name):
    t = x.shape[0]
    taps = w.shape[0]
    tb, tc = _rows(t, 512), _tile(width, 1024)
    assert col0 % tc == 0

    def body(x_ref, xp_ref, w_ref, b_ref, o_ref):
        i = pl.program_id(1)
        prev8 = jnp.where(i > 0, xp_ref[...], 0.0)
        u = _conv_apply(x_ref[...], prev8, w_ref[...], b_ref[...], taps)
        if act:
            u = u * _sigmoid(u)
        o_ref[...] = u

    cur, prev, _ = _conv_specs(tb, tc, col0, t)
    par = pl.BlockSpec((taps, tc), lambda j, i: (0, j))
    bias = pl.BlockSpec((1, tc), lambda j, i: (0, j))
    return pl.pallas_call(
        body, name=name, grid=(width // tc, t // tb), in_specs=[cur, prev, par, bias],
        out_specs=pl.BlockSpec((tb, tc), lambda j, i: (i, j)), out_shape=jax.ShapeDtypeStruct((t, width), F32),
        compiler_params=_cp(("parallel", "parallel")))(x, x, w, b)


def _conv_silu_dact(x, w, b, dout, *, col0, width, name):
    t = x.shape[0]
    taps = w.shape[0]
    tb, tc = _rows(t, 512), _tile(width, 1024)

    def body(x_ref, xp_ref, w_ref, b_ref, d_ref, o_ref):
        i = pl.program_id(1)
        prev8 = jnp.where(i > 0, xp_ref[...], 0.0)
        u = _conv_apply(x_ref[...], prev8, w_ref[...], b_ref[...], taps)
        sg = _sigmoid(u)
        o_ref[...] = d_ref[...] * (sg * (1.0 + u * (1.0 - sg)))

    cur, prev, _ = _conv_specs(tb, tc, col0, t)
    par = pl.BlockSpec((taps, tc), lambda j, i: (0, j))
    bias = pl.BlockSpec((1, tc), lambda j, i: (0, j))
    out = pl.BlockSpec((tb, tc), lambda j, i: (i, j))
    return pl.pallas_call(
        body, name=name, grid=(width // tc, t // tb), in_specs=[cur, prev, par, bias, out],
        out_specs=out, out_shape=jax.ShapeDtypeStruct((t, width), F32),
        compiler_params=_cp(("parallel", "parallel")))(x, x, w, b, dout)


def _conv_bwd(x, w, du, *, col0, width, name):
    t = x.shape[0]
    taps = w.shape[0]
    tb, tc = _rows(t, 512), _tile(width, 1024)
    nrow = t // tb

    def body(x_ref, xp_ref, w_ref, du_ref, dun_ref, dx_ref, dw_ref, db_ref):
        i = pl.program_id(1)
        xv = x_ref[...]
        prev8 = jnp.where(i > 0, xp_ref[...], 0.0)
        next8 = jnp.where(i < nrow - 1, dun_ref[...], 0.0)
        dx, dwv, dbv = _conv_grads(du_ref[...], next8, xv, prev8, w_ref[...], taps)
        dx_ref[...] = dx.astype(BF16)

        @pl.when(i == 0)
        def _():
            dw_ref[...] = dwv
            db_ref[...] = dbv

        @pl.when(i > 0)
        def _():
            dw_ref[...] += dwv
            db_ref[...] += dbv

    cur, prev, _ = _conv_specs(tb, tc, col0, t)
    dcur, _, dnxt = _conv_specs(tb, tc, 0, t)
    par = pl.BlockSpec((taps, tc), lambda j, i: (0, j))
    bias = pl.BlockSpec((1, tc), lambda j, i: (0, j))
    return pl.pallas_call(
        body, name=name, grid=(width // tc, nrow), in_specs=[cur, prev, par, dcur, dnxt],
        out_specs=[dcur, par, bias],
        out_shape=[jax.ShapeDtypeStruct((t, width), BF16), jax.ShapeDtypeStruct((taps, width), F32),
                   jax.ShapeDtypeStruct((1, width), F32)],
        compiler_params=_cp(("parallel", "arbitrary")))(x, x, w, du, du)


def _ffn_specs(tb, tc, t):
    nc = D_FF // tc

    def cur(half):
        return pl.BlockSpec((tb, tc), lambda j, i: (i, half * nc + j))

    def prev(half):
        return pl.BlockSpec((8, tc), lambda j, i: (jnp.maximum(i * (tb // 8) - 1, 0), half * nc + j))

    def nxt(half):
        return pl.BlockSpec((8, tc), lambda j, i: (jnp.minimum((i + 1) * (tb // 8), t // 8 - 1), half * nc + j))

    def par(rows, half):
        return pl.BlockSpec((rows, tc), lambda j, i: (0, half * nc + j))

    return cur, prev, nxt, par


def _ffn_act_fwd(u0, w, b):
    t = u0.shape[0]
    tb, tc = _rows(t, 512), _tile(D_FF, 1408)
    cur, prev, _, par = _ffn_specs(tb, tc, t)

    def body(g_ref, gp_ref, v_ref, vp_ref, wg_ref, wv_ref, bg_ref, bv_ref, o_ref):
        i = pl.program_id(1)
        ug = _conv_apply(g_ref[...], jnp.where(i > 0, gp_ref[...], 0.0), wg_ref[...], bg_ref[...], FFN_CONV)
        uv = _conv_apply(v_ref[...], jnp.where(i > 0, vp_ref[...], 0.0), wv_ref[...], bv_ref[...], FFN_CONV)
        o_ref[...] = (ug * _sigmoid(ug) * uv).astype(BF16)

    return pl.pallas_call(
        body, name="ffn_act_fwd", grid=(D_FF // tc, t // tb),
        in_specs=[cur(0), prev(0), cur(1), prev(1), par(FFN_CONV, 0), par(FFN_CONV, 1), par(1, 0), par(1, 1)],
        out_specs=pl.BlockSpec((tb, tc), lambda j, i: (i, j)), out_shape=jax.ShapeDtypeStruct((t, D_FF), BF16),
        compiler_params=_cp(("parallel", "parallel")))(u0, u0, u0, u0, w, w, b, b)


def _ffn_act_bwd(u0, w, b, da):
    t = u0.shape[0]
    tb, tc = _rows(t, 256), _tile(D_FF, 1408)
    nrow = t // tb
    taps = FFN_CONV
    cur, prev, nxt, par = _ffn_specs(tb, tc, t)

    def dact(ug, uv, dav):
        sg = _sigmoid(ug)
        return dav * uv * (sg * (1.0 + ug * (1.0 - sg))), dav * ug * sg

    def body(g_ref, gp_ref, gn_ref, v_ref, vp_ref, vn_ref, wg_ref, wv_ref, bg_ref, bv_ref, da_ref, dan_ref,
             dx_ref, dw_ref, db_ref):
        i = pl.program_id(1)
        xg, xv = g_ref[...], v_ref[...]
        gp = jnp.where(i > 0, gp_ref[...], 0.0)
        vp = jnp.where(i > 0, vp_ref[...], 0.0)
        wg, wv, bg, bv = wg_ref[...], wv_ref[...], bg_ref[...], bv_ref[...]
        dug, duv = dact(_conv_apply(xg, gp, wg, bg, taps), _conv_apply(xv, vp, wv, bv, taps),
                        da_ref[...].astype(F32))
        dan = jnp.where(i < nrow - 1, dan_ref[...].astype(F32)[:8], 0.0)
        dugn, duvn = dact(_conv_apply(gn_ref[...], xg[tb - 8:], wg, bg, taps),
                          _conv_apply(vn_ref[...], xv[tb - 8:], wv, bv, taps), dan)
        dxg, dwg, dbg = _conv_grads(dug, dugn, xg, gp, wg, taps)
        dxv, dwv, dbv = _conv_grads(duv, duvn, xv, vp, wv, taps)
        dx_ref[0] = dxg.astype(BF16)
        dx_ref[1] = dxv.astype(BF16)

        @pl.when(i == 0)
        def _():
            dw_ref[0] = dwg
            dw_ref[1] = dwv
            db_ref[0] = dbg
            db_ref[1] = dbv

        @pl.when(i > 0)
        def _():
            dw_ref[0] += dwg
            dw_ref[1] += dwv
            db_ref[0] += dbg
            db_ref[1] += dbv

    da_cur = pl.BlockSpec((tb, tc), lambda j, i: (i, j))
    da_nxt = pl.BlockSpec((16, tc), lambda j, i: (jnp.minimum((i + 1) * (tb // 16), t // 16 - 1), j))
    return pl.pallas_call(
        body, name="ffn_act_bwd", grid=(D_FF // tc, nrow),
        in_specs=[cur(0), prev(0), nxt(0), cur(1), prev(1), nxt(1), par(taps, 0), par(taps, 1), par(1, 0),
                  par(1, 1), da_cur, da_nxt],
        out_specs=[pl.BlockSpec((2, tb, tc), lambda j, i: (0, i, j)),
                   pl.BlockSpec((2, taps, tc), lambda j, i: (0, 0, j)),
                   pl.BlockSpec((2, 1, tc), lambda j, i: (0, 0, j))],
        out_shape=[jax.ShapeDtypeStruct((2, t, D_FF), BF16), jax.ShapeDtypeStruct((2, taps, D_FF), F32),
                   jax.ShapeDtypeStruct((2, 1, D_FF), F32)],
        compiler_params=_cp(("parallel", "arbitrary")))(u0, u0, u0, u0, u0, u0, w, w, b, b, da, da)


def _head_masks():
    lane = _iota((1, 4 * SSD_HEAD_DIM), 1)
    return [((lane >= r * SSD_HEAD_DIM) & (lane < (r + 1) * SSD_HEAD_DIM)).astype(F32) for r in range(4)]


def _segsum(v):
    first = _iota((1, LANES), 1) < SSD_HEAD_DIM
    halves = []
    for k in range(2):
        vh = v[:, k * LANES:(k + 1) * LANES]
        both = jnp.sum(vh, axis=1, keepdims=True)
        one = jnp.sum(jnp.where(first, vh, 0.0), axis=1, keepdims=True)
        halves.append(jnp.where(first, one, both - one))
    return jnp.concatenate(halves, axis=1)


def _ssd_common(raw_e, prow, rawr4, bcol, acol):
    n = SSD_CHUNK
    dt_e = _softplus(raw_e + prow[0:1, :])
    a_e = -jnp.exp(prow[1:2, :])
    d_e = prow[2:3, :]
    tril = (_iota((n, n), 0) >= _iota((n, n), 1)).astype(F32)
    acs_e = _dot_exact(tril, dt_e * a_e)
    last_e = acs_e[n - 1:n, :]
    dtr4 = _softplus(rawr4 + bcol)
    triu = (_iota((n, n), 0) <= _iota((n, n), 1)).astype(F32)
    acs_r4 = _dot_exact(dtr4 * (-jnp.exp(acol)), triu)
    return dt_e, a_e, d_e, acs_e, last_e, acs_r4


def _decay_matrix(acs_e, acs_r4, r):
    n = SSD_CHUNK
    col = acs_e[:, r * SSD_HEAD_DIM:r * SSD_HEAD_DIM + 1]
    seg = col - acs_r4[r:r + 1, :]
    causal = _iota((n, n), 0) >= _iota((n, n), 1)
    return jnp.exp(jnp.where(causal, seg, NEG))


def _ssd_specs(t, rev):
    nc = t // SSD_CHUNK
    xb, bb, cb = 0, SSD_INNER // SSD_STATE, (SSD_INNER + BC_WIDTH) // SSD_STATE

    def ch(c):
        return (nc - 1 - c) if rev else c

    x = pl.BlockSpec((SSD_CHUNK, 256), lambda g, c: (ch(c), xb + g))
    bm = pl.BlockSpec((SSD_CHUNK, SSD_STATE), lambda g, c: (ch(c), bb + g))
    cm = pl.BlockSpec((SSD_CHUNK, SSD_STATE), lambda g, c: (ch(c), cb + g))
    dtc = pl.BlockSpec((1, SSD_CHUNK, 256), lambda g, c: (g, ch(c), 0))
    dtr = pl.BlockSpec((1, 4, SSD_CHUNK), lambda g, c: (g, 0, ch(c)))
    prow = pl.BlockSpec((1, 3, 256), lambda g, c: (g, 0, 0))
    pcol = pl.BlockSpec((1, 4, 1), lambda g, c: (g, 0, 0))
    st = pl.BlockSpec((1, 1, SSD_STATE, 256), lambda g, c: (g, ch(c), 0, 0))
    return x, bm, cm, dtc, dtr, prow, pcol, st, ch


def _ssd_params(dt_raw, dt_bias, a_log, ssd_d):
    t = dt_raw.shape[0]
    by_group = dt_raw.reshape(t, SSD_GROUPS, 4)
    dtc = jnp.repeat(by_group, SSD_HEAD_DIM, axis=2).transpose(1, 0, 2)
    dtr = by_group.transpose(1, 2, 0)
    prow = jnp.repeat(jnp.stack([dt_bias.reshape(SSD_GROUPS, 4), a_log.reshape(SSD_GROUPS, 4),
                                 ssd_d.reshape(SSD_GROUPS, 4)], axis=1), SSD_HEAD_DIM, axis=2)
    bcol = dt_bias.reshape(SSD_GROUPS, 4, 1)
    acol = a_log.reshape(SSD_GROUPS, 4, 1)
    return dtc, dtr, prow, bcol, acol


def _ssd_fwd(xbc, params):
    t = xbc.shape[0]
    nc = t // SSD_CHUNK
    dtc, dtr, prow, bcol, acol = params

    def body(x_ref, b_ref, c_ref, dtc_ref, dtr_ref, prow_ref, bcol_ref, acol_ref, y_ref, st_ref, s_scr):
        c = pl.program_id(1)

        @pl.when(c == 0)
        def _():
            s_scr[...] = jnp.zeros_like(s_scr)

        masks = _head_masks()
        dt_e, a_e, d_e, acs_e, last_e, acs_r4 = _ssd_common(
            dtc_ref[0], prow_ref[0], dtr_ref[0], bcol_ref[0], acol_ref[0])
        xv = x_ref[...]
        bm, cm = b_ref[...], c_ref[...]
        s = s_scr[...]
        st_ref[0, 0] = s
        xdt = xv * dt_e
        cb = _dot(cm, bm, 'nt')
        y = _dot(cm, s) * jnp.exp(acs_e) + xv * d_e
        for r in range(4):
            mr = cb * _decay_matrix(acs_e, acs_r4, r)
            y = y + _dot(mr, xdt * masks[r])
        y_ref[...] = y
        w = xdt * jnp.exp(last_e - acs_e)
        s_scr[...] = s * jnp.exp(last_e) + _dot(bm.T, w)

    x, bm, cm, dtcs, dtrs, prs, pcs, st, _ = _ssd_specs(t, False)
    return pl.pallas_call(
        body, name="ssd_fwd", grid=(SSD_GROUPS, nc), in_specs=[x, bm, cm, dtcs, dtrs, prs, pcs, pcs],
        out_specs=[pl.BlockSpec((SSD_CHUNK, 256), lambda g, c: (c, g)), st],
        out_shape=[jax.ShapeDtypeStruct((t, SSD_INNER), F32),
                   jax.ShapeDtypeStruct((SSD_GROUPS, nc, SSD_STATE, 256), F32)],
        scratch_shapes=[pltpu.VMEM((SSD_STATE, 256), F32)],
        compiler_params=_cp(("parallel", "arbitrary")))(xbc, xbc, xbc, dtc, dtr, prow, bcol, acol)


def _ssd_bwd(xbc, params, states, dy):
    t = xbc.shape[0]
    nc = t // SSD_CHUNK
    n = SSD_CHUNK
    dtc, dtr, prow, bcol, acol = params

    def body(x_ref, b_ref, c_ref, dtc_ref, dtr_ref, prow_ref, bcol_ref, acol_ref, st_ref, dy_ref,
             dx_ref, db_ref, dc_ref, ddt_ref, dp_ref, ds_scr):
        c = pl.program_id(1)

        @pl.when(c == 0)
        def _():
            ds_scr[...] = jnp.zeros_like(ds_scr)
            dp_ref[...] = jnp.zeros_like(dp_ref)

        masks = _head_masks()
        raw_e = dtc_ref[0]
        prw = prow_ref[0]
        dt_e, a_e, d_e, acs_e, last_e, acs_r4 = _ssd_common(raw_e, prw, dtr_ref[0], bcol_ref[0], acol_ref[0])
        xv = x_ref[...]
        bm, cm = b_ref[...], c_ref[...]
        s = st_ref[0, 0]
        ds = ds_scr[...]
        dyv = dy_ref[...]
        e_e = jnp.exp(acs_e)
        dec_e = jnp.exp(last_e - acs_e)
        cd_e = jnp.exp(last_e)
        xdt = xv * dt_e
        w = xdt * dec_e
        b16, c16, s16, ds16 = bm.astype(BF16), cm.astype(BF16), s.astype(BF16), ds.astype(BF16)
        cb = _dot(c16, b16, 'nt')
        yoff_raw = _dot(c16, s16)
        dye = dyv * e_e
        dye16 = dye.astype(BF16)
        dcm = _dot(dye16, s16, 'nt')
        ds_scr[...] = ds * cd_e + _dot(cm.T, dye16)
        dacs_e = _segsum(dyv * yoff_raw) * e_e
        dw = _dot(b16, ds16)
        dbm = _dot(w, ds16, 'nt')
        tdec = _segsum(dw * xdt) * dec_e
        dacs_e = dacs_e - tdec
        dlast_e = jnp.sum(tdec, axis=0, keepdims=True)
        dxdt = dw * dec_e
        dlast_e = dlast_e + _segsum(jnp.sum(ds * s, axis=0, keepdims=True)) * cd_e
        dcb = jnp.zeros((n, n), F32)
        for r in range(4):
            lm = _decay_matrix(acs_e, acs_r4, r)
            mr = cb * lm
            dyr16 = (dyv * masks[r]).astype(BF16)
            dm = _dot(dyr16, xdt * masks[r], 'nt')
            dcb = dcb + dm * lm
            dseg = dm * mr
            dcol = jnp.sum(dseg, axis=1, keepdims=True) - jnp.sum(dseg.T, axis=1, keepdims=True)
            dacs_e = dacs_e + dcol * masks[r]
            dxdt = dxdt + _dot(mr.T, dyr16)
        dcm = dcm + _dot(dcb, b16)
        dbm = dbm + _dot(dcb.T, c16)
        dacs_e = dacs_e + jnp.where(_iota((n, 1), 0) == n - 1, dlast_e, 0.0)
        triu = (_iota((n, n), 0) <= _iota((n, n), 1)).astype(F32)
        ddta_e = _dot_exact(triu, dacs_e)
        ddt_e = ddta_e * a_e + _segsum(dxdt * xv)
        dx_ref[...] = dxdt * dt_e + dyv * d_e
        db_ref[...] = dbm
        dc_ref[...] = dcm
        draw_e = ddt_e * _sigmoid(raw_e + prw[0:1, :])
        ddt_ref[0] = draw_e
        dbias = jnp.sum(draw_e, axis=0, keepdims=True)
        dalog = jnp.sum(ddta_e * dt_e, axis=0, keepdims=True) * a_e
        dd = _segsum(jnp.sum(dyv * xv, axis=0, keepdims=True))
        row3 = _iota((3, 1), 0)
        dp_ref[0] += (jnp.where(row3 == 0, dbias, 0.0) + jnp.where(row3 == 1, dalog, 0.0)
                      + jnp.where(row3 == 2, dd, 0.0))

    x, bm, cm, dtcs, dtrs, prs, pcs, st, ch = _ssd_specs(t, True)
    yblk = pl.BlockSpec((SSD_CHUNK, 256), lambda g, c: (ch(c), g))
    nblk = pl.BlockSpec((SSD_CHUNK, SSD_STATE), lambda g, c: (ch(c), g))
    return pl.pallas_call(
        body, name="ssd_bwd", grid=(SSD_GROUPS, nc),
        in_specs=[x, bm, cm, dtcs, dtrs, prs, pcs, pcs, st, yblk],
        out_specs=[yblk, nblk, nblk, dtcs, prs],
        out_shape=[jax.ShapeDtypeStruct((t, SSD_INNER), F32), jax.ShapeDtypeStruct((t, BC_WIDTH), F32),
                   jax.ShapeDtypeStruct((t, BC_WIDTH), F32), jax.ShapeDtypeStruct((SSD_GROUPS, t, 256), F32),
                   jax.ShapeDtypeStruct((SSD_GROUPS, 3, 256), F32)],
        scratch_shapes=[pltpu.VMEM((SSD_STATE, 256), F32)],
        compiler_params=_cp(("parallel", "arbitrary")))(xbc, xbc, xbc, dtc, dtr, prow, bcol, acol, states, dy)


GROUP_W = SSD_INNER // SSD_GROUPS


def _mix_specs(tb):
    row = pl.BlockSpec((tb, 2048), lambda i: (i, 0))
    zlo = pl.BlockSpec((tb, 1024), lambda i: (i, O_Z // 1024))
    zhi = pl.BlockSpec((tb, 1024), lambda i: (i, O_Z // 1024 + 1))
    vec = pl.BlockSpec((1, 2048), lambda i: (0, 0))
    return row, zlo, zhi, vec


def _mix_fwd(attn, y, proj, g_attn, g_ssd):
    t = attn.shape[0]
    tb = _rows(t, 256)

    def body(a_ref, y_ref, zlo_ref, zhi_ref, ga_ref, gs_ref, o_ref):
        av = a_ref[...]
        r = lax.rsqrt(jnp.mean(av * av, axis=-1, keepdims=True) + EPS)
        o_ref[:, :ATTN_WIDTH] = (av * r * ga_ref[...]).astype(BF16)
        for g in range(SSD_GROUPS):
            lo, hi = g * GROUP_W, (g + 1) * GROUP_W
            zref = zlo_ref if g < 4 else zhi_ref
            z = zref[:, lo % 1024:lo % 1024 + GROUP_W]
            yg = y_ref[:, lo:hi] * (z * _sigmoid(z))
            rg = lax.rsqrt(jnp.mean(yg * yg, axis=-1, keepdims=True) + EPS)
            o_ref[:, ATTN_WIDTH + lo:ATTN_WIDTH + hi] = (yg * rg * gs_ref[:, lo:hi]).astype(BF16)

    row, zlo, zhi, vec = _mix_specs(tb)
    return pl.pallas_call(
        body, name="mix_fwd", grid=(t // tb,), in_specs=[row, row, zlo, zhi, vec, vec],
        out_specs=pl.BlockSpec((tb, 4096), lambda i: (i, 0)), out_shape=jax.ShapeDtypeStruct((t, 4096), BF16),
        compiler_params=_cp(("parallel",)))(attn, y, proj, proj, g_attn, g_ssd)


def _mix_bwd(dmix, attn, y, proj, g_attn, g_ssd):
    t = attn.shape[0]
    tb = _rows(t, 256)

    def body(dm_ref, a_ref, y_ref, zlo_ref, zhi_ref, ga_ref, gs_ref, da_ref, dy_ref, dz_ref, dga_ref, dgs_ref):
        i = pl.program_id(0)
        av = a_ref[...]
        dn = dm_ref[:, :ATTN_WIDTH].astype(F32)
        r = lax.rsqrt(jnp.mean(av * av, axis=-1, keepdims=True) + EPS)
        u = dn * ga_ref[...]
        da_ref[...] = r * u - av * (r * r * r * jnp.mean(u * av, axis=-1, keepdims=True))
        dga = jnp.sum(dn * av * r, axis=0, keepdims=True)

        @pl.when(i == 0)
        def _():
            dga_ref[...] = dga

        @pl.when(i > 0)
        def _():
            dga_ref[...] += dga

        for g in range(SSD_GROUPS):
            lo, hi = g * GROUP_W, (g + 1) * GROUP_W
            zref = zlo_ref if g < 4 else zhi_ref
            z = zref[:, lo % 1024:lo % 1024 + GROUP_W]
            yv = y_ref[:, lo:hi]
            sg = _sigmoid(z)
            sz = z * sg
            yg = yv * sz
            rg = lax.rsqrt(jnp.mean(yg * yg, axis=-1, keepdims=True) + EPS)
            do = dm_ref[:, ATTN_WIDTH + lo:ATTN_WIDTH + hi].astype(F32)
            ug = do * gs_ref[:, lo:hi]
            dyg = rg * ug - yg * (rg * rg * rg * jnp.mean(ug * yg, axis=-1, keepdims=True))
            dy_ref[:, lo:hi] = dyg * sz
            dz_ref[:, lo:hi] = (dyg * yv * (sg * (1.0 + z * (1.0 - sg)))).astype(BF16)
            dgs = jnp.sum(do * yg * rg, axis=0, keepdims=True)

            @pl.when(i == 0)
            def _():
                dgs_ref[:, lo:hi] = dgs

            @pl.when(i > 0)
            def _():
                dgs_ref[:, lo:hi] += dgs

    row, zlo, zhi, vec = _mix_specs(tb)
    return pl.pallas_call(
        body, name="mix_bwd", grid=(t // tb,),
        in_specs=[pl.BlockSpec((tb, 4096), lambda i: (i, 0)), row, row, zlo, zhi, vec, vec],
        out_specs=[row, row, row, vec, vec],
        out_shape=[jax.ShapeDtypeStruct((t, 2048), F32), jax.ShapeDtypeStruct((t, 2048), F32),
                   jax.ShapeDtypeStruct((t, 2048), BF16), jax.ShapeDtypeStruct((1, 2048), F32),
                   jax.ShapeDtypeStruct((1, 2048), F32)],
        compiler_params=_cp(("arbitrary",)))(dmix, attn, y, proj, proj, g_attn, g_ssd)


def _adamw(w, g, m, v, name):
    r, c = w.shape
    tb = _rows(r, 256)
    c1 = 1.0 - ADAM_B1 ** ADAM_STEP
    c2 = 1.0 - ADAM_B2 ** ADAM_STEP

    def body(w_ref, g_ref, m_ref, v_ref, d_ref, m2_ref, v2_ref):
        gv = g_ref[...]
        m2 = ADAM_B1 * m_ref[...] + (1.0 - ADAM_B1) * gv
        v2 = ADAM_B2 * v_ref[...] + (1.0 - ADAM_B2) * (gv * gv)
        d_ref[...] = -ADAM_LR * ((m2 / c1) / (jnp.sqrt(v2 / c2) + ADAM_EPS) + ADAM_WD * w_ref[...])
        m2_ref[...] = m2
        v2_ref[...] = v2

    blk = pl.BlockSpec((tb, c), lambda i: (i, 0))
    shp = jax.ShapeDtypeStruct((r, c), F32)
    return pl.pallas_call(body, name=name, grid=(r // tb,), in_specs=[blk] * 4, out_specs=[blk] * 3,
                          out_shape=[shp] * 3, compiler_params=_cp(("parallel",)))(w, g, m, v)


def _adamw_halves(w, mine, theirs, m, v, pos, name):
    r, c = w.shape
    h = r // 2
    tb = _rows(h, 128)
    nh = h // tb
    c1 = 1.0 - ADAM_B1 ** ADAM_STEP
    c2 = 1.0 - ADAM_B2 ** ADAM_STEP

    def body(pos_ref, w_ref, a_ref, b_ref, m_ref, v_ref, g_ref, d_ref, m2_ref, v2_ref):
        own_rows = (pl.program_id(0) // nh) == pos_ref[0]
        gv = jnp.where(own_rows, a_ref[...], b_ref[...])
        m2 = ADAM_B1 * m_ref[...] + (1.0 - ADAM_B1) * gv
        v2 = ADAM_B2 * v_ref[...] + (1.0 - ADAM_B2) * (gv * gv)
        g_ref[...] = gv
        d_ref[...] = -ADAM_LR * ((m2 / c1) / (jnp.sqrt(v2 / c2) + ADAM_EPS) + ADAM_WD * w_ref[...])
        m2_ref[...] = m2
        v2_ref[...] = v2

    full = pl.BlockSpec((tb, c), lambda i, pref: (i, 0))
    half = pl.BlockSpec((tb, c), lambda i, pref: (i % nh, 0))
    shp = jax.ShapeDtypeStruct((r, c), F32)
    grid_spec = pltpu.PrefetchScalarGridSpec(num_scalar_prefetch=1, grid=(r // tb,),
                                             in_specs=[full, half, half, full, full], out_specs=[full] * 4)
    return pl.pallas_call(body, name=name, grid_spec=grid_spec, out_shape=[shp] * 4,
                          compiler_params=_cp(("parallel",)))(pos, w, mine, theirs, m, v)


def _sum_own_half(g4, recv, pos, name):
    _, r, c = g4.shape
    h = r // 2
    tb = _rows(h, 128)
    nh = h // tb

    def body(pos_ref, a_ref, b_ref, o_ref):
        o_ref[...] = (a_ref[...] + b_ref[...]).astype(BF16)

    grid_spec = pltpu.PrefetchScalarGridSpec(
        num_scalar_prefetch=1, grid=(N_CHIPS, nh),
        in_specs=[pl.BlockSpec((1, tb, c), lambda j, i, pref: (j, pref[0] * nh + i, 0)),
                  pl.BlockSpec((1, tb, c), lambda j, i, pref: (j, i, 0))],
        out_specs=pl.BlockSpec((1, tb, c), lambda j, i, pref: (j, i, 0)))
    return pl.pallas_call(body, name=name, grid_spec=grid_spec,
                          out_shape=jax.ShapeDtypeStruct((N_CHIPS, h, c), BF16),
                          compiler_params=_cp(("parallel", "parallel")))(pos, g4, recv)


def _sum_chips(g4, recv, parts, pos, name):
    _, r, c = g4.shape
    h = r // 2
    tb = _rows(h, 128)
    nh = h // tb

    def body(pos_ref, a_ref, b_ref, p_ref, o_ref):
        own = a_ref[0] + b_ref[0]
        o_ref[...] = ((own + p_ref[0].astype(F32)) + p_ref[1].astype(F32)) + p_ref[2].astype(F32)

    grid_spec = pltpu.PrefetchScalarGridSpec(
        num_scalar_prefetch=1, grid=(nh,),
        in_specs=[pl.BlockSpec((1, tb, c), lambda i, pref: (pref[1], pref[0] * nh + i, 0)),
                  pl.BlockSpec((1, tb, c), lambda i, pref: (pref[1], i, 0)),
                  pl.BlockSpec((3, tb, c), lambda i, pref: (0, i, 0))],
        out_specs=pl.BlockSpec((tb, c), lambda i, pref: (i, 0)))
    return pl.pallas_call(body, name=name, grid_spec=grid_spec, out_shape=jax.ShapeDtypeStruct((h, c), F32),
                          compiler_params=_cp(("parallel",)))(pos, g4, recv, parts)


def _me():
    return lax.axis_index("x"), lax.axis_index("y"), lax.axis_index("c")


def _flip(v, bit):
    return (1 - v) if bit else v


CHIP_FLIPS = [(1, 0), (0, 1), (1, 1)]


def _allgather_weights(shards):
    n = len(shards)

    def body(*refs):
        ins, outs = refs[:n], refs[n:2 * n]
        send_sems, recv_sems = refs[2 * n:]
        x, y, c = _me()
        chip = 2 * x + y
        sib = (x, y, 1 - c)

        def remote(src, dst, k, to):
            return pltpu.make_async_remote_copy(src_ref=src, dst_ref=dst, send_sem=send_sems.at[k],
                                                recv_sem=recv_sems.at[k], device_id=to, device_id_type=MESH)

        sends = []
        for t in range(n):
            h = ins[t].shape[0] // 2
            mine = pl.ds(c * h, h)
            for k, (fx, fy) in enumerate(CHIP_FLIPS):
                cp = remote(ins[t].at[mine], outs[t].at[chip, mine], 6 * t + k, (_flip(x, fx), _flip(y, fy), c))
                cp.start()
                sends.append(cp)
        for t in range(n):
            h = ins[t].shape[0] // 2
            mine = pl.ds(c * h, h)
            for k, (fx, fy) in enumerate(CHIP_FLIPS):
                src_chip = 2 * _flip(x, fx) + _flip(y, fy)
                landed = outs[t].at[src_chip, mine]
                remote(landed, landed, 6 * t + k, (x, y, c)).wait_recv()
                fw = remote(landed, landed, 6 * t + 3 + k, sib)
                fw.start()
                sends.append(fw)
        for t in range(n):
            h = ins[t].shape[0] // 2
            other = pl.ds((1 - c) * h, h)
            for k, (fx, fy) in enumerate(CHIP_FLIPS):
                src_chip = 2 * _flip(x, fx) + _flip(y, fy)
                got = outs[t].at[src_chip, other]
                remote(got, got, 6 * t + 3 + k, (x, y, c)).wait_recv()
        for cp in sends:
            cp.wait_send()

    return pl.pallas_call(
        body, name="allgather_weights", in_specs=[HBM_SPEC] * n, out_specs=[HBM_SPEC] * n,
        out_shape=[jax.ShapeDtypeStruct((N_CHIPS,) + s.shape, s.dtype) for s in shards],
        scratch_shapes=[pltpu.SemaphoreType.DMA((6 * n,)), pltpu.SemaphoreType.DMA((6 * n,))],
        compiler_params=pltpu.CompilerParams(has_side_effects=True))(*shards)


def _exchange_halves(g4s, name):
    n = len(g4s)

    def body(*refs):
        ins, outs = refs[:n], refs[n:2 * n]
        send_sems, recv_sems = refs[2 * n:]
        x, y, c = _me()
        cps = []
        for t in range(n):
            h = ins[t].shape[1] // 2
            cp = pltpu.make_async_remote_copy(
                src_ref=ins[t].at[:, pl.ds((1 - c) * h, h)], dst_ref=outs[t], send_sem=send_sems.at[t],
                recv_sem=recv_sems.at[t], device_id=(x, y, 1 - c), device_id_type=MESH)
            cp.start()
            cps.append(cp)
        for cp in cps:
            cp.wait()

    return pl.pallas_call(
        body, name=name, in_specs=[HBM_SPEC] * n, out_specs=[HBM_SPEC] * n,
        out_shape=[jax.ShapeDtypeStruct((N_CHIPS, g.shape[1] // 2, g.shape[2]), g.dtype) for g in g4s],
        scratch_shapes=[pltpu.SemaphoreType.DMA((n,)), pltpu.SemaphoreType.DMA((n,))],
        compiler_params=pltpu.CompilerParams(has_side_effects=True))(*g4s)


def _share_halves(ghs, name):
    n = len(ghs)

    def body(*refs):
        ins, outs = refs[:n], refs[n:2 * n]
        send_sems, recv_sems = refs[2 * n:]
        x, y, c = _me()
        cps = []
        for t in range(n):
            cp = pltpu.make_async_remote_copy(
                src_ref=ins[t], dst_ref=outs[t], send_sem=send_sems.at[t], recv_sem=recv_sems.at[t],
                device_id=(x, y, 1 - c), device_id_type=MESH)
            cp.start()
            cps.append(cp)
        for cp in cps:
            cp.wait()

    return pl.pallas_call(
        body, name=name, in_specs=[HBM_SPEC] * n, out_specs=[HBM_SPEC] * n,
        out_shape=[jax.ShapeDtypeStruct(g.shape, g.dtype) for g in ghs],
        scratch_shapes=[pltpu.SemaphoreType.DMA((n,)), pltpu.SemaphoreType.DMA((n,))],
        compiler_params=pltpu.CompilerParams(has_side_effects=True))(*ghs)


SEM_SPEC = pl.BlockSpec(memory_space=pltpu.SEMAPHORE)
ANY_SPEC = pl.BlockSpec(memory_space=pl.ANY)
DATAFLOW = pltpu.SideEffectType.DATAFLOW_SIDE_EFFECTING
TOKEN = jax.ShapeDtypeStruct((8, LANES), F32)


def _in_hbm(a):
    return pltpu.with_memory_space_constraint(a, pltpu.HBM)


def _push_start(srcs, land_shapes, route, peers, name):
    n, npeer = len(srcs), len(peers)
    lands = [lax.empty(shp, s.dtype) for shp, s in zip(land_shapes, srcs)]

    def body(*refs):
        ins, lnd = refs[:n], refs[n:2 * n]
        send_sems, recv_sems = refs[2 * n], refs[2 * n + 1]
        token = refs[-1]
        x, y, c = _me()
        for t in range(n):
            for k, (fx, fy, fc) in enumerate(peers):
                src, dst = route(ins[t], lnd[t], k, x, y, c)
                pltpu.make_async_remote_copy(
                    src_ref=src, dst_ref=dst, send_sem=send_sems.at[npeer * t + k],
                    recv_sem=recv_sems.at[npeer * t + k],
                    device_id=(_flip(x, fx), _flip(y, fy), _flip(c, fc)), device_id_type=MESH).start()
        token[...] = jnp.zeros_like(token)

    bufs = [_in_hbm(a) for a in list(srcs) + lands]
    outs = pl.pallas_call(
        body, name=name,
        out_shape=(pltpu.SemaphoreType.DMA((npeer * n,)), pltpu.SemaphoreType.DMA((npeer * n,)),
                   *[pltpu.HBM(b.shape, b.dtype) for b in bufs], TOKEN),
        in_specs=[HBM_SPEC] * (2 * n),
        out_specs=(SEM_SPEC, SEM_SPEC, *[HBM_SPEC] * (2 * n), pl.BlockSpec(memory_space=pltpu.VMEM)),
        input_output_aliases={i: 2 + i for i in range(2 * n)},
        compiler_params=pltpu.CompilerParams(has_side_effects=DATAFLOW))(*bufs)
    return outs[0], outs[1], list(outs[2:2 + n]), list(outs[2 + n:2 + 2 * n]), outs[-1]


def _push_wait(send_sems, recv_sems, srcs, lands, after, route, peers, name):
    n, npeer = len(srcs), len(peers)

    def body(*refs):
        ins, lnd = refs[:n], refs[n:2 * n]
        ssem, rsem = refs[2 * n], refs[2 * n + 1]
        x, y, c = _me()
        for t in range(n):
            for k, (fx, fy, fc) in enumerate(peers):
                src, dst = route(ins[t], lnd[t], k, x, y, c)
                cp = pltpu.make_async_remote_copy(
                    src_ref=src, dst_ref=dst, send_sem=ssem.at[npeer * t + k], recv_sem=rsem.at[npeer * t + k],
                    device_id=(_flip(x, fx), _flip(y, fy), _flip(c, fc)), device_id_type=MESH)
                cp.wait_send()
                cp.wait_recv()

    bufs = list(srcs) + list(lands)
    outs = pl.pallas_call(
        body, name=name, out_shape=tuple(pltpu.HBM(b.shape, b.dtype) for b in bufs),
        in_specs=[HBM_SPEC] * (2 * n) + [SEM_SPEC, SEM_SPEC, ANY_SPEC], out_specs=tuple([HBM_SPEC] * (2 * n)),
        input_output_aliases={i: i for i in range(2 * n)},
        compiler_params=pltpu.CompilerParams(has_side_effects=DATAFLOW))(*bufs, send_sems, recv_sems, after)
    return list(outs[:n]), list(outs[n:])


OTHER_CHIPS = [(fx, fy, 0) for fx, fy in CHIP_FLIPS]
SIBLING = [(0, 0, 1)]


def _route_gather(src, land, k, x, y, c):
    return src, land.at[2 * x + y]


def _route_gather_wait(src, land, k, x, y, c):
    fx, fy = CHIP_FLIPS[k]
    return src, land.at[2 * _flip(x, fx) + _flip(y, fy)]


def _route_scatter(src, land, k, x, y, c):
    fx, fy = CHIP_FLIPS[k]
    return src.at[2 * _flip(x, fx) + _flip(y, fy)], land.at[k]


def _route_exchange(src, land, k, x, y, c):
    h = land.shape[1]
    return src.at[:, pl.ds((1 - c) * h, h)], land


def _allreduce_small(v):
    r = v.shape[0]

    def body(v_ref, o_ref, buf, send_sems, recv_sems):
        x, y, c = _me()
        me = 4 * x + 2 * y + c
        buf[0] = v_ref[...]
        cps = []
        for k in range(1, 8):
            kx, ky, kc = (k >> 2) & 1, (k >> 1) & 1, k & 1
            cp = pltpu.make_async_remote_copy(
                src_ref=v_ref, dst_ref=buf.at[k], send_sem=send_sems.at[k - 1], recv_sem=recv_sems.at[k - 1],
                device_id=(_flip(x, kx), _flip(y, ky), _flip(c, kc)), device_id_type=MESH)
            cp.start()
            cps.append(cp)
        for cp in cps:
            cp.wait()
        acc = buf[me]
        for d in range(1, 8):
            acc = acc + buf[jnp.bitwise_xor(me, d)]
        o_ref[...] = acc

    vm = pl.BlockSpec(memory_space=pltpu.VMEM)
    return pl.pallas_call(
        body, name="allreduce_small", in_specs=[vm], out_specs=vm, out_shape=jax.ShapeDtypeStruct(v.shape, F32),
        scratch_shapes=[pltpu.VMEM((8, r, LANES), F32), pltpu.SemaphoreType.DMA((7,)),
                        pltpu.SemaphoreType.DMA((7,))],
        compiler_params=pltpu.CompilerParams(has_side_effects=True, vmem_limit_bytes=VMEM_LIMIT))(v)


def _grad_exchange_start(g4, tag):
    land = (N_CHIPS, g4.shape[1] // 2, g4.shape[2])
    send_sems, recv_sems, srcs, lands, token = _push_start(
        [g4], [land], _route_exchange, SIBLING, name="grad_exchange_start_" + tag)
    return (send_sems, recv_sems, srcs, lands, tag), token


def _grad_scatter_start(state, pos, after):
    send_sems, recv_sems, srcs, lands, tag = state
    (g4,), (recv,) = _push_wait(send_sems, recv_sems, srcs, lands, after, _route_exchange, SIBLING,
                                name="grad_exchange_wait_" + tag)
    return _grad_pair_scatter(g4, recv, pos, tag)


def _grad_pair_scatter(g4, recv, pos, tag):
    p16 = _sum_own_half(g4, recv, pos, name="grad_sum_pair_" + tag)
    send_sems, recv_sems, srcs, lands, token = _push_start(
        [p16], [(3,) + p16.shape[1:]], _route_scatter, OTHER_CHIPS, name="grad_scatter_start_" + tag)
    return (g4, recv, send_sems, recv_sems, srcs, lands, tag), token


def _grad_reduce_begin(g4, pos, tag):
    recv = _exchange_halves([g4], name="grad_exchange_halves_" + tag)[0]
    return _grad_pair_scatter(g4, recv, pos, tag)


def _grad_reduce_finish(state, pos, after):
    g4, recv, send_sems, recv_sems, srcs, lands, tag = state
    parts = _push_wait(send_sems, recv_sems, srcs, lands, after, _route_scatter, OTHER_CHIPS,
                       name="grad_scatter_wait_" + tag)[1][0]
    mine = _sum_chips(g4, recv, parts, pos, name="grad_sum_chips_" + tag)
    return mine, _share_halves([mine], name="grad_share_halves_" + tag)[0]


def _local_step(x, tgt, p, w_in_main, w_in_dt, hooks):
    t = x.shape[0]
    tables = _rope_tables(t)
    sinks = p['sinks'].reshape(N_Q_HEADS)

    def told(name, value):
        return tuple(hooks.grad_ready(name, value))

    xn = _rmsnorm_fwd(x, p['norm_mix'], "norm_mix_fwd")
    proj = _matmul(xn, w_in_main, mode='nn', name="in_proj", deps=hooks.first_deps)
    dt_raw = _matmul(xn, w_in_dt, mode='nn', name="in_proj_dt")[:, :SSD_HEADS]
    attn = _attn_fwd(proj, sinks, tables)
    conv_b = p['ssd_conv_b']
    xbc = _conv_fwd(proj, p['ssd_conv_w'], conv_b, col0=O_XBC, width=CONV_CH, act=True, name="ssd_conv_fwd")
    sp = _ssd_params(dt_raw, p['dt_bias'].reshape(-1), p['a_log'].reshape(-1), p['ssd_d'].reshape(-1))
    y, states = _ssd_fwd(xbc, sp)
    mix = _mix_fwd(attn, y, proj, p['attn_out_norm'], p['ssd_norm'])
    w_out = hooks.weight('w_out', mix)
    h1 = _matmul(mix, w_out, mode='nn', name="out_proj", add=x)
    hn = _rmsnorm_fwd(h1, p['norm_ffn'], "norm_ffn_fwd")
    w_up = hooks.weight('w_up', hn)
    u0 = _matmul(hn, w_up, mode='nn', name="ffn_up", b_owner=True, tn=1408)
    a = _ffn_act_fwd(u0, p['ffn_conv_w'], p['ffn_conv_b'])
    w_down = hooks.weight('w_down', a)
    h2 = _matmul(a, w_down, mode='nn', name="ffn_down", add=h1, tk=1408)
    loss, dh2, g_norm_final = _final_loss(h2, p['norm_final'].reshape(1, D_MODEL), tgt)

    g = {}
    da = _matmul(dh2, w_down, mode='nt', name="ffn_down_dx", out_dtype=BF16, tn=1408)
    g['w_down'] = _matmul(a, dh2, mode='tn', name="ffn_down_dw", tm=1408, tk=1024)
    dep = told('w_down', g['w_down'])
    du0, dcw, dcb = _ffn_act_bwd(u0, p['ffn_conv_w'], p['ffn_conv_b'], da)
    g['ffn_conv_w'] = dcw.transpose(1, 0, 2).reshape(FFN_CONV, 2 * D_FF)
    g['ffn_conv_b'] = dcb.transpose(1, 0, 2).reshape(1, 2 * D_FF)
    g['w_up'] = _matmul(hn, du0, mode='tn', name="ffn_up_dw", deps=dep, b_halves=True, owner_major=True,
                        tn=1408)
    dep = told('w_up', g['w_up'])
    dhn = _matmul(du0, w_up, mode='nt', name="ffn_up_dx", out_dtype=BF16, deps=dep, a_halves=True,
                  b_owner=True, tk=2816)
    dh1, g['norm_ffn'] = _rmsnorm_bwd(h1, p['norm_ffn'], dhn, dh2, "norm_ffn_bwd")

    g['w_out'] = _matmul(mix, dh1, mode='tn', name="out_proj_dw")
    dep = told('w_out', g['w_out'])
    dmix = _matmul(dh1, w_out, mode='nt', name="out_proj_dx", out_dtype=BF16, deps=dep)
    dattn, dy, dz, g['attn_out_norm'], g['ssd_norm'] = _mix_bwd(dmix, attn, y, proj, p['attn_out_norm'],
                                                                p['ssd_norm'])
    dq, dk, dv, dsink = _attn_bwd(proj, sinks, tables, dattn)
    g['sinks'] = dsink[:, :, 0].reshape(1, N_Q_HEADS)
    dxs, dbm, dcm, ddt8, dpar = _ssd_bwd(xbc, sp, states, dy)
    dpar = dpar[:, :, ::SSD_HEAD_DIM]
    g['dt_bias'] = dpar[:, 0, :].reshape(1, SSD_HEADS)
    g['a_log'] = dpar[:, 1, :].reshape(1, SSD_HEADS)
    g['ssd_d'] = dpar[:, 2, :].reshape(1, SSD_HEADS)
    dxbc_act = jnp.concatenate([dxs, dbm, dcm], axis=1)
    dconv = _conv_silu_dact(proj, p['ssd_conv_w'], conv_b, dxbc_act, col0=O_XBC, width=CONV_CH,
                            name="ssd_conv_dact")
    dxbc, g['ssd_conv_w'], g['ssd_conv_b'] = _conv_bwd(proj, p['ssd_conv_w'], dconv, col0=O_XBC, width=CONV_CH,
                                                       name="ssd_conv_bwd")
    dproj = jnp.concatenate([dq, dk, dv, dz, dxbc], axis=1)
    ddt = ddt8[:, :, ::SSD_HEAD_DIM].transpose(1, 0, 2).reshape(t, SSD_HEADS)
    ddt_pad = jnp.pad(ddt, ((0, 0), (0, LANES - SSD_HEADS))).astype(BF16)
    g['w_in'] = (_matmul(xn, dproj, mode='tn', name="in_proj_dw"),
                 _matmul(xn, ddt_pad, mode='tn', name="in_proj_dt_dw"))
    dep = told('w_in', g['w_in'])
    dxn_dt = _matmul(ddt_pad, w_in_dt, mode='nt', name="in_proj_dt_dx", deps=dep)
    dxn = _matmul(dproj, w_in_main, mode='nt', name="in_proj_dx", out_dtype=BF16, add=dxn_dt, tk=2304)
    dx, g['norm_mix'] = _rmsnorm_bwd(x, p['norm_mix'], dxn, dh1, "norm_mix_bwd")
    g['norm_final'] = g_norm_final
    return loss, dx, g


def _pack(arrs):
    flat = jnp.concatenate([a.reshape(-1) for a in arrs])
    n = flat.shape[0]
    rows = -(-n // LANES)
    rows = -(-rows // 8) * 8
    return jnp.pad(flat, (0, rows * LANES - n)).reshape(rows, LANES)


def _unpack(packed, shapes):
    flat = packed.reshape(-1)
    out, off = [], 0
    for s in shapes:
        n = 1
        for d in s:
            n *= d
        out.append(flat[off:off + n].reshape(s))
        off += n
    return out


def _whole_from_cols(gathered, own, chip):
    return jnp.concatenate([jnp.where(chip == j, own, gathered[j]) for j in range(N_CHIPS)], axis=1)


class _StepHooks:
    def __init__(self, first_deps, weight, grad_ready):
        self.first_deps = first_deps
        self.weight = weight
        self.grad_ready = grad_ready


def kernel(x, norm_mix, w_in, sinks, attn_out_norm, ssd_conv_w, ssd_conv_b, dt_bias, a_log, ssd_d, ssd_norm, w_out, norm_ffn, w_up, ffn_conv_w, ffn_conv_b, w_down, norm_final, loss_target, m_norm_mix, m_w_in, m_sinks, m_attn_out_norm, m_ssd_conv_w, m_ssd_conv_b, m_dt_bias, m_a_log, m_ssd_d, m_ssd_norm, m_w_out, m_norm_ffn, m_w_up, m_ffn_conv_w, m_ffn_conv_b, m_w_down, m_norm_final, v_norm_mix, v_w_in, v_sinks, v_attn_out_norm, v_ssd_conv_w, v_ssd_conv_b, v_dt_bias, v_a_log, v_ssd_d, v_ssd_norm, v_w_out, v_norm_ffn, v_w_up, v_ffn_conv_w, v_ffn_conv_b, v_w_down, v_norm_final):
    args = dict(locals())
    w = {n: args[n] for n in WEIGHTS}
    m = {n: args['m_' + n] for n in WEIGHTS}
    v = {n: args['v_' + n] for n in WEIGHTS}
    xi, yi, ci = _me()
    chip = 2 * xi + yi
    pos = jnp.stack([ci, chip]).astype(jnp.int32)

    def place(shard, full_cols):
        z = jnp.zeros((shard.shape[0], full_cols), F32)
        return lax.dynamic_update_slice(z, shard * 0.5, (0, chip * shard.shape[1]))

    conv_pack = _pack([place(ssd_conv_w[0], CONV_CH), place(ffn_conv_w[0], 2 * D_FF)])
    conv_full = _allreduce_small(conv_pack)
    ssd_conv_w_full, ffn_conv_w_full = _unpack(conv_full, [(SSD_CONV, CONV_CH), (FFN_CONV, 2 * D_FF)])

    in_shard = w_in[0].astype(BF16)
    full_in = _whole_from_cols(_allgather_weights([in_shard])[0], in_shard, chip)
    w_in_main = full_in[:, :MAIN_WIDTH]
    w_in_dt = jnp.pad(full_in[:, MAIN_WIDTH:], ((0, 0), (0, LANES - SSD_HEADS)))
    gathers = {}
    first_deps = []
    for n, shard in (('w_out', w_out[0]), ('w_up', w_up[0]), ('w_down', w_down[0])):
        shard = shard.astype(BF16)
        gathers[n] = _push_start([shard], [(N_CHIPS,) + shard.shape], _route_gather, OTHER_CHIPS,
                                 name="gather_start_" + n)
        first_deps.append(gathers[n][4])

    def weight(name, after):
        send_sems, recv_sems, srcs, lands, _ = gathers[name]
        (own,), (got,) = _push_wait(send_sems, recv_sems, srcs, lands, after, _route_gather_wait, OTHER_CHIPS,
                                    name="gather_wait_" + name)
        whole = lax.dynamic_update_slice(got, own[None], (chip, 0, 0))
        return whole if name == 'w_up' else whole.reshape(-1, D_MODEL)

    reductions, exchanging = {}, {}

    def grad_ready(name, value):
        if name == 'w_in':
            main, dtp = value
            cw = IN_PROJ_WIDTH // N_CHIPS
            g4 = jnp.stack([main[:, j * cw:(j + 1) * cw] for j in range(N_CHIPS - 1)]
                           + [jnp.concatenate([main[:, (N_CHIPS - 1) * cw:], dtp[:, :SSD_HEADS]], axis=1)])
        else:
            g4 = value if value.ndim == 3 else value.reshape(N_CHIPS, -1, value.shape[1])
        tokens = []
        for prev in list(exchanging):
            reductions[prev], token = _grad_scatter_start(exchanging.pop(prev), pos, g4)
            tokens.append(token)
        if name == 'w_in':
            reductions[name], token = _grad_reduce_begin(g4, pos, name)
        else:
            exchanging[name], token = _grad_exchange_start(g4, name)
        return tokens + [token]

    small = {
        'norm_mix': norm_mix, 'sinks': sinks, 'attn_out_norm': attn_out_norm, 'ssd_conv_w': ssd_conv_w_full,
        'ssd_conv_b': ssd_conv_b, 'dt_bias': dt_bias, 'a_log': a_log, 'ssd_d': ssd_d, 'ssd_norm': ssd_norm,
        'norm_ffn': norm_ffn, 'ffn_conv_w': ffn_conv_w_full, 'ffn_conv_b': ffn_conv_b, 'norm_final': norm_final,
    }
    loss, dx, g = _local_step(x[0], loss_target[0], small, w_in_main, w_in_dt,
                              _StepHooks(tuple(first_deps), weight, grad_ready))
    gbig = {n: _grad_reduce_finish(reductions[n], pos, dx) for n in ('w_down', 'w_up', 'w_out', 'w_in')}

    small_names = [n for n in WEIGHTS if n not in BIG]
    small_g = [loss[:, :1]] + [g[n] for n in small_names]
    small_shapes = [(1, 1)] + [tuple(a.shape) for a in small_g[1:]]
    red = _unpack(_allreduce_small(_pack(small_g)), small_shapes)
    loss_out = red[0].reshape(())
    gsm = dict(zip(small_names, red[1:]))
    gsm['ssd_conv_w'] = lax.dynamic_slice(gsm['ssd_conv_w'], (0, chip * ssd_conv_w.shape[2]),
                                          (SSD_CONV, ssd_conv_w.shape[2]))
    gsm['ffn_conv_w'] = lax.dynamic_slice(gsm['ffn_conv_w'], (0, chip * ffn_conv_w.shape[2]),
                                          (FFN_CONV, ffn_conv_w.shape[2]))

    grads, deltas, new_m, new_v = {}, {}, {}, {}
    for n in BIG:
        mine, theirs = gbig[n]
        gg, d, m2, v2 = _adamw_halves(w[n][0], mine, theirs, m[n][0], v[n][0], pos, name="adamw_" + n)
        grads[n], deltas[n], new_m[n], new_v[n] = gg[None], d[None], m2[None], v2[None]
    shapes = [tuple(w[n].shape) for n in small_names]
    gp = _pack([gsm[n] for n in small_names])
    d, m2, v2 = _adamw(_pack([w[n] for n in small_names]), gp, _pack([m[n] for n in small_names]),
                       _pack([v[n] for n in small_names]), name="adamw_small")
    for n, gg, dd, mm, vv in zip(small_names, _unpack(gp, shapes), _unpack(d, shapes), _unpack(m2, shapes),
                                 _unpack(v2, shapes)):
        grads[n], deltas[n], new_m[n], new_v[n] = gg, dd, mm, vv

    return (loss_out, dx[None], *[grads[n] for n in WEIGHTS], *[deltas[n] for n in WEIGHTS],
            *[new_m[n] for n in WEIGHTS], *[new_v[n] for n in WEIGHTS])
```

```python
import functools

import jax
import jax.numpy as jnp
from jax import lax
from jax.experimental import pallas as pl
from jax.experimental.pallas import tpu as pltpu

F32 = jnp.float32
BF16 = jnp.bfloat16

D_MODEL = 2048
N_Q_HEADS = 32
N_KV_HEADS = 8
HEAD_DIM = 64
WINDOW = 128
ATTN_BLOCK = 128
ROT_DIM = 16
ROPE_THETA = 500000.0
SSD_HEADS = 32
SSD_HEAD_DIM = 64
SSD_INNER = 2048
SSD_GROUPS = 8
SSD_STATE = 128
SSD_CONV = 4
SSD_CHUNK = 128
ATTN_WIDTH = 2048
KV_WIDTH = 512
BC_WIDTH = 1024
CONV_CH = 4096
IN_PROJ_WIDTH = 9248
MAIN_WIDTH = 9216
D_FF = 5632
FFN_CONV = 3
EPS = 1e-6
O_Q, O_K, O_V, O_Z, O_XBC, O_DT = 0, 2048, 2560, 3072, 5120, 9216

ADAM_LR = 0.001
ADAM_B1 = 0.9
ADAM_B2 = 0.999
ADAM_EPS = 1e-08
ADAM_WD = 0.01
ADAM_STEP = 10

N_CHIPS = 4
NEG = -1e30
LANES = 128
VMEM_LIMIT = 48 * 1024 * 1024
MESH = pl.DeviceIdType.MESH
HBM_SPEC = pl.BlockSpec(memory_space=pltpu.HBM)
TOKEN = jax.ShapeDtypeStruct((8, LANES), F32)

WEIGHTS = ['norm_mix', 'w_in', 'sinks', 'attn_out_norm', 'ssd_conv_w', 'ssd_conv_b', 'dt_bias', 'a_log', 'ssd_d',
           'ssd_norm', 'w_out', 'norm_ffn', 'w_up', 'ffn_conv_w', 'ffn_conv_b', 'w_down', 'norm_final']
BIG = ['w_in', 'w_out', 'w_up', 'w_down']


def _cp(sem=None, vmem=VMEM_LIMIT):
    kw = {'vmem_limit_bytes': vmem}
    if sem is not None:
        kw['dimension_semantics'] = sem
    return pltpu.CompilerParams(**kw)


def _tile(n, pref):
    if n <= pref:
        return n
    t = (pref // LANES) * LANES
    while t > LANES and n % t:
        t -= LANES
    assert n % t == 0, (n, pref)
    return t


def _rows(n, pref):
    t = min(n, pref)
    while n % t:
        t -= 8
    if 4 * t < pref:
        t = pref
        while n % t:
            t += 8
    return t


def _iota(shape, dim):
    return lax.broadcasted_iota(jnp.int32, shape, dim)


def _dot(a, b, mode='nn'):
    dn = {'nn': (((1,), (0,)), ((), ())), 'nt': (((1,), (1,)), ((), ())), 'tn': (((0,), (0,)), ((), ()))}[mode]
    return lax.dot_general(a.astype(BF16), b.astype(BF16), dn, preferred_element_type=F32)


def _dot_exact(a, b):
    return lax.dot_general(a, b, (((1,), (0,)), ((), ())), precision=lax.Precision.HIGHEST,
                           preferred_element_type=F32)


def _sigmoid(x):
    return 1.0 / (1.0 + jnp.exp(-x))


def _softplus(x):
    return jnp.maximum(x, 0.0) + jnp.log(1.0 + jnp.exp(-jnp.abs(x)))


def _matmul(a, b, *, mode, name, out_dtype=F32, add=None, deps=(), tm=1024, tn=1024, tk=2048,
            a_halves=False, b_halves=False, b_owner=False, owner_major=False, n_limit=None, k_limit=None,
            m_rows=None):
    ash, bsh = (a.shape[1:] if a_halves else a.shape), (b.shape[1:] if (b_halves or b_owner) else b.shape)
    if mode == 'nn':
        (m, k), (k2, n) = ash, bsh
    elif mode == 'nt':
        (m, k), (n, k2) = ash, bsh
    else:
        (k, m), (k2, n) = ash, bsh
    if n_limit is not None:
        assert mode == 'nt' and n_limit <= n
        n = n_limit
    if k_limit is not None:
        assert mode == 'nn' and k_limit <= k2
        k2 = k_limit
    if a_halves:
        assert mode == 'nt'
        k = 2 * k
    if b_halves:
        assert mode == 'tn'
        n = 2 * n
    if b_owner:
        assert mode in ('nn', 'nt')
        if mode == 'nn':
            n = 4 * n
        else:
            k2 = 4 * k2
    assert k == k2, (a.shape, b.shape, mode)
    tm = _tile(m, tm)
    tn = _tile(n // 4 if (owner_major or (b_owner and mode == 'nn')) else (n // 2 if b_halves else n), tn)
    tk = _tile(k // 4 if (b_owner and mode == 'nt') else (k // 2 if a_halves else k), tk)
    nk = k // tk
    has_add = add is not None
    assert not (has_add and owner_major)

    def body(*refs):
        a_ref, b_ref = refs[:2]
        add_ref = refs[2] if has_add else None

        def finish(r, o_ref):
            if has_add:
                r = r + add_ref[...].astype(F32)
            o_ref[...] = r.astype(out_dtype)

        if nk == 1:
            finish(_dot(a_ref[...], b_ref[...], mode), refs[-1])
            return
        o_ref, acc = refs[-2:]
        kk = pl.program_id(2)

        @pl.when(kk == 0)
        def _():
            acc[...] = _dot(a_ref[...], b_ref[...], mode)

        @pl.when((kk > 0) & (kk < nk - 1))
        def _():
            acc[...] += _dot(a_ref[...], b_ref[...], mode)

        @pl.when(kk == nk - 1)
        def _():
            finish(acc[...] + _dot(a_ref[...], b_ref[...], mode), o_ref)

    if mode == 'tn':
        a_spec = pl.BlockSpec((tk, tm), lambda i, j, kk: (kk, i))
    elif a_halves:
        nkh = nk // 2
        a_spec = pl.BlockSpec((None, tm, tk), lambda i, j, kk: (kk // nkh, i, kk % nkh))
    else:
        a_spec = pl.BlockSpec((tm, tk), lambda i, j, kk: (i, kk))
    if mode == 'nt' and b_owner:
        nkq = nk // 4
        b_spec = pl.BlockSpec((None, tn, tk), lambda i, j, kk: (kk // nkq, j, kk % nkq))
    elif mode == 'nt':
        b_spec = pl.BlockSpec((tn, tk), lambda i, j, kk: (j, kk))
    elif b_owner:
        njq = (n // 4) // tn
        b_spec = pl.BlockSpec((None, tk, tn), lambda i, j, kk: (j // njq, kk, j % njq))
    elif b_halves:
        njh = (n // 2) // tn
        b_spec = pl.BlockSpec((None, tk, tn), lambda i, j, kk: (j // njh, kk, j % njh))
    else:
        b_spec = pl.BlockSpec((tk, tn), lambda i, j, kk: (kk, j))
    if owner_major:
        njo = (n // 4) // tn
        o_spec = pl.BlockSpec((None, tm, tn), lambda i, j, kk: (j // njo, i, j % njo))
        out_shape = jax.ShapeDtypeStruct((N_CHIPS, m, n // 4), out_dtype)
    else:
        o_spec = pl.BlockSpec((tm, tn), lambda i, j, kk: (i, j))
        out_shape = jax.ShapeDtypeStruct((m if m_rows is None else m_rows, n), out_dtype)
    dep_spec = pl.BlockSpec((8, LANES), lambda i, j, kk: (0, 0))
    in_specs = [a_spec, b_spec] + ([pl.BlockSpec((tm, tn), lambda i, j, kk: (i, j))] if has_add else [])
    in_specs += [dep_spec] * len(deps)
    args = (a, b) + ((add,) if has_add else ()) + tuple(deps)
    return pl.pallas_call(
        body, name=name, grid=(m // tm, n // tn, nk), in_specs=in_specs, out_specs=o_spec, out_shape=out_shape,
        scratch_shapes=[pltpu.VMEM((tm, tn), F32)] if nk > 1 else [],
        compiler_params=_cp(("parallel", "parallel", "arbitrary")))(*args)


def _rmsnorm_fwd(x, g, name, deps=()):
    t, d = x.shape
    tb = _rows(t, 256)

    def body(x_ref, g_ref, *rest):
        o_ref = rest[-1]
        xv = x_ref[...]
        r = lax.rsqrt(jnp.mean(xv * xv, axis=-1, keepdims=True) + EPS)
        o_ref[...] = (xv * r * g_ref[...]).astype(BF16)

    dep_spec = pl.BlockSpec((8, LANES), lambda i: (0, 0))
    return pl.pallas_call(
        body, name=name, grid=(t // tb,),
        in_specs=[pl.BlockSpec((tb, d), lambda i: (i, 0)), pl.BlockSpec((1, d), lambda i: (0, 0))]
        + [dep_spec] * len(deps),
        out_specs=pl.BlockSpec((tb, d), lambda i: (i, 0)), out_shape=jax.ShapeDtypeStruct((t, d), BF16),
        compiler_params=_cp(("parallel",)))(x, g, *deps)


def _rmsnorm_bwd(x, g, dy, res, name):
    t, d = x.shape
    tb = _rows(t, 256)

    def body(x_ref, g_ref, dy_ref, res_ref, dx_ref, dg_ref):
        i = pl.program_id(0)
        xv = x_ref[...]
        dyv = dy_ref[...].astype(F32)
        r = lax.rsqrt(jnp.mean(xv * xv, axis=-1, keepdims=True) + EPS)
        u = dyv * g_ref[...]
        dx = r * u - xv * (r * r * r * jnp.mean(u * xv, axis=-1, keepdims=True))
        dx_ref[...] = dx + res_ref[...]
        part = jnp.sum(dyv * xv * r, axis=0, keepdims=True)

        @pl.when(i == 0)
        def _():
            dg_ref[...] = part

        @pl.when(i > 0)
        def _():
            dg_ref[...] += part

    row = pl.BlockSpec((tb, d), lambda i: (i, 0))
    vec = pl.BlockSpec((1, d), lambda i: (0, 0))
    return pl.pallas_call(
        body, name=name, grid=(t // tb,), in_specs=[row, vec, row, row], out_specs=[row, vec],
        out_shape=[jax.ShapeDtypeStruct((t, d), F32), jax.ShapeDtypeStruct((1, d), F32)],
        compiler_params=_cp(("arbitrary",)))(x, g, dy, res)


def _final_loss(h, g, tgt):
    t, d = h.shape
    tb = _rows(t, 256)

    def body(h_ref, g_ref, t_ref, loss_ref, dh_ref, dg_ref):
        i = pl.program_id(0)
        hv = h_ref[...]
        gv = g_ref[...]
        r = lax.rsqrt(jnp.mean(hv * hv, axis=-1, keepdims=True) + EPS)
        y = hv * r * gv
        diff = y - t_ref[...]
        lpart = jnp.sum(jnp.sum(diff * diff, axis=1, keepdims=True), axis=0, keepdims=True) * (0.5 / d)
        dy = diff * (1.0 / d)
        u = dy * gv
        dh_ref[...] = r * u - hv * (r * r * r * jnp.mean(u * hv, axis=-1, keepdims=True))
        gpart = jnp.sum(dy * hv * r, axis=0, keepdims=True)
        lrow = jnp.broadcast_to(lpart, (1, LANES))

        @pl.when(i == 0)
        def _():
            loss_ref[...] = lrow
            dg_ref[...] = gpart

        @pl.when(i > 0)
        def _():
            loss_ref[...] += lrow
            dg_ref[...] += gpart

    row = pl.BlockSpec((tb, d), lambda i: (i, 0))
    vec = pl.BlockSpec((1, d), lambda i: (0, 0))
    return pl.pallas_call(
        body, name="final_loss", grid=(t // tb,), in_specs=[row, vec, row],
        out_specs=[pl.BlockSpec((1, LANES), lambda i: (0, 0)), row, vec],
        out_shape=[jax.ShapeDtypeStruct((1, LANES), F32), jax.ShapeDtypeStruct((t, d), F32),
                   jax.ShapeDtypeStruct((1, d), F32)],
        compiler_params=_cp(("arbitrary",)))(h, g, tgt)


def _rope_tables(t):
    pos = jnp.arange(t, dtype=F32)
    inv = 1.0 / (ROPE_THETA ** (jnp.arange(0, ROT_DIM, 2, dtype=F32) / ROT_DIM))
    ang = pos[:, None] * inv[None, :]
    cos, sin = jnp.cos(ang), jnp.sin(ang)
    half = ROT_DIM // 2
    rest = HEAD_DIM - ROT_DIM
    c = jnp.concatenate([cos, cos, jnp.ones((t, rest), F32)], axis=1)
    s1 = jnp.concatenate([-sin, jnp.zeros((t, half + rest), F32)], axis=1)
    s2 = jnp.concatenate([jnp.zeros((t, half), F32), sin, jnp.zeros((t, rest), F32)], axis=1)
    return tuple(jnp.tile(v, (1, LANES // HEAD_DIM)) for v in (c, s1, s2))


def _rope(x, c, s1, s2):
    half = ROT_DIM // 2
    return x * c + pltpu.roll(x, LANES - half, 1) * s1 + pltpu.roll(x, half, 1) * s2


def _rope_t(g, c, s1, s2):
    half = ROT_DIM // 2
    return g * c + pltpu.roll(g * s1, half, 1) + pltpu.roll(g * s2, LANES - half, 1)


def _attn_mask(i):
    qi = _iota((ATTN_BLOCK, 2 * ATTN_BLOCK), 0)
    kj = _iota((ATTN_BLOCK, 2 * ATTN_BLOCK), 1)
    rel = qi + ATTN_BLOCK - kj
    first_key = jnp.where(i > 0, 0, ATTN_BLOCK)
    return (rel >= 0) & (rel < WINDOW) & (kj >= first_key)


def _attn_mask4(i):
    n = 4 * ATTN_BLOCK
    qi = jnp.bitwise_and(_iota((n, 2 * ATTN_BLOCK), 0), ATTN_BLOCK - 1)
    kj = _iota((n, 2 * ATTN_BLOCK), 1)
    rel = qi + ATTN_BLOCK - kj
    first_key = jnp.where(i > 0, 0, ATTN_BLOCK)
    return (rel >= 0) & (rel < WINDOW) & (kj >= first_key)


def _half_masks():
    lane = _iota((1, LANES), 1)
    return [(lane < HEAD_DIM).astype(F32), (lane >= HEAD_DIM).astype(F32)]


def _stack_heads(blocks, hm, j):
    pieces = []
    for r in range(4):
        qb, half = (4 * j + r) // 2, (4 * j + r) % 2
        piece = blocks[qb] * hm[half]
        if half != j:
            piece = pltpu.roll(piece, HEAD_DIM, 1)
        pieces.append(piece)
    return jnp.concatenate(pieces, axis=0)


def _unstack_heads(stacked, j):
    out = []
    for qb in (2 * j, 2 * j + 1):
        acc = None
        for half in range(2):
            r = 2 * qb + half - 4 * j
            piece = stacked[r * ATTN_BLOCK:(r + 1) * ATTN_BLOCK]
            if half != j:
                piece = pltpu.roll(piece, HEAD_DIM, 1)
            acc = piece if acc is None else acc + piece
        out.append((qb, acc))
    return out


def _sink_column(sink_ref, base):
    return jnp.concatenate([jnp.full((ATTN_BLOCK, 1), sink_ref[base + r], F32) for r in range(4)], axis=0)


def _attn_specs(nb_clamp):
    blk = ATTN_BLOCK
    kb, vb = O_K // LANES, O_V // LANES

    def cur(i):
        return jnp.minimum(i, nb_clamp)

    def prev(i):
        return jnp.maximum(jnp.minimum(i, nb_clamp + 1) - 1, 0)

    q = pl.BlockSpec((blk, 512), lambda p, i: (cur(i), p))
    kc = pl.BlockSpec((blk, LANES), lambda p, i: (cur(i), kb + p))
    kp = pl.BlockSpec((blk, LANES), lambda p, i: (prev(i), kb + p))
    vc = pl.BlockSpec((blk, LANES), lambda p, i: (cur(i), vb + p))
    vp = pl.BlockSpec((blk, LANES), lambda p, i: (prev(i), vb + p))
    tc = pl.BlockSpec((blk, LANES), lambda p, i: (cur(i), 0))
    tp = pl.BlockSpec((blk, LANES), lambda p, i: (prev(i), 0))
    return q, kc, kp, vc, vp, tc, tp


def _attn_fwd(proj, sinks, tables):
    t = proj.shape[0]
    nb = t // ATTN_BLOCK
    scale = HEAD_DIM ** -0.5

    def body(sink_ref, q_ref, kc_ref, kp_ref, vc_ref, vp_ref, cc_ref, s1c_ref, s2c_ref, cp_ref, s1p_ref, s2p_ref,
             o_ref):
        p = pl.program_id(0)
        i = pl.program_id(1)
        cc, s1c, s2c = cc_ref[...], s1c_ref[...], s2c_ref[...]
        kband = jnp.concatenate([_rope(kp_ref[...], cp_ref[...], s1p_ref[...], s2p_ref[...]),
                                 _rope(kc_ref[...], cc, s1c, s2c)], axis=0).astype(BF16)
        vband = jnp.concatenate([vp_ref[...], vc_ref[...]], axis=0)
        hm = _half_masks()
        vsel = [(vband * hm[j]).astype(BF16) for j in range(2)]
        valid = _attn_mask(i)
        for qb in range(4):
            qr = _rope(q_ref[:, qb * LANES:(qb + 1) * LANES], cc, s1c, s2c)
            acc = jnp.zeros((ATTN_BLOCK, LANES), F32)
            for half in range(2):
                hh = qb * 2 + half
                j = hh // 4
                qs = qr * hm[half]
                if half != j:
                    qs = pltpu.roll(qs, HEAD_DIM, 1)
                s = jnp.where(valid, _dot(qs, kband, 'nt') * scale, NEG)
                sink = sink_ref[p * 8 + hh]
                m = jnp.maximum(jnp.max(s, axis=1, keepdims=True), sink)
                pe = jnp.exp(s - m)
                den = jnp.sum(pe, axis=1, keepdims=True) + jnp.exp(sink - m)
                o = _dot(pe / den, vsel[j])
                if half != j:
                    o = pltpu.roll(o, HEAD_DIM, 1)
                acc = acc + o
            o_ref[:, qb * LANES:(qb + 1) * LANES] = acc

    q, kc, kp, vc, vp, tc, tp = _attn_specs(nb - 1)
    smem = pl.BlockSpec(memory_space=pltpu.SMEM)
    return pl.pallas_call(
        body, name="attn_fwd", grid=(4, nb),
        in_specs=[smem, q, kc, kp, vc, vp, tc, tc, tc, tp, tp, tp],
        out_specs=pl.BlockSpec((ATTN_BLOCK, 512), lambda p, i: (i, p)),
        out_shape=jax.ShapeDtypeStruct((t, ATTN_WIDTH), F32),
        compiler_params=_cp(("parallel", "arbitrary")))(sinks, proj, proj, proj, proj, proj, *tables, *tables)


def _attn_bwd(proj, sinks, tables, dout):
    t = proj.shape[0]
    nb = t // ATTN_BLOCK
    scale = HEAD_DIM ** -0.5

    def body(sink_ref, q_ref, kc_ref, kp_ref, vc_ref, vp_ref, cc_ref, s1c_ref, s2c_ref, cp_ref, s1p_ref, s2p_ref,
             do_ref, dq_ref, dk_ref, dv_ref, ds_ref, carry_k, carry_v):
        p = pl.program_id(0)
        i = pl.program_id(1)
        ptab = (cp_ref[...], s1p_ref[...], s2p_ref[...])

        @pl.when(i == 0)
        def _():
            carry_k[...] = jnp.zeros_like(carry_k)
            carry_v[...] = jnp.zeros_like(carry_v)
            ds_ref[...] = jnp.zeros_like(ds_ref)

        @pl.when(i < nb)
        def _():
            cc, s1c, s2c = cc_ref[...], s1c_ref[...], s2c_ref[...]
            kband = jnp.concatenate([_rope(kp_ref[...], *ptab), _rope(kc_ref[...], cc, s1c, s2c)], axis=0)
            vband = jnp.concatenate([vp_ref[...], vc_ref[...]], axis=0)
            hm = _half_masks()
            kband16 = kband.astype(BF16)
            vband16 = vband.astype(BF16)
            valid = _attn_mask4(i)
            dkb = jnp.zeros((2 * ATTN_BLOCK, LANES), F32)
            dvb = jnp.zeros((2 * ATTN_BLOCK, LANES), F32)
            row8 = _iota((8, LANES), 0)
            dsink = jnp.zeros((8, LANES), F32)
            qr = [_rope(q_ref[:, qb * LANES:(qb + 1) * LANES], cc, s1c, s2c) for qb in range(4)]
            dob = [do_ref[:, qb * LANES:(qb + 1) * LANES] for qb in range(4)]
            for j in range(2):
                qst = _stack_heads(qr, hm, j).astype(BF16)
                dost = _stack_heads(dob, hm, j).astype(BF16)
                s = jnp.where(valid, _dot(qst, kband16, 'nt') * scale, NEG)
                sink = _sink_column(sink_ref, p * 8 + 4 * j)
                m = jnp.maximum(jnp.max(s, axis=1, keepdims=True), sink)
                pe = jnp.exp(s - m)
                psink = jnp.exp(sink - m)
                den = jnp.sum(pe, axis=1, keepdims=True) + psink
                pr = pe / den
                dvb = dvb + _dot(pr.T, dost)
                dp = _dot(dost, vband16, 'nt')
                delta = jnp.sum(pr * dp, axis=1, keepdims=True)
                dsc = pr * (dp - delta) * scale
                dsk = psink / den * delta
                for r in range(4):
                    part = jnp.sum(dsk[r * ATTN_BLOCK:(r + 1) * ATTN_BLOCK])
                    dsink = dsink + jnp.where(row8 == 4 * j + r, -part, 0.0)
                for qb, dqb in _unstack_heads(_dot(dsc, kband * hm[j]), j):
                    dq_ref[:, qb * LANES:(qb + 1) * LANES] = _rope_t(dqb, cc, s1c, s2c).astype(BF16)
                dkb = dkb + _dot(dsc.T, qst)
            ds_ref[0] += dsink
            dk_ref[...] = _rope_t(carry_k[...] + dkb[:ATTN_BLOCK], *ptab).astype(BF16)
            dv_ref[...] = (carry_v[...] + dvb[:ATTN_BLOCK]).astype(BF16)
            carry_k[...] = dkb[ATTN_BLOCK:]
            carry_v[...] = dvb[ATTN_BLOCK:]

        @pl.when(i == nb)
        def _():
            dk_ref[...] = _rope_t(carry_k[...], *ptab).astype(BF16)
            dv_ref[...] = carry_v[...].astype(BF16)

    q, kc, kp, vc, vp, tc, tp = _attn_specs(nb - 1)
    smem = pl.BlockSpec(memory_space=pltpu.SMEM)
    qblk = pl.BlockSpec((ATTN_BLOCK, 512), lambda p, i: (jnp.minimum(i, nb - 1), p))
    kvout = pl.BlockSpec((ATTN_BLOCK, LANES), lambda p, i: (jnp.maximum(i - 1, 0), p))
    return pl.pallas_call(
        body, name="attn_bwd", grid=(4, nb + 1),
        in_specs=[smem, q, kc, kp, vc, vp, tc, tc, tc, tp, tp, tp, qblk],
        out_specs=[qblk, kvout, kvout, pl.BlockSpec((1, 8, LANES), lambda p, i: (p, 0, 0))],
        out_shape=[jax.ShapeDtypeStruct((t, ATTN_WIDTH), BF16), jax.ShapeDtypeStruct((t, KV_WIDTH), BF16),
                   jax.ShapeDtypeStruct((t, KV_WIDTH), BF16), jax.ShapeDtypeStruct((4, 8, LANES), F32)],
        scratch_shapes=[pltpu.VMEM((ATTN_BLOCK, LANES), F32), pltpu.VMEM((ATTN_BLOCK, LANES), F32)],
        compiler_params=_cp(("parallel", "arbitrary")))(sinks, proj, proj, proj, proj, proj, *tables, *tables, dout)


def _shift_rows(x, prev8, j):
    r = pltpu.roll(x, j, 0)
    head = jnp.where(_iota((8, 1), 0) < j, pltpu.roll(prev8, j, 0), r[:8])
    if x.shape[0] == 8:
        return head
    return jnp.concatenate([head, r[8:]], axis=0)


def _shift_rows_up(x, next8, j):
    n = x.shape[0]
    r = pltpu.roll(x, n - j, 0)
    tail = jnp.where(_iota((8, 1), 0) >= 8 - j, pltpu.roll(next8, 8 - j, 0), r[n - 8:])
    return jnp.concatenate([r[:n - 8], tail], axis=0)


def _conv_apply(x, prev8, w, b, taps):
    u = b + x * w[taps - 1:taps]
    for j in range(1, taps):
        u = u + _shift_rows(x, prev8, j) * w[taps - 1 - j:taps - j]
    return u


def _conv_grads(du, du_next8, x, x_prev8, w, taps):
    dx = du * w[taps - 1:taps]
    rowk = _iota((taps, 1), 0)
    dw = jnp.where(rowk == taps - 1, jnp.sum(du * x, axis=0, keepdims=True), 0.0)
    for j in range(1, taps):
        dx = dx + _shift_rows_up(du, du_next8, j) * w[taps - 1 - j:taps - j]
        part = jnp.sum(du * _shift_rows(x, x_prev8, j), axis=0, keepdims=True)
        dw = dw + jnp.where(rowk == taps - 1 - j, part, 0.0)
    return dx, dw, jnp.sum(du, axis=0, keepdims=True)


def _conv_specs(tb, tc, col0, t):
    c0 = col0 // tc
    cur = pl.BlockSpec((tb, tc), lambda j, i: (i, c0 + j))
    prev = pl.BlockSpec((8, tc), lambda j, i: (jnp.maximum(i * (tb // 8) - 1, 0), c0 + j))
    nxt = pl.BlockSpec((8, tc), lambda j, i: (jnp.minimum((i + 1) * (tb // 8), t // 8 - 1), c0 + j))
    return cur, prev, nxt


def _conv_fwd(x, w, b, *, col0, width, act, name):
    t = x.shape[0]
    taps = w.shape[0]
    tb, tc = _rows(t, 512), _tile(width, 1024)
    assert col0 % tc == 0

    def body(x_ref, xp_ref, w_ref, b_ref, o_ref):
        i = pl.program_id(1)
        prev8 = jnp.where(i > 0, xp_ref[...], 0.0)
        u = _conv_apply(x_ref[...], prev8, w_ref[...], b_ref[...], taps)
        if act:
            u = u * _sigmoid(u)
        o_ref[...] = u

    cur, prev, _ = _conv_specs(tb, tc, col0, t)
    par = pl.BlockSpec((taps, tc), lambda j, i: (0, j))
    bias = pl.BlockSpec((1, tc), lambda j, i: (0, j))
    return pl.pallas_call(
        body, name=name, grid=(width // tc, t // tb), in_specs=[cur, prev, par, bias],
        out_specs=pl.BlockSpec((tb, tc), lambda j, i: (i, j)), out_shape=jax.ShapeDtypeStruct((t, width), F32),
        compiler_params=_cp(("parallel", "parallel")))(x, x, w, b)


def _conv_silu_dact(x, w, b, dout, *, col0, width, name):
    t = x.shape[0]
    taps = w.shape[0]
    tb, tc = _rows(t, 512), _tile(width, 1024)

    def body(x_ref, xp_ref, w_ref, b_ref, d_ref, o_ref):
        i = pl.program_id(1)
        prev8 = jnp.where(i > 0, xp_ref[...], 0.0)
        u = _conv_apply(x_ref[...], prev8, w_ref[...], b_ref[...], taps)
        sg = _sigmoid(u)
        o_ref[...] = d_ref[...] * (sg * (1.0 + u * (1.0 - sg)))

    cur, prev, _ = _conv_specs(tb, tc, col0, t)
    par = pl.BlockSpec((taps, tc), lambda j, i: (0, j))
    bias = pl.BlockSpec((1, tc), lambda j, i: (0, j))
    out = pl.BlockSpec((tb, tc), lambda j, i: (i, j))
    return pl.pallas_call(
        body, name=name, grid=(width // tc, t // tb), in_specs=[cur, prev, par, bias, out],
        out_specs=out, out_shape=jax.ShapeDtypeStruct((t, width), F32),
        compiler_params=_cp(("parallel", "parallel")))(x, x, w, b, dout)


def _conv_bwd(x, w, du, *, col0, width, name):
    t = x.shape[0]
    taps = w.shape[0]
    tb, tc = _rows(t, 512), _tile(width, 1024)
    nrow = t // tb

    def body(x_ref, xp_ref, w_ref, du_ref, dun_ref, dx_ref, dw_ref, db_ref):
        i = pl.program_id(1)
        xv = x_ref[...]
        prev8 = jnp.where(i > 0, xp_ref[...], 0.0)
        next8 = jnp.where(i < nrow - 1, dun_ref[...], 0.0)
        dx, dwv, dbv = _conv_grads(du_ref[...], next8, xv, prev8, w_ref[...], taps)
        dx_ref[...] = dx.astype(BF16)

        @pl.when(i == 0)
        def _():
            dw_ref[...] = dwv
            db_ref[...] = dbv

        @pl.when(i > 0)
        def _():
            dw_ref[...] += dwv
            db_ref[...] += dbv

    cur, prev, _ = _conv_specs(tb, tc, col0, t)
    dcur, _, dnxt = _conv_specs(tb, tc, 0, t)
    par = pl.BlockSpec((taps, tc), lambda j, i: (0, j))
    bias = pl.BlockSpec((1, tc), lambda j, i: (0, j))
    return pl.pallas_call(
        body, name=name, grid=(width // tc, nrow), in_specs=[cur, prev, par, dcur, dnxt],
        out_specs=[dcur, par, bias],
        out_shape=[jax.ShapeDtypeStruct((t, width), BF16), jax.ShapeDtypeStruct((taps, width), F32),
                   jax.ShapeDtypeStruct((1, width), F32)],
        compiler_params=_cp(("parallel", "arbitrary")))(x, x, w, du, du)


def _ffn_specs(tb, tc, t):
    nc = D_FF // tc

    def cur(half):
        return pl.BlockSpec((tb, tc), lambda j, i: (i, half * nc + j))

    def prev(half):
        return pl.BlockSpec((8, tc), lambda j, i: (jnp.maximum(i * (tb // 8) - 1, 0), half * nc + j))

    def nxt(half):
        return pl.BlockSpec((8, tc), lambda j, i: (jnp.minimum((i + 1) * (tb // 8), t // 8 - 1), half * nc + j))

    def par(rows, half):
        return pl.BlockSpec((rows, tc), lambda j, i: (0, half * nc + j))

    return cur, prev, nxt, par


def _ffn_act_fwd(u0, w, b):
    t = u0.shape[0]
    tb, tc = _rows(t, 512), _tile(D_FF, 1408)
    cur, prev, _, par = _ffn_specs(tb, tc, t)

    def body(g_ref, gp_ref, v_ref, vp_ref, wg_ref, wv_ref, bg_ref, bv_ref, o_ref):
        i = pl.program_id(1)
        ug = _conv_apply(g_ref[...], jnp.where(i > 0, gp_ref[...], 0.0), wg_ref[...], bg_ref[...], FFN_CONV)
        uv = _conv_apply(v_ref[...], jnp.where(i > 0, vp_ref[...], 0.0), wv_ref[...], bv_ref[...], FFN_CONV)
        o_ref[...] = (ug * _sigmoid(ug) * uv).astype(BF16)

    return pl.pallas_call(
        body, name="ffn_act_fwd", grid=(D_FF // tc, t // tb),
        in_specs=[cur(0), prev(0), cur(1), prev(1), par(FFN_CONV, 0), par(FFN_CONV, 1), par(1, 0), par(1, 1)],
        out_specs=pl.BlockSpec((tb, tc), lambda j, i: (i, j)), out_shape=jax.ShapeDtypeStruct((t, D_FF), BF16),
        compiler_params=_cp(("parallel", "parallel")))(u0, u0, u0, u0, w, w, b, b)


def _ffn_act_bwd(u0, w, b, da):
    t = u0.shape[0]
    tb, tc = _rows(t, 256), _tile(D_FF, 1408)
    nrow = t // tb
    taps = FFN_CONV
    cur, prev, nxt, par = _ffn_specs(tb, tc, t)

    def dact(ug, uv, dav):
        sg = _sigmoid(ug)
        return dav * uv * (sg * (1.0 + ug * (1.0 - sg))), dav * ug * sg

    def body(g_ref, gp_ref, gn_ref, v_ref, vp_ref, vn_ref, wg_ref, wv_ref, bg_ref, bv_ref, da_ref, dan_ref,
             dx_ref, dw_ref, db_ref):
        i = pl.program_id(1)
        xg, xv = g_ref[...], v_ref[...]
        gp = jnp.where(i > 0, gp_ref[...], 0.0)
        vp = jnp.where(i > 0, vp_ref[...], 0.0)
        wg, wv, bg, bv = wg_ref[...], wv_ref[...], bg_ref[...], bv_ref[...]
        dug, duv = dact(_conv_apply(xg, gp, wg, bg, taps), _conv_apply(xv, vp, wv, bv, taps),
                        da_ref[...].astype(F32))
        dan = jnp.where(i < nrow - 1, dan_ref[...].astype(F32)[:8], 0.0)
        dugn, duvn = dact(_conv_apply(gn_ref[...], xg[tb - 8:], wg, bg, taps),
                          _conv_apply(vn_ref[...], xv[tb - 8:], wv, bv, taps), dan)
        dxg, dwg, dbg = _conv_grads(dug, dugn, xg, gp, wg, taps)
        dxv, dwv, dbv = _conv_grads(duv, duvn, xv, vp, wv, taps)
        dx_ref[0] = dxg.astype(BF16)
        dx_ref[1] = dxv.astype(BF16)

        @pl.when(i == 0)
        def _():
            dw_ref[0] = dwg
            dw_ref[1] = dwv
            db_ref[0] = dbg
            db_ref[1] = dbv

        @pl.when(i > 0)
        def _():
            dw_ref[0] += dwg
            dw_ref[1] += dwv
            db_ref[0] += dbg
            db_ref[1] += dbv

    da_cur = pl.BlockSpec((tb, tc), lambda j, i: (i, j))
    da_nxt = pl.BlockSpec((16, tc), lambda j, i: (jnp.minimum((i + 1) * (tb // 16), t // 16 - 1), j))
    return pl.pallas_call(
        body, name="ffn_act_bwd", grid=(D_FF // tc, nrow),
        in_specs=[cur(0), prev(0), nxt(0), cur(1), prev(1), nxt(1), par(taps, 0), par(taps, 1), par(1, 0),
                  par(1, 1), da_cur, da_nxt],
        out_specs=[pl.BlockSpec((2, tb, tc), lambda j, i: (0, i, j)),
                   pl.BlockSpec((2, taps, tc), lambda j, i: (0, 0, j)),
                   pl.BlockSpec((2, 1, tc), lambda j, i: (0, 0, j))],
        out_shape=[jax.ShapeDtypeStruct((2, t, D_FF), BF16), jax.ShapeDtypeStruct((2, taps, D_FF), F32),
                   jax.ShapeDtypeStruct((2, 1, D_FF), F32)],
        compiler_params=_cp(("parallel", "arbitrary")))(u0, u0, u0, u0, u0, u0, w, w, b, b, da, da)


def _head_masks():
    lane = _iota((1, 4 * SSD_HEAD_DIM), 1)
    return [((lane >= r * SSD_HEAD_DIM) & (lane < (r + 1) * SSD_HEAD_DIM)).astype(F32) for r in range(4)]


def _segsum(v):
    first = _iota((1, LANES), 1) < SSD_HEAD_DIM
    halves = []
    for k in range(2):
        vh = v[:, k * LANES:(k + 1) * LANES]
        both = jnp.sum(vh, axis=1, keepdims=True)
        one = jnp.sum(jnp.where(first, vh, 0.0), axis=1, keepdims=True)
        halves.append(jnp.where(first, one, both - one))
    return jnp.concatenate(halves, axis=1)


def _ssd_common(raw_e, prow, rawr4, bcol, acol):
    n = SSD_CHUNK
    dt_e = _softplus(raw_e + prow[0:1, :])
    a_e = -jnp.exp(prow[1:2, :])
    d_e = prow[2:3, :]
    tril = (_iota((n, n), 0) >= _iota((n, n), 1)).astype(F32)
    acs_e = _dot_exact(tril, dt_e * a_e)
    last_e = acs_e[n - 1:n, :]
    dtr4 = _softplus(rawr4 + bcol)
    triu = (_iota((n, n), 0) <= _iota((n, n), 1)).astype(F32)
    acs_r4 = _dot_exact(dtr4 * (-jnp.exp(acol)), triu)
    return dt_e, a_e, d_e, acs_e, last_e, acs_r4


def _decay_matrix(acs_e, acs_r4, r):
    n = SSD_CHUNK
    col = acs_e[:, r * SSD_HEAD_DIM:r * SSD_HEAD_DIM + 1]
    seg = col - acs_r4[r:r + 1, :]
    causal = _iota((n, n), 0) >= _iota((n, n), 1)
    return jnp.exp(jnp.where(causal, seg, NEG))


def _ssd_specs(t, rev):
    nc = t // SSD_CHUNK
    xb, bb, cb = 0, SSD_INNER // SSD_STATE, (SSD_INNER + BC_WIDTH) // SSD_STATE

    def ch(c):
        return (nc - 1 - c) if rev else c

    x = pl.BlockSpec((SSD_CHUNK, 256), lambda g, c: (ch(c), xb + g))
    bm = pl.BlockSpec((SSD_CHUNK, SSD_STATE), lambda g, c: (ch(c), bb + g))
    cm = pl.BlockSpec((SSD_CHUNK, SSD_STATE), lambda g, c: (ch(c), cb + g))
    dtc = pl.BlockSpec((1, SSD_CHUNK, 256), lambda g, c: (g, ch(c), 0))
    dtr = pl.BlockSpec((1, 4, SSD_CHUNK), lambda g, c: (g, 0, ch(c)))
    prow = pl.BlockSpec((1, 3, 256), lambda g, c: (g, 0, 0))
    pcol = pl.BlockSpec((1, 4, 1), lambda g, c: (g, 0, 0))
    st = pl.BlockSpec((1, 1, SSD_STATE, 256), lambda g, c: (g, ch(c), 0, 0))
    return x, bm, cm, dtc, dtr, prow, pcol, st, ch


def _ssd_params(dt_raw, dt_bias, a_log, ssd_d):
    t = dt_raw.shape[0]
    by_group = dt_raw.reshape(t, SSD_GROUPS, 4)
    dtc = jnp.repeat(by_group, SSD_HEAD_DIM, axis=2).transpose(1, 0, 2)
    dtr = by_group.transpose(1, 2, 0)
    prow = jnp.repeat(jnp.stack([dt_bias.reshape(SSD_GROUPS, 4), a_log.reshape(SSD_GROUPS, 4),
                                 ssd_d.reshape(SSD_GROUPS, 4)], axis=1), SSD_HEAD_DIM, axis=2)
    bcol = dt_bias.reshape(SSD_GROUPS, 4, 1)
    acol = a_log.reshape(SSD_GROUPS, 4, 1)
    return dtc, dtr, prow, bcol, acol


def _ssd_fwd(xbc, params):
    t = xbc.shape[0]
    nc = t // SSD_CHUNK
    dtc, dtr, prow, bcol, acol = params

    def body(x_ref, b_ref, c_ref, dtc_ref, dtr_ref, prow_ref, bcol_ref, acol_ref, y_ref, st_ref, s_scr):
        c = pl.program_id(1)

        @pl.when(c == 0)
        def _():
            s_scr[...] = jnp.zeros_like(s_scr)

        masks = _head_masks()
        dt_e, a_e, d_e, acs_e, last_e, acs_r4 = _ssd_common(
            dtc_ref[0], prow_ref[0], dtr_ref[0], bcol_ref[0], acol_ref[0])
        xv = x_ref[...]
        bm, cm = b_ref[...], c_ref[...]
        s = s_scr[...]
        st_ref[0, 0] = s
        xdt = xv * dt_e
        cb = _dot(cm, bm, 'nt')
        y = _dot(cm, s) * jnp.exp(acs_e) + xv * d_e
        for r in range(4):
            mr = cb * _decay_matrix(acs_e, acs_r4, r)
            y = y + _dot(mr, xdt * masks[r])
        y_ref[...] = y
        w = xdt * jnp.exp(last_e - acs_e)
        s_scr[...] = s * jnp.exp(last_e) + _dot(bm.T, w)

    x, bm, cm, dtcs, dtrs, prs, pcs, st, _ = _ssd_specs(t, False)
    return pl.pallas_call(
        body, name="ssd_fwd", grid=(SSD_GROUPS, nc), in_specs=[x, bm, cm, dtcs, dtrs, prs, pcs, pcs],
        out_specs=[pl.BlockSpec((SSD_CHUNK, 256), lambda g, c: (c, g)), st],
        out_shape=[jax.ShapeDtypeStruct((t, SSD_INNER), F32),
                   jax.ShapeDtypeStruct((SSD_GROUPS, nc, SSD_STATE, 256), F32)],
        scratch_shapes=[pltpu.VMEM((SSD_STATE, 256), F32)],
        compiler_params=_cp(("parallel", "arbitrary")))(xbc, xbc, xbc, dtc, dtr, prow, bcol, acol)


def _ssd_bwd(xbc, params, states, dy):
    t = xbc.shape[0]
    nc = t // SSD_CHUNK
    n = SSD_CHUNK
    dtc, dtr, prow, bcol, acol = params

    def body(x_ref, b_ref, c_ref, dtc_ref, dtr_ref, prow_ref, bcol_ref, acol_ref, st_ref, dy_ref,
             dx_ref, db_ref, dc_ref, ddt_ref, dp_ref, ds_scr):
        c = pl.program_id(1)

        @pl.when(c == 0)
        def _():
            ds_scr[...] = jnp.zeros_like(ds_scr)
            dp_ref[...] = jnp.zeros_like(dp_ref)

        masks = _head_masks()
        raw_e = dtc_ref[0]
        prw = prow_ref[0]
        dt_e, a_e, d_e, acs_e, last_e, acs_r4 = _ssd_common(raw_e, prw, dtr_ref[0], bcol_ref[0], acol_ref[0])
        xv = x_ref[...]
        bm, cm = b_ref[...], c_ref[...]
        s = st_ref[0, 0]
        ds = ds_scr[...]
        dyv = dy_ref[...]
        e_e = jnp.exp(acs_e)
        dec_e = jnp.exp(last_e - acs_e)
        cd_e = jnp.exp(last_e)
        xdt = xv * dt_e
        w = xdt * dec_e
        b16, c16, s16, ds16 = bm.astype(BF16), cm.astype(BF16), s.astype(BF16), ds.astype(BF16)
        cb = _dot(c16, b16, 'nt')
        yoff_raw = _dot(c16, s16)
        dye = dyv * e_e
        dye16 = dye.astype(BF16)
        dcm = _dot(dye16, s16, 'nt')
        ds_scr[...] = ds * cd_e + _dot(cm.T, dye16)
        dacs_e = _segsum(dyv * yoff_raw) * e_e
        dw = _dot(b16, ds16)
        dbm = _dot(w, ds16, 'nt')
        tdec = _segsum(dw * xdt) * dec_e
        dacs_e = dacs_e - tdec
        dlast_e = jnp.sum(tdec, axis=0, keepdims=True)
        dxdt = dw * dec_e
        dlast_e = dlast_e + _segsum(jnp.sum(ds * s, axis=0, keepdims=True)) * cd_e
        dcb = jnp.zeros((n, n), F32)
        for r in range(4):
            lm = _decay_matrix(acs_e, acs_r4, r)
            mr = cb * lm
            dyr16 = (dyv * masks[r]).astype(BF16)
            dm = _dot(dyr16, xdt * masks[r], 'nt')
            dcb = dcb + dm * lm
            dseg = dm * mr
            dcol = jnp.sum(dseg, axis=1, keepdims=True) - jnp.sum(dseg.T, axis=1, keepdims=True)
            dacs_e = dacs_e + dcol * masks[r]
            dxdt = dxdt + _dot(mr.T, dyr16)
        dcm = dcm + _dot(dcb, b16)
        dbm = dbm + _dot(dcb.T, c16)
        dacs_e = dacs_e + jnp.where(_iota((n, 1), 0) == n - 1, dlast_e, 0.0)
        triu = (_iota((n, n), 0) <= _iota((n, n), 1)).astype(F32)
        ddta_e = _dot_exact(triu, dacs_e)
        ddt_e = ddta_e * a_e + _segsum(dxdt * xv)
        dx_ref[...] = dxdt * dt_e + dyv * d_e
        db_ref[...] = dbm
        dc_ref[...] = dcm
        draw_e = ddt_e * _sigmoid(raw_e + prw[0:1, :])
        pick = (_iota((4 * SSD_HEAD_DIM, LANES), 0) == SSD_HEAD_DIM * _iota((4 * SSD_HEAD_DIM, LANES), 1)).astype(F32)
        ddt_ref[0] = _dot_exact(draw_e, pick)
        dbias = jnp.sum(draw_e, axis=0, keepdims=True)
        dalog = jnp.sum(ddta_e * dt_e, axis=0, keepdims=True) * a_e
        dd = _segsum(jnp.sum(dyv * xv, axis=0, keepdims=True))
        row3 = _iota((3, 1), 0)
        dp_ref[0] += (jnp.where(row3 == 0, dbias, 0.0) + jnp.where(row3 == 1, dalog, 0.0)
                      + jnp.where(row3 == 2, dd, 0.0))

    x, bm, cm, dtcs, dtrs, prs, pcs, st, ch = _ssd_specs(t, True)
    yblk = pl.BlockSpec((SSD_CHUNK, 256), lambda g, c: (ch(c), g))
    nblk = pl.BlockSpec((SSD_CHUNK, SSD_STATE), lambda g, c: (ch(c), g))
    return pl.pallas_call(
        body, name="ssd_bwd", grid=(SSD_GROUPS, nc),
        in_specs=[x, bm, cm, dtcs, dtrs, prs, pcs, pcs, st, yblk],
        out_specs=[yblk, nblk, nblk, pl.BlockSpec((1, SSD_CHUNK, LANES), lambda g, c: (g, ch(c), 0)), prs],
        out_shape=[jax.ShapeDtypeStruct((t, SSD_INNER), F32), jax.ShapeDtypeStruct((t, BC_WIDTH), F32),
                   jax.ShapeDtypeStruct((t, BC_WIDTH), F32), jax.ShapeDtypeStruct((SSD_GROUPS, t, LANES), F32),
                   jax.ShapeDtypeStruct((SSD_GROUPS, 3, 256), F32)],
        scratch_shapes=[pltpu.VMEM((SSD_STATE, 256), F32)],
        compiler_params=_cp(("parallel", "arbitrary")))(xbc, xbc, xbc, dtc, dtr, prow, bcol, acol, states, dy)


GROUP_W = SSD_INNER // SSD_GROUPS


def _mix_specs(tb):
    row = pl.BlockSpec((tb, 2048), lambda i: (i, 0))
    zlo = pl.BlockSpec((tb, 1024), lambda i: (i, O_Z // 1024))
    zhi = pl.BlockSpec((tb, 1024), lambda i: (i, O_Z // 1024 + 1))
    vec = pl.BlockSpec((1, 2048), lambda i: (0, 0))
    return row, zlo, zhi, vec


def _mix_fwd(attn, y, proj, g_attn, g_ssd):
    t = attn.shape[0]
    tb = _rows(t, 256)

    def body(a_ref, y_ref, zlo_ref, zhi_ref, ga_ref, gs_ref, o_ref):
        av = a_ref[...]
        r = lax.rsqrt(jnp.mean(av * av, axis=-1, keepdims=True) + EPS)
        o_ref[:, :ATTN_WIDTH] = (av * r * ga_ref[...]).astype(BF16)
        for g in range(SSD_GROUPS):
            lo, hi = g * GROUP_W, (g + 1) * GROUP_W
            zref = zlo_ref if g < 4 else zhi_ref
            z = zref[:, lo % 1024:lo % 1024 + GROUP_W]
            yg = y_ref[:, lo:hi] * (z * _sigmoid(z))
            rg = lax.rsqrt(jnp.mean(yg * yg, axis=-1, keepdims=True) + EPS)
            o_ref[:, ATTN_WIDTH + lo:ATTN_WIDTH + hi] = (yg * rg * gs_ref[:, lo:hi]).astype(BF16)

    row, zlo, zhi, vec = _mix_specs(tb)
    return pl.pallas_call(
        body, name="mix_fwd", grid=(t // tb,), in_specs=[row, row, zlo, zhi, vec, vec],
        out_specs=pl.BlockSpec((tb, 4096), lambda i: (i, 0)), out_shape=jax.ShapeDtypeStruct((t, 4096), BF16),
        compiler_params=_cp(("parallel",)))(attn, y, proj, proj, g_attn, g_ssd)


def _mix_bwd(dmix, attn, y, proj, g_attn, g_ssd):
    t = attn.shape[0]
    tb = _rows(t, 256)

    def body(dm_ref, a_ref, y_ref, zlo_ref, zhi_ref, ga_ref, gs_ref, da_ref, dy_ref, dz_ref, dga_ref, dgs_ref):
        i = pl.program_id(0)
        av = a_ref[...]
        dn = dm_ref[:, :ATTN_WIDTH].astype(F32)
        r = lax.rsqrt(jnp.mean(av * av, axis=-1, keepdims=True) + EPS)
        u = dn * ga_ref[...]
        da_ref[...] = r * u - av * (r * r * r * jnp.mean(u * av, axis=-1, keepdims=True))
        dga = jnp.sum(dn * av * r, axis=0, keepdims=True)

        @pl.when(i == 0)
        def _():
            dga_ref[...] = dga

        @pl.when(i > 0)
        def _():
            dga_ref[...] += dga

        for g in range(SSD_GROUPS):
            lo, hi = g * GROUP_W, (g + 1) * GROUP_W
            zref = zlo_ref if g < 4 else zhi_ref
            z = zref[:, lo % 1024:lo % 1024 + GROUP_W]
            yv = y_ref[:, lo:hi]
            sg = _sigmoid(z)
            sz = z * sg
            yg = yv * sz
            rg = lax.rsqrt(jnp.mean(yg * yg, axis=-1, keepdims=True) + EPS)
            do = dm_ref[:, ATTN_WIDTH + lo:ATTN_WIDTH + hi].astype(F32)
            ug = do * gs_ref[:, lo:hi]
            dyg = rg * ug - yg * (rg * rg * rg * jnp.mean(ug * yg, axis=-1, keepdims=True))
            dy_ref[:, lo:hi] = dyg * sz
            dz_ref[:, lo:hi] = (dyg * yv * (sg * (1.0 + z * (1.0 - sg)))).astype(BF16)
            dgs = jnp.sum(do * yg * rg, axis=0, keepdims=True)

            @pl.when(i == 0)
            def _():
                dgs_ref[:, lo:hi] = dgs

            @pl.when(i > 0)
            def _():
                dgs_ref[:, lo:hi] += dgs

    row, zlo, zhi, vec = _mix_specs(tb)
    return pl.pallas_call(
        body, name="mix_bwd", grid=(t // tb,),
        in_specs=[pl.BlockSpec((tb, 4096), lambda i: (i, 0)), row, row, zlo, zhi, vec, vec],
        out_specs=[row, row, row, vec, vec],
        out_shape=[jax.ShapeDtypeStruct((t, 2048), F32), jax.ShapeDtypeStruct((t, 2048), F32),
                   jax.ShapeDtypeStruct((t, 2048), BF16), jax.ShapeDtypeStruct((1, 2048), F32),
                   jax.ShapeDtypeStruct((1, 2048), F32)],
        compiler_params=_cp(("arbitrary",)))(dmix, attn, y, proj, proj, g_attn, g_ssd)


def _adamw(w, g, m, v, name):
    r, c = w.shape
    tb = _rows(r, 256)
    c1 = 1.0 - ADAM_B1 ** ADAM_STEP
    c2 = 1.0 - ADAM_B2 ** ADAM_STEP

    def body(w_ref, g_ref, m_ref, v_ref, d_ref, m2_ref, v2_ref):
        gv = g_ref[...]
        m2 = ADAM_B1 * m_ref[...] + (1.0 - ADAM_B1) * gv
        v2 = ADAM_B2 * v_ref[...] + (1.0 - ADAM_B2) * (gv * gv)
        d_ref[...] = -ADAM_LR * ((m2 / c1) / (jnp.sqrt(v2 / c2) + ADAM_EPS) + ADAM_WD * w_ref[...])
        m2_ref[...] = m2
        v2_ref[...] = v2

    blk = pl.BlockSpec((tb, c), lambda i: (i, 0))
    shp = jax.ShapeDtypeStruct((r, c), F32)
    return pl.pallas_call(body, name=name, grid=(r // tb,), in_specs=[blk] * 4, out_specs=[blk] * 3,
                          out_shape=[shp] * 3, compiler_params=_cp(("parallel",)))(w, g, m, v)


def _adamw_halves(w, mine, theirs, m, v, pos, name, cols=False):
    r, c = w.shape
    h = r if cols else r // 2
    tb = _rows(h, 128)
    nh = h // tb
    c1 = 1.0 - ADAM_B1 ** ADAM_STEP
    c2 = 1.0 - ADAM_B2 ** ADAM_STEP

    def body(pos_ref, w_ref, a_ref, b_ref, m_ref, v_ref, g_ref, d_ref, m2_ref, v2_ref):
        which = pl.program_id(1) if cols else pl.program_id(0) // nh
        gv = jnp.where(which == pos_ref[0], a_ref[...], b_ref[...])
        m2 = ADAM_B1 * m_ref[...] + (1.0 - ADAM_B1) * gv
        v2 = ADAM_B2 * v_ref[...] + (1.0 - ADAM_B2) * (gv * gv)
        g_ref[...] = gv
        d_ref[...] = -ADAM_LR * ((m2 / c1) / (jnp.sqrt(v2 / c2) + ADAM_EPS) + ADAM_WD * w_ref[...])
        m2_ref[...] = m2
        v2_ref[...] = v2

    if cols:
        full = pl.BlockSpec((tb, c // 2), lambda i, j, pref: (i, j))
        half = pl.BlockSpec((tb, c // 2), lambda i, j, pref: (i, 0))
        grid = (nh, 2)
    else:
        full = pl.BlockSpec((tb, c), lambda i, pref: (i, 0))
        half = pl.BlockSpec((tb, c), lambda i, pref: (i % nh, 0))
        grid = (r // tb,)
    shp = jax.ShapeDtypeStruct((r, c), F32)
    grid_spec = pltpu.PrefetchScalarGridSpec(num_scalar_prefetch=1, grid=grid,
                                             in_specs=[full, half, half, full, full], out_specs=[full] * 4)
    return pl.pallas_call(body, name=name, grid_spec=grid_spec, out_shape=[shp] * 4,
                          compiler_params=_cp(("parallel",) * len(grid)))(pos, w, mine, theirs, m, v)


def _sum_own_half(g4, recv, pos, name, cols=False):
    _, r, c = g4.shape
    h, c = (r, c // 2) if cols else (r // 2, c)
    tb = _rows(h, 128)
    nh = h // tb
    own = (lambda j, i, pref: (j, i, pref[0])) if cols else (lambda j, i, pref: (j, pref[0] * nh + i, 0))

    def body(pos_ref, a_ref, b_ref, o_ref):
        o_ref[...] = (a_ref[...] + b_ref[...]).astype(BF16)

    grid_spec = pltpu.PrefetchScalarGridSpec(
        num_scalar_prefetch=1, grid=(N_CHIPS, nh),
        in_specs=[pl.BlockSpec((1, tb, c), own), pl.BlockSpec((1, tb, c), lambda j, i, pref: (j, i, 0))],
        out_specs=pl.BlockSpec((1, tb, c), lambda j, i, pref: (j, i, 0)))
    return pl.pallas_call(body, name=name, grid_spec=grid_spec,
                          out_shape=jax.ShapeDtypeStruct((N_CHIPS, h, c), BF16),
                          compiler_params=_cp(("parallel", "parallel")))(pos, g4, recv)


def _sum_chips(g4, recv, parts, pos, name, cols=False):
    _, r, c = g4.shape
    h, c = (r, c // 2) if cols else (r // 2, c)
    tb = _rows(h, 128)
    nh = h // tb
    own = (lambda i, pref: (pref[1], i, pref[0])) if cols else (lambda i, pref: (pref[1], pref[0] * nh + i, 0))

    def body(pos_ref, a_ref, b_ref, p_ref, o_ref):
        own = a_ref[0] + b_ref[0]
        o_ref[...] = ((own + p_ref[0].astype(F32)) + p_ref[1].astype(F32)) + p_ref[2].astype(F32)

    grid_spec = pltpu.PrefetchScalarGridSpec(
        num_scalar_prefetch=1, grid=(nh,),
        in_specs=[pl.BlockSpec((1, tb, c), own),
                  pl.BlockSpec((1, tb, c), lambda i, pref: (pref[1], i, 0)),
                  pl.BlockSpec((3, tb, c), lambda i, pref: (0, i, 0))],
        out_specs=pl.BlockSpec((tb, c), lambda i, pref: (i, 0)))
    return pl.pallas_call(body, name=name, grid_spec=grid_spec, out_shape=jax.ShapeDtypeStruct((h, c), F32),
                          compiler_params=_cp(("parallel",)))(pos, g4, recv, parts)


def _me():
    return lax.axis_index("x"), lax.axis_index("y"), lax.axis_index("c")


def _flip(v, bit):
    return (1 - v) if bit else v


CHIP_FLIPS = [(1, 0), (0, 1), (1, 1)]


def _allgather_weights(shards, cols=False):
    n = len(shards)

    def body(*refs):
        ins, outs, token = refs[:n], refs[n:2 * n], refs[2 * n]
        send_sems, recv_sems = refs[2 * n + 1:]
        x, y, c = _me()
        chip = 2 * x + y
        sib = (x, y, 1 - c)

        def remote(src, dst, k, to):
            return pltpu.make_async_remote_copy(src_ref=src, dst_ref=dst, send_sem=send_sems.at[k],
                                                recv_sem=recv_sems.at[k], device_id=to, device_id_type=MESH)

        def half(ref, which):
            if cols:
                h = ref.shape[1] // 2
                return ref.at[:, pl.ds(which * h, h)]
            h = ref.shape[0] // 2
            return ref.at[pl.ds(which * h, h)]

        sends = []
        for t in range(n):
            for k, (fx, fy) in enumerate(CHIP_FLIPS):
                cp = remote(half(ins[t], c), half(outs[t].at[chip], c), 6 * t + k, (_flip(x, fx), _flip(y, fy), c))
                cp.start()
                sends.append(cp)
        for t in range(n):
            for k, (fx, fy) in enumerate(CHIP_FLIPS):
                landed = half(outs[t].at[2 * _flip(x, fx) + _flip(y, fy)], c)
                remote(landed, landed, 6 * t + k, (x, y, c)).wait_recv()
                fw = remote(landed, landed, 6 * t + 3 + k, sib)
                fw.start()
                sends.append(fw)
        for t in range(n):
            for k, (fx, fy) in enumerate(CHIP_FLIPS):
                got = half(outs[t].at[2 * _flip(x, fx) + _flip(y, fy)], 1 - c)
                remote(got, got, 6 * t + 3 + k, (x, y, c)).wait_recv()
        for cp in sends:
            cp.wait_send()
        token[...] = jnp.zeros_like(token)

    outs = pl.pallas_call(
        body, name="allgather_weights", in_specs=[HBM_SPEC] * n,
        out_specs=[HBM_SPEC] * n + [pl.BlockSpec(memory_space=pltpu.VMEM)],
        out_shape=[jax.ShapeDtypeStruct((N_CHIPS,) + s.shape, s.dtype) for s in shards] + [TOKEN],
        scratch_shapes=[pltpu.SemaphoreType.DMA((6 * n,)), pltpu.SemaphoreType.DMA((6 * n,))],
        compiler_params=pltpu.CompilerParams(has_side_effects=True))(*shards)
    return list(outs[:n]), outs[n]


def _exchange_halves(g4s, name, cols=False):
    n = len(g4s)

    def land(g):
        return (N_CHIPS, g.shape[1], g.shape[2] // 2) if cols else (N_CHIPS, g.shape[1] // 2, g.shape[2])

    def body(*refs):
        ins, outs = refs[:n], refs[n:2 * n]
        send_sems, recv_sems = refs[2 * n:]
        x, y, c = _me()
        cps = []
        for t in range(n):
            if cols:
                h = ins[t].shape[2] // 2
                theirs = ins[t].at[:, :, pl.ds((1 - c) * h, h)]
            else:
                h = ins[t].shape[1] // 2
                theirs = ins[t].at[:, pl.ds((1 - c) * h, h)]
            cp = pltpu.make_async_remote_copy(
                src_ref=theirs, dst_ref=outs[t], send_sem=send_sems.at[t],
                recv_sem=recv_sems.at[t], device_id=(x, y, 1 - c), device_id_type=MESH)
            cp.start()
            cps.append(cp)
        for cp in cps:
            cp.wait()

    return pl.pallas_call(
        body, name=name, in_specs=[HBM_SPEC] * n, out_specs=[HBM_SPEC] * n,
        out_shape=[jax.ShapeDtypeStruct(land(g), g.dtype) for g in g4s],
        scratch_shapes=[pltpu.SemaphoreType.DMA((n,)), pltpu.SemaphoreType.DMA((n,))],
        compiler_params=pltpu.CompilerParams(has_side_effects=True))(*g4s)


def _share_halves(ghs, name):
    n = len(ghs)

    def body(*refs):
        ins, outs = refs[:n], refs[n:2 * n]
        send_sems, recv_sems = refs[2 * n:]
        x, y, c = _me()
        cps = []
        for t in range(n):
            cp = pltpu.make_async_remote_copy(
                src_ref=ins[t], dst_ref=outs[t], send_sem=send_sems.at[t], recv_sem=recv_sems.at[t],
                device_id=(x, y, 1 - c), device_id_type=MESH)
            cp.start()
            cps.append(cp)
        for cp in cps:
            cp.wait()

    return pl.pallas_call(
        body, name=name, in_specs=[HBM_SPEC] * n, out_specs=[HBM_SPEC] * n,
        out_shape=[jax.ShapeDtypeStruct(g.shape, g.dtype) for g in ghs],
        scratch_shapes=[pltpu.SemaphoreType.DMA((n,)), pltpu.SemaphoreType.DMA((n,))],
        compiler_params=pltpu.CompilerParams(has_side_effects=True))(*ghs)


SEM_SPEC = pl.BlockSpec(memory_space=pltpu.SEMAPHORE)
ANY_SPEC = pl.BlockSpec(memory_space=pl.ANY)
DATAFLOW = pltpu.SideEffectType.DATAFLOW_SIDE_EFFECTING


def _in_hbm(a):
    return pltpu.with_memory_space_constraint(a, pltpu.HBM)


def _push_start(srcs, land_shapes, route, peers, name):
    n, npeer = len(srcs), len(peers)
    lands = [lax.empty(shp, s.dtype) for shp, s in zip(land_shapes, srcs)]

    def body(*refs):
        ins, lnd = refs[:n], refs[n:2 * n]
        send_sems, recv_sems = refs[2 * n], refs[2 * n + 1]
        token = refs[-1]
        x, y, c = _me()
        for t in range(n):
            for k, (fx, fy, fc) in enumerate(peers):
                src, dst = route(ins[t], lnd[t], k, x, y, c)
                pltpu.make_async_remote_copy(
                    src_ref=src, dst_ref=dst, send_sem=send_sems.at[npeer * t + k],
                    recv_sem=recv_sems.at[npeer * t + k],
                    device_id=(_flip(x, fx), _flip(y, fy), _flip(c, fc)), device_id_type=MESH).start()
        token[...] = jnp.zeros_like(token)

    bufs = [_in_hbm(a) for a in list(srcs) + lands]
    outs = pl.pallas_call(
        body, name=name,
        out_shape=(pltpu.SemaphoreType.DMA((npeer * n,)), pltpu.SemaphoreType.DMA((npeer * n,)),
                   *[pltpu.HBM(b.shape, b.dtype) for b in bufs], TOKEN),
        in_specs=[HBM_SPEC] * (2 * n),
        out_specs=(SEM_SPEC, SEM_SPEC, *[HBM_SPEC] * (2 * n), pl.BlockSpec(memory_space=pltpu.VMEM)),
        input_output_aliases={i: 2 + i for i in range(2 * n)},
        compiler_params=pltpu.CompilerParams(has_side_effects=DATAFLOW))(*bufs)
    return outs[0], outs[1], list(outs[2:2 + n]), list(outs[2 + n:2 + 2 * n]), outs[-1]


def _push_wait(send_sems, recv_sems, srcs, lands, after, route, peers, name):
    n, npeer = len(srcs), len(peers)

    def body(*refs):
        ins, lnd = refs[:n], refs[n:2 * n]
        ssem, rsem = refs[2 * n], refs[2 * n + 1]
        x, y, c = _me()
        for t in range(n):
            for k, (fx, fy, fc) in enumerate(peers):
                src, dst = route(ins[t], lnd[t], k, x, y, c)
                cp = pltpu.make_async_remote_copy(
                    src_ref=src, dst_ref=dst, send_sem=ssem.at[npeer * t + k], recv_sem=rsem.at[npeer * t + k],
                    device_id=(_flip(x, fx), _flip(y, fy), _flip(c, fc)), device_id_type=MESH)
                cp.wait_send()
                cp.wait_recv()

    bufs = list(srcs) + list(lands)
    outs = pl.pallas_call(
        body, name=name, out_shape=tuple(pltpu.HBM(b.shape, b.dtype) for b in bufs),
        in_specs=[HBM_SPEC] * (2 * n) + [SEM_SPEC, SEM_SPEC, ANY_SPEC], out_specs=tuple([HBM_SPEC] * (2 * n)),
        input_output_aliases={i: i for i in range(2 * n)},
        compiler_params=pltpu.CompilerParams(has_side_effects=DATAFLOW))(*bufs, send_sems, recv_sems, after)
    return list(outs[:n]), list(outs[n:])


OTHER_CHIPS = [(fx, fy, 0) for fx, fy in CHIP_FLIPS]
SIBLING = [(0, 0, 1)]


def _route_gather(src, land, k, x, y, c):
    return src, land.at[2 * x + y]


def _route_gather_wait(src, land, k, x, y, c):
    fx, fy = CHIP_FLIPS[k]
    return src, land.at[2 * _flip(x, fx) + _flip(y, fy)]


def _route_scatter(src, land, k, x, y, c):
    fx, fy = CHIP_FLIPS[k]
    return src.at[2 * _flip(x, fx) + _flip(y, fy)], land.at[k]


def _route_exchange(src, land, k, x, y, c):
    h = land.shape[1]
    return src.at[:, pl.ds((1 - c) * h, h)], land


def _allreduce_small(v):
    r = v.shape[0]

    def body(v_ref, o_ref, buf, send_sems, recv_sems):
        x, y, c = _me()
        me = 4 * x + 2 * y + c
        buf[0] = v_ref[...]
        cps = []
        for k in range(1, 8):
            kx, ky, kc = (k >> 2) & 1, (k >> 1) & 1, k & 1
            cp = pltpu.make_async_remote_copy(
                src_ref=v_ref, dst_ref=buf.at[k], send_sem=send_sems.at[k - 1], recv_sem=recv_sems.at[k - 1],
                device_id=(_flip(x, kx), _flip(y, ky), _flip(c, kc)), device_id_type=MESH)
            cp.start()
            cps.append(cp)
        for cp in cps:
            cp.wait()
        acc = buf[me]
        for d in range(1, 8):
            acc = acc + buf[jnp.bitwise_xor(me, d)]
        o_ref[...] = acc

    vm = pl.BlockSpec(memory_space=pltpu.VMEM)
    return pl.pallas_call(
        body, name="allreduce_small", in_specs=[vm], out_specs=vm, out_shape=jax.ShapeDtypeStruct(v.shape, F32),
        scratch_shapes=[pltpu.VMEM((8, r, LANES), F32), pltpu.SemaphoreType.DMA((7,)),
                        pltpu.SemaphoreType.DMA((7,))],
        compiler_params=pltpu.CompilerParams(has_side_effects=True, vmem_limit_bytes=VMEM_LIMIT))(v)


def _grad_exchange_start(g4, tag):
    land = (N_CHIPS, g4.shape[1] // 2, g4.shape[2])
    send_sems, recv_sems, srcs, lands, token = _push_start(
        [g4], [land], _route_exchange, SIBLING, name="grad_exchange_start_" + tag)
    return (send_sems, recv_sems, srcs, lands, tag), token


def _grad_scatter_start(state, pos, after):
    send_sems, recv_sems, srcs, lands, tag = state
    (g4,), (recv,) = _push_wait(send_sems, recv_sems, srcs, lands, after, _route_exchange, SIBLING,
                                name="grad_exchange_wait_" + tag)
    return _grad_pair_scatter(g4, recv, pos, tag)


def _grad_pair_scatter(g4, recv, pos, tag, cols=False):
    p16 = _sum_own_half(g4, recv, pos, name="grad_sum_pair_" + tag, cols=cols)
    send_sems, recv_sems, srcs, lands, token = _push_start(
        [p16], [(3,) + p16.shape[1:]], _route_scatter, OTHER_CHIPS, name="grad_scatter_start_" + tag)
    return (g4, recv, send_sems, recv_sems, srcs, lands, tag, cols), token


def _grad_reduce_begin(g4, pos, tag, cols=False):
    recv = _exchange_halves([g4], name="grad_exchange_halves_" + tag, cols=cols)[0]
    return _grad_pair_scatter(g4, recv, pos, tag, cols)


def _grad_reduce_finish(state, pos, after):
    g4, recv, send_sems, recv_sems, srcs, lands, tag, cols = state
    parts = _push_wait(send_sems, recv_sems, srcs, lands, after, _route_scatter, OTHER_CHIPS,
                       name="grad_scatter_wait_" + tag)[1][0]
    mine = _sum_chips(g4, recv, parts, pos, name="grad_sum_chips_" + tag, cols=cols)
    return mine, _share_halves([mine], name="grad_share_halves_" + tag)[0]


def _local_step(x, tgt, p, w_in_t, w_in_dt, hooks):
    t = x.shape[0]
    tables = _rope_tables(t)
    sinks = p['sinks'].reshape(N_Q_HEADS)

    def told(name, value):
        return tuple(hooks.grad_ready(name, value))

    xn = _rmsnorm_fwd(x, p['norm_mix'], "norm_mix_fwd", deps=hooks.first_deps)
    proj = _matmul(xn, w_in_t, mode='nt', name="in_proj", n_limit=MAIN_WIDTH)
    dt_raw = _matmul(xn, w_in_dt, mode='nt', name="in_proj_dt")[:, :SSD_HEADS]
    attn = _attn_fwd(proj, sinks, tables)
    conv_b = p['ssd_conv_b']
    xbc = _conv_fwd(proj, p['ssd_conv_w'], conv_b, col0=O_XBC, width=CONV_CH, act=True, name="ssd_conv_fwd")
    sp = _ssd_params(dt_raw, p['dt_bias'].reshape(-1), p['a_log'].reshape(-1), p['ssd_d'].reshape(-1))
    y, states = _ssd_fwd(xbc, sp)
    mix = _mix_fwd(attn, y, proj, p['attn_out_norm'], p['ssd_norm'])
    w_out = hooks.weight('w_out', mix)
    h1 = _matmul(mix, w_out, mode='nn', name="out_proj", add=x)
    hn = _rmsnorm_fwd(h1, p['norm_ffn'], "norm_ffn_fwd")
    w_up = hooks.weight('w_up', hn)
    u0 = _matmul(hn, w_up, mode='nn', name="ffn_up", b_owner=True, tn=1408)
    a = _ffn_act_fwd(u0, p['ffn_conv_w'], p['ffn_conv_b'])
    w_down = hooks.weight('w_down', a)
    h2 = _matmul(a, w_down, mode='nn', name="ffn_down", add=h1, tk=1408)
    loss, dh2, g_norm_final = _final_loss(h2, p['norm_final'].reshape(1, D_MODEL), tgt)

    g = {}
    da = _matmul(dh2, w_down, mode='nt', name="ffn_down_dx", out_dtype=BF16, tn=1408)
    g['w_down'] = _matmul(a, dh2, mode='tn', name="ffn_down_dw", tm=1408, tk=1024)
    dep = told('w_down', g['w_down'])
    du0, dcw, dcb = _ffn_act_bwd(u0, p['ffn_conv_w'], p['ffn_conv_b'], da)
    g['ffn_conv_w'] = dcw.transpose(1, 0, 2).reshape(FFN_CONV, 2 * D_FF)
    g['ffn_conv_b'] = dcb.transpose(1, 0, 2).reshape(1, 2 * D_FF)
    g['w_up'] = _matmul(hn, du0, mode='tn', name="ffn_up_dw", deps=dep, b_halves=True, owner_major=True,
                        tn=1408)
    dep = told('w_up', g['w_up'])
    dhn = _matmul(du0, w_up, mode='nt', name="ffn_up_dx", out_dtype=BF16, deps=dep, a_halves=True,
                  b_owner=True, tk=2816)
    dh1, g['norm_ffn'] = _rmsnorm_bwd(h1, p['norm_ffn'], dhn, dh2, "norm_ffn_bwd")

    g['w_out'] = _matmul(mix, dh1, mode='tn', name="out_proj_dw")
    dep = told('w_out', g['w_out'])
    dmix = _matmul(dh1, w_out, mode='nt', name="out_proj_dx", out_dtype=BF16, deps=dep)
    dattn, dy, dz, g['attn_out_norm'], g['ssd_norm'] = _mix_bwd(dmix, attn, y, proj, p['attn_out_norm'],
                                                                p['ssd_norm'])
    dq, dk, dv, dsink = _attn_bwd(proj, sinks, tables, dattn)
    g['sinks'] = dsink[:, :, 0].reshape(1, N_Q_HEADS)
    dxs, dbm, dcm, ddt8, dpar = _ssd_bwd(xbc, sp, states, dy)
    dpar = dpar[:, :, ::SSD_HEAD_DIM]
    g['dt_bias'] = dpar[:, 0, :].reshape(1, SSD_HEADS)
    g['a_log'] = dpar[:, 1, :].reshape(1, SSD_HEADS)
    g['ssd_d'] = dpar[:, 2, :].reshape(1, SSD_HEADS)
    dxbc_act = jnp.concatenate([dxs, dbm, dcm], axis=1)
    dconv = _conv_silu_dact(proj, p['ssd_conv_w'], conv_b, dxbc_act, col0=O_XBC, width=CONV_CH,
                            name="ssd_conv_dact")
    dxbc, g['ssd_conv_w'], g['ssd_conv_b'] = _conv_bwd(proj, p['ssd_conv_w'], dconv, col0=O_XBC, width=CONV_CH,
                                                       name="ssd_conv_bwd")
    dproj = jnp.concatenate([dq, dk, dv, dz, dxbc], axis=1)
    ddt = ddt8[:, :, :4].transpose(1, 0, 2).reshape(t, SSD_HEADS)
    ddt_pad = jnp.pad(ddt, ((0, 0), (0, LANES - SSD_HEADS))).astype(BF16)
    g['w_in'] = (_matmul(dproj, xn, mode='tn', name="in_proj_dw", m_rows=IN_PROJ_WIDTH),
                 _matmul(ddt_pad, xn, mode='tn', name="in_proj_dt_dw"))
    dep = told('w_in', g['w_in'])
    dxn_dt = _matmul(ddt_pad, w_in_dt, mode='nn', name="in_proj_dt_dx", deps=dep)
    dxn = _matmul(dproj, w_in_t, mode='nn', name="in_proj_dx", out_dtype=BF16, add=dxn_dt, k_limit=MAIN_WIDTH,
                  tk=2304)
    dx, g['norm_mix'] = _rmsnorm_bwd(x, p['norm_mix'], dxn, dh1, "norm_mix_bwd")
    g['norm_final'] = g_norm_final
    return loss, dx, g


def _pack(arrs):
    flat = jnp.concatenate([a.reshape(-1) for a in arrs])
    n = flat.shape[0]
    rows = -(-n // LANES)
    rows = -(-rows // 8) * 8
    return jnp.pad(flat, (0, rows * LANES - n)).reshape(rows, LANES)


def _unpack(packed, shapes):
    flat = packed.reshape(-1)
    out, off = [], 0
    for s in shapes:
        n = 1
        for d in s:
            n *= d
        out.append(flat[off:off + n].reshape(s))
        off += n
    return out


class _StepHooks:
    def __init__(self, first_deps, weight, grad_ready):
        self.first_deps = first_deps
        self.weight = weight
        self.grad_ready = grad_ready


def kernel(x, norm_mix, w_in, sinks, attn_out_norm, ssd_conv_w, ssd_conv_b, dt_bias, a_log, ssd_d, ssd_norm, w_out, norm_ffn, w_up, ffn_conv_w, ffn_conv_b, w_down, norm_final, loss_target, m_norm_mix, m_w_in, m_sinks, m_attn_out_norm, m_ssd_conv_w, m_ssd_conv_b, m_dt_bias, m_a_log, m_ssd_d, m_ssd_norm, m_w_out, m_norm_ffn, m_w_up, m_ffn_conv_w, m_ffn_conv_b, m_w_down, m_norm_final, v_norm_mix, v_w_in, v_sinks, v_attn_out_norm, v_ssd_conv_w, v_ssd_conv_b, v_dt_bias, v_a_log, v_ssd_d, v_ssd_norm, v_w_out, v_norm_ffn, v_w_up, v_ffn_conv_w, v_ffn_conv_b, v_w_down, v_norm_final):
    args = dict(locals())
    w = {n: args[n] for n in WEIGHTS}
    m = {n: args['m_' + n] for n in WEIGHTS}
    v = {n: args['v_' + n] for n in WEIGHTS}
    xi, yi, ci = _me()
    chip = 2 * xi + yi
    pos = jnp.stack([ci, chip]).astype(jnp.int32)

    def place(shard, full_cols):
        z = jnp.zeros((shard.shape[0], full_cols), F32)
        return lax.dynamic_update_slice(z, shard * 0.5, (0, chip * shard.shape[1]))

    conv_pack = _pack([place(ssd_conv_w[0], CONV_CH), place(ffn_conv_w[0], 2 * D_FF)])
    conv_full = _allreduce_small(conv_pack)
    ssd_conv_w_full, ffn_conv_w_full = _unpack(conv_full, [(SSD_CONV, CONV_CH), (FFN_CONV, 2 * D_FF)])

    w_in_t, m_in_t, v_in_t = (jnp.transpose(a[0]) for a in (w_in, m_w_in, v_w_in))
    in_shard = w_in_t.astype(BF16)
    (gathered,), order = _allgather_weights([in_shard], cols=True)
    full_in_t = lax.dynamic_update_slice(gathered, in_shard[None], (chip, 0, 0)).reshape(IN_PROJ_WIDTH, D_MODEL)
    w_in_dt = jnp.pad(full_in_t[MAIN_WIDTH:], ((0, LANES - SSD_HEADS), (0, 0)))
    gathers = {}
    order = order[:1, :1]
    for n, shard in (('w_out', w_out[0]), ('w_up', w_up[0]), ('w_down', w_down[0])):
        shard = (shard + order).astype(BF16)
        gathers[n] = _push_start([shard], [(N_CHIPS,) + shard.shape], _route_gather, OTHER_CHIPS,
                                 name="gather_start_" + n)
        order = gathers[n][4][:1, :1]
    first_deps = [gathers['w_down'][4]]

    def weight(name, after):
        send_sems, recv_sems, srcs, lands, _ = gathers[name]
        (own,), (got,) = _push_wait(send_sems, recv_sems, srcs, lands, after, _route_gather_wait, OTHER_CHIPS,
                                    name="gather_wait_" + name)
        whole = lax.dynamic_update_slice(got, own[None], (chip, 0, 0))
        return whole if name == 'w_up' else whole.reshape(-1, D_MODEL)

    reductions, exchanging = {}, {}

    def grad_ready(name, value):
        if name == 'w_in':
            main, dtp = value
            value = lax.dynamic_update_slice(main, dtp[:SSD_HEADS], (MAIN_WIDTH, 0))
        g4 = value if value.ndim == 3 else value.reshape(N_CHIPS, -1, value.shape[1])
        tokens = []
        for prev in list(exchanging):
            reductions[prev], token = _grad_scatter_start(exchanging.pop(prev), pos, g4)
            tokens.append(token)
        if name == 'w_in':
            reductions[name], token = _grad_reduce_begin(g4, pos, name, cols=True)
        else:
            exchanging[name], token = _grad_exchange_start(g4, name)
        return tokens + [token]

    small = {
        'norm_mix': norm_mix, 'sinks': sinks, 'attn_out_norm': attn_out_norm, 'ssd_conv_w': ssd_conv_w_full,
        'ssd_conv_b': ssd_conv_b, 'dt_bias': dt_bias, 'a_log': a_log, 'ssd_d': ssd_d, 'ssd_norm': ssd_norm,
        'norm_ffn': norm_ffn, 'ffn_conv_w': ffn_conv_w_full, 'ffn_conv_b': ffn_conv_b, 'norm_final': norm_final,
    }
    loss, dx, g = _local_step(x[0], loss_target[0], small, full_in_t, w_in_dt,
                              _StepHooks(tuple(first_deps), weight, grad_ready))
    gbig = {n: _grad_reduce_finish(reductions[n], pos, dx) for n in ('w_down', 'w_up', 'w_out', 'w_in')}

    small_names = [n for n in WEIGHTS if n not in BIG]
    small_g = [loss[:, :1]] + [g[n] for n in small_names]
    small_shapes = [(1, 1)] + [tuple(a.shape) for a in small_g[1:]]
    red = _unpack(_allreduce_small(_pack(small_g)), small_shapes)
    loss_out = red[0].reshape(())
    gsm = dict(zip(small_names, red[1:]))
    gsm['ssd_conv_w'] = lax.dynamic_slice(gsm['ssd_conv_w'], (0, chip * ssd_conv_w.shape[2]),
                                          (SSD_CONV, ssd_conv_w.shape[2]))
    gsm['ffn_conv_w'] = lax.dynamic_slice(gsm['ffn_conv_w'], (0, chip * ffn_conv_w.shape[2]),
                                          (FFN_CONV, ffn_conv_w.shape[2]))

    grads, deltas, new_m, new_v = {}, {}, {}, {}
    for n in BIG:
        mine, theirs = gbig[n]
        if n == 'w_in':
            outs = _adamw_halves(w_in_t, mine, theirs, m_in_t, v_in_t, pos, name="adamw_" + n, cols=True)
            outs = [jnp.transpose(o) for o in outs]
        else:
            outs = _adamw_halves(w[n][0], mine, theirs, m[n][0], v[n][0], pos, name="adamw_" + n)
        grads[n], deltas[n], new_m[n], new_v[n] = [o[None] for o in outs]
    shapes = [tuple(w[n].shape) for n in small_names]
    gp = _pack([gsm[n] for n in small_names])
    d, m2, v2 = _adamw(_pack([w[n] for n in small_names]), gp, _pack([m[n] for n in small_names]),
                       _pack([v[n] for n in small_names]), name="adamw_small")
    for n, gg, dd, mm, vv in zip(small_names, _unpack(gp, shapes), _unpack(d, shapes), _unpack(m2, shapes),
                                 _unpack(v2, shapes)):
        grads[n], deltas[n], new_m[n], new_v[n] = gg, dd, mm, vv

    return (loss_out, dx[None], *[grads[n] for n in WEIGHTS], *[deltas[n] for n in WEIGHTS],
            *[new_m[n] for n in WEIGHTS], *[new_v[n] for n in WEIGHTS])
```

```python
import functools

import jax
import jax.numpy as jnp
from jax import lax
from jax.experimental import pallas as pl
from jax.experimental.pallas import tpu as pltpu

F32 = jnp.float32
BF16 = jnp.bfloat16

D_MODEL = 2048
N_Q_HEADS = 32
N_KV_HEADS = 8
HEAD_DIM = 64
WINDOW = 128
ATTN_BLOCK = 128
ROT_DIM = 16
ROPE_THETA = 500000.0
SSD_HEADS = 32
SSD_HEAD_DIM = 64
SSD_INNER = 2048
SSD_GROUPS = 8
SSD_STATE = 128
SSD_CONV = 4
SSD_CHUNK = 128
ATTN_WIDTH = 2048
KV_WIDTH = 512
BC_WIDTH = 1024
CONV_CH = 4096
IN_PROJ_WIDTH = 9248
MAIN_WIDTH = 9216
D_FF = 5632
FFN_CONV = 3
EPS = 1e-6
O_Q, O_K, O_V, O_Z, O_XBC, O_DT = 0, 2048, 2560, 3072, 5120, 9216

ADAM_LR = 0.001
ADAM_B1 = 0.9
ADAM_B2 = 0.999
ADAM_EPS = 1e-08
ADAM_WD = 0.01
ADAM_STEP = 10

N_CHIPS = 4
NEG = -1e30
LANES = 128
VMEM_LIMIT = 48 * 1024 * 1024
MESH = pl.DeviceIdType.MESH
HBM_SPEC = pl.BlockSpec(memory_space=pltpu.HBM)
TOKEN = jax.ShapeDtypeStruct((8, LANES), F32)

WEIGHTS = ['norm_mix', 'w_in', 'sinks', 'attn_out_norm', 'ssd_conv_w', 'ssd_conv_b', 'dt_bias', 'a_log', 'ssd_d',
           'ssd_norm', 'w_out', 'norm_ffn', 'w_up', 'ffn_conv_w', 'ffn_conv_b', 'w_down', 'norm_final']
BIG = ['w_in', 'w_out', 'w_up', 'w_down']


def _cp(sem=None, vmem=VMEM_LIMIT):
    kw = {'vmem_limit_bytes': vmem}
    if sem is not None:
        kw['dimension_semantics'] = sem
    return pltpu.CompilerParams(**kw)


def _tile(n, pref):
    if n <= pref:
        return n
    t = (pref // LANES) * LANES
    while t > LANES and n % t:
        t -= LANES
    assert n % t == 0, (n, pref)
    return t


def _rows(n, pref):
    t = min(n, pref)
    while n % t:
        t -= 8
    if 4 * t < pref:
        t = pref
        while n % t:
            t += 8
    return t


def _iota(shape, dim):
    return lax.broadcasted_iota(jnp.int32, shape, dim)


def _dot(a, b, mode='nn'):
    dn = {'nn': (((1,), (0,)), ((), ())), 'nt': (((1,), (1,)), ((), ())), 'tn': (((0,), (0,)), ((), ()))}[mode]
    return lax.dot_general(a.astype(BF16), b.astype(BF16), dn, preferred_element_type=F32)


def _dot_exact(a, b):
    return lax.dot_general(a, b, (((1,), (0,)), ((), ())), precision=lax.Precision.HIGHEST,
                           preferred_element_type=F32)


def _sigmoid(x):
    return 1.0 / (1.0 + jnp.exp(-x))


def _softplus(x):
    return jnp.maximum(x, 0.0) + jnp.log(1.0 + jnp.exp(-jnp.abs(x)))


def _matmul(a, b, *, mode, name, out_dtype=F32, add=None, deps=(), tm=1024, tn=1024, tk=2048,
            a_halves=False, b_halves=False, b_owner=False, owner_major=False, n_limit=None, k_limit=None,
            m_rows=None):
    ash, bsh = (a.shape[1:] if a_halves else a.shape), (b.shape[1:] if (b_halves or b_owner) else b.shape)
    if mode == 'nn':
        (m, k), (k2, n) = ash, bsh
    elif mode == 'nt':
        (m, k), (n, k2) = ash, bsh
    else:
        (k, m), (k2, n) = ash, bsh
    if n_limit is not None:
        assert mode == 'nt' and n_limit <= n
        n = n_limit
    if k_limit is not None:
        assert mode == 'nn' and k_limit <= k2
        k2 = k_limit
    if a_halves:
        assert mode == 'nt'
        k = 2 * k
    if b_halves:
        assert mode == 'tn'
        n = 2 * n
    if b_owner:
        assert mode in ('nn', 'nt')
        if mode == 'nn':
            n = 4 * n
        else:
            k2 = 4 * k2
    assert k == k2, (a.shape, b.shape, mode)
    tm = _tile(m, tm)
    tn = _tile(n // 4 if (owner_major or (b_owner and mode == 'nn')) else (n // 2 if b_halves else n), tn)
    tk = _tile(k // 4 if (b_owner and mode == 'nt') else (k // 2 if a_halves else k), tk)
    nk = k // tk
    has_add = add is not None
    assert not (has_add and owner_major)

    def body(*refs):
        a_ref, b_ref = refs[:2]
        add_ref = refs[2] if has_add else None

        def finish(r, o_ref):
            if has_add:
                r = r + add_ref[...].astype(F32)
            o_ref[...] = r.astype(out_dtype)

        if nk == 1:
            finish(_dot(a_ref[...], b_ref[...], mode), refs[-1])
            return
        o_ref, acc = refs[-2:]
        kk = pl.program_id(2)

        @pl.when(kk == 0)
        def _():
            acc[...] = _dot(a_ref[...], b_ref[...], mode)

        @pl.when((kk > 0) & (kk < nk - 1))
        def _():
            acc[...] += _dot(a_ref[...], b_ref[...], mode)

        @pl.when(kk == nk - 1)
        def _():
            finish(acc[...] + _dot(a_ref[...], b_ref[...], mode), o_ref)

    if mode == 'tn':
        a_spec = pl.BlockSpec((tk, tm), lambda i, j, kk: (kk, i))
    elif a_halves:
        nkh = nk // 2
        a_spec = pl.BlockSpec((None, tm, tk), lambda i, j, kk: (kk // nkh, i, kk % nkh))
    else:
        a_spec = pl.BlockSpec((tm, tk), lambda i, j, kk: (i, kk))
    if mode == 'nt' and b_owner:
        nkq = nk // 4
        b_spec = pl.BlockSpec((None, tn, tk), lambda i, j, kk: (kk // nkq, j, kk % nkq))
    elif mode == 'nt':
        b_spec = pl.BlockSpec((tn, tk), lambda i, j, kk: (j, kk))
    elif b_owner:
        njq = (n // 4) // tn
        b_spec = pl.BlockSpec((None, tk, tn), lambda i, j, kk: (j // njq, kk, j % njq))
    elif b_halves:
        njh = (n // 2) // tn
        b_spec = pl.BlockSpec((None, tk, tn), lambda i, j, kk: (j // njh, kk, j % njh))
    else:
        b_spec = pl.BlockSpec((tk, tn), lambda i, j, kk: (kk, j))
    if owner_major:
        njo = (n // 4) // tn
        o_spec = pl.BlockSpec((None, tm, tn), lambda i, j, kk: (j // njo, i, j % njo))
        out_shape = jax.ShapeDtypeStruct((N_CHIPS, m, n // 4), out_dtype)
    else:
        o_spec = pl.BlockSpec((tm, tn), lambda i, j, kk: (i, j))
        out_shape = jax.ShapeDtypeStruct((m if m_rows is None else m_rows, n), out_dtype)
    dep_spec = pl.BlockSpec((8, LANES), lambda i, j, kk: (0, 0))
    in_specs = [a_spec, b_spec] + ([pl.BlockSpec((tm, tn), lambda i, j, kk: (i, j))] if has_add else [])
    in_specs += [dep_spec] * len(deps)
    args = (a, b) + ((add,) if has_add else ()) + tuple(deps)
    return pl.pallas_call(
        body, name=name, grid=(m // tm, n // tn, nk), in_specs=in_specs, out_specs=o_spec, out_shape=out_shape,
        scratch_shapes=[pltpu.VMEM((tm, tn), F32)] if nk > 1 else [],
        compiler_params=_cp(("parallel", "parallel", "arbitrary")))(*args)


def _rmsnorm_fwd(x, g, name, deps=()):
    t, d = x.shape
    tb = _rows(t, 256)

    def body(x_ref, g_ref, *rest):
        o_ref = rest[-1]
        xv = x_ref[...]
        r = lax.rsqrt(jnp.mean(xv * xv, axis=-1, keepdims=True) + EPS)
        o_ref[...] = (xv * r * g_ref[...]).astype(BF16)

    dep_spec = pl.BlockSpec((8, LANES), lambda i: (0, 0))
    return pl.pallas_call(
        body, name=name, grid=(t // tb,),
        in_specs=[pl.BlockSpec((tb, d), lambda i: (i, 0)), pl.BlockSpec((1, d), lambda i: (0, 0))]
        + [dep_spec] * len(deps),
        out_specs=pl.BlockSpec((tb, d), lambda i: (i, 0)), out_shape=jax.ShapeDtypeStruct((t, d), BF16),
        compiler_params=_cp(("parallel",)))(x, g, *deps)


def _rmsnorm_bwd(x, g, dy, res, name):
    t, d = x.shape
    tb = _rows(t, 256)

    def body(x_ref, g_ref, dy_ref, res_ref, dx_ref, dg_ref):
        i = pl.program_id(0)
        xv = x_ref[...]
        dyv = dy_ref[...].astype(F32)
        r = lax.rsqrt(jnp.mean(xv * xv, axis=-1, keepdims=True) + EPS)
        u = dyv * g_ref[...]
        dx = r * u - xv * (r * r * r * jnp.mean(u * xv, axis=-1, keepdims=True))
        dx_ref[...] = dx + res_ref[...]
        part = jnp.sum(dyv * xv * r, axis=0, keepdims=True)

        @pl.when(i == 0)
        def _():
            dg_ref[...] = part

        @pl.when(i > 0)
        def _():
            dg_ref[...] += part

    row = pl.BlockSpec((tb, d), lambda i: (i, 0))
    vec = pl.BlockSpec((1, d), lambda i: (0, 0))
    return pl.pallas_call(
        body, name=name, grid=(t // tb,), in_specs=[row, vec, row, row], out_specs=[row, vec],
        out_shape=[jax.ShapeDtypeStruct((t, d), F32), jax.ShapeDtypeStruct((1, d), F32)],
        compiler_params=_cp(("arbitrary",)))(x, g, dy, res)


def _final_loss(h, g, tgt):
    t, d = h.shape
    tb = _rows(t, 256)

    def body(h_ref, g_ref, t_ref, loss_ref, dh_ref, dg_ref):
        i = pl.program_id(0)
        hv = h_ref[...]
        gv = g_ref[...]
        r = lax.rsqrt(jnp.mean(hv * hv, axis=-1, keepdims=True) + EPS)
        y = hv * r * gv
        diff = y - t_ref[...]
        lpart = jnp.sum(jnp.sum(diff * diff, axis=1, keepdims=True), axis=0, keepdims=True) * (0.5 / d)
        dy = diff * (1.0 / d)
        u = dy * gv
        dh_ref[...] = r * u - hv * (r * r * r * jnp.mean(u * hv, axis=-1, keepdims=True))
        gpart = jnp.sum(dy * hv * r, axis=0, keepdims=True)
        lrow = jnp.broadcast_to(lpart, (1, LANES))

        @pl.when(i == 0)
        def _():
            loss_ref[...] = lrow
            dg_ref[...] = gpart

        @pl.when(i > 0)
        def _():
            loss_ref[...] += lrow
            dg_ref[...] += gpart

    row = pl.BlockSpec((tb, d), lambda i: (i, 0))
    vec = pl.BlockSpec((1, d), lambda i: (0, 0))
    return pl.pallas_call(
        body, name="final_loss", grid=(t // tb,), in_specs=[row, vec, row],
        out_specs=[pl.BlockSpec((1, LANES), lambda i: (0, 0)), row, vec],
        out_shape=[jax.ShapeDtypeStruct((1, LANES), F32), jax.ShapeDtypeStruct((t, d), F32),
                   jax.ShapeDtypeStruct((1, d), F32)],
        compiler_params=_cp(("arbitrary",)))(h, g, tgt)


def _rope_tables(t):
    pos = jnp.arange(t, dtype=F32)
    inv = 1.0 / (ROPE_THETA ** (jnp.arange(0, ROT_DIM, 2, dtype=F32) / ROT_DIM))
    ang = pos[:, None] * inv[None, :]
    cos, sin = jnp.cos(ang), jnp.sin(ang)
    half = ROT_DIM // 2
    rest = HEAD_DIM - ROT_DIM
    c = jnp.concatenate([cos, cos, jnp.ones((t, rest), F32)], axis=1)
    s1 = jnp.concatenate([-sin, jnp.zeros((t, half + rest), F32)], axis=1)
    s2 = jnp.concatenate([jnp.zeros((t, half), F32), sin, jnp.zeros((t, rest), F32)], axis=1)
    return tuple(jnp.tile(v, (1, LANES // HEAD_DIM)) for v in (c, s1, s2))


def _rope(x, c, s1, s2):
    half = ROT_DIM // 2
    return x * c + pltpu.roll(x, LANES - half, 1) * s1 + pltpu.roll(x, half, 1) * s2


def _rope_t(g, c, s1, s2):
    half = ROT_DIM // 2
    return g * c + pltpu.roll(g * s1, half, 1) + pltpu.roll(g * s2, LANES - half, 1)


def _attn_mask(i):
    qi = _iota((ATTN_BLOCK, 2 * ATTN_BLOCK), 0)
    kj = _iota((ATTN_BLOCK, 2 * ATTN_BLOCK), 1)
    rel = qi + ATTN_BLOCK - kj
    first_key = jnp.where(i > 0, 0, ATTN_BLOCK)
    return (rel >= 0) & (rel < WINDOW) & (kj >= first_key)


def _attn_mask4(i):
    n = 4 * ATTN_BLOCK
    qi = jnp.bitwise_and(_iota((n, 2 * ATTN_BLOCK), 0), ATTN_BLOCK - 1)
    kj = _iota((n, 2 * ATTN_BLOCK), 1)
    rel = qi + ATTN_BLOCK - kj
    first_key = jnp.where(i > 0, 0, ATTN_BLOCK)
    return (rel >= 0) & (rel < WINDOW) & (kj >= first_key)


def _half_masks():
    lane = _iota((1, LANES), 1)
    return [(lane < HEAD_DIM).astype(F32), (lane >= HEAD_DIM).astype(F32)]


def _stack_heads(blocks, hm, j):
    pieces = []
    for r in range(4):
        qb, half = (4 * j + r) // 2, (4 * j + r) % 2
        piece = blocks[qb] * hm[half]
        if half != j:
            piece = pltpu.roll(piece, HEAD_DIM, 1)
        pieces.append(piece)
    return jnp.concatenate(pieces, axis=0)


def _unstack_heads(stacked, j):
    out = []
    for qb in (2 * j, 2 * j + 1):
        acc = None
        for half in range(2):
            r = 2 * qb + half - 4 * j
            piece = stacked[r * ATTN_BLOCK:(r + 1) * ATTN_BLOCK]
            if half != j:
                piece = pltpu.roll(piece, HEAD_DIM, 1)
            acc = piece if acc is None else acc + piece
        out.append((qb, acc))
    return out


def _sink_column(sink_ref, base):
    return jnp.concatenate([jnp.full((ATTN_BLOCK, 1), sink_ref[base + r], F32) for r in range(4)], axis=0)


def _attn_specs(nb_clamp):
    blk = ATTN_BLOCK
    kb, vb = O_K // LANES, O_V // LANES

    def cur(i):
        return jnp.minimum(i, nb_clamp)

    def prev(i):
        return jnp.maximum(jnp.minimum(i, nb_clamp + 1) - 1, 0)

    q = pl.BlockSpec((blk, 512), lambda p, i: (cur(i), p))
    kc = pl.BlockSpec((blk, LANES), lambda p, i: (cur(i), kb + p))
    kp = pl.BlockSpec((blk, LANES), lambda p, i: (prev(i), kb + p))
    vc = pl.BlockSpec((blk, LANES), lambda p, i: (cur(i), vb + p))
    vp = pl.BlockSpec((blk, LANES), lambda p, i: (prev(i), vb + p))
    tc = pl.BlockSpec((blk, LANES), lambda p, i: (cur(i), 0))
    tp = pl.BlockSpec((blk, LANES), lambda p, i: (prev(i), 0))
    return q, kc, kp, vc, vp, tc, tp


def _attn_fwd(proj, sinks, tables):
    t = proj.shape[0]
    nb = t // ATTN_BLOCK
    scale = HEAD_DIM ** -0.5

    def body(sink_ref, q_ref, kc_ref, kp_ref, vc_ref, vp_ref, cc_ref, s1c_ref, s2c_ref, cp_ref, s1p_ref, s2p_ref,
             o_ref):
        p = pl.program_id(0)
        i = pl.program_id(1)
        cc, s1c, s2c = cc_ref[...], s1c_ref[...], s2c_ref[...]
        kband = jnp.concatenate([_rope(kp_ref[...], cp_ref[...], s1p_ref[...], s2p_ref[...]),
                                 _rope(kc_ref[...], cc, s1c, s2c)], axis=0).astype(BF16)
        vband = jnp.concatenate([vp_ref[...], vc_ref[...]], axis=0)
        hm = _half_masks()
        vsel = [(vband * hm[j]).astype(BF16) for j in range(2)]
        valid = _attn_mask(i)
        for qb in range(4):
            qr = _rope(q_ref[:, qb * LANES:(qb + 1) * LANES], cc, s1c, s2c)
            acc = jnp.zeros((ATTN_BLOCK, LANES), F32)
            for half in range(2):
                hh = qb * 2 + half
                j = hh // 4
                qs = qr * hm[half]
                if half != j:
                    qs = pltpu.roll(qs, HEAD_DIM, 1)
                s = jnp.where(valid, _dot(qs, kband, 'nt') * scale, NEG)
                sink = sink_ref[p * 8 + hh]
                m = jnp.maximum(jnp.max(s, axis=1, keepdims=True), sink)
                pe = jnp.exp(s - m)
                den = jnp.sum(pe, axis=1, keepdims=True) + jnp.exp(sink - m)
                o = _dot(pe / den, vsel[j])
                if half != j:
                    o = pltpu.roll(o, HEAD_DIM, 1)
                acc = acc + o
            o_ref[:, qb * LANES:(qb + 1) * LANES] = acc

    q, kc, kp, vc, vp, tc, tp = _attn_specs(nb - 1)
    smem = pl.BlockSpec(memory_space=pltpu.SMEM)
    return pl.pallas_call(
        body, name="attn_fwd", grid=(4, nb),
        in_specs=[smem, q, kc, kp, vc, vp, tc, tc, tc, tp, tp, tp],
        out_specs=pl.BlockSpec((ATTN_BLOCK, 512), lambda p, i: (i, p)),
        out_shape=jax.ShapeDtypeStruct((t, ATTN_WIDTH), F32),
        compiler_params=_cp(("parallel", "arbitrary")))(sinks, proj, proj, proj, proj, proj, *tables, *tables)


def _attn_bwd(proj, sinks, tables, dout):
    t = proj.shape[0]
    nb = t // ATTN_BLOCK
    scale = HEAD_DIM ** -0.5

    def body(sink_ref, q_ref, kc_ref, kp_ref, vc_ref, vp_ref, cc_ref, s1c_ref, s2c_ref, cp_ref, s1p_ref, s2p_ref,
             do_ref, dq_ref, dk_ref, dv_ref, ds_ref, carry_k, carry_v):
        p = pl.program_id(0)
        i = pl.program_id(1)
        ptab = (cp_ref[...], s1p_ref[...], s2p_ref[...])

        @pl.when(i == 0)
        def _():
            carry_k[...] = jnp.zeros_like(carry_k)
            carry_v[...] = jnp.zeros_like(carry_v)
            ds_ref[...] = jnp.zeros_like(ds_ref)

        @pl.when(i < nb)
        def _():
            cc, s1c, s2c = cc_ref[...], s1c_ref[...], s2c_ref[...]
            kband = jnp.concatenate([_rope(kp_ref[...], *ptab), _rope(kc_ref[...], cc, s1c, s2c)], axis=0)
            vband = jnp.concatenate([vp_ref[...], vc_ref[...]], axis=0)
            hm = _half_masks()
            kband16 = kband.astype(BF16)
            vband16 = vband.astype(BF16)
            valid = _attn_mask4(i)
            dkb = jnp.zeros((2 * ATTN_BLOCK, LANES), F32)
            dvb = jnp.zeros((2 * ATTN_BLOCK, LANES), F32)
            row8 = _iota((8, LANES), 0)
            dsink = jnp.zeros((8, LANES), F32)
            qr = [_rope(q_ref[:, qb * LANES:(qb + 1) * LANES], cc, s1c, s2c) for qb in range(4)]
            dob = [do_ref[:, qb * LANES:(qb + 1) * LANES] for qb in range(4)]
            for j in range(2):
                qst = _stack_heads(qr, hm, j).astype(BF16)
                dost = _stack_heads(dob, hm, j).astype(BF16)
                s = jnp.where(valid, _dot(qst, kband16, 'nt') * scale, NEG)
                sink = _sink_column(sink_ref, p * 8 + 4 * j)
                m = jnp.maximum(jnp.max(s, axis=1, keepdims=True), sink)
                pe = jnp.exp(s - m)
                psink = jnp.exp(sink - m)
                den = jnp.sum(pe, axis=1, keepdims=True) + psink
                pr = pe / den
                dvb = dvb + _dot(pr.T, dost)
                dp = _dot(dost, vband16, 'nt')
                delta = jnp.sum(pr * dp, axis=1, keepdims=True)
                dsc = pr * (dp - delta) * scale
                dsk = psink / den * delta
                for r in range(4):
                    part = jnp.sum(dsk[r * ATTN_BLOCK:(r + 1) * ATTN_BLOCK])
                    dsink = dsink + jnp.where(row8 == 4 * j + r, -part, 0.0)
                for qb, dqb in _unstack_heads(_dot(dsc, kband * hm[j]), j):
                    dq_ref[:, qb * LANES:(qb + 1) * LANES] = _rope_t(dqb, cc, s1c, s2c).astype(BF16)
                dkb = dkb + _dot(dsc.T, qst)
            ds_ref[0] += dsink
            dk_ref[...] = _rope_t(carry_k[...] + dkb[:ATTN_BLOCK], *ptab).astype(BF16)
            dv_ref[...] = (carry_v[...] + dvb[:ATTN_BLOCK]).astype(BF16)
            carry_k[...] = dkb[ATTN_BLOCK:]
            carry_v[...] = dvb[ATTN_BLOCK:]

        @pl.when(i == nb)
        def _():
            dk_ref[...] = _rope_t(carry_k[...], *ptab).astype(BF16)
            dv_ref[...] = carry_v[...].astype(BF16)

    q, kc, kp, vc, vp, tc, tp = _attn_specs(nb - 1)
    smem = pl.BlockSpec(memory_space=pltpu.SMEM)
    qblk = pl.BlockSpec((ATTN_BLOCK, 512), lambda p, i: (jnp.minimum(i, nb - 1), p))
    kvout = pl.BlockSpec((ATTN_BLOCK, LANES), lambda p, i: (jnp.maximum(i - 1, 0), p))
    return pl.pallas_call(
        body, name="attn_bwd", grid=(4, nb + 1),
        in_specs=[smem, q, kc, kp, vc, vp, tc, tc, tc, tp, tp, tp, qblk],
        out_specs=[qblk, kvout, kvout, pl.BlockSpec((1, 8, LANES), lambda p, i: (p, 0, 0))],
        out_shape=[jax.ShapeDtypeStruct((t, ATTN_WIDTH), BF16), jax.ShapeDtypeStruct((t, KV_WIDTH), BF16),
                   jax.ShapeDtypeStruct((t, KV_WIDTH), BF16), jax.ShapeDtypeStruct((4, 8, LANES), F32)],
        scratch_shapes=[pltpu.VMEM((ATTN_BLOCK, LANES), F32), pltpu.VMEM((ATTN_BLOCK, LANES), F32)],
        compiler_params=_cp(("parallel", "arbitrary")))(sinks, proj, proj, proj, proj, proj, *tables, *tables, dout)


def _shift_rows(x, prev8, j):
    r = pltpu.roll(x, j, 0)
    head = jnp.where(_iota((8, 1), 0) < j, pltpu.roll(prev8, j, 0), r[:8])
    if x.shape[0] == 8:
        return head
    return jnp.concatenate([head, r[8:]], axis=0)


def _shift_rows_up(x, next8, j):
    n = x.shape[0]
    r = pltpu.roll(x, n - j, 0)
    tail = jnp.where(_iota((8, 1), 0) >= 8 - j, pltpu.roll(next8, 8 - j, 0), r[n - 8:])
    return jnp.concatenate([r[:n - 8], tail], axis=0)


def _conv_apply(x, prev8, w, b, taps):
    u = b + x * w[taps - 1:taps]
    for j in range(1, taps):
        u = u + _shift_rows(x, prev8, j) * w[taps - 1 - j:taps - j]
    return u


def _conv_grads(du, du_next8, x, x_prev8, w, taps):
    dx = du * w[taps - 1:taps]
    rowk = _iota((taps, 1), 0)
    dw = jnp.where(rowk == taps - 1, jnp.sum(du * x, axis=0, keepdims=True), 0.0)
    for j in range(1, taps):
        dx = dx + _shift_rows_up(du, du_next8, j) * w[taps - 1 - j:taps - j]
        part = jnp.sum(du * _shift_rows(x, x_prev8, j), axis=0, keepdims=True)
        dw = dw + jnp.where(rowk == taps - 1 - j, part, 0.0)
    return dx, dw, jnp.sum(du, axis=0, keepdims=True)


def _conv_specs(tb, tc, col0, t):
    c0 = col0 // tc
    cur = pl.BlockSpec((tb, tc), lambda j, i: (i, c0 + j))
    prev = pl.BlockSpec((8, tc), lambda j, i: (jnp.maximum(i * (tb // 8) - 1, 0), c0 + j))
    nxt = pl.BlockSpec((8, tc), lambda j, i: (jnp.minimum((i + 1) * (tb // 8), t // 8 - 1), c0 + j))
    return cur, prev, nxt


def _conv_fwd(x, w, b, *, col0, width, act, name):
    t = x.shape[0]
    taps = w.shape[0]
    tb, tc = _rows(t, 512), _tile(width, 1024)
    assert col0 % tc == 0

    def body(x_ref, xp_ref, w_ref, b_ref, o_ref):
        i = pl.program_id(1)
        prev8 = jnp.where(i > 0, xp_ref[...], 0.0)
        u = _conv_apply(x_ref[...], prev8, w_ref[...], b_ref[...], taps)
        if act:
            u = u * _sigmoid(u)
        o_ref[...] = u

    cur, prev, _ = _conv_specs(tb, tc, col0, t)
    par = pl.BlockSpec((taps, tc), lambda j, i: (0, j))
    bias = pl.BlockSpec((1, tc), lambda j, i: (0, j))
    return pl.pallas_call(
        body, name=name, grid=(width // tc, t // tb), in_specs=[cur, prev, par, bias],
        out_specs=pl.BlockSpec((tb, tc), lambda j, i: (i, j)), out_shape=jax.ShapeDtypeStruct((t, width), F32),
        compiler_params=_cp(("parallel", "parallel")))(x, x, w, b)


def _conv_silu_dact(x, w, b, dout, *, col0, width, name):
    t = x.shape[0]
    taps = w.shape[0]
    tb, tc = _rows(t, 512), _tile(width, 1024)

    def body(x_ref, xp_ref, w_ref, b_ref, d_ref, o_ref):
        i = pl.program_id(1)
        prev8 = jnp.where(i > 0, xp_ref[...], 0.0)
        u = _conv_apply(x_ref[...], prev8, w_ref[...], b_ref[...], taps)
        sg = _sigmoid(u)
        o_ref[...] = d_ref[...] * (sg * (1.0 + u * (1.0 - sg)))

    cur, prev, _ = _conv_specs(tb, tc, col0, t)
    par = pl.BlockSpec((taps, tc), lambda j, i: (0, j))
    bias = pl.BlockSpec((1, tc), lambda j, i: (0, j))
    out = pl.BlockSpec((tb, tc), lambda j, i: (i, j))
    return pl.pallas_call(
        body, name=name, grid=(width // tc, t // tb), in_specs=[cur, prev, par, bias, out],
        out_specs=out, out_shape=jax.ShapeDtypeStruct((t, width), F32),
        compiler_params=_cp(("parallel", "parallel")))(x, x, w, b, dout)


def _conv_bwd(x, w, du, *, col0, width, name):
    t = x.shape[0]
    taps = w.shape[0]
    tb, tc = _rows(t, 512), _tile(width, 1024)
    nrow = t // tb

    def body(x_ref, xp_ref, w_ref, du_ref, dun_ref, dx_ref, dw_ref, db_ref):
        i = pl.program_id(1)
        xv = x_ref[...]
        prev8 = jnp.where(i > 0, xp_ref[...], 0.0)
        next8 = jnp.where(i < nrow - 1, dun_ref[...], 0.0)
        dx, dwv, dbv = _conv_grads(du_ref[...], next8, xv, prev8, w_ref[...], taps)
        dx_ref[...] = dx.astype(BF16)

        @pl.when(i == 0)
        def _():
            dw_ref[...] = dwv
            db_ref[...] = dbv

        @pl.when(i > 0)
        def _():
            dw_ref[...] += dwv
            db_ref[...] += dbv

    cur, prev, _ = _conv_specs(tb, tc, col0, t)
    dcur, _, dnxt = _conv_specs(tb, tc, 0, t)
    par = pl.BlockSpec((taps, tc), lambda j, i: (0, j))
    bias = pl.BlockSpec((1, tc), lambda j, i: (0, j))
    return pl.pallas_call(
        body, name=name, grid=(width // tc, nrow), in_specs=[cur, prev, par, dcur, dnxt],
        out_specs=[dcur, par, bias],
        out_shape=[jax.ShapeDtypeStruct((t, width), BF16), jax.ShapeDtypeStruct((taps, width), F32),
                   jax.ShapeDtypeStruct((1, width), F32)],
        compiler_params=_cp(("parallel", "arbitrary")))(x, x, w, du, du)


def _ffn_specs(tb, tc, t):
    nc = D_FF // tc

    def cur(half):
        return pl.BlockSpec((tb, tc), lambda j, i: (i, half * nc + j))

    def prev(half):
        return pl.BlockSpec((8, tc), lambda j, i: (jnp.maximum(i * (tb // 8) - 1, 0), half * nc + j))

    def nxt(half):
        return pl.BlockSpec((8, tc), lambda j, i: (jnp.minimum((i + 1) * (tb // 8), t // 8 - 1), half * nc + j))

    def par(rows, half):
        return pl.BlockSpec((rows, tc), lambda j, i: (0, half * nc + j))

    return cur, prev, nxt, par


def _ffn_act_fwd(u0, w, b):
    t = u0.shape[0]
    tb, tc = _rows(t, 512), _tile(D_FF, 1408)
    cur, prev, _, par = _ffn_specs(tb, tc, t)

    def body(g_ref, gp_ref, v_ref, vp_ref, wg_ref, wv_ref, bg_ref, bv_ref, o_ref):
        i = pl.program_id(1)
        ug = _conv_apply(g_ref[...], jnp.where(i > 0, gp_ref[...], 0.0), wg_ref[...], bg_ref[...], FFN_CONV)
        uv = _conv_apply(v_ref[...], jnp.where(i > 0, vp_ref[...], 0.0), wv_ref[...], bv_ref[...], FFN_CONV)
        o_ref[...] = (ug * _sigmoid(ug) * uv).astype(BF16)

    return pl.pallas_call(
        body, name="ffn_act_fwd", grid=(D_FF // tc, t // tb),
        in_specs=[cur(0), prev(0), cur(1), prev(1), par(FFN_CONV, 0), par(FFN_CONV, 1), par(1, 0), par(1, 1)],
        out_specs=pl.BlockSpec((tb, tc), lambda j, i: (i, j)), out_shape=jax.ShapeDtypeStruct((t, D_FF), BF16),
        compiler_params=_cp(("parallel", "parallel")))(u0, u0, u0, u0, w, w, b, b)


def _ffn_act_bwd(u0, w, b, da):
    t = u0.shape[0]
    tb, tc = _rows(t, 256), _tile(D_FF, 1408)
    nrow = t // tb
    taps = FFN_CONV
    cur, prev, nxt, par = _ffn_specs(tb, tc, t)

    def dact(ug, uv, dav):
        sg = _sigmoid(ug)
        return dav * uv * (sg * (1.0 + ug * (1.0 - sg))), dav * ug * sg

    def body(g_ref, gp_ref, gn_ref, v_ref, vp_ref, vn_ref, wg_ref, wv_ref, bg_ref, bv_ref, da_ref, dan_ref,
             dx_ref, dw_ref, db_ref):
        i = pl.program_id(1)
        xg, xv = g_ref[...], v_ref[...]
        gp = jnp.where(i > 0, gp_ref[...], 0.0)
        vp = jnp.where(i > 0, vp_ref[...], 0.0)
        wg, wv, bg, bv = wg_ref[...], wv_ref[...], bg_ref[...], bv_ref[...]
        dug, duv = dact(_conv_apply(xg, gp, wg, bg, taps), _conv_apply(xv, vp, wv, bv, taps),
                        da_ref[...].astype(F32))
        dan = jnp.where(i < nrow - 1, dan_ref[...].astype(F32)[:8], 0.0)
        dugn, duvn = dact(_conv_apply(gn_ref[...], xg[tb - 8:], wg, bg, taps),
                          _conv_apply(vn_ref[...], xv[tb - 8:], wv, bv, taps), dan)
        dxg, dwg, dbg = _conv_grads(dug, dugn, xg, gp, wg, taps)
        dxv, dwv, dbv = _conv_grads(duv, duvn, xv, vp, wv, taps)
        dx_ref[0] = dxg.astype(BF16)
        dx_ref[1] = dxv.astype(BF16)

        @pl.when(i == 0)
        def _():
            dw_ref[0] = dwg
            dw_ref[1] = dwv
            db_ref[0] = dbg
            db_ref[1] = dbv

        @pl.when(i > 0)
        def _():
            dw_ref[0] += dwg
            dw_ref[1] += dwv
            db_ref[0] += dbg
            db_ref[1] += dbv

    da_cur = pl.BlockSpec((tb, tc), lambda j, i: (i, j))
    da_nxt = pl.BlockSpec((16, tc), lambda j, i: (jnp.minimum((i + 1) * (tb // 16), t // 16 - 1), j))
    return pl.pallas_call(
        body, name="ffn_act_bwd", grid=(D_FF // tc, nrow),
        in_specs=[cur(0), prev(0), nxt(0), cur(1), prev(1), nxt(1), par(taps, 0), par(taps, 1), par(1, 0),
                  par(1, 1), da_cur, da_nxt],
        out_specs=[pl.BlockSpec((2, tb, tc), lambda j, i: (0, i, j)),
                   pl.BlockSpec((2, taps, tc), lambda j, i: (0, 0, j)),
                   pl.BlockSpec((2, 1, tc), lambda j, i: (0, 0, j))],
        out_shape=[jax.ShapeDtypeStruct((2, t, D_FF), BF16), jax.ShapeDtypeStruct((2, taps, D_FF), F32),
                   jax.ShapeDtypeStruct((2, 1, D_FF), F32)],
        compiler_params=_cp(("parallel", "arbitrary")))(u0, u0, u0, u0, u0, u0, w, w, b, b, da, da)


def _head_masks():
    lane = _iota((1, 4 * SSD_HEAD_DIM), 1)
    return [((lane >= r * SSD_HEAD_DIM) & (lane < (r + 1) * SSD_HEAD_DIM)).astype(F32) for r in range(4)]


def _segsum(v):
    first = _iota((1, LANES), 1) < SSD_HEAD_DIM
    halves = []
    for k in range(2):
        vh = v[:, k * LANES:(k + 1) * LANES]
        both = jnp.sum(vh, axis=1, keepdims=True)
        one = jnp.sum(jnp.where(first, vh, 0.0), axis=1, keepdims=True)
        halves.append(jnp.where(first, one, both - one))
    return jnp.concatenate(halves, axis=1)


def _ssd_common(raw_e, prow, rawr4, bcol, acol):
    n = SSD_CHUNK
    dt_e = _softplus(raw_e + prow[0:1, :])
    a_e = -jnp.exp(prow[1:2, :])
    d_e = prow[2:3, :]
    tril = (_iota((n, n), 0) >= _iota((n, n), 1)).astype(F32)
    acs_e = _dot_exact(tril, dt_e * a_e)
    last_e = acs_e[n - 1:n, :]
    dtr4 = _softplus(rawr4 + bcol)
    triu = (_iota((n, n), 0) <= _iota((n, n), 1)).astype(F32)
    acs_r4 = _dot_exact(dtr4 * (-jnp.exp(acol)), triu)
    return dt_e, a_e, d_e, acs_e, last_e, acs_r4


def _decay_matrix(acs_e, acs_r4, r):
    n = SSD_CHUNK
    col = acs_e[:, r * SSD_HEAD_DIM:r * SSD_HEAD_DIM + 1]
    seg = col - acs_r4[r:r + 1, :]
    causal = _iota((n, n), 0) >= _iota((n, n), 1)
    return jnp.exp(jnp.where(causal, seg, NEG))


def _ssd_specs(t, rev):
    nc = t // SSD_CHUNK
    xb, bb, cb = 0, SSD_INNER // SSD_STATE, (SSD_INNER + BC_WIDTH) // SSD_STATE

    def ch(c):
        return (nc - 1 - c) if rev else c

    x = pl.BlockSpec((SSD_CHUNK, 256), lambda g, c: (ch(c), xb + g))
    bm = pl.BlockSpec((SSD_CHUNK, SSD_STATE), lambda g, c: (ch(c), bb + g))
    cm = pl.BlockSpec((SSD_CHUNK, SSD_STATE), lambda g, c: (ch(c), cb + g))
    dtc = pl.BlockSpec((1, SSD_CHUNK, 256), lambda g, c: (g, ch(c), 0))
    dtr = pl.BlockSpec((1, 4, SSD_CHUNK), lambda g, c: (g, 0, ch(c)))
    prow = pl.BlockSpec((1, 3, 256), lambda g, c: (g, 0, 0))
    pcol = pl.BlockSpec((1, 4, 1), lambda g, c: (g, 0, 0))
    st = pl.BlockSpec((1, 1, SSD_STATE, 256), lambda g, c: (g, ch(c), 0, 0))
    return x, bm, cm, dtc, dtr, prow, pcol, st, ch


def _ssd_params(dt_raw, dt_bias, a_log, ssd_d):
    t = dt_raw.shape[0]
    by_group = dt_raw.reshape(t, SSD_GROUPS, 4)
    dtc = jnp.repeat(by_group, SSD_HEAD_DIM, axis=2).transpose(1, 0, 2)
    dtr = by_group.transpose(1, 2, 0)
    prow = jnp.repeat(jnp.stack([dt_bias.reshape(SSD_GROUPS, 4), a_log.reshape(SSD_GROUPS, 4),
                                 ssd_d.reshape(SSD_GROUPS, 4)], axis=1), SSD_HEAD_DIM, axis=2)
    bcol = dt_bias.reshape(SSD_GROUPS, 4, 1)
    acol = a_log.reshape(SSD_GROUPS, 4, 1)
    return dtc, dtr, prow, bcol, acol


def _ssd_fwd(xbc, params):
    t = xbc.shape[0]
    nc = t // SSD_CHUNK
    dtc, dtr, prow, bcol, acol = params

    def body(x_ref, b_ref, c_ref, dtc_ref, dtr_ref, prow_ref, bcol_ref, acol_ref, y_ref, st_ref, s_scr):
        c = pl.program_id(1)

        @pl.when(c == 0)
        def _():
            s_scr[...] = jnp.zeros_like(s_scr)

        masks = _head_masks()
        dt_e, a_e, d_e, acs_e, last_e, acs_r4 = _ssd_common(
            dtc_ref[0], prow_ref[0], dtr_ref[0], bcol_ref[0], acol_ref[0])
        xv = x_ref[...]
        bm, cm = b_ref[...], c_ref[...]
        s = s_scr[...]
        st_ref[0, 0] = s
        xdt = xv * dt_e
        cb = _dot(cm, bm, 'nt')
        y = _dot(cm, s) * jnp.exp(acs_e) + xv * d_e
        for r in range(4):
            mr = cb * _decay_matrix(acs_e, acs_r4, r)
            y = y + _dot(mr, xdt * masks[r])
        y_ref[...] = y
        w = xdt * jnp.exp(last_e - acs_e)
        s_scr[...] = s * jnp.exp(last_e) + _dot(bm.T, w)

    x, bm, cm, dtcs, dtrs, prs, pcs, st, _ = _ssd_specs(t, False)
    return pl.pallas_call(
        body, name="ssd_fwd", grid=(SSD_GROUPS, nc), in_specs=[x, bm, cm, dtcs, dtrs, prs, pcs, pcs],
        out_specs=[pl.BlockSpec((SSD_CHUNK, 256), lambda g, c: (c, g)), st],
        out_shape=[jax.ShapeDtypeStruct((t, SSD_INNER), F32),
                   jax.ShapeDtypeStruct((SSD_GROUPS, nc, SSD_STATE, 256), F32)],
        scratch_shapes=[pltpu.VMEM((SSD_STATE, 256), F32)],
        compiler_params=_cp(("parallel", "arbitrary")))(xbc, xbc, xbc, dtc, dtr, prow, bcol, acol)


def _ssd_bwd(xbc, params, states, dy):
    t = xbc.shape[0]
    nc = t // SSD_CHUNK
    n = SSD_CHUNK
    dtc, dtr, prow, bcol, acol = params

    def body(x_ref, b_ref, c_ref, dtc_ref, dtr_ref, prow_ref, bcol_ref, acol_ref, st_ref, dy_ref,
             dx_ref, db_ref, dc_ref, ddt_ref, dp_ref, ds_scr):
        c = pl.program_id(1)

        @pl.when(c == 0)
        def _():
            ds_scr[...] = jnp.zeros_like(ds_scr)
            dp_ref[...] = jnp.zeros_like(dp_ref)

        masks = _head_masks()
        raw_e = dtc_ref[0]
        prw = prow_ref[0]
        dt_e, a_e, d_e, acs_e, last_e, acs_r4 = _ssd_common(raw_e, prw, dtr_ref[0], bcol_ref[0], acol_ref[0])
        xv = x_ref[...]
        bm, cm = b_ref[...], c_ref[...]
        s = st_ref[0, 0]
        ds = ds_scr[...]
        dyv = dy_ref[...]
        e_e = jnp.exp(acs_e)
        dec_e = jnp.exp(last_e - acs_e)
        cd_e = jnp.exp(last_e)
        xdt = xv * dt_e
        w = xdt * dec_e
        b16, c16, s16, ds16 = bm.astype(BF16), cm.astype(BF16), s.astype(BF16), ds.astype(BF16)
        cb = _dot(c16, b16, 'nt')
        yoff_raw = _dot(c16, s16)
        dye = dyv * e_e
        dye16 = dye.astype(BF16)
        dcm = _dot(dye16, s16, 'nt')
        ds_scr[...] = ds * cd_e + _dot(cm.T, dye16)
        dacs_e = _segsum(dyv * yoff_raw) * e_e
        dw = _dot(b16, ds16)
        dbm = _dot(w, ds16, 'nt')
        tdec = _segsum(dw * xdt) * dec_e
        dacs_e = dacs_e - tdec
        dlast_e = jnp.sum(tdec, axis=0, keepdims=True)
        dxdt = dw * dec_e
        dlast_e = dlast_e + _segsum(jnp.sum(ds * s, axis=0, keepdims=True)) * cd_e
        dcb = jnp.zeros((n, n), F32)
        for r in range(4):
            lm = _decay_matrix(acs_e, acs_r4, r)
            mr = cb * lm
            dyr16 = (dyv * masks[r]).astype(BF16)
            dm = _dot(dyr16, xdt * masks[r], 'nt')
            dcb = dcb + dm * lm
            dseg = dm * mr
            dcol = jnp.sum(dseg, axis=1, keepdims=True) - jnp.sum(dseg.T, axis=1, keepdims=True)
            dacs_e = dacs_e + dcol * masks[r]
            dxdt = dxdt + _dot(mr.T, dyr16)
        dcm = dcm + _dot(dcb, b16)
        dbm = dbm + _dot(dcb.T, c16)
        dacs_e = dacs_e + jnp.where(_iota((n, 1), 0) == n - 1, dlast_e, 0.0)
        triu = (_iota((n, n), 0) <= _iota((n, n), 1)).astype(F32)
        ddta_e = _dot_exact(triu, dacs_e)
        ddt_e = ddta_e * a_e + _segsum(dxdt * xv)
        dx_ref[...] = dxdt * dt_e + dyv * d_e
        db_ref[...] = dbm
        dc_ref[...] = dcm
        draw_e = ddt_e * _sigmoid(raw_e + prw[0:1, :])
        draw_t = draw_e.T
        ddt_ref[0] = jnp.concatenate([draw_t[r * SSD_HEAD_DIM:r * SSD_HEAD_DIM + 1] for r in range(4)], axis=0)
        dbias = jnp.sum(draw_e, axis=0, keepdims=True)
        dalog = jnp.sum(ddta_e * dt_e, axis=0, keepdims=True) * a_e
        dd = _segsum(jnp.sum(dyv * xv, axis=0, keepdims=True))
        row3 = _iota((3, 1), 0)
        dp_ref[0] += (jnp.where(row3 == 0, dbias, 0.0) + jnp.where(row3 == 1, dalog, 0.0)
                      + jnp.where(row3 == 2, dd, 0.0))

    x, bm, cm, dtcs, dtrs, prs, pcs, st, ch = _ssd_specs(t, True)
    yblk = pl.BlockSpec((SSD_CHUNK, 256), lambda g, c: (ch(c), g))
    nblk = pl.BlockSpec((SSD_CHUNK, SSD_STATE), lambda g, c: (ch(c), g))
    return pl.pallas_call(
        body, name="ssd_bwd", grid=(SSD_GROUPS, nc),
        in_specs=[x, bm, cm, dtcs, dtrs, prs, pcs, pcs, st, yblk],
        out_specs=[yblk, nblk, nblk, dtrs, prs],
        out_shape=[jax.ShapeDtypeStruct((t, SSD_INNER), F32), jax.ShapeDtypeStruct((t, BC_WIDTH), F32),
                   jax.ShapeDtypeStruct((t, BC_WIDTH), F32), jax.ShapeDtypeStruct((SSD_GROUPS, 4, t), F32),
                   jax.ShapeDtypeStruct((SSD_GROUPS, 3, 256), F32)],
        scratch_shapes=[pltpu.VMEM((SSD_STATE, 256), F32)],
        compiler_params=_cp(("parallel", "arbitrary")))(xbc, xbc, xbc, dtc, dtr, prow, bcol, acol, states, dy)


GROUP_W = SSD_INNER // SSD_GROUPS


def _mix_specs(tb):
    row = pl.BlockSpec((tb, 2048), lambda i: (i, 0))
    zlo = pl.BlockSpec((tb, 1024), lambda i: (i, O_Z // 1024))
    zhi = pl.BlockSpec((tb, 1024), lambda i: (i, O_Z // 1024 + 1))
    vec = pl.BlockSpec((1, 2048), lambda i: (0, 0))
    return row, zlo, zhi, vec


def _mix_fwd(attn, y, proj, g_attn, g_ssd):
    t = attn.shape[0]
    tb = _rows(t, 256)

    def body(a_ref, y_ref, zlo_ref, zhi_ref, ga_ref, gs_ref, o_ref):
        av = a_ref[...]
        r = lax.rsqrt(jnp.mean(av * av, axis=-1, keepdims=True) + EPS)
        o_ref[:, :ATTN_WIDTH] = (av * r * ga_ref[...]).astype(BF16)
        for g in range(SSD_GROUPS):
            lo, hi = g * GROUP_W, (g + 1) * GROUP_W
            zref = zlo_ref if g < 4 else zhi_ref
            z = zref[:, lo % 1024:lo % 1024 + GROUP_W]
            yg = y_ref[:, lo:hi] * (z * _sigmoid(z))
            rg = lax.rsqrt(jnp.mean(yg * yg, axis=-1, keepdims=True) + EPS)
            o_ref[:, ATTN_WIDTH + lo:ATTN_WIDTH + hi] = (yg * rg * gs_ref[:, lo:hi]).astype(BF16)

    row, zlo, zhi, vec = _mix_specs(tb)
    return pl.pallas_call(
        body, name="mix_fwd", grid=(t // tb,), in_specs=[row, row, zlo, zhi, vec, vec],
        out_specs=pl.BlockSpec((tb, 4096), lambda i: (i, 0)), out_shape=jax.ShapeDtypeStruct((t, 4096), BF16),
        compiler_params=_cp(("parallel",)))(attn, y, proj, proj, g_attn, g_ssd)


def _mix_bwd(dmix, attn, y, proj, g_attn, g_ssd):
    t = attn.shape[0]
    tb = _rows(t, 256)

    def body(dm_ref, a_ref, y_ref, zlo_ref, zhi_ref, ga_ref, gs_ref, da_ref, dy_ref, dz_ref, dga_ref, dgs_ref):
        i = pl.program_id(0)
        av = a_ref[...]
        dn = dm_ref[:, :ATTN_WIDTH].astype(F32)
        r = lax.rsqrt(jnp.mean(av * av, axis=-1, keepdims=True) + EPS)
        u = dn * ga_ref[...]
        da_ref[...] = r * u - av * (r * r * r * jnp.mean(u * av, axis=-1, keepdims=True))
        dga = jnp.sum(dn * av * r, axis=0, keepdims=True)

        @pl.when(i == 0)
        def _():
            dga_ref[...] = dga

        @pl.when(i > 0)
        def _():
            dga_ref[...] += dga

        for g in range(SSD_GROUPS):
            lo, hi = g * GROUP_W, (g + 1) * GROUP_W
            zref = zlo_ref if g < 4 else zhi_ref
            z = zref[:, lo % 1024:lo % 1024 + GROUP_W]
            yv = y_ref[:, lo:hi]
            sg = _sigmoid(z)
            sz = z * sg
            yg = yv * sz
            rg = lax.rsqrt(jnp.mean(yg * yg, axis=-1, keepdims=True) + EPS)
            do = dm_ref[:, ATTN_WIDTH + lo:ATTN_WIDTH + hi].astype(F32)
            ug = do * gs_ref[:, lo:hi]
            dyg = rg * ug - yg * (rg * rg * rg * jnp.mean(ug * yg, axis=-1, keepdims=True))
            dy_ref[:, lo:hi] = dyg * sz
            dz_ref[:, lo:hi] = (dyg * yv * (sg * (1.0 + z * (1.0 - sg)))).astype(BF16)
            dgs = jnp.sum(do * yg * rg, axis=0, keepdims=True)

            @pl.when(i == 0)
            def _():
                dgs_ref[:, lo:hi] = dgs

            @pl.when(i > 0)
            def _():
                dgs_ref[:, lo:hi] += dgs

    row, zlo, zhi, vec = _mix_specs(tb)
    return pl.pallas_call(
        body, name="mix_bwd", grid=(t // tb,),
        in_specs=[pl.BlockSpec((tb, 4096), lambda i: (i, 0)), row, row, zlo, zhi, vec, vec],
        out_specs=[row, row, row, vec, vec],
        out_shape=[jax.ShapeDtypeStruct((t, 2048), F32), jax.ShapeDtypeStruct((t, 2048), F32),
                   jax.ShapeDtypeStruct((t, 2048), BF16), jax.ShapeDtypeStruct((1, 2048), F32),
                   jax.ShapeDtypeStruct((1, 2048), F32)],
        compiler_params=_cp(("arbitrary",)))(dmix, attn, y, proj, proj, g_attn, g_ssd)


def _adamw(w, g, m, v, name):
    r, c = w.shape
    tb = _rows(r, 256)
    c1 = 1.0 - ADAM_B1 ** ADAM_STEP
    c2 = 1.0 - ADAM_B2 ** ADAM_STEP

    def body(w_ref, g_ref, m_ref, v_ref, d_ref, m2_ref, v2_ref):
        gv = g_ref[...]
        m2 = ADAM_B1 * m_ref[...] + (1.0 - ADAM_B1) * gv
        v2 = ADAM_B2 * v_ref[...] + (1.0 - ADAM_B2) * (gv * gv)
        d_ref[...] = -ADAM_LR * ((m2 / c1) / (jnp.sqrt(v2 / c2) + ADAM_EPS) + ADAM_WD * w_ref[...])
        m2_ref[...] = m2
        v2_ref[...] = v2

    blk = pl.BlockSpec((tb, c), lambda i: (i, 0))
    shp = jax.ShapeDtypeStruct((r, c), F32)
    return pl.pallas_call(body, name=name, grid=(r // tb,), in_specs=[blk] * 4, out_specs=[blk] * 3,
                          out_shape=[shp] * 3, compiler_params=_cp(("parallel",)))(w, g, m, v)


def _adamw_halves(w, mine, theirs, m, v, pos, name, cols=False):
    r, c = w.shape
    h = r if cols else r // 2
    tb = _rows(h, 128)
    nh = h // tb
    c1 = 1.0 - ADAM_B1 ** ADAM_STEP
    c2 = 1.0 - ADAM_B2 ** ADAM_STEP

    def body(pos_ref, w_ref, a_ref, b_ref, m_ref, v_ref, g_ref, d_ref, m2_ref, v2_ref):
        which = pl.program_id(1) if cols else pl.program_id(0) // nh
        gv = jnp.where(which == pos_ref[0], a_ref[...], b_ref[...])
        m2 = ADAM_B1 * m_ref[...] + (1.0 - ADAM_B1) * gv
        v2 = ADAM_B2 * v_ref[...] + (1.0 - ADAM_B2) * (gv * gv)
        g_ref[...] = gv
        d_ref[...] = -ADAM_LR * ((m2 / c1) / (jnp.sqrt(v2 / c2) + ADAM_EPS) + ADAM_WD * w_ref[...])
        m2_ref[...] = m2
        v2_ref[...] = v2

    if cols:
        full = pl.BlockSpec((tb, c // 2), lambda i, j, pref: (i, j))
        half = pl.BlockSpec((tb, c // 2), lambda i, j, pref: (i, 0))
        grid = (nh, 2)
    else:
        full = pl.BlockSpec((tb, c), lambda i, pref: (i, 0))
        half = pl.BlockSpec((tb, c), lambda i, pref: (i % nh, 0))
        grid = (r // tb,)
    shp = jax.ShapeDtypeStruct((r, c), F32)
    grid_spec = pltpu.PrefetchScalarGridSpec(num_scalar_prefetch=1, grid=grid,
                                             in_specs=[full, half, half, full, full], out_specs=[full] * 4)
    return pl.pallas_call(body, name=name, grid_spec=grid_spec, out_shape=[shp] * 4,
                          compiler_params=_cp(("parallel",) * len(grid)))(pos, w, mine, theirs, m, v)


def _sum_own_half(g4, recv, pos, name, cols=False):
    _, r, c = g4.shape
    h, c = (r, c // 2) if cols else (r // 2, c)
    tb = _rows(h, 128)
    nh = h // tb
    own = (lambda j, i, pref: (j, i, pref[0])) if cols else (lambda j, i, pref: (j, pref[0] * nh + i, 0))

    def body(pos_ref, a_ref, b_ref, o_ref):
        o_ref[...] = (a_ref[...] + b_ref[...]).astype(BF16)

    grid_spec = pltpu.PrefetchScalarGridSpec(
        num_scalar_prefetch=1, grid=(N_CHIPS, nh),
        in_specs=[pl.BlockSpec((1, tb, c), own), pl.BlockSpec((1, tb, c), lambda j, i, pref: (j, i, 0))],
        out_specs=pl.BlockSpec((1, tb, c), lambda j, i, pref: (j, i, 0)))
    return pl.pallas_call(body, name=name, grid_spec=grid_spec,
                          out_shape=jax.ShapeDtypeStruct((N_CHIPS, h, c), BF16),
                          compiler_params=_cp(("parallel", "parallel")))(pos, g4, recv)


def _sum_chips(g4, recv, parts, pos, name, cols=False):
    _, r, c = g4.shape
    h, c = (r, c // 2) if cols else (r // 2, c)
    tb = _rows(h, 128)
    nh = h // tb
    own = (lambda i, pref: (pref[1], i, pref[0])) if cols else (lambda i, pref: (pref[1], pref[0] * nh + i, 0))

    def body(pos_ref, a_ref, b_ref, p_ref, o_ref):
        own = a_ref[0] + b_ref[0]
        o_ref[...] = ((own + p_ref[0].astype(F32)) + p_ref[1].astype(F32)) + p_ref[2].astype(F32)

    grid_spec = pltpu.PrefetchScalarGridSpec(
        num_scalar_prefetch=1, grid=(nh,),
        in_specs=[pl.BlockSpec((1, tb, c), own),
                  pl.BlockSpec((1, tb, c), lambda i, pref: (pref[1], i, 0)),
                  pl.BlockSpec((3, tb, c), lambda i, pref: (0, i, 0))],
        out_specs=pl.BlockSpec((tb, c), lambda i, pref: (i, 0)))
    return pl.pallas_call(body, name=name, grid_spec=grid_spec, out_shape=jax.ShapeDtypeStruct((h, c), F32),
                          compiler_params=_cp(("parallel",)))(pos, g4, recv, parts)


def _me():
    return lax.axis_index("x"), lax.axis_index("y"), lax.axis_index("c")


def _flip(v, bit):
    return (1 - v) if bit else v


CHIP_FLIPS = [(1, 0), (0, 1), (1, 1)]


def _allgather_weights(shards, after, cols=False):
    n = len(shards)

    def body(*refs):
        ins, outs, token = refs[:n], refs[n + 1:2 * n + 1], refs[2 * n + 1]
        send_sems, recv_sems = refs[2 * n + 2:]
        x, y, c = _me()
        chip = 2 * x + y
        sib = (x, y, 1 - c)

        def remote(src, dst, k, to):
            return pltpu.make_async_remote_copy(src_ref=src, dst_ref=dst, send_sem=send_sems.at[k],
                                                recv_sem=recv_sems.at[k], device_id=to, device_id_type=MESH)

        def half(ref, which):
            if cols:
                h = ref.shape[1] // 2
                return ref.at[:, pl.ds(which * h, h)]
            h = ref.shape[0] // 2
            return ref.at[pl.ds(which * h, h)]

        sends = []
        for t in range(n):
            for k, (fx, fy) in enumerate(CHIP_FLIPS):
                cp = remote(half(ins[t], c), half(outs[t].at[chip], c), 6 * t + k, (_flip(x, fx), _flip(y, fy), c))
                cp.start()
                sends.append(cp)
        for t in range(n):
            for k, (fx, fy) in enumerate(CHIP_FLIPS):
                landed = half(outs[t].at[2 * _flip(x, fx) + _flip(y, fy)], c)
                remote(landed, landed, 6 * t + k, (x, y, c)).wait_recv()
                fw = remote(landed, landed, 6 * t + 3 + k, sib)
                fw.start()
                sends.append(fw)
        for t in range(n):
            for k, (fx, fy) in enumerate(CHIP_FLIPS):
                got = half(outs[t].at[2 * _flip(x, fx) + _flip(y, fy)], 1 - c)
                remote(got, got, 6 * t + 3 + k, (x, y, c)).wait_recv()
        for cp in sends:
            cp.wait_send()
        token[...] = jnp.zeros_like(token)

    outs = pl.pallas_call(
        body, name="allgather_weights", in_specs=[HBM_SPEC] * n + [pl.BlockSpec(memory_space=pl.ANY)],
        out_specs=[HBM_SPEC] * n + [pl.BlockSpec(memory_space=pltpu.VMEM)],
        out_shape=[jax.ShapeDtypeStruct((N_CHIPS,) + s.shape, s.dtype) for s in shards] + [TOKEN],
        scratch_shapes=[pltpu.SemaphoreType.DMA((6 * n,)), pltpu.SemaphoreType.DMA((6 * n,))],
        compiler_params=pltpu.CompilerParams(has_side_effects=True))(*shards, after)
    return list(outs[:n]), outs[n]


def _exchange_halves(g4s, name, cols=False):
    n = len(g4s)

    def land(g):
        return (N_CHIPS, g.shape[1], g.shape[2] // 2) if cols else (N_CHIPS, g.shape[1] // 2, g.shape[2])

    def body(*refs):
        ins, outs = refs[:n], refs[n:2 * n]
        send_sems, recv_sems = refs[2 * n:]
        x, y, c = _me()
        cps = []
        for t in range(n):
            if cols:
                h = ins[t].shape[2] // 2
                theirs = ins[t].at[:, :, pl.ds((1 - c) * h, h)]
            else:
                h = ins[t].shape[1] // 2
                theirs = ins[t].at[:, pl.ds((1 - c) * h, h)]
            cp = pltpu.make_async_remote_copy(
                src_ref=theirs, dst_ref=outs[t], send_sem=send_sems.at[t],
                recv_sem=recv_sems.at[t], device_id=(x, y, 1 - c), device_id_type=MESH)
            cp.start()
            cps.append(cp)
        for cp in cps:
            cp.wait()

    return pl.pallas_call(
        body, name=name, in_specs=[HBM_SPEC] * n, out_specs=[HBM_SPEC] * n,
        out_shape=[jax.ShapeDtypeStruct(land(g), g.dtype) for g in g4s],
        scratch_shapes=[pltpu.SemaphoreType.DMA((n,)), pltpu.SemaphoreType.DMA((n,))],
        compiler_params=pltpu.CompilerParams(has_side_effects=True))(*g4s)


def _share_halves(ghs, name):
    n = len(ghs)

    def body(*refs):
        ins, outs = refs[:n], refs[n:2 * n]
        send_sems, recv_sems = refs[2 * n:]
        x, y, c = _me()
        cps = []
        for t in range(n):
            cp = pltpu.make_async_remote_copy(
                src_ref=ins[t], dst_ref=outs[t], send_sem=send_sems.at[t], recv_sem=recv_sems.at[t],
                device_id=(x, y, 1 - c), device_id_type=MESH)
            cp.start()
            cps.append(cp)
        for cp in cps:
            cp.wait()

    return pl.pallas_call(
        body, name=name, in_specs=[HBM_SPEC] * n, out_specs=[HBM_SPEC] * n,
        out_shape=[jax.ShapeDtypeStruct(g.shape, g.dtype) for g in ghs],
        scratch_shapes=[pltpu.SemaphoreType.DMA((n,)), pltpu.SemaphoreType.DMA((n,))],
        compiler_params=pltpu.CompilerParams(has_side_effects=True))(*ghs)


SEM_SPEC = pl.BlockSpec(memory_space=pltpu.SEMAPHORE)
ANY_SPEC = pl.BlockSpec(memory_space=pl.ANY)
DATAFLOW = pltpu.SideEffectType.DATAFLOW_SIDE_EFFECTING


def _in_hbm(a):
    return pltpu.with_memory_space_constraint(a, pltpu.HBM)


def _push_start(srcs, land_shapes, route, peers, name):
    n, npeer = len(srcs), len(peers)
    lands = [lax.empty(shp, s.dtype) for shp, s in zip(land_shapes, srcs)]

    def body(*refs):
        ins, lnd = refs[:n], refs[n:2 * n]
        send_sems, recv_sems = refs[2 * n], refs[2 * n + 1]
        token = refs[-1]
        x, y, c = _me()
        for t in range(n):
            for k, (fx, fy, fc) in enumerate(peers):
                src, dst = route(ins[t], lnd[t], k, x, y, c)
                pltpu.make_async_remote_copy(
                    src_ref=src, dst_ref=dst, send_sem=send_sems.at[npeer * t + k],
                    recv_sem=recv_sems.at[npeer * t + k],
                    device_id=(_flip(x, fx), _flip(y, fy), _flip(c, fc)), device_id_type=MESH).start()
        token[...] = jnp.zeros_like(token)

    bufs = [_in_hbm(a) for a in list(srcs) + lands]
    outs = pl.pallas_call(
        body, name=name,
        out_shape=(pltpu.SemaphoreType.DMA((npeer * n,)), pltpu.SemaphoreType.DMA((npeer * n,)),
                   *[pltpu.HBM(b.shape, b.dtype) for b in bufs], TOKEN),
        in_specs=[HBM_SPEC] * (2 * n),
        out_specs=(SEM_SPEC, SEM_SPEC, *[HBM_SPEC] * (2 * n), pl.BlockSpec(memory_space=pltpu.VMEM)),
        input_output_aliases={i: 2 + i for i in range(2 * n)},
        compiler_params=pltpu.CompilerParams(has_side_effects=DATAFLOW))(*bufs)
    return outs[0], outs[1], list(outs[2:2 + n]), list(outs[2 + n:2 + 2 * n]), outs[-1]


def _push_wait(send_sems, recv_sems, srcs, lands, after, route, peers, name):
    n, npeer = len(srcs), len(peers)

    def body(*refs):
        ins, lnd = refs[:n], refs[n:2 * n]
        ssem, rsem = refs[2 * n], refs[2 * n + 1]
        x, y, c = _me()
        for t in range(n):
            for k, (fx, fy, fc) in enumerate(peers):
                src, dst = route(ins[t], lnd[t], k, x, y, c)
                cp = pltpu.make_async_remote_copy(
                    src_ref=src, dst_ref=dst, send_sem=ssem.at[npeer * t + k], recv_sem=rsem.at[npeer * t + k],
                    device_id=(_flip(x, fx), _flip(y, fy), _flip(c, fc)), device_id_type=MESH)
                cp.wait_send()
                cp.wait_recv()

    bufs = list(srcs) + list(lands)
    outs = pl.pallas_call(
        body, name=name, out_shape=tuple(pltpu.HBM(b.shape, b.dtype) for b in bufs),
        in_specs=[HBM_SPEC] * (2 * n) + [SEM_SPEC, SEM_SPEC, ANY_SPEC], out_specs=tuple([HBM_SPEC] * (2 * n)),
        input_output_aliases={i: i for i in range(2 * n)},
        compiler_params=pltpu.CompilerParams(has_side_effects=DATAFLOW))(*bufs, send_sems, recv_sems, after)
    return list(outs[:n]), list(outs[n:])


OTHER_CHIPS = [(fx, fy, 0) for fx, fy in CHIP_FLIPS]
SIBLING = [(0, 0, 1)]


def _route_gather(src, land, k, x, y, c):
    return src, land.at[2 * x + y]


def _route_gather_wait(src, land, k, x, y, c):
    fx, fy = CHIP_FLIPS[k]
    return src, land.at[2 * _flip(x, fx) + _flip(y, fy)]


def _route_scatter(src, land, k, x, y, c):
    fx, fy = CHIP_FLIPS[k]
    return src.at[2 * _flip(x, fx) + _flip(y, fy)], land.at[k]


def _route_exchange(src, land, k, x, y, c):
    h = land.shape[1]
    return src.at[:, pl.ds((1 - c) * h, h)], land


def _allreduce_small(v):
    r = v.shape[0]

    def body(v_ref, o_ref, buf, send_sems, recv_sems):
        x, y, c = _me()
        me = 4 * x + 2 * y + c
        buf[0] = v_ref[...]
        cps = []
        for k in range(1, 8):
            kx, ky, kc = (k >> 2) & 1, (k >> 1) & 1, k & 1
            cp = pltpu.make_async_remote_copy(
                src_ref=v_ref, dst_ref=buf.at[k], send_sem=send_sems.at[k - 1], recv_sem=recv_sems.at[k - 1],
                device_id=(_flip(x, kx), _flip(y, ky), _flip(c, kc)), device_id_type=MESH)
            cp.start()
            cps.append(cp)
        for cp in cps:
            cp.wait()
        acc = buf[me]
        for d in range(1, 8):
            acc = acc + buf[jnp.bitwise_xor(me, d)]
        o_ref[...] = acc

    vm = pl.BlockSpec(memory_space=pltpu.VMEM)
    return pl.pallas_call(
        body, name="allreduce_small", in_specs=[vm], out_specs=vm, out_shape=jax.ShapeDtypeStruct(v.shape, F32),
        scratch_shapes=[pltpu.VMEM((8, r, LANES), F32), pltpu.SemaphoreType.DMA((7,)),
                        pltpu.SemaphoreType.DMA((7,))],
        compiler_params=pltpu.CompilerParams(has_side_effects=True, vmem_limit_bytes=VMEM_LIMIT))(v)


def _grad_exchange_start(g4, tag):
    land = (N_CHIPS, g4.shape[1] // 2, g4.shape[2])
    send_sems, recv_sems, srcs, lands, token = _push_start(
        [g4], [land], _route_exchange, SIBLING, name="grad_exchange_start_" + tag)
    return (send_sems, recv_sems, srcs, lands, tag), token


def _grad_scatter_start(state, pos, after):
    send_sems, recv_sems, srcs, lands, tag = state
    (g4,), (recv,) = _push_wait(send_sems, recv_sems, srcs, lands, after, _route_exchange, SIBLING,
                                name="grad_exchange_wait_" + tag)
    return _grad_pair_scatter(g4, recv, pos, tag)


def _grad_pair_scatter(g4, recv, pos, tag, cols=False):
    p16 = _sum_own_half(g4, recv, pos, name="grad_sum_pair_" + tag, cols=cols)
    send_sems, recv_sems, srcs, lands, token = _push_start(
        [p16], [(3,) + p16.shape[1:]], _route_scatter, OTHER_CHIPS, name="grad_scatter_start_" + tag)
    return (g4, recv, send_sems, recv_sems, srcs, lands, tag, cols), token


def _grad_reduce_begin(g4, pos, tag, cols=False):
    recv = _exchange_halves([g4], name="grad_exchange_halves_" + tag, cols=cols)[0]
    return _grad_pair_scatter(g4, recv, pos, tag, cols)


def _grad_reduce_finish(state, pos, after):
    g4, recv, send_sems, recv_sems, srcs, lands, tag, cols = state
    parts = _push_wait(send_sems, recv_sems, srcs, lands, after, _route_scatter, OTHER_CHIPS,
                       name="grad_scatter_wait_" + tag)[1][0]
    mine = _sum_chips(g4, recv, parts, pos, name="grad_sum_chips_" + tag, cols=cols)
    return mine, _share_halves([mine], name="grad_share_halves_" + tag)[0]


def _local_step(x, tgt, p, w_in_t, w_in_dt, hooks):
    t = x.shape[0]
    tables = _rope_tables(t)
    sinks = p['sinks'].reshape(N_Q_HEADS)

    def told(name, value):
        return tuple(hooks.grad_ready(name, value))

    xn = _rmsnorm_fwd(x, p['norm_mix'], "norm_mix_fwd", deps=hooks.first_deps)
    proj = _matmul(xn, w_in_t, mode='nt', name="in_proj", n_limit=MAIN_WIDTH)
    dt_raw = _matmul(xn, w_in_dt, mode='nt', name="in_proj_dt")[:, :SSD_HEADS]
    attn = _attn_fwd(proj, sinks, tables)
    conv_b = p['ssd_conv_b']
    xbc = _conv_fwd(proj, p['ssd_conv_w'], conv_b, col0=O_XBC, width=CONV_CH, act=True, name="ssd_conv_fwd")
    sp = _ssd_params(dt_raw, p['dt_bias'].reshape(-1), p['a_log'].reshape(-1), p['ssd_d'].reshape(-1))
    y, states = _ssd_fwd(xbc, sp)
    mix = _mix_fwd(attn, y, proj, p['attn_out_norm'], p['ssd_norm'])
    w_out = hooks.weight('w_out', mix)
    h1 = _matmul(mix, w_out, mode='nn', name="out_proj", add=x)
    hn = _rmsnorm_fwd(h1, p['norm_ffn'], "norm_ffn_fwd")
    w_up = hooks.weight('w_up', hn)
    u0 = _matmul(hn, w_up, mode='nn', name="ffn_up", b_owner=True, tn=1408)
    a = _ffn_act_fwd(u0, p['ffn_conv_w'], p['ffn_conv_b'])
    w_down = hooks.weight('w_down', a)
    h2 = _matmul(a, w_down, mode='nn', name="ffn_down", add=h1, tk=1408)
    loss, dh2, g_norm_final = _final_loss(h2, p['norm_final'].reshape(1, D_MODEL), tgt)

    g = {}
    da = _matmul(dh2, w_down, mode='nt', name="ffn_down_dx", out_dtype=BF16, tn=1408)
    g['w_down'] = _matmul(a, dh2, mode='tn', name="ffn_down_dw", tm=1408, tk=1024)
    dep = told('w_down', g['w_down'])
    du0, dcw, dcb = _ffn_act_bwd(u0, p['ffn_conv_w'], p['ffn_conv_b'], da)
    g['ffn_conv_w'] = dcw.transpose(1, 0, 2).reshape(FFN_CONV, 2 * D_FF)
    g['ffn_conv_b'] = dcb.transpose(1, 0, 2).reshape(1, 2 * D_FF)
    g['w_up'] = _matmul(hn, du0, mode='tn', name="ffn_up_dw", deps=dep, b_halves=True, owner_major=True,
                        tn=1408)
    dep = told('w_up', g['w_up'])
    dhn = _matmul(du0, w_up, mode='nt', name="ffn_up_dx", out_dtype=BF16, deps=dep, a_halves=True,
                  b_owner=True, tk=2816)
    dh1, g['norm_ffn'] = _rmsnorm_bwd(h1, p['norm_ffn'], dhn, dh2, "norm_ffn_bwd")

    g['w_out'] = _matmul(mix, dh1, mode='tn', name="out_proj_dw")
    dep = told('w_out', g['w_out'])
    dmix = _matmul(dh1, w_out, mode='nt', name="out_proj_dx", out_dtype=BF16, deps=dep)
    dattn, dy, dz, g['attn_out_norm'], g['ssd_norm'] = _mix_bwd(dmix, attn, y, proj, p['attn_out_norm'],
                                                                p['ssd_norm'])
    dq, dk, dv, dsink = _attn_bwd(proj, sinks, tables, dattn)
    g['sinks'] = dsink[:, :, 0].reshape(1, N_Q_HEADS)
    dxs, dbm, dcm, ddt8, dpar = _ssd_bwd(xbc, sp, states, dy)
    dpar = dpar[:, :, ::SSD_HEAD_DIM]
    g['dt_bias'] = dpar[:, 0, :].reshape(1, SSD_HEADS)
    g['a_log'] = dpar[:, 1, :].reshape(1, SSD_HEADS)
    g['ssd_d'] = dpar[:, 2, :].reshape(1, SSD_HEADS)
    dxbc_act = jnp.concatenate([dxs, dbm, dcm], axis=1)
    dconv = _conv_silu_dact(proj, p['ssd_conv_w'], conv_b, dxbc_act, col0=O_XBC, width=CONV_CH,
                            name="ssd_conv_dact")
    dxbc, g['ssd_conv_w'], g['ssd_conv_b'] = _conv_bwd(proj, p['ssd_conv_w'], dconv, col0=O_XBC, width=CONV_CH,
                                                       name="ssd_conv_bwd")
    dproj = jnp.concatenate([dq, dk, dv, dz, dxbc], axis=1)
    ddt = ddt8.transpose(2, 0, 1).reshape(t, SSD_HEADS)
    ddt_pad = jnp.pad(ddt, ((0, 0), (0, LANES - SSD_HEADS))).astype(BF16)
    g['w_in'] = (_matmul(dproj, xn, mode='tn', name="in_proj_dw", m_rows=IN_PROJ_WIDTH),
                 _matmul(ddt_pad, xn, mode='tn', name="in_proj_dt_dw"))
    dep = told('w_in', g['w_in'])
    dxn_dt = _matmul(ddt_pad, w_in_dt, mode='nn', name="in_proj_dt_dx", deps=dep)
    dxn = _matmul(dproj, w_in_t, mode='nn', name="in_proj_dx", out_dtype=BF16, add=dxn_dt, k_limit=MAIN_WIDTH,
                  tk=2304)
    dx, g['norm_mix'] = _rmsnorm_bwd(x, p['norm_mix'], dxn, dh1, "norm_mix_bwd")
    g['norm_final'] = g_norm_final
    return loss, dx, g


def _pack(arrs):
    flat = jnp.concatenate([a.reshape(-1) for a in arrs])
    n = flat.shape[0]
    rows = -(-n // LANES)
    rows = -(-rows // 8) * 8
    return jnp.pad(flat, (0, rows * LANES - n)).reshape(rows, LANES)


def _unpack(packed, shapes):
    flat = packed.reshape(-1)
    out, off = [], 0
    for s in shapes:
        n = 1
        for d in s:
            n *= d
        out.append(flat[off:off + n].reshape(s))
        off += n
    return out


class _StepHooks:
    def __init__(self, first_deps, weight, grad_ready):
        self.first_deps = first_deps
        self.weight = weight
        self.grad_ready = grad_ready


def kernel(x, norm_mix, w_in, sinks, attn_out_norm, ssd_conv_w, ssd_conv_b, dt_bias, a_log, ssd_d, ssd_norm, w_out, norm_ffn, w_up, ffn_conv_w, ffn_conv_b, w_down, norm_final, loss_target, m_norm_mix, m_w_in, m_sinks, m_attn_out_norm, m_ssd_conv_w, m_ssd_conv_b, m_dt_bias, m_a_log, m_ssd_d, m_ssd_norm, m_w_out, m_norm_ffn, m_w_up, m_ffn_conv_w, m_ffn_conv_b, m_w_down, m_norm_final, v_norm_mix, v_w_in, v_sinks, v_attn_out_norm, v_ssd_conv_w, v_ssd_conv_b, v_dt_bias, v_a_log, v_ssd_d, v_ssd_norm, v_w_out, v_norm_ffn, v_w_up, v_ffn_conv_w, v_ffn_conv_b, v_w_down, v_norm_final):
    args = dict(locals())
    w = {n: args[n] for n in WEIGHTS}
    m = {n: args['m_' + n] for n in WEIGHTS}
    v = {n: args['v_' + n] for n in WEIGHTS}
    xi, yi, ci = _me()
    chip = 2 * xi + yi
    pos = jnp.stack([ci, chip]).astype(jnp.int32)

    def place(shard, full_cols):
        z = jnp.zeros((shard.shape[0], full_cols), F32)
        return lax.dynamic_update_slice(z, shard * 0.5, (0, chip * shard.shape[1]))

    conv_pack = _pack([place(ssd_conv_w[0], CONV_CH), place(ffn_conv_w[0], 2 * D_FF)])
    conv_full = _allreduce_small(conv_pack)
    ssd_conv_w_full, ffn_conv_w_full = _unpack(conv_full, [(SSD_CONV, CONV_CH), (FFN_CONV, 2 * D_FF)])

    w_in_t, m_in_t, v_in_t = (jnp.transpose(a[0]) for a in (w_in, m_w_in, v_w_in))
    in_shard = w_in_t.astype(BF16)
    (gathered,), order = _allgather_weights([in_shard], conv_full, cols=True)
    full_in_t = lax.dynamic_update_slice(gathered, in_shard[None], (chip, 0, 0)).reshape(IN_PROJ_WIDTH, D_MODEL)
    w_in_dt = jnp.pad(full_in_t[MAIN_WIDTH:], ((0, LANES - SSD_HEADS), (0, 0)))
    gathers = {}
    order = order[:1, :1]
    for n, shard in (('w_out', w_out[0]), ('w_up', w_up[0]), ('w_down', w_down[0])):
        shard = (shard + order).astype(BF16)
        gathers[n] = _push_start([shard], [(N_CHIPS,) + shard.shape], _route_gather, OTHER_CHIPS,
                                 name="gather_start_" + n)
        order = gathers[n][4][:1, :1]
    first_deps = [gathers['w_down'][4]]

    def weight(name, after):
        send_sems, recv_sems, srcs, lands, _ = gathers[name]
        (own,), (got,) = _push_wait(send_sems, recv_sems, srcs, lands, after, _route_gather_wait, OTHER_CHIPS,
                                    name="gather_wait_" + name)
        whole = lax.dynamic_update_slice(got, own[None], (chip, 0, 0))
        return whole if name == 'w_up' else whole.reshape(-1, D_MODEL)

    reductions, exchanging = {}, {}

    def grad_ready(name, value):
        if name == 'w_in':
            main, dtp = value
            value = lax.dynamic_update_slice(main, dtp[:SSD_HEADS], (MAIN_WIDTH, 0))
        g4 = value if value.ndim == 3 else value.reshape(N_CHIPS, -1, value.shape[1])
        tokens = []
        for prev in list(exchanging):
            reductions[prev], token = _grad_scatter_start(exchanging.pop(prev), pos, g4)
            tokens.append(token)
        if name == 'w_in':
            reductions[name], token = _grad_reduce_begin(g4, pos, name, cols=True)
        else:
            exchanging[name], token = _grad_exchange_start(g4, name)
        return tokens + [token]

    small = {
        'norm_mix': norm_mix, 'sinks': sinks, 'attn_out_norm': attn_out_norm, 'ssd_conv_w': ssd_conv_w_full,
        'ssd_conv_b': ssd_conv_b, 'dt_bias': dt_bias, 'a_log': a_log, 'ssd_d': ssd_d, 'ssd_norm': ssd_norm,
        'norm_ffn': norm_ffn, 'ffn_conv_w': ffn_conv_w_full, 'ffn_conv_b': ffn_conv_b, 'norm_final': norm_final,
    }
    loss, dx, g = _local_step(x[0], loss_target[0], small, full_in_t, w_in_dt,
                              _StepHooks(tuple(first_deps), weight, grad_ready))
    gbig = {n: _grad_reduce_finish(reductions[n], pos, dx) for n in ('w_down', 'w_up', 'w_out', 'w_in')}

    small_names = [n for n in WEIGHTS if n not in BIG]
    small_g = [loss[:, :1]] + [g[n] for n in small_names]
    small_shapes = [(1, 1)] + [tuple(a.shape) for a in small_g[1:]]
    red = _unpack(_allreduce_small(_pack(small_g)), small_shapes)
    loss_out = red[0].reshape(())
    gsm = dict(zip(small_names, red[1:]))
    gsm['ssd_conv_w'] = lax.dynamic_slice(gsm['ssd_conv_w'], (0, chip * ssd_conv_w.shape[2]),
                                          (SSD_CONV, ssd_conv_w.shape[2]))
    gsm['ffn_conv_w'] = lax.dynamic_slice(gsm['ffn_conv_w'], (0, chip * ffn_conv_w.shape[2]),
                                          (FFN_CONV, ffn_conv_w.shape[2]))

    grads, deltas, new_m, new_v = {}, {}, {}, {}
    for n in BIG:
        mine, theirs = gbig[n]
        if n == 'w_in':
            outs = _adamw_halves(w_in_t, mine, theirs, m_in_t, v_in_t, pos, name="adamw_" + n, cols=True)
            outs = [jnp.transpose(o) for o in outs]
        else:
            outs = _adamw_halves(w[n][0], mine, theirs, m[n][0], v[n][0], pos, name="adamw_" + n)
        grads[n], deltas[n], new_m[n], new_v[n] = [o[None] for o in outs]
    shapes = [tuple(w[n].shape) for n in small_names]
    gp = _pack([gsm[n] for n in small_names])
    d, m2, v2 = _adamw(_pack([w[n] for n in small_names]), gp, _pack([m[n] for n in small_names]),
                       _pack([v[n] for n in small_names]), name="adamw_small")
    for n, gg, dd, mm, vv in zip(small_names, _unpack(gp, shapes), _unpack(d, shapes), _unpack(m2, shapes),
                                 _unpack(v2, shapes)):
        grads[n], deltas[n], new_m[n], new_v[n] = gg, dd, mm, vv

    return (loss_out, dx[None], *[grads[n] for n in WEIGHTS], *[deltas[n] for n in WEIGHTS],
            *[new_m[n] for n in WEIGHTS], *[new_v[n] for n in WEIGHTS])
```

```python
import functools

import jax
import jax.numpy as jnp
from jax import lax
from jax.experimental import pallas as pl
from jax.experimental.pallas import tpu as pltpu

F32 = jnp.float32
BF16 = jnp.bfloat16

D_MODEL = 2048
N_Q_HEADS = 32
N_KV_HEADS = 8
HEAD_DIM = 64
WINDOW = 128
ATTN_BLOCK = 128
ROT_DIM = 16
ROPE_THETA = 500000.0
SSD_HEADS = 32
SSD_HEAD_DIM = 64
SSD_INNER = 2048
SSD_GROUPS = 8
SSD_STATE = 128
SSD_CONV = 4
SSD_CHUNK = 128
ATTN_WIDTH = 2048
KV_WIDTH = 512
BC_WIDTH = 1024
CONV_CH = 4096
IN_PROJ_WIDTH = 9248
MAIN_WIDTH = 9216
D_FF = 5632
FFN_CONV = 3
EPS = 1e-6
O_Q, O_K, O_V, O_Z, O_XBC, O_DT = 0, 2048, 2560, 3072, 5120, 9216

ADAM_LR = 0.001
ADAM_B1 = 0.9
ADAM_B2 = 0.999
ADAM_EPS = 1e-08
ADAM_WD = 0.01
ADAM_STEP = 10

N_CHIPS = 4
NEG = -1e30
LANES = 128
VMEM_LIMIT = 48 * 1024 * 1024
MESH = pl.DeviceIdType.MESH
HBM_SPEC = pl.BlockSpec(memory_space=pltpu.HBM)
TOKEN = jax.ShapeDtypeStruct((8, LANES), F32)

WEIGHTS = ['norm_mix', 'w_in', 'sinks', 'attn_out_norm', 'ssd_conv_w', 'ssd_conv_b', 'dt_bias', 'a_log', 'ssd_d',
           'ssd_norm', 'w_out', 'norm_ffn', 'w_up', 'ffn_conv_w', 'ffn_conv_b', 'w_down', 'norm_final']
BIG = ['w_in', 'w_out', 'w_up', 'w_down']


def _cp(sem=None, vmem=VMEM_LIMIT):
    kw = {'vmem_limit_bytes': vmem}
    if sem is not None:
        kw['dimension_semantics'] = sem
    return pltpu.CompilerParams(**kw)


def _tile(n, pref):
    if n <= pref:
        return n
    t = (pref // LANES) * LANES
    while t > LANES and n % t:
        t -= LANES
    assert n % t == 0, (n, pref)
    return t


def _rows(n, pref):
    t = min(n, pref)
    while n % t:
        t -= 8
    if 4 * t < pref:
        t = pref
        while n % t:
            t += 8
    return t


def _iota(shape, dim):
    return lax.broadcasted_iota(jnp.int32, shape, dim)


def _dot(a, b, mode='nn'):
    dn = {'nn': (((1,), (0,)), ((), ())), 'nt': (((1,), (1,)), ((), ())), 'tn': (((0,), (0,)), ((), ()))}[mode]
    return lax.dot_general(a.astype(BF16), b.astype(BF16), dn, preferred_element_type=F32)


def _dot_exact(a, b):
    return lax.dot_general(a, b, (((1,), (0,)), ((), ())), precision=lax.Precision.HIGHEST,
                           preferred_element_type=F32)


def _sigmoid(x):
    return 1.0 / (1.0 + jnp.exp(-x))


def _softplus(x):
    return jnp.maximum(x, 0.0) + jnp.log(1.0 + jnp.exp(-jnp.abs(x)))


def _matmul(a, b, *, mode, name, out_dtype=F32, add=None, deps=(), tm=1024, tn=1024, tk=2048,
            a_halves=False, b_halves=False, b_owner=False, owner_major=False, n_limit=None, k_limit=None,
            m_rows=None):
    ash, bsh = (a.shape[1:] if a_halves else a.shape), (b.shape[1:] if (b_halves or b_owner) else b.shape)
    if mode == 'nn':
        (m, k), (k2, n) = ash, bsh
    elif mode == 'nt':
        (m, k), (n, k2) = ash, bsh
    else:
        (k, m), (k2, n) = ash, bsh
    if n_limit is not None:
        assert mode == 'nt' and n_limit <= n
        n = n_limit
    if k_limit is not None:
        assert mode == 'nn' and k_limit <= k2
        k2 = k_limit
    if a_halves:
        assert mode == 'nt'
        k = 2 * k
    if b_halves:
        assert mode == 'tn'
        n = 2 * n
    if b_owner:
        assert mode in ('nn', 'nt')
        if mode == 'nn':
            n = 4 * n
        else:
            k2 = 4 * k2
    assert k == k2, (a.shape, b.shape, mode)
    tm = _tile(m, tm)
    tn = _tile(n // 4 if (owner_major or (b_owner and mode == 'nn')) else (n // 2 if b_halves else n), tn)
    tk = _tile(k // 4 if (b_owner and mode == 'nt') else (k // 2 if a_halves else k), tk)
    nk = k // tk
    has_add = add is not None
    assert not (has_add and owner_major)

    def body(*refs):
        a_ref, b_ref = refs[:2]
        add_ref = refs[2] if has_add else None

        def finish(r, o_ref):
            if has_add:
                r = r + add_ref[...].astype(F32)
            o_ref[...] = r.astype(out_dtype)

        if nk == 1:
            finish(_dot(a_ref[...], b_ref[...], mode), refs[-1])
            return
        o_ref, acc = refs[-2:]
        kk = pl.program_id(2)

        @pl.when(kk == 0)
        def _():
            acc[...] = _dot(a_ref[...], b_ref[...], mode)

        @pl.when((kk > 0) & (kk < nk - 1))
        def _():
            acc[...] += _dot(a_ref[...], b_ref[...], mode)

        @pl.when(kk == nk - 1)
        def _():
            finish(acc[...] + _dot(a_ref[...], b_ref[...], mode), o_ref)

    if mode == 'tn':
        a_spec = pl.BlockSpec((tk, tm), lambda i, j, kk: (kk, i))
    elif a_halves:
        nkh = nk // 2
        a_spec = pl.BlockSpec((None, tm, tk), lambda i, j, kk: (kk // nkh, i, kk % nkh))
    else:
        a_spec = pl.BlockSpec((tm, tk), lambda i, j, kk: (i, kk))
    if mode == 'nt' and b_owner:
        nkq = nk // 4
        b_spec = pl.BlockSpec((None, tn, tk), lambda i, j, kk: (kk // nkq, j, kk % nkq))
    elif mode == 'nt':
        b_spec = pl.BlockSpec((tn, tk), lambda i, j, kk: (j, kk))
    elif b_owner:
        njq = (n // 4) // tn
        b_spec = pl.BlockSpec((None, tk, tn), lambda i, j, kk: (j // njq, kk, j % njq))
    elif b_halves:
        njh = (n // 2) // tn
        b_spec = pl.BlockSpec((None, tk, tn), lambda i, j, kk: (j // njh, kk, j % njh))
    else:
        b_spec = pl.BlockSpec((tk, tn), lambda i, j, kk: (kk, j))
    if owner_major:
        njo = (n // 4) // tn
        o_spec = pl.BlockSpec((None, tm, tn), lambda i, j, kk: (j // njo, i, j % njo))
        out_shape = jax.ShapeDtypeStruct((N_CHIPS, m, n // 4), out_dtype)
    else:
        o_spec = pl.BlockSpec((tm, tn), lambda i, j, kk: (i, j))
        out_shape = jax.ShapeDtypeStruct((m if m_rows is None else m_rows, n), out_dtype)
    dep_spec = pl.BlockSpec((8, LANES), lambda i, j, kk: (0, 0))
    in_specs = [a_spec, b_spec] + ([pl.BlockSpec((tm, tn), lambda i, j, kk: (i, j))] if has_add else [])
    in_specs += [dep_spec] * len(deps)
    args = (a, b) + ((add,) if has_add else ()) + tuple(deps)
    return pl.pallas_call(
        body, name=name, grid=(m // tm, n // tn, nk), in_specs=in_specs, out_specs=o_spec, out_shape=out_shape,
        scratch_shapes=[pltpu.VMEM((tm, tn), F32)] if nk > 1 else [],
        compiler_params=_cp(("parallel", "parallel", "arbitrary")))(*args)


def _rmsnorm_fwd(x, g, name, deps=()):
    t, d = x.shape
    tb = _rows(t, 256)

    def body(x_ref, g_ref, *rest):
        o_ref = rest[-1]
        xv = x_ref[...]
        r = lax.rsqrt(jnp.mean(xv * xv, axis=-1, keepdims=True) + EPS)
        o_ref[...] = (xv * r * g_ref[...]).astype(BF16)

    dep_spec = pl.BlockSpec((8, LANES), lambda i: (0, 0))
    return pl.pallas_call(
        body, name=name, grid=(t // tb,),
        in_specs=[pl.BlockSpec((tb, d), lambda i: (i, 0)), pl.BlockSpec((1, d), lambda i: (0, 0))]
        + [dep_spec] * len(deps),
        out_specs=pl.BlockSpec((tb, d), lambda i: (i, 0)), out_shape=jax.ShapeDtypeStruct((t, d), BF16),
        compiler_params=_cp(("parallel",)))(x, g, *deps)


def _rmsnorm_bwd(x, g, dy, res, name):
    t, d = x.shape
    tb = _rows(t, 256)

    def body(x_ref, g_ref, dy_ref, res_ref, dx_ref, dg_ref):
        i = pl.program_id(0)
        xv = x_ref[...]
        dyv = dy_ref[...].astype(F32)
        r = lax.rsqrt(jnp.mean(xv * xv, axis=-1, keepdims=True) + EPS)
        u = dyv * g_ref[...]
        dx = r * u - xv * (r * r * r * jnp.mean(u * xv, axis=-1, keepdims=True))
        dx_ref[...] = dx + res_ref[...]
        part = jnp.sum(dyv * xv * r, axis=0, keepdims=True)

        @pl.when(i == 0)
        def _():
            dg_ref[...] = part

        @pl.when(i > 0)
        def _():
            dg_ref[...] += part

    row = pl.BlockSpec((tb, d), lambda i: (i, 0))
    vec = pl.BlockSpec((1, d), lambda i: (0, 0))
    return pl.pallas_call(
        body, name=name, grid=(t // tb,), in_specs=[row, vec, row, row], out_specs=[row, vec],
        out_shape=[jax.ShapeDtypeStruct((t, d), F32), jax.ShapeDtypeStruct((1, d), F32)],
        compiler_params=_cp(("arbitrary",)))(x, g, dy, res)


def _final_loss(h, g, tgt):
    t, d = h.shape
    tb = _rows(t, 256)

    def body(h_ref, g_ref, t_ref, loss_ref, dh_ref, dg_ref):
        i = pl.program_id(0)
        hv = h_ref[...]
        gv = g_ref[...]
        r = lax.rsqrt(jnp.mean(hv * hv, axis=-1, keepdims=True) + EPS)
        y = hv * r * gv
        diff = y - t_ref[...]
        lpart = jnp.sum(jnp.sum(diff * diff, axis=1, keepdims=True), axis=0, keepdims=True) * (0.5 / d)
        dy = diff * (1.0 / d)
        u = dy * gv
        dh_ref[...] = r * u - hv * (r * r * r * jnp.mean(u * hv, axis=-1, keepdims=True))
        gpart = jnp.sum(dy * hv * r, axis=0, keepdims=True)
        lrow = jnp.broadcast_to(lpart, (1, LANES))

        @pl.when(i == 0)
        def _():
            loss_ref[...] = lrow
            dg_ref[...] = gpart

        @pl.when(i > 0)
        def _():
            loss_ref[...] += lrow
            dg_ref[...] += gpart

    row = pl.BlockSpec((tb, d), lambda i: (i, 0))
    vec = pl.BlockSpec((1, d), lambda i: (0, 0))
    return pl.pallas_call(
        body, name="final_loss", grid=(t // tb,), in_specs=[row, vec, row],
        out_specs=[pl.BlockSpec((1, LANES), lambda i: (0, 0)), row, vec],
        out_shape=[jax.ShapeDtypeStruct((1, LANES), F32), jax.ShapeDtypeStruct((t, d), F32),
                   jax.ShapeDtypeStruct((1, d), F32)],
        compiler_params=_cp(("arbitrary",)))(h, g, tgt)


def _rope_tables(t):
    pos = jnp.arange(t, dtype=F32)
    inv = 1.0 / (ROPE_THETA ** (jnp.arange(0, ROT_DIM, 2, dtype=F32) / ROT_DIM))
    ang = pos[:, None] * inv[None, :]
    cos, sin = jnp.cos(ang), jnp.sin(ang)
    half = ROT_DIM // 2
    rest = HEAD_DIM - ROT_DIM
    c = jnp.concatenate([cos, cos, jnp.ones((t, rest), F32)], axis=1)
    s1 = jnp.concatenate([-sin, jnp.zeros((t, half + rest), F32)], axis=1)
    s2 = jnp.concatenate([jnp.zeros((t, half), F32), sin, jnp.zeros((t, rest), F32)], axis=1)
    return tuple(jnp.tile(v, (1, LANES // HEAD_DIM)) for v in (c, s1, s2))


def _rope(x, c, s1, s2):
    half = ROT_DIM // 2
    return x * c + pltpu.roll(x, LANES - half, 1) * s1 + pltpu.roll(x, half, 1) * s2


def _rope_t(g, c, s1, s2):
    half = ROT_DIM // 2
    return g * c + pltpu.roll(g * s1, half, 1) + pltpu.roll(g * s2, LANES - half, 1)


def _band_masks(i, heads):
    n = heads * ATTN_BLOCK
    q = jnp.bitwise_and(_iota((n, ATTN_BLOCK), 0), ATTN_BLOCK - 1)
    j = _iota((n, ATTN_BLOCK), 1)
    upper = j > q
    return upper, upper & (j < jnp.where(i > 0, 0, ATTN_BLOCK))


def _fold_band(full, upper):
    return jnp.where(upper, full[:, :ATTN_BLOCK], full[:, ATTN_BLOCK:])


def _unfold_band(band, upper):
    return jnp.concatenate([jnp.where(upper, band, 0.0), jnp.where(upper, 0.0, band)], axis=1)


def _half_masks():
    lane = _iota((1, LANES), 1)
    return [(lane < HEAD_DIM).astype(F32), (lane >= HEAD_DIM).astype(F32)]


def _stack_heads(blocks, hm, j):
    pieces = []
    for r in range(4):
        qb, half = (4 * j + r) // 2, (4 * j + r) % 2
        piece = blocks[qb] * hm[half]
        if half != j:
            piece = pltpu.roll(piece, HEAD_DIM, 1)
        pieces.append(piece)
    return jnp.concatenate(pieces, axis=0)


def _unstack_heads(stacked, j):
    out = []
    for qb in (2 * j, 2 * j + 1):
        acc = None
        for half in range(2):
            r = 2 * qb + half - 4 * j
            piece = stacked[r * ATTN_BLOCK:(r + 1) * ATTN_BLOCK]
            if half != j:
                piece = pltpu.roll(piece, HEAD_DIM, 1)
            acc = piece if acc is None else acc + piece
        out.append((qb, acc))
    return out


def _sink_column(sink_ref, base):
    return jnp.concatenate([jnp.full((ATTN_BLOCK, 1), sink_ref[base + r], F32) for r in range(4)], axis=0)


def _attn_specs(nb_clamp):
    blk = ATTN_BLOCK
    kb, vb = O_K // LANES, O_V // LANES

    def cur(i):
        return jnp.minimum(i, nb_clamp)

    def prev(i):
        return jnp.maximum(jnp.minimum(i, nb_clamp + 1) - 1, 0)

    q = pl.BlockSpec((blk, 512), lambda p, i: (cur(i), p))
    kc = pl.BlockSpec((blk, LANES), lambda p, i: (cur(i), kb + p))
    kp = pl.BlockSpec((blk, LANES), lambda p, i: (prev(i), kb + p))
    vc = pl.BlockSpec((blk, LANES), lambda p, i: (cur(i), vb + p))
    vp = pl.BlockSpec((blk, LANES), lambda p, i: (prev(i), vb + p))
    tc = pl.BlockSpec((blk, LANES), lambda p, i: (cur(i), 0))
    tp = pl.BlockSpec((blk, LANES), lambda p, i: (prev(i), 0))
    return q, kc, kp, vc, vp, tc, tp


def _attn_fwd(proj, sinks, tables):
    t = proj.shape[0]
    nb = t // ATTN_BLOCK
    scale = HEAD_DIM ** -0.5

    def body(sink_ref, q_ref, kc_ref, kp_ref, vc_ref, vp_ref, cc_ref, s1c_ref, s2c_ref, cp_ref, s1p_ref, s2p_ref,
             o_ref):
        p = pl.program_id(0)
        i = pl.program_id(1)
        cc, s1c, s2c = cc_ref[...], s1c_ref[...], s2c_ref[...]
        kband = jnp.concatenate([_rope(kp_ref[...], cp_ref[...], s1p_ref[...], s2p_ref[...]),
                                 _rope(kc_ref[...], cc, s1c, s2c)], axis=0).astype(BF16)
        vband = jnp.concatenate([vp_ref[...], vc_ref[...]], axis=0)
        hm = _half_masks()
        vsel = [(vband * hm[j]).astype(BF16) for j in range(2)]
        upper, dropped = _band_masks(i, 1)
        for qb in range(4):
            qr = _rope(q_ref[:, qb * LANES:(qb + 1) * LANES], cc, s1c, s2c)
            acc = jnp.zeros((ATTN_BLOCK, LANES), F32)
            for half in range(2):
                hh = qb * 2 + half
                j = hh // 4
                qs = qr * hm[half]
                if half != j:
                    qs = pltpu.roll(qs, HEAD_DIM, 1)
                s = jnp.where(dropped, NEG, _fold_band(_dot(qs, kband, 'nt'), upper) * scale)
                sink = sink_ref[p * 8 + hh]
                m = jnp.maximum(jnp.max(s, axis=1, keepdims=True), sink)
                pe = jnp.exp(s - m)
                den = jnp.sum(pe, axis=1, keepdims=True) + jnp.exp(sink - m)
                o = _dot(_unfold_band(pe / den, upper), vsel[j])
                if half != j:
                    o = pltpu.roll(o, HEAD_DIM, 1)
                acc = acc + o
            o_ref[:, qb * LANES:(qb + 1) * LANES] = acc

    q, kc, kp, vc, vp, tc, tp = _attn_specs(nb - 1)
    smem = pl.BlockSpec(memory_space=pltpu.SMEM)
    return pl.pallas_call(
        body, name="attn_fwd", grid=(4, nb),
        in_specs=[smem, q, kc, kp, vc, vp, tc, tc, tc, tp, tp, tp],
        out_specs=pl.BlockSpec((ATTN_BLOCK, 512), lambda p, i: (i, p)),
        out_shape=jax.ShapeDtypeStruct((t, ATTN_WIDTH), F32),
        compiler_params=_cp(("parallel", "arbitrary")))(sinks, proj, proj, proj, proj, proj, *tables, *tables)


def _attn_bwd(proj, sinks, tables, dout):
    t = proj.shape[0]
    nb = t // ATTN_BLOCK
    scale = HEAD_DIM ** -0.5

    def body(sink_ref, q_ref, kc_ref, kp_ref, vc_ref, vp_ref, cc_ref, s1c_ref, s2c_ref, cp_ref, s1p_ref, s2p_ref,
             do_ref, dq_ref, dk_ref, dv_ref, ds_ref, carry_k, carry_v):
        p = pl.program_id(0)
        i = pl.program_id(1)
        ptab = (cp_ref[...], s1p_ref[...], s2p_ref[...])

        @pl.when(i == 0)
        def _():
            carry_k[...] = jnp.zeros_like(carry_k)
            carry_v[...] = jnp.zeros_like(carry_v)
            ds_ref[...] = jnp.zeros_like(ds_ref)

        @pl.when(i < nb)
        def _():
            cc, s1c, s2c = cc_ref[...], s1c_ref[...], s2c_ref[...]
            kband = jnp.concatenate([_rope(kp_ref[...], *ptab), _rope(kc_ref[...], cc, s1c, s2c)], axis=0)
            vband = jnp.concatenate([vp_ref[...], vc_ref[...]], axis=0)
            hm = _half_masks()
            kband16 = kband.astype(BF16)
            vband16 = vband.astype(BF16)
            upper, dropped = _band_masks(i, 4)
            dkb = jnp.zeros((2 * ATTN_BLOCK, LANES), F32)
            dvb = jnp.zeros((2 * ATTN_BLOCK, LANES), F32)
            row8 = _iota((8, LANES), 0)
            dsink = jnp.zeros((8, LANES), F32)
            qr = [_rope(q_ref[:, qb * LANES:(qb + 1) * LANES], cc, s1c, s2c) for qb in range(4)]
            dob = [do_ref[:, qb * LANES:(qb + 1) * LANES] for qb in range(4)]
            for j in range(2):
                qst = _stack_heads(qr, hm, j).astype(BF16)
                dost = _stack_heads(dob, hm, j).astype(BF16)
                s = jnp.where(dropped, NEG, _fold_band(_dot(qst, kband16, 'nt'), upper) * scale)
                sink = _sink_column(sink_ref, p * 8 + 4 * j)
                m = jnp.maximum(jnp.max(s, axis=1, keepdims=True), sink)
                pe = jnp.exp(s - m)
                psink = jnp.exp(sink - m)
                den = jnp.sum(pe, axis=1, keepdims=True) + psink
                pr = pe / den
                dvb = dvb + _dot(_unfold_band(pr, upper).T, dost)
                dp = _fold_band(_dot(dost, vband16, 'nt'), upper)
                delta = jnp.sum(pr * dp, axis=1, keepdims=True)
                dsc = _unfold_band(pr * (dp - delta) * scale, upper)
                dsk = psink / den * delta
                for r in range(4):
                    part = jnp.sum(dsk[r * ATTN_BLOCK:(r + 1) * ATTN_BLOCK])
                    dsink = dsink + jnp.where(row8 == 4 * j + r, -part, 0.0)
                for qb, dqb in _unstack_heads(_dot(dsc, kband * hm[j]), j):
                    dq_ref[:, qb * LANES:(qb + 1) * LANES] = _rope_t(dqb, cc, s1c, s2c).astype(BF16)
                dkb = dkb + _dot(dsc.T, qst)
            ds_ref[0] += dsink
            dk_ref[...] = _rope_t(carry_k[...] + dkb[:ATTN_BLOCK], *ptab).astype(BF16)
            dv_ref[...] = (carry_v[...] + dvb[:ATTN_BLOCK]).astype(BF16)
            carry_k[...] = dkb[ATTN_BLOCK:]
            carry_v[...] = dvb[ATTN_BLOCK:]

        @pl.when(i == nb)
        def _():
            dk_ref[...] = _rope_t(carry_k[...], *ptab).astype(BF16)
            dv_ref[...] = carry_v[...].astype(BF16)

    q, kc, kp, vc, vp, tc, tp = _attn_specs(nb - 1)
    smem = pl.BlockSpec(memory_space=pltpu.SMEM)
    qblk = pl.BlockSpec((ATTN_BLOCK, 512), lambda p, i: (jnp.minimum(i, nb - 1), p))
    kvout = pl.BlockSpec((ATTN_BLOCK, LANES), lambda p, i: (jnp.maximum(i - 1, 0), p))
    return pl.pallas_call(
        body, name="attn_bwd", grid=(4, nb + 1),
        in_specs=[smem, q, kc, kp, vc, vp, tc, tc, tc, tp, tp, tp, qblk],
        out_specs=[qblk, kvout, kvout, pl.BlockSpec((1, 8, LANES), lambda p, i: (p, 0, 0))],
        out_shape=[jax.ShapeDtypeStruct((t, ATTN_WIDTH), BF16), jax.ShapeDtypeStruct((t, KV_WIDTH), BF16),
                   jax.ShapeDtypeStruct((t, KV_WIDTH), BF16), jax.ShapeDtypeStruct((4, 8, LANES), F32)],
        scratch_shapes=[pltpu.VMEM((ATTN_BLOCK, LANES), F32), pltpu.VMEM((ATTN_BLOCK, LANES), F32)],
        compiler_params=_cp(("parallel", "arbitrary")))(sinks, proj, proj, proj, proj, proj, *tables, *tables, dout)


def _shift_rows(x, prev8, j):
    r = pltpu.roll(x, j, 0)
    head = jnp.where(_iota((8, 1), 0) < j, pltpu.roll(prev8, j, 0), r[:8])
    if x.shape[0] == 8:
        return head
    return jnp.concatenate([head, r[8:]], axis=0)


def _shift_rows_up(x, next8, j):
    n = x.shape[0]
    r = pltpu.roll(x, n - j, 0)
    tail = jnp.where(_iota((8, 1), 0) >= 8 - j, pltpu.roll(next8, 8 - j, 0), r[n - 8:])
    return jnp.concatenate([r[:n - 8], tail], axis=0)


def _conv_apply(x, prev8, w, b, taps):
    u = b + x * w[taps - 1:taps]
    for j in range(1, taps):
        u = u + _shift_rows(x, prev8, j) * w[taps - 1 - j:taps - j]
    return u


def _conv_grads(du, du_next8, x, w, taps):
    dx = du * w[taps - 1:taps]
    rowk = _iota((taps, 1), 0)
    dw = jnp.where(rowk == taps - 1, jnp.sum(du * x, axis=0, keepdims=True), 0.0)
    for j in range(1, taps):
        ahead = _shift_rows_up(du, du_next8, j)
        dx = dx + ahead * w[taps - 1 - j:taps - j]
        dw = dw + jnp.where(rowk == taps - 1 - j, jnp.sum(ahead * x, axis=0, keepdims=True), 0.0)
    return dx, dw, jnp.sum(du, axis=0, keepdims=True)


def _conv_specs(tb, tc, col0, t):
    c0 = col0 // tc
    cur = pl.BlockSpec((tb, tc), lambda j, i: (i, c0 + j))
    prev = pl.BlockSpec((8, tc), lambda j, i: (jnp.maximum(i * (tb // 8) - 1, 0), c0 + j))
    nxt = pl.BlockSpec((8, tc), lambda j, i: (jnp.minimum((i + 1) * (tb // 8), t // 8 - 1), c0 + j))
    return cur, prev, nxt


def _conv_fwd(x, w, b, *, col0, width, act, name):
    t = x.shape[0]
    taps = w.shape[0]
    tb, tc = _rows(t, 512), _tile(width, 1024)
    assert col0 % tc == 0

    def body(x_ref, xp_ref, w_ref, b_ref, o_ref):
        i = pl.program_id(1)
        prev8 = jnp.where(i > 0, xp_ref[...], 0.0)
        u = _conv_apply(x_ref[...], prev8, w_ref[...], b_ref[...], taps)
        if act:
            u = u * _sigmoid(u)
        o_ref[...] = u

    cur, prev, _ = _conv_specs(tb, tc, col0, t)
    par = pl.BlockSpec((taps, tc), lambda j, i: (0, j))
    bias = pl.BlockSpec((1, tc), lambda j, i: (0, j))
    return pl.pallas_call(
        body, name=name, grid=(width // tc, t // tb), in_specs=[cur, prev, par, bias],
        out_specs=pl.BlockSpec((tb, tc), lambda j, i: (i, j)), out_shape=jax.ShapeDtypeStruct((t, width), F32),
        compiler_params=_cp(("parallel", "parallel")))(x, x, w, b)


def _conv_silu_dact(x, w, b, dout, *, col0, width, name):
    t = x.shape[0]
    taps = w.shape[0]
    tb, tc = _rows(t, 512), _tile(width, 1024)

    def body(x_ref, xp_ref, w_ref, b_ref, d_ref, o_ref):
        i = pl.program_id(1)
        prev8 = jnp.where(i > 0, xp_ref[...], 0.0)
        u = _conv_apply(x_ref[...], prev8, w_ref[...], b_ref[...], taps)
        sg = _sigmoid(u)
        o_ref[...] = d_ref[...] * (sg * (1.0 + u * (1.0 - sg)))

    cur, prev, _ = _conv_specs(tb, tc, col0, t)
    par = pl.BlockSpec((taps, tc), lambda j, i: (0, j))
    bias = pl.BlockSpec((1, tc), lambda j, i: (0, j))
    out = pl.BlockSpec((tb, tc), lambda j, i: (i, j))
    return pl.pallas_call(
        body, name=name, grid=(width // tc, t // tb), in_specs=[cur, prev, par, bias, out],
        out_specs=out, out_shape=jax.ShapeDtypeStruct((t, width), F32),
        compiler_params=_cp(("parallel", "parallel")))(x, x, w, b, dout)


def _conv_bwd(x, w, du, *, col0, width, name):
    t = x.shape[0]
    taps = w.shape[0]
    tb, tc = _rows(t, 512), _tile(width, 1024)
    nrow = t // tb

    def body(x_ref, w_ref, du_ref, dun_ref, dx_ref, dw_ref, db_ref):
        i = pl.program_id(1)
        next8 = jnp.where(i < nrow - 1, dun_ref[...], 0.0)
        dx, dwv, dbv = _conv_grads(du_ref[...], next8, x_ref[...], w_ref[...], taps)
        dx_ref[...] = dx.astype(BF16)

        @pl.when(i == 0)
        def _():
            dw_ref[...] = dwv
            db_ref[...] = dbv

        @pl.when(i > 0)
        def _():
            dw_ref[...] += dwv
            db_ref[...] += dbv

    cur, _, _ = _conv_specs(tb, tc, col0, t)
    dcur, _, dnxt = _conv_specs(tb, tc, 0, t)
    par = pl.BlockSpec((taps, tc), lambda j, i: (0, j))
    bias = pl.BlockSpec((1, tc), lambda j, i: (0, j))
    return pl.pallas_call(
        body, name=name, grid=(width // tc, nrow), in_specs=[cur, par, dcur, dnxt],
        out_specs=[dcur, par, bias],
        out_shape=[jax.ShapeDtypeStruct((t, width), BF16), jax.ShapeDtypeStruct((taps, width), F32),
                   jax.ShapeDtypeStruct((1, width), F32)],
        compiler_params=_cp(("parallel", "arbitrary")))(x, w, du, du)


def _ffn_specs(tb, tc, t):
    nc = D_FF // tc

    def cur(half):
        return pl.BlockSpec((tb, tc), lambda j, i: (i, half * nc + j))

    def prev(half):
        return pl.BlockSpec((8, tc), lambda j, i: (jnp.maximum(i * (tb // 8) - 1, 0), half * nc + j))

    def nxt(half):
        return pl.BlockSpec((8, tc), lambda j, i: (jnp.minimum((i + 1) * (tb // 8), t // 8 - 1), half * nc + j))

    def par(rows, half):
        return pl.BlockSpec((rows, tc), lambda j, i: (0, half * nc + j))

    return cur, prev, nxt, par


def _ffn_act_fwd(u0, w, b):
    t = u0.shape[0]
    tb, tc = _rows(t, 512), _tile(D_FF, 1408)
    cur, prev, _, par = _ffn_specs(tb, tc, t)

    def body(g_ref, gp_ref, v_ref, vp_ref, wg_ref, wv_ref, bg_ref, bv_ref, o_ref):
        i = pl.program_id(1)
        ug = _conv_apply(g_ref[...], jnp.where(i > 0, gp_ref[...], 0.0), wg_ref[...], bg_ref[...], FFN_CONV)
        uv = _conv_apply(v_ref[...], jnp.where(i > 0, vp_ref[...], 0.0), wv_ref[...], bv_ref[...], FFN_CONV)
        o_ref[...] = (ug * _sigmoid(ug) * uv).astype(BF16)

    return pl.pallas_call(
        body, name="ffn_act_fwd", grid=(D_FF // tc, t // tb),
        in_specs=[cur(0), prev(0), cur(1), prev(1), par(FFN_CONV, 0), par(FFN_CONV, 1), par(1, 0), par(1, 1)],
        out_specs=pl.BlockSpec((tb, tc), lambda j, i: (i, j)), out_shape=jax.ShapeDtypeStruct((t, D_FF), BF16),
        compiler_params=_cp(("parallel", "parallel")))(u0, u0, u0, u0, w, w, b, b)


def _ffn_act_bwd(u0, w, b, da):
    t = u0.shape[0]
    tb, tc = _rows(t, 256), _tile(D_FF, 1408)
    nrow = t // tb
    taps = FFN_CONV
    cur, prev, nxt, par = _ffn_specs(tb, tc, t)

    def dact(ug, uv, dav):
        sg = _sigmoid(ug)
        return dav * uv * (sg * (1.0 + ug * (1.0 - sg))), dav * ug * sg

    def body(g_ref, gp_ref, gn_ref, v_ref, vp_ref, vn_ref, wg_ref, wv_ref, bg_ref, bv_ref, da_ref, dan_ref,
             dx_ref, dw_ref, db_ref):
        i = pl.program_id(1)
        xg, xv = g_ref[...], v_ref[...]
        gp = jnp.where(i > 0, gp_ref[...], 0.0)
        vp = jnp.where(i > 0, vp_ref[...], 0.0)
        wg, wv, bg, bv = wg_ref[...], wv_ref[...], bg_ref[...], bv_ref[...]
        dug, duv = dact(_conv_apply(xg, gp, wg, bg, taps), _conv_apply(xv, vp, wv, bv, taps),
                        da_ref[...].astype(F32))
        dan = jnp.where(i < nrow - 1, dan_ref[...].astype(F32)[:8], 0.0)
        dugn, duvn = dact(_conv_apply(gn_ref[...], xg[tb - 8:], wg, bg, taps),
                          _conv_apply(vn_ref[...], xv[tb - 8:], wv, bv, taps), dan)
        dxg, dwg, dbg = _conv_grads(dug, dugn, xg, wg, taps)
        dxv, dwv, dbv = _conv_grads(duv, duvn, xv, wv, taps)
        dx_ref[0] = dxg.astype(BF16)
        dx_ref[1] = dxv.astype(BF16)

        @pl.when(i == 0)
        def _():
            dw_ref[0] = dwg
            dw_ref[1] = dwv
            db_ref[0] = dbg
            db_ref[1] = dbv

        @pl.when(i > 0)
        def _():
            dw_ref[0] += dwg
            dw_ref[1] += dwv
            db_ref[0] += dbg
            db_ref[1] += dbv

    da_cur = pl.BlockSpec((tb, tc), lambda j, i: (i, j))
    da_nxt = pl.BlockSpec((16, tc), lambda j, i: (jnp.minimum((i + 1) * (tb // 16), t // 16 - 1), j))
    return pl.pallas_call(
        body, name="ffn_act_bwd", grid=(D_FF // tc, nrow),
        in_specs=[cur(0), prev(0), nxt(0), cur(1), prev(1), nxt(1), par(taps, 0), par(taps, 1), par(1, 0),
                  par(1, 1), da_cur, da_nxt],
        out_specs=[pl.BlockSpec((2, tb, tc), lambda j, i: (0, i, j)),
                   pl.BlockSpec((2, taps, tc), lambda j, i: (0, 0, j)),
                   pl.BlockSpec((2, 1, tc), lambda j, i: (0, 0, j))],
        out_shape=[jax.ShapeDtypeStruct((2, t, D_FF), BF16), jax.ShapeDtypeStruct((2, taps, D_FF), F32),
                   jax.ShapeDtypeStruct((2, 1, D_FF), F32)],
        compiler_params=_cp(("parallel", "arbitrary")))(u0, u0, u0, u0, u0, u0, w, w, b, b, da, da)


def _head_masks():
    lane = _iota((1, 4 * SSD_HEAD_DIM), 1)
    return [((lane >= r * SSD_HEAD_DIM) & (lane < (r + 1) * SSD_HEAD_DIM)).astype(F32) for r in range(4)]


def _segsum(v):
    first = _iota((1, LANES), 1) < SSD_HEAD_DIM
    halves = []
    for k in range(2):
        vh = v[:, k * LANES:(k + 1) * LANES]
        both = jnp.sum(vh, axis=1, keepdims=True)
        one = jnp.sum(jnp.where(first, vh, 0.0), axis=1, keepdims=True)
        halves.append(jnp.where(first, one, both - one))
    return jnp.concatenate(halves, axis=1)


def _ssd_common(raw_e, prow, rawr4, bcol, acol):
    n = SSD_CHUNK
    dt_e = _softplus(raw_e + prow[0:1, :])
    a_e = -jnp.exp(prow[1:2, :])
    d_e = prow[2:3, :]
    tril = (_iota((n, n), 0) >= _iota((n, n), 1)).astype(F32)
    acs_e = _dot_exact(tril, dt_e * a_e)
    last_e = acs_e[n - 1:n, :]
    dtr4 = _softplus(rawr4 + bcol)
    triu = (_iota((n, n), 0) <= _iota((n, n), 1)).astype(F32)
    acs_r4 = _dot_exact(dtr4 * (-jnp.exp(acol)), triu)
    return dt_e, a_e, d_e, acs_e, last_e, acs_r4


def _decay_matrix(acs_e, acs_r4, r):
    n = SSD_CHUNK
    col = acs_e[:, r * SSD_HEAD_DIM:r * SSD_HEAD_DIM + 1]
    seg = col - acs_r4[r:r + 1, :]
    causal = _iota((n, n), 0) >= _iota((n, n), 1)
    return jnp.exp(jnp.where(causal, seg, NEG))


SSD_STEP_CHUNKS = 4
SSD_ROWS = SSD_STEP_CHUNKS * SSD_CHUNK


def _ssd_specs(t, rev):
    nb = t // SSD_ROWS
    xb, bb, cb = 0, SSD_INNER // SSD_STATE, (SSD_INNER + BC_WIDTH) // SSD_STATE

    def ch(c):
        return (nb - 1 - c) if rev else c

    x = pl.BlockSpec((SSD_ROWS, 256), lambda g, c: (ch(c), xb + g))
    bm = pl.BlockSpec((SSD_ROWS, SSD_STATE), lambda g, c: (ch(c), bb + g))
    cm = pl.BlockSpec((SSD_ROWS, SSD_STATE), lambda g, c: (ch(c), cb + g))
    dtc = pl.BlockSpec((1, SSD_ROWS, 256), lambda g, c: (g, ch(c), 0))
    dtr = pl.BlockSpec((1, 4, SSD_ROWS), lambda g, c: (g, 0, ch(c)))
    prow = pl.BlockSpec((1, 3, 256), lambda g, c: (g, 0, 0))
    pcol = pl.BlockSpec((1, 4, 1), lambda g, c: (g, 0, 0))
    st = pl.BlockSpec((1, SSD_STEP_CHUNKS, SSD_STATE, 256), lambda g, c: (g, ch(c), 0, 0))
    return x, bm, cm, dtc, dtr, prow, pcol, st, ch


def _ssd_params(dt_raw, dt_bias, a_log, ssd_d):
    t = dt_raw.shape[0]
    by_group = dt_raw.reshape(t, SSD_GROUPS, 4)
    dtc = jnp.repeat(by_group, SSD_HEAD_DIM, axis=2).transpose(1, 0, 2)
    dtr = by_group.transpose(1, 2, 0)
    prow = jnp.repeat(jnp.stack([dt_bias.reshape(SSD_GROUPS, 4), a_log.reshape(SSD_GROUPS, 4),
                                 ssd_d.reshape(SSD_GROUPS, 4)], axis=1), SSD_HEAD_DIM, axis=2)
    bcol = dt_bias.reshape(SSD_GROUPS, 4, 1)
    acol = a_log.reshape(SSD_GROUPS, 4, 1)
    return dtc, dtr, prow, bcol, acol


def _ssd_fwd(xbc, params):
    t = xbc.shape[0]
    nc = t // SSD_CHUNK
    dtc, dtr, prow, bcol, acol = params

    def body(x_ref, b_ref, c_ref, dtc_ref, dtr_ref, prow_ref, bcol_ref, acol_ref, y_ref, st_ref, s_scr):
        c = pl.program_id(1)

        @pl.when(c == 0)
        def _():
            s_scr[...] = jnp.zeros_like(s_scr)

        masks = _head_masks()
        s = s_scr[...]
        for k in range(SSD_STEP_CHUNKS):
            rows = slice(k * SSD_CHUNK, (k + 1) * SSD_CHUNK)
            dt_e, a_e, d_e, acs_e, last_e, acs_r4 = _ssd_common(
                dtc_ref[0, rows], prow_ref[0], dtr_ref[0][:, rows], bcol_ref[0], acol_ref[0])
            xv = x_ref[rows]
            bm, cm = b_ref[rows], c_ref[rows]
            st_ref[0, k] = s
            xdt = xv * dt_e
            cb = _dot(cm, bm, 'nt')
            y = _dot(cm, s) * jnp.exp(acs_e) + xv * d_e
            for r in range(4):
                mr = cb * _decay_matrix(acs_e, acs_r4, r)
                y = y + _dot(mr, xdt * masks[r])
            y_ref[rows] = y
            w = xdt * jnp.exp(last_e - acs_e)
            s = s * jnp.exp(last_e) + _dot(bm.T, w)
        s_scr[...] = s

    x, bm, cm, dtcs, dtrs, prs, pcs, st, _ = _ssd_specs(t, False)
    return pl.pallas_call(
        body, name="ssd_fwd", grid=(SSD_GROUPS, t // SSD_ROWS), in_specs=[x, bm, cm, dtcs, dtrs, prs, pcs, pcs],
        out_specs=[pl.BlockSpec((SSD_ROWS, 256), lambda g, c: (c, g)), st],
        out_shape=[jax.ShapeDtypeStruct((t, SSD_INNER), F32),
                   jax.ShapeDtypeStruct((SSD_GROUPS, nc, SSD_STATE, 256), F32)],
        scratch_shapes=[pltpu.VMEM((SSD_STATE, 256), F32)],
        compiler_params=_cp(("parallel", "arbitrary")))(xbc, xbc, xbc, dtc, dtr, prow, bcol, acol)


def _ssd_bwd(xbc, params, states, dy):
    t = xbc.shape[0]
    nc = t // SSD_CHUNK
    n = SSD_CHUNK
    dtc, dtr, prow, bcol, acol = params

    def body(x_ref, b_ref, c_ref, dtc_ref, dtr_ref, prow_ref, bcol_ref, acol_ref, st_ref, dy_ref,
             dx_ref, db_ref, dc_ref, ddt_ref, dp_ref, ds_scr):
        c = pl.program_id(1)

        @pl.when(c == 0)
        def _():
            ds_scr[...] = jnp.zeros_like(ds_scr)
            dp_ref[...] = jnp.zeros_like(dp_ref)

        masks = _head_masks()
        ds = ds_scr[...]
        for k in reversed(range(SSD_STEP_CHUNKS)):
            rows = slice(k * SSD_CHUNK, (k + 1) * SSD_CHUNK)
            raw_e = dtc_ref[0, rows]
            prw = prow_ref[0]
            dt_e, a_e, d_e, acs_e, last_e, acs_r4 = _ssd_common(raw_e, prw, dtr_ref[0][:, rows], bcol_ref[0], acol_ref[0])
            xv = x_ref[rows]
            bm, cm = b_ref[rows], c_ref[rows]
            s = st_ref[0, k]
            dyv = dy_ref[rows]
            e_e = jnp.exp(acs_e)
            dec_e = jnp.exp(last_e - acs_e)
            cd_e = jnp.exp(last_e)
            xdt = xv * dt_e
            w = xdt * dec_e
            b16, c16, s16, ds16 = bm.astype(BF16), cm.astype(BF16), s.astype(BF16), ds.astype(BF16)
            cb = _dot(c16, b16, 'nt')
            yoff_raw = _dot(c16, s16)
            dye = dyv * e_e
            dye16 = dye.astype(BF16)
            dcm = _dot(dye16, s16, 'nt')
            ds_prev = ds * cd_e + _dot(cm.T, dye16)
            dacs_e = _segsum(dyv * yoff_raw) * e_e
            dw = _dot(b16, ds16)
            dbm = _dot(w, ds16, 'nt')
            tdec = _segsum(dw * xdt) * dec_e
            dacs_e = dacs_e - tdec
            dlast_e = jnp.sum(tdec, axis=0, keepdims=True)
            dxdt = dw * dec_e
            dlast_e = dlast_e + _segsum(jnp.sum(ds * s, axis=0, keepdims=True)) * cd_e
            dcb = jnp.zeros((n, n), F32)
            for r in range(4):
                lm = _decay_matrix(acs_e, acs_r4, r)
                mr = cb * lm
                dyr16 = (dyv * masks[r]).astype(BF16)
                dm = _dot(dyr16, xdt * masks[r], 'nt')
                dcb = dcb + dm * lm
                dseg = dm * mr
                dcol = jnp.sum(dseg, axis=1, keepdims=True) - jnp.sum(dseg.T, axis=1, keepdims=True)
                dacs_e = dacs_e + dcol * masks[r]
                dxdt = dxdt + _dot(mr.T, dyr16)
            dcm = dcm + _dot(dcb, b16)
            dbm = dbm + _dot(dcb.T, c16)
            dacs_e = dacs_e + jnp.where(_iota((n, 1), 0) == n - 1, dlast_e, 0.0)
            triu = (_iota((n, n), 0) <= _iota((n, n), 1)).astype(F32)
            ddta_e = _dot_exact(triu, dacs_e)
            ddt_e = ddta_e * a_e + _segsum(dxdt * xv)
            dx_ref[rows] = dxdt * dt_e + dyv * d_e
            db_ref[rows] = dbm
            dc_ref[rows] = dcm
            draw_e = ddt_e * _sigmoid(raw_e + prw[0:1, :])
            draw_t = draw_e.T
            ddt_ref[0, :, rows] = jnp.concatenate([draw_t[r * SSD_HEAD_DIM:r * SSD_HEAD_DIM + 1] for r in range(4)], axis=0)
            dbias = jnp.sum(draw_e, axis=0, keepdims=True)
            dalog = jnp.sum(ddta_e * dt_e, axis=0, keepdims=True) * a_e
            dd = _segsum(jnp.sum(dyv * xv, axis=0, keepdims=True))
            row3 = _iota((3, 1), 0)
            dp_ref[0] += (jnp.where(row3 == 0, dbias, 0.0) + jnp.where(row3 == 1, dalog, 0.0)
                          + jnp.where(row3 == 2, dd, 0.0))
            ds = ds_prev
        ds_scr[...] = ds


    x, bm, cm, dtcs, dtrs, prs, pcs, st, ch = _ssd_specs(t, True)
    yblk = pl.BlockSpec((SSD_ROWS, 256), lambda g, c: (ch(c), g))
    nblk = pl.BlockSpec((SSD_ROWS, SSD_STATE), lambda g, c: (ch(c), g))
    return pl.pallas_call(
        body, name="ssd_bwd", grid=(SSD_GROUPS, t // SSD_ROWS),
        in_specs=[x, bm, cm, dtcs, dtrs, prs, pcs, pcs, st, yblk],
        out_specs=[yblk, nblk, nblk, dtrs, prs],
        out_shape=[jax.ShapeDtypeStruct((t, SSD_INNER), F32), jax.ShapeDtypeStruct((t, BC_WIDTH), F32),
                   jax.ShapeDtypeStruct((t, BC_WIDTH), F32), jax.ShapeDtypeStruct((SSD_GROUPS, 4, t), F32),
                   jax.ShapeDtypeStruct((SSD_GROUPS, 3, 256), F32)],
        scratch_shapes=[pltpu.VMEM((SSD_STATE, 256), F32)],
        compiler_params=_cp(("parallel", "arbitrary")))(xbc, xbc, xbc, dtc, dtr, prow, bcol, acol, states, dy)


GROUP_W = SSD_INNER // SSD_GROUPS


def _mix_specs(tb):
    row = pl.BlockSpec((tb, 2048), lambda i: (i, 0))
    zlo = pl.BlockSpec((tb, 1024), lambda i: (i, O_Z // 1024))
    zhi = pl.BlockSpec((tb, 1024), lambda i: (i, O_Z // 1024 + 1))
    vec = pl.BlockSpec((1, 2048), lambda i: (0, 0))
    return row, zlo, zhi, vec


def _mix_fwd(attn, y, proj, g_attn, g_ssd):
    t = attn.shape[0]
    tb = _rows(t, 256)

    def body(a_ref, y_ref, zlo_ref, zhi_ref, ga_ref, gs_ref, o_ref):
        av = a_ref[...]
        r = lax.rsqrt(jnp.mean(av * av, axis=-1, keepdims=True) + EPS)
        o_ref[:, :ATTN_WIDTH] = (av * r * ga_ref[...]).astype(BF16)
        for g in range(SSD_GROUPS):
            lo, hi = g * GROUP_W, (g + 1) * GROUP_W
            zref = zlo_ref if g < 4 else zhi_ref
            z = zref[:, lo % 1024:lo % 1024 + GROUP_W]
            yg = y_ref[:, lo:hi] * (z * _sigmoid(z))
            rg = lax.rsqrt(jnp.mean(yg * yg, axis=-1, keepdims=True) + EPS)
            o_ref[:, ATTN_WIDTH + lo:ATTN_WIDTH + hi] = (yg * rg * gs_ref[:, lo:hi]).astype(BF16)

    row, zlo, zhi, vec = _mix_specs(tb)
    return pl.pallas_call(
        body, name="mix_fwd", grid=(t // tb,), in_specs=[row, row, zlo, zhi, vec, vec],
        out_specs=pl.BlockSpec((tb, 4096), lambda i: (i, 0)), out_shape=jax.ShapeDtypeStruct((t, 4096), BF16),
        compiler_params=_cp(("parallel",)))(attn, y, proj, proj, g_attn, g_ssd)


def _mix_bwd(dmix, attn, y, proj, g_attn, g_ssd):
    t = attn.shape[0]
    tb = _rows(t, 256)

    def body(dm_ref, a_ref, y_ref, zlo_ref, zhi_ref, ga_ref, gs_ref, da_ref, dy_ref, dz_ref, dga_ref, dgs_ref):
        i = pl.program_id(0)
        av = a_ref[...]
        dn = dm_ref[:, :ATTN_WIDTH].astype(F32)
        r = lax.rsqrt(jnp.mean(av * av, axis=-1, keepdims=True) + EPS)
        u = dn * ga_ref[...]
        da_ref[...] = r * u - av * (r * r * r * jnp.mean(u * av, axis=-1, keepdims=True))
        dga = jnp.sum(dn * av * r, axis=0, keepdims=True)

        @pl.when(i == 0)
        def _():
            dga_ref[...] = dga

        @pl.when(i > 0)
        def _():
            dga_ref[...] += dga

        for g in range(SSD_GROUPS):
            lo, hi = g * GROUP_W, (g + 1) * GROUP_W
            zref = zlo_ref if g < 4 else zhi_ref
            z = zref[:, lo % 1024:lo % 1024 + GROUP_W]
            yv = y_ref[:, lo:hi]
            sg = _sigmoid(z)
            sz = z * sg
            yg = yv * sz
            rg = lax.rsqrt(jnp.mean(yg * yg, axis=-1, keepdims=True) + EPS)
            do = dm_ref[:, ATTN_WIDTH + lo:ATTN_WIDTH + hi].astype(F32)
            ug = do * gs_ref[:, lo:hi]
            dyg = rg * ug - yg * (rg * rg * rg * jnp.mean(ug * yg, axis=-1, keepdims=True))
            dy_ref[:, lo:hi] = dyg * sz
            dz_ref[:, lo:hi] = (dyg * yv * (sg * (1.0 + z * (1.0 - sg)))).astype(BF16)
            dgs = jnp.sum(do * yg * rg, axis=0, keepdims=True)

            @pl.when(i == 0)
            def _():
                dgs_ref[:, lo:hi] = dgs

            @pl.when(i > 0)
            def _():
                dgs_ref[:, lo:hi] += dgs

    row, zlo, zhi, vec = _mix_specs(tb)
    return pl.pallas_call(
        body, name="mix_bwd", grid=(t // tb,),
        in_specs=[pl.BlockSpec((tb, 4096), lambda i: (i, 0)), row, row, zlo, zhi, vec, vec],
        out_specs=[row, row, row, vec, vec],
        out_shape=[jax.ShapeDtypeStruct((t, 2048), F32), jax.ShapeDtypeStruct((t, 2048), F32),
                   jax.ShapeDtypeStruct((t, 2048), BF16), jax.ShapeDtypeStruct((1, 2048), F32),
                   jax.ShapeDtypeStruct((1, 2048), F32)],
        compiler_params=_cp(("arbitrary",)))(dmix, attn, y, proj, proj, g_attn, g_ssd)


def _adamw(w, g, m, v, name):
    r, c = w.shape
    tb = _rows(r, 256)
    c1 = 1.0 - ADAM_B1 ** ADAM_STEP
    c2 = 1.0 - ADAM_B2 ** ADAM_STEP

    def body(w_ref, g_ref, m_ref, v_ref, d_ref, m2_ref, v2_ref):
        gv = g_ref[...]
        m2 = ADAM_B1 * m_ref[...] + (1.0 - ADAM_B1) * gv
        v2 = ADAM_B2 * v_ref[...] + (1.0 - ADAM_B2) * (gv * gv)
        d_ref[...] = -ADAM_LR * ((m2 / c1) / (jnp.sqrt(v2 / c2) + ADAM_EPS) + ADAM_WD * w_ref[...])
        m2_ref[...] = m2
        v2_ref[...] = v2

    blk = pl.BlockSpec((tb, c), lambda i: (i, 0))
    shp = jax.ShapeDtypeStruct((r, c), F32)
    return pl.pallas_call(body, name=name, grid=(r // tb,), in_specs=[blk] * 4, out_specs=[blk] * 3,
                          out_shape=[shp] * 3, compiler_params=_cp(("parallel",)))(w, g, m, v)


def _adamw_halves(w, mine, theirs, m, v, pos, name, cols=False):
    r, c = w.shape
    h = r if cols else r // 2
    tb = _rows(h, 128)
    nh = h // tb
    c1 = 1.0 - ADAM_B1 ** ADAM_STEP
    c2 = 1.0 - ADAM_B2 ** ADAM_STEP

    def body(pos_ref, w_ref, a_ref, b_ref, m_ref, v_ref, g_ref, d_ref, m2_ref, v2_ref):
        which = pl.program_id(1) if cols else pl.program_id(0) // nh
        gv = jnp.where(which == pos_ref[0], a_ref[...], b_ref[...])
        m2 = ADAM_B1 * m_ref[...] + (1.0 - ADAM_B1) * gv
        v2 = ADAM_B2 * v_ref[...] + (1.0 - ADAM_B2) * (gv * gv)
        g_ref[...] = gv
        d_ref[...] = -ADAM_LR * ((m2 / c1) / (jnp.sqrt(v2 / c2) + ADAM_EPS) + ADAM_WD * w_ref[...])
        m2_ref[...] = m2
        v2_ref[...] = v2

    if cols:
        full = pl.BlockSpec((tb, c // 2), lambda i, j, pref: (i, j))
        half = pl.BlockSpec((tb, c // 2), lambda i, j, pref: (i, 0))
        grid = (nh, 2)
    else:
        full = pl.BlockSpec((tb, c), lambda i, pref: (i, 0))
        half = pl.BlockSpec((tb, c), lambda i, pref: (i % nh, 0))
        grid = (r // tb,)
    shp = jax.ShapeDtypeStruct((r, c), F32)
    grid_spec = pltpu.PrefetchScalarGridSpec(num_scalar_prefetch=1, grid=grid,
                                             in_specs=[full, half, half, full, full], out_specs=[full] * 4)
    return pl.pallas_call(body, name=name, grid_spec=grid_spec, out_shape=[shp] * 4,
                          compiler_params=_cp(("parallel",) * len(grid)))(pos, w, mine, theirs, m, v)


def _sum_own_half(g4, recv, pos, name, cols=False):
    _, r, c = g4.shape
    h, c = (r, c // 2) if cols else (r // 2, c)
    tb = _rows(h, 128)
    nh = h // tb
    own = (lambda j, i, pref: (j, i, pref[0])) if cols else (lambda j, i, pref: (j, pref[0] * nh + i, 0))

    def body(pos_ref, a_ref, b_ref, o_ref):
        o_ref[...] = (a_ref[...] + b_ref[...]).astype(BF16)

    grid_spec = pltpu.PrefetchScalarGridSpec(
        num_scalar_prefetch=1, grid=(N_CHIPS, nh),
        in_specs=[pl.BlockSpec((1, tb, c), own), pl.BlockSpec((1, tb, c), lambda j, i, pref: (j, i, 0))],
        out_specs=pl.BlockSpec((1, tb, c), lambda j, i, pref: (j, i, 0)))
    return pl.pallas_call(body, name=name, grid_spec=grid_spec,
                          out_shape=jax.ShapeDtypeStruct((N_CHIPS, h, c), BF16),
                          compiler_params=_cp(("parallel", "parallel")))(pos, g4, recv)


def _sum_chips(g4, recv, parts, pos, name, cols=False):
    _, r, c = g4.shape
    h, c = (r, c // 2) if cols else (r // 2, c)
    tb = _rows(h, 128)
    nh = h // tb
    own = (lambda i, pref: (pref[1], i, pref[0])) if cols else (lambda i, pref: (pref[1], pref[0] * nh + i, 0))

    def body(pos_ref, a_ref, b_ref, p_ref, o_ref):
        own = a_ref[0] + b_ref[0]
        o_ref[...] = ((own + p_ref[0].astype(F32)) + p_ref[1].astype(F32)) + p_ref[2].astype(F32)

    grid_spec = pltpu.PrefetchScalarGridSpec(
        num_scalar_prefetch=1, grid=(nh,),
        in_specs=[pl.BlockSpec((1, tb, c), own),
                  pl.BlockSpec((1, tb, c), lambda i, pref: (pref[1], i, 0)),
                  pl.BlockSpec((3, tb, c), lambda i, pref: (0, i, 0))],
        out_specs=pl.BlockSpec((tb, c), lambda i, pref: (i, 0)))
    return pl.pallas_call(body, name=name, grid_spec=grid_spec, out_shape=jax.ShapeDtypeStruct((h, c), F32),
                          compiler_params=_cp(("parallel",)))(pos, g4, recv, parts)


def _me():
    return lax.axis_index("x"), lax.axis_index("y"), lax.axis_index("c")


def _flip(v, bit):
    return (1 - v) if bit else v


CHIP_FLIPS = [(1, 0), (0, 1), (1, 1)]


def _allgather_weights(shards, after, cols=False):
    n = len(shards)

    def body(*refs):
        ins, outs, token = refs[:n], refs[n + 1:2 * n + 1], refs[2 * n + 1]
        send_sems, recv_sems = refs[2 * n + 2:]
        x, y, c = _me()
        chip = 2 * x + y
        sib = (x, y, 1 - c)

        def remote(src, dst, k, to):
            return pltpu.make_async_remote_copy(src_ref=src, dst_ref=dst, send_sem=send_sems.at[k],
                                                recv_sem=recv_sems.at[k], device_id=to, device_id_type=MESH)

        def half(ref, which):
            if cols:
                h = ref.shape[1] // 2
                return ref.at[:, pl.ds(which * h, h)]
            h = ref.shape[0] // 2
            return ref.at[pl.ds(which * h, h)]

        sends = []
        for t in range(n):
            for k, (fx, fy) in enumerate(CHIP_FLIPS):
                cp = remote(half(ins[t], c), half(outs[t].at[chip], c), 6 * t + k, (_flip(x, fx), _flip(y, fy), c))
                cp.start()
                sends.append(cp)
        for t in range(n):
            for k, (fx, fy) in enumerate(CHIP_FLIPS):
                landed = half(outs[t].at[2 * _flip(x, fx) + _flip(y, fy)], c)
                remote(landed, landed, 6 * t + k, (x, y, c)).wait_recv()
                fw = remote(landed, landed, 6 * t + 3 + k, sib)
                fw.start()
                sends.append(fw)
        for t in range(n):
            for k, (fx, fy) in enumerate(CHIP_FLIPS):
                got = half(outs[t].at[2 * _flip(x, fx) + _flip(y, fy)], 1 - c)
                remote(got, got, 6 * t + 3 + k, (x, y, c)).wait_recv()
        for cp in sends:
            cp.wait_send()
        token[...] = jnp.zeros_like(token)

    outs = pl.pallas_call(
        body, name="allgather_weights", in_specs=[HBM_SPEC] * n + [pl.BlockSpec(memory_space=pl.ANY)],
        out_specs=[HBM_SPEC] * n + [pl.BlockSpec(memory_space=pltpu.VMEM)],
        out_shape=[jax.ShapeDtypeStruct((N_CHIPS,) + s.shape, s.dtype) for s in shards] + [TOKEN],
        scratch_shapes=[pltpu.SemaphoreType.DMA((6 * n,)), pltpu.SemaphoreType.DMA((6 * n,))],
        compiler_params=pltpu.CompilerParams(has_side_effects=True))(*shards, after)
    return list(outs[:n]), outs[n]


def _exchange_halves(g4s, name, cols=False):
    n = len(g4s)

    def land(g):
        return (N_CHIPS, g.shape[1], g.shape[2] // 2) if cols else (N_CHIPS, g.shape[1] // 2, g.shape[2])

    def body(*refs):
        ins, outs = refs[:n], refs[n:2 * n]
        send_sems, recv_sems = refs[2 * n:]
        x, y, c = _me()
        cps = []
        for t in range(n):
            if cols:
                h = ins[t].shape[2] // 2
                theirs = ins[t].at[:, :, pl.ds((1 - c) * h, h)]
            else:
                h = ins[t].shape[1] // 2
                theirs = ins[t].at[:, pl.ds((1 - c) * h, h)]
            cp = pltpu.make_async_remote_copy(
                src_ref=theirs, dst_ref=outs[t], send_sem=send_sems.at[t],
                recv_sem=recv_sems.at[t], device_id=(x, y, 1 - c), device_id_type=MESH)
            cp.start()
            cps.append(cp)
        for cp in cps:
            cp.wait()

    return pl.pallas_call(
        body, name=name, in_specs=[HBM_SPEC] * n, out_specs=[HBM_SPEC] * n,
        out_shape=[jax.ShapeDtypeStruct(land(g), g.dtype) for g in g4s],
        scratch_shapes=[pltpu.SemaphoreType.DMA((n,)), pltpu.SemaphoreType.DMA((n,))],
        compiler_params=pltpu.CompilerParams(has_side_effects=True))(*g4s)


def _share_halves(ghs, name):
    n = len(ghs)

    def body(*refs):
        ins, outs = refs[:n], refs[n:2 * n]
        send_sems, recv_sems = refs[2 * n:]
        x, y, c = _me()
        cps = []
        for t in range(n):
            cp = pltpu.make_async_remote_copy(
                src_ref=ins[t], dst_ref=outs[t], send_sem=send_sems.at[t], recv_sem=recv_sems.at[t],
                device_id=(x, y, 1 - c), device_id_type=MESH)
            cp.start()
            cps.append(cp)
        for cp in cps:
            cp.wait()

    return pl.pallas_call(
        body, name=name, in_specs=[HBM_SPEC] * n, out_specs=[HBM_SPEC] * n,
        out_shape=[jax.ShapeDtypeStruct(g.shape, g.dtype) for g in ghs],
        scratch_shapes=[pltpu.SemaphoreType.DMA((n,)), pltpu.SemaphoreType.DMA((n,))],
        compiler_params=pltpu.CompilerParams(has_side_effects=True))(*ghs)


SEM_SPEC = pl.BlockSpec(memory_space=pltpu.SEMAPHORE)
ANY_SPEC = pl.BlockSpec(memory_space=pl.ANY)
DATAFLOW = pltpu.SideEffectType.DATAFLOW_SIDE_EFFECTING


def _in_hbm(a):
    return pltpu.with_memory_space_constraint(a, pltpu.HBM)


def _push_start(srcs, land_shapes, route, peers, name):
    n, npeer = len(srcs), len(peers)
    lands = [lax.empty(shp, s.dtype) for shp, s in zip(land_shapes, srcs)]

    def body(*refs):
        ins, lnd = refs[:n], refs[n:2 * n]
        send_sems, recv_sems = refs[2 * n], refs[2 * n + 1]
        token = refs[-1]
        x, y, c = _me()
        for t in range(n):
            for k, (fx, fy, fc) in enumerate(peers):
                src, dst = route(ins[t], lnd[t], k, x, y, c)
                pltpu.make_async_remote_copy(
                    src_ref=src, dst_ref=dst, send_sem=send_sems.at[npeer * t + k],
                    recv_sem=recv_sems.at[npeer * t + k],
                    device_id=(_flip(x, fx), _flip(y, fy), _flip(c, fc)), device_id_type=MESH).start()
        token[...] = jnp.zeros_like(token)

    bufs = [_in_hbm(a) for a in list(srcs) + lands]
    outs = pl.pallas_call(
        body, name=name,
        out_shape=(pltpu.SemaphoreType.DMA((npeer * n,)), pltpu.SemaphoreType.DMA((npeer * n,)),
                   *[pltpu.HBM(b.shape, b.dtype) for b in bufs], TOKEN),
        in_specs=[HBM_SPEC] * (2 * n),
        out_specs=(SEM_SPEC, SEM_SPEC, *[HBM_SPEC] * (2 * n), pl.BlockSpec(memory_space=pltpu.VMEM)),
        input_output_aliases={i: 2 + i for i in range(2 * n)},
        compiler_params=pltpu.CompilerParams(has_side_effects=DATAFLOW))(*bufs)
    return outs[0], outs[1], list(outs[2:2 + n]), list(outs[2 + n:2 + 2 * n]), outs[-1]


def _push_wait(send_sems, recv_sems, srcs, lands, after, route, peers, name):
    n, npeer = len(srcs), len(peers)

    def body(*refs):
        ins, lnd = refs[:n], refs[n:2 * n]
        ssem, rsem = refs[2 * n], refs[2 * n + 1]
        x, y, c = _me()
        for t in range(n):
            for k, (fx, fy, fc) in enumerate(peers):
                src, dst = route(ins[t], lnd[t], k, x, y, c)
                cp = pltpu.make_async_remote_copy(
                    src_ref=src, dst_ref=dst, send_sem=ssem.at[npeer * t + k], recv_sem=rsem.at[npeer * t + k],
                    device_id=(_flip(x, fx), _flip(y, fy), _flip(c, fc)), device_id_type=MESH)
                cp.wait_send()
                cp.wait_recv()

    bufs = list(srcs) + list(lands)
    outs = pl.pallas_call(
        body, name=name, out_shape=tuple(pltpu.HBM(b.shape, b.dtype) for b in bufs),
        in_specs=[HBM_SPEC] * (2 * n) + [SEM_SPEC, SEM_SPEC, ANY_SPEC], out_specs=tuple([HBM_SPEC] * (2 * n)),
        input_output_aliases={i: i for i in range(2 * n)},
        compiler_params=pltpu.CompilerParams(has_side_effects=DATAFLOW))(*bufs, send_sems, recv_sems, after)
    return list(outs[:n]), list(outs[n:])


OTHER_CHIPS = [(fx, fy, 0) for fx, fy in CHIP_FLIPS]
SIBLING = [(0, 0, 1)]


def _route_gather(src, land, k, x, y, c):
    return src, land.at[2 * x + y]


def _route_gather_wait(src, land, k, x, y, c):
    fx, fy = CHIP_FLIPS[k]
    return src, land.at[2 * _flip(x, fx) + _flip(y, fy)]


def _route_scatter(src, land, k, x, y, c):
    fx, fy = CHIP_FLIPS[k]
    return src.at[2 * _flip(x, fx) + _flip(y, fy)], land.at[k]


def _route_exchange(src, land, k, x, y, c):
    h = land.shape[1]
    return src.at[:, pl.ds((1 - c) * h, h)], land


def _allreduce_small(v):
    r = v.shape[0]

    def body(v_ref, o_ref, buf, send_sems, recv_sems):
        x, y, c = _me()
        me = 4 * x + 2 * y + c
        buf[0] = v_ref[...]
        cps = []
        for k in range(1, 8):
            kx, ky, kc = (k >> 2) & 1, (k >> 1) & 1, k & 1
            cp = pltpu.make_async_remote_copy(
                src_ref=v_ref, dst_ref=buf.at[k], send_sem=send_sems.at[k - 1], recv_sem=recv_sems.at[k - 1],
                device_id=(_flip(x, kx), _flip(y, ky), _flip(c, kc)), device_id_type=MESH)
            cp.start()
            cps.append(cp)
        for cp in cps:
            cp.wait()
        acc = buf[me]
        for d in range(1, 8):
            acc = acc + buf[jnp.bitwise_xor(me, d)]
        o_ref[...] = acc

    vm = pl.BlockSpec(memory_space=pltpu.VMEM)
    return pl.pallas_call(
        body, name="allreduce_small", in_specs=[vm], out_specs=vm, out_shape=jax.ShapeDtypeStruct(v.shape, F32),
        scratch_shapes=[pltpu.VMEM((8, r, LANES), F32), pltpu.SemaphoreType.DMA((7,)),
                        pltpu.SemaphoreType.DMA((7,))],
        compiler_params=pltpu.CompilerParams(has_side_effects=True, vmem_limit_bytes=VMEM_LIMIT))(v)


def _grad_exchange_start(g4, tag):
    land = (N_CHIPS, g4.shape[1] // 2, g4.shape[2])
    send_sems, recv_sems, srcs, lands, token = _push_start(
        [g4], [land], _route_exchange, SIBLING, name="grad_exchange_start_" + tag)
    return (send_sems, recv_sems, srcs, lands, tag), token


def _grad_scatter_start(state, pos, after):
    send_sems, recv_sems, srcs, lands, tag = state
    (g4,), (recv,) = _push_wait(send_sems, recv_sems, srcs, lands, after, _route_exchange, SIBLING,
                                name="grad_exchange_wait_" + tag)
    return _grad_pair_scatter(g4, recv, pos, tag)


def _grad_pair_scatter(g4, recv, pos, tag, cols=False):
    p16 = _sum_own_half(g4, recv, pos, name="grad_sum_pair_" + tag, cols=cols)
    send_sems, recv_sems, srcs, lands, token = _push_start(
        [p16], [(3,) + p16.shape[1:]], _route_scatter, OTHER_CHIPS, name="grad_scatter_start_" + tag)
    return (g4, recv, send_sems, recv_sems, srcs, lands, tag, cols), token


def _grad_reduce_begin(g4, pos, tag, cols=False):
    recv = _exchange_halves([g4], name="grad_exchange_halves_" + tag, cols=cols)[0]
    return _grad_pair_scatter(g4, recv, pos, tag, cols)


def _grad_reduce_finish(state, pos, after):
    g4, recv, send_sems, recv_sems, srcs, lands, tag, cols = state
    parts = _push_wait(send_sems, recv_sems, srcs, lands, after, _route_scatter, OTHER_CHIPS,
                       name="grad_scatter_wait_" + tag)[1][0]
    mine = _sum_chips(g4, recv, parts, pos, name="grad_sum_chips_" + tag, cols=cols)
    return mine, _share_halves([mine], name="grad_share_halves_" + tag)[0]


def _local_step(x, tgt, p, w_in_t, w_in_dt, hooks):
    t = x.shape[0]
    tables = _rope_tables(t)
    sinks = p['sinks'].reshape(N_Q_HEADS)

    def told(name, value):
        return tuple(hooks.grad_ready(name, value))

    xn = _rmsnorm_fwd(x, p['norm_mix'], "norm_mix_fwd", deps=hooks.first_deps)
    proj = _matmul(xn, w_in_t, mode='nt', name="in_proj", n_limit=MAIN_WIDTH)
    dt_raw = _matmul(xn, w_in_dt, mode='nt', name="in_proj_dt")[:, :SSD_HEADS]
    attn = _attn_fwd(proj, sinks, tables)
    conv_b = p['ssd_conv_b']
    xbc = _conv_fwd(proj, p['ssd_conv_w'], conv_b, col0=O_XBC, width=CONV_CH, act=True, name="ssd_conv_fwd")
    sp = _ssd_params(dt_raw, p['dt_bias'].reshape(-1), p['a_log'].reshape(-1), p['ssd_d'].reshape(-1))
    y, states = _ssd_fwd(xbc, sp)
    mix = _mix_fwd(attn, y, proj, p['attn_out_norm'], p['ssd_norm'])
    w_out = hooks.weight('w_out', mix)
    h1 = _matmul(mix, w_out, mode='nn', name="out_proj", add=x)
    hn = _rmsnorm_fwd(h1, p['norm_ffn'], "norm_ffn_fwd")
    w_up = hooks.weight('w_up', hn)
    u0 = _matmul(hn, w_up, mode='nn', name="ffn_up", b_owner=True, tn=1408)
    a = _ffn_act_fwd(u0, p['ffn_conv_w'], p['ffn_conv_b'])
    w_down = hooks.weight('w_down', a)
    h2 = _matmul(a, w_down, mode='nn', name="ffn_down", add=h1, tk=1408)
    loss, dh2, g_norm_final = _final_loss(h2, p['norm_final'].reshape(1, D_MODEL), tgt)

    g = {}
    da = _matmul(dh2, w_down, mode='nt', name="ffn_down_dx", out_dtype=BF16, tn=1408)
    g['w_down'] = _matmul(a, dh2, mode='tn', name="ffn_down_dw", tm=1408, tk=1024)
    dep = told('w_down', g['w_down'])
    du0, dcw, dcb = _ffn_act_bwd(u0, p['ffn_conv_w'], p['ffn_conv_b'], da)
    g['ffn_conv_w'] = dcw.transpose(1, 0, 2).reshape(FFN_CONV, 2 * D_FF)
    g['ffn_conv_b'] = dcb.transpose(1, 0, 2).reshape(1, 2 * D_FF)
    g['w_up'] = _matmul(hn, du0, mode='tn', name="ffn_up_dw", deps=dep, b_halves=True, owner_major=True,
                        tn=1408)
    dep = told('w_up', g['w_up'])
    dhn = _matmul(du0, w_up, mode='nt', name="ffn_up_dx", out_dtype=BF16, deps=dep, a_halves=True,
                  b_owner=True, tk=2816)
    dh1, g['norm_ffn'] = _rmsnorm_bwd(h1, p['norm_ffn'], dhn, dh2, "norm_ffn_bwd")

    g['w_out'] = _matmul(mix, dh1, mode='tn', name="out_proj_dw")
    dep = told('w_out', g['w_out'])
    dmix = _matmul(dh1, w_out, mode='nt', name="out_proj_dx", out_dtype=BF16, deps=dep)
    dattn, dy, dz, g['attn_out_norm'], g['ssd_norm'] = _mix_bwd(dmix, attn, y, proj, p['attn_out_norm'],
                                                                p['ssd_norm'])
    dq, dk, dv, dsink = _attn_bwd(proj, sinks, tables, dattn)
    g['sinks'] = dsink[:, :, 0].reshape(1, N_Q_HEADS)
    dxs, dbm, dcm, ddt8, dpar = _ssd_bwd(xbc, sp, states, dy)
    dpar = dpar[:, :, ::SSD_HEAD_DIM]
    g['dt_bias'] = dpar[:, 0, :].reshape(1, SSD_HEADS)
    g['a_log'] = dpar[:, 1, :].reshape(1, SSD_HEADS)
    g['ssd_d'] = dpar[:, 2, :].reshape(1, SSD_HEADS)
    dxbc_act = jnp.concatenate([dxs, dbm, dcm], axis=1)
    dconv = _conv_silu_dact(proj, p['ssd_conv_w'], conv_b, dxbc_act, col0=O_XBC, width=CONV_CH,
                            name="ssd_conv_dact")
    dxbc, g['ssd_conv_w'], g['ssd_conv_b'] = _conv_bwd(proj, p['ssd_conv_w'], dconv, col0=O_XBC, width=CONV_CH,
                                                       name="ssd_conv_bwd")
    dproj = jnp.concatenate([dq, dk, dv, dz, dxbc], axis=1)
    ddt = ddt8.transpose(2, 0, 1).reshape(t, SSD_HEADS)
    ddt_pad = jnp.pad(ddt, ((0, 0), (0, LANES - SSD_HEADS))).astype(BF16)
    g['w_in'] = (_matmul(dproj, xn, mode='tn', name="in_proj_dw", m_rows=IN_PROJ_WIDTH),
                 _matmul(ddt_pad, xn, mode='tn', name="in_proj_dt_dw"))
    dep = told('w_in', g['w_in'])
    dxn_dt = _matmul(ddt_pad, w_in_dt, mode='nn', name="in_proj_dt_dx", deps=dep)
    dxn = _matmul(dproj, w_in_t, mode='nn', name="in_proj_dx", out_dtype=BF16, add=dxn_dt, k_limit=MAIN_WIDTH,
                  tk=2304)
    dx, g['norm_mix'] = _rmsnorm_bwd(x, p['norm_mix'], dxn, dh1, "norm_mix_bwd")
    g['norm_final'] = g_norm_final
    return loss, dx, g


def _pack(arrs):
    flat = jnp.concatenate([a.reshape(-1) for a in arrs])
    n = flat.shape[0]
    rows = -(-n // LANES)
    rows = -(-rows // 8) * 8
    return jnp.pad(flat, (0, rows * LANES - n)).reshape(rows, LANES)


def _unpack(packed, shapes):
    flat = packed.reshape(-1)
    out, off = [], 0
    for s in shapes:
        n = 1
        for d in s:
            n *= d
        out.append(flat[off:off + n].reshape(s))
        off += n
    return out


class _StepHooks:
    def __init__(self, first_deps, weight, grad_ready):
        self.first_deps = first_deps
        self.weight = weight
        self.grad_ready = grad_ready


def kernel(x, norm_mix, w_in, sinks, attn_out_norm, ssd_conv_w, ssd_conv_b, dt_bias, a_log, ssd_d, ssd_norm, w_out, norm_ffn, w_up, ffn_conv_w, ffn_conv_b, w_down, norm_final, loss_target, m_norm_mix, m_w_in, m_sinks, m_attn_out_norm, m_ssd_conv_w, m_ssd_conv_b, m_dt_bias, m_a_log, m_ssd_d, m_ssd_norm, m_w_out, m_norm_ffn, m_w_up, m_ffn_conv_w, m_ffn_conv_b, m_w_down, m_norm_final, v_norm_mix, v_w_in, v_sinks, v_attn_out_norm, v_ssd_conv_w, v_ssd_conv_b, v_dt_bias, v_a_log, v_ssd_d, v_ssd_norm, v_w_out, v_norm_ffn, v_w_up, v_ffn_conv_w, v_ffn_conv_b, v_w_down, v_norm_final):
    args = dict(locals())
    w = {n: args[n] for n in WEIGHTS}
    m = {n: args['m_' + n] for n in WEIGHTS}
    v = {n: args['v_' + n] for n in WEIGHTS}
    xi, yi, ci = _me()
    chip = 2 * xi + yi
    pos = jnp.stack([ci, chip]).astype(jnp.int32)

    def place(shard, full_cols):
        z = jnp.zeros((shard.shape[0], full_cols), F32)
        return lax.dynamic_update_slice(z, shard * 0.5, (0, chip * shard.shape[1]))

    conv_pack = _pack([place(ssd_conv_w[0], CONV_CH), place(ffn_conv_w[0], 2 * D_FF)])
    conv_full = _allreduce_small(conv_pack)
    ssd_conv_w_full, ffn_conv_w_full = _unpack(conv_full, [(SSD_CONV, CONV_CH), (FFN_CONV, 2 * D_FF)])

    w_in_t, m_in_t, v_in_t = (jnp.transpose(a[0]) for a in (w_in, m_w_in, v_w_in))
    in_shard = w_in_t.astype(BF16)
    (gathered,), order = _allgather_weights([in_shard], conv_full, cols=True)
    full_in_t = lax.dynamic_update_slice(gathered, in_shard[None], (chip, 0, 0)).reshape(IN_PROJ_WIDTH, D_MODEL)
    w_in_dt = jnp.pad(full_in_t[MAIN_WIDTH:], ((0, LANES - SSD_HEADS), (0, 0)))
    gathers = {}
    order = order[:1, :1]
    for n, shard in (('w_out', w_out[0]), ('w_up', w_up[0]), ('w_down', w_down[0])):
        shard = (shard + order).astype(BF16)
        gathers[n] = _push_start([shard], [(N_CHIPS,) + shard.shape], _route_gather, OTHER_CHIPS,
                                 name="gather_start_" + n)
        order = gathers[n][4][:1, :1]
    first_deps = [gathers['w_down'][4]]

    def weight(name, after):
        send_sems, recv_sems, srcs, lands, _ = gathers[name]
        (own,), (got,) = _push_wait(send_sems, recv_sems, srcs, lands, after, _route_gather_wait, OTHER_CHIPS,
                                    name="gather_wait_" + name)
        whole = lax.dynamic_update_slice(got, own[None], (chip, 0, 0))
        return whole if name == 'w_up' else whole.reshape(-1, D_MODEL)

    reductions, exchanging = {}, {}

    def grad_ready(name, value):
        if name == 'w_in':
            main, dtp = value
            value = lax.dynamic_update_slice(main, dtp[:SSD_HEADS], (MAIN_WIDTH, 0))
        g4 = value if value.ndim == 3 else value.reshape(N_CHIPS, -1, value.shape[1])
        tokens = []
        for prev in list(exchanging):
            reductions[prev], token = _grad_scatter_start(exchanging.pop(prev), pos, g4)
            tokens.append(token)
        if name == 'w_in':
            reductions[name], token = _grad_reduce_begin(g4, pos, name, cols=True)
        else:
            exchanging[name], token = _grad_exchange_start(g4, name)
        return tokens + [token]

    small = {
        'norm_mix': norm_mix, 'sinks': sinks, 'attn_out_norm': attn_out_norm, 'ssd_conv_w': ssd_conv_w_full,
        'ssd_conv_b': ssd_conv_b, 'dt_bias': dt_bias, 'a_log': a_log, 'ssd_d': ssd_d, 'ssd_norm': ssd_norm,
        'norm_ffn': norm_ffn, 'ffn_conv_w': ffn_conv_w_full, 'ffn_conv_b': ffn_conv_b, 'norm_final': norm_final,
    }
    loss, dx, g = _local_step(x[0], loss_target[0], small, full_in_t, w_in_dt,
                              _StepHooks(tuple(first_deps), weight, grad_ready))
    gbig = {n: _grad_reduce_finish(reductions[n], pos, dx) for n in ('w_down', 'w_up', 'w_out', 'w_in')}

    small_names = [n for n in WEIGHTS if n not in BIG]
    small_g = [loss[:, :1]] + [g[n] for n in small_names]
    small_shapes = [(1, 1)] + [tuple(a.shape) for a in small_g[1:]]
    red = _unpack(_allreduce_small(_pack(small_g)), small_shapes)
    loss_out = red[0].reshape(())
    gsm = dict(zip(small_names, red[1:]))
    gsm['ssd_conv_w'] = lax.dynamic_slice(gsm['ssd_conv_w'], (0, chip * ssd_conv_w.shape[2]),
                                          (SSD_CONV, ssd_conv_w.shape[2]))
    gsm['ffn_conv_w'] = lax.dynamic_slice(gsm['ffn_conv_w'], (0, chip * ffn_conv_w.shape[2]),
                                          (FFN_CONV, ffn_conv_w.shape[2]))

    grads, deltas, new_m, new_v = {}, {}, {}, {}
    for n in BIG:
        mine, theirs = gbig[n]
        if n == 'w_in':
            outs = _adamw_halves(w_in_t, mine, theirs, m_in_t, v_in_t, pos, name="adamw_" + n, cols=True)
            outs = [jnp.transpose(o) for o in outs]
        else:
            outs = _adamw_halves(w[n][0], mine, theirs, m[n][0], v[n][0], pos, name="adamw_" + n)
        grads[n], deltas[n], new_m[n], new_v[n] = [o[None] for o in outs]
    shapes = [tuple(w[n].shape) for n in small_names]
    gp = _pack([gsm[n] for n in small_names])
    d, m2, v2 = _adamw(_pack([w[n] for n in small_names]), gp, _pack([m[n] for n in small_names]),
                       _pack([v[n] for n in small_names]), name="adamw_small")
    for n, gg, dd, mm, vv in zip(small_names, _unpack(gp, shapes), _unpack(d, shapes), _unpack(m2, shapes),
                                 _unpack(v2, shapes)):
        grads[n], deltas[n], new_m[n], new_v[n] = gg, dd, mm, vv

    return (loss_out, dx[None], *[grads[n] for n in WEIGHTS], *[deltas[n] for n in WEIGHTS],
            *[new_m[n] for n in WEIGHTS], *[new_v[n] for n in WEIGHTS])
```

```python
import functools

import jax
import jax.numpy as jnp
from jax import lax
from jax.experimental import pallas as pl
from jax.experimental.pallas import tpu as pltpu

F32 = jnp.float32
BF16 = jnp.bfloat16

D_MODEL = 2048
N_Q_HEADS = 32
N_KV_HEADS = 8
HEAD_DIM = 64
WINDOW = 128
ATTN_BLOCK = 128
ROT_DIM = 16
ROPE_THETA = 500000.0
SSD_HEADS = 32
SSD_HEAD_DIM = 64
SSD_INNER = 2048
SSD_GROUPS = 8
SSD_STATE = 128
SSD_CONV = 4
SSD_CHUNK = 128
ATTN_WIDTH = 2048
KV_WIDTH = 512
BC_WIDTH = 1024
CONV_CH = 4096
IN_PROJ_WIDTH = 9248
MAIN_WIDTH = 9216
D_FF = 5632
FFN_CONV = 3
EPS = 1e-6
O_Q, O_K, O_V, O_Z, O_XBC, O_DT = 0, 2048, 2560, 3072, 5120, 9216

ADAM_LR = 0.001
ADAM_B1 = 0.9
ADAM_B2 = 0.999
ADAM_EPS = 1e-08
ADAM_WD = 0.01
ADAM_STEP = 10

N_CHIPS = 4
NEG = -1e30
LANES = 128
VMEM_LIMIT = 48 * 1024 * 1024
MESH = pl.DeviceIdType.MESH
HBM_SPEC = pl.BlockSpec(memory_space=pltpu.HBM)
TOKEN = jax.ShapeDtypeStruct((8, LANES), F32)

WEIGHTS = ['norm_mix', 'w_in', 'sinks', 'attn_out_norm', 'ssd_conv_w', 'ssd_conv_b', 'dt_bias', 'a_log', 'ssd_d',
           'ssd_norm', 'w_out', 'norm_ffn', 'w_up', 'ffn_conv_w', 'ffn_conv_b', 'w_down', 'norm_final']
BIG = ['w_in', 'w_out', 'w_up', 'w_down']


def _cp(sem=None, vmem=VMEM_LIMIT):
    kw = {'vmem_limit_bytes': vmem}
    if sem is not None:
        kw['dimension_semantics'] = sem
    return pltpu.CompilerParams(**kw)


def _tile(n, pref):
    if n <= pref:
        return n
    t = (pref // LANES) * LANES
    while t > LANES and n % t:
        t -= LANES
    assert n % t == 0, (n, pref)
    return t


def _rows(n, pref):
    t = min(n, pref)
    while n % t:
        t -= 8
    if 4 * t < pref:
        t = pref
        while n % t:
            t += 8
    return t


def _iota(shape, dim):
    return lax.broadcasted_iota(jnp.int32, shape, dim)


def _dot(a, b, mode='nn'):
    dn = {'nn': (((1,), (0,)), ((), ())), 'nt': (((1,), (1,)), ((), ())), 'tn': (((0,), (0,)), ((), ()))}[mode]
    return lax.dot_general(a.astype(BF16), b.astype(BF16), dn, preferred_element_type=F32)


def _dot_exact(a, b):
    return lax.dot_general(a, b, (((1,), (0,)), ((), ())), precision=lax.Precision.HIGHEST,
                           preferred_element_type=F32)


def _sigmoid(x):
    return 1.0 / (1.0 + jnp.exp(-x))


def _softplus(x):
    return jnp.maximum(x, 0.0) + jnp.log(1.0 + jnp.exp(-jnp.abs(x)))


def _matmul(a, b, *, mode, name, out_dtype=F32, add=None, deps=(), tm=1024, tn=1024, tk=2048,
            a_halves=False, b_halves=False, b_owner=False, owner_major=False, n_limit=None, k_limit=None,
            m_rows=None):
    ash, bsh = (a.shape[1:] if a_halves else a.shape), (b.shape[1:] if (b_halves or b_owner) else b.shape)
    if mode == 'nn':
        (m, k), (k2, n) = ash, bsh
    elif mode == 'nt':
        (m, k), (n, k2) = ash, bsh
    else:
        (k, m), (k2, n) = ash, bsh
    if n_limit is not None:
        assert mode == 'nt' and n_limit <= n
        n = n_limit
    if k_limit is not None:
        assert mode == 'nn' and k_limit <= k2
        k2 = k_limit
    if a_halves:
        assert mode == 'nt'
        k = 2 * k
    if b_halves:
        assert mode == 'tn'
        n = 2 * n
    if b_owner:
        assert mode in ('nn', 'nt')
        if mode == 'nn':
            n = 4 * n
        else:
            k2 = 4 * k2
    assert k == k2, (a.shape, b.shape, mode)
    tm = _tile(m, tm)
    tn = _tile(n // 4 if (owner_major or (b_owner and mode == 'nn')) else (n // 2 if b_halves else n), tn)
    tk = _tile(k // 4 if (b_owner and mode == 'nt') else (k // 2 if a_halves else k), tk)
    nk = k // tk
    has_add = add is not None
    assert not (has_add and owner_major)

    def body(*refs):
        a_ref, b_ref = refs[:2]
        add_ref = refs[2] if has_add else None

        def finish(r, o_ref):
            if has_add:
                r = r + add_ref[...].astype(F32)
            o_ref[...] = r.astype(out_dtype)

        if nk == 1:
            finish(_dot(a_ref[...], b_ref[...], mode), refs[-1])
            return
        o_ref, acc = refs[-2:]
        kk = pl.program_id(2)

        @pl.when(kk == 0)
        def _():
            acc[...] = _dot(a_ref[...], b_ref[...], mode)

        @pl.when((kk > 0) & (kk < nk - 1))
        def _():
            acc[...] += _dot(a_ref[...], b_ref[...], mode)

        @pl.when(kk == nk - 1)
        def _():
            finish(acc[...] + _dot(a_ref[...], b_ref[...], mode), o_ref)

    if mode == 'tn':
        a_spec = pl.BlockSpec((tk, tm), lambda i, j, kk: (kk, i))
    elif a_halves:
        nkh = nk // 2
        a_spec = pl.BlockSpec((None, tm, tk), lambda i, j, kk: (kk // nkh, i, kk % nkh))
    else:
        a_spec = pl.BlockSpec((tm, tk), lambda i, j, kk: (i, kk))
    if mode == 'nt' and b_owner:
        nkq = nk // 4
        b_spec = pl.BlockSpec((None, tn, tk), lambda i, j, kk: (kk // nkq, j, kk % nkq))
    elif mode == 'nt':
        b_spec = pl.BlockSpec((tn, tk), lambda i, j, kk: (j, kk))
    elif b_owner:
        njq = (n // 4) // tn
        b_spec = pl.BlockSpec((None, tk, tn), lambda i, j, kk: (j // njq, kk, j % njq))
    elif b_halves:
        njh = (n // 2) // tn
        b_spec = pl.BlockSpec((None, tk, tn), lambda i, j, kk: (j // njh, kk, j % njh))
    else:
        b_spec = pl.BlockSpec((tk, tn), lambda i, j, kk: (kk, j))
    if owner_major:
        njo = (n // 4) // tn
        o_spec = pl.BlockSpec((None, tm, tn), lambda i, j, kk: (j // njo, i, j % njo))
        out_shape = jax.ShapeDtypeStruct((N_CHIPS, m, n // 4), out_dtype)
    else:
        o_spec = pl.BlockSpec((tm, tn), lambda i, j, kk: (i, j))
        out_shape = jax.ShapeDtypeStruct((m if m_rows is None else m_rows, n), out_dtype)
    dep_spec = pl.BlockSpec((8, LANES), lambda i, j, kk: (0, 0))
    in_specs = [a_spec, b_spec] + ([pl.BlockSpec((tm, tn), lambda i, j, kk: (i, j))] if has_add else [])
    in_specs += [dep_spec] * len(deps)
    args = (a, b) + ((add,) if has_add else ()) + tuple(deps)
    return pl.pallas_call(
        body, name=name, grid=(m // tm, n // tn, nk), in_specs=in_specs, out_specs=o_spec, out_shape=out_shape,
        scratch_shapes=[pltpu.VMEM((tm, tn), F32)] if nk > 1 else [],
        compiler_params=_cp(("parallel", "parallel", "arbitrary")))(*args)


def _rmsnorm_fwd(x, g, name, deps=()):
    t, d = x.shape
    tb = _rows(t, 256)

    def body(x_ref, g_ref, *rest):
        o_ref = rest[-1]
        xv = x_ref[...]
        r = lax.rsqrt(jnp.mean(xv * xv, axis=-1, keepdims=True) + EPS)
        o_ref[...] = (xv * r * g_ref[...]).astype(BF16)

    dep_spec = pl.BlockSpec((8, LANES), lambda i: (0, 0))
    return pl.pallas_call(
        body, name=name, grid=(t // tb,),
        in_specs=[pl.BlockSpec((tb, d), lambda i: (i, 0)), pl.BlockSpec((1, d), lambda i: (0, 0))]
        + [dep_spec] * len(deps),
        out_specs=pl.BlockSpec((tb, d), lambda i: (i, 0)), out_shape=jax.ShapeDtypeStruct((t, d), BF16),
        compiler_params=_cp(("parallel",)))(x, g, *deps)


def _rmsnorm_bwd(x, g, dy, res, name):
    t, d = x.shape
    tb = _rows(t, 256)

    def body(x_ref, g_ref, dy_ref, res_ref, dx_ref, dx16_ref, dg_ref):
        i = pl.program_id(0)
        xv = x_ref[...]
        dyv = dy_ref[...].astype(F32)
        r = lax.rsqrt(jnp.mean(xv * xv, axis=-1, keepdims=True) + EPS)
        u = dyv * g_ref[...]
        dx = r * u - xv * (r * r * r * jnp.mean(u * xv, axis=-1, keepdims=True)) + res_ref[...]
        dx_ref[...] = dx
        dx16_ref[...] = dx.astype(BF16)
        part = jnp.sum(dyv * xv * r, axis=0, keepdims=True)

        @pl.when(i == 0)
        def _():
            dg_ref[...] = part

        @pl.when(i > 0)
        def _():
            dg_ref[...] += part

    row = pl.BlockSpec((tb, d), lambda i: (i, 0))
    vec = pl.BlockSpec((1, d), lambda i: (0, 0))
    return pl.pallas_call(
        body, name=name, grid=(t // tb,), in_specs=[row, vec, row, row], out_specs=[row, row, vec],
        out_shape=[jax.ShapeDtypeStruct((t, d), F32), jax.ShapeDtypeStruct((t, d), BF16),
                   jax.ShapeDtypeStruct((1, d), F32)],
        compiler_params=_cp(("arbitrary",)))(x, g, dy, res)


def _final_loss(h, g, tgt):
    t, d = h.shape
    tb = _rows(t, 256)

    def body(h_ref, g_ref, t_ref, loss_ref, dh_ref, dh16_ref, dg_ref):
        i = pl.program_id(0)
        hv = h_ref[...]
        gv = g_ref[...]
        r = lax.rsqrt(jnp.mean(hv * hv, axis=-1, keepdims=True) + EPS)
        y = hv * r * gv
        diff = y - t_ref[...]
        lpart = jnp.sum(jnp.sum(diff * diff, axis=1, keepdims=True), axis=0, keepdims=True) * (0.5 / d)
        dy = diff * (1.0 / d)
        u = dy * gv
        dh = r * u - hv * (r * r * r * jnp.mean(u * hv, axis=-1, keepdims=True))
        dh_ref[...] = dh
        dh16_ref[...] = dh.astype(BF16)
        gpart = jnp.sum(dy * hv * r, axis=0, keepdims=True)
        lrow = jnp.broadcast_to(lpart, (1, LANES))

        @pl.when(i == 0)
        def _():
            loss_ref[...] = lrow
            dg_ref[...] = gpart

        @pl.when(i > 0)
        def _():
            loss_ref[...] += lrow
            dg_ref[...] += gpart

    row = pl.BlockSpec((tb, d), lambda i: (i, 0))
    vec = pl.BlockSpec((1, d), lambda i: (0, 0))
    return pl.pallas_call(
        body, name="final_loss", grid=(t // tb,), in_specs=[row, vec, row],
        out_specs=[pl.BlockSpec((1, LANES), lambda i: (0, 0)), row, row, vec],
        out_shape=[jax.ShapeDtypeStruct((1, LANES), F32), jax.ShapeDtypeStruct((t, d), F32),
                   jax.ShapeDtypeStruct((t, d), BF16), jax.ShapeDtypeStruct((1, d), F32)],
        compiler_params=_cp(("arbitrary",)))(h, g, tgt)


def _rope_tables(t):
    pos = jnp.arange(t, dtype=F32)
    inv = 1.0 / (ROPE_THETA ** (jnp.arange(0, ROT_DIM, 2, dtype=F32) / ROT_DIM))
    ang = pos[:, None] * inv[None, :]
    cos, sin = jnp.cos(ang), jnp.sin(ang)
    half = ROT_DIM // 2
    rest = HEAD_DIM - ROT_DIM
    c = jnp.concatenate([cos, cos, jnp.ones((t, rest), F32)], axis=1)
    s1 = jnp.concatenate([-sin, jnp.zeros((t, half + rest), F32)], axis=1)
    s2 = jnp.concatenate([jnp.zeros((t, half), F32), sin, jnp.zeros((t, rest), F32)], axis=1)
    return tuple(jnp.tile(v, (1, LANES // HEAD_DIM)) for v in (c, s1, s2))


def _rope(x, c, s1, s2):
    half = ROT_DIM // 2
    return x * c + pltpu.roll(x, LANES - half, 1) * s1 + pltpu.roll(x, half, 1) * s2


def _rope_t(g, c, s1, s2):
    half = ROT_DIM // 2
    return g * c + pltpu.roll(g * s1, half, 1) + pltpu.roll(g * s2, LANES - half, 1)


def _band_masks(i, heads):
    n = heads * ATTN_BLOCK
    q = jnp.bitwise_and(_iota((n, ATTN_BLOCK), 0), ATTN_BLOCK - 1)
    j = _iota((n, ATTN_BLOCK), 1)
    upper = j > q
    return upper, upper & (j < jnp.where(i > 0, 0, ATTN_BLOCK))


def _fold_band(full, upper):
    return jnp.where(upper, full[:, :ATTN_BLOCK], full[:, ATTN_BLOCK:])


def _unfold_band(band, upper):
    return jnp.concatenate([jnp.where(upper, band, 0.0), jnp.where(upper, 0.0, band)], axis=1)


def _half_masks():
    lane = _iota((1, LANES), 1)
    return [(lane < HEAD_DIM).astype(F32), (lane >= HEAD_DIM).astype(F32)]


def _stack_heads(blocks, hm, j):
    pieces = []
    for r in range(4):
        qb, half = (4 * j + r) // 2, (4 * j + r) % 2
        piece = blocks[qb] * hm[half]
        if half != j:
            piece = pltpu.roll(piece, HEAD_DIM, 1)
        pieces.append(piece)
    return jnp.concatenate(pieces, axis=0)


def _unstack_heads(stacked, j):
    out = []
    for qb in (2 * j, 2 * j + 1):
        acc = None
        for half in range(2):
            r = 2 * qb + half - 4 * j
            piece = stacked[r * ATTN_BLOCK:(r + 1) * ATTN_BLOCK]
            if half != j:
                piece = pltpu.roll(piece, HEAD_DIM, 1)
            acc = piece if acc is None else acc + piece
        out.append((qb, acc))
    return out


def _sink_column(sink_ref, base):
    return jnp.concatenate([jnp.full((ATTN_BLOCK, 1), sink_ref[base + r], F32) for r in range(4)], axis=0)


def _attn_specs(nb_clamp):
    blk = ATTN_BLOCK
    kb, vb = O_K // LANES, O_V // LANES

    def cur(i):
        return jnp.minimum(i, nb_clamp)

    def prev(i):
        return jnp.maximum(jnp.minimum(i, nb_clamp + 1) - 1, 0)

    q = pl.BlockSpec((blk, 512), lambda p, i: (cur(i), p))
    kc = pl.BlockSpec((blk, LANES), lambda p, i: (cur(i), kb + p))
    kp = pl.BlockSpec((blk, LANES), lambda p, i: (prev(i), kb + p))
    vc = pl.BlockSpec((blk, LANES), lambda p, i: (cur(i), vb + p))
    vp = pl.BlockSpec((blk, LANES), lambda p, i: (prev(i), vb + p))
    tc = pl.BlockSpec((blk, LANES), lambda p, i: (cur(i), 0))
    tp = pl.BlockSpec((blk, LANES), lambda p, i: (prev(i), 0))
    return q, kc, kp, vc, vp, tc, tp


def _attn_fwd(proj, sinks, tables):
    t = proj.shape[0]
    nb = t // ATTN_BLOCK
    scale = HEAD_DIM ** -0.5

    def body(sink_ref, q_ref, kc_ref, kp_ref, vc_ref, vp_ref, cc_ref, s1c_ref, s2c_ref, cp_ref, s1p_ref, s2p_ref,
             o_ref):
        p = pl.program_id(0)
        i = pl.program_id(1)
        cc, s1c, s2c = cc_ref[...], s1c_ref[...], s2c_ref[...]
        kband = jnp.concatenate([_rope(kp_ref[...], cp_ref[...], s1p_ref[...], s2p_ref[...]),
                                 _rope(kc_ref[...], cc, s1c, s2c)], axis=0).astype(BF16)
        vband = jnp.concatenate([vp_ref[...], vc_ref[...]], axis=0)
        hm = _half_masks()
        vsel = [(vband * hm[j]).astype(BF16) for j in range(2)]
        upper, dropped = _band_masks(i, 1)
        for qb in range(4):
            qr = _rope(q_ref[:, qb * LANES:(qb + 1) * LANES], cc, s1c, s2c)
            acc = jnp.zeros((ATTN_BLOCK, LANES), F32)
            for half in range(2):
                hh = qb * 2 + half
                j = hh // 4
                qs = qr * hm[half]
                if half != j:
                    qs = pltpu.roll(qs, HEAD_DIM, 1)
                s = jnp.where(dropped, NEG, _fold_band(_dot(qs, kband, 'nt'), upper) * scale)
                sink = sink_ref[p * 8 + hh]
                m = jnp.maximum(jnp.max(s, axis=1, keepdims=True), sink)
                pe = jnp.exp(s - m)
                den = jnp.sum(pe, axis=1, keepdims=True) + jnp.exp(sink - m)
                o = _dot(_unfold_band(pe / den, upper), vsel[j])
                if half != j:
                    o = pltpu.roll(o, HEAD_DIM, 1)
                acc = acc + o
            o_ref[:, qb * LANES:(qb + 1) * LANES] = acc

    q, kc, kp, vc, vp, tc, tp = _attn_specs(nb - 1)
    smem = pl.BlockSpec(memory_space=pltpu.SMEM)
    return pl.pallas_call(
        body, name="attn_fwd", grid=(4, nb),
        in_specs=[smem, q, kc, kp, vc, vp, tc, tc, tc, tp, tp, tp],
        out_specs=pl.BlockSpec((ATTN_BLOCK, 512), lambda p, i: (i, p)),
        out_shape=jax.ShapeDtypeStruct((t, ATTN_WIDTH), F32),
        compiler_params=_cp(("parallel", "arbitrary")))(sinks, proj, proj, proj, proj, proj, *tables, *tables)


def _attn_bwd(proj, sinks, tables, dout):
    t = proj.shape[0]
    nb = t // ATTN_BLOCK
    scale = HEAD_DIM ** -0.5

    def body(sink_ref, q_ref, kc_ref, kp_ref, vc_ref, vp_ref, cc_ref, s1c_ref, s2c_ref, cp_ref, s1p_ref, s2p_ref,
             do_ref, dq_ref, dk_ref, dv_ref, ds_ref, carry_k, carry_v):
        p = pl.program_id(0)
        i = pl.program_id(1)
        ptab = (cp_ref[...], s1p_ref[...], s2p_ref[...])

        @pl.when(i == 0)
        def _():
            carry_k[...] = jnp.zeros_like(carry_k)
            carry_v[...] = jnp.zeros_like(carry_v)
            ds_ref[...] = jnp.zeros_like(ds_ref)

        @pl.when(i < nb)
        def _():
            cc, s1c, s2c = cc_ref[...], s1c_ref[...], s2c_ref[...]
            kband = jnp.concatenate([_rope(kp_ref[...], *ptab), _rope(kc_ref[...], cc, s1c, s2c)], axis=0)
            vband = jnp.concatenate([vp_ref[...], vc_ref[...]], axis=0)
            hm = _half_masks()
            kband16 = kband.astype(BF16)
            vband16 = vband.astype(BF16)
            upper, dropped = _band_masks(i, 4)
            dkb = jnp.zeros((2 * ATTN_BLOCK, LANES), F32)
            dvb = jnp.zeros((2 * ATTN_BLOCK, LANES), F32)
            row8 = _iota((8, LANES), 0)
            dsink = jnp.zeros((8, LANES), F32)
            qr = [_rope(q_ref[:, qb * LANES:(qb + 1) * LANES], cc, s1c, s2c) for qb in range(4)]
            dob = [do_ref[:, qb * LANES:(qb + 1) * LANES] for qb in range(4)]
            for j in range(2):
                qst = _stack_heads(qr, hm, j).astype(BF16)
                dost = _stack_heads(dob, hm, j).astype(BF16)
                s = jnp.where(dropped, NEG, _fold_band(_dot(qst, kband16, 'nt'), upper) * scale)
                sink = _sink_column(sink_ref, p * 8 + 4 * j)
                m = jnp.maximum(jnp.max(s, axis=1, keepdims=True), sink)
                pe = jnp.exp(s - m)
                psink = jnp.exp(sink - m)
                den = jnp.sum(pe, axis=1, keepdims=True) + psink
                pr = pe / den
                dvb = dvb + _dot(_unfold_band(pr, upper).T, dost)
                dp = _fold_band(_dot(dost, vband16, 'nt'), upper)
                delta = jnp.sum(pr * dp, axis=1, keepdims=True)
                dsc = _unfold_band(pr * (dp - delta) * scale, upper)
                dsk = psink / den * delta
                for r in range(4):
                    part = jnp.sum(dsk[r * ATTN_BLOCK:(r + 1) * ATTN_BLOCK])
                    dsink = dsink + jnp.where(row8 == 4 * j + r, -part, 0.0)
                for qb, dqb in _unstack_heads(_dot(dsc, kband * hm[j]), j):
                    dq_ref[:, qb * LANES:(qb + 1) * LANES] = _rope_t(dqb, cc, s1c, s2c).astype(BF16)
                dkb = dkb + _dot(dsc.T, qst)
            ds_ref[0] += dsink
            dk_ref[...] = _rope_t(carry_k[...] + dkb[:ATTN_BLOCK], *ptab).astype(BF16)
            dv_ref[...] = (carry_v[...] + dvb[:ATTN_BLOCK]).astype(BF16)
            carry_k[...] = dkb[ATTN_BLOCK:]
            carry_v[...] = dvb[ATTN_BLOCK:]

        @pl.when(i == nb)
        def _():
            dk_ref[...] = _rope_t(carry_k[...], *ptab).astype(BF16)
            dv_ref[...] = carry_v[...].astype(BF16)

    q, kc, kp, vc, vp, tc, tp = _attn_specs(nb - 1)
    smem = pl.BlockSpec(memory_space=pltpu.SMEM)
    qblk = pl.BlockSpec((ATTN_BLOCK, 512), lambda p, i: (jnp.minimum(i, nb - 1), p))
    kvout = pl.BlockSpec((ATTN_BLOCK, LANES), lambda p, i: (jnp.maximum(i - 1, 0), p))
    return pl.pallas_call(
        body, name="attn_bwd", grid=(4, nb + 1),
        in_specs=[smem, q, kc, kp, vc, vp, tc, tc, tc, tp, tp, tp, qblk],
        out_specs=[qblk, kvout, kvout, pl.BlockSpec((1, 8, LANES), lambda p, i: (p, 0, 0))],
        out_shape=[jax.ShapeDtypeStruct((t, ATTN_WIDTH), BF16), jax.ShapeDtypeStruct((t, KV_WIDTH), BF16),
                   jax.ShapeDtypeStruct((t, KV_WIDTH), BF16), jax.ShapeDtypeStruct((4, 8, LANES), F32)],
        scratch_shapes=[pltpu.VMEM((ATTN_BLOCK, LANES), F32), pltpu.VMEM((ATTN_BLOCK, LANES), F32)],
        compiler_params=_cp(("parallel", "arbitrary")))(sinks, proj, proj, proj, proj, proj, *tables, *tables, dout)


def _shift_rows(x, prev8, j):
    r = pltpu.roll(x, j, 0)
    head = jnp.where(_iota((8, 1), 0) < j, pltpu.roll(prev8, j, 0), r[:8])
    if x.shape[0] == 8:
        return head
    return jnp.concatenate([head, r[8:]], axis=0)


def _shift_rows_up(x, next8, j):
    n = x.shape[0]
    r = pltpu.roll(x, n - j, 0)
    tail = jnp.where(_iota((8, 1), 0) >= 8 - j, pltpu.roll(next8, 8 - j, 0), r[n - 8:])
    return jnp.concatenate([r[:n - 8], tail], axis=0)


def _conv_apply(x, prev8, w, b, taps):
    u = b + x * w[taps - 1:taps]
    for j in range(1, taps):
        u = u + _shift_rows(x, prev8, j) * w[taps - 1 - j:taps - j]
    return u


def _conv_grads(du, du_next8, x, w, taps):
    dx = du * w[taps - 1:taps]
    rowk = _iota((taps, 1), 0)
    dw = jnp.where(rowk == taps - 1, jnp.sum(du * x, axis=0, keepdims=True), 0.0)
    for j in range(1, taps):
        ahead = _shift_rows_up(du, du_next8, j)
        dx = dx + ahead * w[taps - 1 - j:taps - j]
        dw = dw + jnp.where(rowk == taps - 1 - j, jnp.sum(ahead * x, axis=0, keepdims=True), 0.0)
    return dx, dw, jnp.sum(du, axis=0, keepdims=True)


def _conv_specs(tb, tc, col0, t):
    c0 = col0 // tc
    cur = pl.BlockSpec((tb, tc), lambda j, i: (i, c0 + j))
    prev = pl.BlockSpec((8, tc), lambda j, i: (jnp.maximum(i * (tb // 8) - 1, 0), c0 + j))
    nxt = pl.BlockSpec((8, tc), lambda j, i: (jnp.minimum((i + 1) * (tb // 8), t // 8 - 1), c0 + j))
    return cur, prev, nxt


def _conv_fwd(x, w, b, *, col0, width, act, name):
    t = x.shape[0]
    taps = w.shape[0]
    tb, tc = _rows(t, 512), _tile(width, 1024)
    assert col0 % tc == 0

    def body(x_ref, xp_ref, w_ref, b_ref, o_ref):
        i = pl.program_id(1)
        prev8 = jnp.where(i > 0, xp_ref[...], 0.0)
        u = _conv_apply(x_ref[...], prev8, w_ref[...], b_ref[...], taps)
        if act:
            u = u * _sigmoid(u)
        o_ref[...] = u

    cur, prev, _ = _conv_specs(tb, tc, col0, t)
    par = pl.BlockSpec((taps, tc), lambda j, i: (0, j))
    bias = pl.BlockSpec((1, tc), lambda j, i: (0, j))
    return pl.pallas_call(
        body, name=name, grid=(width // tc, t // tb), in_specs=[cur, prev, par, bias],
        out_specs=pl.BlockSpec((tb, tc), lambda j, i: (i, j)), out_shape=jax.ShapeDtypeStruct((t, width), F32),
        compiler_params=_cp(("parallel", "parallel")))(x, x, w, b)


def _conv_silu_dact(x, w, b, dout, *, col0, width, name):
    t = x.shape[0]
    taps = w.shape[0]
    tb, tc = _rows(t, 512), _tile(width, 1024)

    def body(x_ref, xp_ref, w_ref, b_ref, d_ref, o_ref):
        i = pl.program_id(1)
        prev8 = jnp.where(i > 0, xp_ref[...], 0.0)
        u = _conv_apply(x_ref[...], prev8, w_ref[...], b_ref[...], taps)
        sg = _sigmoid(u)
        o_ref[...] = d_ref[...] * (sg * (1.0 + u * (1.0 - sg)))

    cur, prev, _ = _conv_specs(tb, tc, col0, t)
    par = pl.BlockSpec((taps, tc), lambda j, i: (0, j))
    bias = pl.BlockSpec((1, tc), lambda j, i: (0, j))
    out = pl.BlockSpec((tb, tc), lambda j, i: (i, j))
    return pl.pallas_call(
        body, name=name, grid=(width // tc, t // tb), in_specs=[cur, prev, par, bias, out],
        out_specs=out, out_shape=jax.ShapeDtypeStruct((t, width), F32),
        compiler_params=_cp(("parallel", "parallel")))(x, x, w, b, dout)


def _conv_bwd(x, w, du, *, col0, width, name):
    t = x.shape[0]
    taps = w.shape[0]
    tb, tc = _rows(t, 512), _tile(width, 1024)
    nrow = t // tb

    def body(x_ref, w_ref, du_ref, dun_ref, dx_ref, dw_ref, db_ref):
        i = pl.program_id(1)
        next8 = jnp.where(i < nrow - 1, dun_ref[...], 0.0)
        dx, dwv, dbv = _conv_grads(du_ref[...], next8, x_ref[...], w_ref[...], taps)
        dx_ref[...] = dx.astype(BF16)

        @pl.when(i == 0)
        def _():
            dw_ref[...] = dwv
            db_ref[...] = dbv

        @pl.when(i > 0)
        def _():
            dw_ref[...] += dwv
            db_ref[...] += dbv

    cur, _, _ = _conv_specs(tb, tc, col0, t)
    dcur, _, dnxt = _conv_specs(tb, tc, 0, t)
    par = pl.BlockSpec((taps, tc), lambda j, i: (0, j))
    bias = pl.BlockSpec((1, tc), lambda j, i: (0, j))
    return pl.pallas_call(
        body, name=name, grid=(width // tc, nrow), in_specs=[cur, par, dcur, dnxt],
        out_specs=[dcur, par, bias],
        out_shape=[jax.ShapeDtypeStruct((t, width), BF16), jax.ShapeDtypeStruct((taps, width), F32),
                   jax.ShapeDtypeStruct((1, width), F32)],
        compiler_params=_cp(("parallel", "arbitrary")))(x, w, du, du)


def _ffn_specs(tb, tc, t):
    nc = D_FF // tc

    def cur(half):
        return pl.BlockSpec((tb, tc), lambda j, i: (i, half * nc + j))

    def prev(half):
        return pl.BlockSpec((8, tc), lambda j, i: (jnp.maximum(i * (tb // 8) - 1, 0), half * nc + j))

    def nxt(half):
        return pl.BlockSpec((8, tc), lambda j, i: (jnp.minimum((i + 1) * (tb // 8), t // 8 - 1), half * nc + j))

    def par(rows, half):
        return pl.BlockSpec((rows, tc), lambda j, i: (0, half * nc + j))

    return cur, prev, nxt, par


def _ffn_act_fwd(u0, w, b):
    t = u0.shape[0]
    tb, tc = _rows(t, 512), _tile(D_FF, 1408)
    cur, prev, _, par = _ffn_specs(tb, tc, t)

    def body(g_ref, gp_ref, v_ref, vp_ref, wg_ref, wv_ref, bg_ref, bv_ref, o_ref):
        i = pl.program_id(1)
        ug = _conv_apply(g_ref[...], jnp.where(i > 0, gp_ref[...], 0.0), wg_ref[...], bg_ref[...], FFN_CONV)
        uv = _conv_apply(v_ref[...], jnp.where(i > 0, vp_ref[...], 0.0), wv_ref[...], bv_ref[...], FFN_CONV)
        o_ref[...] = (ug * _sigmoid(ug) * uv).astype(BF16)

    return pl.pallas_call(
        body, name="ffn_act_fwd", grid=(D_FF // tc, t // tb),
        in_specs=[cur(0), prev(0), cur(1), prev(1), par(FFN_CONV, 0), par(FFN_CONV, 1), par(1, 0), par(1, 1)],
        out_specs=pl.BlockSpec((tb, tc), lambda j, i: (i, j)), out_shape=jax.ShapeDtypeStruct((t, D_FF), BF16),
        compiler_params=_cp(("parallel", "parallel")))(u0, u0, u0, u0, w, w, b, b)


def _ffn_act_bwd(u0, w, b, da):
    t = u0.shape[0]
    tb, tc = _rows(t, 256), _tile(D_FF, 1408)
    nrow = t // tb
    taps = FFN_CONV
    cur, prev, nxt, par = _ffn_specs(tb, tc, t)

    def dact(ug, uv, dav):
        sg = _sigmoid(ug)
        return dav * uv * (sg * (1.0 + ug * (1.0 - sg))), dav * ug * sg

    def body(g_ref, gp_ref, gn_ref, v_ref, vp_ref, vn_ref, wg_ref, wv_ref, bg_ref, bv_ref, da_ref, dan_ref,
             dx_ref, dw_ref, db_ref):
        i = pl.program_id(1)
        xg, xv = g_ref[...], v_ref[...]
        gp = jnp.where(i > 0, gp_ref[...], 0.0)
        vp = jnp.where(i > 0, vp_ref[...], 0.0)
        wg, wv, bg, bv = wg_ref[...], wv_ref[...], bg_ref[...], bv_ref[...]
        dug, duv = dact(_conv_apply(xg, gp, wg, bg, taps), _conv_apply(xv, vp, wv, bv, taps),
                        da_ref[...].astype(F32))
        dan = jnp.where(i < nrow - 1, dan_ref[...].astype(F32)[:8], 0.0)
        dugn, duvn = dact(_conv_apply(gn_ref[...], xg[tb - 8:], wg, bg, taps),
                          _conv_apply(vn_ref[...], xv[tb - 8:], wv, bv, taps), dan)
        dxg, dwg, dbg = _conv_grads(dug, dugn, xg, wg, taps)
        dxv, dwv, dbv = _conv_grads(duv, duvn, xv, wv, taps)
        dx_ref[0] = dxg.astype(BF16)
        dx_ref[1] = dxv.astype(BF16)

        @pl.when(i == 0)
        def _():
            dw_ref[0] = dwg
            dw_ref[1] = dwv
            db_ref[0] = dbg
            db_ref[1] = dbv

        @pl.when(i > 0)
        def _():
            dw_ref[0] += dwg
            dw_ref[1] += dwv
            db_ref[0] += dbg
            db_ref[1] += dbv

    da_cur = pl.BlockSpec((tb, tc), lambda j, i: (i, j))
    da_nxt = pl.BlockSpec((16, tc), lambda j, i: (jnp.minimum((i + 1) * (tb // 16), t // 16 - 1), j))
    return pl.pallas_call(
        body, name="ffn_act_bwd", grid=(D_FF // tc, nrow),
        in_specs=[cur(0), prev(0), nxt(0), cur(1), prev(1), nxt(1), par(taps, 0), par(taps, 1), par(1, 0),
                  par(1, 1), da_cur, da_nxt],
        out_specs=[pl.BlockSpec((2, tb, tc), lambda j, i: (0, i, j)),
                   pl.BlockSpec((2, taps, tc), lambda j, i: (0, 0, j)),
                   pl.BlockSpec((2, 1, tc), lambda j, i: (0, 0, j))],
        out_shape=[jax.ShapeDtypeStruct((2, t, D_FF), BF16), jax.ShapeDtypeStruct((2, taps, D_FF), F32),
                   jax.ShapeDtypeStruct((2, 1, D_FF), F32)],
        compiler_params=_cp(("parallel", "arbitrary")))(u0, u0, u0, u0, u0, u0, w, w, b, b, da, da)


def _head_masks():
    lane = _iota((1, 4 * SSD_HEAD_DIM), 1)
    return [((lane >= r * SSD_HEAD_DIM) & (lane < (r + 1) * SSD_HEAD_DIM)).astype(F32) for r in range(4)]


def _segsum(v):
    first = _iota((1, LANES), 1) < SSD_HEAD_DIM
    halves = []
    for k in range(2):
        vh = v[:, k * LANES:(k + 1) * LANES]
        both = jnp.sum(vh, axis=1, keepdims=True)
        one = jnp.sum(jnp.where(first, vh, 0.0), axis=1, keepdims=True)
        halves.append(jnp.where(first, one, both - one))
    return jnp.concatenate(halves, axis=1)


def _ssd_common(raw_e, prow, rawr4, bcol, acol):
    n = SSD_CHUNK
    dt_e = _softplus(raw_e + prow[0:1, :])
    a_e = -jnp.exp(prow[1:2, :])
    d_e = prow[2:3, :]
    tril = (_iota((n, n), 0) >= _iota((n, n), 1)).astype(F32)
    acs_e = _dot_exact(tril, dt_e * a_e)
    last_e = acs_e[n - 1:n, :]
    dtr4 = _softplus(rawr4 + bcol)
    triu = (_iota((n, n), 0) <= _iota((n, n), 1)).astype(F32)
    acs_r4 = _dot_exact(dtr4 * (-jnp.exp(acol)), triu)
    return dt_e, a_e, d_e, acs_e, last_e, acs_r4


def _decay_matrix(acs_e, acs_r4, r):
    n = SSD_CHUNK
    col = acs_e[:, r * SSD_HEAD_DIM:r * SSD_HEAD_DIM + 1]
    seg = col - acs_r4[r:r + 1, :]
    causal = _iota((n, n), 0) >= _iota((n, n), 1)
    return jnp.exp(jnp.where(causal, seg, NEG))


SSD_STEP_CHUNKS = 4
SSD_ROWS = SSD_STEP_CHUNKS * SSD_CHUNK


def _ssd_specs(t, rev):
    nb = t // SSD_ROWS
    xb, bb, cb = 0, SSD_INNER // SSD_STATE, (SSD_INNER + BC_WIDTH) // SSD_STATE

    def ch(c):
        return (nb - 1 - c) if rev else c

    x = pl.BlockSpec((SSD_ROWS, 256), lambda g, c: (ch(c), xb + g))
    bm = pl.BlockSpec((SSD_ROWS, SSD_STATE), lambda g, c: (ch(c), bb + g))
    cm = pl.BlockSpec((SSD_ROWS, SSD_STATE), lambda g, c: (ch(c), cb + g))
    dtc = pl.BlockSpec((1, SSD_ROWS, 256), lambda g, c: (g, ch(c), 0))
    dtr = pl.BlockSpec((1, 4, SSD_ROWS), lambda g, c: (g, 0, ch(c)))
    prow = pl.BlockSpec((1, 3, 256), lambda g, c: (g, 0, 0))
    pcol = pl.BlockSpec((1, 4, 1), lambda g, c: (g, 0, 0))
    st = pl.BlockSpec((1, SSD_STEP_CHUNKS, SSD_STATE, 256), lambda g, c: (g, ch(c), 0, 0))
    return x, bm, cm, dtc, dtr, prow, pcol, st, ch


def _ssd_params(dt_raw, dt_bias, a_log, ssd_d):
    t = dt_raw.shape[0]
    by_group = dt_raw.reshape(t, SSD_GROUPS, 4)
    dtc = jnp.repeat(by_group, SSD_HEAD_DIM, axis=2).transpose(1, 0, 2)
    dtr = by_group.transpose(1, 2, 0)
    prow = jnp.repeat(jnp.stack([dt_bias.reshape(SSD_GROUPS, 4), a_log.reshape(SSD_GROUPS, 4),
                                 ssd_d.reshape(SSD_GROUPS, 4)], axis=1), SSD_HEAD_DIM, axis=2)
    bcol = dt_bias.reshape(SSD_GROUPS, 4, 1)
    acol = a_log.reshape(SSD_GROUPS, 4, 1)
    return dtc, dtr, prow, bcol, acol


def _ssd_fwd(xbc, params):
    t = xbc.shape[0]
    nc = t // SSD_CHUNK
    dtc, dtr, prow, bcol, acol = params

    def body(x_ref, b_ref, c_ref, dtc_ref, dtr_ref, prow_ref, bcol_ref, acol_ref, y_ref, st_ref, s_scr):
        c = pl.program_id(1)

        @pl.when(c == 0)
        def _():
            s_scr[...] = jnp.zeros_like(s_scr)

        masks = _head_masks()
        s = s_scr[...]
        for k in range(SSD_STEP_CHUNKS):
            rows = slice(k * SSD_CHUNK, (k + 1) * SSD_CHUNK)
            dt_e, a_e, d_e, acs_e, last_e, acs_r4 = _ssd_common(
                dtc_ref[0, rows], prow_ref[0], dtr_ref[0][:, rows], bcol_ref[0], acol_ref[0])
            xv = x_ref[rows]
            bm, cm = b_ref[rows], c_ref[rows]
            st_ref[0, k] = s
            xdt = xv * dt_e
            cb = _dot(cm, bm, 'nt')
            y = _dot(cm, s) * jnp.exp(acs_e) + xv * d_e
            for r in range(4):
                mr = cb * _decay_matrix(acs_e, acs_r4, r)
                y = y + _dot(mr, xdt * masks[r])
            y_ref[rows] = y
            w = xdt * jnp.exp(last_e - acs_e)
            s = s * jnp.exp(last_e) + _dot(bm.T, w)
        s_scr[...] = s

    x, bm, cm, dtcs, dtrs, prs, pcs, st, _ = _ssd_specs(t, False)
    return pl.pallas_call(
        body, name="ssd_fwd", grid=(SSD_GROUPS, t // SSD_ROWS), in_specs=[x, bm, cm, dtcs, dtrs, prs, pcs, pcs],
        out_specs=[pl.BlockSpec((SSD_ROWS, 256), lambda g, c: (c, g)), st],
        out_shape=[jax.ShapeDtypeStruct((t, SSD_INNER), F32),
                   jax.ShapeDtypeStruct((SSD_GROUPS, nc, SSD_STATE, 256), F32)],
        scratch_shapes=[pltpu.VMEM((SSD_STATE, 256), F32)],
        compiler_params=_cp(("parallel", "arbitrary")))(xbc, xbc, xbc, dtc, dtr, prow, bcol, acol)


def _ssd_bwd(xbc, params, states, dy):
    t = xbc.shape[0]
    nc = t // SSD_CHUNK
    n = SSD_CHUNK
    dtc, dtr, prow, bcol, acol = params

    def body(x_ref, b_ref, c_ref, dtc_ref, dtr_ref, prow_ref, bcol_ref, acol_ref, st_ref, dy_ref,
             dx_ref, db_ref, dc_ref, ddt_ref, dp_ref, ds_scr):
        c = pl.program_id(1)

        @pl.when(c == 0)
        def _():
            ds_scr[...] = jnp.zeros_like(ds_scr)
            dp_ref[...] = jnp.zeros_like(dp_ref)

        masks = _head_masks()
        ds = ds_scr[...]
        for k in reversed(range(SSD_STEP_CHUNKS)):
            rows = slice(k * SSD_CHUNK, (k + 1) * SSD_CHUNK)
            raw_e = dtc_ref[0, rows]
            prw = prow_ref[0]
            dt_e, a_e, d_e, acs_e, last_e, acs_r4 = _ssd_common(raw_e, prw, dtr_ref[0][:, rows], bcol_ref[0], acol_ref[0])
            xv = x_ref[rows]
            bm, cm = b_ref[rows], c_ref[rows]
            s = st_ref[0, k]
            dyv = dy_ref[rows]
            e_e = jnp.exp(acs_e)
            dec_e = jnp.exp(last_e - acs_e)
            cd_e = jnp.exp(last_e)
            xdt = xv * dt_e
            w = xdt * dec_e
            b16, c16, s16, ds16 = bm.astype(BF16), cm.astype(BF16), s.astype(BF16), ds.astype(BF16)
            cb = _dot(c16, b16, 'nt')
            yoff_raw = _dot(c16, s16)
            dye = dyv * e_e
            dye16 = dye.astype(BF16)
            dcm = _dot(dye16, s16, 'nt')
            ds_prev = ds * cd_e + _dot(cm.T, dye16)
            dacs_e = _segsum(dyv * yoff_raw) * e_e
            dw = _dot(b16, ds16)
            dbm = _dot(w, ds16, 'nt')
            tdec = _segsum(dw * xdt) * dec_e
            dacs_e = dacs_e - tdec
            dlast_e = jnp.sum(tdec, axis=0, keepdims=True)
            dxdt = dw * dec_e
            dlast_e = dlast_e + _segsum(jnp.sum(ds * s, axis=0, keepdims=True)) * cd_e
            dcb = jnp.zeros((n, n), F32)
            for r in range(4):
                lm = _decay_matrix(acs_e, acs_r4, r)
                mr = cb * lm
                dyr16 = (dyv * masks[r]).astype(BF16)
                dm = _dot(dyr16, xdt * masks[r], 'nt')
                dcb = dcb + dm * lm
                dseg = dm * mr
                dcol = jnp.sum(dseg, axis=1, keepdims=True) - jnp.sum(dseg.T, axis=1, keepdims=True)
                dacs_e = dacs_e + dcol * masks[r]
                dxdt = dxdt + _dot(mr.T, dyr16)
            dcm = dcm + _dot(dcb, b16)
            dbm = dbm + _dot(dcb.T, c16)
            dacs_e = dacs_e + jnp.where(_iota((n, 1), 0) == n - 1, dlast_e, 0.0)
            triu = (_iota((n, n), 0) <= _iota((n, n), 1)).astype(F32)
            ddta_e = _dot_exact(triu, dacs_e)
            ddt_e = ddta_e * a_e + _segsum(dxdt * xv)
            dx_ref[rows] = dxdt * dt_e + dyv * d_e
            db_ref[rows] = dbm
            dc_ref[rows] = dcm
            draw_e = ddt_e * _sigmoid(raw_e + prw[0:1, :])
            draw_t = draw_e.T
            ddt_ref[0, :, rows] = jnp.concatenate([draw_t[r * SSD_HEAD_DIM:r * SSD_HEAD_DIM + 1] for r in range(4)], axis=0)
            dbias = jnp.sum(draw_e, axis=0, keepdims=True)
            dalog = jnp.sum(ddta_e * dt_e, axis=0, keepdims=True) * a_e
            dd = _segsum(jnp.sum(dyv * xv, axis=0, keepdims=True))
            row3 = _iota((3, 1), 0)
            dp_ref[0] += (jnp.where(row3 == 0, dbias, 0.0) + jnp.where(row3 == 1, dalog, 0.0)
                          + jnp.where(row3 == 2, dd, 0.0))
            ds = ds_prev
        ds_scr[...] = ds


    x, bm, cm, dtcs, dtrs, prs, pcs, st, ch = _ssd_specs(t, True)
    yblk = pl.BlockSpec((SSD_ROWS, 256), lambda g, c: (ch(c), g))
    nblk = pl.BlockSpec((SSD_ROWS, SSD_STATE), lambda g, c: (ch(c), g))
    return pl.pallas_call(
        body, name="ssd_bwd", grid=(SSD_GROUPS, t // SSD_ROWS),
        in_specs=[x, bm, cm, dtcs, dtrs, prs, pcs, pcs, st, yblk],
        out_specs=[yblk, nblk, nblk, dtrs, prs],
        out_shape=[jax.ShapeDtypeStruct((t, SSD_INNER), F32), jax.ShapeDtypeStruct((t, BC_WIDTH), F32),
                   jax.ShapeDtypeStruct((t, BC_WIDTH), F32), jax.ShapeDtypeStruct((SSD_GROUPS, 4, t), F32),
                   jax.ShapeDtypeStruct((SSD_GROUPS, 3, 256), F32)],
        scratch_shapes=[pltpu.VMEM((SSD_STATE, 256), F32)],
        compiler_params=_cp(("parallel", "arbitrary")))(xbc, xbc, xbc, dtc, dtr, prow, bcol, acol, states, dy)


GROUP_W = SSD_INNER // SSD_GROUPS


def _mix_specs(tb):
    row = pl.BlockSpec((tb, 2048), lambda i: (i, 0))
    zlo = pl.BlockSpec((tb, 1024), lambda i: (i, O_Z // 1024))
    zhi = pl.BlockSpec((tb, 1024), lambda i: (i, O_Z // 1024 + 1))
    vec = pl.BlockSpec((1, 2048), lambda i: (0, 0))
    return row, zlo, zhi, vec


def _mix_fwd(attn, y, proj, g_attn, g_ssd):
    t = attn.shape[0]
    tb = _rows(t, 256)

    def body(a_ref, y_ref, zlo_ref, zhi_ref, ga_ref, gs_ref, o_ref):
        av = a_ref[...]
        r = lax.rsqrt(jnp.mean(av * av, axis=-1, keepdims=True) + EPS)
        o_ref[:, :ATTN_WIDTH] = (av * r * ga_ref[...]).astype(BF16)
        for g in range(SSD_GROUPS):
            lo, hi = g * GROUP_W, (g + 1) * GROUP_W
            zref = zlo_ref if g < 4 else zhi_ref
            z = zref[:, lo % 1024:lo % 1024 + GROUP_W]
            yg = y_ref[:, lo:hi] * (z * _sigmoid(z))
            rg = lax.rsqrt(jnp.mean(yg * yg, axis=-1, keepdims=True) + EPS)
            o_ref[:, ATTN_WIDTH + lo:ATTN_WIDTH + hi] = (yg * rg * gs_ref[:, lo:hi]).astype(BF16)

    row, zlo, zhi, vec = _mix_specs(tb)
    return pl.pallas_call(
        body, name="mix_fwd", grid=(t // tb,), in_specs=[row, row, zlo, zhi, vec, vec],
        out_specs=pl.BlockSpec((tb, 4096), lambda i: (i, 0)), out_shape=jax.ShapeDtypeStruct((t, 4096), BF16),
        compiler_params=_cp(("parallel",)))(attn, y, proj, proj, g_attn, g_ssd)


def _mix_bwd(dmix, attn, y, proj, g_attn, g_ssd):
    t = attn.shape[0]
    tb = _rows(t, 256)

    def body(dm_ref, a_ref, y_ref, zlo_ref, zhi_ref, ga_ref, gs_ref, da_ref, dy_ref, dz_ref, dga_ref, dgs_ref):
        i = pl.program_id(0)
        av = a_ref[...]
        dn = dm_ref[:, :ATTN_WIDTH].astype(F32)
        r = lax.rsqrt(jnp.mean(av * av, axis=-1, keepdims=True) + EPS)
        u = dn * ga_ref[...]
        da_ref[...] = r * u - av * (r * r * r * jnp.mean(u * av, axis=-1, keepdims=True))
        dga = jnp.sum(dn * av * r, axis=0, keepdims=True)

        @pl.when(i == 0)
        def _():
            dga_ref[...] = dga

        @pl.when(i > 0)
        def _():
            dga_ref[...] += dga

        for g in range(SSD_GROUPS):
            lo, hi = g * GROUP_W, (g + 1) * GROUP_W
            zref = zlo_ref if g < 4 else zhi_ref
            z = zref[:, lo % 1024:lo % 1024 + GROUP_W]
            yv = y_ref[:, lo:hi]
            sg = _sigmoid(z)
            sz = z * sg
            yg = yv * sz
            rg = lax.rsqrt(jnp.mean(yg * yg, axis=-1, keepdims=True) + EPS)
            do = dm_ref[:, ATTN_WIDTH + lo:ATTN_WIDTH + hi].astype(F32)
            ug = do * gs_ref[:, lo:hi]
            dyg = rg * ug - yg * (rg * rg * rg * jnp.mean(ug * yg, axis=-1, keepdims=True))
            dy_ref[:, lo:hi] = dyg * sz
            dz_ref[:, lo:hi] = (dyg * yv * (sg * (1.0 + z * (1.0 - sg)))).astype(BF16)
            dgs = jnp.sum(do * yg * rg, axis=0, keepdims=True)

            @pl.when(i == 0)
            def _():
                dgs_ref[:, lo:hi] = dgs

            @pl.when(i > 0)
            def _():
                dgs_ref[:, lo:hi] += dgs

    row, zlo, zhi, vec = _mix_specs(tb)
    return pl.pallas_call(
        body, name="mix_bwd", grid=(t // tb,),
        in_specs=[pl.BlockSpec((tb, 4096), lambda i: (i, 0)), row, row, zlo, zhi, vec, vec],
        out_specs=[row, row, row, vec, vec],
        out_shape=[jax.ShapeDtypeStruct((t, 2048), F32), jax.ShapeDtypeStruct((t, 2048), F32),
                   jax.ShapeDtypeStruct((t, 2048), BF16), jax.ShapeDtypeStruct((1, 2048), F32),
                   jax.ShapeDtypeStruct((1, 2048), F32)],
        compiler_params=_cp(("arbitrary",)))(dmix, attn, y, proj, proj, g_attn, g_ssd)


def _adamw(w, g, m, v, name):
    r, c = w.shape
    tb = _rows(r, 256)
    c1 = 1.0 - ADAM_B1 ** ADAM_STEP
    c2 = 1.0 - ADAM_B2 ** ADAM_STEP

    def body(w_ref, g_ref, m_ref, v_ref, d_ref, m2_ref, v2_ref):
        gv = g_ref[...]
        m2 = ADAM_B1 * m_ref[...] + (1.0 - ADAM_B1) * gv
        v2 = ADAM_B2 * v_ref[...] + (1.0 - ADAM_B2) * (gv * gv)
        d_ref[...] = -ADAM_LR * ((m2 / c1) / (jnp.sqrt(v2 / c2) + ADAM_EPS) + ADAM_WD * w_ref[...])
        m2_ref[...] = m2
        v2_ref[...] = v2

    blk = pl.BlockSpec((tb, c), lambda i: (i, 0))
    shp = jax.ShapeDtypeStruct((r, c), F32)
    return pl.pallas_call(body, name=name, grid=(r // tb,), in_specs=[blk] * 4, out_specs=[blk] * 3,
                          out_shape=[shp] * 3, compiler_params=_cp(("parallel",)))(w, g, m, v)


def _adamw_halves(w, mine, theirs, m, v, pos, name, cols=False):
    r, c = w.shape
    h = r if cols else r // 2
    tb = _rows(h, 128)
    nh = h // tb
    c1 = 1.0 - ADAM_B1 ** ADAM_STEP
    c2 = 1.0 - ADAM_B2 ** ADAM_STEP

    def body(pos_ref, w_ref, a_ref, b_ref, m_ref, v_ref, g_ref, d_ref, m2_ref, v2_ref):
        which = pl.program_id(1) if cols else pl.program_id(0) // nh
        gv = jnp.where(which == pos_ref[0], a_ref[...], b_ref[...])
        m2 = ADAM_B1 * m_ref[...] + (1.0 - ADAM_B1) * gv
        v2 = ADAM_B2 * v_ref[...] + (1.0 - ADAM_B2) * (gv * gv)
        g_ref[...] = gv
        d_ref[...] = -ADAM_LR * ((m2 / c1) / (jnp.sqrt(v2 / c2) + ADAM_EPS) + ADAM_WD * w_ref[...])
        m2_ref[...] = m2
        v2_ref[...] = v2

    if cols:
        full = pl.BlockSpec((tb, c // 2), lambda i, j, pref: (i, j))
        mine_spec = theirs_spec = pl.BlockSpec((tb, c // 2), lambda i, j, pref: (i, 0))
        grid = (nh, 2)
    else:
        full = pl.BlockSpec((tb, c), lambda i, pref: (i, 0))
        mine_spec = pl.BlockSpec((tb, c), lambda i, pref: (jnp.where(i // nh == pref[0], i % nh,
                                                                     jnp.where(pref[0] == 0, nh - 1, 0)), 0))
        theirs_spec = pl.BlockSpec((tb, c), lambda i, pref: (jnp.where(i // nh != pref[0], i % nh,
                                                                       jnp.where(pref[0] == 0, 0, nh - 1)), 0))
        grid = (r // tb,)
    shp = jax.ShapeDtypeStruct((r, c), F32)
    grid_spec = pltpu.PrefetchScalarGridSpec(num_scalar_prefetch=1, grid=grid,
                                             in_specs=[full, mine_spec, theirs_spec, full, full],
                                             out_specs=[full] * 4)
    return pl.pallas_call(body, name=name, grid_spec=grid_spec, out_shape=[shp] * 4,
                          compiler_params=_cp(("parallel",) * len(grid)))(pos, w, mine, theirs, m, v)


def _sum_own_half(g4, recv, pos, name, cols=False):
    _, r, c = g4.shape
    h, c = (r, c // 2) if cols else (r // 2, c)
    tb = _rows(h, 128)
    nh = h // tb

    def slot(j, pref):
        return (pref[1] + 1 + j) % N_CHIPS

    if cols:
        own = lambda j, i, pref: (slot(j, pref), i, pref[0])
    else:
        own = lambda j, i, pref: (slot(j, pref), pref[0] * nh + i, 0)
    same = lambda j, i, pref: (slot(j, pref), i, 0)

    def body(pos_ref, a_ref, b_ref, o_ref):
        o_ref[...] = (a_ref[...] + b_ref[...]).astype(BF16)

    grid_spec = pltpu.PrefetchScalarGridSpec(
        num_scalar_prefetch=1, grid=(N_CHIPS - 1, nh),
        in_specs=[pl.BlockSpec((1, tb, c), own), pl.BlockSpec((1, tb, c), same)],
        out_specs=pl.BlockSpec((1, tb, c), same))
    return pl.pallas_call(body, name=name, grid_spec=grid_spec,
                          out_shape=jax.ShapeDtypeStruct((N_CHIPS, h, c), BF16),
                          compiler_params=_cp(("parallel", "parallel")))(pos, g4, recv)


def _sum_chips(g4, recv, parts, pos, name, cols=False):
    _, r, c = g4.shape
    h, c = (r, c // 2) if cols else (r // 2, c)
    tb = _rows(h, 128)
    nh = h // tb
    own = (lambda i, pref: (pref[1], i, pref[0])) if cols else (lambda i, pref: (pref[1], pref[0] * nh + i, 0))

    def body(pos_ref, a_ref, b_ref, p_ref, o_ref):
        own = a_ref[0] + b_ref[0]
        o_ref[...] = ((own + p_ref[0].astype(F32)) + p_ref[1].astype(F32)) + p_ref[2].astype(F32)

    grid_spec = pltpu.PrefetchScalarGridSpec(
        num_scalar_prefetch=1, grid=(nh,),
        in_specs=[pl.BlockSpec((1, tb, c), own),
                  pl.BlockSpec((1, tb, c), lambda i, pref: (pref[1], i, 0)),
                  pl.BlockSpec((3, tb, c), lambda i, pref: (0, i, 0))],
        out_specs=pl.BlockSpec((tb, c), lambda i, pref: (i, 0)))
    return pl.pallas_call(body, name=name, grid_spec=grid_spec, out_shape=jax.ShapeDtypeStruct((h, c), F32),
                          compiler_params=_cp(("parallel",)))(pos, g4, recv, parts)


def _me():
    return lax.axis_index("x"), lax.axis_index("y"), lax.axis_index("c")


def _flip(v, bit):
    return (1 - v) if bit else v


CHIP_FLIPS = [(1, 0), (0, 1), (1, 1)]


def _allgather_weights(shards, after, cols=False):
    n = len(shards)

    def body(*refs):
        ins, outs, token = refs[:n], refs[n + 1:2 * n + 1], refs[2 * n + 1]
        send_sems, recv_sems = refs[2 * n + 2:]
        x, y, c = _me()
        chip = 2 * x + y
        sib = (x, y, 1 - c)

        def remote(src, dst, k, to):
            return pltpu.make_async_remote_copy(src_ref=src, dst_ref=dst, send_sem=send_sems.at[k],
                                                recv_sem=recv_sems.at[k], device_id=to, device_id_type=MESH)

        def half(ref, which):
            if cols:
                h = ref.shape[1] // 2
                return ref.at[:, pl.ds(which * h, h)]
            h = ref.shape[0] // 2
            return ref.at[pl.ds(which * h, h)]

        sends = []
        for t in range(n):
            for k, (fx, fy) in enumerate(CHIP_FLIPS):
                cp = remote(half(ins[t], c), half(outs[t].at[chip], c), 6 * t + k, (_flip(x, fx), _flip(y, fy), c))
                cp.start()
                sends.append(cp)
        for t in range(n):
            for k, (fx, fy) in enumerate(CHIP_FLIPS):
                landed = half(outs[t].at[2 * _flip(x, fx) + _flip(y, fy)], c)
                remote(landed, landed, 6 * t + k, (x, y, c)).wait_recv()
                fw = remote(landed, landed, 6 * t + 3 + k, sib)
                fw.start()
                sends.append(fw)
        for t in range(n):
            for k, (fx, fy) in enumerate(CHIP_FLIPS):
                got = half(outs[t].at[2 * _flip(x, fx) + _flip(y, fy)], 1 - c)
                remote(got, got, 6 * t + 3 + k, (x, y, c)).wait_recv()
        for cp in sends:
            cp.wait_send()
        token[...] = jnp.zeros_like(token)

    outs = pl.pallas_call(
        body, name="allgather_weights", in_specs=[HBM_SPEC] * n + [pl.BlockSpec(memory_space=pl.ANY)],
        out_specs=[HBM_SPEC] * n + [pl.BlockSpec(memory_space=pltpu.VMEM)],
        out_shape=[jax.ShapeDtypeStruct((N_CHIPS,) + s.shape, s.dtype) for s in shards] + [TOKEN],
        scratch_shapes=[pltpu.SemaphoreType.DMA((6 * n,)), pltpu.SemaphoreType.DMA((6 * n,))],
        compiler_params=pltpu.CompilerParams(has_side_effects=True))(*shards, after)
    return list(outs[:n]), outs[n]


def _share_halves(ghs, name):
    n = len(ghs)

    def body(*refs):
        ins, outs = refs[:n], refs[n:2 * n]
        send_sems, recv_sems = refs[2 * n:]
        x, y, c = _me()
        cps = []
        for t in range(n):
            cp = pltpu.make_async_remote_copy(
                src_ref=ins[t], dst_ref=outs[t], send_sem=send_sems.at[t], recv_sem=recv_sems.at[t],
                device_id=(x, y, 1 - c), device_id_type=MESH)
            cp.start()
            cps.append(cp)
        for cp in cps:
            cp.wait()

    return pl.pallas_call(
        body, name=name, in_specs=[HBM_SPEC] * n, out_specs=[HBM_SPEC] * n,
        out_shape=[jax.ShapeDtypeStruct(g.shape, g.dtype) for g in ghs],
        scratch_shapes=[pltpu.SemaphoreType.DMA((n,)), pltpu.SemaphoreType.DMA((n,))],
        compiler_params=pltpu.CompilerParams(has_side_effects=True))(*ghs)


SEM_SPEC = pl.BlockSpec(memory_space=pltpu.SEMAPHORE)
ANY_SPEC = pl.BlockSpec(memory_space=pl.ANY)
DATAFLOW = pltpu.SideEffectType.DATAFLOW_SIDE_EFFECTING


def _in_hbm(a):
    return pltpu.with_memory_space_constraint(a, pltpu.HBM)


def _push_start(srcs, land_shapes, route, peers, name):
    n, npeer = len(srcs), len(peers)
    lands = [lax.empty(shp, s.dtype) for shp, s in zip(land_shapes, srcs)]

    def body(*refs):
        ins, lnd = refs[:n], refs[n:2 * n]
        send_sems, recv_sems = refs[2 * n], refs[2 * n + 1]
        token = refs[-1]
        x, y, c = _me()
        for t in range(n):
            for k, (fx, fy, fc) in enumerate(peers):
                src, dst = route(ins[t], lnd[t], k, x, y, c)
                pltpu.make_async_remote_copy(
                    src_ref=src, dst_ref=dst, send_sem=send_sems.at[npeer * t + k],
                    recv_sem=recv_sems.at[npeer * t + k],
                    device_id=(_flip(x, fx), _flip(y, fy), _flip(c, fc)), device_id_type=MESH).start()
        token[...] = jnp.zeros_like(token)

    bufs = [_in_hbm(a) for a in list(srcs) + lands]
    outs = pl.pallas_call(
        body, name=name,
        out_shape=(pltpu.SemaphoreType.DMA((npeer * n,)), pltpu.SemaphoreType.DMA((npeer * n,)),
                   *[pltpu.HBM(b.shape, b.dtype) for b in bufs], TOKEN),
        in_specs=[HBM_SPEC] * (2 * n),
        out_specs=(SEM_SPEC, SEM_SPEC, *[HBM_SPEC] * (2 * n), pl.BlockSpec(memory_space=pltpu.VMEM)),
        input_output_aliases={i: 2 + i for i in range(2 * n)},
        compiler_params=pltpu.CompilerParams(has_side_effects=DATAFLOW))(*bufs)
    return outs[0], outs[1], list(outs[2:2 + n]), list(outs[2 + n:2 + 2 * n]), outs[-1]


def _push_wait(send_sems, recv_sems, srcs, lands, after, route, peers, name):
    n, npeer = len(srcs), len(peers)

    def body(*refs):
        ins, lnd = refs[:n], refs[n:2 * n]
        ssem, rsem = refs[2 * n], refs[2 * n + 1]
        x, y, c = _me()
        for t in range(n):
            for k, (fx, fy, fc) in enumerate(peers):
                src, dst = route(ins[t], lnd[t], k, x, y, c)
                cp = pltpu.make_async_remote_copy(
                    src_ref=src, dst_ref=dst, send_sem=ssem.at[npeer * t + k], recv_sem=rsem.at[npeer * t + k],
                    device_id=(_flip(x, fx), _flip(y, fy), _flip(c, fc)), device_id_type=MESH)
                cp.wait_send()
                cp.wait_recv()

    bufs = list(srcs) + list(lands)
    outs = pl.pallas_call(
        body, name=name, out_shape=tuple(pltpu.HBM(b.shape, b.dtype) for b in bufs),
        in_specs=[HBM_SPEC] * (2 * n) + [SEM_SPEC, SEM_SPEC, ANY_SPEC], out_specs=tuple([HBM_SPEC] * (2 * n)),
        input_output_aliases={i: i for i in range(2 * n)},
        compiler_params=pltpu.CompilerParams(has_side_effects=DATAFLOW))(*bufs, send_sems, recv_sems, after)
    return list(outs[:n]), list(outs[n:])


OTHER_CHIPS = [(fx, fy, 0) for fx, fy in CHIP_FLIPS]
SIBLING = [(0, 0, 1)]


def _route_gather(src, land, k, x, y, c):
    return src, land.at[2 * x + y]


def _route_gather_wait(src, land, k, x, y, c):
    fx, fy = CHIP_FLIPS[k]
    return src, land.at[2 * _flip(x, fx) + _flip(y, fy)]


def _route_scatter(src, land, k, x, y, c):
    fx, fy = CHIP_FLIPS[k]
    return src.at[2 * _flip(x, fx) + _flip(y, fy)], land.at[k]


def _route_exchange(src, land, k, x, y, c):
    h = land.shape[1]
    return src.at[:, pl.ds((1 - c) * h, h)], land


def _route_exchange_cols(src, land, k, x, y, c):
    h = land.shape[2]
    return src.at[:, :, pl.ds((1 - c) * h, h)], land


def _allreduce_small(v):
    r = v.shape[0]

    def body(v_ref, o_ref, buf, send_sems, recv_sems):
        x, y, c = _me()
        me = 4 * x + 2 * y + c
        buf[0] = v_ref[...]
        cps = []
        for k in range(1, 8):
            kx, ky, kc = (k >> 2) & 1, (k >> 1) & 1, k & 1
            cp = pltpu.make_async_remote_copy(
                src_ref=v_ref, dst_ref=buf.at[k], send_sem=send_sems.at[k - 1], recv_sem=recv_sems.at[k - 1],
                device_id=(_flip(x, kx), _flip(y, ky), _flip(c, kc)), device_id_type=MESH)
            cp.start()
            cps.append(cp)
        for cp in cps:
            cp.wait()
        acc = buf[me]
        for d in range(1, 8):
            acc = acc + buf[jnp.bitwise_xor(me, d)]
        o_ref[...] = acc

    vm = pl.BlockSpec(memory_space=pltpu.VMEM)
    return pl.pallas_call(
        body, name="allreduce_small", in_specs=[vm], out_specs=vm, out_shape=jax.ShapeDtypeStruct(v.shape, F32),
        scratch_shapes=[pltpu.VMEM((8, r, LANES), F32), pltpu.SemaphoreType.DMA((7,)),
                        pltpu.SemaphoreType.DMA((7,))],
        compiler_params=pltpu.CompilerParams(has_side_effects=True, vmem_limit_bytes=VMEM_LIMIT))(v)


def _grad_exchange_start(g4, tag, cols=False):
    land = (N_CHIPS, g4.shape[1], g4.shape[2] // 2) if cols else (N_CHIPS, g4.shape[1] // 2, g4.shape[2])
    route = _route_exchange_cols if cols else _route_exchange
    send_sems, recv_sems, srcs, lands, token = _push_start(
        [g4], [land], route, SIBLING, name="grad_exchange_start_" + tag)
    return (send_sems, recv_sems, srcs, lands, tag, cols), token


def _grad_scatter_start(state, pos, after):
    send_sems, recv_sems, srcs, lands, tag, cols = state
    route = _route_exchange_cols if cols else _route_exchange
    (g4,), (recv,) = _push_wait(send_sems, recv_sems, srcs, lands, after, route, SIBLING,
                                name="grad_exchange_wait_" + tag)
    return _grad_pair_scatter(g4, recv, pos, tag, cols)


def _grad_pair_scatter(g4, recv, pos, tag, cols=False):
    p16 = _sum_own_half(g4, recv, pos, name="grad_sum_pair_" + tag, cols=cols)
    send_sems, recv_sems, srcs, lands, token = _push_start(
        [p16], [(3,) + p16.shape[1:]], _route_scatter, OTHER_CHIPS, name="grad_scatter_start_" + tag)
    return (g4, recv, send_sems, recv_sems, srcs, lands, tag, cols), token


def _grad_reduce_finish(state, pos, after):
    g4, recv, send_sems, recv_sems, srcs, lands, tag, cols = state
    parts = _push_wait(send_sems, recv_sems, srcs, lands, after, _route_scatter, OTHER_CHIPS,
                       name="grad_scatter_wait_" + tag)[1][0]
    mine = _sum_chips(g4, recv, parts, pos, name="grad_sum_chips_" + tag, cols=cols)
    return mine, _share_halves([mine], name="grad_share_halves_" + tag)[0]


def _local_step(x, tgt, p, w_in_t, w_in_dt, hooks):
    t = x.shape[0]
    tables = _rope_tables(t)
    sinks = p['sinks'].reshape(N_Q_HEADS)

    def told(name, value):
        return tuple(hooks.grad_ready(name, value))

    xn = _rmsnorm_fwd(x, p['norm_mix'], "norm_mix_fwd", deps=hooks.first_deps)
    proj = _matmul(xn, w_in_t, mode='nt', name="in_proj", n_limit=MAIN_WIDTH)
    dt_raw = _matmul(xn, w_in_dt, mode='nt', name="in_proj_dt")[:, :SSD_HEADS]
    attn = _attn_fwd(proj, sinks, tables)
    conv_b = p['ssd_conv_b']
    xbc = _conv_fwd(proj, p['ssd_conv_w'], conv_b, col0=O_XBC, width=CONV_CH, act=True, name="ssd_conv_fwd")
    sp = _ssd_params(dt_raw, p['dt_bias'].reshape(-1), p['a_log'].reshape(-1), p['ssd_d'].reshape(-1))
    y, states = _ssd_fwd(xbc, sp)
    mix = _mix_fwd(attn, y, proj, p['attn_out_norm'], p['ssd_norm'])
    w_out = hooks.weight('w_out', mix)
    h1 = _matmul(mix, w_out, mode='nn', name="out_proj", add=x)
    hn = _rmsnorm_fwd(h1, p['norm_ffn'], "norm_ffn_fwd")
    w_up = hooks.weight('w_up', hn)
    u0 = _matmul(hn, w_up, mode='nn', name="ffn_up", b_owner=True, tn=1408)
    a = _ffn_act_fwd(u0, p['ffn_conv_w'], p['ffn_conv_b'])
    w_down = hooks.weight('w_down', a)
    h2 = _matmul(a, w_down, mode='nn', name="ffn_down", add=h1, tk=2816)
    loss, dh2, dh2_16, g_norm_final = _final_loss(h2, p['norm_final'].reshape(1, D_MODEL), tgt)

    g = {}
    da = _matmul(dh2_16, w_down, mode='nt', name="ffn_down_dx", out_dtype=BF16, tn=1408)
    g['w_down'] = _matmul(a, dh2_16, mode='tn', name="ffn_down_dw", tm=1408)
    dep = told('w_down', g['w_down'])
    du0, dcw, dcb = _ffn_act_bwd(u0, p['ffn_conv_w'], p['ffn_conv_b'], da)
    g['ffn_conv_w'] = dcw.transpose(1, 0, 2).reshape(FFN_CONV, 2 * D_FF)
    g['ffn_conv_b'] = dcb.transpose(1, 0, 2).reshape(1, 2 * D_FF)
    g['w_up'] = _matmul(hn, du0, mode='tn', name="ffn_up_dw", deps=dep, b_halves=True, owner_major=True,
                        tn=1408)
    dep = told('w_up', g['w_up'])
    dhn = _matmul(du0, w_up, mode='nt', name="ffn_up_dx", out_dtype=BF16, deps=dep, a_halves=True,
                  b_owner=True, tk=2816)
    dh1, dh1_16, g['norm_ffn'] = _rmsnorm_bwd(h1, p['norm_ffn'], dhn, dh2, "norm_ffn_bwd")

    g['w_out'] = _matmul(mix, dh1_16, mode='tn', name="out_proj_dw")
    dep = told('w_out', g['w_out'])
    dmix = _matmul(dh1_16, w_out, mode='nt', name="out_proj_dx", out_dtype=BF16, deps=dep)
    dattn, dy, dz, g['attn_out_norm'], g['ssd_norm'] = _mix_bwd(dmix, attn, y, proj, p['attn_out_norm'],
                                                                p['ssd_norm'])
    dq, dk, dv, dsink = _attn_bwd(proj, sinks, tables, dattn)
    g['sinks'] = dsink[:, :, 0].reshape(1, N_Q_HEADS)
    dxs, dbm, dcm, ddt8, dpar = _ssd_bwd(xbc, sp, states, dy)
    dpar = dpar[:, :, ::SSD_HEAD_DIM]
    g['dt_bias'] = dpar[:, 0, :].reshape(1, SSD_HEADS)
    g['a_log'] = dpar[:, 1, :].reshape(1, SSD_HEADS)
    g['ssd_d'] = dpar[:, 2, :].reshape(1, SSD_HEADS)
    dxbc_act = jnp.concatenate([dxs, dbm, dcm], axis=1)
    dconv = _conv_silu_dact(proj, p['ssd_conv_w'], conv_b, dxbc_act, col0=O_XBC, width=CONV_CH,
                            name="ssd_conv_dact")
    dxbc, g['ssd_conv_w'], g['ssd_conv_b'] = _conv_bwd(proj, p['ssd_conv_w'], dconv, col0=O_XBC, width=CONV_CH,
                                                       name="ssd_conv_bwd")
    dproj = jnp.concatenate([dq, dk, dv, dz, dxbc], axis=1)
    ddt = ddt8.transpose(2, 0, 1).reshape(t, SSD_HEADS)
    ddt_pad = jnp.pad(ddt, ((0, 0), (0, LANES - SSD_HEADS))).astype(BF16)
    g['w_in'] = (_matmul(dproj, xn, mode='tn', name="in_proj_dw", m_rows=IN_PROJ_WIDTH),
                 _matmul(ddt_pad, xn, mode='tn', name="in_proj_dt_dw"))
    dep = told('w_in', g['w_in'])
    dxn_dt = _matmul(ddt_pad, w_in_dt, mode='nn', name="in_proj_dt_dx", deps=dep)
    dxn = _matmul(dproj, w_in_t, mode='nn', name="in_proj_dx", out_dtype=BF16, add=dxn_dt, k_limit=MAIN_WIDTH,
                  tk=2304)
    dx, _, g['norm_mix'] = _rmsnorm_bwd(x, p['norm_mix'], dxn, dh1, "norm_mix_bwd")
    g['norm_final'] = g_norm_final
    return loss, dx, g


def _pack(arrs):
    flat = jnp.concatenate([a.reshape(-1) for a in arrs])
    n = flat.shape[0]
    rows = -(-n // LANES)
    rows = -(-rows // 8) * 8
    return jnp.pad(flat, (0, rows * LANES - n)).reshape(rows, LANES)


def _unpack(packed, shapes):
    flat = packed.reshape(-1)
    out, off = [], 0
    for s in shapes:
        n = 1
        for d in s:
            n *= d
        out.append(flat[off:off + n].reshape(s))
        off += n
    return out


class _StepHooks:
    def __init__(self, first_deps, weight, grad_ready):
        self.first_deps = first_deps
        self.weight = weight
        self.grad_ready = grad_ready


def kernel(x, norm_mix, w_in, sinks, attn_out_norm, ssd_conv_w, ssd_conv_b, dt_bias, a_log, ssd_d, ssd_norm, w_out, norm_ffn, w_up, ffn_conv_w, ffn_conv_b, w_down, norm_final, loss_target, m_norm_mix, m_w_in, m_sinks, m_attn_out_norm, m_ssd_conv_w, m_ssd_conv_b, m_dt_bias, m_a_log, m_ssd_d, m_ssd_norm, m_w_out, m_norm_ffn, m_w_up, m_ffn_conv_w, m_ffn_conv_b, m_w_down, m_norm_final, v_norm_mix, v_w_in, v_sinks, v_attn_out_norm, v_ssd_conv_w, v_ssd_conv_b, v_dt_bias, v_a_log, v_ssd_d, v_ssd_norm, v_w_out, v_norm_ffn, v_w_up, v_ffn_conv_w, v_ffn_conv_b, v_w_down, v_norm_final):
    args = dict(locals())
    w = {n: args[n] for n in WEIGHTS}
    m = {n: args['m_' + n] for n in WEIGHTS}
    v = {n: args['v_' + n] for n in WEIGHTS}
    xi, yi, ci = _me()
    chip = 2 * xi + yi
    pos = jnp.stack([ci, chip]).astype(jnp.int32)

    def place(shard, full_cols):
        z = jnp.zeros((shard.shape[0], full_cols), F32)
        return lax.dynamic_update_slice(z, shard * 0.5, (0, chip * shard.shape[1]))

    conv_pack = _pack([place(ssd_conv_w[0], CONV_CH), place(ffn_conv_w[0], 2 * D_FF)])
    conv_full = _allreduce_small(conv_pack)
    ssd_conv_w_full, ffn_conv_w_full = _unpack(conv_full, [(SSD_CONV, CONV_CH), (FFN_CONV, 2 * D_FF)])

    w_in_t, m_in_t, v_in_t = (jnp.transpose(a[0]) for a in (w_in, m_w_in, v_w_in))
    in_shard = w_in_t.astype(BF16)
    (gathered,), order = _allgather_weights([in_shard], conv_full, cols=True)
    full_in_t = lax.dynamic_update_slice(gathered, in_shard[None], (chip, 0, 0)).reshape(IN_PROJ_WIDTH, D_MODEL)
    w_in_dt = jnp.pad(full_in_t[MAIN_WIDTH:], ((0, LANES - SSD_HEADS), (0, 0)))
    gathers = {}
    order = order[:1, :1]
    for n, shard in (('w_out', w_out[0]), ('w_up', w_up[0]), ('w_down', w_down[0])):
        shard = (shard + order).astype(BF16)
        gathers[n] = _push_start([shard], [(N_CHIPS,) + shard.shape], _route_gather, OTHER_CHIPS,
                                 name="gather_start_" + n)
        order = gathers[n][4][:1, :1]
    first_deps = [gathers['w_down'][4]]

    def weight(name, after):
        send_sems, recv_sems, srcs, lands, _ = gathers[name]
        (own,), (got,) = _push_wait(send_sems, recv_sems, srcs, lands, after, _route_gather_wait, OTHER_CHIPS,
                                    name="gather_wait_" + name)
        whole = lax.dynamic_update_slice(got, own[None], (chip, 0, 0))
        return whole if name == 'w_up' else whole.reshape(-1, D_MODEL)

    reductions, exchanging = {}, {}

    def grad_ready(name, value):
        if name == 'w_in':
            main, dtp = value
            value = lax.dynamic_update_slice(main, dtp[:SSD_HEADS], (MAIN_WIDTH, 0))
        g4 = value if value.ndim == 3 else value.reshape(N_CHIPS, -1, value.shape[1])
        tokens = []
        for prev in list(exchanging):
            reductions[prev], token = _grad_scatter_start(exchanging.pop(prev), pos, g4)
            tokens.append(token)
        exchanging[name], token = _grad_exchange_start(g4, name, cols=(name == 'w_in'))
        return tokens + [token]

    small = {
        'norm_mix': norm_mix, 'sinks': sinks, 'attn_out_norm': attn_out_norm, 'ssd_conv_w': ssd_conv_w_full,
        'ssd_conv_b': ssd_conv_b, 'dt_bias': dt_bias, 'a_log': a_log, 'ssd_d': ssd_d, 'ssd_norm': ssd_norm,
        'norm_ffn': norm_ffn, 'ffn_conv_w': ffn_conv_w_full, 'ffn_conv_b': ffn_conv_b, 'norm_final': norm_final,
    }
    loss, dx, g = _local_step(x[0], loss_target[0], small, full_in_t, w_in_dt,
                              _StepHooks(tuple(first_deps), weight, grad_ready))
    for n in list(exchanging):
        reductions[n], _ = _grad_scatter_start(exchanging.pop(n), pos, dx)
    gbig = {n: _grad_reduce_finish(reductions[n], pos, dx) for n in ('w_down', 'w_up', 'w_out', 'w_in')}

    small_names = [n for n in WEIGHTS if n not in BIG]
    small_g = [loss[:, :1]] + [g[n] for n in small_names]
    small_shapes = [(1, 1)] + [tuple(a.shape) for a in small_g[1:]]
    red = _unpack(_allreduce_small(_pack(small_g)), small_shapes)
    loss_out = red[0].reshape(())
    gsm = dict(zip(small_names, red[1:]))
    gsm['ssd_conv_w'] = lax.dynamic_slice(gsm['ssd_conv_w'], (0, chip * ssd_conv_w.shape[2]),
                                          (SSD_CONV, ssd_conv_w.shape[2]))
    gsm['ffn_conv_w'] = lax.dynamic_slice(gsm['ffn_conv_w'], (0, chip * ffn_conv_w.shape[2]),
                                          (FFN_CONV, ffn_conv_w.shape[2]))

    grads, deltas, new_m, new_v = {}, {}, {}, {}
    for n in BIG:
        mine, theirs = gbig[n]
        if n == 'w_in':
            outs = _adamw_halves(w_in_t, mine, theirs, m_in_t, v_in_t, pos, name="adamw_" + n, cols=True)
            outs = [jnp.transpose(o) for o in outs]
        else:
            outs = _adamw_halves(w[n][0], mine, theirs, m[n][0], v[n][0], pos, name="adamw_" + n)
        grads[n], deltas[n], new_m[n], new_v[n] = [o[None] for o in outs]
    shapes = [tuple(w[n].shape) for n in small_names]
    gp = _pack([gsm[n] for n in small_names])
    d, m2, v2 = _adamw(_pack([w[n] for n in small_names]), gp, _pack([m[n] for n in small_names]),
                       _pack([v[n] for n in small_names]), name="adamw_small")
    for n, gg, dd, mm, vv in zip(small_names, _unpack(gp, shapes), _unpack(d, shapes), _unpack(m2, shapes),
                                 _unpack(v2, shapes)):
        grads[n], deltas[n], new_m[n], new_v[n] = gg, dd, mm, vv

    return (loss_out, dx[None], *[grads[n] for n in WEIGHTS], *[deltas[n] for n in WEIGHTS],
            *[new_m[n] for n in WEIGHTS], *[new_v[n] for n in WEIGHTS])
```

```python
import functools

import jax
import jax.numpy as jnp
from jax import lax
from jax.experimental import pallas as pl
from jax.experimental.pallas import tpu as pltpu

F32 = jnp.float32
BF16 = jnp.bfloat16

D_MODEL = 2048
N_Q_HEADS = 32
N_KV_HEADS = 8
HEAD_DIM = 64
WINDOW = 128
ATTN_BLOCK = 128
ROT_DIM = 16
ROPE_THETA = 500000.0
SSD_HEADS = 32
SSD_HEAD_DIM = 64
SSD_INNER = 2048
SSD_GROUPS = 8
SSD_STATE = 128
SSD_CONV = 4
SSD_CHUNK = 128
ATTN_WIDTH = 2048
KV_WIDTH = 512
BC_WIDTH = 1024
CONV_CH = 4096
IN_PROJ_WIDTH = 9248
MAIN_WIDTH = 9216
D_FF = 5632
FFN_CONV = 3
EPS = 1e-6
O_Q, O_K, O_V, O_Z, O_XBC, O_DT = 0, 2048, 2560, 3072, 5120, 9216

ADAM_LR = 0.001
ADAM_B1 = 0.9
ADAM_B2 = 0.999
ADAM_EPS = 1e-08
ADAM_WD = 0.01
ADAM_STEP = 10

N_CHIPS = 4
NEG = -1e30
LANES = 128
VMEM_LIMIT = 48 * 1024 * 1024
MESH = pl.DeviceIdType.MESH
HBM_SPEC = pl.BlockSpec(memory_space=pltpu.HBM)
TOKEN = jax.ShapeDtypeStruct((8, LANES), F32)

WEIGHTS = ['norm_mix', 'w_in', 'sinks', 'attn_out_norm', 'ssd_conv_w', 'ssd_conv_b', 'dt_bias', 'a_log', 'ssd_d',
           'ssd_norm', 'w_out', 'norm_ffn', 'w_up', 'ffn_conv_w', 'ffn_conv_b', 'w_down', 'norm_final']
BIG = ['w_in', 'w_out', 'w_up', 'w_down']


def _cp(sem=None, vmem=VMEM_LIMIT):
    kw = {'vmem_limit_bytes': vmem}
    if sem is not None:
        kw['dimension_semantics'] = sem
    return pltpu.CompilerParams(**kw)


def _tile(n, pref):
    if n <= pref:
        return n
    t = (pref // LANES) * LANES
    while t > LANES and n % t:
        t -= LANES
    assert n % t == 0, (n, pref)
    return t


def _rows(n, pref):
    t = min(n, pref)
    while n % t:
        t -= 8
    if 4 * t < pref:
        t = pref
        while n % t:
            t += 8
    return t


def _iota(shape, dim):
    return lax.broadcasted_iota(jnp.int32, shape, dim)


def _dot(a, b, mode='nn'):
    dn = {'nn': (((1,), (0,)), ((), ())), 'nt': (((1,), (1,)), ((), ())), 'tn': (((0,), (0,)), ((), ()))}[mode]
    return lax.dot_general(a.astype(BF16), b.astype(BF16), dn, preferred_element_type=F32)


def _dot_exact(a, b):
    return lax.dot_general(a, b, (((1,), (0,)), ((), ())), precision=lax.Precision.HIGHEST,
                           preferred_element_type=F32)


def _sigmoid(x):
    return 1.0 / (1.0 + jnp.exp(-x))


def _softplus(x):
    return jnp.maximum(x, 0.0) + jnp.log(1.0 + jnp.exp(-jnp.abs(x)))


def _matmul(a, b, *, mode, name, out_dtype=F32, add=None, deps=(), tm=1024, tn=1024, tk=2048,
            a_halves=False, b_halves=False, b_owner=False, owner_major=False, n_limit=None, k_limit=None,
            m_rows=None):
    ash, bsh = (a.shape[1:] if a_halves else a.shape), (b.shape[1:] if (b_halves or b_owner) else b.shape)
    if mode == 'nn':
        (m, k), (k2, n) = ash, bsh
    elif mode == 'nt':
        (m, k), (n, k2) = ash, bsh
    else:
        (k, m), (k2, n) = ash, bsh
    if n_limit is not None:
        assert mode == 'nt' and n_limit <= n
        n = n_limit
    if k_limit is not None:
        assert mode == 'nn' and k_limit <= k2
        k2 = k_limit
    if a_halves:
        assert mode == 'nt'
        k = 2 * k
    if b_halves:
        assert mode == 'tn'
        n = 2 * n
    if b_owner:
        assert mode in ('nn', 'nt')
        if mode == 'nn':
            n = 4 * n
        else:
            k2 = 4 * k2
    assert k == k2, (a.shape, b.shape, mode)
    tm = _tile(m, tm)
    tn = _tile(n // 4 if (owner_major or (b_owner and mode == 'nn')) else (n // 2 if b_halves else n), tn)
    tk = _tile(k // 4 if (b_owner and mode == 'nt') else (k // 2 if a_halves else k), tk)
    nk = k // tk
    has_add = add is not None
    assert not (has_add and owner_major)

    def body(*refs):
        a_ref, b_ref = refs[:2]
        add_ref = refs[2] if has_add else None

        def finish(r, o_ref):
            if has_add:
                r = r + add_ref[...].astype(F32)
            o_ref[...] = r.astype(out_dtype)

        if nk == 1:
            finish(_dot(a_ref[...], b_ref[...], mode), refs[-1])
            return
        o_ref, acc = refs[-2:]
        kk = pl.program_id(2)

        @pl.when(kk == 0)
        def _():
            acc[...] = _dot(a_ref[...], b_ref[...], mode)

        @pl.when((kk > 0) & (kk < nk - 1))
        def _():
            acc[...] += _dot(a_ref[...], b_ref[...], mode)

        @pl.when(kk == nk - 1)
        def _():
            finish(acc[...] + _dot(a_ref[...], b_ref[...], mode), o_ref)

    if mode == 'tn':
        a_spec = pl.BlockSpec((tk, tm), lambda i, j, kk: (kk, i))
    elif a_halves:
        nkh = nk // 2
        a_spec = pl.BlockSpec((None, tm, tk), lambda i, j, kk: (kk // nkh, i, kk % nkh))
    else:
        a_spec = pl.BlockSpec((tm, tk), lambda i, j, kk: (i, kk))
    if mode == 'nt' and b_owner:
        nkq = nk // 4
        b_spec = pl.BlockSpec((None, tn, tk), lambda i, j, kk: (kk // nkq, j, kk % nkq))
    elif mode == 'nt':
        b_spec = pl.BlockSpec((tn, tk), lambda i, j, kk: (j, kk))
    elif b_owner:
        njq = (n // 4) // tn
        b_spec = pl.BlockSpec((None, tk, tn), lambda i, j, kk: (j // njq, kk, j % njq))
    elif b_halves:
        njh = (n // 2) // tn
        b_spec = pl.BlockSpec((None, tk, tn), lambda i, j, kk: (j // njh, kk, j % njh))
    else:
        b_spec = pl.BlockSpec((tk, tn), lambda i, j, kk: (kk, j))
    if owner_major:
        njo = (n // 4) // tn
        o_spec = pl.BlockSpec((None, tm, tn), lambda i, j, kk: (j // njo, i, j % njo))
        out_shape = jax.ShapeDtypeStruct((N_CHIPS, m, n // 4), out_dtype)
    else:
        o_spec = pl.BlockSpec((tm, tn), lambda i, j, kk: (i, j))
        out_shape = jax.ShapeDtypeStruct((m if m_rows is None else m_rows, n), out_dtype)
    dep_spec = pl.BlockSpec((8, LANES), lambda i, j, kk: (0, 0))
    in_specs = [a_spec, b_spec] + ([pl.BlockSpec((tm, tn), lambda i, j, kk: (i, j))] if has_add else [])
    in_specs += [dep_spec] * len(deps)
    args = (a, b) + ((add,) if has_add else ()) + tuple(deps)
    return pl.pallas_call(
        body, name=name, grid=(m // tm, n // tn, nk), in_specs=in_specs, out_specs=o_spec, out_shape=out_shape,
        scratch_shapes=[pltpu.VMEM((tm, tn), F32)] if nk > 1 else [],
        compiler_params=_cp(("parallel", "parallel", "arbitrary")))(*args)


def _rmsnorm_fwd(x, g, name, deps=()):
    t, d = x.shape
    tb = _rows(t, 256)

    def body(x_ref, g_ref, *rest):
        o_ref = rest[-1]
        xv = x_ref[...]
        r = lax.rsqrt(jnp.mean(xv * xv, axis=-1, keepdims=True) + EPS)
        o_ref[...] = (xv * r * g_ref[...]).astype(BF16)

    dep_spec = pl.BlockSpec((8, LANES), lambda i: (0, 0))
    return pl.pallas_call(
        body, name=name, grid=(t // tb,),
        in_specs=[pl.BlockSpec((tb, d), lambda i: (i, 0)), pl.BlockSpec((1, d), lambda i: (0, 0))]
        + [dep_spec] * len(deps),
        out_specs=pl.BlockSpec((tb, d), lambda i: (i, 0)), out_shape=jax.ShapeDtypeStruct((t, d), BF16),
        compiler_params=_cp(("parallel",)))(x, g, *deps)


def _rmsnorm_bwd(x, g, dy, res, name, deps=()):
    t, d = x.shape
    tb = _rows(t, 256)

    def body(x_ref, g_ref, dy_ref, res_ref, *rest):
        dx_ref, dx16_ref, dg_ref = rest[-3:]
        i = pl.program_id(0)
        xv = x_ref[...]
        dyv = dy_ref[...].astype(F32)
        r = lax.rsqrt(jnp.mean(xv * xv, axis=-1, keepdims=True) + EPS)
        u = dyv * g_ref[...]
        dx = r * u - xv * (r * r * r * jnp.mean(u * xv, axis=-1, keepdims=True)) + res_ref[...]
        dx_ref[...] = dx
        dx16_ref[...] = dx.astype(BF16)
        part = jnp.sum(dyv * xv * r, axis=0, keepdims=True)

        @pl.when(i == 0)
        def _():
            dg_ref[...] = part

        @pl.when(i > 0)
        def _():
            dg_ref[...] += part

    row = pl.BlockSpec((tb, d), lambda i: (i, 0))
    vec = pl.BlockSpec((1, d), lambda i: (0, 0))
    return pl.pallas_call(
        body, name=name, grid=(t // tb,),
        in_specs=[row, vec, row, row] + [pl.BlockSpec((8, LANES), lambda i: (0, 0))] * len(deps),
        out_specs=[row, row, vec],
        out_shape=[jax.ShapeDtypeStruct((t, d), F32), jax.ShapeDtypeStruct((t, d), BF16),
                   jax.ShapeDtypeStruct((1, d), F32)],
        compiler_params=_cp(("arbitrary",)))(x, g, dy, res, *deps)


def _final_loss(h, g, tgt):
    t, d = h.shape
    tb = _rows(t, 256)

    def body(h_ref, g_ref, t_ref, loss_ref, dh_ref, dh16_ref, dg_ref):
        i = pl.program_id(0)
        hv = h_ref[...]
        gv = g_ref[...]
        r = lax.rsqrt(jnp.mean(hv * hv, axis=-1, keepdims=True) + EPS)
        y = hv * r * gv
        diff = y - t_ref[...]
        lpart = jnp.sum(jnp.sum(diff * diff, axis=1, keepdims=True), axis=0, keepdims=True) * (0.5 / d)
        dy = diff * (1.0 / d)
        u = dy * gv
        dh = r * u - hv * (r * r * r * jnp.mean(u * hv, axis=-1, keepdims=True))
        dh_ref[...] = dh
        dh16_ref[...] = dh.astype(BF16)
        gpart = jnp.sum(dy * hv * r, axis=0, keepdims=True)
        lrow = jnp.broadcast_to(lpart, (1, LANES))

        @pl.when(i == 0)
        def _():
            loss_ref[...] = lrow
            dg_ref[...] = gpart

        @pl.when(i > 0)
        def _():
            loss_ref[...] += lrow
            dg_ref[...] += gpart

    row = pl.BlockSpec((tb, d), lambda i: (i, 0))
    vec = pl.BlockSpec((1, d), lambda i: (0, 0))
    return pl.pallas_call(
        body, name="final_loss", grid=(t // tb,), in_specs=[row, vec, row],
        out_specs=[pl.BlockSpec((1, LANES), lambda i: (0, 0)), row, row, vec],
        out_shape=[jax.ShapeDtypeStruct((1, LANES), F32), jax.ShapeDtypeStruct((t, d), F32),
                   jax.ShapeDtypeStruct((t, d), BF16), jax.ShapeDtypeStruct((1, d), F32)],
        compiler_params=_cp(("arbitrary",)))(h, g, tgt)


def _rope_tables(t):
    pos = jnp.arange(t, dtype=F32)
    inv = 1.0 / (ROPE_THETA ** (jnp.arange(0, ROT_DIM, 2, dtype=F32) / ROT_DIM))
    ang = pos[:, None] * inv[None, :]
    cos, sin = jnp.cos(ang), jnp.sin(ang)
    half = ROT_DIM // 2
    rest = HEAD_DIM - ROT_DIM
    c = jnp.concatenate([cos, cos, jnp.ones((t, rest), F32)], axis=1)
    s1 = jnp.concatenate([-sin, jnp.zeros((t, half + rest), F32)], axis=1)
    s2 = jnp.concatenate([jnp.zeros((t, half), F32), sin, jnp.zeros((t, rest), F32)], axis=1)
    return tuple(jnp.tile(v, (1, LANES // HEAD_DIM)) for v in (c, s1, s2))


def _rope(x, c, s1, s2):
    half = ROT_DIM // 2
    return x * c + pltpu.roll(x, LANES - half, 1) * s1 + pltpu.roll(x, half, 1) * s2


def _rope_t(g, c, s1, s2):
    half = ROT_DIM // 2
    return g * c + pltpu.roll(g * s1, half, 1) + pltpu.roll(g * s2, LANES - half, 1)


def _band_masks(i, heads):
    n = heads * ATTN_BLOCK
    q = jnp.bitwise_and(_iota((n, ATTN_BLOCK), 0), ATTN_BLOCK - 1)
    j = _iota((n, ATTN_BLOCK), 1)
    upper = j > q
    return upper, upper & (j < jnp.where(i > 0, 0, ATTN_BLOCK))


def _fold_band(full, upper):
    return jnp.where(upper, full[:, :ATTN_BLOCK], full[:, ATTN_BLOCK:])


def _unfold_band(band, upper):
    return jnp.concatenate([jnp.where(upper, band, 0.0), jnp.where(upper, 0.0, band)], axis=1)


def _half_masks():
    lane = _iota((1, LANES), 1)
    return [(lane < HEAD_DIM).astype(F32), (lane >= HEAD_DIM).astype(F32)]


def _stack_heads(blocks, hm, j):
    pieces = []
    for r in range(4):
        qb, half = (4 * j + r) // 2, (4 * j + r) % 2
        piece = blocks[qb] * hm[half]
        if half != j:
            piece = pltpu.roll(piece, HEAD_DIM, 1)
        pieces.append(piece)
    return jnp.concatenate(pieces, axis=0)


def _unstack_heads(stacked, j):
    out = []
    for qb in (2 * j, 2 * j + 1):
        acc = None
        for half in range(2):
            r = 2 * qb + half - 4 * j
            piece = stacked[r * ATTN_BLOCK:(r + 1) * ATTN_BLOCK]
            if half != j:
                piece = pltpu.roll(piece, HEAD_DIM, 1)
            acc = piece if acc is None else acc + piece
        out.append((qb, acc))
    return out


def _sink_column(sink_ref, base):
    return jnp.concatenate([jnp.full((ATTN_BLOCK, 1), sink_ref[base + r], F32) for r in range(4)], axis=0)


def _attn_specs(nb_clamp):
    blk = ATTN_BLOCK
    kb, vb = O_K // LANES, O_V // LANES

    def cur(i):
        return jnp.minimum(i, nb_clamp)

    def prev(i):
        return jnp.maximum(jnp.minimum(i, nb_clamp + 1) - 1, 0)

    q = pl.BlockSpec((blk, 512), lambda p, i: (cur(i), p))
    kc = pl.BlockSpec((blk, LANES), lambda p, i: (cur(i), kb + p))
    kp = pl.BlockSpec((blk, LANES), lambda p, i: (prev(i), kb + p))
    vc = pl.BlockSpec((blk, LANES), lambda p, i: (cur(i), vb + p))
    vp = pl.BlockSpec((blk, LANES), lambda p, i: (prev(i), vb + p))
    tc = pl.BlockSpec((blk, LANES), lambda p, i: (cur(i), 0))
    tp = pl.BlockSpec((blk, LANES), lambda p, i: (prev(i), 0))
    return q, kc, kp, vc, vp, tc, tp


def _attn_fwd(proj, sinks, tables):
    t = proj.shape[0]
    nb = t // ATTN_BLOCK
    scale = HEAD_DIM ** -0.5

    def body(sink_ref, q_ref, kc_ref, kp_ref, vc_ref, vp_ref, cc_ref, s1c_ref, s2c_ref, cp_ref, s1p_ref, s2p_ref,
             o_ref):
        p = pl.program_id(0)
        i = pl.program_id(1)
        cc, s1c, s2c = cc_ref[...], s1c_ref[...], s2c_ref[...]
        kband = jnp.concatenate([_rope(kp_ref[...], cp_ref[...], s1p_ref[...], s2p_ref[...]),
                                 _rope(kc_ref[...], cc, s1c, s2c)], axis=0).astype(BF16)
        vband = jnp.concatenate([vp_ref[...], vc_ref[...]], axis=0)
        hm = _half_masks()
        vsel = [(vband * hm[j]).astype(BF16) for j in range(2)]
        upper, dropped = _band_masks(i, 1)
        for qb in range(4):
            qr = _rope(q_ref[:, qb * LANES:(qb + 1) * LANES], cc, s1c, s2c)
            acc = jnp.zeros((ATTN_BLOCK, LANES), F32)
            for half in range(2):
                hh = qb * 2 + half
                j = hh // 4
                qs = qr * hm[half]
                if half != j:
                    qs = pltpu.roll(qs, HEAD_DIM, 1)
                s = jnp.where(dropped, NEG, _fold_band(_dot(qs, kband, 'nt'), upper) * scale)
                sink = sink_ref[p * 8 + hh]
                m = jnp.maximum(jnp.max(s, axis=1, keepdims=True), sink)
                pe = jnp.exp(s - m)
                den = jnp.sum(pe, axis=1, keepdims=True) + jnp.exp(sink - m)
                o = _dot(_unfold_band(pe / den, upper), vsel[j])
                if half != j:
                    o = pltpu.roll(o, HEAD_DIM, 1)
                acc = acc + o
            o_ref[:, qb * LANES:(qb + 1) * LANES] = acc

    q, kc, kp, vc, vp, tc, tp = _attn_specs(nb - 1)
    smem = pl.BlockSpec(memory_space=pltpu.SMEM)
    return pl.pallas_call(
        body, name="attn_fwd", grid=(4, nb),
        in_specs=[smem, q, kc, kp, vc, vp, tc, tc, tc, tp, tp, tp],
        out_specs=pl.BlockSpec((ATTN_BLOCK, 512), lambda p, i: (i, p)),
        out_shape=jax.ShapeDtypeStruct((t, ATTN_WIDTH), F32),
        compiler_params=_cp(("parallel", "arbitrary")))(sinks, proj, proj, proj, proj, proj, *tables, *tables)


def _attn_bwd(proj, sinks, tables, dout):
    t = proj.shape[0]
    nb = t // ATTN_BLOCK
    scale = HEAD_DIM ** -0.5

    def body(sink_ref, q_ref, kc_ref, kp_ref, vc_ref, vp_ref, cc_ref, s1c_ref, s2c_ref, cp_ref, s1p_ref, s2p_ref,
             do_ref, dq_ref, dk_ref, dv_ref, ds_ref, carry_k, carry_v):
        p = pl.program_id(0)
        i = pl.program_id(1)
        ptab = (cp_ref[...], s1p_ref[...], s2p_ref[...])

        @pl.when(i == 0)
        def _():
            carry_k[...] = jnp.zeros_like(carry_k)
            carry_v[...] = jnp.zeros_like(carry_v)
            ds_ref[...] = jnp.zeros_like(ds_ref)

        @pl.when(i < nb)
        def _():
            cc, s1c, s2c = cc_ref[...], s1c_ref[...], s2c_ref[...]
            kband = jnp.concatenate([_rope(kp_ref[...], *ptab), _rope(kc_ref[...], cc, s1c, s2c)], axis=0)
            vband = jnp.concatenate([vp_ref[...], vc_ref[...]], axis=0)
            hm = _half_masks()
            kband16 = kband.astype(BF16)
            vband16 = vband.astype(BF16)
            upper, dropped = _band_masks(i, 4)
            dkb = jnp.zeros((2 * ATTN_BLOCK, LANES), F32)
            dvb = jnp.zeros((2 * ATTN_BLOCK, LANES), F32)
            row8 = _iota((8, LANES), 0)
            dsink = jnp.zeros((8, LANES), F32)
            qr = [_rope(q_ref[:, qb * LANES:(qb + 1) * LANES], cc, s1c, s2c) for qb in range(4)]
            dob = [do_ref[:, qb * LANES:(qb + 1) * LANES] for qb in range(4)]
            for j in range(2):
                qst = _stack_heads(qr, hm, j).astype(BF16)
                dost = _stack_heads(dob, hm, j).astype(BF16)
                s = jnp.where(dropped, NEG, _fold_band(_dot(qst, kband16, 'nt'), upper) * scale)
                sink = _sink_column(sink_ref, p * 8 + 4 * j)
                m = jnp.maximum(jnp.max(s, axis=1, keepdims=True), sink)
                pe = jnp.exp(s - m)
                psink = jnp.exp(sink - m)
                den = jnp.sum(pe, axis=1, keepdims=True) + psink
                pr = pe / den
                dvb = dvb + _dot(_unfold_band(pr, upper).T, dost)
                dp = _fold_band(_dot(dost, vband16, 'nt'), upper)
                delta = jnp.sum(pr * dp, axis=1, keepdims=True)
                dsc = _unfold_band(pr * (dp - delta) * scale, upper)
                dsk = psink / den * delta
                for r in range(4):
                    part = jnp.sum(dsk[r * ATTN_BLOCK:(r + 1) * ATTN_BLOCK])
                    dsink = dsink + jnp.where(row8 == 4 * j + r, -part, 0.0)
                for qb, dqb in _unstack_heads(_dot(dsc, kband * hm[j]), j):
                    dq_ref[:, qb * LANES:(qb + 1) * LANES] = _rope_t(dqb, cc, s1c, s2c).astype(BF16)
                dkb = dkb + _dot(dsc.T, qst)
            ds_ref[0] += dsink
            dk_ref[...] = _rope_t(carry_k[...] + dkb[:ATTN_BLOCK], *ptab).astype(BF16)
            dv_ref[...] = (carry_v[...] + dvb[:ATTN_BLOCK]).astype(BF16)
            carry_k[...] = dkb[ATTN_BLOCK:]
            carry_v[...] = dvb[ATTN_BLOCK:]

        @pl.when(i == nb)
        def _():
            dk_ref[...] = _rope_t(carry_k[...], *ptab).astype(BF16)
            dv_ref[...] = carry_v[...].astype(BF16)

    q, kc, kp, vc, vp, tc, tp = _attn_specs(nb - 1)
    smem = pl.BlockSpec(memory_space=pltpu.SMEM)
    qblk = pl.BlockSpec((ATTN_BLOCK, 512), lambda p, i: (jnp.minimum(i, nb - 1), p))
    kvout = pl.BlockSpec((ATTN_BLOCK, LANES), lambda p, i: (jnp.maximum(i - 1, 0), p))
    return pl.pallas_call(
        body, name="attn_bwd", grid=(4, nb + 1),
        in_specs=[smem, q, kc, kp, vc, vp, tc, tc, tc, tp, tp, tp, qblk],
        out_specs=[qblk, kvout, kvout, pl.BlockSpec((1, 8, LANES), lambda p, i: (p, 0, 0))],
        out_shape=[jax.ShapeDtypeStruct((t, ATTN_WIDTH), BF16), jax.ShapeDtypeStruct((t, KV_WIDTH), BF16),
                   jax.ShapeDtypeStruct((t, KV_WIDTH), BF16), jax.ShapeDtypeStruct((4, 8, LANES), F32)],
        scratch_shapes=[pltpu.VMEM((ATTN_BLOCK, LANES), F32), pltpu.VMEM((ATTN_BLOCK, LANES), F32)],
        compiler_params=_cp(("parallel", "arbitrary")))(sinks, proj, proj, proj, proj, proj, *tables, *tables, dout)


def _shift_rows(x, prev8, j):
    r = pltpu.roll(x, j, 0)
    head = jnp.where(_iota((8, 1), 0) < j, pltpu.roll(prev8, j, 0), r[:8])
    if x.shape[0] == 8:
        return head
    return jnp.concatenate([head, r[8:]], axis=0)


def _shift_rows_up(x, next8, j):
    n = x.shape[0]
    r = pltpu.roll(x, n - j, 0)
    tail = jnp.where(_iota((8, 1), 0) >= 8 - j, pltpu.roll(next8, 8 - j, 0), r[n - 8:])
    return jnp.concatenate([r[:n - 8], tail], axis=0)


def _conv_apply(x, prev8, w, b, taps):
    u = b + x * w[taps - 1:taps]
    for j in range(1, taps):
        u = u + _shift_rows(x, prev8, j) * w[taps - 1 - j:taps - j]
    return u


def _conv_grads(du, du_next8, x, w, taps):
    dx = du * w[taps - 1:taps]
    rowk = _iota((taps, 1), 0)
    dw = jnp.where(rowk == taps - 1, jnp.sum(du * x, axis=0, keepdims=True), 0.0)
    for j in range(1, taps):
        ahead = _shift_rows_up(du, du_next8, j)
        dx = dx + ahead * w[taps - 1 - j:taps - j]
        dw = dw + jnp.where(rowk == taps - 1 - j, jnp.sum(ahead * x, axis=0, keepdims=True), 0.0)
    return dx, dw, jnp.sum(du, axis=0, keepdims=True)


def _conv_specs(tb, tc, col0, t):
    c0 = col0 // tc
    cur = pl.BlockSpec((tb, tc), lambda j, i: (i, c0 + j))
    prev = pl.BlockSpec((8, tc), lambda j, i: (jnp.maximum(i * (tb // 8) - 1, 0), c0 + j))
    nxt = pl.BlockSpec((8, tc), lambda j, i: (jnp.minimum((i + 1) * (tb // 8), t // 8 - 1), c0 + j))
    return cur, prev, nxt


def _conv_fwd(x, w, b, *, col0, width, act, name):
    t = x.shape[0]
    taps = w.shape[0]
    tb, tc = _rows(t, 512), _tile(width, 1024)
    assert col0 % tc == 0

    def body(x_ref, xp_ref, w_ref, b_ref, o_ref):
        i = pl.program_id(1)
        prev8 = jnp.where(i > 0, xp_ref[...], 0.0)
        u = _conv_apply(x_ref[...], prev8, w_ref[...], b_ref[...], taps)
        if act:
            u = u * _sigmoid(u)
        o_ref[...] = u

    cur, prev, _ = _conv_specs(tb, tc, col0, t)
    par = pl.BlockSpec((taps, tc), lambda j, i: (0, j))
    bias = pl.BlockSpec((1, tc), lambda j, i: (0, j))
    return pl.pallas_call(
        body, name=name, grid=(width // tc, t // tb), in_specs=[cur, prev, par, bias],
        out_specs=pl.BlockSpec((tb, tc), lambda j, i: (i, j)), out_shape=jax.ShapeDtypeStruct((t, width), F32),
        compiler_params=_cp(("parallel", "parallel")))(x, x, w, b)


def _conv_silu_dact(x, w, b, dout, *, col0, width, name):
    t = x.shape[0]
    taps = w.shape[0]
    tb, tc = _rows(t, 512), _tile(width, 1024)

    def body(x_ref, xp_ref, w_ref, b_ref, d_ref, o_ref):
        i = pl.program_id(1)
        prev8 = jnp.where(i > 0, xp_ref[...], 0.0)
        u = _conv_apply(x_ref[...], prev8, w_ref[...], b_ref[...], taps)
        sg = _sigmoid(u)
        o_ref[...] = d_ref[...] * (sg * (1.0 + u * (1.0 - sg)))

    cur, prev, _ = _conv_specs(tb, tc, col0, t)
    par = pl.BlockSpec((taps, tc), lambda j, i: (0, j))
    bias = pl.BlockSpec((1, tc), lambda j, i: (0, j))
    out = pl.BlockSpec((tb, tc), lambda j, i: (i, j))
    return pl.pallas_call(
        body, name=name, grid=(width // tc, t // tb), in_specs=[cur, prev, par, bias, out],
        out_specs=out, out_shape=jax.ShapeDtypeStruct((t, width), F32),
        compiler_params=_cp(("parallel", "parallel")))(x, x, w, b, dout)


def _conv_bwd(x, w, du, *, col0, width, name):
    t = x.shape[0]
    taps = w.shape[0]
    tb, tc = _rows(t, 512), _tile(width, 1024)
    nrow = t // tb

    def body(x_ref, w_ref, du_ref, dun_ref, dx_ref, dw_ref, db_ref):
        i = pl.program_id(1)
        next8 = jnp.where(i < nrow - 1, dun_ref[...], 0.0)
        dx, dwv, dbv = _conv_grads(du_ref[...], next8, x_ref[...], w_ref[...], taps)
        dx_ref[...] = dx.astype(BF16)

        @pl.when(i == 0)
        def _():
            dw_ref[...] = dwv
            db_ref[...] = dbv

        @pl.when(i > 0)
        def _():
            dw_ref[...] += dwv
            db_ref[...] += dbv

    cur, _, _ = _conv_specs(tb, tc, col0, t)
    dcur, _, dnxt = _conv_specs(tb, tc, 0, t)
    par = pl.BlockSpec((taps, tc), lambda j, i: (0, j))
    bias = pl.BlockSpec((1, tc), lambda j, i: (0, j))
    return pl.pallas_call(
        body, name=name, grid=(width // tc, nrow), in_specs=[cur, par, dcur, dnxt],
        out_specs=[dcur, par, bias],
        out_shape=[jax.ShapeDtypeStruct((t, width), BF16), jax.ShapeDtypeStruct((taps, width), F32),
                   jax.ShapeDtypeStruct((1, width), F32)],
        compiler_params=_cp(("parallel", "arbitrary")))(x, w, du, du)


def _ffn_specs(tb, tc, t):
    nc = D_FF // tc

    def cur(half):
        return pl.BlockSpec((tb, tc), lambda j, i: (i, half * nc + j))

    def prev(half):
        return pl.BlockSpec((8, tc), lambda j, i: (jnp.maximum(i * (tb // 8) - 1, 0), half * nc + j))

    def nxt(half):
        return pl.BlockSpec((8, tc), lambda j, i: (jnp.minimum((i + 1) * (tb // 8), t // 8 - 1), half * nc + j))

    def par(rows, half):
        return pl.BlockSpec((rows, tc), lambda j, i: (0, half * nc + j))

    return cur, prev, nxt, par


def _ffn_act_fwd(u0, w, b):
    t = u0.shape[0]
    tb, tc = _rows(t, 512), _tile(D_FF, 1408)
    cur, prev, _, par = _ffn_specs(tb, tc, t)

    def body(g_ref, gp_ref, v_ref, vp_ref, wg_ref, wv_ref, bg_ref, bv_ref, o_ref):
        i = pl.program_id(1)
        ug = _conv_apply(g_ref[...], jnp.where(i > 0, gp_ref[...], 0.0), wg_ref[...], bg_ref[...], FFN_CONV)
        uv = _conv_apply(v_ref[...], jnp.where(i > 0, vp_ref[...], 0.0), wv_ref[...], bv_ref[...], FFN_CONV)
        o_ref[...] = (ug * _sigmoid(ug) * uv).astype(BF16)

    return pl.pallas_call(
        body, name="ffn_act_fwd", grid=(D_FF // tc, t // tb),
        in_specs=[cur(0), prev(0), cur(1), prev(1), par(FFN_CONV, 0), par(FFN_CONV, 1), par(1, 0), par(1, 1)],
        out_specs=pl.BlockSpec((tb, tc), lambda j, i: (i, j)), out_shape=jax.ShapeDtypeStruct((t, D_FF), BF16),
        compiler_params=_cp(("parallel", "parallel")))(u0, u0, u0, u0, w, w, b, b)


def _ffn_act_bwd(u0, w, b, da):
    t = u0.shape[0]
    tb, tc = _rows(t, 256), _tile(D_FF, 1408)
    nrow = t // tb
    taps = FFN_CONV
    cur, prev, nxt, par = _ffn_specs(tb, tc, t)

    def dact(ug, uv, dav):
        sg = _sigmoid(ug)
        return dav * uv * (sg * (1.0 + ug * (1.0 - sg))), dav * ug * sg

    def body(g_ref, gp_ref, gn_ref, v_ref, vp_ref, vn_ref, wg_ref, wv_ref, bg_ref, bv_ref, da_ref, dan_ref,
             dx_ref, dw_ref, db_ref):
        i = pl.program_id(1)
        xg, xv = g_ref[...], v_ref[...]
        gp = jnp.where(i > 0, gp_ref[...], 0.0)
        vp = jnp.where(i > 0, vp_ref[...], 0.0)
        wg, wv, bg, bv = wg_ref[...], wv_ref[...], bg_ref[...], bv_ref[...]
        dug, duv = dact(_conv_apply(xg, gp, wg, bg, taps), _conv_apply(xv, vp, wv, bv, taps),
                        da_ref[...].astype(F32))
        dan = jnp.where(i < nrow - 1, dan_ref[...].astype(F32)[:8], 0.0)
        dugn, duvn = dact(_conv_apply(gn_ref[...], xg[tb - 8:], wg, bg, taps),
                          _conv_apply(vn_ref[...], xv[tb - 8:], wv, bv, taps), dan)
        dxg, dwg, dbg = _conv_grads(dug, dugn, xg, wg, taps)
        dxv, dwv, dbv = _conv_grads(duv, duvn, xv, wv, taps)
        dx_ref[0] = dxg.astype(BF16)
        dx_ref[1] = dxv.astype(BF16)

        @pl.when(i == 0)
        def _():
            dw_ref[0] = dwg
            dw_ref[1] = dwv
            db_ref[0] = dbg
            db_ref[1] = dbv

        @pl.when(i > 0)
        def _():
            dw_ref[0] += dwg
            dw_ref[1] += dwv
            db_ref[0] += dbg
            db_ref[1] += dbv

    da_cur = pl.BlockSpec((tb, tc), lambda j, i: (i, j))
    da_nxt = pl.BlockSpec((16, tc), lambda j, i: (jnp.minimum((i + 1) * (tb // 16), t // 16 - 1), j))
    return pl.pallas_call(
        body, name="ffn_act_bwd", grid=(D_FF // tc, nrow),
        in_specs=[cur(0), prev(0), nxt(0), cur(1), prev(1), nxt(1), par(taps, 0), par(taps, 1), par(1, 0),
                  par(1, 1), da_cur, da_nxt],
        out_specs=[pl.BlockSpec((2, tb, tc), lambda j, i: (0, i, j)),
                   pl.BlockSpec((2, taps, tc), lambda j, i: (0, 0, j)),
                   pl.BlockSpec((2, 1, tc), lambda j, i: (0, 0, j))],
        out_shape=[jax.ShapeDtypeStruct((2, t, D_FF), BF16), jax.ShapeDtypeStruct((2, taps, D_FF), F32),
                   jax.ShapeDtypeStruct((2, 1, D_FF), F32)],
        compiler_params=_cp(("parallel", "arbitrary")))(u0, u0, u0, u0, u0, u0, w, w, b, b, da, da)


def _head_masks():
    lane = _iota((1, 4 * SSD_HEAD_DIM), 1)
    return [((lane >= r * SSD_HEAD_DIM) & (lane < (r + 1) * SSD_HEAD_DIM)).astype(F32) for r in range(4)]


def _segsum(v):
    first = _iota((1, LANES), 1) < SSD_HEAD_DIM
    halves = []
    for k in range(2):
        vh = v[:, k * LANES:(k + 1) * LANES]
        both = jnp.sum(vh, axis=1, keepdims=True)
        one = jnp.sum(jnp.where(first, vh, 0.0), axis=1, keepdims=True)
        halves.append(jnp.where(first, one, both - one))
    return jnp.concatenate(halves, axis=1)


def _ssd_common(raw_e, prow, rawr4, bcol, acol):
    n = SSD_CHUNK
    dt_e = _softplus(raw_e + prow[0:1, :])
    a_e = -jnp.exp(prow[1:2, :])
    d_e = prow[2:3, :]
    tril = (_iota((n, n), 0) >= _iota((n, n), 1)).astype(F32)
    acs_e = _dot_exact(tril, dt_e * a_e)
    last_e = acs_e[n - 1:n, :]
    dtr4 = _softplus(rawr4 + bcol)
    triu = (_iota((n, n), 0) <= _iota((n, n), 1)).astype(F32)
    acs_r4 = _dot_exact(dtr4 * (-jnp.exp(acol)), triu)
    return dt_e, a_e, d_e, acs_e, last_e, acs_r4


def _decay_matrix(acs_e, acs_r4, r):
    n = SSD_CHUNK
    col = acs_e[:, r * SSD_HEAD_DIM:r * SSD_HEAD_DIM + 1]
    seg = col - acs_r4[r:r + 1, :]
    causal = _iota((n, n), 0) >= _iota((n, n), 1)
    return jnp.exp(jnp.where(causal, seg, NEG))


SSD_STEP_CHUNKS = 4
SSD_ROWS = SSD_STEP_CHUNKS * SSD_CHUNK


def _ssd_specs(t, rev):
    nb = t // SSD_ROWS
    xb, bb, cb = 0, SSD_INNER // SSD_STATE, (SSD_INNER + BC_WIDTH) // SSD_STATE

    def ch(c):
        return (nb - 1 - c) if rev else c

    x = pl.BlockSpec((SSD_ROWS, 256), lambda g, c: (ch(c), xb + g))
    bm = pl.BlockSpec((SSD_ROWS, SSD_STATE), lambda g, c: (ch(c), bb + g))
    cm = pl.BlockSpec((SSD_ROWS, SSD_STATE), lambda g, c: (ch(c), cb + g))
    dtc = pl.BlockSpec((1, SSD_ROWS, 256), lambda g, c: (g, ch(c), 0))
    dtr = pl.BlockSpec((1, 4, SSD_ROWS), lambda g, c: (g, 0, ch(c)))
    prow = pl.BlockSpec((1, 3, 256), lambda g, c: (g, 0, 0))
    pcol = pl.BlockSpec((1, 4, 1), lambda g, c: (g, 0, 0))
    st = pl.BlockSpec((1, SSD_STEP_CHUNKS, SSD_STATE, 256), lambda g, c: (g, ch(c), 0, 0))
    return x, bm, cm, dtc, dtr, prow, pcol, st, ch


def _ssd_params(dt_raw, dt_bias, a_log, ssd_d):
    t = dt_raw.shape[0]
    by_group = dt_raw.reshape(t, SSD_GROUPS, 4)
    dtc = jnp.repeat(by_group, SSD_HEAD_DIM, axis=2).transpose(1, 0, 2)
    dtr = by_group.transpose(1, 2, 0)
    prow = jnp.repeat(jnp.stack([dt_bias.reshape(SSD_GROUPS, 4), a_log.reshape(SSD_GROUPS, 4),
                                 ssd_d.reshape(SSD_GROUPS, 4)], axis=1), SSD_HEAD_DIM, axis=2)
    bcol = dt_bias.reshape(SSD_GROUPS, 4, 1)
    acol = a_log.reshape(SSD_GROUPS, 4, 1)
    return dtc, dtr, prow, bcol, acol


def _ssd_fwd(xbc, params):
    t = xbc.shape[0]
    nc = t // SSD_CHUNK
    dtc, dtr, prow, bcol, acol = params

    def body(x_ref, b_ref, c_ref, dtc_ref, dtr_ref, prow_ref, bcol_ref, acol_ref, y_ref, st_ref, s_scr):
        c = pl.program_id(1)

        @pl.when(c == 0)
        def _():
            s_scr[...] = jnp.zeros_like(s_scr)

        masks = _head_masks()
        s = s_scr[...]
        for k in range(SSD_STEP_CHUNKS):
            rows = slice(k * SSD_CHUNK, (k + 1) * SSD_CHUNK)
            dt_e, a_e, d_e, acs_e, last_e, acs_r4 = _ssd_common(
                dtc_ref[0, rows], prow_ref[0], dtr_ref[0][:, rows], bcol_ref[0], acol_ref[0])
            xv = x_ref[rows]
            bm, cm = b_ref[rows], c_ref[rows]
            st_ref[0, k] = s
            xdt = xv * dt_e
            cb = _dot(cm, bm, 'nt')
            y = _dot(cm, s) * jnp.exp(acs_e) + xv * d_e
            for r in range(4):
                mr = cb * _decay_matrix(acs_e, acs_r4, r)
                y = y + _dot(mr, xdt * masks[r])
            y_ref[rows] = y
            w = xdt * jnp.exp(last_e - acs_e)
            s = s * jnp.exp(last_e) + _dot(bm.T, w)
        s_scr[...] = s

    x, bm, cm, dtcs, dtrs, prs, pcs, st, _ = _ssd_specs(t, False)
    return pl.pallas_call(
        body, name="ssd_fwd", grid=(SSD_GROUPS, t // SSD_ROWS), in_specs=[x, bm, cm, dtcs, dtrs, prs, pcs, pcs],
        out_specs=[pl.BlockSpec((SSD_ROWS, 256), lambda g, c: (c, g)), st],
        out_shape=[jax.ShapeDtypeStruct((t, SSD_INNER), F32),
                   jax.ShapeDtypeStruct((SSD_GROUPS, nc, SSD_STATE, 256), F32)],
        scratch_shapes=[pltpu.VMEM((SSD_STATE, 256), F32)],
        compiler_params=_cp(("parallel", "arbitrary")))(xbc, xbc, xbc, dtc, dtr, prow, bcol, acol)


def _ssd_bwd(xbc, params, states, dy):
    t = xbc.shape[0]
    nc = t // SSD_CHUNK
    n = SSD_CHUNK
    dtc, dtr, prow, bcol, acol = params

    def body(x_ref, b_ref, c_ref, dtc_ref, dtr_ref, prow_ref, bcol_ref, acol_ref, st_ref, dy_ref,
             dx_ref, db_ref, dc_ref, ddt_ref, dp_ref, ds_scr):
        c = pl.program_id(1)

        @pl.when(c == 0)
        def _():
            ds_scr[...] = jnp.zeros_like(ds_scr)
            dp_ref[...] = jnp.zeros_like(dp_ref)

        masks = _head_masks()
        ds = ds_scr[...]
        for k in reversed(range(SSD_STEP_CHUNKS)):
            rows = slice(k * SSD_CHUNK, (k + 1) * SSD_CHUNK)
            raw_e = dtc_ref[0, rows]
            prw = prow_ref[0]
            dt_e, a_e, d_e, acs_e, last_e, acs_r4 = _ssd_common(raw_e, prw, dtr_ref[0][:, rows], bcol_ref[0], acol_ref[0])
            xv = x_ref[rows]
            bm, cm = b_ref[rows], c_ref[rows]
            s = st_ref[0, k]
            dyv = dy_ref[rows]
            e_e = jnp.exp(acs_e)
            dec_e = jnp.exp(last_e - acs_e)
            cd_e = jnp.exp(last_e)
            xdt = xv * dt_e
            w = xdt * dec_e
            b16, c16, s16, ds16 = bm.astype(BF16), cm.astype(BF16), s.astype(BF16), ds.astype(BF16)
            cb = _dot(c16, b16, 'nt')
            yoff_raw = _dot(c16, s16)
            dye = dyv * e_e
            dye16 = dye.astype(BF16)
            dcm = _dot(dye16, s16, 'nt')
            ds_prev = ds * cd_e + _dot(cm.T, dye16)
            dacs_e = _segsum(dyv * yoff_raw) * e_e
            dw = _dot(b16, ds16)
            dbm = _dot(w, ds16, 'nt')
            tdec = _segsum(dw * xdt) * dec_e
            dacs_e = dacs_e - tdec
            dlast_e = jnp.sum(tdec, axis=0, keepdims=True)
            dxdt = dw * dec_e
            dlast_e = dlast_e + _segsum(jnp.sum(ds * s, axis=0, keepdims=True)) * cd_e
            dcb = jnp.zeros((n, n), F32)
            for r in range(4):
                lm = _decay_matrix(acs_e, acs_r4, r)
                mr = cb * lm
                dyr16 = (dyv * masks[r]).astype(BF16)
                dm = _dot(dyr16, xdt * masks[r], 'nt')
                dcb = dcb + dm * lm
                dseg = dm * mr
                dcol = jnp.sum(dseg, axis=1, keepdims=True) - jnp.sum(dseg.T, axis=1, keepdims=True)
                dacs_e = dacs_e + dcol * masks[r]
                dxdt = dxdt + _dot(mr.T, dyr16)
            dcm = dcm + _dot(dcb, b16)
            dbm = dbm + _dot(dcb.T, c16)
            dacs_e = dacs_e + jnp.where(_iota((n, 1), 0) == n - 1, dlast_e, 0.0)
            triu = (_iota((n, n), 0) <= _iota((n, n), 1)).astype(F32)
            ddta_e = _dot_exact(triu, dacs_e)
            ddt_e = ddta_e * a_e + _segsum(dxdt * xv)
            dx_ref[rows] = dxdt * dt_e + dyv * d_e
            db_ref[rows] = dbm
            dc_ref[rows] = dcm
            draw_e = ddt_e * _sigmoid(raw_e + prw[0:1, :])
            draw_t = draw_e.T
            ddt_ref[0, :, rows] = jnp.concatenate([draw_t[r * SSD_HEAD_DIM:r * SSD_HEAD_DIM + 1] for r in range(4)], axis=0)
            dbias = jnp.sum(draw_e, axis=0, keepdims=True)
            dalog = jnp.sum(ddta_e * dt_e, axis=0, keepdims=True) * a_e
            dd = _segsum(jnp.sum(dyv * xv, axis=0, keepdims=True))
            row3 = _iota((3, 1), 0)
            dp_ref[0] += (jnp.where(row3 == 0, dbias, 0.0) + jnp.where(row3 == 1, dalog, 0.0)
                          + jnp.where(row3 == 2, dd, 0.0))
            ds = ds_prev
        ds_scr[...] = ds


    x, bm, cm, dtcs, dtrs, prs, pcs, st, ch = _ssd_specs(t, True)
    yblk = pl.BlockSpec((SSD_ROWS, 256), lambda g, c: (ch(c), g))
    nblk = pl.BlockSpec((SSD_ROWS, SSD_STATE), lambda g, c: (ch(c), g))
    return pl.pallas_call(
        body, name="ssd_bwd", grid=(SSD_GROUPS, t // SSD_ROWS),
        in_specs=[x, bm, cm, dtcs, dtrs, prs, pcs, pcs, st, yblk],
        out_specs=[yblk, nblk, nblk, dtrs, prs],
        out_shape=[jax.ShapeDtypeStruct((t, SSD_INNER), F32), jax.ShapeDtypeStruct((t, BC_WIDTH), F32),
                   jax.ShapeDtypeStruct((t, BC_WIDTH), F32), jax.ShapeDtypeStruct((SSD_GROUPS, 4, t), F32),
                   jax.ShapeDtypeStruct((SSD_GROUPS, 3, 256), F32)],
        scratch_shapes=[pltpu.VMEM((SSD_STATE, 256), F32)],
        compiler_params=_cp(("parallel", "arbitrary")))(xbc, xbc, xbc, dtc, dtr, prow, bcol, acol, states, dy)


GROUP_W = SSD_INNER // SSD_GROUPS


def _mix_specs(tb):
    row = pl.BlockSpec((tb, 2048), lambda i: (i, 0))
    zlo = pl.BlockSpec((tb, 1024), lambda i: (i, O_Z // 1024))
    zhi = pl.BlockSpec((tb, 1024), lambda i: (i, O_Z // 1024 + 1))
    vec = pl.BlockSpec((1, 2048), lambda i: (0, 0))
    return row, zlo, zhi, vec


def _mix_fwd(attn, y, proj, g_attn, g_ssd):
    t = attn.shape[0]
    tb = _rows(t, 256)

    def body(a_ref, y_ref, zlo_ref, zhi_ref, ga_ref, gs_ref, o_ref):
        av = a_ref[...]
        r = lax.rsqrt(jnp.mean(av * av, axis=-1, keepdims=True) + EPS)
        o_ref[:, :ATTN_WIDTH] = (av * r * ga_ref[...]).astype(BF16)
        for g in range(SSD_GROUPS):
            lo, hi = g * GROUP_W, (g + 1) * GROUP_W
            zref = zlo_ref if g < 4 else zhi_ref
            z = zref[:, lo % 1024:lo % 1024 + GROUP_W]
            yg = y_ref[:, lo:hi] * (z * _sigmoid(z))
            rg = lax.rsqrt(jnp.mean(yg * yg, axis=-1, keepdims=True) + EPS)
            o_ref[:, ATTN_WIDTH + lo:ATTN_WIDTH + hi] = (yg * rg * gs_ref[:, lo:hi]).astype(BF16)

    row, zlo, zhi, vec = _mix_specs(tb)
    return pl.pallas_call(
        body, name="mix_fwd", grid=(t // tb,), in_specs=[row, row, zlo, zhi, vec, vec],
        out_specs=pl.BlockSpec((tb, 4096), lambda i: (i, 0)), out_shape=jax.ShapeDtypeStruct((t, 4096), BF16),
        compiler_params=_cp(("parallel",)))(attn, y, proj, proj, g_attn, g_ssd)


def _mix_bwd(dmix, attn, y, proj, g_attn, g_ssd):
    t = attn.shape[0]
    tb = _rows(t, 256)

    def body(dm_ref, a_ref, y_ref, zlo_ref, zhi_ref, ga_ref, gs_ref, da_ref, dy_ref, dz_ref, dga_ref, dgs_ref):
        i = pl.program_id(0)
        av = a_ref[...]
        dn = dm_ref[:, :ATTN_WIDTH].astype(F32)
        r = lax.rsqrt(jnp.mean(av * av, axis=-1, keepdims=True) + EPS)
        u = dn * ga_ref[...]
        da_ref[...] = r * u - av * (r * r * r * jnp.mean(u * av, axis=-1, keepdims=True))
        dga = jnp.sum(dn * av * r, axis=0, keepdims=True)

        @pl.when(i == 0)
        def _():
            dga_ref[...] = dga

        @pl.when(i > 0)
        def _():
            dga_ref[...] += dga

        for g in range(SSD_GROUPS):
            lo, hi = g * GROUP_W, (g + 1) * GROUP_W
            zref = zlo_ref if g < 4 else zhi_ref
            z = zref[:, lo % 1024:lo % 1024 + GROUP_W]
            yv = y_ref[:, lo:hi]
            sg = _sigmoid(z)
            sz = z * sg
            yg = yv * sz
            rg = lax.rsqrt(jnp.mean(yg * yg, axis=-1, keepdims=True) + EPS)
            do = dm_ref[:, ATTN_WIDTH + lo:ATTN_WIDTH + hi].astype(F32)
            ug = do * gs_ref[:, lo:hi]
            dyg = rg * ug - yg * (rg * rg * rg * jnp.mean(ug * yg, axis=-1, keepdims=True))
            dy_ref[:, lo:hi] = dyg * sz
            dz_ref[:, lo:hi] = (dyg * yv * (sg * (1.0 + z * (1.0 - sg)))).astype(BF16)
            dgs = jnp.sum(do * yg * rg, axis=0, keepdims=True)

            @pl.when(i == 0)
            def _():
                dgs_ref[:, lo:hi] = dgs

            @pl.when(i > 0)
            def _():
                dgs_ref[:, lo:hi] += dgs

    row, zlo, zhi, vec = _mix_specs(tb)
    return pl.pallas_call(
        body, name="mix_bwd", grid=(t // tb,),
        in_specs=[pl.BlockSpec((tb, 4096), lambda i: (i, 0)), row, row, zlo, zhi, vec, vec],
        out_specs=[row, row, row, vec, vec],
        out_shape=[jax.ShapeDtypeStruct((t, 2048), F32), jax.ShapeDtypeStruct((t, 2048), F32),
                   jax.ShapeDtypeStruct((t, 2048), BF16), jax.ShapeDtypeStruct((1, 2048), F32),
                   jax.ShapeDtypeStruct((1, 2048), F32)],
        compiler_params=_cp(("arbitrary",)))(dmix, attn, y, proj, proj, g_attn, g_ssd)


def _adamw(w, g, m, v, name):
    r, c = w.shape
    tb = _rows(r, 256)
    c1 = 1.0 - ADAM_B1 ** ADAM_STEP
    c2 = 1.0 - ADAM_B2 ** ADAM_STEP

    def body(w_ref, g_ref, m_ref, v_ref, d_ref, m2_ref, v2_ref):
        gv = g_ref[...]
        m2 = ADAM_B1 * m_ref[...] + (1.0 - ADAM_B1) * gv
        v2 = ADAM_B2 * v_ref[...] + (1.0 - ADAM_B2) * (gv * gv)
        d_ref[...] = -ADAM_LR * ((m2 / c1) / (jnp.sqrt(v2 / c2) + ADAM_EPS) + ADAM_WD * w_ref[...])
        m2_ref[...] = m2
        v2_ref[...] = v2

    blk = pl.BlockSpec((tb, c), lambda i: (i, 0))
    shp = jax.ShapeDtypeStruct((r, c), F32)
    return pl.pallas_call(body, name=name, grid=(r // tb,), in_specs=[blk] * 4, out_specs=[blk] * 3,
                          out_shape=[shp] * 3, compiler_params=_cp(("parallel",)))(w, g, m, v)


def _adamw_halves(w, mine, theirs, m, v, pos, name, cols=False):
    r, c = w.shape
    h = r if cols else r // 2
    tb = _rows(h, 128)
    nh = h // tb
    c1 = 1.0 - ADAM_B1 ** ADAM_STEP
    c2 = 1.0 - ADAM_B2 ** ADAM_STEP

    def body(pos_ref, w_ref, a_ref, b_ref, m_ref, v_ref, g_ref, d_ref, m2_ref, v2_ref):
        which = pl.program_id(1) if cols else pl.program_id(0) // nh
        gv = jnp.where(which == pos_ref[0], a_ref[...], b_ref[...])
        m2 = ADAM_B1 * m_ref[...] + (1.0 - ADAM_B1) * gv
        v2 = ADAM_B2 * v_ref[...] + (1.0 - ADAM_B2) * (gv * gv)
        g_ref[...] = gv
        d_ref[...] = -ADAM_LR * ((m2 / c1) / (jnp.sqrt(v2 / c2) + ADAM_EPS) + ADAM_WD * w_ref[...])
        m2_ref[...] = m2
        v2_ref[...] = v2

    if cols:
        full = pl.BlockSpec((tb, c // 2), lambda i, j, pref: (i, j))
        mine_spec = theirs_spec = pl.BlockSpec((tb, c // 2), lambda i, j, pref: (i, 0))
        grid = (nh, 2)
    else:
        full = pl.BlockSpec((tb, c), lambda i, pref: (i, 0))
        mine_spec = pl.BlockSpec((tb, c), lambda i, pref: (jnp.where(i // nh == pref[0], i % nh,
                                                                     jnp.where(pref[0] == 0, nh - 1, 0)), 0))
        theirs_spec = pl.BlockSpec((tb, c), lambda i, pref: (jnp.where(i // nh != pref[0], i % nh,
                                                                       jnp.where(pref[0] == 0, 0, nh - 1)), 0))
        grid = (r // tb,)
    shp = jax.ShapeDtypeStruct((r, c), F32)
    grid_spec = pltpu.PrefetchScalarGridSpec(num_scalar_prefetch=1, grid=grid,
                                             in_specs=[full, mine_spec, theirs_spec, full, full],
                                             out_specs=[full] * 4)
    return pl.pallas_call(body, name=name, grid_spec=grid_spec, out_shape=[shp] * 4,
                          compiler_params=_cp(("parallel",) * len(grid)))(pos, w, mine, theirs, m, v)


def _sum_own_half(g4, recv, pos, name, cols=False):
    _, r, c = g4.shape
    h, c = (r, c // 2) if cols else (r // 2, c)
    tb = _rows(h, 128)
    nh = h // tb

    def slot(j, pref):
        return (pref[1] + 1 + j) % N_CHIPS

    if cols:
        own = lambda j, i, pref: (slot(j, pref), i, pref[0])
    else:
        own = lambda j, i, pref: (slot(j, pref), pref[0] * nh + i, 0)
    same = lambda j, i, pref: (slot(j, pref), i, 0)

    def body(pos_ref, a_ref, b_ref, o_ref):
        o_ref[...] = (a_ref[...] + b_ref[...]).astype(BF16)

    grid_spec = pltpu.PrefetchScalarGridSpec(
        num_scalar_prefetch=1, grid=(N_CHIPS - 1, nh),
        in_specs=[pl.BlockSpec((1, tb, c), own), pl.BlockSpec((1, tb, c), same)],
        out_specs=pl.BlockSpec((1, tb, c), same))
    return pl.pallas_call(body, name=name, grid_spec=grid_spec,
                          out_shape=jax.ShapeDtypeStruct((N_CHIPS, h, c), BF16),
                          compiler_params=_cp(("parallel", "parallel")))(pos, g4, recv)


def _sum_chips(g4, recv, parts, pos, name, cols=False):
    _, r, c = g4.shape
    h, c = (r, c // 2) if cols else (r // 2, c)
    tb = _rows(h, 128)
    nh = h // tb
    own = (lambda i, pref: (pref[1], i, pref[0])) if cols else (lambda i, pref: (pref[1], pref[0] * nh + i, 0))

    def body(pos_ref, a_ref, b_ref, p_ref, o_ref):
        own = a_ref[0] + b_ref[0]
        o_ref[...] = ((own + p_ref[0].astype(F32)) + p_ref[1].astype(F32)) + p_ref[2].astype(F32)

    grid_spec = pltpu.PrefetchScalarGridSpec(
        num_scalar_prefetch=1, grid=(nh,),
        in_specs=[pl.BlockSpec((1, tb, c), own),
                  pl.BlockSpec((1, tb, c), lambda i, pref: (pref[1], i, 0)),
                  pl.BlockSpec((3, tb, c), lambda i, pref: (0, i, 0))],
        out_specs=pl.BlockSpec((tb, c), lambda i, pref: (i, 0)))
    return pl.pallas_call(body, name=name, grid_spec=grid_spec, out_shape=jax.ShapeDtypeStruct((h, c), F32),
                          compiler_params=_cp(("parallel",)))(pos, g4, recv, parts)


def _me():
    return lax.axis_index("x"), lax.axis_index("y"), lax.axis_index("c")


def _flip(v, bit):
    return (1 - v) if bit else v


CHIP_FLIPS = [(1, 0), (0, 1), (1, 1)]


def _allgather_weights(shards, after, cols=False):
    n = len(shards)

    def body(*refs):
        ins, outs, token = refs[:n], refs[n + 1:2 * n + 1], refs[2 * n + 1]
        send_sems, recv_sems = refs[2 * n + 2:]
        x, y, c = _me()
        chip = 2 * x + y
        sib = (x, y, 1 - c)

        def remote(src, dst, k, to):
            return pltpu.make_async_remote_copy(src_ref=src, dst_ref=dst, send_sem=send_sems.at[k],
                                                recv_sem=recv_sems.at[k], device_id=to, device_id_type=MESH)

        def half(ref, which):
            if cols:
                h = ref.shape[1] // 2
                return ref.at[:, pl.ds(which * h, h)]
            h = ref.shape[0] // 2
            return ref.at[pl.ds(which * h, h)]

        sends = []
        for t in range(n):
            for k, (fx, fy) in enumerate(CHIP_FLIPS):
                cp = remote(half(ins[t], c), half(outs[t].at[chip], c), 6 * t + k, (_flip(x, fx), _flip(y, fy), c))
                cp.start()
                sends.append(cp)
        for t in range(n):
            for k, (fx, fy) in enumerate(CHIP_FLIPS):
                landed = half(outs[t].at[2 * _flip(x, fx) + _flip(y, fy)], c)
                remote(landed, landed, 6 * t + k, (x, y, c)).wait_recv()
                fw = remote(landed, landed, 6 * t + 3 + k, sib)
                fw.start()
                sends.append(fw)
        for t in range(n):
            for k, (fx, fy) in enumerate(CHIP_FLIPS):
                got = half(outs[t].at[2 * _flip(x, fx) + _flip(y, fy)], 1 - c)
                remote(got, got, 6 * t + 3 + k, (x, y, c)).wait_recv()
        for cp in sends:
            cp.wait_send()
        token[...] = jnp.zeros_like(token)

    outs = pl.pallas_call(
        body, name="allgather_weights", in_specs=[HBM_SPEC] * n + [pl.BlockSpec(memory_space=pl.ANY)],
        out_specs=[HBM_SPEC] * n + [pl.BlockSpec(memory_space=pltpu.VMEM)],
        out_shape=[jax.ShapeDtypeStruct((N_CHIPS,) + s.shape, s.dtype) for s in shards] + [TOKEN],
        scratch_shapes=[pltpu.SemaphoreType.DMA((6 * n,)), pltpu.SemaphoreType.DMA((6 * n,))],
        compiler_params=pltpu.CompilerParams(has_side_effects=True))(*shards, after)
    return list(outs[:n]), outs[n]


def _share_halves(ghs, name):
    n = len(ghs)

    def body(*refs):
        ins, outs = refs[:n], refs[n:2 * n]
        send_sems, recv_sems = refs[2 * n:]
        x, y, c = _me()
        cps = []
        for t in range(n):
            cp = pltpu.make_async_remote_copy(
                src_ref=ins[t], dst_ref=outs[t], send_sem=send_sems.at[t], recv_sem=recv_sems.at[t],
                device_id=(x, y, 1 - c), device_id_type=MESH)
            cp.start()
            cps.append(cp)
        for cp in cps:
            cp.wait()

    return pl.pallas_call(
        body, name=name, in_specs=[HBM_SPEC] * n, out_specs=[HBM_SPEC] * n,
        out_shape=[jax.ShapeDtypeStruct(g.shape, g.dtype) for g in ghs],
        scratch_shapes=[pltpu.SemaphoreType.DMA((n,)), pltpu.SemaphoreType.DMA((n,))],
        compiler_params=pltpu.CompilerParams(has_side_effects=True))(*ghs)


SEM_SPEC = pl.BlockSpec(memory_space=pltpu.SEMAPHORE)
ANY_SPEC = pl.BlockSpec(memory_space=pl.ANY)
DATAFLOW = pltpu.SideEffectType.DATAFLOW_SIDE_EFFECTING


def _in_hbm(a):
    return pltpu.with_memory_space_constraint(a, pltpu.HBM)


def _push_start(srcs, land_shapes, route, peers, name):
    n, npeer = len(srcs), len(peers)
    lands = [lax.empty(shp, s.dtype) for shp, s in zip(land_shapes, srcs)]

    def body(*refs):
        ins, lnd = refs[:n], refs[n:2 * n]
        send_sems, recv_sems = refs[2 * n], refs[2 * n + 1]
        token = refs[-1]
        x, y, c = _me()
        for t in range(n):
            for k, (fx, fy, fc) in enumerate(peers):
                src, dst = route(ins[t], lnd[t], k, x, y, c)
                pltpu.make_async_remote_copy(
                    src_ref=src, dst_ref=dst, send_sem=send_sems.at[npeer * t + k],
                    recv_sem=recv_sems.at[npeer * t + k],
                    device_id=(_flip(x, fx), _flip(y, fy), _flip(c, fc)), device_id_type=MESH).start()
        token[...] = jnp.zeros_like(token)

    bufs = [_in_hbm(a) for a in list(srcs) + lands]
    outs = pl.pallas_call(
        body, name=name,
        out_shape=(pltpu.SemaphoreType.DMA((npeer * n,)), pltpu.SemaphoreType.DMA((npeer * n,)),
                   *[pltpu.HBM(b.shape, b.dtype) for b in bufs], TOKEN),
        in_specs=[HBM_SPEC] * (2 * n),
        out_specs=(SEM_SPEC, SEM_SPEC, *[HBM_SPEC] * (2 * n), pl.BlockSpec(memory_space=pltpu.VMEM)),
        input_output_aliases={i: 2 + i for i in range(2 * n)},
        compiler_params=pltpu.CompilerParams(has_side_effects=DATAFLOW))(*bufs)
    return outs[0], outs[1], list(outs[2:2 + n]), list(outs[2 + n:2 + 2 * n]), outs[-1]


def _push_wait(send_sems, recv_sems, srcs, lands, after, route, peers, name):
    n, npeer = len(srcs), len(peers)

    def body(*refs):
        ins, lnd = refs[:n], refs[n:2 * n]
        ssem, rsem = refs[2 * n], refs[2 * n + 1]
        x, y, c = _me()
        for t in range(n):
            for k, (fx, fy, fc) in enumerate(peers):
                src, dst = route(ins[t], lnd[t], k, x, y, c)
                cp = pltpu.make_async_remote_copy(
                    src_ref=src, dst_ref=dst, send_sem=ssem.at[npeer * t + k], recv_sem=rsem.at[npeer * t + k],
                    device_id=(_flip(x, fx), _flip(y, fy), _flip(c, fc)), device_id_type=MESH)
                cp.wait_send()
                cp.wait_recv()

    bufs = list(srcs) + list(lands)
    outs = pl.pallas_call(
        body, name=name, out_shape=tuple(pltpu.HBM(b.shape, b.dtype) for b in bufs),
        in_specs=[HBM_SPEC] * (2 * n) + [SEM_SPEC, SEM_SPEC, ANY_SPEC], out_specs=tuple([HBM_SPEC] * (2 * n)),
        input_output_aliases={i: i for i in range(2 * n)},
        compiler_params=pltpu.CompilerParams(has_side_effects=DATAFLOW))(*bufs, send_sems, recv_sems, after)
    return list(outs[:n]), list(outs[n:])


OTHER_CHIPS = [(fx, fy, 0) for fx, fy in CHIP_FLIPS]
SIBLING = [(0, 0, 1)]


def _route_gather(src, land, k, x, y, c):
    return src, land.at[2 * x + y]


def _route_gather_wait(src, land, k, x, y, c):
    fx, fy = CHIP_FLIPS[k]
    return src, land.at[2 * _flip(x, fx) + _flip(y, fy)]


def _route_scatter(src, land, k, x, y, c):
    fx, fy = CHIP_FLIPS[k]
    return src.at[2 * _flip(x, fx) + _flip(y, fy)], land.at[k]


def _route_exchange(src, land, k, x, y, c):
    h = land.shape[1]
    return src.at[:, pl.ds((1 - c) * h, h)], land


def _route_exchange_cols(src, land, k, x, y, c):
    h = land.shape[2]
    return src.at[:, :, pl.ds((1 - c) * h, h)], land


def _allreduce_small(v):
    r = v.shape[0]

    def body(v_ref, o_ref, buf, send_sems, recv_sems):
        x, y, c = _me()
        me = 4 * x + 2 * y + c
        buf[0] = v_ref[...]
        cps = []
        for k in range(1, 8):
            kx, ky, kc = (k >> 2) & 1, (k >> 1) & 1, k & 1
            cp = pltpu.make_async_remote_copy(
                src_ref=v_ref, dst_ref=buf.at[k], send_sem=send_sems.at[k - 1], recv_sem=recv_sems.at[k - 1],
                device_id=(_flip(x, kx), _flip(y, ky), _flip(c, kc)), device_id_type=MESH)
            cp.start()
            cps.append(cp)
        for cp in cps:
            cp.wait()
        acc = buf[me]
        for d in range(1, 8):
            acc = acc + buf[jnp.bitwise_xor(me, d)]
        o_ref[...] = acc

    vm = pl.BlockSpec(memory_space=pltpu.VMEM)
    return pl.pallas_call(
        body, name="allreduce_small", in_specs=[vm], out_specs=vm, out_shape=jax.ShapeDtypeStruct(v.shape, F32),
        scratch_shapes=[pltpu.VMEM((8, r, LANES), F32), pltpu.SemaphoreType.DMA((7,)),
                        pltpu.SemaphoreType.DMA((7,))],
        compiler_params=pltpu.CompilerParams(has_side_effects=True, vmem_limit_bytes=VMEM_LIMIT))(v)


def _grad_exchange_start(g4, tag, cols=False):
    land = (N_CHIPS, g4.shape[1], g4.shape[2] // 2) if cols else (N_CHIPS, g4.shape[1] // 2, g4.shape[2])
    route = _route_exchange_cols if cols else _route_exchange
    send_sems, recv_sems, srcs, lands, token = _push_start(
        [g4], [land], route, SIBLING, name="grad_exchange_start_" + tag)
    return (send_sems, recv_sems, srcs, lands, tag, cols), token


def _grad_scatter_start(state, pos, after):
    send_sems, recv_sems, srcs, lands, tag, cols = state
    route = _route_exchange_cols if cols else _route_exchange
    (g4,), (recv,) = _push_wait(send_sems, recv_sems, srcs, lands, after, route, SIBLING,
                                name="grad_exchange_wait_" + tag)
    return _grad_pair_scatter(g4, recv, pos, tag, cols)


def _grad_pair_scatter(g4, recv, pos, tag, cols=False):
    p16 = _sum_own_half(g4, recv, pos, name="grad_sum_pair_" + tag, cols=cols)
    send_sems, recv_sems, srcs, lands, token = _push_start(
        [p16], [(3,) + p16.shape[1:]], _route_scatter, OTHER_CHIPS, name="grad_scatter_start_" + tag)
    return (g4, recv, send_sems, recv_sems, srcs, lands, tag, cols), token


def _grad_reduce_finish(state, pos, after):
    g4, recv, send_sems, recv_sems, srcs, lands, tag, cols = state
    parts = _push_wait(send_sems, recv_sems, srcs, lands, after, _route_scatter, OTHER_CHIPS,
                       name="grad_scatter_wait_" + tag)[1][0]
    mine = _sum_chips(g4, recv, parts, pos, name="grad_sum_chips_" + tag, cols=cols)
    return mine, _share_halves([mine], name="grad_share_halves_" + tag)[0]


def _local_step(x, tgt, p, w_in_t, w_in_dt, hooks):
    t = x.shape[0]
    tables = _rope_tables(t)
    sinks = p['sinks'].reshape(N_Q_HEADS)

    def told(name, value):
        return tuple(hooks.grad_ready(name, value))

    xn = _rmsnorm_fwd(x, p['norm_mix'], "norm_mix_fwd", deps=hooks.first_deps)
    proj = _matmul(xn, w_in_t, mode='nt', name="in_proj", n_limit=MAIN_WIDTH)
    dt_raw = _matmul(xn, w_in_dt, mode='nt', name="in_proj_dt")[:, :SSD_HEADS]
    attn = _attn_fwd(proj, sinks, tables)
    conv_b = p['ssd_conv_b']
    xbc = _conv_fwd(proj, p['ssd_conv_w'], conv_b, col0=O_XBC, width=CONV_CH, act=True, name="ssd_conv_fwd")
    sp = _ssd_params(dt_raw, p['dt_bias'].reshape(-1), p['a_log'].reshape(-1), p['ssd_d'].reshape(-1))
    y, states = _ssd_fwd(xbc, sp)
    mix = _mix_fwd(attn, y, proj, p['attn_out_norm'], p['ssd_norm'])
    w_out = hooks.weight('w_out', mix)
    h1 = _matmul(mix, w_out, mode='nn', name="out_proj", add=x)
    hn = _rmsnorm_fwd(h1, p['norm_ffn'], "norm_ffn_fwd")
    w_up = hooks.weight('w_up', hn)
    u0 = _matmul(hn, w_up, mode='nn', name="ffn_up", b_owner=True, tn=1408)
    a = _ffn_act_fwd(u0, p['ffn_conv_w'], p['ffn_conv_b'])
    w_down = hooks.weight('w_down', a)
    h2 = _matmul(a, w_down, mode='nn', name="ffn_down", add=h1, tk=2816)
    loss, dh2, dh2_16, g_norm_final = _final_loss(h2, p['norm_final'].reshape(1, D_MODEL), tgt)

    g = {}
    da = _matmul(dh2_16, w_down, mode='nt', name="ffn_down_dx", out_dtype=BF16, tn=1408)
    g['w_down'] = _matmul(a, dh2_16, mode='tn', name="ffn_down_dw", tm=1408)
    dep = told('w_down', g['w_down'])
    du0, dcw, dcb = _ffn_act_bwd(u0, p['ffn_conv_w'], p['ffn_conv_b'], da)
    g['ffn_conv_w'] = dcw.transpose(1, 0, 2).reshape(FFN_CONV, 2 * D_FF)
    g['ffn_conv_b'] = dcb.transpose(1, 0, 2).reshape(1, 2 * D_FF)
    g['w_up'] = _matmul(hn, du0, mode='tn', name="ffn_up_dw", deps=dep, b_halves=True, owner_major=True,
                        tn=1408)
    dep = told('w_up', g['w_up'])
    dhn = _matmul(du0, w_up, mode='nt', name="ffn_up_dx", out_dtype=BF16, deps=dep, a_halves=True,
                  b_owner=True, tk=2816)
    dh1, dh1_16, g['norm_ffn'] = _rmsnorm_bwd(h1, p['norm_ffn'], dhn, dh2, "norm_ffn_bwd")

    g['w_out'] = _matmul(mix, dh1_16, mode='tn', name="out_proj_dw")
    dep = told('w_out', g['w_out'])
    dmix = _matmul(dh1_16, w_out, mode='nt', name="out_proj_dx", out_dtype=BF16, deps=dep)
    dattn, dy, dz, g['attn_out_norm'], g['ssd_norm'] = _mix_bwd(dmix, attn, y, proj, p['attn_out_norm'],
                                                                p['ssd_norm'])
    dq, dk, dv, dsink = _attn_bwd(proj, sinks, tables, dattn)
    g['sinks'] = dsink[:, :, 0].reshape(1, N_Q_HEADS)
    dxs, dbm, dcm, ddt8, dpar = _ssd_bwd(xbc, sp, states, dy)
    dpar = dpar[:, :, ::SSD_HEAD_DIM]
    g['dt_bias'] = dpar[:, 0, :].reshape(1, SSD_HEADS)
    g['a_log'] = dpar[:, 1, :].reshape(1, SSD_HEADS)
    g['ssd_d'] = dpar[:, 2, :].reshape(1, SSD_HEADS)
    dxbc_act = jnp.concatenate([dxs, dbm, dcm], axis=1)
    dconv = _conv_silu_dact(proj, p['ssd_conv_w'], conv_b, dxbc_act, col0=O_XBC, width=CONV_CH,
                            name="ssd_conv_dact")
    dxbc, g['ssd_conv_w'], g['ssd_conv_b'] = _conv_bwd(proj, p['ssd_conv_w'], dconv, col0=O_XBC, width=CONV_CH,
                                                       name="ssd_conv_bwd")
    dproj = jnp.concatenate([dq, dk, dv, dz, dxbc], axis=1)
    ddt = ddt8.transpose(2, 0, 1).reshape(t, SSD_HEADS)
    ddt_pad = jnp.pad(ddt, ((0, 0), (0, LANES - SSD_HEADS))).astype(BF16)
    g['w_in'] = (_matmul(dproj, xn, mode='tn', name="in_proj_dw", m_rows=IN_PROJ_WIDTH),
                 _matmul(ddt_pad, xn, mode='tn', name="in_proj_dt_dw"))
    dep = told('w_in', g['w_in'])
    dxn_dt = _matmul(ddt_pad, w_in_dt, mode='nn', name="in_proj_dt_dx", deps=dep)
    dxn = _matmul(dproj, w_in_t, mode='nn', name="in_proj_dx", out_dtype=BF16, add=dxn_dt, k_limit=MAIN_WIDTH,
                  tk=2304)
    dx, _, g['norm_mix'] = _rmsnorm_bwd(x, p['norm_mix'], dxn, dh1, "norm_mix_bwd", deps=tuple(hooks.flush(dxn)))
    g['norm_final'] = g_norm_final
    return loss, dx, g


def _pack(arrs):
    flat = jnp.concatenate([a.reshape(-1) for a in arrs])
    n = flat.shape[0]
    rows = -(-n // LANES)
    rows = -(-rows // 8) * 8
    return jnp.pad(flat, (0, rows * LANES - n)).reshape(rows, LANES)


def _unpack(packed, shapes):
    flat = packed.reshape(-1)
    out, off = [], 0
    for s in shapes:
        n = 1
        for d in s:
            n *= d
        out.append(flat[off:off + n].reshape(s))
        off += n
    return out


class _StepHooks:
    def __init__(self, first_deps, weight, grad_ready, flush):
        self.first_deps = first_deps
        self.weight = weight
        self.grad_ready = grad_ready
        self.flush = flush


def kernel(x, norm_mix, w_in, sinks, attn_out_norm, ssd_conv_w, ssd_conv_b, dt_bias, a_log, ssd_d, ssd_norm, w_out, norm_ffn, w_up, ffn_conv_w, ffn_conv_b, w_down, norm_final, loss_target, m_norm_mix, m_w_in, m_sinks, m_attn_out_norm, m_ssd_conv_w, m_ssd_conv_b, m_dt_bias, m_a_log, m_ssd_d, m_ssd_norm, m_w_out, m_norm_ffn, m_w_up, m_ffn_conv_w, m_ffn_conv_b, m_w_down, m_norm_final, v_norm_mix, v_w_in, v_sinks, v_attn_out_norm, v_ssd_conv_w, v_ssd_conv_b, v_dt_bias, v_a_log, v_ssd_d, v_ssd_norm, v_w_out, v_norm_ffn, v_w_up, v_ffn_conv_w, v_ffn_conv_b, v_w_down, v_norm_final):
    args = dict(locals())
    w = {n: args[n] for n in WEIGHTS}
    m = {n: args['m_' + n] for n in WEIGHTS}
    v = {n: args['v_' + n] for n in WEIGHTS}
    xi, yi, ci = _me()
    chip = 2 * xi + yi
    pos = jnp.stack([ci, chip]).astype(jnp.int32)

    def place(shard, full_cols):
        z = jnp.zeros((shard.shape[0], full_cols), F32)
        return lax.dynamic_update_slice(z, shard * 0.5, (0, chip * shard.shape[1]))

    conv_pack = _pack([place(ssd_conv_w[0], CONV_CH), place(ffn_conv_w[0], 2 * D_FF)])
    conv_full = _allreduce_small(conv_pack)
    ssd_conv_w_full, ffn_conv_w_full = _unpack(conv_full, [(SSD_CONV, CONV_CH), (FFN_CONV, 2 * D_FF)])

    w_in_t, m_in_t, v_in_t = (jnp.transpose(a[0]) for a in (w_in, m_w_in, v_w_in))
    in_shard = w_in_t.astype(BF16)
    (gathered,), order = _allgather_weights([in_shard], conv_full, cols=True)
    full_in_t = lax.dynamic_update_slice(gathered, in_shard[None], (chip, 0, 0)).reshape(IN_PROJ_WIDTH, D_MODEL)
    w_in_dt = jnp.pad(full_in_t[MAIN_WIDTH:], ((0, LANES - SSD_HEADS), (0, 0)))
    gathers = {}
    order = order[:1, :1]
    for n, shard in (('w_out', w_out[0]), ('w_up', w_up[0]), ('w_down', w_down[0])):
        shard = (shard + order).astype(BF16)
        gathers[n] = _push_start([shard], [(N_CHIPS,) + shard.shape], _route_gather, OTHER_CHIPS,
                                 name="gather_start_" + n)
        order = gathers[n][4][:1, :1]
    first_deps = [gathers['w_down'][4]]

    def weight(name, after):
        send_sems, recv_sems, srcs, lands, _ = gathers[name]
        (own,), (got,) = _push_wait(send_sems, recv_sems, srcs, lands, after, _route_gather_wait, OTHER_CHIPS,
                                    name="gather_wait_" + name)
        whole = lax.dynamic_update_slice(got, own[None], (chip, 0, 0))
        return whole if name == 'w_up' else whole.reshape(-1, D_MODEL)

    reductions, exchanging = {}, {}

    def flush(after):
        tokens = []
        for prev in list(exchanging):
            reductions[prev], token = _grad_scatter_start(exchanging.pop(prev), pos, after)
            tokens.append(token)
        return tokens

    def grad_ready(name, value):
        if name == 'w_in':
            main, dtp = value
            value = lax.dynamic_update_slice(main, dtp[:SSD_HEADS], (MAIN_WIDTH, 0))
        g4 = value if value.ndim == 3 else value.reshape(N_CHIPS, -1, value.shape[1])
        tokens = flush(g4)
        exchanging[name], token = _grad_exchange_start(g4, name, cols=(name == 'w_in'))
        return tokens + [token]

    small = {
        'norm_mix': norm_mix, 'sinks': sinks, 'attn_out_norm': attn_out_norm, 'ssd_conv_w': ssd_conv_w_full,
        'ssd_conv_b': ssd_conv_b, 'dt_bias': dt_bias, 'a_log': a_log, 'ssd_d': ssd_d, 'ssd_norm': ssd_norm,
        'norm_ffn': norm_ffn, 'ffn_conv_w': ffn_conv_w_full, 'ffn_conv_b': ffn_conv_b, 'norm_final': norm_final,
    }
    loss, dx, g = _local_step(x[0], loss_target[0], small, full_in_t, w_in_dt,
                              _StepHooks(tuple(first_deps), weight, grad_ready, flush))

    small_names = [n for n in WEIGHTS if n not in BIG]
    small_g = [loss[:, :1]] + [g[n] for n in small_names]
    small_shapes = [(1, 1)] + [tuple(a.shape) for a in small_g[1:]]
    red = _unpack(_allreduce_small(_pack(small_g)), small_shapes)
    loss_out = red[0].reshape(())
    gsm = dict(zip(small_names, red[1:]))
    gsm['ssd_conv_w'] = lax.dynamic_slice(gsm['ssd_conv_w'], (0, chip * ssd_conv_w.shape[2]),
                                          (SSD_CONV, ssd_conv_w.shape[2]))
    gsm['ffn_conv_w'] = lax.dynamic_slice(gsm['ffn_conv_w'], (0, chip * ffn_conv_w.shape[2]),
                                          (FFN_CONV, ffn_conv_w.shape[2]))

    grads, deltas, new_m, new_v = {}, {}, {}, {}
    after = dx
    for n in ('w_down', 'w_up', 'w_out', 'w_in'):
        mine, theirs = _grad_reduce_finish(reductions[n], pos, after)
        if n == 'w_in':
            outs = _adamw_halves(w_in_t, mine, theirs, m_in_t, v_in_t, pos, name="adamw_" + n, cols=True)
            outs = [jnp.transpose(o) for o in outs]
        else:
            outs = _adamw_halves(w[n][0], mine, theirs, m[n][0], v[n][0], pos, name="adamw_" + n)
        after = outs[1]
        grads[n], deltas[n], new_m[n], new_v[n] = [o[None] for o in outs]
    shapes = [tuple(w[n].shape) for n in small_names]
    gp = _pack([gsm[n] for n in small_names])
    d, m2, v2 = _adamw(_pack([w[n] for n in small_names]), gp, _pack([m[n] for n in small_names]),
                       _pack([v[n] for n in small_names]), name="adamw_small")
    for n, gg, dd, mm, vv in zip(small_names, _unpack(gp, shapes), _unpack(d, shapes), _unpack(m2, shapes),
                                 _unpack(v2, shapes)):
        grads[n], deltas[n], new_m[n], new_v[n] = gg, dd, mm, vv

    return (loss_out, dx[None], *[grads[n] for n in WEIGHTS], *[deltas[n] for n in WEIGHTS],
            *[new_m[n] for n in WEIGHTS], *[new_v[n] for n in WEIGHTS])
```

```python
import functools

import jax
import jax.numpy as jnp
from jax import lax
from jax.experimental import pallas as pl
from jax.experimental.pallas import tpu as pltpu

F32 = jnp.float32
BF16 = jnp.bfloat16

D_MODEL = 2048
N_Q_HEADS = 32
N_KV_HEADS = 8
HEAD_DIM = 64
WINDOW = 128
ATTN_BLOCK = 128
ROT_DIM = 16
ROPE_THETA = 500000.0
SSD_HEADS = 32
SSD_HEAD_DIM = 64
SSD_INNER = 2048
SSD_GROUPS = 8
SSD_STATE = 128
SSD_CONV = 4
SSD_CHUNK = 128
ATTN_WIDTH = 2048
KV_WIDTH = 512
BC_WIDTH = 1024
CONV_CH = 4096
IN_PROJ_WIDTH = 9248
MAIN_WIDTH = 9216
D_FF = 5632
FFN_CONV = 3
EPS = 1e-6
O_Q, O_K, O_V, O_Z, O_XBC, O_DT = 0, 2048, 2560, 3072, 5120, 9216

ADAM_LR = 0.001
ADAM_B1 = 0.9
ADAM_B2 = 0.999
ADAM_EPS = 1e-08
ADAM_WD = 0.01
ADAM_STEP = 10

N_CHIPS = 4
NEG = -1e30
LANES = 128
VMEM_LIMIT = 48 * 1024 * 1024
MESH = pl.DeviceIdType.MESH
HBM_SPEC = pl.BlockSpec(memory_space=pltpu.HBM)
TOKEN = jax.ShapeDtypeStruct((8, LANES), F32)

WEIGHTS = ['norm_mix', 'w_in', 'sinks', 'attn_out_norm', 'ssd_conv_w', 'ssd_conv_b', 'dt_bias', 'a_log', 'ssd_d',
           'ssd_norm', 'w_out', 'norm_ffn', 'w_up', 'ffn_conv_w', 'ffn_conv_b', 'w_down', 'norm_final']
BIG = ['w_in', 'w_out', 'w_up', 'w_down']


def _cp(sem=None, vmem=VMEM_LIMIT):
    kw = {'vmem_limit_bytes': vmem}
    if sem is not None:
        kw['dimension_semantics'] = sem
    return pltpu.CompilerParams(**kw)


def _tile(n, pref):
    if n <= pref:
        return n
    t = (pref // LANES) * LANES
    while t > LANES and n % t:
        t -= LANES
    assert n % t == 0, (n, pref)
    return t


def _rows(n, pref):
    t = min(n, pref)
    while n % t:
        t -= 8
    if 4 * t < pref:
        t = pref
        while n % t:
            t += 8
    return t


def _iota(shape, dim):
    return lax.broadcasted_iota(jnp.int32, shape, dim)


def _dot(a, b, mode='nn'):
    dn = {'nn': (((1,), (0,)), ((), ())), 'nt': (((1,), (1,)), ((), ())), 'tn': (((0,), (0,)), ((), ()))}[mode]
    return lax.dot_general(a.astype(BF16), b.astype(BF16), dn, preferred_element_type=F32)


def _dot_exact(a, b):
    return lax.dot_general(a, b, (((1,), (0,)), ((), ())), precision=lax.Precision.HIGHEST,
                           preferred_element_type=F32)


def _sigmoid(x):
    return 1.0 / (1.0 + jnp.exp(-x))


def _softplus(x):
    return jnp.maximum(x, 0.0) + jnp.log(1.0 + jnp.exp(-jnp.abs(x)))


def _matmul(a, b, *, mode, name, out_dtype=F32, add=None, deps=(), tm=1024, tn=1024, tk=2048,
            a_halves=False, b_halves=False, b_owner=False, owner_major=False, n_limit=None, k_limit=None,
            m_rows=None):
    ash, bsh = (a.shape[1:] if a_halves else a.shape), (b.shape[1:] if (b_halves or b_owner) else b.shape)
    if mode == 'nn':
        (m, k), (k2, n) = ash, bsh
    elif mode == 'nt':
        (m, k), (n, k2) = ash, bsh
    else:
        (k, m), (k2, n) = ash, bsh
    if n_limit is not None:
        assert mode == 'nt' and n_limit <= n
        n = n_limit
    if k_limit is not None:
        assert mode == 'nn' and k_limit <= k2
        k2 = k_limit
    if a_halves:
        assert mode == 'nt'
        k = 2 * k
    if b_halves:
        assert mode == 'tn'
        n = 2 * n
    if b_owner:
        assert mode in ('nn', 'nt')
        if mode == 'nn':
            n = 4 * n
        else:
            k2 = 4 * k2
    assert k == k2, (a.shape, b.shape, mode)
    tm = _tile(m, tm)
    tn = _tile(n // 4 if (owner_major or (b_owner and mode == 'nn')) else (n // 2 if b_halves else n), tn)
    tk = _tile(k // 4 if (b_owner and mode == 'nt') else (k // 2 if a_halves else k), tk)
    nk = k // tk
    has_add = add is not None
    assert not (has_add and owner_major)

    def body(*refs):
        a_ref, b_ref = refs[:2]
        add_ref = refs[2] if has_add else None

        def finish(r, o_ref):
            if has_add:
                r = r + add_ref[...].astype(F32)
            o_ref[...] = r.astype(out_dtype)

        if nk == 1:
            finish(_dot(a_ref[...], b_ref[...], mode), refs[-1])
            return
        o_ref, acc = refs[-2:]
        kk = pl.program_id(2)

        @pl.when(kk == 0)
        def _():
            acc[...] = _dot(a_ref[...], b_ref[...], mode)

        @pl.when((kk > 0) & (kk < nk - 1))
        def _():
            acc[...] += _dot(a_ref[...], b_ref[...], mode)

        @pl.when(kk == nk - 1)
        def _():
            finish(acc[...] + _dot(a_ref[...], b_ref[...], mode), o_ref)

    if mode == 'tn':
        a_spec = pl.BlockSpec((tk, tm), lambda i, j, kk: (kk, i))
    elif a_halves:
        nkh = nk // 2
        a_spec = pl.BlockSpec((None, tm, tk), lambda i, j, kk: (kk // nkh, i, kk % nkh))
    else:
        a_spec = pl.BlockSpec((tm, tk), lambda i, j, kk: (i, kk))
    if mode == 'nt' and b_owner:
        nkq = nk // 4
        b_spec = pl.BlockSpec((None, tn, tk), lambda i, j, kk: (kk // nkq, j, kk % nkq))
    elif mode == 'nt':
        b_spec = pl.BlockSpec((tn, tk), lambda i, j, kk: (j, kk))
    elif b_owner:
        njq = (n // 4) // tn
        b_spec = pl.BlockSpec((None, tk, tn), lambda i, j, kk: (j // njq, kk, j % njq))
    elif b_halves:
        njh = (n // 2) // tn
        b_spec = pl.BlockSpec((None, tk, tn), lambda i, j, kk: (j // njh, kk, j % njh))
    else:
        b_spec = pl.BlockSpec((tk, tn), lambda i, j, kk: (kk, j))
    if owner_major:
        njo = (n // 4) // tn
        o_spec = pl.BlockSpec((None, tm, tn), lambda i, j, kk: (j // njo, i, j % njo))
        out_shape = jax.ShapeDtypeStruct((N_CHIPS, m, n // 4), out_dtype)
    else:
        o_spec = pl.BlockSpec((tm, tn), lambda i, j, kk: (i, j))
        out_shape = jax.ShapeDtypeStruct((m if m_rows is None else m_rows, n), out_dtype)
    dep_spec = pl.BlockSpec((8, LANES), lambda i, j, kk: (0, 0))
    in_specs = [a_spec, b_spec] + ([pl.BlockSpec((tm, tn), lambda i, j, kk: (i, j))] if has_add else [])
    in_specs += [dep_spec] * len(deps)
    args = (a, b) + ((add,) if has_add else ()) + tuple(deps)
    return pl.pallas_call(
        body, name=name, grid=(m // tm, n // tn, nk), in_specs=in_specs, out_specs=o_spec, out_shape=out_shape,
        scratch_shapes=[pltpu.VMEM((tm, tn), F32)] if nk > 1 else [],
        compiler_params=_cp(("parallel", "parallel", "arbitrary")))(*args)


def _rmsnorm_fwd(x, g, name, deps=()):
    t, d = x.shape
    tb = _rows(t, 256)

    def body(x_ref, g_ref, *rest):
        o_ref = rest[-1]
        xv = x_ref[...]
        r = lax.rsqrt(jnp.mean(xv * xv, axis=-1, keepdims=True) + EPS)
        o_ref[...] = (xv * r * g_ref[...]).astype(BF16)

    dep_spec = pl.BlockSpec((8, LANES), lambda i: (0, 0))
    return pl.pallas_call(
        body, name=name, grid=(t // tb,),
        in_specs=[pl.BlockSpec((tb, d), lambda i: (i, 0)), pl.BlockSpec((1, d), lambda i: (0, 0))]
        + [dep_spec] * len(deps),
        out_specs=pl.BlockSpec((tb, d), lambda i: (i, 0)), out_shape=jax.ShapeDtypeStruct((t, d), BF16),
        compiler_params=_cp(("parallel",)))(x, g, *deps)


def _rmsnorm_bwd(x, g, dy, res, name, deps=()):
    t, d = x.shape
    tb = _rows(t, 256)

    def body(x_ref, g_ref, dy_ref, res_ref, *rest):
        dx_ref, dx16_ref, dg_ref = rest[-3:]
        i = pl.program_id(0)
        xv = x_ref[...]
        dyv = dy_ref[...].astype(F32)
        r = lax.rsqrt(jnp.mean(xv * xv, axis=-1, keepdims=True) + EPS)
        u = dyv * g_ref[...]
        dx = r * u - xv * (r * r * r * jnp.mean(u * xv, axis=-1, keepdims=True)) + res_ref[...]
        dx_ref[...] = dx
        dx16_ref[...] = dx.astype(BF16)
        part = jnp.sum(dyv * xv * r, axis=0, keepdims=True)

        @pl.when(i == 0)
        def _():
            dg_ref[...] = part

        @pl.when(i > 0)
        def _():
            dg_ref[...] += part

    row = pl.BlockSpec((tb, d), lambda i: (i, 0))
    vec = pl.BlockSpec((1, d), lambda i: (0, 0))
    return pl.pallas_call(
        body, name=name, grid=(t // tb,),
        in_specs=[row, vec, row, row] + [pl.BlockSpec((8, LANES), lambda i: (0, 0))] * len(deps),
        out_specs=[row, row, vec],
        out_shape=[jax.ShapeDtypeStruct((t, d), F32), jax.ShapeDtypeStruct((t, d), BF16),
                   jax.ShapeDtypeStruct((1, d), F32)],
        compiler_params=_cp(("arbitrary",)))(x, g, dy, res, *deps)


def _final_loss(h, g, tgt):
    t, d = h.shape
    tb = _rows(t, 256)

    def body(h_ref, g_ref, t_ref, loss_ref, dh_ref, dh16_ref, dg_ref):
        i = pl.program_id(0)
        hv = h_ref[...]
        gv = g_ref[...]
        r = lax.rsqrt(jnp.mean(hv * hv, axis=-1, keepdims=True) + EPS)
        y = hv * r * gv
        diff = y - t_ref[...]
        lpart = jnp.sum(jnp.sum(diff * diff, axis=1, keepdims=True), axis=0, keepdims=True) * (0.5 / d)
        dy = diff * (1.0 / d)
        u = dy * gv
        dh = r * u - hv * (r * r * r * jnp.mean(u * hv, axis=-1, keepdims=True))
        dh_ref[...] = dh
        dh16_ref[...] = dh.astype(BF16)
        gpart = jnp.sum(dy * hv * r, axis=0, keepdims=True)
        lrow = jnp.broadcast_to(lpart, (1, LANES))

        @pl.when(i == 0)
        def _():
            loss_ref[...] = lrow
            dg_ref[...] = gpart

        @pl.when(i > 0)
        def _():
            loss_ref[...] += lrow
            dg_ref[...] += gpart

    row = pl.BlockSpec((tb, d), lambda i: (i, 0))
    vec = pl.BlockSpec((1, d), lambda i: (0, 0))
    return pl.pallas_call(
        body, name="final_loss", grid=(t // tb,), in_specs=[row, vec, row],
        out_specs=[pl.BlockSpec((1, LANES), lambda i: (0, 0)), row, row, vec],
        out_shape=[jax.ShapeDtypeStruct((1, LANES), F32), jax.ShapeDtypeStruct((t, d), F32),
                   jax.ShapeDtypeStruct((t, d), BF16), jax.ShapeDtypeStruct((1, d), F32)],
        compiler_params=_cp(("arbitrary",)))(h, g, tgt)


def _rope_tables(t):
    pos = jnp.arange(t, dtype=F32)
    inv = 1.0 / (ROPE_THETA ** (jnp.arange(0, ROT_DIM, 2, dtype=F32) / ROT_DIM))
    ang = pos[:, None] * inv[None, :]
    cos, sin = jnp.cos(ang), jnp.sin(ang)
    half = ROT_DIM // 2
    rest = HEAD_DIM - ROT_DIM
    c = jnp.concatenate([cos, cos, jnp.ones((t, rest), F32)], axis=1)
    s1 = jnp.concatenate([-sin, jnp.zeros((t, half + rest), F32)], axis=1)
    s2 = jnp.concatenate([jnp.zeros((t, half), F32), sin, jnp.zeros((t, rest), F32)], axis=1)
    return tuple(jnp.tile(v, (1, LANES // HEAD_DIM)) for v in (c, s1, s2))


def _rope(x, c, s1, s2):
    half = ROT_DIM // 2
    return x * c + pltpu.roll(x, LANES - half, 1) * s1 + pltpu.roll(x, half, 1) * s2


def _rope_t(g, c, s1, s2):
    half = ROT_DIM // 2
    return g * c + pltpu.roll(g * s1, half, 1) + pltpu.roll(g * s2, LANES - half, 1)


def _band_masks(i, heads):
    n = heads * ATTN_BLOCK
    q = jnp.bitwise_and(_iota((n, ATTN_BLOCK), 0), ATTN_BLOCK - 1)
    j = _iota((n, ATTN_BLOCK), 1)
    upper = j > q
    return upper, upper & (j < jnp.where(i > 0, 0, ATTN_BLOCK))


def _fold_band(full, upper):
    return jnp.where(upper, full[:, :ATTN_BLOCK], full[:, ATTN_BLOCK:])


def _unfold_band(band, upper):
    return jnp.concatenate([jnp.where(upper, band, 0.0), jnp.where(upper, 0.0, band)], axis=1)


def _half_masks():
    lane = _iota((1, LANES), 1)
    return [(lane < HEAD_DIM).astype(F32), (lane >= HEAD_DIM).astype(F32)]


def _stack_heads(blocks, hm, j):
    pieces = []
    for r in range(4):
        qb, half = (4 * j + r) // 2, (4 * j + r) % 2
        piece = blocks[qb] * hm[half]
        if half != j:
            piece = pltpu.roll(piece, HEAD_DIM, 1)
        pieces.append(piece)
    return jnp.concatenate(pieces, axis=0)


def _unstack_heads(stacked, j):
    out = []
    for qb in (2 * j, 2 * j + 1):
        acc = None
        for half in range(2):
            r = 2 * qb + half - 4 * j
            piece = stacked[r * ATTN_BLOCK:(r + 1) * ATTN_BLOCK]
            if half != j:
                piece = pltpu.roll(piece, HEAD_DIM, 1)
            acc = piece if acc is None else acc + piece
        out.append((qb, acc))
    return out


def _sink_column(sink_ref, base):
    return jnp.concatenate([jnp.full((ATTN_BLOCK, 1), sink_ref[base + r], F32) for r in range(4)], axis=0)


def _attn_specs(nb_clamp):
    blk = ATTN_BLOCK
    kb, vb = O_K // LANES, O_V // LANES

    def cur(i):
        return jnp.minimum(i, nb_clamp)

    def prev(i):
        return jnp.maximum(jnp.minimum(i, nb_clamp + 1) - 1, 0)

    q = pl.BlockSpec((blk, 512), lambda p, i: (cur(i), p))
    kc = pl.BlockSpec((blk, LANES), lambda p, i: (cur(i), kb + p))
    kp = pl.BlockSpec((blk, LANES), lambda p, i: (prev(i), kb + p))
    vc = pl.BlockSpec((blk, LANES), lambda p, i: (cur(i), vb + p))
    vp = pl.BlockSpec((blk, LANES), lambda p, i: (prev(i), vb + p))
    tc = pl.BlockSpec((blk, LANES), lambda p, i: (cur(i), 0))
    tp = pl.BlockSpec((blk, LANES), lambda p, i: (prev(i), 0))
    return q, kc, kp, vc, vp, tc, tp


def _attn_fwd(proj, sinks, tables):
    t = proj.shape[0]
    nb = t // ATTN_BLOCK
    scale = HEAD_DIM ** -0.5

    def body(sink_ref, q_ref, kc_ref, kp_ref, vc_ref, vp_ref, cc_ref, s1c_ref, s2c_ref, cp_ref, s1p_ref, s2p_ref,
             o_ref):
        p = pl.program_id(0)
        i = pl.program_id(1)
        cc, s1c, s2c = cc_ref[...], s1c_ref[...], s2c_ref[...]
        kband = jnp.concatenate([_rope(kp_ref[...], cp_ref[...], s1p_ref[...], s2p_ref[...]),
                                 _rope(kc_ref[...], cc, s1c, s2c)], axis=0).astype(BF16)
        vband = jnp.concatenate([vp_ref[...], vc_ref[...]], axis=0)
        hm = _half_masks()
        vsel = [(vband * hm[j]).astype(BF16) for j in range(2)]
        upper, dropped = _band_masks(i, 1)
        for qb in range(4):
            qr = _rope(q_ref[:, qb * LANES:(qb + 1) * LANES], cc, s1c, s2c)
            acc = jnp.zeros((ATTN_BLOCK, LANES), F32)
            for half in range(2):
                hh = qb * 2 + half
                j = hh // 4
                qs = qr * hm[half]
                if half != j:
                    qs = pltpu.roll(qs, HEAD_DIM, 1)
                s = jnp.where(dropped, NEG, _fold_band(_dot(qs, kband, 'nt'), upper) * scale)
                sink = sink_ref[p * 8 + hh]
                m = jnp.maximum(jnp.max(s, axis=1, keepdims=True), sink)
                pe = jnp.exp(s - m)
                den = jnp.sum(pe, axis=1, keepdims=True) + jnp.exp(sink - m)
                o = _dot(_unfold_band(pe / den, upper), vsel[j])
                if half != j:
                    o = pltpu.roll(o, HEAD_DIM, 1)
                acc = acc + o
            o_ref[:, qb * LANES:(qb + 1) * LANES] = acc

    q, kc, kp, vc, vp, tc, tp = _attn_specs(nb - 1)
    smem = pl.BlockSpec(memory_space=pltpu.SMEM)
    return pl.pallas_call(
        body, name="attn_fwd", grid=(4, nb),
        in_specs=[smem, q, kc, kp, vc, vp, tc, tc, tc, tp, tp, tp],
        out_specs=pl.BlockSpec((ATTN_BLOCK, 512), lambda p, i: (i, p)),
        out_shape=jax.ShapeDtypeStruct((t, ATTN_WIDTH), F32),
        compiler_params=_cp(("parallel", "arbitrary")))(sinks, proj, proj, proj, proj, proj, *tables, *tables)


def _attn_bwd(proj, sinks, tables, dout):
    t = proj.shape[0]
    nb = t // ATTN_BLOCK
    scale = HEAD_DIM ** -0.5

    def body(sink_ref, q_ref, kc_ref, kp_ref, vc_ref, vp_ref, cc_ref, s1c_ref, s2c_ref, cp_ref, s1p_ref, s2p_ref,
             do_ref, dq_ref, dk_ref, dv_ref, ds_ref, carry_k, carry_v):
        p = pl.program_id(0)
        i = pl.program_id(1)
        ptab = (cp_ref[...], s1p_ref[...], s2p_ref[...])

        @pl.when(i == 0)
        def _():
            carry_k[...] = jnp.zeros_like(carry_k)
            carry_v[...] = jnp.zeros_like(carry_v)
            ds_ref[...] = jnp.zeros_like(ds_ref)

        @pl.when(i < nb)
        def _():
            cc, s1c, s2c = cc_ref[...], s1c_ref[...], s2c_ref[...]
            kband = jnp.concatenate([_rope(kp_ref[...], *ptab), _rope(kc_ref[...], cc, s1c, s2c)], axis=0)
            vband = jnp.concatenate([vp_ref[...], vc_ref[...]], axis=0)
            hm = _half_masks()
            kband16 = kband.astype(BF16)
            vband16 = vband.astype(BF16)
            upper, dropped = _band_masks(i, 4)
            dkb = jnp.zeros((2 * ATTN_BLOCK, LANES), F32)
            dvb = jnp.zeros((2 * ATTN_BLOCK, LANES), F32)
            row8 = _iota((8, LANES), 0)
            dsink = jnp.zeros((8, LANES), F32)
            qr = [_rope(q_ref[:, qb * LANES:(qb + 1) * LANES], cc, s1c, s2c) for qb in range(4)]
            dob = [do_ref[:, qb * LANES:(qb + 1) * LANES] for qb in range(4)]
            for j in range(2):
                qst = _stack_heads(qr, hm, j).astype(BF16)
                dost = _stack_heads(dob, hm, j).astype(BF16)
                s = jnp.where(dropped, NEG, _fold_band(_dot(qst, kband16, 'nt'), upper) * scale)
                sink = _sink_column(sink_ref, p * 8 + 4 * j)
                m = jnp.maximum(jnp.max(s, axis=1, keepdims=True), sink)
                pe = jnp.exp(s - m)
                psink = jnp.exp(sink - m)
                den = jnp.sum(pe, axis=1, keepdims=True) + psink
                pr = pe / den
                dvb = dvb + _dot(_unfold_band(pr, upper).T, dost)
                dp = _fold_band(_dot(dost, vband16, 'nt'), upper)
                delta = jnp.sum(pr * dp, axis=1, keepdims=True)
                dsc = _unfold_band(pr * (dp - delta) * scale, upper)
                dsk = psink / den * delta
                for r in range(4):
                    part = jnp.sum(dsk[r * ATTN_BLOCK:(r + 1) * ATTN_BLOCK])
                    dsink = dsink + jnp.where(row8 == 4 * j + r, -part, 0.0)
                for qb, dqb in _unstack_heads(_dot(dsc, kband * hm[j]), j):
                    dq_ref[:, qb * LANES:(qb + 1) * LANES] = _rope_t(dqb, cc, s1c, s2c).astype(BF16)
                dkb = dkb + _dot(dsc.T, qst)
            ds_ref[0] += dsink
            dk_ref[...] = _rope_t(carry_k[...] + dkb[:ATTN_BLOCK], *ptab).astype(BF16)
            dv_ref[...] = (carry_v[...] + dvb[:ATTN_BLOCK]).astype(BF16)
            carry_k[...] = dkb[ATTN_BLOCK:]
            carry_v[...] = dvb[ATTN_BLOCK:]

        @pl.when(i == nb)
        def _():
            dk_ref[...] = _rope_t(carry_k[...], *ptab).astype(BF16)
            dv_ref[...] = carry_v[...].astype(BF16)

    q, kc, kp, vc, vp, tc, tp = _attn_specs(nb - 1)
    smem = pl.BlockSpec(memory_space=pltpu.SMEM)
    qblk = pl.BlockSpec((ATTN_BLOCK, 512), lambda p, i: (jnp.minimum(i, nb - 1), p))
    kvout = pl.BlockSpec((ATTN_BLOCK, LANES), lambda p, i: (jnp.maximum(i - 1, 0), p))
    return pl.pallas_call(
        body, name="attn_bwd", grid=(4, nb + 1),
        in_specs=[smem, q, kc, kp, vc, vp, tc, tc, tc, tp, tp, tp, qblk],
        out_specs=[qblk, kvout, kvout, pl.BlockSpec((1, 8, LANES), lambda p, i: (p, 0, 0))],
        out_shape=[jax.ShapeDtypeStruct((t, ATTN_WIDTH), BF16), jax.ShapeDtypeStruct((t, KV_WIDTH), BF16),
                   jax.ShapeDtypeStruct((t, KV_WIDTH), BF16), jax.ShapeDtypeStruct((4, 8, LANES), F32)],
        scratch_shapes=[pltpu.VMEM((ATTN_BLOCK, LANES), F32), pltpu.VMEM((ATTN_BLOCK, LANES), F32)],
        compiler_params=_cp(("parallel", "arbitrary")))(sinks, proj, proj, proj, proj, proj, *tables, *tables, dout)


def _shift_rows(x, prev8, j):
    r = pltpu.roll(x, j, 0)
    head = jnp.where(_iota((8, 1), 0) < j, pltpu.roll(prev8, j, 0), r[:8])
    if x.shape[0] == 8:
        return head
    return jnp.concatenate([head, r[8:]], axis=0)


def _shift_rows_up(x, next8, j):
    n = x.shape[0]
    r = pltpu.roll(x, n - j, 0)
    tail = jnp.where(_iota((8, 1), 0) >= 8 - j, pltpu.roll(next8, 8 - j, 0), r[n - 8:])
    return jnp.concatenate([r[:n - 8], tail], axis=0)


def _conv_apply(x, prev8, w, b, taps):
    u = b + x * w[taps - 1:taps]
    for j in range(1, taps):
        u = u + _shift_rows(x, prev8, j) * w[taps - 1 - j:taps - j]
    return u


def _conv_grads(du, du_next8, x, w, taps):
    dx = du * w[taps - 1:taps]
    rowk = _iota((taps, 1), 0)
    dw = jnp.where(rowk == taps - 1, jnp.sum(du * x, axis=0, keepdims=True), 0.0)
    for j in range(1, taps):
        ahead = _shift_rows_up(du, du_next8, j)
        dx = dx + ahead * w[taps - 1 - j:taps - j]
        dw = dw + jnp.where(rowk == taps - 1 - j, jnp.sum(ahead * x, axis=0, keepdims=True), 0.0)
    return dx, dw, jnp.sum(du, axis=0, keepdims=True)


def _conv_specs(tb, tc, col0, t):
    c0 = col0 // tc
    cur = pl.BlockSpec((tb, tc), lambda j, i: (i, c0 + j))
    prev = pl.BlockSpec((8, tc), lambda j, i: (jnp.maximum(i * (tb // 8) - 1, 0), c0 + j))
    nxt = pl.BlockSpec((8, tc), lambda j, i: (jnp.minimum((i + 1) * (tb // 8), t // 8 - 1), c0 + j))
    return cur, prev, nxt


def _conv_fwd(x, w, b, *, col0, width, act, name):
    t = x.shape[0]
    taps = w.shape[0]
    tb, tc = _rows(t, 512), _tile(width, 1024)
    assert col0 % tc == 0

    def body(x_ref, xp_ref, w_ref, b_ref, o_ref):
        i = pl.program_id(1)
        prev8 = jnp.where(i > 0, xp_ref[...], 0.0)
        u = _conv_apply(x_ref[...], prev8, w_ref[...], b_ref[...], taps)
        if act:
            u = u * _sigmoid(u)
        o_ref[...] = u

    cur, prev, _ = _conv_specs(tb, tc, col0, t)
    par = pl.BlockSpec((taps, tc), lambda j, i: (0, j))
    bias = pl.BlockSpec((1, tc), lambda j, i: (0, j))
    return pl.pallas_call(
        body, name=name, grid=(width // tc, t // tb), in_specs=[cur, prev, par, bias],
        out_specs=pl.BlockSpec((tb, tc), lambda j, i: (i, j)), out_shape=jax.ShapeDtypeStruct((t, width), F32),
        compiler_params=_cp(("parallel", "parallel")))(x, x, w, b)


def _conv_silu_dact(x, w, b, dout, *, col0, width, name):
    t = x.shape[0]
    taps = w.shape[0]
    tb, tc = _rows(t, 512), _tile(width, 1024)

    def body(x_ref, xp_ref, w_ref, b_ref, d_ref, o_ref):
        i = pl.program_id(1)
        prev8 = jnp.where(i > 0, xp_ref[...], 0.0)
        u = _conv_apply(x_ref[...], prev8, w_ref[...], b_ref[...], taps)
        sg = _sigmoid(u)
        o_ref[...] = d_ref[...] * (sg * (1.0 + u * (1.0 - sg)))

    cur, prev, _ = _conv_specs(tb, tc, col0, t)
    par = pl.BlockSpec((taps, tc), lambda j, i: (0, j))
    bias = pl.BlockSpec((1, tc), lambda j, i: (0, j))
    out = pl.BlockSpec((tb, tc), lambda j, i: (i, j))
    return pl.pallas_call(
        body, name=name, grid=(width // tc, t // tb), in_specs=[cur, prev, par, bias, out],
        out_specs=out, out_shape=jax.ShapeDtypeStruct((t, width), F32),
        compiler_params=_cp(("parallel", "parallel")))(x, x, w, b, dout)


def _conv_bwd(x, w, du, *, col0, width, name):
    t = x.shape[0]
    taps = w.shape[0]
    tb, tc = _rows(t, 512), _tile(width, 1024)
    nrow = t // tb

    def body(x_ref, w_ref, du_ref, dun_ref, dx_ref, dw_ref, db_ref):
        i = pl.program_id(1)
        next8 = jnp.where(i < nrow - 1, dun_ref[...], 0.0)
        dx, dwv, dbv = _conv_grads(du_ref[...], next8, x_ref[...], w_ref[...], taps)
        dx_ref[...] = dx.astype(BF16)

        @pl.when(i == 0)
        def _():
            dw_ref[...] = dwv
            db_ref[...] = dbv

        @pl.when(i > 0)
        def _():
            dw_ref[...] += dwv
            db_ref[...] += dbv

    cur, _, _ = _conv_specs(tb, tc, col0, t)
    dcur, _, dnxt = _conv_specs(tb, tc, 0, t)
    par = pl.BlockSpec((taps, tc), lambda j, i: (0, j))
    bias = pl.BlockSpec((1, tc), lambda j, i: (0, j))
    return pl.pallas_call(
        body, name=name, grid=(width // tc, nrow), in_specs=[cur, par, dcur, dnxt],
        out_specs=[dcur, par, bias],
        out_shape=[jax.ShapeDtypeStruct((t, width), BF16), jax.ShapeDtypeStruct((taps, width), F32),
                   jax.ShapeDtypeStruct((1, width), F32)],
        compiler_params=_cp(("parallel", "arbitrary")))(x, w, du, du)


def _ffn_specs(tb, tc, t):
    nc = D_FF // tc

    def cur(half):
        return pl.BlockSpec((tb, tc), lambda j, i: (i, half * nc + j))

    def prev(half):
        return pl.BlockSpec((8, tc), lambda j, i: (jnp.maximum(i * (tb // 8) - 1, 0), half * nc + j))

    def nxt(half):
        return pl.BlockSpec((8, tc), lambda j, i: (jnp.minimum((i + 1) * (tb // 8), t // 8 - 1), half * nc + j))

    def par(rows, half):
        return pl.BlockSpec((rows, tc), lambda j, i: (0, half * nc + j))

    return cur, prev, nxt, par


def _ffn_act_fwd(u0, w, b):
    t = u0.shape[0]
    tb, tc = _rows(t, 512), _tile(D_FF, 1408)
    cur, prev, _, par = _ffn_specs(tb, tc, t)

    def body(g_ref, gp_ref, v_ref, vp_ref, wg_ref, wv_ref, bg_ref, bv_ref, o_ref):
        i = pl.program_id(1)
        ug = _conv_apply(g_ref[...], jnp.where(i > 0, gp_ref[...], 0.0), wg_ref[...], bg_ref[...], FFN_CONV)
        uv = _conv_apply(v_ref[...], jnp.where(i > 0, vp_ref[...], 0.0), wv_ref[...], bv_ref[...], FFN_CONV)
        o_ref[...] = (ug * _sigmoid(ug) * uv).astype(BF16)

    return pl.pallas_call(
        body, name="ffn_act_fwd", grid=(D_FF // tc, t // tb),
        in_specs=[cur(0), prev(0), cur(1), prev(1), par(FFN_CONV, 0), par(FFN_CONV, 1), par(1, 0), par(1, 1)],
        out_specs=pl.BlockSpec((tb, tc), lambda j, i: (i, j)), out_shape=jax.ShapeDtypeStruct((t, D_FF), BF16),
        compiler_params=_cp(("parallel", "parallel")))(u0, u0, u0, u0, w, w, b, b)


def _ffn_act_bwd(u0, w, b, da):
    t = u0.shape[0]
    tb, tc = _rows(t, 256), _tile(D_FF, 1408)
    nrow = t // tb
    taps = FFN_CONV
    cur, prev, nxt, par = _ffn_specs(tb, tc, t)

    def dact(ug, uv, dav):
        sg = _sigmoid(ug)
        return dav * uv * (sg * (1.0 + ug * (1.0 - sg))), dav * ug * sg

    def body(g_ref, gp_ref, gn_ref, v_ref, vp_ref, vn_ref, wg_ref, wv_ref, bg_ref, bv_ref, da_ref, dan_ref,
             dx_ref, dw_ref, db_ref):
        i = pl.program_id(1)
        xg, xv = g_ref[...], v_ref[...]
        gp = jnp.where(i > 0, gp_ref[...], 0.0)
        vp = jnp.where(i > 0, vp_ref[...], 0.0)
        wg, wv, bg, bv = wg_ref[...], wv_ref[...], bg_ref[...], bv_ref[...]
        dug, duv = dact(_conv_apply(xg, gp, wg, bg, taps), _conv_apply(xv, vp, wv, bv, taps),
                        da_ref[...].astype(F32))
        dan = jnp.where(i < nrow - 1, dan_ref[...].astype(F32)[:8], 0.0)
        dugn, duvn = dact(_conv_apply(gn_ref[...], xg[tb - 8:], wg, bg, taps),
                          _conv_apply(vn_ref[...], xv[tb - 8:], wv, bv, taps), dan)
        dxg, dwg, dbg = _conv_grads(dug, dugn, xg, wg, taps)
        dxv, dwv, dbv = _conv_grads(duv, duvn, xv, wv, taps)
        dx_ref[0] = dxg.astype(BF16)
        dx_ref[1] = dxv.astype(BF16)

        @pl.when(i == 0)
        def _():
            dw_ref[0] = dwg
            dw_ref[1] = dwv
            db_ref[0] = dbg
            db_ref[1] = dbv

        @pl.when(i > 0)
        def _():
            dw_ref[0] += dwg
            dw_ref[1] += dwv
            db_ref[0] += dbg
            db_ref[1] += dbv

    da_cur = pl.BlockSpec((tb, tc), lambda j, i: (i, j))
    da_nxt = pl.BlockSpec((16, tc), lambda j, i: (jnp.minimum((i + 1) * (tb // 16), t // 16 - 1), j))
    return pl.pallas_call(
        body, name="ffn_act_bwd", grid=(D_FF // tc, nrow),
        in_specs=[cur(0), prev(0), nxt(0), cur(1), prev(1), nxt(1), par(taps, 0), par(taps, 1), par(1, 0),
                  par(1, 1), da_cur, da_nxt],
        out_specs=[pl.BlockSpec((2, tb, tc), lambda j, i: (0, i, j)),
                   pl.BlockSpec((2, taps, tc), lambda j, i: (0, 0, j)),
                   pl.BlockSpec((2, 1, tc), lambda j, i: (0, 0, j))],
        out_shape=[jax.ShapeDtypeStruct((2, t, D_FF), BF16), jax.ShapeDtypeStruct((2, taps, D_FF), F32),
                   jax.ShapeDtypeStruct((2, 1, D_FF), F32)],
        compiler_params=_cp(("parallel", "arbitrary")))(u0, u0, u0, u0, u0, u0, w, w, b, b, da, da)


def _head_masks():
    lane = _iota((1, 4 * SSD_HEAD_DIM), 1)
    return [((lane >= r * SSD_HEAD_DIM) & (lane < (r + 1) * SSD_HEAD_DIM)).astype(F32) for r in range(4)]


def _segsum(v):
    first = _iota((1, LANES), 1) < SSD_HEAD_DIM
    halves = []
    for k in range(2):
        vh = v[:, k * LANES:(k + 1) * LANES]
        both = jnp.sum(vh, axis=1, keepdims=True)
        one = jnp.sum(jnp.where(first, vh, 0.0), axis=1, keepdims=True)
        halves.append(jnp.where(first, one, both - one))
    return jnp.concatenate(halves, axis=1)


def _ssd_common(raw_e, prow, rawr4, bcol, acol):
    n = SSD_CHUNK
    dt_e = _softplus(raw_e + prow[0:1, :])
    a_e = -jnp.exp(prow[1:2, :])
    d_e = prow[2:3, :]
    tril = (_iota((n, n), 0) >= _iota((n, n), 1)).astype(F32)
    acs_e = _dot_exact(tril, dt_e * a_e)
    last_e = acs_e[n - 1:n, :]
    dtr4 = _softplus(rawr4 + bcol)
    triu = (_iota((n, n), 0) <= _iota((n, n), 1)).astype(F32)
    acs_r4 = _dot_exact(dtr4 * (-jnp.exp(acol)), triu)
    return dt_e, a_e, d_e, acs_e, last_e, acs_r4


def _decay_matrix(acs_e, acs_r4, r):
    n = SSD_CHUNK
    col = acs_e[:, r * SSD_HEAD_DIM:r * SSD_HEAD_DIM + 1]
    seg = col - acs_r4[r:r + 1, :]
    causal = _iota((n, n), 0) >= _iota((n, n), 1)
    return jnp.exp(jnp.where(causal, seg, NEG))


SSD_STEP_CHUNKS = 4
SSD_ROWS = SSD_STEP_CHUNKS * SSD_CHUNK


def _ssd_specs(t, rev):
    nb = t // SSD_ROWS
    xb, bb, cb = 0, SSD_INNER // SSD_STATE, (SSD_INNER + BC_WIDTH) // SSD_STATE

    def ch(c):
        return (nb - 1 - c) if rev else c

    x = pl.BlockSpec((SSD_ROWS, 256), lambda g, c: (ch(c), xb + g))
    bm = pl.BlockSpec((SSD_ROWS, SSD_STATE), lambda g, c: (ch(c), bb + g))
    cm = pl.BlockSpec((SSD_ROWS, SSD_STATE), lambda g, c: (ch(c), cb + g))
    dtc = pl.BlockSpec((1, SSD_ROWS, 256), lambda g, c: (g, ch(c), 0))
    dtr = pl.BlockSpec((1, 4, SSD_ROWS), lambda g, c: (g, 0, ch(c)))
    prow = pl.BlockSpec((1, 3, 256), lambda g, c: (g, 0, 0))
    pcol = pl.BlockSpec((1, 4, 1), lambda g, c: (g, 0, 0))
    st = pl.BlockSpec((1, SSD_STEP_CHUNKS, SSD_STATE, 256), lambda g, c: (g, ch(c), 0, 0))
    return x, bm, cm, dtc, dtr, prow, pcol, st, ch


def _ssd_params(dt_raw, dt_bias, a_log, ssd_d):
    t = dt_raw.shape[0]
    by_group = dt_raw.reshape(t, SSD_GROUPS, 4)
    dtc = jnp.repeat(by_group, SSD_HEAD_DIM, axis=2).transpose(1, 0, 2)
    dtr = by_group.transpose(1, 2, 0)
    prow = jnp.repeat(jnp.stack([dt_bias.reshape(SSD_GROUPS, 4), a_log.reshape(SSD_GROUPS, 4),
                                 ssd_d.reshape(SSD_GROUPS, 4)], axis=1), SSD_HEAD_DIM, axis=2)
    bcol = dt_bias.reshape(SSD_GROUPS, 4, 1)
    acol = a_log.reshape(SSD_GROUPS, 4, 1)
    return dtc, dtr, prow, bcol, acol


def _ssd_fwd(xbc, params):
    t = xbc.shape[0]
    nc = t // SSD_CHUNK
    dtc, dtr, prow, bcol, acol = params

    def body(x_ref, b_ref, c_ref, dtc_ref, dtr_ref, prow_ref, bcol_ref, acol_ref, y_ref, st_ref, s_scr):
        c = pl.program_id(1)

        @pl.when(c == 0)
        def _():
            s_scr[...] = jnp.zeros_like(s_scr)

        masks = _head_masks()
        s = s_scr[...]
        for k in range(SSD_STEP_CHUNKS):
            rows = slice(k * SSD_CHUNK, (k + 1) * SSD_CHUNK)
            dt_e, a_e, d_e, acs_e, last_e, acs_r4 = _ssd_common(
                dtc_ref[0, rows], prow_ref[0], dtr_ref[0][:, rows], bcol_ref[0], acol_ref[0])
            xv = x_ref[rows]
            bm, cm = b_ref[rows], c_ref[rows]
            st_ref[0, k] = s
            xdt = xv * dt_e
            cb = _dot(cm, bm, 'nt')
            y = _dot(cm, s) * jnp.exp(acs_e) + xv * d_e
            for r in range(4):
                mr = cb * _decay_matrix(acs_e, acs_r4, r)
                y = y + _dot(mr, xdt * masks[r])
            y_ref[rows] = y
            w = xdt * jnp.exp(last_e - acs_e)
            s = s * jnp.exp(last_e) + _dot(bm.T, w)
        s_scr[...] = s

    x, bm, cm, dtcs, dtrs, prs, pcs, st, _ = _ssd_specs(t, False)
    return pl.pallas_call(
        body, name="ssd_fwd", grid=(SSD_GROUPS, t // SSD_ROWS), in_specs=[x, bm, cm, dtcs, dtrs, prs, pcs, pcs],
        out_specs=[pl.BlockSpec((SSD_ROWS, 256), lambda g, c: (c, g)), st],
        out_shape=[jax.ShapeDtypeStruct((t, SSD_INNER), F32),
                   jax.ShapeDtypeStruct((SSD_GROUPS, nc, SSD_STATE, 256), F32)],
        scratch_shapes=[pltpu.VMEM((SSD_STATE, 256), F32)],
        compiler_params=_cp(("parallel", "arbitrary")))(xbc, xbc, xbc, dtc, dtr, prow, bcol, acol)


def _ssd_bwd(xbc, params, states, dy):
    t = xbc.shape[0]
    nc = t // SSD_CHUNK
    n = SSD_CHUNK
    dtc, dtr, prow, bcol, acol = params

    def body(x_ref, b_ref, c_ref, dtc_ref, dtr_ref, prow_ref, bcol_ref, acol_ref, st_ref, dy_ref,
             dx_ref, db_ref, dc_ref, ddt_ref, dp_ref, ds_scr):
        c = pl.program_id(1)

        @pl.when(c == 0)
        def _():
            ds_scr[...] = jnp.zeros_like(ds_scr)
            dp_ref[...] = jnp.zeros_like(dp_ref)

        masks = _head_masks()
        ds = ds_scr[...]
        for k in reversed(range(SSD_STEP_CHUNKS)):
            rows = slice(k * SSD_CHUNK, (k + 1) * SSD_CHUNK)
            raw_e = dtc_ref[0, rows]
            prw = prow_ref[0]
            dt_e, a_e, d_e, acs_e, last_e, acs_r4 = _ssd_common(raw_e, prw, dtr_ref[0][:, rows], bcol_ref[0], acol_ref[0])
            xv = x_ref[rows]
            bm, cm = b_ref[rows], c_ref[rows]
            s = st_ref[0, k]
            dyv = dy_ref[rows]
            e_e = jnp.exp(acs_e)
            dec_e = jnp.exp(last_e - acs_e)
            cd_e = jnp.exp(last_e)
            xdt = xv * dt_e
            w = xdt * dec_e
            b16, c16, s16, ds16 = bm.astype(BF16), cm.astype(BF16), s.astype(BF16), ds.astype(BF16)
            cb = _dot(c16, b16, 'nt')
            yoff_raw = _dot(c16, s16)
            dye = dyv * e_e
            dye16 = dye.astype(BF16)
            dcm = _dot(dye16, s16, 'nt')
            ds_prev = ds * cd_e + _dot(cm.T, dye16)
            dacs_e = _segsum(dyv * yoff_raw) * e_e
            dw = _dot(b16, ds16)
            dbm = _dot(w, ds16, 'nt')
            tdec = _segsum(dw * xdt) * dec_e
            dacs_e = dacs_e - tdec
            dlast_e = jnp.sum(tdec, axis=0, keepdims=True)
            dxdt = dw * dec_e
            dlast_e = dlast_e + _segsum(jnp.sum(ds * s, axis=0, keepdims=True)) * cd_e
            dcb = jnp.zeros((n, n), F32)
            for r in range(4):
                lm = _decay_matrix(acs_e, acs_r4, r)
                mr = cb * lm
                dyr16 = (dyv * masks[r]).astype(BF16)
                dm = _dot(dyr16, xdt * masks[r], 'nt')
                dcb = dcb + dm * lm
                dseg = dm * mr
                dcol = jnp.sum(dseg, axis=1, keepdims=True) - jnp.sum(dseg.T, axis=1, keepdims=True)
                dacs_e = dacs_e + dcol * masks[r]
                dxdt = dxdt + _dot(mr.T, dyr16)
            dcm = dcm + _dot(dcb, b16)
            dbm = dbm + _dot(dcb.T, c16)
            dacs_e = dacs_e + jnp.where(_iota((n, 1), 0) == n - 1, dlast_e, 0.0)
            triu = (_iota((n, n), 0) <= _iota((n, n), 1)).astype(F32)
            ddta_e = _dot_exact(triu, dacs_e)
            ddt_e = ddta_e * a_e + _segsum(dxdt * xv)
            dx_ref[rows] = dxdt * dt_e + dyv * d_e
            db_ref[rows] = dbm
            dc_ref[rows] = dcm
            draw_e = ddt_e * _sigmoid(raw_e + prw[0:1, :])
            draw_t = draw_e.T
            ddt_ref[0, :, rows] = jnp.concatenate([draw_t[r * SSD_HEAD_DIM:r * SSD_HEAD_DIM + 1] for r in range(4)], axis=0)
            dbias = jnp.sum(draw_e, axis=0, keepdims=True)
            dalog = jnp.sum(ddta_e * dt_e, axis=0, keepdims=True) * a_e
            dd = _segsum(jnp.sum(dyv * xv, axis=0, keepdims=True))
            row3 = _iota((3, 1), 0)
            dp_ref[0] += (jnp.where(row3 == 0, dbias, 0.0) + jnp.where(row3 == 1, dalog, 0.0)
                          + jnp.where(row3 == 2, dd, 0.0))
            ds = ds_prev
        ds_scr[...] = ds


    x, bm, cm, dtcs, dtrs, prs, pcs, st, ch = _ssd_specs(t, True)
    yblk = pl.BlockSpec((SSD_ROWS, 256), lambda g, c: (ch(c), g))
    nblk = pl.BlockSpec((SSD_ROWS, SSD_STATE), lambda g, c: (ch(c), g))
    return pl.pallas_call(
        body, name="ssd_bwd", grid=(SSD_GROUPS, t // SSD_ROWS),
        in_specs=[x, bm, cm, dtcs, dtrs, prs, pcs, pcs, st, yblk],
        out_specs=[yblk, nblk, nblk, dtrs, prs],
        out_shape=[jax.ShapeDtypeStruct((t, SSD_INNER), F32), jax.ShapeDtypeStruct((t, BC_WIDTH), F32),
                   jax.ShapeDtypeStruct((t, BC_WIDTH), F32), jax.ShapeDtypeStruct((SSD_GROUPS, 4, t), F32),
                   jax.ShapeDtypeStruct((SSD_GROUPS, 3, 256), F32)],
        scratch_shapes=[pltpu.VMEM((SSD_STATE, 256), F32)],
        compiler_params=_cp(("parallel", "arbitrary")))(xbc, xbc, xbc, dtc, dtr, prow, bcol, acol, states, dy)


GROUP_W = SSD_INNER // SSD_GROUPS


def _mix_specs(tb):
    row = pl.BlockSpec((tb, 2048), lambda i: (i, 0))
    zlo = pl.BlockSpec((tb, 1024), lambda i: (i, O_Z // 1024))
    zhi = pl.BlockSpec((tb, 1024), lambda i: (i, O_Z // 1024 + 1))
    vec = pl.BlockSpec((1, 2048), lambda i: (0, 0))
    return row, zlo, zhi, vec


def _mix_fwd(attn, y, proj, g_attn, g_ssd):
    t = attn.shape[0]
    tb = _rows(t, 256)

    def body(a_ref, y_ref, zlo_ref, zhi_ref, ga_ref, gs_ref, o_ref):
        av = a_ref[...]
        r = lax.rsqrt(jnp.mean(av * av, axis=-1, keepdims=True) + EPS)
        o_ref[:, :ATTN_WIDTH] = (av * r * ga_ref[...]).astype(BF16)
        for g in range(SSD_GROUPS):
            lo, hi = g * GROUP_W, (g + 1) * GROUP_W
            zref = zlo_ref if g < 4 else zhi_ref
            z = zref[:, lo % 1024:lo % 1024 + GROUP_W]
            yg = y_ref[:, lo:hi] * (z * _sigmoid(z))
            rg = lax.rsqrt(jnp.mean(yg * yg, axis=-1, keepdims=True) + EPS)
            o_ref[:, ATTN_WIDTH + lo:ATTN_WIDTH + hi] = (yg * rg * gs_ref[:, lo:hi]).astype(BF16)

    row, zlo, zhi, vec = _mix_specs(tb)
    return pl.pallas_call(
        body, name="mix_fwd", grid=(t // tb,), in_specs=[row, row, zlo, zhi, vec, vec],
        out_specs=pl.BlockSpec((tb, 4096), lambda i: (i, 0)), out_shape=jax.ShapeDtypeStruct((t, 4096), BF16),
        compiler_params=_cp(("parallel",)))(attn, y, proj, proj, g_attn, g_ssd)


def _mix_bwd(dmix, attn, y, proj, g_attn, g_ssd):
    t = attn.shape[0]
    tb = _rows(t, 256)

    def body(dm_ref, a_ref, y_ref, zlo_ref, zhi_ref, ga_ref, gs_ref, da_ref, dy_ref, dz_ref, dga_ref, dgs_ref):
        i = pl.program_id(0)
        av = a_ref[...]
        dn = dm_ref[:, :ATTN_WIDTH].astype(F32)
        r = lax.rsqrt(jnp.mean(av * av, axis=-1, keepdims=True) + EPS)
        u = dn * ga_ref[...]
        da_ref[...] = r * u - av * (r * r * r * jnp.mean(u * av, axis=-1, keepdims=True))
        dga = jnp.sum(dn * av * r, axis=0, keepdims=True)

        @pl.when(i == 0)
        def _():
            dga_ref[...] = dga

        @pl.when(i > 0)
        def _():
            dga_ref[...] += dga

        for g in range(SSD_GROUPS):
            lo, hi = g * GROUP_W, (g + 1) * GROUP_W
            zref = zlo_ref if g < 4 else zhi_ref
            z = zref[:, lo % 1024:lo % 1024 + GROUP_W]
            yv = y_ref[:, lo:hi]
            sg = _sigmoid(z)
            sz = z * sg
            yg = yv * sz
            rg = lax.rsqrt(jnp.mean(yg * yg, axis=-1, keepdims=True) + EPS)
            do = dm_ref[:, ATTN_WIDTH + lo:ATTN_WIDTH + hi].astype(F32)
            ug = do * gs_ref[:, lo:hi]
            dyg = rg * ug - yg * (rg * rg * rg * jnp.mean(ug * yg, axis=-1, keepdims=True))
            dy_ref[:, lo:hi] = dyg * sz
            dz_ref[:, lo:hi] = (dyg * yv * (sg * (1.0 + z * (1.0 - sg)))).astype(BF16)
            dgs = jnp.sum(do * yg * rg, axis=0, keepdims=True)

            @pl.when(i == 0)
            def _():
                dgs_ref[:, lo:hi] = dgs

            @pl.when(i > 0)
            def _():
                dgs_ref[:, lo:hi] += dgs

    row, zlo, zhi, vec = _mix_specs(tb)
    return pl.pallas_call(
        body, name="mix_bwd", grid=(t // tb,),
        in_specs=[pl.BlockSpec((tb, 4096), lambda i: (i, 0)), row, row, zlo, zhi, vec, vec],
        out_specs=[row, row, row, vec, vec],
        out_shape=[jax.ShapeDtypeStruct((t, 2048), F32), jax.ShapeDtypeStruct((t, 2048), F32),
                   jax.ShapeDtypeStruct((t, 2048), BF16), jax.ShapeDtypeStruct((1, 2048), F32),
                   jax.ShapeDtypeStruct((1, 2048), F32)],
        compiler_params=_cp(("arbitrary",)))(dmix, attn, y, proj, proj, g_attn, g_ssd)


def _adamw(w, g, m, v, name):
    r, c = w.shape
    tb = _rows(r, 256)
    c1 = 1.0 - ADAM_B1 ** ADAM_STEP
    c2 = 1.0 - ADAM_B2 ** ADAM_STEP

    def body(w_ref, g_ref, m_ref, v_ref, d_ref, m2_ref, v2_ref):
        gv = g_ref[...]
        m2 = ADAM_B1 * m_ref[...] + (1.0 - ADAM_B1) * gv
        v2 = ADAM_B2 * v_ref[...] + (1.0 - ADAM_B2) * (gv * gv)
        d_ref[...] = -ADAM_LR * ((m2 / c1) / (jnp.sqrt(v2 / c2) + ADAM_EPS) + ADAM_WD * w_ref[...])
        m2_ref[...] = m2
        v2_ref[...] = v2

    blk = pl.BlockSpec((tb, c), lambda i: (i, 0))
    shp = jax.ShapeDtypeStruct((r, c), F32)
    return pl.pallas_call(body, name=name, grid=(r // tb,), in_specs=[blk] * 4, out_specs=[blk] * 3,
                          out_shape=[shp] * 3, compiler_params=_cp(("parallel",)))(w, g, m, v)


def _adamw_halves(w, mine, theirs, m, v, pos, name, cols=False):
    r, c = w.shape
    h = r if cols else r // 2
    tb = _rows(h, 128)
    nh = h // tb
    c1 = 1.0 - ADAM_B1 ** ADAM_STEP
    c2 = 1.0 - ADAM_B2 ** ADAM_STEP

    def body(pos_ref, w_ref, a_ref, b_ref, m_ref, v_ref, g_ref, d_ref, m2_ref, v2_ref):
        which = pl.program_id(1) if cols else pl.program_id(0) // nh
        gv = jnp.where(which == pos_ref[0], a_ref[...], b_ref[...])
        m2 = ADAM_B1 * m_ref[...] + (1.0 - ADAM_B1) * gv
        v2 = ADAM_B2 * v_ref[...] + (1.0 - ADAM_B2) * (gv * gv)
        g_ref[...] = gv
        d_ref[...] = -ADAM_LR * ((m2 / c1) / (jnp.sqrt(v2 / c2) + ADAM_EPS) + ADAM_WD * w_ref[...])
        m2_ref[...] = m2
        v2_ref[...] = v2

    if cols:
        full = pl.BlockSpec((tb, c // 2), lambda i, j, pref: (i, j))
        mine_spec = theirs_spec = pl.BlockSpec((tb, c // 2), lambda i, j, pref: (i, 0))
        grid = (nh, 2)
    else:
        full = pl.BlockSpec((tb, c), lambda i, pref: (i, 0))
        mine_spec = pl.BlockSpec((tb, c), lambda i, pref: (jnp.where(i // nh == pref[0], i % nh,
                                                                     jnp.where(pref[0] == 0, nh - 1, 0)), 0))
        theirs_spec = pl.BlockSpec((tb, c), lambda i, pref: (jnp.where(i // nh != pref[0], i % nh,
                                                                       jnp.where(pref[0] == 0, 0, nh - 1)), 0))
        grid = (r // tb,)
    shp = jax.ShapeDtypeStruct((r, c), F32)
    grid_spec = pltpu.PrefetchScalarGridSpec(num_scalar_prefetch=1, grid=grid,
                                             in_specs=[full, mine_spec, theirs_spec, full, full],
                                             out_specs=[full] * 4)
    return pl.pallas_call(body, name=name, grid_spec=grid_spec, out_shape=[shp] * 4,
                          compiler_params=_cp(("parallel",) * len(grid)))(pos, w, mine, theirs, m, v)


def _sum_own_half(g4, recv, pos, name, cols=False):
    _, r, c = g4.shape
    h, c = (r, c // 2) if cols else (r // 2, c)
    tb = _rows(h, 128)
    nh = h // tb

    def slot(j, pref):
        return (pref[1] + 1 + j) % N_CHIPS

    if cols:
        own = lambda j, i, pref: (slot(j, pref), i, pref[0])
    else:
        own = lambda j, i, pref: (slot(j, pref), pref[0] * nh + i, 0)
    same = lambda j, i, pref: (slot(j, pref), i, 0)

    def body(pos_ref, a_ref, b_ref, o_ref):
        o_ref[...] = (a_ref[...] + b_ref[...]).astype(BF16)

    grid_spec = pltpu.PrefetchScalarGridSpec(
        num_scalar_prefetch=1, grid=(N_CHIPS - 1, nh),
        in_specs=[pl.BlockSpec((1, tb, c), own), pl.BlockSpec((1, tb, c), same)],
        out_specs=pl.BlockSpec((1, tb, c), same))
    return pl.pallas_call(body, name=name, grid_spec=grid_spec,
                          out_shape=jax.ShapeDtypeStruct((N_CHIPS, h, c), BF16),
                          compiler_params=_cp(("parallel", "parallel")))(pos, g4, recv)


def _sum_chips(g4, recv, parts, pos, name, cols=False):
    _, r, c = g4.shape
    h, c = (r, c // 2) if cols else (r // 2, c)
    tb = _rows(h, 128)
    nh = h // tb
    own = (lambda i, pref: (pref[1], i, pref[0])) if cols else (lambda i, pref: (pref[1], pref[0] * nh + i, 0))

    def body(pos_ref, a_ref, b_ref, p_ref, o_ref):
        own = a_ref[0] + b_ref[0]
        o_ref[...] = ((own + p_ref[0].astype(F32)) + p_ref[1].astype(F32)) + p_ref[2].astype(F32)

    grid_spec = pltpu.PrefetchScalarGridSpec(
        num_scalar_prefetch=1, grid=(nh,),
        in_specs=[pl.BlockSpec((1, tb, c), own),
                  pl.BlockSpec((1, tb, c), lambda i, pref: (pref[1], i, 0)),
                  pl.BlockSpec((3, tb, c), lambda i, pref: (0, i, 0))],
        out_specs=pl.BlockSpec((tb, c), lambda i, pref: (i, 0)))
    return pl.pallas_call(body, name=name, grid_spec=grid_spec, out_shape=jax.ShapeDtypeStruct((h, c), F32),
                          compiler_params=_cp(("parallel",)))(pos, g4, recv, parts)


def _me():
    return lax.axis_index("x"), lax.axis_index("y"), lax.axis_index("c")


def _flip(v, bit):
    return (1 - v) if bit else v


CHIP_FLIPS = [(1, 0), (0, 1), (1, 1)]


def _allgather_weights(shards, after, cols=False):
    n = len(shards)

    def body(*refs):
        ins, outs, token = refs[:n], refs[n + 1:2 * n + 1], refs[2 * n + 1]
        send_sems, recv_sems = refs[2 * n + 2:]
        x, y, c = _me()
        chip = 2 * x + y
        sib = (x, y, 1 - c)

        def remote(src, dst, k, to):
            return pltpu.make_async_remote_copy(src_ref=src, dst_ref=dst, send_sem=send_sems.at[k],
                                                recv_sem=recv_sems.at[k], device_id=to, device_id_type=MESH)

        def half(ref, which):
            if cols:
                h = ref.shape[1] // 2
                return ref.at[:, pl.ds(which * h, h)]
            h = ref.shape[0] // 2
            return ref.at[pl.ds(which * h, h)]

        sends = []
        for t in range(n):
            for k, (fx, fy) in enumerate(CHIP_FLIPS):
                cp = remote(half(ins[t], c), half(outs[t].at[chip], c), 6 * t + k, (_flip(x, fx), _flip(y, fy), c))
                cp.start()
                sends.append(cp)
        for t in range(n):
            for k, (fx, fy) in enumerate(CHIP_FLIPS):
                landed = half(outs[t].at[2 * _flip(x, fx) + _flip(y, fy)], c)
                remote(landed, landed, 6 * t + k, (x, y, c)).wait_recv()
                fw = remote(landed, landed, 6 * t + 3 + k, sib)
                fw.start()
                sends.append(fw)
        for t in range(n):
            for k, (fx, fy) in enumerate(CHIP_FLIPS):
                got = half(outs[t].at[2 * _flip(x, fx) + _flip(y, fy)], 1 - c)
                remote(got, got, 6 * t + 3 + k, (x, y, c)).wait_recv()
        for cp in sends:
            cp.wait_send()
        token[...] = jnp.zeros_like(token)

    outs = pl.pallas_call(
        body, name="allgather_weights", in_specs=[HBM_SPEC] * n + [pl.BlockSpec(memory_space=pl.ANY)],
        out_specs=[HBM_SPEC] * n + [pl.BlockSpec(memory_space=pltpu.VMEM)],
        out_shape=[jax.ShapeDtypeStruct((N_CHIPS,) + s.shape, s.dtype) for s in shards] + [TOKEN],
        scratch_shapes=[pltpu.SemaphoreType.DMA((6 * n,)), pltpu.SemaphoreType.DMA((6 * n,))],
        compiler_params=pltpu.CompilerParams(has_side_effects=True))(*shards, after)
    return list(outs[:n]), outs[n]


def _share_halves(ghs, name):
    n = len(ghs)

    def body(*refs):
        ins, outs = refs[:n], refs[n:2 * n]
        send_sems, recv_sems = refs[2 * n:]
        x, y, c = _me()
        cps = []
        for t in range(n):
            cp = pltpu.make_async_remote_copy(
                src_ref=ins[t], dst_ref=outs[t], send_sem=send_sems.at[t], recv_sem=recv_sems.at[t],
                device_id=(x, y, 1 - c), device_id_type=MESH)
            cp.start()
            cps.append(cp)
        for cp in cps:
            cp.wait()

    return pl.pallas_call(
        body, name=name, in_specs=[HBM_SPEC] * n, out_specs=[HBM_SPEC] * n,
        out_shape=[jax.ShapeDtypeStruct(g.shape, g.dtype) for g in ghs],
        scratch_shapes=[pltpu.SemaphoreType.DMA((n,)), pltpu.SemaphoreType.DMA((n,))],
        compiler_params=pltpu.CompilerParams(has_side_effects=True))(*ghs)


SEM_SPEC = pl.BlockSpec(memory_space=pltpu.SEMAPHORE)
ANY_SPEC = pl.BlockSpec(memory_space=pl.ANY)
DATAFLOW = pltpu.SideEffectType.DATAFLOW_SIDE_EFFECTING


def _in_hbm(a):
    return pltpu.with_memory_space_constraint(a, pltpu.HBM)


def _push_start(srcs, land_shapes, route, peers, name):
    n, npeer = len(srcs), len(peers)
    lands = [lax.empty(shp, s.dtype) for shp, s in zip(land_shapes, srcs)]

    def body(*refs):
        ins, lnd = refs[:n], refs[n:2 * n]
        send_sems, recv_sems = refs[2 * n], refs[2 * n + 1]
        token = refs[-1]
        x, y, c = _me()
        for t in range(n):
            for k, (fx, fy, fc) in enumerate(peers):
                src, dst = route(ins[t], lnd[t], k, x, y, c)
                pltpu.make_async_remote_copy(
                    src_ref=src, dst_ref=dst, send_sem=send_sems.at[npeer * t + k],
                    recv_sem=recv_sems.at[npeer * t + k],
                    device_id=(_flip(x, fx), _flip(y, fy), _flip(c, fc)), device_id_type=MESH).start()
        token[...] = jnp.zeros_like(token)

    bufs = [_in_hbm(a) for a in list(srcs) + lands]
    outs = pl.pallas_call(
        body, name=name,
        out_shape=(pltpu.SemaphoreType.DMA((npeer * n,)), pltpu.SemaphoreType.DMA((npeer * n,)),
                   *[pltpu.HBM(b.shape, b.dtype) for b in bufs], TOKEN),
        in_specs=[HBM_SPEC] * (2 * n),
        out_specs=(SEM_SPEC, SEM_SPEC, *[HBM_SPEC] * (2 * n), pl.BlockSpec(memory_space=pltpu.VMEM)),
        input_output_aliases={i: 2 + i for i in range(2 * n)},
        compiler_params=pltpu.CompilerParams(has_side_effects=DATAFLOW))(*bufs)
    return outs[0], outs[1], list(outs[2:2 + n]), list(outs[2 + n:2 + 2 * n]), outs[-1]


def _push_wait(send_sems, recv_sems, srcs, lands, after, route, peers, name):
    n, npeer = len(srcs), len(peers)

    def body(*refs):
        ins, lnd = refs[:n], refs[n:2 * n]
        ssem, rsem = refs[2 * n], refs[2 * n + 1]
        x, y, c = _me()
        for t in range(n):
            for k, (fx, fy, fc) in enumerate(peers):
                src, dst = route(ins[t], lnd[t], k, x, y, c)
                cp = pltpu.make_async_remote_copy(
                    src_ref=src, dst_ref=dst, send_sem=ssem.at[npeer * t + k], recv_sem=rsem.at[npeer * t + k],
                    device_id=(_flip(x, fx), _flip(y, fy), _flip(c, fc)), device_id_type=MESH)
                cp.wait_send()
                cp.wait_recv()

    bufs = list(srcs) + list(lands)
    outs = pl.pallas_call(
        body, name=name, out_shape=tuple(pltpu.HBM(b.shape, b.dtype) for b in bufs),
        in_specs=[HBM_SPEC] * (2 * n) + [SEM_SPEC, SEM_SPEC, ANY_SPEC], out_specs=tuple([HBM_SPEC] * (2 * n)),
        input_output_aliases={i: i for i in range(2 * n)},
        compiler_params=pltpu.CompilerParams(has_side_effects=DATAFLOW))(*bufs, send_sems, recv_sems, after)
    return list(outs[:n]), list(outs[n:])


OTHER_CHIPS = [(fx, fy, 0) for fx, fy in CHIP_FLIPS]
SIBLING = [(0, 0, 1)]


def _route_gather(src, land, k, x, y, c):
    return src, land.at[2 * x + y]


def _route_gather_wait(src, land, k, x, y, c):
    fx, fy = CHIP_FLIPS[k]
    return src, land.at[2 * _flip(x, fx) + _flip(y, fy)]


def _route_scatter(src, land, k, x, y, c):
    fx, fy = CHIP_FLIPS[k]
    return src.at[2 * _flip(x, fx) + _flip(y, fy)], land.at[k]


def _route_exchange(src, land, k, x, y, c):
    h = land.shape[1]
    return src.at[:, pl.ds((1 - c) * h, h)], land


def _route_exchange_cols(src, land, k, x, y, c):
    h = land.shape[2]
    return src.at[:, :, pl.ds((1 - c) * h, h)], land


def _allreduce_small(v):
    r = v.shape[0]

    def body(v_ref, o_ref, buf, send_sems, recv_sems):
        x, y, c = _me()
        me = 4 * x + 2 * y + c
        buf[0] = v_ref[...]
        cps = []
        for k in range(1, 8):
            kx, ky, kc = (k >> 2) & 1, (k >> 1) & 1, k & 1
            cp = pltpu.make_async_remote_copy(
                src_ref=v_ref, dst_ref=buf.at[k], send_sem=send_sems.at[k - 1], recv_sem=recv_sems.at[k - 1],
                device_id=(_flip(x, kx), _flip(y, ky), _flip(c, kc)), device_id_type=MESH)
            cp.start()
            cps.append(cp)
        for cp in cps:
            cp.wait()
        acc = buf[me]
        for d in range(1, 8):
            acc = acc + buf[jnp.bitwise_xor(me, d)]
        o_ref[...] = acc

    vm = pl.BlockSpec(memory_space=pltpu.VMEM)
    return pl.pallas_call(
        body, name="allreduce_small", in_specs=[vm], out_specs=vm, out_shape=jax.ShapeDtypeStruct(v.shape, F32),
        scratch_shapes=[pltpu.VMEM((8, r, LANES), F32), pltpu.SemaphoreType.DMA((7,)),
                        pltpu.SemaphoreType.DMA((7,))],
        compiler_params=pltpu.CompilerParams(has_side_effects=True, vmem_limit_bytes=VMEM_LIMIT))(v)


def _grad_exchange_start(g4, tag, cols=False):
    land = (N_CHIPS, g4.shape[1], g4.shape[2] // 2) if cols else (N_CHIPS, g4.shape[1] // 2, g4.shape[2])
    route = _route_exchange_cols if cols else _route_exchange
    send_sems, recv_sems, srcs, lands, token = _push_start(
        [g4], [land], route, SIBLING, name="grad_exchange_start_" + tag)
    return (send_sems, recv_sems, srcs, lands, tag, cols), token


def _grad_scatter_start(state, pos, after):
    send_sems, recv_sems, srcs, lands, tag, cols = state
    route = _route_exchange_cols if cols else _route_exchange
    (g4,), (recv,) = _push_wait(send_sems, recv_sems, srcs, lands, after, route, SIBLING,
                                name="grad_exchange_wait_" + tag)
    return _grad_pair_scatter(g4, recv, pos, tag, cols)


def _grad_pair_scatter(g4, recv, pos, tag, cols=False):
    p16 = _sum_own_half(g4, recv, pos, name="grad_sum_pair_" + tag, cols=cols)
    send_sems, recv_sems, srcs, lands, token = _push_start(
        [p16], [(3,) + p16.shape[1:]], _route_scatter, OTHER_CHIPS, name="grad_scatter_start_" + tag)
    return (g4, recv, send_sems, recv_sems, srcs, lands, tag, cols), token


def _grad_reduce_finish(state, pos, after):
    g4, recv, send_sems, recv_sems, srcs, lands, tag, cols = state
    parts = _push_wait(send_sems, recv_sems, srcs, lands, after, _route_scatter, OTHER_CHIPS,
                       name="grad_scatter_wait_" + tag)[1][0]
    mine = _sum_chips(g4, recv, parts, pos, name="grad_sum_chips_" + tag, cols=cols)
    return mine, _share_halves([mine], name="grad_share_halves_" + tag)[0]


def _local_step(x, tgt, p, w_in_t, w_in_dt, hooks):
    t = x.shape[0]
    tables = _rope_tables(t)
    sinks = p['sinks'].reshape(N_Q_HEADS)

    def told(name, value):
        return tuple(hooks.grad_ready(name, value))

    xn = _rmsnorm_fwd(x, p['norm_mix'], "norm_mix_fwd", deps=hooks.first_deps)
    proj = _matmul(xn, w_in_t, mode='nt', name="in_proj", n_limit=MAIN_WIDTH)
    dt_raw = _matmul(xn, w_in_dt, mode='nt', name="in_proj_dt")[:, :SSD_HEADS]
    attn = _attn_fwd(proj, sinks, tables)
    conv_b = p['ssd_conv_b']
    xbc = _conv_fwd(proj, p['ssd_conv_w'], conv_b, col0=O_XBC, width=CONV_CH, act=True, name="ssd_conv_fwd")
    sp = _ssd_params(dt_raw, p['dt_bias'].reshape(-1), p['a_log'].reshape(-1), p['ssd_d'].reshape(-1))
    y, states = _ssd_fwd(xbc, sp)
    mix = _mix_fwd(attn, y, proj, p['attn_out_norm'], p['ssd_norm'])
    w_out = hooks.weight('w_out', mix)
    h1 = _matmul(mix, w_out, mode='nn', name="out_proj", add=x)
    hn = _rmsnorm_fwd(h1, p['norm_ffn'], "norm_ffn_fwd")
    w_up = hooks.weight('w_up', hn)
    u0 = _matmul(hn, w_up, mode='nn', name="ffn_up", b_owner=True, tn=1408)
    a = _ffn_act_fwd(u0, p['ffn_conv_w'], p['ffn_conv_b'])
    w_down = hooks.weight('w_down', a)
    h2 = _matmul(a, w_down, mode='nn', name="ffn_down", add=h1, tk=2816)
    loss, dh2, dh2_16, g_norm_final = _final_loss(h2, p['norm_final'].reshape(1, D_MODEL), tgt)

    g = {}
    da = _matmul(dh2_16, w_down, mode='nt', name="ffn_down_dx", out_dtype=BF16, tn=1408)
    g['w_down'] = _matmul(a, dh2_16, mode='tn', name="ffn_down_dw", tm=1408)
    dep = told('w_down', g['w_down'])
    du0, dcw, dcb = _ffn_act_bwd(u0, p['ffn_conv_w'], p['ffn_conv_b'], da)
    g['ffn_conv_w'] = dcw.transpose(1, 0, 2).reshape(FFN_CONV, 2 * D_FF)
    g['ffn_conv_b'] = dcb.transpose(1, 0, 2).reshape(1, 2 * D_FF)
    g['w_up'] = _matmul(hn, du0, mode='tn', name="ffn_up_dw", deps=dep, b_halves=True, owner_major=True,
                        tn=1408)
    dep = told('w_up', g['w_up'])
    dhn = _matmul(du0, w_up, mode='nt', name="ffn_up_dx", out_dtype=BF16, deps=dep, a_halves=True,
                  b_owner=True, tk=2816)
    dh1, dh1_16, g['norm_ffn'] = _rmsnorm_bwd(h1, p['norm_ffn'], dhn, dh2, "norm_ffn_bwd")

    g['w_out'] = _matmul(mix, dh1_16, mode='tn', name="out_proj_dw")
    dep = told('w_out', g['w_out'])
    dmix = _matmul(dh1_16, w_out, mode='nt', name="out_proj_dx", out_dtype=BF16, deps=dep)
    dattn, dy, dz, g['attn_out_norm'], g['ssd_norm'] = _mix_bwd(dmix, attn, y, proj, p['attn_out_norm'],
                                                                p['ssd_norm'])
    dq, dk, dv, dsink = _attn_bwd(proj, sinks, tables, dattn)
    g['sinks'] = dsink[:, :, 0].reshape(1, N_Q_HEADS)
    dxs, dbm, dcm, ddt8, dpar = _ssd_bwd(xbc, sp, states, dy)
    dpar = dpar[:, :, ::SSD_HEAD_DIM]
    g['dt_bias'] = dpar[:, 0, :].reshape(1, SSD_HEADS)
    g['a_log'] = dpar[:, 1, :].reshape(1, SSD_HEADS)
    g['ssd_d'] = dpar[:, 2, :].reshape(1, SSD_HEADS)
    dxbc_act = jnp.concatenate([dxs, dbm, dcm], axis=1)
    dconv = _conv_silu_dact(proj, p['ssd_conv_w'], conv_b, dxbc_act, col0=O_XBC, width=CONV_CH,
                            name="ssd_conv_dact")
    dxbc, g['ssd_conv_w'], g['ssd_conv_b'] = _conv_bwd(proj, p['ssd_conv_w'], dconv, col0=O_XBC, width=CONV_CH,
                                                       name="ssd_conv_bwd")
    dproj = jnp.concatenate([dq, dk, dv, dz, dxbc], axis=1)
    ddt = ddt8.transpose(2, 0, 1).reshape(t, SSD_HEADS)
    ddt_pad = jnp.pad(ddt, ((0, 0), (0, LANES - SSD_HEADS))).astype(BF16)
    g['w_in'] = (_matmul(dproj, xn, mode='tn', name="in_proj_dw", m_rows=IN_PROJ_WIDTH),
                 _matmul(ddt_pad, xn, mode='tn', name="in_proj_dt_dw"))
    dep = told('w_in', g['w_in'])
    dxn_dt = _matmul(ddt_pad, w_in_dt, mode='nn', name="in_proj_dt_dx", deps=dep)
    dxn = _matmul(dproj, w_in_t, mode='nn', name="in_proj_dx", out_dtype=BF16, add=dxn_dt, k_limit=MAIN_WIDTH,
                  tk=2304)
    dx, _, g['norm_mix'] = _rmsnorm_bwd(x, p['norm_mix'], dxn, dh1, "norm_mix_bwd")
    g['norm_final'] = g_norm_final
    return loss, dx, g


def _pack(arrs):
    flat = jnp.concatenate([a.reshape(-1) for a in arrs])
    n = flat.shape[0]
    rows = -(-n // LANES)
    rows = -(-rows // 8) * 8
    return jnp.pad(flat, (0, rows * LANES - n)).reshape(rows, LANES)


def _unpack(packed, shapes):
    flat = packed.reshape(-1)
    out, off = [], 0
    for s in shapes:
        n = 1
        for d in s:
            n *= d
        out.append(flat[off:off + n].reshape(s))
        off += n
    return out


class _StepHooks:
    def __init__(self, first_deps, weight, grad_ready):
        self.first_deps = first_deps
        self.weight = weight
        self.grad_ready = grad_ready


def kernel(x, norm_mix, w_in, sinks, attn_out_norm, ssd_conv_w, ssd_conv_b, dt_bias, a_log, ssd_d, ssd_norm, w_out, norm_ffn, w_up, ffn_conv_w, ffn_conv_b, w_down, norm_final, loss_target, m_norm_mix, m_w_in, m_sinks, m_attn_out_norm, m_ssd_conv_w, m_ssd_conv_b, m_dt_bias, m_a_log, m_ssd_d, m_ssd_norm, m_w_out, m_norm_ffn, m_w_up, m_ffn_conv_w, m_ffn_conv_b, m_w_down, m_norm_final, v_norm_mix, v_w_in, v_sinks, v_attn_out_norm, v_ssd_conv_w, v_ssd_conv_b, v_dt_bias, v_a_log, v_ssd_d, v_ssd_norm, v_w_out, v_norm_ffn, v_w_up, v_ffn_conv_w, v_ffn_conv_b, v_w_down, v_norm_final):
    args = dict(locals())
    w = {n: args[n] for n in WEIGHTS}
    m = {n: args['m_' + n] for n in WEIGHTS}
    v = {n: args['v_' + n] for n in WEIGHTS}
    xi, yi, ci = _me()
    chip = 2 * xi + yi
    pos = jnp.stack([ci, chip]).astype(jnp.int32)

    def place(shard, full_cols):
        z = jnp.zeros((shard.shape[0], full_cols), F32)
        return lax.dynamic_update_slice(z, shard * 0.5, (0, chip * shard.shape[1]))

    conv_pack = _pack([place(ssd_conv_w[0], CONV_CH), place(ffn_conv_w[0], 2 * D_FF)])
    conv_full = _allreduce_small(conv_pack)
    ssd_conv_w_full, ffn_conv_w_full = _unpack(conv_full, [(SSD_CONV, CONV_CH), (FFN_CONV, 2 * D_FF)])

    w_in_t, m_in_t, v_in_t = (jnp.transpose(a[0]) for a in (w_in, m_w_in, v_w_in))
    in_shard = w_in_t.astype(BF16)
    (gathered,), order = _allgather_weights([in_shard], conv_full, cols=True)
    full_in_t = lax.dynamic_update_slice(gathered, in_shard[None], (chip, 0, 0)).reshape(IN_PROJ_WIDTH, D_MODEL)
    w_in_dt = jnp.pad(full_in_t[MAIN_WIDTH:], ((0, LANES - SSD_HEADS), (0, 0)))
    gathers = {}
    order = order[:1, :1]
    for n, shard in (('w_out', w_out[0]), ('w_up', w_up[0]), ('w_down', w_down[0])):
        shard = (shard + order).astype(BF16)
        gathers[n] = _push_start([shard], [(N_CHIPS,) + shard.shape], _route_gather, OTHER_CHIPS,
                                 name="gather_start_" + n)
        order = gathers[n][4][:1, :1]
    first_deps = [gathers['w_down'][4]]

    def weight(name, after):
        send_sems, recv_sems, srcs, lands, _ = gathers[name]
        (own,), (got,) = _push_wait(send_sems, recv_sems, srcs, lands, after, _route_gather_wait, OTHER_CHIPS,
                                    name="gather_wait_" + name)
        whole = lax.dynamic_update_slice(got, own[None], (chip, 0, 0))
        return whole if name == 'w_up' else whole.reshape(-1, D_MODEL)

    reductions, exchanging = {}, {}

    def flush(after):
        tokens = []
        for prev in list(exchanging):
            reductions[prev], token = _grad_scatter_start(exchanging.pop(prev), pos, after)
            tokens.append(token)
        return tokens

    def grad_ready(name, value):
        if name == 'w_in':
            main, dtp = value
            value = lax.dynamic_update_slice(main, dtp[:SSD_HEADS], (MAIN_WIDTH, 0))
        g4 = value if value.ndim == 3 else value.reshape(N_CHIPS, -1, value.shape[1])
        tokens = flush(g4)
        exchanging[name], token = _grad_exchange_start(g4, name, cols=(name == 'w_in'))
        return tokens + [token]

    small = {
        'norm_mix': norm_mix, 'sinks': sinks, 'attn_out_norm': attn_out_norm, 'ssd_conv_w': ssd_conv_w_full,
        'ssd_conv_b': ssd_conv_b, 'dt_bias': dt_bias, 'a_log': a_log, 'ssd_d': ssd_d, 'ssd_norm': ssd_norm,
        'norm_ffn': norm_ffn, 'ffn_conv_w': ffn_conv_w_full, 'ffn_conv_b': ffn_conv_b, 'norm_final': norm_final,
    }
    loss, dx, g = _local_step(x[0], loss_target[0], small, full_in_t, w_in_dt,
                              _StepHooks(tuple(first_deps), weight, grad_ready))

    small_names = [n for n in WEIGHTS if n not in BIG]
    small_g = [loss[:, :1]] + [g[n] for n in small_names]
    small_shapes = [(1, 1)] + [tuple(a.shape) for a in small_g[1:]]
    reduced = _allreduce_small(_pack(small_g))
    flush(reduced)
    red = _unpack(reduced, small_shapes)
    loss_out = red[0].reshape(())
    gsm = dict(zip(small_names, red[1:]))
    gsm['ssd_conv_w'] = lax.dynamic_slice(gsm['ssd_conv_w'], (0, chip * ssd_conv_w.shape[2]),
                                          (SSD_CONV, ssd_conv_w.shape[2]))
    gsm['ffn_conv_w'] = lax.dynamic_slice(gsm['ffn_conv_w'], (0, chip * ffn_conv_w.shape[2]),
                                          (FFN_CONV, ffn_conv_w.shape[2]))

    grads, deltas, new_m, new_v = {}, {}, {}, {}
    after = reduced
    for n in ('w_down', 'w_up', 'w_out', 'w_in'):
        mine, theirs = _grad_reduce_finish(reductions[n], pos, after)
        if n == 'w_in':
            outs = _adamw_halves(w_in_t, mine, theirs, m_in_t, v_in_t, pos, name="adamw_" + n, cols=True)
            outs = [jnp.transpose(o) for o in outs]
        else:
            outs = _adamw_halves(w[n][0], mine, theirs, m[n][0], v[n][0], pos, name="adamw_" + n)
        after = outs[1]
        grads[n], deltas[n], new_m[n], new_v[n] = [o[None] for o in outs]
    shapes = [tuple(w[n].shape) for n in small_names]
    gp = _pack([gsm[n] for n in small_names])
    d, m2, v2 = _adamw(_pack([w[n] for n in small_names]), gp, _pack([m[n] for n in small_names]),
                       _pack([v[n] for n in small_names]), name="adamw_small")
    for n, gg, dd, mm, vv in zip(small_names, _unpack(gp, shapes), _unpack(d, shapes), _unpack(m2, shapes),
                                 _unpack(v2, shapes)):
        grads[n], deltas[n], new_m[n], new_v[n] = gg, dd, mm, vv

    return (loss_out, dx[None], *[grads[n] for n in WEIGHTS], *[deltas[n] for n in WEIGHTS],
            *[new_m[n] for n in WEIGHTS], *[new_v[n] for n in WEIGHTS])
```

```python
import functools

import jax
import jax.numpy as jnp
from jax import lax
from jax.experimental import pallas as pl
from jax.experimental.pallas import tpu as pltpu

F32 = jnp.float32
BF16 = jnp.bfloat16

D_MODEL = 2048
N_Q_HEADS = 32
N_KV_HEADS = 8
HEAD_DIM = 64
WINDOW = 128
ATTN_BLOCK = 128
ROT_DIM = 16
ROPE_THETA = 500000.0
SSD_HEADS = 32
SSD_HEAD_DIM = 64
SSD_INNER = 2048
SSD_GROUPS = 8
SSD_STATE = 128
SSD_CONV = 4
SSD_CHUNK = 128
ATTN_WIDTH = 2048
KV_WIDTH = 512
BC_WIDTH = 1024
CONV_CH = 4096
IN_PROJ_WIDTH = 9248
MAIN_WIDTH = 9216
D_FF = 5632
FFN_CONV = 3
EPS = 1e-6
O_Q, O_K, O_V, O_Z, O_XBC, O_DT = 0, 2048, 2560, 3072, 5120, 9216

ADAM_LR = 0.001
ADAM_B1 = 0.9
ADAM_B2 = 0.999
ADAM_EPS = 1e-08
ADAM_WD = 0.01
ADAM_STEP = 10

N_CHIPS = 4
NEG = -1e30
LANES = 128
VMEM_LIMIT = 48 * 1024 * 1024
MESH = pl.DeviceIdType.MESH
HBM_SPEC = pl.BlockSpec(memory_space=pltpu.HBM)
TOKEN = jax.ShapeDtypeStruct((8, LANES), F32)

WEIGHTS = ['norm_mix', 'w_in', 'sinks', 'attn_out_norm', 'ssd_conv_w', 'ssd_conv_b', 'dt_bias', 'a_log', 'ssd_d',
           'ssd_norm', 'w_out', 'norm_ffn', 'w_up', 'ffn_conv_w', 'ffn_conv_b', 'w_down', 'norm_final']
BIG = ['w_in', 'w_out', 'w_up', 'w_down']


def _cp(sem=None, vmem=VMEM_LIMIT):
    kw = {'vmem_limit_bytes': vmem}
    if sem is not None:
        kw['dimension_semantics'] = sem
    return pltpu.CompilerParams(**kw)


def _tile(n, pref):
    if n <= pref:
        return n
    t = (pref // LANES) * LANES
    while t > LANES and n % t:
        t -= LANES
    assert n % t == 0, (n, pref)
    return t


def _rows(n, pref):
    t = min(n, pref)
    while n % t:
        t -= 8
    if 4 * t < pref:
        t = pref
        while n % t:
            t += 8
    return t


def _iota(shape, dim):
    return lax.broadcasted_iota(jnp.int32, shape, dim)


def _dot(a, b, mode='nn'):
    dn = {'nn': (((1,), (0,)), ((), ())), 'nt': (((1,), (1,)), ((), ())), 'tn': (((0,), (0,)), ((), ()))}[mode]
    return lax.dot_general(a.astype(BF16), b.astype(BF16), dn, preferred_element_type=F32)


def _dot_exact(a, b):
    return lax.dot_general(a, b, (((1,), (0,)), ((), ())), precision=lax.Precision.HIGHEST,
                           preferred_element_type=F32)


def _sigmoid(x):
    return 1.0 / (1.0 + jnp.exp(-x))


def _softplus(x):
    return jnp.maximum(x, 0.0) + jnp.log(1.0 + jnp.exp(-jnp.abs(x)))


def _matmul(a, b, *, mode, name, out_dtype=F32, add=None, deps=(), tm=1024, tn=1024, tk=2048,
            a_halves=False, b_halves=False, b_owner=False, owner_major=False, n_limit=None, k_limit=None,
            m_rows=None):
    ash, bsh = (a.shape[1:] if a_halves else a.shape), (b.shape[1:] if (b_halves or b_owner) else b.shape)
    if mode == 'nn':
        (m, k), (k2, n) = ash, bsh
    elif mode == 'nt':
        (m, k), (n, k2) = ash, bsh
    else:
        (k, m), (k2, n) = ash, bsh
    if n_limit is not None:
        assert mode == 'nt' and n_limit <= n
        n = n_limit
    if k_limit is not None:
        assert mode == 'nn' and k_limit <= k2
        k2 = k_limit
    if a_halves:
        assert mode == 'nt'
        k = 2 * k
    if b_halves:
        assert mode == 'tn'
        n = 2 * n
    if b_owner:
        assert mode in ('nn', 'nt')
        if mode == 'nn':
            n = 4 * n
        else:
            k2 = 4 * k2
    assert k == k2, (a.shape, b.shape, mode)
    tm = _tile(m, tm)
    tn = _tile(n // 4 if (owner_major or (b_owner and mode == 'nn')) else (n // 2 if b_halves else n), tn)
    tk = _tile(k // 4 if (b_owner and mode == 'nt') else (k // 2 if a_halves else k), tk)
    nk = k // tk
    has_add = add is not None
    assert not (has_add and owner_major)

    def body(*refs):
        a_ref, b_ref = refs[:2]
        add_ref = refs[2] if has_add else None

        def finish(r, o_ref):
            if has_add:
                r = r + add_ref[...].astype(F32)
            o_ref[...] = r.astype(out_dtype)

        if nk == 1:
            finish(_dot(a_ref[...], b_ref[...], mode), refs[-1])
            return
        o_ref, acc = refs[-2:]
        kk = pl.program_id(2)

        @pl.when(kk == 0)
        def _():
            acc[...] = _dot(a_ref[...], b_ref[...], mode)

        @pl.when((kk > 0) & (kk < nk - 1))
        def _():
            acc[...] += _dot(a_ref[...], b_ref[...], mode)

        @pl.when(kk == nk - 1)
        def _():
            finish(acc[...] + _dot(a_ref[...], b_ref[...], mode), o_ref)

    if mode == 'tn':
        a_spec = pl.BlockSpec((tk, tm), lambda i, j, kk: (kk, i))
    elif a_halves:
        nkh = nk // 2
        a_spec = pl.BlockSpec((None, tm, tk), lambda i, j, kk: (kk // nkh, i, kk % nkh))
    else:
        a_spec = pl.BlockSpec((tm, tk), lambda i, j, kk: (i, kk))
    if mode == 'nt' and b_owner:
        nkq = nk // 4
        b_spec = pl.BlockSpec((None, tn, tk), lambda i, j, kk: (kk // nkq, j, kk % nkq))
    elif mode == 'nt':
        b_spec = pl.BlockSpec((tn, tk), lambda i, j, kk: (j, kk))
    elif b_owner:
        njq = (n // 4) // tn
        b_spec = pl.BlockSpec((None, tk, tn), lambda i, j, kk: (j // njq, kk, j % njq))
    elif b_halves:
        njh = (n // 2) // tn
        b_spec = pl.BlockSpec((None, tk, tn), lambda i, j, kk: (j // njh, kk, j % njh))
    else:
        b_spec = pl.BlockSpec((tk, tn), lambda i, j, kk: (kk, j))
    if owner_major:
        njo = (n // 4) // tn
        o_spec = pl.BlockSpec((None, tm, tn), lambda i, j, kk: (j // njo, i, j % njo))
        out_shape = jax.ShapeDtypeStruct((N_CHIPS, m, n // 4), out_dtype)
    else:
        o_spec = pl.BlockSpec((tm, tn), lambda i, j, kk: (i, j))
        out_shape = jax.ShapeDtypeStruct((m if m_rows is None else m_rows, n), out_dtype)
    dep_spec = pl.BlockSpec((8, LANES), lambda i, j, kk: (0, 0))
    in_specs = [a_spec, b_spec] + ([pl.BlockSpec((tm, tn), lambda i, j, kk: (i, j))] if has_add else [])
    in_specs += [dep_spec] * len(deps)
    args = (a, b) + ((add,) if has_add else ()) + tuple(deps)
    return pl.pallas_call(
        body, name=name, grid=(m // tm, n // tn, nk), in_specs=in_specs, out_specs=o_spec, out_shape=out_shape,
        scratch_shapes=[pltpu.VMEM((tm, tn), F32)] if nk > 1 else [],
        compiler_params=_cp(("parallel", "parallel", "arbitrary")))(*args)


def _rmsnorm_fwd(x, g, name, deps=()):
    t, d = x.shape
    tb = _rows(t, 256)

    def body(x_ref, g_ref, *rest):
        o_ref = rest[-1]
        xv = x_ref[...]
        r = lax.rsqrt(jnp.mean(xv * xv, axis=-1, keepdims=True) + EPS)
        o_ref[...] = (xv * r * g_ref[...]).astype(BF16)

    dep_spec = pl.BlockSpec((8, LANES), lambda i: (0, 0))
    return pl.pallas_call(
        body, name=name, grid=(t // tb,),
        in_specs=[pl.BlockSpec((tb, d), lambda i: (i, 0)), pl.BlockSpec((1, d), lambda i: (0, 0))]
        + [dep_spec] * len(deps),
        out_specs=pl.BlockSpec((tb, d), lambda i: (i, 0)), out_shape=jax.ShapeDtypeStruct((t, d), BF16),
        compiler_params=_cp(("parallel",)))(x, g, *deps)


def _rmsnorm_bwd(x, g, dy, res, name, deps=()):
    t, d = x.shape
    tb = _rows(t, 256)

    def body(x_ref, g_ref, dy_ref, res_ref, *rest):
        dx_ref, dx16_ref, dg_ref = rest[-3:]
        i = pl.program_id(0)
        xv = x_ref[...]
        dyv = dy_ref[...].astype(F32)
        r = lax.rsqrt(jnp.mean(xv * xv, axis=-1, keepdims=True) + EPS)
        u = dyv * g_ref[...]
        dx = r * u - xv * (r * r * r * jnp.mean(u * xv, axis=-1, keepdims=True)) + res_ref[...]
        dx_ref[...] = dx
        dx16_ref[...] = dx.astype(BF16)
        part = jnp.sum(dyv * xv * r, axis=0, keepdims=True)

        @pl.when(i == 0)
        def _():
            dg_ref[...] = part

        @pl.when(i > 0)
        def _():
            dg_ref[...] += part

    row = pl.BlockSpec((tb, d), lambda i: (i, 0))
    vec = pl.BlockSpec((1, d), lambda i: (0, 0))
    return pl.pallas_call(
        body, name=name, grid=(t // tb,),
        in_specs=[row, vec, row, row] + [pl.BlockSpec((8, LANES), lambda i: (0, 0))] * len(deps),
        out_specs=[row, row, vec],
        out_shape=[jax.ShapeDtypeStruct((t, d), F32), jax.ShapeDtypeStruct((t, d), BF16),
                   jax.ShapeDtypeStruct((1, d), F32)],
        compiler_params=_cp(("arbitrary",)))(x, g, dy, res, *deps)


def _final_loss(h, g, tgt):
    t, d = h.shape
    tb = _rows(t, 256)

    def body(h_ref, g_ref, t_ref, loss_ref, dh_ref, dh16_ref, dg_ref):
        i = pl.program_id(0)
        hv = h_ref[...]
        gv = g_ref[...]
        r = lax.rsqrt(jnp.mean(hv * hv, axis=-1, keepdims=True) + EPS)
        y = hv * r * gv
        diff = y - t_ref[...]
        lpart = jnp.sum(jnp.sum(diff * diff, axis=1, keepdims=True), axis=0, keepdims=True) * (0.5 / d)
        dy = diff * (1.0 / d)
        u = dy * gv
        dh = r * u - hv * (r * r * r * jnp.mean(u * hv, axis=-1, keepdims=True))
        dh_ref[...] = dh
        dh16_ref[...] = dh.astype(BF16)
        gpart = jnp.sum(dy * hv * r, axis=0, keepdims=True)
        lrow = jnp.broadcast_to(lpart, (1, LANES))

        @pl.when(i == 0)
        def _():
            loss_ref[...] = lrow
            dg_ref[...] = gpart

        @pl.when(i > 0)
        def _():
            loss_ref[...] += lrow
            dg_ref[...] += gpart

    row = pl.BlockSpec((tb, d), lambda i: (i, 0))
    vec = pl.BlockSpec((1, d), lambda i: (0, 0))
    return pl.pallas_call(
        body, name="final_loss", grid=(t // tb,), in_specs=[row, vec, row],
        out_specs=[pl.BlockSpec((1, LANES), lambda i: (0, 0)), row, row, vec],
        out_shape=[jax.ShapeDtypeStruct((1, LANES), F32), jax.ShapeDtypeStruct((t, d), F32),
                   jax.ShapeDtypeStruct((t, d), BF16), jax.ShapeDtypeStruct((1, d), F32)],
        compiler_params=_cp(("arbitrary",)))(h, g, tgt)


def _rope_tables(t):
    pos = jnp.arange(t, dtype=F32)
    inv = 1.0 / (ROPE_THETA ** (jnp.arange(0, ROT_DIM, 2, dtype=F32) / ROT_DIM))
    ang = pos[:, None] * inv[None, :]
    cos, sin = jnp.cos(ang), jnp.sin(ang)
    half = ROT_DIM // 2
    rest = HEAD_DIM - ROT_DIM
    c = jnp.concatenate([cos, cos, jnp.ones((t, rest), F32)], axis=1)
    s1 = jnp.concatenate([-sin, jnp.zeros((t, half + rest), F32)], axis=1)
    s2 = jnp.concatenate([jnp.zeros((t, half), F32), sin, jnp.zeros((t, rest), F32)], axis=1)
    return jnp.concatenate([jnp.tile(v, (1, LANES // HEAD_DIM)) for v in (c, s1, s2)], axis=1)


def _split_tables(tab):
    return tab[:, :LANES], tab[:, LANES:2 * LANES], tab[:, 2 * LANES:]


def _rope(x, c, s1, s2):
    half = ROT_DIM // 2
    return x * c + pltpu.roll(x, LANES - half, 1) * s1 + pltpu.roll(x, half, 1) * s2


def _rope_t(g, c, s1, s2):
    half = ROT_DIM // 2
    return g * c + pltpu.roll(g * s1, half, 1) + pltpu.roll(g * s2, LANES - half, 1)


def _band_masks(i, heads):
    n = heads * ATTN_BLOCK
    q = jnp.bitwise_and(_iota((n, ATTN_BLOCK), 0), ATTN_BLOCK - 1)
    j = _iota((n, ATTN_BLOCK), 1)
    upper = j > q
    return upper, upper & (j < jnp.where(i > 0, 0, ATTN_BLOCK))


def _fold_band(full, upper):
    return jnp.where(upper, full[:, :ATTN_BLOCK], full[:, ATTN_BLOCK:])


def _unfold_band(band, upper):
    return jnp.concatenate([jnp.where(upper, band, 0.0), jnp.where(upper, 0.0, band)], axis=1)


def _half_masks():
    lane = _iota((1, LANES), 1)
    return [(lane < HEAD_DIM).astype(F32), (lane >= HEAD_DIM).astype(F32)]


def _stack_heads(blocks, hm, j):
    pieces = []
    for r in range(4):
        qb, half = (4 * j + r) // 2, (4 * j + r) % 2
        piece = blocks[qb] * hm[half]
        if half != j:
            piece = pltpu.roll(piece, HEAD_DIM, 1)
        pieces.append(piece)
    return jnp.concatenate(pieces, axis=0)


def _unstack_heads(stacked, j):
    out = []
    for qb in (2 * j, 2 * j + 1):
        acc = None
        for half in range(2):
            r = 2 * qb + half - 4 * j
            piece = stacked[r * ATTN_BLOCK:(r + 1) * ATTN_BLOCK]
            if half != j:
                piece = pltpu.roll(piece, HEAD_DIM, 1)
            acc = piece if acc is None else acc + piece
        out.append((qb, acc))
    return out


def _sink_column(sink_ref, base):
    return jnp.concatenate([jnp.full((ATTN_BLOCK, 1), sink_ref[base + r], F32) for r in range(4)], axis=0)


def _attn_specs(nb_clamp):
    blk = ATTN_BLOCK
    kb, vb = O_K // LANES, O_V // LANES

    def cur(i):
        return jnp.minimum(i, nb_clamp)

    def prev(i):
        return jnp.maximum(jnp.minimum(i, nb_clamp + 1) - 1, 0)

    q = pl.BlockSpec((blk, 512), lambda p, i: (cur(i), p))
    kc = pl.BlockSpec((blk, LANES), lambda p, i: (cur(i), kb + p))
    kp = pl.BlockSpec((blk, LANES), lambda p, i: (prev(i), kb + p))
    vc = pl.BlockSpec((blk, LANES), lambda p, i: (cur(i), vb + p))
    vp = pl.BlockSpec((blk, LANES), lambda p, i: (prev(i), vb + p))
    tc = pl.BlockSpec((blk, 3 * LANES), lambda p, i: (cur(i), 0))
    tp = pl.BlockSpec((blk, 3 * LANES), lambda p, i: (prev(i), 0))
    return q, kc, kp, vc, vp, tc, tp


def _attn_fwd(proj, sinks, tables):
    t = proj.shape[0]
    nb = t // ATTN_BLOCK
    scale = HEAD_DIM ** -0.5

    def body(sink_ref, q_ref, kc_ref, kp_ref, vc_ref, vp_ref, tc_ref, tp_ref, o_ref):
        p = pl.program_id(0)
        i = pl.program_id(1)
        cc, s1c, s2c = _split_tables(tc_ref[...])
        kband = jnp.concatenate([_rope(kp_ref[...], *_split_tables(tp_ref[...])),
                                 _rope(kc_ref[...], cc, s1c, s2c)], axis=0).astype(BF16)
        vband = jnp.concatenate([vp_ref[...], vc_ref[...]], axis=0)
        hm = _half_masks()
        vsel = [(vband * hm[j]).astype(BF16) for j in range(2)]
        upper, dropped = _band_masks(i, 1)
        for qb in range(4):
            qr = _rope(q_ref[:, qb * LANES:(qb + 1) * LANES], cc, s1c, s2c)
            acc = jnp.zeros((ATTN_BLOCK, LANES), F32)
            for half in range(2):
                hh = qb * 2 + half
                j = hh // 4
                qs = qr * hm[half]
                if half != j:
                    qs = pltpu.roll(qs, HEAD_DIM, 1)
                s = jnp.where(dropped, NEG, _fold_band(_dot(qs, kband, 'nt'), upper) * scale)
                sink = sink_ref[p * 8 + hh]
                m = jnp.maximum(jnp.max(s, axis=1, keepdims=True), sink)
                pe = jnp.exp(s - m)
                den = jnp.sum(pe, axis=1, keepdims=True) + jnp.exp(sink - m)
                o = _dot(_unfold_band(pe / den, upper), vsel[j])
                if half != j:
                    o = pltpu.roll(o, HEAD_DIM, 1)
                acc = acc + o
            o_ref[:, qb * LANES:(qb + 1) * LANES] = acc

    q, kc, kp, vc, vp, tc, tp = _attn_specs(nb - 1)
    smem = pl.BlockSpec(memory_space=pltpu.SMEM)
    return pl.pallas_call(
        body, name="attn_fwd", grid=(4, nb),
        in_specs=[smem, q, kc, kp, vc, vp, tc, tp],
        out_specs=pl.BlockSpec((ATTN_BLOCK, 512), lambda p, i: (i, p)),
        out_shape=jax.ShapeDtypeStruct((t, ATTN_WIDTH), F32),
        compiler_params=_cp(("parallel", "arbitrary")))(sinks, proj, proj, proj, proj, proj, tables, tables)


def _attn_bwd(proj, sinks, tables, dout):
    t = proj.shape[0]
    nb = t // ATTN_BLOCK
    scale = HEAD_DIM ** -0.5

    def body(sink_ref, q_ref, kc_ref, kp_ref, vc_ref, vp_ref, tc_ref, tp_ref,
             do_ref, dq_ref, dk_ref, dv_ref, ds_ref, carry_k, carry_v):
        p = pl.program_id(0)
        i = pl.program_id(1)
        ptab = _split_tables(tp_ref[...])

        @pl.when(i == 0)
        def _():
            carry_k[...] = jnp.zeros_like(carry_k)
            carry_v[...] = jnp.zeros_like(carry_v)
            ds_ref[...] = jnp.zeros_like(ds_ref)

        @pl.when(i < nb)
        def _():
            cc, s1c, s2c = _split_tables(tc_ref[...])
            kband = jnp.concatenate([_rope(kp_ref[...], *ptab), _rope(kc_ref[...], cc, s1c, s2c)], axis=0)
            vband = jnp.concatenate([vp_ref[...], vc_ref[...]], axis=0)
            hm = _half_masks()
            kband16 = kband.astype(BF16)
            vband16 = vband.astype(BF16)
            upper, dropped = _band_masks(i, 4)
            dkb = jnp.zeros((2 * ATTN_BLOCK, LANES), F32)
            dvb = jnp.zeros((2 * ATTN_BLOCK, LANES), F32)
            row8 = _iota((8, LANES), 0)
            dsink = jnp.zeros((8, LANES), F32)
            qr = [_rope(q_ref[:, qb * LANES:(qb + 1) * LANES], cc, s1c, s2c) for qb in range(4)]
            dob = [do_ref[:, qb * LANES:(qb + 1) * LANES] for qb in range(4)]
            for j in range(2):
                qst = _stack_heads(qr, hm, j).astype(BF16)
                dost = _stack_heads(dob, hm, j).astype(BF16)
                s = jnp.where(dropped, NEG, _fold_band(_dot(qst, kband16, 'nt'), upper) * scale)
                sink = _sink_column(sink_ref, p * 8 + 4 * j)
                m = jnp.maximum(jnp.max(s, axis=1, keepdims=True), sink)
                pe = jnp.exp(s - m)
                psink = jnp.exp(sink - m)
                den = jnp.sum(pe, axis=1, keepdims=True) + psink
                pr = pe / den
                dvb = dvb + _dot(_unfold_band(pr, upper).T, dost)
                dp = _fold_band(_dot(dost, vband16, 'nt'), upper)
                delta = jnp.sum(pr * dp, axis=1, keepdims=True)
                dsc = _unfold_band(pr * (dp - delta) * scale, upper)
                dsk = psink / den * delta
                for r in range(4):
                    part = jnp.sum(dsk[r * ATTN_BLOCK:(r + 1) * ATTN_BLOCK])
                    dsink = dsink + jnp.where(row8 == 4 * j + r, -part, 0.0)
                for qb, dqb in _unstack_heads(_dot(dsc, kband * hm[j]), j):
                    dq_ref[:, qb * LANES:(qb + 1) * LANES] = _rope_t(dqb, cc, s1c, s2c).astype(BF16)
                dkb = dkb + _dot(dsc.T, qst)
            ds_ref[0] += dsink
            dk_ref[...] = _rope_t(carry_k[...] + dkb[:ATTN_BLOCK], *ptab).astype(BF16)
            dv_ref[...] = (carry_v[...] + dvb[:ATTN_BLOCK]).astype(BF16)
            carry_k[...] = dkb[ATTN_BLOCK:]
            carry_v[...] = dvb[ATTN_BLOCK:]

        @pl.when(i == nb)
        def _():
            dk_ref[...] = _rope_t(carry_k[...], *ptab).astype(BF16)
            dv_ref[...] = carry_v[...].astype(BF16)

    q, kc, kp, vc, vp, tc, tp = _attn_specs(nb - 1)
    smem = pl.BlockSpec(memory_space=pltpu.SMEM)
    qblk = pl.BlockSpec((ATTN_BLOCK, 512), lambda p, i: (jnp.minimum(i, nb - 1), p))
    kvout = pl.BlockSpec((ATTN_BLOCK, LANES), lambda p, i: (jnp.maximum(i - 1, 0), p))
    return pl.pallas_call(
        body, name="attn_bwd", grid=(4, nb + 1),
        in_specs=[smem, q, kc, kp, vc, vp, tc, tp, qblk],
        out_specs=[qblk, kvout, kvout, pl.BlockSpec((1, 8, LANES), lambda p, i: (p, 0, 0))],
        out_shape=[jax.ShapeDtypeStruct((t, ATTN_WIDTH), BF16), jax.ShapeDtypeStruct((t, KV_WIDTH), BF16),
                   jax.ShapeDtypeStruct((t, KV_WIDTH), BF16), jax.ShapeDtypeStruct((4, 8, LANES), F32)],
        scratch_shapes=[pltpu.VMEM((ATTN_BLOCK, LANES), F32), pltpu.VMEM((ATTN_BLOCK, LANES), F32)],
        compiler_params=_cp(("parallel", "arbitrary")))(sinks, proj, proj, proj, proj, proj, tables, tables, dout)


def _shift_rows(x, prev8, j):
    r = pltpu.roll(x, j, 0)
    head = jnp.where(_iota((8, 1), 0) < j, pltpu.roll(prev8, j, 0), r[:8])
    if x.shape[0] == 8:
        return head
    return jnp.concatenate([head, r[8:]], axis=0)


def _shift_rows_up(x, next8, j):
    n = x.shape[0]
    r = pltpu.roll(x, n - j, 0)
    tail = jnp.where(_iota((8, 1), 0) >= 8 - j, pltpu.roll(next8, 8 - j, 0), r[n - 8:])
    return jnp.concatenate([r[:n - 8], tail], axis=0)


def _conv_apply(x, prev8, w, b, taps):
    u = b + x * w[taps - 1:taps]
    for j in range(1, taps):
        u = u + _shift_rows(x, prev8, j) * w[taps - 1 - j:taps - j]
    return u


def _conv_grads(du, du_next8, x, w, taps):
    dx = du * w[taps - 1:taps]
    rowk = _iota((taps, 1), 0)
    dw = jnp.where(rowk == taps - 1, jnp.sum(du * x, axis=0, keepdims=True), 0.0)
    for j in range(1, taps):
        ahead = _shift_rows_up(du, du_next8, j)
        dx = dx + ahead * w[taps - 1 - j:taps - j]
        dw = dw + jnp.where(rowk == taps - 1 - j, jnp.sum(ahead * x, axis=0, keepdims=True), 0.0)
    return dx, dw, jnp.sum(du, axis=0, keepdims=True)


def _conv_specs(tb, tc, col0, t):
    c0 = col0 // tc
    cur = pl.BlockSpec((tb, tc), lambda j, i: (i, c0 + j))
    prev = pl.BlockSpec((8, tc), lambda j, i: (jnp.maximum(i * (tb // 8) - 1, 0), c0 + j))
    nxt = pl.BlockSpec((8, tc), lambda j, i: (jnp.minimum((i + 1) * (tb // 8), t // 8 - 1), c0 + j))
    return cur, prev, nxt


def _conv_fwd(x, w, b, *, col0, width, act, name):
    t = x.shape[0]
    taps = w.shape[0]
    tb, tc = _rows(t, 512), _tile(width, 1024)
    assert col0 % tc == 0

    def body(x_ref, xp_ref, w_ref, b_ref, o_ref):
        i = pl.program_id(1)
        prev8 = jnp.where(i > 0, xp_ref[...], 0.0)
        u = _conv_apply(x_ref[...], prev8, w_ref[...], b_ref[...], taps)
        if act:
            u = u * _sigmoid(u)
        o_ref[...] = u

    cur, prev, _ = _conv_specs(tb, tc, col0, t)
    par = pl.BlockSpec((taps, tc), lambda j, i: (0, j))
    bias = pl.BlockSpec((1, tc), lambda j, i: (0, j))
    return pl.pallas_call(
        body, name=name, grid=(width // tc, t // tb), in_specs=[cur, prev, par, bias],
        out_specs=pl.BlockSpec((tb, tc), lambda j, i: (i, j)), out_shape=jax.ShapeDtypeStruct((t, width), F32),
        compiler_params=_cp(("parallel", "parallel")))(x, x, w, b)


def _conv_silu_dact(x, w, b, dout, *, col0, width, name):
    t = x.shape[0]
    taps = w.shape[0]
    tb, tc = _rows(t, 512), _tile(width, 1024)

    def body(x_ref, xp_ref, w_ref, b_ref, d_ref, o_ref):
        i = pl.program_id(1)
        prev8 = jnp.where(i > 0, xp_ref[...], 0.0)
        u = _conv_apply(x_ref[...], prev8, w_ref[...], b_ref[...], taps)
        sg = _sigmoid(u)
        o_ref[...] = d_ref[...] * (sg * (1.0 + u * (1.0 - sg)))

    cur, prev, _ = _conv_specs(tb, tc, col0, t)
    par = pl.BlockSpec((taps, tc), lambda j, i: (0, j))
    bias = pl.BlockSpec((1, tc), lambda j, i: (0, j))
    out = pl.BlockSpec((tb, tc), lambda j, i: (i, j))
    return pl.pallas_call(
        body, name=name, grid=(width // tc, t // tb), in_specs=[cur, prev, par, bias, out],
        out_specs=out, out_shape=jax.ShapeDtypeStruct((t, width), F32),
        compiler_params=_cp(("parallel", "parallel")))(x, x, w, b, dout)


def _conv_bwd(x, w, du, *, col0, width, name):
    t = x.shape[0]
    taps = w.shape[0]
    tb, tc = _rows(t, 512), _tile(width, 1024)
    nrow = t // tb

    def body(x_ref, w_ref, du_ref, dun_ref, dx_ref, dw_ref, db_ref):
        i = pl.program_id(1)
        next8 = jnp.where(i < nrow - 1, dun_ref[...], 0.0)
        dx, dwv, dbv = _conv_grads(du_ref[...], next8, x_ref[...], w_ref[...], taps)
        dx_ref[...] = dx.astype(BF16)

        @pl.when(i == 0)
        def _():
            dw_ref[...] = dwv
            db_ref[...] = dbv

        @pl.when(i > 0)
        def _():
            dw_ref[...] += dwv
            db_ref[...] += dbv

    cur, _, _ = _conv_specs(tb, tc, col0, t)
    dcur, _, dnxt = _conv_specs(tb, tc, 0, t)
    par = pl.BlockSpec((taps, tc), lambda j, i: (0, j))
    bias = pl.BlockSpec((1, tc), lambda j, i: (0, j))
    return pl.pallas_call(
        body, name=name, grid=(width // tc, nrow), in_specs=[cur, par, dcur, dnxt],
        out_specs=[dcur, par, bias],
        out_shape=[jax.ShapeDtypeStruct((t, width), BF16), jax.ShapeDtypeStruct((taps, width), F32),
                   jax.ShapeDtypeStruct((1, width), F32)],
        compiler_params=_cp(("parallel", "arbitrary")))(x, w, du, du)


def _ffn_specs(tb, tc, t):
    nc = D_FF // tc

    def cur(half):
        return pl.BlockSpec((tb, tc), lambda j, i: (i, half * nc + j))

    def prev(half):
        return pl.BlockSpec((8, tc), lambda j, i: (jnp.maximum(i * (tb // 8) - 1, 0), half * nc + j))

    def nxt(half):
        return pl.BlockSpec((8, tc), lambda j, i: (jnp.minimum((i + 1) * (tb // 8), t // 8 - 1), half * nc + j))

    def par(rows, half):
        return pl.BlockSpec((rows, tc), lambda j, i: (0, half * nc + j))

    return cur, prev, nxt, par


def _ffn_act_fwd(u0, w, b):
    t = u0.shape[0]
    tb, tc = _rows(t, 512), _tile(D_FF, 1408)
    cur, prev, _, par = _ffn_specs(tb, tc, t)

    def body(g_ref, gp_ref, v_ref, vp_ref, wg_ref, wv_ref, bg_ref, bv_ref, o_ref):
        i = pl.program_id(1)
        ug = _conv_apply(g_ref[...], jnp.where(i > 0, gp_ref[...], 0.0), wg_ref[...], bg_ref[...], FFN_CONV)
        uv = _conv_apply(v_ref[...], jnp.where(i > 0, vp_ref[...], 0.0), wv_ref[...], bv_ref[...], FFN_CONV)
        o_ref[...] = (ug * _sigmoid(ug) * uv).astype(BF16)

    return pl.pallas_call(
        body, name="ffn_act_fwd", grid=(D_FF // tc, t // tb),
        in_specs=[cur(0), prev(0), cur(1), prev(1), par(FFN_CONV, 0), par(FFN_CONV, 1), par(1, 0), par(1, 1)],
        out_specs=pl.BlockSpec((tb, tc), lambda j, i: (i, j)), out_shape=jax.ShapeDtypeStruct((t, D_FF), BF16),
        compiler_params=_cp(("parallel", "parallel")))(u0, u0, u0, u0, w, w, b, b)


def _ffn_act_bwd(u0, w, b, da):
    t = u0.shape[0]
    tb, tc = _rows(t, 256), _tile(D_FF, 1408)
    nrow = t // tb
    taps = FFN_CONV
    cur, prev, nxt, par = _ffn_specs(tb, tc, t)

    def dact(ug, uv, dav):
        sg = _sigmoid(ug)
        return dav * uv * (sg * (1.0 + ug * (1.0 - sg))), dav * ug * sg

    def body(g_ref, gp_ref, gn_ref, v_ref, vp_ref, vn_ref, wg_ref, wv_ref, bg_ref, bv_ref, da_ref, dan_ref,
             dx_ref, dw_ref, db_ref):
        i = pl.program_id(1)
        xg, xv = g_ref[...], v_ref[...]
        gp = jnp.where(i > 0, gp_ref[...], 0.0)
        vp = jnp.where(i > 0, vp_ref[...], 0.0)
        wg, wv, bg, bv = wg_ref[...], wv_ref[...], bg_ref[...], bv_ref[...]
        dug, duv = dact(_conv_apply(xg, gp, wg, bg, taps), _conv_apply(xv, vp, wv, bv, taps),
                        da_ref[...].astype(F32))
        dan = jnp.where(i < nrow - 1, dan_ref[...].astype(F32)[:8], 0.0)
        dugn, duvn = dact(_conv_apply(gn_ref[...], xg[tb - 8:], wg, bg, taps),
                          _conv_apply(vn_ref[...], xv[tb - 8:], wv, bv, taps), dan)
        dxg, dwg, dbg = _conv_grads(dug, dugn, xg, wg, taps)
        dxv, dwv, dbv = _conv_grads(duv, duvn, xv, wv, taps)
        dx_ref[0] = dxg.astype(BF16)
        dx_ref[1] = dxv.astype(BF16)

        @pl.when(i == 0)
        def _():
            dw_ref[0] = dwg
            dw_ref[1] = dwv
            db_ref[0] = dbg
            db_ref[1] = dbv

        @pl.when(i > 0)
        def _():
            dw_ref[0] += dwg
            dw_ref[1] += dwv
            db_ref[0] += dbg
            db_ref[1] += dbv

    da_cur = pl.BlockSpec((tb, tc), lambda j, i: (i, j))
    da_nxt = pl.BlockSpec((16, tc), lambda j, i: (jnp.minimum((i + 1) * (tb // 16), t // 16 - 1), j))
    return pl.pallas_call(
        body, name="ffn_act_bwd", grid=(D_FF // tc, nrow),
        in_specs=[cur(0), prev(0), nxt(0), cur(1), prev(1), nxt(1), par(taps, 0), par(taps, 1), par(1, 0),
                  par(1, 1), da_cur, da_nxt],
        out_specs=[pl.BlockSpec((2, tb, tc), lambda j, i: (0, i, j)),
                   pl.BlockSpec((2, taps, tc), lambda j, i: (0, 0, j)),
                   pl.BlockSpec((2, 1, tc), lambda j, i: (0, 0, j))],
        out_shape=[jax.ShapeDtypeStruct((2, t, D_FF), BF16), jax.ShapeDtypeStruct((2, taps, D_FF), F32),
                   jax.ShapeDtypeStruct((2, 1, D_FF), F32)],
        compiler_params=_cp(("parallel", "arbitrary")))(u0, u0, u0, u0, u0, u0, w, w, b, b, da, da)


def _head_masks():
    lane = _iota((1, 4 * SSD_HEAD_DIM), 1)
    return [((lane >= r * SSD_HEAD_DIM) & (lane < (r + 1) * SSD_HEAD_DIM)).astype(F32) for r in range(4)]


def _segsum(v):
    first = _iota((1, LANES), 1) < SSD_HEAD_DIM
    halves = []
    for k in range(2):
        vh = v[:, k * LANES:(k + 1) * LANES]
        both = jnp.sum(vh, axis=1, keepdims=True)
        one = jnp.sum(jnp.where(first, vh, 0.0), axis=1, keepdims=True)
        halves.append(jnp.where(first, one, both - one))
    return jnp.concatenate(halves, axis=1)


def _ssd_common(raw_e, prow, rawr4, bcol, acol):
    n = SSD_CHUNK
    dt_e = _softplus(raw_e + prow[0:1, :])
    a_e = -jnp.exp(prow[1:2, :])
    d_e = prow[2:3, :]
    tril = (_iota((n, n), 0) >= _iota((n, n), 1)).astype(F32)
    acs_e = _dot_exact(tril, dt_e * a_e)
    last_e = acs_e[n - 1:n, :]
    dtr4 = _softplus(rawr4 + bcol)
    triu = (_iota((n, n), 0) <= _iota((n, n), 1)).astype(F32)
    acs_r4 = _dot_exact(dtr4 * (-jnp.exp(acol)), triu)
    return dt_e, a_e, d_e, acs_e, last_e, acs_r4


def _decay_matrix(acs_e, acs_r4, r):
    n = SSD_CHUNK
    col = acs_e[:, r * SSD_HEAD_DIM:r * SSD_HEAD_DIM + 1]
    seg = col - acs_r4[r:r + 1, :]
    causal = _iota((n, n), 0) >= _iota((n, n), 1)
    return jnp.exp(jnp.where(causal, seg, NEG))


SSD_STEP_CHUNKS = 4
SSD_ROWS = SSD_STEP_CHUNKS * SSD_CHUNK


def _ssd_specs(t, rev):
    nb = t // SSD_ROWS
    xb, bb, cb = 0, SSD_INNER // SSD_STATE, (SSD_INNER + BC_WIDTH) // SSD_STATE

    def ch(c):
        return (nb - 1 - c) if rev else c

    x = pl.BlockSpec((SSD_ROWS, 256), lambda g, c: (ch(c), xb + g))
    bm = pl.BlockSpec((SSD_ROWS, SSD_STATE), lambda g, c: (ch(c), bb + g))
    cm = pl.BlockSpec((SSD_ROWS, SSD_STATE), lambda g, c: (ch(c), cb + g))
    dtc = pl.BlockSpec((1, SSD_ROWS, 256), lambda g, c: (g, ch(c), 0))
    dtr = pl.BlockSpec((1, 4, SSD_ROWS), lambda g, c: (g, 0, ch(c)))
    prow = pl.BlockSpec((1, 3, 256), lambda g, c: (g, 0, 0))
    pcol = pl.BlockSpec((1, 4, 1), lambda g, c: (g, 0, 0))
    st = pl.BlockSpec((1, SSD_STEP_CHUNKS, SSD_STATE, 256), lambda g, c: (g, ch(c), 0, 0))
    return x, bm, cm, dtc, dtr, prow, pcol, st, ch


def _ssd_params(dt_raw, dt_bias, a_log, ssd_d):
    t = dt_raw.shape[0]
    by_group = dt_raw.reshape(t, SSD_GROUPS, 4)
    dtc = jnp.repeat(by_group, SSD_HEAD_DIM, axis=2).transpose(1, 0, 2)
    dtr = by_group.transpose(1, 2, 0)
    prow = jnp.repeat(jnp.stack([dt_bias.reshape(SSD_GROUPS, 4), a_log.reshape(SSD_GROUPS, 4),
                                 ssd_d.reshape(SSD_GROUPS, 4)], axis=1), SSD_HEAD_DIM, axis=2)
    bcol = dt_bias.reshape(SSD_GROUPS, 4, 1)
    acol = a_log.reshape(SSD_GROUPS, 4, 1)
    return dtc, dtr, prow, bcol, acol


def _ssd_fwd(xbc, params):
    t = xbc.shape[0]
    nc = t // SSD_CHUNK
    dtc, dtr, prow, bcol, acol = params

    def body(x_ref, b_ref, c_ref, dtc_ref, dtr_ref, prow_ref, bcol_ref, acol_ref, y_ref, st_ref, s_scr):
        c = pl.program_id(1)

        @pl.when(c == 0)
        def _():
            s_scr[...] = jnp.zeros_like(s_scr)

        masks = _head_masks()
        s = s_scr[...]
        for k in range(SSD_STEP_CHUNKS):
            rows = slice(k * SSD_CHUNK, (k + 1) * SSD_CHUNK)
            dt_e, a_e, d_e, acs_e, last_e, acs_r4 = _ssd_common(
                dtc_ref[0, rows], prow_ref[0], dtr_ref[0][:, rows], bcol_ref[0], acol_ref[0])
            xv = x_ref[rows]
            bm, cm = b_ref[rows], c_ref[rows]
            st_ref[0, k] = s
            xdt = xv * dt_e
            cb = _dot(cm, bm, 'nt')
            y = _dot(cm, s) * jnp.exp(acs_e) + xv * d_e
            for r in range(4):
                mr = cb * _decay_matrix(acs_e, acs_r4, r)
                y = y + _dot(mr, xdt * masks[r])
            y_ref[rows] = y
            w = xdt * jnp.exp(last_e - acs_e)
            s = s * jnp.exp(last_e) + _dot(bm.T, w)
        s_scr[...] = s

    x, bm, cm, dtcs, dtrs, prs, pcs, st, _ = _ssd_specs(t, False)
    return pl.pallas_call(
        body, name="ssd_fwd", grid=(SSD_GROUPS, t // SSD_ROWS), in_specs=[x, bm, cm, dtcs, dtrs, prs, pcs, pcs],
        out_specs=[pl.BlockSpec((SSD_ROWS, 256), lambda g, c: (c, g)), st],
        out_shape=[jax.ShapeDtypeStruct((t, SSD_INNER), F32),
                   jax.ShapeDtypeStruct((SSD_GROUPS, nc, SSD_STATE, 256), F32)],
        scratch_shapes=[pltpu.VMEM((SSD_STATE, 256), F32)],
        compiler_params=_cp(("parallel", "arbitrary")))(xbc, xbc, xbc, dtc, dtr, prow, bcol, acol)


def _ssd_bwd(xbc, params, states, dy):
    t = xbc.shape[0]
    nc = t // SSD_CHUNK
    n = SSD_CHUNK
    dtc, dtr, prow, bcol, acol = params

    def body(x_ref, b_ref, c_ref, dtc_ref, dtr_ref, prow_ref, bcol_ref, acol_ref, st_ref, dy_ref,
             dx_ref, db_ref, dc_ref, ddt_ref, dp_ref, ds_scr):
        c = pl.program_id(1)

        @pl.when(c == 0)
        def _():
            ds_scr[...] = jnp.zeros_like(ds_scr)
            dp_ref[...] = jnp.zeros_like(dp_ref)

        masks = _head_masks()
        ds = ds_scr[...]
        for k in reversed(range(SSD_STEP_CHUNKS)):
            rows = slice(k * SSD_CHUNK, (k + 1) * SSD_CHUNK)
            raw_e = dtc_ref[0, rows]
            prw = prow_ref[0]
            dt_e, a_e, d_e, acs_e, last_e, acs_r4 = _ssd_common(raw_e, prw, dtr_ref[0][:, rows], bcol_ref[0], acol_ref[0])
            xv = x_ref[rows]
            bm, cm = b_ref[rows], c_ref[rows]
            s = st_ref[0, k]
            dyv = dy_ref[rows]
            e_e = jnp.exp(acs_e)
            dec_e = jnp.exp(last_e - acs_e)
            cd_e = jnp.exp(last_e)
            xdt = xv * dt_e
            w = xdt * dec_e
            b16, c16, s16, ds16 = bm.astype(BF16), cm.astype(BF16), s.astype(BF16), ds.astype(BF16)
            cb = _dot(c16, b16, 'nt')
            yoff_raw = _dot(c16, s16)
            dye = dyv * e_e
            dye16 = dye.astype(BF16)
            dcm = _dot(dye16, s16, 'nt')
            ds_prev = ds * cd_e + _dot(cm.T, dye16)
            dacs_e = _segsum(dyv * yoff_raw) * e_e
            dw = _dot(b16, ds16)
            dbm = _dot(w, ds16, 'nt')
            tdec = _segsum(dw * xdt) * dec_e
            dacs_e = dacs_e - tdec
            dlast_e = jnp.sum(tdec, axis=0, keepdims=True)
            dxdt = dw * dec_e
            dlast_e = dlast_e + _segsum(jnp.sum(ds * s, axis=0, keepdims=True)) * cd_e
            dcb = jnp.zeros((n, n), F32)
            for r in range(4):
                lm = _decay_matrix(acs_e, acs_r4, r)
                mr = cb * lm
                dyr16 = (dyv * masks[r]).astype(BF16)
                dm = _dot(dyr16, xdt * masks[r], 'nt')
                dcb = dcb + dm * lm
                dseg = dm * mr
                dcol = jnp.sum(dseg, axis=1, keepdims=True) - jnp.sum(dseg.T, axis=1, keepdims=True)
                dacs_e = dacs_e + dcol * masks[r]
                dxdt = dxdt + _dot(mr.T, dyr16)
            dcm = dcm + _dot(dcb, b16)
            dbm = dbm + _dot(dcb.T, c16)
            dacs_e = dacs_e + jnp.where(_iota((n, 1), 0) == n - 1, dlast_e, 0.0)
            triu = (_iota((n, n), 0) <= _iota((n, n), 1)).astype(F32)
            ddta_e = _dot_exact(triu, dacs_e)
            ddt_e = ddta_e * a_e + _segsum(dxdt * xv)
            dx_ref[rows] = dxdt * dt_e + dyv * d_e
            db_ref[rows] = dbm
            dc_ref[rows] = dcm
            draw_e = ddt_e * _sigmoid(raw_e + prw[0:1, :])
            draw_t = draw_e.T
            ddt_ref[0, :, rows] = jnp.concatenate([draw_t[r * SSD_HEAD_DIM:r * SSD_HEAD_DIM + 1] for r in range(4)], axis=0)
            dbias = jnp.sum(draw_e, axis=0, keepdims=True)
            dalog = jnp.sum(ddta_e * dt_e, axis=0, keepdims=True) * a_e
            dd = _segsum(jnp.sum(dyv * xv, axis=0, keepdims=True))
            row3 = _iota((3, 1), 0)
            dp_ref[0] += (jnp.where(row3 == 0, dbias, 0.0) + jnp.where(row3 == 1, dalog, 0.0)
                          + jnp.where(row3 == 2, dd, 0.0))
            ds = ds_prev
        ds_scr[...] = ds


    x, bm, cm, dtcs, dtrs, prs, pcs, st, ch = _ssd_specs(t, True)
    yblk = pl.BlockSpec((SSD_ROWS, 256), lambda g, c: (ch(c), g))
    nblk = pl.BlockSpec((SSD_ROWS, SSD_STATE), lambda g, c: (ch(c), g))
    return pl.pallas_call(
        body, name="ssd_bwd", grid=(SSD_GROUPS, t // SSD_ROWS),
        in_specs=[x, bm, cm, dtcs, dtrs, prs, pcs, pcs, st, yblk],
        out_specs=[yblk, nblk, nblk, dtrs, prs],
        out_shape=[jax.ShapeDtypeStruct((t, SSD_INNER), F32), jax.ShapeDtypeStruct((t, BC_WIDTH), F32),
                   jax.ShapeDtypeStruct((t, BC_WIDTH), F32), jax.ShapeDtypeStruct((SSD_GROUPS, 4, t), F32),
                   jax.ShapeDtypeStruct((SSD_GROUPS, 3, 256), F32)],
        scratch_shapes=[pltpu.VMEM((SSD_STATE, 256), F32)],
        compiler_params=_cp(("parallel", "arbitrary")))(xbc, xbc, xbc, dtc, dtr, prow, bcol, acol, states, dy)


GROUP_W = SSD_INNER // SSD_GROUPS


def _mix_specs(tb):
    row = pl.BlockSpec((tb, 2048), lambda i: (i, 0))
    zlo = pl.BlockSpec((tb, 1024), lambda i: (i, O_Z // 1024))
    zhi = pl.BlockSpec((tb, 1024), lambda i: (i, O_Z // 1024 + 1))
    vec = pl.BlockSpec((1, 2048), lambda i: (0, 0))
    return row, zlo, zhi, vec


def _mix_fwd(attn, y, proj, g_attn, g_ssd):
    t = attn.shape[0]
    tb = _rows(t, 256)

    def body(a_ref, y_ref, zlo_ref, zhi_ref, ga_ref, gs_ref, o_ref):
        av = a_ref[...]
        r = lax.rsqrt(jnp.mean(av * av, axis=-1, keepdims=True) + EPS)
        o_ref[:, :ATTN_WIDTH] = (av * r * ga_ref[...]).astype(BF16)
        for g in range(SSD_GROUPS):
            lo, hi = g * GROUP_W, (g + 1) * GROUP_W
            zref = zlo_ref if g < 4 else zhi_ref
            z = zref[:, lo % 1024:lo % 1024 + GROUP_W]
            yg = y_ref[:, lo:hi] * (z * _sigmoid(z))
            rg = lax.rsqrt(jnp.mean(yg * yg, axis=-1, keepdims=True) + EPS)
            o_ref[:, ATTN_WIDTH + lo:ATTN_WIDTH + hi] = (yg * rg * gs_ref[:, lo:hi]).astype(BF16)

    row, zlo, zhi, vec = _mix_specs(tb)
    return pl.pallas_call(
        body, name="mix_fwd", grid=(t // tb,), in_specs=[row, row, zlo, zhi, vec, vec],
        out_specs=pl.BlockSpec((tb, 4096), lambda i: (i, 0)), out_shape=jax.ShapeDtypeStruct((t, 4096), BF16),
        compiler_params=_cp(("parallel",)))(attn, y, proj, proj, g_attn, g_ssd)


def _mix_bwd(dmix, attn, y, proj, g_attn, g_ssd):
    t = attn.shape[0]
    tb = _rows(t, 256)

    def body(dm_ref, a_ref, y_ref, zlo_ref, zhi_ref, ga_ref, gs_ref, da_ref, dy_ref, dz_ref, dga_ref, dgs_ref):
        i = pl.program_id(0)
        av = a_ref[...]
        dn = dm_ref[:, :ATTN_WIDTH].astype(F32)
        r = lax.rsqrt(jnp.mean(av * av, axis=-1, keepdims=True) + EPS)
        u = dn * ga_ref[...]
        da_ref[...] = r * u - av * (r * r * r * jnp.mean(u * av, axis=-1, keepdims=True))
        dga = jnp.sum(dn * av * r, axis=0, keepdims=True)

        @pl.when(i == 0)
        def _():
            dga_ref[...] = dga

        @pl.when(i > 0)
        def _():
            dga_ref[...] += dga

        for g in range(SSD_GROUPS):
            lo, hi = g * GROUP_W, (g + 1) * GROUP_W
            zref = zlo_ref if g < 4 else zhi_ref
            z = zref[:, lo % 1024:lo % 1024 + GROUP_W]
            yv = y_ref[:, lo:hi]
            sg = _sigmoid(z)
            sz = z * sg
            yg = yv * sz
            rg = lax.rsqrt(jnp.mean(yg * yg, axis=-1, keepdims=True) + EPS)
            do = dm_ref[:, ATTN_WIDTH + lo:ATTN_WIDTH + hi].astype(F32)
            ug = do * gs_ref[:, lo:hi]
            dyg = rg * ug - yg * (rg * rg * rg * jnp.mean(ug * yg, axis=-1, keepdims=True))
            dy_ref[:, lo:hi] = dyg * sz
            dz_ref[:, lo:hi] = (dyg * yv * (sg * (1.0 + z * (1.0 - sg)))).astype(BF16)
            dgs = jnp.sum(do * yg * rg, axis=0, keepdims=True)

            @pl.when(i == 0)
            def _():
                dgs_ref[:, lo:hi] = dgs

            @pl.when(i > 0)
            def _():
                dgs_ref[:, lo:hi] += dgs

    row, zlo, zhi, vec = _mix_specs(tb)
    return pl.pallas_call(
        body, name="mix_bwd", grid=(t // tb,),
        in_specs=[pl.BlockSpec((tb, 4096), lambda i: (i, 0)), row, row, zlo, zhi, vec, vec],
        out_specs=[row, row, row, vec, vec],
        out_shape=[jax.ShapeDtypeStruct((t, 2048), F32), jax.ShapeDtypeStruct((t, 2048), F32),
                   jax.ShapeDtypeStruct((t, 2048), BF16), jax.ShapeDtypeStruct((1, 2048), F32),
                   jax.ShapeDtypeStruct((1, 2048), F32)],
        compiler_params=_cp(("arbitrary",)))(dmix, attn, y, proj, proj, g_attn, g_ssd)


def _adamw(w, g, m, v, name):
    r, c = w.shape
    tb = _rows(r, 256)
    c1 = 1.0 - ADAM_B1 ** ADAM_STEP
    c2 = 1.0 - ADAM_B2 ** ADAM_STEP

    def body(w_ref, g_ref, m_ref, v_ref, d_ref, m2_ref, v2_ref):
        gv = g_ref[...]
        m2 = ADAM_B1 * m_ref[...] + (1.0 - ADAM_B1) * gv
        v2 = ADAM_B2 * v_ref[...] + (1.0 - ADAM_B2) * (gv * gv)
        d_ref[...] = -ADAM_LR * ((m2 / c1) / (jnp.sqrt(v2 / c2) + ADAM_EPS) + ADAM_WD * w_ref[...])
        m2_ref[...] = m2
        v2_ref[...] = v2

    blk = pl.BlockSpec((tb, c), lambda i: (i, 0))
    shp = jax.ShapeDtypeStruct((r, c), F32)
    return pl.pallas_call(body, name=name, grid=(r // tb,), in_specs=[blk] * 4, out_specs=[blk] * 3,
                          out_shape=[shp] * 3, compiler_params=_cp(("parallel",)))(w, g, m, v)


def _adamw_halves(w, mine, theirs, m, v, pos, name, cols=False):
    r, c = w.shape
    h = r if cols else r // 2
    tb = _rows(h, 128)
    nh = h // tb
    c1 = 1.0 - ADAM_B1 ** ADAM_STEP
    c2 = 1.0 - ADAM_B2 ** ADAM_STEP

    def body(pos_ref, w_ref, a_ref, b_ref, m_ref, v_ref, g_ref, d_ref, m2_ref, v2_ref):
        which = pl.program_id(1) if cols else pl.program_id(0) // nh
        gv = jnp.where(which == pos_ref[0], a_ref[...], b_ref[...])
        m2 = ADAM_B1 * m_ref[...] + (1.0 - ADAM_B1) * gv
        v2 = ADAM_B2 * v_ref[...] + (1.0 - ADAM_B2) * (gv * gv)
        g_ref[...] = gv
        d_ref[...] = -ADAM_LR * ((m2 / c1) / (jnp.sqrt(v2 / c2) + ADAM_EPS) + ADAM_WD * w_ref[...])
        m2_ref[...] = m2
        v2_ref[...] = v2

    if cols:
        full = pl.BlockSpec((tb, c // 2), lambda i, j, pref: (i, j))
        mine_spec = theirs_spec = pl.BlockSpec((tb, c // 2), lambda i, j, pref: (i, 0))
        grid = (nh, 2)
    else:
        full = pl.BlockSpec((tb, c), lambda i, pref: (i, 0))
        mine_spec = pl.BlockSpec((tb, c), lambda i, pref: (jnp.where(i // nh == pref[0], i % nh,
                                                                     jnp.where(pref[0] == 0, nh - 1, 0)), 0))
        theirs_spec = pl.BlockSpec((tb, c), lambda i, pref: (jnp.where(i // nh != pref[0], i % nh,
                                                                       jnp.where(pref[0] == 0, 0, nh - 1)), 0))
        grid = (r // tb,)
    shp = jax.ShapeDtypeStruct((r, c), F32)
    grid_spec = pltpu.PrefetchScalarGridSpec(num_scalar_prefetch=1, grid=grid,
                                             in_specs=[full, mine_spec, theirs_spec, full, full],
                                             out_specs=[full] * 4)
    return pl.pallas_call(body, name=name, grid_spec=grid_spec, out_shape=[shp] * 4,
                          compiler_params=_cp(("parallel",) * len(grid)))(pos, w, mine, theirs, m, v)


def _sum_own_half(g4, recv, pos, name, cols=False):
    _, r, c = g4.shape
    h, c = (r, c // 2) if cols else (r // 2, c)
    tb = _rows(h, 128)
    nh = h // tb

    def slot(j, pref):
        return (pref[1] + 1 + j) % N_CHIPS

    if cols:
        own = lambda j, i, pref: (slot(j, pref), i, pref[0])
    else:
        own = lambda j, i, pref: (slot(j, pref), pref[0] * nh + i, 0)
    same = lambda j, i, pref: (slot(j, pref), i, 0)

    def body(pos_ref, a_ref, b_ref, o_ref):
        o_ref[...] = (a_ref[...] + b_ref[...]).astype(BF16)

    grid_spec = pltpu.PrefetchScalarGridSpec(
        num_scalar_prefetch=1, grid=(N_CHIPS - 1, nh),
        in_specs=[pl.BlockSpec((1, tb, c), own), pl.BlockSpec((1, tb, c), same)],
        out_specs=pl.BlockSpec((1, tb, c), same))
    return pl.pallas_call(body, name=name, grid_spec=grid_spec,
                          out_shape=jax.ShapeDtypeStruct((N_CHIPS, h, c), BF16),
                          compiler_params=_cp(("parallel", "parallel")))(pos, g4, recv)


def _sum_chips(g4, recv, parts, pos, name, cols=False):
    _, r, c = g4.shape
    h, c = (r, c // 2) if cols else (r // 2, c)
    tb = _rows(h, 128)
    nh = h // tb
    own = (lambda i, pref: (pref[1], i, pref[0])) if cols else (lambda i, pref: (pref[1], pref[0] * nh + i, 0))

    def body(pos_ref, a_ref, b_ref, p_ref, o_ref):
        own = a_ref[0] + b_ref[0]
        o_ref[...] = ((own + p_ref[0].astype(F32)) + p_ref[1].astype(F32)) + p_ref[2].astype(F32)

    grid_spec = pltpu.PrefetchScalarGridSpec(
        num_scalar_prefetch=1, grid=(nh,),
        in_specs=[pl.BlockSpec((1, tb, c), own),
                  pl.BlockSpec((1, tb, c), lambda i, pref: (pref[1], i, 0)),
                  pl.BlockSpec((3, tb, c), lambda i, pref: (0, i, 0))],
        out_specs=pl.BlockSpec((tb, c), lambda i, pref: (i, 0)))
    return pl.pallas_call(body, name=name, grid_spec=grid_spec, out_shape=jax.ShapeDtypeStruct((h, c), F32),
                          compiler_params=_cp(("parallel",)))(pos, g4, recv, parts)


def _me():
    return lax.axis_index("x"), lax.axis_index("y"), lax.axis_index("c")


def _flip(v, bit):
    return (1 - v) if bit else v


CHIP_FLIPS = [(1, 0), (0, 1), (1, 1)]


def _allgather_weights(shards, after, cols=False):
    n = len(shards)

    def body(*refs):
        ins, outs, token = refs[:n], refs[n + 1:2 * n + 1], refs[2 * n + 1]
        send_sems, recv_sems = refs[2 * n + 2:]
        x, y, c = _me()
        chip = 2 * x + y
        sib = (x, y, 1 - c)

        def remote(src, dst, k, to):
            return pltpu.make_async_remote_copy(src_ref=src, dst_ref=dst, send_sem=send_sems.at[k],
                                                recv_sem=recv_sems.at[k], device_id=to, device_id_type=MESH)

        def half(ref, which):
            if cols:
                h = ref.shape[1] // 2
                return ref.at[:, pl.ds(which * h, h)]
            h = ref.shape[0] // 2
            return ref.at[pl.ds(which * h, h)]

        sends = []
        for t in range(n):
            for k, (fx, fy) in enumerate(CHIP_FLIPS):
                cp = remote(half(ins[t], c), half(outs[t].at[chip], c), 6 * t + k, (_flip(x, fx), _flip(y, fy), c))
                cp.start()
                sends.append(cp)
        for t in range(n):
            for k, (fx, fy) in enumerate(CHIP_FLIPS):
                landed = half(outs[t].at[2 * _flip(x, fx) + _flip(y, fy)], c)
                remote(landed, landed, 6 * t + k, (x, y, c)).wait_recv()
                fw = remote(landed, landed, 6 * t + 3 + k, sib)
                fw.start()
                sends.append(fw)
        for t in range(n):
            for k, (fx, fy) in enumerate(CHIP_FLIPS):
                got = half(outs[t].at[2 * _flip(x, fx) + _flip(y, fy)], 1 - c)
                remote(got, got, 6 * t + 3 + k, (x, y, c)).wait_recv()
        for cp in sends:
            cp.wait_send()
        token[...] = jnp.zeros_like(token)

    outs = pl.pallas_call(
        body, name="allgather_weights", in_specs=[HBM_SPEC] * n + [pl.BlockSpec(memory_space=pl.ANY)],
        out_specs=[HBM_SPEC] * n + [pl.BlockSpec(memory_space=pltpu.VMEM)],
        out_shape=[jax.ShapeDtypeStruct((N_CHIPS,) + s.shape, s.dtype) for s in shards] + [TOKEN],
        scratch_shapes=[pltpu.SemaphoreType.DMA((6 * n,)), pltpu.SemaphoreType.DMA((6 * n,))],
        compiler_params=pltpu.CompilerParams(has_side_effects=True))(*shards, after)
    return list(outs[:n]), outs[n]


def _share_halves(ghs, name):
    n = len(ghs)

    def body(*refs):
        ins, outs = refs[:n], refs[n:2 * n]
        send_sems, recv_sems = refs[2 * n:]
        x, y, c = _me()
        cps = []
        for t in range(n):
            cp = pltpu.make_async_remote_copy(
                src_ref=ins[t], dst_ref=outs[t], send_sem=send_sems.at[t], recv_sem=recv_sems.at[t],
                device_id=(x, y, 1 - c), device_id_type=MESH)
            cp.start()
            cps.append(cp)
        for cp in cps:
            cp.wait()

    return pl.pallas_call(
        body, name=name, in_specs=[HBM_SPEC] * n, out_specs=[HBM_SPEC] * n,
        out_shape=[jax.ShapeDtypeStruct(g.shape, g.dtype) for g in ghs],
        scratch_shapes=[pltpu.SemaphoreType.DMA((n,)), pltpu.SemaphoreType.DMA((n,))],
        compiler_params=pltpu.CompilerParams(has_side_effects=True))(*ghs)


SEM_SPEC = pl.BlockSpec(memory_space=pltpu.SEMAPHORE)
ANY_SPEC = pl.BlockSpec(memory_space=pl.ANY)
DATAFLOW = pltpu.SideEffectType.DATAFLOW_SIDE_EFFECTING


def _in_hbm(a):
    return pltpu.with_memory_space_constraint(a, pltpu.HBM)


def _push_start(srcs, land_shapes, route, peers, name):
    n, npeer = len(srcs), len(peers)
    lands = [lax.empty(shp, s.dtype) for shp, s in zip(land_shapes, srcs)]

    def body(*refs):
        ins, lnd = refs[:n], refs[n:2 * n]
        send_sems, recv_sems = refs[2 * n], refs[2 * n + 1]
        token = refs[-1]
        x, y, c = _me()
        for t in range(n):
            for k, (fx, fy, fc) in enumerate(peers):
                src, dst = route(ins[t], lnd[t], k, x, y, c)
                pltpu.make_async_remote_copy(
                    src_ref=src, dst_ref=dst, send_sem=send_sems.at[npeer * t + k],
                    recv_sem=recv_sems.at[npeer * t + k],
                    device_id=(_flip(x, fx), _flip(y, fy), _flip(c, fc)), device_id_type=MESH).start()
        token[...] = jnp.zeros_like(token)

    bufs = [_in_hbm(a) for a in list(srcs) + lands]
    outs = pl.pallas_call(
        body, name=name,
        out_shape=(pltpu.SemaphoreType.DMA((npeer * n,)), pltpu.SemaphoreType.DMA((npeer * n,)),
                   *[pltpu.HBM(b.shape, b.dtype) for b in bufs], TOKEN),
        in_specs=[HBM_SPEC] * (2 * n),
        out_specs=(SEM_SPEC, SEM_SPEC, *[HBM_SPEC] * (2 * n), pl.BlockSpec(memory_space=pltpu.VMEM)),
        input_output_aliases={i: 2 + i for i in range(2 * n)},
        compiler_params=pltpu.CompilerParams(has_side_effects=DATAFLOW))(*bufs)
    return outs[0], outs[1], list(outs[2:2 + n]), list(outs[2 + n:2 + 2 * n]), outs[-1]


def _push_wait(send_sems, recv_sems, srcs, lands, after, route, peers, name):
    n, npeer = len(srcs), len(peers)

    def body(*refs):
        ins, lnd = refs[:n], refs[n:2 * n]
        ssem, rsem = refs[2 * n], refs[2 * n + 1]
        x, y, c = _me()
        for t in range(n):
            for k, (fx, fy, fc) in enumerate(peers):
                src, dst = route(ins[t], lnd[t], k, x, y, c)
                cp = pltpu.make_async_remote_copy(
                    src_ref=src, dst_ref=dst, send_sem=ssem.at[npeer * t + k], recv_sem=rsem.at[npeer * t + k],
                    device_id=(_flip(x, fx), _flip(y, fy), _flip(c, fc)), device_id_type=MESH)
                cp.wait_send()
                cp.wait_recv()

    bufs = list(srcs) + list(lands)
    outs = pl.pallas_call(
        body, name=name, out_shape=tuple(pltpu.HBM(b.shape, b.dtype) for b in bufs),
        in_specs=[HBM_SPEC] * (2 * n) + [SEM_SPEC, SEM_SPEC, ANY_SPEC], out_specs=tuple([HBM_SPEC] * (2 * n)),
        input_output_aliases={i: i for i in range(2 * n)},
        compiler_params=pltpu.CompilerParams(has_side_effects=DATAFLOW))(*bufs, send_sems, recv_sems, after)
    return list(outs[:n]), list(outs[n:])


OTHER_CHIPS = [(fx, fy, 0) for fx, fy in CHIP_FLIPS]
SIBLING = [(0, 0, 1)]


def _route_gather(src, land, k, x, y, c):
    return src, land.at[2 * x + y]


def _route_gather_wait(src, land, k, x, y, c):
    fx, fy = CHIP_FLIPS[k]
    return src, land.at[2 * _flip(x, fx) + _flip(y, fy)]


def _route_scatter(src, land, k, x, y, c):
    fx, fy = CHIP_FLIPS[k]
    return src.at[2 * _flip(x, fx) + _flip(y, fy)], land.at[k]


def _route_exchange(src, land, k, x, y, c):
    h = land.shape[1]
    return src.at[:, pl.ds((1 - c) * h, h)], land


def _route_exchange_cols(src, land, k, x, y, c):
    h = land.shape[2]
    return src.at[:, :, pl.ds((1 - c) * h, h)], land


def _allreduce_small(v):
    r = v.shape[0]

    def body(v_ref, o_ref, buf, send_sems, recv_sems):
        x, y, c = _me()
        me = 4 * x + 2 * y + c
        buf[0] = v_ref[...]
        cps = []
        for k in range(1, 8):
            kx, ky, kc = (k >> 2) & 1, (k >> 1) & 1, k & 1
            cp = pltpu.make_async_remote_copy(
                src_ref=v_ref, dst_ref=buf.at[k], send_sem=send_sems.at[k - 1], recv_sem=recv_sems.at[k - 1],
                device_id=(_flip(x, kx), _flip(y, ky), _flip(c, kc)), device_id_type=MESH)
            cp.start()
            cps.append(cp)
        for cp in cps:
            cp.wait()
        acc = buf[me]
        for d in range(1, 8):
            acc = acc + buf[jnp.bitwise_xor(me, d)]
        o_ref[...] = acc

    vm = pl.BlockSpec(memory_space=pltpu.VMEM)
    return pl.pallas_call(
        body, name="allreduce_small", in_specs=[vm], out_specs=vm, out_shape=jax.ShapeDtypeStruct(v.shape, F32),
        scratch_shapes=[pltpu.VMEM((8, r, LANES), F32), pltpu.SemaphoreType.DMA((7,)),
                        pltpu.SemaphoreType.DMA((7,))],
        compiler_params=pltpu.CompilerParams(has_side_effects=True, vmem_limit_bytes=VMEM_LIMIT))(v)


def _grad_exchange_start(g4, tag, cols=False):
    land = (N_CHIPS, g4.shape[1], g4.shape[2] // 2) if cols else (N_CHIPS, g4.shape[1] // 2, g4.shape[2])
    route = _route_exchange_cols if cols else _route_exchange
    send_sems, recv_sems, srcs, lands, token = _push_start(
        [g4], [land], route, SIBLING, name="grad_exchange_start_" + tag)
    return (send_sems, recv_sems, srcs, lands, tag, cols), token


def _grad_scatter_start(state, pos, after):
    send_sems, recv_sems, srcs, lands, tag, cols = state
    route = _route_exchange_cols if cols else _route_exchange
    (g4,), (recv,) = _push_wait(send_sems, recv_sems, srcs, lands, after, route, SIBLING,
                                name="grad_exchange_wait_" + tag)
    return _grad_pair_scatter(g4, recv, pos, tag, cols)


def _grad_pair_scatter(g4, recv, pos, tag, cols=False):
    p16 = _sum_own_half(g4, recv, pos, name="grad_sum_pair_" + tag, cols=cols)
    send_sems, recv_sems, srcs, lands, token = _push_start(
        [p16], [(3,) + p16.shape[1:]], _route_scatter, OTHER_CHIPS, name="grad_scatter_start_" + tag)
    return (g4, recv, send_sems, recv_sems, srcs, lands, tag, cols), token


def _grad_reduce_finish(state, pos, after):
    g4, recv, send_sems, recv_sems, srcs, lands, tag, cols = state
    parts = _push_wait(send_sems, recv_sems, srcs, lands, after, _route_scatter, OTHER_CHIPS,
                       name="grad_scatter_wait_" + tag)[1][0]
    mine = _sum_chips(g4, recv, parts, pos, name="grad_sum_chips_" + tag, cols=cols)
    return mine, _share_halves([mine], name="grad_share_halves_" + tag)[0]


def _local_step(x, tgt, p, w_in_t, w_in_dt, hooks):
    t = x.shape[0]
    tables = _rope_tables(t)
    sinks = p['sinks'].reshape(N_Q_HEADS)

    def told(name, value):
        return tuple(hooks.grad_ready(name, value))

    xn = _rmsnorm_fwd(x, p['norm_mix'], "norm_mix_fwd", deps=hooks.first_deps)
    proj = _matmul(xn, w_in_t, mode='nt', name="in_proj", n_limit=MAIN_WIDTH)
    dt_raw = _matmul(xn, w_in_dt, mode='nt', name="in_proj_dt")[:, :SSD_HEADS]
    attn = _attn_fwd(proj, sinks, tables)
    conv_b = p['ssd_conv_b']
    xbc = _conv_fwd(proj, p['ssd_conv_w'], conv_b, col0=O_XBC, width=CONV_CH, act=True, name="ssd_conv_fwd")
    sp = _ssd_params(dt_raw, p['dt_bias'].reshape(-1), p['a_log'].reshape(-1), p['ssd_d'].reshape(-1))
    y, states = _ssd_fwd(xbc, sp)
    mix = _mix_fwd(attn, y, proj, p['attn_out_norm'], p['ssd_norm'])
    w_out = hooks.weight('w_out', mix)
    h1 = _matmul(mix, w_out, mode='nn', name="out_proj", add=x)
    hn = _rmsnorm_fwd(h1, p['norm_ffn'], "norm_ffn_fwd")
    w_up = hooks.weight('w_up', hn)
    u0 = _matmul(hn, w_up, mode='nn', name="ffn_up", b_owner=True, tn=1408)
    a = _ffn_act_fwd(u0, p['ffn_conv_w'], p['ffn_conv_b'])
    w_down = hooks.weight('w_down', a)
    h2 = _matmul(a, w_down, mode='nn', name="ffn_down", add=h1, tk=2816)
    loss, dh2, dh2_16, g_norm_final = _final_loss(h2, p['norm_final'].reshape(1, D_MODEL), tgt)

    g = {}
    da = _matmul(dh2_16, w_down, mode='nt', name="ffn_down_dx", out_dtype=BF16, tn=1408)
    g['w_down'] = _matmul(a, dh2_16, mode='tn', name="ffn_down_dw", tm=1408)
    dep = told('w_down', g['w_down'])
    du0, dcw, dcb = _ffn_act_bwd(u0, p['ffn_conv_w'], p['ffn_conv_b'], da)
    g['ffn_conv_w'] = dcw.transpose(1, 0, 2).reshape(FFN_CONV, 2 * D_FF)
    g['ffn_conv_b'] = dcb.transpose(1, 0, 2).reshape(1, 2 * D_FF)
    g['w_up'] = _matmul(hn, du0, mode='tn', name="ffn_up_dw", deps=dep, b_halves=True, owner_major=True,
                        tn=1408)
    dep = told('w_up', g['w_up'])
    dhn = _matmul(du0, w_up, mode='nt', name="ffn_up_dx", out_dtype=BF16, deps=dep, a_halves=True,
                  b_owner=True, tk=2816)
    dh1, dh1_16, g['norm_ffn'] = _rmsnorm_bwd(h1, p['norm_ffn'], dhn, dh2, "norm_ffn_bwd")

    g['w_out'] = _matmul(mix, dh1_16, mode='tn', name="out_proj_dw")
    dep = told('w_out', g['w_out'])
    dmix = _matmul(dh1_16, w_out, mode='nt', name="out_proj_dx", out_dtype=BF16, deps=dep)
    dattn, dy, dz, g['attn_out_norm'], g['ssd_norm'] = _mix_bwd(dmix, attn, y, proj, p['attn_out_norm'],
                                                                p['ssd_norm'])
    dq, dk, dv, dsink = _attn_bwd(proj, sinks, tables, dattn)
    g['sinks'] = dsink[:, :, 0].reshape(1, N_Q_HEADS)
    dxs, dbm, dcm, ddt8, dpar = _ssd_bwd(xbc, sp, states, dy)
    dpar = dpar[:, :, ::SSD_HEAD_DIM]
    g['dt_bias'] = dpar[:, 0, :].reshape(1, SSD_HEADS)
    g['a_log'] = dpar[:, 1, :].reshape(1, SSD_HEADS)
    g['ssd_d'] = dpar[:, 2, :].reshape(1, SSD_HEADS)
    dxbc_act = jnp.concatenate([dxs, dbm, dcm], axis=1)
    dconv = _conv_silu_dact(proj, p['ssd_conv_w'], conv_b, dxbc_act, col0=O_XBC, width=CONV_CH,
                            name="ssd_conv_dact")
    dxbc, g['ssd_conv_w'], g['ssd_conv_b'] = _conv_bwd(proj, p['ssd_conv_w'], dconv, col0=O_XBC, width=CONV_CH,
                                                       name="ssd_conv_bwd")
    dproj = jnp.concatenate([dq, dk, dv, dz, dxbc], axis=1)
    ddt = ddt8.transpose(2, 0, 1).reshape(t, SSD_HEADS)
    ddt_pad = jnp.pad(ddt, ((0, 0), (0, LANES - SSD_HEADS))).astype(BF16)
    g['w_in'] = (_matmul(dproj, xn, mode='tn', name="in_proj_dw", m_rows=IN_PROJ_WIDTH),
                 _matmul(ddt_pad, xn, mode='tn', name="in_proj_dt_dw"))
    dep = told('w_in', g['w_in'])
    dxn_dt = _matmul(ddt_pad, w_in_dt, mode='nn', name="in_proj_dt_dx", deps=dep)
    dxn = _matmul(dproj, w_in_t, mode='nn', name="in_proj_dx", out_dtype=BF16, add=dxn_dt, k_limit=MAIN_WIDTH,
                  tk=2304)
    dx, _, g['norm_mix'] = _rmsnorm_bwd(x, p['norm_mix'], dxn, dh1, "norm_mix_bwd")
    g['norm_final'] = g_norm_final
    return loss, dx, g


def _pack(arrs):
    flat = jnp.concatenate([a.reshape(-1) for a in arrs])
    n = flat.shape[0]
    rows = -(-n // LANES)
    rows = -(-rows // 8) * 8
    return jnp.pad(flat, (0, rows * LANES - n)).reshape(rows, LANES)


def _unpack(packed, shapes):
    flat = packed.reshape(-1)
    out, off = [], 0
    for s in shapes:
        n = 1
        for d in s:
            n *= d
        out.append(flat[off:off + n].reshape(s))
        off += n
    return out


class _StepHooks:
    def __init__(self, first_deps, weight, grad_ready):
        self.first_deps = first_deps
        self.weight = weight
        self.grad_ready = grad_ready


def kernel(x, norm_mix, w_in, sinks, attn_out_norm, ssd_conv_w, ssd_conv_b, dt_bias, a_log, ssd_d, ssd_norm, w_out, norm_ffn, w_up, ffn_conv_w, ffn_conv_b, w_down, norm_final, loss_target, m_norm_mix, m_w_in, m_sinks, m_attn_out_norm, m_ssd_conv_w, m_ssd_conv_b, m_dt_bias, m_a_log, m_ssd_d, m_ssd_norm, m_w_out, m_norm_ffn, m_w_up, m_ffn_conv_w, m_ffn_conv_b, m_w_down, m_norm_final, v_norm_mix, v_w_in, v_sinks, v_attn_out_norm, v_ssd_conv_w, v_ssd_conv_b, v_dt_bias, v_a_log, v_ssd_d, v_ssd_norm, v_w_out, v_norm_ffn, v_w_up, v_ffn_conv_w, v_ffn_conv_b, v_w_down, v_norm_final):
    args = dict(locals())
    w = {n: args[n] for n in WEIGHTS}
    m = {n: args['m_' + n] for n in WEIGHTS}
    v = {n: args['v_' + n] for n in WEIGHTS}
    xi, yi, ci = _me()
    chip = 2 * xi + yi
    pos = jnp.stack([ci, chip]).astype(jnp.int32)

    def place(shard, full_cols):
        z = jnp.zeros((shard.shape[0], full_cols), F32)
        return lax.dynamic_update_slice(z, shard * 0.5, (0, chip * shard.shape[1]))

    conv_pack = _pack([place(ssd_conv_w[0], CONV_CH), place(ffn_conv_w[0], 2 * D_FF)])
    conv_full = _allreduce_small(conv_pack)
    ssd_conv_w_full, ffn_conv_w_full = _unpack(conv_full, [(SSD_CONV, CONV_CH), (FFN_CONV, 2 * D_FF)])

    w_in_t, m_in_t, v_in_t = (jnp.transpose(a[0]) for a in (w_in, m_w_in, v_w_in))
    in_shard = w_in_t.astype(BF16)
    (gathered,), order = _allgather_weights([in_shard], conv_full, cols=True)
    full_in_t = lax.dynamic_update_slice(gathered, in_shard[None], (chip, 0, 0)).reshape(IN_PROJ_WIDTH, D_MODEL)
    w_in_dt = jnp.pad(full_in_t[MAIN_WIDTH:], ((0, LANES - SSD_HEADS), (0, 0)))
    gathers = {}
    order = order[:1, :1]
    for n, shard in (('w_out', w_out[0]), ('w_up', w_up[0]), ('w_down', w_down[0])):
        shard = (shard + order).astype(BF16)
        gathers[n] = _push_start([shard], [(N_CHIPS,) + shard.shape], _route_gather, OTHER_CHIPS,
                                 name="gather_start_" + n)
        order = gathers[n][4][:1, :1]
    first_deps = [gathers['w_down'][4]]

    def weight(name, after):
        send_sems, recv_sems, srcs, lands, _ = gathers[name]
        (own,), (got,) = _push_wait(send_sems, recv_sems, srcs, lands, after, _route_gather_wait, OTHER_CHIPS,
                                    name="gather_wait_" + name)
        whole = lax.dynamic_update_slice(got, own[None], (chip, 0, 0))
        return whole if name == 'w_up' else whole.reshape(-1, D_MODEL)

    reductions, exchanging = {}, {}

    def flush(after):
        tokens = []
        for prev in list(exchanging):
            reductions[prev], token = _grad_scatter_start(exchanging.pop(prev), pos, after)
            tokens.append(token)
        return tokens

    def grad_ready(name, value):
        if name == 'w_in':
            main, dtp = value
            value = lax.dynamic_update_slice(main, dtp[:SSD_HEADS], (MAIN_WIDTH, 0))
        g4 = value if value.ndim == 3 else value.reshape(N_CHIPS, -1, value.shape[1])
        tokens = flush(g4)
        exchanging[name], token = _grad_exchange_start(g4, name, cols=(name == 'w_in'))
        return tokens + [token]

    small = {
        'norm_mix': norm_mix, 'sinks': sinks, 'attn_out_norm': attn_out_norm, 'ssd_conv_w': ssd_conv_w_full,
        'ssd_conv_b': ssd_conv_b, 'dt_bias': dt_bias, 'a_log': a_log, 'ssd_d': ssd_d, 'ssd_norm': ssd_norm,
        'norm_ffn': norm_ffn, 'ffn_conv_w': ffn_conv_w_full, 'ffn_conv_b': ffn_conv_b, 'norm_final': norm_final,
    }
    loss, dx, g = _local_step(x[0], loss_target[0], small, full_in_t, w_in_dt,
                              _StepHooks(tuple(first_deps), weight, grad_ready))

    small_names = [n for n in WEIGHTS if n not in BIG]
    small_g = [loss[:, :1]] + [g[n] for n in small_names]
    small_shapes = [(1, 1)] + [tuple(a.shape) for a in small_g[1:]]
    reduced = _allreduce_small(_pack(small_g))
    started = flush(reduced)[-1]
    red = _unpack(reduced, small_shapes)
    loss_out = red[0].reshape(())
    gsm = dict(zip(small_names, red[1:]))
    gsm['ssd_conv_w'] = lax.dynamic_slice(gsm['ssd_conv_w'], (0, chip * ssd_conv_w.shape[2]),
                                          (SSD_CONV, ssd_conv_w.shape[2]))
    gsm['ffn_conv_w'] = lax.dynamic_slice(gsm['ffn_conv_w'], (0, chip * ffn_conv_w.shape[2]),
                                          (FFN_CONV, ffn_conv_w.shape[2]))

    grads, deltas, new_m, new_v = {}, {}, {}, {}
    after = started
    for n in ('w_down', 'w_up', 'w_out', 'w_in'):
        mine, theirs = _grad_reduce_finish(reductions[n], pos, after)
        if n == 'w_in':
            outs = _adamw_halves(w_in_t, mine, theirs, m_in_t, v_in_t, pos, name="adamw_" + n, cols=True)
            outs = [jnp.transpose(o) for o in outs]
        else:
            outs = _adamw_halves(w[n][0], mine, theirs, m[n][0], v[n][0], pos, name="adamw_" + n)
        after = outs[1]
        grads[n], deltas[n], new_m[n], new_v[n] = [o[None] for o in outs]
    shapes = [tuple(w[n].shape) for n in small_names]
    gp = _pack([gsm[n] for n in small_names])
    d, m2, v2 = _adamw(_pack([w[n] for n in small_names]), gp, _pack([m[n] for n in small_names]),
                       _pack([v[n] for n in small_names]), name="adamw_small")
    for n, gg, dd, mm, vv in zip(small_names, _unpack(gp, shapes), _unpack(d, shapes), _unpack(m2, shapes),
                                 _unpack(v2, shapes)):
        grads[n], deltas[n], new_m[n], new_v[n] = gg, dd, mm, vv

    return (loss_out, dx[None], *[grads[n] for n in WEIGHTS], *[deltas[n] for n in WEIGHTS],
            *[new_m[n] for n in WEIGHTS], *[new_v[n] for n in WEIGHTS])
```

```python
import functools

import jax
import jax.numpy as jnp
from jax import lax
from jax.experimental import pallas as pl
from jax.experimental.pallas import tpu as pltpu

F32 = jnp.float32
BF16 = jnp.bfloat16

D_MODEL = 2048
N_Q_HEADS = 32
N_KV_HEADS = 8
HEAD_DIM = 64
WINDOW = 128
ATTN_BLOCK = 128
ROT_DIM = 16
ROPE_THETA = 500000.0
SSD_HEADS = 32
SSD_HEAD_DIM = 64
SSD_INNER = 2048
SSD_GROUPS = 8
SSD_STATE = 128
SSD_CONV = 4
SSD_CHUNK = 128
ATTN_WIDTH = 2048
KV_WIDTH = 512
BC_WIDTH = 1024
CONV_CH = 4096
IN_PROJ_WIDTH = 9248
MAIN_WIDTH = 9216
D_FF = 5632
FFN_CONV = 3
EPS = 1e-6
O_Q, O_K, O_V, O_Z, O_XBC, O_DT = 0, 2048, 2560, 3072, 5120, 9216

ADAM_LR = 0.001
ADAM_B1 = 0.9
ADAM_B2 = 0.999
ADAM_EPS = 1e-08
ADAM_WD = 0.01
ADAM_STEP = 10

N_CHIPS = 4
NEG = -1e30
LANES = 128
VMEM_LIMIT = 48 * 1024 * 1024
MESH = pl.DeviceIdType.MESH
HBM_SPEC = pl.BlockSpec(memory_space=pltpu.HBM)
TOKEN = jax.ShapeDtypeStruct((8, LANES), F32)

WEIGHTS = ['norm_mix', 'w_in', 'sinks', 'attn_out_norm', 'ssd_conv_w', 'ssd_conv_b', 'dt_bias', 'a_log', 'ssd_d',
           'ssd_norm', 'w_out', 'norm_ffn', 'w_up', 'ffn_conv_w', 'ffn_conv_b', 'w_down', 'norm_final']
BIG = ['w_in', 'w_out', 'w_up', 'w_down']


def _cp(sem=None, vmem=VMEM_LIMIT):
    kw = {'vmem_limit_bytes': vmem}
    if sem is not None:
        kw['dimension_semantics'] = sem
    return pltpu.CompilerParams(**kw)


def _tile(n, pref):
    if n <= pref:
        return n
    t = (pref // LANES) * LANES
    while t > LANES and n % t:
        t -= LANES
    assert n % t == 0, (n, pref)
    return t


def _rows(n, pref):
    t = min(n, pref)
    while n % t:
        t -= 8
    if 4 * t < pref:
        t = pref
        while n % t:
            t += 8
    return t


def _iota(shape, dim):
    return lax.broadcasted_iota(jnp.int32, shape, dim)


def _dot(a, b, mode='nn'):
    dn = {'nn': (((1,), (0,)), ((), ())), 'nt': (((1,), (1,)), ((), ())), 'tn': (((0,), (0,)), ((), ()))}[mode]
    return lax.dot_general(a.astype(BF16), b.astype(BF16), dn, preferred_element_type=F32)


def _dot_exact(a, b):
    return lax.dot_general(a, b, (((1,), (0,)), ((), ())), precision=lax.Precision.HIGHEST,
                           preferred_element_type=F32)


def _sigmoid(x):
    return 1.0 / (1.0 + jnp.exp(-x))


def _softplus(x):
    return jnp.maximum(x, 0.0) + jnp.log(1.0 + jnp.exp(-jnp.abs(x)))


def _matmul(a, b, *, mode, name, out_dtype=F32, add=None, deps=(), tm=1024, tn=1024, tk=2048,
            a_halves=False, b_halves=False, b_owner=False, owner_major=False, n_limit=None, k_limit=None,
            m_rows=None):
    ash, bsh = (a.shape[1:] if a_halves else a.shape), (b.shape[1:] if (b_halves or b_owner) else b.shape)
    if mode == 'nn':
        (m, k), (k2, n) = ash, bsh
    elif mode == 'nt':
        (m, k), (n, k2) = ash, bsh
    else:
        (k, m), (k2, n) = ash, bsh
    if n_limit is not None:
        assert mode == 'nt' and n_limit <= n
        n = n_limit
    if k_limit is not None:
        assert mode == 'nn' and k_limit <= k2
        k2 = k_limit
    if a_halves:
        assert mode == 'nt'
        k = 2 * k
    if b_halves:
        assert mode == 'tn'
        n = 2 * n
    if b_owner:
        assert mode in ('nn', 'nt')
        if mode == 'nn':
            n = 4 * n
        else:
            k2 = 4 * k2
    assert k == k2, (a.shape, b.shape, mode)
    tm = _tile(m, tm)
    tn = _tile(n // 4 if (owner_major or (b_owner and mode == 'nn')) else (n // 2 if b_halves else n), tn)
    tk = _tile(k // 4 if (b_owner and mode == 'nt') else (k // 2 if a_halves else k), tk)
    nk = k // tk
    has_add = add is not None
    assert not (has_add and owner_major)

    def body(*refs):
        a_ref, b_ref = refs[:2]
        add_ref = refs[2] if has_add else None

        def finish(r, o_ref):
            if has_add:
                r = r + add_ref[...].astype(F32)
            o_ref[...] = r.astype(out_dtype)

        if nk == 1:
            finish(_dot(a_ref[...], b_ref[...], mode), refs[-1])
            return
        o_ref, acc = refs[-2:]
        kk = pl.program_id(2)

        @pl.when(kk == 0)
        def _():
            acc[...] = _dot(a_ref[...], b_ref[...], mode)

        @pl.when((kk > 0) & (kk < nk - 1))
        def _():
            acc[...] += _dot(a_ref[...], b_ref[...], mode)

        @pl.when(kk == nk - 1)
        def _():
            finish(acc[...] + _dot(a_ref[...], b_ref[...], mode), o_ref)

    if mode == 'tn':
        a_spec = pl.BlockSpec((tk, tm), lambda i, j, kk: (kk, i))
    elif a_halves:
        nkh = nk // 2
        a_spec = pl.BlockSpec((None, tm, tk), lambda i, j, kk: (kk // nkh, i, kk % nkh))
    else:
        a_spec = pl.BlockSpec((tm, tk), lambda i, j, kk: (i, kk))
    if mode == 'nt' and b_owner:
        nkq = nk // 4
        b_spec = pl.BlockSpec((None, tn, tk), lambda i, j, kk: (kk // nkq, j, kk % nkq))
    elif mode == 'nt':
        b_spec = pl.BlockSpec((tn, tk), lambda i, j, kk: (j, kk))
    elif b_owner:
        njq = (n // 4) // tn
        b_spec = pl.BlockSpec((None, tk, tn), lambda i, j, kk: (j // njq, kk, j % njq))
    elif b_halves:
        njh = (n // 2) // tn
        b_spec = pl.BlockSpec((None, tk, tn), lambda i, j, kk: (j // njh, kk, j % njh))
    else:
        b_spec = pl.BlockSpec((tk, tn), lambda i, j, kk: (kk, j))
    if owner_major:
        njo = (n // 4) // tn
        o_spec = pl.BlockSpec((None, tm, tn), lambda i, j, kk: (j // njo, i, j % njo))
        out_shape = jax.ShapeDtypeStruct((N_CHIPS, m, n // 4), out_dtype)
    else:
        o_spec = pl.BlockSpec((tm, tn), lambda i, j, kk: (i, j))
        out_shape = jax.ShapeDtypeStruct((m if m_rows is None else m_rows, n), out_dtype)
    dep_spec = pl.BlockSpec((8, LANES), lambda i, j, kk: (0, 0))
    in_specs = [a_spec, b_spec] + ([pl.BlockSpec((tm, tn), lambda i, j, kk: (i, j))] if has_add else [])
    in_specs += [dep_spec] * len(deps)
    args = (a, b) + ((add,) if has_add else ()) + tuple(deps)
    return pl.pallas_call(
        body, name=name, grid=(m // tm, n // tn, nk), in_specs=in_specs, out_specs=o_spec, out_shape=out_shape,
        scratch_shapes=[pltpu.VMEM((tm, tn), F32)] if nk > 1 else [],
        compiler_params=_cp(("parallel", "parallel", "arbitrary")))(*args)


def _rmsnorm_fwd(x, g, name, deps=()):
    t, d = x.shape
    tb = _rows(t, 256)

    def body(x_ref, g_ref, *rest):
        o_ref = rest[-1]
        xv = x_ref[...]
        r = lax.rsqrt(jnp.mean(xv * xv, axis=-1, keepdims=True) + EPS)
        o_ref[...] = (xv * r * g_ref[...]).astype(BF16)

    dep_spec = pl.BlockSpec((8, LANES), lambda i: (0, 0))
    return pl.pallas_call(
        body, name=name, grid=(t // tb,),
        in_specs=[pl.BlockSpec((tb, d), lambda i: (i, 0)), pl.BlockSpec((1, d), lambda i: (0, 0))]
        + [dep_spec] * len(deps),
        out_specs=pl.BlockSpec((tb, d), lambda i: (i, 0)), out_shape=jax.ShapeDtypeStruct((t, d), BF16),
        compiler_params=_cp(("parallel",)))(x, g, *deps)


def _rmsnorm_bwd(x, g, dy, res, name, deps=()):
    t, d = x.shape
    tb = _rows(t, 256)

    def body(x_ref, g_ref, dy_ref, res_ref, *rest):
        dx_ref, dx16_ref, dg_ref = rest[-3:]
        i = pl.program_id(0)
        xv = x_ref[...]
        dyv = dy_ref[...].astype(F32)
        r = lax.rsqrt(jnp.mean(xv * xv, axis=-1, keepdims=True) + EPS)
        u = dyv * g_ref[...]
        dx = r * u - xv * (r * r * r * jnp.mean(u * xv, axis=-1, keepdims=True)) + res_ref[...]
        dx_ref[...] = dx
        dx16_ref[...] = dx.astype(BF16)
        part = jnp.sum(dyv * xv * r, axis=0, keepdims=True)

        @pl.when(i == 0)
        def _():
            dg_ref[...] = part

        @pl.when(i > 0)
        def _():
            dg_ref[...] += part

    row = pl.BlockSpec((tb, d), lambda i: (i, 0))
    vec = pl.BlockSpec((1, d), lambda i: (0, 0))
    return pl.pallas_call(
        body, name=name, grid=(t // tb,),
        in_specs=[row, vec, row, row] + [pl.BlockSpec((8, LANES), lambda i: (0, 0))] * len(deps),
        out_specs=[row, row, vec],
        out_shape=[jax.ShapeDtypeStruct((t, d), F32), jax.ShapeDtypeStruct((t, d), BF16),
                   jax.ShapeDtypeStruct((1, d), F32)],
        compiler_params=_cp(("arbitrary",)))(x, g, dy, res, *deps)


def _final_loss(h, g, tgt):
    t, d = h.shape
    tb = _rows(t, 256)

    def body(h_ref, g_ref, t_ref, loss_ref, dh_ref, dh16_ref, dg_ref):
        i = pl.program_id(0)
        hv = h_ref[...]
        gv = g_ref[...]
        r = lax.rsqrt(jnp.mean(hv * hv, axis=-1, keepdims=True) + EPS)
        y = hv * r * gv
        diff = y - t_ref[...]
        lpart = jnp.sum(jnp.sum(diff * diff, axis=1, keepdims=True), axis=0, keepdims=True) * (0.5 / d)
        dy = diff * (1.0 / d)
        u = dy * gv
        dh = r * u - hv * (r * r * r * jnp.mean(u * hv, axis=-1, keepdims=True))
        dh_ref[...] = dh
        dh16_ref[...] = dh.astype(BF16)
        gpart = jnp.sum(dy * hv * r, axis=0, keepdims=True)
        lrow = jnp.broadcast_to(lpart, (1, LANES))

        @pl.when(i == 0)
        def _():
            loss_ref[...] = lrow
            dg_ref[...] = gpart

        @pl.when(i > 0)
        def _():
            loss_ref[...] += lrow
            dg_ref[...] += gpart

    row = pl.BlockSpec((tb, d), lambda i: (i, 0))
    vec = pl.BlockSpec((1, d), lambda i: (0, 0))
    return pl.pallas_call(
        body, name="final_loss", grid=(t // tb,), in_specs=[row, vec, row],
        out_specs=[pl.BlockSpec((1, LANES), lambda i: (0, 0)), row, row, vec],
        out_shape=[jax.ShapeDtypeStruct((1, LANES), F32), jax.ShapeDtypeStruct((t, d), F32),
                   jax.ShapeDtypeStruct((t, d), BF16), jax.ShapeDtypeStruct((1, d), F32)],
        compiler_params=_cp(("arbitrary",)))(h, g, tgt)


def _rope_tables(t):
    pos = jnp.arange(t, dtype=F32)
    inv = 1.0 / (ROPE_THETA ** (jnp.arange(0, ROT_DIM, 2, dtype=F32) / ROT_DIM))
    ang = pos[:, None] * inv[None, :]
    cos, sin = jnp.cos(ang), jnp.sin(ang)
    half = ROT_DIM // 2
    rest = HEAD_DIM - ROT_DIM
    c = jnp.concatenate([cos, cos, jnp.ones((t, rest), F32)], axis=1)
    s1 = jnp.concatenate([-sin, jnp.zeros((t, half + rest), F32)], axis=1)
    s2 = jnp.concatenate([jnp.zeros((t, half), F32), sin, jnp.zeros((t, rest), F32)], axis=1)
    return jnp.concatenate([jnp.tile(v, (1, LANES // HEAD_DIM)) for v in (c, s1, s2)], axis=1)


def _split_tables(tab):
    return tab[:, :LANES], tab[:, LANES:2 * LANES], tab[:, 2 * LANES:]


def _rope(x, c, s1, s2):
    half = ROT_DIM // 2
    return x * c + pltpu.roll(x, LANES - half, 1) * s1 + pltpu.roll(x, half, 1) * s2


def _rope_t(g, c, s1, s2):
    half = ROT_DIM // 2
    return g * c + pltpu.roll(g * s1, half, 1) + pltpu.roll(g * s2, LANES - half, 1)


def _band_masks(i, heads):
    n = heads * ATTN_BLOCK
    q = jnp.bitwise_and(_iota((n, ATTN_BLOCK), 0), ATTN_BLOCK - 1)
    j = _iota((n, ATTN_BLOCK), 1)
    upper = j > q
    return upper, upper & (j < jnp.where(i > 0, 0, ATTN_BLOCK))


def _fold_band(full, upper):
    return jnp.where(upper, full[:, :ATTN_BLOCK], full[:, ATTN_BLOCK:])


def _unfold_band(band, upper):
    return jnp.concatenate([jnp.where(upper, band, 0.0), jnp.where(upper, 0.0, band)], axis=1)


def _half_masks():
    lane = _iota((1, LANES), 1)
    return [(lane < HEAD_DIM).astype(F32), (lane >= HEAD_DIM).astype(F32)]


def _stack_heads(blocks, hm, j):
    pieces = []
    for r in range(4):
        qb, half = (4 * j + r) // 2, (4 * j + r) % 2
        piece = blocks[qb] * hm[half]
        if half != j:
            piece = pltpu.roll(piece, HEAD_DIM, 1)
        pieces.append(piece)
    return jnp.concatenate(pieces, axis=0)


def _unstack_heads(stacked, j):
    out = []
    for qb in (2 * j, 2 * j + 1):
        acc = None
        for half in range(2):
            r = 2 * qb + half - 4 * j
            piece = stacked[r * ATTN_BLOCK:(r + 1) * ATTN_BLOCK]
            if half != j:
                piece = pltpu.roll(piece, HEAD_DIM, 1)
            acc = piece if acc is None else acc + piece
        out.append((qb, acc))
    return out


def _sink_column(sink_ref, base):
    return jnp.concatenate([jnp.full((ATTN_BLOCK, 1), sink_ref[base + r], F32) for r in range(4)], axis=0)


def _attn_specs(nb_clamp):
    blk = ATTN_BLOCK
    kb, vb = O_K // LANES, O_V // LANES

    def cur(i):
        return jnp.minimum(i, nb_clamp)

    def prev(i):
        return jnp.maximum(jnp.minimum(i, nb_clamp + 1) - 1, 0)

    q = pl.BlockSpec((blk, 512), lambda p, i: (cur(i), p))
    kc = pl.BlockSpec((blk, LANES), lambda p, i: (cur(i), kb + p))
    kp = pl.BlockSpec((blk, LANES), lambda p, i: (prev(i), kb + p))
    vc = pl.BlockSpec((blk, LANES), lambda p, i: (cur(i), vb + p))
    vp = pl.BlockSpec((blk, LANES), lambda p, i: (prev(i), vb + p))
    tc = pl.BlockSpec((blk, 3 * LANES), lambda p, i: (cur(i), 0))
    tp = pl.BlockSpec((blk, 3 * LANES), lambda p, i: (prev(i), 0))
    return q, kc, kp, vc, vp, tc, tp


def _attn_fwd(proj, sinks, tables):
    t = proj.shape[0]
    nb = t // ATTN_BLOCK
    scale = HEAD_DIM ** -0.5

    def body(sink_ref, q_ref, kc_ref, kp_ref, vc_ref, vp_ref, tc_ref, tp_ref, o_ref):
        p = pl.program_id(0)
        i = pl.program_id(1)
        cc, s1c, s2c = _split_tables(tc_ref[...])
        kband = jnp.concatenate([_rope(kp_ref[...], *_split_tables(tp_ref[...])),
                                 _rope(kc_ref[...], cc, s1c, s2c)], axis=0).astype(BF16)
        vband = jnp.concatenate([vp_ref[...], vc_ref[...]], axis=0)
        hm = _half_masks()
        vsel = [(vband * hm[j]).astype(BF16) for j in range(2)]
        upper, dropped = _band_masks(i, 1)
        qr = [_rope(q_ref[:, qb * LANES:(qb + 1) * LANES], cc, s1c, s2c) for qb in range(4)]

        def scores(hh):
            qb, half, j = hh // 2, hh % 2, hh // 4
            qs = qr[qb] * hm[half]
            if half != j:
                qs = pltpu.roll(qs, HEAD_DIM, 1)
            return _dot(qs, kband, 'nt')

        ahead = scores(0)
        acc = None
        for hh in range(8):
            qb, half, j = hh // 2, hh % 2, hh // 4
            raw = ahead
            if hh + 1 < 8:
                ahead = scores(hh + 1)
            s = jnp.where(dropped, NEG, _fold_band(raw, upper) * scale)
            sink = sink_ref[p * 8 + hh]
            m = jnp.maximum(jnp.max(s, axis=1, keepdims=True), sink)
            pe = jnp.exp(s - m)
            den = jnp.sum(pe, axis=1, keepdims=True) + jnp.exp(sink - m)
            o = _dot(_unfold_band(pe / den, upper), vsel[j])
            if half != j:
                o = pltpu.roll(o, HEAD_DIM, 1)
            acc = o if half == 0 else acc + o
            if half == 1:
                o_ref[:, qb * LANES:(qb + 1) * LANES] = acc

    q, kc, kp, vc, vp, tc, tp = _attn_specs(nb - 1)
    smem = pl.BlockSpec(memory_space=pltpu.SMEM)
    return pl.pallas_call(
        body, name="attn_fwd", grid=(4, nb),
        in_specs=[smem, q, kc, kp, vc, vp, tc, tp],
        out_specs=pl.BlockSpec((ATTN_BLOCK, 512), lambda p, i: (i, p)),
        out_shape=jax.ShapeDtypeStruct((t, ATTN_WIDTH), F32),
        compiler_params=_cp(("parallel", "arbitrary")))(sinks, proj, proj, proj, proj, proj, tables, tables)


def _attn_bwd(proj, sinks, tables, dout):
    t = proj.shape[0]
    nb = t // ATTN_BLOCK
    scale = HEAD_DIM ** -0.5

    def body(sink_ref, q_ref, kc_ref, kp_ref, vc_ref, vp_ref, tc_ref, tp_ref,
             do_ref, dq_ref, dk_ref, dv_ref, ds_ref, carry_k, carry_v):
        p = pl.program_id(0)
        i = pl.program_id(1)
        ptab = _split_tables(tp_ref[...])

        @pl.when(i == 0)
        def _():
            carry_k[...] = jnp.zeros_like(carry_k)
            carry_v[...] = jnp.zeros_like(carry_v)
            ds_ref[...] = jnp.zeros_like(ds_ref)

        @pl.when(i < nb)
        def _():
            cc, s1c, s2c = _split_tables(tc_ref[...])
            kband = jnp.concatenate([_rope(kp_ref[...], *ptab), _rope(kc_ref[...], cc, s1c, s2c)], axis=0)
            vband = jnp.concatenate([vp_ref[...], vc_ref[...]], axis=0)
            hm = _half_masks()
            kband16 = kband.astype(BF16)
            vband16 = vband.astype(BF16)
            upper, dropped = _band_masks(i, 4)
            dkb = jnp.zeros((2 * ATTN_BLOCK, LANES), F32)
            dvb = jnp.zeros((2 * ATTN_BLOCK, LANES), F32)
            row8 = _iota((8, LANES), 0)
            dsink = jnp.zeros((8, LANES), F32)
            qr = [_rope(q_ref[:, qb * LANES:(qb + 1) * LANES], cc, s1c, s2c) for qb in range(4)]
            dob = [do_ref[:, qb * LANES:(qb + 1) * LANES] for qb in range(4)]
            for j in range(2):
                qst = _stack_heads(qr, hm, j).astype(BF16)
                dost = _stack_heads(dob, hm, j).astype(BF16)
                s = jnp.where(dropped, NEG, _fold_band(_dot(qst, kband16, 'nt'), upper) * scale)
                sink = _sink_column(sink_ref, p * 8 + 4 * j)
                m = jnp.maximum(jnp.max(s, axis=1, keepdims=True), sink)
                pe = jnp.exp(s - m)
                psink = jnp.exp(sink - m)
                den = jnp.sum(pe, axis=1, keepdims=True) + psink
                pr = pe / den
                dvb = dvb + _dot(_unfold_band(pr, upper).T, dost)
                dp = _fold_band(_dot(dost, vband16, 'nt'), upper)
                delta = jnp.sum(pr * dp, axis=1, keepdims=True)
                dsc = _unfold_band(pr * (dp - delta) * scale, upper)
                dsk = psink / den * delta
                for r in range(4):
                    part = jnp.sum(dsk[r * ATTN_BLOCK:(r + 1) * ATTN_BLOCK])
                    dsink = dsink + jnp.where(row8 == 4 * j + r, -part, 0.0)
                for qb, dqb in _unstack_heads(_dot(dsc, kband * hm[j]), j):
                    dq_ref[:, qb * LANES:(qb + 1) * LANES] = _rope_t(dqb, cc, s1c, s2c).astype(BF16)
                dkb = dkb + _dot(dsc.T, qst)
            ds_ref[0] += dsink
            dk_ref[...] = _rope_t(carry_k[...] + dkb[:ATTN_BLOCK], *ptab).astype(BF16)
            dv_ref[...] = (carry_v[...] + dvb[:ATTN_BLOCK]).astype(BF16)
            carry_k[...] = dkb[ATTN_BLOCK:]
            carry_v[...] = dvb[ATTN_BLOCK:]

        @pl.when(i == nb)
        def _():
            dk_ref[...] = _rope_t(carry_k[...], *ptab).astype(BF16)
            dv_ref[...] = carry_v[...].astype(BF16)

    q, kc, kp, vc, vp, tc, tp = _attn_specs(nb - 1)
    smem = pl.BlockSpec(memory_space=pltpu.SMEM)
    qblk = pl.BlockSpec((ATTN_BLOCK, 512), lambda p, i: (jnp.minimum(i, nb - 1), p))
    kvout = pl.BlockSpec((ATTN_BLOCK, LANES), lambda p, i: (jnp.maximum(i - 1, 0), p))
    return pl.pallas_call(
        body, name="attn_bwd", grid=(4, nb + 1),
        in_specs=[smem, q, kc, kp, vc, vp, tc, tp, qblk],
        out_specs=[qblk, kvout, kvout, pl.BlockSpec((1, 8, LANES), lambda p, i: (p, 0, 0))],
        out_shape=[jax.ShapeDtypeStruct((t, ATTN_WIDTH), BF16), jax.ShapeDtypeStruct((t, KV_WIDTH), BF16),
                   jax.ShapeDtypeStruct((t, KV_WIDTH), BF16), jax.ShapeDtypeStruct((4, 8, LANES), F32)],
        scratch_shapes=[pltpu.VMEM((ATTN_BLOCK, LANES), F32), pltpu.VMEM((ATTN_BLOCK, LANES), F32)],
        compiler_params=_cp(("parallel", "arbitrary")))(sinks, proj, proj, proj, proj, proj, tables, tables, dout)


def _shift_rows(x, prev8, j):
    n, c = x.shape
    r = pltpu.roll(x.reshape(n // 8, 8, c), j, 1)
    before = pltpu.roll(prev8, j, 0)[None]
    if n > 8:
        before = jnp.concatenate([before, r[:-1]], axis=0)
    return jnp.where(_iota((1, 8, 1), 1) < j, before, r).reshape(n, c)


def _shift_rows_up(x, next8, j):
    n, c = x.shape
    r = pltpu.roll(x.reshape(n // 8, 8, c), 8 - j, 1)
    after = pltpu.roll(next8, 8 - j, 0)[None]
    if n > 8:
        after = jnp.concatenate([r[1:], after], axis=0)
    return jnp.where(_iota((1, 8, 1), 1) >= 8 - j, after, r).reshape(n, c)


def _conv_apply(x, prev8, w, b, taps):
    u = b + x * w[taps - 1:taps]
    for j in range(1, taps):
        u = u + _shift_rows(x, prev8, j) * w[taps - 1 - j:taps - j]
    return u


def _conv_grads(du, du_next8, x, w, taps):
    dx = du * w[taps - 1:taps]
    rowk = _iota((taps, 1), 0)
    dw = jnp.where(rowk == taps - 1, jnp.sum(du * x, axis=0, keepdims=True), 0.0)
    for j in range(1, taps):
        ahead = _shift_rows_up(du, du_next8, j)
        dx = dx + ahead * w[taps - 1 - j:taps - j]
        dw = dw + jnp.where(rowk == taps - 1 - j, jnp.sum(ahead * x, axis=0, keepdims=True), 0.0)
    return dx, dw, jnp.sum(du, axis=0, keepdims=True)


def _conv_specs(tb, tc, col0, t):
    c0 = col0 // tc
    cur = pl.BlockSpec((tb, tc), lambda j, i: (i, c0 + j))
    prev = pl.BlockSpec((8, tc), lambda j, i: (jnp.maximum(i * (tb // 8) - 1, 0), c0 + j))
    nxt = pl.BlockSpec((8, tc), lambda j, i: (jnp.minimum((i + 1) * (tb // 8), t // 8 - 1), c0 + j))
    return cur, prev, nxt


def _conv_fwd(x, w, b, *, col0, width, act, name):
    t = x.shape[0]
    taps = w.shape[0]
    tb, tc = _rows(t, 512), _tile(width, 1024)
    assert col0 % tc == 0

    def body(x_ref, xp_ref, w_ref, b_ref, o_ref):
        i = pl.program_id(1)
        prev8 = jnp.where(i > 0, xp_ref[...], 0.0)
        u = _conv_apply(x_ref[...], prev8, w_ref[...], b_ref[...], taps)
        if act:
            u = u * _sigmoid(u)
        o_ref[...] = u

    cur, prev, _ = _conv_specs(tb, tc, col0, t)
    par = pl.BlockSpec((taps, tc), lambda j, i: (0, j))
    bias = pl.BlockSpec((1, tc), lambda j, i: (0, j))
    return pl.pallas_call(
        body, name=name, grid=(width // tc, t // tb), in_specs=[cur, prev, par, bias],
        out_specs=pl.BlockSpec((tb, tc), lambda j, i: (i, j)), out_shape=jax.ShapeDtypeStruct((t, width), F32),
        compiler_params=_cp(("parallel", "parallel")))(x, x, w, b)


def _conv_silu_dact(x, w, b, dout, *, col0, width, name):
    t = x.shape[0]
    taps = w.shape[0]
    tb, tc = _rows(t, 512), _tile(width, 1024)

    def body(x_ref, xp_ref, w_ref, b_ref, d_ref, o_ref):
        i = pl.program_id(1)
        prev8 = jnp.where(i > 0, xp_ref[...], 0.0)
        u = _conv_apply(x_ref[...], prev8, w_ref[...], b_ref[...], taps)
        sg = _sigmoid(u)
        o_ref[...] = d_ref[...] * (sg * (1.0 + u * (1.0 - sg)))

    cur, prev, _ = _conv_specs(tb, tc, col0, t)
    par = pl.BlockSpec((taps, tc), lambda j, i: (0, j))
    bias = pl.BlockSpec((1, tc), lambda j, i: (0, j))
    out = pl.BlockSpec((tb, tc), lambda j, i: (i, j))
    return pl.pallas_call(
        body, name=name, grid=(width // tc, t // tb), in_specs=[cur, prev, par, bias, out],
        out_specs=out, out_shape=jax.ShapeDtypeStruct((t, width), F32),
        compiler_params=_cp(("parallel", "parallel")))(x, x, w, b, dout)


def _conv_bwd(x, w, du, *, col0, width, name):
    t = x.shape[0]
    taps = w.shape[0]
    tb, tc = _rows(t, 512), _tile(width, 1024)
    nrow = t // tb

    def body(x_ref, w_ref, du_ref, dun_ref, dx_ref, dw_ref, db_ref):
        i = pl.program_id(1)
        next8 = jnp.where(i < nrow - 1, dun_ref[...], 0.0)
        dx, dwv, dbv = _conv_grads(du_ref[...], next8, x_ref[...], w_ref[...], taps)
        dx_ref[...] = dx.astype(BF16)

        @pl.when(i == 0)
        def _():
            dw_ref[...] = dwv
            db_ref[...] = dbv

        @pl.when(i > 0)
        def _():
            dw_ref[...] += dwv
            db_ref[...] += dbv

    cur, _, _ = _conv_specs(tb, tc, col0, t)
    dcur, _, dnxt = _conv_specs(tb, tc, 0, t)
    par = pl.BlockSpec((taps, tc), lambda j, i: (0, j))
    bias = pl.BlockSpec((1, tc), lambda j, i: (0, j))
    return pl.pallas_call(
        body, name=name, grid=(width // tc, nrow), in_specs=[cur, par, dcur, dnxt],
        out_specs=[dcur, par, bias],
        out_shape=[jax.ShapeDtypeStruct((t, width), BF16), jax.ShapeDtypeStruct((taps, width), F32),
                   jax.ShapeDtypeStruct((1, width), F32)],
        compiler_params=_cp(("parallel", "arbitrary")))(x, w, du, du)


def _ffn_specs(tb, tc, t):
    nc = D_FF // tc

    def cur(half):
        return pl.BlockSpec((tb, tc), lambda j, i: (i, half * nc + j))

    def prev(half):
        return pl.BlockSpec((8, tc), lambda j, i: (jnp.maximum(i * (tb // 8) - 1, 0), half * nc + j))

    def nxt(half):
        return pl.BlockSpec((8, tc), lambda j, i: (jnp.minimum((i + 1) * (tb // 8), t // 8 - 1), half * nc + j))

    def par(rows, half):
        return pl.BlockSpec((rows, tc), lambda j, i: (0, half * nc + j))

    return cur, prev, nxt, par


def _ffn_act_fwd(u0, w, b):
    t = u0.shape[0]
    tb, tc = _rows(t, 512), _tile(D_FF, 1408)
    cur, prev, _, par = _ffn_specs(tb, tc, t)

    def body(g_ref, gp_ref, v_ref, vp_ref, wg_ref, wv_ref, bg_ref, bv_ref, o_ref, u_ref):
        i = pl.program_id(1)
        ug = _conv_apply(g_ref[...], jnp.where(i > 0, gp_ref[...], 0.0), wg_ref[...], bg_ref[...], FFN_CONV)
        uv = _conv_apply(v_ref[...], jnp.where(i > 0, vp_ref[...], 0.0), wv_ref[...], bv_ref[...], FFN_CONV)
        o_ref[...] = (ug * _sigmoid(ug) * uv).astype(BF16)
        u_ref[0] = ug
        u_ref[1] = uv

    return pl.pallas_call(
        body, name="ffn_act_fwd", grid=(D_FF // tc, t // tb),
        in_specs=[cur(0), prev(0), cur(1), prev(1), par(FFN_CONV, 0), par(FFN_CONV, 1), par(1, 0), par(1, 1)],
        out_specs=[pl.BlockSpec((tb, tc), lambda j, i: (i, j)), pl.BlockSpec((2, tb, tc), lambda j, i: (0, i, j))],
        out_shape=[jax.ShapeDtypeStruct((t, D_FF), BF16), jax.ShapeDtypeStruct((2, t, D_FF), F32)],
        compiler_params=_cp(("parallel", "parallel")))(u0, u0, u0, u0, w, w, b, b)


def _ffn_act_bwd(u0, u, w, da):
    t = u0.shape[0]
    tb, tc = _rows(t, 256), _tile(D_FF, 1408)
    nrow = t // tb
    taps = FFN_CONV
    cur, _, _, par = _ffn_specs(tb, tc, t)

    def dact(ug, uv, dav):
        sg = _sigmoid(ug)
        return dav * uv * (sg * (1.0 + ug * (1.0 - sg))), dav * ug * sg

    def body(g_ref, v_ref, u_ref, un_ref, wg_ref, wv_ref, da_ref, dan_ref, dx_ref, dw_ref, db_ref):
        i = pl.program_id(1)
        dug, duv = dact(u_ref[0], u_ref[1], da_ref[...].astype(F32))
        dan = jnp.where(i < nrow - 1, dan_ref[...].astype(F32)[:8], 0.0)
        dugn, duvn = dact(un_ref[0], un_ref[1], dan)
        dxg, dwg, dbg = _conv_grads(dug, dugn, g_ref[...], wg_ref[...], taps)
        dxv, dwv, dbv = _conv_grads(duv, duvn, v_ref[...], wv_ref[...], taps)
        dx_ref[0] = dxg.astype(BF16)
        dx_ref[1] = dxv.astype(BF16)

        @pl.when(i == 0)
        def _():
            dw_ref[0] = dwg
            dw_ref[1] = dwv
            db_ref[0] = dbg
            db_ref[1] = dbv

        @pl.when(i > 0)
        def _():
            dw_ref[0] += dwg
            dw_ref[1] += dwv
            db_ref[0] += dbg
            db_ref[1] += dbv

    both = pl.BlockSpec((2, tb, tc), lambda j, i: (0, i, j))
    both_nxt = pl.BlockSpec((2, 8, tc), lambda j, i: (0, jnp.minimum((i + 1) * (tb // 8), t // 8 - 1), j))
    da_cur = pl.BlockSpec((tb, tc), lambda j, i: (i, j))
    da_nxt = pl.BlockSpec((16, tc), lambda j, i: (jnp.minimum((i + 1) * (tb // 16), t // 16 - 1), j))
    return pl.pallas_call(
        body, name="ffn_act_bwd", grid=(D_FF // tc, nrow),
        in_specs=[cur(0), cur(1), both, both_nxt, par(taps, 0), par(taps, 1), da_cur, da_nxt],
        out_specs=[both, pl.BlockSpec((2, taps, tc), lambda j, i: (0, 0, j)),
                   pl.BlockSpec((2, 1, tc), lambda j, i: (0, 0, j))],
        out_shape=[jax.ShapeDtypeStruct((2, t, D_FF), BF16), jax.ShapeDtypeStruct((2, taps, D_FF), F32),
                   jax.ShapeDtypeStruct((2, 1, D_FF), F32)],
        compiler_params=_cp(("parallel", "arbitrary")))(u0, u0, u, u, w, w, da, da)


def _head_masks():
    lane = _iota((1, 4 * SSD_HEAD_DIM), 1)
    return [((lane >= r * SSD_HEAD_DIM) & (lane < (r + 1) * SSD_HEAD_DIM)).astype(F32) for r in range(4)]


def _segsum(v):
    first = _iota((1, LANES), 1) < SSD_HEAD_DIM
    halves = []
    for k in range(2):
        vh = v[:, k * LANES:(k + 1) * LANES]
        both = jnp.sum(vh, axis=1, keepdims=True)
        one = jnp.sum(jnp.where(first, vh, 0.0), axis=1, keepdims=True)
        halves.append(jnp.where(first, one, both - one))
    return jnp.concatenate(halves, axis=1)


def _ssd_common(raw_e, prow, rawr4, bcol, acol):
    n = SSD_CHUNK
    dt_e = _softplus(raw_e + prow[0:1, :])
    a_e = -jnp.exp(prow[1:2, :])
    d_e = prow[2:3, :]
    tril = (_iota((n, n), 0) >= _iota((n, n), 1)).astype(F32)
    acs_e = _dot_exact(tril, dt_e * a_e)
    last_e = acs_e[n - 1:n, :]
    dtr4 = _softplus(rawr4 + bcol)
    triu = (_iota((n, n), 0) <= _iota((n, n), 1)).astype(F32)
    acs_r4 = _dot_exact(dtr4 * (-jnp.exp(acol)), triu)
    return dt_e, a_e, d_e, acs_e, last_e, acs_r4


def _decay_matrix(acs_e, acs_r4, r):
    n = SSD_CHUNK
    col = acs_e[:, r * SSD_HEAD_DIM:r * SSD_HEAD_DIM + 1]
    seg = col - acs_r4[r:r + 1, :]
    causal = _iota((n, n), 0) >= _iota((n, n), 1)
    return jnp.exp(jnp.where(causal, seg, NEG))


SSD_STEP_CHUNKS = 4
SSD_ROWS = SSD_STEP_CHUNKS * SSD_CHUNK


def _ssd_specs(t, rev):
    nb = t // SSD_ROWS
    xb, bb, cb = 0, SSD_INNER // SSD_STATE, (SSD_INNER + BC_WIDTH) // SSD_STATE

    def ch(c):
        return (nb - 1 - c) if rev else c

    x = pl.BlockSpec((SSD_ROWS, 256), lambda g, c: (ch(c), xb + g))
    bm = pl.BlockSpec((SSD_ROWS, SSD_STATE), lambda g, c: (ch(c), bb + g))
    cm = pl.BlockSpec((SSD_ROWS, SSD_STATE), lambda g, c: (ch(c), cb + g))
    dtc = pl.BlockSpec((1, SSD_ROWS, 256), lambda g, c: (g, ch(c), 0))
    dtr = pl.BlockSpec((1, 4, SSD_ROWS), lambda g, c: (g, 0, ch(c)))
    prow = pl.BlockSpec((1, 3, 256), lambda g, c: (g, 0, 0))
    pcol = pl.BlockSpec((1, 4, 1), lambda g, c: (g, 0, 0))
    st = pl.BlockSpec((1, SSD_STEP_CHUNKS, SSD_STATE, 256), lambda g, c: (g, ch(c), 0, 0))
    return x, bm, cm, dtc, dtr, prow, pcol, st, ch


def _ssd_params(dt_raw, dt_bias, a_log, ssd_d):
    t = dt_raw.shape[0]
    by_group = dt_raw.reshape(t, SSD_GROUPS, 4)
    dtc = jnp.repeat(by_group, SSD_HEAD_DIM, axis=2).transpose(1, 0, 2)
    dtr = by_group.transpose(1, 2, 0)
    prow = jnp.repeat(jnp.stack([dt_bias.reshape(SSD_GROUPS, 4), a_log.reshape(SSD_GROUPS, 4),
                                 ssd_d.reshape(SSD_GROUPS, 4)], axis=1), SSD_HEAD_DIM, axis=2)
    bcol = dt_bias.reshape(SSD_GROUPS, 4, 1)
    acol = a_log.reshape(SSD_GROUPS, 4, 1)
    return dtc, dtr, prow, bcol, acol


def _ssd_fwd(xbc, params):
    t = xbc.shape[0]
    nc = t // SSD_CHUNK
    dtc, dtr, prow, bcol, acol = params

    def body(x_ref, b_ref, c_ref, dtc_ref, dtr_ref, prow_ref, bcol_ref, acol_ref, y_ref, st_ref, s_scr):
        c = pl.program_id(1)

        @pl.when(c == 0)
        def _():
            s_scr[...] = jnp.zeros_like(s_scr)

        masks = _head_masks()
        s = s_scr[...]
        for k in range(SSD_STEP_CHUNKS):
            rows = slice(k * SSD_CHUNK, (k + 1) * SSD_CHUNK)
            dt_e, a_e, d_e, acs_e, last_e, acs_r4 = _ssd_common(
                dtc_ref[0, rows], prow_ref[0], dtr_ref[0][:, rows], bcol_ref[0], acol_ref[0])
            xv = x_ref[rows]
            bm, cm = b_ref[rows], c_ref[rows]
            st_ref[0, k] = s
            xdt = xv * dt_e
            cb = _dot(cm, bm, 'nt')
            y = _dot(cm, s) * jnp.exp(acs_e) + xv * d_e
            for r in range(4):
                mr = cb * _decay_matrix(acs_e, acs_r4, r)
                y = y + _dot(mr, xdt * masks[r])
            y_ref[rows] = y
            w = xdt * jnp.exp(last_e - acs_e)
            s = s * jnp.exp(last_e) + _dot(bm.T, w)
        s_scr[...] = s

    x, bm, cm, dtcs, dtrs, prs, pcs, st, _ = _ssd_specs(t, False)
    return pl.pallas_call(
        body, name="ssd_fwd", grid=(SSD_GROUPS, t // SSD_ROWS), in_specs=[x, bm, cm, dtcs, dtrs, prs, pcs, pcs],
        out_specs=[pl.BlockSpec((SSD_ROWS, 256), lambda g, c: (c, g)), st],
        out_shape=[jax.ShapeDtypeStruct((t, SSD_INNER), F32),
                   jax.ShapeDtypeStruct((SSD_GROUPS, nc, SSD_STATE, 256), F32)],
        scratch_shapes=[pltpu.VMEM((SSD_STATE, 256), F32)],
        compiler_params=_cp(("parallel", "arbitrary")))(xbc, xbc, xbc, dtc, dtr, prow, bcol, acol)


def _ssd_bwd(xbc, params, states, dy):
    t = xbc.shape[0]
    nc = t // SSD_CHUNK
    n = SSD_CHUNK
    dtc, dtr, prow, bcol, acol = params

    def body(x_ref, b_ref, c_ref, dtc_ref, dtr_ref, prow_ref, bcol_ref, acol_ref, st_ref, dy_ref,
             dx_ref, db_ref, dc_ref, ddt_ref, dp_ref, ds_scr):
        c = pl.program_id(1)

        @pl.when(c == 0)
        def _():
            ds_scr[...] = jnp.zeros_like(ds_scr)
            dp_ref[...] = jnp.zeros_like(dp_ref)

        masks = _head_masks()
        ds = ds_scr[...]
        for k in reversed(range(SSD_STEP_CHUNKS)):
            rows = slice(k * SSD_CHUNK, (k + 1) * SSD_CHUNK)
            raw_e = dtc_ref[0, rows]
            prw = prow_ref[0]
            dt_e, a_e, d_e, acs_e, last_e, acs_r4 = _ssd_common(raw_e, prw, dtr_ref[0][:, rows], bcol_ref[0], acol_ref[0])
            xv = x_ref[rows]
            bm, cm = b_ref[rows], c_ref[rows]
            s = st_ref[0, k]
            dyv = dy_ref[rows]
            e_e = jnp.exp(acs_e)
            dec_e = jnp.exp(last_e - acs_e)
            cd_e = jnp.exp(last_e)
            xdt = xv * dt_e
            w = xdt * dec_e
            b16, c16, s16, ds16 = bm.astype(BF16), cm.astype(BF16), s.astype(BF16), ds.astype(BF16)
            cb = _dot(c16, b16, 'nt')
            yoff_raw = _dot(c16, s16)
            dye = dyv * e_e
            dye16 = dye.astype(BF16)
            dcm = _dot(dye16, s16, 'nt')
            ds_prev = ds * cd_e + _dot(cm.T, dye16)
            dacs_e = _segsum(dyv * yoff_raw) * e_e
            dw = _dot(b16, ds16)
            dbm = _dot(w, ds16, 'nt')
            tdec = _segsum(dw * xdt) * dec_e
            dacs_e = dacs_e - tdec
            dlast_e = jnp.sum(tdec, axis=0, keepdims=True)
            dxdt = dw * dec_e
            dlast_e = dlast_e + _segsum(jnp.sum(ds * s, axis=0, keepdims=True)) * cd_e
            dcb = jnp.zeros((n, n), F32)
            for r in range(4):
                lm = _decay_matrix(acs_e, acs_r4, r)
                mr = cb * lm
                dyr16 = (dyv * masks[r]).astype(BF16)
                dm = _dot(dyr16, xdt * masks[r], 'nt')
                dcb = dcb + dm * lm
                dseg = dm * mr
                dcol = jnp.sum(dseg, axis=1, keepdims=True) - jnp.sum(dseg.T, axis=1, keepdims=True)
                dacs_e = dacs_e + dcol * masks[r]
                dxdt = dxdt + _dot(mr.T, dyr16)
            dcm = dcm + _dot(dcb, b16)
            dbm = dbm + _dot(dcb.T, c16)
            dacs_e = dacs_e + jnp.where(_iota((n, 1), 0) == n - 1, dlast_e, 0.0)
            triu = (_iota((n, n), 0) <= _iota((n, n), 1)).astype(F32)
            ddta_e = _dot_exact(triu, dacs_e)
            ddt_e = ddta_e * a_e + _segsum(dxdt * xv)
            dx_ref[rows] = dxdt * dt_e + dyv * d_e
            db_ref[rows] = dbm
            dc_ref[rows] = dcm
            draw_e = ddt_e * _sigmoid(raw_e + prw[0:1, :])
            draw_t = draw_e.T
            ddt_ref[0, :, rows] = jnp.concatenate([draw_t[r * SSD_HEAD_DIM:r * SSD_HEAD_DIM + 1] for r in range(4)], axis=0)
            dbias = jnp.sum(draw_e, axis=0, keepdims=True)
            dalog = jnp.sum(ddta_e * dt_e, axis=0, keepdims=True) * a_e
            dd = _segsum(jnp.sum(dyv * xv, axis=0, keepdims=True))
            row3 = _iota((3, 1), 0)
            dp_ref[0] += (jnp.where(row3 == 0, dbias, 0.0) + jnp.where(row3 == 1, dalog, 0.0)
                          + jnp.where(row3 == 2, dd, 0.0))
            ds = ds_prev
        ds_scr[...] = ds


    x, bm, cm, dtcs, dtrs, prs, pcs, st, ch = _ssd_specs(t, True)
    yblk = pl.BlockSpec((SSD_ROWS, 256), lambda g, c: (ch(c), g))
    nblk = pl.BlockSpec((SSD_ROWS, SSD_STATE), lambda g, c: (ch(c), g))
    return pl.pallas_call(
        body, name="ssd_bwd", grid=(SSD_GROUPS, t // SSD_ROWS),
        in_specs=[x, bm, cm, dtcs, dtrs, prs, pcs, pcs, st, yblk],
        out_specs=[yblk, nblk, nblk, dtrs, prs],
        out_shape=[jax.ShapeDtypeStruct((t, SSD_INNER), F32), jax.ShapeDtypeStruct((t, BC_WIDTH), F32),
                   jax.ShapeDtypeStruct((t, BC_WIDTH), F32), jax.ShapeDtypeStruct((SSD_GROUPS, 4, t), F32),
                   jax.ShapeDtypeStruct((SSD_GROUPS, 3, 256), F32)],
        scratch_shapes=[pltpu.VMEM((SSD_STATE, 256), F32)],
        compiler_params=_cp(("parallel", "arbitrary")))(xbc, xbc, xbc, dtc, dtr, prow, bcol, acol, states, dy)


GROUP_W = SSD_INNER // SSD_GROUPS


def _mix_specs(tb):
    row = pl.BlockSpec((tb, 2048), lambda i: (i, 0))
    zlo = pl.BlockSpec((tb, 1024), lambda i: (i, O_Z // 1024))
    zhi = pl.BlockSpec((tb, 1024), lambda i: (i, O_Z // 1024 + 1))
    vec = pl.BlockSpec((1, 2048), lambda i: (0, 0))
    return row, zlo, zhi, vec


def _mix_fwd(attn, y, proj, g_attn, g_ssd):
    t = attn.shape[0]
    tb = _rows(t, 256)

    def body(a_ref, y_ref, zlo_ref, zhi_ref, ga_ref, gs_ref, o_ref):
        av = a_ref[...]
        r = lax.rsqrt(jnp.mean(av * av, axis=-1, keepdims=True) + EPS)
        o_ref[:, :ATTN_WIDTH] = (av * r * ga_ref[...]).astype(BF16)
        for g in range(SSD_GROUPS):
            lo, hi = g * GROUP_W, (g + 1) * GROUP_W
            zref = zlo_ref if g < 4 else zhi_ref
            z = zref[:, lo % 1024:lo % 1024 + GROUP_W]
            yg = y_ref[:, lo:hi] * (z * _sigmoid(z))
            rg = lax.rsqrt(jnp.mean(yg * yg, axis=-1, keepdims=True) + EPS)
            o_ref[:, ATTN_WIDTH + lo:ATTN_WIDTH + hi] = (yg * rg * gs_ref[:, lo:hi]).astype(BF16)

    row, zlo, zhi, vec = _mix_specs(tb)
    return pl.pallas_call(
        body, name="mix_fwd", grid=(t // tb,), in_specs=[row, row, zlo, zhi, vec, vec],
        out_specs=pl.BlockSpec((tb, 4096), lambda i: (i, 0)), out_shape=jax.ShapeDtypeStruct((t, 4096), BF16),
        compiler_params=_cp(("parallel",)))(attn, y, proj, proj, g_attn, g_ssd)


def _mix_bwd(dmix, attn, y, proj, g_attn, g_ssd):
    t = attn.shape[0]
    tb = _rows(t, 256)

    def body(dm_ref, a_ref, y_ref, zlo_ref, zhi_ref, ga_ref, gs_ref, da_ref, dy_ref, dz_ref, dga_ref, dgs_ref):
        i = pl.program_id(0)
        av = a_ref[...]
        dn = dm_ref[:, :ATTN_WIDTH].astype(F32)
        r = lax.rsqrt(jnp.mean(av * av, axis=-1, keepdims=True) + EPS)
        u = dn * ga_ref[...]
        da_ref[...] = r * u - av * (r * r * r * jnp.mean(u * av, axis=-1, keepdims=True))
        dga = jnp.sum(dn * av * r, axis=0, keepdims=True)

        @pl.when(i == 0)
        def _():
            dga_ref[...] = dga

        @pl.when(i > 0)
        def _():
            dga_ref[...] += dga

        for g in range(SSD_GROUPS):
            lo, hi = g * GROUP_W, (g + 1) * GROUP_W
            zref = zlo_ref if g < 4 else zhi_ref
            z = zref[:, lo % 1024:lo % 1024 + GROUP_W]
            yv = y_ref[:, lo:hi]
            sg = _sigmoid(z)
            sz = z * sg
            yg = yv * sz
            rg = lax.rsqrt(jnp.mean(yg * yg, axis=-1, keepdims=True) + EPS)
            do = dm_ref[:, ATTN_WIDTH + lo:ATTN_WIDTH + hi].astype(F32)
            ug = do * gs_ref[:, lo:hi]
            dyg = rg * ug - yg * (rg * rg * rg * jnp.mean(ug * yg, axis=-1, keepdims=True))
            dy_ref[:, lo:hi] = dyg * sz
            dz_ref[:, lo:hi] = (dyg * yv * (sg * (1.0 + z * (1.0 - sg)))).astype(BF16)
            dgs = jnp.sum(do * yg * rg, axis=0, keepdims=True)

            @pl.when(i == 0)
            def _():
                dgs_ref[:, lo:hi] = dgs

            @pl.when(i > 0)
            def _():
                dgs_ref[:, lo:hi] += dgs

    row, zlo, zhi, vec = _mix_specs(tb)
    return pl.pallas_call(
        body, name="mix_bwd", grid=(t // tb,),
        in_specs=[pl.BlockSpec((tb, 4096), lambda i: (i, 0)), row, row, zlo, zhi, vec, vec],
        out_specs=[row, row, row, vec, vec],
        out_shape=[jax.ShapeDtypeStruct((t, 2048), F32), jax.ShapeDtypeStruct((t, 2048), F32),
                   jax.ShapeDtypeStruct((t, 2048), BF16), jax.ShapeDtypeStruct((1, 2048), F32),
                   jax.ShapeDtypeStruct((1, 2048), F32)],
        compiler_params=_cp(("arbitrary",)))(dmix, attn, y, proj, proj, g_attn, g_ssd)


def _adamw(w, g, m, v, name):
    r, c = w.shape
    tb = _rows(r, 256)
    c1 = 1.0 - ADAM_B1 ** ADAM_STEP
    c2 = 1.0 - ADAM_B2 ** ADAM_STEP

    def body(w_ref, g_ref, m_ref, v_ref, d_ref, m2_ref, v2_ref):
        gv = g_ref[...]
        m2 = ADAM_B1 * m_ref[...] + (1.0 - ADAM_B1) * gv
        v2 = ADAM_B2 * v_ref[...] + (1.0 - ADAM_B2) * (gv * gv)
        d_ref[...] = -ADAM_LR * ((m2 / c1) / (jnp.sqrt(v2 / c2) + ADAM_EPS) + ADAM_WD * w_ref[...])
        m2_ref[...] = m2
        v2_ref[...] = v2

    blk = pl.BlockSpec((tb, c), lambda i: (i, 0))
    shp = jax.ShapeDtypeStruct((r, c), F32)
    return pl.pallas_call(body, name=name, grid=(r // tb,), in_specs=[blk] * 4, out_specs=[blk] * 3,
                          out_shape=[shp] * 3, compiler_params=_cp(("parallel",)))(w, g, m, v)


def _adamw_halves(w, mine, theirs, m, v, pos, name, cols=False):
    r, c = w.shape
    h = r if cols else r // 2
    tb = _rows(h, 128)
    nh = h // tb
    c1 = 1.0 - ADAM_B1 ** ADAM_STEP
    c2 = 1.0 - ADAM_B2 ** ADAM_STEP

    def body(pos_ref, w_ref, a_ref, b_ref, m_ref, v_ref, g_ref, d_ref, m2_ref, v2_ref):
        which = pl.program_id(1) if cols else pl.program_id(0) // nh
        gv = jnp.where(which == pos_ref[0], a_ref[...], b_ref[...])
        m2 = ADAM_B1 * m_ref[...] + (1.0 - ADAM_B1) * gv
        v2 = ADAM_B2 * v_ref[...] + (1.0 - ADAM_B2) * (gv * gv)
        g_ref[...] = gv
        d_ref[...] = -ADAM_LR * ((m2 / c1) / (jnp.sqrt(v2 / c2) + ADAM_EPS) + ADAM_WD * w_ref[...])
        m2_ref[...] = m2
        v2_ref[...] = v2

    if cols:
        full = pl.BlockSpec((tb, c // 2), lambda i, j, pref: (i, j))
        mine_spec = theirs_spec = pl.BlockSpec((tb, c // 2), lambda i, j, pref: (i, 0))
        grid = (nh, 2)
    else:
        full = pl.BlockSpec((tb, c), lambda i, pref: (i, 0))
        mine_spec = pl.BlockSpec((tb, c), lambda i, pref: (jnp.where(i // nh == pref[0], i % nh,
                                                                     jnp.where(pref[0] == 0, nh - 1, 0)), 0))
        theirs_spec = pl.BlockSpec((tb, c), lambda i, pref: (jnp.where(i // nh != pref[0], i % nh,
                                                                       jnp.where(pref[0] == 0, 0, nh - 1)), 0))
        grid = (r // tb,)
    shp = jax.ShapeDtypeStruct((r, c), F32)
    grid_spec = pltpu.PrefetchScalarGridSpec(num_scalar_prefetch=1, grid=grid,
                                             in_specs=[full, mine_spec, theirs_spec, full, full],
                                             out_specs=[full] * 4)
    return pl.pallas_call(body, name=name, grid_spec=grid_spec, out_shape=[shp] * 4,
                          compiler_params=_cp(("parallel",) * len(grid)))(pos, w, mine, theirs, m, v)


def _sum_own_half(g4, recv, pos, name, cols=False):
    _, r, c = g4.shape
    h, c = (r, c // 2) if cols else (r // 2, c)
    tb = _rows(h, 128)
    nh = h // tb

    def slot(j, pref):
        return (pref[1] + 1 + j) % N_CHIPS

    if cols:
        own = lambda j, i, pref: (slot(j, pref), i, pref[0])
    else:
        own = lambda j, i, pref: (slot(j, pref), pref[0] * nh + i, 0)
    same = lambda j, i, pref: (slot(j, pref), i, 0)

    def body(pos_ref, a_ref, b_ref, o_ref):
        o_ref[...] = (a_ref[...] + b_ref[...]).astype(BF16)

    grid_spec = pltpu.PrefetchScalarGridSpec(
        num_scalar_prefetch=1, grid=(N_CHIPS - 1, nh),
        in_specs=[pl.BlockSpec((1, tb, c), own), pl.BlockSpec((1, tb, c), same)],
        out_specs=pl.BlockSpec((1, tb, c), same))
    return pl.pallas_call(body, name=name, grid_spec=grid_spec,
                          out_shape=jax.ShapeDtypeStruct((N_CHIPS, h, c), BF16),
                          compiler_params=_cp(("parallel", "parallel")))(pos, g4, recv)


def _sum_chips(g4, recv, parts, pos, name, cols=False):
    _, r, c = g4.shape
    h, c = (r, c // 2) if cols else (r // 2, c)
    tb = _rows(h, 128)
    nh = h // tb
    own = (lambda i, pref: (pref[1], i, pref[0])) if cols else (lambda i, pref: (pref[1], pref[0] * nh + i, 0))

    def body(pos_ref, a_ref, b_ref, p_ref, o_ref):
        own = a_ref[0] + b_ref[0]
        o_ref[...] = ((own + p_ref[0].astype(F32)) + p_ref[1].astype(F32)) + p_ref[2].astype(F32)

    grid_spec = pltpu.PrefetchScalarGridSpec(
        num_scalar_prefetch=1, grid=(nh,),
        in_specs=[pl.BlockSpec((1, tb, c), own),
                  pl.BlockSpec((1, tb, c), lambda i, pref: (pref[1], i, 0)),
                  pl.BlockSpec((3, tb, c), lambda i, pref: (0, i, 0))],
        out_specs=pl.BlockSpec((tb, c), lambda i, pref: (i, 0)))
    return pl.pallas_call(body, name=name, grid_spec=grid_spec, out_shape=jax.ShapeDtypeStruct((h, c), F32),
                          compiler_params=_cp(("parallel",)))(pos, g4, recv, parts)


def _me():
    return lax.axis_index("x"), lax.axis_index("y"), lax.axis_index("c")


def _flip(v, bit):
    return (1 - v) if bit else v


CHIP_FLIPS = [(1, 0), (0, 1), (1, 1)]


def _allgather_weights(shards, after, cols=False):
    n = len(shards)

    def body(*refs):
        ins, outs, token = refs[:n], refs[n + 1:2 * n + 1], refs[2 * n + 1]
        send_sems, recv_sems = refs[2 * n + 2:]
        x, y, c = _me()
        chip = 2 * x + y
        sib = (x, y, 1 - c)

        def remote(src, dst, k, to):
            return pltpu.make_async_remote_copy(src_ref=src, dst_ref=dst, send_sem=send_sems.at[k],
                                                recv_sem=recv_sems.at[k], device_id=to, device_id_type=MESH)

        def half(ref, which):
            if cols:
                h = ref.shape[1] // 2
                return ref.at[:, pl.ds(which * h, h)]
            h = ref.shape[0] // 2
            return ref.at[pl.ds(which * h, h)]

        sends = []
        for t in range(n):
            for k, (fx, fy) in enumerate(CHIP_FLIPS):
                cp = remote(half(ins[t], c), half(outs[t].at[chip], c), 6 * t + k, (_flip(x, fx), _flip(y, fy), c))
                cp.start()
                sends.append(cp)
        for t in range(n):
            for k, (fx, fy) in enumerate(CHIP_FLIPS):
                landed = half(outs[t].at[2 * _flip(x, fx) + _flip(y, fy)], c)
                remote(landed, landed, 6 * t + k, (x, y, c)).wait_recv()
                fw = remote(landed, landed, 6 * t + 3 + k, sib)
                fw.start()
                sends.append(fw)
        for t in range(n):
            for k, (fx, fy) in enumerate(CHIP_FLIPS):
                got = half(outs[t].at[2 * _flip(x, fx) + _flip(y, fy)], 1 - c)
                remote(got, got, 6 * t + 3 + k, (x, y, c)).wait_recv()
        for cp in sends:
            cp.wait_send()
        token[...] = jnp.zeros_like(token)

    outs = pl.pallas_call(
        body, name="allgather_weights", in_specs=[HBM_SPEC] * n + [pl.BlockSpec(memory_space=pl.ANY)],
        out_specs=[HBM_SPEC] * n + [pl.BlockSpec(memory_space=pltpu.VMEM)],
        out_shape=[jax.ShapeDtypeStruct((N_CHIPS,) + s.shape, s.dtype) for s in shards] + [TOKEN],
        scratch_shapes=[pltpu.SemaphoreType.DMA((6 * n,)), pltpu.SemaphoreType.DMA((6 * n,))],
        compiler_params=pltpu.CompilerParams(has_side_effects=True))(*shards, after)
    return list(outs[:n]), outs[n]


def _share_halves(ghs, name):
    n = len(ghs)

    def body(*refs):
        ins, outs = refs[:n], refs[n:2 * n]
        send_sems, recv_sems = refs[2 * n:]
        x, y, c = _me()
        cps = []
        for t in range(n):
            cp = pltpu.make_async_remote_copy(
                src_ref=ins[t], dst_ref=outs[t], send_sem=send_sems.at[t], recv_sem=recv_sems.at[t],
                device_id=(x, y, 1 - c), device_id_type=MESH)
            cp.start()
            cps.append(cp)
        for cp in cps:
            cp.wait()

    return pl.pallas_call(
        body, name=name, in_specs=[HBM_SPEC] * n, out_specs=[HBM_SPEC] * n,
        out_shape=[jax.ShapeDtypeStruct(g.shape, g.dtype) for g in ghs],
        scratch_shapes=[pltpu.SemaphoreType.DMA((n,)), pltpu.SemaphoreType.DMA((n,))],
        compiler_params=pltpu.CompilerParams(has_side_effects=True))(*ghs)


SEM_SPEC = pl.BlockSpec(memory_space=pltpu.SEMAPHORE)
ANY_SPEC = pl.BlockSpec(memory_space=pl.ANY)
DATAFLOW = pltpu.SideEffectType.DATAFLOW_SIDE_EFFECTING


def _in_hbm(a):
    return pltpu.with_memory_space_constraint(a, pltpu.HBM)


def _push_start(srcs, land_shapes, route, peers, name):
    n, npeer = len(srcs), len(peers)
    lands = [lax.empty(shp, s.dtype) for shp, s in zip(land_shapes, srcs)]

    def body(*refs):
        ins, lnd = refs[:n], refs[n:2 * n]
        send_sems, recv_sems = refs[2 * n], refs[2 * n + 1]
        token = refs[-1]
        x, y, c = _me()
        for t in range(n):
            for k, (fx, fy, fc) in enumerate(peers):
                src, dst = route(ins[t], lnd[t], k, x, y, c)
                pltpu.make_async_remote_copy(
                    src_ref=src, dst_ref=dst, send_sem=send_sems.at[npeer * t + k],
                    recv_sem=recv_sems.at[npeer * t + k],
                    device_id=(_flip(x, fx), _flip(y, fy), _flip(c, fc)), device_id_type=MESH).start()
        token[...] = jnp.zeros_like(token)

    bufs = [_in_hbm(a) for a in list(srcs) + lands]
    outs = pl.pallas_call(
        body, name=name,
        out_shape=(pltpu.SemaphoreType.DMA((npeer * n,)), pltpu.SemaphoreType.DMA((npeer * n,)),
                   *[pltpu.HBM(b.shape, b.dtype) for b in bufs], TOKEN),
        in_specs=[HBM_SPEC] * (2 * n),
        out_specs=(SEM_SPEC, SEM_SPEC, *[HBM_SPEC] * (2 * n), pl.BlockSpec(memory_space=pltpu.VMEM)),
        input_output_aliases={i: 2 + i for i in range(2 * n)},
        compiler_params=pltpu.CompilerParams(has_side_effects=DATAFLOW))(*bufs)
    return outs[0], outs[1], list(outs[2:2 + n]), list(outs[2 + n:2 + 2 * n]), outs[-1]


def _push_wait(send_sems, recv_sems, srcs, lands, after, route, peers, name):
    n, npeer = len(srcs), len(peers)

    def body(*refs):
        ins, lnd = refs[:n], refs[n:2 * n]
        ssem, rsem = refs[2 * n], refs[2 * n + 1]
        x, y, c = _me()
        for t in range(n):
            for k, (fx, fy, fc) in enumerate(peers):
                src, dst = route(ins[t], lnd[t], k, x, y, c)
                cp = pltpu.make_async_remote_copy(
                    src_ref=src, dst_ref=dst, send_sem=ssem.at[npeer * t + k], recv_sem=rsem.at[npeer * t + k],
                    device_id=(_flip(x, fx), _flip(y, fy), _flip(c, fc)), device_id_type=MESH)
                cp.wait_send()
                cp.wait_recv()

    bufs = list(srcs) + list(lands)
    outs = pl.pallas_call(
        body, name=name, out_shape=tuple(pltpu.HBM(b.shape, b.dtype) for b in bufs),
        in_specs=[HBM_SPEC] * (2 * n) + [SEM_SPEC, SEM_SPEC, ANY_SPEC], out_specs=tuple([HBM_SPEC] * (2 * n)),
        input_output_aliases={i: i for i in range(2 * n)},
        compiler_params=pltpu.CompilerParams(has_side_effects=DATAFLOW))(*bufs, send_sems, recv_sems, after)
    return list(outs[:n]), list(outs[n:])


OTHER_CHIPS = [(fx, fy, 0) for fx, fy in CHIP_FLIPS]
SIBLING = [(0, 0, 1)]


def _route_gather(src, land, k, x, y, c):
    return src, land.at[2 * x + y]


def _route_gather_wait(src, land, k, x, y, c):
    fx, fy = CHIP_FLIPS[k]
    return src, land.at[2 * _flip(x, fx) + _flip(y, fy)]


def _route_scatter(src, land, k, x, y, c):
    fx, fy = CHIP_FLIPS[k]
    return src.at[2 * _flip(x, fx) + _flip(y, fy)], land.at[k]


def _route_exchange(src, land, k, x, y, c):
    h = land.shape[1]
    return src.at[:, pl.ds((1 - c) * h, h)], land


def _route_exchange_cols(src, land, k, x, y, c):
    h = land.shape[2]
    return src.at[:, :, pl.ds((1 - c) * h, h)], land


def _allreduce_small(v):
    r = v.shape[0]

    def body(v_ref, o_ref, buf, send_sems, recv_sems):
        x, y, c = _me()
        me = 4 * x + 2 * y + c
        buf[0] = v_ref[...]
        cps = []
        for k in range(1, 8):
            kx, ky, kc = (k >> 2) & 1, (k >> 1) & 1, k & 1
            cp = pltpu.make_async_remote_copy(
                src_ref=v_ref, dst_ref=buf.at[k], send_sem=send_sems.at[k - 1], recv_sem=recv_sems.at[k - 1],
                device_id=(_flip(x, kx), _flip(y, ky), _flip(c, kc)), device_id_type=MESH)
            cp.start()
            cps.append(cp)
        for cp in cps:
            cp.wait()
        acc = buf[me]
        for d in range(1, 8):
            acc = acc + buf[jnp.bitwise_xor(me, d)]
        o_ref[...] = acc

    vm = pl.BlockSpec(memory_space=pltpu.VMEM)
    return pl.pallas_call(
        body, name="allreduce_small", in_specs=[vm], out_specs=vm, out_shape=jax.ShapeDtypeStruct(v.shape, F32),
        scratch_shapes=[pltpu.VMEM((8, r, LANES), F32), pltpu.SemaphoreType.DMA((7,)),
                        pltpu.SemaphoreType.DMA((7,))],
        compiler_params=pltpu.CompilerParams(has_side_effects=True, vmem_limit_bytes=VMEM_LIMIT))(v)


def _grad_exchange_start(g4, tag, cols=False):
    land = (N_CHIPS, g4.shape[1], g4.shape[2] // 2) if cols else (N_CHIPS, g4.shape[1] // 2, g4.shape[2])
    route = _route_exchange_cols if cols else _route_exchange
    send_sems, recv_sems, srcs, lands, token = _push_start(
        [g4], [land], route, SIBLING, name="grad_exchange_start_" + tag)
    return (send_sems, recv_sems, srcs, lands, tag, cols), token


def _grad_scatter_start(state, pos, after):
    send_sems, recv_sems, srcs, lands, tag, cols = state
    route = _route_exchange_cols if cols else _route_exchange
    (g4,), (recv,) = _push_wait(send_sems, recv_sems, srcs, lands, after, route, SIBLING,
                                name="grad_exchange_wait_" + tag)
    return _grad_pair_scatter(g4, recv, pos, tag, cols)


def _grad_pair_scatter(g4, recv, pos, tag, cols=False):
    p16 = _sum_own_half(g4, recv, pos, name="grad_sum_pair_" + tag, cols=cols)
    send_sems, recv_sems, srcs, lands, token = _push_start(
        [p16], [(3,) + p16.shape[1:]], _route_scatter, OTHER_CHIPS, name="grad_scatter_start_" + tag)
    return (g4, recv, send_sems, recv_sems, srcs, lands, tag, cols), token


def _grad_reduce_finish(state, pos, after):
    g4, recv, send_sems, recv_sems, srcs, lands, tag, cols = state
    parts = _push_wait(send_sems, recv_sems, srcs, lands, after, _route_scatter, OTHER_CHIPS,
                       name="grad_scatter_wait_" + tag)[1][0]
    mine = _sum_chips(g4, recv, parts, pos, name="grad_sum_chips_" + tag, cols=cols)
    return mine, _share_halves([mine], name="grad_share_halves_" + tag)[0]


def _local_step(x, tgt, p, w_in_t, w_in_dt, hooks):
    t = x.shape[0]
    tables = _rope_tables(t)
    sinks = p['sinks'].reshape(N_Q_HEADS)

    def told(name, value):
        return tuple(hooks.grad_ready(name, value))

    xn = _rmsnorm_fwd(x, p['norm_mix'], "norm_mix_fwd", deps=hooks.first_deps)
    proj = _matmul(xn, w_in_t, mode='nt', name="in_proj", n_limit=MAIN_WIDTH)
    dt_raw = _matmul(xn, w_in_dt, mode='nt', name="in_proj_dt")[:, :SSD_HEADS]
    attn = _attn_fwd(proj, sinks, tables)
    conv_b = p['ssd_conv_b']
    xbc = _conv_fwd(proj, p['ssd_conv_w'], conv_b, col0=O_XBC, width=CONV_CH, act=True, name="ssd_conv_fwd")
    sp = _ssd_params(dt_raw, p['dt_bias'].reshape(-1), p['a_log'].reshape(-1), p['ssd_d'].reshape(-1))
    y, states = _ssd_fwd(xbc, sp)
    mix = _mix_fwd(attn, y, proj, p['attn_out_norm'], p['ssd_norm'])
    w_out = hooks.weight('w_out', mix)
    h1 = _matmul(mix, w_out, mode='nn', name="out_proj", add=x)
    hn = _rmsnorm_fwd(h1, p['norm_ffn'], "norm_ffn_fwd")
    w_up = hooks.weight('w_up', hn)
    u0 = _matmul(hn, w_up, mode='nn', name="ffn_up", b_owner=True, tn=1408)
    a, u = _ffn_act_fwd(u0, p['ffn_conv_w'], p['ffn_conv_b'])
    w_down = hooks.weight('w_down', a)
    h2 = _matmul(a, w_down, mode='nn', name="ffn_down", add=h1, tk=2816)
    loss, dh2, dh2_16, g_norm_final = _final_loss(h2, p['norm_final'].reshape(1, D_MODEL), tgt)

    g = {}
    da = _matmul(dh2_16, w_down, mode='nt', name="ffn_down_dx", out_dtype=BF16, tn=1408)
    g['w_down'] = _matmul(a, dh2_16, mode='tn', name="ffn_down_dw", tm=1408)
    dep = told('w_down', g['w_down'])
    du0, dcw, dcb = _ffn_act_bwd(u0, u, p['ffn_conv_w'], da)
    g['ffn_conv_w'] = dcw.transpose(1, 0, 2).reshape(FFN_CONV, 2 * D_FF)
    g['ffn_conv_b'] = dcb.transpose(1, 0, 2).reshape(1, 2 * D_FF)
    g['w_up'] = _matmul(hn, du0, mode='tn', name="ffn_up_dw", deps=dep, b_halves=True, owner_major=True,
                        tn=1408)
    dep = told('w_up', g['w_up'])
    dhn = _matmul(du0, w_up, mode='nt', name="ffn_up_dx", out_dtype=BF16, deps=dep, a_halves=True,
                  b_owner=True, tk=2816)
    dh1, dh1_16, g['norm_ffn'] = _rmsnorm_bwd(h1, p['norm_ffn'], dhn, dh2, "norm_ffn_bwd")

    g['w_out'] = _matmul(mix, dh1_16, mode='tn', name="out_proj_dw")
    dep = told('w_out', g['w_out'])
    dmix = _matmul(dh1_16, w_out, mode='nt', name="out_proj_dx", out_dtype=BF16, deps=dep)
    dattn, dy, dz, g['attn_out_norm'], g['ssd_norm'] = _mix_bwd(dmix, attn, y, proj, p['attn_out_norm'],
                                                                p['ssd_norm'])
    dq, dk, dv, dsink = _attn_bwd(proj, sinks, tables, dattn)
    g['sinks'] = dsink[:, :, 0].reshape(1, N_Q_HEADS)
    dxs, dbm, dcm, ddt8, dpar = _ssd_bwd(xbc, sp, states, dy)
    dpar = dpar[:, :, ::SSD_HEAD_DIM]
    g['dt_bias'] = dpar[:, 0, :].reshape(1, SSD_HEADS)
    g['a_log'] = dpar[:, 1, :].reshape(1, SSD_HEADS)
    g['ssd_d'] = dpar[:, 2, :].reshape(1, SSD_HEADS)
    dxbc_act = jnp.concatenate([dxs, dbm, dcm], axis=1)
    dconv = _conv_silu_dact(proj, p['ssd_conv_w'], conv_b, dxbc_act, col0=O_XBC, width=CONV_CH,
                            name="ssd_conv_dact")
    dxbc, g['ssd_conv_w'], g['ssd_conv_b'] = _conv_bwd(proj, p['ssd_conv_w'], dconv, col0=O_XBC, width=CONV_CH,
                                                       name="ssd_conv_bwd")
    dproj = jnp.concatenate([dq, dk, dv, dz, dxbc], axis=1)
    ddt = ddt8.transpose(2, 0, 1).reshape(t, SSD_HEADS)
    ddt_pad = jnp.pad(ddt, ((0, 0), (0, LANES - SSD_HEADS))).astype(BF16)
    g['w_in'] = (_matmul(dproj, xn, mode='tn', name="in_proj_dw", m_rows=IN_PROJ_WIDTH),
                 _matmul(ddt_pad, xn, mode='tn', name="in_proj_dt_dw"))
    dep = told('w_in', g['w_in'])
    dxn_dt = _matmul(ddt_pad, w_in_dt, mode='nn', name="in_proj_dt_dx", deps=dep)
    dxn = _matmul(dproj, w_in_t, mode='nn', name="in_proj_dx", out_dtype=BF16, add=dxn_dt, k_limit=MAIN_WIDTH,
                  tk=2304)
    dx, _, g['norm_mix'] = _rmsnorm_bwd(x, p['norm_mix'], dxn, dh1, "norm_mix_bwd")
    g['norm_final'] = g_norm_final
    return loss, dx, g


def _pack(arrs):
    flat = jnp.concatenate([a.reshape(-1) for a in arrs])
    n = flat.shape[0]
    rows = -(-n // LANES)
    rows = -(-rows // 8) * 8
    return jnp.pad(flat, (0, rows * LANES - n)).reshape(rows, LANES)


def _unpack(packed, shapes):
    flat = packed.reshape(-1)
    out, off = [], 0
    for s in shapes:
        n = 1
        for d in s:
            n *= d
        out.append(flat[off:off + n].reshape(s))
        off += n
    return out


class _StepHooks:
    def __init__(self, first_deps, weight, grad_ready):
        self.first_deps = first_deps
        self.weight = weight
        self.grad_ready = grad_ready


def kernel(x, norm_mix, w_in, sinks, attn_out_norm, ssd_conv_w, ssd_conv_b, dt_bias, a_log, ssd_d, ssd_norm, w_out, norm_ffn, w_up, ffn_conv_w, ffn_conv_b, w_down, norm_final, loss_target, m_norm_mix, m_w_in, m_sinks, m_attn_out_norm, m_ssd_conv_w, m_ssd_conv_b, m_dt_bias, m_a_log, m_ssd_d, m_ssd_norm, m_w_out, m_norm_ffn, m_w_up, m_ffn_conv_w, m_ffn_conv_b, m_w_down, m_norm_final, v_norm_mix, v_w_in, v_sinks, v_attn_out_norm, v_ssd_conv_w, v_ssd_conv_b, v_dt_bias, v_a_log, v_ssd_d, v_ssd_norm, v_w_out, v_norm_ffn, v_w_up, v_ffn_conv_w, v_ffn_conv_b, v_w_down, v_norm_final):
    args = dict(locals())
    w = {n: args[n] for n in WEIGHTS}
    m = {n: args['m_' + n] for n in WEIGHTS}
    v = {n: args['v_' + n] for n in WEIGHTS}
    xi, yi, ci = _me()
    chip = 2 * xi + yi
    pos = jnp.stack([ci, chip]).astype(jnp.int32)

    def place(shard, full_cols):
        z = jnp.zeros((shard.shape[0], full_cols), F32)
        return lax.dynamic_update_slice(z, shard * 0.5, (0, chip * shard.shape[1]))

    conv_pack = _pack([place(ssd_conv_w[0], CONV_CH), place(ffn_conv_w[0], 2 * D_FF)])
    conv_full = _allreduce_small(conv_pack)
    ssd_conv_w_full, ffn_conv_w_full = _unpack(conv_full, [(SSD_CONV, CONV_CH), (FFN_CONV, 2 * D_FF)])

    w_in_t, m_in_t, v_in_t = (jnp.transpose(a[0]) for a in (w_in, m_w_in, v_w_in))
    in_shard = w_in_t.astype(BF16)
    (gathered,), order = _allgather_weights([in_shard], conv_full, cols=True)
    full_in_t = lax.dynamic_update_slice(gathered, in_shard[None], (chip, 0, 0)).reshape(IN_PROJ_WIDTH, D_MODEL)
    w_in_dt = jnp.pad(full_in_t[MAIN_WIDTH:], ((0, LANES - SSD_HEADS), (0, 0)))
    gathers = {}
    order = order[:1, :1]
    for n, shard in (('w_out', w_out[0]), ('w_up', w_up[0]), ('w_down', w_down[0])):
        shard = (shard + order).astype(BF16)
        gathers[n] = _push_start([shard], [(N_CHIPS,) + shard.shape], _route_gather, OTHER_CHIPS,
                                 name="gather_start_" + n)
        order = gathers[n][4][:1, :1]
    first_deps = [gathers['w_down'][4]]

    def weight(name, after):
        send_sems, recv_sems, srcs, lands, _ = gathers[name]
        (own,), (got,) = _push_wait(send_sems, recv_sems, srcs, lands, after, _route_gather_wait, OTHER_CHIPS,
                                    name="gather_wait_" + name)
        whole = lax.dynamic_update_slice(got, own[None], (chip, 0, 0))
        return whole if name == 'w_up' else whole.reshape(-1, D_MODEL)

    reductions, exchanging = {}, {}

    def flush(after):
        tokens = []
        for prev in list(exchanging):
            reductions[prev], token = _grad_scatter_start(exchanging.pop(prev), pos, after)
            tokens.append(token)
        return tokens

    def grad_ready(name, value):
        if name == 'w_in':
            main, dtp = value
            value = lax.dynamic_update_slice(main, dtp[:SSD_HEADS], (MAIN_WIDTH, 0))
        g4 = value if value.ndim == 3 else value.reshape(N_CHIPS, -1, value.shape[1])
        tokens = flush(g4)
        exchanging[name], token = _grad_exchange_start(g4, name, cols=(name == 'w_in'))
        return tokens + [token]

    small = {
        'norm_mix': norm_mix, 'sinks': sinks, 'attn_out_norm': attn_out_norm, 'ssd_conv_w': ssd_conv_w_full,
        'ssd_conv_b': ssd_conv_b, 'dt_bias': dt_bias, 'a_log': a_log, 'ssd_d': ssd_d, 'ssd_norm': ssd_norm,
        'norm_ffn': norm_ffn, 'ffn_conv_w': ffn_conv_w_full, 'ffn_conv_b': ffn_conv_b, 'norm_final': norm_final,
    }
    loss, dx, g = _local_step(x[0], loss_target[0], small, full_in_t, w_in_dt,
                              _StepHooks(tuple(first_deps), weight, grad_ready))

    small_names = [n for n in WEIGHTS if n not in BIG]
    small_g = [loss[:, :1]] + [g[n] for n in small_names]
    small_shapes = [(1, 1)] + [tuple(a.shape) for a in small_g[1:]]
    reduced = _allreduce_small(_pack(small_g))
    started = flush(reduced)[-1]
    red = _unpack(reduced, small_shapes)
    loss_out = red[0].reshape(())
    gsm = dict(zip(small_names, red[1:]))
    gsm['ssd_conv_w'] = lax.dynamic_slice(gsm['ssd_conv_w'], (0, chip * ssd_conv_w.shape[2]),
                                          (SSD_CONV, ssd_conv_w.shape[2]))
    gsm['ffn_conv_w'] = lax.dynamic_slice(gsm['ffn_conv_w'], (0, chip * ffn_conv_w.shape[2]),
                                          (FFN_CONV, ffn_conv_w.shape[2]))

    grads, deltas, new_m, new_v = {}, {}, {}, {}
    after = started
    for n in ('w_down', 'w_up', 'w_out', 'w_in'):
        mine, theirs = _grad_reduce_finish(reductions[n], pos, after)
        if n == 'w_in':
            outs = _adamw_halves(w_in_t, mine, theirs, m_in_t, v_in_t, pos, name="adamw_" + n, cols=True)
            outs = [jnp.transpose(o) for o in outs]
        else:
            outs = _adamw_halves(w[n][0], mine, theirs, m[n][0], v[n][0], pos, name="adamw_" + n)
        after = outs[1]
        grads[n], deltas[n], new_m[n], new_v[n] = [o[None] for o in outs]
    shapes = [tuple(w[n].shape) for n in small_names]
    gp = _pack([gsm[n] for n in small_names])
    d, m2, v2 = _adamw(_pack([w[n] for n in small_names]), gp, _pack([m[n] for n in small_names]),
                       _pack([v[n] for n in small_names]), name="adamw_small")
    for n, gg, dd, mm, vv in zip(small_names, _unpack(gp, shapes), _unpack(d, shapes), _unpack(m2, shapes),
                                 _unpack(v2, shapes)):
        grads[n], deltas[n], new_m[n], new_v[n] = gg, dd, mm, vv

    return (loss_out, dx[None], *[grads[n] for n in WEIGHTS], *[deltas[n] for n in WEIGHTS],
            *[new_m[n] for n in WEIGHTS], *[new_v[n] for n in WEIGHTS])
```

```python
import functools

import jax
import jax.numpy as jnp
from jax import lax
from jax.experimental import pallas as pl
from jax.experimental.pallas import tpu as pltpu

F32 = jnp.float32
BF16 = jnp.bfloat16

D_MODEL = 2048
N_Q_HEADS = 32
N_KV_HEADS = 8
HEAD_DIM = 64
WINDOW = 128
ATTN_BLOCK = 128
ROT_DIM = 16
ROPE_THETA = 500000.0
SSD_HEADS = 32
SSD_HEAD_DIM = 64
SSD_INNER = 2048
SSD_GROUPS = 8
SSD_STATE = 128
SSD_CONV = 4
SSD_CHUNK = 128
ATTN_WIDTH = 2048
KV_WIDTH = 512
BC_WIDTH = 1024
CONV_CH = 4096
IN_PROJ_WIDTH = 9248
MAIN_WIDTH = 9216
D_FF = 5632
FFN_CONV = 3
EPS = 1e-6
O_Q, O_K, O_V, O_Z, O_XBC, O_DT = 0, 2048, 2560, 3072, 5120, 9216

ADAM_LR = 0.001
ADAM_B1 = 0.9
ADAM_B2 = 0.999
ADAM_EPS = 1e-08
ADAM_WD = 0.01
ADAM_STEP = 10

N_CHIPS = 4
NEG = -1e30
LANES = 128
VMEM_LIMIT = 48 * 1024 * 1024
MESH = pl.DeviceIdType.MESH
HBM_SPEC = pl.BlockSpec(memory_space=pltpu.HBM)
TOKEN = jax.ShapeDtypeStruct((8, LANES), F32)

WEIGHTS = ['norm_mix', 'w_in', 'sinks', 'attn_out_norm', 'ssd_conv_w', 'ssd_conv_b', 'dt_bias', 'a_log', 'ssd_d',
           'ssd_norm', 'w_out', 'norm_ffn', 'w_up', 'ffn_conv_w', 'ffn_conv_b', 'w_down', 'norm_final']
BIG = ['w_in', 'w_out', 'w_up', 'w_down']


def _cp(sem=None, vmem=VMEM_LIMIT):
    kw = {'vmem_limit_bytes': vmem}
    if sem is not None:
        kw['dimension_semantics'] = sem
    return pltpu.CompilerParams(**kw)


def _tile(n, pref):
    if n <= pref:
        return n
    t = (pref // LANES) * LANES
    while t > LANES and n % t:
        t -= LANES
    assert n % t == 0, (n, pref)
    return t


def _rows(n, pref):
    t = min(n, pref)
    while n % t:
        t -= 8
    if 4 * t < pref:
        t = pref
        while n % t:
            t += 8
    return t


def _iota(shape, dim):
    return lax.broadcasted_iota(jnp.int32, shape, dim)


def _dot(a, b, mode='nn'):
    dn = {'nn': (((1,), (0,)), ((), ())), 'nt': (((1,), (1,)), ((), ())), 'tn': (((0,), (0,)), ((), ()))}[mode]
    return lax.dot_general(a.astype(BF16), b.astype(BF16), dn, preferred_element_type=F32)


def _dot_exact(a, b):
    return lax.dot_general(a, b, (((1,), (0,)), ((), ())), precision=lax.Precision.HIGHEST,
                           preferred_element_type=F32)


def _sigmoid(x):
    return 1.0 / (1.0 + jnp.exp(-x))


def _softplus(x):
    return jnp.maximum(x, 0.0) + jnp.log(1.0 + jnp.exp(-jnp.abs(x)))


def _matmul(a, b, *, mode, name, out_dtype=F32, add=None, deps=(), tm=1024, tn=1024, tk=2048,
            a_halves=False, b_halves=False, b_owner=False, owner_major=False, n_limit=None, k_limit=None,
            m_rows=None):
    ash, bsh = (a.shape[1:] if a_halves else a.shape), (b.shape[1:] if (b_halves or b_owner) else b.shape)
    if mode == 'nn':
        (m, k), (k2, n) = ash, bsh
    elif mode == 'nt':
        (m, k), (n, k2) = ash, bsh
    else:
        (k, m), (k2, n) = ash, bsh
    if n_limit is not None:
        assert mode == 'nt' and n_limit <= n
        n = n_limit
    if k_limit is not None:
        assert mode == 'nn' and k_limit <= k2
        k2 = k_limit
    if a_halves:
        assert mode == 'nt'
        k = 2 * k
    if b_halves:
        assert mode == 'tn'
        n = 2 * n
    if b_owner:
        assert mode in ('nn', 'nt')
        if mode == 'nn':
            n = 4 * n
        else:
            k2 = 4 * k2
    assert k == k2, (a.shape, b.shape, mode)
    tm = _tile(m, tm)
    tn = _tile(n // 4 if (owner_major or (b_owner and mode == 'nn')) else (n // 2 if b_halves else n), tn)
    tk = _tile(k // 4 if (b_owner and mode == 'nt') else (k // 2 if a_halves else k), tk)
    nk = k // tk
    has_add = add is not None
    assert not (has_add and owner_major)

    def body(*refs):
        a_ref, b_ref = refs[:2]
        add_ref = refs[2] if has_add else None

        def finish(r, o_ref):
            if has_add:
                r = r + add_ref[...].astype(F32)
            o_ref[...] = r.astype(out_dtype)

        if nk == 1:
            finish(_dot(a_ref[...], b_ref[...], mode), refs[-1])
            return
        o_ref, acc = refs[-2:]
        kk = pl.program_id(2)

        @pl.when(kk == 0)
        def _():
            acc[...] = _dot(a_ref[...], b_ref[...], mode)

        @pl.when((kk > 0) & (kk < nk - 1))
        def _():
            acc[...] += _dot(a_ref[...], b_ref[...], mode)

        @pl.when(kk == nk - 1)
        def _():
            finish(acc[...] + _dot(a_ref[...], b_ref[...], mode), o_ref)

    if mode == 'tn':
        a_spec = pl.BlockSpec((tk, tm), lambda i, j, kk: (kk, i))
    elif a_halves:
        nkh = nk // 2
        a_spec = pl.BlockSpec((None, tm, tk), lambda i, j, kk: (kk // nkh, i, kk % nkh))
    else:
        a_spec = pl.BlockSpec((tm, tk), lambda i, j, kk: (i, kk))
    if mode == 'nt' and b_owner:
        nkq = nk // 4
        b_spec = pl.BlockSpec((None, tn, tk), lambda i, j, kk: (kk // nkq, j, kk % nkq))
    elif mode == 'nt':
        b_spec = pl.BlockSpec((tn, tk), lambda i, j, kk: (j, kk))
    elif b_owner:
        njq = (n // 4) // tn
        b_spec = pl.BlockSpec((None, tk, tn), lambda i, j, kk: (j // njq, kk, j % njq))
    elif b_halves:
        njh = (n // 2) // tn
        b_spec = pl.BlockSpec((None, tk, tn), lambda i, j, kk: (j // njh, kk, j % njh))
    else:
        b_spec = pl.BlockSpec((tk, tn), lambda i, j, kk: (kk, j))
    if owner_major:
        njo = (n // 4) // tn
        o_spec = pl.BlockSpec((None, tm, tn), lambda i, j, kk: (j // njo, i, j % njo))
        out_shape = jax.ShapeDtypeStruct((N_CHIPS, m, n // 4), out_dtype)
    else:
        o_spec = pl.BlockSpec((tm, tn), lambda i, j, kk: (i, j))
        out_shape = jax.ShapeDtypeStruct((m if m_rows is None else m_rows, n), out_dtype)
    dep_spec = pl.BlockSpec((8, LANES), lambda i, j, kk: (0, 0))
    in_specs = [a_spec, b_spec] + ([pl.BlockSpec((tm, tn), lambda i, j, kk: (i, j))] if has_add else [])
    in_specs += [dep_spec] * len(deps)
    args = (a, b) + ((add,) if has_add else ()) + tuple(deps)
    return pl.pallas_call(
        body, name=name, grid=(m // tm, n // tn, nk), in_specs=in_specs, out_specs=o_spec, out_shape=out_shape,
        scratch_shapes=[pltpu.VMEM((tm, tn), F32)] if nk > 1 else [],
        compiler_params=_cp(("parallel", "parallel", "arbitrary")))(*args)


def _rmsnorm_fwd(x, g, name, deps=()):
    t, d = x.shape
    tb = _rows(t, 256)

    def body(x_ref, g_ref, *rest):
        o_ref = rest[-1]
        xv = x_ref[...]
        r = lax.rsqrt(jnp.mean(xv * xv, axis=-1, keepdims=True) + EPS)
        o_ref[...] = (xv * r * g_ref[...]).astype(BF16)

    dep_spec = pl.BlockSpec((8, LANES), lambda i: (0, 0))
    return pl.pallas_call(
        body, name=name, grid=(t // tb,),
        in_specs=[pl.BlockSpec((tb, d), lambda i: (i, 0)), pl.BlockSpec((1, d), lambda i: (0, 0))]
        + [dep_spec] * len(deps),
        out_specs=pl.BlockSpec((tb, d), lambda i: (i, 0)), out_shape=jax.ShapeDtypeStruct((t, d), BF16),
        compiler_params=_cp(("parallel",)))(x, g, *deps)


def _rmsnorm_bwd(x, g, dy, res, name, deps=()):
    t, d = x.shape
    tb = _rows(t, 256)

    def body(x_ref, g_ref, dy_ref, res_ref, *rest):
        dx_ref, dx16_ref, dg_ref = rest[-3:]
        i = pl.program_id(0)
        xv = x_ref[...]
        dyv = dy_ref[...].astype(F32)
        r = lax.rsqrt(jnp.mean(xv * xv, axis=-1, keepdims=True) + EPS)
        u = dyv * g_ref[...]
        dx = r * u - xv * (r * r * r * jnp.mean(u * xv, axis=-1, keepdims=True)) + res_ref[...]
        dx_ref[...] = dx
        dx16_ref[...] = dx.astype(BF16)
        part = jnp.sum(dyv * xv * r, axis=0, keepdims=True)

        @pl.when(i == 0)
        def _():
            dg_ref[...] = part

        @pl.when(i > 0)
        def _():
            dg_ref[...] += part

    row = pl.BlockSpec((tb, d), lambda i: (i, 0))
    vec = pl.BlockSpec((1, d), lambda i: (0, 0))
    return pl.pallas_call(
        body, name=name, grid=(t // tb,),
        in_specs=[row, vec, row, row] + [pl.BlockSpec((8, LANES), lambda i: (0, 0))] * len(deps),
        out_specs=[row, row, vec],
        out_shape=[jax.ShapeDtypeStruct((t, d), F32), jax.ShapeDtypeStruct((t, d), BF16),
                   jax.ShapeDtypeStruct((1, d), F32)],
        compiler_params=_cp(("arbitrary",)))(x, g, dy, res, *deps)


def _final_loss(h, g, tgt):
    t, d = h.shape
    tb = _rows(t, 256)

    def body(h_ref, g_ref, t_ref, loss_ref, dh_ref, dh16_ref, dg_ref):
        i = pl.program_id(0)
        hv = h_ref[...]
        gv = g_ref[...]
        r = lax.rsqrt(jnp.mean(hv * hv, axis=-1, keepdims=True) + EPS)
        y = hv * r * gv
        diff = y - t_ref[...]
        lpart = jnp.sum(jnp.sum(diff * diff, axis=1, keepdims=True), axis=0, keepdims=True) * (0.5 / d)
        dy = diff * (1.0 / d)
        u = dy * gv
        dh = r * u - hv * (r * r * r * jnp.mean(u * hv, axis=-1, keepdims=True))
        dh_ref[...] = dh
        dh16_ref[...] = dh.astype(BF16)
        gpart = jnp.sum(dy * hv * r, axis=0, keepdims=True)
        lrow = jnp.broadcast_to(lpart, (1, LANES))

        @pl.when(i == 0)
        def _():
            loss_ref[...] = lrow
            dg_ref[...] = gpart

        @pl.when(i > 0)
        def _():
            loss_ref[...] += lrow
            dg_ref[...] += gpart

    row = pl.BlockSpec((tb, d), lambda i: (i, 0))
    vec = pl.BlockSpec((1, d), lambda i: (0, 0))
    return pl.pallas_call(
        body, name="final_loss", grid=(t // tb,), in_specs=[row, vec, row],
        out_specs=[pl.BlockSpec((1, LANES), lambda i: (0, 0)), row, row, vec],
        out_shape=[jax.ShapeDtypeStruct((1, LANES), F32), jax.ShapeDtypeStruct((t, d), F32),
                   jax.ShapeDtypeStruct((t, d), BF16), jax.ShapeDtypeStruct((1, d), F32)],
        compiler_params=_cp(("arbitrary",)))(h, g, tgt)


def _rope_tables(t):
    pos = jnp.arange(t, dtype=F32)
    inv = 1.0 / (ROPE_THETA ** (jnp.arange(0, ROT_DIM, 2, dtype=F32) / ROT_DIM))
    ang = pos[:, None] * inv[None, :]
    cos, sin = jnp.cos(ang), jnp.sin(ang)
    half = ROT_DIM // 2
    rest = HEAD_DIM - ROT_DIM
    c = jnp.concatenate([cos, cos, jnp.ones((t, rest), F32)], axis=1)
    s1 = jnp.concatenate([-sin, jnp.zeros((t, half + rest), F32)], axis=1)
    s2 = jnp.concatenate([jnp.zeros((t, half), F32), sin, jnp.zeros((t, rest), F32)], axis=1)
    return jnp.concatenate([jnp.tile(v, (1, LANES // HEAD_DIM)) for v in (c, s1, s2)], axis=1)


def _split_tables(tab):
    return tab[:, :LANES], tab[:, LANES:2 * LANES], tab[:, 2 * LANES:]


def _rope(x, c, s1, s2):
    half = ROT_DIM // 2
    return x * c + pltpu.roll(x, LANES - half, 1) * s1 + pltpu.roll(x, half, 1) * s2


def _rope_t(g, c, s1, s2):
    half = ROT_DIM // 2
    return g * c + pltpu.roll(g * s1, half, 1) + pltpu.roll(g * s2, LANES - half, 1)


def _band_masks(i, heads):
    n = heads * ATTN_BLOCK
    q = jnp.bitwise_and(_iota((n, ATTN_BLOCK), 0), ATTN_BLOCK - 1)
    j = _iota((n, ATTN_BLOCK), 1)
    upper = j > q
    return upper, upper & (j < jnp.where(i > 0, 0, ATTN_BLOCK))


def _fold_band(full, upper):
    return jnp.where(upper, full[:, :ATTN_BLOCK], full[:, ATTN_BLOCK:])


def _unfold_band(band, upper):
    return jnp.concatenate([jnp.where(upper, band, 0.0), jnp.where(upper, 0.0, band)], axis=1)


def _half_masks():
    lane = _iota((1, LANES), 1)
    return [(lane < HEAD_DIM).astype(F32), (lane >= HEAD_DIM).astype(F32)]


def _stack_heads(blocks, hm, j):
    pieces = []
    for r in range(4):
        qb, half = (4 * j + r) // 2, (4 * j + r) % 2
        piece = blocks[qb] * hm[half]
        if half != j:
            piece = pltpu.roll(piece, HEAD_DIM, 1)
        pieces.append(piece)
    return jnp.concatenate(pieces, axis=0)


def _unstack_heads(stacked, j):
    out = []
    for qb in (2 * j, 2 * j + 1):
        acc = None
        for half in range(2):
            r = 2 * qb + half - 4 * j
            piece = stacked[r * ATTN_BLOCK:(r + 1) * ATTN_BLOCK]
            if half != j:
                piece = pltpu.roll(piece, HEAD_DIM, 1)
            acc = piece if acc is None else acc + piece
        out.append((qb, acc))
    return out


def _sink_column(sink_ref, base):
    return jnp.concatenate([jnp.full((ATTN_BLOCK, 1), sink_ref[base + r], F32) for r in range(4)], axis=0)


def _attn_specs(nb_clamp):
    blk = ATTN_BLOCK
    kb, vb = O_K // LANES, O_V // LANES

    def cur(i):
        return jnp.minimum(i, nb_clamp)

    def prev(i):
        return jnp.maximum(jnp.minimum(i, nb_clamp + 1) - 1, 0)

    q = pl.BlockSpec((blk, 512), lambda p, i: (cur(i), p))
    kc = pl.BlockSpec((blk, LANES), lambda p, i: (cur(i), kb + p))
    kp = pl.BlockSpec((blk, LANES), lambda p, i: (prev(i), kb + p))
    vc = pl.BlockSpec((blk, LANES), lambda p, i: (cur(i), vb + p))
    vp = pl.BlockSpec((blk, LANES), lambda p, i: (prev(i), vb + p))
    tc = pl.BlockSpec((blk, 3 * LANES), lambda p, i: (cur(i), 0))
    tp = pl.BlockSpec((blk, 3 * LANES), lambda p, i: (prev(i), 0))
    return q, kc, kp, vc, vp, tc, tp


def _attn_fwd(proj, sinks, tables):
    t = proj.shape[0]
    nb = t // ATTN_BLOCK
    scale = HEAD_DIM ** -0.5

    def body(sink_ref, q_ref, kc_ref, kp_ref, vc_ref, vp_ref, tc_ref, tp_ref, o_ref):
        p = pl.program_id(0)
        i = pl.program_id(1)
        cc, s1c, s2c = _split_tables(tc_ref[...])
        kband = jnp.concatenate([_rope(kp_ref[...], *_split_tables(tp_ref[...])),
                                 _rope(kc_ref[...], cc, s1c, s2c)], axis=0).astype(BF16)
        vband = jnp.concatenate([vp_ref[...], vc_ref[...]], axis=0)
        hm = _half_masks()
        vsel = [(vband * hm[j]).astype(BF16) for j in range(2)]
        upper, dropped = _band_masks(i, 1)
        qr = [_rope(q_ref[:, qb * LANES:(qb + 1) * LANES], cc, s1c, s2c) for qb in range(4)]

        def scores(hh):
            qb, half, j = hh // 2, hh % 2, hh // 4
            qs = qr[qb] * hm[half]
            if half != j:
                qs = pltpu.roll(qs, HEAD_DIM, 1)
            return _dot(qs, kband, 'nt')

        ahead = scores(0)
        acc = None
        for hh in range(8):
            qb, half, j = hh // 2, hh % 2, hh // 4
            raw = ahead
            if hh + 1 < 8:
                ahead = scores(hh + 1)
            s = jnp.where(dropped, NEG, _fold_band(raw, upper) * scale)
            sink = sink_ref[p * 8 + hh]
            m = jnp.maximum(jnp.max(s, axis=1, keepdims=True), sink)
            pe = jnp.exp(s - m)
            den = jnp.sum(pe, axis=1, keepdims=True) + jnp.exp(sink - m)
            o = _dot(_unfold_band(pe / den, upper), vsel[j])
            if half != j:
                o = pltpu.roll(o, HEAD_DIM, 1)
            acc = o if half == 0 else acc + o
            if half == 1:
                o_ref[:, qb * LANES:(qb + 1) * LANES] = acc

    q, kc, kp, vc, vp, tc, tp = _attn_specs(nb - 1)
    smem = pl.BlockSpec(memory_space=pltpu.SMEM)
    return pl.pallas_call(
        body, name="attn_fwd", grid=(4, nb),
        in_specs=[smem, q, kc, kp, vc, vp, tc, tp],
        out_specs=pl.BlockSpec((ATTN_BLOCK, 512), lambda p, i: (i, p)),
        out_shape=jax.ShapeDtypeStruct((t, ATTN_WIDTH), F32),
        compiler_params=_cp(("parallel", "arbitrary")))(sinks, proj, proj, proj, proj, proj, tables, tables)


def _attn_bwd(proj, sinks, tables, dout):
    t = proj.shape[0]
    nb = t // ATTN_BLOCK
    scale = HEAD_DIM ** -0.5

    def body(sink_ref, q_ref, kc_ref, kp_ref, vc_ref, vp_ref, tc_ref, tp_ref,
             do_ref, dq_ref, dk_ref, dv_ref, ds_ref, carry_k, carry_v):
        p = pl.program_id(0)
        i = pl.program_id(1)
        ptab = _split_tables(tp_ref[...])

        @pl.when(i == 0)
        def _():
            carry_k[...] = jnp.zeros_like(carry_k)
            carry_v[...] = jnp.zeros_like(carry_v)
            ds_ref[...] = jnp.zeros_like(ds_ref)

        @pl.when(i < nb)
        def _():
            cc, s1c, s2c = _split_tables(tc_ref[...])
            kband = jnp.concatenate([_rope(kp_ref[...], *ptab), _rope(kc_ref[...], cc, s1c, s2c)], axis=0)
            vband = jnp.concatenate([vp_ref[...], vc_ref[...]], axis=0)
            hm = _half_masks()
            kband16 = kband.astype(BF16)
            vband16 = vband.astype(BF16)
            upper, dropped = _band_masks(i, 4)
            dkb = jnp.zeros((2 * ATTN_BLOCK, LANES), F32)
            dvb = jnp.zeros((2 * ATTN_BLOCK, LANES), F32)
            row8 = _iota((8, LANES), 0)
            dsink = jnp.zeros((8, LANES), F32)
            qr = [_rope(q_ref[:, qb * LANES:(qb + 1) * LANES], cc, s1c, s2c) for qb in range(4)]
            dob = [do_ref[:, qb * LANES:(qb + 1) * LANES] for qb in range(4)]
            for j in range(2):
                qst = _stack_heads(qr, hm, j).astype(BF16)
                dost = _stack_heads(dob, hm, j).astype(BF16)
                s = jnp.where(dropped, NEG, _fold_band(_dot(qst, kband16, 'nt'), upper) * scale)
                sink = _sink_column(sink_ref, p * 8 + 4 * j)
                m = jnp.maximum(jnp.max(s, axis=1, keepdims=True), sink)
                pe = jnp.exp(s - m)
                psink = jnp.exp(sink - m)
                den = jnp.sum(pe, axis=1, keepdims=True) + psink
                pr = pe / den
                dvb = dvb + _dot(_unfold_band(pr, upper).T, dost)
                dp = _fold_band(_dot(dost, vband16, 'nt'), upper)
                delta = jnp.sum(pr * dp, axis=1, keepdims=True)
                dsc = _unfold_band(pr * (dp - delta) * scale, upper)
                dsk = psink / den * delta
                for r in range(4):
                    part = jnp.sum(dsk[r * ATTN_BLOCK:(r + 1) * ATTN_BLOCK])
                    dsink = dsink + jnp.where(row8 == 4 * j + r, -part, 0.0)
                for qb, dqb in _unstack_heads(_dot(dsc, kband * hm[j]), j):
                    dq_ref[:, qb * LANES:(qb + 1) * LANES] = _rope_t(dqb, cc, s1c, s2c).astype(BF16)
                dkb = dkb + _dot(dsc.T, qst)
            ds_ref[0] += dsink
            dk_ref[...] = _rope_t(carry_k[...] + dkb[:ATTN_BLOCK], *ptab).astype(BF16)
            dv_ref[...] = (carry_v[...] + dvb[:ATTN_BLOCK]).astype(BF16)
            carry_k[...] = dkb[ATTN_BLOCK:]
            carry_v[...] = dvb[ATTN_BLOCK:]

        @pl.when(i == nb)
        def _():
            dk_ref[...] = _rope_t(carry_k[...], *ptab).astype(BF16)
            dv_ref[...] = carry_v[...].astype(BF16)

    q, kc, kp, vc, vp, tc, tp = _attn_specs(nb - 1)
    smem = pl.BlockSpec(memory_space=pltpu.SMEM)
    qblk = pl.BlockSpec((ATTN_BLOCK, 512), lambda p, i: (jnp.minimum(i, nb - 1), p))
    kvout = pl.BlockSpec((ATTN_BLOCK, LANES), lambda p, i: (jnp.maximum(i - 1, 0), p))
    return pl.pallas_call(
        body, name="attn_bwd", grid=(4, nb + 1),
        in_specs=[smem, q, kc, kp, vc, vp, tc, tp, qblk],
        out_specs=[qblk, kvout, kvout, pl.BlockSpec((1, 8, LANES), lambda p, i: (p, 0, 0))],
        out_shape=[jax.ShapeDtypeStruct((t, ATTN_WIDTH), BF16), jax.ShapeDtypeStruct((t, KV_WIDTH), BF16),
                   jax.ShapeDtypeStruct((t, KV_WIDTH), BF16), jax.ShapeDtypeStruct((4, 8, LANES), F32)],
        scratch_shapes=[pltpu.VMEM((ATTN_BLOCK, LANES), F32), pltpu.VMEM((ATTN_BLOCK, LANES), F32)],
        compiler_params=_cp(("parallel", "arbitrary")))(sinks, proj, proj, proj, proj, proj, tables, tables, dout)


def _shift_rows(x, prev8, j):
    n, c = x.shape
    r = pltpu.roll(x.reshape(n // 8, 8, c), j, 1)
    before = pltpu.roll(prev8, j, 0)[None]
    if n > 8:
        before = jnp.concatenate([before, r[:-1]], axis=0)
    return jnp.where(_iota((1, 8, 1), 1) < j, before, r).reshape(n, c)


def _shift_rows_up(x, next8, j):
    n, c = x.shape
    r = pltpu.roll(x.reshape(n // 8, 8, c), 8 - j, 1)
    after = pltpu.roll(next8, 8 - j, 0)[None]
    if n > 8:
        after = jnp.concatenate([r[1:], after], axis=0)
    return jnp.where(_iota((1, 8, 1), 1) >= 8 - j, after, r).reshape(n, c)


def _conv_apply(x, prev8, w, b, taps):
    u = b + x * w[taps - 1:taps]
    for j in range(1, taps):
        u = u + _shift_rows(x, prev8, j) * w[taps - 1 - j:taps - j]
    return u


def _conv_grads(du, du_next8, x, w, taps):
    dx = du * w[taps - 1:taps]
    rowk = _iota((taps, 1), 0)
    dw = jnp.where(rowk == taps - 1, jnp.sum(du * x, axis=0, keepdims=True), 0.0)
    for j in range(1, taps):
        ahead = _shift_rows_up(du, du_next8, j)
        dx = dx + ahead * w[taps - 1 - j:taps - j]
        dw = dw + jnp.where(rowk == taps - 1 - j, jnp.sum(ahead * x, axis=0, keepdims=True), 0.0)
    return dx, dw, jnp.sum(du, axis=0, keepdims=True)


def _conv_specs(tb, tc, col0, t):
    c0 = col0 // tc
    cur = pl.BlockSpec((tb, tc), lambda j, i: (i, c0 + j))
    prev = pl.BlockSpec((8, tc), lambda j, i: (jnp.maximum(i * (tb // 8) - 1, 0), c0 + j))
    nxt = pl.BlockSpec((8, tc), lambda j, i: (jnp.minimum((i + 1) * (tb // 8), t // 8 - 1), c0 + j))
    return cur, prev, nxt


def _conv_silu_fwd(x, w, b, *, col0, width, name):
    t = x.shape[0]
    taps = w.shape[0]
    tb, tc = _rows(t, 512), _tile(width, 1024)
    assert col0 % tc == 0

    def body(x_ref, xp_ref, w_ref, b_ref, o_ref, u_ref):
        i = pl.program_id(1)
        prev8 = jnp.where(i > 0, xp_ref[...], 0.0)
        u = _conv_apply(x_ref[...], prev8, w_ref[...], b_ref[...], taps)
        u_ref[...] = u
        o_ref[...] = u * _sigmoid(u)

    cur, prev, _ = _conv_specs(tb, tc, col0, t)
    par = pl.BlockSpec((taps, tc), lambda j, i: (0, j))
    bias = pl.BlockSpec((1, tc), lambda j, i: (0, j))
    out = pl.BlockSpec((tb, tc), lambda j, i: (i, j))
    shp = jax.ShapeDtypeStruct((t, width), F32)
    return pl.pallas_call(
        body, name=name, grid=(width // tc, t // tb), in_specs=[cur, prev, par, bias], out_specs=[out, out],
        out_shape=[shp, shp], compiler_params=_cp(("parallel", "parallel")))(x, x, w, b)


def _dsilu(u):
    sg = _sigmoid(u)
    return sg * (1.0 + u * (1.0 - sg))


def _ssd_conv_bwd(x, w, dxs, dbm, dcm, *, col0, name):
    t = x.shape[0]
    taps = w.shape[0]
    tb, tc = _rows(t, 512), BC_WIDTH
    nrow, ncol = t // tb, CONV_CH // tc
    c0 = col0 // tc

    def body(x_ref, w_ref, xs_ref, xsn_ref, bm_ref, bmn_ref, cm_ref, cmn_ref, dx_ref, dw_ref, db_ref):
        i = pl.program_id(0)
        j = pl.program_id(1)

        def run(du_ref, dun_ref):
            next8 = jnp.where(i < nrow - 1, dun_ref[...], 0.0)
            dx, dwv, dbv = _conv_grads(du_ref[...], next8, x_ref[...], w_ref[...], taps)
            dx_ref[...] = dx.astype(BF16)

            @pl.when(i == 0)
            def _():
                dw_ref[j] = dwv
                db_ref[j] = dbv

            @pl.when(i > 0)
            def _():
                dw_ref[j] += dwv
                db_ref[j] += dbv

        pl.when(j < 2)(lambda: run(xs_ref, xsn_ref))
        pl.when(j == 2)(lambda: run(bm_ref, bmn_ref))
        pl.when(j == 3)(lambda: run(cm_ref, cmn_ref))

    def nxt_row(i):
        return jnp.minimum((i + 1) * (tb // 8), t // 8 - 1)

    xs_col = lambda j: jnp.minimum(j, SSD_INNER // tc - 1)
    in_specs = [pl.BlockSpec((tb, tc), lambda i, j: (i, c0 + j)), pl.BlockSpec((taps, tc), lambda i, j: (0, j)),
                pl.BlockSpec((tb, tc), lambda i, j: (i, xs_col(j))),
                pl.BlockSpec((8, tc), lambda i, j: (nxt_row(i), xs_col(j))),
                pl.BlockSpec((tb, tc), lambda i, j: (i, 0)), pl.BlockSpec((8, tc), lambda i, j: (nxt_row(i), 0)),
                pl.BlockSpec((tb, tc), lambda i, j: (i, 0)), pl.BlockSpec((8, tc), lambda i, j: (nxt_row(i), 0))]
    dx, dw, db = pl.pallas_call(
        body, name=name, grid=(nrow, ncol), in_specs=in_specs,
        out_specs=[pl.BlockSpec((tb, tc), lambda i, j: (i, j)),
                   pl.BlockSpec((ncol, taps, tc), lambda i, j: (0, 0, 0)),
                   pl.BlockSpec((ncol, 1, tc), lambda i, j: (0, 0, 0))],
        out_shape=[jax.ShapeDtypeStruct((t, CONV_CH), BF16), jax.ShapeDtypeStruct((ncol, taps, tc), F32),
                   jax.ShapeDtypeStruct((ncol, 1, tc), F32)],
        compiler_params=_cp(("arbitrary", "arbitrary")))(x, w, dxs, dxs, dbm, dbm, dcm, dcm)
    return dx, dw.transpose(1, 0, 2).reshape(taps, CONV_CH), db.transpose(1, 0, 2).reshape(1, CONV_CH)


def _ffn_specs(tb, tc, t):
    nc = D_FF // tc

    def cur(half):
        return pl.BlockSpec((tb, tc), lambda j, i: (i, half * nc + j))

    def prev(half):
        return pl.BlockSpec((8, tc), lambda j, i: (jnp.maximum(i * (tb // 8) - 1, 0), half * nc + j))

    def nxt(half):
        return pl.BlockSpec((8, tc), lambda j, i: (jnp.minimum((i + 1) * (tb // 8), t // 8 - 1), half * nc + j))

    def par(rows, half):
        return pl.BlockSpec((rows, tc), lambda j, i: (0, half * nc + j))

    return cur, prev, nxt, par


def _ffn_act_fwd(u0, w, b):
    t = u0.shape[0]
    tb, tc = _rows(t, 512), _tile(D_FF, 1408)
    cur, prev, _, par = _ffn_specs(tb, tc, t)

    def body(g_ref, gp_ref, v_ref, vp_ref, wg_ref, wv_ref, bg_ref, bv_ref, o_ref, u_ref):
        i = pl.program_id(1)
        ug = _conv_apply(g_ref[...], jnp.where(i > 0, gp_ref[...], 0.0), wg_ref[...], bg_ref[...], FFN_CONV)
        uv = _conv_apply(v_ref[...], jnp.where(i > 0, vp_ref[...], 0.0), wv_ref[...], bv_ref[...], FFN_CONV)
        o_ref[...] = (ug * _sigmoid(ug) * uv).astype(BF16)
        u_ref[0] = ug
        u_ref[1] = uv

    return pl.pallas_call(
        body, name="ffn_act_fwd", grid=(D_FF // tc, t // tb),
        in_specs=[cur(0), prev(0), cur(1), prev(1), par(FFN_CONV, 0), par(FFN_CONV, 1), par(1, 0), par(1, 1)],
        out_specs=[pl.BlockSpec((tb, tc), lambda j, i: (i, j)), pl.BlockSpec((2, tb, tc), lambda j, i: (0, i, j))],
        out_shape=[jax.ShapeDtypeStruct((t, D_FF), BF16), jax.ShapeDtypeStruct((2, t, D_FF), F32)],
        compiler_params=_cp(("parallel", "parallel")))(u0, u0, u0, u0, w, w, b, b)


def _ffn_act_bwd(u0, u, w, da):
    t = u0.shape[0]
    tb, tc = _rows(t, 256), _tile(D_FF, 1408)
    nrow = t // tb
    taps = FFN_CONV
    cur, _, _, par = _ffn_specs(tb, tc, t)

    def dact(ug, uv, dav):
        sg = _sigmoid(ug)
        return dav * uv * (sg * (1.0 + ug * (1.0 - sg))), dav * ug * sg

    def body(g_ref, v_ref, u_ref, un_ref, wg_ref, wv_ref, da_ref, dan_ref, dx_ref, dw_ref, db_ref):
        i = pl.program_id(1)
        dug, duv = dact(u_ref[0], u_ref[1], da_ref[...].astype(F32))
        dan = jnp.where(i < nrow - 1, dan_ref[...].astype(F32)[:8], 0.0)
        dugn, duvn = dact(un_ref[0], un_ref[1], dan)
        dxg, dwg, dbg = _conv_grads(dug, dugn, g_ref[...], wg_ref[...], taps)
        dxv, dwv, dbv = _conv_grads(duv, duvn, v_ref[...], wv_ref[...], taps)
        dx_ref[0] = dxg.astype(BF16)
        dx_ref[1] = dxv.astype(BF16)

        @pl.when(i == 0)
        def _():
            dw_ref[0] = dwg
            dw_ref[1] = dwv
            db_ref[0] = dbg
            db_ref[1] = dbv

        @pl.when(i > 0)
        def _():
            dw_ref[0] += dwg
            dw_ref[1] += dwv
            db_ref[0] += dbg
            db_ref[1] += dbv

    both = pl.BlockSpec((2, tb, tc), lambda j, i: (0, i, j))
    both_nxt = pl.BlockSpec((2, 8, tc), lambda j, i: (0, jnp.minimum((i + 1) * (tb // 8), t // 8 - 1), j))
    da_cur = pl.BlockSpec((tb, tc), lambda j, i: (i, j))
    da_nxt = pl.BlockSpec((16, tc), lambda j, i: (jnp.minimum((i + 1) * (tb // 16), t // 16 - 1), j))
    return pl.pallas_call(
        body, name="ffn_act_bwd", grid=(D_FF // tc, nrow),
        in_specs=[cur(0), cur(1), both, both_nxt, par(taps, 0), par(taps, 1), da_cur, da_nxt],
        out_specs=[both, pl.BlockSpec((2, taps, tc), lambda j, i: (0, 0, j)),
                   pl.BlockSpec((2, 1, tc), lambda j, i: (0, 0, j))],
        out_shape=[jax.ShapeDtypeStruct((2, t, D_FF), BF16), jax.ShapeDtypeStruct((2, taps, D_FF), F32),
                   jax.ShapeDtypeStruct((2, 1, D_FF), F32)],
        compiler_params=_cp(("parallel", "arbitrary")))(u0, u0, u, u, w, w, da, da)


def _head_masks():
    lane = _iota((1, 4 * SSD_HEAD_DIM), 1)
    return [((lane >= r * SSD_HEAD_DIM) & (lane < (r + 1) * SSD_HEAD_DIM)).astype(F32) for r in range(4)]


def _segsum(v):
    first = _iota((1, LANES), 1) < SSD_HEAD_DIM
    halves = []
    for k in range(2):
        vh = v[:, k * LANES:(k + 1) * LANES]
        both = jnp.sum(vh, axis=1, keepdims=True)
        one = jnp.sum(jnp.where(first, vh, 0.0), axis=1, keepdims=True)
        halves.append(jnp.where(first, one, both - one))
    return jnp.concatenate(halves, axis=1)


def _ssd_common(raw_e, prow, rawr4, bcol, acol):
    n = SSD_CHUNK
    dt_e = _softplus(raw_e + prow[0:1, :])
    a_e = -jnp.exp(prow[1:2, :])
    d_e = prow[2:3, :]
    tril = (_iota((n, n), 0) >= _iota((n, n), 1)).astype(F32)
    acs_e = _dot_exact(tril, dt_e * a_e)
    last_e = acs_e[n - 1:n, :]
    dtr4 = _softplus(rawr4 + bcol)
    triu = (_iota((n, n), 0) <= _iota((n, n), 1)).astype(F32)
    acs_r4 = _dot_exact(dtr4 * (-jnp.exp(acol)), triu)
    return dt_e, a_e, d_e, acs_e, last_e, acs_r4


def _decay_matrix(acs_e, acs_r4, r):
    n = SSD_CHUNK
    col = acs_e[:, r * SSD_HEAD_DIM:r * SSD_HEAD_DIM + 1]
    seg = col - acs_r4[r:r + 1, :]
    causal = _iota((n, n), 0) >= _iota((n, n), 1)
    return jnp.exp(jnp.where(causal, seg, NEG))


SSD_STEP_CHUNKS = 4
SSD_ROWS = SSD_STEP_CHUNKS * SSD_CHUNK


def _ssd_specs(t, rev):
    nb = t // SSD_ROWS
    xb, bb, cb = 0, SSD_INNER // SSD_STATE, (SSD_INNER + BC_WIDTH) // SSD_STATE

    def ch(c):
        return (nb - 1 - c) if rev else c

    x = pl.BlockSpec((SSD_ROWS, 256), lambda g, c: (ch(c), xb + g))
    bm = pl.BlockSpec((SSD_ROWS, SSD_STATE), lambda g, c: (ch(c), bb + g))
    cm = pl.BlockSpec((SSD_ROWS, SSD_STATE), lambda g, c: (ch(c), cb + g))
    dtc = pl.BlockSpec((1, SSD_ROWS, 256), lambda g, c: (g, ch(c), 0))
    dtr = pl.BlockSpec((1, 4, SSD_ROWS), lambda g, c: (g, 0, ch(c)))
    prow = pl.BlockSpec((1, 3, 256), lambda g, c: (g, 0, 0))
    pcol = pl.BlockSpec((1, 4, 1), lambda g, c: (g, 0, 0))
    st = pl.BlockSpec((1, SSD_STEP_CHUNKS, SSD_STATE, 256), lambda g, c: (g, ch(c), 0, 0))
    return x, bm, cm, dtc, dtr, prow, pcol, st, ch


def _ssd_params(dt_raw, dt_bias, a_log, ssd_d):
    t = dt_raw.shape[0]
    by_group = dt_raw.reshape(t, SSD_GROUPS, 4)
    dtc = jnp.repeat(by_group, SSD_HEAD_DIM, axis=2).transpose(1, 0, 2)
    dtr = by_group.transpose(1, 2, 0)
    prow = jnp.repeat(jnp.stack([dt_bias.reshape(SSD_GROUPS, 4), a_log.reshape(SSD_GROUPS, 4),
                                 ssd_d.reshape(SSD_GROUPS, 4)], axis=1), SSD_HEAD_DIM, axis=2)
    bcol = dt_bias.reshape(SSD_GROUPS, 4, 1)
    acol = a_log.reshape(SSD_GROUPS, 4, 1)
    return dtc, dtr, prow, bcol, acol


def _ssd_fwd(xbc, params):
    t = xbc.shape[0]
    nc = t // SSD_CHUNK
    dtc, dtr, prow, bcol, acol = params

    def body(x_ref, b_ref, c_ref, dtc_ref, dtr_ref, prow_ref, bcol_ref, acol_ref, y_ref, st_ref, s_scr):
        c = pl.program_id(1)

        @pl.when(c == 0)
        def _():
            s_scr[...] = jnp.zeros_like(s_scr)

        masks = _head_masks()
        s = s_scr[...]
        for k in range(SSD_STEP_CHUNKS):
            rows = slice(k * SSD_CHUNK, (k + 1) * SSD_CHUNK)
            dt_e, a_e, d_e, acs_e, last_e, acs_r4 = _ssd_common(
                dtc_ref[0, rows], prow_ref[0], dtr_ref[0][:, rows], bcol_ref[0], acol_ref[0])
            xv = x_ref[rows]
            bm, cm = b_ref[rows], c_ref[rows]
            st_ref[0, k] = s
            xdt = xv * dt_e
            cb = _dot(cm, bm, 'nt')
            y = _dot(cm, s) * jnp.exp(acs_e) + xv * d_e
            for r in range(4):
                mr = cb * _decay_matrix(acs_e, acs_r4, r)
                y = y + _dot(mr, xdt * masks[r])
            y_ref[rows] = y
            w = xdt * jnp.exp(last_e - acs_e)
            s = s * jnp.exp(last_e) + _dot(bm.T, w)
        s_scr[...] = s

    x, bm, cm, dtcs, dtrs, prs, pcs, st, _ = _ssd_specs(t, False)
    return pl.pallas_call(
        body, name="ssd_fwd", grid=(SSD_GROUPS, t // SSD_ROWS), in_specs=[x, bm, cm, dtcs, dtrs, prs, pcs, pcs],
        out_specs=[pl.BlockSpec((SSD_ROWS, 256), lambda g, c: (c, g)), st],
        out_shape=[jax.ShapeDtypeStruct((t, SSD_INNER), F32),
                   jax.ShapeDtypeStruct((SSD_GROUPS, nc, SSD_STATE, 256), F32)],
        scratch_shapes=[pltpu.VMEM((SSD_STATE, 256), F32)],
        compiler_params=_cp(("parallel", "arbitrary")))(xbc, xbc, xbc, dtc, dtr, prow, bcol, acol)


def _ssd_bwd(xbc, pre, params, states, dy):
    t = xbc.shape[0]
    nc = t // SSD_CHUNK
    n = SSD_CHUNK
    dtc, dtr, prow, bcol, acol = params

    def body(x_ref, b_ref, c_ref, ux_ref, ub_ref, uc_ref, dtc_ref, dtr_ref, prow_ref, bcol_ref, acol_ref, st_ref,
             dy_ref, dx_ref, db_ref, dc_ref, ddt_ref, dp_ref, ds_scr):
        c = pl.program_id(1)

        @pl.when(c == 0)
        def _():
            ds_scr[...] = jnp.zeros_like(ds_scr)
            dp_ref[...] = jnp.zeros_like(dp_ref)

        masks = _head_masks()
        ds = ds_scr[...]
        for k in reversed(range(SSD_STEP_CHUNKS)):
            rows = slice(k * SSD_CHUNK, (k + 1) * SSD_CHUNK)
            raw_e = dtc_ref[0, rows]
            prw = prow_ref[0]
            dt_e, a_e, d_e, acs_e, last_e, acs_r4 = _ssd_common(raw_e, prw, dtr_ref[0][:, rows], bcol_ref[0], acol_ref[0])
            xv = x_ref[rows]
            bm, cm = b_ref[rows], c_ref[rows]
            s = st_ref[0, k]
            dyv = dy_ref[rows]
            e_e = jnp.exp(acs_e)
            dec_e = jnp.exp(last_e - acs_e)
            cd_e = jnp.exp(last_e)
            xdt = xv * dt_e
            w = xdt * dec_e
            b16, c16, s16, ds16 = bm.astype(BF16), cm.astype(BF16), s.astype(BF16), ds.astype(BF16)
            cb = _dot(c16, b16, 'nt')
            yoff_raw = _dot(c16, s16)
            dye = dyv * e_e
            dye16 = dye.astype(BF16)
            dcm = _dot(dye16, s16, 'nt')
            ds_prev = ds * cd_e + _dot(cm.T, dye16)
            dacs_e = _segsum(dyv * yoff_raw) * e_e
            dw = _dot(b16, ds16)
            dbm = _dot(w, ds16, 'nt')
            tdec = _segsum(dw * xdt) * dec_e
            dacs_e = dacs_e - tdec
            dlast_e = jnp.sum(tdec, axis=0, keepdims=True)
            dxdt = dw * dec_e
            dlast_e = dlast_e + _segsum(jnp.sum(ds * s, axis=0, keepdims=True)) * cd_e
            dcb = jnp.zeros((n, n), F32)
            for r in range(4):
                lm = _decay_matrix(acs_e, acs_r4, r)
                mr = cb * lm
                dyr16 = (dyv * masks[r]).astype(BF16)
                dm = _dot(dyr16, xdt * masks[r], 'nt')
                dcb = dcb + dm * lm
                dseg = dm * mr
                dcol = jnp.sum(dseg, axis=1, keepdims=True) - jnp.sum(dseg.T, axis=1, keepdims=True)
                dacs_e = dacs_e + dcol * masks[r]
                dxdt = dxdt + _dot(mr.T, dyr16)
            dcm = dcm + _dot(dcb, b16)
            dbm = dbm + _dot(dcb.T, c16)
            dacs_e = dacs_e + jnp.where(_iota((n, 1), 0) == n - 1, dlast_e, 0.0)
            triu = (_iota((n, n), 0) <= _iota((n, n), 1)).astype(F32)
            ddta_e = _dot_exact(triu, dacs_e)
            ddt_e = ddta_e * a_e + _segsum(dxdt * xv)
            dx_ref[rows] = (dxdt * dt_e + dyv * d_e) * _dsilu(ux_ref[rows])
            db_ref[rows] = dbm * _dsilu(ub_ref[rows])
            dc_ref[rows] = dcm * _dsilu(uc_ref[rows])
            draw_e = ddt_e * _sigmoid(raw_e + prw[0:1, :])
            draw_t = draw_e.T
            ddt_ref[0, :, rows] = jnp.concatenate([draw_t[r * SSD_HEAD_DIM:r * SSD_HEAD_DIM + 1] for r in range(4)], axis=0)
            dbias = jnp.sum(draw_e, axis=0, keepdims=True)
            dalog = jnp.sum(ddta_e * dt_e, axis=0, keepdims=True) * a_e
            dd = _segsum(jnp.sum(dyv * xv, axis=0, keepdims=True))
            row3 = _iota((3, 1), 0)
            dp_ref[0] += (jnp.where(row3 == 0, dbias, 0.0) + jnp.where(row3 == 1, dalog, 0.0)
                          + jnp.where(row3 == 2, dd, 0.0))
            ds = ds_prev
        ds_scr[...] = ds


    x, bm, cm, dtcs, dtrs, prs, pcs, st, ch = _ssd_specs(t, True)
    yblk = pl.BlockSpec((SSD_ROWS, 256), lambda g, c: (ch(c), g))
    nblk = pl.BlockSpec((SSD_ROWS, SSD_STATE), lambda g, c: (ch(c), g))
    return pl.pallas_call(
        body, name="ssd_bwd", grid=(SSD_GROUPS, t // SSD_ROWS),
        in_specs=[x, bm, cm, x, bm, cm, dtcs, dtrs, prs, pcs, pcs, st, yblk],
        out_specs=[yblk, nblk, nblk, dtrs, prs],
        out_shape=[jax.ShapeDtypeStruct((t, SSD_INNER), F32), jax.ShapeDtypeStruct((t, BC_WIDTH), F32),
                   jax.ShapeDtypeStruct((t, BC_WIDTH), F32), jax.ShapeDtypeStruct((SSD_GROUPS, 4, t), F32),
                   jax.ShapeDtypeStruct((SSD_GROUPS, 3, 256), F32)],
        scratch_shapes=[pltpu.VMEM((SSD_STATE, 256), F32)],
        compiler_params=_cp(("parallel", "arbitrary")))(xbc, xbc, xbc, pre, pre, pre, dtc, dtr, prow, bcol, acol,
                                                         states, dy)


GROUP_W = SSD_INNER // SSD_GROUPS


def _mix_specs(tb):
    row = pl.BlockSpec((tb, 2048), lambda i: (i, 0))
    zlo = pl.BlockSpec((tb, 1024), lambda i: (i, O_Z // 1024))
    zhi = pl.BlockSpec((tb, 1024), lambda i: (i, O_Z // 1024 + 1))
    vec = pl.BlockSpec((1, 2048), lambda i: (0, 0))
    return row, zlo, zhi, vec


def _mix_fwd(attn, y, proj, g_attn, g_ssd):
    t = attn.shape[0]
    tb = _rows(t, 256)

    def body(a_ref, y_ref, zlo_ref, zhi_ref, ga_ref, gs_ref, o_ref):
        av = a_ref[...]
        r = lax.rsqrt(jnp.mean(av * av, axis=-1, keepdims=True) + EPS)
        o_ref[:, :ATTN_WIDTH] = (av * r * ga_ref[...]).astype(BF16)
        for g in range(SSD_GROUPS):
            lo, hi = g * GROUP_W, (g + 1) * GROUP_W
            zref = zlo_ref if g < 4 else zhi_ref
            z = zref[:, lo % 1024:lo % 1024 + GROUP_W]
            yg = y_ref[:, lo:hi] * (z * _sigmoid(z))
            rg = lax.rsqrt(jnp.mean(yg * yg, axis=-1, keepdims=True) + EPS)
            o_ref[:, ATTN_WIDTH + lo:ATTN_WIDTH + hi] = (yg * rg * gs_ref[:, lo:hi]).astype(BF16)

    row, zlo, zhi, vec = _mix_specs(tb)
    return pl.pallas_call(
        body, name="mix_fwd", grid=(t // tb,), in_specs=[row, row, zlo, zhi, vec, vec],
        out_specs=pl.BlockSpec((tb, 4096), lambda i: (i, 0)), out_shape=jax.ShapeDtypeStruct((t, 4096), BF16),
        compiler_params=_cp(("parallel",)))(attn, y, proj, proj, g_attn, g_ssd)


def _mix_bwd(dmix, attn, y, proj, g_attn, g_ssd):
    t = attn.shape[0]
    tb = _rows(t, 256)

    def body(dm_ref, a_ref, y_ref, zlo_ref, zhi_ref, ga_ref, gs_ref, da_ref, dy_ref, dz_ref, dga_ref, dgs_ref):
        i = pl.program_id(0)
        av = a_ref[...]
        dn = dm_ref[:, :ATTN_WIDTH].astype(F32)
        r = lax.rsqrt(jnp.mean(av * av, axis=-1, keepdims=True) + EPS)
        u = dn * ga_ref[...]
        da_ref[...] = r * u - av * (r * r * r * jnp.mean(u * av, axis=-1, keepdims=True))
        dga = jnp.sum(dn * av * r, axis=0, keepdims=True)

        @pl.when(i == 0)
        def _():
            dga_ref[...] = dga

        @pl.when(i > 0)
        def _():
            dga_ref[...] += dga

        for g in range(SSD_GROUPS):
            lo, hi = g * GROUP_W, (g + 1) * GROUP_W
            zref = zlo_ref if g < 4 else zhi_ref
            z = zref[:, lo % 1024:lo % 1024 + GROUP_W]
            yv = y_ref[:, lo:hi]
            sg = _sigmoid(z)
            sz = z * sg
            yg = yv * sz
            rg = lax.rsqrt(jnp.mean(yg * yg, axis=-1, keepdims=True) + EPS)
            do = dm_ref[:, ATTN_WIDTH + lo:ATTN_WIDTH + hi].astype(F32)
            ug = do * gs_ref[:, lo:hi]
            dyg = rg * ug - yg * (rg * rg * rg * jnp.mean(ug * yg, axis=-1, keepdims=True))
            dy_ref[:, lo:hi] = dyg * sz
            dz_ref[:, lo:hi] = (dyg * yv * (sg * (1.0 + z * (1.0 - sg)))).astype(BF16)
            dgs = jnp.sum(do * yg * rg, axis=0, keepdims=True)

            @pl.when(i == 0)
            def _():
                dgs_ref[:, lo:hi] = dgs

            @pl.when(i > 0)
            def _():
                dgs_ref[:, lo:hi] += dgs

    row, zlo, zhi, vec = _mix_specs(tb)
    return pl.pallas_call(
        body, name="mix_bwd", grid=(t // tb,),
        in_specs=[pl.BlockSpec((tb, 4096), lambda i: (i, 0)), row, row, zlo, zhi, vec, vec],
        out_specs=[row, row, row, vec, vec],
        out_shape=[jax.ShapeDtypeStruct((t, 2048), F32), jax.ShapeDtypeStruct((t, 2048), F32),
                   jax.ShapeDtypeStruct((t, 2048), BF16), jax.ShapeDtypeStruct((1, 2048), F32),
                   jax.ShapeDtypeStruct((1, 2048), F32)],
        compiler_params=_cp(("arbitrary",)))(dmix, attn, y, proj, proj, g_attn, g_ssd)


def _adamw(w, g, m, v, name):
    r, c = w.shape
    tb = _rows(r, 256)
    c1 = 1.0 - ADAM_B1 ** ADAM_STEP
    c2 = 1.0 - ADAM_B2 ** ADAM_STEP

    def body(w_ref, g_ref, m_ref, v_ref, d_ref, m2_ref, v2_ref):
        gv = g_ref[...]
        m2 = ADAM_B1 * m_ref[...] + (1.0 - ADAM_B1) * gv
        v2 = ADAM_B2 * v_ref[...] + (1.0 - ADAM_B2) * (gv * gv)
        d_ref[...] = -ADAM_LR * ((m2 / c1) / (jnp.sqrt(v2 / c2) + ADAM_EPS) + ADAM_WD * w_ref[...])
        m2_ref[...] = m2
        v2_ref[...] = v2

    blk = pl.BlockSpec((tb, c), lambda i: (i, 0))
    shp = jax.ShapeDtypeStruct((r, c), F32)
    return pl.pallas_call(body, name=name, grid=(r // tb,), in_specs=[blk] * 4, out_specs=[blk] * 3,
                          out_shape=[shp] * 3, compiler_params=_cp(("parallel",)))(w, g, m, v)


def _adamw_halves(w, mine, theirs, m, v, pos, name, cols=False):
    r, c = w.shape
    h = r if cols else r // 2
    tb = _rows(h, 128)
    nh = h // tb
    c1 = 1.0 - ADAM_B1 ** ADAM_STEP
    c2 = 1.0 - ADAM_B2 ** ADAM_STEP

    def body(pos_ref, w_ref, a_ref, b_ref, m_ref, v_ref, g_ref, d_ref, m2_ref, v2_ref):
        which = pl.program_id(1) if cols else pl.program_id(0) // nh
        gv = jnp.where(which == pos_ref[0], a_ref[...], b_ref[...])
        m2 = ADAM_B1 * m_ref[...] + (1.0 - ADAM_B1) * gv
        v2 = ADAM_B2 * v_ref[...] + (1.0 - ADAM_B2) * (gv * gv)
        g_ref[...] = gv
        d_ref[...] = -ADAM_LR * ((m2 / c1) / (jnp.sqrt(v2 / c2) + ADAM_EPS) + ADAM_WD * w_ref[...])
        m2_ref[...] = m2
        v2_ref[...] = v2

    if cols:
        full = pl.BlockSpec((tb, c // 2), lambda i, j, pref: (i, j))
        mine_spec = theirs_spec = pl.BlockSpec((tb, c // 2), lambda i, j, pref: (i, 0))
        grid = (nh, 2)
    else:
        full = pl.BlockSpec((tb, c), lambda i, pref: (i, 0))
        mine_spec = pl.BlockSpec((tb, c), lambda i, pref: (jnp.where(i // nh == pref[0], i % nh,
                                                                     jnp.where(pref[0] == 0, nh - 1, 0)), 0))
        theirs_spec = pl.BlockSpec((tb, c), lambda i, pref: (jnp.where(i // nh != pref[0], i % nh,
                                                                       jnp.where(pref[0] == 0, 0, nh - 1)), 0))
        grid = (r // tb,)
    shp = jax.ShapeDtypeStruct((r, c), F32)
    grid_spec = pltpu.PrefetchScalarGridSpec(num_scalar_prefetch=1, grid=grid,
                                             in_specs=[full, mine_spec, theirs_spec, full, full],
                                             out_specs=[full] * 4)
    return pl.pallas_call(body, name=name, grid_spec=grid_spec, out_shape=[shp] * 4,
                          compiler_params=_cp(("parallel",) * len(grid)))(pos, w, mine, theirs, m, v)


def _sum_own_half(g4, recv, pos, name, cols=False):
    _, r, c = g4.shape
    h, c = (r, c // 2) if cols else (r // 2, c)
    tb = _rows(h, 128)
    nh = h // tb

    def slot(j, pref):
        return (pref[1] + 1 + j) % N_CHIPS

    if cols:
        own = lambda j, i, pref: (slot(j, pref), i, pref[0])
    else:
        own = lambda j, i, pref: (slot(j, pref), pref[0] * nh + i, 0)
    same = lambda j, i, pref: (slot(j, pref), i, 0)

    def body(pos_ref, a_ref, b_ref, o_ref):
        o_ref[...] = (a_ref[...] + b_ref[...]).astype(BF16)

    grid_spec = pltpu.PrefetchScalarGridSpec(
        num_scalar_prefetch=1, grid=(N_CHIPS - 1, nh),
        in_specs=[pl.BlockSpec((1, tb, c), own), pl.BlockSpec((1, tb, c), same)],
        out_specs=pl.BlockSpec((1, tb, c), same))
    return pl.pallas_call(body, name=name, grid_spec=grid_spec,
                          out_shape=jax.ShapeDtypeStruct((N_CHIPS, h, c), BF16),
                          compiler_params=_cp(("parallel", "parallel")))(pos, g4, recv)


def _sum_chips(g4, recv, parts, pos, name, cols=False):
    _, r, c = g4.shape
    h, c = (r, c // 2) if cols else (r // 2, c)
    tb = _rows(h, 128)
    nh = h // tb
    own = (lambda i, pref: (pref[1], i, pref[0])) if cols else (lambda i, pref: (pref[1], pref[0] * nh + i, 0))

    def body(pos_ref, a_ref, b_ref, p_ref, o_ref):
        own = a_ref[0] + b_ref[0]
        o_ref[...] = ((own + p_ref[0].astype(F32)) + p_ref[1].astype(F32)) + p_ref[2].astype(F32)

    grid_spec = pltpu.PrefetchScalarGridSpec(
        num_scalar_prefetch=1, grid=(nh,),
        in_specs=[pl.BlockSpec((1, tb, c), own),
                  pl.BlockSpec((1, tb, c), lambda i, pref: (pref[1], i, 0)),
                  pl.BlockSpec((3, tb, c), lambda i, pref: (0, i, 0))],
        out_specs=pl.BlockSpec((tb, c), lambda i, pref: (i, 0)))
    return pl.pallas_call(body, name=name, grid_spec=grid_spec, out_shape=jax.ShapeDtypeStruct((h, c), F32),
                          compiler_params=_cp(("parallel",)))(pos, g4, recv, parts)


def _me():
    return lax.axis_index("x"), lax.axis_index("y"), lax.axis_index("c")


def _flip(v, bit):
    return (1 - v) if bit else v


CHIP_FLIPS = [(1, 0), (0, 1), (1, 1)]


def _allgather_weights(shards, after, cols=False):
    n = len(shards)

    def body(*refs):
        ins, outs, token = refs[:n], refs[n + 1:2 * n + 1], refs[2 * n + 1]
        send_sems, recv_sems = refs[2 * n + 2:]
        x, y, c = _me()
        chip = 2 * x + y
        sib = (x, y, 1 - c)

        def remote(src, dst, k, to):
            return pltpu.make_async_remote_copy(src_ref=src, dst_ref=dst, send_sem=send_sems.at[k],
                                                recv_sem=recv_sems.at[k], device_id=to, device_id_type=MESH)

        def half(ref, which):
            if cols:
                h = ref.shape[1] // 2
                return ref.at[:, pl.ds(which * h, h)]
            h = ref.shape[0] // 2
            return ref.at[pl.ds(which * h, h)]

        sends = []
        for t in range(n):
            for k, (fx, fy) in enumerate(CHIP_FLIPS):
                cp = remote(half(ins[t], c), half(outs[t].at[chip], c), 6 * t + k, (_flip(x, fx), _flip(y, fy), c))
                cp.start()
                sends.append(cp)
        for t in range(n):
            for k, (fx, fy) in enumerate(CHIP_FLIPS):
                landed = half(outs[t].at[2 * _flip(x, fx) + _flip(y, fy)], c)
                remote(landed, landed, 6 * t + k, (x, y, c)).wait_recv()
                fw = remote(landed, landed, 6 * t + 3 + k, sib)
                fw.start()
                sends.append(fw)
        for t in range(n):
            for k, (fx, fy) in enumerate(CHIP_FLIPS):
                got = half(outs[t].at[2 * _flip(x, fx) + _flip(y, fy)], 1 - c)
                remote(got, got, 6 * t + 3 + k, (x, y, c)).wait_recv()
        for cp in sends:
            cp.wait_send()
        token[...] = jnp.zeros_like(token)

    outs = pl.pallas_call(
        body, name="allgather_weights", in_specs=[HBM_SPEC] * n + [pl.BlockSpec(memory_space=pl.ANY)],
        out_specs=[HBM_SPEC] * n + [pl.BlockSpec(memory_space=pltpu.VMEM)],
        out_shape=[jax.ShapeDtypeStruct((N_CHIPS,) + s.shape, s.dtype) for s in shards] + [TOKEN],
        scratch_shapes=[pltpu.SemaphoreType.DMA((6 * n,)), pltpu.SemaphoreType.DMA((6 * n,))],
        compiler_params=pltpu.CompilerParams(has_side_effects=True))(*shards, after)
    return list(outs[:n]), outs[n]


def _share_halves(ghs, name):
    n = len(ghs)

    def body(*refs):
        ins, outs = refs[:n], refs[n:2 * n]
        send_sems, recv_sems = refs[2 * n:]
        x, y, c = _me()
        cps = []
        for t in range(n):
            cp = pltpu.make_async_remote_copy(
                src_ref=ins[t], dst_ref=outs[t], send_sem=send_sems.at[t], recv_sem=recv_sems.at[t],
                device_id=(x, y, 1 - c), device_id_type=MESH)
            cp.start()
            cps.append(cp)
        for cp in cps:
            cp.wait()

    return pl.pallas_call(
        body, name=name, in_specs=[HBM_SPEC] * n, out_specs=[HBM_SPEC] * n,
        out_shape=[jax.ShapeDtypeStruct(g.shape, g.dtype) for g in ghs],
        scratch_shapes=[pltpu.SemaphoreType.DMA((n,)), pltpu.SemaphoreType.DMA((n,))],
        compiler_params=pltpu.CompilerParams(has_side_effects=True))(*ghs)


SEM_SPEC = pl.BlockSpec(memory_space=pltpu.SEMAPHORE)
ANY_SPEC = pl.BlockSpec(memory_space=pl.ANY)
DATAFLOW = pltpu.SideEffectType.DATAFLOW_SIDE_EFFECTING


def _in_hbm(a):
    return pltpu.with_memory_space_constraint(a, pltpu.HBM)


def _push_start(srcs, land_shapes, route, peers, name):
    n, npeer = len(srcs), len(peers)
    lands = [lax.empty(shp, s.dtype) for shp, s in zip(land_shapes, srcs)]

    def body(*refs):
        ins, lnd = refs[:n], refs[n:2 * n]
        send_sems, recv_sems = refs[2 * n], refs[2 * n + 1]
        token = refs[-1]
        x, y, c = _me()
        for t in range(n):
            for k, (fx, fy, fc) in enumerate(peers):
                src, dst = route(ins[t], lnd[t], k, x, y, c)
                pltpu.make_async_remote_copy(
                    src_ref=src, dst_ref=dst, send_sem=send_sems.at[npeer * t + k],
                    recv_sem=recv_sems.at[npeer * t + k],
                    device_id=(_flip(x, fx), _flip(y, fy), _flip(c, fc)), device_id_type=MESH).start()
        token[...] = jnp.zeros_like(token)

    bufs = [_in_hbm(a) for a in list(srcs) + lands]
    outs = pl.pallas_call(
        body, name=name,
        out_shape=(pltpu.SemaphoreType.DMA((npeer * n,)), pltpu.SemaphoreType.DMA((npeer * n,)),
                   *[pltpu.HBM(b.shape, b.dtype) for b in bufs], TOKEN),
        in_specs=[HBM_SPEC] * (2 * n),
        out_specs=(SEM_SPEC, SEM_SPEC, *[HBM_SPEC] * (2 * n), pl.BlockSpec(memory_space=pltpu.VMEM)),
        input_output_aliases={i: 2 + i for i in range(2 * n)},
        compiler_params=pltpu.CompilerParams(has_side_effects=DATAFLOW))(*bufs)
    return outs[0], outs[1], list(outs[2:2 + n]), list(outs[2 + n:2 + 2 * n]), outs[-1]


def _push_wait(send_sems, recv_sems, srcs, lands, after, route, peers, name):
    n, npeer = len(srcs), len(peers)

    def body(*refs):
        ins, lnd = refs[:n], refs[n:2 * n]
        ssem, rsem = refs[2 * n], refs[2 * n + 1]
        x, y, c = _me()
        for t in range(n):
            for k, (fx, fy, fc) in enumerate(peers):
                src, dst = route(ins[t], lnd[t], k, x, y, c)
                cp = pltpu.make_async_remote_copy(
                    src_ref=src, dst_ref=dst, send_sem=ssem.at[npeer * t + k], recv_sem=rsem.at[npeer * t + k],
                    device_id=(_flip(x, fx), _flip(y, fy), _flip(c, fc)), device_id_type=MESH)
                cp.wait_send()
                cp.wait_recv()

    bufs = list(srcs) + list(lands)
    outs = pl.pallas_call(
        body, name=name, out_shape=tuple(pltpu.HBM(b.shape, b.dtype) for b in bufs),
        in_specs=[HBM_SPEC] * (2 * n) + [SEM_SPEC, SEM_SPEC, ANY_SPEC], out_specs=tuple([HBM_SPEC] * (2 * n)),
        input_output_aliases={i: i for i in range(2 * n)},
        compiler_params=pltpu.CompilerParams(has_side_effects=DATAFLOW))(*bufs, send_sems, recv_sems, after)
    return list(outs[:n]), list(outs[n:])


OTHER_CHIPS = [(fx, fy, 0) for fx, fy in CHIP_FLIPS]
SIBLING = [(0, 0, 1)]


def _route_gather(src, land, k, x, y, c):
    return src, land.at[2 * x + y]


def _route_gather_wait(src, land, k, x, y, c):
    fx, fy = CHIP_FLIPS[k]
    return src, land.at[2 * _flip(x, fx) + _flip(y, fy)]


def _route_scatter(src, land, k, x, y, c):
    fx, fy = CHIP_FLIPS[k]
    return src.at[2 * _flip(x, fx) + _flip(y, fy)], land.at[k]


def _route_exchange(src, land, k, x, y, c):
    h = land.shape[1]
    return src.at[:, pl.ds((1 - c) * h, h)], land


def _route_exchange_cols(src, land, k, x, y, c):
    h = land.shape[2]
    return src.at[:, :, pl.ds((1 - c) * h, h)], land


def _allreduce_small(v):
    r = v.shape[0]

    def body(v_ref, o_ref, buf, send_sems, recv_sems):
        x, y, c = _me()
        me = 4 * x + 2 * y + c
        buf[0] = v_ref[...]
        cps = []
        for k in range(1, 8):
            kx, ky, kc = (k >> 2) & 1, (k >> 1) & 1, k & 1
            cp = pltpu.make_async_remote_copy(
                src_ref=v_ref, dst_ref=buf.at[k], send_sem=send_sems.at[k - 1], recv_sem=recv_sems.at[k - 1],
                device_id=(_flip(x, kx), _flip(y, ky), _flip(c, kc)), device_id_type=MESH)
            cp.start()
            cps.append(cp)
        for cp in cps:
            cp.wait()
        acc = buf[me]
        for d in range(1, 8):
            acc = acc + buf[jnp.bitwise_xor(me, d)]
        o_ref[...] = acc

    vm = pl.BlockSpec(memory_space=pltpu.VMEM)
    return pl.pallas_call(
        body, name="allreduce_small", in_specs=[vm], out_specs=vm, out_shape=jax.ShapeDtypeStruct(v.shape, F32),
        scratch_shapes=[pltpu.VMEM((8, r, LANES), F32), pltpu.SemaphoreType.DMA((7,)),
                        pltpu.SemaphoreType.DMA((7,))],
        compiler_params=pltpu.CompilerParams(has_side_effects=True, vmem_limit_bytes=VMEM_LIMIT))(v)


def _grad_exchange_start(g4, tag, cols=False):
    land = (N_CHIPS, g4.shape[1], g4.shape[2] // 2) if cols else (N_CHIPS, g4.shape[1] // 2, g4.shape[2])
    route = _route_exchange_cols if cols else _route_exchange
    send_sems, recv_sems, srcs, lands, token = _push_start(
        [g4], [land], route, SIBLING, name="grad_exchange_start_" + tag)
    return (send_sems, recv_sems, srcs, lands, tag, cols), token


def _grad_scatter_start(state, pos, after):
    send_sems, recv_sems, srcs, lands, tag, cols = state
    route = _route_exchange_cols if cols else _route_exchange
    (g4,), (recv,) = _push_wait(send_sems, recv_sems, srcs, lands, after, route, SIBLING,
                                name="grad_exchange_wait_" + tag)
    return _grad_pair_scatter(g4, recv, pos, tag, cols)


def _grad_pair_scatter(g4, recv, pos, tag, cols=False):
    p16 = _sum_own_half(g4, recv, pos, name="grad_sum_pair_" + tag, cols=cols)
    send_sems, recv_sems, srcs, lands, token = _push_start(
        [p16], [(3,) + p16.shape[1:]], _route_scatter, OTHER_CHIPS, name="grad_scatter_start_" + tag)
    return (g4, recv, send_sems, recv_sems, srcs, lands, tag, cols), token


def _grad_reduce_finish(state, pos, after):
    g4, recv, send_sems, recv_sems, srcs, lands, tag, cols = state
    parts = _push_wait(send_sems, recv_sems, srcs, lands, after, _route_scatter, OTHER_CHIPS,
                       name="grad_scatter_wait_" + tag)[1][0]
    mine = _sum_chips(g4, recv, parts, pos, name="grad_sum_chips_" + tag, cols=cols)
    return mine, _share_halves([mine], name="grad_share_halves_" + tag)[0]


def _local_step(x, tgt, p, w_in_t, w_in_dt, hooks):
    t = x.shape[0]
    tables = _rope_tables(t)
    sinks = p['sinks'].reshape(N_Q_HEADS)

    def told(name, value):
        return tuple(hooks.grad_ready(name, value))

    xn = _rmsnorm_fwd(x, p['norm_mix'], "norm_mix_fwd", deps=hooks.first_deps)
    proj = _matmul(xn, w_in_t, mode='nt', name="in_proj", n_limit=MAIN_WIDTH)
    dt_raw = _matmul(xn, w_in_dt, mode='nt', name="in_proj_dt")[:, :SSD_HEADS]
    attn = _attn_fwd(proj, sinks, tables)
    conv_b = p['ssd_conv_b']
    xbc, xbc_pre = _conv_silu_fwd(proj, p['ssd_conv_w'], conv_b, col0=O_XBC, width=CONV_CH, name="ssd_conv_fwd")
    sp = _ssd_params(dt_raw, p['dt_bias'].reshape(-1), p['a_log'].reshape(-1), p['ssd_d'].reshape(-1))
    y, states = _ssd_fwd(xbc, sp)
    mix = _mix_fwd(attn, y, proj, p['attn_out_norm'], p['ssd_norm'])
    w_out = hooks.weight('w_out', mix)
    h1 = _matmul(mix, w_out, mode='nn', name="out_proj", add=x)
    hn = _rmsnorm_fwd(h1, p['norm_ffn'], "norm_ffn_fwd")
    w_up = hooks.weight('w_up', hn)
    u0 = _matmul(hn, w_up, mode='nn', name="ffn_up", b_owner=True, tn=1408)
    a, u = _ffn_act_fwd(u0, p['ffn_conv_w'], p['ffn_conv_b'])
    w_down = hooks.weight('w_down', a)
    h2 = _matmul(a, w_down, mode='nn', name="ffn_down", add=h1, tk=2816)
    loss, dh2, dh2_16, g_norm_final = _final_loss(h2, p['norm_final'].reshape(1, D_MODEL), tgt)

    g = {}
    da = _matmul(dh2_16, w_down, mode='nt', name="ffn_down_dx", out_dtype=BF16, tn=1408)
    g['w_down'] = _matmul(a, dh2_16, mode='tn', name="ffn_down_dw", tm=1408)
    dep = told('w_down', g['w_down'])
    du0, dcw, dcb = _ffn_act_bwd(u0, u, p['ffn_conv_w'], da)
    g['ffn_conv_w'] = dcw.transpose(1, 0, 2).reshape(FFN_CONV, 2 * D_FF)
    g['ffn_conv_b'] = dcb.transpose(1, 0, 2).reshape(1, 2 * D_FF)
    g['w_up'] = _matmul(hn, du0, mode='tn', name="ffn_up_dw", deps=dep, b_halves=True, owner_major=True,
                        tn=1408)
    dep = told('w_up', g['w_up'])
    dhn = _matmul(du0, w_up, mode='nt', name="ffn_up_dx", out_dtype=BF16, deps=dep, a_halves=True,
                  b_owner=True, tk=2816)
    dh1, dh1_16, g['norm_ffn'] = _rmsnorm_bwd(h1, p['norm_ffn'], dhn, dh2, "norm_ffn_bwd")

    g['w_out'] = _matmul(mix, dh1_16, mode='tn', name="out_proj_dw")
    dep = told('w_out', g['w_out'])
    dmix = _matmul(dh1_16, w_out, mode='nt', name="out_proj_dx", out_dtype=BF16, deps=dep)
    dattn, dy, dz, g['attn_out_norm'], g['ssd_norm'] = _mix_bwd(dmix, attn, y, proj, p['attn_out_norm'],
                                                                p['ssd_norm'])
    dq, dk, dv, dsink = _attn_bwd(proj, sinks, tables, dattn)
    g['sinks'] = dsink[:, :, 0].reshape(1, N_Q_HEADS)
    dxs, dbm, dcm, ddt8, dpar = _ssd_bwd(xbc, xbc_pre, sp, states, dy)
    dpar = dpar[:, :, ::SSD_HEAD_DIM]
    g['dt_bias'] = dpar[:, 0, :].reshape(1, SSD_HEADS)
    g['a_log'] = dpar[:, 1, :].reshape(1, SSD_HEADS)
    g['ssd_d'] = dpar[:, 2, :].reshape(1, SSD_HEADS)
    dxbc, g['ssd_conv_w'], g['ssd_conv_b'] = _ssd_conv_bwd(proj, p['ssd_conv_w'], dxs, dbm, dcm, col0=O_XBC,
                                                           name="ssd_conv_bwd")
    dproj = jnp.concatenate([dq, dk, dv, dz, dxbc], axis=1)
    ddt = ddt8.transpose(2, 0, 1).reshape(t, SSD_HEADS)
    ddt_pad = jnp.pad(ddt, ((0, 0), (0, LANES - SSD_HEADS))).astype(BF16)
    g['w_in'] = (_matmul(dproj, xn, mode='tn', name="in_proj_dw", m_rows=IN_PROJ_WIDTH),
                 _matmul(ddt_pad, xn, mode='tn', name="in_proj_dt_dw"))
    dep = told('w_in', g['w_in'])
    dxn_dt = _matmul(ddt_pad, w_in_dt, mode='nn', name="in_proj_dt_dx", deps=dep)
    dxn = _matmul(dproj, w_in_t, mode='nn', name="in_proj_dx", out_dtype=BF16, add=dxn_dt, k_limit=MAIN_WIDTH,
                  tk=2304)
    dx, _, g['norm_mix'] = _rmsnorm_bwd(x, p['norm_mix'], dxn, dh1, "norm_mix_bwd")
    g['norm_final'] = g_norm_final
    return loss, dx, g


def _pack(arrs):
    flat = jnp.concatenate([a.reshape(-1) for a in arrs])
    n = flat.shape[0]
    rows = -(-n // LANES)
    rows = -(-rows // 8) * 8
    return jnp.pad(flat, (0, rows * LANES - n)).reshape(rows, LANES)


def _unpack(packed, shapes):
    flat = packed.reshape(-1)
    out, off = [], 0
    for s in shapes:
        n = 1
        for d in s:
            n *= d
        out.append(flat[off:off + n].reshape(s))
        off += n
    return out


class _StepHooks:
    def __init__(self, first_deps, weight, grad_ready):
        self.first_deps = first_deps
        self.weight = weight
        self.grad_ready = grad_ready


def kernel(x, norm_mix, w_in, sinks, attn_out_norm, ssd_conv_w, ssd_conv_b, dt_bias, a_log, ssd_d, ssd_norm, w_out, norm_ffn, w_up, ffn_conv_w, ffn_conv_b, w_down, norm_final, loss_target, m_norm_mix, m_w_in, m_sinks, m_attn_out_norm, m_ssd_conv_w, m_ssd_conv_b, m_dt_bias, m_a_log, m_ssd_d, m_ssd_norm, m_w_out, m_norm_ffn, m_w_up, m_ffn_conv_w, m_ffn_conv_b, m_w_down, m_norm_final, v_norm_mix, v_w_in, v_sinks, v_attn_out_norm, v_ssd_conv_w, v_ssd_conv_b, v_dt_bias, v_a_log, v_ssd_d, v_ssd_norm, v_w_out, v_norm_ffn, v_w_up, v_ffn_conv_w, v_ffn_conv_b, v_w_down, v_norm_final):
    args = dict(locals())
    w = {n: args[n] for n in WEIGHTS}
    m = {n: args['m_' + n] for n in WEIGHTS}
    v = {n: args['v_' + n] for n in WEIGHTS}
    xi, yi, ci = _me()
    chip = 2 * xi + yi
    pos = jnp.stack([ci, chip]).astype(jnp.int32)

    def place(shard, full_cols):
        z = jnp.zeros((shard.shape[0], full_cols), F32)
        return lax.dynamic_update_slice(z, shard * 0.5, (0, chip * shard.shape[1]))

    conv_pack = _pack([place(ssd_conv_w[0], CONV_CH), place(ffn_conv_w[0], 2 * D_FF)])
    conv_full = _allreduce_small(conv_pack)
    ssd_conv_w_full, ffn_conv_w_full = _unpack(conv_full, [(SSD_CONV, CONV_CH), (FFN_CONV, 2 * D_FF)])

    w_in_t, m_in_t, v_in_t = (jnp.transpose(a[0]) for a in (w_in, m_w_in, v_w_in))
    in_shard = w_in_t.astype(BF16)
    (gathered,), order = _allgather_weights([in_shard], conv_full, cols=True)
    full_in_t = lax.dynamic_update_slice(gathered, in_shard[None], (chip, 0, 0)).reshape(IN_PROJ_WIDTH, D_MODEL)
    w_in_dt = jnp.pad(full_in_t[MAIN_WIDTH:], ((0, LANES - SSD_HEADS), (0, 0)))
    gathers = {}
    order = order[:1, :1]
    for n, shard in (('w_out', w_out[0]), ('w_up', w_up[0]), ('w_down', w_down[0])):
        shard = (shard + order).astype(BF16)
        gathers[n] = _push_start([shard], [(N_CHIPS,) + shard.shape], _route_gather, OTHER_CHIPS,
                                 name="gather_start_" + n)
        order = gathers[n][4][:1, :1]
    first_deps = [gathers['w_down'][4]]

    def weight(name, after):
        send_sems, recv_sems, srcs, lands, _ = gathers[name]
        (own,), (got,) = _push_wait(send_sems, recv_sems, srcs, lands, after, _route_gather_wait, OTHER_CHIPS,
                                    name="gather_wait_" + name)
        whole = lax.dynamic_update_slice(got, own[None], (chip, 0, 0))
        return whole if name == 'w_up' else whole.reshape(-1, D_MODEL)

    reductions, exchanging = {}, {}

    def flush(after):
        tokens = []
        for prev in list(exchanging):
            reductions[prev], token = _grad_scatter_start(exchanging.pop(prev), pos, after)
            tokens.append(token)
        return tokens

    def grad_ready(name, value):
        if name == 'w_in':
            main, dtp = value
            value = lax.dynamic_update_slice(main, dtp[:SSD_HEADS], (MAIN_WIDTH, 0))
        g4 = value if value.ndim == 3 else value.reshape(N_CHIPS, -1, value.shape[1])
        tokens = flush(g4)
        exchanging[name], token = _grad_exchange_start(g4, name, cols=(name == 'w_in'))
        return tokens + [token]

    small = {
        'norm_mix': norm_mix, 'sinks': sinks, 'attn_out_norm': attn_out_norm, 'ssd_conv_w': ssd_conv_w_full,
        'ssd_conv_b': ssd_conv_b, 'dt_bias': dt_bias, 'a_log': a_log, 'ssd_d': ssd_d, 'ssd_norm': ssd_norm,
        'norm_ffn': norm_ffn, 'ffn_conv_w': ffn_conv_w_full, 'ffn_conv_b': ffn_conv_b, 'norm_final': norm_final,
    }
    loss, dx, g = _local_step(x[0], loss_target[0], small, full_in_t, w_in_dt,
                              _StepHooks(tuple(first_deps), weight, grad_ready))

    small_names = [n for n in WEIGHTS if n not in BIG]
    small_g = [loss[:, :1]] + [g[n] for n in small_names]
    small_shapes = [(1, 1)] + [tuple(a.shape) for a in small_g[1:]]
    reduced = _allreduce_small(_pack(small_g))
    started = flush(reduced)[-1]
    red = _unpack(reduced, small_shapes)
    loss_out = red[0].reshape(())
    gsm = dict(zip(small_names, red[1:]))
    gsm['ssd_conv_w'] = lax.dynamic_slice(gsm['ssd_conv_w'], (0, chip * ssd_conv_w.shape[2]),
                                          (SSD_CONV, ssd_conv_w.shape[2]))
    gsm['ffn_conv_w'] = lax.dynamic_slice(gsm['ffn_conv_w'], (0, chip * ffn_conv_w.shape[2]),
                                          (FFN_CONV, ffn_conv_w.shape[2]))

    grads, deltas, new_m, new_v = {}, {}, {}, {}
    after = started
    for n in ('w_down', 'w_up', 'w_out', 'w_in'):
        mine, theirs = _grad_reduce_finish(reductions[n], pos, after)
        if n == 'w_in':
            outs = _adamw_halves(w_in_t, mine, theirs, m_in_t, v_in_t, pos, name="adamw_" + n, cols=True)
            outs = [jnp.transpose(o) for o in outs]
        else:
            outs = _adamw_halves(w[n][0], mine, theirs, m[n][0], v[n][0], pos, name="adamw_" + n)
        after = outs[1]
        grads[n], deltas[n], new_m[n], new_v[n] = [o[None] for o in outs]
    shapes = [tuple(w[n].shape) for n in small_names]
    gp = _pack([gsm[n] for n in small_names])
    d, m2, v2 = _adamw(_pack([w[n] for n in small_names]), gp, _pack([m[n] for n in small_names]),
                       _pack([v[n] for n in small_names]), name="adamw_small")
    for n, gg, dd, mm, vv in zip(small_names, _unpack(gp, shapes), _unpack(d, shapes), _unpack(m2, shapes),
                                 _unpack(v2, shapes)):
        grads[n], deltas[n], new_m[n], new_v[n] = gg, dd, mm, vv

    return (loss_out, dx[None], *[grads[n] for n in WEIGHTS], *[deltas[n] for n in WEIGHTS],
            *[new_m[n] for n in WEIGHTS], *[new_v[n] for n in WEIGHTS])
```

```python
import functools

import jax
import jax.numpy as jnp
from jax import lax
from jax.experimental import pallas as pl
from jax.experimental.pallas import tpu as pltpu

F32 = jnp.float32
BF16 = jnp.bfloat16

D_MODEL = 2048
N_Q_HEADS = 32
N_KV_HEADS = 8
HEAD_DIM = 64
WINDOW = 128
ATTN_BLOCK = 128
ROT_DIM = 16
ROPE_THETA = 500000.0
SSD_HEADS = 32
SSD_HEAD_DIM = 64
SSD_INNER = 2048
SSD_GROUPS = 8
SSD_STATE = 128
SSD_CONV = 4
SSD_CHUNK = 128
ATTN_WIDTH = 2048
KV_WIDTH = 512
BC_WIDTH = 1024
CONV_CH = 4096
IN_PROJ_WIDTH = 9248
MAIN_WIDTH = 9216
D_FF = 5632
FFN_CONV = 3
EPS = 1e-6
O_Q, O_K, O_V, O_Z, O_XBC, O_DT = 0, 2048, 2560, 3072, 5120, 9216

ADAM_LR = 0.001
ADAM_B1 = 0.9
ADAM_B2 = 0.999
ADAM_EPS = 1e-08
ADAM_WD = 0.01
ADAM_STEP = 10

N_CHIPS = 4
NEG = -1e30
LANES = 128
VMEM_LIMIT = 48 * 1024 * 1024
MESH = pl.DeviceIdType.MESH
HBM_SPEC = pl.BlockSpec(memory_space=pltpu.HBM)
TOKEN = jax.ShapeDtypeStruct((8, LANES), F32)

WEIGHTS = ['norm_mix', 'w_in', 'sinks', 'attn_out_norm', 'ssd_conv_w', 'ssd_conv_b', 'dt_bias', 'a_log', 'ssd_d',
           'ssd_norm', 'w_out', 'norm_ffn', 'w_up', 'ffn_conv_w', 'ffn_conv_b', 'w_down', 'norm_final']
BIG = ['w_in', 'w_out', 'w_up', 'w_down']


def _cp(sem=None, vmem=VMEM_LIMIT):
    kw = {'vmem_limit_bytes': vmem}
    if sem is not None:
        kw['dimension_semantics'] = sem
    return pltpu.CompilerParams(**kw)


def _tile(n, pref):
    if n <= pref:
        return n
    t = (pref // LANES) * LANES
    while t > LANES and n % t:
        t -= LANES
    assert n % t == 0, (n, pref)
    return t


def _rows(n, pref):
    t = min(n, pref)
    while n % t:
        t -= 8
    if 4 * t < pref:
        t = pref
        while n % t:
            t += 8
    return t


def _iota(shape, dim):
    return lax.broadcasted_iota(jnp.int32, shape, dim)


def _dot(a, b, mode='nn'):
    dn = {'nn': (((1,), (0,)), ((), ())), 'nt': (((1,), (1,)), ((), ())), 'tn': (((0,), (0,)), ((), ()))}[mode]
    return lax.dot_general(a.astype(BF16), b.astype(BF16), dn, preferred_element_type=F32)


def _dot_exact(a, b):
    return lax.dot_general(a, b, (((1,), (0,)), ((), ())), precision=lax.Precision.HIGHEST,
                           preferred_element_type=F32)


def _sigmoid(x):
    return 1.0 / (1.0 + jnp.exp(-x))


def _softplus(x):
    return jnp.maximum(x, 0.0) + jnp.log(1.0 + jnp.exp(-jnp.abs(x)))


def _matmul(a, b, *, mode, name, out_dtype=F32, add=None, deps=(), tm=1024, tn=1024, tk=2048,
            a_halves=False, b_halves=False, b_owner=False, owner_major=False, n_limit=None, k_limit=None,
            m_rows=None):
    ash, bsh = (a.shape[1:] if a_halves else a.shape), (b.shape[1:] if (b_halves or b_owner) else b.shape)
    if mode == 'nn':
        (m, k), (k2, n) = ash, bsh
    elif mode == 'nt':
        (m, k), (n, k2) = ash, bsh
    else:
        (k, m), (k2, n) = ash, bsh
    if n_limit is not None:
        assert mode == 'nt' and n_limit <= n
        n = n_limit
    if k_limit is not None:
        assert mode == 'nn' and k_limit <= k2
        k2 = k_limit
    if a_halves:
        assert mode == 'nt'
        k = 2 * k
    if b_halves:
        assert mode == 'tn'
        n = 2 * n
    if b_owner:
        assert mode in ('nn', 'nt')
        if mode == 'nn':
            n = 4 * n
        else:
            k2 = 4 * k2
    assert k == k2, (a.shape, b.shape, mode)
    tm = _tile(m, tm)
    tn = _tile(n // 4 if (owner_major or (b_owner and mode == 'nn')) else (n // 2 if b_halves else n), tn)
    tk = _tile(k // 4 if (b_owner and mode == 'nt') else (k // 2 if a_halves else k), tk)
    nk = k // tk
    has_add = add is not None
    assert not (has_add and owner_major)

    def body(*refs):
        a_ref, b_ref = refs[:2]
        add_ref = refs[2] if has_add else None

        def finish(r, o_ref):
            if has_add:
                r = r + add_ref[...].astype(F32)
            o_ref[...] = r.astype(out_dtype)

        if nk == 1:
            finish(_dot(a_ref[...], b_ref[...], mode), refs[-1])
            return
        o_ref, acc = refs[-2:]
        kk = pl.program_id(2)

        @pl.when(kk == 0)
        def _():
            acc[...] = _dot(a_ref[...], b_ref[...], mode)

        @pl.when((kk > 0) & (kk < nk - 1))
        def _():
            acc[...] += _dot(a_ref[...], b_ref[...], mode)

        @pl.when(kk == nk - 1)
        def _():
            finish(acc[...] + _dot(a_ref[...], b_ref[...], mode), o_ref)

    if mode == 'tn':
        a_spec = pl.BlockSpec((tk, tm), lambda i, j, kk: (kk, i))
    elif a_halves:
        nkh = nk // 2
        a_spec = pl.BlockSpec((None, tm, tk), lambda i, j, kk: (kk // nkh, i, kk % nkh))
    else:
        a_spec = pl.BlockSpec((tm, tk), lambda i, j, kk: (i, kk))
    if mode == 'nt' and b_owner:
        nkq = nk // 4
        b_spec = pl.BlockSpec((None, tn, tk), lambda i, j, kk: (kk // nkq, j, kk % nkq))
    elif mode == 'nt':
        b_spec = pl.BlockSpec((tn, tk), lambda i, j, kk: (j, kk))
    elif b_owner:
        njq = (n // 4) // tn
        b_spec = pl.BlockSpec((None, tk, tn), lambda i, j, kk: (j // njq, kk, j % njq))
    elif b_halves:
        njh = (n // 2) // tn
        b_spec = pl.BlockSpec((None, tk, tn), lambda i, j, kk: (j // njh, kk, j % njh))
    else:
        b_spec = pl.BlockSpec((tk, tn), lambda i, j, kk: (kk, j))
    if owner_major:
        njo = (n // 4) // tn
        o_spec = pl.BlockSpec((None, tm, tn), lambda i, j, kk: (j // njo, i, j % njo))
        out_shape = jax.ShapeDtypeStruct((N_CHIPS, m, n // 4), out_dtype)
    else:
        o_spec = pl.BlockSpec((tm, tn), lambda i, j, kk: (i, j))
        out_shape = jax.ShapeDtypeStruct((m if m_rows is None else m_rows, n), out_dtype)
    dep_spec = pl.BlockSpec((8, LANES), lambda i, j, kk: (0, 0))
    in_specs = [a_spec, b_spec] + ([pl.BlockSpec((tm, tn), lambda i, j, kk: (i, j))] if has_add else [])
    in_specs += [dep_spec] * len(deps)
    args = (a, b) + ((add,) if has_add else ()) + tuple(deps)
    return pl.pallas_call(
        body, name=name, grid=(m // tm, n // tn, nk), in_specs=in_specs, out_specs=o_spec, out_shape=out_shape,
        scratch_shapes=[pltpu.VMEM((tm, tn), F32)] if nk > 1 else [],
        compiler_params=_cp(("parallel", "parallel", "arbitrary")))(*args)


def _rmsnorm_fwd(x, g, name, deps=()):
    t, d = x.shape
    tb = _rows(t, 256)

    def body(x_ref, g_ref, *rest):
        o_ref = rest[-1]
        xv = x_ref[...]
        r = lax.rsqrt(jnp.mean(xv * xv, axis=-1, keepdims=True) + EPS)
        o_ref[...] = (xv * r * g_ref[...]).astype(BF16)

    dep_spec = pl.BlockSpec((8, LANES), lambda i: (0, 0))
    return pl.pallas_call(
        body, name=name, grid=(t // tb,),
        in_specs=[pl.BlockSpec((tb, d), lambda i: (i, 0)), pl.BlockSpec((1, d), lambda i: (0, 0))]
        + [dep_spec] * len(deps),
        out_specs=pl.BlockSpec((tb, d), lambda i: (i, 0)), out_shape=jax.ShapeDtypeStruct((t, d), BF16),
        compiler_params=_cp(("parallel",)))(x, g, *deps)


def _rmsnorm_bwd(x, g, dy, res, name, deps=()):
    t, d = x.shape
    tb = _rows(t, 256)

    def body(x_ref, g_ref, dy_ref, res_ref, *rest):
        dx_ref, dx16_ref, dg_ref = rest[-3:]
        i = pl.program_id(0)
        xv = x_ref[...]
        dyv = dy_ref[...].astype(F32)
        r = lax.rsqrt(jnp.mean(xv * xv, axis=-1, keepdims=True) + EPS)
        u = dyv * g_ref[...]
        dx = r * u - xv * (r * r * r * jnp.mean(u * xv, axis=-1, keepdims=True)) + res_ref[...]
        dx_ref[...] = dx
        dx16_ref[...] = dx.astype(BF16)
        part = jnp.sum(dyv * xv * r, axis=0, keepdims=True)

        @pl.when(i == 0)
        def _():
            dg_ref[...] = part

        @pl.when(i > 0)
        def _():
            dg_ref[...] += part

    row = pl.BlockSpec((tb, d), lambda i: (i, 0))
    vec = pl.BlockSpec((1, d), lambda i: (0, 0))
    return pl.pallas_call(
        body, name=name, grid=(t // tb,),
        in_specs=[row, vec, row, row] + [pl.BlockSpec((8, LANES), lambda i: (0, 0))] * len(deps),
        out_specs=[row, row, vec],
        out_shape=[jax.ShapeDtypeStruct((t, d), F32), jax.ShapeDtypeStruct((t, d), BF16),
                   jax.ShapeDtypeStruct((1, d), F32)],
        compiler_params=_cp(("arbitrary",)))(x, g, dy, res, *deps)


def _final_loss(h, g, tgt):
    t, d = h.shape
    tb = _rows(t, 256)

    def body(h_ref, g_ref, t_ref, loss_ref, dh_ref, dh16_ref, dg_ref):
        i = pl.program_id(0)
        hv = h_ref[...]
        gv = g_ref[...]
        r = lax.rsqrt(jnp.mean(hv * hv, axis=-1, keepdims=True) + EPS)
        y = hv * r * gv
        diff = y - t_ref[...]
        lpart = jnp.sum(jnp.sum(diff * diff, axis=1, keepdims=True), axis=0, keepdims=True) * (0.5 / d)
        dy = diff * (1.0 / d)
        u = dy * gv
        dh = r * u - hv * (r * r * r * jnp.mean(u * hv, axis=-1, keepdims=True))
        dh_ref[...] = dh
        dh16_ref[...] = dh.astype(BF16)
        gpart = jnp.sum(dy * hv * r, axis=0, keepdims=True)
        lrow = jnp.broadcast_to(lpart, (1, LANES))

        @pl.when(i == 0)
        def _():
            loss_ref[...] = lrow
            dg_ref[...] = gpart

        @pl.when(i > 0)
        def _():
            loss_ref[...] += lrow
            dg_ref[...] += gpart

    row = pl.BlockSpec((tb, d), lambda i: (i, 0))
    vec = pl.BlockSpec((1, d), lambda i: (0, 0))
    return pl.pallas_call(
        body, name="final_loss", grid=(t // tb,), in_specs=[row, vec, row],
        out_specs=[pl.BlockSpec((1, LANES), lambda i: (0, 0)), row, row, vec],
        out_shape=[jax.ShapeDtypeStruct((1, LANES), F32), jax.ShapeDtypeStruct((t, d), F32),
                   jax.ShapeDtypeStruct((t, d), BF16), jax.ShapeDtypeStruct((1, d), F32)],
        compiler_params=_cp(("arbitrary",)))(h, g, tgt)


def _rope_tables(t):
    pos = jnp.arange(t, dtype=F32)
    inv = 1.0 / (ROPE_THETA ** (jnp.arange(0, ROT_DIM, 2, dtype=F32) / ROT_DIM))
    ang = pos[:, None] * inv[None, :]
    cos, sin = jnp.cos(ang), jnp.sin(ang)
    half = ROT_DIM // 2
    rest = HEAD_DIM - ROT_DIM
    c = jnp.concatenate([cos, cos, jnp.ones((t, rest), F32)], axis=1)
    s1 = jnp.concatenate([-sin, jnp.zeros((t, half + rest), F32)], axis=1)
    s2 = jnp.concatenate([jnp.zeros((t, half), F32), sin, jnp.zeros((t, rest), F32)], axis=1)
    return jnp.concatenate([jnp.tile(v, (1, LANES // HEAD_DIM)) for v in (c, s1, s2)], axis=1)


def _split_tables(tab):
    return tab[:, :LANES], tab[:, LANES:2 * LANES], tab[:, 2 * LANES:]


def _rope(x, c, s1, s2):
    half = ROT_DIM // 2
    return x * c + pltpu.roll(x, LANES - half, 1) * s1 + pltpu.roll(x, half, 1) * s2


def _rope_t(g, c, s1, s2):
    half = ROT_DIM // 2
    return g * c + pltpu.roll(g * s1, half, 1) + pltpu.roll(g * s2, LANES - half, 1)


def _band_masks(i, heads):
    n = heads * ATTN_BLOCK
    q = jnp.bitwise_and(_iota((n, ATTN_BLOCK), 0), ATTN_BLOCK - 1)
    j = _iota((n, ATTN_BLOCK), 1)
    upper = j > q
    return upper, upper & (j < jnp.where(i > 0, 0, ATTN_BLOCK))


def _fold_band(full, upper):
    return jnp.where(upper, full[:, :ATTN_BLOCK], full[:, ATTN_BLOCK:])


def _unfold_band(band, upper):
    return jnp.concatenate([jnp.where(upper, band, 0.0), jnp.where(upper, 0.0, band)], axis=1)


def _half_masks():
    lane = _iota((1, LANES), 1)
    return [(lane < HEAD_DIM).astype(F32), (lane >= HEAD_DIM).astype(F32)]


def _stack_heads(blocks, hm, j):
    pieces = []
    for r in range(4):
        qb, half = (4 * j + r) // 2, (4 * j + r) % 2
        piece = blocks[qb] * hm[half]
        if half != j:
            piece = pltpu.roll(piece, HEAD_DIM, 1)
        pieces.append(piece)
    return jnp.concatenate(pieces, axis=0)


def _unstack_heads(stacked, j):
    out = []
    for qb in (2 * j, 2 * j + 1):
        acc = None
        for half in range(2):
            r = 2 * qb + half - 4 * j
            piece = stacked[r * ATTN_BLOCK:(r + 1) * ATTN_BLOCK]
            if half != j:
                piece = pltpu.roll(piece, HEAD_DIM, 1)
            acc = piece if acc is None else acc + piece
        out.append((qb, acc))
    return out


def _sink_column(sink_ref, base):
    return jnp.concatenate([jnp.full((ATTN_BLOCK, 1), sink_ref[base + r], F32) for r in range(4)], axis=0)


def _attn_specs(nb_clamp):
    blk = ATTN_BLOCK
    kb, vb = O_K // LANES, O_V // LANES

    def cur(i):
        return jnp.minimum(i, nb_clamp)

    def prev(i):
        return jnp.maximum(jnp.minimum(i, nb_clamp + 1) - 1, 0)

    q = pl.BlockSpec((blk, 512), lambda p, i: (cur(i), p))
    kc = pl.BlockSpec((blk, LANES), lambda p, i: (cur(i), kb + p))
    kp = pl.BlockSpec((blk, LANES), lambda p, i: (prev(i), kb + p))
    vc = pl.BlockSpec((blk, LANES), lambda p, i: (cur(i), vb + p))
    vp = pl.BlockSpec((blk, LANES), lambda p, i: (prev(i), vb + p))
    tc = pl.BlockSpec((blk, 3 * LANES), lambda p, i: (cur(i), 0))
    tp = pl.BlockSpec((blk, 3 * LANES), lambda p, i: (prev(i), 0))
    return q, kc, kp, vc, vp, tc, tp


def _attn_fwd(proj, sinks, tables):
    t = proj.shape[0]
    nb = t // ATTN_BLOCK
    scale = HEAD_DIM ** -0.5

    def body(sink_ref, q_ref, kc_ref, kp_ref, vc_ref, vp_ref, tc_ref, tp_ref, o_ref):
        p = pl.program_id(0)
        i = pl.program_id(1)
        cc, s1c, s2c = _split_tables(tc_ref[...])
        kband = jnp.concatenate([_rope(kp_ref[...], *_split_tables(tp_ref[...])),
                                 _rope(kc_ref[...], cc, s1c, s2c)], axis=0).astype(BF16)
        vband = jnp.concatenate([vp_ref[...], vc_ref[...]], axis=0)
        hm = _half_masks()
        vsel = [(vband * hm[j]).astype(BF16) for j in range(2)]
        upper, dropped = _band_masks(i, 1)
        qr = [_rope(q_ref[:, qb * LANES:(qb + 1) * LANES], cc, s1c, s2c) for qb in range(4)]

        def scores(hh):
            qb, half, j = hh // 2, hh % 2, hh // 4
            qs = qr[qb] * hm[half]
            if half != j:
                qs = pltpu.roll(qs, HEAD_DIM, 1)
            return _dot(qs, kband, 'nt')

        ahead = scores(0)
        acc = None
        for hh in range(8):
            qb, half, j = hh // 2, hh % 2, hh // 4
            raw = ahead
            if hh + 1 < 8:
                ahead = scores(hh + 1)
            s = jnp.where(dropped, NEG, _fold_band(raw, upper) * scale)
            sink = sink_ref[p * 8 + hh]
            m = jnp.maximum(jnp.max(s, axis=1, keepdims=True), sink)
            pe = jnp.exp(s - m)
            den = jnp.sum(pe, axis=1, keepdims=True) + jnp.exp(sink - m)
            o = _dot(_unfold_band(pe / den, upper), vsel[j])
            if half != j:
                o = pltpu.roll(o, HEAD_DIM, 1)
            acc = o if half == 0 else acc + o
            if half == 1:
                o_ref[:, qb * LANES:(qb + 1) * LANES] = acc

    q, kc, kp, vc, vp, tc, tp = _attn_specs(nb - 1)
    smem = pl.BlockSpec(memory_space=pltpu.SMEM)
    return pl.pallas_call(
        body, name="attn_fwd", grid=(4, nb),
        in_specs=[smem, q, kc, kp, vc, vp, tc, tp],
        out_specs=pl.BlockSpec((ATTN_BLOCK, 512), lambda p, i: (i, p)),
        out_shape=jax.ShapeDtypeStruct((t, ATTN_WIDTH), F32),
        compiler_params=_cp(("parallel", "arbitrary")))(sinks, proj, proj, proj, proj, proj, tables, tables)


def _attn_bwd(proj, sinks, tables, dout):
    t = proj.shape[0]
    nb = t // ATTN_BLOCK
    scale = HEAD_DIM ** -0.5

    def body(sink_ref, q_ref, kc_ref, kp_ref, vc_ref, vp_ref, tc_ref, tp_ref,
             do_ref, dq_ref, dk_ref, dv_ref, ds_ref, carry_k, carry_v):
        p = pl.program_id(0)
        i = pl.program_id(1)
        ptab = _split_tables(tp_ref[...])

        @pl.when(i == 0)
        def _():
            carry_k[...] = jnp.zeros_like(carry_k)
            carry_v[...] = jnp.zeros_like(carry_v)
            ds_ref[...] = jnp.zeros_like(ds_ref)

        @pl.when(i < nb)
        def _():
            cc, s1c, s2c = _split_tables(tc_ref[...])
            kband = jnp.concatenate([_rope(kp_ref[...], *ptab), _rope(kc_ref[...], cc, s1c, s2c)], axis=0)
            vband = jnp.concatenate([vp_ref[...], vc_ref[...]], axis=0)
            hm = _half_masks()
            kband16 = kband.astype(BF16)
            vband16 = vband.astype(BF16)
            upper, dropped = _band_masks(i, 4)
            dkb = jnp.zeros((2 * ATTN_BLOCK, LANES), F32)
            dvb = jnp.zeros((2 * ATTN_BLOCK, LANES), F32)
            row8 = _iota((8, LANES), 0)
            dsink = jnp.zeros((8, LANES), F32)
            qr = [_rope(q_ref[:, qb * LANES:(qb + 1) * LANES], cc, s1c, s2c) for qb in range(4)]
            dob = [do_ref[:, qb * LANES:(qb + 1) * LANES] for qb in range(4)]
            for j in range(2):
                qst = _stack_heads(qr, hm, j).astype(BF16)
                dost = _stack_heads(dob, hm, j).astype(BF16)
                s = jnp.where(dropped, NEG, _fold_band(_dot(qst, kband16, 'nt'), upper) * scale)
                sink = _sink_column(sink_ref, p * 8 + 4 * j)
                m = jnp.maximum(jnp.max(s, axis=1, keepdims=True), sink)
                pe = jnp.exp(s - m)
                psink = jnp.exp(sink - m)
                den = jnp.sum(pe, axis=1, keepdims=True) + psink
                pr = pe / den
                dvb = dvb + _dot(_unfold_band(pr, upper).T, dost)
                dp = _fold_band(_dot(dost, vband16, 'nt'), upper)
                delta = jnp.sum(pr * dp, axis=1, keepdims=True)
                dsc = _unfold_band(pr * (dp - delta) * scale, upper)
                dsk = psink / den * delta
                for r in range(4):
                    part = jnp.sum(dsk[r * ATTN_BLOCK:(r + 1) * ATTN_BLOCK])
                    dsink = dsink + jnp.where(row8 == 4 * j + r, -part, 0.0)
                for qb, dqb in _unstack_heads(_dot(dsc, kband * hm[j]), j):
                    dq_ref[:, qb * LANES:(qb + 1) * LANES] = _rope_t(dqb, cc, s1c, s2c).astype(BF16)
                dkb = dkb + _dot(dsc.T, qst)
            ds_ref[0] += dsink
            dk_ref[...] = _rope_t(carry_k[...] + dkb[:ATTN_BLOCK], *ptab).astype(BF16)
            dv_ref[...] = (carry_v[...] + dvb[:ATTN_BLOCK]).astype(BF16)
            carry_k[...] = dkb[ATTN_BLOCK:]
            carry_v[...] = dvb[ATTN_BLOCK:]

        @pl.when(i == nb)
        def _():
            dk_ref[...] = _rope_t(carry_k[...], *ptab).astype(BF16)
            dv_ref[...] = carry_v[...].astype(BF16)

    q, kc, kp, vc, vp, tc, tp = _attn_specs(nb - 1)
    smem = pl.BlockSpec(memory_space=pltpu.SMEM)
    qblk = pl.BlockSpec((ATTN_BLOCK, 512), lambda p, i: (jnp.minimum(i, nb - 1), p))
    kvout = pl.BlockSpec((ATTN_BLOCK, LANES), lambda p, i: (jnp.maximum(i - 1, 0), p))
    return pl.pallas_call(
        body, name="attn_bwd", grid=(4, nb + 1),
        in_specs=[smem, q, kc, kp, vc, vp, tc, tp, qblk],
        out_specs=[qblk, kvout, kvout, pl.BlockSpec((1, 8, LANES), lambda p, i: (p, 0, 0))],
        out_shape=[jax.ShapeDtypeStruct((t, ATTN_WIDTH), BF16), jax.ShapeDtypeStruct((t, KV_WIDTH), BF16),
                   jax.ShapeDtypeStruct((t, KV_WIDTH), BF16), jax.ShapeDtypeStruct((4, 8, LANES), F32)],
        scratch_shapes=[pltpu.VMEM((ATTN_BLOCK, LANES), F32), pltpu.VMEM((ATTN_BLOCK, LANES), F32)],
        compiler_params=_cp(("parallel", "arbitrary")))(sinks, proj, proj, proj, proj, proj, tables, tables, dout)


def _shift_rows(x, prev8, j):
    n, c = x.shape
    r = pltpu.roll(x.reshape(n // 8, 8, c), j, 1)
    before = pltpu.roll(prev8, j, 0)[None]
    if n > 8:
        before = jnp.concatenate([before, r[:-1]], axis=0)
    return jnp.where(_iota((1, 8, 1), 1) < j, before, r).reshape(n, c)


def _shift_rows_up(x, next8, j):
    n, c = x.shape
    r = pltpu.roll(x.reshape(n // 8, 8, c), 8 - j, 1)
    after = pltpu.roll(next8, 8 - j, 0)[None]
    if n > 8:
        after = jnp.concatenate([r[1:], after], axis=0)
    return jnp.where(_iota((1, 8, 1), 1) >= 8 - j, after, r).reshape(n, c)


def _conv_apply(x, prev8, w, b, taps):
    u = b + x * w[taps - 1:taps]
    for j in range(1, taps):
        u = u + _shift_rows(x, prev8, j) * w[taps - 1 - j:taps - j]
    return u


def _conv_grads(du, du_next8, x, w, taps):
    dx = du * w[taps - 1:taps]
    rowk = _iota((taps, 1), 0)
    dw = jnp.where(rowk == taps - 1, jnp.sum(du * x, axis=0, keepdims=True), 0.0)
    for j in range(1, taps):
        ahead = _shift_rows_up(du, du_next8, j)
        dx = dx + ahead * w[taps - 1 - j:taps - j]
        dw = dw + jnp.where(rowk == taps - 1 - j, jnp.sum(ahead * x, axis=0, keepdims=True), 0.0)
    return dx, dw, jnp.sum(du, axis=0, keepdims=True)


def _conv_specs(tb, tc, col0, t):
    c0 = col0 // tc
    cur = pl.BlockSpec((tb, tc), lambda j, i: (i, c0 + j))
    prev = pl.BlockSpec((8, tc), lambda j, i: (jnp.maximum(i * (tb // 8) - 1, 0), c0 + j))
    nxt = pl.BlockSpec((8, tc), lambda j, i: (jnp.minimum((i + 1) * (tb // 8), t // 8 - 1), c0 + j))
    return cur, prev, nxt


def _conv_silu_fwd(x, w, b, *, col0, width, name):
    t = x.shape[0]
    taps = w.shape[0]
    tb, tc = _rows(t, 512), _tile(width, 1024)
    assert col0 % tc == 0

    def body(x_ref, xp_ref, w_ref, b_ref, o_ref, u_ref):
        i = pl.program_id(1)
        prev8 = jnp.where(i > 0, xp_ref[...], 0.0)
        u = _conv_apply(x_ref[...], prev8, w_ref[...], b_ref[...], taps)
        u_ref[...] = u
        o_ref[...] = u * _sigmoid(u)

    cur, prev, _ = _conv_specs(tb, tc, col0, t)
    par = pl.BlockSpec((taps, tc), lambda j, i: (0, j))
    bias = pl.BlockSpec((1, tc), lambda j, i: (0, j))
    out = pl.BlockSpec((tb, tc), lambda j, i: (i, j))
    shp = jax.ShapeDtypeStruct((t, width), F32)
    return pl.pallas_call(
        body, name=name, grid=(width // tc, t // tb), in_specs=[cur, prev, par, bias], out_specs=[out, out],
        out_shape=[shp, shp], compiler_params=_cp(("parallel", "parallel")))(x, x, w, b)


def _dsilu(u):
    sg = _sigmoid(u)
    return sg * (1.0 + u * (1.0 - sg))


def _ssd_conv_bwd(x, w, dxs, dbm, dcm, *, col0, name):
    t = x.shape[0]
    taps = w.shape[0]
    tb, tc = _rows(t, 512), BC_WIDTH
    nrow, ncol = t // tb, CONV_CH // tc
    c0 = col0 // tc

    def body(x_ref, w_ref, xs_ref, xsn_ref, bm_ref, bmn_ref, cm_ref, cmn_ref, dx_ref, dw_ref, db_ref):
        i = pl.program_id(0)
        j = pl.program_id(1)

        def run(du_ref, dun_ref):
            next8 = jnp.where(i < nrow - 1, dun_ref[...], 0.0)
            dx, dwv, dbv = _conv_grads(du_ref[...], next8, x_ref[...], w_ref[...], taps)
            dx_ref[...] = dx.astype(BF16)

            @pl.when(i == 0)
            def _():
                dw_ref[j] = dwv
                db_ref[j] = dbv

            @pl.when(i > 0)
            def _():
                dw_ref[j] += dwv
                db_ref[j] += dbv

        pl.when(j < 2)(lambda: run(xs_ref, xsn_ref))
        pl.when(j == 2)(lambda: run(bm_ref, bmn_ref))
        pl.when(j == 3)(lambda: run(cm_ref, cmn_ref))

    def nxt_row(i):
        return jnp.minimum((i + 1) * (tb // 8), t // 8 - 1)

    xs_col = lambda j: jnp.minimum(j, SSD_INNER // tc - 1)
    in_specs = [pl.BlockSpec((tb, tc), lambda i, j: (i, c0 + j)), pl.BlockSpec((taps, tc), lambda i, j: (0, j)),
                pl.BlockSpec((tb, tc), lambda i, j: (i, xs_col(j))),
                pl.BlockSpec((8, tc), lambda i, j: (nxt_row(i), xs_col(j))),
                pl.BlockSpec((tb, tc), lambda i, j: (i, 0)), pl.BlockSpec((8, tc), lambda i, j: (nxt_row(i), 0)),
                pl.BlockSpec((tb, tc), lambda i, j: (i, 0)), pl.BlockSpec((8, tc), lambda i, j: (nxt_row(i), 0))]
    dx, dw, db = pl.pallas_call(
        body, name=name, grid=(nrow, ncol), in_specs=in_specs,
        out_specs=[pl.BlockSpec((tb, tc), lambda i, j: (i, j)),
                   pl.BlockSpec((ncol, taps, tc), lambda i, j: (0, 0, 0)),
                   pl.BlockSpec((ncol, 1, tc), lambda i, j: (0, 0, 0))],
        out_shape=[jax.ShapeDtypeStruct((t, CONV_CH), BF16), jax.ShapeDtypeStruct((ncol, taps, tc), F32),
                   jax.ShapeDtypeStruct((ncol, 1, tc), F32)],
        compiler_params=_cp(("arbitrary", "arbitrary")))(x, w, dxs, dxs, dbm, dbm, dcm, dcm)
    return dx, dw.transpose(1, 0, 2).reshape(taps, CONV_CH), db.transpose(1, 0, 2).reshape(1, CONV_CH)


def _ffn_specs(tb, tc, t):
    nc = D_FF // tc

    def cur(half):
        return pl.BlockSpec((tb, tc), lambda j, i: (i, half * nc + j))

    def prev(half):
        return pl.BlockSpec((8, tc), lambda j, i: (jnp.maximum(i * (tb // 8) - 1, 0), half * nc + j))

    def nxt(half):
        return pl.BlockSpec((8, tc), lambda j, i: (jnp.minimum((i + 1) * (tb // 8), t // 8 - 1), half * nc + j))

    def par(rows, half):
        return pl.BlockSpec((rows, tc), lambda j, i: (0, half * nc + j))

    return cur, prev, nxt, par


def _ffn_act_fwd(u0, w, b):
    t = u0.shape[0]
    tb, tc = _rows(t, 512), _tile(D_FF, 1408)
    cur, prev, _, par = _ffn_specs(tb, tc, t)

    def body(g_ref, gp_ref, v_ref, vp_ref, wg_ref, wv_ref, bg_ref, bv_ref, o_ref, u_ref):
        i = pl.program_id(1)
        ug = _conv_apply(g_ref[...], jnp.where(i > 0, gp_ref[...], 0.0), wg_ref[...], bg_ref[...], FFN_CONV)
        uv = _conv_apply(v_ref[...], jnp.where(i > 0, vp_ref[...], 0.0), wv_ref[...], bv_ref[...], FFN_CONV)
        o_ref[...] = (ug * _sigmoid(ug) * uv).astype(BF16)
        u_ref[0] = ug
        u_ref[1] = uv

    return pl.pallas_call(
        body, name="ffn_act_fwd", grid=(D_FF // tc, t // tb),
        in_specs=[cur(0), prev(0), cur(1), prev(1), par(FFN_CONV, 0), par(FFN_CONV, 1), par(1, 0), par(1, 1)],
        out_specs=[pl.BlockSpec((tb, tc), lambda j, i: (i, j)), pl.BlockSpec((2, tb, tc), lambda j, i: (0, i, j))],
        out_shape=[jax.ShapeDtypeStruct((t, D_FF), BF16), jax.ShapeDtypeStruct((2, t, D_FF), F32)],
        compiler_params=_cp(("parallel", "parallel")))(u0, u0, u0, u0, w, w, b, b)


def _ffn_act_bwd(u0, u, w, da):
    t = u0.shape[0]
    tb, tc = _rows(t, 256), _tile(D_FF, 1408)
    nrow = t // tb
    taps = FFN_CONV
    cur, _, _, par = _ffn_specs(tb, tc, t)

    def dact(ug, uv, dav):
        sg = _sigmoid(ug)
        return dav * uv * (sg * (1.0 + ug * (1.0 - sg))), dav * ug * sg

    def body(g_ref, v_ref, u_ref, un_ref, wg_ref, wv_ref, da_ref, dan_ref, dx_ref, dw_ref, db_ref):
        i = pl.program_id(1)
        dug, duv = dact(u_ref[0], u_ref[1], da_ref[...].astype(F32))
        dan = jnp.where(i < nrow - 1, dan_ref[...].astype(F32)[:8], 0.0)
        dugn, duvn = dact(un_ref[0], un_ref[1], dan)
        dxg, dwg, dbg = _conv_grads(dug, dugn, g_ref[...], wg_ref[...], taps)
        dxv, dwv, dbv = _conv_grads(duv, duvn, v_ref[...], wv_ref[...], taps)
        dx_ref[0] = dxg.astype(BF16)
        dx_ref[1] = dxv.astype(BF16)

        @pl.when(i == 0)
        def _():
            dw_ref[0] = dwg
            dw_ref[1] = dwv
            db_ref[0] = dbg
            db_ref[1] = dbv

        @pl.when(i > 0)
        def _():
            dw_ref[0] += dwg
            dw_ref[1] += dwv
            db_ref[0] += dbg
            db_ref[1] += dbv

    both = pl.BlockSpec((2, tb, tc), lambda j, i: (0, i, j))
    both_nxt = pl.BlockSpec((2, 8, tc), lambda j, i: (0, jnp.minimum((i + 1) * (tb // 8), t // 8 - 1), j))
    da_cur = pl.BlockSpec((tb, tc), lambda j, i: (i, j))
    da_nxt = pl.BlockSpec((16, tc), lambda j, i: (jnp.minimum((i + 1) * (tb // 16), t // 16 - 1), j))
    return pl.pallas_call(
        body, name="ffn_act_bwd", grid=(D_FF // tc, nrow),
        in_specs=[cur(0), cur(1), both, both_nxt, par(taps, 0), par(taps, 1), da_cur, da_nxt],
        out_specs=[both, pl.BlockSpec((2, taps, tc), lambda j, i: (0, 0, j)),
                   pl.BlockSpec((2, 1, tc), lambda j, i: (0, 0, j))],
        out_shape=[jax.ShapeDtypeStruct((2, t, D_FF), BF16), jax.ShapeDtypeStruct((2, taps, D_FF), F32),
                   jax.ShapeDtypeStruct((2, 1, D_FF), F32)],
        compiler_params=_cp(("parallel", "arbitrary")))(u0, u0, u, u, w, w, da, da)


def _head_masks():
    lane = _iota((1, 4 * SSD_HEAD_DIM), 1)
    return [((lane >= r * SSD_HEAD_DIM) & (lane < (r + 1) * SSD_HEAD_DIM)).astype(F32) for r in range(4)]


def _segsum(v):
    first = _iota((1, LANES), 1) < SSD_HEAD_DIM
    halves = []
    for k in range(2):
        vh = v[:, k * LANES:(k + 1) * LANES]
        both = jnp.sum(vh, axis=1, keepdims=True)
        one = jnp.sum(jnp.where(first, vh, 0.0), axis=1, keepdims=True)
        halves.append(jnp.where(first, one, both - one))
    return jnp.concatenate(halves, axis=1)


def _ssd_common(raw_e, prow, rawr4, bcol, acol):
    n = SSD_CHUNK
    dt_e = _softplus(raw_e + prow[0:1, :])
    a_e = -jnp.exp(prow[1:2, :])
    d_e = prow[2:3, :]
    tril = (_iota((n, n), 0) >= _iota((n, n), 1)).astype(F32)
    acs_e = _dot_exact(tril, dt_e * a_e)
    last_e = acs_e[n - 1:n, :]
    dtr4 = _softplus(rawr4 + bcol)
    triu = (_iota((n, n), 0) <= _iota((n, n), 1)).astype(F32)
    acs_r4 = _dot_exact(dtr4 * (-jnp.exp(acol)), triu)
    return dt_e, a_e, d_e, acs_e, last_e, acs_r4


def _decay_matrix(acs_e, acs_r4, r):
    n = SSD_CHUNK
    col = acs_e[:, r * SSD_HEAD_DIM:r * SSD_HEAD_DIM + 1]
    seg = col - acs_r4[r:r + 1, :]
    causal = _iota((n, n), 0) >= _iota((n, n), 1)
    return jnp.exp(jnp.where(causal, seg, NEG))


SSD_STEP_CHUNKS = 8
SSD_ROWS = SSD_STEP_CHUNKS * SSD_CHUNK


def _ssd_specs(t, rev):
    nb = t // SSD_ROWS
    xb, bb, cb = 0, SSD_INNER // SSD_STATE, (SSD_INNER + BC_WIDTH) // SSD_STATE

    def ch(c):
        return (nb - 1 - c) if rev else c

    x = pl.BlockSpec((SSD_ROWS, 256), lambda g, c: (ch(c), xb + g))
    bm = pl.BlockSpec((SSD_ROWS, SSD_STATE), lambda g, c: (ch(c), bb + g))
    cm = pl.BlockSpec((SSD_ROWS, SSD_STATE), lambda g, c: (ch(c), cb + g))
    dtc = pl.BlockSpec((1, SSD_ROWS, 256), lambda g, c: (g, ch(c), 0))
    dtr = pl.BlockSpec((1, 4, SSD_ROWS), lambda g, c: (g, 0, ch(c)))
    prow = pl.BlockSpec((1, 3, 256), lambda g, c: (g, 0, 0))
    pcol = pl.BlockSpec((1, 4, 1), lambda g, c: (g, 0, 0))
    st = pl.BlockSpec((1, SSD_STEP_CHUNKS, SSD_STATE, 256), lambda g, c: (g, ch(c), 0, 0))
    return x, bm, cm, dtc, dtr, prow, pcol, st, ch


def _ssd_params(dt_raw, dt_bias, a_log, ssd_d):
    t = dt_raw.shape[0]
    by_group = dt_raw.reshape(t, SSD_GROUPS, 4)
    dtc = jnp.repeat(by_group, SSD_HEAD_DIM, axis=2).transpose(1, 0, 2)
    dtr = by_group.transpose(1, 2, 0)
    prow = jnp.repeat(jnp.stack([dt_bias.reshape(SSD_GROUPS, 4), a_log.reshape(SSD_GROUPS, 4),
                                 ssd_d.reshape(SSD_GROUPS, 4)], axis=1), SSD_HEAD_DIM, axis=2)
    bcol = dt_bias.reshape(SSD_GROUPS, 4, 1)
    acol = a_log.reshape(SSD_GROUPS, 4, 1)
    return dtc, dtr, prow, bcol, acol


def _ssd_fwd(xbc, params):
    t = xbc.shape[0]
    nc = t // SSD_CHUNK
    dtc, dtr, prow, bcol, acol = params

    def body(x_ref, b_ref, c_ref, dtc_ref, dtr_ref, prow_ref, bcol_ref, acol_ref, y_ref, st_ref, s_scr):
        c = pl.program_id(1)

        @pl.when(c == 0)
        def _():
            s_scr[...] = jnp.zeros_like(s_scr)

        masks = _head_masks()
        s = s_scr[...]
        for k in range(SSD_STEP_CHUNKS):
            rows = slice(k * SSD_CHUNK, (k + 1) * SSD_CHUNK)
            dt_e, a_e, d_e, acs_e, last_e, acs_r4 = _ssd_common(
                dtc_ref[0, rows], prow_ref[0], dtr_ref[0][:, rows], bcol_ref[0], acol_ref[0])
            xv = x_ref[rows]
            bm, cm = b_ref[rows], c_ref[rows]
            st_ref[0, k] = s
            xdt = xv * dt_e
            cb = _dot(cm, bm, 'nt')
            y = _dot(cm, s) * jnp.exp(acs_e) + xv * d_e
            for r in range(4):
                mr = cb * _decay_matrix(acs_e, acs_r4, r)
                y = y + _dot(mr, xdt * masks[r])
            y_ref[rows] = y
            w = xdt * jnp.exp(last_e - acs_e)
            s = s * jnp.exp(last_e) + _dot(bm.T, w)
        s_scr[...] = s

    x, bm, cm, dtcs, dtrs, prs, pcs, st, _ = _ssd_specs(t, False)
    return pl.pallas_call(
        body, name="ssd_fwd", grid=(SSD_GROUPS, t // SSD_ROWS), in_specs=[x, bm, cm, dtcs, dtrs, prs, pcs, pcs],
        out_specs=[pl.BlockSpec((SSD_ROWS, 256), lambda g, c: (c, g)), st],
        out_shape=[jax.ShapeDtypeStruct((t, SSD_INNER), F32),
                   jax.ShapeDtypeStruct((SSD_GROUPS, nc, SSD_STATE, 256), F32)],
        scratch_shapes=[pltpu.VMEM((SSD_STATE, 256), F32)],
        compiler_params=_cp(("parallel", "arbitrary")))(xbc, xbc, xbc, dtc, dtr, prow, bcol, acol)


def _ssd_bwd(xbc, pre, params, states, dy):
    t = xbc.shape[0]
    nc = t // SSD_CHUNK
    n = SSD_CHUNK
    dtc, dtr, prow, bcol, acol = params

    def body(x_ref, b_ref, c_ref, ux_ref, ub_ref, uc_ref, dtc_ref, dtr_ref, prow_ref, bcol_ref, acol_ref, st_ref,
             dy_ref, dx_ref, db_ref, dc_ref, ddt_ref, dp_ref, ds_scr):
        c = pl.program_id(1)

        @pl.when(c == 0)
        def _():
            ds_scr[...] = jnp.zeros_like(ds_scr)
            dp_ref[...] = jnp.zeros_like(dp_ref)

        masks = _head_masks()
        ds = ds_scr[...]
        for k in reversed(range(SSD_STEP_CHUNKS)):
            rows = slice(k * SSD_CHUNK, (k + 1) * SSD_CHUNK)
            raw_e = dtc_ref[0, rows]
            prw = prow_ref[0]
            dt_e, a_e, d_e, acs_e, last_e, acs_r4 = _ssd_common(raw_e, prw, dtr_ref[0][:, rows], bcol_ref[0], acol_ref[0])
            xv = x_ref[rows]
            bm, cm = b_ref[rows], c_ref[rows]
            s = st_ref[0, k]
            dyv = dy_ref[rows]
            e_e = jnp.exp(acs_e)
            dec_e = jnp.exp(last_e - acs_e)
            cd_e = jnp.exp(last_e)
            xdt = xv * dt_e
            w = xdt * dec_e
            b16, c16, s16, ds16 = bm.astype(BF16), cm.astype(BF16), s.astype(BF16), ds.astype(BF16)
            cb = _dot(c16, b16, 'nt')
            yoff_raw = _dot(c16, s16)
            dye = dyv * e_e
            dye16 = dye.astype(BF16)
            dcm = _dot(dye16, s16, 'nt')
            ds_prev = ds * cd_e + _dot(cm.T, dye16)
            dacs_e = _segsum(dyv * yoff_raw) * e_e
            dw = _dot(b16, ds16)
            dbm = _dot(w, ds16, 'nt')
            tdec = _segsum(dw * xdt) * dec_e
            dacs_e = dacs_e - tdec
            dlast_e = jnp.sum(tdec, axis=0, keepdims=True)
            dxdt = dw * dec_e
            dlast_e = dlast_e + _segsum(jnp.sum(ds * s, axis=0, keepdims=True)) * cd_e
            dcb = jnp.zeros((n, n), F32)
            for r in range(4):
                lm = _decay_matrix(acs_e, acs_r4, r)
                mr = cb * lm
                dyr16 = (dyv * masks[r]).astype(BF16)
                dm = _dot(dyr16, xdt * masks[r], 'nt')
                dcb = dcb + dm * lm
                dseg = dm * mr
                dcol = jnp.sum(dseg, axis=1, keepdims=True) - jnp.sum(dseg.T, axis=1, keepdims=True)
                dacs_e = dacs_e + dcol * masks[r]
                dxdt = dxdt + _dot(mr.T, dyr16)
            dcm = dcm + _dot(dcb, b16)
            dbm = dbm + _dot(dcb.T, c16)
            dacs_e = dacs_e + jnp.where(_iota((n, 1), 0) == n - 1, dlast_e, 0.0)
            triu = (_iota((n, n), 0) <= _iota((n, n), 1)).astype(F32)
            ddta_e = _dot_exact(triu, dacs_e)
            ddt_e = ddta_e * a_e + _segsum(dxdt * xv)
            dx_ref[rows] = (dxdt * dt_e + dyv * d_e) * _dsilu(ux_ref[rows])
            db_ref[rows] = dbm * _dsilu(ub_ref[rows])
            dc_ref[rows] = dcm * _dsilu(uc_ref[rows])
            draw_e = ddt_e * _sigmoid(raw_e + prw[0:1, :])
            draw_t = draw_e.T
            ddt_ref[0, :, rows] = jnp.concatenate([draw_t[r * SSD_HEAD_DIM:r * SSD_HEAD_DIM + 1] for r in range(4)], axis=0)
            dbias = jnp.sum(draw_e, axis=0, keepdims=True)
            dalog = jnp.sum(ddta_e * dt_e, axis=0, keepdims=True) * a_e
            dd = _segsum(jnp.sum(dyv * xv, axis=0, keepdims=True))
            row3 = _iota((3, 1), 0)
            dp_ref[0] += (jnp.where(row3 == 0, dbias, 0.0) + jnp.where(row3 == 1, dalog, 0.0)
                          + jnp.where(row3 == 2, dd, 0.0))
            ds = ds_prev
        ds_scr[...] = ds


    x, bm, cm, dtcs, dtrs, prs, pcs, st, ch = _ssd_specs(t, True)
    yblk = pl.BlockSpec((SSD_ROWS, 256), lambda g, c: (ch(c), g))
    nblk = pl.BlockSpec((SSD_ROWS, SSD_STATE), lambda g, c: (ch(c), g))
    return pl.pallas_call(
        body, name="ssd_bwd", grid=(SSD_GROUPS, t // SSD_ROWS),
        in_specs=[x, bm, cm, x, bm, cm, dtcs, dtrs, prs, pcs, pcs, st, yblk],
        out_specs=[yblk, nblk, nblk, dtrs, prs],
        out_shape=[jax.ShapeDtypeStruct((t, SSD_INNER), F32), jax.ShapeDtypeStruct((t, BC_WIDTH), F32),
                   jax.ShapeDtypeStruct((t, BC_WIDTH), F32), jax.ShapeDtypeStruct((SSD_GROUPS, 4, t), F32),
                   jax.ShapeDtypeStruct((SSD_GROUPS, 3, 256), F32)],
        scratch_shapes=[pltpu.VMEM((SSD_STATE, 256), F32)],
        compiler_params=_cp(("parallel", "arbitrary")))(xbc, xbc, xbc, pre, pre, pre, dtc, dtr, prow, bcol, acol,
                                                         states, dy)


GROUP_W = SSD_INNER // SSD_GROUPS


def _mix_specs(tb):
    row = pl.BlockSpec((tb, 2048), lambda i: (i, 0))
    zlo = pl.BlockSpec((tb, 1024), lambda i: (i, O_Z // 1024))
    zhi = pl.BlockSpec((tb, 1024), lambda i: (i, O_Z // 1024 + 1))
    vec = pl.BlockSpec((1, 2048), lambda i: (0, 0))
    return row, zlo, zhi, vec


def _mix_fwd(attn, y, proj, g_attn, g_ssd):
    t = attn.shape[0]
    tb = _rows(t, 256)

    def body(a_ref, y_ref, zlo_ref, zhi_ref, ga_ref, gs_ref, o_ref):
        av = a_ref[...]
        r = lax.rsqrt(jnp.mean(av * av, axis=-1, keepdims=True) + EPS)
        o_ref[:, :ATTN_WIDTH] = (av * r * ga_ref[...]).astype(BF16)
        for g in range(SSD_GROUPS):
            lo, hi = g * GROUP_W, (g + 1) * GROUP_W
            zref = zlo_ref if g < 4 else zhi_ref
            z = zref[:, lo % 1024:lo % 1024 + GROUP_W]
            yg = y_ref[:, lo:hi] * (z * _sigmoid(z))
            rg = lax.rsqrt(jnp.mean(yg * yg, axis=-1, keepdims=True) + EPS)
            o_ref[:, ATTN_WIDTH + lo:ATTN_WIDTH + hi] = (yg * rg * gs_ref[:, lo:hi]).astype(BF16)

    row, zlo, zhi, vec = _mix_specs(tb)
    return pl.pallas_call(
        body, name="mix_fwd", grid=(t // tb,), in_specs=[row, row, zlo, zhi, vec, vec],
        out_specs=pl.BlockSpec((tb, 4096), lambda i: (i, 0)), out_shape=jax.ShapeDtypeStruct((t, 4096), BF16),
        compiler_params=_cp(("parallel",)))(attn, y, proj, proj, g_attn, g_ssd)


def _mix_bwd(dmix, attn, y, proj, g_attn, g_ssd):
    t = attn.shape[0]
    tb = _rows(t, 256)

    def body(dm_ref, a_ref, y_ref, zlo_ref, zhi_ref, ga_ref, gs_ref, da_ref, dy_ref, dz_ref, dga_ref, dgs_ref):
        i = pl.program_id(0)
        av = a_ref[...]
        dn = dm_ref[:, :ATTN_WIDTH].astype(F32)
        r = lax.rsqrt(jnp.mean(av * av, axis=-1, keepdims=True) + EPS)
        u = dn * ga_ref[...]
        da_ref[...] = r * u - av * (r * r * r * jnp.mean(u * av, axis=-1, keepdims=True))
        dga = jnp.sum(dn * av * r, axis=0, keepdims=True)

        @pl.when(i == 0)
        def _():
            dga_ref[...] = dga

        @pl.when(i > 0)
        def _():
            dga_ref[...] += dga

        for g in range(SSD_GROUPS):
            lo, hi = g * GROUP_W, (g + 1) * GROUP_W
            zref = zlo_ref if g < 4 else zhi_ref
            z = zref[:, lo % 1024:lo % 1024 + GROUP_W]
            yv = y_ref[:, lo:hi]
            sg = _sigmoid(z)
            sz = z * sg
            yg = yv * sz
            rg = lax.rsqrt(jnp.mean(yg * yg, axis=-1, keepdims=True) + EPS)
            do = dm_ref[:, ATTN_WIDTH + lo:ATTN_WIDTH + hi].astype(F32)
            ug = do * gs_ref[:, lo:hi]
            dyg = rg * ug - yg * (rg * rg * rg * jnp.mean(ug * yg, axis=-1, keepdims=True))
            dy_ref[:, lo:hi] = dyg * sz
            dz_ref[:, lo:hi] = (dyg * yv * (sg * (1.0 + z * (1.0 - sg)))).astype(BF16)
            dgs = jnp.sum(do * yg * rg, axis=0, keepdims=True)

            @pl.when(i == 0)
            def _():
                dgs_ref[:, lo:hi] = dgs

            @pl.when(i > 0)
            def _():
                dgs_ref[:, lo:hi] += dgs

    row, zlo, zhi, vec = _mix_specs(tb)
    return pl.pallas_call(
        body, name="mix_bwd", grid=(t // tb,),
        in_specs=[pl.BlockSpec((tb, 4096), lambda i: (i, 0)), row, row, zlo, zhi, vec, vec],
        out_specs=[row, row, row, vec, vec],
        out_shape=[jax.ShapeDtypeStruct((t, 2048), F32), jax.ShapeDtypeStruct((t, 2048), F32),
                   jax.ShapeDtypeStruct((t, 2048), BF16), jax.ShapeDtypeStruct((1, 2048), F32),
                   jax.ShapeDtypeStruct((1, 2048), F32)],
        compiler_params=_cp(("arbitrary",)))(dmix, attn, y, proj, proj, g_attn, g_ssd)


def _adamw(w, g, m, v, name):
    r, c = w.shape
    tb = _rows(r, 256)
    c1 = 1.0 - ADAM_B1 ** ADAM_STEP
    c2 = 1.0 - ADAM_B2 ** ADAM_STEP

    def body(w_ref, g_ref, m_ref, v_ref, d_ref, m2_ref, v2_ref):
        gv = g_ref[...]
        m2 = ADAM_B1 * m_ref[...] + (1.0 - ADAM_B1) * gv
        v2 = ADAM_B2 * v_ref[...] + (1.0 - ADAM_B2) * (gv * gv)
        d_ref[...] = -ADAM_LR * ((m2 / c1) / (jnp.sqrt(v2 / c2) + ADAM_EPS) + ADAM_WD * w_ref[...])
        m2_ref[...] = m2
        v2_ref[...] = v2

    blk = pl.BlockSpec((tb, c), lambda i: (i, 0))
    shp = jax.ShapeDtypeStruct((r, c), F32)
    return pl.pallas_call(body, name=name, grid=(r // tb,), in_specs=[blk] * 4, out_specs=[blk] * 3,
                          out_shape=[shp] * 3, compiler_params=_cp(("parallel",)))(w, g, m, v)


def _adamw_halves(w, mine, theirs, m, v, pos, name, cols=False):
    r, c = w.shape
    h = r if cols else r // 2
    tb = _rows(h, 128)
    nh = h // tb
    c1 = 1.0 - ADAM_B1 ** ADAM_STEP
    c2 = 1.0 - ADAM_B2 ** ADAM_STEP

    def body(pos_ref, w_ref, a_ref, b_ref, m_ref, v_ref, g_ref, d_ref, m2_ref, v2_ref):
        which = pl.program_id(1) if cols else pl.program_id(0) // nh
        gv = jnp.where(which == pos_ref[0], a_ref[...], b_ref[...])
        m2 = ADAM_B1 * m_ref[...] + (1.0 - ADAM_B1) * gv
        v2 = ADAM_B2 * v_ref[...] + (1.0 - ADAM_B2) * (gv * gv)
        g_ref[...] = gv
        d_ref[...] = -ADAM_LR * ((m2 / c1) / (jnp.sqrt(v2 / c2) + ADAM_EPS) + ADAM_WD * w_ref[...])
        m2_ref[...] = m2
        v2_ref[...] = v2

    if cols:
        full = pl.BlockSpec((tb, c // 2), lambda i, j, pref: (i, j))
        mine_spec = theirs_spec = pl.BlockSpec((tb, c // 2), lambda i, j, pref: (i, 0))
        grid = (nh, 2)
    else:
        full = pl.BlockSpec((tb, c), lambda i, pref: (i, 0))
        mine_spec = pl.BlockSpec((tb, c), lambda i, pref: (jnp.where(i // nh == pref[0], i % nh,
                                                                     jnp.where(pref[0] == 0, nh - 1, 0)), 0))
        theirs_spec = pl.BlockSpec((tb, c), lambda i, pref: (jnp.where(i // nh != pref[0], i % nh,
                                                                       jnp.where(pref[0] == 0, 0, nh - 1)), 0))
        grid = (r // tb,)
    shp = jax.ShapeDtypeStruct((r, c), F32)
    grid_spec = pltpu.PrefetchScalarGridSpec(num_scalar_prefetch=1, grid=grid,
                                             in_specs=[full, mine_spec, theirs_spec, full, full],
                                             out_specs=[full] * 4)
    return pl.pallas_call(body, name=name, grid_spec=grid_spec, out_shape=[shp] * 4,
                          compiler_params=_cp(("parallel",) * len(grid)))(pos, w, mine, theirs, m, v)


def _sum_own_half(g4, recv, pos, name, cols=False):
    _, r, c = g4.shape
    h, c = (r, c // 2) if cols else (r // 2, c)
    tb = _rows(h, 128)
    nh = h // tb

    def slot(j, pref):
        return (pref[1] + 1 + j) % N_CHIPS

    if cols:
        own = lambda j, i, pref: (slot(j, pref), i, pref[0])
    else:
        own = lambda j, i, pref: (slot(j, pref), pref[0] * nh + i, 0)
    same = lambda j, i, pref: (slot(j, pref), i, 0)

    def body(pos_ref, a_ref, b_ref, o_ref):
        o_ref[...] = (a_ref[...] + b_ref[...]).astype(BF16)

    grid_spec = pltpu.PrefetchScalarGridSpec(
        num_scalar_prefetch=1, grid=(N_CHIPS - 1, nh),
        in_specs=[pl.BlockSpec((1, tb, c), own), pl.BlockSpec((1, tb, c), same)],
        out_specs=pl.BlockSpec((1, tb, c), same))
    return pl.pallas_call(body, name=name, grid_spec=grid_spec,
                          out_shape=jax.ShapeDtypeStruct((N_CHIPS, h, c), BF16),
                          compiler_params=_cp(("parallel", "parallel")))(pos, g4, recv)


def _sum_chips(g4, recv, parts, pos, name, cols=False):
    _, r, c = g4.shape
    h, c = (r, c // 2) if cols else (r // 2, c)
    tb = _rows(h, 128)
    nh = h // tb
    own = (lambda i, pref: (pref[1], i, pref[0])) if cols else (lambda i, pref: (pref[1], pref[0] * nh + i, 0))

    def body(pos_ref, a_ref, b_ref, p_ref, o_ref):
        own = a_ref[0] + b_ref[0]
        o_ref[...] = ((own + p_ref[0].astype(F32)) + p_ref[1].astype(F32)) + p_ref[2].astype(F32)

    grid_spec = pltpu.PrefetchScalarGridSpec(
        num_scalar_prefetch=1, grid=(nh,),
        in_specs=[pl.BlockSpec((1, tb, c), own),
                  pl.BlockSpec((1, tb, c), lambda i, pref: (pref[1], i, 0)),
                  pl.BlockSpec((3, tb, c), lambda i, pref: (0, i, 0))],
        out_specs=pl.BlockSpec((tb, c), lambda i, pref: (i, 0)))
    return pl.pallas_call(body, name=name, grid_spec=grid_spec, out_shape=jax.ShapeDtypeStruct((h, c), F32),
                          compiler_params=_cp(("parallel",)))(pos, g4, recv, parts)


def _me():
    return lax.axis_index("x"), lax.axis_index("y"), lax.axis_index("c")


def _flip(v, bit):
    return (1 - v) if bit else v


CHIP_FLIPS = [(1, 0), (0, 1), (1, 1)]


def _allgather_weights(shards, after, cols=False):
    n = len(shards)

    def body(*refs):
        ins, outs, token = refs[:n], refs[n + 1:2 * n + 1], refs[2 * n + 1]
        send_sems, recv_sems = refs[2 * n + 2:]
        x, y, c = _me()
        chip = 2 * x + y
        sib = (x, y, 1 - c)

        def remote(src, dst, k, to):
            return pltpu.make_async_remote_copy(src_ref=src, dst_ref=dst, send_sem=send_sems.at[k],
                                                recv_sem=recv_sems.at[k], device_id=to, device_id_type=MESH)

        def half(ref, which):
            if cols:
                h = ref.shape[1] // 2
                return ref.at[:, pl.ds(which * h, h)]
            h = ref.shape[0] // 2
            return ref.at[pl.ds(which * h, h)]

        sends = []
        for t in range(n):
            for k, (fx, fy) in enumerate(CHIP_FLIPS):
                cp = remote(half(ins[t], c), half(outs[t].at[chip], c), 6 * t + k, (_flip(x, fx), _flip(y, fy), c))
                cp.start()
                sends.append(cp)
        for t in range(n):
            for k, (fx, fy) in enumerate(CHIP_FLIPS):
                landed = half(outs[t].at[2 * _flip(x, fx) + _flip(y, fy)], c)
                remote(landed, landed, 6 * t + k, (x, y, c)).wait_recv()
                fw = remote(landed, landed, 6 * t + 3 + k, sib)
                fw.start()
                sends.append(fw)
        for t in range(n):
            for k, (fx, fy) in enumerate(CHIP_FLIPS):
                got = half(outs[t].at[2 * _flip(x, fx) + _flip(y, fy)], 1 - c)
                remote(got, got, 6 * t + 3 + k, (x, y, c)).wait_recv()
        for cp in sends:
            cp.wait_send()
        token[...] = jnp.zeros_like(token)

    outs = pl.pallas_call(
        body, name="allgather_weights", in_specs=[HBM_SPEC] * n + [pl.BlockSpec(memory_space=pl.ANY)],
        out_specs=[HBM_SPEC] * n + [pl.BlockSpec(memory_space=pltpu.VMEM)],
        out_shape=[jax.ShapeDtypeStruct((N_CHIPS,) + s.shape, s.dtype) for s in shards] + [TOKEN],
        scratch_shapes=[pltpu.SemaphoreType.DMA((6 * n,)), pltpu.SemaphoreType.DMA((6 * n,))],
        compiler_params=pltpu.CompilerParams(has_side_effects=True))(*shards, after)
    return list(outs[:n]), outs[n]


def _share_halves(ghs, name):
    n = len(ghs)

    def body(*refs):
        ins, outs = refs[:n], refs[n:2 * n]
        send_sems, recv_sems = refs[2 * n:]
        x, y, c = _me()
        cps = []
        for t in range(n):
            cp = pltpu.make_async_remote_copy(
                src_ref=ins[t], dst_ref=outs[t], send_sem=send_sems.at[t], recv_sem=recv_sems.at[t],
                device_id=(x, y, 1 - c), device_id_type=MESH)
            cp.start()
            cps.append(cp)
        for cp in cps:
            cp.wait()

    return pl.pallas_call(
        body, name=name, in_specs=[HBM_SPEC] * n, out_specs=[HBM_SPEC] * n,
        out_shape=[jax.ShapeDtypeStruct(g.shape, g.dtype) for g in ghs],
        scratch_shapes=[pltpu.SemaphoreType.DMA((n,)), pltpu.SemaphoreType.DMA((n,))],
        compiler_params=pltpu.CompilerParams(has_side_effects=True))(*ghs)


SEM_SPEC = pl.BlockSpec(memory_space=pltpu.SEMAPHORE)
ANY_SPEC = pl.BlockSpec(memory_space=pl.ANY)
DATAFLOW = pltpu.SideEffectType.DATAFLOW_SIDE_EFFECTING


def _in_hbm(a):
    return pltpu.with_memory_space_constraint(a, pltpu.HBM)


def _push_start(srcs, land_shapes, route, peers, name):
    n, npeer = len(srcs), len(peers)
    lands = [lax.empty(shp, s.dtype) for shp, s in zip(land_shapes, srcs)]

    def body(*refs):
        ins, lnd = refs[:n], refs[n:2 * n]
        send_sems, recv_sems = refs[2 * n], refs[2 * n + 1]
        token = refs[-1]
        x, y, c = _me()
        for t in range(n):
            for k, (fx, fy, fc) in enumerate(peers):
                src, dst = route(ins[t], lnd[t], k, x, y, c)
                pltpu.make_async_remote_copy(
                    src_ref=src, dst_ref=dst, send_sem=send_sems.at[npeer * t + k],
                    recv_sem=recv_sems.at[npeer * t + k],
                    device_id=(_flip(x, fx), _flip(y, fy), _flip(c, fc)), device_id_type=MESH).start()
        token[...] = jnp.zeros_like(token)

    bufs = [_in_hbm(a) for a in list(srcs) + lands]
    outs = pl.pallas_call(
        body, name=name,
        out_shape=(pltpu.SemaphoreType.DMA((npeer * n,)), pltpu.SemaphoreType.DMA((npeer * n,)),
                   *[pltpu.HBM(b.shape, b.dtype) for b in bufs], TOKEN),
        in_specs=[HBM_SPEC] * (2 * n),
        out_specs=(SEM_SPEC, SEM_SPEC, *[HBM_SPEC] * (2 * n), pl.BlockSpec(memory_space=pltpu.VMEM)),
        input_output_aliases={i: 2 + i for i in range(2 * n)},
        compiler_params=pltpu.CompilerParams(has_side_effects=DATAFLOW))(*bufs)
    return outs[0], outs[1], list(outs[2:2 + n]), list(outs[2 + n:2 + 2 * n]), outs[-1]


def _push_wait(send_sems, recv_sems, srcs, lands, after, route, peers, name):
    n, npeer = len(srcs), len(peers)

    def body(*refs):
        ins, lnd = refs[:n], refs[n:2 * n]
        ssem, rsem = refs[2 * n], refs[2 * n + 1]
        x, y, c = _me()
        for t in range(n):
            for k, (fx, fy, fc) in enumerate(peers):
                src, dst = route(ins[t], lnd[t], k, x, y, c)
                cp = pltpu.make_async_remote_copy(
                    src_ref=src, dst_ref=dst, send_sem=ssem.at[npeer * t + k], recv_sem=rsem.at[npeer * t + k],
                    device_id=(_flip(x, fx), _flip(y, fy), _flip(c, fc)), device_id_type=MESH)
                cp.wait_send()
                cp.wait_recv()

    bufs = list(srcs) + list(lands)
    outs = pl.pallas_call(
        body, name=name, out_shape=tuple(pltpu.HBM(b.shape, b.dtype) for b in bufs),
        in_specs=[HBM_SPEC] * (2 * n) + [SEM_SPEC, SEM_SPEC, ANY_SPEC], out_specs=tuple([HBM_SPEC] * (2 * n)),
        input_output_aliases={i: i for i in range(2 * n)},
        compiler_params=pltpu.CompilerParams(has_side_effects=DATAFLOW))(*bufs, send_sems, recv_sems, after)
    return list(outs[:n]), list(outs[n:])


OTHER_CHIPS = [(fx, fy, 0) for fx, fy in CHIP_FLIPS]
SIBLING = [(0, 0, 1)]


def _route_gather(src, land, k, x, y, c):
    return src, land.at[2 * x + y]


def _route_gather_wait(src, land, k, x, y, c):
    fx, fy = CHIP_FLIPS[k]
    return src, land.at[2 * _flip(x, fx) + _flip(y, fy)]


def _route_scatter(src, land, k, x, y, c):
    fx, fy = CHIP_FLIPS[k]
    return src.at[2 * _flip(x, fx) + _flip(y, fy)], land.at[k]


def _route_exchange(src, land, k, x, y, c):
    h = land.shape[1]
    return src.at[:, pl.ds((1 - c) * h, h)], land


def _route_exchange_cols(src, land, k, x, y, c):
    h = land.shape[2]
    return src.at[:, :, pl.ds((1 - c) * h, h)], land


def _allreduce_small(v):
    r = v.shape[0]

    def body(v_ref, o_ref, buf, send_sems, recv_sems):
        x, y, c = _me()
        me = 4 * x + 2 * y + c
        buf[0] = v_ref[...]
        cps = []
        for k in range(1, 8):
            kx, ky, kc = (k >> 2) & 1, (k >> 1) & 1, k & 1
            cp = pltpu.make_async_remote_copy(
                src_ref=v_ref, dst_ref=buf.at[k], send_sem=send_sems.at[k - 1], recv_sem=recv_sems.at[k - 1],
                device_id=(_flip(x, kx), _flip(y, ky), _flip(c, kc)), device_id_type=MESH)
            cp.start()
            cps.append(cp)
        for cp in cps:
            cp.wait()
        acc = buf[me]
        for d in range(1, 8):
            acc = acc + buf[jnp.bitwise_xor(me, d)]
        o_ref[...] = acc

    vm = pl.BlockSpec(memory_space=pltpu.VMEM)
    return pl.pallas_call(
        body, name="allreduce_small", in_specs=[vm], out_specs=vm, out_shape=jax.ShapeDtypeStruct(v.shape, F32),
        scratch_shapes=[pltpu.VMEM((8, r, LANES), F32), pltpu.SemaphoreType.DMA((7,)),
                        pltpu.SemaphoreType.DMA((7,))],
        compiler_params=pltpu.CompilerParams(has_side_effects=True, vmem_limit_bytes=VMEM_LIMIT))(v)


def _grad_exchange_start(g4, tag, cols=False):
    land = (N_CHIPS, g4.shape[1], g4.shape[2] // 2) if cols else (N_CHIPS, g4.shape[1] // 2, g4.shape[2])
    route = _route_exchange_cols if cols else _route_exchange
    send_sems, recv_sems, srcs, lands, token = _push_start(
        [g4], [land], route, SIBLING, name="grad_exchange_start_" + tag)
    return (send_sems, recv_sems, srcs, lands, tag, cols), token


def _grad_scatter_start(state, pos, after):
    send_sems, recv_sems, srcs, lands, tag, cols = state
    route = _route_exchange_cols if cols else _route_exchange
    (g4,), (recv,) = _push_wait(send_sems, recv_sems, srcs, lands, after, route, SIBLING,
                                name="grad_exchange_wait_" + tag)
    return _grad_pair_scatter(g4, recv, pos, tag, cols)


def _grad_pair_scatter(g4, recv, pos, tag, cols=False):
    p16 = _sum_own_half(g4, recv, pos, name="grad_sum_pair_" + tag, cols=cols)
    send_sems, recv_sems, srcs, lands, token = _push_start(
        [p16], [(3,) + p16.shape[1:]], _route_scatter, OTHER_CHIPS, name="grad_scatter_start_" + tag)
    return (g4, recv, send_sems, recv_sems, srcs, lands, tag, cols), token


def _grad_reduce_finish(state, pos, after):
    g4, recv, send_sems, recv_sems, srcs, lands, tag, cols = state
    parts = _push_wait(send_sems, recv_sems, srcs, lands, after, _route_scatter, OTHER_CHIPS,
                       name="grad_scatter_wait_" + tag)[1][0]
    mine = _sum_chips(g4, recv, parts, pos, name="grad_sum_chips_" + tag, cols=cols)
    return mine, _share_halves([mine], name="grad_share_halves_" + tag)[0]


def _local_step(x, tgt, p, w_in_t, w_in_dt, hooks):
    t = x.shape[0]
    tables = _rope_tables(t)
    sinks = p['sinks'].reshape(N_Q_HEADS)

    def told(name, value):
        return tuple(hooks.grad_ready(name, value))

    xn = _rmsnorm_fwd(x, p['norm_mix'], "norm_mix_fwd", deps=hooks.first_deps)
    proj = _matmul(xn, w_in_t, mode='nt', name="in_proj", n_limit=MAIN_WIDTH)
    dt_raw = _matmul(xn, w_in_dt, mode='nt', name="in_proj_dt")[:, :SSD_HEADS]
    attn = _attn_fwd(proj, sinks, tables)
    conv_b = p['ssd_conv_b']
    xbc, xbc_pre = _conv_silu_fwd(proj, p['ssd_conv_w'], conv_b, col0=O_XBC, width=CONV_CH, name="ssd_conv_fwd")
    sp = _ssd_params(dt_raw, p['dt_bias'].reshape(-1), p['a_log'].reshape(-1), p['ssd_d'].reshape(-1))
    y, states = _ssd_fwd(xbc, sp)
    mix = _mix_fwd(attn, y, proj, p['attn_out_norm'], p['ssd_norm'])
    w_out = hooks.weight('w_out', mix)
    h1 = _matmul(mix, w_out, mode='nn', name="out_proj", add=x)
    hn = _rmsnorm_fwd(h1, p['norm_ffn'], "norm_ffn_fwd")
    w_up = hooks.weight('w_up', hn)
    u0 = _matmul(hn, w_up, mode='nn', name="ffn_up", b_owner=True, tn=1408)
    a, u = _ffn_act_fwd(u0, p['ffn_conv_w'], p['ffn_conv_b'])
    w_down = hooks.weight('w_down', a)
    h2 = _matmul(a, w_down, mode='nn', name="ffn_down", add=h1, tk=2816)
    loss, dh2, dh2_16, g_norm_final = _final_loss(h2, p['norm_final'].reshape(1, D_MODEL), tgt)

    g = {}
    da = _matmul(dh2_16, w_down, mode='nt', name="ffn_down_dx", out_dtype=BF16, tn=1408)
    g['w_down'] = _matmul(a, dh2_16, mode='tn', name="ffn_down_dw", tm=1408)
    dep = told('w_down', g['w_down'])
    du0, dcw, dcb = _ffn_act_bwd(u0, u, p['ffn_conv_w'], da)
    g['ffn_conv_w'] = dcw.transpose(1, 0, 2).reshape(FFN_CONV, 2 * D_FF)
    g['ffn_conv_b'] = dcb.transpose(1, 0, 2).reshape(1, 2 * D_FF)
    g['w_up'] = _matmul(hn, du0, mode='tn', name="ffn_up_dw", deps=dep, b_halves=True, owner_major=True,
                        tn=1408)
    dep = told('w_up', g['w_up'])
    dhn = _matmul(du0, w_up, mode='nt', name="ffn_up_dx", out_dtype=BF16, deps=dep, a_halves=True,
                  b_owner=True, tk=2816)
    dh1, dh1_16, g['norm_ffn'] = _rmsnorm_bwd(h1, p['norm_ffn'], dhn, dh2, "norm_ffn_bwd")

    g['w_out'] = _matmul(mix, dh1_16, mode='tn', name="out_proj_dw")
    dep = told('w_out', g['w_out'])
    dmix = _matmul(dh1_16, w_out, mode='nt', name="out_proj_dx", out_dtype=BF16, deps=dep)
    dattn, dy, dz, g['attn_out_norm'], g['ssd_norm'] = _mix_bwd(dmix, attn, y, proj, p['attn_out_norm'],
                                                                p['ssd_norm'])
    dq, dk, dv, dsink = _attn_bwd(proj, sinks, tables, dattn)
    g['sinks'] = dsink[:, :, 0].reshape(1, N_Q_HEADS)
    dxs, dbm, dcm, ddt8, dpar = _ssd_bwd(xbc, xbc_pre, sp, states, dy)
    dpar = dpar[:, :, ::SSD_HEAD_DIM]
    g['dt_bias'] = dpar[:, 0, :].reshape(1, SSD_HEADS)
    g['a_log'] = dpar[:, 1, :].reshape(1, SSD_HEADS)
    g['ssd_d'] = dpar[:, 2, :].reshape(1, SSD_HEADS)
    dxbc, g['ssd_conv_w'], g['ssd_conv_b'] = _ssd_conv_bwd(proj, p['ssd_conv_w'], dxs, dbm, dcm, col0=O_XBC,
                                                           name="ssd_conv_bwd")
    dproj = jnp.concatenate([dq, dk, dv, dz, dxbc], axis=1)
    ddt = ddt8.transpose(2, 0, 1).reshape(t, SSD_HEADS)
    ddt_pad = jnp.pad(ddt, ((0, 0), (0, LANES - SSD_HEADS))).astype(BF16)
    g['w_in'] = (_matmul(dproj, xn, mode='tn', name="in_proj_dw", m_rows=IN_PROJ_WIDTH),
                 _matmul(ddt_pad, xn, mode='tn', name="in_proj_dt_dw"))
    dep = told('w_in', g['w_in'])
    dxn_dt = _matmul(ddt_pad, w_in_dt, mode='nn', name="in_proj_dt_dx", deps=dep)
    dxn = _matmul(dproj, w_in_t, mode='nn', name="in_proj_dx", out_dtype=BF16, add=dxn_dt, k_limit=MAIN_WIDTH,
                  tk=2304)
    dx, _, g['norm_mix'] = _rmsnorm_bwd(x, p['norm_mix'], dxn, dh1, "norm_mix_bwd")
    g['norm_final'] = g_norm_final
    return loss, dx, g


def _pack(arrs):
    flat = jnp.concatenate([a.reshape(-1) for a in arrs])
    n = flat.shape[0]
    rows = -(-n // LANES)
    rows = -(-rows // 8) * 8
    return jnp.pad(flat, (0, rows * LANES - n)).reshape(rows, LANES)


def _unpack(packed, shapes):
    flat = packed.reshape(-1)
    out, off = [], 0
    for s in shapes:
        n = 1
        for d in s:
            n *= d
        out.append(flat[off:off + n].reshape(s))
        off += n
    return out


class _StepHooks:
    def __init__(self, first_deps, weight, grad_ready):
        self.first_deps = first_deps
        self.weight = weight
        self.grad_ready = grad_ready


def kernel(x, norm_mix, w_in, sinks, attn_out_norm, ssd_conv_w, ssd_conv_b, dt_bias, a_log, ssd_d, ssd_norm, w_out, norm_ffn, w_up, ffn_conv_w, ffn_conv_b, w_down, norm_final, loss_target, m_norm_mix, m_w_in, m_sinks, m_attn_out_norm, m_ssd_conv_w, m_ssd_conv_b, m_dt_bias, m_a_log, m_ssd_d, m_ssd_norm, m_w_out, m_norm_ffn, m_w_up, m_ffn_conv_w, m_ffn_conv_b, m_w_down, m_norm_final, v_norm_mix, v_w_in, v_sinks, v_attn_out_norm, v_ssd_conv_w, v_ssd_conv_b, v_dt_bias, v_a_log, v_ssd_d, v_ssd_norm, v_w_out, v_norm_ffn, v_w_up, v_ffn_conv_w, v_ffn_conv_b, v_w_down, v_norm_final):
    args = dict(locals())
    w = {n: args[n] for n in WEIGHTS}
    m = {n: args['m_' + n] for n in WEIGHTS}
    v = {n: args['v_' + n] for n in WEIGHTS}
    xi, yi, ci = _me()
    chip = 2 * xi + yi
    pos = jnp.stack([ci, chip]).astype(jnp.int32)

    def place(shard, full_cols):
        z = jnp.zeros((shard.shape[0], full_cols), F32)
        return lax.dynamic_update_slice(z, shard * 0.5, (0, chip * shard.shape[1]))

    conv_pack = _pack([place(ssd_conv_w[0], CONV_CH), place(ffn_conv_w[0], 2 * D_FF)])
    conv_full = _allreduce_small(conv_pack)
    ssd_conv_w_full, ffn_conv_w_full = _unpack(conv_full, [(SSD_CONV, CONV_CH), (FFN_CONV, 2 * D_FF)])

    w_in_t, m_in_t, v_in_t = (jnp.transpose(a[0]) for a in (w_in, m_w_in, v_w_in))
    in_shard = w_in_t.astype(BF16)
    (gathered,), order = _allgather_weights([in_shard], conv_full, cols=True)
    full_in_t = lax.dynamic_update_slice(gathered, in_shard[None], (chip, 0, 0)).reshape(IN_PROJ_WIDTH, D_MODEL)
    w_in_dt = jnp.pad(full_in_t[MAIN_WIDTH:], ((0, LANES - SSD_HEADS), (0, 0)))
    gathers = {}
    order = order[:1, :1]
    for n, shard in (('w_out', w_out[0]), ('w_up', w_up[0]), ('w_down', w_down[0])):
        shard = (shard + order).astype(BF16)
        gathers[n] = _push_start([shard], [(N_CHIPS,) + shard.shape], _route_gather, OTHER_CHIPS,
                                 name="gather_start_" + n)
        order = gathers[n][4][:1, :1]
    first_deps = [gathers['w_down'][4]]

    def weight(name, after):
        send_sems, recv_sems, srcs, lands, _ = gathers[name]
        (own,), (got,) = _push_wait(send_sems, recv_sems, srcs, lands, after, _route_gather_wait, OTHER_CHIPS,
                                    name="gather_wait_" + name)
        whole = lax.dynamic_update_slice(got, own[None], (chip, 0, 0))
        return whole if name == 'w_up' else whole.reshape(-1, D_MODEL)

    reductions, exchanging = {}, {}

    def flush(after):
        tokens = []
        for prev in list(exchanging):
            reductions[prev], token = _grad_scatter_start(exchanging.pop(prev), pos, after)
            tokens.append(token)
        return tokens

    def grad_ready(name, value):
        if name == 'w_in':
            main, dtp = value
            value = lax.dynamic_update_slice(main, dtp[:SSD_HEADS], (MAIN_WIDTH, 0))
        g4 = value if value.ndim == 3 else value.reshape(N_CHIPS, -1, value.shape[1])
        tokens = flush(g4)
        exchanging[name], token = _grad_exchange_start(g4, name, cols=(name == 'w_in'))
        return tokens + [token]

    small = {
        'norm_mix': norm_mix, 'sinks': sinks, 'attn_out_norm': attn_out_norm, 'ssd_conv_w': ssd_conv_w_full,
        'ssd_conv_b': ssd_conv_b, 'dt_bias': dt_bias, 'a_log': a_log, 'ssd_d': ssd_d, 'ssd_norm': ssd_norm,
        'norm_ffn': norm_ffn, 'ffn_conv_w': ffn_conv_w_full, 'ffn_conv_b': ffn_conv_b, 'norm_final': norm_final,
    }
    loss, dx, g = _local_step(x[0], loss_target[0], small, full_in_t, w_in_dt,
                              _StepHooks(tuple(first_deps), weight, grad_ready))

    small_names = [n for n in WEIGHTS if n not in BIG]
    small_g = [loss[:, :1]] + [g[n] for n in small_names]
    small_shapes = [(1, 1)] + [tuple(a.shape) for a in small_g[1:]]
    reduced = _allreduce_small(_pack(small_g))
    started = flush(reduced)[-1]
    red = _unpack(reduced, small_shapes)
    loss_out = red[0].reshape(())
    gsm = dict(zip(small_names, red[1:]))
    gsm['ssd_conv_w'] = lax.dynamic_slice(gsm['ssd_conv_w'], (0, chip * ssd_conv_w.shape[2]),
                                          (SSD_CONV, ssd_conv_w.shape[2]))
    gsm['ffn_conv_w'] = lax.dynamic_slice(gsm['ffn_conv_w'], (0, chip * ffn_conv_w.shape[2]),
                                          (FFN_CONV, ffn_conv_w.shape[2]))

    grads, deltas, new_m, new_v = {}, {}, {}, {}
    after = started
    for n in ('w_down', 'w_up', 'w_out', 'w_in'):
        mine, theirs = _grad_reduce_finish(reductions[n], pos, after)
        if n == 'w_in':
            outs = _adamw_halves(w_in_t, mine, theirs, m_in_t, v_in_t, pos, name="adamw_" + n, cols=True)
            outs = [jnp.transpose(o) for o in outs]
        else:
            outs = _adamw_halves(w[n][0], mine, theirs, m[n][0], v[n][0], pos, name="adamw_" + n)
        after = outs[1]
        grads[n], deltas[n], new_m[n], new_v[n] = [o[None] for o in outs]
    shapes = [tuple(w[n].shape) for n in small_names]
    gp = _pack([gsm[n] for n in small_names])
    d, m2, v2 = _adamw(_pack([w[n] for n in small_names]), gp, _pack([m[n] for n in small_names]),
                       _pack([v[n] for n in small_names]), name="adamw_small")
    for n, gg, dd, mm, vv in zip(small_names, _unpack(gp, shapes), _unpack(d, shapes), _unpack(m2, shapes),
                                 _unpack(v2, shapes)):
        grads[n], deltas[n], new_m[n], new_v[n] = gg, dd, mm, vv

    return (loss_out, dx[None], *[grads[n] for n in WEIGHTS], *[deltas[n] for n in WEIGHTS],
            *[new_m[n] for n in WEIGHTS], *[new_v[n] for n in WEIGHTS])
```

```python
import functools

import jax
import jax.numpy as jnp
from jax import lax
from jax.experimental import pallas as pl
from jax.experimental.pallas import tpu as pltpu

F32 = jnp.float32
BF16 = jnp.bfloat16

D_MODEL = 2048
N_Q_HEADS = 32
N_KV_HEADS = 8
HEAD_DIM = 64
WINDOW = 128
ATTN_BLOCK = 128
ROT_DIM = 16
ROPE_THETA = 500000.0
SSD_HEADS = 32
SSD_HEAD_DIM = 64
SSD_INNER = 2048
SSD_GROUPS = 8
SSD_STATE = 128
SSD_CONV = 4
SSD_CHUNK = 128
ATTN_WIDTH = 2048
KV_WIDTH = 512
BC_WIDTH = 1024
CONV_CH = 4096
IN_PROJ_WIDTH = 9248
MAIN_WIDTH = 9216
D_FF = 5632
FFN_CONV = 3
EPS = 1e-6
O_Q, O_K, O_V, O_Z, O_XBC, O_DT = 0, 2048, 2560, 3072, 5120, 9216

ADAM_LR = 0.001
ADAM_B1 = 0.9
ADAM_B2 = 0.999
ADAM_EPS = 1e-08
ADAM_WD = 0.01
ADAM_STEP = 10

N_CHIPS = 4
NEG = -1e30
LANES = 128
VMEM_LIMIT = 48 * 1024 * 1024
MESH = pl.DeviceIdType.MESH
HBM_SPEC = pl.BlockSpec(memory_space=pltpu.HBM)
TOKEN = jax.ShapeDtypeStruct((8, LANES), F32)

WEIGHTS = ['norm_mix', 'w_in', 'sinks', 'attn_out_norm', 'ssd_conv_w', 'ssd_conv_b', 'dt_bias', 'a_log', 'ssd_d',
           'ssd_norm', 'w_out', 'norm_ffn', 'w_up', 'ffn_conv_w', 'ffn_conv_b', 'w_down', 'norm_final']
BIG = ['w_in', 'w_out', 'w_up', 'w_down']


def _cp(sem=None, vmem=VMEM_LIMIT):
    kw = {'vmem_limit_bytes': vmem}
    if sem is not None:
        kw['dimension_semantics'] = sem
    return pltpu.CompilerParams(**kw)


def _tile(n, pref):
    if n <= pref:
        return n
    t = (pref // LANES) * LANES
    while t > LANES and n % t:
        t -= LANES
    assert n % t == 0, (n, pref)
    return t


def _rows(n, pref):
    t = min(n, pref)
    while n % t:
        t -= 8
    if 4 * t < pref:
        t = pref
        while n % t:
            t += 8
    return t


def _iota(shape, dim):
    return lax.broadcasted_iota(jnp.int32, shape, dim)


def _dot(a, b, mode='nn'):
    dn = {'nn': (((1,), (0,)), ((), ())), 'nt': (((1,), (1,)), ((), ())), 'tn': (((0,), (0,)), ((), ()))}[mode]
    return lax.dot_general(a.astype(BF16), b.astype(BF16), dn, preferred_element_type=F32)


def _dot_exact(a, b):
    return lax.dot_general(a, b, (((1,), (0,)), ((), ())), precision=lax.Precision.HIGHEST,
                           preferred_element_type=F32)


def _sigmoid(x):
    return 1.0 / (1.0 + jnp.exp(-x))


def _softplus(x):
    return jnp.maximum(x, 0.0) + jnp.log(1.0 + jnp.exp(-jnp.abs(x)))


def _matmul(a, b, *, mode, name, out_dtype=F32, add=None, deps=(), tm=1024, tn=1024, tk=2048,
            a_halves=False, b_halves=False, b_owner=False, owner_major=False, n_limit=None, k_limit=None,
            m_rows=None):
    ash, bsh = (a.shape[1:] if a_halves else a.shape), (b.shape[1:] if (b_halves or b_owner) else b.shape)
    if mode == 'nn':
        (m, k), (k2, n) = ash, bsh
    elif mode == 'nt':
        (m, k), (n, k2) = ash, bsh
    else:
        (k, m), (k2, n) = ash, bsh
    if n_limit is not None:
        assert mode == 'nt' and n_limit <= n
        n = n_limit
    if k_limit is not None:
        assert mode == 'nn' and k_limit <= k2
        k2 = k_limit
    if a_halves:
        assert mode == 'nt'
        k = 2 * k
    if b_halves:
        assert mode == 'tn'
        n = 2 * n
    if b_owner:
        assert mode in ('nn', 'nt')
        if mode == 'nn':
            n = 4 * n
        else:
            k2 = 4 * k2
    assert k == k2, (a.shape, b.shape, mode)
    tm = _tile(m, tm)
    tn = _tile(n // 4 if (owner_major or (b_owner and mode == 'nn')) else (n // 2 if b_halves else n), tn)
    tk = _tile(k // 4 if (b_owner and mode == 'nt') else (k // 2 if a_halves else k), tk)
    nk = k // tk
    has_add = add is not None
    assert not (has_add and owner_major)

    def body(*refs):
        a_ref, b_ref = refs[:2]
        add_ref = refs[2] if has_add else None

        def finish(r, o_ref):
            if has_add:
                r = r + add_ref[...].astype(F32)
            o_ref[...] = r.astype(out_dtype)

        if nk == 1:
            finish(_dot(a_ref[...], b_ref[...], mode), refs[-1])
            return
        o_ref, acc = refs[-2:]
        kk = pl.program_id(2)

        @pl.when(kk == 0)
        def _():
            acc[...] = _dot(a_ref[...], b_ref[...], mode)

        @pl.when((kk > 0) & (kk < nk - 1))
        def _():
            acc[...] += _dot(a_ref[...], b_ref[...], mode)

        @pl.when(kk == nk - 1)
        def _():
            finish(acc[...] + _dot(a_ref[...], b_ref[...], mode), o_ref)

    if mode == 'tn':
        a_spec = pl.BlockSpec((tk, tm), lambda i, j, kk: (kk, i))
    elif a_halves:
        nkh = nk // 2
        a_spec = pl.BlockSpec((None, tm, tk), lambda i, j, kk: (kk // nkh, i, kk % nkh))
    else:
        a_spec = pl.BlockSpec((tm, tk), lambda i, j, kk: (i, kk))
    if mode == 'nt' and b_owner:
        nkq = nk // 4
        b_spec = pl.BlockSpec((None, tn, tk), lambda i, j, kk: (kk // nkq, j, kk % nkq))
    elif mode == 'nt':
        b_spec = pl.BlockSpec((tn, tk), lambda i, j, kk: (j, kk))
    elif b_owner:
        njq = (n // 4) // tn
        b_spec = pl.BlockSpec((None, tk, tn), lambda i, j, kk: (j // njq, kk, j % njq))
    elif b_halves:
        njh = (n // 2) // tn
        b_spec = pl.BlockSpec((None, tk, tn), lambda i, j, kk: (j // njh, kk, j % njh))
    else:
        b_spec = pl.BlockSpec((tk, tn), lambda i, j, kk: (kk, j))
    if owner_major:
        njo = (n // 4) // tn
        o_spec = pl.BlockSpec((None, tm, tn), lambda i, j, kk: (j // njo, i, j % njo))
        out_shape = jax.ShapeDtypeStruct((N_CHIPS, m, n // 4), out_dtype)
    else:
        o_spec = pl.BlockSpec((tm, tn), lambda i, j, kk: (i, j))
        out_shape = jax.ShapeDtypeStruct((m if m_rows is None else m_rows, n), out_dtype)
    dep_spec = pl.BlockSpec((8, LANES), lambda i, j, kk: (0, 0))
    in_specs = [a_spec, b_spec] + ([pl.BlockSpec((tm, tn), lambda i, j, kk: (i, j))] if has_add else [])
    in_specs += [dep_spec] * len(deps)
    args = (a, b) + ((add,) if has_add else ()) + tuple(deps)
    return pl.pallas_call(
        body, name=name, grid=(m // tm, n // tn, nk), in_specs=in_specs, out_specs=o_spec, out_shape=out_shape,
        scratch_shapes=[pltpu.VMEM((tm, tn), F32)] if nk > 1 else [],
        compiler_params=_cp(("parallel", "parallel", "arbitrary")))(*args)


def _rmsnorm_fwd(x, g, name, deps=()):
    t, d = x.shape
    tb = _rows(t, 256)

    def body(x_ref, g_ref, *rest):
        o_ref = rest[-1]
        xv = x_ref[...]
        r = lax.rsqrt(jnp.mean(xv * xv, axis=-1, keepdims=True) + EPS)
        o_ref[...] = (xv * r * g_ref[...]).astype(BF16)

    dep_spec = pl.BlockSpec((8, LANES), lambda i: (0, 0))
    return pl.pallas_call(
        body, name=name, grid=(t // tb,),
        in_specs=[pl.BlockSpec((tb, d), lambda i: (i, 0)), pl.BlockSpec((1, d), lambda i: (0, 0))]
        + [dep_spec] * len(deps),
        out_specs=pl.BlockSpec((tb, d), lambda i: (i, 0)), out_shape=jax.ShapeDtypeStruct((t, d), BF16),
        compiler_params=_cp(("parallel",)))(x, g, *deps)


def _rmsnorm_bwd(x, g, dy, res, name, deps=()):
    t, d = x.shape
    tb = _rows(t, 256)

    def body(x_ref, g_ref, dy_ref, res_ref, *rest):
        dx_ref, dx16_ref, dg_ref = rest[-3:]
        i = pl.program_id(0)
        xv = x_ref[...]
        dyv = dy_ref[...].astype(F32)
        r = lax.rsqrt(jnp.mean(xv * xv, axis=-1, keepdims=True) + EPS)
        u = dyv * g_ref[...]
        dx = r * u - xv * (r * r * r * jnp.mean(u * xv, axis=-1, keepdims=True)) + res_ref[...]
        dx_ref[...] = dx
        dx16_ref[...] = dx.astype(BF16)
        part = jnp.sum(dyv * xv * r, axis=0, keepdims=True)

        @pl.when(i == 0)
        def _():
            dg_ref[...] = part

        @pl.when(i > 0)
        def _():
            dg_ref[...] += part

    row = pl.BlockSpec((tb, d), lambda i: (i, 0))
    vec = pl.BlockSpec((1, d), lambda i: (0, 0))
    return pl.pallas_call(
        body, name=name, grid=(t // tb,),
        in_specs=[row, vec, row, row] + [pl.BlockSpec((8, LANES), lambda i: (0, 0))] * len(deps),
        out_specs=[row, row, vec],
        out_shape=[jax.ShapeDtypeStruct((t, d), F32), jax.ShapeDtypeStruct((t, d), BF16),
                   jax.ShapeDtypeStruct((1, d), F32)],
        compiler_params=_cp(("arbitrary",)))(x, g, dy, res, *deps)


def _final_loss(h, g, tgt):
    t, d = h.shape
    tb = _rows(t, 256)

    def body(h_ref, g_ref, t_ref, loss_ref, dh_ref, dh16_ref, dg_ref):
        i = pl.program_id(0)
        hv = h_ref[...]
        gv = g_ref[...]
        r = lax.rsqrt(jnp.mean(hv * hv, axis=-1, keepdims=True) + EPS)
        y = hv * r * gv
        diff = y - t_ref[...]
        lpart = jnp.sum(jnp.sum(diff * diff, axis=1, keepdims=True), axis=0, keepdims=True) * (0.5 / d)
        dy = diff * (1.0 / d)
        u = dy * gv
        dh = r * u - hv * (r * r * r * jnp.mean(u * hv, axis=-1, keepdims=True))
        dh_ref[...] = dh
        dh16_ref[...] = dh.astype(BF16)
        gpart = jnp.sum(dy * hv * r, axis=0, keepdims=True)
        lrow = jnp.broadcast_to(lpart, (1, LANES))

        @pl.when(i == 0)
        def _():
            loss_ref[...] = lrow
            dg_ref[...] = gpart

        @pl.when(i > 0)
        def _():
            loss_ref[...] += lrow
            dg_ref[...] += gpart

    row = pl.BlockSpec((tb, d), lambda i: (i, 0))
    vec = pl.BlockSpec((1, d), lambda i: (0, 0))
    return pl.pallas_call(
        body, name="final_loss", grid=(t // tb,), in_specs=[row, vec, row],
        out_specs=[pl.BlockSpec((1, LANES), lambda i: (0, 0)), row, row, vec],
        out_shape=[jax.ShapeDtypeStruct((1, LANES), F32), jax.ShapeDtypeStruct((t, d), F32),
                   jax.ShapeDtypeStruct((t, d), BF16), jax.ShapeDtypeStruct((1, d), F32)],
        compiler_params=_cp(("arbitrary",)))(h, g, tgt)


def _rope_tables(t):
    pos = jnp.arange(t, dtype=F32)
    inv = 1.0 / (ROPE_THETA ** (jnp.arange(0, ROT_DIM, 2, dtype=F32) / ROT_DIM))
    ang = pos[:, None] * inv[None, :]
    cos, sin = jnp.cos(ang), jnp.sin(ang)
    half = ROT_DIM // 2
    rest = HEAD_DIM - ROT_DIM
    c = jnp.concatenate([cos, cos, jnp.ones((t, rest), F32)], axis=1)
    s1 = jnp.concatenate([-sin, jnp.zeros((t, half + rest), F32)], axis=1)
    s2 = jnp.concatenate([jnp.zeros((t, half), F32), sin, jnp.zeros((t, rest), F32)], axis=1)
    return jnp.concatenate([jnp.tile(v, (1, LANES // HEAD_DIM)) for v in (c, s1, s2)], axis=1)


def _split_tables(tab):
    return tab[:, :LANES], tab[:, LANES:2 * LANES], tab[:, 2 * LANES:]


def _rope(x, c, s1, s2):
    half = ROT_DIM // 2
    return x * c + pltpu.roll(x, LANES - half, 1) * s1 + pltpu.roll(x, half, 1) * s2


def _rope_t(g, c, s1, s2):
    half = ROT_DIM // 2
    return g * c + pltpu.roll(g * s1, half, 1) + pltpu.roll(g * s2, LANES - half, 1)


def _band_masks(i, heads):
    n = heads * ATTN_BLOCK
    q = jnp.bitwise_and(_iota((n, ATTN_BLOCK), 0), ATTN_BLOCK - 1)
    j = _iota((n, ATTN_BLOCK), 1)
    upper = j > q
    return upper, upper & (j < jnp.where(i > 0, 0, ATTN_BLOCK))


def _fold_band(full, upper):
    return jnp.where(upper, full[:, :ATTN_BLOCK], full[:, ATTN_BLOCK:])


def _unfold_band(band, upper):
    return jnp.concatenate([jnp.where(upper, band, 0.0), jnp.where(upper, 0.0, band)], axis=1)


def _half_masks():
    lane = _iota((1, LANES), 1)
    return [(lane < HEAD_DIM).astype(F32), (lane >= HEAD_DIM).astype(F32)]


def _stack_heads(blocks, hm, j):
    pieces = []
    for r in range(4):
        qb, half = (4 * j + r) // 2, (4 * j + r) % 2
        piece = blocks[qb] * hm[half]
        if half != j:
            piece = pltpu.roll(piece, HEAD_DIM, 1)
        pieces.append(piece)
    return jnp.concatenate(pieces, axis=0)


def _unstack_heads(stacked, j):
    out = []
    for qb in (2 * j, 2 * j + 1):
        acc = None
        for half in range(2):
            r = 2 * qb + half - 4 * j
            piece = stacked[r * ATTN_BLOCK:(r + 1) * ATTN_BLOCK]
            if half != j:
                piece = pltpu.roll(piece, HEAD_DIM, 1)
            acc = piece if acc is None else acc + piece
        out.append((qb, acc))
    return out


def _sink_column(sink_ref, base):
    return jnp.concatenate([jnp.full((ATTN_BLOCK, 1), sink_ref[base + r], F32) for r in range(4)], axis=0)


def _attn_specs(nb_clamp):
    blk = ATTN_BLOCK
    kb, vb = O_K // LANES, O_V // LANES

    def cur(i):
        return jnp.minimum(i, nb_clamp)

    def prev(i):
        return jnp.maximum(jnp.minimum(i, nb_clamp + 1) - 1, 0)

    q = pl.BlockSpec((blk, 512), lambda p, i: (cur(i), p))
    kc = pl.BlockSpec((blk, LANES), lambda p, i: (cur(i), kb + p))
    kp = pl.BlockSpec((blk, LANES), lambda p, i: (prev(i), kb + p))
    vc = pl.BlockSpec((blk, LANES), lambda p, i: (cur(i), vb + p))
    vp = pl.BlockSpec((blk, LANES), lambda p, i: (prev(i), vb + p))
    tc = pl.BlockSpec((blk, 3 * LANES), lambda p, i: (cur(i), 0))
    tp = pl.BlockSpec((blk, 3 * LANES), lambda p, i: (prev(i), 0))
    return q, kc, kp, vc, vp, tc, tp


def _attn_fwd(proj, sinks, tables):
    t = proj.shape[0]
    nb = t // ATTN_BLOCK
    scale = HEAD_DIM ** -0.5

    def body(sink_ref, q_ref, kc_ref, kp_ref, vc_ref, vp_ref, tc_ref, tp_ref, o_ref):
        p = pl.program_id(0)
        i = pl.program_id(1)
        cc, s1c, s2c = _split_tables(tc_ref[...])
        kband = jnp.concatenate([_rope(kp_ref[...], *_split_tables(tp_ref[...])),
                                 _rope(kc_ref[...], cc, s1c, s2c)], axis=0).astype(BF16)
        vband = jnp.concatenate([vp_ref[...], vc_ref[...]], axis=0)
        hm = _half_masks()
        vsel = [(vband * hm[j]).astype(BF16) for j in range(2)]
        upper, dropped = _band_masks(i, 1)
        qr = [_rope(q_ref[:, qb * LANES:(qb + 1) * LANES], cc, s1c, s2c) for qb in range(4)]

        def scores(hh):
            qb, half, j = hh // 2, hh % 2, hh // 4
            qs = qr[qb] * hm[half]
            if half != j:
                qs = pltpu.roll(qs, HEAD_DIM, 1)
            return _dot(qs, kband, 'nt')

        ahead = scores(0)
        acc = None
        for hh in range(8):
            qb, half, j = hh // 2, hh % 2, hh // 4
            raw = ahead
            if hh + 1 < 8:
                ahead = scores(hh + 1)
            s = jnp.where(dropped, NEG, _fold_band(raw, upper) * scale)
            sink = sink_ref[p * 8 + hh]
            m = jnp.maximum(jnp.max(s, axis=1, keepdims=True), sink)
            pe = jnp.exp(s - m)
            den = jnp.sum(pe, axis=1, keepdims=True) + jnp.exp(sink - m)
            o = _dot(_unfold_band(pe / den, upper), vsel[j])
            if half != j:
                o = pltpu.roll(o, HEAD_DIM, 1)
            acc = o if half == 0 else acc + o
            if half == 1:
                o_ref[:, qb * LANES:(qb + 1) * LANES] = acc

    q, kc, kp, vc, vp, tc, tp = _attn_specs(nb - 1)
    smem = pl.BlockSpec(memory_space=pltpu.SMEM)
    return pl.pallas_call(
        body, name="attn_fwd", grid=(4, nb),
        in_specs=[smem, q, kc, kp, vc, vp, tc, tp],
        out_specs=pl.BlockSpec((ATTN_BLOCK, 512), lambda p, i: (i, p)),
        out_shape=jax.ShapeDtypeStruct((t, ATTN_WIDTH), F32),
        compiler_params=_cp(("parallel", "arbitrary")))(sinks, proj, proj, proj, proj, proj, tables, tables)


def _attn_bwd(proj, sinks, tables, dout):
    t = proj.shape[0]
    nb = t // ATTN_BLOCK
    scale = HEAD_DIM ** -0.5

    def body(sink_ref, q_ref, kc_ref, kp_ref, vc_ref, vp_ref, tc_ref, tp_ref,
             do_ref, dq_ref, dk_ref, dv_ref, ds_ref, carry_k, carry_v):
        p = pl.program_id(0)
        i = pl.program_id(1)
        ptab = _split_tables(tp_ref[...])

        @pl.when(i == 0)
        def _():
            carry_k[...] = jnp.zeros_like(carry_k)
            carry_v[...] = jnp.zeros_like(carry_v)
            ds_ref[...] = jnp.zeros_like(ds_ref)

        @pl.when(i < nb)
        def _():
            cc, s1c, s2c = _split_tables(tc_ref[...])
            kband = jnp.concatenate([_rope(kp_ref[...], *ptab), _rope(kc_ref[...], cc, s1c, s2c)], axis=0)
            vband = jnp.concatenate([vp_ref[...], vc_ref[...]], axis=0)
            hm = _half_masks()
            kband16 = kband.astype(BF16)
            vband16 = vband.astype(BF16)
            upper, dropped = _band_masks(i, 4)
            dkb = jnp.zeros((2 * ATTN_BLOCK, LANES), F32)
            dvb = jnp.zeros((2 * ATTN_BLOCK, LANES), F32)
            row8 = _iota((8, LANES), 0)
            dsink = jnp.zeros((8, LANES), F32)
            qr = [_rope(q_ref[:, qb * LANES:(qb + 1) * LANES], cc, s1c, s2c) for qb in range(4)]
            dob = [do_ref[:, qb * LANES:(qb + 1) * LANES] for qb in range(4)]
            for j in range(2):
                qst = _stack_heads(qr, hm, j).astype(BF16)
                dost = _stack_heads(dob, hm, j).astype(BF16)
                s = jnp.where(dropped, NEG, _fold_band(_dot(qst, kband16, 'nt'), upper) * scale)
                sink = _sink_column(sink_ref, p * 8 + 4 * j)
                m = jnp.maximum(jnp.max(s, axis=1, keepdims=True), sink)
                pe = jnp.exp(s - m)
                psink = jnp.exp(sink - m)
                den = jnp.sum(pe, axis=1, keepdims=True) + psink
                pr = pe / den
                dvb = dvb + _dot(_unfold_band(pr, upper).T, dost)
                dp = _fold_band(_dot(dost, vband16, 'nt'), upper)
                delta = jnp.sum(pr * dp, axis=1, keepdims=True)
                dsc = _unfold_band(pr * (dp - delta) * scale, upper)
                dsk = psink / den * delta
                for r in range(4):
                    part = jnp.sum(dsk[r * ATTN_BLOCK:(r + 1) * ATTN_BLOCK])
                    dsink = dsink + jnp.where(row8 == 4 * j + r, -part, 0.0)
                for qb, dqb in _unstack_heads(_dot(dsc, kband * hm[j]), j):
                    dq_ref[:, qb * LANES:(qb + 1) * LANES] = _rope_t(dqb, cc, s1c, s2c).astype(BF16)
                dkb = dkb + _dot(dsc.T, qst)
            ds_ref[0] += dsink
            dk_ref[...] = _rope_t(carry_k[...] + dkb[:ATTN_BLOCK], *ptab).astype(BF16)
            dv_ref[...] = (carry_v[...] + dvb[:ATTN_BLOCK]).astype(BF16)
            carry_k[...] = dkb[ATTN_BLOCK:]
            carry_v[...] = dvb[ATTN_BLOCK:]

        @pl.when(i == nb)
        def _():
            dk_ref[...] = _rope_t(carry_k[...], *ptab).astype(BF16)
            dv_ref[...] = carry_v[...].astype(BF16)

    q, kc, kp, vc, vp, tc, tp = _attn_specs(nb - 1)
    smem = pl.BlockSpec(memory_space=pltpu.SMEM)
    qblk = pl.BlockSpec((ATTN_BLOCK, 512), lambda p, i: (jnp.minimum(i, nb - 1), p))
    kvout = pl.BlockSpec((ATTN_BLOCK, LANES), lambda p, i: (jnp.maximum(i - 1, 0), p))
    return pl.pallas_call(
        body, name="attn_bwd", grid=(4, nb + 1),
        in_specs=[smem, q, kc, kp, vc, vp, tc, tp, qblk],
        out_specs=[qblk, kvout, kvout, pl.BlockSpec((1, 8, LANES), lambda p, i: (p, 0, 0))],
        out_shape=[jax.ShapeDtypeStruct((t, ATTN_WIDTH), BF16), jax.ShapeDtypeStruct((t, KV_WIDTH), BF16),
                   jax.ShapeDtypeStruct((t, KV_WIDTH), BF16), jax.ShapeDtypeStruct((4, 8, LANES), F32)],
        scratch_shapes=[pltpu.VMEM((ATTN_BLOCK, LANES), F32), pltpu.VMEM((ATTN_BLOCK, LANES), F32)],
        compiler_params=_cp(("parallel", "arbitrary")))(sinks, proj, proj, proj, proj, proj, tables, tables, dout)


def _shift_rows(x, prev8, j):
    n, c = x.shape
    r = pltpu.roll(x.reshape(n // 8, 8, c), j, 1)
    before = pltpu.roll(prev8, j, 0)[None]
    if n > 8:
        before = jnp.concatenate([before, r[:-1]], axis=0)
    return jnp.where(_iota((1, 8, 1), 1) < j, before, r).reshape(n, c)


def _shift_rows_up(x, next8, j):
    n, c = x.shape
    r = pltpu.roll(x.reshape(n // 8, 8, c), 8 - j, 1)
    after = pltpu.roll(next8, 8 - j, 0)[None]
    if n > 8:
        after = jnp.concatenate([r[1:], after], axis=0)
    return jnp.where(_iota((1, 8, 1), 1) >= 8 - j, after, r).reshape(n, c)


def _conv_apply(x, prev8, w, b, taps):
    u = b + x * w[taps - 1:taps]
    for j in range(1, taps):
        u = u + _shift_rows(x, prev8, j) * w[taps - 1 - j:taps - j]
    return u


def _conv_grads(du, du_next8, x, w, taps):
    dx = du * w[taps - 1:taps]
    rowk = _iota((taps, 1), 0)
    dw = jnp.where(rowk == taps - 1, jnp.sum(du * x, axis=0, keepdims=True), 0.0)
    for j in range(1, taps):
        ahead = _shift_rows_up(du, du_next8, j)
        dx = dx + ahead * w[taps - 1 - j:taps - j]
        dw = dw + jnp.where(rowk == taps - 1 - j, jnp.sum(ahead * x, axis=0, keepdims=True), 0.0)
    return dx, dw, jnp.sum(du, axis=0, keepdims=True)


def _conv_specs(tb, tc, col0, t):
    c0 = col0 // tc
    cur = pl.BlockSpec((tb, tc), lambda j, i: (i, c0 + j))
    prev = pl.BlockSpec((8, tc), lambda j, i: (jnp.maximum(i * (tb // 8) - 1, 0), c0 + j))
    nxt = pl.BlockSpec((8, tc), lambda j, i: (jnp.minimum((i + 1) * (tb // 8), t // 8 - 1), c0 + j))
    return cur, prev, nxt


def _conv_silu_fwd(x, w, b, *, col0, width, name):
    t = x.shape[0]
    taps = w.shape[0]
    tb, tc = _rows(t, 512), _tile(width, 1024)
    assert col0 % tc == 0

    def body(x_ref, xp_ref, w_ref, b_ref, o_ref, u_ref):
        i = pl.program_id(1)
        prev8 = jnp.where(i > 0, xp_ref[...], 0.0)
        u = _conv_apply(x_ref[...], prev8, w_ref[...], b_ref[...], taps)
        u_ref[...] = u
        o_ref[...] = u * _sigmoid(u)

    cur, prev, _ = _conv_specs(tb, tc, col0, t)
    par = pl.BlockSpec((taps, tc), lambda j, i: (0, j))
    bias = pl.BlockSpec((1, tc), lambda j, i: (0, j))
    out = pl.BlockSpec((tb, tc), lambda j, i: (i, j))
    shp = jax.ShapeDtypeStruct((t, width), F32)
    return pl.pallas_call(
        body, name=name, grid=(width // tc, t // tb), in_specs=[cur, prev, par, bias], out_specs=[out, out],
        out_shape=[shp, shp], compiler_params=_cp(("parallel", "parallel")))(x, x, w, b)


def _dsilu(u):
    sg = _sigmoid(u)
    return sg * (1.0 + u * (1.0 - sg))


def _ssd_conv_bwd(x, w, dxs, dbm, dcm, *, col0, name):
    t = x.shape[0]
    taps = w.shape[0]
    tb, tc = _rows(t, 512), BC_WIDTH
    nrow, ncol = t // tb, CONV_CH // tc
    c0 = col0 // tc

    def body(x_ref, w_ref, xs_ref, xsn_ref, bm_ref, bmn_ref, cm_ref, cmn_ref, dx_ref, dw_ref, db_ref):
        i = pl.program_id(0)
        j = pl.program_id(1)

        def run(du_ref, dun_ref):
            next8 = jnp.where(i < nrow - 1, dun_ref[...], 0.0)
            dx, dwv, dbv = _conv_grads(du_ref[...], next8, x_ref[...], w_ref[...], taps)
            dx_ref[...] = dx.astype(BF16)

            @pl.when(i == 0)
            def _():
                dw_ref[j] = dwv
                db_ref[j] = dbv

            @pl.when(i > 0)
            def _():
                dw_ref[j] += dwv
                db_ref[j] += dbv

        pl.when(j < 2)(lambda: run(xs_ref, xsn_ref))
        pl.when(j == 2)(lambda: run(bm_ref, bmn_ref))
        pl.when(j == 3)(lambda: run(cm_ref, cmn_ref))

    def nxt_row(i):
        return jnp.minimum((i + 1) * (tb // 8), t // 8 - 1)

    xs_col = lambda j: jnp.minimum(j, SSD_INNER // tc - 1)
    in_specs = [pl.BlockSpec((tb, tc), lambda i, j: (i, c0 + j)), pl.BlockSpec((taps, tc), lambda i, j: (0, j)),
                pl.BlockSpec((tb, tc), lambda i, j: (i, xs_col(j))),
                pl.BlockSpec((8, tc), lambda i, j: (nxt_row(i), xs_col(j))),
                pl.BlockSpec((tb, tc), lambda i, j: (i, 0)), pl.BlockSpec((8, tc), lambda i, j: (nxt_row(i), 0)),
                pl.BlockSpec((tb, tc), lambda i, j: (i, 0)), pl.BlockSpec((8, tc), lambda i, j: (nxt_row(i), 0))]
    dx, dw, db = pl.pallas_call(
        body, name=name, grid=(nrow, ncol), in_specs=in_specs,
        out_specs=[pl.BlockSpec((tb, tc), lambda i, j: (i, j)),
                   pl.BlockSpec((ncol, taps, tc), lambda i, j: (0, 0, 0)),
                   pl.BlockSpec((ncol, 1, tc), lambda i, j: (0, 0, 0))],
        out_shape=[jax.ShapeDtypeStruct((t, CONV_CH), BF16), jax.ShapeDtypeStruct((ncol, taps, tc), F32),
                   jax.ShapeDtypeStruct((ncol, 1, tc), F32)],
        compiler_params=_cp(("arbitrary", "arbitrary")))(x, w, dxs, dxs, dbm, dbm, dcm, dcm)
    return dx, dw.transpose(1, 0, 2).reshape(taps, CONV_CH), db.transpose(1, 0, 2).reshape(1, CONV_CH)


def _ffn_specs(tb, tc, t):
    nc = D_FF // tc

    def cur(half):
        return pl.BlockSpec((tb, tc), lambda j, i: (i, half * nc + j))

    def prev(half):
        return pl.BlockSpec((8, tc), lambda j, i: (jnp.maximum(i * (tb // 8) - 1, 0), half * nc + j))

    def nxt(half):
        return pl.BlockSpec((8, tc), lambda j, i: (jnp.minimum((i + 1) * (tb // 8), t // 8 - 1), half * nc + j))

    def par(rows, half):
        return pl.BlockSpec((rows, tc), lambda j, i: (0, half * nc + j))

    return cur, prev, nxt, par


def _ffn_act_fwd(u0, w, b):
    t = u0.shape[0]
    tb, tc = _rows(t, 512), _tile(D_FF, 1408)
    cur, prev, _, par = _ffn_specs(tb, tc, t)

    def body(g_ref, gp_ref, v_ref, vp_ref, wg_ref, wv_ref, bg_ref, bv_ref, o_ref, u_ref):
        i = pl.program_id(1)
        ug = _conv_apply(g_ref[...], jnp.where(i > 0, gp_ref[...], 0.0), wg_ref[...], bg_ref[...], FFN_CONV)
        uv = _conv_apply(v_ref[...], jnp.where(i > 0, vp_ref[...], 0.0), wv_ref[...], bv_ref[...], FFN_CONV)
        o_ref[...] = (ug * _sigmoid(ug) * uv).astype(BF16)
        u_ref[0] = ug
        u_ref[1] = uv

    return pl.pallas_call(
        body, name="ffn_act_fwd", grid=(D_FF // tc, t // tb),
        in_specs=[cur(0), prev(0), cur(1), prev(1), par(FFN_CONV, 0), par(FFN_CONV, 1), par(1, 0), par(1, 1)],
        out_specs=[pl.BlockSpec((tb, tc), lambda j, i: (i, j)), pl.BlockSpec((2, tb, tc), lambda j, i: (0, i, j))],
        out_shape=[jax.ShapeDtypeStruct((t, D_FF), BF16), jax.ShapeDtypeStruct((2, t, D_FF), F32)],
        compiler_params=_cp(("parallel", "parallel")))(u0, u0, u0, u0, w, w, b, b)


def _ffn_act_bwd(u0, u, w, da):
    t = u0.shape[0]
    tb, tc = _rows(t, 256), _tile(D_FF, 1408)
    nrow = t // tb
    taps = FFN_CONV
    cur, _, _, par = _ffn_specs(tb, tc, t)

    def dact(ug, uv, dav):
        sg = _sigmoid(ug)
        return dav * uv * (sg * (1.0 + ug * (1.0 - sg))), dav * ug * sg

    def body(g_ref, v_ref, u_ref, un_ref, wg_ref, wv_ref, da_ref, dan_ref, dx_ref, dw_ref, db_ref):
        i = pl.program_id(1)
        dug, duv = dact(u_ref[0], u_ref[1], da_ref[...].astype(F32))
        dan = jnp.where(i < nrow - 1, dan_ref[...].astype(F32)[:8], 0.0)
        dugn, duvn = dact(un_ref[0], un_ref[1], dan)
        dxg, dwg, dbg = _conv_grads(dug, dugn, g_ref[...], wg_ref[...], taps)
        dxv, dwv, dbv = _conv_grads(duv, duvn, v_ref[...], wv_ref[...], taps)
        dx_ref[0] = dxg.astype(BF16)
        dx_ref[1] = dxv.astype(BF16)

        @pl.when(i == 0)
        def _():
            dw_ref[0] = dwg
            dw_ref[1] = dwv
            db_ref[0] = dbg
            db_ref[1] = dbv

        @pl.when(i > 0)
        def _():
            dw_ref[0] += dwg
            dw_ref[1] += dwv
            db_ref[0] += dbg
            db_ref[1] += dbv

    both = pl.BlockSpec((2, tb, tc), lambda j, i: (0, i, j))
    both_nxt = pl.BlockSpec((2, 8, tc), lambda j, i: (0, jnp.minimum((i + 1) * (tb // 8), t // 8 - 1), j))
    da_cur = pl.BlockSpec((tb, tc), lambda j, i: (i, j))
    da_nxt = pl.BlockSpec((16, tc), lambda j, i: (jnp.minimum((i + 1) * (tb // 16), t // 16 - 1), j))
    return pl.pallas_call(
        body, name="ffn_act_bwd", grid=(D_FF // tc, nrow),
        in_specs=[cur(0), cur(1), both, both_nxt, par(taps, 0), par(taps, 1), da_cur, da_nxt],
        out_specs=[both, pl.BlockSpec((2, taps, tc), lambda j, i: (0, 0, j)),
                   pl.BlockSpec((2, 1, tc), lambda j, i: (0, 0, j))],
        out_shape=[jax.ShapeDtypeStruct((2, t, D_FF), BF16), jax.ShapeDtypeStruct((2, taps, D_FF), F32),
                   jax.ShapeDtypeStruct((2, 1, D_FF), F32)],
        compiler_params=_cp(("parallel", "arbitrary")))(u0, u0, u, u, w, w, da, da)


def _head_masks():
    lane = _iota((1, 4 * SSD_HEAD_DIM), 1)
    return [((lane >= r * SSD_HEAD_DIM) & (lane < (r + 1) * SSD_HEAD_DIM)).astype(F32) for r in range(4)]


def _segsum(v):
    first = _iota((1, LANES), 1) < SSD_HEAD_DIM
    halves = []
    for k in range(2):
        vh = v[:, k * LANES:(k + 1) * LANES]
        both = jnp.sum(vh, axis=1, keepdims=True)
        one = jnp.sum(jnp.where(first, vh, 0.0), axis=1, keepdims=True)
        halves.append(jnp.where(first, one, both - one))
    return jnp.concatenate(halves, axis=1)


def _ssd_common(raw_e, prow, rawr4, bcol, acol):
    n = SSD_CHUNK
    dt_e = _softplus(raw_e + prow[0:1, :])
    a_e = -jnp.exp(prow[1:2, :])
    d_e = prow[2:3, :]
    tril = (_iota((n, n), 0) >= _iota((n, n), 1)).astype(F32)
    acs_e = _dot_exact(tril, dt_e * a_e)
    last_e = acs_e[n - 1:n, :]
    dtr4 = _softplus(rawr4 + bcol)
    triu = (_iota((n, n), 0) <= _iota((n, n), 1)).astype(F32)
    acs_r4 = _dot_exact(dtr4 * (-jnp.exp(acol)), triu)
    return dt_e, a_e, d_e, acs_e, last_e, acs_r4


def _decay_matrix(acs_e, acs_r4, r):
    n = SSD_CHUNK
    col = acs_e[:, r * SSD_HEAD_DIM:r * SSD_HEAD_DIM + 1]
    seg = col - acs_r4[r:r + 1, :]
    causal = _iota((n, n), 0) >= _iota((n, n), 1)
    return jnp.exp(jnp.where(causal, seg, NEG))


SSD_STEP_CHUNKS = 4
SSD_ROWS = SSD_STEP_CHUNKS * SSD_CHUNK


def _ssd_specs(t, rev):
    nb = t // SSD_ROWS
    xb, bb, cb = 0, SSD_INNER // SSD_STATE, (SSD_INNER + BC_WIDTH) // SSD_STATE

    def ch(c):
        return (nb - 1 - c) if rev else c

    x = pl.BlockSpec((SSD_ROWS, 256), lambda g, c: (ch(c), xb + g))
    bm = pl.BlockSpec((SSD_ROWS, SSD_STATE), lambda g, c: (ch(c), bb + g))
    cm = pl.BlockSpec((SSD_ROWS, SSD_STATE), lambda g, c: (ch(c), cb + g))
    dtc = pl.BlockSpec((1, SSD_ROWS, 256), lambda g, c: (g, ch(c), 0))
    dtr = pl.BlockSpec((1, 4, SSD_ROWS), lambda g, c: (g, 0, ch(c)))
    prow = pl.BlockSpec((1, 3, 256), lambda g, c: (g, 0, 0))
    pcol = pl.BlockSpec((1, 4, 1), lambda g, c: (g, 0, 0))
    st = pl.BlockSpec((1, SSD_STEP_CHUNKS, SSD_STATE, 256), lambda g, c: (g, ch(c), 0, 0))
    return x, bm, cm, dtc, dtr, prow, pcol, st, ch


def _ssd_params(dt_raw, dt_bias, a_log, ssd_d):
    t = dt_raw.shape[0]
    by_group = dt_raw.reshape(t, SSD_GROUPS, 4)
    dtc = jnp.repeat(by_group, SSD_HEAD_DIM, axis=2).transpose(1, 0, 2)
    dtr = by_group.transpose(1, 2, 0)
    prow = jnp.repeat(jnp.stack([dt_bias.reshape(SSD_GROUPS, 4), a_log.reshape(SSD_GROUPS, 4),
                                 ssd_d.reshape(SSD_GROUPS, 4)], axis=1), SSD_HEAD_DIM, axis=2)
    bcol = dt_bias.reshape(SSD_GROUPS, 4, 1)
    acol = a_log.reshape(SSD_GROUPS, 4, 1)
    return dtc, dtr, prow, bcol, acol


def _ssd_fwd(xbc, params):
    t = xbc.shape[0]
    nc = t // SSD_CHUNK
    dtc, dtr, prow, bcol, acol = params

    def body(x_ref, b_ref, c_ref, dtc_ref, dtr_ref, prow_ref, bcol_ref, acol_ref, y_ref, st_ref, s_scr):
        c = pl.program_id(1)

        @pl.when(c == 0)
        def _():
            s_scr[...] = jnp.zeros_like(s_scr)

        masks = _head_masks()
        s = s_scr[...]
        for k in range(SSD_STEP_CHUNKS):
            rows = slice(k * SSD_CHUNK, (k + 1) * SSD_CHUNK)
            dt_e, a_e, d_e, acs_e, last_e, acs_r4 = _ssd_common(
                dtc_ref[0, rows], prow_ref[0], dtr_ref[0][:, rows], bcol_ref[0], acol_ref[0])
            xv = x_ref[rows]
            bm, cm = b_ref[rows], c_ref[rows]
            st_ref[0, k] = s
            xdt = xv * dt_e
            cb = _dot(cm, bm, 'nt')
            y = _dot(cm, s) * jnp.exp(acs_e) + xv * d_e
            for r in range(4):
                mr = cb * _decay_matrix(acs_e, acs_r4, r)
                y = y + _dot(mr, xdt * masks[r])
            y_ref[rows] = y
            w = xdt * jnp.exp(last_e - acs_e)
            s = s * jnp.exp(last_e) + _dot(bm.T, w)
        s_scr[...] = s

    x, bm, cm, dtcs, dtrs, prs, pcs, st, _ = _ssd_specs(t, False)
    return pl.pallas_call(
        body, name="ssd_fwd", grid=(SSD_GROUPS, t // SSD_ROWS), in_specs=[x, bm, cm, dtcs, dtrs, prs, pcs, pcs],
        out_specs=[pl.BlockSpec((SSD_ROWS, 256), lambda g, c: (c, g)), st],
        out_shape=[jax.ShapeDtypeStruct((t, SSD_INNER), F32),
                   jax.ShapeDtypeStruct((SSD_GROUPS, nc, SSD_STATE, 256), F32)],
        scratch_shapes=[pltpu.VMEM((SSD_STATE, 256), F32)],
        compiler_params=_cp(("parallel", "arbitrary")))(xbc, xbc, xbc, dtc, dtr, prow, bcol, acol)


def _ssd_bwd(xbc, pre, params, states, dy):
    t = xbc.shape[0]
    nc = t // SSD_CHUNK
    n = SSD_CHUNK
    dtc, dtr, prow, bcol, acol = params

    def body(x_ref, b_ref, c_ref, ux_ref, ub_ref, uc_ref, dtc_ref, dtr_ref, prow_ref, bcol_ref, acol_ref, st_ref,
             dy_ref, dx_ref, db_ref, dc_ref, ddt_ref, dp_ref, ds_scr):
        c = pl.program_id(1)

        @pl.when(c == 0)
        def _():
            ds_scr[...] = jnp.zeros_like(ds_scr)
            dp_ref[...] = jnp.zeros_like(dp_ref)

        masks = _head_masks()
        ds = ds_scr[...]
        for k in reversed(range(SSD_STEP_CHUNKS)):
            rows = slice(k * SSD_CHUNK, (k + 1) * SSD_CHUNK)
            raw_e = dtc_ref[0, rows]
            prw = prow_ref[0]
            dt_e, a_e, d_e, acs_e, last_e, acs_r4 = _ssd_common(raw_e, prw, dtr_ref[0][:, rows], bcol_ref[0], acol_ref[0])
            xv = x_ref[rows]
            bm, cm = b_ref[rows], c_ref[rows]
            s = st_ref[0, k]
            dyv = dy_ref[rows]
            e_e = jnp.exp(acs_e)
            dec_e = jnp.exp(last_e - acs_e)
            cd_e = jnp.exp(last_e)
            xdt = xv * dt_e
            w = xdt * dec_e
            b16, c16, s16, ds16 = bm.astype(BF16), cm.astype(BF16), s.astype(BF16), ds.astype(BF16)
            cb = _dot(c16, b16, 'nt')
            yoff_raw = _dot(c16, s16)
            dye = dyv * e_e
            dye16 = dye.astype(BF16)
            dcm = _dot(dye16, s16, 'nt')
            ds_prev = ds * cd_e + _dot(cm.T, dye16)
            dacs_e = _segsum(dyv * yoff_raw) * e_e
            dw = _dot(b16, ds16)
            dbm = _dot(w, ds16, 'nt')
            tdec = _segsum(dw * xdt) * dec_e
            dacs_e = dacs_e - tdec
            dlast_e = jnp.sum(tdec, axis=0, keepdims=True)
            dxdt = dw * dec_e
            dlast_e = dlast_e + _segsum(jnp.sum(ds * s, axis=0, keepdims=True)) * cd_e
            dcb = jnp.zeros((n, n), F32)
            for r in range(4):
                lm = _decay_matrix(acs_e, acs_r4, r)
                mr = cb * lm
                dyr16 = (dyv * masks[r]).astype(BF16)
                dm = _dot(dyr16, xdt * masks[r], 'nt')
                dcb = dcb + dm * lm
                dseg = dm * mr
                dcol = jnp.sum(dseg, axis=1, keepdims=True) - jnp.sum(dseg.T, axis=1, keepdims=True)
                dacs_e = dacs_e + dcol * masks[r]
                dxdt = dxdt + _dot(mr.T, dyr16)
            dcm = dcm + _dot(dcb, b16)
            dbm = dbm + _dot(dcb.T, c16)
            dacs_e = dacs_e + jnp.where(_iota((n, 1), 0) == n - 1, dlast_e, 0.0)
            triu = (_iota((n, n), 0) <= _iota((n, n), 1)).astype(F32)
            ddta_e = _dot_exact(triu, dacs_e)
            ddt_e = ddta_e * a_e + _segsum(dxdt * xv)
            dx_ref[rows] = (dxdt * dt_e + dyv * d_e) * _dsilu(ux_ref[rows])
            db_ref[rows] = dbm * _dsilu(ub_ref[rows])
            dc_ref[rows] = dcm * _dsilu(uc_ref[rows])
            draw_e = ddt_e * _sigmoid(raw_e + prw[0:1, :])
            draw_t = draw_e.T
            ddt_ref[0, :, rows] = jnp.concatenate([draw_t[r * SSD_HEAD_DIM:r * SSD_HEAD_DIM + 1] for r in range(4)], axis=0)
            dbias = jnp.sum(draw_e, axis=0, keepdims=True)
            dalog = jnp.sum(ddta_e * dt_e, axis=0, keepdims=True) * a_e
            dd = _segsum(jnp.sum(dyv * xv, axis=0, keepdims=True))
            row3 = _iota((3, 1), 0)
            dp_ref[0] += (jnp.where(row3 == 0, dbias, 0.0) + jnp.where(row3 == 1, dalog, 0.0)
                          + jnp.where(row3 == 2, dd, 0.0))
            ds = ds_prev
        ds_scr[...] = ds


    x, bm, cm, dtcs, dtrs, prs, pcs, st, ch = _ssd_specs(t, True)
    yblk = pl.BlockSpec((SSD_ROWS, 256), lambda g, c: (ch(c), g))
    nblk = pl.BlockSpec((SSD_ROWS, SSD_STATE), lambda g, c: (ch(c), g))
    return pl.pallas_call(
        body, name="ssd_bwd", grid=(SSD_GROUPS, t // SSD_ROWS),
        in_specs=[x, bm, cm, x, bm, cm, dtcs, dtrs, prs, pcs, pcs, st, yblk],
        out_specs=[yblk, nblk, nblk, dtrs, prs],
        out_shape=[jax.ShapeDtypeStruct((t, SSD_INNER), F32), jax.ShapeDtypeStruct((t, BC_WIDTH), F32),
                   jax.ShapeDtypeStruct((t, BC_WIDTH), F32), jax.ShapeDtypeStruct((SSD_GROUPS, 4, t), F32),
                   jax.ShapeDtypeStruct((SSD_GROUPS, 3, 256), F32)],
        scratch_shapes=[pltpu.VMEM((SSD_STATE, 256), F32)],
        compiler_params=_cp(("parallel", "arbitrary")))(xbc, xbc, xbc, pre, pre, pre, dtc, dtr, prow, bcol, acol,
                                                         states, dy)


GROUP_W = SSD_INNER // SSD_GROUPS


def _mix_specs(tb):
    row = pl.BlockSpec((tb, 2048), lambda i: (i, 0))
    zlo = pl.BlockSpec((tb, 1024), lambda i: (i, O_Z // 1024))
    zhi = pl.BlockSpec((tb, 1024), lambda i: (i, O_Z // 1024 + 1))
    vec = pl.BlockSpec((1, 2048), lambda i: (0, 0))
    return row, zlo, zhi, vec


def _mix_fwd(attn, y, proj, g_attn, g_ssd):
    t = attn.shape[0]
    tb = _rows(t, 256)

    def body(a_ref, y_ref, zlo_ref, zhi_ref, ga_ref, gs_ref, o_ref):
        av = a_ref[...]
        r = lax.rsqrt(jnp.mean(av * av, axis=-1, keepdims=True) + EPS)
        o_ref[:, :ATTN_WIDTH] = (av * r * ga_ref[...]).astype(BF16)
        for g in range(SSD_GROUPS):
            lo, hi = g * GROUP_W, (g + 1) * GROUP_W
            zref = zlo_ref if g < 4 else zhi_ref
            z = zref[:, lo % 1024:lo % 1024 + GROUP_W]
            yg = y_ref[:, lo:hi] * (z * _sigmoid(z))
            rg = lax.rsqrt(jnp.mean(yg * yg, axis=-1, keepdims=True) + EPS)
            o_ref[:, ATTN_WIDTH + lo:ATTN_WIDTH + hi] = (yg * rg * gs_ref[:, lo:hi]).astype(BF16)

    row, zlo, zhi, vec = _mix_specs(tb)
    return pl.pallas_call(
        body, name="mix_fwd", grid=(t // tb,), in_specs=[row, row, zlo, zhi, vec, vec],
        out_specs=pl.BlockSpec((tb, 4096), lambda i: (i, 0)), out_shape=jax.ShapeDtypeStruct((t, 4096), BF16),
        compiler_params=_cp(("parallel",)))(attn, y, proj, proj, g_attn, g_ssd)


def _mix_bwd(dmix, attn, y, proj, g_attn, g_ssd):
    t = attn.shape[0]
    tb = _rows(t, 256)

    def body(dm_ref, a_ref, y_ref, zlo_ref, zhi_ref, ga_ref, gs_ref, da_ref, dy_ref, dz_ref, dga_ref, dgs_ref):
        i = pl.program_id(0)
        av = a_ref[...]
        dn = dm_ref[:, :ATTN_WIDTH].astype(F32)
        r = lax.rsqrt(jnp.mean(av * av, axis=-1, keepdims=True) + EPS)
        u = dn * ga_ref[...]
        da_ref[...] = r * u - av * (r * r * r * jnp.mean(u * av, axis=-1, keepdims=True))
        dga = jnp.sum(dn * av * r, axis=0, keepdims=True)

        @pl.when(i == 0)
        def _():
            dga_ref[...] = dga

        @pl.when(i > 0)
        def _():
            dga_ref[...] += dga

        for g in range(SSD_GROUPS):
            lo, hi = g * GROUP_W, (g + 1) * GROUP_W
            zref = zlo_ref if g < 4 else zhi_ref
            z = zref[:, lo % 1024:lo % 1024 + GROUP_W]
            yv = y_ref[:, lo:hi]
            sg = _sigmoid(z)
            sz = z * sg
            yg = yv * sz
            rg = lax.rsqrt(jnp.mean(yg * yg, axis=-1, keepdims=True) + EPS)
            do = dm_ref[:, ATTN_WIDTH + lo:ATTN_WIDTH + hi].astype(F32)
            ug = do * gs_ref[:, lo:hi]
            dyg = rg * ug - yg * (rg * rg * rg * jnp.mean(ug * yg, axis=-1, keepdims=True))
            dy_ref[:, lo:hi] = dyg * sz
            dz_ref[:, lo:hi] = (dyg * yv * (sg * (1.0 + z * (1.0 - sg)))).astype(BF16)
            dgs = jnp.sum(do * yg * rg, axis=0, keepdims=True)

            @pl.when(i == 0)
            def _():
                dgs_ref[:, lo:hi] = dgs

            @pl.when(i > 0)
            def _():
                dgs_ref[:, lo:hi] += dgs

    row, zlo, zhi, vec = _mix_specs(tb)
    return pl.pallas_call(
        body, name="mix_bwd", grid=(t // tb,),
        in_specs=[pl.BlockSpec((tb, 4096), lambda i: (i, 0)), row, row, zlo, zhi, vec, vec],
        out_specs=[row, row, row, vec, vec],
        out_shape=[jax.ShapeDtypeStruct((t, 2048), F32), jax.ShapeDtypeStruct((t, 2048), F32),
                   jax.ShapeDtypeStruct((t, 2048), BF16), jax.ShapeDtypeStruct((1, 2048), F32),
                   jax.ShapeDtypeStruct((1, 2048), F32)],
        compiler_params=_cp(("arbitrary",)))(dmix, attn, y, proj, proj, g_attn, g_ssd)


def _adamw(w, g, m, v, name):
    r, c = w.shape
    tb = _rows(r, 256)
    c1 = 1.0 - ADAM_B1 ** ADAM_STEP
    c2 = 1.0 - ADAM_B2 ** ADAM_STEP

    def body(w_ref, g_ref, m_ref, v_ref, d_ref, m2_ref, v2_ref):
        gv = g_ref[...]
        m2 = ADAM_B1 * m_ref[...] + (1.0 - ADAM_B1) * gv
        v2 = ADAM_B2 * v_ref[...] + (1.0 - ADAM_B2) * (gv * gv)
        d_ref[...] = -ADAM_LR * ((m2 / c1) / (jnp.sqrt(v2 / c2) + ADAM_EPS) + ADAM_WD * w_ref[...])
        m2_ref[...] = m2
        v2_ref[...] = v2

    blk = pl.BlockSpec((tb, c), lambda i: (i, 0))
    shp = jax.ShapeDtypeStruct((r, c), F32)
    return pl.pallas_call(body, name=name, grid=(r // tb,), in_specs=[blk] * 4, out_specs=[blk] * 3,
                          out_shape=[shp] * 3, compiler_params=_cp(("parallel",)))(w, g, m, v)


def _adamw_halves(w, mine, theirs, m, v, pos, name, cols=False):
    r, c = w.shape
    h = r if cols else r // 2
    tb = _rows(h, 128)
    nh = h // tb
    c1 = 1.0 - ADAM_B1 ** ADAM_STEP
    c2 = 1.0 - ADAM_B2 ** ADAM_STEP

    def body(pos_ref, w_ref, a_ref, b_ref, m_ref, v_ref, g_ref, d_ref, m2_ref, v2_ref):
        which = pl.program_id(1) if cols else pl.program_id(0) // nh
        gv = jnp.where(which == pos_ref[0], a_ref[...], b_ref[...])
        m2 = ADAM_B1 * m_ref[...] + (1.0 - ADAM_B1) * gv
        v2 = ADAM_B2 * v_ref[...] + (1.0 - ADAM_B2) * (gv * gv)
        g_ref[...] = gv
        d_ref[...] = -ADAM_LR * ((m2 / c1) / (jnp.sqrt(v2 / c2) + ADAM_EPS) + ADAM_WD * w_ref[...])
        m2_ref[...] = m2
        v2_ref[...] = v2

    if cols:
        full = pl.BlockSpec((tb, c // 2), lambda i, j, pref: (i, j))
        mine_spec = theirs_spec = pl.BlockSpec((tb, c // 2), lambda i, j, pref: (i, 0))
        grid = (nh, 2)
    else:
        full = pl.BlockSpec((tb, c), lambda i, pref: (i, 0))
        mine_spec = pl.BlockSpec((tb, c), lambda i, pref: (jnp.where(i // nh == pref[0], i % nh,
                                                                     jnp.where(pref[0] == 0, nh - 1, 0)), 0))
        theirs_spec = pl.BlockSpec((tb, c), lambda i, pref: (jnp.where(i // nh != pref[0], i % nh,
                                                                       jnp.where(pref[0] == 0, 0, nh - 1)), 0))
        grid = (r // tb,)
    shp = jax.ShapeDtypeStruct((r, c), F32)
    grid_spec = pltpu.PrefetchScalarGridSpec(num_scalar_prefetch=1, grid=grid,
                                             in_specs=[full, mine_spec, theirs_spec, full, full],
                                             out_specs=[full] * 4)
    return pl.pallas_call(body, name=name, grid_spec=grid_spec, out_shape=[shp] * 4,
                          compiler_params=_cp(("parallel",) * len(grid)))(pos, w, mine, theirs, m, v)


def _sum_own_half(g4, recv, pos, name, cols=False):
    _, r, c = g4.shape
    h, c = (r, c // 2) if cols else (r // 2, c)
    tb = _rows(h, 128)
    nh = h // tb

    def slot(j, pref):
        return (pref[1] + 1 + j) % N_CHIPS

    if cols:
        own = lambda j, i, pref: (slot(j, pref), i, pref[0])
    else:
        own = lambda j, i, pref: (slot(j, pref), pref[0] * nh + i, 0)
    same = lambda j, i, pref: (slot(j, pref), i, 0)

    def body(pos_ref, a_ref, b_ref, o_ref):
        o_ref[...] = (a_ref[...] + b_ref[...]).astype(BF16)

    grid_spec = pltpu.PrefetchScalarGridSpec(
        num_scalar_prefetch=1, grid=(N_CHIPS - 1, nh),
        in_specs=[pl.BlockSpec((1, tb, c), own), pl.BlockSpec((1, tb, c), same)],
        out_specs=pl.BlockSpec((1, tb, c), same))
    return pl.pallas_call(body, name=name, grid_spec=grid_spec,
                          out_shape=jax.ShapeDtypeStruct((N_CHIPS, h, c), BF16),
                          compiler_params=_cp(("parallel", "parallel")))(pos, g4, recv)


def _sum_chips(g4, recv, parts, pos, name, cols=False):
    _, r, c = g4.shape
    h, c = (r, c // 2) if cols else (r // 2, c)
    tb = _rows(h, 128)
    nh = h // tb
    own = (lambda i, pref: (pref[1], i, pref[0])) if cols else (lambda i, pref: (pref[1], pref[0] * nh + i, 0))

    def body(pos_ref, a_ref, b_ref, p_ref, o_ref):
        own = a_ref[0] + b_ref[0]
        o_ref[...] = ((own + p_ref[0].astype(F32)) + p_ref[1].astype(F32)) + p_ref[2].astype(F32)

    grid_spec = pltpu.PrefetchScalarGridSpec(
        num_scalar_prefetch=1, grid=(nh,),
        in_specs=[pl.BlockSpec((1, tb, c), own),
                  pl.BlockSpec((1, tb, c), lambda i, pref: (pref[1], i, 0)),
                  pl.BlockSpec((3, tb, c), lambda i, pref: (0, i, 0))],
        out_specs=pl.BlockSpec((tb, c), lambda i, pref: (i, 0)))
    return pl.pallas_call(body, name=name, grid_spec=grid_spec, out_shape=jax.ShapeDtypeStruct((h, c), F32),
                          compiler_params=_cp(("parallel",)))(pos, g4, recv, parts)


def _me():
    return lax.axis_index("x"), lax.axis_index("y"), lax.axis_index("c")


def _flip(v, bit):
    return (1 - v) if bit else v


CHIP_FLIPS = [(1, 0), (0, 1), (1, 1)]


def _forward_halves(gathered):
    def body(g_ref, o_ref, token, send_sems, recv_sems):
        x, y, c = _me()
        h = g_ref.shape[2] // 2
        cps = []
        for k, (fx, fy) in enumerate(CHIP_FLIPS):
            peer_chip = 2 * _flip(x, fx) + _flip(y, fy)
            mine = o_ref.at[peer_chip, :, pl.ds(c * h, h)]
            cp = pltpu.make_async_remote_copy(src_ref=mine, dst_ref=mine, send_sem=send_sems.at[k],
                                              recv_sem=recv_sems.at[k], device_id=(x, y, 1 - c), device_id_type=MESH)
            cp.start()
            cps.append(cp)
        for k, (fx, fy) in enumerate(CHIP_FLIPS):
            peer_chip = 2 * _flip(x, fx) + _flip(y, fy)
            theirs = o_ref.at[peer_chip, :, pl.ds((1 - c) * h, h)]
            pltpu.make_async_remote_copy(src_ref=theirs, dst_ref=theirs, send_sem=send_sems.at[k],
                                         recv_sem=recv_sems.at[k], device_id=(x, y, 1 - c),
                                         device_id_type=MESH).wait_recv()
        for cp in cps:
            cp.wait_send()
        token[...] = jnp.zeros_like(token)

    return pl.pallas_call(
        body, name="gather_forward_w_in", in_specs=[HBM_SPEC],
        out_specs=[HBM_SPEC, pl.BlockSpec(memory_space=pltpu.VMEM)],
        out_shape=[jax.ShapeDtypeStruct(gathered.shape, gathered.dtype), TOKEN],
        scratch_shapes=[pltpu.SemaphoreType.DMA((3,)), pltpu.SemaphoreType.DMA((3,))],
        input_output_aliases={0: 0},
        compiler_params=pltpu.CompilerParams(has_side_effects=True))(gathered)


def _share_halves(ghs, name):
    n = len(ghs)

    def body(*refs):
        ins, outs = refs[:n], refs[n:2 * n]
        send_sems, recv_sems = refs[2 * n:]
        x, y, c = _me()
        cps = []
        for t in range(n):
            cp = pltpu.make_async_remote_copy(
                src_ref=ins[t], dst_ref=outs[t], send_sem=send_sems.at[t], recv_sem=recv_sems.at[t],
                device_id=(x, y, 1 - c), device_id_type=MESH)
            cp.start()
            cps.append(cp)
        for cp in cps:
            cp.wait()

    return pl.pallas_call(
        body, name=name, in_specs=[HBM_SPEC] * n, out_specs=[HBM_SPEC] * n,
        out_shape=[jax.ShapeDtypeStruct(g.shape, g.dtype) for g in ghs],
        scratch_shapes=[pltpu.SemaphoreType.DMA((n,)), pltpu.SemaphoreType.DMA((n,))],
        compiler_params=pltpu.CompilerParams(has_side_effects=True))(*ghs)


SEM_SPEC = pl.BlockSpec(memory_space=pltpu.SEMAPHORE)
ANY_SPEC = pl.BlockSpec(memory_space=pl.ANY)
DATAFLOW = pltpu.SideEffectType.DATAFLOW_SIDE_EFFECTING


def _in_hbm(a):
    return pltpu.with_memory_space_constraint(a, pltpu.HBM)


def _push_start(srcs, land_shapes, route, peers, name):
    n, npeer = len(srcs), len(peers)
    lands = [lax.empty(shp, s.dtype) for shp, s in zip(land_shapes, srcs)]

    def body(*refs):
        ins, lnd = refs[:n], refs[n:2 * n]
        send_sems, recv_sems = refs[2 * n], refs[2 * n + 1]
        token = refs[-1]
        x, y, c = _me()
        for t in range(n):
            for k, (fx, fy, fc) in enumerate(peers):
                src, dst = route(ins[t], lnd[t], k, x, y, c)
                pltpu.make_async_remote_copy(
                    src_ref=src, dst_ref=dst, send_sem=send_sems.at[npeer * t + k],
                    recv_sem=recv_sems.at[npeer * t + k],
                    device_id=(_flip(x, fx), _flip(y, fy), _flip(c, fc)), device_id_type=MESH).start()
        token[...] = jnp.zeros_like(token)

    bufs = [_in_hbm(a) for a in list(srcs) + lands]
    outs = pl.pallas_call(
        body, name=name,
        out_shape=(pltpu.SemaphoreType.DMA((npeer * n,)), pltpu.SemaphoreType.DMA((npeer * n,)),
                   *[pltpu.HBM(b.shape, b.dtype) for b in bufs], TOKEN),
        in_specs=[HBM_SPEC] * (2 * n),
        out_specs=(SEM_SPEC, SEM_SPEC, *[HBM_SPEC] * (2 * n), pl.BlockSpec(memory_space=pltpu.VMEM)),
        input_output_aliases={i: 2 + i for i in range(2 * n)},
        compiler_params=pltpu.CompilerParams(has_side_effects=DATAFLOW))(*bufs)
    return outs[0], outs[1], list(outs[2:2 + n]), list(outs[2 + n:2 + 2 * n]), outs[-1]


def _push_wait(send_sems, recv_sems, srcs, lands, after, route, peers, name):
    n, npeer = len(srcs), len(peers)

    def body(*refs):
        ins, lnd = refs[:n], refs[n:2 * n]
        ssem, rsem = refs[2 * n], refs[2 * n + 1]
        x, y, c = _me()
        for t in range(n):
            for k, (fx, fy, fc) in enumerate(peers):
                src, dst = route(ins[t], lnd[t], k, x, y, c)
                cp = pltpu.make_async_remote_copy(
                    src_ref=src, dst_ref=dst, send_sem=ssem.at[npeer * t + k], recv_sem=rsem.at[npeer * t + k],
                    device_id=(_flip(x, fx), _flip(y, fy), _flip(c, fc)), device_id_type=MESH)
                cp.wait_send()
                cp.wait_recv()

    bufs = list(srcs) + list(lands)
    outs = pl.pallas_call(
        body, name=name, out_shape=tuple(pltpu.HBM(b.shape, b.dtype) for b in bufs),
        in_specs=[HBM_SPEC] * (2 * n) + [SEM_SPEC, SEM_SPEC, ANY_SPEC], out_specs=tuple([HBM_SPEC] * (2 * n)),
        input_output_aliases={i: i for i in range(2 * n)},
        compiler_params=pltpu.CompilerParams(has_side_effects=DATAFLOW))(*bufs, send_sems, recv_sems, after)
    return list(outs[:n]), list(outs[n:])


OTHER_CHIPS = [(fx, fy, 0) for fx, fy in CHIP_FLIPS]
SIBLING = [(0, 0, 1)]


def _route_gather(src, land, k, x, y, c):
    return src, land.at[2 * x + y]


def _route_gather_half(src, land, k, x, y, c):
    h = src.shape[1] // 2
    return src.at[:, pl.ds(c * h, h)], land.at[2 * x + y, :, pl.ds(c * h, h)]


def _route_gather_half_wait(src, land, k, x, y, c):
    fx, fy = CHIP_FLIPS[k]
    h = src.shape[1] // 2
    return src.at[:, pl.ds(c * h, h)], land.at[2 * _flip(x, fx) + _flip(y, fy), :, pl.ds(c * h, h)]


def _route_gather_wait(src, land, k, x, y, c):
    fx, fy = CHIP_FLIPS[k]
    return src, land.at[2 * _flip(x, fx) + _flip(y, fy)]


def _route_scatter(src, land, k, x, y, c):
    fx, fy = CHIP_FLIPS[k]
    return src.at[2 * _flip(x, fx) + _flip(y, fy)], land.at[k]


def _route_exchange(src, land, k, x, y, c):
    h = land.shape[1]
    return src.at[:, pl.ds((1 - c) * h, h)], land


def _route_exchange_cols(src, land, k, x, y, c):
    h = land.shape[2]
    return src.at[:, :, pl.ds((1 - c) * h, h)], land


def _allreduce_small(v):
    r = v.shape[0]

    def body(v_ref, o_ref, token, buf, send_sems, recv_sems):
        x, y, c = _me()
        me = 4 * x + 2 * y + c
        buf[0] = v_ref[...]
        cps = []
        for k in range(1, 8):
            kx, ky, kc = (k >> 2) & 1, (k >> 1) & 1, k & 1
            cp = pltpu.make_async_remote_copy(
                src_ref=v_ref, dst_ref=buf.at[k], send_sem=send_sems.at[k - 1], recv_sem=recv_sems.at[k - 1],
                device_id=(_flip(x, kx), _flip(y, ky), _flip(c, kc)), device_id_type=MESH)
            cp.start()
            cps.append(cp)
        for cp in cps:
            cp.wait()
        acc = buf[me]
        for d in range(1, 8):
            acc = acc + buf[jnp.bitwise_xor(me, d)]
        o_ref[...] = acc
        token[...] = jnp.zeros_like(token)

    vm = pl.BlockSpec(memory_space=pltpu.VMEM)
    return pl.pallas_call(
        body, name="allreduce_small", in_specs=[vm], out_specs=[vm, vm],
        out_shape=[jax.ShapeDtypeStruct(v.shape, F32), TOKEN],
        scratch_shapes=[pltpu.VMEM((8, r, LANES), F32), pltpu.SemaphoreType.DMA((7,)),
                        pltpu.SemaphoreType.DMA((7,))],
        compiler_params=pltpu.CompilerParams(has_side_effects=True, vmem_limit_bytes=VMEM_LIMIT))(v)


def _grad_exchange_start(g4, tag, cols=False):
    land = (N_CHIPS, g4.shape[1], g4.shape[2] // 2) if cols else (N_CHIPS, g4.shape[1] // 2, g4.shape[2])
    route = _route_exchange_cols if cols else _route_exchange
    send_sems, recv_sems, srcs, lands, token = _push_start(
        [g4], [land], route, SIBLING, name="grad_exchange_start_" + tag)
    return (send_sems, recv_sems, srcs, lands, tag, cols), token


def _grad_scatter_start(state, pos, after):
    send_sems, recv_sems, srcs, lands, tag, cols = state
    route = _route_exchange_cols if cols else _route_exchange
    (g4,), (recv,) = _push_wait(send_sems, recv_sems, srcs, lands, after, route, SIBLING,
                                name="grad_exchange_wait_" + tag)
    return _grad_pair_scatter(g4, recv, pos, tag, cols)


def _grad_pair_scatter(g4, recv, pos, tag, cols=False):
    p16 = _sum_own_half(g4, recv, pos, name="grad_sum_pair_" + tag, cols=cols)
    send_sems, recv_sems, srcs, lands, token = _push_start(
        [p16], [(3,) + p16.shape[1:]], _route_scatter, OTHER_CHIPS, name="grad_scatter_start_" + tag)
    return (g4, recv, send_sems, recv_sems, srcs, lands, tag, cols), token


def _grad_reduce_finish(state, pos, after):
    g4, recv, send_sems, recv_sems, srcs, lands, tag, cols = state
    parts = _push_wait(send_sems, recv_sems, srcs, lands, after, _route_scatter, OTHER_CHIPS,
                       name="grad_scatter_wait_" + tag)[1][0]
    mine = _sum_chips(g4, recv, parts, pos, name="grad_sum_chips_" + tag, cols=cols)
    return mine, _share_halves([mine], name="grad_share_halves_" + tag)[0]


def _local_step(x, tgt, p, hooks):
    t = x.shape[0]
    tables = _rope_tables(t)
    sinks = p['sinks'].reshape(N_Q_HEADS)

    def told(name, value):
        return tuple(hooks.grad_ready(name, value))

    xn = _rmsnorm_fwd(x, p['norm_mix'], "norm_mix_fwd", deps=hooks.first_deps)
    w_in_t, w_in_dt, in_deps = hooks.weight_in(xn)
    proj = _matmul(xn, w_in_t, mode='nt', name="in_proj", n_limit=MAIN_WIDTH, deps=in_deps)
    dt_raw = _matmul(xn, w_in_dt, mode='nt', name="in_proj_dt")[:, :SSD_HEADS]
    attn = _attn_fwd(proj, sinks, tables)
    conv_b = p['ssd_conv_b']
    xbc, xbc_pre = _conv_silu_fwd(proj, p['ssd_conv_w'], conv_b, col0=O_XBC, width=CONV_CH, name="ssd_conv_fwd")
    sp = _ssd_params(dt_raw, p['dt_bias'].reshape(-1), p['a_log'].reshape(-1), p['ssd_d'].reshape(-1))
    y, states = _ssd_fwd(xbc, sp)
    mix = _mix_fwd(attn, y, proj, p['attn_out_norm'], p['ssd_norm'])
    w_out = hooks.weight('w_out', mix)
    h1 = _matmul(mix, w_out, mode='nn', name="out_proj", add=x)
    hn = _rmsnorm_fwd(h1, p['norm_ffn'], "norm_ffn_fwd")
    w_up = hooks.weight('w_up', hn)
    u0 = _matmul(hn, w_up, mode='nn', name="ffn_up", b_owner=True, tn=1408)
    a, u = _ffn_act_fwd(u0, p['ffn_conv_w'], p['ffn_conv_b'])
    w_down = hooks.weight('w_down', a)
    h2 = _matmul(a, w_down, mode='nn', name="ffn_down", add=h1, tk=2816)
    loss, dh2, dh2_16, g_norm_final = _final_loss(h2, p['norm_final'].reshape(1, D_MODEL), tgt)

    g = {}
    da = _matmul(dh2_16, w_down, mode='nt', name="ffn_down_dx", out_dtype=BF16, tn=1408)
    g['w_down'] = _matmul(a, dh2_16, mode='tn', name="ffn_down_dw", tm=1408)
    dep = told('w_down', g['w_down'])
    du0, dcw, dcb = _ffn_act_bwd(u0, u, p['ffn_conv_w'], da)
    g['ffn_conv_w'] = dcw.transpose(1, 0, 2).reshape(FFN_CONV, 2 * D_FF)
    g['ffn_conv_b'] = dcb.transpose(1, 0, 2).reshape(1, 2 * D_FF)
    g['w_up'] = _matmul(hn, du0, mode='tn', name="ffn_up_dw", deps=dep, b_halves=True, owner_major=True,
                        tn=1408)
    dep = told('w_up', g['w_up'])
    dhn = _matmul(du0, w_up, mode='nt', name="ffn_up_dx", out_dtype=BF16, deps=dep, a_halves=True,
                  b_owner=True, tk=2816)
    dh1, dh1_16, g['norm_ffn'] = _rmsnorm_bwd(h1, p['norm_ffn'], dhn, dh2, "norm_ffn_bwd")

    g['w_out'] = _matmul(mix, dh1_16, mode='tn', name="out_proj_dw")
    dep = told('w_out', g['w_out'])
    dmix = _matmul(dh1_16, w_out, mode='nt', name="out_proj_dx", out_dtype=BF16, deps=dep)
    dattn, dy, dz, g['attn_out_norm'], g['ssd_norm'] = _mix_bwd(dmix, attn, y, proj, p['attn_out_norm'],
                                                                p['ssd_norm'])
    dq, dk, dv, dsink = _attn_bwd(proj, sinks, tables, dattn)
    g['sinks'] = dsink[:, :, 0].reshape(1, N_Q_HEADS)
    dxs, dbm, dcm, ddt8, dpar = _ssd_bwd(xbc, xbc_pre, sp, states, dy)
    dpar = dpar[:, :, ::SSD_HEAD_DIM]
    g['dt_bias'] = dpar[:, 0, :].reshape(1, SSD_HEADS)
    g['a_log'] = dpar[:, 1, :].reshape(1, SSD_HEADS)
    g['ssd_d'] = dpar[:, 2, :].reshape(1, SSD_HEADS)
    dxbc, g['ssd_conv_w'], g['ssd_conv_b'] = _ssd_conv_bwd(proj, p['ssd_conv_w'], dxs, dbm, dcm, col0=O_XBC,
                                                           name="ssd_conv_bwd")
    dproj = jnp.concatenate([dq, dk, dv, dz, dxbc], axis=1)
    ddt = ddt8.transpose(2, 0, 1).reshape(t, SSD_HEADS)
    ddt_pad = jnp.pad(ddt, ((0, 0), (0, LANES - SSD_HEADS))).astype(BF16)
    g['w_in'] = (_matmul(dproj, xn, mode='tn', name="in_proj_dw", m_rows=IN_PROJ_WIDTH),
                 _matmul(ddt_pad, xn, mode='tn', name="in_proj_dt_dw"))
    dep = told('w_in', g['w_in'])
    dxn_dt = _matmul(ddt_pad, w_in_dt, mode='nn', name="in_proj_dt_dx", deps=dep)
    dxn = _matmul(dproj, w_in_t, mode='nn', name="in_proj_dx", out_dtype=BF16, add=dxn_dt, k_limit=MAIN_WIDTH,
                  tk=2304)
    dx, _, g['norm_mix'] = _rmsnorm_bwd(x, p['norm_mix'], dxn, dh1, "norm_mix_bwd")
    g['norm_final'] = g_norm_final
    return loss, dx, g


def _pack(arrs):
    flat = jnp.concatenate([a.reshape(-1) for a in arrs])
    n = flat.shape[0]
    rows = -(-n // LANES)
    rows = -(-rows // 8) * 8
    return jnp.pad(flat, (0, rows * LANES - n)).reshape(rows, LANES)


def _unpack(packed, shapes):
    flat = packed.reshape(-1)
    out, off = [], 0
    for s in shapes:
        n = 1
        for d in s:
            n *= d
        out.append(flat[off:off + n].reshape(s))
        off += n
    return out


class _StepHooks:
    def __init__(self, first_deps, weight_in, weight, grad_ready):
        self.first_deps = first_deps
        self.weight_in = weight_in
        self.weight = weight
        self.grad_ready = grad_ready


def kernel(x, norm_mix, w_in, sinks, attn_out_norm, ssd_conv_w, ssd_conv_b, dt_bias, a_log, ssd_d, ssd_norm, w_out, norm_ffn, w_up, ffn_conv_w, ffn_conv_b, w_down, norm_final, loss_target, m_norm_mix, m_w_in, m_sinks, m_attn_out_norm, m_ssd_conv_w, m_ssd_conv_b, m_dt_bias, m_a_log, m_ssd_d, m_ssd_norm, m_w_out, m_norm_ffn, m_w_up, m_ffn_conv_w, m_ffn_conv_b, m_w_down, m_norm_final, v_norm_mix, v_w_in, v_sinks, v_attn_out_norm, v_ssd_conv_w, v_ssd_conv_b, v_dt_bias, v_a_log, v_ssd_d, v_ssd_norm, v_w_out, v_norm_ffn, v_w_up, v_ffn_conv_w, v_ffn_conv_b, v_w_down, v_norm_final):
    args = dict(locals())
    w = {n: args[n] for n in WEIGHTS}
    m = {n: args['m_' + n] for n in WEIGHTS}
    v = {n: args['v_' + n] for n in WEIGHTS}
    xi, yi, ci = _me()
    chip = 2 * xi + yi
    pos = jnp.stack([ci, chip]).astype(jnp.int32)

    def place(shard, full_cols):
        z = jnp.zeros((shard.shape[0], full_cols), F32)
        return lax.dynamic_update_slice(z, shard * 0.5, (0, chip * shard.shape[1]))

    conv_pack = _pack([place(ssd_conv_w[0], CONV_CH), place(ffn_conv_w[0], 2 * D_FF)])
    conv_full, conv_done = _allreduce_small(conv_pack)
    ssd_conv_w_full, ffn_conv_w_full = _unpack(conv_full, [(SSD_CONV, CONV_CH), (FFN_CONV, 2 * D_FF)])

    w_in_t, m_in_t, v_in_t = (jnp.transpose(a[0]) for a in (w_in, m_w_in, v_w_in))
    in_shard = (w_in_t + conv_done[:1, :1]).astype(BF16)
    in_gather = _push_start([in_shard], [(N_CHIPS,) + in_shard.shape], _route_gather_half, OTHER_CHIPS,
                            name="gather_start_w_in")
    gathers = {}

    def weight_in(after):
        send_sems, recv_sems, srcs, lands, _ = in_gather
        (own,), (got,) = _push_wait(send_sems, recv_sems, srcs, lands, after, _route_gather_half_wait, OTHER_CHIPS,
                                    name="gather_wait_w_in")
        got, order = _forward_halves(got)
        full_in_t = lax.dynamic_update_slice(got, own[None], (chip, 0, 0)).reshape(IN_PROJ_WIDTH, D_MODEL)
        w_in_dt = jnp.pad(full_in_t[MAIN_WIDTH:], ((0, LANES - SSD_HEADS), (0, 0)))
        order = order[:1, :1]
        for n, shard in (('w_out', w_out[0]), ('w_up', w_up[0]), ('w_down', w_down[0])):
            shard = (shard + order).astype(BF16)
            gathers[n] = _push_start([shard], [(N_CHIPS,) + shard.shape], _route_gather, OTHER_CHIPS,
                                     name="gather_start_" + n)
            order = gathers[n][4][:1, :1]
        return full_in_t, w_in_dt, (gathers['w_down'][4],)

    def weight(name, after):
        send_sems, recv_sems, srcs, lands, _ = gathers[name]
        (own,), (got,) = _push_wait(send_sems, recv_sems, srcs, lands, after, _route_gather_wait, OTHER_CHIPS,
                                    name="gather_wait_" + name)
        whole = lax.dynamic_update_slice(got, own[None], (chip, 0, 0))
        return whole if name == 'w_up' else whole.reshape(-1, D_MODEL)

    reductions, exchanging = {}, {}

    def flush(after):
        tokens = []
        for prev in list(exchanging):
            reductions[prev], token = _grad_scatter_start(exchanging.pop(prev), pos, after)
            tokens.append(token)
        return tokens

    def grad_ready(name, value):
        if name == 'w_in':
            main, dtp = value
            value = lax.dynamic_update_slice(main, dtp[:SSD_HEADS], (MAIN_WIDTH, 0))
        g4 = value if value.ndim == 3 else value.reshape(N_CHIPS, -1, value.shape[1])
        tokens = flush(g4)
        exchanging[name], token = _grad_exchange_start(g4, name, cols=(name == 'w_in'))
        return tokens + [token]

    small = {
        'norm_mix': norm_mix, 'sinks': sinks, 'attn_out_norm': attn_out_norm, 'ssd_conv_w': ssd_conv_w_full,
        'ssd_conv_b': ssd_conv_b, 'dt_bias': dt_bias, 'a_log': a_log, 'ssd_d': ssd_d, 'ssd_norm': ssd_norm,
        'norm_ffn': norm_ffn, 'ffn_conv_w': ffn_conv_w_full, 'ffn_conv_b': ffn_conv_b, 'norm_final': norm_final,
    }
    loss, dx, g = _local_step(x[0], loss_target[0], small,
                              _StepHooks((in_gather[4],), weight_in, weight, grad_ready))

    small_names = [n for n in WEIGHTS if n not in BIG]
    small_g = [loss[:, :1]] + [g[n] for n in small_names]
    small_shapes = [(1, 1)] + [tuple(a.shape) for a in small_g[1:]]
    reduced, _ = _allreduce_small(_pack(small_g))
    started = flush(reduced)[-1]
    red = _unpack(reduced, small_shapes)
    loss_out = red[0].reshape(())
    gsm = dict(zip(small_names, red[1:]))
    gsm['ssd_conv_w'] = lax.dynamic_slice(gsm['ssd_conv_w'], (0, chip * ssd_conv_w.shape[2]),
                                          (SSD_CONV, ssd_conv_w.shape[2]))
    gsm['ffn_conv_w'] = lax.dynamic_slice(gsm['ffn_conv_w'], (0, chip * ffn_conv_w.shape[2]),
                                          (FFN_CONV, ffn_conv_w.shape[2]))

    grads, deltas, new_m, new_v = {}, {}, {}, {}
    after = started
    for n in ('w_down', 'w_up', 'w_out', 'w_in'):
        mine, theirs = _grad_reduce_finish(reductions[n], pos, after)
        if n == 'w_in':
            outs = _adamw_halves(w_in_t, mine, theirs, m_in_t, v_in_t, pos, name="adamw_" + n, cols=True)
            outs = [jnp.transpose(o) for o in outs]
        else:
            outs = _adamw_halves(w[n][0], mine, theirs, m[n][0], v[n][0], pos, name="adamw_" + n)
        after = outs[1]
        grads[n], deltas[n], new_m[n], new_v[n] = [o[None] for o in outs]
    shapes = [tuple(w[n].shape) for n in small_names]
    gp = _pack([gsm[n] for n in small_names])
    d, m2, v2 = _adamw(_pack([w[n] for n in small_names]), gp, _pack([m[n] for n in small_names]),
                       _pack([v[n] for n in small_names]), name="adamw_small")
    for n, gg, dd, mm, vv in zip(small_names, _unpack(gp, shapes), _unpack(d, shapes), _unpack(m2, shapes),
                                 _unpack(v2, shapes)):
        grads[n], deltas[n], new_m[n], new_v[n] = gg, dd, mm, vv

    return (loss_out, dx[None], *[grads[n] for n in WEIGHTS], *[deltas[n] for n in WEIGHTS],
            *[new_m[n] for n in WEIGHTS], *[new_v[n] for n in WEIGHTS])
```

```python
import functools

import jax
import jax.numpy as jnp
from jax import lax
from jax.experimental import pallas as pl
from jax.experimental.pallas import tpu as pltpu

F32 = jnp.float32
BF16 = jnp.bfloat16

D_MODEL = 2048
N_Q_HEADS = 32
N_KV_HEADS = 8
HEAD_DIM = 64
WINDOW = 128
ATTN_BLOCK = 128
ROT_DIM = 16
ROPE_THETA = 500000.0
SSD_HEADS = 32
SSD_HEAD_DIM = 64
SSD_INNER = 2048
SSD_GROUPS = 8
SSD_STATE = 128
SSD_CONV = 4
SSD_CHUNK = 128
ATTN_WIDTH = 2048
KV_WIDTH = 512
BC_WIDTH = 1024
CONV_CH = 4096
IN_PROJ_WIDTH = 9248
MAIN_WIDTH = 9216
D_FF = 5632
FFN_CONV = 3
EPS = 1e-6
O_Q, O_K, O_V, O_Z, O_XBC, O_DT = 0, 2048, 2560, 3072, 5120, 9216

ADAM_LR = 0.001
ADAM_B1 = 0.9
ADAM_B2 = 0.999
ADAM_EPS = 1e-08
ADAM_WD = 0.01
ADAM_STEP = 10

N_CHIPS = 4
NEG = -1e30
LANES = 128
VMEM_LIMIT = 48 * 1024 * 1024
MESH = pl.DeviceIdType.MESH
HBM_SPEC = pl.BlockSpec(memory_space=pltpu.HBM)
TOKEN = jax.ShapeDtypeStruct((8, LANES), F32)

WEIGHTS = ['norm_mix', 'w_in', 'sinks', 'attn_out_norm', 'ssd_conv_w', 'ssd_conv_b', 'dt_bias', 'a_log', 'ssd_d',
           'ssd_norm', 'w_out', 'norm_ffn', 'w_up', 'ffn_conv_w', 'ffn_conv_b', 'w_down', 'norm_final']
BIG = ['w_in', 'w_out', 'w_up', 'w_down']


def _cp(sem=None, vmem=VMEM_LIMIT):
    kw = {'vmem_limit_bytes': vmem}
    if sem is not None:
        kw['dimension_semantics'] = sem
    return pltpu.CompilerParams(**kw)


def _tile(n, pref):
    if n <= pref:
        return n
    t = (pref // LANES) * LANES
    while t > LANES and n % t:
        t -= LANES
    assert n % t == 0, (n, pref)
    return t


def _rows(n, pref):
    t = min(n, pref)
    while n % t:
        t -= 8
    if 4 * t < pref:
        t = pref
        while n % t:
            t += 8
    return t


def _iota(shape, dim):
    return lax.broadcasted_iota(jnp.int32, shape, dim)


def _dot(a, b, mode='nn'):
    dn = {'nn': (((1,), (0,)), ((), ())), 'nt': (((1,), (1,)), ((), ())), 'tn': (((0,), (0,)), ((), ()))}[mode]
    return lax.dot_general(a.astype(BF16), b.astype(BF16), dn, preferred_element_type=F32)


def _dot_exact(a, b):
    return lax.dot_general(a, b, (((1,), (0,)), ((), ())), precision=lax.Precision.HIGHEST,
                           preferred_element_type=F32)


def _sigmoid(x):
    return 1.0 / (1.0 + jnp.exp(-x))


def _softplus(x):
    return jnp.maximum(x, 0.0) + jnp.log(1.0 + jnp.exp(-jnp.abs(x)))


def _matmul(a, b, *, mode, name, out_dtype=F32, add=None, deps=(), tm=1024, tn=1024, tk=2048,
            a_halves=False, b_halves=False, b_owner=False, owner_major=False, n_limit=None, k_limit=None,
            m_rows=None):
    ash, bsh = (a.shape[1:] if a_halves else a.shape), (b.shape[1:] if (b_halves or b_owner) else b.shape)
    if mode == 'nn':
        (m, k), (k2, n) = ash, bsh
    elif mode == 'nt':
        (m, k), (n, k2) = ash, bsh
    else:
        (k, m), (k2, n) = ash, bsh
    if n_limit is not None:
        assert mode == 'nt' and n_limit <= n
        n = n_limit
    if k_limit is not None:
        assert mode == 'nn' and k_limit <= k2
        k2 = k_limit
    if a_halves:
        assert mode == 'nt'
        k = 2 * k
    if b_halves:
        assert mode == 'tn'
        n = 2 * n
    if b_owner:
        assert mode in ('nn', 'nt')
        if mode == 'nn':
            n = 4 * n
        else:
            k2 = 4 * k2
    assert k == k2, (a.shape, b.shape, mode)
    tm = _tile(m, tm)
    tn = _tile(n // 4 if (owner_major or (b_owner and mode == 'nn')) else (n // 2 if b_halves else n), tn)
    tk = _tile(k // 4 if (b_owner and mode == 'nt') else (k // 2 if a_halves else k), tk)
    nk = k // tk
    has_add = add is not None
    assert not (has_add and owner_major)

    def body(*refs):
        a_ref, b_ref = refs[:2]
        add_ref = refs[2] if has_add else None

        def finish(r, o_ref):
            if has_add:
                r = r + add_ref[...].astype(F32)
            o_ref[...] = r.astype(out_dtype)

        if nk == 1:
            finish(_dot(a_ref[...], b_ref[...], mode), refs[-1])
            return
        o_ref, acc = refs[-2:]
        kk = pl.program_id(2)

        @pl.when(kk == 0)
        def _():
            acc[...] = _dot(a_ref[...], b_ref[...], mode)

        @pl.when((kk > 0) & (kk < nk - 1))
        def _():
            acc[...] += _dot(a_ref[...], b_ref[...], mode)

        @pl.when(kk == nk - 1)
        def _():
            finish(acc[...] + _dot(a_ref[...], b_ref[...], mode), o_ref)

    if mode == 'tn':
        a_spec = pl.BlockSpec((tk, tm), lambda i, j, kk: (kk, i))
    elif a_halves:
        nkh = nk // 2
        a_spec = pl.BlockSpec((None, tm, tk), lambda i, j, kk: (kk // nkh, i, kk % nkh))
    else:
        a_spec = pl.BlockSpec((tm, tk), lambda i, j, kk: (i, kk))
    if mode == 'nt' and b_owner:
        nkq = nk // 4
        b_spec = pl.BlockSpec((None, tn, tk), lambda i, j, kk: (kk // nkq, j, kk % nkq))
    elif mode == 'nt':
        b_spec = pl.BlockSpec((tn, tk), lambda i, j, kk: (j, kk))
    elif b_owner:
        njq = (n // 4) // tn
        b_spec = pl.BlockSpec((None, tk, tn), lambda i, j, kk: (j // njq, kk, j % njq))
    elif b_halves:
        njh = (n // 2) // tn
        b_spec = pl.BlockSpec((None, tk, tn), lambda i, j, kk: (j // njh, kk, j % njh))
    else:
        b_spec = pl.BlockSpec((tk, tn), lambda i, j, kk: (kk, j))
    if owner_major:
        njo = (n // 4) // tn
        o_spec = pl.BlockSpec((None, tm, tn), lambda i, j, kk: (j // njo, i, j % njo))
        out_shape = jax.ShapeDtypeStruct((N_CHIPS, m, n // 4), out_dtype)
    else:
        o_spec = pl.BlockSpec((tm, tn), lambda i, j, kk: (i, j))
        out_shape = jax.ShapeDtypeStruct((m if m_rows is None else m_rows, n), out_dtype)
    dep_spec = pl.BlockSpec((8, LANES), lambda i, j, kk: (0, 0))
    in_specs = [a_spec, b_spec] + ([pl.BlockSpec((tm, tn), lambda i, j, kk: (i, j))] if has_add else [])
    in_specs += [dep_spec] * len(deps)
    args = (a, b) + ((add,) if has_add else ()) + tuple(deps)
    return pl.pallas_call(
        body, name=name, grid=(m // tm, n // tn, nk), in_specs=in_specs, out_specs=o_spec, out_shape=out_shape,
        scratch_shapes=[pltpu.VMEM((tm, tn), F32)] if nk > 1 else [],
        compiler_params=_cp(("parallel", "parallel", "arbitrary")))(*args)


def _rmsnorm_fwd(x, g, name, deps=()):
    t, d = x.shape
    tb = _rows(t, 256)

    def body(x_ref, g_ref, *rest):
        o_ref = rest[-1]
        xv = x_ref[...]
        r = lax.rsqrt(jnp.mean(xv * xv, axis=-1, keepdims=True) + EPS)
        o_ref[...] = (xv * r * g_ref[...]).astype(BF16)

    dep_spec = pl.BlockSpec((8, LANES), lambda i: (0, 0))
    return pl.pallas_call(
        body, name=name, grid=(t // tb,),
        in_specs=[pl.BlockSpec((tb, d), lambda i: (i, 0)), pl.BlockSpec((1, d), lambda i: (0, 0))]
        + [dep_spec] * len(deps),
        out_specs=pl.BlockSpec((tb, d), lambda i: (i, 0)), out_shape=jax.ShapeDtypeStruct((t, d), BF16),
        compiler_params=_cp(("parallel",)))(x, g, *deps)


def _rmsnorm_bwd(x, g, dy, res, name, deps=()):
    t, d = x.shape
    tb = _rows(t, 256)

    def body(x_ref, g_ref, dy_ref, res_ref, *rest):
        dx_ref, dx16_ref, dg_ref = rest[-3:]
        i = pl.program_id(0)
        xv = x_ref[...]
        dyv = dy_ref[...].astype(F32)
        r = lax.rsqrt(jnp.mean(xv * xv, axis=-1, keepdims=True) + EPS)
        u = dyv * g_ref[...]
        dx = r * u - xv * (r * r * r * jnp.mean(u * xv, axis=-1, keepdims=True)) + res_ref[...]
        dx_ref[...] = dx
        dx16_ref[...] = dx.astype(BF16)
        part = jnp.sum(dyv * xv * r, axis=0, keepdims=True)

        @pl.when(i == 0)
        def _():
            dg_ref[...] = part

        @pl.when(i > 0)
        def _():
            dg_ref[...] += part

    row = pl.BlockSpec((tb, d), lambda i: (i, 0))
    vec = pl.BlockSpec((1, d), lambda i: (0, 0))
    return pl.pallas_call(
        body, name=name, grid=(t // tb,),
        in_specs=[row, vec, row, row] + [pl.BlockSpec((8, LANES), lambda i: (0, 0))] * len(deps),
        out_specs=[row, row, vec],
        out_shape=[jax.ShapeDtypeStruct((t, d), F32), jax.ShapeDtypeStruct((t, d), BF16),
                   jax.ShapeDtypeStruct((1, d), F32)],
        compiler_params=_cp(("arbitrary",)))(x, g, dy, res, *deps)


def _final_loss(h, g, tgt):
    t, d = h.shape
    tb = _rows(t, 256)

    def body(h_ref, g_ref, t_ref, loss_ref, dh_ref, dh16_ref, dg_ref):
        i = pl.program_id(0)
        hv = h_ref[...]
        gv = g_ref[...]
        r = lax.rsqrt(jnp.mean(hv * hv, axis=-1, keepdims=True) + EPS)
        y = hv * r * gv
        diff = y - t_ref[...]
        lpart = jnp.sum(jnp.sum(diff * diff, axis=1, keepdims=True), axis=0, keepdims=True) * (0.5 / d)
        dy = diff * (1.0 / d)
        u = dy * gv
        dh = r * u - hv * (r * r * r * jnp.mean(u * hv, axis=-1, keepdims=True))
        dh_ref[...] = dh
        dh16_ref[...] = dh.astype(BF16)
        gpart = jnp.sum(dy * hv * r, axis=0, keepdims=True)
        lrow = jnp.broadcast_to(lpart, (1, LANES))

        @pl.when(i == 0)
        def _():
            loss_ref[...] = lrow
            dg_ref[...] = gpart

        @pl.when(i > 0)
        def _():
            loss_ref[...] += lrow
            dg_ref[...] += gpart

    row = pl.BlockSpec((tb, d), lambda i: (i, 0))
    vec = pl.BlockSpec((1, d), lambda i: (0, 0))
    return pl.pallas_call(
        body, name="final_loss", grid=(t // tb,), in_specs=[row, vec, row],
        out_specs=[pl.BlockSpec((1, LANES), lambda i: (0, 0)), row, row, vec],
        out_shape=[jax.ShapeDtypeStruct((1, LANES), F32), jax.ShapeDtypeStruct((t, d), F32),
                   jax.ShapeDtypeStruct((t, d), BF16), jax.ShapeDtypeStruct((1, d), F32)],
        compiler_params=_cp(("arbitrary",)))(h, g, tgt)


def _rope_tables(t):
    pos = jnp.arange(t, dtype=F32)
    inv = 1.0 / (ROPE_THETA ** (jnp.arange(0, ROT_DIM, 2, dtype=F32) / ROT_DIM))
    ang = pos[:, None] * inv[None, :]
    cos, sin = jnp.cos(ang), jnp.sin(ang)
    half = ROT_DIM // 2
    rest = HEAD_DIM - ROT_DIM
    c = jnp.concatenate([cos, cos, jnp.ones((t, rest), F32)], axis=1)
    s1 = jnp.concatenate([-sin, jnp.zeros((t, half + rest), F32)], axis=1)
    s2 = jnp.concatenate([jnp.zeros((t, half), F32), sin, jnp.zeros((t, rest), F32)], axis=1)
    return jnp.concatenate([jnp.tile(v, (1, LANES // HEAD_DIM)) for v in (c, s1, s2)], axis=1)


def _split_tables(tab):
    return tab[:, :LANES], tab[:, LANES:2 * LANES], tab[:, 2 * LANES:]


def _rope(x, c, s1, s2):
    half = ROT_DIM // 2
    return x * c + pltpu.roll(x, LANES - half, 1) * s1 + pltpu.roll(x, half, 1) * s2


def _rope_t(g, c, s1, s2):
    half = ROT_DIM // 2
    return g * c + pltpu.roll(g * s1, half, 1) + pltpu.roll(g * s2, LANES - half, 1)


def _band_masks(i, heads):
    n = heads * ATTN_BLOCK
    q = jnp.bitwise_and(_iota((n, ATTN_BLOCK), 0), ATTN_BLOCK - 1)
    j = _iota((n, ATTN_BLOCK), 1)
    upper = j > q
    return upper, upper & (j < jnp.where(i > 0, 0, ATTN_BLOCK))


def _fold_band(full, upper):
    return jnp.where(upper, full[:, :ATTN_BLOCK], full[:, ATTN_BLOCK:])


def _unfold_band(band, upper):
    return jnp.concatenate([jnp.where(upper, band, 0.0), jnp.where(upper, 0.0, band)], axis=1)


def _half_masks():
    lane = _iota((1, LANES), 1)
    return [(lane < HEAD_DIM).astype(F32), (lane >= HEAD_DIM).astype(F32)]


def _stack_heads(blocks, hm, j):
    pieces = []
    for r in range(4):
        qb, half = (4 * j + r) // 2, (4 * j + r) % 2
        piece = blocks[qb] * hm[half]
        if half != j:
            piece = pltpu.roll(piece, HEAD_DIM, 1)
        pieces.append(piece)
    return jnp.concatenate(pieces, axis=0)


def _unstack_heads(stacked, j):
    out = []
    for qb in (2 * j, 2 * j + 1):
        acc = None
        for half in range(2):
            r = 2 * qb + half - 4 * j
            piece = stacked[r * ATTN_BLOCK:(r + 1) * ATTN_BLOCK]
            if half != j:
                piece = pltpu.roll(piece, HEAD_DIM, 1)
            acc = piece if acc is None else acc + piece
        out.append((qb, acc))
    return out


def _sink_column(sink_ref, base):
    return jnp.concatenate([jnp.full((ATTN_BLOCK, 1), sink_ref[base + r], F32) for r in range(4)], axis=0)


def _attn_specs(nb_clamp):
    blk = ATTN_BLOCK
    kb, vb = O_K // LANES, O_V // LANES

    def cur(i):
        return jnp.minimum(i, nb_clamp)

    def prev(i):
        return jnp.maximum(jnp.minimum(i, nb_clamp + 1) - 1, 0)

    q = pl.BlockSpec((blk, 512), lambda p, i: (cur(i), p))
    kc = pl.BlockSpec((blk, LANES), lambda p, i: (cur(i), kb + p))
    kp = pl.BlockSpec((blk, LANES), lambda p, i: (prev(i), kb + p))
    vc = pl.BlockSpec((blk, LANES), lambda p, i: (cur(i), vb + p))
    vp = pl.BlockSpec((blk, LANES), lambda p, i: (prev(i), vb + p))
    tc = pl.BlockSpec((blk, 3 * LANES), lambda p, i: (cur(i), 0))
    tp = pl.BlockSpec((blk, 3 * LANES), lambda p, i: (prev(i), 0))
    return q, kc, kp, vc, vp, tc, tp


def _attn_fwd(proj, sinks, tables):
    t = proj.shape[0]
    nb = t // ATTN_BLOCK
    scale = HEAD_DIM ** -0.5

    def body(sink_ref, q_ref, kc_ref, kp_ref, vc_ref, vp_ref, tc_ref, tp_ref, o_ref):
        p = pl.program_id(0)
        i = pl.program_id(1)
        cc, s1c, s2c = _split_tables(tc_ref[...])
        kband = jnp.concatenate([_rope(kp_ref[...], *_split_tables(tp_ref[...])),
                                 _rope(kc_ref[...], cc, s1c, s2c)], axis=0).astype(BF16)
        vband = jnp.concatenate([vp_ref[...], vc_ref[...]], axis=0)
        hm = _half_masks()
        vsel = [(vband * hm[j]).astype(BF16) for j in range(2)]
        upper, dropped = _band_masks(i, 1)
        qr = [_rope(q_ref[:, qb * LANES:(qb + 1) * LANES], cc, s1c, s2c) for qb in range(4)]

        def scores(hh):
            qb, half, j = hh // 2, hh % 2, hh // 4
            qs = qr[qb] * hm[half]
            if half != j:
                qs = pltpu.roll(qs, HEAD_DIM, 1)
            return _dot(qs, kband, 'nt')

        ahead = scores(0)
        acc = None
        for hh in range(8):
            qb, half, j = hh // 2, hh % 2, hh // 4
            raw = ahead
            if hh + 1 < 8:
                ahead = scores(hh + 1)
            s = jnp.where(dropped, NEG, _fold_band(raw, upper) * scale)
            sink = sink_ref[p * 8 + hh]
            m = jnp.maximum(jnp.max(s, axis=1, keepdims=True), sink)
            pe = jnp.exp(s - m)
            den = jnp.sum(pe, axis=1, keepdims=True) + jnp.exp(sink - m)
            o = _dot(_unfold_band(pe / den, upper), vsel[j])
            if half != j:
                o = pltpu.roll(o, HEAD_DIM, 1)
            acc = o if half == 0 else acc + o
            if half == 1:
                o_ref[:, qb * LANES:(qb + 1) * LANES] = acc

    q, kc, kp, vc, vp, tc, tp = _attn_specs(nb - 1)
    smem = pl.BlockSpec(memory_space=pltpu.SMEM)
    return pl.pallas_call(
        body, name="attn_fwd", grid=(4, nb),
        in_specs=[smem, q, kc, kp, vc, vp, tc, tp],
        out_specs=pl.BlockSpec((ATTN_BLOCK, 512), lambda p, i: (i, p)),
        out_shape=jax.ShapeDtypeStruct((t, ATTN_WIDTH), F32),
        compiler_params=_cp(("parallel", "arbitrary")))(sinks, proj, proj, proj, proj, proj, tables, tables)


def _attn_bwd(proj, sinks, tables, dout):
    t = proj.shape[0]
    nb = t // ATTN_BLOCK
    scale = HEAD_DIM ** -0.5

    def body(sink_ref, q_ref, kc_ref, kp_ref, vc_ref, vp_ref, tc_ref, tp_ref,
             do_ref, dq_ref, dk_ref, dv_ref, ds_ref, carry_k, carry_v):
        p = pl.program_id(0)
        i = pl.program_id(1)
        ptab = _split_tables(tp_ref[...])

        @pl.when(i == 0)
        def _():
            carry_k[...] = jnp.zeros_like(carry_k)
            carry_v[...] = jnp.zeros_like(carry_v)
            ds_ref[...] = jnp.zeros_like(ds_ref)

        @pl.when(i < nb)
        def _():
            cc, s1c, s2c = _split_tables(tc_ref[...])
            kband = jnp.concatenate([_rope(kp_ref[...], *ptab), _rope(kc_ref[...], cc, s1c, s2c)], axis=0)
            vband = jnp.concatenate([vp_ref[...], vc_ref[...]], axis=0)
            hm = _half_masks()
            kband16 = kband.astype(BF16)
            vband16 = vband.astype(BF16)
            upper, dropped = _band_masks(i, 4)
            dkb = jnp.zeros((2 * ATTN_BLOCK, LANES), F32)
            dvb = jnp.zeros((2 * ATTN_BLOCK, LANES), F32)
            row8 = _iota((8, LANES), 0)
            dsink = jnp.zeros((8, LANES), F32)
            qr = [_rope(q_ref[:, qb * LANES:(qb + 1) * LANES], cc, s1c, s2c) for qb in range(4)]
            dob = [do_ref[:, qb * LANES:(qb + 1) * LANES] for qb in range(4)]
            for j in range(2):
                qst = _stack_heads(qr, hm, j).astype(BF16)
                dost = _stack_heads(dob, hm, j).astype(BF16)
                s = jnp.where(dropped, NEG, _fold_band(_dot(qst, kband16, 'nt'), upper) * scale)
                sink = _sink_column(sink_ref, p * 8 + 4 * j)
                m = jnp.maximum(jnp.max(s, axis=1, keepdims=True), sink)
                pe = jnp.exp(s - m)
                psink = jnp.exp(sink - m)
                den = jnp.sum(pe, axis=1, keepdims=True) + psink
                pr = pe / den
                dvb = dvb + _dot(_unfold_band(pr, upper).T, dost)
                dp = _fold_band(_dot(dost, vband16, 'nt'), upper)
                delta = jnp.sum(pr * dp, axis=1, keepdims=True)
                dsc = _unfold_band(pr * (dp - delta) * scale, upper)
                dsk = psink / den * delta
                for r in range(4):
                    part = jnp.sum(dsk[r * ATTN_BLOCK:(r + 1) * ATTN_BLOCK])
                    dsink = dsink + jnp.where(row8 == 4 * j + r, -part, 0.0)
                for qb, dqb in _unstack_heads(_dot(dsc, kband * hm[j]), j):
                    dq_ref[:, qb * LANES:(qb + 1) * LANES] = _rope_t(dqb, cc, s1c, s2c).astype(BF16)
                dkb = dkb + _dot(dsc.T, qst)
            ds_ref[0] += dsink
            dk_ref[...] = _rope_t(carry_k[...] + dkb[:ATTN_BLOCK], *ptab).astype(BF16)
            dv_ref[...] = (carry_v[...] + dvb[:ATTN_BLOCK]).astype(BF16)
            carry_k[...] = dkb[ATTN_BLOCK:]
            carry_v[...] = dvb[ATTN_BLOCK:]

        @pl.when(i == nb)
        def _():
            dk_ref[...] = _rope_t(carry_k[...], *ptab).astype(BF16)
            dv_ref[...] = carry_v[...].astype(BF16)

    q, kc, kp, vc, vp, tc, tp = _attn_specs(nb - 1)
    smem = pl.BlockSpec(memory_space=pltpu.SMEM)
    qblk = pl.BlockSpec((ATTN_BLOCK, 512), lambda p, i: (jnp.minimum(i, nb - 1), p))
    kvout = pl.BlockSpec((ATTN_BLOCK, LANES), lambda p, i: (jnp.maximum(i - 1, 0), p))
    return pl.pallas_call(
        body, name="attn_bwd", grid=(4, nb + 1),
        in_specs=[smem, q, kc, kp, vc, vp, tc, tp, qblk],
        out_specs=[qblk, kvout, kvout, pl.BlockSpec((1, 8, LANES), lambda p, i: (p, 0, 0))],
        out_shape=[jax.ShapeDtypeStruct((t, ATTN_WIDTH), BF16), jax.ShapeDtypeStruct((t, KV_WIDTH), BF16),
                   jax.ShapeDtypeStruct((t, KV_WIDTH), BF16), jax.ShapeDtypeStruct((4, 8, LANES), F32)],
        scratch_shapes=[pltpu.VMEM((ATTN_BLOCK, LANES), F32), pltpu.VMEM((ATTN_BLOCK, LANES), F32)],
        compiler_params=_cp(("parallel", "arbitrary")))(sinks, proj, proj, proj, proj, proj, tables, tables, dout)


def _shift_rows(x, prev8, j):
    n, c = x.shape
    r = pltpu.roll(x.reshape(n // 8, 8, c), j, 1)
    before = pltpu.roll(prev8, j, 0)[None]
    if n > 8:
        before = jnp.concatenate([before, r[:-1]], axis=0)
    return jnp.where(_iota((1, 8, 1), 1) < j, before, r).reshape(n, c)


def _shift_rows_up(x, next8, j):
    n, c = x.shape
    r = pltpu.roll(x.reshape(n // 8, 8, c), 8 - j, 1)
    after = pltpu.roll(next8, 8 - j, 0)[None]
    if n > 8:
        after = jnp.concatenate([r[1:], after], axis=0)
    return jnp.where(_iota((1, 8, 1), 1) >= 8 - j, after, r).reshape(n, c)


def _conv_apply(x, prev8, w, b, taps):
    u = b + x * w[taps - 1:taps]
    for j in range(1, taps):
        u = u + _shift_rows(x, prev8, j) * w[taps - 1 - j:taps - j]
    return u


def _conv_grads(du, du_next8, x, w, taps):
    dx = du * w[taps - 1:taps]
    rowk = _iota((taps, 1), 0)
    dw = jnp.where(rowk == taps - 1, jnp.sum(du * x, axis=0, keepdims=True), 0.0)
    for j in range(1, taps):
        ahead = _shift_rows_up(du, du_next8, j)
        dx = dx + ahead * w[taps - 1 - j:taps - j]
        dw = dw + jnp.where(rowk == taps - 1 - j, jnp.sum(ahead * x, axis=0, keepdims=True), 0.0)
    return dx, dw, jnp.sum(du, axis=0, keepdims=True)


def _conv_specs(tb, tc, col0, t):
    c0 = col0 // tc
    cur = pl.BlockSpec((tb, tc), lambda j, i: (i, c0 + j))
    prev = pl.BlockSpec((8, tc), lambda j, i: (jnp.maximum(i * (tb // 8) - 1, 0), c0 + j))
    nxt = pl.BlockSpec((8, tc), lambda j, i: (jnp.minimum((i + 1) * (tb // 8), t // 8 - 1), c0 + j))
    return cur, prev, nxt


def _conv_silu_fwd(x, w, b, *, col0, width, name):
    t = x.shape[0]
    taps = w.shape[0]
    tb, tc = _rows(t, 512), _tile(width, 1024)
    assert col0 % tc == 0

    def body(x_ref, xp_ref, w_ref, b_ref, o_ref, u_ref):
        i = pl.program_id(1)
        prev8 = jnp.where(i > 0, xp_ref[...], 0.0)
        u = _conv_apply(x_ref[...], prev8, w_ref[...], b_ref[...], taps)
        u_ref[...] = u
        o_ref[...] = u * _sigmoid(u)

    cur, prev, _ = _conv_specs(tb, tc, col0, t)
    par = pl.BlockSpec((taps, tc), lambda j, i: (0, j))
    bias = pl.BlockSpec((1, tc), lambda j, i: (0, j))
    out = pl.BlockSpec((tb, tc), lambda j, i: (i, j))
    shp = jax.ShapeDtypeStruct((t, width), F32)
    return pl.pallas_call(
        body, name=name, grid=(width // tc, t // tb), in_specs=[cur, prev, par, bias], out_specs=[out, out],
        out_shape=[shp, shp], compiler_params=_cp(("parallel", "parallel")))(x, x, w, b)


def _dsilu(u):
    sg = _sigmoid(u)
    return sg * (1.0 + u * (1.0 - sg))


def _ssd_conv_bwd(x, w, dxs, dbm, dcm, *, col0, name):
    t = x.shape[0]
    taps = w.shape[0]
    tb, tc = _rows(t, 512), BC_WIDTH
    nrow, ncol = t // tb, CONV_CH // tc
    c0 = col0 // tc

    def body(x_ref, w_ref, xs_ref, xsn_ref, bm_ref, bmn_ref, cm_ref, cmn_ref, dx_ref, dw_ref, db_ref):
        i = pl.program_id(0)
        j = pl.program_id(1)

        def run(du_ref, dun_ref):
            next8 = jnp.where(i < nrow - 1, dun_ref[...], 0.0)
            dx, dwv, dbv = _conv_grads(du_ref[...], next8, x_ref[...], w_ref[...], taps)
            dx_ref[...] = dx.astype(BF16)

            @pl.when(i == 0)
            def _():
                dw_ref[j] = dwv
                db_ref[j] = dbv

            @pl.when(i > 0)
            def _():
                dw_ref[j] += dwv
                db_ref[j] += dbv

        pl.when(j < 2)(lambda: run(xs_ref, xsn_ref))
        pl.when(j == 2)(lambda: run(bm_ref, bmn_ref))
        pl.when(j == 3)(lambda: run(cm_ref, cmn_ref))

    def nxt_row(i):
        return jnp.minimum((i + 1) * (tb // 8), t // 8 - 1)

    xs_col = lambda j: jnp.minimum(j, SSD_INNER // tc - 1)
    in_specs = [pl.BlockSpec((tb, tc), lambda i, j: (i, c0 + j)), pl.BlockSpec((taps, tc), lambda i, j: (0, j)),
                pl.BlockSpec((tb, tc), lambda i, j: (i, xs_col(j))),
                pl.BlockSpec((8, tc), lambda i, j: (nxt_row(i), xs_col(j))),
                pl.BlockSpec((tb, tc), lambda i, j: (i, 0)), pl.BlockSpec((8, tc), lambda i, j: (nxt_row(i), 0)),
                pl.BlockSpec((tb, tc), lambda i, j: (i, 0)), pl.BlockSpec((8, tc), lambda i, j: (nxt_row(i), 0))]
    dx, dw, db = pl.pallas_call(
        body, name=name, grid=(nrow, ncol), in_specs=in_specs,
        out_specs=[pl.BlockSpec((tb, tc), lambda i, j: (i, j)),
                   pl.BlockSpec((ncol, taps, tc), lambda i, j: (0, 0, 0)),
                   pl.BlockSpec((ncol, 1, tc), lambda i, j: (0, 0, 0))],
        out_shape=[jax.ShapeDtypeStruct((t, CONV_CH), BF16), jax.ShapeDtypeStruct((ncol, taps, tc), F32),
                   jax.ShapeDtypeStruct((ncol, 1, tc), F32)],
        compiler_params=_cp(("arbitrary", "arbitrary")))(x, w, dxs, dxs, dbm, dbm, dcm, dcm)
    return dx, dw.transpose(1, 0, 2).reshape(taps, CONV_CH), db.transpose(1, 0, 2).reshape(1, CONV_CH)


def _ffn_specs(tb, tc, t):
    nc = D_FF // tc

    def cur(half):
        return pl.BlockSpec((tb, tc), lambda j, i: (i, half * nc + j))

    def prev(half):
        return pl.BlockSpec((8, tc), lambda j, i: (jnp.maximum(i * (tb // 8) - 1, 0), half * nc + j))

    def nxt(half):
        return pl.BlockSpec((8, tc), lambda j, i: (jnp.minimum((i + 1) * (tb // 8), t // 8 - 1), half * nc + j))

    def par(rows, half):
        return pl.BlockSpec((rows, tc), lambda j, i: (0, half * nc + j))

    return cur, prev, nxt, par


def _ffn_act_fwd(u0, w, b):
    t = u0.shape[0]
    tb, tc = _rows(t, 512), _tile(D_FF, 1408)
    cur, prev, _, par = _ffn_specs(tb, tc, t)

    def body(g_ref, gp_ref, v_ref, vp_ref, wg_ref, wv_ref, bg_ref, bv_ref, o_ref, u_ref):
        i = pl.program_id(1)
        ug = _conv_apply(g_ref[...], jnp.where(i > 0, gp_ref[...], 0.0), wg_ref[...], bg_ref[...], FFN_CONV)
        uv = _conv_apply(v_ref[...], jnp.where(i > 0, vp_ref[...], 0.0), wv_ref[...], bv_ref[...], FFN_CONV)
        o_ref[...] = (ug * _sigmoid(ug) * uv).astype(BF16)
        u_ref[0] = ug
        u_ref[1] = uv

    return pl.pallas_call(
        body, name="ffn_act_fwd", grid=(D_FF // tc, t // tb),
        in_specs=[cur(0), prev(0), cur(1), prev(1), par(FFN_CONV, 0), par(FFN_CONV, 1), par(1, 0), par(1, 1)],
        out_specs=[pl.BlockSpec((tb, tc), lambda j, i: (i, j)), pl.BlockSpec((2, tb, tc), lambda j, i: (0, i, j))],
        out_shape=[jax.ShapeDtypeStruct((t, D_FF), BF16), jax.ShapeDtypeStruct((2, t, D_FF), F32)],
        compiler_params=_cp(("parallel", "parallel")))(u0, u0, u0, u0, w, w, b, b)


def _ffn_act_bwd(u0, u, w, da):
    t = u0.shape[0]
    tb, tc = _rows(t, 256), _tile(D_FF, 1408)
    nrow = t // tb
    taps = FFN_CONV
    cur, _, _, par = _ffn_specs(tb, tc, t)

    def dact(ug, uv, dav):
        sg = _sigmoid(ug)
        return dav * uv * (sg * (1.0 + ug * (1.0 - sg))), dav * ug * sg

    def body(g_ref, v_ref, u_ref, un_ref, wg_ref, wv_ref, da_ref, dan_ref, dx_ref, dw_ref, db_ref):
        i = pl.program_id(1)
        dug, duv = dact(u_ref[0], u_ref[1], da_ref[...].astype(F32))
        dan = jnp.where(i < nrow - 1, dan_ref[...].astype(F32)[:8], 0.0)
        dugn, duvn = dact(un_ref[0], un_ref[1], dan)
        dxg, dwg, dbg = _conv_grads(dug, dugn, g_ref[...], wg_ref[...], taps)
        dxv, dwv, dbv = _conv_grads(duv, duvn, v_ref[...], wv_ref[...], taps)
        dx_ref[0] = dxg.astype(BF16)
        dx_ref[1] = dxv.astype(BF16)

        @pl.when(i == 0)
        def _():
            dw_ref[0] = dwg
            dw_ref[1] = dwv
            db_ref[0] = dbg
            db_ref[1] = dbv

        @pl.when(i > 0)
        def _():
            dw_ref[0] += dwg
            dw_ref[1] += dwv
            db_ref[0] += dbg
            db_ref[1] += dbv

    both = pl.BlockSpec((2, tb, tc), lambda j, i: (0, i, j))
    both_nxt = pl.BlockSpec((2, 8, tc), lambda j, i: (0, jnp.minimum((i + 1) * (tb // 8), t // 8 - 1), j))
    da_cur = pl.BlockSpec((tb, tc), lambda j, i: (i, j))
    da_nxt = pl.BlockSpec((16, tc), lambda j, i: (jnp.minimum((i + 1) * (tb // 16), t // 16 - 1), j))
    return pl.pallas_call(
        body, name="ffn_act_bwd", grid=(D_FF // tc, nrow),
        in_specs=[cur(0), cur(1), both, both_nxt, par(taps, 0), par(taps, 1), da_cur, da_nxt],
        out_specs=[both, pl.BlockSpec((2, taps, tc), lambda j, i: (0, 0, j)),
                   pl.BlockSpec((2, 1, tc), lambda j, i: (0, 0, j))],
        out_shape=[jax.ShapeDtypeStruct((2, t, D_FF), BF16), jax.ShapeDtypeStruct((2, taps, D_FF), F32),
                   jax.ShapeDtypeStruct((2, 1, D_FF), F32)],
        compiler_params=_cp(("parallel", "arbitrary")))(u0, u0, u, u, w, w, da, da)


def _head_masks():
    lane = _iota((1, 4 * SSD_HEAD_DIM), 1)
    return [((lane >= r * SSD_HEAD_DIM) & (lane < (r + 1) * SSD_HEAD_DIM)).astype(F32) for r in range(4)]


def _segsum(v):
    first = _iota((1, LANES), 1) < SSD_HEAD_DIM
    halves = []
    for k in range(2):
        vh = v[:, k * LANES:(k + 1) * LANES]
        both = jnp.sum(vh, axis=1, keepdims=True)
        one = jnp.sum(jnp.where(first, vh, 0.0), axis=1, keepdims=True)
        halves.append(jnp.where(first, one, both - one))
    return jnp.concatenate(halves, axis=1)


def _ssd_common(raw_e, prow, rawr4, bcol, acol):
    n = SSD_CHUNK
    dt_e = _softplus(raw_e + prow[0:1, :])
    a_e = -jnp.exp(prow[1:2, :])
    d_e = prow[2:3, :]
    tril = (_iota((n, n), 0) >= _iota((n, n), 1)).astype(F32)
    acs_e = _dot_exact(tril, dt_e * a_e)
    last_e = acs_e[n - 1:n, :]
    dtr4 = _softplus(rawr4 + bcol)
    triu = (_iota((n, n), 0) <= _iota((n, n), 1)).astype(F32)
    acs_r4 = _dot_exact(dtr4 * (-jnp.exp(acol)), triu)
    return dt_e, a_e, d_e, acs_e, last_e, acs_r4


def _decay_matrix(acs_e, acs_r4, r):
    n = SSD_CHUNK
    col = acs_e[:, r * SSD_HEAD_DIM:r * SSD_HEAD_DIM + 1]
    seg = col - acs_r4[r:r + 1, :]
    causal = _iota((n, n), 0) >= _iota((n, n), 1)
    return jnp.exp(jnp.where(causal, seg, NEG))


SSD_STEP_CHUNKS = 4
SSD_ROWS = SSD_STEP_CHUNKS * SSD_CHUNK


def _ssd_specs(t, rev):
    nb = t // SSD_ROWS
    xb, bb, cb = 0, SSD_INNER // SSD_STATE, (SSD_INNER + BC_WIDTH) // SSD_STATE

    def ch(c):
        return (nb - 1 - c) if rev else c

    x = pl.BlockSpec((SSD_ROWS, 256), lambda g, c: (ch(c), xb + g))
    bm = pl.BlockSpec((SSD_ROWS, SSD_STATE), lambda g, c: (ch(c), bb + g))
    cm = pl.BlockSpec((SSD_ROWS, SSD_STATE), lambda g, c: (ch(c), cb + g))
    dtc = pl.BlockSpec((1, SSD_ROWS, 256), lambda g, c: (g, ch(c), 0))
    dtr = pl.BlockSpec((1, 4, SSD_ROWS), lambda g, c: (g, 0, ch(c)))
    prow = pl.BlockSpec((1, 3, 256), lambda g, c: (g, 0, 0))
    pcol = pl.BlockSpec((1, 4, 1), lambda g, c: (g, 0, 0))
    st = pl.BlockSpec((1, SSD_STEP_CHUNKS, SSD_STATE, 256), lambda g, c: (g, ch(c), 0, 0))
    return x, bm, cm, dtc, dtr, prow, pcol, st, ch


def _ssd_params(dt_raw, dt_bias, a_log, ssd_d):
    t = dt_raw.shape[0]
    by_group = dt_raw.reshape(t, SSD_GROUPS, 4)
    dtc = jnp.repeat(by_group, SSD_HEAD_DIM, axis=2).transpose(1, 0, 2)
    dtr = by_group.transpose(1, 2, 0)
    prow = jnp.repeat(jnp.stack([dt_bias.reshape(SSD_GROUPS, 4), a_log.reshape(SSD_GROUPS, 4),
                                 ssd_d.reshape(SSD_GROUPS, 4)], axis=1), SSD_HEAD_DIM, axis=2)
    bcol = dt_bias.reshape(SSD_GROUPS, 4, 1)
    acol = a_log.reshape(SSD_GROUPS, 4, 1)
    return dtc, dtr, prow, bcol, acol


def _ssd_fwd(xbc, params):
    t = xbc.shape[0]
    nc = t // SSD_CHUNK
    dtc, dtr, prow, bcol, acol = params

    def body(x_ref, b_ref, c_ref, dtc_ref, dtr_ref, prow_ref, bcol_ref, acol_ref, y_ref, st_ref, s_scr):
        c = pl.program_id(1)

        @pl.when(c == 0)
        def _():
            s_scr[...] = jnp.zeros_like(s_scr)

        masks = _head_masks()
        s = s_scr[...]
        for k in range(SSD_STEP_CHUNKS):
            rows = slice(k * SSD_CHUNK, (k + 1) * SSD_CHUNK)
            dt_e, a_e, d_e, acs_e, last_e, acs_r4 = _ssd_common(
                dtc_ref[0, rows], prow_ref[0], dtr_ref[0][:, rows], bcol_ref[0], acol_ref[0])
            xv = x_ref[rows]
            bm, cm = b_ref[rows], c_ref[rows]
            st_ref[0, k] = s
            xdt = xv * dt_e
            cb = _dot(cm, bm, 'nt')
            y = _dot(cm, s) * jnp.exp(acs_e) + xv * d_e
            for r in range(4):
                mr = cb * _decay_matrix(acs_e, acs_r4, r)
                y = y + _dot(mr, xdt * masks[r])
            y_ref[rows] = y
            w = xdt * jnp.exp(last_e - acs_e)
            s = s * jnp.exp(last_e) + _dot(bm.T, w)
        s_scr[...] = s

    x, bm, cm, dtcs, dtrs, prs, pcs, st, _ = _ssd_specs(t, False)
    return pl.pallas_call(
        body, name="ssd_fwd", grid=(SSD_GROUPS, t // SSD_ROWS), in_specs=[x, bm, cm, dtcs, dtrs, prs, pcs, pcs],
        out_specs=[pl.BlockSpec((SSD_ROWS, 256), lambda g, c: (c, g)), st],
        out_shape=[jax.ShapeDtypeStruct((t, SSD_INNER), F32),
                   jax.ShapeDtypeStruct((SSD_GROUPS, nc, SSD_STATE, 256), F32)],
        scratch_shapes=[pltpu.VMEM((SSD_STATE, 256), F32)],
        compiler_params=_cp(("parallel", "arbitrary")))(xbc, xbc, xbc, dtc, dtr, prow, bcol, acol)


def _ssd_bwd(xbc, pre, params, states, dy):
    t = xbc.shape[0]
    nc = t // SSD_CHUNK
    n = SSD_CHUNK
    dtc, dtr, prow, bcol, acol = params

    def body(x_ref, b_ref, c_ref, ux_ref, ub_ref, uc_ref, dtc_ref, dtr_ref, prow_ref, bcol_ref, acol_ref, st_ref,
             dy_ref, dx_ref, db_ref, dc_ref, ddt_ref, dp_ref, ds_scr):
        c = pl.program_id(1)

        @pl.when(c == 0)
        def _():
            ds_scr[...] = jnp.zeros_like(ds_scr)
            dp_ref[...] = jnp.zeros_like(dp_ref)

        masks = _head_masks()
        ds = ds_scr[...]
        for k in reversed(range(SSD_STEP_CHUNKS)):
            rows = slice(k * SSD_CHUNK, (k + 1) * SSD_CHUNK)
            raw_e = dtc_ref[0, rows]
            prw = prow_ref[0]
            dt_e, a_e, d_e, acs_e, last_e, acs_r4 = _ssd_common(raw_e, prw, dtr_ref[0][:, rows], bcol_ref[0], acol_ref[0])
            xv = x_ref[rows]
            bm, cm = b_ref[rows], c_ref[rows]
            s = st_ref[0, k]
            dyv = dy_ref[rows]
            e_e = jnp.exp(acs_e)
            dec_e = jnp.exp(last_e - acs_e)
            cd_e = jnp.exp(last_e)
            xdt = xv * dt_e
            w = xdt * dec_e
            b16, c16, s16, ds16 = bm.astype(BF16), cm.astype(BF16), s.astype(BF16), ds.astype(BF16)
            cb = _dot(c16, b16, 'nt')
            yoff_raw = _dot(c16, s16)
            dye = dyv * e_e
            dye16 = dye.astype(BF16)
            dcm = _dot(dye16, s16, 'nt')
            ds_prev = ds * cd_e + _dot(cm.T, dye16)
            dacs_e = _segsum(dyv * yoff_raw) * e_e
            dw = _dot(b16, ds16)
            dbm = _dot(w, ds16, 'nt')
            tdec = _segsum(dw * xdt) * dec_e
            dacs_e = dacs_e - tdec
            dlast_e = jnp.sum(tdec, axis=0, keepdims=True)
            dxdt = dw * dec_e
            dlast_e = dlast_e + _segsum(jnp.sum(ds * s, axis=0, keepdims=True)) * cd_e
            dcb = jnp.zeros((n, n), F32)
            for r in range(4):
                lm = _decay_matrix(acs_e, acs_r4, r)
                mr = cb * lm
                dyr16 = (dyv * masks[r]).astype(BF16)
                dm = _dot(dyr16, xdt * masks[r], 'nt')
                dcb = dcb + dm * lm
                dseg = dm * mr
                dcol = jnp.sum(dseg, axis=1, keepdims=True) - jnp.sum(dseg.T, axis=1, keepdims=True)
                dacs_e = dacs_e + dcol * masks[r]
                dxdt = dxdt + _dot(mr.T, dyr16)
            dcm = dcm + _dot(dcb, b16)
            dbm = dbm + _dot(dcb.T, c16)
            dacs_e = dacs_e + jnp.where(_iota((n, 1), 0) == n - 1, dlast_e, 0.0)
            triu = (_iota((n, n), 0) <= _iota((n, n), 1)).astype(F32)
            ddta_e = _dot_exact(triu, dacs_e)
            ddt_e = ddta_e * a_e + _segsum(dxdt * xv)
            dx_ref[rows] = (dxdt * dt_e + dyv * d_e) * _dsilu(ux_ref[rows])
            db_ref[rows] = dbm * _dsilu(ub_ref[rows])
            dc_ref[rows] = dcm * _dsilu(uc_ref[rows])
            draw_e = ddt_e * _sigmoid(raw_e + prw[0:1, :])
            draw_t = draw_e.T
            ddt_ref[0, :, rows] = jnp.concatenate([draw_t[r * SSD_HEAD_DIM:r * SSD_HEAD_DIM + 1] for r in range(4)], axis=0)
            dbias = jnp.sum(draw_e, axis=0, keepdims=True)
            dalog = jnp.sum(ddta_e * dt_e, axis=0, keepdims=True) * a_e
            dd = _segsum(jnp.sum(dyv * xv, axis=0, keepdims=True))
            row3 = _iota((3, 1), 0)
            dp_ref[0] += (jnp.where(row3 == 0, dbias, 0.0) + jnp.where(row3 == 1, dalog, 0.0)
                          + jnp.where(row3 == 2, dd, 0.0))
            ds = ds_prev
        ds_scr[...] = ds


    x, bm, cm, dtcs, dtrs, prs, pcs, st, ch = _ssd_specs(t, True)
    yblk = pl.BlockSpec((SSD_ROWS, 256), lambda g, c: (ch(c), g))
    nblk = pl.BlockSpec((SSD_ROWS, SSD_STATE), lambda g, c: (ch(c), g))
    return pl.pallas_call(
        body, name="ssd_bwd", grid=(SSD_GROUPS, t // SSD_ROWS),
        in_specs=[x, bm, cm, x, bm, cm, dtcs, dtrs, prs, pcs, pcs, st, yblk],
        out_specs=[yblk, nblk, nblk, dtrs, prs],
        out_shape=[jax.ShapeDtypeStruct((t, SSD_INNER), F32), jax.ShapeDtypeStruct((t, BC_WIDTH), F32),
                   jax.ShapeDtypeStruct((t, BC_WIDTH), F32), jax.ShapeDtypeStruct((SSD_GROUPS, 4, t), F32),
                   jax.ShapeDtypeStruct((SSD_GROUPS, 3, 256), F32)],
        scratch_shapes=[pltpu.VMEM((SSD_STATE, 256), F32)],
        compiler_params=_cp(("parallel", "arbitrary")))(xbc, xbc, xbc, pre, pre, pre, dtc, dtr, prow, bcol, acol,
                                                         states, dy)


GROUP_W = SSD_INNER // SSD_GROUPS


def _mix_specs(tb):
    row = pl.BlockSpec((tb, 2048), lambda i: (i, 0))
    zlo = pl.BlockSpec((tb, 1024), lambda i: (i, O_Z // 1024))
    zhi = pl.BlockSpec((tb, 1024), lambda i: (i, O_Z // 1024 + 1))
    vec = pl.BlockSpec((1, 2048), lambda i: (0, 0))
    return row, zlo, zhi, vec


def _mix_fwd(attn, y, proj, g_attn, g_ssd):
    t = attn.shape[0]
    tb = _rows(t, 256)

    def body(a_ref, y_ref, zlo_ref, zhi_ref, ga_ref, gs_ref, o_ref):
        av = a_ref[...]
        r = lax.rsqrt(jnp.mean(av * av, axis=-1, keepdims=True) + EPS)
        o_ref[:, :ATTN_WIDTH] = (av * r * ga_ref[...]).astype(BF16)
        for g in range(SSD_GROUPS):
            lo, hi = g * GROUP_W, (g + 1) * GROUP_W
            zref = zlo_ref if g < 4 else zhi_ref
            z = zref[:, lo % 1024:lo % 1024 + GROUP_W]
            yg = y_ref[:, lo:hi] * (z * _sigmoid(z))
            rg = lax.rsqrt(jnp.mean(yg * yg, axis=-1, keepdims=True) + EPS)
            o_ref[:, ATTN_WIDTH + lo:ATTN_WIDTH + hi] = (yg * rg * gs_ref[:, lo:hi]).astype(BF16)

    row, zlo, zhi, vec = _mix_specs(tb)
    return pl.pallas_call(
        body, name="mix_fwd", grid=(t // tb,), in_specs=[row, row, zlo, zhi, vec, vec],
        out_specs=pl.BlockSpec((tb, 4096), lambda i: (i, 0)), out_shape=jax.ShapeDtypeStruct((t, 4096), BF16),
        compiler_params=_cp(("parallel",)))(attn, y, proj, proj, g_attn, g_ssd)


def _mix_bwd(dmix, attn, y, proj, g_attn, g_ssd):
    t = attn.shape[0]
    tb = _rows(t, 256)

    def body(dm_ref, a_ref, y_ref, zlo_ref, zhi_ref, ga_ref, gs_ref, da_ref, dy_ref, dz_ref, dga_ref, dgs_ref):
        i = pl.program_id(0)
        av = a_ref[...]
        dn = dm_ref[:, :ATTN_WIDTH].astype(F32)
        r = lax.rsqrt(jnp.mean(av * av, axis=-1, keepdims=True) + EPS)
        u = dn * ga_ref[...]
        da_ref[...] = r * u - av * (r * r * r * jnp.mean(u * av, axis=-1, keepdims=True))
        dga = jnp.sum(dn * av * r, axis=0, keepdims=True)

        @pl.when(i == 0)
        def _():
            dga_ref[...] = dga

        @pl.when(i > 0)
        def _():
            dga_ref[...] += dga

        for g in range(SSD_GROUPS):
            lo, hi = g * GROUP_W, (g + 1) * GROUP_W
            zref = zlo_ref if g < 4 else zhi_ref
            z = zref[:, lo % 1024:lo % 1024 + GROUP_W]
            yv = y_ref[:, lo:hi]
            sg = _sigmoid(z)
            sz = z * sg
            yg = yv * sz
            rg = lax.rsqrt(jnp.mean(yg * yg, axis=-1, keepdims=True) + EPS)
            do = dm_ref[:, ATTN_WIDTH + lo:ATTN_WIDTH + hi].astype(F32)
            ug = do * gs_ref[:, lo:hi]
            dyg = rg * ug - yg * (rg * rg * rg * jnp.mean(ug * yg, axis=-1, keepdims=True))
            dy_ref[:, lo:hi] = dyg * sz
            dz_ref[:, lo:hi] = (dyg * yv * (sg * (1.0 + z * (1.0 - sg)))).astype(BF16)
            dgs = jnp.sum(do * yg * rg, axis=0, keepdims=True)

            @pl.when(i == 0)
            def _():
                dgs_ref[:, lo:hi] = dgs

            @pl.when(i > 0)
            def _():
                dgs_ref[:, lo:hi] += dgs

    row, zlo, zhi, vec = _mix_specs(tb)
    return pl.pallas_call(
        body, name="mix_bwd", grid=(t // tb,),
        in_specs=[pl.BlockSpec((tb, 4096), lambda i: (i, 0)), row, row, zlo, zhi, vec, vec],
        out_specs=[row, row, row, vec, vec],
        out_shape=[jax.ShapeDtypeStruct((t, 2048), F32), jax.ShapeDtypeStruct((t, 2048), F32),
                   jax.ShapeDtypeStruct((t, 2048), BF16), jax.ShapeDtypeStruct((1, 2048), F32),
                   jax.ShapeDtypeStruct((1, 2048), F32)],
        compiler_params=_cp(("arbitrary",)))(dmix, attn, y, proj, proj, g_attn, g_ssd)


def _adamw(w, g, m, v, name):
    r, c = w.shape
    tb = _rows(r, 256)
    c1 = 1.0 - ADAM_B1 ** ADAM_STEP
    c2 = 1.0 - ADAM_B2 ** ADAM_STEP

    def body(w_ref, g_ref, m_ref, v_ref, d_ref, m2_ref, v2_ref):
        gv = g_ref[...]
        m2 = ADAM_B1 * m_ref[...] + (1.0 - ADAM_B1) * gv
        v2 = ADAM_B2 * v_ref[...] + (1.0 - ADAM_B2) * (gv * gv)
        d_ref[...] = -ADAM_LR * ((m2 / c1) / (jnp.sqrt(v2 / c2) + ADAM_EPS) + ADAM_WD * w_ref[...])
        m2_ref[...] = m2
        v2_ref[...] = v2

    blk = pl.BlockSpec((tb, c), lambda i: (i, 0))
    shp = jax.ShapeDtypeStruct((r, c), F32)
    return pl.pallas_call(body, name=name, grid=(r // tb,), in_specs=[blk] * 4, out_specs=[blk] * 3,
                          out_shape=[shp] * 3, compiler_params=_cp(("parallel",)))(w, g, m, v)


def _adamw_halves(w, mine, theirs, m, v, pos, name, cols=False):
    r, c = w.shape
    h = r if cols else r // 2
    tb = _rows(h, 128)
    nh = h // tb
    c1 = 1.0 - ADAM_B1 ** ADAM_STEP
    c2 = 1.0 - ADAM_B2 ** ADAM_STEP

    def body(pos_ref, w_ref, a_ref, b_ref, m_ref, v_ref, g_ref, d_ref, m2_ref, v2_ref):
        which = pl.program_id(1) if cols else pl.program_id(0) // nh
        gv = jnp.where(which == pos_ref[0], a_ref[...], b_ref[...])
        m2 = ADAM_B1 * m_ref[...] + (1.0 - ADAM_B1) * gv
        v2 = ADAM_B2 * v_ref[...] + (1.0 - ADAM_B2) * (gv * gv)
        g_ref[...] = gv
        d_ref[...] = -ADAM_LR * ((m2 / c1) / (jnp.sqrt(v2 / c2) + ADAM_EPS) + ADAM_WD * w_ref[...])
        m2_ref[...] = m2
        v2_ref[...] = v2

    if cols:
        full = pl.BlockSpec((tb, c // 2), lambda i, j, pref: (i, j))
        mine_spec = theirs_spec = pl.BlockSpec((tb, c // 2), lambda i, j, pref: (i, 0))
        grid = (nh, 2)
    else:
        full = pl.BlockSpec((tb, c), lambda i, pref: (i, 0))
        mine_spec = pl.BlockSpec((tb, c), lambda i, pref: (jnp.where(i // nh == pref[0], i % nh,
                                                                     jnp.where(pref[0] == 0, nh - 1, 0)), 0))
        theirs_spec = pl.BlockSpec((tb, c), lambda i, pref: (jnp.where(i // nh != pref[0], i % nh,
                                                                       jnp.where(pref[0] == 0, 0, nh - 1)), 0))
        grid = (r // tb,)
    shp = jax.ShapeDtypeStruct((r, c), F32)
    grid_spec = pltpu.PrefetchScalarGridSpec(num_scalar_prefetch=1, grid=grid,
                                             in_specs=[full, mine_spec, theirs_spec, full, full],
                                             out_specs=[full] * 4)
    return pl.pallas_call(body, name=name, grid_spec=grid_spec, out_shape=[shp] * 4,
                          compiler_params=_cp(("parallel",) * len(grid)))(pos, w, mine, theirs, m, v)


def _sum_own_half(g4, recv, pos, name, cols=False):
    _, r, c = g4.shape
    h, c = (r, c // 2) if cols else (r // 2, c)
    tb = _rows(h, 128)
    nh = h // tb

    def slot(j, pref):
        return (pref[1] + 1 + j) % N_CHIPS

    if cols:
        own = lambda j, i, pref: (slot(j, pref), i, pref[0])
    else:
        own = lambda j, i, pref: (slot(j, pref), pref[0] * nh + i, 0)
    same = lambda j, i, pref: (slot(j, pref), i, 0)

    def body(pos_ref, a_ref, b_ref, o_ref):
        o_ref[...] = (a_ref[...] + b_ref[...]).astype(BF16)

    grid_spec = pltpu.PrefetchScalarGridSpec(
        num_scalar_prefetch=1, grid=(N_CHIPS - 1, nh),
        in_specs=[pl.BlockSpec((1, tb, c), own), pl.BlockSpec((1, tb, c), same)],
        out_specs=pl.BlockSpec((1, tb, c), same))
    return pl.pallas_call(body, name=name, grid_spec=grid_spec,
                          out_shape=jax.ShapeDtypeStruct((N_CHIPS, h, c), BF16),
                          compiler_params=_cp(("parallel", "parallel")))(pos, g4, recv)


def _sum_chips(g4, recv, parts, pos, name, cols=False):
    _, r, c = g4.shape
    h, c = (r, c // 2) if cols else (r // 2, c)
    tb = _rows(h, 128)
    nh = h // tb
    own = (lambda i, pref: (pref[1], i, pref[0])) if cols else (lambda i, pref: (pref[1], pref[0] * nh + i, 0))

    def body(pos_ref, a_ref, b_ref, p_ref, o_ref):
        own = a_ref[0] + b_ref[0]
        o_ref[...] = ((own + p_ref[0].astype(F32)) + p_ref[1].astype(F32)) + p_ref[2].astype(F32)

    grid_spec = pltpu.PrefetchScalarGridSpec(
        num_scalar_prefetch=1, grid=(nh,),
        in_specs=[pl.BlockSpec((1, tb, c), own),
                  pl.BlockSpec((1, tb, c), lambda i, pref: (pref[1], i, 0)),
                  pl.BlockSpec((3, tb, c), lambda i, pref: (0, i, 0))],
        out_specs=pl.BlockSpec((tb, c), lambda i, pref: (i, 0)))
    return pl.pallas_call(body, name=name, grid_spec=grid_spec, out_shape=jax.ShapeDtypeStruct((h, c), F32),
                          compiler_params=_cp(("parallel",)))(pos, g4, recv, parts)


def _me():
    return lax.axis_index("x"), lax.axis_index("y"), lax.axis_index("c")


def _flip(v, bit):
    return (1 - v) if bit else v


CHIP_FLIPS = [(1, 0), (0, 1), (1, 1)]


def _forward_halves(gathered):
    def body(g_ref, o_ref, token, send_sems, recv_sems):
        x, y, c = _me()
        h = g_ref.shape[2] // 2
        cps = []
        for k, (fx, fy) in enumerate(CHIP_FLIPS):
            peer_chip = 2 * _flip(x, fx) + _flip(y, fy)
            mine = o_ref.at[peer_chip, :, pl.ds(c * h, h)]
            cp = pltpu.make_async_remote_copy(src_ref=mine, dst_ref=mine, send_sem=send_sems.at[k],
                                              recv_sem=recv_sems.at[k], device_id=(x, y, 1 - c), device_id_type=MESH)
            cp.start()
            cps.append(cp)
        for k, (fx, fy) in enumerate(CHIP_FLIPS):
            peer_chip = 2 * _flip(x, fx) + _flip(y, fy)
            theirs = o_ref.at[peer_chip, :, pl.ds((1 - c) * h, h)]
            pltpu.make_async_remote_copy(src_ref=theirs, dst_ref=theirs, send_sem=send_sems.at[k],
                                         recv_sem=recv_sems.at[k], device_id=(x, y, 1 - c),
                                         device_id_type=MESH).wait_recv()
        for cp in cps:
            cp.wait_send()
        token[...] = jnp.zeros_like(token)

    return pl.pallas_call(
        body, name="gather_forward_w_in", in_specs=[HBM_SPEC],
        out_specs=[HBM_SPEC, pl.BlockSpec(memory_space=pltpu.VMEM)],
        out_shape=[jax.ShapeDtypeStruct(gathered.shape, gathered.dtype), TOKEN],
        scratch_shapes=[pltpu.SemaphoreType.DMA((3,)), pltpu.SemaphoreType.DMA((3,))],
        input_output_aliases={0: 0},
        compiler_params=pltpu.CompilerParams(has_side_effects=True))(gathered)


def _share_halves(ghs, name):
    n = len(ghs)

    def body(*refs):
        ins, outs = refs[:n], refs[n:2 * n]
        send_sems, recv_sems = refs[2 * n:]
        x, y, c = _me()
        cps = []
        for t in range(n):
            cp = pltpu.make_async_remote_copy(
                src_ref=ins[t], dst_ref=outs[t], send_sem=send_sems.at[t], recv_sem=recv_sems.at[t],
                device_id=(x, y, 1 - c), device_id_type=MESH)
            cp.start()
            cps.append(cp)
        for cp in cps:
            cp.wait()

    return pl.pallas_call(
        body, name=name, in_specs=[HBM_SPEC] * n, out_specs=[HBM_SPEC] * n,
        out_shape=[jax.ShapeDtypeStruct(g.shape, g.dtype) for g in ghs],
        scratch_shapes=[pltpu.SemaphoreType.DMA((n,)), pltpu.SemaphoreType.DMA((n,))],
        compiler_params=pltpu.CompilerParams(has_side_effects=True))(*ghs)


SEM_SPEC = pl.BlockSpec(memory_space=pltpu.SEMAPHORE)
ANY_SPEC = pl.BlockSpec(memory_space=pl.ANY)
DATAFLOW = pltpu.SideEffectType.DATAFLOW_SIDE_EFFECTING


def _in_hbm(a):
    return pltpu.with_memory_space_constraint(a, pltpu.HBM)


def _push_start(srcs, land_shapes, route, peers, name):
    n, npeer = len(srcs), len(peers)
    lands = [lax.empty(shp, s.dtype) for shp, s in zip(land_shapes, srcs)]

    def body(*refs):
        ins, lnd = refs[:n], refs[n:2 * n]
        send_sems, recv_sems = refs[2 * n], refs[2 * n + 1]
        token = refs[-1]
        x, y, c = _me()
        for t in range(n):
            for k, (fx, fy, fc) in enumerate(peers):
                src, dst = route(ins[t], lnd[t], k, x, y, c)
                pltpu.make_async_remote_copy(
                    src_ref=src, dst_ref=dst, send_sem=send_sems.at[npeer * t + k],
                    recv_sem=recv_sems.at[npeer * t + k],
                    device_id=(_flip(x, fx), _flip(y, fy), _flip(c, fc)), device_id_type=MESH).start()
        token[...] = jnp.zeros_like(token)

    bufs = [_in_hbm(a) for a in list(srcs) + lands]
    outs = pl.pallas_call(
        body, name=name,
        out_shape=(pltpu.SemaphoreType.DMA((npeer * n,)), pltpu.SemaphoreType.DMA((npeer * n,)),
                   *[pltpu.HBM(b.shape, b.dtype) for b in bufs], TOKEN),
        in_specs=[HBM_SPEC] * (2 * n),
        out_specs=(SEM_SPEC, SEM_SPEC, *[HBM_SPEC] * (2 * n), pl.BlockSpec(memory_space=pltpu.VMEM)),
        input_output_aliases={i: 2 + i for i in range(2 * n)},
        compiler_params=pltpu.CompilerParams(has_side_effects=DATAFLOW))(*bufs)
    return outs[0], outs[1], list(outs[2:2 + n]), list(outs[2 + n:2 + 2 * n]), outs[-1]


def _push_wait(send_sems, recv_sems, srcs, lands, after, route, peers, name):
    n, npeer = len(srcs), len(peers)

    def body(*refs):
        ins, lnd = refs[:n], refs[n:2 * n]
        ssem, rsem = refs[2 * n], refs[2 * n + 1]
        x, y, c = _me()
        for t in range(n):
            for k, (fx, fy, fc) in enumerate(peers):
                src, dst = route(ins[t], lnd[t], k, x, y, c)
                cp = pltpu.make_async_remote_copy(
                    src_ref=src, dst_ref=dst, send_sem=ssem.at[npeer * t + k], recv_sem=rsem.at[npeer * t + k],
                    device_id=(_flip(x, fx), _flip(y, fy), _flip(c, fc)), device_id_type=MESH)
                cp.wait_send()
                cp.wait_recv()

    bufs = list(srcs) + list(lands)
    outs = pl.pallas_call(
        body, name=name, out_shape=tuple(pltpu.HBM(b.shape, b.dtype) for b in bufs),
        in_specs=[HBM_SPEC] * (2 * n) + [SEM_SPEC, SEM_SPEC, ANY_SPEC], out_specs=tuple([HBM_SPEC] * (2 * n)),
        input_output_aliases={i: i for i in range(2 * n)},
        compiler_params=pltpu.CompilerParams(has_side_effects=DATAFLOW))(*bufs, send_sems, recv_sems, after)
    return list(outs[:n]), list(outs[n:])


OTHER_CHIPS = [(fx, fy, 0) for fx, fy in CHIP_FLIPS]
SIBLING = [(0, 0, 1)]


def _route_gather(src, land, k, x, y, c):
    return src, land.at[2 * x + y]


def _route_gather_half(src, land, k, x, y, c):
    h = src.shape[1] // 2
    return src.at[:, pl.ds(c * h, h)], land.at[2 * x + y, :, pl.ds(c * h, h)]


def _route_gather_half_wait(src, land, k, x, y, c):
    fx, fy = CHIP_FLIPS[k]
    h = src.shape[1] // 2
    return src.at[:, pl.ds(c * h, h)], land.at[2 * _flip(x, fx) + _flip(y, fy), :, pl.ds(c * h, h)]


def _route_gather_wait(src, land, k, x, y, c):
    fx, fy = CHIP_FLIPS[k]
    return src, land.at[2 * _flip(x, fx) + _flip(y, fy)]


def _route_scatter(src, land, k, x, y, c):
    fx, fy = CHIP_FLIPS[k]
    return src.at[2 * _flip(x, fx) + _flip(y, fy)], land.at[k]


def _route_exchange(src, land, k, x, y, c):
    h = land.shape[1]
    return src.at[:, pl.ds((1 - c) * h, h)], land


def _route_exchange_cols(src, land, k, x, y, c):
    h = land.shape[2]
    return src.at[:, :, pl.ds((1 - c) * h, h)], land


def _allreduce_small(v):
    r = v.shape[0]

    def body(v_ref, o_ref, token, buf, send_sems, recv_sems):
        x, y, c = _me()
        me = 4 * x + 2 * y + c
        buf[0] = v_ref[...]
        cps = []
        for k in range(1, 8):
            kx, ky, kc = (k >> 2) & 1, (k >> 1) & 1, k & 1
            cp = pltpu.make_async_remote_copy(
                src_ref=v_ref, dst_ref=buf.at[k], send_sem=send_sems.at[k - 1], recv_sem=recv_sems.at[k - 1],
                device_id=(_flip(x, kx), _flip(y, ky), _flip(c, kc)), device_id_type=MESH)
            cp.start()
            cps.append(cp)
        for cp in cps:
            cp.wait()
        acc = buf[me]
        for d in range(1, 8):
            acc = acc + buf[jnp.bitwise_xor(me, d)]
        o_ref[...] = acc
        token[...] = jnp.zeros_like(token)

    vm = pl.BlockSpec(memory_space=pltpu.VMEM)
    return pl.pallas_call(
        body, name="allreduce_small", in_specs=[vm], out_specs=[vm, vm],
        out_shape=[jax.ShapeDtypeStruct(v.shape, F32), TOKEN],
        scratch_shapes=[pltpu.VMEM((8, r, LANES), F32), pltpu.SemaphoreType.DMA((7,)),
                        pltpu.SemaphoreType.DMA((7,))],
        compiler_params=pltpu.CompilerParams(has_side_effects=True, vmem_limit_bytes=VMEM_LIMIT))(v)


def _grad_exchange_start(g4, tag, cols=False):
    land = (N_CHIPS, g4.shape[1], g4.shape[2] // 2) if cols else (N_CHIPS, g4.shape[1] // 2, g4.shape[2])
    route = _route_exchange_cols if cols else _route_exchange
    send_sems, recv_sems, srcs, lands, token = _push_start(
        [g4], [land], route, SIBLING, name="grad_exchange_start_" + tag)
    return (send_sems, recv_sems, srcs, lands, tag, cols), token


def _grad_scatter_start(state, pos, after):
    send_sems, recv_sems, srcs, lands, tag, cols = state
    route = _route_exchange_cols if cols else _route_exchange
    (g4,), (recv,) = _push_wait(send_sems, recv_sems, srcs, lands, after, route, SIBLING,
                                name="grad_exchange_wait_" + tag)
    return _grad_pair_scatter(g4, recv, pos, tag, cols)


def _grad_pair_scatter(g4, recv, pos, tag, cols=False):
    p16 = _sum_own_half(g4, recv, pos, name="grad_sum_pair_" + tag, cols=cols)
    send_sems, recv_sems, srcs, lands, token = _push_start(
        [p16], [(3,) + p16.shape[1:]], _route_scatter, OTHER_CHIPS, name="grad_scatter_start_" + tag)
    return (g4, recv, send_sems, recv_sems, srcs, lands, tag, cols), token


def _grad_reduce_finish(state, pos, after):
    g4, recv, send_sems, recv_sems, srcs, lands, tag, cols = state
    parts = _push_wait(send_sems, recv_sems, srcs, lands, after, _route_scatter, OTHER_CHIPS,
                       name="grad_scatter_wait_" + tag)[1][0]
    mine = _sum_chips(g4, recv, parts, pos, name="grad_sum_chips_" + tag, cols=cols)
    return mine, _share_halves([mine], name="grad_share_halves_" + tag)[0]


def _local_step(x, tgt, p, hooks):
    t = x.shape[0]
    tables = _rope_tables(t)
    sinks = p['sinks'].reshape(N_Q_HEADS)

    def told(name, value):
        return tuple(hooks.grad_ready(name, value))

    xn = _rmsnorm_fwd(x, p['norm_mix'], "norm_mix_fwd", deps=hooks.first_deps)
    w_in_t, w_in_dt, in_deps = hooks.weight_in(xn)
    proj = _matmul(xn, w_in_t, mode='nt', name="in_proj", n_limit=MAIN_WIDTH, deps=in_deps)
    dt_raw = _matmul(xn, w_in_dt, mode='nt', name="in_proj_dt")[:, :SSD_HEADS]
    attn = _attn_fwd(proj, sinks, tables)
    conv_b = p['ssd_conv_b']
    xbc, xbc_pre = _conv_silu_fwd(proj, p['ssd_conv_w'], conv_b, col0=O_XBC, width=CONV_CH, name="ssd_conv_fwd")
    sp = _ssd_params(dt_raw, p['dt_bias'].reshape(-1), p['a_log'].reshape(-1), p['ssd_d'].reshape(-1))
    y, states = _ssd_fwd(xbc, sp)
    mix = _mix_fwd(attn, y, proj, p['attn_out_norm'], p['ssd_norm'])
    w_out = hooks.weight('w_out', mix)
    h1 = _matmul(mix, w_out, mode='nn', name="out_proj", add=x)
    hn = _rmsnorm_fwd(h1, p['norm_ffn'], "norm_ffn_fwd")
    w_up = hooks.weight('w_up', hn)
    u0 = _matmul(hn, w_up, mode='nn', name="ffn_up", b_owner=True, tn=1408)
    a, u = _ffn_act_fwd(u0, p['ffn_conv_w'], p['ffn_conv_b'])
    w_down = hooks.weight('w_down', a)
    h2 = _matmul(a, w_down, mode='nn', name="ffn_down", add=h1, tk=2816)
    loss, dh2, dh2_16, g_norm_final = _final_loss(h2, p['norm_final'].reshape(1, D_MODEL), tgt)

    g = {}
    da = _matmul(dh2_16, w_down, mode='nt', name="ffn_down_dx", out_dtype=BF16, tn=1408)
    g['w_down'] = _matmul(a, dh2_16, mode='tn', name="ffn_down_dw", tm=1408)
    dep = told('w_down', g['w_down'])
    du0, dcw, dcb = _ffn_act_bwd(u0, u, p['ffn_conv_w'], da)
    g['ffn_conv_w'] = dcw.transpose(1, 0, 2).reshape(FFN_CONV, 2 * D_FF)
    g['ffn_conv_b'] = dcb.transpose(1, 0, 2).reshape(1, 2 * D_FF)
    g['w_up'] = _matmul(hn, du0, mode='tn', name="ffn_up_dw", deps=dep, b_halves=True, owner_major=True,
                        tn=1408)
    dep = told('w_up', g['w_up'])
    dhn = _matmul(du0, w_up, mode='nt', name="ffn_up_dx", out_dtype=BF16, deps=dep, a_halves=True,
                  b_owner=True, tk=2816)
    dh1, dh1_16, g['norm_ffn'] = _rmsnorm_bwd(h1, p['norm_ffn'], dhn, dh2, "norm_ffn_bwd")

    g['w_out'] = _matmul(mix, dh1_16, mode='tn', name="out_proj_dw")
    dep = told('w_out', g['w_out'])
    dmix = _matmul(dh1_16, w_out, mode='nt', name="out_proj_dx", out_dtype=BF16, deps=dep)
    dattn, dy, dz, g['attn_out_norm'], g['ssd_norm'] = _mix_bwd(dmix, attn, y, proj, p['attn_out_norm'],
                                                                p['ssd_norm'])
    dq, dk, dv, dsink = _attn_bwd(proj, sinks, tables, dattn)
    g['sinks'] = dsink[:, :, 0].reshape(1, N_Q_HEADS)
    dxs, dbm, dcm, ddt8, dpar = _ssd_bwd(xbc, xbc_pre, sp, states, dy)
    dpar = dpar[:, :, ::SSD_HEAD_DIM]
    g['dt_bias'] = dpar[:, 0, :].reshape(1, SSD_HEADS)
    g['a_log'] = dpar[:, 1, :].reshape(1, SSD_HEADS)
    g['ssd_d'] = dpar[:, 2, :].reshape(1, SSD_HEADS)
    dxbc, g['ssd_conv_w'], g['ssd_conv_b'] = _ssd_conv_bwd(proj, p['ssd_conv_w'], dxs, dbm, dcm, col0=O_XBC,
                                                           name="ssd_conv_bwd")
    dproj = jnp.concatenate([dq, dk, dv, dz, dxbc], axis=1)
    ddt = ddt8.transpose(2, 0, 1).reshape(t, SSD_HEADS)
    ddt_pad = jnp.pad(ddt, ((0, 0), (0, LANES - SSD_HEADS))).astype(BF16)
    g['w_in'] = (_matmul(dproj, xn, mode='tn', name="in_proj_dw", m_rows=IN_PROJ_WIDTH),
                 _matmul(ddt_pad, xn, mode='tn', name="in_proj_dt_dw"))
    dep = told('w_in', g['w_in'])
    dxn_dt = _matmul(ddt_pad, w_in_dt, mode='nn', name="in_proj_dt_dx", deps=dep)
    dxn = _matmul(dproj, w_in_t, mode='nn', name="in_proj_dx", out_dtype=BF16, add=dxn_dt, k_limit=MAIN_WIDTH,
                  tk=2304)
    dx, _, g['norm_mix'] = _rmsnorm_bwd(x, p['norm_mix'], dxn, dh1, "norm_mix_bwd")
    g['norm_final'] = g_norm_final
    return loss, dx, g


def _pack(arrs):
    flat = jnp.concatenate([a.reshape(-1) for a in arrs])
    n = flat.shape[0]
    rows = -(-n // LANES)
    rows = -(-rows // 8) * 8
    return jnp.pad(flat, (0, rows * LANES - n)).reshape(rows, LANES)


def _unpack(packed, shapes):
    flat = packed.reshape(-1)
    out, off = [], 0
    for s in shapes:
        n = 1
        for d in s:
            n *= d
        out.append(flat[off:off + n].reshape(s))
        off += n
    return out


class _StepHooks:
    def __init__(self, first_deps, weight_in, weight, grad_ready):
        self.first_deps = first_deps
        self.weight_in = weight_in
        self.weight = weight
        self.grad_ready = grad_ready


def kernel(x, norm_mix, w_in, sinks, attn_out_norm, ssd_conv_w, ssd_conv_b, dt_bias, a_log, ssd_d, ssd_norm, w_out, norm_ffn, w_up, ffn_conv_w, ffn_conv_b, w_down, norm_final, loss_target, m_norm_mix, m_w_in, m_sinks, m_attn_out_norm, m_ssd_conv_w, m_ssd_conv_b, m_dt_bias, m_a_log, m_ssd_d, m_ssd_norm, m_w_out, m_norm_ffn, m_w_up, m_ffn_conv_w, m_ffn_conv_b, m_w_down, m_norm_final, v_norm_mix, v_w_in, v_sinks, v_attn_out_norm, v_ssd_conv_w, v_ssd_conv_b, v_dt_bias, v_a_log, v_ssd_d, v_ssd_norm, v_w_out, v_norm_ffn, v_w_up, v_ffn_conv_w, v_ffn_conv_b, v_w_down, v_norm_final):
    args = dict(locals())
    w = {n: args[n] for n in WEIGHTS}
    m = {n: args['m_' + n] for n in WEIGHTS}
    v = {n: args['v_' + n] for n in WEIGHTS}
    xi, yi, ci = _me()
    chip = 2 * xi + yi
    pos = jnp.stack([ci, chip]).astype(jnp.int32)

    def place(shard, full_cols):
        z = jnp.zeros((shard.shape[0], full_cols), F32)
        return lax.dynamic_update_slice(z, shard * 0.5, (0, chip * shard.shape[1]))

    conv_pack = _pack([place(ssd_conv_w[0], CONV_CH), place(ffn_conv_w[0], 2 * D_FF)])
    conv_full, conv_done = _allreduce_small(conv_pack)
    ssd_conv_w_full, ffn_conv_w_full = _unpack(conv_full, [(SSD_CONV, CONV_CH), (FFN_CONV, 2 * D_FF)])

    w_in_t, m_in_t, v_in_t = (jnp.transpose(a[0]) for a in (w_in, m_w_in, v_w_in))
    in_shard = (w_in_t + conv_done[:1, :1]).astype(BF16)
    in_gather = _push_start([in_shard], [(N_CHIPS,) + in_shard.shape], _route_gather_half, OTHER_CHIPS,
                            name="gather_start_w_in")
    gathers = {}
    order = in_gather[4][:1, :1]
    for n, shard in (('w_out', w_out[0]), ('w_up', w_up[0]), ('w_down', w_down[0])):
        shard = (shard + order).astype(BF16)
        gathers[n] = _push_start([shard], [(N_CHIPS,) + shard.shape], _route_gather, OTHER_CHIPS,
                                 name="gather_start_" + n)
        order = gathers[n][4][:1, :1]

    def weight_in(after):
        send_sems, recv_sems, srcs, lands, _ = in_gather
        (own,), (got,) = _push_wait(send_sems, recv_sems, srcs, lands, after, _route_gather_half_wait, OTHER_CHIPS,
                                    name="gather_wait_w_in")
        got, _ = _forward_halves(got)
        full_in_t = lax.dynamic_update_slice(got, own[None], (chip, 0, 0)).reshape(IN_PROJ_WIDTH, D_MODEL)
        w_in_dt = jnp.pad(full_in_t[MAIN_WIDTH:], ((0, LANES - SSD_HEADS), (0, 0)))
        return full_in_t, w_in_dt, ()

    def weight(name, after):
        send_sems, recv_sems, srcs, lands, _ = gathers[name]
        (own,), (got,) = _push_wait(send_sems, recv_sems, srcs, lands, after, _route_gather_wait, OTHER_CHIPS,
                                    name="gather_wait_" + name)
        whole = lax.dynamic_update_slice(got, own[None], (chip, 0, 0))
        return whole if name == 'w_up' else whole.reshape(-1, D_MODEL)

    reductions, exchanging = {}, {}

    def flush(after):
        tokens = []
        for prev in list(exchanging):
            reductions[prev], token = _grad_scatter_start(exchanging.pop(prev), pos, after)
            tokens.append(token)
        return tokens

    def grad_ready(name, value):
        if name == 'w_in':
            main, dtp = value
            value = lax.dynamic_update_slice(main, dtp[:SSD_HEADS], (MAIN_WIDTH, 0))
        g4 = value if value.ndim == 3 else value.reshape(N_CHIPS, -1, value.shape[1])
        tokens = flush(g4)
        exchanging[name], token = _grad_exchange_start(g4, name, cols=(name == 'w_in'))
        return tokens + [token]

    small = {
        'norm_mix': norm_mix, 'sinks': sinks, 'attn_out_norm': attn_out_norm, 'ssd_conv_w': ssd_conv_w_full,
        'ssd_conv_b': ssd_conv_b, 'dt_bias': dt_bias, 'a_log': a_log, 'ssd_d': ssd_d, 'ssd_norm': ssd_norm,
        'norm_ffn': norm_ffn, 'ffn_conv_w': ffn_conv_w_full, 'ffn_conv_b': ffn_conv_b, 'norm_final': norm_final,
    }
    loss, dx, g = _local_step(x[0], loss_target[0], small,
                              _StepHooks((gathers['w_down'][4],), weight_in, weight, grad_ready))

    small_names = [n for n in WEIGHTS if n not in BIG]
    small_g = [loss[:, :1]] + [g[n] for n in small_names]
    small_shapes = [(1, 1)] + [tuple(a.shape) for a in small_g[1:]]
    reduced, _ = _allreduce_small(_pack(small_g))
    started = flush(reduced)[-1]
    red = _unpack(reduced, small_shapes)
    loss_out = red[0].reshape(())
    gsm = dict(zip(small_names, red[1:]))
    gsm['ssd_conv_w'] = lax.dynamic_slice(gsm['ssd_conv_w'], (0, chip * ssd_conv_w.shape[2]),
                                          (SSD_CONV, ssd_conv_w.shape[2]))
    gsm['ffn_conv_w'] = lax.dynamic_slice(gsm['ffn_conv_w'], (0, chip * ffn_conv_w.shape[2]),
                                          (FFN_CONV, ffn_conv_w.shape[2]))

    grads, deltas, new_m, new_v = {}, {}, {}, {}
    after = started
    for n in ('w_down', 'w_up', 'w_out', 'w_in'):
        mine, theirs = _grad_reduce_finish(reductions[n], pos, after)
        if n == 'w_in':
            outs = _adamw_halves(w_in_t, mine, theirs, m_in_t, v_in_t, pos, name="adamw_" + n, cols=True)
            outs = [jnp.transpose(o) for o in outs]
        else:
            outs = _adamw_halves(w[n][0], mine, theirs, m[n][0], v[n][0], pos, name="adamw_" + n)
        after = outs[1]
        grads[n], deltas[n], new_m[n], new_v[n] = [o[None] for o in outs]
    shapes = [tuple(w[n].shape) for n in small_names]
    gp = _pack([gsm[n] for n in small_names])
    d, m2, v2 = _adamw(_pack([w[n] for n in small_names]), gp, _pack([m[n] for n in small_names]),
                       _pack([v[n] for n in small_names]), name="adamw_small")
    for n, gg, dd, mm, vv in zip(small_names, _unpack(gp, shapes), _unpack(d, shapes), _unpack(m2, shapes),
                                 _unpack(v2, shapes)):
        grads[n], deltas[n], new_m[n], new_v[n] = gg, dd, mm, vv

    return (loss_out, dx[None], *[grads[n] for n in WEIGHTS], *[deltas[n] for n in WEIGHTS],
            *[new_m[n] for n in WEIGHTS], *[new_v[n] for n in WEIGHTS])
```

```python
import functools

import jax
import jax.numpy as jnp
from jax import lax
from jax.experimental import pallas as pl
from jax.experimental.pallas import tpu as pltpu

F32 = jnp.float32
BF16 = jnp.bfloat16

D_MODEL = 2048
N_Q_HEADS = 32
N_KV_HEADS = 8
HEAD_DIM = 64
WINDOW = 128
ATTN_BLOCK = 128
ROT_DIM = 16
ROPE_THETA = 500000.0
SSD_HEADS = 32
SSD_HEAD_DIM = 64
SSD_INNER = 2048
SSD_GROUPS = 8
SSD_STATE = 128
SSD_CONV = 4
SSD_CHUNK = 128
ATTN_WIDTH = 2048
KV_WIDTH = 512
BC_WIDTH = 1024
CONV_CH = 4096
IN_PROJ_WIDTH = 9248
MAIN_WIDTH = 9216
D_FF = 5632
FFN_CONV = 3
EPS = 1e-6
O_Q, O_K, O_V, O_Z, O_XBC, O_DT = 0, 2048, 2560, 3072, 5120, 9216

ADAM_LR = 0.001
ADAM_B1 = 0.9
ADAM_B2 = 0.999
ADAM_EPS = 1e-08
ADAM_WD = 0.01
ADAM_STEP = 10

N_CHIPS = 4
NEG = -1e30
LANES = 128
VMEM_LIMIT = 48 * 1024 * 1024
MESH = pl.DeviceIdType.MESH
HBM_SPEC = pl.BlockSpec(memory_space=pltpu.HBM)
TOKEN = jax.ShapeDtypeStruct((8, LANES), F32)

WEIGHTS = ['norm_mix', 'w_in', 'sinks', 'attn_out_norm', 'ssd_conv_w', 'ssd_conv_b', 'dt_bias', 'a_log', 'ssd_d',
           'ssd_norm', 'w_out', 'norm_ffn', 'w_up', 'ffn_conv_w', 'ffn_conv_b', 'w_down', 'norm_final']
BIG = ['w_in', 'w_out', 'w_up', 'w_down']


def _cp(sem=None, vmem=VMEM_LIMIT):
    kw = {'vmem_limit_bytes': vmem}
    if sem is not None:
        kw['dimension_semantics'] = sem
    return pltpu.CompilerParams(**kw)


def _tile(n, pref):
    if n <= pref:
        return n
    t = (pref // LANES) * LANES
    while t > LANES and n % t:
        t -= LANES
    assert n % t == 0, (n, pref)
    return t


def _rows(n, pref):
    t = min(n, pref)
    while n % t:
        t -= 8
    if 4 * t < pref:
        t = pref
        while n % t:
            t += 8
    return t


def _iota(shape, dim):
    return lax.broadcasted_iota(jnp.int32, shape, dim)


def _dot(a, b, mode='nn'):
    dn = {'nn': (((1,), (0,)), ((), ())), 'nt': (((1,), (1,)), ((), ())), 'tn': (((0,), (0,)), ((), ()))}[mode]
    return lax.dot_general(a.astype(BF16), b.astype(BF16), dn, preferred_element_type=F32)


def _dot_exact(a, b):
    return lax.dot_general(a, b, (((1,), (0,)), ((), ())), precision=lax.Precision.HIGHEST,
                           preferred_element_type=F32)


def _sigmoid(x):
    return 1.0 / (1.0 + jnp.exp(-x))


def _softplus(x):
    return jnp.maximum(x, 0.0) + jnp.log(1.0 + jnp.exp(-jnp.abs(x)))


def _matmul(a, b, *, mode, name, out_dtype=F32, add=None, deps=(), tm=1024, tn=1024, tk=2048,
            a_halves=False, b_halves=False, b_owner=False, owner_major=False, n_limit=None, k_limit=None,
            m_rows=None):
    ash, bsh = (a.shape[1:] if a_halves else a.shape), (b.shape[1:] if (b_halves or b_owner) else b.shape)
    if mode == 'nn':
        (m, k), (k2, n) = ash, bsh
    elif mode == 'nt':
        (m, k), (n, k2) = ash, bsh
    else:
        (k, m), (k2, n) = ash, bsh
    if n_limit is not None:
        assert mode == 'nt' and n_limit <= n
        n = n_limit
    if k_limit is not None:
        assert mode == 'nn' and k_limit <= k2
        k2 = k_limit
    if a_halves:
        assert mode == 'nt'
        k = 2 * k
    if b_halves:
        assert mode == 'tn'
        n = 2 * n
    if b_owner:
        assert mode in ('nn', 'nt')
        if mode == 'nn':
            n = 4 * n
        else:
            k2 = 4 * k2
    assert k == k2, (a.shape, b.shape, mode)
    tm = _tile(m, tm)
    tn = _tile(n // 4 if (owner_major or (b_owner and mode == 'nn')) else (n // 2 if b_halves else n), tn)
    tk = _tile(k // 4 if (b_owner and mode == 'nt') else (k // 2 if a_halves else k), tk)
    nk = k // tk
    has_add = add is not None
    assert not (has_add and owner_major)

    def body(*refs):
        a_ref, b_ref = refs[:2]
        add_ref = refs[2] if has_add else None

        def finish(r, o_ref):
            if has_add:
                r = r + add_ref[...].astype(F32)
            o_ref[...] = r.astype(out_dtype)

        if nk == 1:
            finish(_dot(a_ref[...], b_ref[...], mode), refs[-1])
            return
        o_ref, acc = refs[-2:]
        kk = pl.program_id(2)

        @pl.when(kk == 0)
        def _():
            acc[...] = _dot(a_ref[...], b_ref[...], mode)

        @pl.when((kk > 0) & (kk < nk - 1))
        def _():
            acc[...] += _dot(a_ref[...], b_ref[...], mode)

        @pl.when(kk == nk - 1)
        def _():
            finish(acc[...] + _dot(a_ref[...], b_ref[...], mode), o_ref)

    if mode == 'tn':
        a_spec = pl.BlockSpec((tk, tm), lambda i, j, kk: (kk, i))
    elif a_halves:
        nkh = nk // 2
        a_spec = pl.BlockSpec((None, tm, tk), lambda i, j, kk: (kk // nkh, i, kk % nkh))
    else:
        a_spec = pl.BlockSpec((tm, tk), lambda i, j, kk: (i, kk))
    if mode == 'nt' and b_owner:
        nkq = nk // 4
        b_spec = pl.BlockSpec((None, tn, tk), lambda i, j, kk: (kk // nkq, j, kk % nkq))
    elif mode == 'nt':
        b_spec = pl.BlockSpec((tn, tk), lambda i, j, kk: (j, kk))
    elif b_owner:
        njq = (n // 4) // tn
        b_spec = pl.BlockSpec((None, tk, tn), lambda i, j, kk: (j // njq, kk, j % njq))
    elif b_halves:
        njh = (n // 2) // tn
        b_spec = pl.BlockSpec((None, tk, tn), lambda i, j, kk: (j // njh, kk, j % njh))
    else:
        b_spec = pl.BlockSpec((tk, tn), lambda i, j, kk: (kk, j))
    if owner_major:
        njo = (n // 4) // tn
        o_spec = pl.BlockSpec((None, tm, tn), lambda i, j, kk: (j // njo, i, j % njo))
        out_shape = jax.ShapeDtypeStruct((N_CHIPS, m, n // 4), out_dtype)
    else:
        o_spec = pl.BlockSpec((tm, tn), lambda i, j, kk: (i, j))
        out_shape = jax.ShapeDtypeStruct((m if m_rows is None else m_rows, n), out_dtype)
    dep_spec = pl.BlockSpec((8, LANES), lambda i, j, kk: (0, 0))
    in_specs = [a_spec, b_spec] + ([pl.BlockSpec((tm, tn), lambda i, j, kk: (i, j))] if has_add else [])
    in_specs += [dep_spec] * len(deps)
    args = (a, b) + ((add,) if has_add else ()) + tuple(deps)
    return pl.pallas_call(
        body, name=name, grid=(m // tm, n // tn, nk), in_specs=in_specs, out_specs=o_spec, out_shape=out_shape,
        scratch_shapes=[pltpu.VMEM((tm, tn), F32)] if nk > 1 else [],
        compiler_params=_cp(("parallel", "parallel", "arbitrary")))(*args)


def _rmsnorm_fwd(x, g, name, deps=()):
    t, d = x.shape
    tb = _rows(t, 256)

    def body(x_ref, g_ref, *rest):
        o_ref = rest[-1]
        xv = x_ref[...]
        r = lax.rsqrt(jnp.mean(xv * xv, axis=-1, keepdims=True) + EPS)
        o_ref[...] = (xv * r * g_ref[...]).astype(BF16)

    dep_spec = pl.BlockSpec((8, LANES), lambda i: (0, 0))
    return pl.pallas_call(
        body, name=name, grid=(t // tb,),
        in_specs=[pl.BlockSpec((tb, d), lambda i: (i, 0)), pl.BlockSpec((1, d), lambda i: (0, 0))]
        + [dep_spec] * len(deps),
        out_specs=pl.BlockSpec((tb, d), lambda i: (i, 0)), out_shape=jax.ShapeDtypeStruct((t, d), BF16),
        compiler_params=_cp(("parallel",)))(x, g, *deps)


def _rmsnorm_bwd(x, g, dy, res, name, deps=()):
    t, d = x.shape
    tb = _rows(t, 256)

    def body(x_ref, g_ref, dy_ref, res_ref, *rest):
        dx_ref, dx16_ref, dg_ref = rest[-3:]
        i = pl.program_id(0)
        xv = x_ref[...]
        dyv = dy_ref[...].astype(F32)
        r = lax.rsqrt(jnp.mean(xv * xv, axis=-1, keepdims=True) + EPS)
        u = dyv * g_ref[...]
        dx = r * u - xv * (r * r * r * jnp.mean(u * xv, axis=-1, keepdims=True)) + res_ref[...]
        dx_ref[...] = dx
        dx16_ref[...] = dx.astype(BF16)
        part = jnp.sum(dyv * xv * r, axis=0, keepdims=True)

        @pl.when(i == 0)
        def _():
            dg_ref[...] = part

        @pl.when(i > 0)
        def _():
            dg_ref[...] += part

    row = pl.BlockSpec((tb, d), lambda i: (i, 0))
    vec = pl.BlockSpec((1, d), lambda i: (0, 0))
    return pl.pallas_call(
        body, name=name, grid=(t // tb,),
        in_specs=[row, vec, row, row] + [pl.BlockSpec((8, LANES), lambda i: (0, 0))] * len(deps),
        out_specs=[row, row, vec],
        out_shape=[jax.ShapeDtypeStruct((t, d), F32), jax.ShapeDtypeStruct((t, d), BF16),
                   jax.ShapeDtypeStruct((1, d), F32)],
        compiler_params=_cp(("arbitrary",)))(x, g, dy, res, *deps)


def _final_loss(h, g, tgt):
    t, d = h.shape
    tb = _rows(t, 256)

    def body(h_ref, g_ref, t_ref, loss_ref, dh_ref, dh16_ref, dg_ref):
        i = pl.program_id(0)
        hv = h_ref[...]
        gv = g_ref[...]
        r = lax.rsqrt(jnp.mean(hv * hv, axis=-1, keepdims=True) + EPS)
        y = hv * r * gv
        diff = y - t_ref[...]
        lpart = jnp.sum(jnp.sum(diff * diff, axis=1, keepdims=True), axis=0, keepdims=True) * (0.5 / d)
        dy = diff * (1.0 / d)
        u = dy * gv
        dh = r * u - hv * (r * r * r * jnp.mean(u * hv, axis=-1, keepdims=True))
        dh_ref[...] = dh
        dh16_ref[...] = dh.astype(BF16)
        gpart = jnp.sum(dy * hv * r, axis=0, keepdims=True)
        lrow = jnp.broadcast_to(lpart, (1, LANES))

        @pl.when(i == 0)
        def _():
            loss_ref[...] = lrow
            dg_ref[...] = gpart

        @pl.when(i > 0)
        def _():
            loss_ref[...] += lrow
            dg_ref[...] += gpart

    row = pl.BlockSpec((tb, d), lambda i: (i, 0))
    vec = pl.BlockSpec((1, d), lambda i: (0, 0))
    return pl.pallas_call(
        body, name="final_loss", grid=(t // tb,), in_specs=[row, vec, row],
        out_specs=[pl.BlockSpec((1, LANES), lambda i: (0, 0)), row, row, vec],
        out_shape=[jax.ShapeDtypeStruct((1, LANES), F32), jax.ShapeDtypeStruct((t, d), F32),
                   jax.ShapeDtypeStruct((t, d), BF16), jax.ShapeDtypeStruct((1, d), F32)],
        compiler_params=_cp(("arbitrary",)))(h, g, tgt)


def _rope_tables(t):
    pos = jnp.arange(t, dtype=F32)
    inv = 1.0 / (ROPE_THETA ** (jnp.arange(0, ROT_DIM, 2, dtype=F32) / ROT_DIM))
    ang = pos[:, None] * inv[None, :]
    cos, sin = jnp.cos(ang), jnp.sin(ang)
    half = ROT_DIM // 2
    rest = HEAD_DIM - ROT_DIM
    c = jnp.concatenate([cos, cos, jnp.ones((t, rest), F32)], axis=1)
    s1 = jnp.concatenate([-sin, jnp.zeros((t, half + rest), F32)], axis=1)
    s2 = jnp.concatenate([jnp.zeros((t, half), F32), sin, jnp.zeros((t, rest), F32)], axis=1)
    return jnp.concatenate([jnp.tile(v, (1, LANES // HEAD_DIM)) for v in (c, s1, s2)], axis=1)


def _split_tables(tab):
    return tab[:, :LANES], tab[:, LANES:2 * LANES], tab[:, 2 * LANES:]


def _rope(x, c, s1, s2):
    half = ROT_DIM // 2
    return x * c + pltpu.roll(x, LANES - half, 1) * s1 + pltpu.roll(x, half, 1) * s2


def _rope_t(g, c, s1, s2):
    half = ROT_DIM // 2
    return g * c + pltpu.roll(g * s1, half, 1) + pltpu.roll(g * s2, LANES - half, 1)


def _band_masks(i, heads):
    n = heads * ATTN_BLOCK
    q = jnp.bitwise_and(_iota((n, ATTN_BLOCK), 0), ATTN_BLOCK - 1)
    j = _iota((n, ATTN_BLOCK), 1)
    upper = j > q
    return upper, upper & (j < jnp.where(i > 0, 0, ATTN_BLOCK))


def _fold_band(full, upper):
    return jnp.where(upper, full[:, :ATTN_BLOCK], full[:, ATTN_BLOCK:])


def _unfold_band(band, upper):
    return jnp.concatenate([jnp.where(upper, band, 0.0), jnp.where(upper, 0.0, band)], axis=1)


def _half_masks():
    lane = _iota((1, LANES), 1)
    return [(lane < HEAD_DIM).astype(F32), (lane >= HEAD_DIM).astype(F32)]


def _stack_heads(blocks, hm, j):
    pieces = []
    for r in range(4):
        qb, half = (4 * j + r) // 2, (4 * j + r) % 2
        piece = blocks[qb] * hm[half]
        if half != j:
            piece = pltpu.roll(piece, HEAD_DIM, 1)
        pieces.append(piece)
    return jnp.concatenate(pieces, axis=0)


def _unstack_heads(stacked, j):
    out = []
    for qb in (2 * j, 2 * j + 1):
        acc = None
        for half in range(2):
            r = 2 * qb + half - 4 * j
            piece = stacked[r * ATTN_BLOCK:(r + 1) * ATTN_BLOCK]
            if half != j:
                piece = pltpu.roll(piece, HEAD_DIM, 1)
            acc = piece if acc is None else acc + piece
        out.append((qb, acc))
    return out


def _sink_column(sink_ref, base):
    return jnp.concatenate([jnp.full((ATTN_BLOCK, 1), sink_ref[base + r], F32) for r in range(4)], axis=0)


def _attn_specs(nb_clamp):
    blk = ATTN_BLOCK
    kb, vb = O_K // LANES, O_V // LANES

    def cur(i):
        return jnp.minimum(i, nb_clamp)

    def prev(i):
        return jnp.maximum(jnp.minimum(i, nb_clamp + 1) - 1, 0)

    q = pl.BlockSpec((blk, 512), lambda p, i: (cur(i), p))
    kc = pl.BlockSpec((blk, LANES), lambda p, i: (cur(i), kb + p))
    kp = pl.BlockSpec((blk, LANES), lambda p, i: (prev(i), kb + p))
    vc = pl.BlockSpec((blk, LANES), lambda p, i: (cur(i), vb + p))
    vp = pl.BlockSpec((blk, LANES), lambda p, i: (prev(i), vb + p))
    tc = pl.BlockSpec((blk, 3 * LANES), lambda p, i: (cur(i), 0))
    tp = pl.BlockSpec((blk, 3 * LANES), lambda p, i: (prev(i), 0))
    return q, kc, kp, vc, vp, tc, tp


def _attn_fwd(proj, sinks, tables):
    t = proj.shape[0]
    nb = t // ATTN_BLOCK
    scale = HEAD_DIM ** -0.5

    def body(sink_ref, q_ref, kc_ref, kp_ref, vc_ref, vp_ref, tc_ref, tp_ref, o_ref):
        p = pl.program_id(0)
        i = pl.program_id(1)
        cc, s1c, s2c = _split_tables(tc_ref[...])
        kband = jnp.concatenate([_rope(kp_ref[...], *_split_tables(tp_ref[...])),
                                 _rope(kc_ref[...], cc, s1c, s2c)], axis=0).astype(BF16)
        vband = jnp.concatenate([vp_ref[...], vc_ref[...]], axis=0)
        hm = _half_masks()
        vsel = [(vband * hm[j]).astype(BF16) for j in range(2)]
        upper, dropped = _band_masks(i, 1)
        qr = [_rope(q_ref[:, qb * LANES:(qb + 1) * LANES], cc, s1c, s2c) for qb in range(4)]

        def scores(hh):
            qb, half, j = hh // 2, hh % 2, hh // 4
            qs = qr[qb] * hm[half]
            if half != j:
                qs = pltpu.roll(qs, HEAD_DIM, 1)
            return _dot(qs, kband, 'nt')

        ahead = scores(0)
        acc = None
        for hh in range(8):
            qb, half, j = hh // 2, hh % 2, hh // 4
            raw = ahead
            if hh + 1 < 8:
                ahead = scores(hh + 1)
            s = jnp.where(dropped, NEG, _fold_band(raw, upper) * scale)
            sink = sink_ref[p * 8 + hh]
            m = jnp.maximum(jnp.max(s, axis=1, keepdims=True), sink)
            pe = jnp.exp(s - m)
            den = jnp.sum(pe, axis=1, keepdims=True) + jnp.exp(sink - m)
            o = _dot(_unfold_band(pe / den, upper), vsel[j])
            if half != j:
                o = pltpu.roll(o, HEAD_DIM, 1)
            acc = o if half == 0 else acc + o
            if half == 1:
                o_ref[:, qb * LANES:(qb + 1) * LANES] = acc

    q, kc, kp, vc, vp, tc, tp = _attn_specs(nb - 1)
    smem = pl.BlockSpec(memory_space=pltpu.SMEM)
    return pl.pallas_call(
        body, name="attn_fwd", grid=(4, nb),
        in_specs=[smem, q, kc, kp, vc, vp, tc, tp],
        out_specs=pl.BlockSpec((ATTN_BLOCK, 512), lambda p, i: (i, p)),
        out_shape=jax.ShapeDtypeStruct((t, ATTN_WIDTH), F32),
        compiler_params=_cp(("parallel", "arbitrary")))(sinks, proj, proj, proj, proj, proj, tables, tables)


def _attn_bwd(proj, sinks, tables, dout):
    t = proj.shape[0]
    nb = t // ATTN_BLOCK
    scale = HEAD_DIM ** -0.5

    def body(sink_ref, q_ref, kc_ref, kp_ref, vc_ref, vp_ref, tc_ref, tp_ref,
             do_ref, dq_ref, dk_ref, dv_ref, ds_ref, carry_k, carry_v):
        p = pl.program_id(0)
        i = pl.program_id(1)
        ptab = _split_tables(tp_ref[...])

        @pl.when(i == 0)
        def _():
            carry_k[...] = jnp.zeros_like(carry_k)
            carry_v[...] = jnp.zeros_like(carry_v)
            ds_ref[...] = jnp.zeros_like(ds_ref)

        @pl.when(i < nb)
        def _():
            cc, s1c, s2c = _split_tables(tc_ref[...])
            kband = jnp.concatenate([_rope(kp_ref[...], *ptab), _rope(kc_ref[...], cc, s1c, s2c)], axis=0)
            vband = jnp.concatenate([vp_ref[...], vc_ref[...]], axis=0)
            hm = _half_masks()
            kband16 = kband.astype(BF16)
            vband16 = vband.astype(BF16)
            upper, dropped = _band_masks(i, 4)
            dkb = jnp.zeros((2 * ATTN_BLOCK, LANES), F32)
            dvb = jnp.zeros((2 * ATTN_BLOCK, LANES), F32)
            row8 = _iota((8, LANES), 0)
            dsink = jnp.zeros((8, LANES), F32)
            qr = [_rope(q_ref[:, qb * LANES:(qb + 1) * LANES], cc, s1c, s2c) for qb in range(4)]
            dob = [do_ref[:, qb * LANES:(qb + 1) * LANES] for qb in range(4)]
            for j in range(2):
                qst = _stack_heads(qr, hm, j).astype(BF16)
                dost = _stack_heads(dob, hm, j).astype(BF16)
                s = jnp.where(dropped, NEG, _fold_band(_dot(qst, kband16, 'nt'), upper) * scale)
                sink = _sink_column(sink_ref, p * 8 + 4 * j)
                m = jnp.maximum(jnp.max(s, axis=1, keepdims=True), sink)
                pe = jnp.exp(s - m)
                psink = jnp.exp(sink - m)
                den = jnp.sum(pe, axis=1, keepdims=True) + psink
                pr = pe / den
                dvb = dvb + _dot(_unfold_band(pr, upper).T, dost)
                dp = _fold_band(_dot(dost, vband16, 'nt'), upper)
                delta = jnp.sum(pr * dp, axis=1, keepdims=True)
                dsc = _unfold_band(pr * (dp - delta) * scale, upper)
                dsk = psink / den * delta
                for r in range(4):
                    part = jnp.sum(dsk[r * ATTN_BLOCK:(r + 1) * ATTN_BLOCK])
                    dsink = dsink + jnp.where(row8 == 4 * j + r, -part, 0.0)
                for qb, dqb in _unstack_heads(_dot(dsc, kband * hm[j]), j):
                    dq_ref[:, qb * LANES:(qb + 1) * LANES] = _rope_t(dqb, cc, s1c, s2c).astype(BF16)
                dkb = dkb + _dot(dsc.T, qst)
            ds_ref[0] += dsink
            dk_ref[...] = _rope_t(carry_k[...] + dkb[:ATTN_BLOCK], *ptab).astype(BF16)
            dv_ref[...] = (carry_v[...] + dvb[:ATTN_BLOCK]).astype(BF16)
            carry_k[...] = dkb[ATTN_BLOCK:]
            carry_v[...] = dvb[ATTN_BLOCK:]

        @pl.when(i == nb)
        def _():
            dk_ref[...] = _rope_t(carry_k[...], *ptab).astype(BF16)
            dv_ref[...] = carry_v[...].astype(BF16)

    q, kc, kp, vc, vp, tc, tp = _attn_specs(nb - 1)
    smem = pl.BlockSpec(memory_space=pltpu.SMEM)
    qblk = pl.BlockSpec((ATTN_BLOCK, 512), lambda p, i: (jnp.minimum(i, nb - 1), p))
    kvout = pl.BlockSpec((ATTN_BLOCK, LANES), lambda p, i: (jnp.maximum(i - 1, 0), p))
    return pl.pallas_call(
        body, name="attn_bwd", grid=(4, nb + 1),
        in_specs=[smem, q, kc, kp, vc, vp, tc, tp, qblk],
        out_specs=[qblk, kvout, kvout, pl.BlockSpec((1, 8, LANES), lambda p, i: (p, 0, 0))],
        out_shape=[jax.ShapeDtypeStruct((t, ATTN_WIDTH), BF16), jax.ShapeDtypeStruct((t, KV_WIDTH), BF16),
                   jax.ShapeDtypeStruct((t, KV_WIDTH), BF16), jax.ShapeDtypeStruct((4, 8, LANES), F32)],
        scratch_shapes=[pltpu.VMEM((ATTN_BLOCK, LANES), F32), pltpu.VMEM((ATTN_BLOCK, LANES), F32)],
        compiler_params=_cp(("parallel", "arbitrary")))(sinks, proj, proj, proj, proj, proj, tables, tables, dout)


def _shift_rows(x, prev8, j):
    n, c = x.shape
    r = pltpu.roll(x.reshape(n // 8, 8, c), j, 1)
    before = pltpu.roll(prev8, j, 0)[None]
    if n > 8:
        before = jnp.concatenate([before, r[:-1]], axis=0)
    return jnp.where(_iota((1, 8, 1), 1) < j, before, r).reshape(n, c)


def _shift_rows_up(x, next8, j):
    n, c = x.shape
    r = pltpu.roll(x.reshape(n // 8, 8, c), 8 - j, 1)
    after = pltpu.roll(next8, 8 - j, 0)[None]
    if n > 8:
        after = jnp.concatenate([r[1:], after], axis=0)
    return jnp.where(_iota((1, 8, 1), 1) >= 8 - j, after, r).reshape(n, c)


def _conv_apply(x, prev8, w, b, taps):
    u = b + x * w[taps - 1:taps]
    for j in range(1, taps):
        u = u + _shift_rows(x, prev8, j) * w[taps - 1 - j:taps - j]
    return u


def _conv_grads(du, du_next8, x, w, taps):
    dx = du * w[taps - 1:taps]
    rowk = _iota((taps, 1), 0)
    dw = jnp.where(rowk == taps - 1, jnp.sum(du * x, axis=0, keepdims=True), 0.0)
    for j in range(1, taps):
        ahead = _shift_rows_up(du, du_next8, j)
        dx = dx + ahead * w[taps - 1 - j:taps - j]
        dw = dw + jnp.where(rowk == taps - 1 - j, jnp.sum(ahead * x, axis=0, keepdims=True), 0.0)
    return dx, dw, jnp.sum(du, axis=0, keepdims=True)


def _conv_specs(tb, tc, col0, t):
    c0 = col0 // tc
    cur = pl.BlockSpec((tb, tc), lambda j, i: (i, c0 + j))
    prev = pl.BlockSpec((8, tc), lambda j, i: (jnp.maximum(i * (tb // 8) - 1, 0), c0 + j))
    nxt = pl.BlockSpec((8, tc), lambda j, i: (jnp.minimum((i + 1) * (tb // 8), t // 8 - 1), c0 + j))
    return cur, prev, nxt


def _conv_silu_fwd(x, w, b, *, col0, width, name):
    t = x.shape[0]
    taps = w.shape[0]
    tb, tc = _rows(t, 512), _tile(width, 1024)
    assert col0 % tc == 0

    def body(x_ref, xp_ref, w_ref, b_ref, o_ref, u_ref):
        i = pl.program_id(1)
        prev8 = jnp.where(i > 0, xp_ref[...], 0.0)
        u = _conv_apply(x_ref[...], prev8, w_ref[...], b_ref[...], taps)
        u_ref[...] = u
        o_ref[...] = u * _sigmoid(u)

    cur, prev, _ = _conv_specs(tb, tc, col0, t)
    par = pl.BlockSpec((taps, tc), lambda j, i: (0, j))
    bias = pl.BlockSpec((1, tc), lambda j, i: (0, j))
    out = pl.BlockSpec((tb, tc), lambda j, i: (i, j))
    shp = jax.ShapeDtypeStruct((t, width), F32)
    return pl.pallas_call(
        body, name=name, grid=(width // tc, t // tb), in_specs=[cur, prev, par, bias], out_specs=[out, out],
        out_shape=[shp, shp], compiler_params=_cp(("parallel", "parallel")))(x, x, w, b)


def _dsilu(u):
    sg = _sigmoid(u)
    return sg * (1.0 + u * (1.0 - sg))


def _ssd_conv_bwd(x, w, dxs, dbm, dcm, *, col0, name):
    t = x.shape[0]
    taps = w.shape[0]
    tb, tc = _rows(t, 512), BC_WIDTH
    nrow, ncol = t // tb, CONV_CH // tc
    c0 = col0 // tc

    def body(x_ref, w_ref, xs_ref, xsn_ref, bm_ref, bmn_ref, cm_ref, cmn_ref, dx_ref, dw_ref, db_ref):
        i = pl.program_id(0)
        j = pl.program_id(1)

        def run(du_ref, dun_ref):
            next8 = jnp.where(i < nrow - 1, dun_ref[...], 0.0)
            dx, dwv, dbv = _conv_grads(du_ref[...], next8, x_ref[...], w_ref[...], taps)
            dx_ref[...] = dx.astype(BF16)

            @pl.when(i == 0)
            def _():
                dw_ref[j] = dwv
                db_ref[j] = dbv

            @pl.when(i > 0)
            def _():
                dw_ref[j] += dwv
                db_ref[j] += dbv

        pl.when(j < 2)(lambda: run(xs_ref, xsn_ref))
        pl.when(j == 2)(lambda: run(bm_ref, bmn_ref))
        pl.when(j == 3)(lambda: run(cm_ref, cmn_ref))

    def nxt_row(i):
        return jnp.minimum((i + 1) * (tb // 8), t // 8 - 1)

    xs_col = lambda j: jnp.minimum(j, SSD_INNER // tc - 1)
    in_specs = [pl.BlockSpec((tb, tc), lambda i, j: (i, c0 + j)), pl.BlockSpec((taps, tc), lambda i, j: (0, j)),
                pl.BlockSpec((tb, tc), lambda i, j: (i, xs_col(j))),
                pl.BlockSpec((8, tc), lambda i, j: (nxt_row(i), xs_col(j))),
                pl.BlockSpec((tb, tc), lambda i, j: (i, 0)), pl.BlockSpec((8, tc), lambda i, j: (nxt_row(i), 0)),
                pl.BlockSpec((tb, tc), lambda i, j: (i, 0)), pl.BlockSpec((8, tc), lambda i, j: (nxt_row(i), 0))]
    dx, dw, db = pl.pallas_call(
        body, name=name, grid=(nrow, ncol), in_specs=in_specs,
        out_specs=[pl.BlockSpec((tb, tc), lambda i, j: (i, j)),
                   pl.BlockSpec((ncol, taps, tc), lambda i, j: (0, 0, 0)),
                   pl.BlockSpec((ncol, 1, tc), lambda i, j: (0, 0, 0))],
        out_shape=[jax.ShapeDtypeStruct((t, CONV_CH), BF16), jax.ShapeDtypeStruct((ncol, taps, tc), F32),
                   jax.ShapeDtypeStruct((ncol, 1, tc), F32)],
        compiler_params=_cp(("arbitrary", "arbitrary")))(x, w, dxs, dxs, dbm, dbm, dcm, dcm)
    return dx, dw.transpose(1, 0, 2).reshape(taps, CONV_CH), db.transpose(1, 0, 2).reshape(1, CONV_CH)


def _ffn_specs(tb, tc, t):
    nc = D_FF // tc

    def cur(half):
        return pl.BlockSpec((tb, tc), lambda j, i: (i, half * nc + j))

    def prev(half):
        return pl.BlockSpec((8, tc), lambda j, i: (jnp.maximum(i * (tb // 8) - 1, 0), half * nc + j))

    def nxt(half):
        return pl.BlockSpec((8, tc), lambda j, i: (jnp.minimum((i + 1) * (tb // 8), t // 8 - 1), half * nc + j))

    def par(rows, half):
        return pl.BlockSpec((rows, tc), lambda j, i: (0, half * nc + j))

    return cur, prev, nxt, par


def _ffn_act_fwd(u0, w, b):
    t = u0.shape[0]
    tb, tc = _rows(t, 512), _tile(D_FF, 1408)
    cur, prev, _, par = _ffn_specs(tb, tc, t)

    def body(g_ref, gp_ref, v_ref, vp_ref, wg_ref, wv_ref, bg_ref, bv_ref, o_ref, u_ref):
        i = pl.program_id(1)
        ug = _conv_apply(g_ref[...], jnp.where(i > 0, gp_ref[...], 0.0), wg_ref[...], bg_ref[...], FFN_CONV)
        uv = _conv_apply(v_ref[...], jnp.where(i > 0, vp_ref[...], 0.0), wv_ref[...], bv_ref[...], FFN_CONV)
        o_ref[...] = (ug * _sigmoid(ug) * uv).astype(BF16)
        u_ref[0] = ug
        u_ref[1] = uv

    return pl.pallas_call(
        body, name="ffn_act_fwd", grid=(D_FF // tc, t // tb),
        in_specs=[cur(0), prev(0), cur(1), prev(1), par(FFN_CONV, 0), par(FFN_CONV, 1), par(1, 0), par(1, 1)],
        out_specs=[pl.BlockSpec((tb, tc), lambda j, i: (i, j)), pl.BlockSpec((2, tb, tc), lambda j, i: (0, i, j))],
        out_shape=[jax.ShapeDtypeStruct((t, D_FF), BF16), jax.ShapeDtypeStruct((2, t, D_FF), F32)],
        compiler_params=_cp(("parallel", "parallel")))(u0, u0, u0, u0, w, w, b, b)


def _ffn_act_bwd(u0, u, w, da):
    t = u0.shape[0]
    tb, tc = _rows(t, 256), _tile(D_FF, 1408)
    nrow = t // tb
    taps = FFN_CONV
    cur, _, _, par = _ffn_specs(tb, tc, t)

    def dact(ug, uv, dav):
        sg = _sigmoid(ug)
        return dav * uv * (sg * (1.0 + ug * (1.0 - sg))), dav * ug * sg

    def body(g_ref, v_ref, u_ref, un_ref, wg_ref, wv_ref, da_ref, dan_ref, dx_ref, dw_ref, db_ref):
        i = pl.program_id(1)
        dug, duv = dact(u_ref[0], u_ref[1], da_ref[...].astype(F32))
        dan = jnp.where(i < nrow - 1, dan_ref[...].astype(F32)[:8], 0.0)
        dugn, duvn = dact(un_ref[0], un_ref[1], dan)
        dxg, dwg, dbg = _conv_grads(dug, dugn, g_ref[...], wg_ref[...], taps)
        dxv, dwv, dbv = _conv_grads(duv, duvn, v_ref[...], wv_ref[...], taps)
        dx_ref[0] = dxg.astype(BF16)
        dx_ref[1] = dxv.astype(BF16)

        @pl.when(i == 0)
        def _():
            dw_ref[0] = dwg
            dw_ref[1] = dwv
            db_ref[0] = dbg
            db_ref[1] = dbv

        @pl.when(i > 0)
        def _():
            dw_ref[0] += dwg
            dw_ref[1] += dwv
            db_ref[0] += dbg
            db_ref[1] += dbv

    both = pl.BlockSpec((2, tb, tc), lambda j, i: (0, i, j))
    both_nxt = pl.BlockSpec((2, 8, tc), lambda j, i: (0, jnp.minimum((i + 1) * (tb // 8), t // 8 - 1), j))
    da_cur = pl.BlockSpec((tb, tc), lambda j, i: (i, j))
    da_nxt = pl.BlockSpec((16, tc), lambda j, i: (jnp.minimum((i + 1) * (tb // 16), t // 16 - 1), j))
    return pl.pallas_call(
        body, name="ffn_act_bwd", grid=(D_FF // tc, nrow),
        in_specs=[cur(0), cur(1), both, both_nxt, par(taps, 0), par(taps, 1), da_cur, da_nxt],
        out_specs=[both, pl.BlockSpec((2, taps, tc), lambda j, i: (0, 0, j)),
                   pl.BlockSpec((2, 1, tc), lambda j, i: (0, 0, j))],
        out_shape=[jax.ShapeDtypeStruct((2, t, D_FF), BF16), jax.ShapeDtypeStruct((2, taps, D_FF), F32),
                   jax.ShapeDtypeStruct((2, 1, D_FF), F32)],
        compiler_params=_cp(("parallel", "arbitrary")))(u0, u0, u, u, w, w, da, da)


def _head_masks():
    lane = _iota((1, 4 * SSD_HEAD_DIM), 1)
    return [((lane >= r * SSD_HEAD_DIM) & (lane < (r + 1) * SSD_HEAD_DIM)).astype(F32) for r in range(4)]


def _segsum(v):
    first = _iota((1, LANES), 1) < SSD_HEAD_DIM
    halves = []
    for k in range(2):
        vh = v[:, k * LANES:(k + 1) * LANES]
        both = jnp.sum(vh, axis=1, keepdims=True)
        one = jnp.sum(jnp.where(first, vh, 0.0), axis=1, keepdims=True)
        halves.append(jnp.where(first, one, both - one))
    return jnp.concatenate(halves, axis=1)


def _ssd_common(raw_e, prow, rawr4, bcol, acol):
    n = SSD_CHUNK
    dt_e = _softplus(raw_e + prow[0:1, :])
    a_e = -jnp.exp(prow[1:2, :])
    d_e = prow[2:3, :]
    tril = (_iota((n, n), 0) >= _iota((n, n), 1)).astype(F32)
    acs_e = _dot_exact(tril, dt_e * a_e)
    last_e = acs_e[n - 1:n, :]
    dtr4 = _softplus(rawr4 + bcol)
    triu = (_iota((n, n), 0) <= _iota((n, n), 1)).astype(F32)
    acs_r4 = _dot_exact(dtr4 * (-jnp.exp(acol)), triu)
    return dt_e, a_e, d_e, acs_e, last_e, acs_r4


def _decay_matrix(acs_e, acs_r4, r):
    n = SSD_CHUNK
    col = acs_e[:, r * SSD_HEAD_DIM:r * SSD_HEAD_DIM + 1]
    seg = col - acs_r4[r:r + 1, :]
    causal = _iota((n, n), 0) >= _iota((n, n), 1)
    return jnp.exp(jnp.where(causal, seg, NEG))


SSD_STEP_CHUNKS = 4
SSD_ROWS = SSD_STEP_CHUNKS * SSD_CHUNK


def _ssd_specs(t, rev):
    nb = t // SSD_ROWS
    xb, bb, cb = 0, SSD_INNER // SSD_STATE, (SSD_INNER + BC_WIDTH) // SSD_STATE

    def ch(c):
        return (nb - 1 - c) if rev else c

    x = pl.BlockSpec((SSD_ROWS, 256), lambda g, c: (ch(c), xb + g))
    bm = pl.BlockSpec((SSD_ROWS, SSD_STATE), lambda g, c: (ch(c), bb + g))
    cm = pl.BlockSpec((SSD_ROWS, SSD_STATE), lambda g, c: (ch(c), cb + g))
    dtc = pl.BlockSpec((1, SSD_ROWS, 256), lambda g, c: (g, ch(c), 0))
    dtr = pl.BlockSpec((1, 4, SSD_ROWS), lambda g, c: (g, 0, ch(c)))
    prow = pl.BlockSpec((1, 3, 256), lambda g, c: (g, 0, 0))
    pcol = pl.BlockSpec((1, 4, 1), lambda g, c: (g, 0, 0))
    st = pl.BlockSpec((1, SSD_STEP_CHUNKS, SSD_STATE, 256), lambda g, c: (g, ch(c), 0, 0))
    return x, bm, cm, dtc, dtr, prow, pcol, st, ch


def _ssd_params(dt_raw, dt_bias, a_log, ssd_d):
    t = dt_raw.shape[0]
    by_group = dt_raw.reshape(t, SSD_GROUPS, 4)
    dtc = jnp.repeat(by_group, SSD_HEAD_DIM, axis=2).transpose(1, 0, 2)
    dtr = by_group.transpose(1, 2, 0)
    prow = jnp.repeat(jnp.stack([dt_bias.reshape(SSD_GROUPS, 4), a_log.reshape(SSD_GROUPS, 4),
                                 ssd_d.reshape(SSD_GROUPS, 4)], axis=1), SSD_HEAD_DIM, axis=2)
    bcol = dt_bias.reshape(SSD_GROUPS, 4, 1)
    acol = a_log.reshape(SSD_GROUPS, 4, 1)
    return dtc, dtr, prow, bcol, acol


def _ssd_fwd(xbc, params):
    t = xbc.shape[0]
    nc = t // SSD_CHUNK
    dtc, dtr, prow, bcol, acol = params

    def body(x_ref, b_ref, c_ref, dtc_ref, dtr_ref, prow_ref, bcol_ref, acol_ref, y_ref, st_ref, s_scr):
        c = pl.program_id(1)

        @pl.when(c == 0)
        def _():
            s_scr[...] = jnp.zeros_like(s_scr)

        masks = _head_masks()
        s = s_scr[...]
        for k in range(SSD_STEP_CHUNKS):
            rows = slice(k * SSD_CHUNK, (k + 1) * SSD_CHUNK)
            dt_e, a_e, d_e, acs_e, last_e, acs_r4 = _ssd_common(
                dtc_ref[0, rows], prow_ref[0], dtr_ref[0][:, rows], bcol_ref[0], acol_ref[0])
            xv = x_ref[rows]
            bm, cm = b_ref[rows], c_ref[rows]
            st_ref[0, k] = s
            xdt = xv * dt_e
            cb = _dot(cm, bm, 'nt')
            y = _dot(cm, s) * jnp.exp(acs_e) + xv * d_e
            for r in range(4):
                mr = cb * _decay_matrix(acs_e, acs_r4, r)
                y = y + _dot(mr, xdt * masks[r])
            y_ref[rows] = y
            w = xdt * jnp.exp(last_e - acs_e)
            s = s * jnp.exp(last_e) + _dot(bm.T, w)
        s_scr[...] = s

    x, bm, cm, dtcs, dtrs, prs, pcs, st, _ = _ssd_specs(t, False)
    return pl.pallas_call(
        body, name="ssd_fwd", grid=(SSD_GROUPS, t // SSD_ROWS), in_specs=[x, bm, cm, dtcs, dtrs, prs, pcs, pcs],
        out_specs=[pl.BlockSpec((SSD_ROWS, 256), lambda g, c: (c, g)), st],
        out_shape=[jax.ShapeDtypeStruct((t, SSD_INNER), F32),
                   jax.ShapeDtypeStruct((SSD_GROUPS, nc, SSD_STATE, 256), F32)],
        scratch_shapes=[pltpu.VMEM((SSD_STATE, 256), F32)],
        compiler_params=_cp(("parallel", "arbitrary")))(xbc, xbc, xbc, dtc, dtr, prow, bcol, acol)


def _ssd_bwd(xbc, pre, params, states, dy):
    t = xbc.shape[0]
    nc = t // SSD_CHUNK
    n = SSD_CHUNK
    dtc, dtr, prow, bcol, acol = params

    def body(x_ref, b_ref, c_ref, ux_ref, ub_ref, uc_ref, dtc_ref, dtr_ref, prow_ref, bcol_ref, acol_ref, st_ref,
             dy_ref, dx_ref, db_ref, dc_ref, ddt_ref, dp_ref, ds_scr):
        c = pl.program_id(1)

        @pl.when(c == 0)
        def _():
            ds_scr[...] = jnp.zeros_like(ds_scr)
            dp_ref[...] = jnp.zeros_like(dp_ref)

        masks = _head_masks()
        ds = ds_scr[...]
        for k in reversed(range(SSD_STEP_CHUNKS)):
            rows = slice(k * SSD_CHUNK, (k + 1) * SSD_CHUNK)
            raw_e = dtc_ref[0, rows]
            prw = prow_ref[0]
            dt_e, a_e, d_e, acs_e, last_e, acs_r4 = _ssd_common(raw_e, prw, dtr_ref[0][:, rows], bcol_ref[0], acol_ref[0])
            xv = x_ref[rows]
            bm, cm = b_ref[rows], c_ref[rows]
            s = st_ref[0, k]
            dyv = dy_ref[rows]
            e_e = jnp.exp(acs_e)
            dec_e = jnp.exp(last_e - acs_e)
            cd_e = jnp.exp(last_e)
            xdt = xv * dt_e
            w = xdt * dec_e
            b16, c16, s16, ds16 = bm.astype(BF16), cm.astype(BF16), s.astype(BF16), ds.astype(BF16)
            cb = _dot(c16, b16, 'nt')
            yoff_raw = _dot(c16, s16)
            dye = dyv * e_e
            dye16 = dye.astype(BF16)
            dcm = _dot(dye16, s16, 'nt')
            ds_prev = ds * cd_e + _dot(cm.T, dye16)
            dacs_e = _segsum(dyv * yoff_raw) * e_e
            dw = _dot(b16, ds16)
            dbm = _dot(w, ds16, 'nt')
            tdec = _segsum(dw * xdt) * dec_e
            dacs_e = dacs_e - tdec
            dlast_e = jnp.sum(tdec, axis=0, keepdims=True)
            dxdt = dw * dec_e
            dlast_e = dlast_e + _segsum(jnp.sum(ds * s, axis=0, keepdims=True)) * cd_e
            dcb = jnp.zeros((n, n), F32)
            for r in range(4):
                lm = _decay_matrix(acs_e, acs_r4, r)
                mr = cb * lm
                dyr16 = (dyv * masks[r]).astype(BF16)
                dm = _dot(dyr16, xdt * masks[r], 'nt')
                dcb = dcb + dm * lm
                dseg = dm * mr
                dcol = jnp.sum(dseg, axis=1, keepdims=True) - jnp.sum(dseg.T, axis=1, keepdims=True)
                dacs_e = dacs_e + dcol * masks[r]
                dxdt = dxdt + _dot(mr.T, dyr16)
            dcm = dcm + _dot(dcb, b16)
            dbm = dbm + _dot(dcb.T, c16)
            dacs_e = dacs_e + jnp.where(_iota((n, 1), 0) == n - 1, dlast_e, 0.0)
            triu = (_iota((n, n), 0) <= _iota((n, n), 1)).astype(F32)
            ddta_e = _dot_exact(triu, dacs_e)
            ddt_e = ddta_e * a_e + _segsum(dxdt * xv)
            dx_ref[rows] = (dxdt * dt_e + dyv * d_e) * _dsilu(ux_ref[rows])
            db_ref[rows] = dbm * _dsilu(ub_ref[rows])
            dc_ref[rows] = dcm * _dsilu(uc_ref[rows])
            draw_e = ddt_e * _sigmoid(raw_e + prw[0:1, :])
            draw_t = draw_e.T
            ddt_ref[0, :, rows] = jnp.concatenate([draw_t[r * SSD_HEAD_DIM:r * SSD_HEAD_DIM + 1] for r in range(4)], axis=0)
            dbias = jnp.sum(draw_e, axis=0, keepdims=True)
            dalog = jnp.sum(ddta_e * dt_e, axis=0, keepdims=True) * a_e
            dd = _segsum(jnp.sum(dyv * xv, axis=0, keepdims=True))
            row3 = _iota((3, 1), 0)
            dp_ref[0] += (jnp.where(row3 == 0, dbias, 0.0) + jnp.where(row3 == 1, dalog, 0.0)
                          + jnp.where(row3 == 2, dd, 0.0))
            ds = ds_prev
        ds_scr[...] = ds


    x, bm, cm, dtcs, dtrs, prs, pcs, st, ch = _ssd_specs(t, True)
    yblk = pl.BlockSpec((SSD_ROWS, 256), lambda g, c: (ch(c), g))
    nblk = pl.BlockSpec((SSD_ROWS, SSD_STATE), lambda g, c: (ch(c), g))
    return pl.pallas_call(
        body, name="ssd_bwd", grid=(SSD_GROUPS, t // SSD_ROWS),
        in_specs=[x, bm, cm, x, bm, cm, dtcs, dtrs, prs, pcs, pcs, st, yblk],
        out_specs=[yblk, nblk, nblk, dtrs, prs],
        out_shape=[jax.ShapeDtypeStruct((t, SSD_INNER), F32), jax.ShapeDtypeStruct((t, BC_WIDTH), F32),
                   jax.ShapeDtypeStruct((t, BC_WIDTH), F32), jax.ShapeDtypeStruct((SSD_GROUPS, 4, t), F32),
                   jax.ShapeDtypeStruct((SSD_GROUPS, 3, 256), F32)],
        scratch_shapes=[pltpu.VMEM((SSD_STATE, 256), F32)],
        compiler_params=_cp(("parallel", "arbitrary")))(xbc, xbc, xbc, pre, pre, pre, dtc, dtr, prow, bcol, acol,
                                                         states, dy)


GROUP_W = SSD_INNER // SSD_GROUPS


def _mix_specs(tb):
    row = pl.BlockSpec((tb, 2048), lambda i: (i, 0))
    zlo = pl.BlockSpec((tb, 1024), lambda i: (i, O_Z // 1024))
    zhi = pl.BlockSpec((tb, 1024), lambda i: (i, O_Z // 1024 + 1))
    vec = pl.BlockSpec((1, 2048), lambda i: (0, 0))
    return row, zlo, zhi, vec


def _mix_fwd(attn, y, proj, g_attn, g_ssd):
    t = attn.shape[0]
    tb = _rows(t, 256)

    def body(a_ref, y_ref, zlo_ref, zhi_ref, ga_ref, gs_ref, o_ref):
        av = a_ref[...]
        r = lax.rsqrt(jnp.mean(av * av, axis=-1, keepdims=True) + EPS)
        o_ref[:, :ATTN_WIDTH] = (av * r * ga_ref[...]).astype(BF16)
        for g in range(SSD_GROUPS):
            lo, hi = g * GROUP_W, (g + 1) * GROUP_W
            zref = zlo_ref if g < 4 else zhi_ref
            z = zref[:, lo % 1024:lo % 1024 + GROUP_W]
            yg = y_ref[:, lo:hi] * (z * _sigmoid(z))
            rg = lax.rsqrt(jnp.mean(yg * yg, axis=-1, keepdims=True) + EPS)
            o_ref[:, ATTN_WIDTH + lo:ATTN_WIDTH + hi] = (yg * rg * gs_ref[:, lo:hi]).astype(BF16)

    row, zlo, zhi, vec = _mix_specs(tb)
    return pl.pallas_call(
        body, name="mix_fwd", grid=(t // tb,), in_specs=[row, row, zlo, zhi, vec, vec],
        out_specs=pl.BlockSpec((tb, 4096), lambda i: (i, 0)), out_shape=jax.ShapeDtypeStruct((t, 4096), BF16),
        compiler_params=_cp(("parallel",)))(attn, y, proj, proj, g_attn, g_ssd)


def _mix_bwd(dmix, attn, y, proj, g_attn, g_ssd):
    t = attn.shape[0]
    tb = _rows(t, 256)

    def body(dm_ref, a_ref, y_ref, zlo_ref, zhi_ref, ga_ref, gs_ref, da_ref, dy_ref, dz_ref, dga_ref, dgs_ref):
        i = pl.program_id(0)
        av = a_ref[...]
        dn = dm_ref[:, :ATTN_WIDTH].astype(F32)
        r = lax.rsqrt(jnp.mean(av * av, axis=-1, keepdims=True) + EPS)
        u = dn * ga_ref[...]
        da_ref[...] = r * u - av * (r * r * r * jnp.mean(u * av, axis=-1, keepdims=True))
        dga = jnp.sum(dn * av * r, axis=0, keepdims=True)

        @pl.when(i == 0)
        def _():
            dga_ref[...] = dga

        @pl.when(i > 0)
        def _():
            dga_ref[...] += dga

        for g in range(SSD_GROUPS):
            lo, hi = g * GROUP_W, (g + 1) * GROUP_W
            zref = zlo_ref if g < 4 else zhi_ref
            z = zref[:, lo % 1024:lo % 1024 + GROUP_W]
            yv = y_ref[:, lo:hi]
            sg = _sigmoid(z)
            sz = z * sg
            yg = yv * sz
            rg = lax.rsqrt(jnp.mean(yg * yg, axis=-1, keepdims=True) + EPS)
            do = dm_ref[:, ATTN_WIDTH + lo:ATTN_WIDTH + hi].astype(F32)
            ug = do * gs_ref[:, lo:hi]
            dyg = rg * ug - yg * (rg * rg * rg * jnp.mean(ug * yg, axis=-1, keepdims=True))
            dy_ref[:, lo:hi] = dyg * sz
            dz_ref[:, lo:hi] = (dyg * yv * (sg * (1.0 + z * (1.0 - sg)))).astype(BF16)
            dgs = jnp.sum(do * yg * rg, axis=0, keepdims=True)

            @pl.when(i == 0)
            def _():
                dgs_ref[:, lo:hi] = dgs

            @pl.when(i > 0)
            def _():
                dgs_ref[:, lo:hi] += dgs

    row, zlo, zhi, vec = _mix_specs(tb)
    return pl.pallas_call(
        body, name="mix_bwd", grid=(t // tb,),
        in_specs=[pl.BlockSpec((tb, 4096), lambda i: (i, 0)), row, row, zlo, zhi, vec, vec],
        out_specs=[row, row, row, vec, vec],
        out_shape=[jax.ShapeDtypeStruct((t, 2048), F32), jax.ShapeDtypeStruct((t, 2048), F32),
                   jax.ShapeDtypeStruct((t, 2048), BF16), jax.ShapeDtypeStruct((1, 2048), F32),
                   jax.ShapeDtypeStruct((1, 2048), F32)],
        compiler_params=_cp(("arbitrary",)))(dmix, attn, y, proj, proj, g_attn, g_ssd)


def _adamw(w, g, m, v, name):
    r, c = w.shape
    tb = _rows(r, 256)
    c1 = 1.0 - ADAM_B1 ** ADAM_STEP
    c2 = 1.0 - ADAM_B2 ** ADAM_STEP

    def body(w_ref, g_ref, m_ref, v_ref, d_ref, m2_ref, v2_ref):
        gv = g_ref[...]
        m2 = ADAM_B1 * m_ref[...] + (1.0 - ADAM_B1) * gv
        v2 = ADAM_B2 * v_ref[...] + (1.0 - ADAM_B2) * (gv * gv)
        d_ref[...] = -ADAM_LR * ((m2 / c1) / (jnp.sqrt(v2 / c2) + ADAM_EPS) + ADAM_WD * w_ref[...])
        m2_ref[...] = m2
        v2_ref[...] = v2

    blk = pl.BlockSpec((tb, c), lambda i: (i, 0))
    shp = jax.ShapeDtypeStruct((r, c), F32)
    return pl.pallas_call(body, name=name, grid=(r // tb,), in_specs=[blk] * 4, out_specs=[blk] * 3,
                          out_shape=[shp] * 3, compiler_params=_cp(("parallel",)))(w, g, m, v)


def _adamw_halves(w, mine, theirs, m, v, pos, name, cols=False):
    r, c = w.shape
    h = r if cols else r // 2
    tb = _rows(h, 128)
    nh = h // tb
    c1 = 1.0 - ADAM_B1 ** ADAM_STEP
    c2 = 1.0 - ADAM_B2 ** ADAM_STEP

    def body(pos_ref, w_ref, a_ref, b_ref, m_ref, v_ref, g_ref, d_ref, m2_ref, v2_ref):
        which = pl.program_id(1) if cols else pl.program_id(0) // nh
        gv = jnp.where(which == pos_ref[0], a_ref[...], b_ref[...])
        m2 = ADAM_B1 * m_ref[...] + (1.0 - ADAM_B1) * gv
        v2 = ADAM_B2 * v_ref[...] + (1.0 - ADAM_B2) * (gv * gv)
        g_ref[...] = gv
        d_ref[...] = -ADAM_LR * ((m2 / c1) / (jnp.sqrt(v2 / c2) + ADAM_EPS) + ADAM_WD * w_ref[...])
        m2_ref[...] = m2
        v2_ref[...] = v2

    if cols:
        full = pl.BlockSpec((tb, c // 2), lambda i, j, pref: (i, j))
        mine_spec = theirs_spec = pl.BlockSpec((tb, c // 2), lambda i, j, pref: (i, 0))
        grid = (nh, 2)
    else:
        full = pl.BlockSpec((tb, c), lambda i, pref: (i, 0))
        mine_spec = pl.BlockSpec((tb, c), lambda i, pref: (jnp.where(i // nh == pref[0], i % nh,
                                                                     jnp.where(pref[0] == 0, nh - 1, 0)), 0))
        theirs_spec = pl.BlockSpec((tb, c), lambda i, pref: (jnp.where(i // nh != pref[0], i % nh,
                                                                       jnp.where(pref[0] == 0, 0, nh - 1)), 0))
        grid = (r // tb,)
    shp = jax.ShapeDtypeStruct((r, c), F32)
    grid_spec = pltpu.PrefetchScalarGridSpec(num_scalar_prefetch=1, grid=grid,
                                             in_specs=[full, mine_spec, theirs_spec, full, full],
                                             out_specs=[full] * 4)
    return pl.pallas_call(body, name=name, grid_spec=grid_spec, out_shape=[shp] * 4,
                          compiler_params=_cp(("parallel",) * len(grid)))(pos, w, mine, theirs, m, v)


def _sum_own_half(g4, recv, pos, name, cols=False):
    _, r, c = g4.shape
    h, c = (r, c // 2) if cols else (r // 2, c)
    tb = _rows(h, 128)
    nh = h // tb

    def slot(j, pref):
        return (pref[1] + 1 + j) % N_CHIPS

    if cols:
        own = lambda j, i, pref: (slot(j, pref), i, pref[0])
    else:
        own = lambda j, i, pref: (slot(j, pref), pref[0] * nh + i, 0)
    same = lambda j, i, pref: (slot(j, pref), i, 0)

    def body(pos_ref, a_ref, b_ref, o_ref):
        o_ref[...] = (a_ref[...] + b_ref[...]).astype(BF16)

    grid_spec = pltpu.PrefetchScalarGridSpec(
        num_scalar_prefetch=1, grid=(N_CHIPS - 1, nh),
        in_specs=[pl.BlockSpec((1, tb, c), own), pl.BlockSpec((1, tb, c), same)],
        out_specs=pl.BlockSpec((1, tb, c), same))
    return pl.pallas_call(body, name=name, grid_spec=grid_spec,
                          out_shape=jax.ShapeDtypeStruct((N_CHIPS, h, c), BF16),
                          compiler_params=_cp(("parallel", "parallel")))(pos, g4, recv)


def _sum_chips(g4, recv, parts, pos, name, cols=False):
    _, r, c = g4.shape
    h, c = (r, c // 2) if cols else (r // 2, c)
    tb = _rows(h, 128)
    nh = h // tb
    own = (lambda i, pref: (pref[1], i, pref[0])) if cols else (lambda i, pref: (pref[1], pref[0] * nh + i, 0))

    def body(pos_ref, a_ref, b_ref, p_ref, o_ref):
        own = a_ref[0] + b_ref[0]
        o_ref[...] = ((own + p_ref[0].astype(F32)) + p_ref[1].astype(F32)) + p_ref[2].astype(F32)

    grid_spec = pltpu.PrefetchScalarGridSpec(
        num_scalar_prefetch=1, grid=(nh,),
        in_specs=[pl.BlockSpec((1, tb, c), own),
                  pl.BlockSpec((1, tb, c), lambda i, pref: (pref[1], i, 0)),
                  pl.BlockSpec((3, tb, c), lambda i, pref: (0, i, 0))],
        out_specs=pl.BlockSpec((tb, c), lambda i, pref: (i, 0)))
    return pl.pallas_call(body, name=name, grid_spec=grid_spec, out_shape=jax.ShapeDtypeStruct((h, c), F32),
                          compiler_params=_cp(("parallel",)))(pos, g4, recv, parts)


def _me():
    return lax.axis_index("x"), lax.axis_index("y"), lax.axis_index("c")


def _flip(v, bit):
    return (1 - v) if bit else v


CHIP_FLIPS = [(1, 0), (0, 1), (1, 1)]


def _forward_halves(gathered):
    def body(g_ref, o_ref, token, send_sems, recv_sems):
        x, y, c = _me()
        h = g_ref.shape[2] // 2
        cps = []
        for k, (fx, fy) in enumerate(CHIP_FLIPS):
            peer_chip = 2 * _flip(x, fx) + _flip(y, fy)
            mine = o_ref.at[peer_chip, :, pl.ds(c * h, h)]
            cp = pltpu.make_async_remote_copy(src_ref=mine, dst_ref=mine, send_sem=send_sems.at[k],
                                              recv_sem=recv_sems.at[k], device_id=(x, y, 1 - c), device_id_type=MESH)
            cp.start()
            cps.append(cp)
        for k, (fx, fy) in enumerate(CHIP_FLIPS):
            peer_chip = 2 * _flip(x, fx) + _flip(y, fy)
            theirs = o_ref.at[peer_chip, :, pl.ds((1 - c) * h, h)]
            pltpu.make_async_remote_copy(src_ref=theirs, dst_ref=theirs, send_sem=send_sems.at[k],
                                         recv_sem=recv_sems.at[k], device_id=(x, y, 1 - c),
                                         device_id_type=MESH).wait_recv()
        for cp in cps:
            cp.wait_send()
        token[...] = jnp.zeros_like(token)

    return pl.pallas_call(
        body, name="gather_forward_w_in", in_specs=[HBM_SPEC],
        out_specs=[HBM_SPEC, pl.BlockSpec(memory_space=pltpu.VMEM)],
        out_shape=[jax.ShapeDtypeStruct(gathered.shape, gathered.dtype), TOKEN],
        scratch_shapes=[pltpu.SemaphoreType.DMA((3,)), pltpu.SemaphoreType.DMA((3,))],
        input_output_aliases={0: 0},
        compiler_params=pltpu.CompilerParams(has_side_effects=True))(gathered)


SEM_SPEC = pl.BlockSpec(memory_space=pltpu.SEMAPHORE)
ANY_SPEC = pl.BlockSpec(memory_space=pl.ANY)
DATAFLOW = pltpu.SideEffectType.DATAFLOW_SIDE_EFFECTING


def _in_hbm(a):
    return pltpu.with_memory_space_constraint(a, pltpu.HBM)


def _push_start(srcs, land_shapes, route, peers, name):
    n, npeer = len(srcs), len(peers)
    lands = [lax.empty(shp, s.dtype) for shp, s in zip(land_shapes, srcs)]

    def body(*refs):
        ins, lnd = refs[:n], refs[n:2 * n]
        send_sems, recv_sems = refs[2 * n], refs[2 * n + 1]
        token = refs[-1]
        x, y, c = _me()
        for t in range(n):
            for k, (fx, fy, fc) in enumerate(peers):
                src, dst = route(ins[t], lnd[t], k, x, y, c)
                pltpu.make_async_remote_copy(
                    src_ref=src, dst_ref=dst, send_sem=send_sems.at[npeer * t + k],
                    recv_sem=recv_sems.at[npeer * t + k],
                    device_id=(_flip(x, fx), _flip(y, fy), _flip(c, fc)), device_id_type=MESH).start()
        token[...] = jnp.zeros_like(token)

    bufs = [_in_hbm(a) for a in list(srcs) + lands]
    outs = pl.pallas_call(
        body, name=name,
        out_shape=(pltpu.SemaphoreType.DMA((npeer * n,)), pltpu.SemaphoreType.DMA((npeer * n,)),
                   *[pltpu.HBM(b.shape, b.dtype) for b in bufs], TOKEN),
        in_specs=[HBM_SPEC] * (2 * n),
        out_specs=(SEM_SPEC, SEM_SPEC, *[HBM_SPEC] * (2 * n), pl.BlockSpec(memory_space=pltpu.VMEM)),
        input_output_aliases={i: 2 + i for i in range(2 * n)},
        compiler_params=pltpu.CompilerParams(has_side_effects=DATAFLOW))(*bufs)
    return outs[0], outs[1], list(outs[2:2 + n]), list(outs[2 + n:2 + 2 * n]), outs[-1]


def _push_wait(send_sems, recv_sems, srcs, lands, after, route, peers, name):
    n, npeer = len(srcs), len(peers)

    def body(*refs):
        ins, lnd = refs[:n], refs[n:2 * n]
        ssem, rsem = refs[2 * n], refs[2 * n + 1]
        x, y, c = _me()
        for t in range(n):
            for k, (fx, fy, fc) in enumerate(peers):
                src, dst = route(ins[t], lnd[t], k, x, y, c)
                cp = pltpu.make_async_remote_copy(
                    src_ref=src, dst_ref=dst, send_sem=ssem.at[npeer * t + k], recv_sem=rsem.at[npeer * t + k],
                    device_id=(_flip(x, fx), _flip(y, fy), _flip(c, fc)), device_id_type=MESH)
                cp.wait_send()
                cp.wait_recv()

    bufs = list(srcs) + list(lands)
    outs = pl.pallas_call(
        body, name=name, out_shape=tuple(pltpu.HBM(b.shape, b.dtype) for b in bufs),
        in_specs=[HBM_SPEC] * (2 * n) + [SEM_SPEC, SEM_SPEC, ANY_SPEC], out_specs=tuple([HBM_SPEC] * (2 * n)),
        input_output_aliases={i: i for i in range(2 * n)},
        compiler_params=pltpu.CompilerParams(has_side_effects=DATAFLOW))(*bufs, send_sems, recv_sems, after)
    return list(outs[:n]), list(outs[n:])


OTHER_CHIPS = [(fx, fy, 0) for fx, fy in CHIP_FLIPS]
SIBLING = [(0, 0, 1)]


def _route_gather(src, land, k, x, y, c):
    return src, land.at[2 * x + y]


def _route_gather_half(src, land, k, x, y, c):
    h = src.shape[1] // 2
    return src.at[:, pl.ds(c * h, h)], land.at[2 * x + y, :, pl.ds(c * h, h)]


def _route_gather_half_wait(src, land, k, x, y, c):
    fx, fy = CHIP_FLIPS[k]
    h = src.shape[1] // 2
    return src.at[:, pl.ds(c * h, h)], land.at[2 * _flip(x, fx) + _flip(y, fy), :, pl.ds(c * h, h)]


def _route_gather_wait(src, land, k, x, y, c):
    fx, fy = CHIP_FLIPS[k]
    return src, land.at[2 * _flip(x, fx) + _flip(y, fy)]


def _route_scatter(src, land, k, x, y, c):
    fx, fy = CHIP_FLIPS[k]
    return src.at[2 * _flip(x, fx) + _flip(y, fy)], land.at[k]


def _route_exchange(src, land, k, x, y, c):
    h = land.shape[1]
    return src.at[:, pl.ds((1 - c) * h, h)], land


def _route_whole(src, land, k, x, y, c):
    return src, land


def _route_exchange_cols(src, land, k, x, y, c):
    h = land.shape[2]
    return src.at[:, :, pl.ds((1 - c) * h, h)], land


def _allreduce_small(v):
    r = v.shape[0]

    def body(v_ref, o_ref, token, buf, send_sems, recv_sems):
        x, y, c = _me()
        me = 4 * x + 2 * y + c
        buf[0] = v_ref[...]
        cps = []
        for k in range(1, 8):
            kx, ky, kc = (k >> 2) & 1, (k >> 1) & 1, k & 1
            cp = pltpu.make_async_remote_copy(
                src_ref=v_ref, dst_ref=buf.at[k], send_sem=send_sems.at[k - 1], recv_sem=recv_sems.at[k - 1],
                device_id=(_flip(x, kx), _flip(y, ky), _flip(c, kc)), device_id_type=MESH)
            cp.start()
            cps.append(cp)
        for cp in cps:
            cp.wait()
        acc = buf[me]
        for d in range(1, 8):
            acc = acc + buf[jnp.bitwise_xor(me, d)]
        o_ref[...] = acc
        token[...] = jnp.zeros_like(token)

    vm = pl.BlockSpec(memory_space=pltpu.VMEM)
    return pl.pallas_call(
        body, name="allreduce_small", in_specs=[vm], out_specs=[vm, vm],
        out_shape=[jax.ShapeDtypeStruct(v.shape, F32), TOKEN],
        scratch_shapes=[pltpu.VMEM((8, r, LANES), F32), pltpu.SemaphoreType.DMA((7,)),
                        pltpu.SemaphoreType.DMA((7,))],
        compiler_params=pltpu.CompilerParams(has_side_effects=True, vmem_limit_bytes=VMEM_LIMIT))(v)


def _grad_exchange_start(g4, tag, cols=False):
    land = (N_CHIPS, g4.shape[1], g4.shape[2] // 2) if cols else (N_CHIPS, g4.shape[1] // 2, g4.shape[2])
    route = _route_exchange_cols if cols else _route_exchange
    send_sems, recv_sems, srcs, lands, token = _push_start(
        [g4], [land], route, SIBLING, name="grad_exchange_start_" + tag)
    return (send_sems, recv_sems, srcs, lands, tag, cols), token


def _grad_scatter_start(state, pos, after):
    send_sems, recv_sems, srcs, lands, tag, cols = state
    route = _route_exchange_cols if cols else _route_exchange
    (g4,), (recv,) = _push_wait(send_sems, recv_sems, srcs, lands, after, route, SIBLING,
                                name="grad_exchange_wait_" + tag)
    return _grad_pair_scatter(g4, recv, pos, tag, cols)


def _grad_pair_scatter(g4, recv, pos, tag, cols=False):
    p16 = _sum_own_half(g4, recv, pos, name="grad_sum_pair_" + tag, cols=cols)
    send_sems, recv_sems, srcs, lands, token = _push_start(
        [p16], [(3,) + p16.shape[1:]], _route_scatter, OTHER_CHIPS, name="grad_scatter_start_" + tag)
    return (g4, recv, send_sems, recv_sems, srcs, lands, tag, cols), token


def _grad_sum_and_share(state, pos, after):
    g4, recv, send_sems, recv_sems, srcs, lands, tag, cols = state
    parts = _push_wait(send_sems, recv_sems, srcs, lands, after, _route_scatter, OTHER_CHIPS,
                       name="grad_scatter_wait_" + tag)[1][0]
    mine = _sum_chips(g4, recv, parts, pos, name="grad_sum_chips_" + tag, cols=cols)
    send_sems, recv_sems, srcs, lands, token = _push_start(
        [mine], [mine.shape], _route_whole, SIBLING, name="grad_share_start_" + tag)
    return (send_sems, recv_sems, srcs, lands, tag), token


def _grad_share_wait(state, after):
    send_sems, recv_sems, srcs, lands, tag = state
    (mine,), (theirs,) = _push_wait(send_sems, recv_sems, srcs, lands, after, _route_whole, SIBLING,
                                    name="grad_share_wait_" + tag)
    return mine, theirs


def _local_step(x, tgt, p, hooks):
    t = x.shape[0]
    tables = _rope_tables(t)
    sinks = p['sinks'].reshape(N_Q_HEADS)

    def told(name, value):
        return tuple(hooks.grad_ready(name, value))

    xn = _rmsnorm_fwd(x, p['norm_mix'], "norm_mix_fwd", deps=hooks.first_deps)
    w_in_t, w_in_dt, in_deps = hooks.weight_in(xn)
    proj = _matmul(xn, w_in_t, mode='nt', name="in_proj", n_limit=MAIN_WIDTH, deps=in_deps)
    dt_raw = _matmul(xn, w_in_dt, mode='nt', name="in_proj_dt")[:, :SSD_HEADS]
    ssd_conv_w, ffn_conv_w = hooks.conv_weights(proj)
    p = dict(p, ssd_conv_w=ssd_conv_w, ffn_conv_w=ffn_conv_w)
    attn = _attn_fwd(proj, sinks, tables)
    conv_b = p['ssd_conv_b']
    xbc, xbc_pre = _conv_silu_fwd(proj, p['ssd_conv_w'], conv_b, col0=O_XBC, width=CONV_CH, name="ssd_conv_fwd")
    sp = _ssd_params(dt_raw, p['dt_bias'].reshape(-1), p['a_log'].reshape(-1), p['ssd_d'].reshape(-1))
    y, states = _ssd_fwd(xbc, sp)
    mix = _mix_fwd(attn, y, proj, p['attn_out_norm'], p['ssd_norm'])
    w_out = hooks.weight('w_out', mix)
    h1 = _matmul(mix, w_out, mode='nn', name="out_proj", add=x)
    hn = _rmsnorm_fwd(h1, p['norm_ffn'], "norm_ffn_fwd")
    w_up = hooks.weight('w_up', hn)
    u0 = _matmul(hn, w_up, mode='nn', name="ffn_up", b_owner=True, tn=1408)
    a, u = _ffn_act_fwd(u0, p['ffn_conv_w'], p['ffn_conv_b'])
    w_down = hooks.weight('w_down', a)
    h2 = _matmul(a, w_down, mode='nn', name="ffn_down", add=h1, tk=2816)
    loss, dh2, dh2_16, g_norm_final = _final_loss(h2, p['norm_final'].reshape(1, D_MODEL), tgt)

    g = {}
    da = _matmul(dh2_16, w_down, mode='nt', name="ffn_down_dx", out_dtype=BF16, tn=1408)
    g['w_down'] = _matmul(a, dh2_16, mode='tn', name="ffn_down_dw", tm=1408)
    dep = told('w_down', g['w_down'])
    du0, dcw, dcb = _ffn_act_bwd(u0, u, p['ffn_conv_w'], da)
    g['ffn_conv_w'] = dcw.transpose(1, 0, 2).reshape(FFN_CONV, 2 * D_FF)
    g['ffn_conv_b'] = dcb.transpose(1, 0, 2).reshape(1, 2 * D_FF)
    g['w_up'] = _matmul(hn, du0, mode='tn', name="ffn_up_dw", deps=dep, b_halves=True, owner_major=True,
                        tn=1408)
    dep = told('w_up', g['w_up'])
    dhn = _matmul(du0, w_up, mode='nt', name="ffn_up_dx", out_dtype=BF16, deps=dep, a_halves=True,
                  b_owner=True, tk=2816)
    dh1, dh1_16, g['norm_ffn'] = _rmsnorm_bwd(h1, p['norm_ffn'], dhn, dh2, "norm_ffn_bwd")

    g['w_out'] = _matmul(mix, dh1_16, mode='tn', name="out_proj_dw")
    dep = told('w_out', g['w_out'])
    dmix = _matmul(dh1_16, w_out, mode='nt', name="out_proj_dx", out_dtype=BF16, deps=dep)
    dattn, dy, dz, g['attn_out_norm'], g['ssd_norm'] = _mix_bwd(dmix, attn, y, proj, p['attn_out_norm'],
                                                                p['ssd_norm'])
    dq, dk, dv, dsink = _attn_bwd(proj, sinks, tables, dattn)
    g['sinks'] = dsink[:, :, 0].reshape(1, N_Q_HEADS)
    dxs, dbm, dcm, ddt8, dpar = _ssd_bwd(xbc, xbc_pre, sp, states, dy)
    dpar = dpar[:, :, ::SSD_HEAD_DIM]
    g['dt_bias'] = dpar[:, 0, :].reshape(1, SSD_HEADS)
    g['a_log'] = dpar[:, 1, :].reshape(1, SSD_HEADS)
    g['ssd_d'] = dpar[:, 2, :].reshape(1, SSD_HEADS)
    dxbc, g['ssd_conv_w'], g['ssd_conv_b'] = _ssd_conv_bwd(proj, p['ssd_conv_w'], dxs, dbm, dcm, col0=O_XBC,
                                                           name="ssd_conv_bwd")
    dproj = jnp.concatenate([dq, dk, dv, dz, dxbc], axis=1)
    ddt = ddt8.transpose(2, 0, 1).reshape(t, SSD_HEADS)
    ddt_pad = jnp.pad(ddt, ((0, 0), (0, LANES - SSD_HEADS))).astype(BF16)
    g['w_in'] = (_matmul(dproj, xn, mode='tn', name="in_proj_dw", m_rows=IN_PROJ_WIDTH),
                 _matmul(ddt_pad, xn, mode='tn', name="in_proj_dt_dw"))
    dep = told('w_in', g['w_in'])
    dxn_dt = _matmul(ddt_pad, w_in_dt, mode='nn', name="in_proj_dt_dx", deps=dep)
    dxn = _matmul(dproj, w_in_t, mode='nn', name="in_proj_dx", out_dtype=BF16, add=dxn_dt, k_limit=MAIN_WIDTH,
                  tk=2304)
    dx, _, g['norm_mix'] = _rmsnorm_bwd(x, p['norm_mix'], dxn, dh1, "norm_mix_bwd")
    g['norm_final'] = g_norm_final
    return loss, dx, g


def _pack(arrs):
    flat = jnp.concatenate([a.reshape(-1) for a in arrs])
    n = flat.shape[0]
    rows = -(-n // LANES)
    rows = -(-rows // 8) * 8
    return jnp.pad(flat, (0, rows * LANES - n)).reshape(rows, LANES)


def _unpack(packed, shapes):
    flat = packed.reshape(-1)
    out, off = [], 0
    for s in shapes:
        n = 1
        for d in s:
            n *= d
        out.append(flat[off:off + n].reshape(s))
        off += n
    return out


class _StepHooks:
    def __init__(self, first_deps, weight_in, conv_weights, weight, grad_ready):
        self.first_deps = first_deps
        self.weight_in = weight_in
        self.conv_weights = conv_weights
        self.weight = weight
        self.grad_ready = grad_ready


def kernel(x, norm_mix, w_in, sinks, attn_out_norm, ssd_conv_w, ssd_conv_b, dt_bias, a_log, ssd_d, ssd_norm, w_out, norm_ffn, w_up, ffn_conv_w, ffn_conv_b, w_down, norm_final, loss_target, m_norm_mix, m_w_in, m_sinks, m_attn_out_norm, m_ssd_conv_w, m_ssd_conv_b, m_dt_bias, m_a_log, m_ssd_d, m_ssd_norm, m_w_out, m_norm_ffn, m_w_up, m_ffn_conv_w, m_ffn_conv_b, m_w_down, m_norm_final, v_norm_mix, v_w_in, v_sinks, v_attn_out_norm, v_ssd_conv_w, v_ssd_conv_b, v_dt_bias, v_a_log, v_ssd_d, v_ssd_norm, v_w_out, v_norm_ffn, v_w_up, v_ffn_conv_w, v_ffn_conv_b, v_w_down, v_norm_final):
    args = dict(locals())
    w = {n: args[n] for n in WEIGHTS}
    m = {n: args['m_' + n] for n in WEIGHTS}
    v = {n: args['v_' + n] for n in WEIGHTS}
    xi, yi, ci = _me()
    chip = 2 * xi + yi
    pos = jnp.stack([ci, chip]).astype(jnp.int32)

    conv_shard = _pack([ssd_conv_w[0], ffn_conv_w[0]])
    conv_gather = _push_start([conv_shard], [(N_CHIPS,) + conv_shard.shape], _route_gather, OTHER_CHIPS,
                              name="gather_start_conv")

    def conv_weights(after):
        send_sems, recv_sems, srcs, lands, _ = conv_gather
        (own,), (got,) = _push_wait(send_sems, recv_sems, srcs, lands, after, _route_gather_wait, OTHER_CHIPS,
                                    name="gather_wait_conv")
        whole = lax.dynamic_update_slice(got, own[None], (chip, 0, 0))
        per_chip = [_unpack(whole[j], [ssd_conv_w.shape[1:], ffn_conv_w.shape[1:]]) for j in range(N_CHIPS)]
        return (jnp.concatenate([pc[0] for pc in per_chip], axis=1),
                jnp.concatenate([pc[1] for pc in per_chip], axis=1))

    w_in_t, m_in_t, v_in_t = (jnp.transpose(a[0]) for a in (w_in, m_w_in, v_w_in))
    in_shard = (w_in_t + conv_gather[4][:1, :1]).astype(BF16)
    in_gather = _push_start([in_shard], [(N_CHIPS,) + in_shard.shape], _route_gather_half, OTHER_CHIPS,
                            name="gather_start_w_in")
    gathers = {}
    order = in_gather[4][:1, :1]
    for n, shard in (('w_out', w_out[0]), ('w_up', w_up[0]), ('w_down', w_down[0])):
        shard = (shard + order).astype(BF16)
        gathers[n] = _push_start([shard], [(N_CHIPS,) + shard.shape], _route_gather, OTHER_CHIPS,
                                 name="gather_start_" + n)
        order = gathers[n][4][:1, :1]

    def weight_in(after):
        send_sems, recv_sems, srcs, lands, _ = in_gather
        (own,), (got,) = _push_wait(send_sems, recv_sems, srcs, lands, after, _route_gather_half_wait, OTHER_CHIPS,
                                    name="gather_wait_w_in")
        got, _ = _forward_halves(got)
        full_in_t = lax.dynamic_update_slice(got, own[None], (chip, 0, 0)).reshape(IN_PROJ_WIDTH, D_MODEL)
        w_in_dt = jnp.pad(full_in_t[MAIN_WIDTH:], ((0, LANES - SSD_HEADS), (0, 0)))
        return full_in_t, w_in_dt, ()

    def weight(name, after):
        send_sems, recv_sems, srcs, lands, _ = gathers[name]
        (own,), (got,) = _push_wait(send_sems, recv_sems, srcs, lands, after, _route_gather_wait, OTHER_CHIPS,
                                    name="gather_wait_" + name)
        whole = lax.dynamic_update_slice(got, own[None], (chip, 0, 0))
        return whole if name == 'w_up' else whole.reshape(-1, D_MODEL)

    reductions, exchanging = {}, {}

    def flush(after):
        tokens = []
        for prev in list(exchanging):
            reductions[prev], token = _grad_scatter_start(exchanging.pop(prev), pos, after)
            tokens.append(token)
        return tokens

    def grad_ready(name, value):
        if name == 'w_in':
            main, dtp = value
            value = lax.dynamic_update_slice(main, dtp[:SSD_HEADS], (MAIN_WIDTH, 0))
        g4 = value if value.ndim == 3 else value.reshape(N_CHIPS, -1, value.shape[1])
        tokens = flush(g4)
        exchanging[name], token = _grad_exchange_start(g4, name, cols=(name == 'w_in'))
        return tokens + [token]

    small = {
        'norm_mix': norm_mix, 'sinks': sinks, 'attn_out_norm': attn_out_norm,
        'ssd_conv_b': ssd_conv_b, 'dt_bias': dt_bias, 'a_log': a_log, 'ssd_d': ssd_d, 'ssd_norm': ssd_norm,
        'norm_ffn': norm_ffn, 'ffn_conv_b': ffn_conv_b, 'norm_final': norm_final,
    }
    loss, dx, g = _local_step(x[0], loss_target[0], small,
                              _StepHooks((gathers['w_down'][4],), weight_in, conv_weights, weight, grad_ready))

    small_names = [n for n in WEIGHTS if n not in BIG]
    small_g = [loss[:, :1]] + [g[n] for n in small_names]
    small_shapes = [(1, 1)] + [tuple(a.shape) for a in small_g[1:]]
    reduced, _ = _allreduce_small(_pack(small_g))
    started = flush(reduced)[-1]
    red = _unpack(reduced, small_shapes)
    loss_out = red[0].reshape(())
    gsm = dict(zip(small_names, red[1:]))
    gsm['ssd_conv_w'] = lax.dynamic_slice(gsm['ssd_conv_w'], (0, chip * ssd_conv_w.shape[2]),
                                          (SSD_CONV, ssd_conv_w.shape[2]))
    gsm['ffn_conv_w'] = lax.dynamic_slice(gsm['ffn_conv_w'], (0, chip * ffn_conv_w.shape[2]),
                                          (FFN_CONV, ffn_conv_w.shape[2]))

    grads, deltas, new_m, new_v = {}, {}, {}, {}
    after = started
    shares = {}
    for n in ('w_down', 'w_up', 'w_out'):
        shares[n], after = _grad_sum_and_share(reductions[n], pos, after)
    for n in ('w_down', 'w_up', 'w_out', 'w_in'):
        if n == 'w_in':
            shares[n], after = _grad_sum_and_share(reductions[n], pos, after)
        mine, theirs = _grad_share_wait(shares[n], after)
        if n == 'w_in':
            outs = _adamw_halves(w_in_t, mine, theirs, m_in_t, v_in_t, pos, name="adamw_" + n, cols=True)
            outs = [jnp.transpose(o) for o in outs]
        else:
            outs = _adamw_halves(w[n][0], mine, theirs, m[n][0], v[n][0], pos, name="adamw_" + n)
        after = outs[1]
        grads[n], deltas[n], new_m[n], new_v[n] = [o[None] for o in outs]
    shapes = [tuple(w[n].shape) for n in small_names]
    gp = _pack([gsm[n] for n in small_names])
    d, m2, v2 = _adamw(_pack([w[n] for n in small_names]), gp, _pack([m[n] for n in small_names]),
                       _pack([v[n] for n in small_names]), name="adamw_small")
    for n, gg, dd, mm, vv in zip(small_names, _unpack(gp, shapes), _unpack(d, shapes), _unpack(m2, shapes),
                                 _unpack(v2, shapes)):
        grads[n], deltas[n], new_m[n], new_v[n] = gg, dd, mm, vv

    return (loss_out, dx[None], *[grads[n] for n in WEIGHTS], *[deltas[n] for n in WEIGHTS],
            *[new_m[n] for n in WEIGHTS], *[new_v[n] for n in WEIGHTS])
```

```python
import functools

import jax
import jax.numpy as jnp
from jax import lax
from jax.experimental import pallas as pl
from jax.experimental.pallas import tpu as pltpu

F32 = jnp.float32
BF16 = jnp.bfloat16

D_MODEL = 2048
N_Q_HEADS = 32
N_KV_HEADS = 8
HEAD_DIM = 64
WINDOW = 128
ATTN_BLOCK = 128
ROT_DIM = 16
ROPE_THETA = 500000.0
SSD_HEADS = 32
SSD_HEAD_DIM = 64
SSD_INNER = 2048
SSD_GROUPS = 8
SSD_STATE = 128
SSD_CONV = 4
SSD_CHUNK = 128
ATTN_WIDTH = 2048
KV_WIDTH = 512
BC_WIDTH = 1024
CONV_CH = 4096
IN_PROJ_WIDTH = 9248
MAIN_WIDTH = 9216
D_FF = 5632
FFN_CONV = 3
EPS = 1e-6
O_Q, O_K, O_V, O_Z, O_XBC, O_DT = 0, 2048, 2560, 3072, 5120, 9216

ADAM_LR = 0.001
ADAM_B1 = 0.9
ADAM_B2 = 0.999
ADAM_EPS = 1e-08
ADAM_WD = 0.01
ADAM_STEP = 10

N_CHIPS = 4
NEG = -1e30
LANES = 128
VMEM_LIMIT = 48 * 1024 * 1024
MESH = pl.DeviceIdType.MESH
HBM_SPEC = pl.BlockSpec(memory_space=pltpu.HBM)
TOKEN = jax.ShapeDtypeStruct((8, LANES), F32)

WEIGHTS = ['norm_mix', 'w_in', 'sinks', 'attn_out_norm', 'ssd_conv_w', 'ssd_conv_b', 'dt_bias', 'a_log', 'ssd_d',
           'ssd_norm', 'w_out', 'norm_ffn', 'w_up', 'ffn_conv_w', 'ffn_conv_b', 'w_down', 'norm_final']
BIG = ['w_in', 'w_out', 'w_up', 'w_down']


def _cp(sem=None, vmem=VMEM_LIMIT):
    kw = {'vmem_limit_bytes': vmem}
    if sem is not None:
        kw['dimension_semantics'] = sem
    return pltpu.CompilerParams(**kw)


def _tile(n, pref):
    if n <= pref:
        return n
    t = (pref // LANES) * LANES
    while t > LANES and n % t:
        t -= LANES
    assert n % t == 0, (n, pref)
    return t


def _rows(n, pref):
    t = min(n, pref)
    while n % t:
        t -= 8
    if 4 * t < pref:
        t = pref
        while n % t:
            t += 8
    return t


def _iota(shape, dim):
    return lax.broadcasted_iota(jnp.int32, shape, dim)


def _dot(a, b, mode='nn'):
    dn = {'nn': (((1,), (0,)), ((), ())), 'nt': (((1,), (1,)), ((), ())), 'tn': (((0,), (0,)), ((), ()))}[mode]
    return lax.dot_general(a.astype(BF16), b.astype(BF16), dn, preferred_element_type=F32)


def _dot_exact(a, b):
    return lax.dot_general(a, b, (((1,), (0,)), ((), ())), precision=lax.Precision.HIGHEST,
                           preferred_element_type=F32)


def _sigmoid(x):
    return 1.0 / (1.0 + jnp.exp(-x))


def _softplus(x):
    return jnp.maximum(x, 0.0) + jnp.log(1.0 + jnp.exp(-jnp.abs(x)))


def _matmul(a, b, *, mode, name, out_dtype=F32, add=None, deps=(), tm=1024, tn=1024, tk=2048,
            a_halves=False, b_halves=False, b_owner=False, owner_major=False, n_limit=None, k_limit=None,
            m_rows=None):
    ash, bsh = (a.shape[1:] if a_halves else a.shape), (b.shape[1:] if (b_halves or b_owner) else b.shape)
    if mode == 'nn':
        (m, k), (k2, n) = ash, bsh
    elif mode == 'nt':
        (m, k), (n, k2) = ash, bsh
    else:
        (k, m), (k2, n) = ash, bsh
    if n_limit is not None:
        assert mode == 'nt' and n_limit <= n
        n = n_limit
    if k_limit is not None:
        assert mode == 'nn' and k_limit <= k2
        k2 = k_limit
    if a_halves:
        assert mode == 'nt'
        k = 2 * k
    if b_halves:
        assert mode == 'tn'
        n = 2 * n
    if b_owner:
        assert mode in ('nn', 'nt')
        if mode == 'nn':
            n = 4 * n
        else:
            k2 = 4 * k2
    assert k == k2, (a.shape, b.shape, mode)
    tm = _tile(m, tm)
    tn = _tile(n // 4 if (owner_major or (b_owner and mode == 'nn')) else (n // 2 if b_halves else n), tn)
    tk = _tile(k // 4 if (b_owner and mode == 'nt') else (k // 2 if a_halves else k), tk)
    nk = k // tk
    has_add = add is not None
    assert not (has_add and owner_major)

    def body(*refs):
        a_ref, b_ref = refs[:2]
        add_ref = refs[2] if has_add else None

        def finish(r, o_ref):
            if has_add:
                r = r + add_ref[...].astype(F32)
            o_ref[...] = r.astype(out_dtype)

        if nk == 1:
            finish(_dot(a_ref[...], b_ref[...], mode), refs[-1])
            return
        o_ref, acc = refs[-2:]
        kk = pl.program_id(2)

        @pl.when(kk == 0)
        def _():
            acc[...] = _dot(a_ref[...], b_ref[...], mode)

        @pl.when((kk > 0) & (kk < nk - 1))
        def _():
            acc[...] += _dot(a_ref[...], b_ref[...], mode)

        @pl.when(kk == nk - 1)
        def _():
            finish(acc[...] + _dot(a_ref[...], b_ref[...], mode), o_ref)

    if mode == 'tn':
        a_spec = pl.BlockSpec((tk, tm), lambda i, j, kk: (kk, i))
    elif a_halves:
        nkh = nk // 2
        a_spec = pl.BlockSpec((None, tm, tk), lambda i, j, kk: (kk // nkh, i, kk % nkh))
    else:
        a_spec = pl.BlockSpec((tm, tk), lambda i, j, kk: (i, kk))
    if mode == 'nt' and b_owner:
        nkq = nk // 4
        b_spec = pl.BlockSpec((None, tn, tk), lambda i, j, kk: (kk // nkq, j, kk % nkq))
    elif mode == 'nt':
        b_spec = pl.BlockSpec((tn, tk), lambda i, j, kk: (j, kk))
    elif b_owner:
        njq = (n // 4) // tn
        b_spec = pl.BlockSpec((None, tk, tn), lambda i, j, kk: (j // njq, kk, j % njq))
    elif b_halves:
        njh = (n // 2) // tn
        b_spec = pl.BlockSpec((None, tk, tn), lambda i, j, kk: (j // njh, kk, j % njh))
    else:
        b_spec = pl.BlockSpec((tk, tn), lambda i, j, kk: (kk, j))
    if owner_major:
        njo = (n // 4) // tn
        o_spec = pl.BlockSpec((None, tm, tn), lambda i, j, kk: (j // njo, i, j % njo))
        out_shape = jax.ShapeDtypeStruct((N_CHIPS, m, n // 4), out_dtype)
    else:
        o_spec = pl.BlockSpec((tm, tn), lambda i, j, kk: (i, j))
        out_shape = jax.ShapeDtypeStruct((m if m_rows is None else m_rows, n), out_dtype)
    dep_spec = pl.BlockSpec((8, LANES), lambda i, j, kk: (0, 0))
    in_specs = [a_spec, b_spec] + ([pl.BlockSpec((tm, tn), lambda i, j, kk: (i, j))] if has_add else [])
    in_specs += [dep_spec] * len(deps)
    args = (a, b) + ((add,) if has_add else ()) + tuple(deps)
    return pl.pallas_call(
        body, name=name, grid=(m // tm, n // tn, nk), in_specs=in_specs, out_specs=o_spec, out_shape=out_shape,
        scratch_shapes=[pltpu.VMEM((tm, tn), F32)] if nk > 1 else [],
        compiler_params=_cp(("parallel", "parallel", "arbitrary")))(*args)


def _rmsnorm_fwd(x, g, name, deps=()):
    t, d = x.shape
    tb = _rows(t, 256)

    def body(x_ref, g_ref, *rest):
        o_ref = rest[-1]
        xv = x_ref[...]
        r = lax.rsqrt(jnp.mean(xv * xv, axis=-1, keepdims=True) + EPS)
        o_ref[...] = (xv * r * g_ref[...]).astype(BF16)

    dep_spec = pl.BlockSpec((8, LANES), lambda i: (0, 0))
    return pl.pallas_call(
        body, name=name, grid=(t // tb,),
        in_specs=[pl.BlockSpec((tb, d), lambda i: (i, 0)), pl.BlockSpec((1, d), lambda i: (0, 0))]
        + [dep_spec] * len(deps),
        out_specs=pl.BlockSpec((tb, d), lambda i: (i, 0)), out_shape=jax.ShapeDtypeStruct((t, d), BF16),
        compiler_params=_cp(("parallel",)))(x, g, *deps)


def _rmsnorm_bwd(x, g, dy, res, name, deps=()):
    t, d = x.shape
    tb = _rows(t, 256)

    def body(x_ref, g_ref, dy_ref, res_ref, *rest):
        dx_ref, dx16_ref, dg_ref = rest[-3:]
        i = pl.program_id(0)
        xv = x_ref[...]
        dyv = dy_ref[...].astype(F32)
        r = lax.rsqrt(jnp.mean(xv * xv, axis=-1, keepdims=True) + EPS)
        u = dyv * g_ref[...]
        dx = r * u - xv * (r * r * r * jnp.mean(u * xv, axis=-1, keepdims=True)) + res_ref[...]
        dx_ref[...] = dx
        dx16_ref[...] = dx.astype(BF16)
        part = jnp.sum(dyv * xv * r, axis=0, keepdims=True)

        @pl.when(i == 0)
        def _():
            dg_ref[...] = part

        @pl.when(i > 0)
        def _():
            dg_ref[...] += part

    row = pl.BlockSpec((tb, d), lambda i: (i, 0))
    vec = pl.BlockSpec((1, d), lambda i: (0, 0))
    return pl.pallas_call(
        body, name=name, grid=(t // tb,),
        in_specs=[row, vec, row, row] + [pl.BlockSpec((8, LANES), lambda i: (0, 0))] * len(deps),
        out_specs=[row, row, vec],
        out_shape=[jax.ShapeDtypeStruct((t, d), F32), jax.ShapeDtypeStruct((t, d), BF16),
                   jax.ShapeDtypeStruct((1, d), F32)],
        compiler_params=_cp(("arbitrary",)))(x, g, dy, res, *deps)


def _final_loss(h, g, tgt):
    t, d = h.shape
    tb = _rows(t, 256)

    def body(h_ref, g_ref, t_ref, loss_ref, dh_ref, dh16_ref, dg_ref):
        i = pl.program_id(0)
        hv = h_ref[...]
        gv = g_ref[...]
        r = lax.rsqrt(jnp.mean(hv * hv, axis=-1, keepdims=True) + EPS)
        y = hv * r * gv
        diff = y - t_ref[...]
        lpart = jnp.sum(jnp.sum(diff * diff, axis=1, keepdims=True), axis=0, keepdims=True) * (0.5 / d)
        dy = diff * (1.0 / d)
        u = dy * gv
        dh = r * u - hv * (r * r * r * jnp.mean(u * hv, axis=-1, keepdims=True))
        dh_ref[...] = dh
        dh16_ref[...] = dh.astype(BF16)
        gpart = jnp.sum(dy * hv * r, axis=0, keepdims=True)
        lrow = jnp.broadcast_to(lpart, (1, LANES))

        @pl.when(i == 0)
        def _():
            loss_ref[...] = lrow
            dg_ref[...] = gpart

        @pl.when(i > 0)
        def _():
            loss_ref[...] += lrow
            dg_ref[...] += gpart

    row = pl.BlockSpec((tb, d), lambda i: (i, 0))
    vec = pl.BlockSpec((1, d), lambda i: (0, 0))
    return pl.pallas_call(
        body, name="final_loss", grid=(t // tb,), in_specs=[row, vec, row],
        out_specs=[pl.BlockSpec((1, LANES), lambda i: (0, 0)), row, row, vec],
        out_shape=[jax.ShapeDtypeStruct((1, LANES), F32), jax.ShapeDtypeStruct((t, d), F32),
                   jax.ShapeDtypeStruct((t, d), BF16), jax.ShapeDtypeStruct((1, d), F32)],
        compiler_params=_cp(("arbitrary",)))(h, g, tgt)


def _rope_tables(t):
    pos = jnp.arange(t, dtype=F32)
    inv = 1.0 / (ROPE_THETA ** (jnp.arange(0, ROT_DIM, 2, dtype=F32) / ROT_DIM))
    ang = pos[:, None] * inv[None, :]
    cos, sin = jnp.cos(ang), jnp.sin(ang)
    half = ROT_DIM // 2
    rest = HEAD_DIM - ROT_DIM
    c = jnp.concatenate([cos, cos, jnp.ones((t, rest), F32)], axis=1)
    s1 = jnp.concatenate([-sin, jnp.zeros((t, half + rest), F32)], axis=1)
    s2 = jnp.concatenate([jnp.zeros((t, half), F32), sin, jnp.zeros((t, rest), F32)], axis=1)
    return jnp.concatenate([jnp.tile(v, (1, LANES // HEAD_DIM)) for v in (c, s1, s2)], axis=1)


def _split_tables(tab):
    return tab[:, :LANES], tab[:, LANES:2 * LANES], tab[:, 2 * LANES:]


def _rope(x, c, s1, s2):
    half = ROT_DIM // 2
    return x * c + pltpu.roll(x, LANES - half, 1) * s1 + pltpu.roll(x, half, 1) * s2


def _rope_t(g, c, s1, s2):
    half = ROT_DIM // 2
    return g * c + pltpu.roll(g * s1, half, 1) + pltpu.roll(g * s2, LANES - half, 1)


def _band_masks(i, heads):
    n = heads * ATTN_BLOCK
    q = jnp.bitwise_and(_iota((n, ATTN_BLOCK), 0), ATTN_BLOCK - 1)
    j = _iota((n, ATTN_BLOCK), 1)
    upper = j > q
    return upper, upper & (j < jnp.where(i > 0, 0, ATTN_BLOCK))


def _fold_band(full, upper):
    return jnp.where(upper, full[:, :ATTN_BLOCK], full[:, ATTN_BLOCK:])


def _unfold_band(band, upper):
    return jnp.concatenate([jnp.where(upper, band, 0.0), jnp.where(upper, 0.0, band)], axis=1)


def _half_masks():
    lane = _iota((1, LANES), 1)
    return [(lane < HEAD_DIM).astype(F32), (lane >= HEAD_DIM).astype(F32)]


def _stack_heads(blocks, hm, j):
    pieces = []
    for r in range(4):
        qb, half = (4 * j + r) // 2, (4 * j + r) % 2
        piece = blocks[qb] * hm[half]
        if half != j:
            piece = pltpu.roll(piece, HEAD_DIM, 1)
        pieces.append(piece)
    return jnp.concatenate(pieces, axis=0)


def _unstack_heads(stacked, j):
    out = []
    for qb in (2 * j, 2 * j + 1):
        acc = None
        for half in range(2):
            r = 2 * qb + half - 4 * j
            piece = stacked[r * ATTN_BLOCK:(r + 1) * ATTN_BLOCK]
            if half != j:
                piece = pltpu.roll(piece, HEAD_DIM, 1)
            acc = piece if acc is None else acc + piece
        out.append((qb, acc))
    return out


def _sink_column(sink_ref, base):
    return jnp.concatenate([jnp.full((ATTN_BLOCK, 1), sink_ref[base + r], F32) for r in range(4)], axis=0)


def _attn_specs(nb_clamp):
    blk = ATTN_BLOCK
    kb, vb = O_K // LANES, O_V // LANES

    def cur(i):
        return jnp.minimum(i, nb_clamp)

    def prev(i):
        return jnp.maximum(jnp.minimum(i, nb_clamp + 1) - 1, 0)

    q = pl.BlockSpec((blk, 512), lambda p, i: (cur(i), p))
    kc = pl.BlockSpec((blk, LANES), lambda p, i: (cur(i), kb + p))
    kp = pl.BlockSpec((blk, LANES), lambda p, i: (prev(i), kb + p))
    vc = pl.BlockSpec((blk, LANES), lambda p, i: (cur(i), vb + p))
    vp = pl.BlockSpec((blk, LANES), lambda p, i: (prev(i), vb + p))
    tc = pl.BlockSpec((blk, 3 * LANES), lambda p, i: (cur(i), 0))
    tp = pl.BlockSpec((blk, 3 * LANES), lambda p, i: (prev(i), 0))
    return q, kc, kp, vc, vp, tc, tp


def _attn_fwd(proj, sinks, tables):
    t = proj.shape[0]
    nb = t // ATTN_BLOCK
    scale = HEAD_DIM ** -0.5

    def body(sink_ref, q_ref, kc_ref, kp_ref, vc_ref, vp_ref, tc_ref, tp_ref, o_ref):
        p = pl.program_id(0)
        i = pl.program_id(1)
        cc, s1c, s2c = _split_tables(tc_ref[...])
        kband = jnp.concatenate([_rope(kp_ref[...], *_split_tables(tp_ref[...])),
                                 _rope(kc_ref[...], cc, s1c, s2c)], axis=0).astype(BF16)
        vband = jnp.concatenate([vp_ref[...], vc_ref[...]], axis=0)
        hm = _half_masks()
        vsel = [(vband * hm[j]).astype(BF16) for j in range(2)]
        upper, dropped = _band_masks(i, 1)
        qr = [_rope(q_ref[:, qb * LANES:(qb + 1) * LANES], cc, s1c, s2c) for qb in range(4)]

        def scores(hh):
            qb, half, j = hh // 2, hh % 2, hh // 4
            qs = qr[qb] * hm[half]
            if half != j:
                qs = pltpu.roll(qs, HEAD_DIM, 1)
            return _dot(qs, kband, 'nt')

        ahead = scores(0)
        acc = None
        for hh in range(8):
            qb, half, j = hh // 2, hh % 2, hh // 4
            raw = ahead
            if hh + 1 < 8:
                ahead = scores(hh + 1)
            s = jnp.where(dropped, NEG, _fold_band(raw, upper) * scale)
            sink = sink_ref[p * 8 + hh]
            m = jnp.maximum(jnp.max(s, axis=1, keepdims=True), sink)
            pe = jnp.exp(s - m)
            den = jnp.sum(pe, axis=1, keepdims=True) + jnp.exp(sink - m)
            o = _dot(_unfold_band(pe / den, upper), vsel[j])
            if half != j:
                o = pltpu.roll(o, HEAD_DIM, 1)
            acc = o if half == 0 else acc + o
            if half == 1:
                o_ref[:, qb * LANES:(qb + 1) * LANES] = acc

    q, kc, kp, vc, vp, tc, tp = _attn_specs(nb - 1)
    smem = pl.BlockSpec(memory_space=pltpu.SMEM)
    return pl.pallas_call(
        body, name="attn_fwd", grid=(4, nb),
        in_specs=[smem, q, kc, kp, vc, vp, tc, tp],
        out_specs=pl.BlockSpec((ATTN_BLOCK, 512), lambda p, i: (i, p)),
        out_shape=jax.ShapeDtypeStruct((t, ATTN_WIDTH), F32),
        compiler_params=_cp(("parallel", "arbitrary")))(sinks, proj, proj, proj, proj, proj, tables, tables)


def _attn_bwd(proj, sinks, tables, dout):
    t = proj.shape[0]
    nb = t // ATTN_BLOCK
    scale = HEAD_DIM ** -0.5

    def body(sink_ref, q_ref, kc_ref, kp_ref, vc_ref, vp_ref, tc_ref, tp_ref,
             do_ref, dq_ref, dk_ref, dv_ref, ds_ref, carry_k, carry_v):
        p = pl.program_id(0)
        i = pl.program_id(1)
        ptab = _split_tables(tp_ref[...])

        @pl.when(i == 0)
        def _():
            carry_k[...] = jnp.zeros_like(carry_k)
            carry_v[...] = jnp.zeros_like(carry_v)
            ds_ref[...] = jnp.zeros_like(ds_ref)

        @pl.when(i < nb)
        def _():
            cc, s1c, s2c = _split_tables(tc_ref[...])
            kband = jnp.concatenate([_rope(kp_ref[...], *ptab), _rope(kc_ref[...], cc, s1c, s2c)], axis=0)
            vband = jnp.concatenate([vp_ref[...], vc_ref[...]], axis=0)
            hm = _half_masks()
            kband16 = kband.astype(BF16)
            vband16 = vband.astype(BF16)
            upper, dropped = _band_masks(i, 4)
            dkb = jnp.zeros((2 * ATTN_BLOCK, LANES), F32)
            dvb = jnp.zeros((2 * ATTN_BLOCK, LANES), F32)
            row8 = _iota((8, LANES), 0)
            dsink = jnp.zeros((8, LANES), F32)
            qr = [_rope(q_ref[:, qb * LANES:(qb + 1) * LANES], cc, s1c, s2c) for qb in range(4)]
            dob = [do_ref[:, qb * LANES:(qb + 1) * LANES] for qb in range(4)]
            for j in range(2):
                qst = _stack_heads(qr, hm, j).astype(BF16)
                dost = _stack_heads(dob, hm, j).astype(BF16)
                s = jnp.where(dropped, NEG, _fold_band(_dot(qst, kband16, 'nt'), upper) * scale)
                sink = _sink_column(sink_ref, p * 8 + 4 * j)
                m = jnp.maximum(jnp.max(s, axis=1, keepdims=True), sink)
                pe = jnp.exp(s - m)
                psink = jnp.exp(sink - m)
                den = jnp.sum(pe, axis=1, keepdims=True) + psink
                pr = pe / den
                dvb = dvb + _dot(_unfold_band(pr, upper).T, dost)
                dp = _fold_band(_dot(dost, vband16, 'nt'), upper)
                delta = jnp.sum(pr * dp, axis=1, keepdims=True)
                dsc = _unfold_band(pr * (dp - delta) * scale, upper)
                dsk = psink / den * delta
                for r in range(4):
                    part = jnp.sum(dsk[r * ATTN_BLOCK:(r + 1) * ATTN_BLOCK])
                    dsink = dsink + jnp.where(row8 == 4 * j + r, -part, 0.0)
                for qb, dqb in _unstack_heads(_dot(dsc, kband * hm[j]), j):
                    dq_ref[:, qb * LANES:(qb + 1) * LANES] = _rope_t(dqb, cc, s1c, s2c).astype(BF16)
                dkb = dkb + _dot(dsc.T, qst)
            ds_ref[0] += dsink
            dk_ref[...] = _rope_t(carry_k[...] + dkb[:ATTN_BLOCK], *ptab).astype(BF16)
            dv_ref[...] = (carry_v[...] + dvb[:ATTN_BLOCK]).astype(BF16)
            carry_k[...] = dkb[ATTN_BLOCK:]
            carry_v[...] = dvb[ATTN_BLOCK:]

        @pl.when(i == nb)
        def _():
            dk_ref[...] = _rope_t(carry_k[...], *ptab).astype(BF16)
            dv_ref[...] = carry_v[...].astype(BF16)

    q, kc, kp, vc, vp, tc, tp = _attn_specs(nb - 1)
    smem = pl.BlockSpec(memory_space=pltpu.SMEM)
    qblk = pl.BlockSpec((ATTN_BLOCK, 512), lambda p, i: (jnp.minimum(i, nb - 1), p))
    kvout = pl.BlockSpec((ATTN_BLOCK, LANES), lambda p, i: (jnp.maximum(i - 1, 0), p))
    return pl.pallas_call(
        body, name="attn_bwd", grid=(4, nb + 1),
        in_specs=[smem, q, kc, kp, vc, vp, tc, tp, qblk],
        out_specs=[qblk, kvout, kvout, pl.BlockSpec((1, 8, LANES), lambda p, i: (p, 0, 0))],
        out_shape=[jax.ShapeDtypeStruct((t, ATTN_WIDTH), BF16), jax.ShapeDtypeStruct((t, KV_WIDTH), BF16),
                   jax.ShapeDtypeStruct((t, KV_WIDTH), BF16), jax.ShapeDtypeStruct((4, 8, LANES), F32)],
        scratch_shapes=[pltpu.VMEM((ATTN_BLOCK, LANES), F32), pltpu.VMEM((ATTN_BLOCK, LANES), F32)],
        compiler_params=_cp(("parallel", "arbitrary")))(sinks, proj, proj, proj, proj, proj, tables, tables, dout)


def _shift_rows(x, prev8, j):
    n, c = x.shape
    r = pltpu.roll(x.reshape(n // 8, 8, c), j, 1)
    before = pltpu.roll(prev8, j, 0)[None]
    if n > 8:
        before = jnp.concatenate([before, r[:-1]], axis=0)
    return jnp.where(_iota((1, 8, 1), 1) < j, before, r).reshape(n, c)


def _shift_rows_up(x, next8, j):
    n, c = x.shape
    r = pltpu.roll(x.reshape(n // 8, 8, c), 8 - j, 1)
    after = pltpu.roll(next8, 8 - j, 0)[None]
    if n > 8:
        after = jnp.concatenate([r[1:], after], axis=0)
    return jnp.where(_iota((1, 8, 1), 1) >= 8 - j, after, r).reshape(n, c)


def _conv_apply(x, prev8, w, b, taps):
    u = b + x * w[taps - 1:taps]
    for j in range(1, taps):
        u = u + _shift_rows(x, prev8, j) * w[taps - 1 - j:taps - j]
    return u


def _conv_grads(du, du_next8, x, w, taps):
    dx = du * w[taps - 1:taps]
    rowk = _iota((taps, 1), 0)
    dw = jnp.where(rowk == taps - 1, jnp.sum(du * x, axis=0, keepdims=True), 0.0)
    for j in range(1, taps):
        ahead = _shift_rows_up(du, du_next8, j)
        dx = dx + ahead * w[taps - 1 - j:taps - j]
        dw = dw + jnp.where(rowk == taps - 1 - j, jnp.sum(ahead * x, axis=0, keepdims=True), 0.0)
    return dx, dw, jnp.sum(du, axis=0, keepdims=True)


def _conv_specs(tb, tc, col0, t):
    c0 = col0 // tc
    cur = pl.BlockSpec((tb, tc), lambda j, i: (i, c0 + j))
    prev = pl.BlockSpec((8, tc), lambda j, i: (jnp.maximum(i * (tb // 8) - 1, 0), c0 + j))
    nxt = pl.BlockSpec((8, tc), lambda j, i: (jnp.minimum((i + 1) * (tb // 8), t // 8 - 1), c0 + j))
    return cur, prev, nxt


def _conv_silu_fwd(x, w, b, *, col0, width, name):
    t = x.shape[0]
    taps = w.shape[0]
    tb, tc = _rows(t, 512), _tile(width, 1024)
    assert col0 % tc == 0

    def body(x_ref, xp_ref, w_ref, b_ref, o_ref, u_ref):
        i = pl.program_id(1)
        prev8 = jnp.where(i > 0, xp_ref[...], 0.0)
        u = _conv_apply(x_ref[...], prev8, w_ref[...], b_ref[...], taps)
        u_ref[...] = u
        o_ref[...] = u * _sigmoid(u)

    cur, prev, _ = _conv_specs(tb, tc, col0, t)
    par = pl.BlockSpec((taps, tc), lambda j, i: (0, j))
    bias = pl.BlockSpec((1, tc), lambda j, i: (0, j))
    out = pl.BlockSpec((tb, tc), lambda j, i: (i, j))
    shp = jax.ShapeDtypeStruct((t, width), F32)
    return pl.pallas_call(
        body, name=name, grid=(width // tc, t // tb), in_specs=[cur, prev, par, bias], out_specs=[out, out],
        out_shape=[shp, shp], compiler_params=_cp(("parallel", "parallel")))(x, x, w, b)


def _dsilu(u):
    sg = _sigmoid(u)
    return sg * (1.0 + u * (1.0 - sg))


def _ssd_conv_bwd(x, w, dxs, dbm, dcm, *, col0, name):
    t = x.shape[0]
    taps = w.shape[0]
    tb, tc = _rows(t, 512), BC_WIDTH
    nrow, ncol = t // tb, CONV_CH // tc
    c0 = col0 // tc

    def body(x_ref, w_ref, xs_ref, xsn_ref, bm_ref, bmn_ref, cm_ref, cmn_ref, dx_ref, dw_ref, db_ref):
        i = pl.program_id(0)
        j = pl.program_id(1)

        def run(du_ref, dun_ref):
            next8 = jnp.where(i < nrow - 1, dun_ref[...], 0.0)
            dx, dwv, dbv = _conv_grads(du_ref[...], next8, x_ref[...], w_ref[...], taps)
            dx_ref[...] = dx.astype(BF16)

            @pl.when(i == 0)
            def _():
                dw_ref[j] = dwv
                db_ref[j] = dbv

            @pl.when(i > 0)
            def _():
                dw_ref[j] += dwv
                db_ref[j] += dbv

        pl.when(j < 2)(lambda: run(xs_ref, xsn_ref))
        pl.when(j == 2)(lambda: run(bm_ref, bmn_ref))
        pl.when(j == 3)(lambda: run(cm_ref, cmn_ref))

    def nxt_row(i):
        return jnp.minimum((i + 1) * (tb // 8), t // 8 - 1)

    xs_col = lambda j: jnp.minimum(j, SSD_INNER // tc - 1)
    in_specs = [pl.BlockSpec((tb, tc), lambda i, j: (i, c0 + j)), pl.BlockSpec((taps, tc), lambda i, j: (0, j)),
                pl.BlockSpec((tb, tc), lambda i, j: (i, xs_col(j))),
                pl.BlockSpec((8, tc), lambda i, j: (nxt_row(i), xs_col(j))),
                pl.BlockSpec((tb, tc), lambda i, j: (i, 0)), pl.BlockSpec((8, tc), lambda i, j: (nxt_row(i), 0)),
                pl.BlockSpec((tb, tc), lambda i, j: (i, 0)), pl.BlockSpec((8, tc), lambda i, j: (nxt_row(i), 0))]
    dx, dw, db = pl.pallas_call(
        body, name=name, grid=(nrow, ncol), in_specs=in_specs,
        out_specs=[pl.BlockSpec((tb, tc), lambda i, j: (i, j)),
                   pl.BlockSpec((ncol, taps, tc), lambda i, j: (0, 0, 0)),
                   pl.BlockSpec((ncol, 1, tc), lambda i, j: (0, 0, 0))],
        out_shape=[jax.ShapeDtypeStruct((t, CONV_CH), BF16), jax.ShapeDtypeStruct((ncol, taps, tc), F32),
                   jax.ShapeDtypeStruct((ncol, 1, tc), F32)],
        compiler_params=_cp(("arbitrary", "arbitrary")))(x, w, dxs, dxs, dbm, dbm, dcm, dcm)
    return dx, dw.transpose(1, 0, 2).reshape(taps, CONV_CH), db.transpose(1, 0, 2).reshape(1, CONV_CH)


def _ffn_specs(tb, tc, t):
    nc = D_FF // tc

    def cur(half):
        return pl.BlockSpec((tb, tc), lambda j, i: (i, half * nc + j))

    def prev(half):
        return pl.BlockSpec((8, tc), lambda j, i: (jnp.maximum(i * (tb // 8) - 1, 0), half * nc + j))

    def nxt(half):
        return pl.BlockSpec((8, tc), lambda j, i: (jnp.minimum((i + 1) * (tb // 8), t // 8 - 1), half * nc + j))

    def par(rows, half):
        return pl.BlockSpec((rows, tc), lambda j, i: (0, half * nc + j))

    return cur, prev, nxt, par


def _ffn_act_fwd(u0, w, b):
    t = u0.shape[0]
    tb, tc = _rows(t, 512), _tile(D_FF, 1408)
    cur, prev, _, par = _ffn_specs(tb, tc, t)

    def body(g_ref, gp_ref, v_ref, vp_ref, wg_ref, wv_ref, bg_ref, bv_ref, o_ref, u_ref):
        i = pl.program_id(1)
        ug = _conv_apply(g_ref[...], jnp.where(i > 0, gp_ref[...], 0.0), wg_ref[...], bg_ref[...], FFN_CONV)
        uv = _conv_apply(v_ref[...], jnp.where(i > 0, vp_ref[...], 0.0), wv_ref[...], bv_ref[...], FFN_CONV)
        o_ref[...] = (ug * _sigmoid(ug) * uv).astype(BF16)
        u_ref[0] = ug
        u_ref[1] = uv

    return pl.pallas_call(
        body, name="ffn_act_fwd", grid=(D_FF // tc, t // tb),
        in_specs=[cur(0), prev(0), cur(1), prev(1), par(FFN_CONV, 0), par(FFN_CONV, 1), par(1, 0), par(1, 1)],
        out_specs=[pl.BlockSpec((tb, tc), lambda j, i: (i, j)), pl.BlockSpec((2, tb, tc), lambda j, i: (0, i, j))],
        out_shape=[jax.ShapeDtypeStruct((t, D_FF), BF16), jax.ShapeDtypeStruct((2, t, D_FF), F32)],
        compiler_params=_cp(("parallel", "parallel")))(u0, u0, u0, u0, w, w, b, b)


def _ffn_act_bwd(u0, u, w, da):
    t = u0.shape[0]
    tb, tc = _rows(t, 256), _tile(D_FF, 1408)
    nrow = t // tb
    taps = FFN_CONV
    cur, _, _, par = _ffn_specs(tb, tc, t)

    def dact(ug, uv, dav):
        sg = _sigmoid(ug)
        return dav * uv * (sg * (1.0 + ug * (1.0 - sg))), dav * ug * sg

    def body(g_ref, v_ref, u_ref, un_ref, wg_ref, wv_ref, da_ref, dan_ref, dx_ref, dw_ref, db_ref):
        i = pl.program_id(1)
        dug, duv = dact(u_ref[0], u_ref[1], da_ref[...].astype(F32))
        dan = jnp.where(i < nrow - 1, dan_ref[...].astype(F32)[:8], 0.0)
        dugn, duvn = dact(un_ref[0], un_ref[1], dan)
        dxg, dwg, dbg = _conv_grads(dug, dugn, g_ref[...], wg_ref[...], taps)
        dxv, dwv, dbv = _conv_grads(duv, duvn, v_ref[...], wv_ref[...], taps)
        dx_ref[0] = dxg.astype(BF16)
        dx_ref[1] = dxv.astype(BF16)

        @pl.when(i == 0)
        def _():
            dw_ref[0] = dwg
            dw_ref[1] = dwv
            db_ref[0] = dbg
            db_ref[1] = dbv

        @pl.when(i > 0)
        def _():
            dw_ref[0] += dwg
            dw_ref[1] += dwv
            db_ref[0] += dbg
            db_ref[1] += dbv

    both = pl.BlockSpec((2, tb, tc), lambda j, i: (0, i, j))
    both_nxt = pl.BlockSpec((2, 8, tc), lambda j, i: (0, jnp.minimum((i + 1) * (tb // 8), t // 8 - 1), j))
    da_cur = pl.BlockSpec((tb, tc), lambda j, i: (i, j))
    da_nxt = pl.BlockSpec((16, tc), lambda j, i: (jnp.minimum((i + 1) * (tb // 16), t // 16 - 1), j))
    return pl.pallas_call(
        body, name="ffn_act_bwd", grid=(D_FF // tc, nrow),
        in_specs=[cur(0), cur(1), both, both_nxt, par(taps, 0), par(taps, 1), da_cur, da_nxt],
        out_specs=[both, pl.BlockSpec((2, taps, tc), lambda j, i: (0, 0, j)),
                   pl.BlockSpec((2, 1, tc), lambda j, i: (0, 0, j))],
        out_shape=[jax.ShapeDtypeStruct((2, t, D_FF), BF16), jax.ShapeDtypeStruct((2, taps, D_FF), F32),
                   jax.ShapeDtypeStruct((2, 1, D_FF), F32)],
        compiler_params=_cp(("parallel", "arbitrary")))(u0, u0, u, u, w, w, da, da)


def _head_masks():
    lane = _iota((1, 4 * SSD_HEAD_DIM), 1)
    return [((lane >= r * SSD_HEAD_DIM) & (lane < (r + 1) * SSD_HEAD_DIM)).astype(F32) for r in range(4)]


def _segsum(v):
    first = _iota((1, LANES), 1) < SSD_HEAD_DIM
    halves = []
    for k in range(2):
        vh = v[:, k * LANES:(k + 1) * LANES]
        both = jnp.sum(vh, axis=1, keepdims=True)
        one = jnp.sum(jnp.where(first, vh, 0.0), axis=1, keepdims=True)
        halves.append(jnp.where(first, one, both - one))
    return jnp.concatenate(halves, axis=1)


def _ssd_common(raw_e, prow, rawr4, bcol, acol):
    n = SSD_CHUNK
    dt_e = _softplus(raw_e + prow[0:1, :])
    a_e = -jnp.exp(prow[1:2, :])
    d_e = prow[2:3, :]
    tril = (_iota((n, n), 0) >= _iota((n, n), 1)).astype(F32)
    acs_e = _dot_exact(tril, dt_e * a_e)
    last_e = acs_e[n - 1:n, :]
    dtr4 = _softplus(rawr4 + bcol)
    triu = (_iota((n, n), 0) <= _iota((n, n), 1)).astype(F32)
    acs_r4 = _dot_exact(dtr4 * (-jnp.exp(acol)), triu)
    return dt_e, a_e, d_e, acs_e, last_e, acs_r4


def _decay_matrix(acs_e, acs_r4, r):
    n = SSD_CHUNK
    col = acs_e[:, r * SSD_HEAD_DIM:r * SSD_HEAD_DIM + 1]
    seg = col - acs_r4[r:r + 1, :]
    causal = _iota((n, n), 0) >= _iota((n, n), 1)
    return jnp.exp(jnp.where(causal, seg, NEG))


SSD_STEP_CHUNKS = 4
SSD_ROWS = SSD_STEP_CHUNKS * SSD_CHUNK


def _ssd_specs(t, rev):
    nb = t // SSD_ROWS
    xb, bb, cb = 0, SSD_INNER // SSD_STATE, (SSD_INNER + BC_WIDTH) // SSD_STATE

    def ch(c):
        return (nb - 1 - c) if rev else c

    x = pl.BlockSpec((SSD_ROWS, 256), lambda g, c: (ch(c), xb + g))
    bm = pl.BlockSpec((SSD_ROWS, SSD_STATE), lambda g, c: (ch(c), bb + g))
    cm = pl.BlockSpec((SSD_ROWS, SSD_STATE), lambda g, c: (ch(c), cb + g))
    dtc = pl.BlockSpec((1, SSD_ROWS, 256), lambda g, c: (g, ch(c), 0))
    dtr = pl.BlockSpec((1, 4, SSD_ROWS), lambda g, c: (g, 0, ch(c)))
    prow = pl.BlockSpec((1, 3, 256), lambda g, c: (g, 0, 0))
    pcol = pl.BlockSpec((1, 4, 1), lambda g, c: (g, 0, 0))
    st = pl.BlockSpec((1, SSD_STEP_CHUNKS, SSD_STATE, 256), lambda g, c: (g, ch(c), 0, 0))
    return x, bm, cm, dtc, dtr, prow, pcol, st, ch


def _ssd_params(dt_raw, dt_bias, a_log, ssd_d):
    t = dt_raw.shape[0]
    by_group = dt_raw.reshape(t, SSD_GROUPS, 4)
    dtc = jnp.repeat(by_group, SSD_HEAD_DIM, axis=2).transpose(1, 0, 2)
    dtr = by_group.transpose(1, 2, 0)
    prow = jnp.repeat(jnp.stack([dt_bias.reshape(SSD_GROUPS, 4), a_log.reshape(SSD_GROUPS, 4),
                                 ssd_d.reshape(SSD_GROUPS, 4)], axis=1), SSD_HEAD_DIM, axis=2)
    bcol = dt_bias.reshape(SSD_GROUPS, 4, 1)
    acol = a_log.reshape(SSD_GROUPS, 4, 1)
    return dtc, dtr, prow, bcol, acol


def _ssd_fwd(xbc, params):
    t = xbc.shape[0]
    nc = t // SSD_CHUNK
    dtc, dtr, prow, bcol, acol = params

    def body(x_ref, b_ref, c_ref, dtc_ref, dtr_ref, prow_ref, bcol_ref, acol_ref, y_ref, st_ref, s_scr):
        c = pl.program_id(1)

        @pl.when(c == 0)
        def _():
            s_scr[...] = jnp.zeros_like(s_scr)

        masks = _head_masks()
        s = s_scr[...]
        for k in range(SSD_STEP_CHUNKS):
            rows = slice(k * SSD_CHUNK, (k + 1) * SSD_CHUNK)
            dt_e, a_e, d_e, acs_e, last_e, acs_r4 = _ssd_common(
                dtc_ref[0, rows], prow_ref[0], dtr_ref[0][:, rows], bcol_ref[0], acol_ref[0])
            xv = x_ref[rows]
            bm, cm = b_ref[rows], c_ref[rows]
            st_ref[0, k] = s
            xdt = xv * dt_e
            cb = _dot(cm, bm, 'nt')
            y = _dot(cm, s) * jnp.exp(acs_e) + xv * d_e
            for r in range(4):
                mr = cb * _decay_matrix(acs_e, acs_r4, r)
                y = y + _dot(mr, xdt * masks[r])
            y_ref[rows] = y
            w = xdt * jnp.exp(last_e - acs_e)
            s = s * jnp.exp(last_e) + _dot(bm.T, w)
        s_scr[...] = s

    x, bm, cm, dtcs, dtrs, prs, pcs, st, _ = _ssd_specs(t, False)
    return pl.pallas_call(
        body, name="ssd_fwd", grid=(SSD_GROUPS, t // SSD_ROWS), in_specs=[x, bm, cm, dtcs, dtrs, prs, pcs, pcs],
        out_specs=[pl.BlockSpec((SSD_ROWS, 256), lambda g, c: (c, g)), st],
        out_shape=[jax.ShapeDtypeStruct((t, SSD_INNER), F32),
                   jax.ShapeDtypeStruct((SSD_GROUPS, nc, SSD_STATE, 256), F32)],
        scratch_shapes=[pltpu.VMEM((SSD_STATE, 256), F32)],
        compiler_params=_cp(("parallel", "arbitrary")))(xbc, xbc, xbc, dtc, dtr, prow, bcol, acol)


def _ssd_bwd(xbc, pre, params, states, dy):
    t = xbc.shape[0]
    nc = t // SSD_CHUNK
    n = SSD_CHUNK
    dtc, dtr, prow, bcol, acol = params

    def body(x_ref, b_ref, c_ref, ux_ref, ub_ref, uc_ref, dtc_ref, dtr_ref, prow_ref, bcol_ref, acol_ref, st_ref,
             dy_ref, dx_ref, db_ref, dc_ref, ddt_ref, dp_ref, ds_scr):
        c = pl.program_id(1)

        @pl.when(c == 0)
        def _():
            ds_scr[...] = jnp.zeros_like(ds_scr)
            dp_ref[...] = jnp.zeros_like(dp_ref)

        masks = _head_masks()
        ds = ds_scr[...]
        for k in reversed(range(SSD_STEP_CHUNKS)):
            rows = slice(k * SSD_CHUNK, (k + 1) * SSD_CHUNK)
            raw_e = dtc_ref[0, rows]
            prw = prow_ref[0]
            dt_e, a_e, d_e, acs_e, last_e, acs_r4 = _ssd_common(raw_e, prw, dtr_ref[0][:, rows], bcol_ref[0], acol_ref[0])
            xv = x_ref[rows]
            bm, cm = b_ref[rows], c_ref[rows]
            s = st_ref[0, k]
            dyv = dy_ref[rows]
            e_e = jnp.exp(acs_e)
            dec_e = jnp.exp(last_e - acs_e)
            cd_e = jnp.exp(last_e)
            xdt = xv * dt_e
            w = xdt * dec_e
            b16, c16, s16, ds16 = bm.astype(BF16), cm.astype(BF16), s.astype(BF16), ds.astype(BF16)
            cb = _dot(c16, b16, 'nt')
            yoff_raw = _dot(c16, s16)
            dye = dyv * e_e
            dye16 = dye.astype(BF16)
            dcm = _dot(dye16, s16, 'nt')
            ds_prev = ds * cd_e + _dot(cm.T, dye16)
            dacs_e = _segsum(dyv * yoff_raw) * e_e
            dw = _dot(b16, ds16)
            dbm = _dot(w, ds16, 'nt')
            tdec = _segsum(dw * xdt) * dec_e
            dacs_e = dacs_e - tdec
            dlast_e = jnp.sum(tdec, axis=0, keepdims=True)
            dxdt = dw * dec_e
            dlast_e = dlast_e + _segsum(jnp.sum(ds * s, axis=0, keepdims=True)) * cd_e
            dcb = jnp.zeros((n, n), F32)
            for r in range(4):
                lm = _decay_matrix(acs_e, acs_r4, r)
                mr = cb * lm
                dyr16 = (dyv * masks[r]).astype(BF16)
                dm = _dot(dyr16, xdt * masks[r], 'nt')
                dcb = dcb + dm * lm
                dseg = dm * mr
                dcol = jnp.sum(dseg, axis=1, keepdims=True) - jnp.sum(dseg.T, axis=1, keepdims=True)
                dacs_e = dacs_e + dcol * masks[r]
                dxdt = dxdt + _dot(mr.T, dyr16)
            dcm = dcm + _dot(dcb, b16)
            dbm = dbm + _dot(dcb.T, c16)
            dacs_e = dacs_e + jnp.where(_iota((n, 1), 0) == n - 1, dlast_e, 0.0)
            triu = (_iota((n, n), 0) <= _iota((n, n), 1)).astype(F32)
            ddta_e = _dot_exact(triu, dacs_e)
            ddt_e = ddta_e * a_e + _segsum(dxdt * xv)
            dx_ref[rows] = (dxdt * dt_e + dyv * d_e) * _dsilu(ux_ref[rows])
            db_ref[rows] = dbm * _dsilu(ub_ref[rows])
            dc_ref[rows] = dcm * _dsilu(uc_ref[rows])
            draw_e = ddt_e * _sigmoid(raw_e + prw[0:1, :])
            draw_t = draw_e.T
            ddt_ref[0, :, rows] = jnp.concatenate([draw_t[r * SSD_HEAD_DIM:r * SSD_HEAD_DIM + 1] for r in range(4)], axis=0)
            dbias = jnp.sum(draw_e, axis=0, keepdims=True)
            dalog = jnp.sum(ddta_e * dt_e, axis=0, keepdims=True) * a_e
            dd = _segsum(jnp.sum(dyv * xv, axis=0, keepdims=True))
            row3 = _iota((3, 1), 0)
            dp_ref[0] += (jnp.where(row3 == 0, dbias, 0.0) + jnp.where(row3 == 1, dalog, 0.0)
                          + jnp.where(row3 == 2, dd, 0.0))
            ds = ds_prev
        ds_scr[...] = ds


    x, bm, cm, dtcs, dtrs, prs, pcs, st, ch = _ssd_specs(t, True)
    yblk = pl.BlockSpec((SSD_ROWS, 256), lambda g, c: (ch(c), g))
    nblk = pl.BlockSpec((SSD_ROWS, SSD_STATE), lambda g, c: (ch(c), g))
    return pl.pallas_call(
        body, name="ssd_bwd", grid=(SSD_GROUPS, t // SSD_ROWS),
        in_specs=[x, bm, cm, x, bm, cm, dtcs, dtrs, prs, pcs, pcs, st, yblk],
        out_specs=[yblk, nblk, nblk, dtrs, prs],
        out_shape=[jax.ShapeDtypeStruct((t, SSD_INNER), F32), jax.ShapeDtypeStruct((t, BC_WIDTH), F32),
                   jax.ShapeDtypeStruct((t, BC_WIDTH), F32), jax.ShapeDtypeStruct((SSD_GROUPS, 4, t), F32),
                   jax.ShapeDtypeStruct((SSD_GROUPS, 3, 256), F32)],
        scratch_shapes=[pltpu.VMEM((SSD_STATE, 256), F32)],
        compiler_params=_cp(("parallel", "arbitrary")))(xbc, xbc, xbc, pre, pre, pre, dtc, dtr, prow, bcol, acol,
                                                         states, dy)


GROUP_W = SSD_INNER // SSD_GROUPS


def _mix_specs(tb):
    row = pl.BlockSpec((tb, 2048), lambda i: (i, 0))
    zlo = pl.BlockSpec((tb, 1024), lambda i: (i, O_Z // 1024))
    zhi = pl.BlockSpec((tb, 1024), lambda i: (i, O_Z // 1024 + 1))
    vec = pl.BlockSpec((1, 2048), lambda i: (0, 0))
    return row, zlo, zhi, vec


def _mix_fwd(attn, y, proj, g_attn, g_ssd):
    t = attn.shape[0]
    tb = _rows(t, 256)

    def body(a_ref, y_ref, zlo_ref, zhi_ref, ga_ref, gs_ref, o_ref):
        av = a_ref[...]
        r = lax.rsqrt(jnp.mean(av * av, axis=-1, keepdims=True) + EPS)
        o_ref[:, :ATTN_WIDTH] = (av * r * ga_ref[...]).astype(BF16)
        for g in range(SSD_GROUPS):
            lo, hi = g * GROUP_W, (g + 1) * GROUP_W
            zref = zlo_ref if g < 4 else zhi_ref
            z = zref[:, lo % 1024:lo % 1024 + GROUP_W]
            yg = y_ref[:, lo:hi] * (z * _sigmoid(z))
            rg = lax.rsqrt(jnp.mean(yg * yg, axis=-1, keepdims=True) + EPS)
            o_ref[:, ATTN_WIDTH + lo:ATTN_WIDTH + hi] = (yg * rg * gs_ref[:, lo:hi]).astype(BF16)

    row, zlo, zhi, vec = _mix_specs(tb)
    return pl.pallas_call(
        body, name="mix_fwd", grid=(t // tb,), in_specs=[row, row, zlo, zhi, vec, vec],
        out_specs=pl.BlockSpec((tb, 4096), lambda i: (i, 0)), out_shape=jax.ShapeDtypeStruct((t, 4096), BF16),
        compiler_params=_cp(("parallel",)))(attn, y, proj, proj, g_attn, g_ssd)


def _mix_bwd(dmix, attn, y, proj, g_attn, g_ssd):
    t = attn.shape[0]
    tb = _rows(t, 256)

    def body(dm_ref, a_ref, y_ref, zlo_ref, zhi_ref, ga_ref, gs_ref, da_ref, dy_ref, dz_ref, dga_ref, dgs_ref):
        i = pl.program_id(0)
        av = a_ref[...]
        dn = dm_ref[:, :ATTN_WIDTH].astype(F32)
        r = lax.rsqrt(jnp.mean(av * av, axis=-1, keepdims=True) + EPS)
        u = dn * ga_ref[...]
        da_ref[...] = r * u - av * (r * r * r * jnp.mean(u * av, axis=-1, keepdims=True))
        dga = jnp.sum(dn * av * r, axis=0, keepdims=True)

        @pl.when(i == 0)
        def _():
            dga_ref[...] = dga

        @pl.when(i > 0)
        def _():
            dga_ref[...] += dga

        for g in range(SSD_GROUPS):
            lo, hi = g * GROUP_W, (g + 1) * GROUP_W
            zref = zlo_ref if g < 4 else zhi_ref
            z = zref[:, lo % 1024:lo % 1024 + GROUP_W]
            yv = y_ref[:, lo:hi]
            sg = _sigmoid(z)
            sz = z * sg
            yg = yv * sz
            rg = lax.rsqrt(jnp.mean(yg * yg, axis=-1, keepdims=True) + EPS)
            do = dm_ref[:, ATTN_WIDTH + lo:ATTN_WIDTH + hi].astype(F32)
            ug = do * gs_ref[:, lo:hi]
            dyg = rg * ug - yg * (rg * rg * rg * jnp.mean(ug * yg, axis=-1, keepdims=True))
            dy_ref[:, lo:hi] = dyg * sz
            dz_ref[:, lo:hi] = (dyg * yv * (sg * (1.0 + z * (1.0 - sg)))).astype(BF16)
            dgs = jnp.sum(do * yg * rg, axis=0, keepdims=True)

            @pl.when(i == 0)
            def _():
                dgs_ref[:, lo:hi] = dgs

            @pl.when(i > 0)
            def _():
                dgs_ref[:, lo:hi] += dgs

    row, zlo, zhi, vec = _mix_specs(tb)
    return pl.pallas_call(
        body, name="mix_bwd", grid=(t // tb,),
        in_specs=[pl.BlockSpec((tb, 4096), lambda i: (i, 0)), row, row, zlo, zhi, vec, vec],
        out_specs=[row, row, row, vec, vec],
        out_shape=[jax.ShapeDtypeStruct((t, 2048), F32), jax.ShapeDtypeStruct((t, 2048), F32),
                   jax.ShapeDtypeStruct((t, 2048), BF16), jax.ShapeDtypeStruct((1, 2048), F32),
                   jax.ShapeDtypeStruct((1, 2048), F32)],
        compiler_params=_cp(("arbitrary",)))(dmix, attn, y, proj, proj, g_attn, g_ssd)


def _adamw(w, g, m, v, name):
    r, c = w.shape
    tb = _rows(r, 256)
    c1 = 1.0 - ADAM_B1 ** ADAM_STEP
    c2 = 1.0 - ADAM_B2 ** ADAM_STEP

    def body(w_ref, g_ref, m_ref, v_ref, d_ref, m2_ref, v2_ref):
        gv = g_ref[...]
        m2 = ADAM_B1 * m_ref[...] + (1.0 - ADAM_B1) * gv
        v2 = ADAM_B2 * v_ref[...] + (1.0 - ADAM_B2) * (gv * gv)
        d_ref[...] = -ADAM_LR * ((m2 / c1) / (jnp.sqrt(v2 / c2) + ADAM_EPS) + ADAM_WD * w_ref[...])
        m2_ref[...] = m2
        v2_ref[...] = v2

    blk = pl.BlockSpec((tb, c), lambda i: (i, 0))
    shp = jax.ShapeDtypeStruct((r, c), F32)
    return pl.pallas_call(body, name=name, grid=(r // tb,), in_specs=[blk] * 4, out_specs=[blk] * 3,
                          out_shape=[shp] * 3, compiler_params=_cp(("parallel",)))(w, g, m, v)


def _adamw_halves(w, mine, theirs, m, v, pos, name, cols=False):
    r, c = w.shape
    h = r if cols else r // 2
    tb = _rows(h, 128)
    nh = h // tb
    c1 = 1.0 - ADAM_B1 ** ADAM_STEP
    c2 = 1.0 - ADAM_B2 ** ADAM_STEP

    def body(pos_ref, w_ref, a_ref, b_ref, m_ref, v_ref, g_ref, d_ref, m2_ref, v2_ref):
        which = pl.program_id(1) if cols else pl.program_id(0) // nh
        gv = jnp.where(which == pos_ref[0], a_ref[...], b_ref[...])
        m2 = ADAM_B1 * m_ref[...] + (1.0 - ADAM_B1) * gv
        v2 = ADAM_B2 * v_ref[...] + (1.0 - ADAM_B2) * (gv * gv)
        g_ref[...] = gv
        d_ref[...] = -ADAM_LR * ((m2 / c1) / (jnp.sqrt(v2 / c2) + ADAM_EPS) + ADAM_WD * w_ref[...])
        m2_ref[...] = m2
        v2_ref[...] = v2

    if cols:
        full = pl.BlockSpec((tb, c // 2), lambda i, j, pref: (i, j))
        mine_spec = theirs_spec = pl.BlockSpec((tb, c // 2), lambda i, j, pref: (i, 0))
        grid = (nh, 2)
    else:
        full = pl.BlockSpec((tb, c), lambda i, pref: (i, 0))
        mine_spec = pl.BlockSpec((tb, c), lambda i, pref: (jnp.where(i // nh == pref[0], i % nh,
                                                                     jnp.where(pref[0] == 0, nh - 1, 0)), 0))
        theirs_spec = pl.BlockSpec((tb, c), lambda i, pref: (jnp.where(i // nh != pref[0], i % nh,
                                                                       jnp.where(pref[0] == 0, 0, nh - 1)), 0))
        grid = (r // tb,)
    shp = jax.ShapeDtypeStruct((r, c), F32)
    grid_spec = pltpu.PrefetchScalarGridSpec(num_scalar_prefetch=1, grid=grid,
                                             in_specs=[full, mine_spec, theirs_spec, full, full],
                                             out_specs=[full] * 4)
    return pl.pallas_call(body, name=name, grid_spec=grid_spec, out_shape=[shp] * 4,
                          compiler_params=_cp(("parallel",) * len(grid)))(pos, w, mine, theirs, m, v)


def _sum_own_half(g4, recv, pos, name, cols=False):
    _, r, c = g4.shape
    h, c = (r, c // 2) if cols else (r // 2, c)
    tb = _rows(h, 128)
    nh = h // tb

    def slot(j, pref):
        return (pref[1] + 1 + j) % N_CHIPS

    if cols:
        own = lambda j, i, pref: (slot(j, pref), i, pref[0])
    else:
        own = lambda j, i, pref: (slot(j, pref), pref[0] * nh + i, 0)
    same = lambda j, i, pref: (slot(j, pref), i, 0)

    def body(pos_ref, a_ref, b_ref, o_ref):
        o_ref[...] = (a_ref[...] + b_ref[...]).astype(BF16)

    grid_spec = pltpu.PrefetchScalarGridSpec(
        num_scalar_prefetch=1, grid=(N_CHIPS - 1, nh),
        in_specs=[pl.BlockSpec((1, tb, c), own), pl.BlockSpec((1, tb, c), same)],
        out_specs=pl.BlockSpec((1, tb, c), same))
    return pl.pallas_call(body, name=name, grid_spec=grid_spec,
                          out_shape=jax.ShapeDtypeStruct((N_CHIPS, h, c), BF16),
                          compiler_params=_cp(("parallel", "parallel")))(pos, g4, recv)


def _sum_chips(g4, recv, parts, pos, name, cols=False):
    _, r, c = g4.shape
    h, c = (r, c // 2) if cols else (r // 2, c)
    tb = _rows(h, 128)
    nh = h // tb
    own = (lambda i, pref: (pref[1], i, pref[0])) if cols else (lambda i, pref: (pref[1], pref[0] * nh + i, 0))

    def body(pos_ref, a_ref, b_ref, p_ref, o_ref):
        own = a_ref[0] + b_ref[0]
        o_ref[...] = ((own + p_ref[0].astype(F32)) + p_ref[1].astype(F32)) + p_ref[2].astype(F32)

    grid_spec = pltpu.PrefetchScalarGridSpec(
        num_scalar_prefetch=1, grid=(nh,),
        in_specs=[pl.BlockSpec((1, tb, c), own),
                  pl.BlockSpec((1, tb, c), lambda i, pref: (pref[1], i, 0)),
                  pl.BlockSpec((3, tb, c), lambda i, pref: (0, i, 0))],
        out_specs=pl.BlockSpec((tb, c), lambda i, pref: (i, 0)))
    return pl.pallas_call(body, name=name, grid_spec=grid_spec, out_shape=jax.ShapeDtypeStruct((h, c), F32),
                          compiler_params=_cp(("parallel",)))(pos, g4, recv, parts)


def _me():
    return lax.axis_index("x"), lax.axis_index("y"), lax.axis_index("c")


def _flip(v, bit):
    return (1 - v) if bit else v


CHIP_FLIPS = [(1, 0), (0, 1), (1, 1)]


def _forward_halves(gathered):
    def body(g_ref, o_ref, token, send_sems, recv_sems):
        x, y, c = _me()
        h = g_ref.shape[2] // 2
        cps = []
        for k, (fx, fy) in enumerate(CHIP_FLIPS):
            peer_chip = 2 * _flip(x, fx) + _flip(y, fy)
            mine = o_ref.at[peer_chip, :, pl.ds(c * h, h)]
            cp = pltpu.make_async_remote_copy(src_ref=mine, dst_ref=mine, send_sem=send_sems.at[k],
                                              recv_sem=recv_sems.at[k], device_id=(x, y, 1 - c), device_id_type=MESH)
            cp.start()
            cps.append(cp)
        for k, (fx, fy) in enumerate(CHIP_FLIPS):
            peer_chip = 2 * _flip(x, fx) + _flip(y, fy)
            theirs = o_ref.at[peer_chip, :, pl.ds((1 - c) * h, h)]
            pltpu.make_async_remote_copy(src_ref=theirs, dst_ref=theirs, send_sem=send_sems.at[k],
                                         recv_sem=recv_sems.at[k], device_id=(x, y, 1 - c),
                                         device_id_type=MESH).wait_recv()
        for cp in cps:
            cp.wait_send()
        token[...] = jnp.zeros_like(token)

    return pl.pallas_call(
        body, name="gather_forward_w_in", in_specs=[HBM_SPEC],
        out_specs=[HBM_SPEC, pl.BlockSpec(memory_space=pltpu.VMEM)],
        out_shape=[jax.ShapeDtypeStruct(gathered.shape, gathered.dtype), TOKEN],
        scratch_shapes=[pltpu.SemaphoreType.DMA((3,)), pltpu.SemaphoreType.DMA((3,))],
        input_output_aliases={0: 0},
        compiler_params=pltpu.CompilerParams(has_side_effects=True))(gathered)


SEM_SPEC = pl.BlockSpec(memory_space=pltpu.SEMAPHORE)
ANY_SPEC = pl.BlockSpec(memory_space=pl.ANY)
DATAFLOW = pltpu.SideEffectType.DATAFLOW_SIDE_EFFECTING


def _in_hbm(a):
    return pltpu.with_memory_space_constraint(a, pltpu.HBM)


def _push_start(srcs, land_shapes, route, peers, name):
    n, npeer = len(srcs), len(peers)
    lands = [lax.empty(shp, s.dtype) for shp, s in zip(land_shapes, srcs)]

    def body(*refs):
        ins, lnd = refs[:n], refs[n:2 * n]
        send_sems, recv_sems = refs[2 * n], refs[2 * n + 1]
        token = refs[-1]
        x, y, c = _me()
        for t in range(n):
            for k, (fx, fy, fc) in enumerate(peers):
                src, dst = route(ins[t], lnd[t], k, x, y, c)
                pltpu.make_async_remote_copy(
                    src_ref=src, dst_ref=dst, send_sem=send_sems.at[npeer * t + k],
                    recv_sem=recv_sems.at[npeer * t + k],
                    device_id=(_flip(x, fx), _flip(y, fy), _flip(c, fc)), device_id_type=MESH).start()
        token[...] = jnp.zeros_like(token)

    bufs = [_in_hbm(a) for a in list(srcs) + lands]
    outs = pl.pallas_call(
        body, name=name,
        out_shape=(pltpu.SemaphoreType.DMA((npeer * n,)), pltpu.SemaphoreType.DMA((npeer * n,)),
                   *[pltpu.HBM(b.shape, b.dtype) for b in bufs], TOKEN),
        in_specs=[HBM_SPEC] * (2 * n),
        out_specs=(SEM_SPEC, SEM_SPEC, *[HBM_SPEC] * (2 * n), pl.BlockSpec(memory_space=pltpu.VMEM)),
        input_output_aliases={i: 2 + i for i in range(2 * n)},
        compiler_params=pltpu.CompilerParams(has_side_effects=DATAFLOW))(*bufs)
    return outs[0], outs[1], list(outs[2:2 + n]), list(outs[2 + n:2 + 2 * n]), outs[-1]


def _push_wait(send_sems, recv_sems, srcs, lands, after, route, peers, name):
    n, npeer = len(srcs), len(peers)

    def body(*refs):
        ins, lnd = refs[:n], refs[n:2 * n]
        ssem, rsem = refs[2 * n], refs[2 * n + 1]
        x, y, c = _me()
        for t in range(n):
            for k, (fx, fy, fc) in enumerate(peers):
                src, dst = route(ins[t], lnd[t], k, x, y, c)
                cp = pltpu.make_async_remote_copy(
                    src_ref=src, dst_ref=dst, send_sem=ssem.at[npeer * t + k], recv_sem=rsem.at[npeer * t + k],
                    device_id=(_flip(x, fx), _flip(y, fy), _flip(c, fc)), device_id_type=MESH)
                cp.wait_send()
                cp.wait_recv()

    bufs = list(srcs) + list(lands)
    outs = pl.pallas_call(
        body, name=name, out_shape=tuple(pltpu.HBM(b.shape, b.dtype) for b in bufs),
        in_specs=[HBM_SPEC] * (2 * n) + [SEM_SPEC, SEM_SPEC, ANY_SPEC], out_specs=tuple([HBM_SPEC] * (2 * n)),
        input_output_aliases={i: i for i in range(2 * n)},
        compiler_params=pltpu.CompilerParams(has_side_effects=DATAFLOW))(*bufs, send_sems, recv_sems, after)
    return list(outs[:n]), list(outs[n:])


OTHER_CHIPS = [(fx, fy, 0) for fx, fy in CHIP_FLIPS]
SIBLING = [(0, 0, 1)]


def _route_gather(src, land, k, x, y, c):
    return src, land.at[2 * x + y]


def _route_gather_half(src, land, k, x, y, c):
    h = src.shape[1] // 2
    return src.at[:, pl.ds(c * h, h)], land.at[2 * x + y, :, pl.ds(c * h, h)]


def _route_gather_half_wait(src, land, k, x, y, c):
    fx, fy = CHIP_FLIPS[k]
    h = src.shape[1] // 2
    return src.at[:, pl.ds(c * h, h)], land.at[2 * _flip(x, fx) + _flip(y, fy), :, pl.ds(c * h, h)]


def _route_gather_wait(src, land, k, x, y, c):
    fx, fy = CHIP_FLIPS[k]
    return src, land.at[2 * _flip(x, fx) + _flip(y, fy)]


def _route_scatter(src, land, k, x, y, c):
    fx, fy = CHIP_FLIPS[k]
    return src.at[2 * _flip(x, fx) + _flip(y, fy)], land.at[k]


def _route_exchange(src, land, k, x, y, c):
    h = land.shape[1]
    return src.at[:, pl.ds((1 - c) * h, h)], land


def _route_whole(src, land, k, x, y, c):
    return src, land


def _route_exchange_cols(src, land, k, x, y, c):
    h = land.shape[2]
    return src.at[:, :, pl.ds((1 - c) * h, h)], land


ALL_OTHERS = [((k >> 2) & 1, (k >> 1) & 1, k & 1) for k in range(1, 8)]


def _route_to_all(src, land, k, x, y, c):
    return src, land.at[4 * x + 2 * y + c]


def _route_to_all_wait(src, land, k, x, y, c):
    fx, fy, fc = ALL_OTHERS[k]
    return src, land.at[4 * _flip(x, fx) + 2 * _flip(y, fy) + _flip(c, fc)]


def _sum_devices(parts):
    def body(p_ref, o_ref):
        acc = p_ref[0]
        for d in range(1, 8):
            acc = acc + p_ref[d]
        o_ref[...] = acc

    vm = pl.BlockSpec(memory_space=pltpu.VMEM)
    return pl.pallas_call(body, name="allreduce_sum", in_specs=[vm], out_specs=vm,
                          out_shape=jax.ShapeDtypeStruct(parts.shape[1:], F32),
                          compiler_params=pltpu.CompilerParams(vmem_limit_bytes=VMEM_LIMIT))(parts)


def _grad_exchange_start(g4, tag, cols=False):
    land = (N_CHIPS, g4.shape[1], g4.shape[2] // 2) if cols else (N_CHIPS, g4.shape[1] // 2, g4.shape[2])
    route = _route_exchange_cols if cols else _route_exchange
    send_sems, recv_sems, srcs, lands, token = _push_start(
        [g4], [land], route, SIBLING, name="grad_exchange_start_" + tag)
    return (send_sems, recv_sems, srcs, lands, tag, cols), token


def _grad_scatter_start(state, pos, after):
    send_sems, recv_sems, srcs, lands, tag, cols = state
    route = _route_exchange_cols if cols else _route_exchange
    (g4,), (recv,) = _push_wait(send_sems, recv_sems, srcs, lands, after, route, SIBLING,
                                name="grad_exchange_wait_" + tag)
    return _grad_pair_scatter(g4, recv, pos, tag, cols)


def _grad_pair_scatter(g4, recv, pos, tag, cols=False):
    p16 = _sum_own_half(g4, recv, pos, name="grad_sum_pair_" + tag, cols=cols)
    send_sems, recv_sems, srcs, lands, token = _push_start(
        [p16], [(3,) + p16.shape[1:]], _route_scatter, OTHER_CHIPS, name="grad_scatter_start_" + tag)
    return (g4, recv, send_sems, recv_sems, srcs, lands, tag, cols), token


def _grad_sum_and_share(state, pos, after):
    g4, recv, send_sems, recv_sems, srcs, lands, tag, cols = state
    parts = _push_wait(send_sems, recv_sems, srcs, lands, after, _route_scatter, OTHER_CHIPS,
                       name="grad_scatter_wait_" + tag)[1][0]
    mine = _sum_chips(g4, recv, parts, pos, name="grad_sum_chips_" + tag, cols=cols)
    send_sems, recv_sems, srcs, lands, token = _push_start(
        [mine], [mine.shape], _route_whole, SIBLING, name="grad_share_start_" + tag)
    return (send_sems, recv_sems, srcs, lands, tag), token


def _grad_share_wait(state, after):
    send_sems, recv_sems, srcs, lands, tag = state
    (mine,), (theirs,) = _push_wait(send_sems, recv_sems, srcs, lands, after, _route_whole, SIBLING,
                                    name="grad_share_wait_" + tag)
    return mine, theirs


def _local_step(x, tgt, p, hooks):
    t = x.shape[0]
    tables = _rope_tables(t)
    sinks = p['sinks'].reshape(N_Q_HEADS)

    def told(name, value):
        return tuple(hooks.grad_ready(name, value))

    xn = _rmsnorm_fwd(x, p['norm_mix'], "norm_mix_fwd", deps=hooks.first_deps)
    w_in_t, w_in_dt, in_deps = hooks.weight_in(xn)
    proj = _matmul(xn, w_in_t, mode='nt', name="in_proj", n_limit=MAIN_WIDTH, deps=in_deps)
    dt_raw = _matmul(xn, w_in_dt, mode='nt', name="in_proj_dt")[:, :SSD_HEADS]
    ssd_conv_w, ffn_conv_w = hooks.conv_weights(proj)
    p = dict(p, ssd_conv_w=ssd_conv_w, ffn_conv_w=ffn_conv_w)
    attn = _attn_fwd(proj, sinks, tables)
    conv_b = p['ssd_conv_b']
    xbc, xbc_pre = _conv_silu_fwd(proj, p['ssd_conv_w'], conv_b, col0=O_XBC, width=CONV_CH, name="ssd_conv_fwd")
    sp = _ssd_params(dt_raw, p['dt_bias'].reshape(-1), p['a_log'].reshape(-1), p['ssd_d'].reshape(-1))
    y, states = _ssd_fwd(xbc, sp)
    mix = _mix_fwd(attn, y, proj, p['attn_out_norm'], p['ssd_norm'])
    w_out = hooks.weight('w_out', mix)
    h1 = _matmul(mix, w_out, mode='nn', name="out_proj", add=x)
    hn = _rmsnorm_fwd(h1, p['norm_ffn'], "norm_ffn_fwd")
    w_up = hooks.weight('w_up', hn)
    u0 = _matmul(hn, w_up, mode='nn', name="ffn_up", b_owner=True, tn=1408)
    a, u = _ffn_act_fwd(u0, p['ffn_conv_w'], p['ffn_conv_b'])
    w_down = hooks.weight('w_down', a)
    h2 = _matmul(a, w_down, mode='nn', name="ffn_down", add=h1, tk=2816)
    loss, dh2, dh2_16, g_norm_final = _final_loss(h2, p['norm_final'].reshape(1, D_MODEL), tgt)

    g = {}
    da = _matmul(dh2_16, w_down, mode='nt', name="ffn_down_dx", out_dtype=BF16, tn=1408)
    g['w_down'] = _matmul(a, dh2_16, mode='tn', name="ffn_down_dw", tm=1408)
    dep = told('w_down', g['w_down'])
    du0, dcw, dcb = _ffn_act_bwd(u0, u, p['ffn_conv_w'], da)
    g['ffn_conv_w'] = dcw.transpose(1, 0, 2).reshape(FFN_CONV, 2 * D_FF)
    g['ffn_conv_b'] = dcb.transpose(1, 0, 2).reshape(1, 2 * D_FF)
    g['w_up'] = _matmul(hn, du0, mode='tn', name="ffn_up_dw", deps=dep, b_halves=True, owner_major=True,
                        tn=1408)
    dep = told('w_up', g['w_up'])
    dhn = _matmul(du0, w_up, mode='nt', name="ffn_up_dx", out_dtype=BF16, deps=dep, a_halves=True,
                  b_owner=True, tk=2816)
    dh1, dh1_16, g['norm_ffn'] = _rmsnorm_bwd(h1, p['norm_ffn'], dhn, dh2, "norm_ffn_bwd")

    g['w_out'] = _matmul(mix, dh1_16, mode='tn', name="out_proj_dw")
    dep = told('w_out', g['w_out'])
    dmix = _matmul(dh1_16, w_out, mode='nt', name="out_proj_dx", out_dtype=BF16, deps=dep)
    dattn, dy, dz, g['attn_out_norm'], g['ssd_norm'] = _mix_bwd(dmix, attn, y, proj, p['attn_out_norm'],
                                                                p['ssd_norm'])
    dq, dk, dv, dsink = _attn_bwd(proj, sinks, tables, dattn)
    g['sinks'] = dsink[:, :, 0].reshape(1, N_Q_HEADS)
    dxs, dbm, dcm, ddt8, dpar = _ssd_bwd(xbc, xbc_pre, sp, states, dy)
    dpar = dpar[:, :, ::SSD_HEAD_DIM]
    g['dt_bias'] = dpar[:, 0, :].reshape(1, SSD_HEADS)
    g['a_log'] = dpar[:, 1, :].reshape(1, SSD_HEADS)
    g['ssd_d'] = dpar[:, 2, :].reshape(1, SSD_HEADS)
    dxbc, g['ssd_conv_w'], g['ssd_conv_b'] = _ssd_conv_bwd(proj, p['ssd_conv_w'], dxs, dbm, dcm, col0=O_XBC,
                                                           name="ssd_conv_bwd")
    dproj = jnp.concatenate([dq, dk, dv, dz, dxbc], axis=1)
    ddt = ddt8.transpose(2, 0, 1).reshape(t, SSD_HEADS)
    ddt_pad = jnp.pad(ddt, ((0, 0), (0, LANES - SSD_HEADS))).astype(BF16)
    g['w_in'] = (_matmul(dproj, xn, mode='tn', name="in_proj_dw", m_rows=IN_PROJ_WIDTH),
                 _matmul(ddt_pad, xn, mode='tn', name="in_proj_dt_dw"))
    dep = told('w_in', g['w_in'])
    dxn_dt = _matmul(ddt_pad, w_in_dt, mode='nn', name="in_proj_dt_dx", deps=dep)
    dxn = _matmul(dproj, w_in_t, mode='nn', name="in_proj_dx", out_dtype=BF16, add=dxn_dt, k_limit=MAIN_WIDTH,
                  tk=2304)
    dx, _, g['norm_mix'] = _rmsnorm_bwd(x, p['norm_mix'], dxn, dh1, "norm_mix_bwd")
    g['norm_final'] = g_norm_final
    return loss, dx, g


def _pack(arrs):
    flat = jnp.concatenate([a.reshape(-1) for a in arrs])
    n = flat.shape[0]
    rows = -(-n // LANES)
    rows = -(-rows // 8) * 8
    return jnp.pad(flat, (0, rows * LANES - n)).reshape(rows, LANES)


def _unpack(packed, shapes):
    flat = packed.reshape(-1)
    out, off = [], 0
    for s in shapes:
        n = 1
        for d in s:
            n *= d
        out.append(flat[off:off + n].reshape(s))
        off += n
    return out


class _StepHooks:
    def __init__(self, first_deps, weight_in, conv_weights, weight, grad_ready):
        self.first_deps = first_deps
        self.weight_in = weight_in
        self.conv_weights = conv_weights
        self.weight = weight
        self.grad_ready = grad_ready


def kernel(x, norm_mix, w_in, sinks, attn_out_norm, ssd_conv_w, ssd_conv_b, dt_bias, a_log, ssd_d, ssd_norm, w_out, norm_ffn, w_up, ffn_conv_w, ffn_conv_b, w_down, norm_final, loss_target, m_norm_mix, m_w_in, m_sinks, m_attn_out_norm, m_ssd_conv_w, m_ssd_conv_b, m_dt_bias, m_a_log, m_ssd_d, m_ssd_norm, m_w_out, m_norm_ffn, m_w_up, m_ffn_conv_w, m_ffn_conv_b, m_w_down, m_norm_final, v_norm_mix, v_w_in, v_sinks, v_attn_out_norm, v_ssd_conv_w, v_ssd_conv_b, v_dt_bias, v_a_log, v_ssd_d, v_ssd_norm, v_w_out, v_norm_ffn, v_w_up, v_ffn_conv_w, v_ffn_conv_b, v_w_down, v_norm_final):
    args = dict(locals())
    w = {n: args[n] for n in WEIGHTS}
    m = {n: args['m_' + n] for n in WEIGHTS}
    v = {n: args['v_' + n] for n in WEIGHTS}
    xi, yi, ci = _me()
    chip = 2 * xi + yi
    pos = jnp.stack([ci, chip]).astype(jnp.int32)

    conv_shard = _pack([ssd_conv_w[0], ffn_conv_w[0]])
    conv_gather = _push_start([conv_shard], [(N_CHIPS,) + conv_shard.shape], _route_gather, OTHER_CHIPS,
                              name="gather_start_conv")

    def conv_weights(after):
        send_sems, recv_sems, srcs, lands, _ = conv_gather
        (own,), (got,) = _push_wait(send_sems, recv_sems, srcs, lands, after, _route_gather_wait, OTHER_CHIPS,
                                    name="gather_wait_conv")
        whole = lax.dynamic_update_slice(got, own[None], (chip, 0, 0))
        per_chip = [_unpack(whole[j], [ssd_conv_w.shape[1:], ffn_conv_w.shape[1:]]) for j in range(N_CHIPS)]
        return (jnp.concatenate([pc[0] for pc in per_chip], axis=1),
                jnp.concatenate([pc[1] for pc in per_chip], axis=1))

    w_in_t, m_in_t, v_in_t = (jnp.transpose(a[0]) for a in (w_in, m_w_in, v_w_in))
    in_shard = (w_in_t + conv_gather[4][:1, :1]).astype(BF16)
    in_gather = _push_start([in_shard], [(N_CHIPS,) + in_shard.shape], _route_gather_half, OTHER_CHIPS,
                            name="gather_start_w_in")
    gathers = {}
    order = in_gather[4][:1, :1]
    for n, shard in (('w_out', w_out[0]), ('w_up', w_up[0]), ('w_down', w_down[0])):
        shard = (shard + order).astype(BF16)
        gathers[n] = _push_start([shard], [(N_CHIPS,) + shard.shape], _route_gather, OTHER_CHIPS,
                                 name="gather_start_" + n)
        order = gathers[n][4][:1, :1]

    def weight_in(after):
        send_sems, recv_sems, srcs, lands, _ = in_gather
        (own,), (got,) = _push_wait(send_sems, recv_sems, srcs, lands, after, _route_gather_half_wait, OTHER_CHIPS,
                                    name="gather_wait_w_in")
        got, _ = _forward_halves(got)
        full_in_t = lax.dynamic_update_slice(got, own[None], (chip, 0, 0)).reshape(IN_PROJ_WIDTH, D_MODEL)
        w_in_dt = jnp.pad(full_in_t[MAIN_WIDTH:], ((0, LANES - SSD_HEADS), (0, 0)))
        return full_in_t, w_in_dt, ()

    def weight(name, after):
        send_sems, recv_sems, srcs, lands, _ = gathers[name]
        (own,), (got,) = _push_wait(send_sems, recv_sems, srcs, lands, after, _route_gather_wait, OTHER_CHIPS,
                                    name="gather_wait_" + name)
        whole = lax.dynamic_update_slice(got, own[None], (chip, 0, 0))
        return whole if name == 'w_up' else whole.reshape(-1, D_MODEL)

    reductions, exchanging = {}, {}

    def flush(after):
        tokens = []
        for prev in list(exchanging):
            reductions[prev], token = _grad_scatter_start(exchanging.pop(prev), pos, after)
            tokens.append(token)
        return tokens

    def grad_ready(name, value):
        if name == 'w_in':
            main, dtp = value
            value = lax.dynamic_update_slice(main, dtp[:SSD_HEADS], (MAIN_WIDTH, 0))
        g4 = value if value.ndim == 3 else value.reshape(N_CHIPS, -1, value.shape[1])
        tokens = flush(g4)
        exchanging[name], token = _grad_exchange_start(g4, name, cols=(name == 'w_in'))
        return tokens + [token]

    small = {
        'norm_mix': norm_mix, 'sinks': sinks, 'attn_out_norm': attn_out_norm,
        'ssd_conv_b': ssd_conv_b, 'dt_bias': dt_bias, 'a_log': a_log, 'ssd_d': ssd_d, 'ssd_norm': ssd_norm,
        'norm_ffn': norm_ffn, 'ffn_conv_b': ffn_conv_b, 'norm_final': norm_final,
    }
    loss, dx, g = _local_step(x[0], loss_target[0], small,
                              _StepHooks((gathers['w_down'][4],), weight_in, conv_weights, weight, grad_ready))

    small_names = [n for n in WEIGHTS if n not in BIG]
    small_g = [loss[:, :1]] + [g[n] for n in small_names]
    small_shapes = [(1, 1)] + [tuple(a.shape) for a in small_g[1:]]
    packed = _pack(small_g)
    spread = _push_start([packed], [(8,) + packed.shape], _route_to_all, ALL_OTHERS, name="allreduce_start")
    started = flush(spread[4])[-1]
    grads, deltas, new_m, new_v = {}, {}, {}, {}
    after = started
    shares = {}
    for n in ('w_down', 'w_up', 'w_out'):
        shares[n], after = _grad_sum_and_share(reductions[n], pos, after)
    for n in ('w_down', 'w_up', 'w_out', 'w_in'):
        if n == 'w_in':
            shares[n], after = _grad_sum_and_share(reductions[n], pos, after)
        mine, theirs = _grad_share_wait(shares[n], after)
        if n == 'w_in':
            outs = _adamw_halves(w_in_t, mine, theirs, m_in_t, v_in_t, pos, name="adamw_" + n, cols=True)
            outs = [jnp.transpose(o) for o in outs]
        else:
            outs = _adamw_halves(w[n][0], mine, theirs, m[n][0], v[n][0], pos, name="adamw_" + n)
        after = outs[1]
        grads[n], deltas[n], new_m[n], new_v[n] = [o[None] for o in outs]
    (own,), (landed,) = _push_wait(spread[0], spread[1], spread[2], spread[3], after, _route_to_all_wait, ALL_OTHERS,
                                   name="allreduce_wait")
    landed = lax.dynamic_update_slice(landed, own[None], (4 * xi + 2 * yi + ci, 0, 0))
    red = _unpack(_sum_devices(landed), small_shapes)
    loss_out = red[0].reshape(())
    gsm = dict(zip(small_names, red[1:]))
    gsm['ssd_conv_w'] = lax.dynamic_slice(gsm['ssd_conv_w'], (0, chip * ssd_conv_w.shape[2]),
                                          (SSD_CONV, ssd_conv_w.shape[2]))
    gsm['ffn_conv_w'] = lax.dynamic_slice(gsm['ffn_conv_w'], (0, chip * ffn_conv_w.shape[2]),
                                          (FFN_CONV, ffn_conv_w.shape[2]))

    shapes = [tuple(w[n].shape) for n in small_names]
    gp = _pack([gsm[n] for n in small_names])
    d, m2, v2 = _adamw(_pack([w[n] for n in small_names]), gp, _pack([m[n] for n in small_names]),
                       _pack([v[n] for n in small_names]), name="adamw_small")
    for n, gg, dd, mm, vv in zip(small_names, _unpack(gp, shapes), _unpack(d, shapes), _unpack(m2, shapes),
                                 _unpack(v2, shapes)):
        grads[n], deltas[n], new_m[n], new_v[n] = gg, dd, mm, vv

    return (loss_out, dx[None], *[grads[n] for n in WEIGHTS], *[deltas[n] for n in WEIGHTS],
            *[new_m[n] for n in WEIGHTS], *[new_v[n] for n in WEIGHTS])
```

```python
import functools

import jax
import jax.numpy as jnp
from jax import lax
from jax.experimental import pallas as pl
from jax.experimental.pallas import tpu as pltpu

F32 = jnp.float32
BF16 = jnp.bfloat16

D_MODEL = 2048
N_Q_HEADS = 32
N_KV_HEADS = 8
HEAD_DIM = 64
WINDOW = 128
ATTN_BLOCK = 128
ROT_DIM = 16
ROPE_THETA = 500000.0
SSD_HEADS = 32
SSD_HEAD_DIM = 64
SSD_INNER = 2048
SSD_GROUPS = 8
SSD_STATE = 128
SSD_CONV = 4
SSD_CHUNK = 128
ATTN_WIDTH = 2048
KV_WIDTH = 512
BC_WIDTH = 1024
CONV_CH = 4096
IN_PROJ_WIDTH = 9248
MAIN_WIDTH = 9216
D_FF = 5632
FFN_CONV = 3
EPS = 1e-6
O_Q, O_K, O_V, O_Z, O_XBC, O_DT = 0, 2048, 2560, 3072, 5120, 9216

ADAM_LR = 0.001
ADAM_B1 = 0.9
ADAM_B2 = 0.999
ADAM_EPS = 1e-08
ADAM_WD = 0.01
ADAM_STEP = 10

N_CHIPS = 4
NEG = -1e30
LANES = 128
VMEM_LIMIT = 48 * 1024 * 1024
MESH = pl.DeviceIdType.MESH
HBM_SPEC = pl.BlockSpec(memory_space=pltpu.HBM)
TOKEN = jax.ShapeDtypeStruct((8, LANES), F32)

WEIGHTS = ['norm_mix', 'w_in', 'sinks', 'attn_out_norm', 'ssd_conv_w', 'ssd_conv_b', 'dt_bias', 'a_log', 'ssd_d',
           'ssd_norm', 'w_out', 'norm_ffn', 'w_up', 'ffn_conv_w', 'ffn_conv_b', 'w_down', 'norm_final']
BIG = ['w_in', 'w_out', 'w_up', 'w_down']


def _cp(sem=None, vmem=VMEM_LIMIT):
    kw = {'vmem_limit_bytes': vmem}
    if sem is not None:
        kw['dimension_semantics'] = sem
    return pltpu.CompilerParams(**kw)


def _tile(n, pref):
    if n <= pref:
        return n
    t = (pref // LANES) * LANES
    while t > LANES and n % t:
        t -= LANES
    assert n % t == 0, (n, pref)
    return t


def _rows(n, pref):
    t = min(n, pref)
    while n % t:
        t -= 8
    if 4 * t < pref:
        t = pref
        while n % t:
            t += 8
    return t


def _iota(shape, dim):
    return lax.broadcasted_iota(jnp.int32, shape, dim)


def _dot(a, b, mode='nn'):
    dn = {'nn': (((1,), (0,)), ((), ())), 'nt': (((1,), (1,)), ((), ())), 'tn': (((0,), (0,)), ((), ()))}[mode]
    return lax.dot_general(a.astype(BF16), b.astype(BF16), dn, preferred_element_type=F32)


def _dot_exact(a, b):
    return lax.dot_general(a, b, (((1,), (0,)), ((), ())), precision=lax.Precision.HIGHEST,
                           preferred_element_type=F32)


def _sigmoid(x):
    return 1.0 / (1.0 + jnp.exp(-x))


def _softplus(x):
    return jnp.maximum(x, 0.0) + jnp.log(1.0 + jnp.exp(-jnp.abs(x)))


def _matmul(a, b, *, mode, name, out_dtype=F32, add=None, deps=(), tm=1024, tn=1024, tk=2048,
            a_halves=False, b_halves=False, b_owner=False, owner_major=False, n_limit=None, k_limit=None,
            m_rows=None):
    ash, bsh = (a.shape[1:] if a_halves else a.shape), (b.shape[1:] if (b_halves or b_owner) else b.shape)
    if mode == 'nn':
        (m, k), (k2, n) = ash, bsh
    elif mode == 'nt':
        (m, k), (n, k2) = ash, bsh
    else:
        (k, m), (k2, n) = ash, bsh
    if n_limit is not None:
        assert mode == 'nt' and n_limit <= n
        n = n_limit
    if k_limit is not None:
        assert mode == 'nn' and k_limit <= k2
        k2 = k_limit
    if a_halves:
        assert mode == 'nt'
        k = 2 * k
    if b_halves:
        assert mode == 'tn'
        n = 2 * n
    if b_owner:
        assert mode in ('nn', 'nt')
        if mode == 'nn':
            n = 4 * n
        else:
            k2 = 4 * k2
    assert k == k2, (a.shape, b.shape, mode)
    tm = _tile(m, tm)
    tn = _tile(n // 4 if (owner_major or (b_owner and mode == 'nn')) else (n // 2 if b_halves else n), tn)
    tk = _tile(k // 4 if (b_owner and mode == 'nt') else (k // 2 if a_halves else k), tk)
    nk = k // tk
    has_add = add is not None
    assert not (has_add and owner_major)

    def body(*refs):
        a_ref, b_ref = refs[:2]
        add_ref = refs[2] if has_add else None

        def finish(r, o_ref):
            if has_add:
                r = r + add_ref[...].astype(F32)
            o_ref[...] = r.astype(out_dtype)

        if nk == 1:
            finish(_dot(a_ref[...], b_ref[...], mode), refs[-1])
            return
        o_ref, acc = refs[-2:]
        kk = pl.program_id(2)

        @pl.when(kk == 0)
        def _():
            acc[...] = _dot(a_ref[...], b_ref[...], mode)

        @pl.when((kk > 0) & (kk < nk - 1))
        def _():
            acc[...] += _dot(a_ref[...], b_ref[...], mode)

        @pl.when(kk == nk - 1)
        def _():
            finish(acc[...] + _dot(a_ref[...], b_ref[...], mode), o_ref)

    if mode == 'tn':
        a_spec = pl.BlockSpec((tk, tm), lambda i, j, kk: (kk, i))
    elif a_halves:
        nkh = nk // 2
        a_spec = pl.BlockSpec((None, tm, tk), lambda i, j, kk: (kk // nkh, i, kk % nkh))
    else:
        a_spec = pl.BlockSpec((tm, tk), lambda i, j, kk: (i, kk))
    if mode == 'nt' and b_owner:
        nkq = nk // 4
        b_spec = pl.BlockSpec((None, tn, tk), lambda i, j, kk: (kk // nkq, j, kk % nkq))
    elif mode == 'nt':
        b_spec = pl.BlockSpec((tn, tk), lambda i, j, kk: (j, kk))
    elif b_owner:
        njq = (n // 4) // tn
        b_spec = pl.BlockSpec((None, tk, tn), lambda i, j, kk: (j // njq, kk, j % njq))
    elif b_halves:
        njh = (n // 2) // tn
        b_spec = pl.BlockSpec((None, tk, tn), lambda i, j, kk: (j // njh, kk, j % njh))
    else:
        b_spec = pl.BlockSpec((tk, tn), lambda i, j, kk: (kk, j))
    if owner_major:
        njo = (n // 4) // tn
        o_spec = pl.BlockSpec((None, tm, tn), lambda i, j, kk: (j // njo, i, j % njo))
        out_shape = jax.ShapeDtypeStruct((N_CHIPS, m, n // 4), out_dtype)
    else:
        o_spec = pl.BlockSpec((tm, tn), lambda i, j, kk: (i, j))
        out_shape = jax.ShapeDtypeStruct((m if m_rows is None else m_rows, n), out_dtype)
    dep_spec = pl.BlockSpec((8, LANES), lambda i, j, kk: (0, 0))
    in_specs = [a_spec, b_spec] + ([pl.BlockSpec((tm, tn), lambda i, j, kk: (i, j))] if has_add else [])
    in_specs += [dep_spec] * len(deps)
    args = (a, b) + ((add,) if has_add else ()) + tuple(deps)
    return pl.pallas_call(
        body, name=name, grid=(m // tm, n // tn, nk), in_specs=in_specs, out_specs=o_spec, out_shape=out_shape,
        scratch_shapes=[pltpu.VMEM((tm, tn), F32)] if nk > 1 else [],
        compiler_params=_cp(("parallel", "parallel", "arbitrary")))(*args)


def _rmsnorm_fwd(x, g, name, deps=()):
    t, d = x.shape
    tb = _rows(t, 256)

    def body(x_ref, g_ref, *rest):
        o_ref = rest[-1]
        xv = x_ref[...]
        r = lax.rsqrt(jnp.mean(xv * xv, axis=-1, keepdims=True) + EPS)
        o_ref[...] = (xv * r * g_ref[...]).astype(BF16)

    dep_spec = pl.BlockSpec((8, LANES), lambda i: (0, 0))
    return pl.pallas_call(
        body, name=name, grid=(t // tb,),
        in_specs=[pl.BlockSpec((tb, d), lambda i: (i, 0)), pl.BlockSpec((1, d), lambda i: (0, 0))]
        + [dep_spec] * len(deps),
        out_specs=pl.BlockSpec((tb, d), lambda i: (i, 0)), out_shape=jax.ShapeDtypeStruct((t, d), BF16),
        compiler_params=_cp(("parallel",)))(x, g, *deps)


def _rmsnorm_bwd(x, g, dy, res, name, deps=()):
    t, d = x.shape
    tb = _rows(t, 256)

    def body(x_ref, g_ref, dy_ref, res_ref, *rest):
        dx_ref, dx16_ref, dg_ref = rest[-3:]
        i = pl.program_id(0)
        xv = x_ref[...]
        dyv = dy_ref[...].astype(F32)
        r = lax.rsqrt(jnp.mean(xv * xv, axis=-1, keepdims=True) + EPS)
        u = dyv * g_ref[...]
        dx = r * u - xv * (r * r * r * jnp.mean(u * xv, axis=-1, keepdims=True)) + res_ref[...]
        dx_ref[...] = dx
        dx16_ref[...] = dx.astype(BF16)
        part = jnp.sum(dyv * xv * r, axis=0, keepdims=True)

        @pl.when(i == 0)
        def _():
            dg_ref[...] = part

        @pl.when(i > 0)
        def _():
            dg_ref[...] += part

    row = pl.BlockSpec((tb, d), lambda i: (i, 0))
    vec = pl.BlockSpec((1, d), lambda i: (0, 0))
    return pl.pallas_call(
        body, name=name, grid=(t // tb,),
        in_specs=[row, vec, row, row] + [pl.BlockSpec((8, LANES), lambda i: (0, 0))] * len(deps),
        out_specs=[row, row, vec],
        out_shape=[jax.ShapeDtypeStruct((t, d), F32), jax.ShapeDtypeStruct((t, d), BF16),
                   jax.ShapeDtypeStruct((1, d), F32)],
        compiler_params=_cp(("arbitrary",)))(x, g, dy, res, *deps)


def _final_loss(h, g, tgt):
    t, d = h.shape
    tb = _rows(t, 256)

    def body(h_ref, g_ref, t_ref, loss_ref, dh_ref, dh16_ref, dg_ref):
        i = pl.program_id(0)
        hv = h_ref[...]
        gv = g_ref[...]
        r = lax.rsqrt(jnp.mean(hv * hv, axis=-1, keepdims=True) + EPS)
        y = hv * r * gv
        diff = y - t_ref[...]
        lpart = jnp.sum(jnp.sum(diff * diff, axis=1, keepdims=True), axis=0, keepdims=True) * (0.5 / d)
        dy = diff * (1.0 / d)
        u = dy * gv
        dh = r * u - hv * (r * r * r * jnp.mean(u * hv, axis=-1, keepdims=True))
        dh_ref[...] = dh
        dh16_ref[...] = dh.astype(BF16)
        gpart = jnp.sum(dy * hv * r, axis=0, keepdims=True)
        lrow = jnp.broadcast_to(lpart, (1, LANES))

        @pl.when(i == 0)
        def _():
            loss_ref[...] = lrow
            dg_ref[...] = gpart

        @pl.when(i > 0)
        def _():
            loss_ref[...] += lrow
            dg_ref[...] += gpart

    row = pl.BlockSpec((tb, d), lambda i: (i, 0))
    vec = pl.BlockSpec((1, d), lambda i: (0, 0))
    return pl.pallas_call(
        body, name="final_loss", grid=(t // tb,), in_specs=[row, vec, row],
        out_specs=[pl.BlockSpec((1, LANES), lambda i: (0, 0)), row, row, vec],
        out_shape=[jax.ShapeDtypeStruct((1, LANES), F32), jax.ShapeDtypeStruct((t, d), F32),
                   jax.ShapeDtypeStruct((t, d), BF16), jax.ShapeDtypeStruct((1, d), F32)],
        compiler_params=_cp(("arbitrary",)))(h, g, tgt)


def _rope_tables(t):
    pos = jnp.arange(t, dtype=F32)
    inv = 1.0 / (ROPE_THETA ** (jnp.arange(0, ROT_DIM, 2, dtype=F32) / ROT_DIM))
    ang = pos[:, None] * inv[None, :]
    cos, sin = jnp.cos(ang), jnp.sin(ang)
    half = ROT_DIM // 2
    rest = HEAD_DIM - ROT_DIM
    c = jnp.concatenate([cos, cos, jnp.ones((t, rest), F32)], axis=1)
    s1 = jnp.concatenate([-sin, jnp.zeros((t, half + rest), F32)], axis=1)
    s2 = jnp.concatenate([jnp.zeros((t, half), F32), sin, jnp.zeros((t, rest), F32)], axis=1)
    return jnp.concatenate([jnp.tile(v, (1, LANES // HEAD_DIM)) for v in (c, s1, s2)], axis=1)


def _split_tables(tab):
    return tab[:, :LANES], tab[:, LANES:2 * LANES], tab[:, 2 * LANES:]


def _rope(x, c, s1, s2):
    half = ROT_DIM // 2
    return x * c + pltpu.roll(x, LANES - half, 1) * s1 + pltpu.roll(x, half, 1) * s2


def _rope_t(g, c, s1, s2):
    half = ROT_DIM // 2
    return g * c + pltpu.roll(g * s1, half, 1) + pltpu.roll(g * s2, LANES - half, 1)


def _band_masks(i, heads):
    n = heads * ATTN_BLOCK
    q = jnp.bitwise_and(_iota((n, ATTN_BLOCK), 0), ATTN_BLOCK - 1)
    j = _iota((n, ATTN_BLOCK), 1)
    upper = j > q
    return upper, upper & (j < jnp.where(i > 0, 0, ATTN_BLOCK))


def _fold_band(full, upper):
    return jnp.where(upper, full[:, :ATTN_BLOCK], full[:, ATTN_BLOCK:])


def _unfold_band(band, upper):
    return jnp.concatenate([jnp.where(upper, band, 0.0), jnp.where(upper, 0.0, band)], axis=1)


def _half_masks():
    lane = _iota((1, LANES), 1)
    return [(lane < HEAD_DIM).astype(F32), (lane >= HEAD_DIM).astype(F32)]


def _stack_heads(blocks, hm, j):
    pieces = []
    for r in range(4):
        qb, half = (4 * j + r) // 2, (4 * j + r) % 2
        piece = blocks[qb] * hm[half]
        if half != j:
            piece = pltpu.roll(piece, HEAD_DIM, 1)
        pieces.append(piece)
    return jnp.concatenate(pieces, axis=0)


def _unstack_heads(stacked, j):
    out = []
    for qb in (2 * j, 2 * j + 1):
        acc = None
        for half in range(2):
            r = 2 * qb + half - 4 * j
            piece = stacked[r * ATTN_BLOCK:(r + 1) * ATTN_BLOCK]
            if half != j:
                piece = pltpu.roll(piece, HEAD_DIM, 1)
            acc = piece if acc is None else acc + piece
        out.append((qb, acc))
    return out


def _sink_column(sink_ref, base):
    return jnp.concatenate([jnp.full((ATTN_BLOCK, 1), sink_ref[base + r], F32) for r in range(4)], axis=0)


def _attn_specs(nb_clamp):
    blk = ATTN_BLOCK
    kb, vb = O_K // LANES, O_V // LANES

    def cur(i):
        return jnp.minimum(i, nb_clamp)

    def prev(i):
        return jnp.maximum(jnp.minimum(i, nb_clamp + 1) - 1, 0)

    q = pl.BlockSpec((blk, 512), lambda p, i: (cur(i), p))
    kc = pl.BlockSpec((blk, LANES), lambda p, i: (cur(i), kb + p))
    kp = pl.BlockSpec((blk, LANES), lambda p, i: (prev(i), kb + p))
    vc = pl.BlockSpec((blk, LANES), lambda p, i: (cur(i), vb + p))
    vp = pl.BlockSpec((blk, LANES), lambda p, i: (prev(i), vb + p))
    tc = pl.BlockSpec((blk, 3 * LANES), lambda p, i: (cur(i), 0))
    tp = pl.BlockSpec((blk, 3 * LANES), lambda p, i: (prev(i), 0))
    return q, kc, kp, vc, vp, tc, tp


def _attn_fwd(proj, sinks, tables):
    t = proj.shape[0]
    nb = t // ATTN_BLOCK
    scale = HEAD_DIM ** -0.5

    def body(sink_ref, q_ref, kc_ref, kp_ref, vc_ref, vp_ref, tc_ref, tp_ref, o_ref):
        p = pl.program_id(0)
        i = pl.program_id(1)
        cc, s1c, s2c = _split_tables(tc_ref[...])
        kband = jnp.concatenate([_rope(kp_ref[...], *_split_tables(tp_ref[...])),
                                 _rope(kc_ref[...], cc, s1c, s2c)], axis=0).astype(BF16)
        vband = jnp.concatenate([vp_ref[...], vc_ref[...]], axis=0)
        hm = _half_masks()
        vsel = [(vband * hm[j]).astype(BF16) for j in range(2)]
        upper, dropped = _band_masks(i, 1)
        qr = [_rope(q_ref[:, qb * LANES:(qb + 1) * LANES], cc, s1c, s2c) for qb in range(4)]

        def scores(hh):
            qb, half, j = hh // 2, hh % 2, hh // 4
            qs = qr[qb] * hm[half]
            if half != j:
                qs = pltpu.roll(qs, HEAD_DIM, 1)
            return _dot(qs, kband, 'nt')

        ahead = scores(0)
        acc = None
        for hh in range(8):
            qb, half, j = hh // 2, hh % 2, hh // 4
            raw = ahead
            if hh + 1 < 8:
                ahead = scores(hh + 1)
            s = jnp.where(dropped, NEG, _fold_band(raw, upper) * scale)
            sink = sink_ref[p * 8 + hh]
            m = jnp.maximum(jnp.max(s, axis=1, keepdims=True), sink)
            pe = jnp.exp(s - m)
            den = jnp.sum(pe, axis=1, keepdims=True) + jnp.exp(sink - m)
            o = _dot(_unfold_band(pe / den, upper), vsel[j])
            if half != j:
                o = pltpu.roll(o, HEAD_DIM, 1)
            acc = o if half == 0 else acc + o
            if half == 1:
                o_ref[:, qb * LANES:(qb + 1) * LANES] = acc

    q, kc, kp, vc, vp, tc, tp = _attn_specs(nb - 1)
    smem = pl.BlockSpec(memory_space=pltpu.SMEM)
    return pl.pallas_call(
        body, name="attn_fwd", grid=(4, nb),
        in_specs=[smem, q, kc, kp, vc, vp, tc, tp],
        out_specs=pl.BlockSpec((ATTN_BLOCK, 512), lambda p, i: (i, p)),
        out_shape=jax.ShapeDtypeStruct((t, ATTN_WIDTH), F32),
        compiler_params=_cp(("parallel", "arbitrary")))(sinks, proj, proj, proj, proj, proj, tables, tables)


def _attn_bwd(proj, sinks, tables, dout):
    t = proj.shape[0]
    nb = t // ATTN_BLOCK
    scale = HEAD_DIM ** -0.5

    def body(sink_ref, q_ref, kc_ref, kp_ref, vc_ref, vp_ref, tc_ref, tp_ref,
             do_ref, dq_ref, dk_ref, dv_ref, ds_ref, carry_k, carry_v):
        p = pl.program_id(0)
        i = pl.program_id(1)
        ptab = _split_tables(tp_ref[...])

        @pl.when(i == 0)
        def _():
            carry_k[...] = jnp.zeros_like(carry_k)
            carry_v[...] = jnp.zeros_like(carry_v)
            ds_ref[...] = jnp.zeros_like(ds_ref)

        @pl.when(i < nb)
        def _():
            cc, s1c, s2c = _split_tables(tc_ref[...])
            kband = jnp.concatenate([_rope(kp_ref[...], *ptab), _rope(kc_ref[...], cc, s1c, s2c)], axis=0)
            vband = jnp.concatenate([vp_ref[...], vc_ref[...]], axis=0)
            hm = _half_masks()
            kband16 = kband.astype(BF16)
            vband16 = vband.astype(BF16)
            upper, dropped = _band_masks(i, 4)
            dkb = jnp.zeros((2 * ATTN_BLOCK, LANES), F32)
            dvb = jnp.zeros((2 * ATTN_BLOCK, LANES), F32)
            row8 = _iota((8, LANES), 0)
            dsink = jnp.zeros((8, LANES), F32)
            qr = [_rope(q_ref[:, qb * LANES:(qb + 1) * LANES], cc, s1c, s2c) for qb in range(4)]
            dob = [do_ref[:, qb * LANES:(qb + 1) * LANES] for qb in range(4)]
            for j in range(2):
                qst = _stack_heads(qr, hm, j).astype(BF16)
                dost = _stack_heads(dob, hm, j).astype(BF16)
                s = jnp.where(dropped, NEG, _fold_band(_dot(qst, kband16, 'nt'), upper) * scale)
                sink = _sink_column(sink_ref, p * 8 + 4 * j)
                m = jnp.maximum(jnp.max(s, axis=1, keepdims=True), sink)
                pe = jnp.exp(s - m)
                psink = jnp.exp(sink - m)
                den = jnp.sum(pe, axis=1, keepdims=True) + psink
                pr = pe / den
                dvb = dvb + _dot(_unfold_band(pr, upper).T, dost)
                dp = _fold_band(_dot(dost, vband16, 'nt'), upper)
                delta = jnp.sum(pr * dp, axis=1, keepdims=True)
                dsc = _unfold_band(pr * (dp - delta) * scale, upper)
                dsk = psink / den * delta
                for r in range(4):
                    part = jnp.sum(dsk[r * ATTN_BLOCK:(r + 1) * ATTN_BLOCK])
                    dsink = dsink + jnp.where(row8 == 4 * j + r, -part, 0.0)
                for qb, dqb in _unstack_heads(_dot(dsc, kband * hm[j]), j):
                    dq_ref[:, qb * LANES:(qb + 1) * LANES] = _rope_t(dqb, cc, s1c, s2c).astype(BF16)
                dkb = dkb + _dot(dsc.T, qst)
            ds_ref[0] += dsink
            dk_ref[...] = _rope_t(carry_k[...] + dkb[:ATTN_BLOCK], *ptab).astype(BF16)
            dv_ref[...] = (carry_v[...] + dvb[:ATTN_BLOCK]).astype(BF16)
            carry_k[...] = dkb[ATTN_BLOCK:]
            carry_v[...] = dvb[ATTN_BLOCK:]

        @pl.when(i == nb)
        def _():
            dk_ref[...] = _rope_t(carry_k[...], *ptab).astype(BF16)
            dv_ref[...] = carry_v[...].astype(BF16)

    q, kc, kp, vc, vp, tc, tp = _attn_specs(nb - 1)
    smem = pl.BlockSpec(memory_space=pltpu.SMEM)
    qblk = pl.BlockSpec((ATTN_BLOCK, 512), lambda p, i: (jnp.minimum(i, nb - 1), p))
    kvout = pl.BlockSpec((ATTN_BLOCK, LANES), lambda p, i: (jnp.maximum(i - 1, 0), p))
    return pl.pallas_call(
        body, name="attn_bwd", grid=(4, nb + 1),
        in_specs=[smem, q, kc, kp, vc, vp, tc, tp, qblk],
        out_specs=[qblk, kvout, kvout, pl.BlockSpec((1, 8, LANES), lambda p, i: (p, 0, 0))],
        out_shape=[jax.ShapeDtypeStruct((t, ATTN_WIDTH), BF16), jax.ShapeDtypeStruct((t, KV_WIDTH), BF16),
                   jax.ShapeDtypeStruct((t, KV_WIDTH), BF16), jax.ShapeDtypeStruct((4, 8, LANES), F32)],
        scratch_shapes=[pltpu.VMEM((ATTN_BLOCK, LANES), F32), pltpu.VMEM((ATTN_BLOCK, LANES), F32)],
        compiler_params=_cp(("parallel", "arbitrary")))(sinks, proj, proj, proj, proj, proj, tables, tables, dout)


def _shift_rows(x, prev8, j):
    n, c = x.shape
    r = pltpu.roll(x.reshape(n // 8, 8, c), j, 1)
    before = pltpu.roll(prev8, j, 0)[None]
    if n > 8:
        before = jnp.concatenate([before, r[:-1]], axis=0)
    return jnp.where(_iota((1, 8, 1), 1) < j, before, r).reshape(n, c)


def _shift_rows_up(x, next8, j):
    n, c = x.shape
    r = pltpu.roll(x.reshape(n // 8, 8, c), 8 - j, 1)
    after = pltpu.roll(next8, 8 - j, 0)[None]
    if n > 8:
        after = jnp.concatenate([r[1:], after], axis=0)
    return jnp.where(_iota((1, 8, 1), 1) >= 8 - j, after, r).reshape(n, c)


def _conv_apply(x, prev8, w, b, taps):
    u = b + x * w[taps - 1:taps]
    for j in range(1, taps):
        u = u + _shift_rows(x, prev8, j) * w[taps - 1 - j:taps - j]
    return u


def _conv_grads(du, du_next8, x, w, taps):
    dx = du * w[taps - 1:taps]
    rowk = _iota((taps, 1), 0)
    dw = jnp.where(rowk == taps - 1, jnp.sum(du * x, axis=0, keepdims=True), 0.0)
    for j in range(1, taps):
        ahead = _shift_rows_up(du, du_next8, j)
        dx = dx + ahead * w[taps - 1 - j:taps - j]
        dw = dw + jnp.where(rowk == taps - 1 - j, jnp.sum(ahead * x, axis=0, keepdims=True), 0.0)
    return dx, dw, jnp.sum(du, axis=0, keepdims=True)


def _conv_specs(tb, tc, col0, t):
    c0 = col0 // tc
    cur = pl.BlockSpec((tb, tc), lambda j, i: (i, c0 + j))
    prev = pl.BlockSpec((8, tc), lambda j, i: (jnp.maximum(i * (tb // 8) - 1, 0), c0 + j))
    nxt = pl.BlockSpec((8, tc), lambda j, i: (jnp.minimum((i + 1) * (tb // 8), t // 8 - 1), c0 + j))
    return cur, prev, nxt


def _conv_silu_fwd(x, w, b, *, col0, width, name):
    t = x.shape[0]
    taps = w.shape[0]
    tb, tc = _rows(t, 512), _tile(width, 1024)
    assert col0 % tc == 0

    def body(x_ref, xp_ref, w_ref, b_ref, o_ref, u_ref):
        i = pl.program_id(1)
        prev8 = jnp.where(i > 0, xp_ref[...], 0.0)
        u = _conv_apply(x_ref[...], prev8, w_ref[...], b_ref[...], taps)
        u_ref[...] = u
        o_ref[...] = u * _sigmoid(u)

    cur, prev, _ = _conv_specs(tb, tc, col0, t)
    par = pl.BlockSpec((taps, tc), lambda j, i: (0, j))
    bias = pl.BlockSpec((1, tc), lambda j, i: (0, j))
    out = pl.BlockSpec((tb, tc), lambda j, i: (i, j))
    shp = jax.ShapeDtypeStruct((t, width), F32)
    return pl.pallas_call(
        body, name=name, grid=(width // tc, t // tb), in_specs=[cur, prev, par, bias], out_specs=[out, out],
        out_shape=[shp, shp], compiler_params=_cp(("parallel", "parallel")))(x, x, w, b)


def _dsilu(u):
    sg = _sigmoid(u)
    return sg * (1.0 + u * (1.0 - sg))


def _ssd_conv_bwd(x, w, dxs, dbm, dcm, *, col0, name):
    t = x.shape[0]
    taps = w.shape[0]
    tb, tc = _rows(t, 512), BC_WIDTH
    nrow, ncol = t // tb, CONV_CH // tc
    c0 = col0 // tc

    def body(x_ref, w_ref, xs_ref, xsn_ref, bm_ref, bmn_ref, cm_ref, cmn_ref, dx_ref, dw_ref, db_ref):
        i = pl.program_id(0)
        j = pl.program_id(1)

        def run(du_ref, dun_ref):
            next8 = jnp.where(i < nrow - 1, dun_ref[...], 0.0)
            dx, dwv, dbv = _conv_grads(du_ref[...], next8, x_ref[...], w_ref[...], taps)
            dx_ref[...] = dx.astype(BF16)

            @pl.when(i == 0)
            def _():
                dw_ref[j] = dwv
                db_ref[j] = dbv

            @pl.when(i > 0)
            def _():
                dw_ref[j] += dwv
                db_ref[j] += dbv

        pl.when(j < 2)(lambda: run(xs_ref, xsn_ref))
        pl.when(j == 2)(lambda: run(bm_ref, bmn_ref))
        pl.when(j == 3)(lambda: run(cm_ref, cmn_ref))

    def nxt_row(i):
        return jnp.minimum((i + 1) * (tb // 8), t // 8 - 1)

    xs_col = lambda j: jnp.minimum(j, SSD_INNER // tc - 1)
    in_specs = [pl.BlockSpec((tb, tc), lambda i, j: (i, c0 + j)), pl.BlockSpec((taps, tc), lambda i, j: (0, j)),
                pl.BlockSpec((tb, tc), lambda i, j: (i, xs_col(j))),
                pl.BlockSpec((8, tc), lambda i, j: (nxt_row(i), xs_col(j))),
                pl.BlockSpec((tb, tc), lambda i, j: (i, 0)), pl.BlockSpec((8, tc), lambda i, j: (nxt_row(i), 0)),
                pl.BlockSpec((tb, tc), lambda i, j: (i, 0)), pl.BlockSpec((8, tc), lambda i, j: (nxt_row(i), 0))]
    dx, dw, db = pl.pallas_call(
        body, name=name, grid=(nrow, ncol), in_specs=in_specs,
        out_specs=[pl.BlockSpec((tb, tc), lambda i, j: (i, j)),
                   pl.BlockSpec((ncol, taps, tc), lambda i, j: (0, 0, 0)),
                   pl.BlockSpec((ncol, 1, tc), lambda i, j: (0, 0, 0))],
        out_shape=[jax.ShapeDtypeStruct((t, CONV_CH), BF16), jax.ShapeDtypeStruct((ncol, taps, tc), F32),
                   jax.ShapeDtypeStruct((ncol, 1, tc), F32)],
        compiler_params=_cp(("arbitrary", "arbitrary")))(x, w, dxs, dxs, dbm, dbm, dcm, dcm)
    return dx, dw.transpose(1, 0, 2).reshape(taps, CONV_CH), db.transpose(1, 0, 2).reshape(1, CONV_CH)


def _ffn_specs(tb, tc, t):
    nc = D_FF // tc

    def cur(half):
        return pl.BlockSpec((tb, tc), lambda j, i: (i, half * nc + j))

    def prev(half):
        return pl.BlockSpec((8, tc), lambda j, i: (jnp.maximum(i * (tb // 8) - 1, 0), half * nc + j))

    def nxt(half):
        return pl.BlockSpec((8, tc), lambda j, i: (jnp.minimum((i + 1) * (tb // 8), t // 8 - 1), half * nc + j))

    def par(rows, half):
        return pl.BlockSpec((rows, tc), lambda j, i: (0, half * nc + j))

    return cur, prev, nxt, par


def _ffn_act_fwd(u0, w, b):
    t = u0.shape[0]
    tb, tc = _rows(t, 512), _tile(D_FF, 1408)
    cur, prev, _, par = _ffn_specs(tb, tc, t)

    def body(g_ref, gp_ref, v_ref, vp_ref, wg_ref, wv_ref, bg_ref, bv_ref, o_ref, u_ref):
        i = pl.program_id(1)
        ug = _conv_apply(g_ref[...], jnp.where(i > 0, gp_ref[...], 0.0), wg_ref[...], bg_ref[...], FFN_CONV)
        uv = _conv_apply(v_ref[...], jnp.where(i > 0, vp_ref[...], 0.0), wv_ref[...], bv_ref[...], FFN_CONV)
        o_ref[...] = (ug * _sigmoid(ug) * uv).astype(BF16)
        u_ref[0] = ug
        u_ref[1] = uv

    return pl.pallas_call(
        body, name="ffn_act_fwd", grid=(D_FF // tc, t // tb),
        in_specs=[cur(0), prev(0), cur(1), prev(1), par(FFN_CONV, 0), par(FFN_CONV, 1), par(1, 0), par(1, 1)],
        out_specs=[pl.BlockSpec((tb, tc), lambda j, i: (i, j)), pl.BlockSpec((2, tb, tc), lambda j, i: (0, i, j))],
        out_shape=[jax.ShapeDtypeStruct((t, D_FF), BF16), jax.ShapeDtypeStruct((2, t, D_FF), F32)],
        compiler_params=_cp(("parallel", "parallel")))(u0, u0, u0, u0, w, w, b, b)


def _ffn_act_bwd(u0, u, w, da):
    t = u0.shape[0]
    tb, tc = _rows(t, 256), _tile(D_FF, 1408)
    nrow = t // tb
    taps = FFN_CONV
    cur, _, _, par = _ffn_specs(tb, tc, t)

    def dact(ug, uv, dav):
        sg = _sigmoid(ug)
        return dav * uv * (sg * (1.0 + ug * (1.0 - sg))), dav * ug * sg

    def body(g_ref, v_ref, u_ref, un_ref, wg_ref, wv_ref, da_ref, dan_ref, dx_ref, dw_ref, db_ref):
        i = pl.program_id(1)
        dug, duv = dact(u_ref[0], u_ref[1], da_ref[...].astype(F32))
        dan = jnp.where(i < nrow - 1, dan_ref[...].astype(F32)[:8], 0.0)
        dugn, duvn = dact(un_ref[0], un_ref[1], dan)
        dxg, dwg, dbg = _conv_grads(dug, dugn, g_ref[...], wg_ref[...], taps)
        dxv, dwv, dbv = _conv_grads(duv, duvn, v_ref[...], wv_ref[...], taps)
        dx_ref[0] = dxg.astype(BF16)
        dx_ref[1] = dxv.astype(BF16)

        @pl.when(i == 0)
        def _():
            dw_ref[0] = dwg
            dw_ref[1] = dwv
            db_ref[0] = dbg
            db_ref[1] = dbv

        @pl.when(i > 0)
        def _():
            dw_ref[0] += dwg
            dw_ref[1] += dwv
            db_ref[0] += dbg
            db_ref[1] += dbv

    both = pl.BlockSpec((2, tb, tc), lambda j, i: (0, i, j))
    both_nxt = pl.BlockSpec((2, 8, tc), lambda j, i: (0, jnp.minimum((i + 1) * (tb // 8), t // 8 - 1), j))
    da_cur = pl.BlockSpec((tb, tc), lambda j, i: (i, j))
    da_nxt = pl.BlockSpec((16, tc), lambda j, i: (jnp.minimum((i + 1) * (tb // 16), t // 16 - 1), j))
    return pl.pallas_call(
        body, name="ffn_act_bwd", grid=(D_FF // tc, nrow),
        in_specs=[cur(0), cur(1), both, both_nxt, par(taps, 0), par(taps, 1), da_cur, da_nxt],
        out_specs=[both, pl.BlockSpec((2, taps, tc), lambda j, i: (0, 0, j)),
                   pl.BlockSpec((2, 1, tc), lambda j, i: (0, 0, j))],
        out_shape=[jax.ShapeDtypeStruct((2, t, D_FF), BF16), jax.ShapeDtypeStruct((2, taps, D_FF), F32),
                   jax.ShapeDtypeStruct((2, 1, D_FF), F32)],
        compiler_params=_cp(("parallel", "arbitrary")))(u0, u0, u, u, w, w, da, da)


def _head_masks():
    lane = _iota((1, 4 * SSD_HEAD_DIM), 1)
    return [((lane >= r * SSD_HEAD_DIM) & (lane < (r + 1) * SSD_HEAD_DIM)).astype(F32) for r in range(4)]


def _segsum(v):
    first = _iota((1, LANES), 1) < SSD_HEAD_DIM
    halves = []
    for k in range(2):
        vh = v[:, k * LANES:(k + 1) * LANES]
        both = jnp.sum(vh, axis=1, keepdims=True)
        one = jnp.sum(jnp.where(first, vh, 0.0), axis=1, keepdims=True)
        halves.append(jnp.where(first, one, both - one))
    return jnp.concatenate(halves, axis=1)


def _ssd_common(raw_e, prow, rawr4, bcol, acol):
    n = SSD_CHUNK
    dt_e = _softplus(raw_e + prow[0:1, :])
    a_e = -jnp.exp(prow[1:2, :])
    d_e = prow[2:3, :]
    tril = (_iota((n, n), 0) >= _iota((n, n), 1)).astype(F32)
    acs_e = _dot_exact(tril, dt_e * a_e)
    last_e = acs_e[n - 1:n, :]
    dtr4 = _softplus(rawr4 + bcol)
    triu = (_iota((n, n), 0) <= _iota((n, n), 1)).astype(F32)
    acs_r4 = _dot_exact(dtr4 * (-jnp.exp(acol)), triu)
    return dt_e, a_e, d_e, acs_e, last_e, acs_r4


def _decay_matrix(acs_e, acs_r4, r):
    n = SSD_CHUNK
    col = acs_e[:, r * SSD_HEAD_DIM:r * SSD_HEAD_DIM + 1]
    seg = col - acs_r4[r:r + 1, :]
    causal = _iota((n, n), 0) >= _iota((n, n), 1)
    return jnp.exp(jnp.where(causal, seg, NEG))


SSD_STEP_CHUNKS = 4
SSD_ROWS = SSD_STEP_CHUNKS * SSD_CHUNK


def _ssd_specs(t, rev):
    nb = t // SSD_ROWS
    xb, bb, cb = 0, SSD_INNER // SSD_STATE, (SSD_INNER + BC_WIDTH) // SSD_STATE

    def ch(c):
        return (nb - 1 - c) if rev else c

    x = pl.BlockSpec((SSD_ROWS, 256), lambda g, c: (ch(c), xb + g))
    bm = pl.BlockSpec((SSD_ROWS, SSD_STATE), lambda g, c: (ch(c), bb + g))
    cm = pl.BlockSpec((SSD_ROWS, SSD_STATE), lambda g, c: (ch(c), cb + g))
    dtc = pl.BlockSpec((1, SSD_ROWS, 256), lambda g, c: (g, ch(c), 0))
    dtr = pl.BlockSpec((1, 4, SSD_ROWS), lambda g, c: (g, 0, ch(c)))
    prow = pl.BlockSpec((1, 3, 256), lambda g, c: (g, 0, 0))
    pcol = pl.BlockSpec((1, 4, 1), lambda g, c: (g, 0, 0))
    st = pl.BlockSpec((1, SSD_STEP_CHUNKS, SSD_STATE, 256), lambda g, c: (g, ch(c), 0, 0))
    return x, bm, cm, dtc, dtr, prow, pcol, st, ch


def _ssd_params(dt_raw, dt_bias, a_log, ssd_d):
    t = dt_raw.shape[0]
    by_group = dt_raw.reshape(t, SSD_GROUPS, 4)
    dtc = jnp.repeat(by_group, SSD_HEAD_DIM, axis=2).transpose(1, 0, 2)
    dtr = by_group.transpose(1, 2, 0)
    prow = jnp.repeat(jnp.stack([dt_bias.reshape(SSD_GROUPS, 4), a_log.reshape(SSD_GROUPS, 4),
                                 ssd_d.reshape(SSD_GROUPS, 4)], axis=1), SSD_HEAD_DIM, axis=2)
    bcol = dt_bias.reshape(SSD_GROUPS, 4, 1)
    acol = a_log.reshape(SSD_GROUPS, 4, 1)
    return dtc, dtr, prow, bcol, acol


def _ssd_fwd(xbc, params):
    t = xbc.shape[0]
    nc = t // SSD_CHUNK
    dtc, dtr, prow, bcol, acol = params

    def body(x_ref, b_ref, c_ref, dtc_ref, dtr_ref, prow_ref, bcol_ref, acol_ref, y_ref, st_ref, s_scr):
        c = pl.program_id(1)

        @pl.when(c == 0)
        def _():
            s_scr[...] = jnp.zeros_like(s_scr)

        masks = _head_masks()
        s = s_scr[...]
        for k in range(SSD_STEP_CHUNKS):
            rows = slice(k * SSD_CHUNK, (k + 1) * SSD_CHUNK)
            dt_e, a_e, d_e, acs_e, last_e, acs_r4 = _ssd_common(
                dtc_ref[0, rows], prow_ref[0], dtr_ref[0][:, rows], bcol_ref[0], acol_ref[0])
            xv = x_ref[rows]
            bm, cm = b_ref[rows], c_ref[rows]
            st_ref[0, k] = s
            xdt = xv * dt_e
            cb = _dot(cm, bm, 'nt')
            y = _dot(cm, s) * jnp.exp(acs_e) + xv * d_e
            for r in range(4):
                mr = cb * _decay_matrix(acs_e, acs_r4, r)
                y = y + _dot(mr, xdt * masks[r])
            y_ref[rows] = y
            w = xdt * jnp.exp(last_e - acs_e)
            s = s * jnp.exp(last_e) + _dot(bm.T, w)
        s_scr[...] = s

    x, bm, cm, dtcs, dtrs, prs, pcs, st, _ = _ssd_specs(t, False)
    return pl.pallas_call(
        body, name="ssd_fwd", grid=(SSD_GROUPS, t // SSD_ROWS), in_specs=[x, bm, cm, dtcs, dtrs, prs, pcs, pcs],
        out_specs=[pl.BlockSpec((SSD_ROWS, 256), lambda g, c: (c, g)), st],
        out_shape=[jax.ShapeDtypeStruct((t, SSD_INNER), F32),
                   jax.ShapeDtypeStruct((SSD_GROUPS, nc, SSD_STATE, 256), F32)],
        scratch_shapes=[pltpu.VMEM((SSD_STATE, 256), F32)],
        compiler_params=_cp(("parallel", "arbitrary")))(xbc, xbc, xbc, dtc, dtr, prow, bcol, acol)


def _ssd_bwd(xbc, pre, params, states, dy):
    t = xbc.shape[0]
    nc = t // SSD_CHUNK
    n = SSD_CHUNK
    dtc, dtr, prow, bcol, acol = params

    def body(x_ref, b_ref, c_ref, ux_ref, ub_ref, uc_ref, dtc_ref, dtr_ref, prow_ref, bcol_ref, acol_ref, st_ref,
             dy_ref, dx_ref, db_ref, dc_ref, ddt_ref, dp_ref, ds_scr):
        c = pl.program_id(1)

        @pl.when(c == 0)
        def _():
            ds_scr[...] = jnp.zeros_like(ds_scr)
            dp_ref[...] = jnp.zeros_like(dp_ref)

        masks = _head_masks()
        ds = ds_scr[...]
        for k in reversed(range(SSD_STEP_CHUNKS)):
            rows = slice(k * SSD_CHUNK, (k + 1) * SSD_CHUNK)
            raw_e = dtc_ref[0, rows]
            prw = prow_ref[0]
            dt_e, a_e, d_e, acs_e, last_e, acs_r4 = _ssd_common(raw_e, prw, dtr_ref[0][:, rows], bcol_ref[0], acol_ref[0])
            xv = x_ref[rows]
            bm, cm = b_ref[rows], c_ref[rows]
            s = st_ref[0, k]
            dyv = dy_ref[rows]
            e_e = jnp.exp(acs_e)
            dec_e = jnp.exp(last_e - acs_e)
            cd_e = jnp.exp(last_e)
            xdt = xv * dt_e
            w = xdt * dec_e
            b16, c16, s16, ds16 = bm.astype(BF16), cm.astype(BF16), s.astype(BF16), ds.astype(BF16)
            cb = _dot(c16, b16, 'nt')
            yoff_raw = _dot(c16, s16)
            dye = dyv * e_e
            dye16 = dye.astype(BF16)
            dcm = _dot(dye16, s16, 'nt')
            ds_prev = ds * cd_e + _dot(cm.T, dye16)
            dacs_e = _segsum(dyv * yoff_raw) * e_e
            dw = _dot(b16, ds16)
            dbm = _dot(w, ds16, 'nt')
            tdec = _segsum(dw * xdt) * dec_e
            dacs_e = dacs_e - tdec
            dlast_e = jnp.sum(tdec, axis=0, keepdims=True)
            dxdt = dw * dec_e
            dlast_e = dlast_e + _segsum(jnp.sum(ds * s, axis=0, keepdims=True)) * cd_e
            dcb = jnp.zeros((n, n), F32)
            for r in range(4):
                lm = _decay_matrix(acs_e, acs_r4, r)
                mr = cb * lm
                dyr16 = (dyv * masks[r]).astype(BF16)
                dm = _dot(dyr16, xdt * masks[r], 'nt')
                dcb = dcb + dm * lm
                dseg = dm * mr
                dcol = jnp.sum(dseg, axis=1, keepdims=True) - jnp.sum(dseg.T, axis=1, keepdims=True)
                dacs_e = dacs_e + dcol * masks[r]
                dxdt = dxdt + _dot(mr.T, dyr16)
            dcm = dcm + _dot(dcb, b16)
            dbm = dbm + _dot(dcb.T, c16)
            dacs_e = dacs_e + jnp.where(_iota((n, 1), 0) == n - 1, dlast_e, 0.0)
            triu = (_iota((n, n), 0) <= _iota((n, n), 1)).astype(F32)
            ddta_e = _dot_exact(triu, dacs_e)
            ddt_e = ddta_e * a_e + _segsum(dxdt * xv)
            dx_ref[rows] = (dxdt * dt_e + dyv * d_e) * _dsilu(ux_ref[rows])
            db_ref[rows] = dbm * _dsilu(ub_ref[rows])
            dc_ref[rows] = dcm * _dsilu(uc_ref[rows])
            draw_e = ddt_e * _sigmoid(raw_e + prw[0:1, :])
            draw_t = draw_e.T
            ddt_ref[0, :, rows] = jnp.concatenate([draw_t[r * SSD_HEAD_DIM:r * SSD_HEAD_DIM + 1] for r in range(4)], axis=0)
            dbias = jnp.sum(draw_e, axis=0, keepdims=True)
            dalog = jnp.sum(ddta_e * dt_e, axis=0, keepdims=True) * a_e
            dd = _segsum(jnp.sum(dyv * xv, axis=0, keepdims=True))
            row3 = _iota((3, 1), 0)
            dp_ref[0] += (jnp.where(row3 == 0, dbias, 0.0) + jnp.where(row3 == 1, dalog, 0.0)
                          + jnp.where(row3 == 2, dd, 0.0))
            ds = ds_prev
        ds_scr[...] = ds


    x, bm, cm, dtcs, dtrs, prs, pcs, st, ch = _ssd_specs(t, True)
    yblk = pl.BlockSpec((SSD_ROWS, 256), lambda g, c: (ch(c), g))
    nblk = pl.BlockSpec((SSD_ROWS, SSD_STATE), lambda g, c: (ch(c), g))
    return pl.pallas_call(
        body, name="ssd_bwd", grid=(SSD_GROUPS, t // SSD_ROWS),
        in_specs=[x, bm, cm, x, bm, cm, dtcs, dtrs, prs, pcs, pcs, st, yblk],
        out_specs=[yblk, nblk, nblk, dtrs, prs],
        out_shape=[jax.ShapeDtypeStruct((t, SSD_INNER), F32), jax.ShapeDtypeStruct((t, BC_WIDTH), F32),
                   jax.ShapeDtypeStruct((t, BC_WIDTH), F32), jax.ShapeDtypeStruct((SSD_GROUPS, 4, t), F32),
                   jax.ShapeDtypeStruct((SSD_GROUPS, 3, 256), F32)],
        scratch_shapes=[pltpu.VMEM((SSD_STATE, 256), F32)],
        compiler_params=_cp(("parallel", "arbitrary")))(xbc, xbc, xbc, pre, pre, pre, dtc, dtr, prow, bcol, acol,
                                                         states, dy)


GROUP_W = SSD_INNER // SSD_GROUPS


def _mix_specs(tb):
    row = pl.BlockSpec((tb, 2048), lambda i: (i, 0))
    zlo = pl.BlockSpec((tb, 1024), lambda i: (i, O_Z // 1024))
    zhi = pl.BlockSpec((tb, 1024), lambda i: (i, O_Z // 1024 + 1))
    vec = pl.BlockSpec((1, 2048), lambda i: (0, 0))
    return row, zlo, zhi, vec


def _mix_fwd(attn, y, proj, g_attn, g_ssd):
    t = attn.shape[0]
    tb = _rows(t, 256)

    def body(a_ref, y_ref, zlo_ref, zhi_ref, ga_ref, gs_ref, o_ref):
        av = a_ref[...]
        r = lax.rsqrt(jnp.mean(av * av, axis=-1, keepdims=True) + EPS)
        o_ref[:, :ATTN_WIDTH] = (av * r * ga_ref[...]).astype(BF16)
        for g in range(SSD_GROUPS):
            lo, hi = g * GROUP_W, (g + 1) * GROUP_W
            zref = zlo_ref if g < 4 else zhi_ref
            z = zref[:, lo % 1024:lo % 1024 + GROUP_W]
            yg = y_ref[:, lo:hi] * (z * _sigmoid(z))
            rg = lax.rsqrt(jnp.mean(yg * yg, axis=-1, keepdims=True) + EPS)
            o_ref[:, ATTN_WIDTH + lo:ATTN_WIDTH + hi] = (yg * rg * gs_ref[:, lo:hi]).astype(BF16)

    row, zlo, zhi, vec = _mix_specs(tb)
    return pl.pallas_call(
        body, name="mix_fwd", grid=(t // tb,), in_specs=[row, row, zlo, zhi, vec, vec],
        out_specs=pl.BlockSpec((tb, 4096), lambda i: (i, 0)), out_shape=jax.ShapeDtypeStruct((t, 4096), BF16),
        compiler_params=_cp(("parallel",)))(attn, y, proj, proj, g_attn, g_ssd)


def _mix_bwd(dmix, attn, y, proj, g_attn, g_ssd):
    t = attn.shape[0]
    tb = _rows(t, 256)

    def body(dm_ref, a_ref, y_ref, zlo_ref, zhi_ref, ga_ref, gs_ref, da_ref, dy_ref, dz_ref, dga_ref, dgs_ref):
        i = pl.program_id(0)
        av = a_ref[...]
        dn = dm_ref[:, :ATTN_WIDTH].astype(F32)
        r = lax.rsqrt(jnp.mean(av * av, axis=-1, keepdims=True) + EPS)
        u = dn * ga_ref[...]
        da_ref[...] = r * u - av * (r * r * r * jnp.mean(u * av, axis=-1, keepdims=True))
        dga = jnp.sum(dn * av * r, axis=0, keepdims=True)

        @pl.when(i == 0)
        def _():
            dga_ref[...] = dga

        @pl.when(i > 0)
        def _():
            dga_ref[...] += dga

        for g in range(SSD_GROUPS):
            lo, hi = g * GROUP_W, (g + 1) * GROUP_W
            zref = zlo_ref if g < 4 else zhi_ref
            z = zref[:, lo % 1024:lo % 1024 + GROUP_W]
            yv = y_ref[:, lo:hi]
            sg = _sigmoid(z)
            sz = z * sg
            yg = yv * sz
            rg = lax.rsqrt(jnp.mean(yg * yg, axis=-1, keepdims=True) + EPS)
            do = dm_ref[:, ATTN_WIDTH + lo:ATTN_WIDTH + hi].astype(F32)
            ug = do * gs_ref[:, lo:hi]
            dyg = rg * ug - yg * (rg * rg * rg * jnp.mean(ug * yg, axis=-1, keepdims=True))
            dy_ref[:, lo:hi] = dyg * sz
            dz_ref[:, lo:hi] = (dyg * yv * (sg * (1.0 + z * (1.0 - sg)))).astype(BF16)
            dgs = jnp.sum(do * yg * rg, axis=0, keepdims=True)

            @pl.when(i == 0)
            def _():
                dgs_ref[:, lo:hi] = dgs

            @pl.when(i > 0)
            def _():
                dgs_ref[:, lo:hi] += dgs

    row, zlo, zhi, vec = _mix_specs(tb)
    return pl.pallas_call(
        body, name="mix_bwd", grid=(t // tb,),
        in_specs=[pl.BlockSpec((tb, 4096), lambda i: (i, 0)), row, row, zlo, zhi, vec, vec],
        out_specs=[row, row, row, vec, vec],
        out_shape=[jax.ShapeDtypeStruct((t, 2048), F32), jax.ShapeDtypeStruct((t, 2048), F32),
                   jax.ShapeDtypeStruct((t, 2048), BF16), jax.ShapeDtypeStruct((1, 2048), F32),
                   jax.ShapeDtypeStruct((1, 2048), F32)],
        compiler_params=_cp(("arbitrary",)))(dmix, attn, y, proj, proj, g_attn, g_ssd)


def _adamw(w, g, m, v, name):
    r, c = w.shape
    tb = _rows(r, 256)
    c1 = 1.0 - ADAM_B1 ** ADAM_STEP
    c2 = 1.0 - ADAM_B2 ** ADAM_STEP

    def body(w_ref, g_ref, m_ref, v_ref, d_ref, m2_ref, v2_ref):
        gv = g_ref[...]
        m2 = ADAM_B1 * m_ref[...] + (1.0 - ADAM_B1) * gv
        v2 = ADAM_B2 * v_ref[...] + (1.0 - ADAM_B2) * (gv * gv)
        d_ref[...] = -ADAM_LR * ((m2 / c1) / (jnp.sqrt(v2 / c2) + ADAM_EPS) + ADAM_WD * w_ref[...])
        m2_ref[...] = m2
        v2_ref[...] = v2

    blk = pl.BlockSpec((tb, c), lambda i: (i, 0))
    shp = jax.ShapeDtypeStruct((r, c), F32)
    return pl.pallas_call(body, name=name, grid=(r // tb,), in_specs=[blk] * 4, out_specs=[blk] * 3,
                          out_shape=[shp] * 3, compiler_params=_cp(("parallel",)))(w, g, m, v)


def _adamw_halves(w, mine, theirs, m, v, pos, name, cols=False):
    r, c = w.shape
    h = r if cols else r // 2
    tb = _rows(h, 128)
    nh = h // tb
    c1 = 1.0 - ADAM_B1 ** ADAM_STEP
    c2 = 1.0 - ADAM_B2 ** ADAM_STEP

    def body(pos_ref, w_ref, a_ref, b_ref, m_ref, v_ref, g_ref, d_ref, m2_ref, v2_ref):
        which = pl.program_id(1) if cols else pl.program_id(0) // nh
        gv = jnp.where(which == pos_ref[0], a_ref[...], b_ref[...])
        m2 = ADAM_B1 * m_ref[...] + (1.0 - ADAM_B1) * gv
        v2 = ADAM_B2 * v_ref[...] + (1.0 - ADAM_B2) * (gv * gv)
        g_ref[...] = gv
        d_ref[...] = -ADAM_LR * ((m2 / c1) / (jnp.sqrt(v2 / c2) + ADAM_EPS) + ADAM_WD * w_ref[...])
        m2_ref[...] = m2
        v2_ref[...] = v2

    if cols:
        full = pl.BlockSpec((tb, c // 2), lambda i, j, pref: (i, j))
        mine_spec = theirs_spec = pl.BlockSpec((tb, c // 2), lambda i, j, pref: (i, 0))
        grid = (nh, 2)
    else:
        full = pl.BlockSpec((tb, c), lambda i, pref: (i, 0))
        mine_spec = pl.BlockSpec((tb, c), lambda i, pref: (jnp.where(i // nh == pref[0], i % nh,
                                                                     jnp.where(pref[0] == 0, nh - 1, 0)), 0))
        theirs_spec = pl.BlockSpec((tb, c), lambda i, pref: (jnp.where(i // nh != pref[0], i % nh,
                                                                       jnp.where(pref[0] == 0, 0, nh - 1)), 0))
        grid = (r // tb,)
    shp = jax.ShapeDtypeStruct((r, c), F32)
    grid_spec = pltpu.PrefetchScalarGridSpec(num_scalar_prefetch=1, grid=grid,
                                             in_specs=[full, mine_spec, theirs_spec, full, full],
                                             out_specs=[full] * 4)
    return pl.pallas_call(body, name=name, grid_spec=grid_spec, out_shape=[shp] * 4,
                          compiler_params=_cp(("parallel",) * len(grid)))(pos, w, mine, theirs, m, v)


def _sum_own_half(g4, recv, pos, name, cols=False):
    _, r, c = g4.shape
    h, c = (r, c // 2) if cols else (r // 2, c)
    tb = _rows(h, 128)
    nh = h // tb

    def slot(j, pref):
        return (pref[1] + 1 + j) % N_CHIPS

    if cols:
        own = lambda j, i, pref: (slot(j, pref), i, pref[0])
    else:
        own = lambda j, i, pref: (slot(j, pref), pref[0] * nh + i, 0)
    same = lambda j, i, pref: (slot(j, pref), i, 0)

    def body(pos_ref, a_ref, b_ref, o_ref):
        o_ref[...] = (a_ref[...] + b_ref[...]).astype(BF16)

    grid_spec = pltpu.PrefetchScalarGridSpec(
        num_scalar_prefetch=1, grid=(N_CHIPS - 1, nh),
        in_specs=[pl.BlockSpec((1, tb, c), own), pl.BlockSpec((1, tb, c), same)],
        out_specs=pl.BlockSpec((1, tb, c), same))
    return pl.pallas_call(body, name=name, grid_spec=grid_spec,
                          out_shape=jax.ShapeDtypeStruct((N_CHIPS, h, c), BF16),
                          compiler_params=_cp(("parallel", "parallel")))(pos, g4, recv)


def _sum_chips(g4, recv, parts, pos, name, cols=False):
    _, r, c = g4.shape
    h, c = (r, c // 2) if cols else (r // 2, c)
    tb = _rows(h, 128)
    nh = h // tb
    own = (lambda i, pref: (pref[1], i, pref[0])) if cols else (lambda i, pref: (pref[1], pref[0] * nh + i, 0))

    def body(pos_ref, a_ref, b_ref, p_ref, o_ref):
        own = a_ref[0] + b_ref[0]
        o_ref[...] = ((own + p_ref[0].astype(F32)) + p_ref[1].astype(F32)) + p_ref[2].astype(F32)

    grid_spec = pltpu.PrefetchScalarGridSpec(
        num_scalar_prefetch=1, grid=(nh,),
        in_specs=[pl.BlockSpec((1, tb, c), own),
                  pl.BlockSpec((1, tb, c), lambda i, pref: (pref[1], i, 0)),
                  pl.BlockSpec((3, tb, c), lambda i, pref: (0, i, 0))],
        out_specs=pl.BlockSpec((tb, c), lambda i, pref: (i, 0)))
    return pl.pallas_call(body, name=name, grid_spec=grid_spec, out_shape=jax.ShapeDtypeStruct((h, c), F32),
                          compiler_params=_cp(("parallel",)))(pos, g4, recv, parts)


def _me():
    return lax.axis_index("x"), lax.axis_index("y"), lax.axis_index("c")


def _flip(v, bit):
    return (1 - v) if bit else v


CHIP_FLIPS = [(1, 0), (0, 1), (1, 1)]


def _forward_halves(gathered):
    def body(g_ref, o_ref, token, send_sems, recv_sems):
        x, y, c = _me()
        h = g_ref.shape[2] // 2
        cps = []
        for k, (fx, fy) in enumerate(CHIP_FLIPS):
            peer_chip = 2 * _flip(x, fx) + _flip(y, fy)
            mine = o_ref.at[peer_chip, :, pl.ds(c * h, h)]
            cp = pltpu.make_async_remote_copy(src_ref=mine, dst_ref=mine, send_sem=send_sems.at[k],
                                              recv_sem=recv_sems.at[k], device_id=(x, y, 1 - c), device_id_type=MESH)
            cp.start()
            cps.append(cp)
        for k, (fx, fy) in enumerate(CHIP_FLIPS):
            peer_chip = 2 * _flip(x, fx) + _flip(y, fy)
            theirs = o_ref.at[peer_chip, :, pl.ds((1 - c) * h, h)]
            pltpu.make_async_remote_copy(src_ref=theirs, dst_ref=theirs, send_sem=send_sems.at[k],
                                         recv_sem=recv_sems.at[k], device_id=(x, y, 1 - c),
                                         device_id_type=MESH).wait_recv()
        for cp in cps:
            cp.wait_send()
        token[...] = jnp.zeros_like(token)

    return pl.pallas_call(
        body, name="gather_forward_w_in", in_specs=[HBM_SPEC],
        out_specs=[HBM_SPEC, pl.BlockSpec(memory_space=pltpu.VMEM)],
        out_shape=[jax.ShapeDtypeStruct(gathered.shape, gathered.dtype), TOKEN],
        scratch_shapes=[pltpu.SemaphoreType.DMA((3,)), pltpu.SemaphoreType.DMA((3,))],
        input_output_aliases={0: 0},
        compiler_params=pltpu.CompilerParams(has_side_effects=True))(gathered)


SEM_SPEC = pl.BlockSpec(memory_space=pltpu.SEMAPHORE)
ANY_SPEC = pl.BlockSpec(memory_space=pl.ANY)
DATAFLOW = pltpu.SideEffectType.DATAFLOW_SIDE_EFFECTING


def _in_hbm(a):
    return pltpu.with_memory_space_constraint(a, pltpu.HBM)


def _push_start(srcs, land_shapes, route, peers, name):
    n, npeer = len(srcs), len(peers)
    lands = [lax.empty(shp, s.dtype) for shp, s in zip(land_shapes, srcs)]

    def body(*refs):
        ins, lnd = refs[:n], refs[n:2 * n]
        send_sems, recv_sems = refs[2 * n], refs[2 * n + 1]
        token = refs[-1]
        x, y, c = _me()
        for t in range(n):
            for k, (fx, fy, fc) in enumerate(peers):
                src, dst = route(ins[t], lnd[t], k, x, y, c)
                pltpu.make_async_remote_copy(
                    src_ref=src, dst_ref=dst, send_sem=send_sems.at[npeer * t + k],
                    recv_sem=recv_sems.at[npeer * t + k],
                    device_id=(_flip(x, fx), _flip(y, fy), _flip(c, fc)), device_id_type=MESH).start()
        token[...] = jnp.zeros_like(token)

    bufs = [_in_hbm(a) for a in list(srcs) + lands]
    outs = pl.pallas_call(
        body, name=name,
        out_shape=(pltpu.SemaphoreType.DMA((npeer * n,)), pltpu.SemaphoreType.DMA((npeer * n,)),
                   *[pltpu.HBM(b.shape, b.dtype) for b in bufs], TOKEN),
        in_specs=[HBM_SPEC] * (2 * n),
        out_specs=(SEM_SPEC, SEM_SPEC, *[HBM_SPEC] * (2 * n), pl.BlockSpec(memory_space=pltpu.VMEM)),
        input_output_aliases={i: 2 + i for i in range(2 * n)},
        compiler_params=pltpu.CompilerParams(has_side_effects=DATAFLOW))(*bufs)
    return outs[0], outs[1], list(outs[2:2 + n]), list(outs[2 + n:2 + 2 * n]), outs[-1]


def _push_wait(send_sems, recv_sems, srcs, lands, after, route, peers, name):
    n, npeer = len(srcs), len(peers)

    def body(*refs):
        ins, lnd = refs[:n], refs[n:2 * n]
        ssem, rsem = refs[2 * n], refs[2 * n + 1]
        x, y, c = _me()
        for t in range(n):
            for k, (fx, fy, fc) in enumerate(peers):
                src, dst = route(ins[t], lnd[t], k, x, y, c)
                cp = pltpu.make_async_remote_copy(
                    src_ref=src, dst_ref=dst, send_sem=ssem.at[npeer * t + k], recv_sem=rsem.at[npeer * t + k],
                    device_id=(_flip(x, fx), _flip(y, fy), _flip(c, fc)), device_id_type=MESH)
                cp.wait_send()
                cp.wait_recv()

    bufs = list(srcs) + list(lands)
    outs = pl.pallas_call(
        body, name=name, out_shape=tuple(pltpu.HBM(b.shape, b.dtype) for b in bufs),
        in_specs=[HBM_SPEC] * (2 * n) + [SEM_SPEC, SEM_SPEC, ANY_SPEC], out_specs=tuple([HBM_SPEC] * (2 * n)),
        input_output_aliases={i: i for i in range(2 * n)},
        compiler_params=pltpu.CompilerParams(has_side_effects=DATAFLOW))(*bufs, send_sems, recv_sems, after)
    return list(outs[:n]), list(outs[n:])


OTHER_CHIPS = [(fx, fy, 0) for fx, fy in CHIP_FLIPS]
SIBLING = [(0, 0, 1)]


def _route_gather(src, land, k, x, y, c):
    return src, land.at[2 * x + y]


def _route_gather_half(src, land, k, x, y, c):
    h = src.shape[1] // 2
    return src.at[:, pl.ds(c * h, h)], land.at[2 * x + y, :, pl.ds(c * h, h)]


def _route_gather_half_wait(src, land, k, x, y, c):
    fx, fy = CHIP_FLIPS[k]
    h = src.shape[1] // 2
    return src.at[:, pl.ds(c * h, h)], land.at[2 * _flip(x, fx) + _flip(y, fy), :, pl.ds(c * h, h)]


def _route_gather_wait(src, land, k, x, y, c):
    fx, fy = CHIP_FLIPS[k]
    return src, land.at[2 * _flip(x, fx) + _flip(y, fy)]


def _route_scatter(src, land, k, x, y, c):
    fx, fy = CHIP_FLIPS[k]
    return src.at[2 * _flip(x, fx) + _flip(y, fy)], land.at[k]


def _route_exchange(src, land, k, x, y, c):
    h = land.shape[1]
    return src.at[:, pl.ds((1 - c) * h, h)], land


def _route_whole(src, land, k, x, y, c):
    return src, land


def _route_exchange_cols(src, land, k, x, y, c):
    h = land.shape[2]
    return src.at[:, :, pl.ds((1 - c) * h, h)], land


ALL_OTHERS = [((k >> 2) & 1, (k >> 1) & 1, k & 1) for k in range(1, 8)]


def _route_to_all(src, land, k, x, y, c):
    return src, land.at[4 * x + 2 * y + c]


def _route_to_all_wait(src, land, k, x, y, c):
    fx, fy, fc = ALL_OTHERS[k]
    return src, land.at[4 * _flip(x, fx) + 2 * _flip(y, fy) + _flip(c, fc)]


def _sum_devices(parts):
    def body(p_ref, o_ref):
        acc = p_ref[0]
        for d in range(1, 8):
            acc = acc + p_ref[d]
        o_ref[...] = acc

    vm = pl.BlockSpec(memory_space=pltpu.VMEM)
    return pl.pallas_call(body, name="allreduce_sum", in_specs=[vm], out_specs=vm,
                          out_shape=jax.ShapeDtypeStruct(parts.shape[1:], F32),
                          compiler_params=pltpu.CompilerParams(vmem_limit_bytes=VMEM_LIMIT))(parts)


def _grad_exchange_start(g4, tag, cols=False):
    land = (N_CHIPS, g4.shape[1], g4.shape[2] // 2) if cols else (N_CHIPS, g4.shape[1] // 2, g4.shape[2])
    route = _route_exchange_cols if cols else _route_exchange
    send_sems, recv_sems, srcs, lands, token = _push_start(
        [g4], [land], route, SIBLING, name="grad_exchange_start_" + tag)
    return (send_sems, recv_sems, srcs, lands, tag, cols), token


def _grad_scatter_start(state, pos, after):
    send_sems, recv_sems, srcs, lands, tag, cols = state
    route = _route_exchange_cols if cols else _route_exchange
    (g4,), (recv,) = _push_wait(send_sems, recv_sems, srcs, lands, after, route, SIBLING,
                                name="grad_exchange_wait_" + tag)
    return _grad_pair_scatter(g4, recv, pos, tag, cols)


def _grad_pair_scatter(g4, recv, pos, tag, cols=False):
    p16 = _sum_own_half(g4, recv, pos, name="grad_sum_pair_" + tag, cols=cols)
    send_sems, recv_sems, srcs, lands, token = _push_start(
        [p16], [(3,) + p16.shape[1:]], _route_scatter, OTHER_CHIPS, name="grad_scatter_start_" + tag)
    return (g4, recv, send_sems, recv_sems, srcs, lands, tag, cols), token


def _grad_sum_and_share(state, pos, after):
    g4, recv, send_sems, recv_sems, srcs, lands, tag, cols = state
    parts = _push_wait(send_sems, recv_sems, srcs, lands, after, _route_scatter, OTHER_CHIPS,
                       name="grad_scatter_wait_" + tag)[1][0]
    mine = _sum_chips(g4, recv, parts, pos, name="grad_sum_chips_" + tag, cols=cols)
    send_sems, recv_sems, srcs, lands, token = _push_start(
        [mine], [mine.shape], _route_whole, SIBLING, name="grad_share_start_" + tag)
    return (send_sems, recv_sems, srcs, lands, tag), token


def _grad_share_wait(state, after):
    send_sems, recv_sems, srcs, lands, tag = state
    (mine,), (theirs,) = _push_wait(send_sems, recv_sems, srcs, lands, after, _route_whole, SIBLING,
                                    name="grad_share_wait_" + tag)
    return mine, theirs


def _local_step(x, tgt, p, hooks):
    t = x.shape[0]
    tables = _rope_tables(t)
    sinks = p['sinks'].reshape(N_Q_HEADS)

    def told(name, value):
        return tuple(hooks.grad_ready(name, value))

    xn = _rmsnorm_fwd(x, p['norm_mix'], "norm_mix_fwd", deps=hooks.first_deps)
    w_in_t, w_in_dt, in_deps = hooks.weight_in(xn)
    proj = _matmul(xn, w_in_t, mode='nt', name="in_proj", n_limit=MAIN_WIDTH, deps=in_deps)
    dt_raw = _matmul(xn, w_in_dt, mode='nt', name="in_proj_dt")[:, :SSD_HEADS]
    ssd_conv_w, ffn_conv_w = hooks.conv_weights(proj)
    p = dict(p, ssd_conv_w=ssd_conv_w, ffn_conv_w=ffn_conv_w)
    attn = _attn_fwd(proj, sinks, tables)
    conv_b = p['ssd_conv_b']
    xbc, xbc_pre = _conv_silu_fwd(proj, p['ssd_conv_w'], conv_b, col0=O_XBC, width=CONV_CH, name="ssd_conv_fwd")
    sp = _ssd_params(dt_raw, p['dt_bias'].reshape(-1), p['a_log'].reshape(-1), p['ssd_d'].reshape(-1))
    y, states = _ssd_fwd(xbc, sp)
    mix = _mix_fwd(attn, y, proj, p['attn_out_norm'], p['ssd_norm'])
    w_out = hooks.weight('w_out', mix)
    h1 = _matmul(mix, w_out, mode='nn', name="out_proj", add=x)
    hn = _rmsnorm_fwd(h1, p['norm_ffn'], "norm_ffn_fwd")
    w_up = hooks.weight('w_up', hn)
    u0 = _matmul(hn, w_up, mode='nn', name="ffn_up", b_owner=True, tn=1408)
    a, u = _ffn_act_fwd(u0, p['ffn_conv_w'], p['ffn_conv_b'])
    w_down = hooks.weight('w_down', a)
    h2 = _matmul(a, w_down, mode='nn', name="ffn_down", add=h1, tk=2816)
    loss, dh2, dh2_16, g_norm_final = _final_loss(h2, p['norm_final'].reshape(1, D_MODEL), tgt)

    g = {}
    da = _matmul(dh2_16, w_down, mode='nt', name="ffn_down_dx", out_dtype=BF16, tn=1408)
    g['w_down'] = _matmul(a, dh2_16, mode='tn', name="ffn_down_dw", tm=1408)
    dep = told('w_down', g['w_down'])
    du0, dcw, dcb = _ffn_act_bwd(u0, u, p['ffn_conv_w'], da)
    g['ffn_conv_w'] = dcw.transpose(1, 0, 2).reshape(FFN_CONV, 2 * D_FF)
    g['ffn_conv_b'] = dcb.transpose(1, 0, 2).reshape(1, 2 * D_FF)
    g['w_up'] = _matmul(hn, du0, mode='tn', name="ffn_up_dw", deps=dep, b_halves=True, owner_major=True,
                        tn=1408)
    dep = told('w_up', g['w_up'])
    dhn = _matmul(du0, w_up, mode='nt', name="ffn_up_dx", out_dtype=BF16, deps=dep, a_halves=True,
                  b_owner=True, tk=2816)
    dh1, dh1_16, g['norm_ffn'] = _rmsnorm_bwd(h1, p['norm_ffn'], dhn, dh2, "norm_ffn_bwd")

    g['w_out'] = _matmul(mix, dh1_16, mode='tn', name="out_proj_dw")
    dep = told('w_out', g['w_out'])
    dmix = _matmul(dh1_16, w_out, mode='nt', name="out_proj_dx", out_dtype=BF16, deps=dep)
    dattn, dy, dz, g['attn_out_norm'], g['ssd_norm'] = _mix_bwd(dmix, attn, y, proj, p['attn_out_norm'],
                                                                p['ssd_norm'])
    dq, dk, dv, dsink = _attn_bwd(proj, sinks, tables, dattn)
    g['sinks'] = dsink[:, :, 0].reshape(1, N_Q_HEADS)
    dxs, dbm, dcm, ddt8, dpar = _ssd_bwd(xbc, xbc_pre, sp, states, dy)
    dpar = dpar[:, :, ::SSD_HEAD_DIM]
    g['dt_bias'] = dpar[:, 0, :].reshape(1, SSD_HEADS)
    g['a_log'] = dpar[:, 1, :].reshape(1, SSD_HEADS)
    g['ssd_d'] = dpar[:, 2, :].reshape(1, SSD_HEADS)
    dxbc, g['ssd_conv_w'], g['ssd_conv_b'] = _ssd_conv_bwd(proj, p['ssd_conv_w'], dxs, dbm, dcm, col0=O_XBC,
                                                           name="ssd_conv_bwd")
    dproj = jnp.concatenate([dq, dk, dv, dz, dxbc], axis=1)
    ddt = ddt8.transpose(2, 0, 1).reshape(t, SSD_HEADS)
    ddt_pad = jnp.pad(ddt, ((0, 0), (0, LANES - SSD_HEADS))).astype(BF16)
    g['w_in'] = (_matmul(dproj, xn, mode='tn', name="in_proj_dw", m_rows=IN_PROJ_WIDTH),
                 _matmul(ddt_pad, xn, mode='tn', name="in_proj_dt_dw"))
    dep = told('w_in', g['w_in'])
    dxn_dt = _matmul(ddt_pad, w_in_dt, mode='nn', name="in_proj_dt_dx", deps=dep)
    dxn = _matmul(dproj, w_in_t, mode='nn', name="in_proj_dx", out_dtype=BF16, add=dxn_dt, k_limit=MAIN_WIDTH,
                  tk=2304)
    dx, _, g['norm_mix'] = _rmsnorm_bwd(x, p['norm_mix'], dxn, dh1, "norm_mix_bwd")
    g['norm_final'] = g_norm_final
    return loss, dx, g


def _pack(arrs):
    flat = jnp.concatenate([a.reshape(-1) for a in arrs])
    n = flat.shape[0]
    rows = -(-n // LANES)
    rows = -(-rows // 8) * 8
    return jnp.pad(flat, (0, rows * LANES - n)).reshape(rows, LANES)


def _unpack(packed, shapes):
    flat = packed.reshape(-1)
    out, off = [], 0
    for s in shapes:
        n = 1
        for d in s:
            n *= d
        out.append(flat[off:off + n].reshape(s))
        off += n
    return out


class _StepHooks:
    def __init__(self, first_deps, weight_in, conv_weights, weight, grad_ready):
        self.first_deps = first_deps
        self.weight_in = weight_in
        self.conv_weights = conv_weights
        self.weight = weight
        self.grad_ready = grad_ready


def kernel(x, norm_mix, w_in, sinks, attn_out_norm, ssd_conv_w, ssd_conv_b, dt_bias, a_log, ssd_d, ssd_norm, w_out, norm_ffn, w_up, ffn_conv_w, ffn_conv_b, w_down, norm_final, loss_target, m_norm_mix, m_w_in, m_sinks, m_attn_out_norm, m_ssd_conv_w, m_ssd_conv_b, m_dt_bias, m_a_log, m_ssd_d, m_ssd_norm, m_w_out, m_norm_ffn, m_w_up, m_ffn_conv_w, m_ffn_conv_b, m_w_down, m_norm_final, v_norm_mix, v_w_in, v_sinks, v_attn_out_norm, v_ssd_conv_w, v_ssd_conv_b, v_dt_bias, v_a_log, v_ssd_d, v_ssd_norm, v_w_out, v_norm_ffn, v_w_up, v_ffn_conv_w, v_ffn_conv_b, v_w_down, v_norm_final):
    args = dict(locals())
    w = {n: args[n] for n in WEIGHTS}
    m = {n: args['m_' + n] for n in WEIGHTS}
    v = {n: args['v_' + n] for n in WEIGHTS}
    xi, yi, ci = _me()
    chip = 2 * xi + yi
    pos = jnp.stack([ci, chip]).astype(jnp.int32)

    conv_shard = _pack([ssd_conv_w[0], ffn_conv_w[0]])
    conv_gather = _push_start([conv_shard], [(N_CHIPS,) + conv_shard.shape], _route_gather, OTHER_CHIPS,
                              name="gather_start_conv")

    def conv_weights(after):
        send_sems, recv_sems, srcs, lands, _ = conv_gather
        (own,), (got,) = _push_wait(send_sems, recv_sems, srcs, lands, after, _route_gather_wait, OTHER_CHIPS,
                                    name="gather_wait_conv")
        whole = lax.dynamic_update_slice(got, own[None], (chip, 0, 0))
        per_chip = [_unpack(whole[j], [ssd_conv_w.shape[1:], ffn_conv_w.shape[1:]]) for j in range(N_CHIPS)]
        return (jnp.concatenate([pc[0] for pc in per_chip], axis=1),
                jnp.concatenate([pc[1] for pc in per_chip], axis=1))

    w_in_t, m_in_t, v_in_t = (jnp.transpose(a[0]) for a in (w_in, m_w_in, v_w_in))
    in_shard = (w_in_t + conv_gather[4][:1, :1]).astype(BF16)
    in_gather = _push_start([in_shard], [(N_CHIPS,) + in_shard.shape], _route_gather_half, OTHER_CHIPS,
                            name="gather_start_w_in")
    gathers = {}
    order = in_gather[4][:1, :1]
    for n, shard in (('w_out', w_out[0]), ('w_up', w_up[0]), ('w_down', w_down[0])):
        shard = (shard + order).astype(BF16)
        gathers[n] = _push_start([shard], [(N_CHIPS,) + shard.shape], _route_gather, OTHER_CHIPS,
                                 name="gather_start_" + n)
        order = gathers[n][4][:1, :1]

    def weight_in(after):
        send_sems, recv_sems, srcs, lands, _ = in_gather
        (own,), (got,) = _push_wait(send_sems, recv_sems, srcs, lands, after, _route_gather_half_wait, OTHER_CHIPS,
                                    name="gather_wait_w_in")
        got, _ = _forward_halves(got)
        full_in_t = lax.dynamic_update_slice(got, own[None], (chip, 0, 0)).reshape(IN_PROJ_WIDTH, D_MODEL)
        w_in_dt = jnp.pad(full_in_t[MAIN_WIDTH:], ((0, LANES - SSD_HEADS), (0, 0)))
        return full_in_t, w_in_dt, ()

    def weight(name, after):
        send_sems, recv_sems, srcs, lands, _ = gathers[name]
        (own,), (got,) = _push_wait(send_sems, recv_sems, srcs, lands, after, _route_gather_wait, OTHER_CHIPS,
                                    name="gather_wait_" + name)
        whole = lax.dynamic_update_slice(got, own[None], (chip, 0, 0))
        return whole if name == 'w_up' else whole.reshape(-1, D_MODEL)

    reductions, exchanging = {}, {}

    def flush(after):
        tokens = []
        for prev in list(exchanging):
            reductions[prev], token = _grad_scatter_start(exchanging.pop(prev), pos, after)
            tokens.append(token)
        return tokens

    def grad_ready(name, value):
        if name == 'w_in':
            main, dtp = value
            value = lax.dynamic_update_slice(main, dtp[:SSD_HEADS], (MAIN_WIDTH, 0))
        g4 = value if value.ndim == 3 else value.reshape(N_CHIPS, -1, value.shape[1])
        tokens = flush(g4)
        exchanging[name], token = _grad_exchange_start(g4, name, cols=(name == 'w_in'))
        return tokens + [token]

    small = {
        'norm_mix': norm_mix, 'sinks': sinks, 'attn_out_norm': attn_out_norm,
        'ssd_conv_b': ssd_conv_b, 'dt_bias': dt_bias, 'a_log': a_log, 'ssd_d': ssd_d, 'ssd_norm': ssd_norm,
        'norm_ffn': norm_ffn, 'ffn_conv_b': ffn_conv_b, 'norm_final': norm_final,
    }
    loss, dx, g = _local_step(x[0], loss_target[0], small,
                              _StepHooks((gathers['w_down'][4],), weight_in, conv_weights, weight, grad_ready))

    small_names = [n for n in WEIGHTS if n not in BIG]
    small_g = [loss[:, :1]] + [g[n] for n in small_names]
    small_shapes = [(1, 1)] + [tuple(a.shape) for a in small_g[1:]]
    packed = _pack(small_g)
    spread = _push_start([packed], [(8,) + packed.shape], _route_to_all, ALL_OTHERS, name="allreduce_start")
    started = flush(spread[4])[-1]
    grads, deltas, new_m, new_v = {}, {}, {}, {}
    after = started
    shares = {}
    for n in ('w_down', 'w_up', 'w_out'):
        shares[n], after = _grad_sum_and_share(reductions[n], pos, after)
    for n in ('w_down', 'w_up', 'w_out', 'w_in'):
        if n == 'w_in':
            shares[n], after = _grad_sum_and_share(reductions[n], pos, after)
        mine, theirs = _grad_share_wait(shares[n], after)
        if n == 'w_in':
            outs = _adamw_halves(w_in_t, mine, theirs, m_in_t, v_in_t, pos, name="adamw_" + n, cols=True)
            after = outs[1]
            outs = [jnp.transpose(o) for o in outs]
        else:
            outs = _adamw_halves(w[n][0], mine, theirs, m[n][0], v[n][0], pos, name="adamw_" + n)
            after = outs[1]
        grads[n], deltas[n], new_m[n], new_v[n] = [o[None] for o in outs]
    (own,), (landed,) = _push_wait(spread[0], spread[1], spread[2], spread[3], after, _route_to_all_wait, ALL_OTHERS,
                                   name="allreduce_wait")
    landed = lax.dynamic_update_slice(landed, own[None], (4 * xi + 2 * yi + ci, 0, 0))
    red = _unpack(_sum_devices(landed), small_shapes)
    loss_out = red[0].reshape(())
    gsm = dict(zip(small_names, red[1:]))
    gsm['ssd_conv_w'] = lax.dynamic_slice(gsm['ssd_conv_w'], (0, chip * ssd_conv_w.shape[2]),
                                          (SSD_CONV, ssd_conv_w.shape[2]))
    gsm['ffn_conv_w'] = lax.dynamic_slice(gsm['ffn_conv_w'], (0, chip * ffn_conv_w.shape[2]),
                                          (FFN_CONV, ffn_conv_w.shape[2]))

    shapes = [tuple(w[n].shape) for n in small_names]
    gp = _pack([gsm[n] for n in small_names])
    d, m2, v2 = _adamw(_pack([w[n] for n in small_names]), gp, _pack([m[n] for n in small_names]),
                       _pack([v[n] for n in small_names]), name="adamw_small")
    for n, gg, dd, mm, vv in zip(small_names, _unpack(gp, shapes), _unpack(d, shapes), _unpack(m2, shapes),
                                 _unpack(v2, shapes)):
        grads[n], deltas[n], new_m[n], new_v[n] = gg, dd, mm, vv

    return (loss_out, dx[None], *[grads[n] for n in WEIGHTS], *[deltas[n] for n in WEIGHTS],
            *[new_m[n] for n in WEIGHTS], *[new_v[n] for n in WEIGHTS])
```

```python
import functools

import jax
import jax.numpy as jnp
from jax import lax
from jax.experimental import pallas as pl
from jax.experimental.pallas import tpu as pltpu

F32 = jnp.float32
BF16 = jnp.bfloat16

D_MODEL = 2048
N_Q_HEADS = 32
N_KV_HEADS = 8
HEAD_DIM = 64
WINDOW = 128
ATTN_BLOCK = 128
ROT_DIM = 16
ROPE_THETA = 500000.0
SSD_HEADS = 32
SSD_HEAD_DIM = 64
SSD_INNER = 2048
SSD_GROUPS = 8
SSD_STATE = 128
SSD_CONV = 4
SSD_CHUNK = 128
ATTN_WIDTH = 2048
KV_WIDTH = 512
BC_WIDTH = 1024
CONV_CH = 4096
IN_PROJ_WIDTH = 9248
MAIN_WIDTH = 9216
D_FF = 5632
FFN_CONV = 3
EPS = 1e-6
O_Q, O_K, O_V, O_Z, O_XBC, O_DT = 0, 2048, 2560, 3072, 5120, 9216

ADAM_LR = 0.001
ADAM_B1 = 0.9
ADAM_B2 = 0.999
ADAM_EPS = 1e-08
ADAM_WD = 0.01
ADAM_STEP = 10

N_CHIPS = 4
NEG = -1e30
LANES = 128
VMEM_LIMIT = 48 * 1024 * 1024
MESH = pl.DeviceIdType.MESH
HBM_SPEC = pl.BlockSpec(memory_space=pltpu.HBM)
TOKEN = jax.ShapeDtypeStruct((8, LANES), F32)

WEIGHTS = ['norm_mix', 'w_in', 'sinks', 'attn_out_norm', 'ssd_conv_w', 'ssd_conv_b', 'dt_bias', 'a_log', 'ssd_d',
           'ssd_norm', 'w_out', 'norm_ffn', 'w_up', 'ffn_conv_w', 'ffn_conv_b', 'w_down', 'norm_final']
BIG = ['w_in', 'w_out', 'w_up', 'w_down']


def _cp(sem=None, vmem=VMEM_LIMIT):
    kw = {'vmem_limit_bytes': vmem}
    if sem is not None:
        kw['dimension_semantics'] = sem
    return pltpu.CompilerParams(**kw)


def _tile(n, pref):
    if n <= pref:
        return n
    t = (pref // LANES) * LANES
    while t > LANES and n % t:
        t -= LANES
    assert n % t == 0, (n, pref)
    return t


def _rows(n, pref):
    t = min(n, pref)
    while n % t:
        t -= 8
    if 4 * t < pref:
        t = pref
        while n % t:
            t += 8
    return t


def _iota(shape, dim):
    return lax.broadcasted_iota(jnp.int32, shape, dim)


def _dot(a, b, mode='nn'):
    dn = {'nn': (((1,), (0,)), ((), ())), 'nt': (((1,), (1,)), ((), ())), 'tn': (((0,), (0,)), ((), ()))}[mode]
    return lax.dot_general(a.astype(BF16), b.astype(BF16), dn, preferred_element_type=F32)


def _dot_exact(a, b):
    return lax.dot_general(a, b, (((1,), (0,)), ((), ())), precision=lax.Precision.HIGHEST,
                           preferred_element_type=F32)


def _sigmoid(x):
    return 1.0 / (1.0 + jnp.exp(-x))


def _softplus(x):
    return jnp.maximum(x, 0.0) + jnp.log(1.0 + jnp.exp(-jnp.abs(x)))


def _matmul(a, b, *, mode, name, out_dtype=F32, add=None, deps=(), tm=1024, tn=1024, tk=2048,
            a_halves=False, b_halves=False, b_owner=False, owner_major=False, n_limit=None, k_limit=None,
            m_rows=None):
    ash, bsh = (a.shape[1:] if a_halves else a.shape), (b.shape[1:] if (b_halves or b_owner) else b.shape)
    if mode == 'nn':
        (m, k), (k2, n) = ash, bsh
    elif mode == 'nt':
        (m, k), (n, k2) = ash, bsh
    else:
        (k, m), (k2, n) = ash, bsh
    if n_limit is not None:
        assert mode == 'nt' and n_limit <= n
        n = n_limit
    if k_limit is not None:
        assert mode == 'nn' and k_limit <= k2
        k2 = k_limit
    if a_halves:
        assert mode == 'nt'
        k = 2 * k
    if b_halves:
        assert mode == 'tn'
        n = 2 * n
    if b_owner:
        assert mode in ('nn', 'nt')
        if mode == 'nn':
            n = 4 * n
        else:
            k2 = 4 * k2
    assert k == k2, (a.shape, b.shape, mode)
    tm = _tile(m, tm)
    tn = _tile(n // 4 if (owner_major or (b_owner and mode == 'nn')) else (n // 2 if b_halves else n), tn)
    tk = _tile(k // 4 if (b_owner and mode == 'nt') else (k // 2 if a_halves else k), tk)
    nk = k // tk
    has_add = add is not None
    assert not (has_add and owner_major)

    def body(*refs):
        a_ref, b_ref = refs[:2]
        add_ref = refs[2] if has_add else None

        def finish(r, o_ref):
            if has_add:
                r = r + add_ref[...].astype(F32)
            o_ref[...] = r.astype(out_dtype)

        if nk == 1:
            finish(_dot(a_ref[...], b_ref[...], mode), refs[-1])
            return
        o_ref, acc = refs[-2:]
        kk = pl.program_id(2)

        @pl.when(kk == 0)
        def _():
            acc[...] = _dot(a_ref[...], b_ref[...], mode)

        @pl.when((kk > 0) & (kk < nk - 1))
        def _():
            acc[...] += _dot(a_ref[...], b_ref[...], mode)

        @pl.when(kk == nk - 1)
        def _():
            finish(acc[...] + _dot(a_ref[...], b_ref[...], mode), o_ref)

    if mode == 'tn':
        a_spec = pl.BlockSpec((tk, tm), lambda i, j, kk: (kk, i))
    elif a_halves:
        nkh = nk // 2
        a_spec = pl.BlockSpec((None, tm, tk), lambda i, j, kk: (kk // nkh, i, kk % nkh))
    else:
        a_spec = pl.BlockSpec((tm, tk), lambda i, j, kk: (i, kk))
    if mode == 'nt' and b_owner:
        nkq = nk // 4
        b_spec = pl.BlockSpec((None, tn, tk), lambda i, j, kk: (kk // nkq, j, kk % nkq))
    elif mode == 'nt':
        b_spec = pl.BlockSpec((tn, tk), lambda i, j, kk: (j, kk))
    elif b_owner:
        njq = (n // 4) // tn
        b_spec = pl.BlockSpec((None, tk, tn), lambda i, j, kk: (j // njq, kk, j % njq))
    elif b_halves:
        njh = (n // 2) // tn
        b_spec = pl.BlockSpec((None, tk, tn), lambda i, j, kk: (j // njh, kk, j % njh))
    else:
        b_spec = pl.BlockSpec((tk, tn), lambda i, j, kk: (kk, j))
    if owner_major:
        njo = (n // 4) // tn
        o_spec = pl.BlockSpec((None, tm, tn), lambda i, j, kk: (j // njo, i, j % njo))
        out_shape = jax.ShapeDtypeStruct((N_CHIPS, m, n // 4), out_dtype)
    else:
        o_spec = pl.BlockSpec((tm, tn), lambda i, j, kk: (i, j))
        out_shape = jax.ShapeDtypeStruct((m if m_rows is None else m_rows, n), out_dtype)
    dep_spec = pl.BlockSpec((8, LANES), lambda i, j, kk: (0, 0))
    in_specs = [a_spec, b_spec] + ([pl.BlockSpec((tm, tn), lambda i, j, kk: (i, j))] if has_add else [])
    in_specs += [dep_spec] * len(deps)
    args = (a, b) + ((add,) if has_add else ()) + tuple(deps)
    return pl.pallas_call(
        body, name=name, grid=(m // tm, n // tn, nk), in_specs=in_specs, out_specs=o_spec, out_shape=out_shape,
        scratch_shapes=[pltpu.VMEM((tm, tn), F32)] if nk > 1 else [],
        compiler_params=_cp(("parallel", "parallel", "arbitrary")))(*args)


def _rmsnorm_fwd(x, g, name, deps=()):
    t, d = x.shape
    tb = _rows(t, 256)

    def body(x_ref, g_ref, *rest):
        o_ref = rest[-1]
        xv = x_ref[...]
        r = lax.rsqrt(jnp.mean(xv * xv, axis=-1, keepdims=True) + EPS)
        o_ref[...] = (xv * r * g_ref[...]).astype(BF16)

    dep_spec = pl.BlockSpec((8, LANES), lambda i: (0, 0))
    return pl.pallas_call(
        body, name=name, grid=(t // tb,),
        in_specs=[pl.BlockSpec((tb, d), lambda i: (i, 0)), pl.BlockSpec((1, d), lambda i: (0, 0))]
        + [dep_spec] * len(deps),
        out_specs=pl.BlockSpec((tb, d), lambda i: (i, 0)), out_shape=jax.ShapeDtypeStruct((t, d), BF16),
        compiler_params=_cp(("parallel",)))(x, g, *deps)


def _rmsnorm_bwd(x, g, dy, res, name, deps=()):
    t, d = x.shape
    tb = _rows(t, 256)

    def body(x_ref, g_ref, dy_ref, res_ref, *rest):
        dx_ref, dx16_ref, dg_ref = rest[-3:]
        i = pl.program_id(0)
        xv = x_ref[...]
        dyv = dy_ref[...].astype(F32)
        r = lax.rsqrt(jnp.mean(xv * xv, axis=-1, keepdims=True) + EPS)
        u = dyv * g_ref[...]
        dx = r * u - xv * (r * r * r * jnp.mean(u * xv, axis=-1, keepdims=True)) + res_ref[...]
        dx_ref[...] = dx
        dx16_ref[...] = dx.astype(BF16)
        part = jnp.sum(dyv * xv * r, axis=0, keepdims=True)

        @pl.when(i == 0)
        def _():
            dg_ref[...] = part

        @pl.when(i > 0)
        def _():
            dg_ref[...] += part

    row = pl.BlockSpec((tb, d), lambda i: (i, 0))
    vec = pl.BlockSpec((1, d), lambda i: (0, 0))
    return pl.pallas_call(
        body, name=name, grid=(t // tb,),
        in_specs=[row, vec, row, row] + [pl.BlockSpec((8, LANES), lambda i: (0, 0))] * len(deps),
        out_specs=[row, row, vec],
        out_shape=[jax.ShapeDtypeStruct((t, d), F32), jax.ShapeDtypeStruct((t, d), BF16),
                   jax.ShapeDtypeStruct((1, d), F32)],
        compiler_params=_cp(("arbitrary",)))(x, g, dy, res, *deps)


def _final_loss(h, g, tgt):
    t, d = h.shape
    tb = _rows(t, 256)

    def body(h_ref, g_ref, t_ref, loss_ref, dh_ref, dh16_ref, dg_ref):
        i = pl.program_id(0)
        hv = h_ref[...]
        gv = g_ref[...]
        r = lax.rsqrt(jnp.mean(hv * hv, axis=-1, keepdims=True) + EPS)
        y = hv * r * gv
        diff = y - t_ref[...]
        lpart = jnp.sum(jnp.sum(diff * diff, axis=1, keepdims=True), axis=0, keepdims=True) * (0.5 / d)
        dy = diff * (1.0 / d)
        u = dy * gv
        dh = r * u - hv * (r * r * r * jnp.mean(u * hv, axis=-1, keepdims=True))
        dh_ref[...] = dh
        dh16_ref[...] = dh.astype(BF16)
        gpart = jnp.sum(dy * hv * r, axis=0, keepdims=True)
        lrow = jnp.broadcast_to(lpart, (1, LANES))

        @pl.when(i == 0)
        def _():
            loss_ref[...] = lrow
            dg_ref[...] = gpart

        @pl.when(i > 0)
        def _():
            loss_ref[...] += lrow
            dg_ref[...] += gpart

    row = pl.BlockSpec((tb, d), lambda i: (i, 0))
    vec = pl.BlockSpec((1, d), lambda i: (0, 0))
    return pl.pallas_call(
        body, name="final_loss", grid=(t // tb,), in_specs=[row, vec, row],
        out_specs=[pl.BlockSpec((1, LANES), lambda i: (0, 0)), row, row, vec],
        out_shape=[jax.ShapeDtypeStruct((1, LANES), F32), jax.ShapeDtypeStruct((t, d), F32),
                   jax.ShapeDtypeStruct((t, d), BF16), jax.ShapeDtypeStruct((1, d), F32)],
        compiler_params=_cp(("arbitrary",)))(h, g, tgt)


def _rope_tables(t):
    pos = jnp.arange(t, dtype=F32)
    inv = 1.0 / (ROPE_THETA ** (jnp.arange(0, ROT_DIM, 2, dtype=F32) / ROT_DIM))
    ang = pos[:, None] * inv[None, :]
    cos, sin = jnp.cos(ang), jnp.sin(ang)
    half = ROT_DIM // 2
    rest = HEAD_DIM - ROT_DIM
    c = jnp.concatenate([cos, cos, jnp.ones((t, rest), F32)], axis=1)
    s1 = jnp.concatenate([-sin, jnp.zeros((t, half + rest), F32)], axis=1)
    s2 = jnp.concatenate([jnp.zeros((t, half), F32), sin, jnp.zeros((t, rest), F32)], axis=1)
    return jnp.concatenate([jnp.tile(v, (1, LANES // HEAD_DIM)) for v in (c, s1, s2)], axis=1)


def _split_tables(tab):
    return tab[:, :LANES], tab[:, LANES:2 * LANES], tab[:, 2 * LANES:]


def _rope(x, c, s1, s2):
    half = ROT_DIM // 2
    return x * c + pltpu.roll(x, LANES - half, 1) * s1 + pltpu.roll(x, half, 1) * s2


def _rope_t(g, c, s1, s2):
    half = ROT_DIM // 2
    return g * c + pltpu.roll(g * s1, half, 1) + pltpu.roll(g * s2, LANES - half, 1)


def _band_masks(i, heads):
    n = heads * ATTN_BLOCK
    q = jnp.bitwise_and(_iota((n, ATTN_BLOCK), 0), ATTN_BLOCK - 1)
    j = _iota((n, ATTN_BLOCK), 1)
    upper = j > q
    return upper, upper & (j < jnp.where(i > 0, 0, ATTN_BLOCK))


def _fold_band(full, upper):
    return jnp.where(upper, full[:, :ATTN_BLOCK], full[:, ATTN_BLOCK:])


def _unfold_band(band, upper):
    return jnp.concatenate([jnp.where(upper, band, 0.0), jnp.where(upper, 0.0, band)], axis=1)


def _half_masks():
    lane = _iota((1, LANES), 1)
    return [(lane < HEAD_DIM).astype(F32), (lane >= HEAD_DIM).astype(F32)]


def _stack_heads(blocks, hm, j):
    pieces = []
    for r in range(4):
        qb, half = (4 * j + r) // 2, (4 * j + r) % 2
        piece = blocks[qb] * hm[half]
        if half != j:
            piece = pltpu.roll(piece, HEAD_DIM, 1)
        pieces.append(piece)
    return jnp.concatenate(pieces, axis=0)


def _unstack_heads(stacked, j):
    out = []
    for qb in (2 * j, 2 * j + 1):
        acc = None
        for half in range(2):
            r = 2 * qb + half - 4 * j
            piece = stacked[r * ATTN_BLOCK:(r + 1) * ATTN_BLOCK]
            if half != j:
                piece = pltpu.roll(piece, HEAD_DIM, 1)
            acc = piece if acc is None else acc + piece
        out.append((qb, acc))
    return out


def _sink_column(sink_ref, base):
    return jnp.concatenate([jnp.full((ATTN_BLOCK, 1), sink_ref[base + r], F32) for r in range(4)], axis=0)


def _attn_specs(nb_clamp):
    blk = ATTN_BLOCK
    kb, vb = O_K // LANES, O_V // LANES

    def cur(i):
        return jnp.minimum(i, nb_clamp)

    def prev(i):
        return jnp.maximum(jnp.minimum(i, nb_clamp + 1) - 1, 0)

    q = pl.BlockSpec((blk, 512), lambda p, i: (cur(i), p))
    kc = pl.BlockSpec((blk, LANES), lambda p, i: (cur(i), kb + p))
    kp = pl.BlockSpec((blk, LANES), lambda p, i: (prev(i), kb + p))
    vc = pl.BlockSpec((blk, LANES), lambda p, i: (cur(i), vb + p))
    vp = pl.BlockSpec((blk, LANES), lambda p, i: (prev(i), vb + p))
    tc = pl.BlockSpec((blk, 3 * LANES), lambda p, i: (cur(i), 0))
    tp = pl.BlockSpec((blk, 3 * LANES), lambda p, i: (prev(i), 0))
    return q, kc, kp, vc, vp, tc, tp


def _attn_fwd(proj, sinks, tables):
    t = proj.shape[0]
    nb = t // ATTN_BLOCK
    scale = HEAD_DIM ** -0.5

    def body(sink_ref, q_ref, kc_ref, kp_ref, vc_ref, vp_ref, tc_ref, tp_ref, o_ref):
        p = pl.program_id(0)
        i = pl.program_id(1)
        cc, s1c, s2c = _split_tables(tc_ref[...])
        kband = jnp.concatenate([_rope(kp_ref[...], *_split_tables(tp_ref[...])),
                                 _rope(kc_ref[...], cc, s1c, s2c)], axis=0).astype(BF16)
        vband = jnp.concatenate([vp_ref[...], vc_ref[...]], axis=0)
        hm = _half_masks()
        vsel = [(vband * hm[j]).astype(BF16) for j in range(2)]
        upper, dropped = _band_masks(i, 1)
        qr = [_rope(q_ref[:, qb * LANES:(qb + 1) * LANES], cc, s1c, s2c) for qb in range(4)]

        def scores(hh):
            qb, half, j = hh // 2, hh % 2, hh // 4
            qs = qr[qb] * hm[half]
            if half != j:
                qs = pltpu.roll(qs, HEAD_DIM, 1)
            return _dot(qs, kband, 'nt')

        ahead = scores(0)
        acc = None
        for hh in range(8):
            qb, half, j = hh // 2, hh % 2, hh // 4
            raw = ahead
            if hh + 1 < 8:
                ahead = scores(hh + 1)
            s = jnp.where(dropped, NEG, _fold_band(raw, upper) * scale)
            sink = sink_ref[p * 8 + hh]
            m = jnp.maximum(jnp.max(s, axis=1, keepdims=True), sink)
            pe = jnp.exp(s - m)
            den = jnp.sum(pe, axis=1, keepdims=True) + jnp.exp(sink - m)
            o = _dot(_unfold_band(pe / den, upper), vsel[j])
            if half != j:
                o = pltpu.roll(o, HEAD_DIM, 1)
            acc = o if half == 0 else acc + o
            if half == 1:
                o_ref[:, qb * LANES:(qb + 1) * LANES] = acc

    q, kc, kp, vc, vp, tc, tp = _attn_specs(nb - 1)
    smem = pl.BlockSpec(memory_space=pltpu.SMEM)
    return pl.pallas_call(
        body, name="attn_fwd", grid=(4, nb),
        in_specs=[smem, q, kc, kp, vc, vp, tc, tp],
        out_specs=pl.BlockSpec((ATTN_BLOCK, 512), lambda p, i: (i, p)),
        out_shape=jax.ShapeDtypeStruct((t, ATTN_WIDTH), F32),
        compiler_params=_cp(("parallel", "arbitrary")))(sinks, proj, proj, proj, proj, proj, tables, tables)


def _attn_bwd(proj, sinks, tables, dout):
    t = proj.shape[0]
    nb = t // ATTN_BLOCK
    scale = HEAD_DIM ** -0.5

    def body(sink_ref, q_ref, kc_ref, kp_ref, vc_ref, vp_ref, tc_ref, tp_ref,
             do_ref, dq_ref, dk_ref, dv_ref, ds_ref, carry_k, carry_v):
        p = pl.program_id(0)
        i = pl.program_id(1)
        ptab = _split_tables(tp_ref[...])

        @pl.when(i == 0)
        def _():
            carry_k[...] = jnp.zeros_like(carry_k)
            carry_v[...] = jnp.zeros_like(carry_v)
            ds_ref[...] = jnp.zeros_like(ds_ref)

        @pl.when(i < nb)
        def _():
            cc, s1c, s2c = _split_tables(tc_ref[...])
            kband = jnp.concatenate([_rope(kp_ref[...], *ptab), _rope(kc_ref[...], cc, s1c, s2c)], axis=0)
            vband = jnp.concatenate([vp_ref[...], vc_ref[...]], axis=0)
            hm = _half_masks()
            kband16 = kband.astype(BF16)
            vband16 = vband.astype(BF16)
            upper, dropped = _band_masks(i, 4)
            dkb = jnp.zeros((2 * ATTN_BLOCK, LANES), F32)
            dvb = jnp.zeros((2 * ATTN_BLOCK, LANES), F32)
            row8 = _iota((8, LANES), 0)
            dsink = jnp.zeros((8, LANES), F32)
            qr = [_rope(q_ref[:, qb * LANES:(qb + 1) * LANES], cc, s1c, s2c) for qb in range(4)]
            dob = [do_ref[:, qb * LANES:(qb + 1) * LANES] for qb in range(4)]
            for j in range(2):
                qst = _stack_heads(qr, hm, j).astype(BF16)
                dost = _stack_heads(dob, hm, j).astype(BF16)
                s = jnp.where(dropped, NEG, _fold_band(_dot(qst, kband16, 'nt'), upper) * scale)
                sink = _sink_column(sink_ref, p * 8 + 4 * j)
                m = jnp.maximum(jnp.max(s, axis=1, keepdims=True), sink)
                pe = jnp.exp(s - m)
                psink = jnp.exp(sink - m)
                den = jnp.sum(pe, axis=1, keepdims=True) + psink
                pr = pe / den
                dvb = dvb + _dot(_unfold_band(pr, upper).T, dost)
                dp = _fold_band(_dot(dost, vband16, 'nt'), upper)
                delta = jnp.sum(pr * dp, axis=1, keepdims=True)
                dsc = _unfold_band(pr * (dp - delta) * scale, upper)
                dsk = psink / den * delta
                for r in range(4):
                    part = jnp.sum(dsk[r * ATTN_BLOCK:(r + 1) * ATTN_BLOCK])
                    dsink = dsink + jnp.where(row8 == 4 * j + r, -part, 0.0)
                for qb, dqb in _unstack_heads(_dot(dsc, kband * hm[j]), j):
                    dq_ref[:, qb * LANES:(qb + 1) * LANES] = _rope_t(dqb, cc, s1c, s2c).astype(BF16)
                dkb = dkb + _dot(dsc.T, qst)
            ds_ref[0] += dsink
            dk_ref[...] = _rope_t(carry_k[...] + dkb[:ATTN_BLOCK], *ptab).astype(BF16)
            dv_ref[...] = (carry_v[...] + dvb[:ATTN_BLOCK]).astype(BF16)
            carry_k[...] = dkb[ATTN_BLOCK:]
            carry_v[...] = dvb[ATTN_BLOCK:]

        @pl.when(i == nb)
        def _():
            dk_ref[...] = _rope_t(carry_k[...], *ptab).astype(BF16)
            dv_ref[...] = carry_v[...].astype(BF16)

    q, kc, kp, vc, vp, tc, tp = _attn_specs(nb - 1)
    smem = pl.BlockSpec(memory_space=pltpu.SMEM)
    qblk = pl.BlockSpec((ATTN_BLOCK, 512), lambda p, i: (jnp.minimum(i, nb - 1), p))
    kvout = pl.BlockSpec((ATTN_BLOCK, LANES), lambda p, i: (jnp.maximum(i - 1, 0), p))
    return pl.pallas_call(
        body, name="attn_bwd", grid=(4, nb + 1),
        in_specs=[smem, q, kc, kp, vc, vp, tc, tp, qblk],
        out_specs=[qblk, kvout, kvout, pl.BlockSpec((1, 8, LANES), lambda p, i: (p, 0, 0))],
        out_shape=[jax.ShapeDtypeStruct((t, ATTN_WIDTH), BF16), jax.ShapeDtypeStruct((t, KV_WIDTH), BF16),
                   jax.ShapeDtypeStruct((t, KV_WIDTH), BF16), jax.ShapeDtypeStruct((4, 8, LANES), F32)],
        scratch_shapes=[pltpu.VMEM((ATTN_BLOCK, LANES), F32), pltpu.VMEM((ATTN_BLOCK, LANES), F32)],
        compiler_params=_cp(("parallel", "arbitrary")))(sinks, proj, proj, proj, proj, proj, tables, tables, dout)


def _shift_rows(x, prev8, j):
    n, c = x.shape
    r = pltpu.roll(x.reshape(n // 8, 8, c), j, 1)
    before = pltpu.roll(prev8, j, 0)[None]
    if n > 8:
        before = jnp.concatenate([before, r[:-1]], axis=0)
    return jnp.where(_iota((1, 8, 1), 1) < j, before, r).reshape(n, c)


def _shift_rows_up(x, next8, j):
    n, c = x.shape
    r = pltpu.roll(x.reshape(n // 8, 8, c), 8 - j, 1)
    after = pltpu.roll(next8, 8 - j, 0)[None]
    if n > 8:
        after = jnp.concatenate([r[1:], after], axis=0)
    return jnp.where(_iota((1, 8, 1), 1) >= 8 - j, after, r).reshape(n, c)


def _conv_apply(x, prev8, w, b, taps):
    u = b + x * w[taps - 1:taps]
    for j in range(1, taps):
        u = u + _shift_rows(x, prev8, j) * w[taps - 1 - j:taps - j]
    return u


def _conv_grads(du, du_next8, x, w, taps):
    dx = du * w[taps - 1:taps]
    rowk = _iota((taps, 1), 0)
    dw = jnp.where(rowk == taps - 1, jnp.sum(du * x, axis=0, keepdims=True), 0.0)
    for j in range(1, taps):
        ahead = _shift_rows_up(du, du_next8, j)
        dx = dx + ahead * w[taps - 1 - j:taps - j]
        dw = dw + jnp.where(rowk == taps - 1 - j, jnp.sum(ahead * x, axis=0, keepdims=True), 0.0)
    return dx, dw, jnp.sum(du, axis=0, keepdims=True)


def _conv_specs(tb, tc, col0, t):
    c0 = col0 // tc
    cur = pl.BlockSpec((tb, tc), lambda j, i: (i, c0 + j))
    prev = pl.BlockSpec((8, tc), lambda j, i: (jnp.maximum(i * (tb // 8) - 1, 0), c0 + j))
    nxt = pl.BlockSpec((8, tc), lambda j, i: (jnp.minimum((i + 1) * (tb // 8), t // 8 - 1), c0 + j))
    return cur, prev, nxt


def _conv_silu_fwd(x, w, b, *, col0, width, name):
    t = x.shape[0]
    taps = w.shape[0]
    tb, tc = _rows(t, 512), _tile(width, 1024)
    assert col0 % tc == 0

    def body(x_ref, xp_ref, w_ref, b_ref, o_ref, u_ref):
        i = pl.program_id(1)
        prev8 = jnp.where(i > 0, xp_ref[...], 0.0)
        u = _conv_apply(x_ref[...], prev8, w_ref[...], b_ref[...], taps)
        u_ref[...] = u
        o_ref[...] = u * _sigmoid(u)

    cur, prev, _ = _conv_specs(tb, tc, col0, t)
    par = pl.BlockSpec((taps, tc), lambda j, i: (0, j))
    bias = pl.BlockSpec((1, tc), lambda j, i: (0, j))
    out = pl.BlockSpec((tb, tc), lambda j, i: (i, j))
    shp = jax.ShapeDtypeStruct((t, width), F32)
    return pl.pallas_call(
        body, name=name, grid=(width // tc, t // tb), in_specs=[cur, prev, par, bias], out_specs=[out, out],
        out_shape=[shp, shp], compiler_params=_cp(("parallel", "parallel")))(x, x, w, b)


def _dsilu(u):
    sg = _sigmoid(u)
    return sg * (1.0 + u * (1.0 - sg))


def _ssd_conv_bwd(x, w, dxs, dbm, dcm, *, col0, name):
    t = x.shape[0]
    taps = w.shape[0]
    tb, tc = _rows(t, 512), BC_WIDTH
    nrow, ncol = t // tb, CONV_CH // tc
    c0 = col0 // tc

    def body(x_ref, w_ref, xs_ref, xsn_ref, bm_ref, bmn_ref, cm_ref, cmn_ref, dx_ref, dw_ref, db_ref):
        i = pl.program_id(0)
        j = pl.program_id(1)

        def run(du_ref, dun_ref):
            next8 = jnp.where(i < nrow - 1, dun_ref[...], 0.0)
            dx, dwv, dbv = _conv_grads(du_ref[...], next8, x_ref[...], w_ref[...], taps)
            dx_ref[...] = dx.astype(BF16)

            @pl.when(i == 0)
            def _():
                dw_ref[j] = dwv
                db_ref[j] = dbv

            @pl.when(i > 0)
            def _():
                dw_ref[j] += dwv
                db_ref[j] += dbv

        pl.when(j < 2)(lambda: run(xs_ref, xsn_ref))
        pl.when(j == 2)(lambda: run(bm_ref, bmn_ref))
        pl.when(j == 3)(lambda: run(cm_ref, cmn_ref))

    def nxt_row(i):
        return jnp.minimum((i + 1) * (tb // 8), t // 8 - 1)

    xs_col = lambda j: jnp.minimum(j, SSD_INNER // tc - 1)
    in_specs = [pl.BlockSpec((tb, tc), lambda i, j: (i, c0 + j)), pl.BlockSpec((taps, tc), lambda i, j: (0, j)),
                pl.BlockSpec((tb, tc), lambda i, j: (i, xs_col(j))),
                pl.BlockSpec((8, tc), lambda i, j: (nxt_row(i), xs_col(j))),
                pl.BlockSpec((tb, tc), lambda i, j: (i, 0)), pl.BlockSpec((8, tc), lambda i, j: (nxt_row(i), 0)),
                pl.BlockSpec((tb, tc), lambda i, j: (i, 0)), pl.BlockSpec((8, tc), lambda i, j: (nxt_row(i), 0))]
    dx, dw, db = pl.pallas_call(
        body, name=name, grid=(nrow, ncol), in_specs=in_specs,
        out_specs=[pl.BlockSpec((tb, tc), lambda i, j: (i, j)),
                   pl.BlockSpec((ncol, taps, tc), lambda i, j: (0, 0, 0)),
                   pl.BlockSpec((ncol, 1, tc), lambda i, j: (0, 0, 0))],
        out_shape=[jax.ShapeDtypeStruct((t, CONV_CH), BF16), jax.ShapeDtypeStruct((ncol, taps, tc), F32),
                   jax.ShapeDtypeStruct((ncol, 1, tc), F32)],
        compiler_params=_cp(("arbitrary", "arbitrary")))(x, w, dxs, dxs, dbm, dbm, dcm, dcm)
    return dx, dw.transpose(1, 0, 2).reshape(taps, CONV_CH), db.transpose(1, 0, 2).reshape(1, CONV_CH)


def _ffn_specs(tb, tc, t):
    nc = D_FF // tc

    def cur(half):
        return pl.BlockSpec((tb, tc), lambda j, i: (i, half * nc + j))

    def prev(half):
        return pl.BlockSpec((8, tc), lambda j, i: (jnp.maximum(i * (tb // 8) - 1, 0), half * nc + j))

    def nxt(half):
        return pl.BlockSpec((8, tc), lambda j, i: (jnp.minimum((i + 1) * (tb // 8), t // 8 - 1), half * nc + j))

    def par(rows, half):
        return pl.BlockSpec((rows, tc), lambda j, i: (0, half * nc + j))

    return cur, prev, nxt, par


def _ffn_act_fwd(u0, w, b):
    t = u0.shape[0]
    tb, tc = _rows(t, 512), _tile(D_FF, 1408)
    cur, prev, _, par = _ffn_specs(tb, tc, t)

    def body(g_ref, gp_ref, v_ref, vp_ref, wg_ref, wv_ref, bg_ref, bv_ref, o_ref, u_ref):
        i = pl.program_id(1)
        ug = _conv_apply(g_ref[...], jnp.where(i > 0, gp_ref[...], 0.0), wg_ref[...], bg_ref[...], FFN_CONV)
        uv = _conv_apply(v_ref[...], jnp.where(i > 0, vp_ref[...], 0.0), wv_ref[...], bv_ref[...], FFN_CONV)
        o_ref[...] = (ug * _sigmoid(ug) * uv).astype(BF16)
        u_ref[0] = ug
        u_ref[1] = uv

    return pl.pallas_call(
        body, name="ffn_act_fwd", grid=(D_FF // tc, t // tb),
        in_specs=[cur(0), prev(0), cur(1), prev(1), par(FFN_CONV, 0), par(FFN_CONV, 1), par(1, 0), par(1, 1)],
        out_specs=[pl.BlockSpec((tb, tc), lambda j, i: (i, j)), pl.BlockSpec((2, tb, tc), lambda j, i: (0, i, j))],
        out_shape=[jax.ShapeDtypeStruct((t, D_FF), BF16), jax.ShapeDtypeStruct((2, t, D_FF), F32)],
        compiler_params=_cp(("parallel", "parallel")))(u0, u0, u0, u0, w, w, b, b)


def _ffn_act_bwd(u0, u, w, da):
    t = u0.shape[0]
    tb, tc = _rows(t, 256), _tile(D_FF, 1408)
    nrow = t // tb
    taps = FFN_CONV
    cur, _, _, par = _ffn_specs(tb, tc, t)

    def dact(ug, uv, dav):
        sg = _sigmoid(ug)
        return dav * uv * (sg * (1.0 + ug * (1.0 - sg))), dav * ug * sg

    def body(g_ref, v_ref, u_ref, un_ref, wg_ref, wv_ref, da_ref, dan_ref, dx_ref, dw_ref, db_ref):
        i = pl.program_id(1)
        dug, duv = dact(u_ref[0], u_ref[1], da_ref[...].astype(F32))
        dan = jnp.where(i < nrow - 1, dan_ref[...].astype(F32)[:8], 0.0)
        dugn, duvn = dact(un_ref[0], un_ref[1], dan)
        dxg, dwg, dbg = _conv_grads(dug, dugn, g_ref[...], wg_ref[...], taps)
        dxv, dwv, dbv = _conv_grads(duv, duvn, v_ref[...], wv_ref[...], taps)
        dx_ref[0] = dxg.astype(BF16)
        dx_ref[1] = dxv.astype(BF16)

        @pl.when(i == 0)
        def _():
            dw_ref[0] = dwg
            dw_ref[1] = dwv
            db_ref[0] = dbg
            db_ref[1] = dbv

        @pl.when(i > 0)
        def _():
            dw_ref[0] += dwg
            dw_ref[1] += dwv
            db_ref[0] += dbg
            db_ref[1] += dbv

    both = pl.BlockSpec((2, tb, tc), lambda j, i: (0, i, j))
    both_nxt = pl.BlockSpec((2, 8, tc), lambda j, i: (0, jnp.minimum((i + 1) * (tb // 8), t // 8 - 1), j))
    da_cur = pl.BlockSpec((tb, tc), lambda j, i: (i, j))
    da_nxt = pl.BlockSpec((16, tc), lambda j, i: (jnp.minimum((i + 1) * (tb // 16), t // 16 - 1), j))
    return pl.pallas_call(
        body, name="ffn_act_bwd", grid=(D_FF // tc, nrow),
        in_specs=[cur(0), cur(1), both, both_nxt, par(taps, 0), par(taps, 1), da_cur, da_nxt],
        out_specs=[both, pl.BlockSpec((2, taps, tc), lambda j, i: (0, 0, j)),
                   pl.BlockSpec((2, 1, tc), lambda j, i: (0, 0, j))],
        out_shape=[jax.ShapeDtypeStruct((2, t, D_FF), BF16), jax.ShapeDtypeStruct((2, taps, D_FF), F32),
                   jax.ShapeDtypeStruct((2, 1, D_FF), F32)],
        compiler_params=_cp(("parallel", "arbitrary")))(u0, u0, u, u, w, w, da, da)


def _head_masks():
    lane = _iota((1, 4 * SSD_HEAD_DIM), 1)
    return [((lane >= r * SSD_HEAD_DIM) & (lane < (r + 1) * SSD_HEAD_DIM)).astype(F32) for r in range(4)]


def _segsum(v):
    first = _iota((1, LANES), 1) < SSD_HEAD_DIM
    halves = []
    for k in range(2):
        vh = v[:, k * LANES:(k + 1) * LANES]
        both = jnp.sum(vh, axis=1, keepdims=True)
        one = jnp.sum(jnp.where(first, vh, 0.0), axis=1, keepdims=True)
        halves.append(jnp.where(first, one, both - one))
    return jnp.concatenate(halves, axis=1)


def _ssd_common(raw_e, prow, rawr4, bcol, acol):
    n = SSD_CHUNK
    dt_e = _softplus(raw_e + prow[0:1, :])
    a_e = -jnp.exp(prow[1:2, :])
    d_e = prow[2:3, :]
    tril = (_iota((n, n), 0) >= _iota((n, n), 1)).astype(F32)
    acs_e = _dot_exact(tril, dt_e * a_e)
    last_e = acs_e[n - 1:n, :]
    dtr4 = _softplus(rawr4 + bcol)
    triu = (_iota((n, n), 0) <= _iota((n, n), 1)).astype(F32)
    acs_r4 = _dot_exact(dtr4 * (-jnp.exp(acol)), triu)
    return dt_e, a_e, d_e, acs_e, last_e, acs_r4


def _decay_matrix(acs_e, acs_r4, r):
    n = SSD_CHUNK
    col = acs_e[:, r * SSD_HEAD_DIM:r * SSD_HEAD_DIM + 1]
    seg = col - acs_r4[r:r + 1, :]
    causal = _iota((n, n), 0) >= _iota((n, n), 1)
    return jnp.exp(jnp.where(causal, seg, NEG))


SSD_STEP_CHUNKS = 4
SSD_ROWS = SSD_STEP_CHUNKS * SSD_CHUNK


def _ssd_specs(t, rev):
    nb = t // SSD_ROWS
    xb, bb, cb = 0, SSD_INNER // SSD_STATE, (SSD_INNER + BC_WIDTH) // SSD_STATE

    def ch(c):
        return (nb - 1 - c) if rev else c

    x = pl.BlockSpec((SSD_ROWS, 256), lambda g, c: (ch(c), xb + g))
    bm = pl.BlockSpec((SSD_ROWS, SSD_STATE), lambda g, c: (ch(c), bb + g))
    cm = pl.BlockSpec((SSD_ROWS, SSD_STATE), lambda g, c: (ch(c), cb + g))
    dtc = pl.BlockSpec((1, SSD_ROWS, 256), lambda g, c: (g, ch(c), 0))
    dtr = pl.BlockSpec((1, 4, SSD_ROWS), lambda g, c: (g, 0, ch(c)))
    prow = pl.BlockSpec((1, 3, 256), lambda g, c: (g, 0, 0))
    pcol = pl.BlockSpec((1, 4, 1), lambda g, c: (g, 0, 0))
    st = pl.BlockSpec((1, SSD_STEP_CHUNKS, SSD_STATE, 256), lambda g, c: (g, ch(c), 0, 0))
    return x, bm, cm, dtc, dtr, prow, pcol, st, ch


def _ssd_params(dt_raw, dt_bias, a_log, ssd_d):
    t = dt_raw.shape[0]
    by_group = dt_raw.reshape(t, SSD_GROUPS, 4)
    dtc = jnp.repeat(by_group, SSD_HEAD_DIM, axis=2).transpose(1, 0, 2)
    dtr = by_group.transpose(1, 2, 0)
    prow = jnp.repeat(jnp.stack([dt_bias.reshape(SSD_GROUPS, 4), a_log.reshape(SSD_GROUPS, 4),
                                 ssd_d.reshape(SSD_GROUPS, 4)], axis=1), SSD_HEAD_DIM, axis=2)
    bcol = dt_bias.reshape(SSD_GROUPS, 4, 1)
    acol = a_log.reshape(SSD_GROUPS, 4, 1)
    return dtc, dtr, prow, bcol, acol


def _ssd_fwd(xbc, params):
    t = xbc.shape[0]
    nc = t // SSD_CHUNK
    dtc, dtr, prow, bcol, acol = params

    def body(x_ref, b_ref, c_ref, dtc_ref, dtr_ref, prow_ref, bcol_ref, acol_ref, y_ref, st_ref, s_scr):
        c = pl.program_id(1)

        @pl.when(c == 0)
        def _():
            s_scr[...] = jnp.zeros_like(s_scr)

        masks = _head_masks()
        s = s_scr[...]
        for k in range(SSD_STEP_CHUNKS):
            rows = slice(k * SSD_CHUNK, (k + 1) * SSD_CHUNK)
            dt_e, a_e, d_e, acs_e, last_e, acs_r4 = _ssd_common(
                dtc_ref[0, rows], prow_ref[0], dtr_ref[0][:, rows], bcol_ref[0], acol_ref[0])
            xv = x_ref[rows]
            bm, cm = b_ref[rows], c_ref[rows]
            st_ref[0, k] = s
            xdt = xv * dt_e
            cb = _dot(cm, bm, 'nt')
            y = _dot(cm, s) * jnp.exp(acs_e) + xv * d_e
            for r in range(4):
                mr = cb * _decay_matrix(acs_e, acs_r4, r)
                y = y + _dot(mr, xdt * masks[r])
            y_ref[rows] = y
            w = xdt * jnp.exp(last_e - acs_e)
            s = s * jnp.exp(last_e) + _dot(bm.T, w)
        s_scr[...] = s

    x, bm, cm, dtcs, dtrs, prs, pcs, st, _ = _ssd_specs(t, False)
    return pl.pallas_call(
        body, name="ssd_fwd", grid=(SSD_GROUPS, t // SSD_ROWS), in_specs=[x, bm, cm, dtcs, dtrs, prs, pcs, pcs],
        out_specs=[pl.BlockSpec((SSD_ROWS, 256), lambda g, c: (c, g)), st],
        out_shape=[jax.ShapeDtypeStruct((t, SSD_INNER), F32),
                   jax.ShapeDtypeStruct((SSD_GROUPS, nc, SSD_STATE, 256), F32)],
        scratch_shapes=[pltpu.VMEM((SSD_STATE, 256), F32)],
        compiler_params=_cp(("parallel", "arbitrary")))(xbc, xbc, xbc, dtc, dtr, prow, bcol, acol)


def _ssd_bwd(xbc, pre, params, states, dy):
    t = xbc.shape[0]
    nc = t // SSD_CHUNK
    n = SSD_CHUNK
    dtc, dtr, prow, bcol, acol = params

    def body(x_ref, b_ref, c_ref, ux_ref, ub_ref, uc_ref, dtc_ref, dtr_ref, prow_ref, bcol_ref, acol_ref, st_ref,
             dy_ref, dx_ref, db_ref, dc_ref, ddt_ref, dp_ref, ds_scr):
        c = pl.program_id(1)

        @pl.when(c == 0)
        def _():
            ds_scr[...] = jnp.zeros_like(ds_scr)
            dp_ref[...] = jnp.zeros_like(dp_ref)

        masks = _head_masks()
        ds = ds_scr[...]
        for k in reversed(range(SSD_STEP_CHUNKS)):
            rows = slice(k * SSD_CHUNK, (k + 1) * SSD_CHUNK)
            raw_e = dtc_ref[0, rows]
            prw = prow_ref[0]
            dt_e, a_e, d_e, acs_e, last_e, acs_r4 = _ssd_common(raw_e, prw, dtr_ref[0][:, rows], bcol_ref[0], acol_ref[0])
            xv = x_ref[rows]
            bm, cm = b_ref[rows], c_ref[rows]
            s = st_ref[0, k]
            dyv = dy_ref[rows]
            e_e = jnp.exp(acs_e)
            dec_e = jnp.exp(last_e - acs_e)
            cd_e = jnp.exp(last_e)
            xdt = xv * dt_e
            w = xdt * dec_e
            b16, c16, s16, ds16 = bm.astype(BF16), cm.astype(BF16), s.astype(BF16), ds.astype(BF16)
            cb = _dot(c16, b16, 'nt')
            yoff_raw = _dot(c16, s16)
            dye = dyv * e_e
            dye16 = dye.astype(BF16)
            dcm = _dot(dye16, s16, 'nt')
            ds_prev = ds * cd_e + _dot(cm.T, dye16)
            dacs_e = _segsum(dyv * yoff_raw) * e_e
            dw = _dot(b16, ds16)
            dbm = _dot(w, ds16, 'nt')
            tdec = _segsum(dw * xdt) * dec_e
            dacs_e = dacs_e - tdec
            dlast_e = jnp.sum(tdec, axis=0, keepdims=True)
            dxdt = dw * dec_e
            dlast_e = dlast_e + _segsum(jnp.sum(ds * s, axis=0, keepdims=True)) * cd_e
            dcb = jnp.zeros((n, n), F32)
            for r in range(4):
                lm = _decay_matrix(acs_e, acs_r4, r)
                mr = cb * lm
                dyr16 = (dyv * masks[r]).astype(BF16)
                dm = _dot(dyr16, xdt * masks[r], 'nt')
                dcb = dcb + dm * lm
                dseg = dm * mr
                dcol = jnp.sum(dseg, axis=1, keepdims=True) - jnp.sum(dseg.T, axis=1, keepdims=True)
                dacs_e = dacs_e + dcol * masks[r]
                dxdt = dxdt + _dot(mr.T, dyr16)
            dcm = dcm + _dot(dcb, b16)
            dbm = dbm + _dot(dcb.T, c16)
            dacs_e = dacs_e + jnp.where(_iota((n, 1), 0) == n - 1, dlast_e, 0.0)
            triu = (_iota((n, n), 0) <= _iota((n, n), 1)).astype(F32)
            ddta_e = _dot_exact(triu, dacs_e)
            ddt_e = ddta_e * a_e + _segsum(dxdt * xv)
            dx_ref[rows] = (dxdt * dt_e + dyv * d_e) * _dsilu(ux_ref[rows])
            db_ref[rows] = dbm * _dsilu(ub_ref[rows])
            dc_ref[rows] = dcm * _dsilu(uc_ref[rows])
            draw_e = ddt_e * _sigmoid(raw_e + prw[0:1, :])
            draw_t = draw_e.T
            ddt_ref[0, :, rows] = jnp.concatenate([draw_t[r * SSD_HEAD_DIM:r * SSD_HEAD_DIM + 1] for r in range(4)], axis=0)
            dbias = jnp.sum(draw_e, axis=0, keepdims=True)
            dalog = jnp.sum(ddta_e * dt_e, axis=0, keepdims=True) * a_e
            dd = _segsum(jnp.sum(dyv * xv, axis=0, keepdims=True))
            row3 = _iota((3, 1), 0)
            dp_ref[0] += (jnp.where(row3 == 0, dbias, 0.0) + jnp.where(row3 == 1, dalog, 0.0)
                          + jnp.where(row3 == 2, dd, 0.0))
            ds = ds_prev
        ds_scr[...] = ds


    x, bm, cm, dtcs, dtrs, prs, pcs, st, ch = _ssd_specs(t, True)
    yblk = pl.BlockSpec((SSD_ROWS, 256), lambda g, c: (ch(c), g))
    nblk = pl.BlockSpec((SSD_ROWS, SSD_STATE), lambda g, c: (ch(c), g))
    return pl.pallas_call(
        body, name="ssd_bwd", grid=(SSD_GROUPS, t // SSD_ROWS),
        in_specs=[x, bm, cm, x, bm, cm, dtcs, dtrs, prs, pcs, pcs, st, yblk],
        out_specs=[yblk, nblk, nblk, dtrs, prs],
        out_shape=[jax.ShapeDtypeStruct((t, SSD_INNER), F32), jax.ShapeDtypeStruct((t, BC_WIDTH), F32),
                   jax.ShapeDtypeStruct((t, BC_WIDTH), F32), jax.ShapeDtypeStruct((SSD_GROUPS, 4, t), F32),
                   jax.ShapeDtypeStruct((SSD_GROUPS, 3, 256), F32)],
        scratch_shapes=[pltpu.VMEM((SSD_STATE, 256), F32)],
        compiler_params=_cp(("parallel", "arbitrary")))(xbc, xbc, xbc, pre, pre, pre, dtc, dtr, prow, bcol, acol,
                                                         states, dy)


GROUP_W = SSD_INNER // SSD_GROUPS


def _mix_specs(tb):
    row = pl.BlockSpec((tb, 2048), lambda i: (i, 0))
    zlo = pl.BlockSpec((tb, 1024), lambda i: (i, O_Z // 1024))
    zhi = pl.BlockSpec((tb, 1024), lambda i: (i, O_Z // 1024 + 1))
    vec = pl.BlockSpec((1, 2048), lambda i: (0, 0))
    return row, zlo, zhi, vec


def _mix_fwd(attn, y, proj, g_attn, g_ssd):
    t = attn.shape[0]
    tb = _rows(t, 256)

    def body(a_ref, y_ref, zlo_ref, zhi_ref, ga_ref, gs_ref, o_ref):
        av = a_ref[...]
        r = lax.rsqrt(jnp.mean(av * av, axis=-1, keepdims=True) + EPS)
        o_ref[:, :ATTN_WIDTH] = (av * r * ga_ref[...]).astype(BF16)
        for g in range(SSD_GROUPS):
            lo, hi = g * GROUP_W, (g + 1) * GROUP_W
            zref = zlo_ref if g < 4 else zhi_ref
            z = zref[:, lo % 1024:lo % 1024 + GROUP_W]
            yg = y_ref[:, lo:hi] * (z * _sigmoid(z))
            rg = lax.rsqrt(jnp.mean(yg * yg, axis=-1, keepdims=True) + EPS)
            o_ref[:, ATTN_WIDTH + lo:ATTN_WIDTH + hi] = (yg * rg * gs_ref[:, lo:hi]).astype(BF16)

    row, zlo, zhi, vec = _mix_specs(tb)
    return pl.pallas_call(
        body, name="mix_fwd", grid=(t // tb,), in_specs=[row, row, zlo, zhi, vec, vec],
        out_specs=pl.BlockSpec((tb, 4096), lambda i: (i, 0)), out_shape=jax.ShapeDtypeStruct((t, 4096), BF16),
        compiler_params=_cp(("parallel",)))(attn, y, proj, proj, g_attn, g_ssd)


def _mix_bwd(dmix, attn, y, proj, g_attn, g_ssd):
    t = attn.shape[0]
    tb = _rows(t, 256)

    def body(dm_ref, a_ref, y_ref, zlo_ref, zhi_ref, ga_ref, gs_ref, da_ref, dy_ref, dz_ref, dga_ref, dgs_ref):
        i = pl.program_id(0)
        av = a_ref[...]
        dn = dm_ref[:, :ATTN_WIDTH].astype(F32)
        r = lax.rsqrt(jnp.mean(av * av, axis=-1, keepdims=True) + EPS)
        u = dn * ga_ref[...]
        da_ref[...] = r * u - av * (r * r * r * jnp.mean(u * av, axis=-1, keepdims=True))
        dga = jnp.sum(dn * av * r, axis=0, keepdims=True)

        @pl.when(i == 0)
        def _():
            dga_ref[...] = dga

        @pl.when(i > 0)
        def _():
            dga_ref[...] += dga

        for g in range(SSD_GROUPS):
            lo, hi = g * GROUP_W, (g + 1) * GROUP_W
            zref = zlo_ref if g < 4 else zhi_ref
            z = zref[:, lo % 1024:lo % 1024 + GROUP_W]
            yv = y_ref[:, lo:hi]
            sg = _sigmoid(z)
            sz = z * sg
            yg = yv * sz
            rg = lax.rsqrt(jnp.mean(yg * yg, axis=-1, keepdims=True) + EPS)
            do = dm_ref[:, ATTN_WIDTH + lo:ATTN_WIDTH + hi].astype(F32)
            ug = do * gs_ref[:, lo:hi]
            dyg = rg * ug - yg * (rg * rg * rg * jnp.mean(ug * yg, axis=-1, keepdims=True))
            dy_ref[:, lo:hi] = dyg * sz
            dz_ref[:, lo:hi] = (dyg * yv * (sg * (1.0 + z * (1.0 - sg)))).astype(BF16)
            dgs = jnp.sum(do * yg * rg, axis=0, keepdims=True)

            @pl.when(i == 0)
            def _():
                dgs_ref[:, lo:hi] = dgs

            @pl.when(i > 0)
            def _():
                dgs_ref[:, lo:hi] += dgs

    row, zlo, zhi, vec = _mix_specs(tb)
    return pl.pallas_call(
        body, name="mix_bwd", grid=(t // tb,),
        in_specs=[pl.BlockSpec((tb, 4096), lambda i: (i, 0)), row, row, zlo, zhi, vec, vec],
        out_specs=[row, row, row, vec, vec],
        out_shape=[jax.ShapeDtypeStruct((t, 2048), F32), jax.ShapeDtypeStruct((t, 2048), F32),
                   jax.ShapeDtypeStruct((t, 2048), BF16), jax.ShapeDtypeStruct((1, 2048), F32),
                   jax.ShapeDtypeStruct((1, 2048), F32)],
        compiler_params=_cp(("arbitrary",)))(dmix, attn, y, proj, proj, g_attn, g_ssd)


def _adamw(w, g, m, v, name):
    r, c = w.shape
    tb = _rows(r, 256)
    c1 = 1.0 - ADAM_B1 ** ADAM_STEP
    c2 = 1.0 - ADAM_B2 ** ADAM_STEP

    def body(w_ref, g_ref, m_ref, v_ref, d_ref, m2_ref, v2_ref):
        gv = g_ref[...]
        m2 = ADAM_B1 * m_ref[...] + (1.0 - ADAM_B1) * gv
        v2 = ADAM_B2 * v_ref[...] + (1.0 - ADAM_B2) * (gv * gv)
        d_ref[...] = -ADAM_LR * ((m2 / c1) / (jnp.sqrt(v2 / c2) + ADAM_EPS) + ADAM_WD * w_ref[...])
        m2_ref[...] = m2
        v2_ref[...] = v2

    blk = pl.BlockSpec((tb, c), lambda i: (i, 0))
    shp = jax.ShapeDtypeStruct((r, c), F32)
    return pl.pallas_call(body, name=name, grid=(r // tb,), in_specs=[blk] * 4, out_specs=[blk] * 3,
                          out_shape=[shp] * 3, compiler_params=_cp(("parallel",)))(w, g, m, v)


def _adamw_halves(w, mine, theirs, m, v, pos, name, cols=False):
    r, c = w.shape
    h = r if cols else r // 2
    tb = _rows(h, 128)
    nh = h // tb
    c1 = 1.0 - ADAM_B1 ** ADAM_STEP
    c2 = 1.0 - ADAM_B2 ** ADAM_STEP

    def body(pos_ref, w_ref, a_ref, b_ref, m_ref, v_ref, g_ref, d_ref, m2_ref, v2_ref):
        which = pl.program_id(1) if cols else pl.program_id(0) // nh
        gv = jnp.where(which == pos_ref[0], a_ref[...], b_ref[...])
        m2 = ADAM_B1 * m_ref[...] + (1.0 - ADAM_B1) * gv
        v2 = ADAM_B2 * v_ref[...] + (1.0 - ADAM_B2) * (gv * gv)
        g_ref[...] = gv
        d_ref[...] = -ADAM_LR * ((m2 / c1) / (jnp.sqrt(v2 / c2) + ADAM_EPS) + ADAM_WD * w_ref[...])
        m2_ref[...] = m2
        v2_ref[...] = v2

    if cols:
        full = pl.BlockSpec((tb, c // 2), lambda i, j, pref: (i, j))
        mine_spec = theirs_spec = pl.BlockSpec((tb, c // 2), lambda i, j, pref: (i, 0))
        grid = (nh, 2)
    else:
        full = pl.BlockSpec((tb, c), lambda i, pref: (i, 0))
        mine_spec = pl.BlockSpec((tb, c), lambda i, pref: (jnp.where(i // nh == pref[0], i % nh,
                                                                     jnp.where(pref[0] == 0, nh - 1, 0)), 0))
        theirs_spec = pl.BlockSpec((tb, c), lambda i, pref: (jnp.where(i // nh != pref[0], i % nh,
                                                                       jnp.where(pref[0] == 0, 0, nh - 1)), 0))
        grid = (r // tb,)
    shp = jax.ShapeDtypeStruct((r, c), F32)
    grid_spec = pltpu.PrefetchScalarGridSpec(num_scalar_prefetch=1, grid=grid,
                                             in_specs=[full, mine_spec, theirs_spec, full, full],
                                             out_specs=[full] * 4)
    return pl.pallas_call(body, name=name, grid_spec=grid_spec, out_shape=[shp] * 4,
                          compiler_params=_cp(("parallel",) * len(grid)))(pos, w, mine, theirs, m, v)


def _sum_own_half(g4, recv, pos, name, cols=False):
    _, r, c = g4.shape
    h, c = (r, c // 2) if cols else (r // 2, c)
    tb = _rows(h, 128)
    nh = h // tb

    def slot(j, pref):
        return (pref[1] + 1 + j) % N_CHIPS

    if cols:
        own = lambda j, i, pref: (slot(j, pref), i, pref[0])
    else:
        own = lambda j, i, pref: (slot(j, pref), pref[0] * nh + i, 0)
    same = lambda j, i, pref: (slot(j, pref), i, 0)

    def body(pos_ref, a_ref, b_ref, o_ref):
        o_ref[...] = (a_ref[...] + b_ref[...]).astype(BF16)

    grid_spec = pltpu.PrefetchScalarGridSpec(
        num_scalar_prefetch=1, grid=(N_CHIPS - 1, nh),
        in_specs=[pl.BlockSpec((1, tb, c), own), pl.BlockSpec((1, tb, c), same)],
        out_specs=pl.BlockSpec((1, tb, c), same))
    return pl.pallas_call(body, name=name, grid_spec=grid_spec,
                          out_shape=jax.ShapeDtypeStruct((N_CHIPS, h, c), BF16),
                          compiler_params=_cp(("parallel", "parallel")))(pos, g4, recv)


def _sum_chips(g4, recv, parts, pos, name, cols=False):
    _, r, c = g4.shape
    h, c = (r, c // 2) if cols else (r // 2, c)
    tb = _rows(h, 128)
    nh = h // tb
    own = (lambda i, pref: (pref[1], i, pref[0])) if cols else (lambda i, pref: (pref[1], pref[0] * nh + i, 0))

    def body(pos_ref, a_ref, b_ref, p_ref, o_ref):
        own = a_ref[0] + b_ref[0]
        o_ref[...] = ((own + p_ref[0].astype(F32)) + p_ref[1].astype(F32)) + p_ref[2].astype(F32)

    grid_spec = pltpu.PrefetchScalarGridSpec(
        num_scalar_prefetch=1, grid=(nh,),
        in_specs=[pl.BlockSpec((1, tb, c), own),
                  pl.BlockSpec((1, tb, c), lambda i, pref: (pref[1], i, 0)),
                  pl.BlockSpec((3, tb, c), lambda i, pref: (0, i, 0))],
        out_specs=pl.BlockSpec((tb, c), lambda i, pref: (i, 0)))
    return pl.pallas_call(body, name=name, grid_spec=grid_spec, out_shape=jax.ShapeDtypeStruct((h, c), F32),
                          compiler_params=_cp(("parallel",)))(pos, g4, recv, parts)


def _me():
    return lax.axis_index("x"), lax.axis_index("y"), lax.axis_index("c")


def _flip(v, bit):
    return (1 - v) if bit else v


CHIP_FLIPS = [(1, 0), (0, 1), (1, 1)]


def _forward_halves(gathered):
    def body(g_ref, o_ref, token, send_sems, recv_sems):
        x, y, c = _me()
        h = g_ref.shape[2] // 2
        cps = []
        for k, (fx, fy) in enumerate(CHIP_FLIPS):
            peer_chip = 2 * _flip(x, fx) + _flip(y, fy)
            mine = o_ref.at[peer_chip, :, pl.ds(c * h, h)]
            cp = pltpu.make_async_remote_copy(src_ref=mine, dst_ref=mine, send_sem=send_sems.at[k],
                                              recv_sem=recv_sems.at[k], device_id=(x, y, 1 - c), device_id_type=MESH)
            cp.start()
            cps.append(cp)
        for k, (fx, fy) in enumerate(CHIP_FLIPS):
            peer_chip = 2 * _flip(x, fx) + _flip(y, fy)
            theirs = o_ref.at[peer_chip, :, pl.ds((1 - c) * h, h)]
            pltpu.make_async_remote_copy(src_ref=theirs, dst_ref=theirs, send_sem=send_sems.at[k],
                                         recv_sem=recv_sems.at[k], device_id=(x, y, 1 - c),
                                         device_id_type=MESH).wait_recv()
        for cp in cps:
            cp.wait_send()
        token[...] = jnp.zeros_like(token)

    return pl.pallas_call(
        body, name="gather_forward_w_in", in_specs=[HBM_SPEC],
        out_specs=[HBM_SPEC, pl.BlockSpec(memory_space=pltpu.VMEM)],
        out_shape=[jax.ShapeDtypeStruct(gathered.shape, gathered.dtype), TOKEN],
        scratch_shapes=[pltpu.SemaphoreType.DMA((3,)), pltpu.SemaphoreType.DMA((3,))],
        input_output_aliases={0: 0},
        compiler_params=pltpu.CompilerParams(has_side_effects=True))(gathered)


SEM_SPEC = pl.BlockSpec(memory_space=pltpu.SEMAPHORE)
ANY_SPEC = pl.BlockSpec(memory_space=pl.ANY)
DATAFLOW = pltpu.SideEffectType.DATAFLOW_SIDE_EFFECTING


def _in_hbm(a):
    return pltpu.with_memory_space_constraint(a, pltpu.HBM)


def _push_start(srcs, land_shapes, route, peers, name):
    n, npeer = len(srcs), len(peers)
    lands = [lax.empty(shp, s.dtype) for shp, s in zip(land_shapes, srcs)]

    def body(*refs):
        ins, lnd = refs[:n], refs[n:2 * n]
        send_sems, recv_sems = refs[2 * n], refs[2 * n + 1]
        token = refs[-1]
        x, y, c = _me()
        for t in range(n):
            for k, (fx, fy, fc) in enumerate(peers):
                src, dst = route(ins[t], lnd[t], k, x, y, c)
                pltpu.make_async_remote_copy(
                    src_ref=src, dst_ref=dst, send_sem=send_sems.at[npeer * t + k],
                    recv_sem=recv_sems.at[npeer * t + k],
                    device_id=(_flip(x, fx), _flip(y, fy), _flip(c, fc)), device_id_type=MESH).start()
        token[...] = jnp.zeros_like(token)

    bufs = [_in_hbm(a) for a in list(srcs) + lands]
    outs = pl.pallas_call(
        body, name=name,
        out_shape=(pltpu.SemaphoreType.DMA((npeer * n,)), pltpu.SemaphoreType.DMA((npeer * n,)),
                   *[pltpu.HBM(b.shape, b.dtype) for b in bufs], TOKEN),
        in_specs=[HBM_SPEC] * (2 * n),
        out_specs=(SEM_SPEC, SEM_SPEC, *[HBM_SPEC] * (2 * n), pl.BlockSpec(memory_space=pltpu.VMEM)),
        input_output_aliases={i: 2 + i for i in range(2 * n)},
        compiler_params=pltpu.CompilerParams(has_side_effects=DATAFLOW))(*bufs)
    return outs[0], outs[1], list(outs[2:2 + n]), list(outs[2 + n:2 + 2 * n]), outs[-1]


def _push_wait(send_sems, recv_sems, srcs, lands, after, route, peers, name):
    n, npeer = len(srcs), len(peers)

    def body(*refs):
        ins, lnd = refs[:n], refs[n:2 * n]
        ssem, rsem = refs[2 * n], refs[2 * n + 1]
        x, y, c = _me()
        for t in range(n):
            for k, (fx, fy, fc) in enumerate(peers):
                src, dst = route(ins[t], lnd[t], k, x, y, c)
                cp = pltpu.make_async_remote_copy(
                    src_ref=src, dst_ref=dst, send_sem=ssem.at[npeer * t + k], recv_sem=rsem.at[npeer * t + k],
                    device_id=(_flip(x, fx), _flip(y, fy), _flip(c, fc)), device_id_type=MESH)
                cp.wait_send()
                cp.wait_recv()

    bufs = list(srcs) + list(lands)
    outs = pl.pallas_call(
        body, name=name, out_shape=tuple(pltpu.HBM(b.shape, b.dtype) for b in bufs),
        in_specs=[HBM_SPEC] * (2 * n) + [SEM_SPEC, SEM_SPEC, ANY_SPEC], out_specs=tuple([HBM_SPEC] * (2 * n)),
        input_output_aliases={i: i for i in range(2 * n)},
        compiler_params=pltpu.CompilerParams(has_side_effects=DATAFLOW))(*bufs, send_sems, recv_sems, after)
    return list(outs[:n]), list(outs[n:])


OTHER_CHIPS = [(fx, fy, 0) for fx, fy in CHIP_FLIPS]
SIBLING = [(0, 0, 1)]


def _route_gather(src, land, k, x, y, c):
    return src, land.at[2 * x + y]


def _route_gather_half(src, land, k, x, y, c):
    h = src.shape[1] // 2
    return src.at[:, pl.ds(c * h, h)], land.at[2 * x + y, :, pl.ds(c * h, h)]


def _route_gather_half_wait(src, land, k, x, y, c):
    fx, fy = CHIP_FLIPS[k]
    h = src.shape[1] // 2
    return src.at[:, pl.ds(c * h, h)], land.at[2 * _flip(x, fx) + _flip(y, fy), :, pl.ds(c * h, h)]


def _route_gather_wait(src, land, k, x, y, c):
    fx, fy = CHIP_FLIPS[k]
    return src, land.at[2 * _flip(x, fx) + _flip(y, fy)]


def _route_scatter(src, land, k, x, y, c):
    fx, fy = CHIP_FLIPS[k]
    return src.at[2 * _flip(x, fx) + _flip(y, fy)], land.at[k]


def _route_exchange(src, land, k, x, y, c):
    h = land.shape[1]
    return src.at[:, pl.ds((1 - c) * h, h)], land


def _route_whole(src, land, k, x, y, c):
    return src, land


def _route_exchange_cols(src, land, k, x, y, c):
    h = land.shape[2]
    return src.at[:, :, pl.ds((1 - c) * h, h)], land


ALL_OTHERS = [((k >> 2) & 1, (k >> 1) & 1, k & 1) for k in range(1, 8)]


def _route_to_all(src, land, k, x, y, c):
    return src, land.at[4 * x + 2 * y + c]


def _route_to_all_wait(src, land, k, x, y, c):
    fx, fy, fc = ALL_OTHERS[k]
    return src, land.at[4 * _flip(x, fx) + 2 * _flip(y, fy) + _flip(c, fc)]


def _sum_devices(parts):
    def body(p_ref, o_ref):
        acc = p_ref[0]
        for d in range(1, 8):
            acc = acc + p_ref[d]
        o_ref[...] = acc

    vm = pl.BlockSpec(memory_space=pltpu.VMEM)
    return pl.pallas_call(body, name="allreduce_sum", in_specs=[vm], out_specs=vm,
                          out_shape=jax.ShapeDtypeStruct(parts.shape[1:], F32),
                          compiler_params=pltpu.CompilerParams(vmem_limit_bytes=VMEM_LIMIT))(parts)


def _grad_exchange_start(g4, tag, cols=False):
    land = (N_CHIPS, g4.shape[1], g4.shape[2] // 2) if cols else (N_CHIPS, g4.shape[1] // 2, g4.shape[2])
    route = _route_exchange_cols if cols else _route_exchange
    send_sems, recv_sems, srcs, lands, token = _push_start(
        [g4], [land], route, SIBLING, name="grad_exchange_start_" + tag)
    return (send_sems, recv_sems, srcs, lands, tag, cols), token


def _grad_scatter_start(state, pos, after):
    send_sems, recv_sems, srcs, lands, tag, cols = state
    route = _route_exchange_cols if cols else _route_exchange
    (g4,), (recv,) = _push_wait(send_sems, recv_sems, srcs, lands, after, route, SIBLING,
                                name="grad_exchange_wait_" + tag)
    return _grad_pair_scatter(g4, recv, pos, tag, cols)


def _grad_pair_scatter(g4, recv, pos, tag, cols=False):
    p16 = _sum_own_half(g4, recv, pos, name="grad_sum_pair_" + tag, cols=cols)
    send_sems, recv_sems, srcs, lands, token = _push_start(
        [p16], [(3,) + p16.shape[1:]], _route_scatter, OTHER_CHIPS, name="grad_scatter_start_" + tag)
    return (g4, recv, send_sems, recv_sems, srcs, lands, tag, cols), token


def _grad_sum_and_share(state, pos, after):
    g4, recv, send_sems, recv_sems, srcs, lands, tag, cols = state
    parts = _push_wait(send_sems, recv_sems, srcs, lands, after, _route_scatter, OTHER_CHIPS,
                       name="grad_scatter_wait_" + tag)[1][0]
    mine = _sum_chips(g4, recv, parts, pos, name="grad_sum_chips_" + tag, cols=cols)
    send_sems, recv_sems, srcs, lands, token = _push_start(
        [mine], [mine.shape], _route_whole, SIBLING, name="grad_share_start_" + tag)
    return (send_sems, recv_sems, srcs, lands, tag), token


def _grad_share_wait(state, after):
    send_sems, recv_sems, srcs, lands, tag = state
    (mine,), (theirs,) = _push_wait(send_sems, recv_sems, srcs, lands, after, _route_whole, SIBLING,
                                    name="grad_share_wait_" + tag)
    return mine, theirs


def _local_step(x, tgt, p, hooks):
    t = x.shape[0]
    tables = _rope_tables(t)
    sinks = p['sinks'].reshape(N_Q_HEADS)

    def told(name, value):
        return tuple(hooks.grad_ready(name, value))

    xn = _rmsnorm_fwd(x, p['norm_mix'], "norm_mix_fwd", deps=hooks.first_deps)
    w_in_t, w_in_dt, in_deps = hooks.weight_in(xn)
    proj = _matmul(xn, w_in_t, mode='nt', name="in_proj", n_limit=MAIN_WIDTH, deps=in_deps)
    dt_raw = _matmul(xn, w_in_dt, mode='nt', name="in_proj_dt")[:, :SSD_HEADS]
    ssd_conv_w, ffn_conv_w = hooks.conv_weights(proj)
    p = dict(p, ssd_conv_w=ssd_conv_w, ffn_conv_w=ffn_conv_w)
    attn = _attn_fwd(proj, sinks, tables)
    conv_b = p['ssd_conv_b']
    xbc, xbc_pre = _conv_silu_fwd(proj, p['ssd_conv_w'], conv_b, col0=O_XBC, width=CONV_CH, name="ssd_conv_fwd")
    sp = _ssd_params(dt_raw, p['dt_bias'].reshape(-1), p['a_log'].reshape(-1), p['ssd_d'].reshape(-1))
    y, states = _ssd_fwd(xbc, sp)
    mix = _mix_fwd(attn, y, proj, p['attn_out_norm'], p['ssd_norm'])
    w_out = hooks.weight('w_out', mix)
    h1 = _matmul(mix, w_out, mode='nn', name="out_proj", add=x)
    hn = _rmsnorm_fwd(h1, p['norm_ffn'], "norm_ffn_fwd")
    w_up = hooks.weight('w_up', hn)
    u0 = _matmul(hn, w_up, mode='nn', name="ffn_up", b_owner=True, tn=1408)
    a, u = _ffn_act_fwd(u0, p['ffn_conv_w'], p['ffn_conv_b'])
    w_down = hooks.weight('w_down', a)
    h2 = _matmul(a, w_down, mode='nn', name="ffn_down", add=h1, tk=2816)
    loss, dh2, dh2_16, g_norm_final = _final_loss(h2, p['norm_final'].reshape(1, D_MODEL), tgt)

    g = {}
    da = _matmul(dh2_16, w_down, mode='nt', name="ffn_down_dx", out_dtype=BF16, tn=1408)
    g['w_down'] = _matmul(a, dh2_16, mode='tn', name="ffn_down_dw", tm=1408)
    dep = told('w_down', g['w_down'])
    du0, dcw, dcb = _ffn_act_bwd(u0, u, p['ffn_conv_w'], da)
    g['ffn_conv_w'] = dcw.transpose(1, 0, 2).reshape(FFN_CONV, 2 * D_FF)
    g['ffn_conv_b'] = dcb.transpose(1, 0, 2).reshape(1, 2 * D_FF)
    g['w_up'] = _matmul(hn, du0, mode='tn', name="ffn_up_dw", deps=dep, b_halves=True, owner_major=True,
                        tn=1408)
    dep = told('w_up', g['w_up'])
    dhn = _matmul(du0, w_up, mode='nt', name="ffn_up_dx", out_dtype=BF16, deps=dep, a_halves=True,
                  b_owner=True, tk=2816)
    dh1, dh1_16, g['norm_ffn'] = _rmsnorm_bwd(h1, p['norm_ffn'], dhn, dh2, "norm_ffn_bwd")

    g['w_out'] = _matmul(mix, dh1_16, mode='tn', name="out_proj_dw")
    dep = told('w_out', g['w_out'])
    dmix = _matmul(dh1_16, w_out, mode='nt', name="out_proj_dx", out_dtype=BF16, deps=dep)
    dattn, dy, dz, g['attn_out_norm'], g['ssd_norm'] = _mix_bwd(dmix, attn, y, proj, p['attn_out_norm'],
                                                                p['ssd_norm'])
    dq, dk, dv, dsink = _attn_bwd(proj, sinks, tables, dattn)
    g['sinks'] = dsink[:, :, 0].reshape(1, N_Q_HEADS)
    dxs, dbm, dcm, ddt8, dpar = _ssd_bwd(xbc, xbc_pre, sp, states, dy)
    dpar = dpar[:, :, ::SSD_HEAD_DIM]
    g['dt_bias'] = dpar[:, 0, :].reshape(1, SSD_HEADS)
    g['a_log'] = dpar[:, 1, :].reshape(1, SSD_HEADS)
    g['ssd_d'] = dpar[:, 2, :].reshape(1, SSD_HEADS)
    dxbc, g['ssd_conv_w'], g['ssd_conv_b'] = _ssd_conv_bwd(proj, p['ssd_conv_w'], dxs, dbm, dcm, col0=O_XBC,
                                                           name="ssd_conv_bwd")
    dproj = jnp.concatenate([dq, dk, dv, dz, dxbc], axis=1)
    ddt = ddt8.transpose(2, 0, 1).reshape(t, SSD_HEADS)
    ddt_pad = jnp.pad(ddt, ((0, 0), (0, LANES - SSD_HEADS))).astype(BF16)
    g['w_in'] = (_matmul(dproj, xn, mode='tn', name="in_proj_dw", m_rows=IN_PROJ_WIDTH),
                 _matmul(ddt_pad, xn, mode='tn', name="in_proj_dt_dw"))
    dep = told('w_in', g['w_in'])
    dxn_dt = _matmul(ddt_pad, w_in_dt, mode='nn', name="in_proj_dt_dx", deps=dep)
    dep = told(None, dxn_dt)
    dxn = _matmul(dproj, w_in_t, mode='nn', name="in_proj_dx", out_dtype=BF16, add=dxn_dt, k_limit=MAIN_WIDTH,
                  tk=2304, deps=dep)
    dx, _, g['norm_mix'] = _rmsnorm_bwd(x, p['norm_mix'], dxn, dh1, "norm_mix_bwd")
    g['norm_final'] = g_norm_final
    return loss, dx, g


def _pack(arrs):
    flat = jnp.concatenate([a.reshape(-1) for a in arrs])
    n = flat.shape[0]
    rows = -(-n // LANES)
    rows = -(-rows // 8) * 8
    return jnp.pad(flat, (0, rows * LANES - n)).reshape(rows, LANES)


def _unpack(packed, shapes):
    flat = packed.reshape(-1)
    out, off = [], 0
    for s in shapes:
        n = 1
        for d in s:
            n *= d
        out.append(flat[off:off + n].reshape(s))
        off += n
    return out


class _StepHooks:
    def __init__(self, first_deps, weight_in, conv_weights, weight, grad_ready):
        self.first_deps = first_deps
        self.weight_in = weight_in
        self.conv_weights = conv_weights
        self.weight = weight
        self.grad_ready = grad_ready


def kernel(x, norm_mix, w_in, sinks, attn_out_norm, ssd_conv_w, ssd_conv_b, dt_bias, a_log, ssd_d, ssd_norm, w_out, norm_ffn, w_up, ffn_conv_w, ffn_conv_b, w_down, norm_final, loss_target, m_norm_mix, m_w_in, m_sinks, m_attn_out_norm, m_ssd_conv_w, m_ssd_conv_b, m_dt_bias, m_a_log, m_ssd_d, m_ssd_norm, m_w_out, m_norm_ffn, m_w_up, m_ffn_conv_w, m_ffn_conv_b, m_w_down, m_norm_final, v_norm_mix, v_w_in, v_sinks, v_attn_out_norm, v_ssd_conv_w, v_ssd_conv_b, v_dt_bias, v_a_log, v_ssd_d, v_ssd_norm, v_w_out, v_norm_ffn, v_w_up, v_ffn_conv_w, v_ffn_conv_b, v_w_down, v_norm_final):
    args = dict(locals())
    w = {n: args[n] for n in WEIGHTS}
    m = {n: args['m_' + n] for n in WEIGHTS}
    v = {n: args['v_' + n] for n in WEIGHTS}
    xi, yi, ci = _me()
    chip = 2 * xi + yi
    pos = jnp.stack([ci, chip]).astype(jnp.int32)

    conv_shard = _pack([ssd_conv_w[0], ffn_conv_w[0]])
    conv_gather = _push_start([conv_shard], [(N_CHIPS,) + conv_shard.shape], _route_gather, OTHER_CHIPS,
                              name="gather_start_conv")

    def conv_weights(after):
        send_sems, recv_sems, srcs, lands, _ = conv_gather
        (own,), (got,) = _push_wait(send_sems, recv_sems, srcs, lands, after, _route_gather_wait, OTHER_CHIPS,
                                    name="gather_wait_conv")
        whole = lax.dynamic_update_slice(got, own[None], (chip, 0, 0))
        per_chip = [_unpack(whole[j], [ssd_conv_w.shape[1:], ffn_conv_w.shape[1:]]) for j in range(N_CHIPS)]
        return (jnp.concatenate([pc[0] for pc in per_chip], axis=1),
                jnp.concatenate([pc[1] for pc in per_chip], axis=1))

    w_in_t, m_in_t, v_in_t = (jnp.transpose(a[0]) for a in (w_in, m_w_in, v_w_in))
    in_shard = (w_in_t + conv_gather[4][:1, :1]).astype(BF16)
    in_gather = _push_start([in_shard], [(N_CHIPS,) + in_shard.shape], _route_gather_half, OTHER_CHIPS,
                            name="gather_start_w_in")
    gathers = {}
    order = in_gather[4][:1, :1]
    for n, shard in (('w_out', w_out[0]), ('w_up', w_up[0]), ('w_down', w_down[0])):
        shard = (shard + order).astype(BF16)
        gathers[n] = _push_start([shard], [(N_CHIPS,) + shard.shape], _route_gather, OTHER_CHIPS,
                                 name="gather_start_" + n)
        order = gathers[n][4][:1, :1]

    def weight_in(after):
        send_sems, recv_sems, srcs, lands, _ = in_gather
        (own,), (got,) = _push_wait(send_sems, recv_sems, srcs, lands, after, _route_gather_half_wait, OTHER_CHIPS,
                                    name="gather_wait_w_in")
        got, _ = _forward_halves(got)
        full_in_t = lax.dynamic_update_slice(got, own[None], (chip, 0, 0)).reshape(IN_PROJ_WIDTH, D_MODEL)
        w_in_dt = jnp.pad(full_in_t[MAIN_WIDTH:], ((0, LANES - SSD_HEADS), (0, 0)))
        return full_in_t, w_in_dt, ()

    def weight(name, after):
        send_sems, recv_sems, srcs, lands, _ = gathers[name]
        (own,), (got,) = _push_wait(send_sems, recv_sems, srcs, lands, after, _route_gather_wait, OTHER_CHIPS,
                                    name="gather_wait_" + name)
        whole = lax.dynamic_update_slice(got, own[None], (chip, 0, 0))
        return whole if name == 'w_up' else whole.reshape(-1, D_MODEL)

    reductions, exchanging = {}, {}

    def flush(after):
        tokens = []
        for prev in list(exchanging):
            reductions[prev], token = _grad_scatter_start(exchanging.pop(prev), pos, after)
            tokens.append(token)
        return tokens

    def grad_ready(name, value):
        if name is None:
            return flush(value)
        if name == 'w_in':
            main, dtp = value
            value = lax.dynamic_update_slice(main, dtp[:SSD_HEADS], (MAIN_WIDTH, 0))
        g4 = value if value.ndim == 3 else value.reshape(N_CHIPS, -1, value.shape[1])
        tokens = flush(g4)
        exchanging[name], token = _grad_exchange_start(g4, name, cols=(name == 'w_in'))
        return tokens + [token]

    small = {
        'norm_mix': norm_mix, 'sinks': sinks, 'attn_out_norm': attn_out_norm,
        'ssd_conv_b': ssd_conv_b, 'dt_bias': dt_bias, 'a_log': a_log, 'ssd_d': ssd_d, 'ssd_norm': ssd_norm,
        'norm_ffn': norm_ffn, 'ffn_conv_b': ffn_conv_b, 'norm_final': norm_final,
    }
    loss, dx, g = _local_step(x[0], loss_target[0], small,
                              _StepHooks((gathers['w_down'][4],), weight_in, conv_weights, weight, grad_ready))

    small_names = [n for n in WEIGHTS if n not in BIG]
    small_g = [loss[:, :1]] + [g[n] for n in small_names]
    small_shapes = [(1, 1)] + [tuple(a.shape) for a in small_g[1:]]
    packed = _pack(small_g)
    spread = _push_start([packed], [(8,) + packed.shape], _route_to_all, ALL_OTHERS, name="allreduce_start")
    grads, deltas, new_m, new_v = {}, {}, {}, {}
    after = spread[4]
    shares = {}
    for n in ('w_down', 'w_up', 'w_out', 'w_in'):
        shares[n], after = _grad_sum_and_share(reductions[n], pos, after)
    for n in ('w_down', 'w_up', 'w_out', 'w_in'):
        mine, theirs = _grad_share_wait(shares[n], after)
        if n == 'w_in':
            outs = _adamw_halves(w_in_t, mine, theirs, m_in_t, v_in_t, pos, name="adamw_" + n, cols=True)
            after = outs[1]
            outs = [jnp.transpose(o) for o in outs]
        else:
            outs = _adamw_halves(w[n][0], mine, theirs, m[n][0], v[n][0], pos, name="adamw_" + n)
            after = outs[1]
        grads[n], deltas[n], new_m[n], new_v[n] = [o[None] for o in outs]
    (own,), (landed,) = _push_wait(spread[0], spread[1], spread[2], spread[3], after, _route_to_all_wait, ALL_OTHERS,
                                   name="allreduce_wait")
    landed = lax.dynamic_update_slice(landed, own[None], (4 * xi + 2 * yi + ci, 0, 0))
    red = _unpack(_sum_devices(landed), small_shapes)
    loss_out = red[0].reshape(())
    gsm = dict(zip(small_names, red[1:]))
    gsm['ssd_conv_w'] = lax.dynamic_slice(gsm['ssd_conv_w'], (0, chip * ssd_conv_w.shape[2]),
                                          (SSD_CONV, ssd_conv_w.shape[2]))
    gsm['ffn_conv_w'] = lax.dynamic_slice(gsm['ffn_conv_w'], (0, chip * ffn_conv_w.shape[2]),
                                          (FFN_CONV, ffn_conv_w.shape[2]))

    shapes = [tuple(w[n].shape) for n in small_names]
    gp = _pack([gsm[n] for n in small_names])
    d, m2, v2 = _adamw(_pack([w[n] for n in small_names]), gp, _pack([m[n] for n in small_names]),
                       _pack([v[n] for n in small_names]), name="adamw_small")
    for n, gg, dd, mm, vv in zip(small_names, _unpack(gp, shapes), _unpack(d, shapes), _unpack(m2, shapes),
                                 _unpack(v2, shapes)):
        grads[n], deltas[n], new_m[n], new_v[n] = gg, dd, mm, vv

    return (loss_out, dx[None], *[grads[n] for n in WEIGHTS], *[deltas[n] for n in WEIGHTS],
            *[new_m[n] for n in WEIGHTS], *[new_v[n] for n in WEIGHTS])
```

```python
import functools

import jax
import jax.numpy as jnp
from jax import lax
from jax.experimental import pallas as pl
from jax.experimental.pallas import tpu as pltpu

F32 = jnp.float32
BF16 = jnp.bfloat16

D_MODEL = 2048
N_Q_HEADS = 32
N_KV_HEADS = 8
HEAD_DIM = 64
WINDOW = 128
ATTN_BLOCK = 128
ROT_DIM = 16
ROPE_THETA = 500000.0
SSD_HEADS = 32
SSD_HEAD_DIM = 64
SSD_INNER = 2048
SSD_GROUPS = 8
SSD_STATE = 128
SSD_CONV = 4
SSD_CHUNK = 128
ATTN_WIDTH = 2048
KV_WIDTH = 512
BC_WIDTH = 1024
CONV_CH = 4096
IN_PROJ_WIDTH = 9248
MAIN_WIDTH = 9216
D_FF = 5632
FFN_CONV = 3
EPS = 1e-6
O_Q, O_K, O_V, O_Z, O_XBC, O_DT = 0, 2048, 2560, 3072, 5120, 9216

ADAM_LR = 0.001
ADAM_B1 = 0.9
ADAM_B2 = 0.999
ADAM_EPS = 1e-08
ADAM_WD = 0.01
ADAM_STEP = 10

N_CHIPS = 4
NEG = -1e30
LANES = 128
VMEM_LIMIT = 48 * 1024 * 1024
MESH = pl.DeviceIdType.MESH
HBM_SPEC = pl.BlockSpec(memory_space=pltpu.HBM)
TOKEN = jax.ShapeDtypeStruct((8, LANES), F32)

WEIGHTS = ['norm_mix', 'w_in', 'sinks', 'attn_out_norm', 'ssd_conv_w', 'ssd_conv_b', 'dt_bias', 'a_log', 'ssd_d',
           'ssd_norm', 'w_out', 'norm_ffn', 'w_up', 'ffn_conv_w', 'ffn_conv_b', 'w_down', 'norm_final']
BIG = ['w_in', 'w_out', 'w_up', 'w_down']


def _cp(sem=None, vmem=VMEM_LIMIT):
    kw = {'vmem_limit_bytes': vmem}
    if sem is not None:
        kw['dimension_semantics'] = sem
    return pltpu.CompilerParams(**kw)


def _tile(n, pref):
    if n <= pref:
        return n
    t = (pref // LANES) * LANES
    while t > LANES and n % t:
        t -= LANES
    assert n % t == 0, (n, pref)
    return t


def _rows(n, pref):
    t = min(n, pref)
    while n % t:
        t -= 8
    if 4 * t < pref:
        t = pref
        while n % t:
            t += 8
    return t


def _iota(shape, dim):
    return lax.broadcasted_iota(jnp.int32, shape, dim)


def _dot(a, b, mode='nn'):
    dn = {'nn': (((1,), (0,)), ((), ())), 'nt': (((1,), (1,)), ((), ())), 'tn': (((0,), (0,)), ((), ()))}[mode]
    return lax.dot_general(a.astype(BF16), b.astype(BF16), dn, preferred_element_type=F32)


def _dot_exact(a, b):
    return lax.dot_general(a, b, (((1,), (0,)), ((), ())), precision=lax.Precision.HIGHEST,
                           preferred_element_type=F32)


def _sigmoid(x):
    return 1.0 / (1.0 + jnp.exp(-x))


def _softplus(x):
    return jnp.maximum(x, 0.0) + jnp.log(1.0 + jnp.exp(-jnp.abs(x)))


def _matmul(a, b, *, mode, name, out_dtype=F32, add=None, deps=(), tm=1024, tn=1024, tk=2048,
            a_halves=False, b_halves=False, b_owner=False, owner_major=False, n_limit=None, k_limit=None,
            m_rows=None):
    ash, bsh = (a.shape[1:] if a_halves else a.shape), (b.shape[1:] if (b_halves or b_owner) else b.shape)
    if mode == 'nn':
        (m, k), (k2, n) = ash, bsh
    elif mode == 'nt':
        (m, k), (n, k2) = ash, bsh
    else:
        (k, m), (k2, n) = ash, bsh
    if n_limit is not None:
        assert mode == 'nt' and n_limit <= n
        n = n_limit
    if k_limit is not None:
        assert mode == 'nn' and k_limit <= k2
        k2 = k_limit
    if a_halves:
        assert mode == 'nt'
        k = 2 * k
    if b_halves:
        assert mode == 'tn'
        n = 2 * n
    if b_owner:
        assert mode in ('nn', 'nt')
        if mode == 'nn':
            n = 4 * n
        else:
            k2 = 4 * k2
    assert k == k2, (a.shape, b.shape, mode)
    tm = _tile(m, tm)
    tn = _tile(n // 4 if (owner_major or (b_owner and mode == 'nn')) else (n // 2 if b_halves else n), tn)
    tk = _tile(k // 4 if (b_owner and mode == 'nt') else (k // 2 if a_halves else k), tk)
    nk = k // tk
    has_add = add is not None
    assert not (has_add and owner_major)

    def body(*refs):
        a_ref, b_ref = refs[:2]
        add_ref = refs[2] if has_add else None

        def finish(r, o_ref):
            if has_add:
                r = r + add_ref[...].astype(F32)
            o_ref[...] = r.astype(out_dtype)

        if nk == 1:
            finish(_dot(a_ref[...], b_ref[...], mode), refs[-1])
            return
        o_ref, acc = refs[-2:]
        kk = pl.program_id(2)

        @pl.when(kk == 0)
        def _():
            acc[...] = _dot(a_ref[...], b_ref[...], mode)

        @pl.when((kk > 0) & (kk < nk - 1))
        def _():
            acc[...] += _dot(a_ref[...], b_ref[...], mode)

        @pl.when(kk == nk - 1)
        def _():
            finish(acc[...] + _dot(a_ref[...], b_ref[...], mode), o_ref)

    if mode == 'tn':
        a_spec = pl.BlockSpec((tk, tm), lambda i, j, kk: (kk, i))
    elif a_halves:
        nkh = nk // 2
        a_spec = pl.BlockSpec((None, tm, tk), lambda i, j, kk: (kk // nkh, i, kk % nkh))
    else:
        a_spec = pl.BlockSpec((tm, tk), lambda i, j, kk: (i, kk))
    if mode == 'nt' and b_owner:
        nkq = nk // 4
        b_spec = pl.BlockSpec((None, tn, tk), lambda i, j, kk: (kk // nkq, j, kk % nkq))
    elif mode == 'nt':
        b_spec = pl.BlockSpec((tn, tk), lambda i, j, kk: (j, kk))
    elif b_owner:
        njq = (n // 4) // tn
        b_spec = pl.BlockSpec((None, tk, tn), lambda i, j, kk: (j // njq, kk, j % njq))
    elif b_halves:
        njh = (n // 2) // tn
        b_spec = pl.BlockSpec((None, tk, tn), lambda i, j, kk: (j // njh, kk, j % njh))
    else:
        b_spec = pl.BlockSpec((tk, tn), lambda i, j, kk: (kk, j))
    if owner_major:
        njo = (n // 4) // tn
        o_spec = pl.BlockSpec((None, tm, tn), lambda i, j, kk: (j // njo, i, j % njo))
        out_shape = jax.ShapeDtypeStruct((N_CHIPS, m, n // 4), out_dtype)
    else:
        o_spec = pl.BlockSpec((tm, tn), lambda i, j, kk: (i, j))
        out_shape = jax.ShapeDtypeStruct((m if m_rows is None else m_rows, n), out_dtype)
    dep_spec = pl.BlockSpec((8, LANES), lambda i, j, kk: (0, 0))
    in_specs = [a_spec, b_spec] + ([pl.BlockSpec((tm, tn), lambda i, j, kk: (i, j))] if has_add else [])
    in_specs += [dep_spec] * len(deps)
    args = (a, b) + ((add,) if has_add else ()) + tuple(deps)
    return pl.pallas_call(
        body, name=name, grid=(m // tm, n // tn, nk), in_specs=in_specs, out_specs=o_spec, out_shape=out_shape,
        scratch_shapes=[pltpu.VMEM((tm, tn), F32)] if nk > 1 else [],
        compiler_params=_cp(("parallel", "parallel", "arbitrary")))(*args)


def _rmsnorm_fwd(x, g, name, deps=()):
    t, d = x.shape
    tb = _rows(t, 256)

    def body(x_ref, g_ref, *rest):
        o_ref = rest[-1]
        xv = x_ref[...]
        r = lax.rsqrt(jnp.mean(xv * xv, axis=-1, keepdims=True) + EPS)
        o_ref[...] = (xv * r * g_ref[...]).astype(BF16)

    dep_spec = pl.BlockSpec((8, LANES), lambda i: (0, 0))
    return pl.pallas_call(
        body, name=name, grid=(t // tb,),
        in_specs=[pl.BlockSpec((tb, d), lambda i: (i, 0)), pl.BlockSpec((1, d), lambda i: (0, 0))]
        + [dep_spec] * len(deps),
        out_specs=pl.BlockSpec((tb, d), lambda i: (i, 0)), out_shape=jax.ShapeDtypeStruct((t, d), BF16),
        compiler_params=_cp(("parallel",)))(x, g, *deps)


def _rmsnorm_bwd(x, g, dy, res, name, deps=()):
    t, d = x.shape
    tb = _rows(t, 256)

    def body(x_ref, g_ref, dy_ref, res_ref, *rest):
        dx_ref, dx16_ref, dg_ref = rest[-3:]
        i = pl.program_id(0)
        xv = x_ref[...]
        dyv = dy_ref[...].astype(F32)
        r = lax.rsqrt(jnp.mean(xv * xv, axis=-1, keepdims=True) + EPS)
        u = dyv * g_ref[...]
        dx = r * u - xv * (r * r * r * jnp.mean(u * xv, axis=-1, keepdims=True)) + res_ref[...]
        dx_ref[...] = dx
        dx16_ref[...] = dx.astype(BF16)
        part = jnp.sum(dyv * xv * r, axis=0, keepdims=True)

        @pl.when(i == 0)
        def _():
            dg_ref[...] = part

        @pl.when(i > 0)
        def _():
            dg_ref[...] += part

    row = pl.BlockSpec((tb, d), lambda i: (i, 0))
    vec = pl.BlockSpec((1, d), lambda i: (0, 0))
    return pl.pallas_call(
        body, name=name, grid=(t // tb,),
        in_specs=[row, vec, row, row] + [pl.BlockSpec((8, LANES), lambda i: (0, 0))] * len(deps),
        out_specs=[row, row, vec],
        out_shape=[jax.ShapeDtypeStruct((t, d), F32), jax.ShapeDtypeStruct((t, d), BF16),
                   jax.ShapeDtypeStruct((1, d), F32)],
        compiler_params=_cp(("arbitrary",)))(x, g, dy, res, *deps)


def _final_loss(h, g, tgt):
    t, d = h.shape
    tb = _rows(t, 256)

    def body(h_ref, g_ref, t_ref, loss_ref, dh_ref, dh16_ref, dg_ref):
        i = pl.program_id(0)
        hv = h_ref[...]
        gv = g_ref[...]
        r = lax.rsqrt(jnp.mean(hv * hv, axis=-1, keepdims=True) + EPS)
        y = hv * r * gv
        diff = y - t_ref[...]
        lpart = jnp.sum(jnp.sum(diff * diff, axis=1, keepdims=True), axis=0, keepdims=True) * (0.5 / d)
        dy = diff * (1.0 / d)
        u = dy * gv
        dh = r * u - hv * (r * r * r * jnp.mean(u * hv, axis=-1, keepdims=True))
        dh_ref[...] = dh
        dh16_ref[...] = dh.astype(BF16)
        gpart = jnp.sum(dy * hv * r, axis=0, keepdims=True)
        lrow = jnp.broadcast_to(lpart, (1, LANES))

        @pl.when(i == 0)
        def _():
            loss_ref[...] = lrow
            dg_ref[...] = gpart

        @pl.when(i > 0)
        def _():
            loss_ref[...] += lrow
            dg_ref[...] += gpart

    row = pl.BlockSpec((tb, d), lambda i: (i, 0))
    vec = pl.BlockSpec((1, d), lambda i: (0, 0))
    return pl.pallas_call(
        body, name="final_loss", grid=(t // tb,), in_specs=[row, vec, row],
        out_specs=[pl.BlockSpec((1, LANES), lambda i: (0, 0)), row, row, vec],
        out_shape=[jax.ShapeDtypeStruct((1, LANES), F32), jax.ShapeDtypeStruct((t, d), F32),
                   jax.ShapeDtypeStruct((t, d), BF16), jax.ShapeDtypeStruct((1, d), F32)],
        compiler_params=_cp(("arbitrary",)))(h, g, tgt)


def _rope_tables(t):
    pos = jnp.arange(t, dtype=F32)
    inv = 1.0 / (ROPE_THETA ** (jnp.arange(0, ROT_DIM, 2, dtype=F32) / ROT_DIM))
    ang = pos[:, None] * inv[None, :]
    cos, sin = jnp.cos(ang), jnp.sin(ang)
    half = ROT_DIM // 2
    rest = HEAD_DIM - ROT_DIM
    c = jnp.concatenate([cos, cos, jnp.ones((t, rest), F32)], axis=1)
    s1 = jnp.concatenate([-sin, jnp.zeros((t, half + rest), F32)], axis=1)
    s2 = jnp.concatenate([jnp.zeros((t, half), F32), sin, jnp.zeros((t, rest), F32)], axis=1)
    return jnp.concatenate([jnp.tile(v, (1, LANES // HEAD_DIM)) for v in (c, s1, s2)], axis=1)


def _split_tables(tab):
    return tab[:, :LANES], tab[:, LANES:2 * LANES], tab[:, 2 * LANES:]


def _rope(x, c, s1, s2):
    half = ROT_DIM // 2
    return x * c + pltpu.roll(x, LANES - half, 1) * s1 + pltpu.roll(x, half, 1) * s2


def _rope_t(g, c, s1, s2):
    half = ROT_DIM // 2
    return g * c + pltpu.roll(g * s1, half, 1) + pltpu.roll(g * s2, LANES - half, 1)


def _band_masks(i, heads):
    n = heads * ATTN_BLOCK
    q = jnp.bitwise_and(_iota((n, ATTN_BLOCK), 0), ATTN_BLOCK - 1)
    j = _iota((n, ATTN_BLOCK), 1)
    upper = j > q
    return upper, upper & (j < jnp.where(i > 0, 0, ATTN_BLOCK))


def _fold_band(full, upper):
    return jnp.where(upper, full[:, :ATTN_BLOCK], full[:, ATTN_BLOCK:])


def _unfold_band(band, upper):
    return jnp.concatenate([jnp.where(upper, band, 0.0), jnp.where(upper, 0.0, band)], axis=1)


def _half_masks():
    lane = _iota((1, LANES), 1)
    return [(lane < HEAD_DIM).astype(F32), (lane >= HEAD_DIM).astype(F32)]


def _stack_heads(blocks, hm, j):
    pieces = []
    for r in range(4):
        qb, half = (4 * j + r) // 2, (4 * j + r) % 2
        piece = blocks[qb] * hm[half]
        if half != j:
            piece = pltpu.roll(piece, HEAD_DIM, 1)
        pieces.append(piece)
    return jnp.concatenate(pieces, axis=0)


def _unstack_heads(stacked, j):
    out = []
    for qb in (2 * j, 2 * j + 1):
        acc = None
        for half in range(2):
            r = 2 * qb + half - 4 * j
            piece = stacked[r * ATTN_BLOCK:(r + 1) * ATTN_BLOCK]
            if half != j:
                piece = pltpu.roll(piece, HEAD_DIM, 1)
            acc = piece if acc is None else acc + piece
        out.append((qb, acc))
    return out


def _sink_column(sink_ref, base):
    return jnp.concatenate([jnp.full((ATTN_BLOCK, 1), sink_ref[base + r], F32) for r in range(4)], axis=0)


def _attn_specs(nb_clamp):
    blk = ATTN_BLOCK
    kb, vb = O_K // LANES, O_V // LANES

    def cur(i):
        return jnp.minimum(i, nb_clamp)

    def prev(i):
        return jnp.maximum(jnp.minimum(i, nb_clamp + 1) - 1, 0)

    q = pl.BlockSpec((blk, 512), lambda p, i: (cur(i), p))
    kc = pl.BlockSpec((blk, LANES), lambda p, i: (cur(i), kb + p))
    kp = pl.BlockSpec((blk, LANES), lambda p, i: (prev(i), kb + p))
    vc = pl.BlockSpec((blk, LANES), lambda p, i: (cur(i), vb + p))
    vp = pl.BlockSpec((blk, LANES), lambda p, i: (prev(i), vb + p))
    tc = pl.BlockSpec((blk, 3 * LANES), lambda p, i: (cur(i), 0))
    tp = pl.BlockSpec((blk, 3 * LANES), lambda p, i: (prev(i), 0))
    return q, kc, kp, vc, vp, tc, tp


def _attn_fwd(proj, sinks, tables):
    t = proj.shape[0]
    nb = t // ATTN_BLOCK
    scale = HEAD_DIM ** -0.5

    def body(sink_ref, q_ref, kc_ref, kp_ref, vc_ref, vp_ref, tc_ref, tp_ref, o_ref):
        p = pl.program_id(0)
        i = pl.program_id(1)
        cc, s1c, s2c = _split_tables(tc_ref[...])
        kband = jnp.concatenate([_rope(kp_ref[...], *_split_tables(tp_ref[...])),
                                 _rope(kc_ref[...], cc, s1c, s2c)], axis=0).astype(BF16)
        vband = jnp.concatenate([vp_ref[...], vc_ref[...]], axis=0)
        hm = _half_masks()
        vsel = [(vband * hm[j]).astype(BF16) for j in range(2)]
        upper, dropped = _band_masks(i, 1)
        qr = [_rope(q_ref[:, qb * LANES:(qb + 1) * LANES], cc, s1c, s2c) for qb in range(4)]

        def scores(hh):
            qb, half, j = hh // 2, hh % 2, hh // 4
            qs = qr[qb] * hm[half]
            if half != j:
                qs = pltpu.roll(qs, HEAD_DIM, 1)
            return _dot(qs, kband, 'nt')

        ahead = scores(0)
        acc = None
        for hh in range(8):
            qb, half, j = hh // 2, hh % 2, hh // 4
            raw = ahead
            if hh + 1 < 8:
                ahead = scores(hh + 1)
            s = jnp.where(dropped, NEG, _fold_band(raw, upper) * scale)
            sink = sink_ref[p * 8 + hh]
            m = jnp.maximum(jnp.max(s, axis=1, keepdims=True), sink)
            pe = jnp.exp(s - m)
            den = jnp.sum(pe, axis=1, keepdims=True) + jnp.exp(sink - m)
            o = _dot(_unfold_band(pe / den, upper), vsel[j])
            if half != j:
                o = pltpu.roll(o, HEAD_DIM, 1)
            acc = o if half == 0 else acc + o
            if half == 1:
                o_ref[:, qb * LANES:(qb + 1) * LANES] = acc

    q, kc, kp, vc, vp, tc, tp = _attn_specs(nb - 1)
    smem = pl.BlockSpec(memory_space=pltpu.SMEM)
    return pl.pallas_call(
        body, name="attn_fwd", grid=(4, nb),
        in_specs=[smem, q, kc, kp, vc, vp, tc, tp],
        out_specs=pl.BlockSpec((ATTN_BLOCK, 512), lambda p, i: (i, p)),
        out_shape=jax.ShapeDtypeStruct((t, ATTN_WIDTH), F32),
        compiler_params=_cp(("parallel", "arbitrary")))(sinks, proj, proj, proj, proj, proj, tables, tables)


def _attn_bwd(proj, sinks, tables, dout):
    t = proj.shape[0]
    nb = t // ATTN_BLOCK
    scale = HEAD_DIM ** -0.5

    def body(sink_ref, q_ref, kc_ref, kp_ref, vc_ref, vp_ref, tc_ref, tp_ref,
             do_ref, dq_ref, dk_ref, dv_ref, ds_ref, carry_k, carry_v):
        p = pl.program_id(0)
        i = pl.program_id(1)
        ptab = _split_tables(tp_ref[...])

        @pl.when(i == 0)
        def _():
            carry_k[...] = jnp.zeros_like(carry_k)
            carry_v[...] = jnp.zeros_like(carry_v)
            ds_ref[...] = jnp.zeros_like(ds_ref)

        @pl.when(i < nb)
        def _():
            cc, s1c, s2c = _split_tables(tc_ref[...])
            kband = jnp.concatenate([_rope(kp_ref[...], *ptab), _rope(kc_ref[...], cc, s1c, s2c)], axis=0)
            vband = jnp.concatenate([vp_ref[...], vc_ref[...]], axis=0)
            hm = _half_masks()
            kband16 = kband.astype(BF16)
            vband16 = vband.astype(BF16)
            upper, dropped = _band_masks(i, 4)
            dkb = jnp.zeros((2 * ATTN_BLOCK, LANES), F32)
            dvb = jnp.zeros((2 * ATTN_BLOCK, LANES), F32)
            row8 = _iota((8, LANES), 0)
            dsink = jnp.zeros((8, LANES), F32)
            qr = [_rope(q_ref[:, qb * LANES:(qb + 1) * LANES], cc, s1c, s2c) for qb in range(4)]
            dob = [do_ref[:, qb * LANES:(qb + 1) * LANES] for qb in range(4)]
            for j in range(2):
                qst = _stack_heads(qr, hm, j).astype(BF16)
                dost = _stack_heads(dob, hm, j).astype(BF16)
                s = jnp.where(dropped, NEG, _fold_band(_dot(qst, kband16, 'nt'), upper) * scale)
                sink = _sink_column(sink_ref, p * 8 + 4 * j)
                m = jnp.maximum(jnp.max(s, axis=1, keepdims=True), sink)
                pe = jnp.exp(s - m)
                psink = jnp.exp(sink - m)
                den = jnp.sum(pe, axis=1, keepdims=True) + psink
                pr = pe / den
                dvb = dvb + _dot(_unfold_band(pr, upper).T, dost)
                dp = _fold_band(_dot(dost, vband16, 'nt'), upper)
                delta = jnp.sum(pr * dp, axis=1, keepdims=True)
                dsc = _unfold_band(pr * (dp - delta) * scale, upper)
                dsk = psink / den * delta
                for r in range(4):
                    part = jnp.sum(dsk[r * ATTN_BLOCK:(r + 1) * ATTN_BLOCK])
                    dsink = dsink + jnp.where(row8 == 4 * j + r, -part, 0.0)
                for qb, dqb in _unstack_heads(_dot(dsc, kband * hm[j]), j):
                    dq_ref[:, qb * LANES:(qb + 1) * LANES] = _rope_t(dqb, cc, s1c, s2c).astype(BF16)
                dkb = dkb + _dot(dsc.T, qst)
            ds_ref[0] += dsink
            dk_ref[...] = _rope_t(carry_k[...] + dkb[:ATTN_BLOCK], *ptab).astype(BF16)
            dv_ref[...] = (carry_v[...] + dvb[:ATTN_BLOCK]).astype(BF16)
            carry_k[...] = dkb[ATTN_BLOCK:]
            carry_v[...] = dvb[ATTN_BLOCK:]

        @pl.when(i == nb)
        def _():
            dk_ref[...] = _rope_t(carry_k[...], *ptab).astype(BF16)
            dv_ref[...] = carry_v[...].astype(BF16)

    q, kc, kp, vc, vp, tc, tp = _attn_specs(nb - 1)
    smem = pl.BlockSpec(memory_space=pltpu.SMEM)
    qblk = pl.BlockSpec((ATTN_BLOCK, 512), lambda p, i: (jnp.minimum(i, nb - 1), p))
    kvout = pl.BlockSpec((ATTN_BLOCK, LANES), lambda p, i: (jnp.maximum(i - 1, 0), p))
    return pl.pallas_call(
        body, name="attn_bwd", grid=(4, nb + 1),
        in_specs=[smem, q, kc, kp, vc, vp, tc, tp, qblk],
        out_specs=[qblk, kvout, kvout, pl.BlockSpec((1, 8, LANES), lambda p, i: (p, 0, 0))],
        out_shape=[jax.ShapeDtypeStruct((t, ATTN_WIDTH), BF16), jax.ShapeDtypeStruct((t, KV_WIDTH), BF16),
                   jax.ShapeDtypeStruct((t, KV_WIDTH), BF16), jax.ShapeDtypeStruct((4, 8, LANES), F32)],
        scratch_shapes=[pltpu.VMEM((ATTN_BLOCK, LANES), F32), pltpu.VMEM((ATTN_BLOCK, LANES), F32)],
        compiler_params=_cp(("parallel", "arbitrary")))(sinks, proj, proj, proj, proj, proj, tables, tables, dout)


def _shift_rows(x, prev8, j):
    n, c = x.shape
    r = pltpu.roll(x.reshape(n // 8, 8, c), j, 1)
    before = pltpu.roll(prev8, j, 0)[None]
    if n > 8:
        before = jnp.concatenate([before, r[:-1]], axis=0)
    return jnp.where(_iota((1, 8, 1), 1) < j, before, r).reshape(n, c)


def _shift_rows_up(x, next8, j):
    n, c = x.shape
    r = pltpu.roll(x.reshape(n // 8, 8, c), 8 - j, 1)
    after = pltpu.roll(next8, 8 - j, 0)[None]
    if n > 8:
        after = jnp.concatenate([r[1:], after], axis=0)
    return jnp.where(_iota((1, 8, 1), 1) >= 8 - j, after, r).reshape(n, c)


def _conv_apply(x, prev8, w, b, taps):
    u = b + x * w[taps - 1:taps]
    for j in range(1, taps):
        u = u + _shift_rows(x, prev8, j) * w[taps - 1 - j:taps - j]
    return u


def _conv_grads(du, du_next8, x, w, taps):
    dx = du * w[taps - 1:taps]
    rowk = _iota((taps, 1), 0)
    dw = jnp.where(rowk == taps - 1, jnp.sum(du * x, axis=0, keepdims=True), 0.0)
    for j in range(1, taps):
        ahead = _shift_rows_up(du, du_next8, j)
        dx = dx + ahead * w[taps - 1 - j:taps - j]
        dw = dw + jnp.where(rowk == taps - 1 - j, jnp.sum(ahead * x, axis=0, keepdims=True), 0.0)
    return dx, dw, jnp.sum(du, axis=0, keepdims=True)


def _conv_specs(tb, tc, col0, t):
    c0 = col0 // tc
    cur = pl.BlockSpec((tb, tc), lambda j, i: (i, c0 + j))
    prev = pl.BlockSpec((8, tc), lambda j, i: (jnp.maximum(i * (tb // 8) - 1, 0), c0 + j))
    nxt = pl.BlockSpec((8, tc), lambda j, i: (jnp.minimum((i + 1) * (tb // 8), t // 8 - 1), c0 + j))
    return cur, prev, nxt


def _conv_silu_fwd(x, w, b, *, col0, width, name):
    t = x.shape[0]
    taps = w.shape[0]
    tb, tc = _rows(t, 512), _tile(width, 1024)
    assert col0 % tc == 0

    def body(x_ref, xp_ref, w_ref, b_ref, o_ref, u_ref):
        i = pl.program_id(1)
        prev8 = jnp.where(i > 0, xp_ref[...], 0.0)
        u = _conv_apply(x_ref[...], prev8, w_ref[...], b_ref[...], taps)
        u_ref[...] = u
        o_ref[...] = u * _sigmoid(u)

    cur, prev, _ = _conv_specs(tb, tc, col0, t)
    par = pl.BlockSpec((taps, tc), lambda j, i: (0, j))
    bias = pl.BlockSpec((1, tc), lambda j, i: (0, j))
    out = pl.BlockSpec((tb, tc), lambda j, i: (i, j))
    shp = jax.ShapeDtypeStruct((t, width), F32)
    return pl.pallas_call(
        body, name=name, grid=(width // tc, t // tb), in_specs=[cur, prev, par, bias], out_specs=[out, out],
        out_shape=[shp, shp], compiler_params=_cp(("parallel", "parallel")))(x, x, w, b)


def _dsilu(u):
    sg = _sigmoid(u)
    return sg * (1.0 + u * (1.0 - sg))


def _ssd_conv_bwd(x, w, dxs, dbm, dcm, *, col0, name):
    t = x.shape[0]
    taps = w.shape[0]
    tb, tc = _rows(t, 512), BC_WIDTH
    nrow, ncol = t // tb, CONV_CH // tc
    c0 = col0 // tc

    def body(x_ref, w_ref, xs_ref, xsn_ref, bm_ref, bmn_ref, cm_ref, cmn_ref, dx_ref, dw_ref, db_ref):
        i = pl.program_id(0)
        j = pl.program_id(1)

        def run(du_ref, dun_ref):
            next8 = jnp.where(i < nrow - 1, dun_ref[...], 0.0)
            dx, dwv, dbv = _conv_grads(du_ref[...], next8, x_ref[...], w_ref[...], taps)
            dx_ref[...] = dx.astype(BF16)

            @pl.when(i == 0)
            def _():
                dw_ref[j] = dwv
                db_ref[j] = dbv

            @pl.when(i > 0)
            def _():
                dw_ref[j] += dwv
                db_ref[j] += dbv

        pl.when(j < 2)(lambda: run(xs_ref, xsn_ref))
        pl.when(j == 2)(lambda: run(bm_ref, bmn_ref))
        pl.when(j == 3)(lambda: run(cm_ref, cmn_ref))

    def nxt_row(i):
        return jnp.minimum((i + 1) * (tb // 8), t // 8 - 1)

    xs_col = lambda j: jnp.minimum(j, SSD_INNER // tc - 1)
    in_specs = [pl.BlockSpec((tb, tc), lambda i, j: (i, c0 + j)), pl.BlockSpec((taps, tc), lambda i, j: (0, j)),
                pl.BlockSpec((tb, tc), lambda i, j: (i, xs_col(j))),
                pl.BlockSpec((8, tc), lambda i, j: (nxt_row(i), xs_col(j))),
                pl.BlockSpec((tb, tc), lambda i, j: (i, 0)), pl.BlockSpec((8, tc), lambda i, j: (nxt_row(i), 0)),
                pl.BlockSpec((tb, tc), lambda i, j: (i, 0)), pl.BlockSpec((8, tc), lambda i, j: (nxt_row(i), 0))]
    dx, dw, db = pl.pallas_call(
        body, name=name, grid=(nrow, ncol), in_specs=in_specs,
        out_specs=[pl.BlockSpec((tb, tc), lambda i, j: (i, j)),
                   pl.BlockSpec((ncol, taps, tc), lambda i, j: (0, 0, 0)),
                   pl.BlockSpec((ncol, 1, tc), lambda i, j: (0, 0, 0))],
        out_shape=[jax.ShapeDtypeStruct((t, CONV_CH), BF16), jax.ShapeDtypeStruct((ncol, taps, tc), F32),
                   jax.ShapeDtypeStruct((ncol, 1, tc), F32)],
        compiler_params=_cp(("arbitrary", "arbitrary")))(x, w, dxs, dxs, dbm, dbm, dcm, dcm)
    return dx, dw.transpose(1, 0, 2).reshape(taps, CONV_CH), db.transpose(1, 0, 2).reshape(1, CONV_CH)


def _ffn_specs(tb, tc, t):
    nc = D_FF // tc

    def cur(half):
        return pl.BlockSpec((tb, tc), lambda j, i: (i, half * nc + j))

    def prev(half):
        return pl.BlockSpec((8, tc), lambda j, i: (jnp.maximum(i * (tb // 8) - 1, 0), half * nc + j))

    def nxt(half):
        return pl.BlockSpec((8, tc), lambda j, i: (jnp.minimum((i + 1) * (tb // 8), t // 8 - 1), half * nc + j))

    def par(rows, half):
        return pl.BlockSpec((rows, tc), lambda j, i: (0, half * nc + j))

    return cur, prev, nxt, par


def _ffn_act_fwd(u0, w, b):
    t = u0.shape[0]
    tb, tc = _rows(t, 512), _tile(D_FF, 1408)
    cur, prev, _, par = _ffn_specs(tb, tc, t)

    def body(g_ref, gp_ref, v_ref, vp_ref, wg_ref, wv_ref, bg_ref, bv_ref, o_ref, u_ref):
        i = pl.program_id(1)
        ug = _conv_apply(g_ref[...], jnp.where(i > 0, gp_ref[...], 0.0), wg_ref[...], bg_ref[...], FFN_CONV)
        uv = _conv_apply(v_ref[...], jnp.where(i > 0, vp_ref[...], 0.0), wv_ref[...], bv_ref[...], FFN_CONV)
        o_ref[...] = (ug * _sigmoid(ug) * uv).astype(BF16)
        u_ref[0] = ug
        u_ref[1] = uv

    return pl.pallas_call(
        body, name="ffn_act_fwd", grid=(D_FF // tc, t // tb),
        in_specs=[cur(0), prev(0), cur(1), prev(1), par(FFN_CONV, 0), par(FFN_CONV, 1), par(1, 0), par(1, 1)],
        out_specs=[pl.BlockSpec((tb, tc), lambda j, i: (i, j)), pl.BlockSpec((2, tb, tc), lambda j, i: (0, i, j))],
        out_shape=[jax.ShapeDtypeStruct((t, D_FF), BF16), jax.ShapeDtypeStruct((2, t, D_FF), F32)],
        compiler_params=_cp(("parallel", "parallel")))(u0, u0, u0, u0, w, w, b, b)


def _ffn_act_bwd(u0, u, w, da):
    t = u0.shape[0]
    tb, tc = _rows(t, 256), _tile(D_FF, 1408)
    nrow = t // tb
    taps = FFN_CONV
    cur, _, _, par = _ffn_specs(tb, tc, t)

    def dact(ug, uv, dav):
        sg = _sigmoid(ug)
        return dav * uv * (sg * (1.0 + ug * (1.0 - sg))), dav * ug * sg

    def body(g_ref, v_ref, u_ref, un_ref, wg_ref, wv_ref, da_ref, dan_ref, dx_ref, dw_ref, db_ref):
        i = pl.program_id(1)
        dug, duv = dact(u_ref[0], u_ref[1], da_ref[...].astype(F32))
        dan = jnp.where(i < nrow - 1, dan_ref[...].astype(F32)[:8], 0.0)
        dugn, duvn = dact(un_ref[0], un_ref[1], dan)
        dxg, dwg, dbg = _conv_grads(dug, dugn, g_ref[...], wg_ref[...], taps)
        dxv, dwv, dbv = _conv_grads(duv, duvn, v_ref[...], wv_ref[...], taps)
        dx_ref[0] = dxg.astype(BF16)
        dx_ref[1] = dxv.astype(BF16)

        @pl.when(i == 0)
        def _():
            dw_ref[0] = dwg
            dw_ref[1] = dwv
            db_ref[0] = dbg
            db_ref[1] = dbv

        @pl.when(i > 0)
        def _():
            dw_ref[0] += dwg
            dw_ref[1] += dwv
            db_ref[0] += dbg
            db_ref[1] += dbv

    both = pl.BlockSpec((2, tb, tc), lambda j, i: (0, i, j))
    both_nxt = pl.BlockSpec((2, 8, tc), lambda j, i: (0, jnp.minimum((i + 1) * (tb // 8), t // 8 - 1), j))
    da_cur = pl.BlockSpec((tb, tc), lambda j, i: (i, j))
    da_nxt = pl.BlockSpec((16, tc), lambda j, i: (jnp.minimum((i + 1) * (tb // 16), t // 16 - 1), j))
    return pl.pallas_call(
        body, name="ffn_act_bwd", grid=(D_FF // tc, nrow),
        in_specs=[cur(0), cur(1), both, both_nxt, par(taps, 0), par(taps, 1), da_cur, da_nxt],
        out_specs=[both, pl.BlockSpec((2, taps, tc), lambda j, i: (0, 0, j)),
                   pl.BlockSpec((2, 1, tc), lambda j, i: (0, 0, j))],
        out_shape=[jax.ShapeDtypeStruct((2, t, D_FF), BF16), jax.ShapeDtypeStruct((2, taps, D_FF), F32),
                   jax.ShapeDtypeStruct((2, 1, D_FF), F32)],
        compiler_params=_cp(("parallel", "arbitrary")))(u0, u0, u, u, w, w, da, da)


def _head_masks():
    lane = _iota((1, 4 * SSD_HEAD_DIM), 1)
    return [((lane >= r * SSD_HEAD_DIM) & (lane < (r + 1) * SSD_HEAD_DIM)).astype(F32) for r in range(4)]


def _segsum(v):
    first = _iota((1, LANES), 1) < SSD_HEAD_DIM
    halves = []
    for k in range(2):
        vh = v[:, k * LANES:(k + 1) * LANES]
        both = jnp.sum(vh, axis=1, keepdims=True)
        one = jnp.sum(jnp.where(first, vh, 0.0), axis=1, keepdims=True)
        halves.append(jnp.where(first, one, both - one))
    return jnp.concatenate(halves, axis=1)


def _ssd_common(raw_e, prow, rawr4, bcol, acol):
    n = SSD_CHUNK
    dt_e = _softplus(raw_e + prow[0:1, :])
    a_e = -jnp.exp(prow[1:2, :])
    d_e = prow[2:3, :]
    tril = (_iota((n, n), 0) >= _iota((n, n), 1)).astype(F32)
    acs_e = _dot_exact(tril, dt_e * a_e)
    last_e = acs_e[n - 1:n, :]
    dtr4 = _softplus(rawr4 + bcol)
    triu = (_iota((n, n), 0) <= _iota((n, n), 1)).astype(F32)
    acs_r4 = _dot_exact(dtr4 * (-jnp.exp(acol)), triu)
    return dt_e, a_e, d_e, acs_e, last_e, acs_r4


def _decay_matrix(acs_e, acs_r4, r):
    n = SSD_CHUNK
    col = acs_e[:, r * SSD_HEAD_DIM:r * SSD_HEAD_DIM + 1]
    seg = col - acs_r4[r:r + 1, :]
    causal = _iota((n, n), 0) >= _iota((n, n), 1)
    return jnp.exp(jnp.where(causal, seg, NEG))


SSD_STEP_CHUNKS = 4
SSD_ROWS = SSD_STEP_CHUNKS * SSD_CHUNK


def _ssd_specs(t, rev):
    nb = t // SSD_ROWS
    xb, bb, cb = 0, SSD_INNER // SSD_STATE, (SSD_INNER + BC_WIDTH) // SSD_STATE

    def ch(c):
        return (nb - 1 - c) if rev else c

    x = pl.BlockSpec((SSD_ROWS, 256), lambda g, c: (ch(c), xb + g))
    bm = pl.BlockSpec((SSD_ROWS, SSD_STATE), lambda g, c: (ch(c), bb + g))
    cm = pl.BlockSpec((SSD_ROWS, SSD_STATE), lambda g, c: (ch(c), cb + g))
    dtc = pl.BlockSpec((1, SSD_ROWS, 256), lambda g, c: (g, ch(c), 0))
    dtr = pl.BlockSpec((1, 4, SSD_ROWS), lambda g, c: (g, 0, ch(c)))
    prow = pl.BlockSpec((1, 3, 256), lambda g, c: (g, 0, 0))
    pcol = pl.BlockSpec((1, 4, 1), lambda g, c: (g, 0, 0))
    st = pl.BlockSpec((1, SSD_STEP_CHUNKS, SSD_STATE, 256), lambda g, c: (g, ch(c), 0, 0))
    return x, bm, cm, dtc, dtr, prow, pcol, st, ch


def _ssd_params(dt_raw, dt_bias, a_log, ssd_d):
    t = dt_raw.shape[0]
    by_group = dt_raw.reshape(t, SSD_GROUPS, 4)
    dtc = jnp.repeat(by_group, SSD_HEAD_DIM, axis=2).transpose(1, 0, 2)
    dtr = by_group.transpose(1, 2, 0)
    prow = jnp.repeat(jnp.stack([dt_bias.reshape(SSD_GROUPS, 4), a_log.reshape(SSD_GROUPS, 4),
                                 ssd_d.reshape(SSD_GROUPS, 4)], axis=1), SSD_HEAD_DIM, axis=2)
    bcol = dt_bias.reshape(SSD_GROUPS, 4, 1)
    acol = a_log.reshape(SSD_GROUPS, 4, 1)
    return dtc, dtr, prow, bcol, acol


def _ssd_fwd(xbc, params):
    t = xbc.shape[0]
    nc = t // SSD_CHUNK
    dtc, dtr, prow, bcol, acol = params

    def body(x_ref, b_ref, c_ref, dtc_ref, dtr_ref, prow_ref, bcol_ref, acol_ref, y_ref, st_ref, s_scr):
        c = pl.program_id(1)

        @pl.when(c == 0)
        def _():
            s_scr[...] = jnp.zeros_like(s_scr)

        masks = _head_masks()
        s = s_scr[...]
        for k in range(SSD_STEP_CHUNKS):
            rows = slice(k * SSD_CHUNK, (k + 1) * SSD_CHUNK)
            dt_e, a_e, d_e, acs_e, last_e, acs_r4 = _ssd_common(
                dtc_ref[0, rows], prow_ref[0], dtr_ref[0][:, rows], bcol_ref[0], acol_ref[0])
            xv = x_ref[rows]
            bm, cm = b_ref[rows], c_ref[rows]
            st_ref[0, k] = s
            xdt = xv * dt_e
            cb = _dot(cm, bm, 'nt')
            y = _dot(cm, s) * jnp.exp(acs_e) + xv * d_e
            for r in range(4):
                mr = cb * _decay_matrix(acs_e, acs_r4, r)
                y = y + _dot(mr, xdt * masks[r])
            y_ref[rows] = y
            w = xdt * jnp.exp(last_e - acs_e)
            s = s * jnp.exp(last_e) + _dot(bm.T, w)
        s_scr[...] = s

    x, bm, cm, dtcs, dtrs, prs, pcs, st, _ = _ssd_specs(t, False)
    return pl.pallas_call(
        body, name="ssd_fwd", grid=(SSD_GROUPS, t // SSD_ROWS), in_specs=[x, bm, cm, dtcs, dtrs, prs, pcs, pcs],
        out_specs=[pl.BlockSpec((SSD_ROWS, 256), lambda g, c: (c, g)), st],
        out_shape=[jax.ShapeDtypeStruct((t, SSD_INNER), F32),
                   jax.ShapeDtypeStruct((SSD_GROUPS, nc, SSD_STATE, 256), F32)],
        scratch_shapes=[pltpu.VMEM((SSD_STATE, 256), F32)],
        compiler_params=_cp(("parallel", "arbitrary")))(xbc, xbc, xbc, dtc, dtr, prow, bcol, acol)


def _ssd_bwd(xbc, pre, params, states, dy):
    t = xbc.shape[0]
    nc = t // SSD_CHUNK
    n = SSD_CHUNK
    dtc, dtr, prow, bcol, acol = params

    def body(x_ref, b_ref, c_ref, ux_ref, ub_ref, uc_ref, dtc_ref, dtr_ref, prow_ref, bcol_ref, acol_ref, st_ref,
             dy_ref, dx_ref, db_ref, dc_ref, ddt_ref, dp_ref, ds_scr):
        c = pl.program_id(1)

        @pl.when(c == 0)
        def _():
            ds_scr[...] = jnp.zeros_like(ds_scr)
            dp_ref[...] = jnp.zeros_like(dp_ref)

        masks = _head_masks()
        ds = ds_scr[...]
        for k in reversed(range(SSD_STEP_CHUNKS)):
            rows = slice(k * SSD_CHUNK, (k + 1) * SSD_CHUNK)
            raw_e = dtc_ref[0, rows]
            prw = prow_ref[0]
            dt_e, a_e, d_e, acs_e, last_e, acs_r4 = _ssd_common(raw_e, prw, dtr_ref[0][:, rows], bcol_ref[0], acol_ref[0])
            xv = x_ref[rows]
            bm, cm = b_ref[rows], c_ref[rows]
            s = st_ref[0, k]
            dyv = dy_ref[rows]
            e_e = jnp.exp(acs_e)
            dec_e = jnp.exp(last_e - acs_e)
            cd_e = jnp.exp(last_e)
            xdt = xv * dt_e
            w = xdt * dec_e
            b16, c16, s16, ds16 = bm.astype(BF16), cm.astype(BF16), s.astype(BF16), ds.astype(BF16)
            cb = _dot(c16, b16, 'nt')
            yoff_raw = _dot(c16, s16)
            dye = dyv * e_e
            dye16 = dye.astype(BF16)
            dcm = _dot(dye16, s16, 'nt')
            ds_prev = ds * cd_e + _dot(cm.T, dye16)
            dacs_e = _segsum(dyv * yoff_raw) * e_e
            dw = _dot(b16, ds16)
            dbm = _dot(w, ds16, 'nt')
            tdec = _segsum(dw * xdt) * dec_e
            dacs_e = dacs_e - tdec
            dlast_e = jnp.sum(tdec, axis=0, keepdims=True)
            dxdt = dw * dec_e
            dlast_e = dlast_e + _segsum(jnp.sum(ds * s, axis=0, keepdims=True)) * cd_e
            dcb = jnp.zeros((n, n), F32)
            for r in range(4):
                lm = _decay_matrix(acs_e, acs_r4, r)
                mr = cb * lm
                dyr16 = (dyv * masks[r]).astype(BF16)
                dm = _dot(dyr16, xdt * masks[r], 'nt')
                dcb = dcb + dm * lm
                dseg = dm * mr
                dcol = jnp.sum(dseg, axis=1, keepdims=True) - jnp.sum(dseg.T, axis=1, keepdims=True)
                dacs_e = dacs_e + dcol * masks[r]
                dxdt = dxdt + _dot(mr.T, dyr16)
            dcm = dcm + _dot(dcb, b16)
            dbm = dbm + _dot(dcb.T, c16)
            dacs_e = dacs_e + jnp.where(_iota((n, 1), 0) == n - 1, dlast_e, 0.0)
            triu = (_iota((n, n), 0) <= _iota((n, n), 1)).astype(F32)
            ddta_e = _dot_exact(triu, dacs_e)
            ddt_e = ddta_e * a_e + _segsum(dxdt * xv)
            dx_ref[rows] = (dxdt * dt_e + dyv * d_e) * _dsilu(ux_ref[rows])
            db_ref[rows] = dbm * _dsilu(ub_ref[rows])
            dc_ref[rows] = dcm * _dsilu(uc_ref[rows])
            draw_e = ddt_e * _sigmoid(raw_e + prw[0:1, :])
            draw_t = draw_e.T
            ddt_ref[0, :, rows] = jnp.concatenate([draw_t[r * SSD_HEAD_DIM:r * SSD_HEAD_DIM + 1] for r in range(4)], axis=0)
            dbias = jnp.sum(draw_e, axis=0, keepdims=True)
            dalog = jnp.sum(ddta_e * dt_e, axis=0, keepdims=True) * a_e
            dd = _segsum(jnp.sum(dyv * xv, axis=0, keepdims=True))
            row3 = _iota((3, 1), 0)
            dp_ref[0] += (jnp.where(row3 == 0, dbias, 0.0) + jnp.where(row3 == 1, dalog, 0.0)
                          + jnp.where(row3 == 2, dd, 0.0))
            ds = ds_prev
        ds_scr[...] = ds


    x, bm, cm, dtcs, dtrs, prs, pcs, st, ch = _ssd_specs(t, True)
    yblk = pl.BlockSpec((SSD_ROWS, 256), lambda g, c: (ch(c), g))
    nblk = pl.BlockSpec((SSD_ROWS, SSD_STATE), lambda g, c: (ch(c), g))
    return pl.pallas_call(
        body, name="ssd_bwd", grid=(SSD_GROUPS, t // SSD_ROWS),
        in_specs=[x, bm, cm, x, bm, cm, dtcs, dtrs, prs, pcs, pcs, st, yblk],
        out_specs=[yblk, nblk, nblk, dtrs, prs],
        out_shape=[jax.ShapeDtypeStruct((t, SSD_INNER), F32), jax.ShapeDtypeStruct((t, BC_WIDTH), F32),
                   jax.ShapeDtypeStruct((t, BC_WIDTH), F32), jax.ShapeDtypeStruct((SSD_GROUPS, 4, t), F32),
                   jax.ShapeDtypeStruct((SSD_GROUPS, 3, 256), F32)],
        scratch_shapes=[pltpu.VMEM((SSD_STATE, 256), F32)],
        compiler_params=_cp(("parallel", "arbitrary")))(xbc, xbc, xbc, pre, pre, pre, dtc, dtr, prow, bcol, acol,
                                                         states, dy)


GROUP_W = SSD_INNER // SSD_GROUPS


def _mix_specs(tb):
    row = pl.BlockSpec((tb, 2048), lambda i: (i, 0))
    zlo = pl.BlockSpec((tb, 1024), lambda i: (i, O_Z // 1024))
    zhi = pl.BlockSpec((tb, 1024), lambda i: (i, O_Z // 1024 + 1))
    vec = pl.BlockSpec((1, 2048), lambda i: (0, 0))
    return row, zlo, zhi, vec


def _mix_fwd(attn, y, proj, g_attn, g_ssd):
    t = attn.shape[0]
    tb = _rows(t, 256)

    def body(a_ref, y_ref, zlo_ref, zhi_ref, ga_ref, gs_ref, o_ref):
        av = a_ref[...]
        r = lax.rsqrt(jnp.mean(av * av, axis=-1, keepdims=True) + EPS)
        o_ref[:, :ATTN_WIDTH] = (av * r * ga_ref[...]).astype(BF16)
        for g in range(SSD_GROUPS):
            lo, hi = g * GROUP_W, (g + 1) * GROUP_W
            zref = zlo_ref if g < 4 else zhi_ref
            z = zref[:, lo % 1024:lo % 1024 + GROUP_W]
            yg = y_ref[:, lo:hi] * (z * _sigmoid(z))
            rg = lax.rsqrt(jnp.mean(yg * yg, axis=-1, keepdims=True) + EPS)
            o_ref[:, ATTN_WIDTH + lo:ATTN_WIDTH + hi] = (yg * rg * gs_ref[:, lo:hi]).astype(BF16)

    row, zlo, zhi, vec = _mix_specs(tb)
    return pl.pallas_call(
        body, name="mix_fwd", grid=(t // tb,), in_specs=[row, row, zlo, zhi, vec, vec],
        out_specs=pl.BlockSpec((tb, 4096), lambda i: (i, 0)), out_shape=jax.ShapeDtypeStruct((t, 4096), BF16),
        compiler_params=_cp(("parallel",)))(attn, y, proj, proj, g_attn, g_ssd)


def _mix_bwd(dmix, attn, y, proj, g_attn, g_ssd):
    t = attn.shape[0]
    tb = _rows(t, 256)

    def body(dm_ref, a_ref, y_ref, zlo_ref, zhi_ref, ga_ref, gs_ref, da_ref, dy_ref, dz_ref, dga_ref, dgs_ref):
        i = pl.program_id(0)
        av = a_ref[...]
        dn = dm_ref[:, :ATTN_WIDTH].astype(F32)
        r = lax.rsqrt(jnp.mean(av * av, axis=-1, keepdims=True) + EPS)
        u = dn * ga_ref[...]
        da_ref[...] = r * u - av * (r * r * r * jnp.mean(u * av, axis=-1, keepdims=True))
        dga = jnp.sum(dn * av * r, axis=0, keepdims=True)

        @pl.when(i == 0)
        def _():
            dga_ref[...] = dga

        @pl.when(i > 0)
        def _():
            dga_ref[...] += dga

        for g in range(SSD_GROUPS):
            lo, hi = g * GROUP_W, (g + 1) * GROUP_W
            zref = zlo_ref if g < 4 else zhi_ref
            z = zref[:, lo % 1024:lo % 1024 + GROUP_W]
            yv = y_ref[:, lo:hi]
            sg = _sigmoid(z)
            sz = z * sg
            yg = yv * sz
            rg = lax.rsqrt(jnp.mean(yg * yg, axis=-1, keepdims=True) + EPS)
            do = dm_ref[:, ATTN_WIDTH + lo:ATTN_WIDTH + hi].astype(F32)
            ug = do * gs_ref[:, lo:hi]
            dyg = rg * ug - yg * (rg * rg * rg * jnp.mean(ug * yg, axis=-1, keepdims=True))
            dy_ref[:, lo:hi] = dyg * sz
            dz_ref[:, lo:hi] = (dyg * yv * (sg * (1.0 + z * (1.0 - sg)))).astype(BF16)
            dgs = jnp.sum(do * yg * rg, axis=0, keepdims=True)

            @pl.when(i == 0)
            def _():
                dgs_ref[:, lo:hi] = dgs

            @pl.when(i > 0)
            def _():
                dgs_ref[:, lo:hi] += dgs

    row, zlo, zhi, vec = _mix_specs(tb)
    return pl.pallas_call(
        body, name="mix_bwd", grid=(t // tb,),
        in_specs=[pl.BlockSpec((tb, 4096), lambda i: (i, 0)), row, row, zlo, zhi, vec, vec],
        out_specs=[row, row, row, vec, vec],
        out_shape=[jax.ShapeDtypeStruct((t, 2048), F32), jax.ShapeDtypeStruct((t, 2048), F32),
                   jax.ShapeDtypeStruct((t, 2048), BF16), jax.ShapeDtypeStruct((1, 2048), F32),
                   jax.ShapeDtypeStruct((1, 2048), F32)],
        compiler_params=_cp(("arbitrary",)))(dmix, attn, y, proj, proj, g_attn, g_ssd)


def _adamw(w, g, m, v, name):
    r, c = w.shape
    tb = _rows(r, 256)
    c1 = 1.0 - ADAM_B1 ** ADAM_STEP
    c2 = 1.0 - ADAM_B2 ** ADAM_STEP

    def body(w_ref, g_ref, m_ref, v_ref, d_ref, m2_ref, v2_ref):
        gv = g_ref[...]
        m2 = ADAM_B1 * m_ref[...] + (1.0 - ADAM_B1) * gv
        v2 = ADAM_B2 * v_ref[...] + (1.0 - ADAM_B2) * (gv * gv)
        d_ref[...] = -ADAM_LR * ((m2 / c1) / (jnp.sqrt(v2 / c2) + ADAM_EPS) + ADAM_WD * w_ref[...])
        m2_ref[...] = m2
        v2_ref[...] = v2

    blk = pl.BlockSpec((tb, c), lambda i: (i, 0))
    shp = jax.ShapeDtypeStruct((r, c), F32)
    return pl.pallas_call(body, name=name, grid=(r // tb,), in_specs=[blk] * 4, out_specs=[blk] * 3,
                          out_shape=[shp] * 3, compiler_params=_cp(("parallel",)))(w, g, m, v)


def _adamw_halves(w, mine, theirs, m, v, pos, name, cols=False):
    r, c = w.shape
    h = r if cols else r // 2
    tb = _rows(h, 128)
    nh = h // tb
    c1 = 1.0 - ADAM_B1 ** ADAM_STEP
    c2 = 1.0 - ADAM_B2 ** ADAM_STEP

    def body(pos_ref, w_ref, a_ref, b_ref, m_ref, v_ref, g_ref, d_ref, m2_ref, v2_ref):
        which = pl.program_id(1) if cols else pl.program_id(0) // nh
        gv = jnp.where(which == pos_ref[0], a_ref[...], b_ref[...])
        m2 = ADAM_B1 * m_ref[...] + (1.0 - ADAM_B1) * gv
        v2 = ADAM_B2 * v_ref[...] + (1.0 - ADAM_B2) * (gv * gv)
        g_ref[...] = gv
        d_ref[...] = -ADAM_LR * ((m2 / c1) / (jnp.sqrt(v2 / c2) + ADAM_EPS) + ADAM_WD * w_ref[...])
        m2_ref[...] = m2
        v2_ref[...] = v2

    if cols:
        full = pl.BlockSpec((tb, c // 2), lambda i, j, pref: (i, j))
        mine_spec = theirs_spec = pl.BlockSpec((tb, c // 2), lambda i, j, pref: (i, 0))
        grid = (nh, 2)
    else:
        full = pl.BlockSpec((tb, c), lambda i, pref: (i, 0))
        mine_spec = pl.BlockSpec((tb, c), lambda i, pref: (jnp.where(i // nh == pref[0], i % nh,
                                                                     jnp.where(pref[0] == 0, nh - 1, 0)), 0))
        theirs_spec = pl.BlockSpec((tb, c), lambda i, pref: (jnp.where(i // nh != pref[0], i % nh,
                                                                       jnp.where(pref[0] == 0, 0, nh - 1)), 0))
        grid = (r // tb,)
    shp = jax.ShapeDtypeStruct((r, c), F32)
    grid_spec = pltpu.PrefetchScalarGridSpec(num_scalar_prefetch=1, grid=grid,
                                             in_specs=[full, mine_spec, theirs_spec, full, full],
                                             out_specs=[full] * 4)
    return pl.pallas_call(body, name=name, grid_spec=grid_spec, out_shape=[shp] * 4,
                          compiler_params=_cp(("parallel",) * len(grid)))(pos, w, mine, theirs, m, v)


def _sum_own_half(g4, recv, pos, name, cols=False):
    _, r, c = g4.shape
    h, c = (r, c // 2) if cols else (r // 2, c)
    tb = _rows(h, 128)
    nh = h // tb

    def slot(j, pref):
        return (pref[1] + 1 + j) % N_CHIPS

    if cols:
        own = lambda j, i, pref: (slot(j, pref), i, pref[0])
    else:
        own = lambda j, i, pref: (slot(j, pref), pref[0] * nh + i, 0)
    same = lambda j, i, pref: (slot(j, pref), i, 0)

    def body(pos_ref, a_ref, b_ref, o_ref):
        o_ref[...] = (a_ref[...] + b_ref[...]).astype(BF16)

    grid_spec = pltpu.PrefetchScalarGridSpec(
        num_scalar_prefetch=1, grid=(N_CHIPS - 1, nh),
        in_specs=[pl.BlockSpec((1, tb, c), own), pl.BlockSpec((1, tb, c), same)],
        out_specs=pl.BlockSpec((1, tb, c), same))
    return pl.pallas_call(body, name=name, grid_spec=grid_spec,
                          out_shape=jax.ShapeDtypeStruct((N_CHIPS, h, c), BF16),
                          compiler_params=_cp(("parallel", "parallel")))(pos, g4, recv)


def _sum_chips(g4, recv, parts, pos, name, cols=False):
    _, r, c = g4.shape
    h, c = (r, c // 2) if cols else (r // 2, c)
    tb = _rows(h, 128)
    nh = h // tb
    own = (lambda i, pref: (pref[1], i, pref[0])) if cols else (lambda i, pref: (pref[1], pref[0] * nh + i, 0))

    def body(pos_ref, a_ref, b_ref, p_ref, o_ref):
        own = a_ref[0] + b_ref[0]
        o_ref[...] = ((own + p_ref[0].astype(F32)) + p_ref[1].astype(F32)) + p_ref[2].astype(F32)

    grid_spec = pltpu.PrefetchScalarGridSpec(
        num_scalar_prefetch=1, grid=(nh,),
        in_specs=[pl.BlockSpec((1, tb, c), own),
                  pl.BlockSpec((1, tb, c), lambda i, pref: (pref[1], i, 0)),
                  pl.BlockSpec((3, tb, c), lambda i, pref: (0, i, 0))],
        out_specs=pl.BlockSpec((tb, c), lambda i, pref: (i, 0)))
    return pl.pallas_call(body, name=name, grid_spec=grid_spec, out_shape=jax.ShapeDtypeStruct((h, c), F32),
                          compiler_params=_cp(("parallel",)))(pos, g4, recv, parts)


def _me():
    return lax.axis_index("x"), lax.axis_index("y"), lax.axis_index("c")


def _flip(v, bit):
    return (1 - v) if bit else v


CHIP_FLIPS = [(1, 0), (0, 1), (1, 1)]


def _forward_halves(gathered):
    def body(g_ref, o_ref, token, send_sems, recv_sems):
        x, y, c = _me()
        h = g_ref.shape[2] // 2
        cps = []
        for k, (fx, fy) in enumerate(CHIP_FLIPS):
            peer_chip = 2 * _flip(x, fx) + _flip(y, fy)
            mine = o_ref.at[peer_chip, :, pl.ds(c * h, h)]
            cp = pltpu.make_async_remote_copy(src_ref=mine, dst_ref=mine, send_sem=send_sems.at[k],
                                              recv_sem=recv_sems.at[k], device_id=(x, y, 1 - c), device_id_type=MESH)
            cp.start()
            cps.append(cp)
        for k, (fx, fy) in enumerate(CHIP_FLIPS):
            peer_chip = 2 * _flip(x, fx) + _flip(y, fy)
            theirs = o_ref.at[peer_chip, :, pl.ds((1 - c) * h, h)]
            pltpu.make_async_remote_copy(src_ref=theirs, dst_ref=theirs, send_sem=send_sems.at[k],
                                         recv_sem=recv_sems.at[k], device_id=(x, y, 1 - c),
                                         device_id_type=MESH).wait_recv()
        for cp in cps:
            cp.wait_send()
        token[...] = jnp.zeros_like(token)

    return pl.pallas_call(
        body, name="gather_forward_w_in", in_specs=[HBM_SPEC],
        out_specs=[HBM_SPEC, pl.BlockSpec(memory_space=pltpu.VMEM)],
        out_shape=[jax.ShapeDtypeStruct(gathered.shape, gathered.dtype), TOKEN],
        scratch_shapes=[pltpu.SemaphoreType.DMA((3,)), pltpu.SemaphoreType.DMA((3,))],
        input_output_aliases={0: 0},
        compiler_params=pltpu.CompilerParams(has_side_effects=True))(gathered)


SEM_SPEC = pl.BlockSpec(memory_space=pltpu.SEMAPHORE)
ANY_SPEC = pl.BlockSpec(memory_space=pl.ANY)
DATAFLOW = pltpu.SideEffectType.DATAFLOW_SIDE_EFFECTING


def _in_hbm(a):
    return pltpu.with_memory_space_constraint(a, pltpu.HBM)


def _push_start(srcs, land_shapes, route, peers, name):
    n, npeer = len(srcs), len(peers)
    lands = [lax.empty(shp, s.dtype) for shp, s in zip(land_shapes, srcs)]

    def body(*refs):
        ins, lnd = refs[:n], refs[n:2 * n]
        send_sems, recv_sems = refs[2 * n], refs[2 * n + 1]
        token = refs[-1]
        x, y, c = _me()
        for t in range(n):
            for k, (fx, fy, fc) in enumerate(peers):
                src, dst = route(ins[t], lnd[t], k, x, y, c)
                pltpu.make_async_remote_copy(
                    src_ref=src, dst_ref=dst, send_sem=send_sems.at[npeer * t + k],
                    recv_sem=recv_sems.at[npeer * t + k],
                    device_id=(_flip(x, fx), _flip(y, fy), _flip(c, fc)), device_id_type=MESH).start()
        token[...] = jnp.zeros_like(token)

    bufs = [_in_hbm(a) for a in list(srcs) + lands]
    outs = pl.pallas_call(
        body, name=name,
        out_shape=(pltpu.SemaphoreType.DMA((npeer * n,)), pltpu.SemaphoreType.DMA((npeer * n,)),
                   *[pltpu.HBM(b.shape, b.dtype) for b in bufs], TOKEN),
        in_specs=[HBM_SPEC] * (2 * n),
        out_specs=(SEM_SPEC, SEM_SPEC, *[HBM_SPEC] * (2 * n), pl.BlockSpec(memory_space=pltpu.VMEM)),
        input_output_aliases={i: 2 + i for i in range(2 * n)},
        compiler_params=pltpu.CompilerParams(has_side_effects=DATAFLOW))(*bufs)
    return outs[0], outs[1], list(outs[2:2 + n]), list(outs[2 + n:2 + 2 * n]), outs[-1]


def _push_wait(send_sems, recv_sems, srcs, lands, after, route, peers, name):
    n, npeer = len(srcs), len(peers)

    def body(*refs):
        ins, lnd = refs[:n], refs[n:2 * n]
        ssem, rsem = refs[2 * n], refs[2 * n + 1]
        x, y, c = _me()
        for t in range(n):
            for k, (fx, fy, fc) in enumerate(peers):
                src, dst = route(ins[t], lnd[t], k, x, y, c)
                cp = pltpu.make_async_remote_copy(
                    src_ref=src, dst_ref=dst, send_sem=ssem.at[npeer * t + k], recv_sem=rsem.at[npeer * t + k],
                    device_id=(_flip(x, fx), _flip(y, fy), _flip(c, fc)), device_id_type=MESH)
                cp.wait_send()
                cp.wait_recv()

    bufs = list(srcs) + list(lands)
    outs = pl.pallas_call(
        body, name=name, out_shape=tuple(pltpu.HBM(b.shape, b.dtype) for b in bufs),
        in_specs=[HBM_SPEC] * (2 * n) + [SEM_SPEC, SEM_SPEC, ANY_SPEC], out_specs=tuple([HBM_SPEC] * (2 * n)),
        input_output_aliases={i: i for i in range(2 * n)},
        compiler_params=pltpu.CompilerParams(has_side_effects=DATAFLOW))(*bufs, send_sems, recv_sems, after)
    return list(outs[:n]), list(outs[n:])


OTHER_CHIPS = [(fx, fy, 0) for fx, fy in CHIP_FLIPS]
SIBLING = [(0, 0, 1)]


def _route_gather(src, land, k, x, y, c):
    return src, land.at[2 * x + y]


def _route_gather_half(src, land, k, x, y, c):
    h = src.shape[1] // 2
    return src.at[:, pl.ds(c * h, h)], land.at[2 * x + y, :, pl.ds(c * h, h)]


def _route_gather_half_wait(src, land, k, x, y, c):
    fx, fy = CHIP_FLIPS[k]
    h = src.shape[1] // 2
    return src.at[:, pl.ds(c * h, h)], land.at[2 * _flip(x, fx) + _flip(y, fy), :, pl.ds(c * h, h)]


def _route_gather_wait(src, land, k, x, y, c):
    fx, fy = CHIP_FLIPS[k]
    return src, land.at[2 * _flip(x, fx) + _flip(y, fy)]


def _route_scatter(src, land, k, x, y, c):
    fx, fy = CHIP_FLIPS[k]
    return src.at[2 * _flip(x, fx) + _flip(y, fy)], land.at[k]


def _route_exchange(src, land, k, x, y, c):
    h = land.shape[1]
    return src.at[:, pl.ds((1 - c) * h, h)], land


def _route_whole(src, land, k, x, y, c):
    return src, land


def _route_exchange_cols(src, land, k, x, y, c):
    h = land.shape[2]
    return src.at[:, :, pl.ds((1 - c) * h, h)], land


ALL_OTHERS = [((k >> 2) & 1, (k >> 1) & 1, k & 1) for k in range(1, 8)]


def _route_to_all(src, land, k, x, y, c):
    return src, land.at[4 * x + 2 * y + c]


def _route_to_all_wait(src, land, k, x, y, c):
    fx, fy, fc = ALL_OTHERS[k]
    return src, land.at[4 * _flip(x, fx) + 2 * _flip(y, fy) + _flip(c, fc)]


def _sum_devices(parts):
    def body(p_ref, o_ref):
        acc = p_ref[0]
        for d in range(1, 8):
            acc = acc + p_ref[d]
        o_ref[...] = acc

    vm = pl.BlockSpec(memory_space=pltpu.VMEM)
    return pl.pallas_call(body, name="allreduce_sum", in_specs=[vm], out_specs=vm,
                          out_shape=jax.ShapeDtypeStruct(parts.shape[1:], F32),
                          compiler_params=pltpu.CompilerParams(vmem_limit_bytes=VMEM_LIMIT))(parts)


def _grad_exchange_start(g4, tag, cols=False):
    land = (N_CHIPS, g4.shape[1], g4.shape[2] // 2) if cols else (N_CHIPS, g4.shape[1] // 2, g4.shape[2])
    route = _route_exchange_cols if cols else _route_exchange
    send_sems, recv_sems, srcs, lands, token = _push_start(
        [g4], [land], route, SIBLING, name="grad_exchange_start_" + tag)
    return (send_sems, recv_sems, srcs, lands, tag, cols), token


def _grad_scatter_start(state, pos, after):
    send_sems, recv_sems, srcs, lands, tag, cols = state
    route = _route_exchange_cols if cols else _route_exchange
    (g4,), (recv,) = _push_wait(send_sems, recv_sems, srcs, lands, after, route, SIBLING,
                                name="grad_exchange_wait_" + tag)
    return _grad_pair_scatter(g4, recv, pos, tag, cols)


def _grad_pair_scatter(g4, recv, pos, tag, cols=False):
    p16 = _sum_own_half(g4, recv, pos, name="grad_sum_pair_" + tag, cols=cols)
    send_sems, recv_sems, srcs, lands, token = _push_start(
        [p16], [(3,) + p16.shape[1:]], _route_scatter, OTHER_CHIPS, name="grad_scatter_start_" + tag)
    return (g4, recv, send_sems, recv_sems, srcs, lands, tag, cols), token


def _grad_sum_and_share(state, pos, after):
    g4, recv, send_sems, recv_sems, srcs, lands, tag, cols = state
    parts = _push_wait(send_sems, recv_sems, srcs, lands, after, _route_scatter, OTHER_CHIPS,
                       name="grad_scatter_wait_" + tag)[1][0]
    mine = _sum_chips(g4, recv, parts, pos, name="grad_sum_chips_" + tag, cols=cols)
    send_sems, recv_sems, srcs, lands, token = _push_start(
        [mine], [mine.shape], _route_whole, SIBLING, name="grad_share_start_" + tag)
    return (send_sems, recv_sems, srcs, lands, tag), token


def _grad_share_wait(state, after):
    send_sems, recv_sems, srcs, lands, tag = state
    (mine,), (theirs,) = _push_wait(send_sems, recv_sems, srcs, lands, after, _route_whole, SIBLING,
                                    name="grad_share_wait_" + tag)
    return mine, theirs


def _local_step(x, tgt, p, hooks):
    t = x.shape[0]
    tables = _rope_tables(t)
    sinks = p['sinks'].reshape(N_Q_HEADS)

    def told(name, value):
        return tuple(hooks.grad_ready(name, value))

    xn = _rmsnorm_fwd(x, p['norm_mix'], "norm_mix_fwd", deps=hooks.first_deps)
    w_in_t, w_in_dt, in_deps = hooks.weight_in(xn)
    proj = _matmul(xn, w_in_t, mode='nt', name="in_proj", n_limit=MAIN_WIDTH, deps=in_deps)
    dt_raw = _matmul(xn, w_in_dt, mode='nt', name="in_proj_dt")[:, :SSD_HEADS]
    ssd_conv_w, ffn_conv_w = hooks.conv_weights(proj)
    p = dict(p, ssd_conv_w=ssd_conv_w, ffn_conv_w=ffn_conv_w)
    attn = _attn_fwd(proj, sinks, tables)
    conv_b = p['ssd_conv_b']
    xbc, xbc_pre = _conv_silu_fwd(proj, p['ssd_conv_w'], conv_b, col0=O_XBC, width=CONV_CH, name="ssd_conv_fwd")
    sp = _ssd_params(dt_raw, p['dt_bias'].reshape(-1), p['a_log'].reshape(-1), p['ssd_d'].reshape(-1))
    y, states = _ssd_fwd(xbc, sp)
    mix = _mix_fwd(attn, y, proj, p['attn_out_norm'], p['ssd_norm'])
    w_out = hooks.weight('w_out', mix)
    h1 = _matmul(mix, w_out, mode='nn', name="out_proj", add=x)
    hn = _rmsnorm_fwd(h1, p['norm_ffn'], "norm_ffn_fwd")
    w_up = hooks.weight('w_up', hn)
    u0 = _matmul(hn, w_up, mode='nn', name="ffn_up", b_owner=True, tn=1408)
    a, u = _ffn_act_fwd(u0, p['ffn_conv_w'], p['ffn_conv_b'])
    w_down = hooks.weight('w_down', a)
    h2 = _matmul(a, w_down, mode='nn', name="ffn_down", add=h1, tk=2816)
    loss, dh2, dh2_16, g_norm_final = _final_loss(h2, p['norm_final'].reshape(1, D_MODEL), tgt)

    g = {}
    da = _matmul(dh2_16, w_down, mode='nt', name="ffn_down_dx", out_dtype=BF16, tn=1408)
    g['w_down'] = _matmul(a, dh2_16, mode='tn', name="ffn_down_dw", tm=1408)
    dep = told('w_down', g['w_down'])
    du0, dcw, dcb = _ffn_act_bwd(u0, u, p['ffn_conv_w'], da)
    g['ffn_conv_w'] = dcw.transpose(1, 0, 2).reshape(FFN_CONV, 2 * D_FF)
    g['ffn_conv_b'] = dcb.transpose(1, 0, 2).reshape(1, 2 * D_FF)
    g['w_up'] = _matmul(hn, du0, mode='tn', name="ffn_up_dw", deps=dep, b_halves=True, owner_major=True,
                        tn=1408)
    dep = told('w_up', g['w_up'])
    dhn = _matmul(du0, w_up, mode='nt', name="ffn_up_dx", out_dtype=BF16, deps=dep, a_halves=True,
                  b_owner=True, tk=2816)
    dh1, dh1_16, g['norm_ffn'] = _rmsnorm_bwd(h1, p['norm_ffn'], dhn, dh2, "norm_ffn_bwd")

    g['w_out'] = _matmul(mix, dh1_16, mode='tn', name="out_proj_dw")
    dep = told('w_out', g['w_out'])
    dmix = _matmul(dh1_16, w_out, mode='nt', name="out_proj_dx", out_dtype=BF16, deps=dep)
    dattn, dy, dz, g['attn_out_norm'], g['ssd_norm'] = _mix_bwd(dmix, attn, y, proj, p['attn_out_norm'],
                                                                p['ssd_norm'])
    dq, dk, dv, dsink = _attn_bwd(proj, sinks, tables, dattn)
    g['sinks'] = dsink[:, :, 0].reshape(1, N_Q_HEADS)
    dxs, dbm, dcm, ddt8, dpar = _ssd_bwd(xbc, xbc_pre, sp, states, dy)
    dpar = dpar[:, :, ::SSD_HEAD_DIM]
    g['dt_bias'] = dpar[:, 0, :].reshape(1, SSD_HEADS)
    g['a_log'] = dpar[:, 1, :].reshape(1, SSD_HEADS)
    g['ssd_d'] = dpar[:, 2, :].reshape(1, SSD_HEADS)
    dxbc, g['ssd_conv_w'], g['ssd_conv_b'] = _ssd_conv_bwd(proj, p['ssd_conv_w'], dxs, dbm, dcm, col0=O_XBC,
                                                           name="ssd_conv_bwd")
    dproj = jnp.concatenate([dq, dk, dv, dz, dxbc], axis=1)
    ddt = ddt8.transpose(2, 0, 1).reshape(t, SSD_HEADS)
    ddt_pad = jnp.pad(ddt, ((0, 0), (0, LANES - SSD_HEADS))).astype(BF16)
    g['w_in'] = (_matmul(dproj, xn, mode='tn', name="in_proj_dw", m_rows=IN_PROJ_WIDTH),
                 _matmul(ddt_pad, xn, mode='tn', name="in_proj_dt_dw"))
    dep = told('w_in', g['w_in'])
    dxn_dt = _matmul(ddt_pad, w_in_dt, mode='nn', name="in_proj_dt_dx", deps=dep)
    dxn = _matmul(dproj, w_in_t, mode='nn', name="in_proj_dx", out_dtype=BF16, add=dxn_dt, k_limit=MAIN_WIDTH,
                  tk=2304)
    dep = told(None, dxn)
    dx, _, g['norm_mix'] = _rmsnorm_bwd(x, p['norm_mix'], dxn, dh1, "norm_mix_bwd", deps=dep)
    g['norm_final'] = g_norm_final
    return loss, dx, g


def _pack(arrs):
    flat = jnp.concatenate([a.reshape(-1) for a in arrs])
    n = flat.shape[0]
    rows = -(-n // LANES)
    rows = -(-rows // 8) * 8
    return jnp.pad(flat, (0, rows * LANES - n)).reshape(rows, LANES)


def _unpack(packed, shapes):
    flat = packed.reshape(-1)
    out, off = [], 0
    for s in shapes:
        n = 1
        for d in s:
            n *= d
        out.append(flat[off:off + n].reshape(s))
        off += n
    return out


class _StepHooks:
    def __init__(self, first_deps, weight_in, conv_weights, weight, grad_ready):
        self.first_deps = first_deps
        self.weight_in = weight_in
        self.conv_weights = conv_weights
        self.weight = weight
        self.grad_ready = grad_ready


def kernel(x, norm_mix, w_in, sinks, attn_out_norm, ssd_conv_w, ssd_conv_b, dt_bias, a_log, ssd_d, ssd_norm, w_out, norm_ffn, w_up, ffn_conv_w, ffn_conv_b, w_down, norm_final, loss_target, m_norm_mix, m_w_in, m_sinks, m_attn_out_norm, m_ssd_conv_w, m_ssd_conv_b, m_dt_bias, m_a_log, m_ssd_d, m_ssd_norm, m_w_out, m_norm_ffn, m_w_up, m_ffn_conv_w, m_ffn_conv_b, m_w_down, m_norm_final, v_norm_mix, v_w_in, v_sinks, v_attn_out_norm, v_ssd_conv_w, v_ssd_conv_b, v_dt_bias, v_a_log, v_ssd_d, v_ssd_norm, v_w_out, v_norm_ffn, v_w_up, v_ffn_conv_w, v_ffn_conv_b, v_w_down, v_norm_final):
    args = dict(locals())
    w = {n: args[n] for n in WEIGHTS}
    m = {n: args['m_' + n] for n in WEIGHTS}
    v = {n: args['v_' + n] for n in WEIGHTS}
    xi, yi, ci = _me()
    chip = 2 * xi + yi
    pos = jnp.stack([ci, chip]).astype(jnp.int32)

    conv_shard = _pack([ssd_conv_w[0], ffn_conv_w[0]])
    conv_gather = _push_start([conv_shard], [(N_CHIPS,) + conv_shard.shape], _route_gather, OTHER_CHIPS,
                              name="gather_start_conv")

    def conv_weights(after):
        send_sems, recv_sems, srcs, lands, _ = conv_gather
        (own,), (got,) = _push_wait(send_sems, recv_sems, srcs, lands, after, _route_gather_wait, OTHER_CHIPS,
                                    name="gather_wait_conv")
        whole = lax.dynamic_update_slice(got, own[None], (chip, 0, 0))
        per_chip = [_unpack(whole[j], [ssd_conv_w.shape[1:], ffn_conv_w.shape[1:]]) for j in range(N_CHIPS)]
        return (jnp.concatenate([pc[0] for pc in per_chip], axis=1),
                jnp.concatenate([pc[1] for pc in per_chip], axis=1))

    w_in_t, m_in_t, v_in_t = (jnp.transpose(a[0]) for a in (w_in, m_w_in, v_w_in))
    in_shard = (w_in_t + conv_gather[4][:1, :1]).astype(BF16)
    in_gather = _push_start([in_shard], [(N_CHIPS,) + in_shard.shape], _route_gather_half, OTHER_CHIPS,
                            name="gather_start_w_in")
    gathers = {}
    order = in_gather[4][:1, :1]
    for n, shard in (('w_out', w_out[0]), ('w_up', w_up[0]), ('w_down', w_down[0])):
        shard = (shard + order).astype(BF16)
        gathers[n] = _push_start([shard], [(N_CHIPS,) + shard.shape], _route_gather, OTHER_CHIPS,
                                 name="gather_start_" + n)
        order = gathers[n][4][:1, :1]

    def weight_in(after):
        send_sems, recv_sems, srcs, lands, _ = in_gather
        (own,), (got,) = _push_wait(send_sems, recv_sems, srcs, lands, after, _route_gather_half_wait, OTHER_CHIPS,
                                    name="gather_wait_w_in")
        got, _ = _forward_halves(got)
        full_in_t = lax.dynamic_update_slice(got, own[None], (chip, 0, 0)).reshape(IN_PROJ_WIDTH, D_MODEL)
        w_in_dt = jnp.pad(full_in_t[MAIN_WIDTH:], ((0, LANES - SSD_HEADS), (0, 0)))
        return full_in_t, w_in_dt, ()

    def weight(name, after):
        send_sems, recv_sems, srcs, lands, _ = gathers[name]
        (own,), (got,) = _push_wait(send_sems, recv_sems, srcs, lands, after, _route_gather_wait, OTHER_CHIPS,
                                    name="gather_wait_" + name)
        whole = lax.dynamic_update_slice(got, own[None], (chip, 0, 0))
        return whole if name == 'w_up' else whole.reshape(-1, D_MODEL)

    reductions, exchanging = {}, {}

    def flush(after):
        tokens = []
        for prev in list(exchanging):
            reductions[prev], token = _grad_scatter_start(exchanging.pop(prev), pos, after)
            tokens.append(token)
        return tokens

    def grad_ready(name, value):
        if name is None:
            return flush(value)
        if name == 'w_in':
            main, dtp = value
            value = lax.dynamic_update_slice(main, dtp[:SSD_HEADS], (MAIN_WIDTH, 0))
        g4 = value if value.ndim == 3 else value.reshape(N_CHIPS, -1, value.shape[1])
        tokens = flush(g4)
        exchanging[name], token = _grad_exchange_start(g4, name, cols=(name == 'w_in'))
        return tokens + [token]

    small = {
        'norm_mix': norm_mix, 'sinks': sinks, 'attn_out_norm': attn_out_norm,
        'ssd_conv_b': ssd_conv_b, 'dt_bias': dt_bias, 'a_log': a_log, 'ssd_d': ssd_d, 'ssd_norm': ssd_norm,
        'norm_ffn': norm_ffn, 'ffn_conv_b': ffn_conv_b, 'norm_final': norm_final,
    }
    loss, dx, g = _local_step(x[0], loss_target[0], small,
                              _StepHooks((gathers['w_down'][4],), weight_in, conv_weights, weight, grad_ready))

    small_names = [n for n in WEIGHTS if n not in BIG]
    small_g = [loss[:, :1]] + [g[n] for n in small_names]
    small_shapes = [(1, 1)] + [tuple(a.shape) for a in small_g[1:]]
    packed = _pack(small_g)
    spread = _push_start([packed], [(8,) + packed.shape], _route_to_all, ALL_OTHERS, name="allreduce_start")
    grads, deltas, new_m, new_v = {}, {}, {}, {}
    after = spread[4]
    shares = {}
    for n in ('w_down', 'w_up', 'w_out'):
        shares[n], after = _grad_sum_and_share(reductions[n], pos, after)
    for n in ('w_down', 'w_up', 'w_out', 'w_in'):
        if n == 'w_out':
            shares['w_in'], after = _grad_sum_and_share(reductions['w_in'], pos, after)
        mine, theirs = _grad_share_wait(shares[n], after)
        if n == 'w_in':
            outs = _adamw_halves(w_in_t, mine, theirs, m_in_t, v_in_t, pos, name="adamw_" + n, cols=True)
            after = outs[1]
            outs = [jnp.transpose(o) for o in outs]
        else:
            outs = _adamw_halves(w[n][0], mine, theirs, m[n][0], v[n][0], pos, name="adamw_" + n)
            after = outs[1]
        grads[n], deltas[n], new_m[n], new_v[n] = [o[None] for o in outs]
    (own,), (landed,) = _push_wait(spread[0], spread[1], spread[2], spread[3], after, _route_to_all_wait, ALL_OTHERS,
                                   name="allreduce_wait")
    landed = lax.dynamic_update_slice(landed, own[None], (4 * xi + 2 * yi + ci, 0, 0))
    red = _unpack(_sum_devices(landed), small_shapes)
    loss_out = red[0].reshape(())
    gsm = dict(zip(small_names, red[1:]))
    gsm['ssd_conv_w'] = lax.dynamic_slice(gsm['ssd_conv_w'], (0, chip * ssd_conv_w.shape[2]),
                                          (SSD_CONV, ssd_conv_w.shape[2]))
    gsm['ffn_conv_w'] = lax.dynamic_slice(gsm['ffn_conv_w'], (0, chip * ffn_conv_w.shape[2]),
                                          (FFN_CONV, ffn_conv_w.shape[2]))

    shapes = [tuple(w[n].shape) for n in small_names]
    gp = _pack([gsm[n] for n in small_names])
    d, m2, v2 = _adamw(_pack([w[n] for n in small_names]), gp, _pack([m[n] for n in small_names]),
                       _pack([v[n] for n in small_names]), name="adamw_small")
    for n, gg, dd, mm, vv in zip(small_names, _unpack(gp, shapes), _unpack(d, shapes), _unpack(m2, shapes),
                                 _unpack(v2, shapes)):
        grads[n], deltas[n], new_m[n], new_v[n] = gg, dd, mm, vv

    return (loss_out, dx[None], *[grads[n] for n in WEIGHTS], *[deltas[n] for n in WEIGHTS],
            *[new_m[n] for n in WEIGHTS], *[new_v[n] for n in WEIGHTS])
```

```python
import functools

import jax
import jax.numpy as jnp
from jax import lax
from jax.experimental import pallas as pl
from jax.experimental.pallas import tpu as pltpu

F32 = jnp.float32
BF16 = jnp.bfloat16

D_MODEL = 2048
N_Q_HEADS = 32
N_KV_HEADS = 8
HEAD_DIM = 64
WINDOW = 128
ATTN_BLOCK = 128
ROT_DIM = 16
ROPE_THETA = 500000.0
SSD_HEADS = 32
SSD_HEAD_DIM = 64
SSD_INNER = 2048
SSD_GROUPS = 8
SSD_STATE = 128
SSD_CONV = 4
SSD_CHUNK = 128
ATTN_WIDTH = 2048
KV_WIDTH = 512
BC_WIDTH = 1024
CONV_CH = 4096
IN_PROJ_WIDTH = 9248
MAIN_WIDTH = 9216
D_FF = 5632
FFN_CONV = 3
EPS = 1e-6
O_Q, O_K, O_V, O_Z, O_XBC, O_DT = 0, 2048, 2560, 3072, 5120, 9216

ADAM_LR = 0.001
ADAM_B1 = 0.9
ADAM_B2 = 0.999
ADAM_EPS = 1e-08
ADAM_WD = 0.01
ADAM_STEP = 10

N_CHIPS = 4
NEG = -1e30
LANES = 128
VMEM_LIMIT = 48 * 1024 * 1024
MESH = pl.DeviceIdType.MESH
HBM_SPEC = pl.BlockSpec(memory_space=pltpu.HBM)
TOKEN = jax.ShapeDtypeStruct((8, LANES), F32)

WEIGHTS = ['norm_mix', 'w_in', 'sinks', 'attn_out_norm', 'ssd_conv_w', 'ssd_conv_b', 'dt_bias', 'a_log', 'ssd_d',
           'ssd_norm', 'w_out', 'norm_ffn', 'w_up', 'ffn_conv_w', 'ffn_conv_b', 'w_down', 'norm_final']
BIG = ['w_in', 'w_out', 'w_up', 'w_down']


def _cp(sem=None, vmem=VMEM_LIMIT):
    kw = {'vmem_limit_bytes': vmem}
    if sem is not None:
        kw['dimension_semantics'] = sem
    return pltpu.CompilerParams(**kw)


def _tile(n, pref):
    if n <= pref:
        return n
    t = (pref // LANES) * LANES
    while t > LANES and n % t:
        t -= LANES
    assert n % t == 0, (n, pref)
    return t


def _rows(n, pref):
    t = min(n, pref)
    while n % t:
        t -= 8
    if 4 * t < pref:
        t = pref
        while n % t:
            t += 8
    return t


def _iota(shape, dim):
    return lax.broadcasted_iota(jnp.int32, shape, dim)


def _dot(a, b, mode='nn'):
    dn = {'nn': (((1,), (0,)), ((), ())), 'nt': (((1,), (1,)), ((), ())), 'tn': (((0,), (0,)), ((), ()))}[mode]
    return lax.dot_general(a.astype(BF16), b.astype(BF16), dn, preferred_element_type=F32)


def _dot_exact(a, b):
    return lax.dot_general(a, b, (((1,), (0,)), ((), ())), precision=lax.Precision.HIGHEST,
                           preferred_element_type=F32)


def _sigmoid(x):
    return 1.0 / (1.0 + jnp.exp(-x))


def _softplus(x):
    return jnp.maximum(x, 0.0) + jnp.log(1.0 + jnp.exp(-jnp.abs(x)))


def _matmul(a, b, *, mode, name, out_dtype=F32, add=None, deps=(), tm=1024, tn=1024, tk=2048,
            a_halves=False, b_halves=False, b_owner=False, owner_major=False, n_limit=None, k_limit=None,
            m_rows=None):
    ash, bsh = (a.shape[1:] if a_halves else a.shape), (b.shape[1:] if (b_halves or b_owner) else b.shape)
    if mode == 'nn':
        (m, k), (k2, n) = ash, bsh
    elif mode == 'nt':
        (m, k), (n, k2) = ash, bsh
    else:
        (k, m), (k2, n) = ash, bsh
    if n_limit is not None:
        assert mode == 'nt' and n_limit <= n
        n = n_limit
    if k_limit is not None:
        assert mode == 'nn' and k_limit <= k2
        k2 = k_limit
    if a_halves:
        assert mode == 'nt'
        k = 2 * k
    if b_halves:
        assert mode == 'tn'
        n = 2 * n
    if b_owner:
        assert mode in ('nn', 'nt')
        if mode == 'nn':
            n = 4 * n
        else:
            k2 = 4 * k2
    assert k == k2, (a.shape, b.shape, mode)
    tm = _tile(m, tm)
    tn = _tile(n // 4 if (owner_major or (b_owner and mode == 'nn')) else (n // 2 if b_halves else n), tn)
    tk = _tile(k // 4 if (b_owner and mode == 'nt') else (k // 2 if a_halves else k), tk)
    nk = k // tk
    has_add = add is not None
    assert not (has_add and owner_major)

    def body(*refs):
        a_ref, b_ref = refs[:2]
        add_ref = refs[2] if has_add else None

        def finish(r, o_ref):
            if has_add:
                r = r + add_ref[...].astype(F32)
            o_ref[...] = r.astype(out_dtype)

        if nk == 1:
            finish(_dot(a_ref[...], b_ref[...], mode), refs[-1])
            return
        o_ref, acc = refs[-2:]
        kk = pl.program_id(2)

        @pl.when(kk == 0)
        def _():
            acc[...] = _dot(a_ref[...], b_ref[...], mode)

        @pl.when((kk > 0) & (kk < nk - 1))
        def _():
            acc[...] += _dot(a_ref[...], b_ref[...], mode)

        @pl.when(kk == nk - 1)
        def _():
            finish(acc[...] + _dot(a_ref[...], b_ref[...], mode), o_ref)

    if mode == 'tn':
        a_spec = pl.BlockSpec((tk, tm), lambda i, j, kk: (kk, i))
    elif a_halves:
        nkh = nk // 2
        a_spec = pl.BlockSpec((None, tm, tk), lambda i, j, kk: (kk // nkh, i, kk % nkh))
    else:
        a_spec = pl.BlockSpec((tm, tk), lambda i, j, kk: (i, kk))
    if mode == 'nt' and b_owner:
        nkq = nk // 4
        b_spec = pl.BlockSpec((None, tn, tk), lambda i, j, kk: (kk // nkq, j, kk % nkq))
    elif mode == 'nt':
        b_spec = pl.BlockSpec((tn, tk), lambda i, j, kk: (j, kk))
    elif b_owner:
        njq = (n // 4) // tn
        b_spec = pl.BlockSpec((None, tk, tn), lambda i, j, kk: (j // njq, kk, j % njq))
    elif b_halves:
        njh = (n // 2) // tn
        b_spec = pl.BlockSpec((None, tk, tn), lambda i, j, kk: (j // njh, kk, j % njh))
    else:
        b_spec = pl.BlockSpec((tk, tn), lambda i, j, kk: (kk, j))
    if owner_major:
        njo = (n // 4) // tn
        o_spec = pl.BlockSpec((None, tm, tn), lambda i, j, kk: (j // njo, i, j % njo))
        out_shape = jax.ShapeDtypeStruct((N_CHIPS, m, n // 4), out_dtype)
    else:
        o_spec = pl.BlockSpec((tm, tn), lambda i, j, kk: (i, j))
        out_shape = jax.ShapeDtypeStruct((m if m_rows is None else m_rows, n), out_dtype)
    dep_spec = pl.BlockSpec((8, LANES), lambda i, j, kk: (0, 0))
    in_specs = [a_spec, b_spec] + ([pl.BlockSpec((tm, tn), lambda i, j, kk: (i, j))] if has_add else [])
    in_specs += [dep_spec] * len(deps)
    args = (a, b) + ((add,) if has_add else ()) + tuple(deps)
    return pl.pallas_call(
        body, name=name, grid=(m // tm, n // tn, nk), in_specs=in_specs, out_specs=o_spec, out_shape=out_shape,
        scratch_shapes=[pltpu.VMEM((tm, tn), F32)] if nk > 1 else [],
        compiler_params=_cp(("parallel", "parallel", "arbitrary")))(*args)


def _rmsnorm_fwd(x, g, name, deps=()):
    t, d = x.shape
    tb = _rows(t, 256)

    def body(x_ref, g_ref, *rest):
        o_ref = rest[-1]
        xv = x_ref[...]
        r = lax.rsqrt(jnp.mean(xv * xv, axis=-1, keepdims=True) + EPS)
        o_ref[...] = (xv * r * g_ref[...]).astype(BF16)

    dep_spec = pl.BlockSpec((8, LANES), lambda i: (0, 0))
    return pl.pallas_call(
        body, name=name, grid=(t // tb,),
        in_specs=[pl.BlockSpec((tb, d), lambda i: (i, 0)), pl.BlockSpec((1, d), lambda i: (0, 0))]
        + [dep_spec] * len(deps),
        out_specs=pl.BlockSpec((tb, d), lambda i: (i, 0)), out_shape=jax.ShapeDtypeStruct((t, d), BF16),
        compiler_params=_cp(("parallel",)))(x, g, *deps)


def _rmsnorm_bwd(x, g, dy, res, name, deps=()):
    t, d = x.shape
    tb = _rows(t, 256)

    def body(x_ref, g_ref, dy_ref, res_ref, *rest):
        dx_ref, dx16_ref, dg_ref = rest[-3:]
        i = pl.program_id(0)
        xv = x_ref[...]
        dyv = dy_ref[...].astype(F32)
        r = lax.rsqrt(jnp.mean(xv * xv, axis=-1, keepdims=True) + EPS)
        u = dyv * g_ref[...]
        dx = r * u - xv * (r * r * r * jnp.mean(u * xv, axis=-1, keepdims=True)) + res_ref[...]
        dx_ref[...] = dx
        dx16_ref[...] = dx.astype(BF16)
        part = jnp.sum(dyv * xv * r, axis=0, keepdims=True)

        @pl.when(i == 0)
        def _():
            dg_ref[...] = part

        @pl.when(i > 0)
        def _():
            dg_ref[...] += part

    row = pl.BlockSpec((tb, d), lambda i: (i, 0))
    vec = pl.BlockSpec((1, d), lambda i: (0, 0))
    return pl.pallas_call(
        body, name=name, grid=(t // tb,),
        in_specs=[row, vec, row, row] + [pl.BlockSpec((8, LANES), lambda i: (0, 0))] * len(deps),
        out_specs=[row, row, vec],
        out_shape=[jax.ShapeDtypeStruct((t, d), F32), jax.ShapeDtypeStruct((t, d), BF16),
                   jax.ShapeDtypeStruct((1, d), F32)],
        compiler_params=_cp(("arbitrary",)))(x, g, dy, res, *deps)


def _final_loss(h, g, tgt):
    t, d = h.shape
    tb = _rows(t, 256)

    def body(h_ref, g_ref, t_ref, loss_ref, dh_ref, dh16_ref, dg_ref):
        i = pl.program_id(0)
        hv = h_ref[...]
        gv = g_ref[...]
        r = lax.rsqrt(jnp.mean(hv * hv, axis=-1, keepdims=True) + EPS)
        y = hv * r * gv
        diff = y - t_ref[...]
        lpart = jnp.sum(jnp.sum(diff * diff, axis=1, keepdims=True), axis=0, keepdims=True) * (0.5 / d)
        dy = diff * (1.0 / d)
        u = dy * gv
        dh = r * u - hv * (r * r * r * jnp.mean(u * hv, axis=-1, keepdims=True))
        dh_ref[...] = dh
        dh16_ref[...] = dh.astype(BF16)
        gpart = jnp.sum(dy * hv * r, axis=0, keepdims=True)
        lrow = jnp.broadcast_to(lpart, (1, LANES))

        @pl.when(i == 0)
        def _():
            loss_ref[...] = lrow
            dg_ref[...] = gpart

        @pl.when(i > 0)
        def _():
            loss_ref[...] += lrow
            dg_ref[...] += gpart

    row = pl.BlockSpec((tb, d), lambda i: (i, 0))
    vec = pl.BlockSpec((1, d), lambda i: (0, 0))
    return pl.pallas_call(
        body, name="final_loss", grid=(t // tb,), in_specs=[row, vec, row],
        out_specs=[pl.BlockSpec((1, LANES), lambda i: (0, 0)), row, row, vec],
        out_shape=[jax.ShapeDtypeStruct((1, LANES), F32), jax.ShapeDtypeStruct((t, d), F32),
                   jax.ShapeDtypeStruct((t, d), BF16), jax.ShapeDtypeStruct((1, d), F32)],
        compiler_params=_cp(("arbitrary",)))(h, g, tgt)


def _rope_tables(t):
    pos = jnp.arange(t, dtype=F32)
    inv = 1.0 / (ROPE_THETA ** (jnp.arange(0, ROT_DIM, 2, dtype=F32) / ROT_DIM))
    ang = pos[:, None] * inv[None, :]
    cos, sin = jnp.cos(ang), jnp.sin(ang)
    half = ROT_DIM // 2
    rest = HEAD_DIM - ROT_DIM
    c = jnp.concatenate([cos, cos, jnp.ones((t, rest), F32)], axis=1)
    s1 = jnp.concatenate([-sin, jnp.zeros((t, half + rest), F32)], axis=1)
    s2 = jnp.concatenate([jnp.zeros((t, half), F32), sin, jnp.zeros((t, rest), F32)], axis=1)
    return jnp.concatenate([jnp.tile(v, (1, LANES // HEAD_DIM)) for v in (c, s1, s2)], axis=1)


def _split_tables(tab):
    return tab[:, :LANES], tab[:, LANES:2 * LANES], tab[:, 2 * LANES:]


def _rope(x, c, s1, s2):
    half = ROT_DIM // 2
    return x * c + pltpu.roll(x, LANES - half, 1) * s1 + pltpu.roll(x, half, 1) * s2


def _rope_t(g, c, s1, s2):
    half = ROT_DIM // 2
    return g * c + pltpu.roll(g * s1, half, 1) + pltpu.roll(g * s2, LANES - half, 1)


def _band_masks(i, heads):
    n = heads * ATTN_BLOCK
    q = jnp.bitwise_and(_iota((n, ATTN_BLOCK), 0), ATTN_BLOCK - 1)
    j = _iota((n, ATTN_BLOCK), 1)
    upper = j > q
    return upper, upper & (j < jnp.where(i > 0, 0, ATTN_BLOCK))


def _fold_band(full, upper):
    return jnp.where(upper, full[:, :ATTN_BLOCK], full[:, ATTN_BLOCK:])


def _unfold_band(band, upper):
    return jnp.concatenate([jnp.where(upper, band, 0.0), jnp.where(upper, 0.0, band)], axis=1)


def _half_masks():
    lane = _iota((1, LANES), 1)
    return [(lane < HEAD_DIM).astype(F32), (lane >= HEAD_DIM).astype(F32)]


def _stack_heads(blocks, hm, j):
    pieces = []
    for r in range(4):
        qb, half = (4 * j + r) // 2, (4 * j + r) % 2
        piece = blocks[qb] * hm[half]
        if half != j:
            piece = pltpu.roll(piece, HEAD_DIM, 1)
        pieces.append(piece)
    return jnp.concatenate(pieces, axis=0)


def _unstack_heads(stacked, j):
    out = []
    for qb in (2 * j, 2 * j + 1):
        acc = None
        for half in range(2):
            r = 2 * qb + half - 4 * j
            piece = stacked[r * ATTN_BLOCK:(r + 1) * ATTN_BLOCK]
            if half != j:
                piece = pltpu.roll(piece, HEAD_DIM, 1)
            acc = piece if acc is None else acc + piece
        out.append((qb, acc))
    return out


def _sink_column(sink_ref, base):
    return jnp.concatenate([jnp.full((ATTN_BLOCK, 1), sink_ref[base + r], F32) for r in range(4)], axis=0)


def _attn_specs(nb_clamp):
    blk = ATTN_BLOCK
    kb, vb = O_K // LANES, O_V // LANES

    def cur(i):
        return jnp.minimum(i, nb_clamp)

    def prev(i):
        return jnp.maximum(jnp.minimum(i, nb_clamp + 1) - 1, 0)

    q = pl.BlockSpec((blk, 512), lambda p, i: (cur(i), p))
    kc = pl.BlockSpec((blk, LANES), lambda p, i: (cur(i), kb + p))
    kp = pl.BlockSpec((blk, LANES), lambda p, i: (prev(i), kb + p))
    vc = pl.BlockSpec((blk, LANES), lambda p, i: (cur(i), vb + p))
    vp = pl.BlockSpec((blk, LANES), lambda p, i: (prev(i), vb + p))
    tc = pl.BlockSpec((blk, 3 * LANES), lambda p, i: (cur(i), 0))
    tp = pl.BlockSpec((blk, 3 * LANES), lambda p, i: (prev(i), 0))
    return q, kc, kp, vc, vp, tc, tp


def _attn_fwd(proj, sinks, tables):
    t = proj.shape[0]
    nb = t // ATTN_BLOCK
    scale = HEAD_DIM ** -0.5

    def body(sink_ref, q_ref, kc_ref, kp_ref, vc_ref, vp_ref, tc_ref, tp_ref, o_ref):
        p = pl.program_id(0)
        i = pl.program_id(1)
        cc, s1c, s2c = _split_tables(tc_ref[...])
        kband = jnp.concatenate([_rope(kp_ref[...], *_split_tables(tp_ref[...])),
                                 _rope(kc_ref[...], cc, s1c, s2c)], axis=0).astype(BF16)
        vband = jnp.concatenate([vp_ref[...], vc_ref[...]], axis=0)
        hm = _half_masks()
        vsel = [(vband * hm[j]).astype(BF16) for j in range(2)]
        upper, dropped = _band_masks(i, 1)
        qr = [_rope(q_ref[:, qb * LANES:(qb + 1) * LANES], cc, s1c, s2c) for qb in range(4)]

        def scores(hh):
            qb, half, j = hh // 2, hh % 2, hh // 4
            qs = qr[qb] * hm[half]
            if half != j:
                qs = pltpu.roll(qs, HEAD_DIM, 1)
            return _dot(qs, kband, 'nt')

        ahead = scores(0)
        acc = None
        for hh in range(8):
            qb, half, j = hh // 2, hh % 2, hh // 4
            raw = ahead
            if hh + 1 < 8:
                ahead = scores(hh + 1)
            s = jnp.where(dropped, NEG, _fold_band(raw, upper) * scale)
            sink = sink_ref[p * 8 + hh]
            m = jnp.maximum(jnp.max(s, axis=1, keepdims=True), sink)
            pe = jnp.exp(s - m)
            den = jnp.sum(pe, axis=1, keepdims=True) + jnp.exp(sink - m)
            o = _dot(_unfold_band(pe / den, upper), vsel[j])
            if half != j:
                o = pltpu.roll(o, HEAD_DIM, 1)
            acc = o if half == 0 else acc + o
            if half == 1:
                o_ref[:, qb * LANES:(qb + 1) * LANES] = acc

    q, kc, kp, vc, vp, tc, tp = _attn_specs(nb - 1)
    smem = pl.BlockSpec(memory_space=pltpu.SMEM)
    return pl.pallas_call(
        body, name="attn_fwd", grid=(4, nb),
        in_specs=[smem, q, kc, kp, vc, vp, tc, tp],
        out_specs=pl.BlockSpec((ATTN_BLOCK, 512), lambda p, i: (i, p)),
        out_shape=jax.ShapeDtypeStruct((t, ATTN_WIDTH), F32),
        compiler_params=_cp(("parallel", "arbitrary")))(sinks, proj, proj, proj, proj, proj, tables, tables)


def _attn_bwd(proj, sinks, tables, dout):
    t = proj.shape[0]
    nb = t // ATTN_BLOCK
    scale = HEAD_DIM ** -0.5

    def body(sink_ref, q_ref, kc_ref, kp_ref, vc_ref, vp_ref, tc_ref, tp_ref,
             do_ref, dq_ref, dk_ref, dv_ref, ds_ref, carry_k, carry_v):
        p = pl.program_id(0)
        i = pl.program_id(1)
        ptab = _split_tables(tp_ref[...])

        @pl.when(i == 0)
        def _():
            carry_k[...] = jnp.zeros_like(carry_k)
            carry_v[...] = jnp.zeros_like(carry_v)
            ds_ref[...] = jnp.zeros_like(ds_ref)

        @pl.when(i < nb)
        def _():
            cc, s1c, s2c = _split_tables(tc_ref[...])
            kband = jnp.concatenate([_rope(kp_ref[...], *ptab), _rope(kc_ref[...], cc, s1c, s2c)], axis=0)
            vband = jnp.concatenate([vp_ref[...], vc_ref[...]], axis=0)
            hm = _half_masks()
            kband16 = kband.astype(BF16)
            vband16 = vband.astype(BF16)
            upper, dropped = _band_masks(i, 4)
            dkb = jnp.zeros((2 * ATTN_BLOCK, LANES), F32)
            dvb = jnp.zeros((2 * ATTN_BLOCK, LANES), F32)
            row8 = _iota((8, LANES), 0)
            dsink = jnp.zeros((8, LANES), F32)
            qr = [_rope(q_ref[:, qb * LANES:(qb + 1) * LANES], cc, s1c, s2c) for qb in range(4)]
            dob = [do_ref[:, qb * LANES:(qb + 1) * LANES] for qb in range(4)]
            for j in range(2):
                qst = _stack_heads(qr, hm, j).astype(BF16)
                dost = _stack_heads(dob, hm, j).astype(BF16)
                s = jnp.where(dropped, NEG, _fold_band(_dot(qst, kband16, 'nt'), upper) * scale)
                sink = _sink_column(sink_ref, p * 8 + 4 * j)
                m = jnp.maximum(jnp.max(s, axis=1, keepdims=True), sink)
                pe = jnp.exp(s - m)
                psink = jnp.exp(sink - m)
                den = jnp.sum(pe, axis=1, keepdims=True) + psink
                pr = pe / den
                dvb = dvb + _dot(_unfold_band(pr, upper).T, dost)
                dp = _fold_band(_dot(dost, vband16, 'nt'), upper)
                delta = jnp.sum(pr * dp, axis=1, keepdims=True)
                dsc = _unfold_band(pr * (dp - delta) * scale, upper)
                dsk = psink / den * delta
                for r in range(4):
                    part = jnp.sum(dsk[r * ATTN_BLOCK:(r + 1) * ATTN_BLOCK])
                    dsink = dsink + jnp.where(row8 == 4 * j + r, -part, 0.0)
                for qb, dqb in _unstack_heads(_dot(dsc, kband * hm[j]), j):
                    dq_ref[:, qb * LANES:(qb + 1) * LANES] = _rope_t(dqb, cc, s1c, s2c).astype(BF16)
                dkb = dkb + _dot(dsc.T, qst)
            ds_ref[0] += dsink
            dk_ref[...] = _rope_t(carry_k[...] + dkb[:ATTN_BLOCK], *ptab).astype(BF16)
            dv_ref[...] = (carry_v[...] + dvb[:ATTN_BLOCK]).astype(BF16)
            carry_k[...] = dkb[ATTN_BLOCK:]
            carry_v[...] = dvb[ATTN_BLOCK:]

        @pl.when(i == nb)
        def _():
            dk_ref[...] = _rope_t(carry_k[...], *ptab).astype(BF16)
            dv_ref[...] = carry_v[...].astype(BF16)

    q, kc, kp, vc, vp, tc, tp = _attn_specs(nb - 1)
    smem = pl.BlockSpec(memory_space=pltpu.SMEM)
    qblk = pl.BlockSpec((ATTN_BLOCK, 512), lambda p, i: (jnp.minimum(i, nb - 1), p))
    kvout = pl.BlockSpec((ATTN_BLOCK, LANES), lambda p, i: (jnp.maximum(i - 1, 0), p))
    return pl.pallas_call(
        body, name="attn_bwd", grid=(4, nb + 1),
        in_specs=[smem, q, kc, kp, vc, vp, tc, tp, qblk],
        out_specs=[qblk, kvout, kvout, pl.BlockSpec((1, 8, LANES), lambda p, i: (p, 0, 0))],
        out_shape=[jax.ShapeDtypeStruct((t, ATTN_WIDTH), BF16), jax.ShapeDtypeStruct((t, KV_WIDTH), BF16),
                   jax.ShapeDtypeStruct((t, KV_WIDTH), BF16), jax.ShapeDtypeStruct((4, 8, LANES), F32)],
        scratch_shapes=[pltpu.VMEM((ATTN_BLOCK, LANES), F32), pltpu.VMEM((ATTN_BLOCK, LANES), F32)],
        compiler_params=_cp(("parallel", "arbitrary")))(sinks, proj, proj, proj, proj, proj, tables, tables, dout)


def _shift_rows(x, prev8, j):
    n, c = x.shape
    r = pltpu.roll(x.reshape(n // 8, 8, c), j, 1)
    before = pltpu.roll(prev8, j, 0)[None]
    if n > 8:
        before = jnp.concatenate([before, r[:-1]], axis=0)
    return jnp.where(_iota((1, 8, 1), 1) < j, before, r).reshape(n, c)


def _shift_rows_up(x, next8, j):
    n, c = x.shape
    r = pltpu.roll(x.reshape(n // 8, 8, c), 8 - j, 1)
    after = pltpu.roll(next8, 8 - j, 0)[None]
    if n > 8:
        after = jnp.concatenate([r[1:], after], axis=0)
    return jnp.where(_iota((1, 8, 1), 1) >= 8 - j, after, r).reshape(n, c)


def _conv_apply(x, prev8, w, b, taps):
    u = b + x * w[taps - 1:taps]
    for j in range(1, taps):
        u = u + _shift_rows(x, prev8, j) * w[taps - 1 - j:taps - j]
    return u


def _conv_grads(du, du_next8, x, w, taps):
    dx = du * w[taps - 1:taps]
    rowk = _iota((taps, 1), 0)
    dw = jnp.where(rowk == taps - 1, jnp.sum(du * x, axis=0, keepdims=True), 0.0)
    for j in range(1, taps):
        ahead = _shift_rows_up(du, du_next8, j)
        dx = dx + ahead * w[taps - 1 - j:taps - j]
        dw = dw + jnp.where(rowk == taps - 1 - j, jnp.sum(ahead * x, axis=0, keepdims=True), 0.0)
    return dx, dw, jnp.sum(du, axis=0, keepdims=True)


def _conv_specs(tb, tc, col0, t):
    c0 = col0 // tc
    cur = pl.BlockSpec((tb, tc), lambda j, i: (i, c0 + j))
    prev = pl.BlockSpec((8, tc), lambda j, i: (jnp.maximum(i * (tb // 8) - 1, 0), c0 + j))
    nxt = pl.BlockSpec((8, tc), lambda j, i: (jnp.minimum((i + 1) * (tb // 8), t // 8 - 1), c0 + j))
    return cur, prev, nxt


def _conv_silu_fwd(x, w, b, *, col0, width, name):
    t = x.shape[0]
    taps = w.shape[0]
    tb, tc = _rows(t, 512), _tile(width, 1024)
    assert col0 % tc == 0

    def body(x_ref, xp_ref, w_ref, b_ref, o_ref, u_ref):
        i = pl.program_id(1)
        prev8 = jnp.where(i > 0, xp_ref[...], 0.0)
        u = _conv_apply(x_ref[...], prev8, w_ref[...], b_ref[...], taps)
        u_ref[...] = u
        o_ref[...] = u * _sigmoid(u)

    cur, prev, _ = _conv_specs(tb, tc, col0, t)
    par = pl.BlockSpec((taps, tc), lambda j, i: (0, j))
    bias = pl.BlockSpec((1, tc), lambda j, i: (0, j))
    out = pl.BlockSpec((tb, tc), lambda j, i: (i, j))
    shp = jax.ShapeDtypeStruct((t, width), F32)
    return pl.pallas_call(
        body, name=name, grid=(width // tc, t // tb), in_specs=[cur, prev, par, bias], out_specs=[out, out],
        out_shape=[shp, shp], compiler_params=_cp(("parallel", "parallel")))(x, x, w, b)


def _dsilu(u):
    sg = _sigmoid(u)
    return sg * (1.0 + u * (1.0 - sg))


def _ssd_conv_bwd(x, w, dxs, dbm, dcm, *, col0, name):
    t = x.shape[0]
    taps = w.shape[0]
    tb, tc = _rows(t, 512), BC_WIDTH
    nrow, ncol = t // tb, CONV_CH // tc
    c0 = col0 // tc

    def body(x_ref, w_ref, xs_ref, xsn_ref, bm_ref, bmn_ref, cm_ref, cmn_ref, dx_ref, dw_ref, db_ref):
        i = pl.program_id(0)
        j = pl.program_id(1)

        def run(du_ref, dun_ref):
            next8 = jnp.where(i < nrow - 1, dun_ref[...], 0.0)
            dx, dwv, dbv = _conv_grads(du_ref[...], next8, x_ref[...], w_ref[...], taps)
            dx_ref[...] = dx.astype(BF16)

            @pl.when(i == 0)
            def _():
                dw_ref[j] = dwv
                db_ref[j] = dbv

            @pl.when(i > 0)
            def _():
                dw_ref[j] += dwv
                db_ref[j] += dbv

        pl.when(j < 2)(lambda: run(xs_ref, xsn_ref))
        pl.when(j == 2)(lambda: run(bm_ref, bmn_ref))
        pl.when(j == 3)(lambda: run(cm_ref, cmn_ref))

    def nxt_row(i):
        return jnp.minimum((i + 1) * (tb // 8), t // 8 - 1)

    xs_col = lambda j: jnp.minimum(j, SSD_INNER // tc - 1)
    in_specs = [pl.BlockSpec((tb, tc), lambda i, j: (i, c0 + j)), pl.BlockSpec((taps, tc), lambda i, j: (0, j)),
                pl.BlockSpec((tb, tc), lambda i, j: (i, xs_col(j))),
                pl.BlockSpec((8, tc), lambda i, j: (nxt_row(i), xs_col(j))),
                pl.BlockSpec((tb, tc), lambda i, j: (i, 0)), pl.BlockSpec((8, tc), lambda i, j: (nxt_row(i), 0)),
                pl.BlockSpec((tb, tc), lambda i, j: (i, 0)), pl.BlockSpec((8, tc), lambda i, j: (nxt_row(i), 0))]
    dx, dw, db = pl.pallas_call(
        body, name=name, grid=(nrow, ncol), in_specs=in_specs,
        out_specs=[pl.BlockSpec((tb, tc), lambda i, j: (i, j)),
                   pl.BlockSpec((ncol, taps, tc), lambda i, j: (0, 0, 0)),
                   pl.BlockSpec((ncol, 1, tc), lambda i, j: (0, 0, 0))],
        out_shape=[jax.ShapeDtypeStruct((t, CONV_CH), BF16), jax.ShapeDtypeStruct((ncol, taps, tc), F32),
                   jax.ShapeDtypeStruct((ncol, 1, tc), F32)],
        compiler_params=_cp(("arbitrary", "arbitrary")))(x, w, dxs, dxs, dbm, dbm, dcm, dcm)
    return dx, dw.transpose(1, 0, 2).reshape(taps, CONV_CH), db.transpose(1, 0, 2).reshape(1, CONV_CH)


def _ffn_specs(tb, tc, t):
    nc = D_FF // tc

    def cur(half):
        return pl.BlockSpec((tb, tc), lambda j, i: (i, half * nc + j))

    def prev(half):
        return pl.BlockSpec((8, tc), lambda j, i: (jnp.maximum(i * (tb // 8) - 1, 0), half * nc + j))

    def nxt(half):
        return pl.BlockSpec((8, tc), lambda j, i: (jnp.minimum((i + 1) * (tb // 8), t // 8 - 1), half * nc + j))

    def par(rows, half):
        return pl.BlockSpec((rows, tc), lambda j, i: (0, half * nc + j))

    return cur, prev, nxt, par


def _ffn_act_fwd(u0, w, b):
    t = u0.shape[0]
    tb, tc = _rows(t, 512), _tile(D_FF, 1408)
    cur, prev, _, par = _ffn_specs(tb, tc, t)

    def body(g_ref, gp_ref, v_ref, vp_ref, wg_ref, wv_ref, bg_ref, bv_ref, o_ref, u_ref):
        i = pl.program_id(1)
        ug = _conv_apply(g_ref[...], jnp.where(i > 0, gp_ref[...], 0.0), wg_ref[...], bg_ref[...], FFN_CONV)
        uv = _conv_apply(v_ref[...], jnp.where(i > 0, vp_ref[...], 0.0), wv_ref[...], bv_ref[...], FFN_CONV)
        o_ref[...] = (ug * _sigmoid(ug) * uv).astype(BF16)
        u_ref[0] = ug
        u_ref[1] = uv

    return pl.pallas_call(
        body, name="ffn_act_fwd", grid=(D_FF // tc, t // tb),
        in_specs=[cur(0), prev(0), cur(1), prev(1), par(FFN_CONV, 0), par(FFN_CONV, 1), par(1, 0), par(1, 1)],
        out_specs=[pl.BlockSpec((tb, tc), lambda j, i: (i, j)), pl.BlockSpec((2, tb, tc), lambda j, i: (0, i, j))],
        out_shape=[jax.ShapeDtypeStruct((t, D_FF), BF16), jax.ShapeDtypeStruct((2, t, D_FF), F32)],
        compiler_params=_cp(("parallel", "parallel")))(u0, u0, u0, u0, w, w, b, b)


def _ffn_act_bwd(u0, u, w, da):
    t = u0.shape[0]
    tb, tc = _rows(t, 256), _tile(D_FF, 1408)
    nrow = t // tb
    taps = FFN_CONV
    cur, _, _, par = _ffn_specs(tb, tc, t)

    def dact(ug, uv, dav):
        sg = _sigmoid(ug)
        return dav * uv * (sg * (1.0 + ug * (1.0 - sg))), dav * ug * sg

    def body(g_ref, v_ref, u_ref, un_ref, wg_ref, wv_ref, da_ref, dan_ref, dx_ref, dw_ref, db_ref):
        i = pl.program_id(1)
        dug, duv = dact(u_ref[0], u_ref[1], da_ref[...].astype(F32))
        dan = jnp.where(i < nrow - 1, dan_ref[...].astype(F32)[:8], 0.0)
        dugn, duvn = dact(un_ref[0], un_ref[1], dan)
        dxg, dwg, dbg = _conv_grads(dug, dugn, g_ref[...], wg_ref[...], taps)
        dxv, dwv, dbv = _conv_grads(duv, duvn, v_ref[...], wv_ref[...], taps)
        dx_ref[0] = dxg.astype(BF16)
        dx_ref[1] = dxv.astype(BF16)

        @pl.when(i == 0)
        def _():
            dw_ref[0] = dwg
            dw_ref[1] = dwv
            db_ref[0] = dbg
            db_ref[1] = dbv

        @pl.when(i > 0)
        def _():
            dw_ref[0] += dwg
            dw_ref[1] += dwv
            db_ref[0] += dbg
            db_ref[1] += dbv

    both = pl.BlockSpec((2, tb, tc), lambda j, i: (0, i, j))
    both_nxt = pl.BlockSpec((2, 8, tc), lambda j, i: (0, jnp.minimum((i + 1) * (tb // 8), t // 8 - 1), j))
    da_cur = pl.BlockSpec((tb, tc), lambda j, i: (i, j))
    da_nxt = pl.BlockSpec((16, tc), lambda j, i: (jnp.minimum((i + 1) * (tb // 16), t // 16 - 1), j))
    return pl.pallas_call(
        body, name="ffn_act_bwd", grid=(D_FF // tc, nrow),
        in_specs=[cur(0), cur(1), both, both_nxt, par(taps, 0), par(taps, 1), da_cur, da_nxt],
        out_specs=[both, pl.BlockSpec((2, taps, tc), lambda j, i: (0, 0, j)),
                   pl.BlockSpec((2, 1, tc), lambda j, i: (0, 0, j))],
        out_shape=[jax.ShapeDtypeStruct((2, t, D_FF), BF16), jax.ShapeDtypeStruct((2, taps, D_FF), F32),
                   jax.ShapeDtypeStruct((2, 1, D_FF), F32)],
        compiler_params=_cp(("parallel", "arbitrary")))(u0, u0, u, u, w, w, da, da)


def _head_masks():
    lane = _iota((1, 4 * SSD_HEAD_DIM), 1)
    return [((lane >= r * SSD_HEAD_DIM) & (lane < (r + 1) * SSD_HEAD_DIM)).astype(F32) for r in range(4)]


def _segsum(v):
    first = _iota((1, LANES), 1) < SSD_HEAD_DIM
    halves = []
    for k in range(2):
        vh = v[:, k * LANES:(k + 1) * LANES]
        both = jnp.sum(vh, axis=1, keepdims=True)
        one = jnp.sum(jnp.where(first, vh, 0.0), axis=1, keepdims=True)
        halves.append(jnp.where(first, one, both - one))
    return jnp.concatenate(halves, axis=1)


def _ssd_common(raw_e, prow, rawr4, bcol, acol):
    n = SSD_CHUNK
    dt_e = _softplus(raw_e + prow[0:1, :])
    a_e = -jnp.exp(prow[1:2, :])
    d_e = prow[2:3, :]
    tril = (_iota((n, n), 0) >= _iota((n, n), 1)).astype(F32)
    acs_e = _dot_exact(tril, dt_e * a_e)
    last_e = acs_e[n - 1:n, :]
    dtr4 = _softplus(rawr4 + bcol)
    triu = (_iota((n, n), 0) <= _iota((n, n), 1)).astype(F32)
    acs_r4 = _dot_exact(dtr4 * (-jnp.exp(acol)), triu)
    return dt_e, a_e, d_e, acs_e, last_e, acs_r4


def _decay_matrix(acs_e, acs_r4, r):
    n = SSD_CHUNK
    col = acs_e[:, r * SSD_HEAD_DIM:r * SSD_HEAD_DIM + 1]
    seg = col - acs_r4[r:r + 1, :]
    causal = _iota((n, n), 0) >= _iota((n, n), 1)
    return jnp.exp(jnp.where(causal, seg, NEG))


SSD_STEP_CHUNKS = 4
SSD_ROWS = SSD_STEP_CHUNKS * SSD_CHUNK


def _ssd_specs(t, rev):
    nb = t // SSD_ROWS
    xb, bb, cb = 0, SSD_INNER // SSD_STATE, (SSD_INNER + BC_WIDTH) // SSD_STATE

    def ch(c):
        return (nb - 1 - c) if rev else c

    x = pl.BlockSpec((SSD_ROWS, 256), lambda g, c: (ch(c), xb + g))
    bm = pl.BlockSpec((SSD_ROWS, SSD_STATE), lambda g, c: (ch(c), bb + g))
    cm = pl.BlockSpec((SSD_ROWS, SSD_STATE), lambda g, c: (ch(c), cb + g))
    dtc = pl.BlockSpec((1, SSD_ROWS, 256), lambda g, c: (g, ch(c), 0))
    dtr = pl.BlockSpec((1, 4, SSD_ROWS), lambda g, c: (g, 0, ch(c)))
    prow = pl.BlockSpec((1, 3, 256), lambda g, c: (g, 0, 0))
    pcol = pl.BlockSpec((1, 4, 1), lambda g, c: (g, 0, 0))
    st = pl.BlockSpec((1, SSD_STEP_CHUNKS, SSD_STATE, 256), lambda g, c: (g, ch(c), 0, 0))
    return x, bm, cm, dtc, dtr, prow, pcol, st, ch


def _ssd_params(dt_raw, dt_bias, a_log, ssd_d):
    t = dt_raw.shape[0]
    by_group = dt_raw.reshape(t, SSD_GROUPS, 4)
    dtc = jnp.repeat(by_group, SSD_HEAD_DIM, axis=2).transpose(1, 0, 2)
    dtr = by_group.transpose(1, 2, 0)
    prow = jnp.repeat(jnp.stack([dt_bias.reshape(SSD_GROUPS, 4), a_log.reshape(SSD_GROUPS, 4),
                                 ssd_d.reshape(SSD_GROUPS, 4)], axis=1), SSD_HEAD_DIM, axis=2)
    bcol = dt_bias.reshape(SSD_GROUPS, 4, 1)
    acol = a_log.reshape(SSD_GROUPS, 4, 1)
    return dtc, dtr, prow, bcol, acol


def _ssd_fwd(xbc, params):
    t = xbc.shape[0]
    nc = t // SSD_CHUNK
    dtc, dtr, prow, bcol, acol = params

    def body(x_ref, b_ref, c_ref, dtc_ref, dtr_ref, prow_ref, bcol_ref, acol_ref, y_ref, st_ref, s_scr):
        c = pl.program_id(1)

        @pl.when(c == 0)
        def _():
            s_scr[...] = jnp.zeros_like(s_scr)

        masks = _head_masks()
        s = s_scr[...]
        for k in range(SSD_STEP_CHUNKS):
            rows = slice(k * SSD_CHUNK, (k + 1) * SSD_CHUNK)
            dt_e, a_e, d_e, acs_e, last_e, acs_r4 = _ssd_common(
                dtc_ref[0, rows], prow_ref[0], dtr_ref[0][:, rows], bcol_ref[0], acol_ref[0])
            xv = x_ref[rows]
            bm, cm = b_ref[rows], c_ref[rows]
            st_ref[0, k] = s
            xdt = xv * dt_e
            cb = _dot(cm, bm, 'nt')
            y = _dot(cm, s) * jnp.exp(acs_e) + xv * d_e
            for r in range(4):
                mr = cb * _decay_matrix(acs_e, acs_r4, r)
                y = y + _dot(mr, xdt * masks[r])
            y_ref[rows] = y
            w = xdt * jnp.exp(last_e - acs_e)
            s = s * jnp.exp(last_e) + _dot(bm.T, w)
        s_scr[...] = s

    x, bm, cm, dtcs, dtrs, prs, pcs, st, _ = _ssd_specs(t, False)
    return pl.pallas_call(
        body, name="ssd_fwd", grid=(SSD_GROUPS, t // SSD_ROWS), in_specs=[x, bm, cm, dtcs, dtrs, prs, pcs, pcs],
        out_specs=[pl.BlockSpec((SSD_ROWS, 256), lambda g, c: (c, g)), st],
        out_shape=[jax.ShapeDtypeStruct((t, SSD_INNER), F32),
                   jax.ShapeDtypeStruct((SSD_GROUPS, nc, SSD_STATE, 256), F32)],
        scratch_shapes=[pltpu.VMEM((SSD_STATE, 256), F32)],
        compiler_params=_cp(("parallel", "arbitrary")))(xbc, xbc, xbc, dtc, dtr, prow, bcol, acol)


def _ssd_bwd(xbc, pre, params, states, dy):
    t = xbc.shape[0]
    nc = t // SSD_CHUNK
    n = SSD_CHUNK
    dtc, dtr, prow, bcol, acol = params

    def body(x_ref, b_ref, c_ref, ux_ref, ub_ref, uc_ref, dtc_ref, dtr_ref, prow_ref, bcol_ref, acol_ref, st_ref,
             dy_ref, dx_ref, db_ref, dc_ref, ddt_ref, dp_ref, ds_scr):
        c = pl.program_id(1)

        @pl.when(c == 0)
        def _():
            ds_scr[...] = jnp.zeros_like(ds_scr)
            dp_ref[...] = jnp.zeros_like(dp_ref)

        masks = _head_masks()
        ds = ds_scr[...]
        for k in reversed(range(SSD_STEP_CHUNKS)):
            rows = slice(k * SSD_CHUNK, (k + 1) * SSD_CHUNK)
            raw_e = dtc_ref[0, rows]
            prw = prow_ref[0]
            dt_e, a_e, d_e, acs_e, last_e, acs_r4 = _ssd_common(raw_e, prw, dtr_ref[0][:, rows], bcol_ref[0], acol_ref[0])
            xv = x_ref[rows]
            bm, cm = b_ref[rows], c_ref[rows]
            s = st_ref[0, k]
            dyv = dy_ref[rows]
            e_e = jnp.exp(acs_e)
            dec_e = jnp.exp(last_e - acs_e)
            cd_e = jnp.exp(last_e)
            xdt = xv * dt_e
            w = xdt * dec_e
            b16, c16, s16, ds16 = bm.astype(BF16), cm.astype(BF16), s.astype(BF16), ds.astype(BF16)
            cb = _dot(c16, b16, 'nt')
            yoff_raw = _dot(c16, s16)
            dye = dyv * e_e
            dye16 = dye.astype(BF16)
            dcm = _dot(dye16, s16, 'nt')
            ds_prev = ds * cd_e + _dot(cm.T, dye16)
            dacs_e = _segsum(dyv * yoff_raw) * e_e
            dw = _dot(b16, ds16)
            dbm = _dot(w, ds16, 'nt')
            tdec = _segsum(dw * xdt) * dec_e
            dacs_e = dacs_e - tdec
            dlast_e = jnp.sum(tdec, axis=0, keepdims=True)
            dxdt = dw * dec_e
            dlast_e = dlast_e + _segsum(jnp.sum(ds * s, axis=0, keepdims=True)) * cd_e
            dcb = jnp.zeros((n, n), F32)
            for r in range(4):
                lm = _decay_matrix(acs_e, acs_r4, r)
                mr = cb * lm
                dyr16 = (dyv * masks[r]).astype(BF16)
                dm = _dot(dyr16, xdt * masks[r], 'nt')
                dcb = dcb + dm * lm
                dseg = dm * mr
                dcol = jnp.sum(dseg, axis=1, keepdims=True) - jnp.sum(dseg.T, axis=1, keepdims=True)
                dacs_e = dacs_e + dcol * masks[r]
                dxdt = dxdt + _dot(mr.T, dyr16)
            dcm = dcm + _dot(dcb, b16)
            dbm = dbm + _dot(dcb.T, c16)
            dacs_e = dacs_e + jnp.where(_iota((n, 1), 0) == n - 1, dlast_e, 0.0)
            triu = (_iota((n, n), 0) <= _iota((n, n), 1)).astype(F32)
            ddta_e = _dot_exact(triu, dacs_e)
            ddt_e = ddta_e * a_e + _segsum(dxdt * xv)
            dx_ref[rows] = (dxdt * dt_e + dyv * d_e) * _dsilu(ux_ref[rows])
            db_ref[rows] = dbm * _dsilu(ub_ref[rows])
            dc_ref[rows] = dcm * _dsilu(uc_ref[rows])
            draw_e = ddt_e * _sigmoid(raw_e + prw[0:1, :])
            draw_t = draw_e.T
            ddt_ref[0, :, rows] = jnp.concatenate([draw_t[r * SSD_HEAD_DIM:r * SSD_HEAD_DIM + 1] for r in range(4)], axis=0)
            dbias = jnp.sum(draw_e, axis=0, keepdims=True)
            dalog = jnp.sum(ddta_e * dt_e, axis=0, keepdims=True) * a_e
            dd = _segsum(jnp.sum(dyv * xv, axis=0, keepdims=True))
            row3 = _iota((3, 1), 0)
            dp_ref[0] += (jnp.where(row3 == 0, dbias, 0.0) + jnp.where(row3 == 1, dalog, 0.0)
                          + jnp.where(row3 == 2, dd, 0.0))
            ds = ds_prev
        ds_scr[...] = ds


    x, bm, cm, dtcs, dtrs, prs, pcs, st, ch = _ssd_specs(t, True)
    yblk = pl.BlockSpec((SSD_ROWS, 256), lambda g, c: (ch(c), g))
    nblk = pl.BlockSpec((SSD_ROWS, SSD_STATE), lambda g, c: (ch(c), g))
    return pl.pallas_call(
        body, name="ssd_bwd", grid=(SSD_GROUPS, t // SSD_ROWS),
        in_specs=[x, bm, cm, x, bm, cm, dtcs, dtrs, prs, pcs, pcs, st, yblk],
        out_specs=[yblk, nblk, nblk, dtrs, prs],
        out_shape=[jax.ShapeDtypeStruct((t, SSD_INNER), F32), jax.ShapeDtypeStruct((t, BC_WIDTH), F32),
                   jax.ShapeDtypeStruct((t, BC_WIDTH), F32), jax.ShapeDtypeStruct((SSD_GROUPS, 4, t), F32),
                   jax.ShapeDtypeStruct((SSD_GROUPS, 3, 256), F32)],
        scratch_shapes=[pltpu.VMEM((SSD_STATE, 256), F32)],
        compiler_params=_cp(("parallel", "arbitrary")))(xbc, xbc, xbc, pre, pre, pre, dtc, dtr, prow, bcol, acol,
                                                         states, dy)


GROUP_W = SSD_INNER // SSD_GROUPS


def _mix_specs(tb):
    row = pl.BlockSpec((tb, 2048), lambda i: (i, 0))
    zlo = pl.BlockSpec((tb, 1024), lambda i: (i, O_Z // 1024))
    zhi = pl.BlockSpec((tb, 1024), lambda i: (i, O_Z // 1024 + 1))
    vec = pl.BlockSpec((1, 2048), lambda i: (0, 0))
    return row, zlo, zhi, vec


def _mix_fwd(attn, y, proj, g_attn, g_ssd):
    t = attn.shape[0]
    tb = _rows(t, 256)

    def body(a_ref, y_ref, zlo_ref, zhi_ref, ga_ref, gs_ref, o_ref):
        av = a_ref[...]
        r = lax.rsqrt(jnp.mean(av * av, axis=-1, keepdims=True) + EPS)
        o_ref[:, :ATTN_WIDTH] = (av * r * ga_ref[...]).astype(BF16)
        for g in range(SSD_GROUPS):
            lo, hi = g * GROUP_W, (g + 1) * GROUP_W
            zref = zlo_ref if g < 4 else zhi_ref
            z = zref[:, lo % 1024:lo % 1024 + GROUP_W]
            yg = y_ref[:, lo:hi] * (z * _sigmoid(z))
            rg = lax.rsqrt(jnp.mean(yg * yg, axis=-1, keepdims=True) + EPS)
            o_ref[:, ATTN_WIDTH + lo:ATTN_WIDTH + hi] = (yg * rg * gs_ref[:, lo:hi]).astype(BF16)

    row, zlo, zhi, vec = _mix_specs(tb)
    return pl.pallas_call(
        body, name="mix_fwd", grid=(t // tb,), in_specs=[row, row, zlo, zhi, vec, vec],
        out_specs=pl.BlockSpec((tb, 4096), lambda i: (i, 0)), out_shape=jax.ShapeDtypeStruct((t, 4096), BF16),
        compiler_params=_cp(("parallel",)))(attn, y, proj, proj, g_attn, g_ssd)


def _mix_bwd(dmix, attn, y, proj, g_attn, g_ssd):
    t = attn.shape[0]
    tb = _rows(t, 256)

    def body(dm_ref, a_ref, y_ref, zlo_ref, zhi_ref, ga_ref, gs_ref, da_ref, dy_ref, dz_ref, dga_ref, dgs_ref):
        i = pl.program_id(0)
        av = a_ref[...]
        dn = dm_ref[:, :ATTN_WIDTH].astype(F32)
        r = lax.rsqrt(jnp.mean(av * av, axis=-1, keepdims=True) + EPS)
        u = dn * ga_ref[...]
        da_ref[...] = r * u - av * (r * r * r * jnp.mean(u * av, axis=-1, keepdims=True))
        dga = jnp.sum(dn * av * r, axis=0, keepdims=True)

        @pl.when(i == 0)
        def _():
            dga_ref[...] = dga

        @pl.when(i > 0)
        def _():
            dga_ref[...] += dga

        for g in range(SSD_GROUPS):
            lo, hi = g * GROUP_W, (g + 1) * GROUP_W
            zref = zlo_ref if g < 4 else zhi_ref
            z = zref[:, lo % 1024:lo % 1024 + GROUP_W]
            yv = y_ref[:, lo:hi]
            sg = _sigmoid(z)
            sz = z * sg
            yg = yv * sz
            rg = lax.rsqrt(jnp.mean(yg * yg, axis=-1, keepdims=True) + EPS)
            do = dm_ref[:, ATTN_WIDTH + lo:ATTN_WIDTH + hi].astype(F32)
            ug = do * gs_ref[:, lo:hi]
            dyg = rg * ug - yg * (rg * rg * rg * jnp.mean(ug * yg, axis=-1, keepdims=True))
            dy_ref[:, lo:hi] = dyg * sz
            dz_ref[:, lo:hi] = (dyg * yv * (sg * (1.0 + z * (1.0 - sg)))).astype(BF16)
            dgs = jnp.sum(do * yg * rg, axis=0, keepdims=True)

            @pl.when(i == 0)
            def _():
                dgs_ref[:, lo:hi] = dgs

            @pl.when(i > 0)
            def _():
                dgs_ref[:, lo:hi] += dgs

    row, zlo, zhi, vec = _mix_specs(tb)
    return pl.pallas_call(
        body, name="mix_bwd", grid=(t // tb,),
        in_specs=[pl.BlockSpec((tb, 4096), lambda i: (i, 0)), row, row, zlo, zhi, vec, vec],
        out_specs=[row, row, row, vec, vec],
        out_shape=[jax.ShapeDtypeStruct((t, 2048), F32), jax.ShapeDtypeStruct((t, 2048), F32),
                   jax.ShapeDtypeStruct((t, 2048), BF16), jax.ShapeDtypeStruct((1, 2048), F32),
                   jax.ShapeDtypeStruct((1, 2048), F32)],
        compiler_params=_cp(("arbitrary",)))(dmix, attn, y, proj, proj, g_attn, g_ssd)


def _adamw(w, g, m, v, name):
    r, c = w.shape
    tb = _rows(r, 256)
    c1 = 1.0 - ADAM_B1 ** ADAM_STEP
    c2 = 1.0 - ADAM_B2 ** ADAM_STEP

    def body(w_ref, g_ref, m_ref, v_ref, d_ref, m2_ref, v2_ref):
        gv = g_ref[...]
        m2 = ADAM_B1 * m_ref[...] + (1.0 - ADAM_B1) * gv
        v2 = ADAM_B2 * v_ref[...] + (1.0 - ADAM_B2) * (gv * gv)
        d_ref[...] = -ADAM_LR * ((m2 / c1) / (jnp.sqrt(v2 / c2) + ADAM_EPS) + ADAM_WD * w_ref[...])
        m2_ref[...] = m2
        v2_ref[...] = v2

    blk = pl.BlockSpec((tb, c), lambda i: (i, 0))
    shp = jax.ShapeDtypeStruct((r, c), F32)
    return pl.pallas_call(body, name=name, grid=(r // tb,), in_specs=[blk] * 4, out_specs=[blk] * 3,
                          out_shape=[shp] * 3, compiler_params=_cp(("parallel",)))(w, g, m, v)


def _adamw_halves(w, mine, theirs, m, v, pos, name, cols=False):
    r, c = w.shape
    h = r if cols else r // 2
    tb = _rows(h, 128)
    nh = h // tb
    c1 = 1.0 - ADAM_B1 ** ADAM_STEP
    c2 = 1.0 - ADAM_B2 ** ADAM_STEP

    def body(pos_ref, w_ref, a_ref, b_ref, m_ref, v_ref, g_ref, d_ref, m2_ref, v2_ref):
        which = pl.program_id(1) if cols else pl.program_id(0) // nh
        gv = jnp.where(which == pos_ref[0], a_ref[...], b_ref[...])
        m2 = ADAM_B1 * m_ref[...] + (1.0 - ADAM_B1) * gv
        v2 = ADAM_B2 * v_ref[...] + (1.0 - ADAM_B2) * (gv * gv)
        g_ref[...] = gv
        d_ref[...] = -ADAM_LR * ((m2 / c1) / (jnp.sqrt(v2 / c2) + ADAM_EPS) + ADAM_WD * w_ref[...])
        m2_ref[...] = m2
        v2_ref[...] = v2

    if cols:
        full = pl.BlockSpec((tb, c // 2), lambda i, j, pref: (i, j))
        mine_spec = theirs_spec = pl.BlockSpec((tb, c // 2), lambda i, j, pref: (i, 0))
        grid = (nh, 2)
    else:
        full = pl.BlockSpec((tb, c), lambda i, pref: (i, 0))
        mine_spec = pl.BlockSpec((tb, c), lambda i, pref: (jnp.where(i // nh == pref[0], i % nh,
                                                                     jnp.where(pref[0] == 0, nh - 1, 0)), 0))
        theirs_spec = pl.BlockSpec((tb, c), lambda i, pref: (jnp.where(i // nh != pref[0], i % nh,
                                                                       jnp.where(pref[0] == 0, 0, nh - 1)), 0))
        grid = (r // tb,)
    shp = jax.ShapeDtypeStruct((r, c), F32)
    grid_spec = pltpu.PrefetchScalarGridSpec(num_scalar_prefetch=1, grid=grid,
                                             in_specs=[full, mine_spec, theirs_spec, full, full],
                                             out_specs=[full] * 4)
    return pl.pallas_call(body, name=name, grid_spec=grid_spec, out_shape=[shp] * 4,
                          compiler_params=_cp(("parallel",) * len(grid)))(pos, w, mine, theirs, m, v)


def _sum_own_half(g4, recv, pos, name, cols=False):
    _, r, c = g4.shape
    h, c = (r, c // 2) if cols else (r // 2, c)
    tb = _rows(h, 128)
    nh = h // tb

    def slot(j, pref):
        return (pref[1] + 1 + j) % N_CHIPS

    if cols:
        own = lambda j, i, pref: (slot(j, pref), i, pref[0])
    else:
        own = lambda j, i, pref: (slot(j, pref), pref[0] * nh + i, 0)
    same = lambda j, i, pref: (slot(j, pref), i, 0)

    def body(pos_ref, a_ref, b_ref, o_ref):
        o_ref[...] = (a_ref[...] + b_ref[...]).astype(BF16)

    grid_spec = pltpu.PrefetchScalarGridSpec(
        num_scalar_prefetch=1, grid=(N_CHIPS - 1, nh),
        in_specs=[pl.BlockSpec((1, tb, c), own), pl.BlockSpec((1, tb, c), same)],
        out_specs=pl.BlockSpec((1, tb, c), same))
    return pl.pallas_call(body, name=name, grid_spec=grid_spec,
                          out_shape=jax.ShapeDtypeStruct((N_CHIPS, h, c), BF16),
                          compiler_params=_cp(("parallel", "parallel")))(pos, g4, recv)


def _sum_chips(g4, recv, parts, pos, name, cols=False):
    _, r, c = g4.shape
    h, c = (r, c // 2) if cols else (r // 2, c)
    tb = _rows(h, 128)
    nh = h // tb
    own = (lambda i, pref: (pref[1], i, pref[0])) if cols else (lambda i, pref: (pref[1], pref[0] * nh + i, 0))

    def body(pos_ref, a_ref, b_ref, p_ref, o_ref):
        own = a_ref[0] + b_ref[0]
        o_ref[...] = ((own + p_ref[0].astype(F32)) + p_ref[1].astype(F32)) + p_ref[2].astype(F32)

    grid_spec = pltpu.PrefetchScalarGridSpec(
        num_scalar_prefetch=1, grid=(nh,),
        in_specs=[pl.BlockSpec((1, tb, c), own),
                  pl.BlockSpec((1, tb, c), lambda i, pref: (pref[1], i, 0)),
                  pl.BlockSpec((3, tb, c), lambda i, pref: (0, i, 0))],
        out_specs=pl.BlockSpec((tb, c), lambda i, pref: (i, 0)))
    return pl.pallas_call(body, name=name, grid_spec=grid_spec, out_shape=jax.ShapeDtypeStruct((h, c), F32),
                          compiler_params=_cp(("parallel",)))(pos, g4, recv, parts)


def _me():
    return lax.axis_index("x"), lax.axis_index("y"), lax.axis_index("c")


def _flip(v, bit):
    return (1 - v) if bit else v


CHIP_FLIPS = [(1, 0), (0, 1), (1, 1)]


def _forward_halves(gathered, own):
    def body(g_ref, own_ref, o_ref, token, send_sems, recv_sems, own_sem):
        x, y, c = _me()
        h = g_ref.shape[2] // 2
        own_cp = pltpu.make_async_copy(own_ref, o_ref.at[2 * x + y], own_sem)
        own_cp.start()
        cps = []
        for k, (fx, fy) in enumerate(CHIP_FLIPS):
            peer_chip = 2 * _flip(x, fx) + _flip(y, fy)
            mine = o_ref.at[peer_chip, :, pl.ds(c * h, h)]
            cp = pltpu.make_async_remote_copy(src_ref=mine, dst_ref=mine, send_sem=send_sems.at[k],
                                              recv_sem=recv_sems.at[k], device_id=(x, y, 1 - c), device_id_type=MESH)
            cp.start()
            cps.append(cp)
        for k, (fx, fy) in enumerate(CHIP_FLIPS):
            peer_chip = 2 * _flip(x, fx) + _flip(y, fy)
            theirs = o_ref.at[peer_chip, :, pl.ds((1 - c) * h, h)]
            pltpu.make_async_remote_copy(src_ref=theirs, dst_ref=theirs, send_sem=send_sems.at[k],
                                         recv_sem=recv_sems.at[k], device_id=(x, y, 1 - c),
                                         device_id_type=MESH).wait_recv()
        for cp in cps:
            cp.wait_send()
        own_cp.wait()
        token[...] = jnp.zeros_like(token)

    return pl.pallas_call(
        body, name="gather_forward_w_in", in_specs=[HBM_SPEC, HBM_SPEC],
        out_specs=[HBM_SPEC, pl.BlockSpec(memory_space=pltpu.VMEM)],
        out_shape=[jax.ShapeDtypeStruct(gathered.shape, gathered.dtype), TOKEN],
        scratch_shapes=[pltpu.SemaphoreType.DMA((3,)), pltpu.SemaphoreType.DMA((3,)), pltpu.SemaphoreType.DMA(())],
        input_output_aliases={0: 0},
        compiler_params=pltpu.CompilerParams(has_side_effects=True))(gathered, own)


SEM_SPEC = pl.BlockSpec(memory_space=pltpu.SEMAPHORE)
ANY_SPEC = pl.BlockSpec(memory_space=pl.ANY)
DATAFLOW = pltpu.SideEffectType.DATAFLOW_SIDE_EFFECTING


def _in_hbm(a):
    return pltpu.with_memory_space_constraint(a, pltpu.HBM)


def _push_start(srcs, land_shapes, route, peers, name):
    n, npeer = len(srcs), len(peers)
    lands = [lax.empty(shp, s.dtype) for shp, s in zip(land_shapes, srcs)]

    def body(*refs):
        ins, lnd = refs[:n], refs[n:2 * n]
        send_sems, recv_sems = refs[2 * n], refs[2 * n + 1]
        token = refs[-1]
        x, y, c = _me()
        for t in range(n):
            for k, (fx, fy, fc) in enumerate(peers):
                src, dst = route(ins[t], lnd[t], k, x, y, c)
                pltpu.make_async_remote_copy(
                    src_ref=src, dst_ref=dst, send_sem=send_sems.at[npeer * t + k],
                    recv_sem=recv_sems.at[npeer * t + k],
                    device_id=(_flip(x, fx), _flip(y, fy), _flip(c, fc)), device_id_type=MESH).start()
        token[...] = jnp.zeros_like(token)

    bufs = [_in_hbm(a) for a in list(srcs) + lands]
    outs = pl.pallas_call(
        body, name=name,
        out_shape=(pltpu.SemaphoreType.DMA((npeer * n,)), pltpu.SemaphoreType.DMA((npeer * n,)),
                   *[pltpu.HBM(b.shape, b.dtype) for b in bufs], TOKEN),
        in_specs=[HBM_SPEC] * (2 * n),
        out_specs=(SEM_SPEC, SEM_SPEC, *[HBM_SPEC] * (2 * n), pl.BlockSpec(memory_space=pltpu.VMEM)),
        input_output_aliases={i: 2 + i for i in range(2 * n)},
        compiler_params=pltpu.CompilerParams(has_side_effects=DATAFLOW))(*bufs)
    return outs[0], outs[1], list(outs[2:2 + n]), list(outs[2 + n:2 + 2 * n]), outs[-1]


def _push_wait(send_sems, recv_sems, srcs, lands, after, route, peers, name):
    n, npeer = len(srcs), len(peers)

    def body(*refs):
        ins, lnd = refs[:n], refs[n:2 * n]
        ssem, rsem = refs[2 * n], refs[2 * n + 1]
        x, y, c = _me()
        for t in range(n):
            for k, (fx, fy, fc) in enumerate(peers):
                src, dst = route(ins[t], lnd[t], k, x, y, c)
                cp = pltpu.make_async_remote_copy(
                    src_ref=src, dst_ref=dst, send_sem=ssem.at[npeer * t + k], recv_sem=rsem.at[npeer * t + k],
                    device_id=(_flip(x, fx), _flip(y, fy), _flip(c, fc)), device_id_type=MESH)
                cp.wait_send()
                cp.wait_recv()

    bufs = list(srcs) + list(lands)
    outs = pl.pallas_call(
        body, name=name, out_shape=tuple(pltpu.HBM(b.shape, b.dtype) for b in bufs),
        in_specs=[HBM_SPEC] * (2 * n) + [SEM_SPEC, SEM_SPEC, ANY_SPEC], out_specs=tuple([HBM_SPEC] * (2 * n)),
        input_output_aliases={i: i for i in range(2 * n)},
        compiler_params=pltpu.CompilerParams(has_side_effects=DATAFLOW))(*bufs, send_sems, recv_sems, after)
    return list(outs[:n]), list(outs[n:])


OTHER_CHIPS = [(fx, fy, 0) for fx, fy in CHIP_FLIPS]
SIBLING = [(0, 0, 1)]


def _route_gather(src, land, k, x, y, c):
    return src, land.at[2 * x + y]


def _route_gather_half(src, land, k, x, y, c):
    h = src.shape[1] // 2
    return src.at[:, pl.ds(c * h, h)], land.at[2 * x + y, :, pl.ds(c * h, h)]


def _route_gather_half_wait(src, land, k, x, y, c):
    fx, fy = CHIP_FLIPS[k]
    h = src.shape[1] // 2
    return src.at[:, pl.ds(c * h, h)], land.at[2 * _flip(x, fx) + _flip(y, fy), :, pl.ds(c * h, h)]


def _route_gather_wait(src, land, k, x, y, c):
    fx, fy = CHIP_FLIPS[k]
    return src, land.at[2 * _flip(x, fx) + _flip(y, fy)]


def _route_scatter(src, land, k, x, y, c):
    fx, fy = CHIP_FLIPS[k]
    return src.at[2 * _flip(x, fx) + _flip(y, fy)], land.at[k]


def _route_exchange(src, land, k, x, y, c):
    h = land.shape[1]
    return src.at[:, pl.ds((1 - c) * h, h)], land


def _route_whole(src, land, k, x, y, c):
    return src, land


def _route_exchange_cols(src, land, k, x, y, c):
    h = land.shape[2]
    return src.at[:, :, pl.ds((1 - c) * h, h)], land


ALL_OTHERS = [((k >> 2) & 1, (k >> 1) & 1, k & 1) for k in range(1, 8)]


def _route_to_all(src, land, k, x, y, c):
    return src, land.at[4 * x + 2 * y + c]


def _route_to_all_wait(src, land, k, x, y, c):
    fx, fy, fc = ALL_OTHERS[k]
    return src, land.at[4 * _flip(x, fx) + 2 * _flip(y, fy) + _flip(c, fc)]


def _sum_devices(parts):
    def body(p_ref, o_ref):
        acc = p_ref[0]
        for d in range(1, 8):
            acc = acc + p_ref[d]
        o_ref[...] = acc

    vm = pl.BlockSpec(memory_space=pltpu.VMEM)
    return pl.pallas_call(body, name="allreduce_sum", in_specs=[vm], out_specs=vm,
                          out_shape=jax.ShapeDtypeStruct(parts.shape[1:], F32),
                          compiler_params=pltpu.CompilerParams(vmem_limit_bytes=VMEM_LIMIT))(parts)


def _grad_exchange_start(g4, tag, cols=False):
    land = (N_CHIPS, g4.shape[1], g4.shape[2] // 2) if cols else (N_CHIPS, g4.shape[1] // 2, g4.shape[2])
    route = _route_exchange_cols if cols else _route_exchange
    send_sems, recv_sems, srcs, lands, token = _push_start(
        [g4], [land], route, SIBLING, name="grad_exchange_start_" + tag)
    return (send_sems, recv_sems, srcs, lands, tag, cols), token


def _grad_scatter_start(state, pos, after):
    send_sems, recv_sems, srcs, lands, tag, cols = state
    route = _route_exchange_cols if cols else _route_exchange
    (g4,), (recv,) = _push_wait(send_sems, recv_sems, srcs, lands, after, route, SIBLING,
                                name="grad_exchange_wait_" + tag)
    return _grad_pair_scatter(g4, recv, pos, tag, cols)


def _grad_pair_scatter(g4, recv, pos, tag, cols=False):
    p16 = _sum_own_half(g4, recv, pos, name="grad_sum_pair_" + tag, cols=cols)
    send_sems, recv_sems, srcs, lands, token = _push_start(
        [p16], [(3,) + p16.shape[1:]], _route_scatter, OTHER_CHIPS, name="grad_scatter_start_" + tag)
    return (g4, recv, send_sems, recv_sems, srcs, lands, tag, cols), token


def _grad_sum_and_share(state, pos, after):
    g4, recv, send_sems, recv_sems, srcs, lands, tag, cols = state
    parts = _push_wait(send_sems, recv_sems, srcs, lands, after, _route_scatter, OTHER_CHIPS,
                       name="grad_scatter_wait_" + tag)[1][0]
    mine = _sum_chips(g4, recv, parts, pos, name="grad_sum_chips_" + tag, cols=cols)
    send_sems, recv_sems, srcs, lands, token = _push_start(
        [mine], [mine.shape], _route_whole, SIBLING, name="grad_share_start_" + tag)
    return (send_sems, recv_sems, srcs, lands, tag), token


def _grad_share_wait(state, after):
    send_sems, recv_sems, srcs, lands, tag = state
    (mine,), (theirs,) = _push_wait(send_sems, recv_sems, srcs, lands, after, _route_whole, SIBLING,
                                    name="grad_share_wait_" + tag)
    return mine, theirs


def _local_step(x, tgt, p, hooks):
    t = x.shape[0]
    tables = _rope_tables(t)
    sinks = p['sinks'].reshape(N_Q_HEADS)

    def told(name, value):
        return tuple(hooks.grad_ready(name, value))

    xn = _rmsnorm_fwd(x, p['norm_mix'], "norm_mix_fwd", deps=hooks.first_deps)
    w_in_t, w_in_dt, in_deps = hooks.weight_in(xn)
    proj = _matmul(xn, w_in_t, mode='nt', name="in_proj", n_limit=MAIN_WIDTH, deps=in_deps)
    dt_raw = _matmul(xn, w_in_dt, mode='nt', name="in_proj_dt")[:, :SSD_HEADS]
    ssd_conv_w, ffn_conv_w = hooks.conv_weights(proj)
    p = dict(p, ssd_conv_w=ssd_conv_w, ffn_conv_w=ffn_conv_w)
    attn = _attn_fwd(proj, sinks, tables)
    conv_b = p['ssd_conv_b']
    xbc, xbc_pre = _conv_silu_fwd(proj, p['ssd_conv_w'], conv_b, col0=O_XBC, width=CONV_CH, name="ssd_conv_fwd")
    sp = _ssd_params(dt_raw, p['dt_bias'].reshape(-1), p['a_log'].reshape(-1), p['ssd_d'].reshape(-1))
    y, states = _ssd_fwd(xbc, sp)
    mix = _mix_fwd(attn, y, proj, p['attn_out_norm'], p['ssd_norm'])
    w_out = hooks.weight('w_out', mix)
    h1 = _matmul(mix, w_out, mode='nn', name="out_proj", add=x)
    hn = _rmsnorm_fwd(h1, p['norm_ffn'], "norm_ffn_fwd")
    w_up = hooks.weight('w_up', hn)
    u0 = _matmul(hn, w_up, mode='nn', name="ffn_up", b_owner=True, tn=1408)
    a, u = _ffn_act_fwd(u0, p['ffn_conv_w'], p['ffn_conv_b'])
    w_down = hooks.weight('w_down', a)
    h2 = _matmul(a, w_down, mode='nn', name="ffn_down", add=h1, tk=2816)
    loss, dh2, dh2_16, g_norm_final = _final_loss(h2, p['norm_final'].reshape(1, D_MODEL), tgt)

    g = {}
    da = _matmul(dh2_16, w_down, mode='nt', name="ffn_down_dx", out_dtype=BF16, tn=1408)
    g['w_down'] = _matmul(a, dh2_16, mode='tn', name="ffn_down_dw", tm=1408)
    dep = told('w_down', g['w_down'])
    du0, dcw, dcb = _ffn_act_bwd(u0, u, p['ffn_conv_w'], da)
    g['ffn_conv_w'] = dcw.transpose(1, 0, 2).reshape(FFN_CONV, 2 * D_FF)
    g['ffn_conv_b'] = dcb.transpose(1, 0, 2).reshape(1, 2 * D_FF)
    g['w_up'] = _matmul(hn, du0, mode='tn', name="ffn_up_dw", deps=dep, b_halves=True, owner_major=True,
                        tn=1408)
    dep = told('w_up', g['w_up'])
    dhn = _matmul(du0, w_up, mode='nt', name="ffn_up_dx", out_dtype=BF16, deps=dep, a_halves=True,
                  b_owner=True, tk=2816)
    dh1, dh1_16, g['norm_ffn'] = _rmsnorm_bwd(h1, p['norm_ffn'], dhn, dh2, "norm_ffn_bwd")

    g['w_out'] = _matmul(mix, dh1_16, mode='tn', name="out_proj_dw")
    dep = told('w_out', g['w_out'])
    dmix = _matmul(dh1_16, w_out, mode='nt', name="out_proj_dx", out_dtype=BF16, deps=dep)
    dattn, dy, dz, g['attn_out_norm'], g['ssd_norm'] = _mix_bwd(dmix, attn, y, proj, p['attn_out_norm'],
                                                                p['ssd_norm'])
    dq, dk, dv, dsink = _attn_bwd(proj, sinks, tables, dattn)
    g['sinks'] = dsink[:, :, 0].reshape(1, N_Q_HEADS)
    dxs, dbm, dcm, ddt8, dpar = _ssd_bwd(xbc, xbc_pre, sp, states, dy)
    dpar = dpar[:, :, ::SSD_HEAD_DIM]
    g['dt_bias'] = dpar[:, 0, :].reshape(1, SSD_HEADS)
    g['a_log'] = dpar[:, 1, :].reshape(1, SSD_HEADS)
    g['ssd_d'] = dpar[:, 2, :].reshape(1, SSD_HEADS)
    dxbc, g['ssd_conv_w'], g['ssd_conv_b'] = _ssd_conv_bwd(proj, p['ssd_conv_w'], dxs, dbm, dcm, col0=O_XBC,
                                                           name="ssd_conv_bwd")
    dproj = jnp.concatenate([dq, dk, dv, dz, dxbc], axis=1)
    ddt = ddt8.transpose(2, 0, 1).reshape(t, SSD_HEADS)
    ddt_pad = jnp.pad(ddt, ((0, 0), (0, LANES - SSD_HEADS))).astype(BF16)
    g['w_in'] = (_matmul(dproj, xn, mode='tn', name="in_proj_dw", m_rows=IN_PROJ_WIDTH),
                 _matmul(ddt_pad, xn, mode='tn', name="in_proj_dt_dw"))
    dep = told('w_in', g['w_in'])
    dxn_dt = _matmul(ddt_pad, w_in_dt, mode='nn', name="in_proj_dt_dx", deps=dep)
    dxn = _matmul(dproj, w_in_t, mode='nn', name="in_proj_dx", out_dtype=BF16, add=dxn_dt, k_limit=MAIN_WIDTH,
                  tk=2304)
    dep = told(None, dxn)
    dx, _, g['norm_mix'] = _rmsnorm_bwd(x, p['norm_mix'], dxn, dh1, "norm_mix_bwd", deps=dep)
    g['norm_final'] = g_norm_final
    return loss, dx, g


def _pack(arrs):
    flat = jnp.concatenate([a.reshape(-1) for a in arrs])
    n = flat.shape[0]
    rows = -(-n // LANES)
    rows = -(-rows // 8) * 8
    return jnp.pad(flat, (0, rows * LANES - n)).reshape(rows, LANES)


def _unpack(packed, shapes):
    flat = packed.reshape(-1)
    out, off = [], 0
    for s in shapes:
        n = 1
        for d in s:
            n *= d
        out.append(flat[off:off + n].reshape(s))
        off += n
    return out


class _StepHooks:
    def __init__(self, first_deps, weight_in, conv_weights, weight, grad_ready):
        self.first_deps = first_deps
        self.weight_in = weight_in
        self.conv_weights = conv_weights
        self.weight = weight
        self.grad_ready = grad_ready


def kernel(x, norm_mix, w_in, sinks, attn_out_norm, ssd_conv_w, ssd_conv_b, dt_bias, a_log, ssd_d, ssd_norm, w_out, norm_ffn, w_up, ffn_conv_w, ffn_conv_b, w_down, norm_final, loss_target, m_norm_mix, m_w_in, m_sinks, m_attn_out_norm, m_ssd_conv_w, m_ssd_conv_b, m_dt_bias, m_a_log, m_ssd_d, m_ssd_norm, m_w_out, m_norm_ffn, m_w_up, m_ffn_conv_w, m_ffn_conv_b, m_w_down, m_norm_final, v_norm_mix, v_w_in, v_sinks, v_attn_out_norm, v_ssd_conv_w, v_ssd_conv_b, v_dt_bias, v_a_log, v_ssd_d, v_ssd_norm, v_w_out, v_norm_ffn, v_w_up, v_ffn_conv_w, v_ffn_conv_b, v_w_down, v_norm_final):
    args = dict(locals())
    w = {n: args[n] for n in WEIGHTS}
    m = {n: args['m_' + n] for n in WEIGHTS}
    v = {n: args['v_' + n] for n in WEIGHTS}
    xi, yi, ci = _me()
    chip = 2 * xi + yi
    pos = jnp.stack([ci, chip]).astype(jnp.int32)

    w_in_t, m_in_t, v_in_t = (jnp.transpose(a[0]) for a in (w_in, m_w_in, v_w_in))
    in_shard = w_in_t.astype(BF16)
    in_gather = _push_start([in_shard], [(N_CHIPS,) + in_shard.shape], _route_gather_half, OTHER_CHIPS,
                            name="gather_start_w_in")
    conv_shard = _pack([ssd_conv_w[0], ffn_conv_w[0]]) + in_gather[4][:1, :1]
    conv_gather = _push_start([conv_shard], [(N_CHIPS,) + conv_shard.shape], _route_gather, OTHER_CHIPS,
                              name="gather_start_conv")

    def conv_weights(after):
        send_sems, recv_sems, srcs, lands, _ = conv_gather
        (own,), (got,) = _push_wait(send_sems, recv_sems, srcs, lands, after, _route_gather_wait, OTHER_CHIPS,
                                    name="gather_wait_conv")
        whole = lax.dynamic_update_slice(got, own[None], (chip, 0, 0))
        per_chip = [_unpack(whole[j], [ssd_conv_w.shape[1:], ffn_conv_w.shape[1:]]) for j in range(N_CHIPS)]
        return (jnp.concatenate([pc[0] for pc in per_chip], axis=1),
                jnp.concatenate([pc[1] for pc in per_chip], axis=1))

    gathers = {}
    order = conv_gather[4][:1, :1]
    for n, shard in (('w_out', w_out[0]), ('w_up', w_up[0]), ('w_down', w_down[0])):
        shard = (shard + order).astype(BF16)
        gathers[n] = _push_start([shard], [(N_CHIPS,) + shard.shape], _route_gather, OTHER_CHIPS,
                                 name="gather_start_" + n)
        order = gathers[n][4][:1, :1]

    def weight_in(after):
        send_sems, recv_sems, srcs, lands, _ = in_gather
        (own,), (got,) = _push_wait(send_sems, recv_sems, srcs, lands, after, _route_gather_half_wait, OTHER_CHIPS,
                                    name="gather_wait_w_in")
        full_in_t = _forward_halves(got, own)[0].reshape(IN_PROJ_WIDTH, D_MODEL)
        w_in_dt = jnp.pad(full_in_t[MAIN_WIDTH:], ((0, LANES - SSD_HEADS), (0, 0)))
        return full_in_t, w_in_dt, ()

    def weight(name, after):
        send_sems, recv_sems, srcs, lands, _ = gathers[name]
        (own,), (got,) = _push_wait(send_sems, recv_sems, srcs, lands, after, _route_gather_wait, OTHER_CHIPS,
                                    name="gather_wait_" + name)
        whole = lax.dynamic_update_slice(got, own[None], (chip, 0, 0))
        return whole if name == 'w_up' else whole.reshape(-1, D_MODEL)

    reductions, exchanging = {}, {}

    def flush(after):
        tokens = []
        for prev in list(exchanging):
            reductions[prev], token = _grad_scatter_start(exchanging.pop(prev), pos, after)
            tokens.append(token)
        return tokens

    def grad_ready(name, value):
        if name is None:
            return flush(value)
        if name == 'w_in':
            main, dtp = value
            value = lax.dynamic_update_slice(main, dtp[:SSD_HEADS], (MAIN_WIDTH, 0))
        g4 = value if value.ndim == 3 else value.reshape(N_CHIPS, -1, value.shape[1])
        tokens = flush(g4)
        exchanging[name], token = _grad_exchange_start(g4, name, cols=(name == 'w_in'))
        return tokens + [token]

    small = {
        'norm_mix': norm_mix, 'sinks': sinks, 'attn_out_norm': attn_out_norm,
        'ssd_conv_b': ssd_conv_b, 'dt_bias': dt_bias, 'a_log': a_log, 'ssd_d': ssd_d, 'ssd_norm': ssd_norm,
        'norm_ffn': norm_ffn, 'ffn_conv_b': ffn_conv_b, 'norm_final': norm_final,
    }
    loss, dx, g = _local_step(x[0], loss_target[0], small,
                              _StepHooks((gathers['w_down'][4],), weight_in, conv_weights, weight, grad_ready))

    small_names = [n for n in WEIGHTS if n not in BIG]
    small_g = [loss[:, :1]] + [g[n] for n in small_names]
    small_shapes = [(1, 1)] + [tuple(a.shape) for a in small_g[1:]]
    packed = _pack(small_g)
    spread = _push_start([packed], [(8,) + packed.shape], _route_to_all, ALL_OTHERS, name="allreduce_start")
    grads, deltas, new_m, new_v = {}, {}, {}, {}
    after = spread[4]
    shares = {}
    for n in ('w_down', 'w_up', 'w_out'):
        shares[n], after = _grad_sum_and_share(reductions[n], pos, after)
    for n in ('w_down', 'w_up', 'w_out', 'w_in'):
        if n == 'w_out':
            shares['w_in'], after = _grad_sum_and_share(reductions['w_in'], pos, after)
        mine, theirs = _grad_share_wait(shares[n], after)
        if n == 'w_in':
            outs = _adamw_halves(w_in_t, mine, theirs, m_in_t, v_in_t, pos, name="adamw_" + n, cols=True)
            after = outs[1]
            outs = [jnp.transpose(o) for o in outs]
        else:
            outs = _adamw_halves(w[n][0], mine, theirs, m[n][0], v[n][0], pos, name="adamw_" + n)
            after = outs[1]
        grads[n], deltas[n], new_m[n], new_v[n] = [o[None] for o in outs]
    (own,), (landed,) = _push_wait(spread[0], spread[1], spread[2], spread[3], after, _route_to_all_wait, ALL_OTHERS,
                                   name="allreduce_wait")
    landed = lax.dynamic_update_slice(landed, own[None], (4 * xi + 2 * yi + ci, 0, 0))
    red = _unpack(_sum_devices(landed), small_shapes)
    loss_out = red[0].reshape(())
    gsm = dict(zip(small_names, red[1:]))
    gsm['ssd_conv_w'] = lax.dynamic_slice(gsm['ssd_conv_w'], (0, chip * ssd_conv_w.shape[2]),
                                          (SSD_CONV, ssd_conv_w.shape[2]))
    gsm['ffn_conv_w'] = lax.dynamic_slice(gsm['ffn_conv_w'], (0, chip * ffn_conv_w.shape[2]),
                                          (FFN_CONV, ffn_conv_w.shape[2]))

    shapes = [tuple(w[n].shape) for n in small_names]
    gp = _pack([gsm[n] for n in small_names])
    d, m2, v2 = _adamw(_pack([w[n] for n in small_names]), gp, _pack([m[n] for n in small_names]),
                       _pack([v[n] for n in small_names]), name="adamw_small")
    for n, gg, dd, mm, vv in zip(small_names, _unpack(gp, shapes), _unpack(d, shapes), _unpack(m2, shapes),
                                 _unpack(v2, shapes)):
        grads[n], deltas[n], new_m[n], new_v[n] = gg, dd, mm, vv

    return (loss_out, dx[None], *[grads[n] for n in WEIGHTS], *[deltas[n] for n in WEIGHTS],
            *[new_m[n] for n in WEIGHTS], *[new_v[n] for n in WEIGHTS])
```

```python
import functools

import jax
import jax.numpy as jnp
from jax import lax
from jax.experimental import pallas as pl
from jax.experimental.pallas import tpu as pltpu

F32 = jnp.float32
BF16 = jnp.bfloat16

D_MODEL = 2048
N_Q_HEADS = 32
N_KV_HEADS = 8
HEAD_DIM = 64
WINDOW = 128
ATTN_BLOCK = 128
ROT_DIM = 16
ROPE_THETA = 500000.0
SSD_HEADS = 32
SSD_HEAD_DIM = 64
SSD_INNER = 2048
SSD_GROUPS = 8
SSD_STATE = 128
SSD_CONV = 4
SSD_CHUNK = 128
ATTN_WIDTH = 2048
KV_WIDTH = 512
BC_WIDTH = 1024
CONV_CH = 4096
IN_PROJ_WIDTH = 9248
MAIN_WIDTH = 9216
D_FF = 5632
FFN_CONV = 3
EPS = 1e-6
O_Q, O_K, O_V, O_Z, O_XBC, O_DT = 0, 2048, 2560, 3072, 5120, 9216

ADAM_LR = 0.001
ADAM_B1 = 0.9
ADAM_B2 = 0.999
ADAM_EPS = 1e-08
ADAM_WD = 0.01
ADAM_STEP = 10

N_CHIPS = 4
NEG = -1e30
LANES = 128
VMEM_LIMIT = 48 * 1024 * 1024
MESH = pl.DeviceIdType.MESH
HBM_SPEC = pl.BlockSpec(memory_space=pltpu.HBM)
TOKEN = jax.ShapeDtypeStruct((8, LANES), F32)

WEIGHTS = ['norm_mix', 'w_in', 'sinks', 'attn_out_norm', 'ssd_conv_w', 'ssd_conv_b', 'dt_bias', 'a_log', 'ssd_d',
           'ssd_norm', 'w_out', 'norm_ffn', 'w_up', 'ffn_conv_w', 'ffn_conv_b', 'w_down', 'norm_final']
BIG = ['w_in', 'w_out', 'w_up', 'w_down']


def _cp(sem=None, vmem=VMEM_LIMIT):
    kw = {'vmem_limit_bytes': vmem}
    if sem is not None:
        kw['dimension_semantics'] = sem
    return pltpu.CompilerParams(**kw)


def _tile(n, pref):
    if n <= pref:
        return n
    t = (pref // LANES) * LANES
    while t > LANES and n % t:
        t -= LANES
    assert n % t == 0, (n, pref)
    return t


def _rows(n, pref):
    t = min(n, pref)
    while n % t:
        t -= 8
    if 4 * t < pref:
        t = pref
        while n % t:
            t += 8
    return t


def _iota(shape, dim):
    return lax.broadcasted_iota(jnp.int32, shape, dim)


def _dot(a, b, mode='nn'):
    dn = {'nn': (((1,), (0,)), ((), ())), 'nt': (((1,), (1,)), ((), ())), 'tn': (((0,), (0,)), ((), ()))}[mode]
    return lax.dot_general(a.astype(BF16), b.astype(BF16), dn, preferred_element_type=F32)


def _dot_exact(a, b):
    return lax.dot_general(a, b, (((1,), (0,)), ((), ())), precision=lax.Precision.HIGHEST,
                           preferred_element_type=F32)


def _sigmoid(x):
    return 1.0 / (1.0 + jnp.exp(-x))


def _softplus(x):
    return jnp.maximum(x, 0.0) + jnp.log(1.0 + jnp.exp(-jnp.abs(x)))


def _matmul(a, b, *, mode, name, out_dtype=F32, add=None, deps=(), tm=1024, tn=1024, tk=2048,
            a_halves=False, b_halves=False, b_owner=False, owner_major=False, n_limit=None, k_limit=None,
            m_rows=None):
    ash, bsh = (a.shape[1:] if a_halves else a.shape), (b.shape[1:] if (b_halves or b_owner) else b.shape)
    if mode == 'nn':
        (m, k), (k2, n) = ash, bsh
    elif mode == 'nt':
        (m, k), (n, k2) = ash, bsh
    else:
        (k, m), (k2, n) = ash, bsh
    if n_limit is not None:
        assert mode == 'nt' and n_limit <= n
        n = n_limit
    if k_limit is not None:
        assert mode == 'nn' and k_limit <= k2
        k2 = k_limit
    if a_halves:
        assert mode == 'nt'
        k = 2 * k
    if b_halves:
        assert mode == 'tn'
        n = 2 * n
    if b_owner:
        assert mode in ('nn', 'nt')
        if mode == 'nn':
            n = 4 * n
        else:
            k2 = 4 * k2
    assert k == k2, (a.shape, b.shape, mode)
    tm = _tile(m, tm)
    tn = _tile(n // 4 if (owner_major or (b_owner and mode == 'nn')) else (n // 2 if b_halves else n), tn)
    tk = _tile(k // 4 if (b_owner and mode == 'nt') else (k // 2 if a_halves else k), tk)
    nk = k // tk
    has_add = add is not None
    assert not (has_add and owner_major)

    def body(*refs):
        a_ref, b_ref = refs[:2]
        add_ref = refs[2] if has_add else None

        def finish(r, o_ref):
            if has_add:
                r = r + add_ref[...].astype(F32)
            o_ref[...] = r.astype(out_dtype)

        if nk == 1:
            finish(_dot(a_ref[...], b_ref[...], mode), refs[-1])
            return
        o_ref, acc = refs[-2:]
        kk = pl.program_id(2)

        @pl.when(kk == 0)
        def _():
            acc[...] = _dot(a_ref[...], b_ref[...], mode)

        @pl.when((kk > 0) & (kk < nk - 1))
        def _():
            acc[...] += _dot(a_ref[...], b_ref[...], mode)

        @pl.when(kk == nk - 1)
        def _():
            finish(acc[...] + _dot(a_ref[...], b_ref[...], mode), o_ref)

    if mode == 'tn':
        a_spec = pl.BlockSpec((tk, tm), lambda i, j, kk: (kk, i))
    elif a_halves:
        nkh = nk // 2
        a_spec = pl.BlockSpec((None, tm, tk), lambda i, j, kk: (kk // nkh, i, kk % nkh))
    else:
        a_spec = pl.BlockSpec((tm, tk), lambda i, j, kk: (i, kk))
    if mode == 'nt' and b_owner:
        nkq = nk // 4
        b_spec = pl.BlockSpec((None, tn, tk), lambda i, j, kk: (kk // nkq, j, kk % nkq))
    elif mode == 'nt':
        b_spec = pl.BlockSpec((tn, tk), lambda i, j, kk: (j, kk))
    elif b_owner:
        njq = (n // 4) // tn
        b_spec = pl.BlockSpec((None, tk, tn), lambda i, j, kk: (j // njq, kk, j % njq))
    elif b_halves:
        njh = (n // 2) // tn
        b_spec = pl.BlockSpec((None, tk, tn), lambda i, j, kk: (j // njh, kk, j % njh))
    else:
        b_spec = pl.BlockSpec((tk, tn), lambda i, j, kk: (kk, j))
    if owner_major:
        njo = (n // 4) // tn
        o_spec = pl.BlockSpec((None, tm, tn), lambda i, j, kk: (j // njo, i, j % njo))
        out_shape = jax.ShapeDtypeStruct((N_CHIPS, m, n // 4), out_dtype)
    else:
        o_spec = pl.BlockSpec((tm, tn), lambda i, j, kk: (i, j))
        out_shape = jax.ShapeDtypeStruct((m if m_rows is None else m_rows, n), out_dtype)
    dep_spec = pl.BlockSpec((8, LANES), lambda i, j, kk: (0, 0))
    in_specs = [a_spec, b_spec] + ([pl.BlockSpec((tm, tn), lambda i, j, kk: (i, j))] if has_add else [])
    in_specs += [dep_spec] * len(deps)
    args = (a, b) + ((add,) if has_add else ()) + tuple(deps)
    return pl.pallas_call(
        body, name=name, grid=(m // tm, n // tn, nk), in_specs=in_specs, out_specs=o_spec, out_shape=out_shape,
        scratch_shapes=[pltpu.VMEM((tm, tn), F32)] if nk > 1 else [],
        compiler_params=_cp(("parallel", "parallel", "arbitrary")))(*args)


def _rmsnorm_fwd(x, g, name, deps=()):
    t, d = x.shape
    tb = _rows(t, 256)

    def body(x_ref, g_ref, *rest):
        o_ref = rest[-1]
        xv = x_ref[...]
        r = lax.rsqrt(jnp.mean(xv * xv, axis=-1, keepdims=True) + EPS)
        o_ref[...] = (xv * r * g_ref[...]).astype(BF16)

    dep_spec = pl.BlockSpec((8, LANES), lambda i: (0, 0))
    return pl.pallas_call(
        body, name=name, grid=(t // tb,),
        in_specs=[pl.BlockSpec((tb, d), lambda i: (i, 0)), pl.BlockSpec((1, d), lambda i: (0, 0))]
        + [dep_spec] * len(deps),
        out_specs=pl.BlockSpec((tb, d), lambda i: (i, 0)), out_shape=jax.ShapeDtypeStruct((t, d), BF16),
        compiler_params=_cp(("parallel",)))(x, g, *deps)


def _rmsnorm_bwd(x, g, dy, res, name, deps=()):
    t, d = x.shape
    tb = _rows(t, 256)

    def body(x_ref, g_ref, dy_ref, res_ref, *rest):
        dx_ref, dx16_ref, dg_ref = rest[-3:]
        i = pl.program_id(0)
        xv = x_ref[...]
        dyv = dy_ref[...].astype(F32)
        r = lax.rsqrt(jnp.mean(xv * xv, axis=-1, keepdims=True) + EPS)
        u = dyv * g_ref[...]
        dx = r * u - xv * (r * r * r * jnp.mean(u * xv, axis=-1, keepdims=True)) + res_ref[...]
        dx_ref[...] = dx
        dx16_ref[...] = dx.astype(BF16)
        part = jnp.sum(dyv * xv * r, axis=0, keepdims=True)

        @pl.when(i == 0)
        def _():
            dg_ref[...] = part

        @pl.when(i > 0)
        def _():
            dg_ref[...] += part

    row = pl.BlockSpec((tb, d), lambda i: (i, 0))
    vec = pl.BlockSpec((1, d), lambda i: (0, 0))
    return pl.pallas_call(
        body, name=name, grid=(t // tb,),
        in_specs=[row, vec, row, row] + [pl.BlockSpec((8, LANES), lambda i: (0, 0))] * len(deps),
        out_specs=[row, row, vec],
        out_shape=[jax.ShapeDtypeStruct((t, d), F32), jax.ShapeDtypeStruct((t, d), BF16),
                   jax.ShapeDtypeStruct((1, d), F32)],
        compiler_params=_cp(("arbitrary",)))(x, g, dy, res, *deps)


def _final_loss(h, g, tgt):
    t, d = h.shape
    tb = _rows(t, 256)

    def body(h_ref, g_ref, t_ref, loss_ref, dh_ref, dh16_ref, dg_ref):
        i = pl.program_id(0)
        hv = h_ref[...]
        gv = g_ref[...]
        r = lax.rsqrt(jnp.mean(hv * hv, axis=-1, keepdims=True) + EPS)
        y = hv * r * gv
        diff = y - t_ref[...]
        lpart = jnp.sum(jnp.sum(diff * diff, axis=1, keepdims=True), axis=0, keepdims=True) * (0.5 / d)
        dy = diff * (1.0 / d)
        u = dy * gv
        dh = r * u - hv * (r * r * r * jnp.mean(u * hv, axis=-1, keepdims=True))
        dh_ref[...] = dh
        dh16_ref[...] = dh.astype(BF16)
        gpart = jnp.sum(dy * hv * r, axis=0, keepdims=True)
        lrow = jnp.broadcast_to(lpart, (1, LANES))

        @pl.when(i == 0)
        def _():
            loss_ref[...] = lrow
            dg_ref[...] = gpart

        @pl.when(i > 0)
        def _():
            loss_ref[...] += lrow
            dg_ref[...] += gpart

    row = pl.BlockSpec((tb, d), lambda i: (i, 0))
    vec = pl.BlockSpec((1, d), lambda i: (0, 0))
    return pl.pallas_call(
        body, name="final_loss", grid=(t // tb,), in_specs=[row, vec, row],
        out_specs=[pl.BlockSpec((1, LANES), lambda i: (0, 0)), row, row, vec],
        out_shape=[jax.ShapeDtypeStruct((1, LANES), F32), jax.ShapeDtypeStruct((t, d), F32),
                   jax.ShapeDtypeStruct((t, d), BF16), jax.ShapeDtypeStruct((1, d), F32)],
        compiler_params=_cp(("arbitrary",)))(h, g, tgt)


def _rope_tables(t):
    pos = jnp.arange(t, dtype=F32)
    inv = 1.0 / (ROPE_THETA ** (jnp.arange(0, ROT_DIM, 2, dtype=F32) / ROT_DIM))
    ang = pos[:, None] * inv[None, :]
    cos, sin = jnp.cos(ang), jnp.sin(ang)
    half = ROT_DIM // 2
    rest = HEAD_DIM - ROT_DIM
    c = jnp.concatenate([cos, cos, jnp.ones((t, rest), F32)], axis=1)
    s1 = jnp.concatenate([-sin, jnp.zeros((t, half + rest), F32)], axis=1)
    s2 = jnp.concatenate([jnp.zeros((t, half), F32), sin, jnp.zeros((t, rest), F32)], axis=1)
    return jnp.concatenate([jnp.tile(v, (1, LANES // HEAD_DIM)) for v in (c, s1, s2)], axis=1)


def _split_tables(tab):
    return tab[:, :LANES], tab[:, LANES:2 * LANES], tab[:, 2 * LANES:]


def _rope(x, c, s1, s2):
    half = ROT_DIM // 2
    return x * c + pltpu.roll(x, LANES - half, 1) * s1 + pltpu.roll(x, half, 1) * s2


def _rope_t(g, c, s1, s2):
    half = ROT_DIM // 2
    return g * c + pltpu.roll(g * s1, half, 1) + pltpu.roll(g * s2, LANES - half, 1)


def _band_masks(i, heads):
    n = heads * ATTN_BLOCK
    q = jnp.bitwise_and(_iota((n, ATTN_BLOCK), 0), ATTN_BLOCK - 1)
    j = _iota((n, ATTN_BLOCK), 1)
    upper = j > q
    return upper, upper & (j < jnp.where(i > 0, 0, ATTN_BLOCK))


def _fold_band(full, upper):
    return jnp.where(upper, full[:, :ATTN_BLOCK], full[:, ATTN_BLOCK:])


def _unfold_band(band, upper):
    return jnp.concatenate([jnp.where(upper, band, 0.0), jnp.where(upper, 0.0, band)], axis=1)


def _half_masks():
    lane = _iota((1, LANES), 1)
    return [(lane < HEAD_DIM).astype(F32), (lane >= HEAD_DIM).astype(F32)]


def _stack_heads(blocks, hm, j):
    pieces = []
    for r in range(4):
        qb, half = (4 * j + r) // 2, (4 * j + r) % 2
        piece = blocks[qb] * hm[half]
        if half != j:
            piece = pltpu.roll(piece, HEAD_DIM, 1)
        pieces.append(piece)
    return jnp.concatenate(pieces, axis=0)


def _unstack_heads(stacked, j):
    out = []
    for qb in (2 * j, 2 * j + 1):
        acc = None
        for half in range(2):
            r = 2 * qb + half - 4 * j
            piece = stacked[r * ATTN_BLOCK:(r + 1) * ATTN_BLOCK]
            if half != j:
                piece = pltpu.roll(piece, HEAD_DIM, 1)
            acc = piece if acc is None else acc + piece
        out.append((qb, acc))
    return out


def _sink_column(sink_ref, base):
    return jnp.concatenate([jnp.full((ATTN_BLOCK, 1), sink_ref[base + r], F32) for r in range(4)], axis=0)


def _attn_specs(nb_clamp):
    blk = ATTN_BLOCK
    kb, vb = O_K // LANES, O_V // LANES

    def cur(i):
        return jnp.minimum(i, nb_clamp)

    def prev(i):
        return jnp.maximum(jnp.minimum(i, nb_clamp + 1) - 1, 0)

    q = pl.BlockSpec((blk, 512), lambda p, i: (cur(i), p))
    kc = pl.BlockSpec((blk, LANES), lambda p, i: (cur(i), kb + p))
    kp = pl.BlockSpec((blk, LANES), lambda p, i: (prev(i), kb + p))
    vc = pl.BlockSpec((blk, LANES), lambda p, i: (cur(i), vb + p))
    vp = pl.BlockSpec((blk, LANES), lambda p, i: (prev(i), vb + p))
    tc = pl.BlockSpec((blk, 3 * LANES), lambda p, i: (cur(i), 0))
    tp = pl.BlockSpec((blk, 3 * LANES), lambda p, i: (prev(i), 0))
    return q, kc, kp, vc, vp, tc, tp


def _attn_fwd(proj, sinks, tables):
    t = proj.shape[0]
    nb = t // ATTN_BLOCK
    scale = HEAD_DIM ** -0.5

    def body(sink_ref, q_ref, kc_ref, kp_ref, vc_ref, vp_ref, tc_ref, tp_ref, o_ref):
        p = pl.program_id(0)
        i = pl.program_id(1)
        cc, s1c, s2c = _split_tables(tc_ref[...])
        kband = jnp.concatenate([_rope(kp_ref[...], *_split_tables(tp_ref[...])),
                                 _rope(kc_ref[...], cc, s1c, s2c)], axis=0).astype(BF16)
        vband = jnp.concatenate([vp_ref[...], vc_ref[...]], axis=0)
        hm = _half_masks()
        vsel = [(vband * hm[j]).astype(BF16) for j in range(2)]
        upper, dropped = _band_masks(i, 1)
        qr = [_rope(q_ref[:, qb * LANES:(qb + 1) * LANES], cc, s1c, s2c) for qb in range(4)]

        def scores(hh):
            qb, half, j = hh // 2, hh % 2, hh // 4
            qs = qr[qb] * hm[half]
            if half != j:
                qs = pltpu.roll(qs, HEAD_DIM, 1)
            return _dot(qs, kband, 'nt')

        ahead = scores(0)
        acc = None
        for hh in range(8):
            qb, half, j = hh // 2, hh % 2, hh // 4
            raw = ahead
            if hh + 1 < 8:
                ahead = scores(hh + 1)
            s = jnp.where(dropped, NEG, _fold_band(raw, upper) * scale)
            sink = sink_ref[p * 8 + hh]
            m = jnp.maximum(jnp.max(s, axis=1, keepdims=True), sink)
            pe = jnp.exp(s - m)
            den = jnp.sum(pe, axis=1, keepdims=True) + jnp.exp(sink - m)
            o = _dot(_unfold_band(pe / den, upper), vsel[j])
            if half != j:
                o = pltpu.roll(o, HEAD_DIM, 1)
            acc = o if half == 0 else acc + o
            if half == 1:
                o_ref[:, qb * LANES:(qb + 1) * LANES] = acc

    q, kc, kp, vc, vp, tc, tp = _attn_specs(nb - 1)
    smem = pl.BlockSpec(memory_space=pltpu.SMEM)
    return pl.pallas_call(
        body, name="attn_fwd", grid=(4, nb),
        in_specs=[smem, q, kc, kp, vc, vp, tc, tp],
        out_specs=pl.BlockSpec((ATTN_BLOCK, 512), lambda p, i: (i, p)),
        out_shape=jax.ShapeDtypeStruct((t, ATTN_WIDTH), F32),
        compiler_params=_cp(("parallel", "arbitrary")))(sinks, proj, proj, proj, proj, proj, tables, tables)


def _attn_bwd(proj, sinks, tables, dout):
    t = proj.shape[0]
    nb = t // ATTN_BLOCK
    scale = HEAD_DIM ** -0.5

    def body(sink_ref, q_ref, kc_ref, kp_ref, vc_ref, vp_ref, tc_ref, tp_ref,
             do_ref, dq_ref, dk_ref, dv_ref, ds_ref, carry_k, carry_v):
        p = pl.program_id(0)
        i = pl.program_id(1)
        ptab = _split_tables(tp_ref[...])

        @pl.when(i == 0)
        def _():
            carry_k[...] = jnp.zeros_like(carry_k)
            carry_v[...] = jnp.zeros_like(carry_v)
            ds_ref[...] = jnp.zeros_like(ds_ref)

        @pl.when(i < nb)
        def _():
            cc, s1c, s2c = _split_tables(tc_ref[...])
            kband = jnp.concatenate([_rope(kp_ref[...], *ptab), _rope(kc_ref[...], cc, s1c, s2c)], axis=0)
            vband = jnp.concatenate([vp_ref[...], vc_ref[...]], axis=0)
            hm = _half_masks()
            kband16 = kband.astype(BF16)
            vband16 = vband.astype(BF16)
            upper, dropped = _band_masks(i, 4)
            dkb = jnp.zeros((2 * ATTN_BLOCK, LANES), F32)
            dvb = jnp.zeros((2 * ATTN_BLOCK, LANES), F32)
            row8 = _iota((8, LANES), 0)
            dsink = jnp.zeros((8, LANES), F32)
            qr = [_rope(q_ref[:, qb * LANES:(qb + 1) * LANES], cc, s1c, s2c) for qb in range(4)]
            dob = [do_ref[:, qb * LANES:(qb + 1) * LANES] for qb in range(4)]
            for j in range(2):
                qst = _stack_heads(qr, hm, j).astype(BF16)
                dost = _stack_heads(dob, hm, j).astype(BF16)
                s = jnp.where(dropped, NEG, _fold_band(_dot(qst, kband16, 'nt'), upper) * scale)
                sink = _sink_column(sink_ref, p * 8 + 4 * j)
                m = jnp.maximum(jnp.max(s, axis=1, keepdims=True), sink)
                pe = jnp.exp(s - m)
                psink = jnp.exp(sink - m)
                den = jnp.sum(pe, axis=1, keepdims=True) + psink
                pr = pe / den
                dvb = dvb + _dot(_unfold_band(pr, upper).T, dost)
                dp = _fold_band(_dot(dost, vband16, 'nt'), upper)
                delta = jnp.sum(pr * dp, axis=1, keepdims=True)
                dsc = _unfold_band(pr * (dp - delta) * scale, upper)
                dsk = psink / den * delta
                for r in range(4):
                    part = jnp.sum(dsk[r * ATTN_BLOCK:(r + 1) * ATTN_BLOCK])
                    dsink = dsink + jnp.where(row8 == 4 * j + r, -part, 0.0)
                for qb, dqb in _unstack_heads(_dot(dsc, kband * hm[j]), j):
                    dq_ref[:, qb * LANES:(qb + 1) * LANES] = _rope_t(dqb, cc, s1c, s2c).astype(BF16)
                dkb = dkb + _dot(dsc.T, qst)
            ds_ref[0] += dsink
            dk_ref[...] = _rope_t(carry_k[...] + dkb[:ATTN_BLOCK], *ptab).astype(BF16)
            dv_ref[...] = (carry_v[...] + dvb[:ATTN_BLOCK]).astype(BF16)
            carry_k[...] = dkb[ATTN_BLOCK:]
            carry_v[...] = dvb[ATTN_BLOCK:]

        @pl.when(i == nb)
        def _():
            dk_ref[...] = _rope_t(carry_k[...], *ptab).astype(BF16)
            dv_ref[...] = carry_v[...].astype(BF16)

    q, kc, kp, vc, vp, tc, tp = _attn_specs(nb - 1)
    smem = pl.BlockSpec(memory_space=pltpu.SMEM)
    qblk = pl.BlockSpec((ATTN_BLOCK, 512), lambda p, i: (jnp.minimum(i, nb - 1), p))
    kvout = pl.BlockSpec((ATTN_BLOCK, LANES), lambda p, i: (jnp.maximum(i - 1, 0), p))
    return pl.pallas_call(
        body, name="attn_bwd", grid=(4, nb + 1),
        in_specs=[smem, q, kc, kp, vc, vp, tc, tp, qblk],
        out_specs=[qblk, kvout, kvout, pl.BlockSpec((1, 8, LANES), lambda p, i: (p, 0, 0))],
        out_shape=[jax.ShapeDtypeStruct((t, ATTN_WIDTH), BF16), jax.ShapeDtypeStruct((t, KV_WIDTH), BF16),
                   jax.ShapeDtypeStruct((t, KV_WIDTH), BF16), jax.ShapeDtypeStruct((4, 8, LANES), F32)],
        scratch_shapes=[pltpu.VMEM((ATTN_BLOCK, LANES), F32), pltpu.VMEM((ATTN_BLOCK, LANES), F32)],
        compiler_params=_cp(("parallel", "arbitrary")))(sinks, proj, proj, proj, proj, proj, tables, tables, dout)


def _shift_rows(x, prev8, j):
    n, c = x.shape
    r = pltpu.roll(x.reshape(n // 8, 8, c), j, 1)
    before = pltpu.roll(prev8, j, 0)[None]
    if n > 8:
        before = jnp.concatenate([before, r[:-1]], axis=0)
    return jnp.where(_iota((1, 8, 1), 1) < j, before, r).reshape(n, c)


def _shift_rows_up(x, next8, j):
    n, c = x.shape
    r = pltpu.roll(x.reshape(n // 8, 8, c), 8 - j, 1)
    after = pltpu.roll(next8, 8 - j, 0)[None]
    if n > 8:
        after = jnp.concatenate([r[1:], after], axis=0)
    return jnp.where(_iota((1, 8, 1), 1) >= 8 - j, after, r).reshape(n, c)


def _conv_apply(x, prev8, w, b, taps):
    u = b + x * w[taps - 1:taps]
    for j in range(1, taps):
        u = u + _shift_rows(x, prev8, j) * w[taps - 1 - j:taps - j]
    return u


def _conv_grads(du, du_next8, x, w, taps):
    dx = du * w[taps - 1:taps]
    rowk = _iota((taps, 1), 0)
    dw = jnp.where(rowk == taps - 1, jnp.sum(du * x, axis=0, keepdims=True), 0.0)
    for j in range(1, taps):
        ahead = _shift_rows_up(du, du_next8, j)
        dx = dx + ahead * w[taps - 1 - j:taps - j]
        dw = dw + jnp.where(rowk == taps - 1 - j, jnp.sum(ahead * x, axis=0, keepdims=True), 0.0)
    return dx, dw, jnp.sum(du, axis=0, keepdims=True)


def _conv_specs(tb, tc, col0, t):
    c0 = col0 // tc
    cur = pl.BlockSpec((tb, tc), lambda j, i: (i, c0 + j))
    prev = pl.BlockSpec((8, tc), lambda j, i: (jnp.maximum(i * (tb // 8) - 1, 0), c0 + j))
    nxt = pl.BlockSpec((8, tc), lambda j, i: (jnp.minimum((i + 1) * (tb // 8), t // 8 - 1), c0 + j))
    return cur, prev, nxt


def _conv_silu_fwd(x, w, b, *, col0, width, name):
    t = x.shape[0]
    taps = w.shape[0]
    tb, tc = _rows(t, 512), _tile(width, 1024)
    assert col0 % tc == 0

    def body(x_ref, xp_ref, w_ref, b_ref, o_ref, u_ref):
        i = pl.program_id(1)
        prev8 = jnp.where(i > 0, xp_ref[...], 0.0)
        u = _conv_apply(x_ref[...], prev8, w_ref[...], b_ref[...], taps)
        u_ref[...] = u
        o_ref[...] = u * _sigmoid(u)

    cur, prev, _ = _conv_specs(tb, tc, col0, t)
    par = pl.BlockSpec((taps, tc), lambda j, i: (0, j))
    bias = pl.BlockSpec((1, tc), lambda j, i: (0, j))
    out = pl.BlockSpec((tb, tc), lambda j, i: (i, j))
    shp = jax.ShapeDtypeStruct((t, width), F32)
    return pl.pallas_call(
        body, name=name, grid=(width // tc, t // tb), in_specs=[cur, prev, par, bias], out_specs=[out, out],
        out_shape=[shp, shp], compiler_params=_cp(("parallel", "parallel")))(x, x, w, b)


def _dsilu(u):
    sg = _sigmoid(u)
    return sg * (1.0 + u * (1.0 - sg))


def _ssd_conv_bwd(x, w, dxs, dbm, dcm, *, col0, name):
    t = x.shape[0]
    taps = w.shape[0]
    tb, tc = _rows(t, 512), BC_WIDTH
    nrow, ncol = t // tb, CONV_CH // tc
    c0 = col0 // tc

    def body(x_ref, w_ref, xs_ref, xsn_ref, bm_ref, bmn_ref, cm_ref, cmn_ref, dx_ref, dw_ref, db_ref):
        i = pl.program_id(0)
        j = pl.program_id(1)

        def run(du_ref, dun_ref):
            next8 = jnp.where(i < nrow - 1, dun_ref[...], 0.0)
            dx, dwv, dbv = _conv_grads(du_ref[...], next8, x_ref[...], w_ref[...], taps)
            dx_ref[...] = dx.astype(BF16)

            @pl.when(i == 0)
            def _():
                dw_ref[j] = dwv
                db_ref[j] = dbv

            @pl.when(i > 0)
            def _():
                dw_ref[j] += dwv
                db_ref[j] += dbv

        pl.when(j < 2)(lambda: run(xs_ref, xsn_ref))
        pl.when(j == 2)(lambda: run(bm_ref, bmn_ref))
        pl.when(j == 3)(lambda: run(cm_ref, cmn_ref))

    def nxt_row(i):
        return jnp.minimum((i + 1) * (tb // 8), t // 8 - 1)

    xs_col = lambda j: jnp.minimum(j, SSD_INNER // tc - 1)
    in_specs = [pl.BlockSpec((tb, tc), lambda i, j: (i, c0 + j)), pl.BlockSpec((taps, tc), lambda i, j: (0, j)),
                pl.BlockSpec((tb, tc), lambda i, j: (i, xs_col(j))),
                pl.BlockSpec((8, tc), lambda i, j: (nxt_row(i), xs_col(j))),
                pl.BlockSpec((tb, tc), lambda i, j: (i, 0)), pl.BlockSpec((8, tc), lambda i, j: (nxt_row(i), 0)),
                pl.BlockSpec((tb, tc), lambda i, j: (i, 0)), pl.BlockSpec((8, tc), lambda i, j: (nxt_row(i), 0))]
    dx, dw, db = pl.pallas_call(
        body, name=name, grid=(nrow, ncol), in_specs=in_specs,
        out_specs=[pl.BlockSpec((tb, tc), lambda i, j: (i, j)),
                   pl.BlockSpec((ncol, taps, tc), lambda i, j: (0, 0, 0)),
                   pl.BlockSpec((ncol, 1, tc), lambda i, j: (0, 0, 0))],
        out_shape=[jax.ShapeDtypeStruct((t, CONV_CH), BF16), jax.ShapeDtypeStruct((ncol, taps, tc), F32),
                   jax.ShapeDtypeStruct((ncol, 1, tc), F32)],
        compiler_params=_cp(("arbitrary", "arbitrary")))(x, w, dxs, dxs, dbm, dbm, dcm, dcm)
    return dx, dw.transpose(1, 0, 2).reshape(taps, CONV_CH), db.transpose(1, 0, 2).reshape(1, CONV_CH)


def _ffn_specs(tb, tc, t):
    nc = D_FF // tc

    def cur(half):
        return pl.BlockSpec((tb, tc), lambda j, i: (i, half * nc + j))

    def prev(half):
        return pl.BlockSpec((8, tc), lambda j, i: (jnp.maximum(i * (tb // 8) - 1, 0), half * nc + j))

    def nxt(half):
        return pl.BlockSpec((8, tc), lambda j, i: (jnp.minimum((i + 1) * (tb // 8), t // 8 - 1), half * nc + j))

    def par(rows, half):
        return pl.BlockSpec((rows, tc), lambda j, i: (0, half * nc + j))

    return cur, prev, nxt, par


def _ffn_act_fwd(u0, w, b):
    t = u0.shape[0]
    tb, tc = _rows(t, 512), _tile(D_FF, 1408)
    cur, prev, _, par = _ffn_specs(tb, tc, t)

    def body(g_ref, gp_ref, v_ref, vp_ref, wg_ref, wv_ref, bg_ref, bv_ref, o_ref, u_ref):
        i = pl.program_id(1)
        ug = _conv_apply(g_ref[...], jnp.where(i > 0, gp_ref[...], 0.0), wg_ref[...], bg_ref[...], FFN_CONV)
        uv = _conv_apply(v_ref[...], jnp.where(i > 0, vp_ref[...], 0.0), wv_ref[...], bv_ref[...], FFN_CONV)
        o_ref[...] = (ug * _sigmoid(ug) * uv).astype(BF16)
        u_ref[0] = ug
        u_ref[1] = uv

    return pl.pallas_call(
        body, name="ffn_act_fwd", grid=(D_FF // tc, t // tb),
        in_specs=[cur(0), prev(0), cur(1), prev(1), par(FFN_CONV, 0), par(FFN_CONV, 1), par(1, 0), par(1, 1)],
        out_specs=[pl.BlockSpec((tb, tc), lambda j, i: (i, j)), pl.BlockSpec((2, tb, tc), lambda j, i: (0, i, j))],
        out_shape=[jax.ShapeDtypeStruct((t, D_FF), BF16), jax.ShapeDtypeStruct((2, t, D_FF), F32)],
        compiler_params=_cp(("parallel", "parallel")))(u0, u0, u0, u0, w, w, b, b)


def _ffn_act_bwd(u0, u, w, da):
    t = u0.shape[0]
    tb, tc = _rows(t, 256), _tile(D_FF, 1408)
    nrow = t // tb
    taps = FFN_CONV
    cur, _, _, par = _ffn_specs(tb, tc, t)

    def dact(ug, uv, dav):
        sg = _sigmoid(ug)
        return dav * uv * (sg * (1.0 + ug * (1.0 - sg))), dav * ug * sg

    def body(g_ref, v_ref, u_ref, un_ref, wg_ref, wv_ref, da_ref, dan_ref, dx_ref, dw_ref, db_ref):
        i = pl.program_id(1)
        dug, duv = dact(u_ref[0], u_ref[1], da_ref[...].astype(F32))
        dan = jnp.where(i < nrow - 1, dan_ref[...].astype(F32)[:8], 0.0)
        dugn, duvn = dact(un_ref[0], un_ref[1], dan)
        dxg, dwg, dbg = _conv_grads(dug, dugn, g_ref[...], wg_ref[...], taps)
        dxv, dwv, dbv = _conv_grads(duv, duvn, v_ref[...], wv_ref[...], taps)
        dx_ref[0] = dxg.astype(BF16)
        dx_ref[1] = dxv.astype(BF16)

        @pl.when(i == 0)
        def _():
            dw_ref[0] = dwg
            dw_ref[1] = dwv
            db_ref[0] = dbg
            db_ref[1] = dbv

        @pl.when(i > 0)
        def _():
            dw_ref[0] += dwg
            dw_ref[1] += dwv
            db_ref[0] += dbg
            db_ref[1] += dbv

    both = pl.BlockSpec((2, tb, tc), lambda j, i: (0, i, j))
    both_nxt = pl.BlockSpec((2, 8, tc), lambda j, i: (0, jnp.minimum((i + 1) * (tb // 8), t // 8 - 1), j))
    da_cur = pl.BlockSpec((tb, tc), lambda j, i: (i, j))
    da_nxt = pl.BlockSpec((16, tc), lambda j, i: (jnp.minimum((i + 1) * (tb // 16), t // 16 - 1), j))
    return pl.pallas_call(
        body, name="ffn_act_bwd", grid=(D_FF // tc, nrow),
        in_specs=[cur(0), cur(1), both, both_nxt, par(taps, 0), par(taps, 1), da_cur, da_nxt],
        out_specs=[both, pl.BlockSpec((2, taps, tc), lambda j, i: (0, 0, j)),
                   pl.BlockSpec((2, 1, tc), lambda j, i: (0, 0, j))],
        out_shape=[jax.ShapeDtypeStruct((2, t, D_FF), BF16), jax.ShapeDtypeStruct((2, taps, D_FF), F32),
                   jax.ShapeDtypeStruct((2, 1, D_FF), F32)],
        compiler_params=_cp(("parallel", "arbitrary")))(u0, u0, u, u, w, w, da, da)


def _head_masks():
    lane = _iota((1, 4 * SSD_HEAD_DIM), 1)
    return [((lane >= r * SSD_HEAD_DIM) & (lane < (r + 1) * SSD_HEAD_DIM)).astype(F32) for r in range(4)]


def _segsum(v):
    first = _iota((1, LANES), 1) < SSD_HEAD_DIM
    halves = []
    for k in range(2):
        vh = v[:, k * LANES:(k + 1) * LANES]
        both = jnp.sum(vh, axis=1, keepdims=True)
        one = jnp.sum(jnp.where(first, vh, 0.0), axis=1, keepdims=True)
        halves.append(jnp.where(first, one, both - one))
    return jnp.concatenate(halves, axis=1)


def _ssd_common(raw_e, prow, rawr4, bcol, acol):
    n = SSD_CHUNK
    dt_e = _softplus(raw_e + prow[0:1, :])
    a_e = -jnp.exp(prow[1:2, :])
    d_e = prow[2:3, :]
    tril = (_iota((n, n), 0) >= _iota((n, n), 1)).astype(F32)
    acs_e = _dot_exact(tril, dt_e * a_e)
    last_e = acs_e[n - 1:n, :]
    dtr4 = _softplus(rawr4 + bcol)
    triu = (_iota((n, n), 0) <= _iota((n, n), 1)).astype(F32)
    acs_r4 = _dot_exact(dtr4 * (-jnp.exp(acol)), triu)
    return dt_e, a_e, d_e, acs_e, last_e, acs_r4


def _decay_matrix(acs_e, acs_r4, r):
    n = SSD_CHUNK
    col = acs_e[:, r * SSD_HEAD_DIM:r * SSD_HEAD_DIM + 1]
    seg = col - acs_r4[r:r + 1, :]
    causal = _iota((n, n), 0) >= _iota((n, n), 1)
    return jnp.exp(jnp.where(causal, seg, NEG))


SSD_STEP_CHUNKS = 4
SSD_ROWS = SSD_STEP_CHUNKS * SSD_CHUNK


def _ssd_specs(t, rev):
    nb = t // SSD_ROWS
    xb, bb, cb = 0, SSD_INNER // SSD_STATE, (SSD_INNER + BC_WIDTH) // SSD_STATE

    def ch(c):
        return (nb - 1 - c) if rev else c

    x = pl.BlockSpec((SSD_ROWS, 256), lambda g, c: (ch(c), xb + g))
    bm = pl.BlockSpec((SSD_ROWS, SSD_STATE), lambda g, c: (ch(c), bb + g))
    cm = pl.BlockSpec((SSD_ROWS, SSD_STATE), lambda g, c: (ch(c), cb + g))
    dtc = pl.BlockSpec((1, SSD_ROWS, 256), lambda g, c: (g, ch(c), 0))
    dtr = pl.BlockSpec((1, 4, SSD_ROWS), lambda g, c: (g, 0, ch(c)))
    prow = pl.BlockSpec((1, 3, 256), lambda g, c: (g, 0, 0))
    pcol = pl.BlockSpec((1, 4, 1), lambda g, c: (g, 0, 0))
    st = pl.BlockSpec((1, SSD_STEP_CHUNKS, SSD_STATE, 256), lambda g, c: (g, ch(c), 0, 0))
    return x, bm, cm, dtc, dtr, prow, pcol, st, ch


def _ssd_params(dt_raw, dt_bias, a_log, ssd_d):
    t = dt_raw.shape[0]
    by_group = dt_raw.reshape(t, SSD_GROUPS, 4)
    dtc = jnp.repeat(by_group, SSD_HEAD_DIM, axis=2).transpose(1, 0, 2)
    dtr = by_group.transpose(1, 2, 0)
    prow = jnp.repeat(jnp.stack([dt_bias.reshape(SSD_GROUPS, 4), a_log.reshape(SSD_GROUPS, 4),
                                 ssd_d.reshape(SSD_GROUPS, 4)], axis=1), SSD_HEAD_DIM, axis=2)
    bcol = dt_bias.reshape(SSD_GROUPS, 4, 1)
    acol = a_log.reshape(SSD_GROUPS, 4, 1)
    return dtc, dtr, prow, bcol, acol


def _ssd_fwd(xbc, params):
    t = xbc.shape[0]
    nc = t // SSD_CHUNK
    dtc, dtr, prow, bcol, acol = params

    def body(x_ref, b_ref, c_ref, dtc_ref, dtr_ref, prow_ref, bcol_ref, acol_ref, y_ref, st_ref, s_scr):
        c = pl.program_id(1)

        @pl.when(c == 0)
        def _():
            s_scr[...] = jnp.zeros_like(s_scr)

        masks = _head_masks()
        s = s_scr[...]
        for k in range(SSD_STEP_CHUNKS):
            rows = slice(k * SSD_CHUNK, (k + 1) * SSD_CHUNK)
            dt_e, a_e, d_e, acs_e, last_e, acs_r4 = _ssd_common(
                dtc_ref[0, rows], prow_ref[0], dtr_ref[0][:, rows], bcol_ref[0], acol_ref[0])
            xv = x_ref[rows]
            bm, cm = b_ref[rows], c_ref[rows]
            st_ref[0, k] = s
            xdt = xv * dt_e
            cb = _dot(cm, bm, 'nt')
            y = _dot(cm, s) * jnp.exp(acs_e) + xv * d_e
            for r in range(4):
                mr = cb * _decay_matrix(acs_e, acs_r4, r)
                y = y + _dot(mr, xdt * masks[r])
            y_ref[rows] = y
            w = xdt * jnp.exp(last_e - acs_e)
            s = s * jnp.exp(last_e) + _dot(bm.T, w)
        s_scr[...] = s

    x, bm, cm, dtcs, dtrs, prs, pcs, st, _ = _ssd_specs(t, False)
    return pl.pallas_call(
        body, name="ssd_fwd", grid=(SSD_GROUPS, t // SSD_ROWS), in_specs=[x, bm, cm, dtcs, dtrs, prs, pcs, pcs],
        out_specs=[pl.BlockSpec((SSD_ROWS, 256), lambda g, c: (c, g)), st],
        out_shape=[jax.ShapeDtypeStruct((t, SSD_INNER), F32),
                   jax.ShapeDtypeStruct((SSD_GROUPS, nc, SSD_STATE, 256), F32)],
        scratch_shapes=[pltpu.VMEM((SSD_STATE, 256), F32)],
        compiler_params=_cp(("parallel", "arbitrary")))(xbc, xbc, xbc, dtc, dtr, prow, bcol, acol)


def _ssd_bwd(xbc, pre, params, states, dy):
    t = xbc.shape[0]
    nc = t // SSD_CHUNK
    n = SSD_CHUNK
    dtc, dtr, prow, bcol, acol = params

    def body(x_ref, b_ref, c_ref, ux_ref, ub_ref, uc_ref, dtc_ref, dtr_ref, prow_ref, bcol_ref, acol_ref, st_ref,
             dy_ref, dx_ref, db_ref, dc_ref, ddt_ref, dp_ref, ds_scr):
        c = pl.program_id(1)

        @pl.when(c == 0)
        def _():
            ds_scr[...] = jnp.zeros_like(ds_scr)
            dp_ref[...] = jnp.zeros_like(dp_ref)

        masks = _head_masks()
        ds = ds_scr[...]
        for k in reversed(range(SSD_STEP_CHUNKS)):
            rows = slice(k * SSD_CHUNK, (k + 1) * SSD_CHUNK)
            raw_e = dtc_ref[0, rows]
            prw = prow_ref[0]
            dt_e, a_e, d_e, acs_e, last_e, acs_r4 = _ssd_common(raw_e, prw, dtr_ref[0][:, rows], bcol_ref[0], acol_ref[0])
            xv = x_ref[rows]
            bm, cm = b_ref[rows], c_ref[rows]
            s = st_ref[0, k]
            dyv = dy_ref[rows]
            e_e = jnp.exp(acs_e)
            dec_e = jnp.exp(last_e - acs_e)
            cd_e = jnp.exp(last_e)
            xdt = xv * dt_e
            w = xdt * dec_e
            b16, c16, s16, ds16 = bm.astype(BF16), cm.astype(BF16), s.astype(BF16), ds.astype(BF16)
            cb = _dot(c16, b16, 'nt')
            yoff_raw = _dot(c16, s16)
            dye = dyv * e_e
            dye16 = dye.astype(BF16)
            dcm = _dot(dye16, s16, 'nt')
            ds_prev = ds * cd_e + _dot(cm.T, dye16)
            dacs_e = _segsum(dyv * yoff_raw) * e_e
            dw = _dot(b16, ds16)
            dbm = _dot(w, ds16, 'nt')
            tdec = _segsum(dw * xdt) * dec_e
            dacs_e = dacs_e - tdec
            dlast_e = jnp.sum(tdec, axis=0, keepdims=True)
            dxdt = dw * dec_e
            dlast_e = dlast_e + _segsum(jnp.sum(ds * s, axis=0, keepdims=True)) * cd_e
            dcb = jnp.zeros((n, n), F32)
            for r in range(4):
                lm = _decay_matrix(acs_e, acs_r4, r)
                mr = cb * lm
                dyr16 = (dyv * masks[r]).astype(BF16)
                dm = _dot(dyr16, xdt * masks[r], 'nt')
                dcb = dcb + dm * lm
                dseg = dm * mr
                dcol = jnp.sum(dseg, axis=1, keepdims=True) - jnp.sum(dseg.T, axis=1, keepdims=True)
                dacs_e = dacs_e + dcol * masks[r]
                dxdt = dxdt + _dot(mr.T, dyr16)
            dcm = dcm + _dot(dcb, b16)
            dbm = dbm + _dot(dcb.T, c16)
            dacs_e = dacs_e + jnp.where(_iota((n, 1), 0) == n - 1, dlast_e, 0.0)
            triu = (_iota((n, n), 0) <= _iota((n, n), 1)).astype(F32)
            ddta_e = _dot_exact(triu, dacs_e)
            ddt_e = ddta_e * a_e + _segsum(dxdt * xv)
            dx_ref[rows] = (dxdt * dt_e + dyv * d_e) * _dsilu(ux_ref[rows])
            db_ref[rows] = dbm * _dsilu(ub_ref[rows])
            dc_ref[rows] = dcm * _dsilu(uc_ref[rows])
            draw_e = ddt_e * _sigmoid(raw_e + prw[0:1, :])
            draw_t = draw_e.T
            ddt_ref[0, :, rows] = jnp.concatenate([draw_t[r * SSD_HEAD_DIM:r * SSD_HEAD_DIM + 1] for r in range(4)], axis=0)
            dbias = jnp.sum(draw_e, axis=0, keepdims=True)
            dalog = jnp.sum(ddta_e * dt_e, axis=0, keepdims=True) * a_e
            dd = _segsum(jnp.sum(dyv * xv, axis=0, keepdims=True))
            row3 = _iota((3, 1), 0)
            dp_ref[0] += (jnp.where(row3 == 0, dbias, 0.0) + jnp.where(row3 == 1, dalog, 0.0)
                          + jnp.where(row3 == 2, dd, 0.0))
            ds = ds_prev
        ds_scr[...] = ds


    x, bm, cm, dtcs, dtrs, prs, pcs, st, ch = _ssd_specs(t, True)
    yblk = pl.BlockSpec((SSD_ROWS, 256), lambda g, c: (ch(c), g))
    nblk = pl.BlockSpec((SSD_ROWS, SSD_STATE), lambda g, c: (ch(c), g))
    return pl.pallas_call(
        body, name="ssd_bwd", grid=(SSD_GROUPS, t // SSD_ROWS),
        in_specs=[x, bm, cm, x, bm, cm, dtcs, dtrs, prs, pcs, pcs, st, yblk],
        out_specs=[yblk, nblk, nblk, dtrs, prs],
        out_shape=[jax.ShapeDtypeStruct((t, SSD_INNER), F32), jax.ShapeDtypeStruct((t, BC_WIDTH), F32),
                   jax.ShapeDtypeStruct((t, BC_WIDTH), F32), jax.ShapeDtypeStruct((SSD_GROUPS, 4, t), F32),
                   jax.ShapeDtypeStruct((SSD_GROUPS, 3, 256), F32)],
        scratch_shapes=[pltpu.VMEM((SSD_STATE, 256), F32)],
        compiler_params=_cp(("parallel", "arbitrary")))(xbc, xbc, xbc, pre, pre, pre, dtc, dtr, prow, bcol, acol,
                                                         states, dy)


GROUP_W = SSD_INNER // SSD_GROUPS


def _mix_specs(tb):
    row = pl.BlockSpec((tb, 2048), lambda i: (i, 0))
    zlo = pl.BlockSpec((tb, 1024), lambda i: (i, O_Z // 1024))
    zhi = pl.BlockSpec((tb, 1024), lambda i: (i, O_Z // 1024 + 1))
    vec = pl.BlockSpec((1, 2048), lambda i: (0, 0))
    return row, zlo, zhi, vec


def _mix_fwd(attn, y, proj, g_attn, g_ssd):
    t = attn.shape[0]
    tb = _rows(t, 256)

    def body(a_ref, y_ref, zlo_ref, zhi_ref, ga_ref, gs_ref, o_ref):
        av = a_ref[...]
        r = lax.rsqrt(jnp.mean(av * av, axis=-1, keepdims=True) + EPS)
        o_ref[:, :ATTN_WIDTH] = (av * r * ga_ref[...]).astype(BF16)
        for g in range(SSD_GROUPS):
            lo, hi = g * GROUP_W, (g + 1) * GROUP_W
            zref = zlo_ref if g < 4 else zhi_ref
            z = zref[:, lo % 1024:lo % 1024 + GROUP_W]
            yg = y_ref[:, lo:hi] * (z * _sigmoid(z))
            rg = lax.rsqrt(jnp.mean(yg * yg, axis=-1, keepdims=True) + EPS)
            o_ref[:, ATTN_WIDTH + lo:ATTN_WIDTH + hi] = (yg * rg * gs_ref[:, lo:hi]).astype(BF16)

    row, zlo, zhi, vec = _mix_specs(tb)
    return pl.pallas_call(
        body, name="mix_fwd", grid=(t // tb,), in_specs=[row, row, zlo, zhi, vec, vec],
        out_specs=pl.BlockSpec((tb, 4096), lambda i: (i, 0)), out_shape=jax.ShapeDtypeStruct((t, 4096), BF16),
        compiler_params=_cp(("parallel",)))(attn, y, proj, proj, g_attn, g_ssd)


def _mix_bwd(dmix, attn, y, proj, g_attn, g_ssd):
    t = attn.shape[0]
    tb = _rows(t, 256)

    def body(dm_ref, a_ref, y_ref, zlo_ref, zhi_ref, ga_ref, gs_ref, da_ref, dy_ref, dz_ref, dga_ref, dgs_ref):
        i = pl.program_id(0)
        av = a_ref[...]
        dn = dm_ref[:, :ATTN_WIDTH].astype(F32)
        r = lax.rsqrt(jnp.mean(av * av, axis=-1, keepdims=True) + EPS)
        u = dn * ga_ref[...]
        da_ref[...] = r * u - av * (r * r * r * jnp.mean(u * av, axis=-1, keepdims=True))
        dga = jnp.sum(dn * av * r, axis=0, keepdims=True)

        @pl.when(i == 0)
        def _():
            dga_ref[...] = dga

        @pl.when(i > 0)
        def _():
            dga_ref[...] += dga

        for g in range(SSD_GROUPS):
            lo, hi = g * GROUP_W, (g + 1) * GROUP_W
            zref = zlo_ref if g < 4 else zhi_ref
            z = zref[:, lo % 1024:lo % 1024 + GROUP_W]
            yv = y_ref[:, lo:hi]
            sg = _sigmoid(z)
            sz = z * sg
            yg = yv * sz
            rg = lax.rsqrt(jnp.mean(yg * yg, axis=-1, keepdims=True) + EPS)
            do = dm_ref[:, ATTN_WIDTH + lo:ATTN_WIDTH + hi].astype(F32)
            ug = do * gs_ref[:, lo:hi]
            dyg = rg * ug - yg * (rg * rg * rg * jnp.mean(ug * yg, axis=-1, keepdims=True))
            dy_ref[:, lo:hi] = dyg * sz
            dz_ref[:, lo:hi] = (dyg * yv * (sg * (1.0 + z * (1.0 - sg)))).astype(BF16)
            dgs = jnp.sum(do * yg * rg, axis=0, keepdims=True)

            @pl.when(i == 0)
            def _():
                dgs_ref[:, lo:hi] = dgs

            @pl.when(i > 0)
            def _():
                dgs_ref[:, lo:hi] += dgs

    row, zlo, zhi, vec = _mix_specs(tb)
    return pl.pallas_call(
        body, name="mix_bwd", grid=(t // tb,),
        in_specs=[pl.BlockSpec((tb, 4096), lambda i: (i, 0)), row, row, zlo, zhi, vec, vec],
        out_specs=[row, row, row, vec, vec],
        out_shape=[jax.ShapeDtypeStruct((t, 2048), F32), jax.ShapeDtypeStruct((t, 2048), F32),
                   jax.ShapeDtypeStruct((t, 2048), BF16), jax.ShapeDtypeStruct((1, 2048), F32),
                   jax.ShapeDtypeStruct((1, 2048), F32)],
        compiler_params=_cp(("arbitrary",)))(dmix, attn, y, proj, proj, g_attn, g_ssd)


def _adamw(w, g, m, v, name):
    r, c = w.shape
    tb = _rows(r, 256)
    c1 = 1.0 - ADAM_B1 ** ADAM_STEP
    c2 = 1.0 - ADAM_B2 ** ADAM_STEP

    def body(w_ref, g_ref, m_ref, v_ref, d_ref, m2_ref, v2_ref):
        gv = g_ref[...]
        m2 = ADAM_B1 * m_ref[...] + (1.0 - ADAM_B1) * gv
        v2 = ADAM_B2 * v_ref[...] + (1.0 - ADAM_B2) * (gv * gv)
        d_ref[...] = -ADAM_LR * ((m2 / c1) / (jnp.sqrt(v2 / c2) + ADAM_EPS) + ADAM_WD * w_ref[...])
        m2_ref[...] = m2
        v2_ref[...] = v2

    blk = pl.BlockSpec((tb, c), lambda i: (i, 0))
    shp = jax.ShapeDtypeStruct((r, c), F32)
    return pl.pallas_call(body, name=name, grid=(r // tb,), in_specs=[blk] * 4, out_specs=[blk] * 3,
                          out_shape=[shp] * 3, compiler_params=_cp(("parallel",)))(w, g, m, v)


def _adamw_halves(w, mine, theirs, m, v, pos, name, cols=False):
    r, c = w.shape
    h = r if cols else r // 2
    tb = _rows(h, 128)
    nh = h // tb
    c1 = 1.0 - ADAM_B1 ** ADAM_STEP
    c2 = 1.0 - ADAM_B2 ** ADAM_STEP

    def body(pos_ref, w_ref, a_ref, b_ref, m_ref, v_ref, g_ref, d_ref, m2_ref, v2_ref):
        which = pl.program_id(1) if cols else pl.program_id(0) // nh
        gv = jnp.where(which == pos_ref[0], a_ref[...], b_ref[...])
        m2 = ADAM_B1 * m_ref[...] + (1.0 - ADAM_B1) * gv
        v2 = ADAM_B2 * v_ref[...] + (1.0 - ADAM_B2) * (gv * gv)
        g_ref[...] = gv
        d_ref[...] = -ADAM_LR * ((m2 / c1) / (jnp.sqrt(v2 / c2) + ADAM_EPS) + ADAM_WD * w_ref[...])
        m2_ref[...] = m2
        v2_ref[...] = v2

    if cols:
        full = pl.BlockSpec((tb, c // 2), lambda i, j, pref: (i, j))
        mine_spec = theirs_spec = pl.BlockSpec((tb, c // 2), lambda i, j, pref: (i, 0))
        grid = (nh, 2)
    else:
        full = pl.BlockSpec((tb, c), lambda i, pref: (i, 0))
        mine_spec = pl.BlockSpec((tb, c), lambda i, pref: (jnp.where(i // nh == pref[0], i % nh,
                                                                     jnp.where(pref[0] == 0, nh - 1, 0)), 0))
        theirs_spec = pl.BlockSpec((tb, c), lambda i, pref: (jnp.where(i // nh != pref[0], i % nh,
                                                                       jnp.where(pref[0] == 0, 0, nh - 1)), 0))
        grid = (r // tb,)
    shp = jax.ShapeDtypeStruct((r, c), F32)
    grid_spec = pltpu.PrefetchScalarGridSpec(num_scalar_prefetch=1, grid=grid,
                                             in_specs=[full, mine_spec, theirs_spec, full, full],
                                             out_specs=[full] * 4)
    return pl.pallas_call(body, name=name, grid_spec=grid_spec, out_shape=[shp] * 4,
                          compiler_params=_cp(("parallel",) * len(grid)))(pos, w, mine, theirs, m, v)


def _sum_own_half(g4, recv, pos, name, cols=False):
    _, r, c = g4.shape
    h, c = (r, c // 2) if cols else (r // 2, c)
    tb = _rows(h, 128)
    nh = h // tb

    def slot(j, pref):
        return (pref[1] + 1 + j) % N_CHIPS

    if cols:
        own = lambda j, i, pref: (slot(j, pref), i, pref[0])
    else:
        own = lambda j, i, pref: (slot(j, pref), pref[0] * nh + i, 0)
    same = lambda j, i, pref: (slot(j, pref), i, 0)

    def body(pos_ref, a_ref, b_ref, o_ref):
        o_ref[...] = (a_ref[...] + b_ref[...]).astype(BF16)

    grid_spec = pltpu.PrefetchScalarGridSpec(
        num_scalar_prefetch=1, grid=(N_CHIPS - 1, nh),
        in_specs=[pl.BlockSpec((1, tb, c), own), pl.BlockSpec((1, tb, c), same)],
        out_specs=pl.BlockSpec((1, tb, c), same))
    return pl.pallas_call(body, name=name, grid_spec=grid_spec,
                          out_shape=jax.ShapeDtypeStruct((N_CHIPS, h, c), BF16),
                          compiler_params=_cp(("parallel", "parallel")))(pos, g4, recv)


def _sum_chips(g4, recv, parts, pos, name, cols=False):
    _, r, c = g4.shape
    h, c = (r, c // 2) if cols else (r // 2, c)
    tb = _rows(h, 128)
    nh = h // tb
    own = (lambda i, pref: (pref[1], i, pref[0])) if cols else (lambda i, pref: (pref[1], pref[0] * nh + i, 0))

    def body(pos_ref, a_ref, b_ref, p_ref, o_ref):
        own = a_ref[0] + b_ref[0]
        o_ref[...] = ((own + p_ref[0].astype(F32)) + p_ref[1].astype(F32)) + p_ref[2].astype(F32)

    grid_spec = pltpu.PrefetchScalarGridSpec(
        num_scalar_prefetch=1, grid=(nh,),
        in_specs=[pl.BlockSpec((1, tb, c), own),
                  pl.BlockSpec((1, tb, c), lambda i, pref: (pref[1], i, 0)),
                  pl.BlockSpec((3, tb, c), lambda i, pref: (0, i, 0))],
        out_specs=pl.BlockSpec((tb, c), lambda i, pref: (i, 0)))
    return pl.pallas_call(body, name=name, grid_spec=grid_spec, out_shape=jax.ShapeDtypeStruct((h, c), F32),
                          compiler_params=_cp(("parallel",)))(pos, g4, recv, parts)


def _me():
    return lax.axis_index("x"), lax.axis_index("y"), lax.axis_index("c")


def _flip(v, bit):
    return (1 - v) if bit else v


CHIP_FLIPS = [(1, 0), (0, 1), (1, 1)]


def _forward_halves(gathered):
    def body(g_ref, o_ref, token, send_sems, recv_sems):
        x, y, c = _me()
        h = g_ref.shape[2] // 2
        cps = []
        for k, (fx, fy) in enumerate(CHIP_FLIPS):
            peer_chip = 2 * _flip(x, fx) + _flip(y, fy)
            mine = o_ref.at[peer_chip, :, pl.ds(c * h, h)]
            cp = pltpu.make_async_remote_copy(src_ref=mine, dst_ref=mine, send_sem=send_sems.at[k],
                                              recv_sem=recv_sems.at[k], device_id=(x, y, 1 - c), device_id_type=MESH)
            cp.start()
            cps.append(cp)
        for k, (fx, fy) in enumerate(CHIP_FLIPS):
            peer_chip = 2 * _flip(x, fx) + _flip(y, fy)
            theirs = o_ref.at[peer_chip, :, pl.ds((1 - c) * h, h)]
            pltpu.make_async_remote_copy(src_ref=theirs, dst_ref=theirs, send_sem=send_sems.at[k],
                                         recv_sem=recv_sems.at[k], device_id=(x, y, 1 - c),
                                         device_id_type=MESH).wait_recv()
        for cp in cps:
            cp.wait_send()
        token[...] = jnp.zeros_like(token)

    return pl.pallas_call(
        body, name="gather_forward_w_in", in_specs=[HBM_SPEC],
        out_specs=[HBM_SPEC, pl.BlockSpec(memory_space=pltpu.VMEM)],
        out_shape=[jax.ShapeDtypeStruct(gathered.shape, gathered.dtype), TOKEN],
        scratch_shapes=[pltpu.SemaphoreType.DMA((3,)), pltpu.SemaphoreType.DMA((3,))],
        input_output_aliases={0: 0},
        compiler_params=pltpu.CompilerParams(has_side_effects=True))(gathered)


SEM_SPEC = pl.BlockSpec(memory_space=pltpu.SEMAPHORE)
ANY_SPEC = pl.BlockSpec(memory_space=pl.ANY)
DATAFLOW = pltpu.SideEffectType.DATAFLOW_SIDE_EFFECTING


def _in_hbm(a):
    return pltpu.with_memory_space_constraint(a, pltpu.HBM)


def _push_start(srcs, land_shapes, route, peers, name):
    n, npeer = len(srcs), len(peers)
    lands = [lax.empty(shp, s.dtype) for shp, s in zip(land_shapes, srcs)]

    def body(*refs):
        ins, lnd = refs[:n], refs[n:2 * n]
        send_sems, recv_sems = refs[2 * n], refs[2 * n + 1]
        token = refs[-1]
        x, y, c = _me()
        for t in range(n):
            for k, (fx, fy, fc) in enumerate(peers):
                src, dst = route(ins[t], lnd[t], k, x, y, c)
                pltpu.make_async_remote_copy(
                    src_ref=src, dst_ref=dst, send_sem=send_sems.at[npeer * t + k],
                    recv_sem=recv_sems.at[npeer * t + k],
                    device_id=(_flip(x, fx), _flip(y, fy), _flip(c, fc)), device_id_type=MESH).start()
        token[...] = jnp.zeros_like(token)

    bufs = [_in_hbm(a) for a in list(srcs) + lands]
    outs = pl.pallas_call(
        body, name=name,
        out_shape=(pltpu.SemaphoreType.DMA((npeer * n,)), pltpu.SemaphoreType.DMA((npeer * n,)),
                   *[pltpu.HBM(b.shape, b.dtype) for b in bufs], TOKEN),
        in_specs=[HBM_SPEC] * (2 * n),
        out_specs=(SEM_SPEC, SEM_SPEC, *[HBM_SPEC] * (2 * n), pl.BlockSpec(memory_space=pltpu.VMEM)),
        input_output_aliases={i: 2 + i for i in range(2 * n)},
        compiler_params=pltpu.CompilerParams(has_side_effects=DATAFLOW))(*bufs)
    return outs[0], outs[1], list(outs[2:2 + n]), list(outs[2 + n:2 + 2 * n]), outs[-1]


def _push_wait(send_sems, recv_sems, srcs, lands, after, route, peers, name):
    n, npeer = len(srcs), len(peers)

    def body(*refs):
        ins, lnd = refs[:n], refs[n:2 * n]
        ssem, rsem = refs[2 * n], refs[2 * n + 1]
        x, y, c = _me()
        for t in range(n):
            for k, (fx, fy, fc) in enumerate(peers):
                src, dst = route(ins[t], lnd[t], k, x, y, c)
                cp = pltpu.make_async_remote_copy(
                    src_ref=src, dst_ref=dst, send_sem=ssem.at[npeer * t + k], recv_sem=rsem.at[npeer * t + k],
                    device_id=(_flip(x, fx), _flip(y, fy), _flip(c, fc)), device_id_type=MESH)
                cp.wait_send()
                cp.wait_recv()

    bufs = list(srcs) + list(lands)
    outs = pl.pallas_call(
        body, name=name, out_shape=tuple(pltpu.HBM(b.shape, b.dtype) for b in bufs),
        in_specs=[HBM_SPEC] * (2 * n) + [SEM_SPEC, SEM_SPEC, ANY_SPEC], out_specs=tuple([HBM_SPEC] * (2 * n)),
        input_output_aliases={i: i for i in range(2 * n)},
        compiler_params=pltpu.CompilerParams(has_side_effects=DATAFLOW))(*bufs, send_sems, recv_sems, after)
    return list(outs[:n]), list(outs[n:])


OTHER_CHIPS = [(fx, fy, 0) for fx, fy in CHIP_FLIPS]
SIBLING = [(0, 0, 1)]


def _route_gather(src, land, k, x, y, c):
    return src, land.at[2 * x + y]


def _route_gather_half(src, land, k, x, y, c):
    h = src.shape[1] // 2
    return src.at[:, pl.ds(c * h, h)], land.at[2 * x + y, :, pl.ds(c * h, h)]


def _route_gather_half_wait(src, land, k, x, y, c):
    fx, fy = CHIP_FLIPS[k]
    h = src.shape[1] // 2
    return src.at[:, pl.ds(c * h, h)], land.at[2 * _flip(x, fx) + _flip(y, fy), :, pl.ds(c * h, h)]


def _route_gather_wait(src, land, k, x, y, c):
    fx, fy = CHIP_FLIPS[k]
    return src, land.at[2 * _flip(x, fx) + _flip(y, fy)]


def _route_scatter(src, land, k, x, y, c):
    fx, fy = CHIP_FLIPS[k]
    return src.at[2 * _flip(x, fx) + _flip(y, fy)], land.at[k]


def _route_exchange(src, land, k, x, y, c):
    h = land.shape[1]
    return src.at[:, pl.ds((1 - c) * h, h)], land


def _route_whole(src, land, k, x, y, c):
    return src, land


def _route_exchange_cols(src, land, k, x, y, c):
    h = land.shape[2]
    return src.at[:, :, pl.ds((1 - c) * h, h)], land


ALL_OTHERS = [((k >> 2) & 1, (k >> 1) & 1, k & 1) for k in range(1, 8)]


def _route_to_all(src, land, k, x, y, c):
    return src, land.at[4 * x + 2 * y + c]


def _route_to_all_wait(src, land, k, x, y, c):
    fx, fy, fc = ALL_OTHERS[k]
    return src, land.at[4 * _flip(x, fx) + 2 * _flip(y, fy) + _flip(c, fc)]


def _sum_devices(parts):
    def body(p_ref, o_ref):
        acc = p_ref[0]
        for d in range(1, 8):
            acc = acc + p_ref[d]
        o_ref[...] = acc

    vm = pl.BlockSpec(memory_space=pltpu.VMEM)
    return pl.pallas_call(body, name="allreduce_sum", in_specs=[vm], out_specs=vm,
                          out_shape=jax.ShapeDtypeStruct(parts.shape[1:], F32),
                          compiler_params=pltpu.CompilerParams(vmem_limit_bytes=VMEM_LIMIT))(parts)


def _grad_exchange_start(g4, tag, cols=False):
    land = (N_CHIPS, g4.shape[1], g4.shape[2] // 2) if cols else (N_CHIPS, g4.shape[1] // 2, g4.shape[2])
    route = _route_exchange_cols if cols else _route_exchange
    send_sems, recv_sems, srcs, lands, token = _push_start(
        [g4], [land], route, SIBLING, name="grad_exchange_start_" + tag)
    return (send_sems, recv_sems, srcs, lands, tag, cols), token


def _grad_scatter_start(state, pos, after):
    send_sems, recv_sems, srcs, lands, tag, cols = state
    route = _route_exchange_cols if cols else _route_exchange
    (g4,), (recv,) = _push_wait(send_sems, recv_sems, srcs, lands, after, route, SIBLING,
                                name="grad_exchange_wait_" + tag)
    return _grad_pair_scatter(g4, recv, pos, tag, cols)


def _grad_pair_scatter(g4, recv, pos, tag, cols=False):
    p16 = _sum_own_half(g4, recv, pos, name="grad_sum_pair_" + tag, cols=cols)
    send_sems, recv_sems, srcs, lands, token = _push_start(
        [p16], [(3,) + p16.shape[1:]], _route_scatter, OTHER_CHIPS, name="grad_scatter_start_" + tag)
    return (g4, recv, send_sems, recv_sems, srcs, lands, tag, cols), token


def _grad_sum_and_share(state, pos, after):
    g4, recv, send_sems, recv_sems, srcs, lands, tag, cols = state
    parts = _push_wait(send_sems, recv_sems, srcs, lands, after, _route_scatter, OTHER_CHIPS,
                       name="grad_scatter_wait_" + tag)[1][0]
    mine = _sum_chips(g4, recv, parts, pos, name="grad_sum_chips_" + tag, cols=cols)
    send_sems, recv_sems, srcs, lands, token = _push_start(
        [mine], [mine.shape], _route_whole, SIBLING, name="grad_share_start_" + tag)
    return (send_sems, recv_sems, srcs, lands, tag), token


def _grad_share_wait(state, after):
    send_sems, recv_sems, srcs, lands, tag = state
    (mine,), (theirs,) = _push_wait(send_sems, recv_sems, srcs, lands, after, _route_whole, SIBLING,
                                    name="grad_share_wait_" + tag)
    return mine, theirs


def _local_step(x, tgt, p, hooks):
    t = x.shape[0]
    tables = _rope_tables(t)
    sinks = p['sinks'].reshape(N_Q_HEADS)

    def told(name, value):
        return tuple(hooks.grad_ready(name, value))

    xn = _rmsnorm_fwd(x, p['norm_mix'], "norm_mix_fwd", deps=hooks.first_deps)
    w_in_t, w_in_dt, in_deps = hooks.weight_in(xn)
    proj = _matmul(xn, w_in_t, mode='nt', name="in_proj", n_limit=MAIN_WIDTH, deps=in_deps)
    dt_raw = _matmul(xn, w_in_dt, mode='nt', name="in_proj_dt")[:, :SSD_HEADS]
    ssd_conv_w, ffn_conv_w = hooks.conv_weights(proj)
    p = dict(p, ssd_conv_w=ssd_conv_w, ffn_conv_w=ffn_conv_w)
    attn = _attn_fwd(proj, sinks, tables)
    conv_b = p['ssd_conv_b']
    xbc, xbc_pre = _conv_silu_fwd(proj, p['ssd_conv_w'], conv_b, col0=O_XBC, width=CONV_CH, name="ssd_conv_fwd")
    sp = _ssd_params(dt_raw, p['dt_bias'].reshape(-1), p['a_log'].reshape(-1), p['ssd_d'].reshape(-1))
    y, states = _ssd_fwd(xbc, sp)
    mix = _mix_fwd(attn, y, proj, p['attn_out_norm'], p['ssd_norm'])
    w_out = hooks.weight('w_out', mix)
    h1 = _matmul(mix, w_out, mode='nn', name="out_proj", add=x)
    hn = _rmsnorm_fwd(h1, p['norm_ffn'], "norm_ffn_fwd")
    w_up = hooks.weight('w_up', hn)
    u0 = _matmul(hn, w_up, mode='nn', name="ffn_up", b_owner=True, tn=1408)
    a, u = _ffn_act_fwd(u0, p['ffn_conv_w'], p['ffn_conv_b'])
    w_down = hooks.weight('w_down', a)
    h2 = _matmul(a, w_down, mode='nn', name="ffn_down", add=h1, tk=2816)
    loss, dh2, dh2_16, g_norm_final = _final_loss(h2, p['norm_final'].reshape(1, D_MODEL), tgt)

    g = {}
    da = _matmul(dh2_16, w_down, mode='nt', name="ffn_down_dx", out_dtype=BF16, tn=1408)
    g['w_down'] = _matmul(a, dh2_16, mode='tn', name="ffn_down_dw", tm=1408)
    dep = told('w_down', g['w_down'])
    du0, dcw, dcb = _ffn_act_bwd(u0, u, p['ffn_conv_w'], da)
    g['ffn_conv_w'] = dcw.transpose(1, 0, 2).reshape(FFN_CONV, 2 * D_FF)
    g['ffn_conv_b'] = dcb.transpose(1, 0, 2).reshape(1, 2 * D_FF)
    g['w_up'] = _matmul(hn, du0, mode='tn', name="ffn_up_dw", deps=dep, b_halves=True, owner_major=True,
                        tn=1408)
    dep = told('w_up', g['w_up'])
    dhn = _matmul(du0, w_up, mode='nt', name="ffn_up_dx", out_dtype=BF16, deps=dep, a_halves=True,
                  b_owner=True, tk=2816)
    dh1, dh1_16, g['norm_ffn'] = _rmsnorm_bwd(h1, p['norm_ffn'], dhn, dh2, "norm_ffn_bwd")

    g['w_out'] = _matmul(mix, dh1_16, mode='tn', name="out_proj_dw")
    dep = told('w_out', g['w_out'])
    dmix = _matmul(dh1_16, w_out, mode='nt', name="out_proj_dx", out_dtype=BF16, deps=dep)
    dattn, dy, dz, g['attn_out_norm'], g['ssd_norm'] = _mix_bwd(dmix, attn, y, proj, p['attn_out_norm'],
                                                                p['ssd_norm'])
    dq, dk, dv, dsink = _attn_bwd(proj, sinks, tables, dattn)
    g['sinks'] = dsink[:, :, 0].reshape(1, N_Q_HEADS)
    dxs, dbm, dcm, ddt8, dpar = _ssd_bwd(xbc, xbc_pre, sp, states, dy)
    dpar = dpar[:, :, ::SSD_HEAD_DIM]
    g['dt_bias'] = dpar[:, 0, :].reshape(1, SSD_HEADS)
    g['a_log'] = dpar[:, 1, :].reshape(1, SSD_HEADS)
    g['ssd_d'] = dpar[:, 2, :].reshape(1, SSD_HEADS)
    dxbc, g['ssd_conv_w'], g['ssd_conv_b'] = _ssd_conv_bwd(proj, p['ssd_conv_w'], dxs, dbm, dcm, col0=O_XBC,
                                                           name="ssd_conv_bwd")
    dproj = jnp.concatenate([dq, dk, dv, dz, dxbc], axis=1)
    ddt = ddt8.transpose(2, 0, 1).reshape(t, SSD_HEADS)
    ddt_pad = jnp.pad(ddt, ((0, 0), (0, LANES - SSD_HEADS))).astype(BF16)
    g['w_in'] = (_matmul(dproj, xn, mode='tn', name="in_proj_dw", m_rows=IN_PROJ_WIDTH),
                 _matmul(ddt_pad, xn, mode='tn', name="in_proj_dt_dw"))
    dep = told('w_in', g['w_in'])
    dxn_dt = _matmul(ddt_pad, w_in_dt, mode='nn', name="in_proj_dt_dx", deps=dep)
    dxn = _matmul(dproj, w_in_t, mode='nn', name="in_proj_dx", out_dtype=BF16, add=dxn_dt, k_limit=MAIN_WIDTH,
                  tk=2304)
    dep = told(None, dxn)
    dx, _, g['norm_mix'] = _rmsnorm_bwd(x, p['norm_mix'], dxn, dh1, "norm_mix_bwd", deps=dep)
    g['norm_final'] = g_norm_final
    return loss, dx, g


def _pack(arrs):
    flat = jnp.concatenate([a.reshape(-1) for a in arrs])
    n = flat.shape[0]
    rows = -(-n // LANES)
    rows = -(-rows // 8) * 8
    return jnp.pad(flat, (0, rows * LANES - n)).reshape(rows, LANES)


def _unpack(packed, shapes):
    flat = packed.reshape(-1)
    out, off = [], 0
    for s in shapes:
        n = 1
        for d in s:
            n *= d
        out.append(flat[off:off + n].reshape(s))
        off += n
    return out


class _StepHooks:
    def __init__(self, first_deps, weight_in, conv_weights, weight, grad_ready):
        self.first_deps = first_deps
        self.weight_in = weight_in
        self.conv_weights = conv_weights
        self.weight = weight
        self.grad_ready = grad_ready


def kernel(x, norm_mix, w_in, sinks, attn_out_norm, ssd_conv_w, ssd_conv_b, dt_bias, a_log, ssd_d, ssd_norm, w_out, norm_ffn, w_up, ffn_conv_w, ffn_conv_b, w_down, norm_final, loss_target, m_norm_mix, m_w_in, m_sinks, m_attn_out_norm, m_ssd_conv_w, m_ssd_conv_b, m_dt_bias, m_a_log, m_ssd_d, m_ssd_norm, m_w_out, m_norm_ffn, m_w_up, m_ffn_conv_w, m_ffn_conv_b, m_w_down, m_norm_final, v_norm_mix, v_w_in, v_sinks, v_attn_out_norm, v_ssd_conv_w, v_ssd_conv_b, v_dt_bias, v_a_log, v_ssd_d, v_ssd_norm, v_w_out, v_norm_ffn, v_w_up, v_ffn_conv_w, v_ffn_conv_b, v_w_down, v_norm_final):
    args = dict(locals())
    w = {n: args[n] for n in WEIGHTS}
    m = {n: args['m_' + n] for n in WEIGHTS}
    v = {n: args['v_' + n] for n in WEIGHTS}
    xi, yi, ci = _me()
    chip = 2 * xi + yi
    pos = jnp.stack([ci, chip]).astype(jnp.int32)

    w_in_t, m_in_t, v_in_t = (jnp.transpose(a[0]) for a in (w_in, m_w_in, v_w_in))
    in_shard = w_in_t.astype(BF16)
    in_gather = _push_start([in_shard], [(N_CHIPS,) + in_shard.shape], _route_gather_half, OTHER_CHIPS,
                            name="gather_start_w_in")
    conv_shard = _pack([ssd_conv_w[0], ffn_conv_w[0]]) + in_gather[4][:1, :1]
    conv_gather = _push_start([conv_shard], [(N_CHIPS,) + conv_shard.shape], _route_gather, OTHER_CHIPS,
                              name="gather_start_conv")

    def conv_weights(after):
        send_sems, recv_sems, srcs, lands, _ = conv_gather
        (own,), (got,) = _push_wait(send_sems, recv_sems, srcs, lands, after, _route_gather_wait, OTHER_CHIPS,
                                    name="gather_wait_conv")
        whole = lax.dynamic_update_slice(got, own[None], (chip, 0, 0))
        per_chip = [_unpack(whole[j], [ssd_conv_w.shape[1:], ffn_conv_w.shape[1:]]) for j in range(N_CHIPS)]
        return (jnp.concatenate([pc[0] for pc in per_chip], axis=1),
                jnp.concatenate([pc[1] for pc in per_chip], axis=1))

    gathers = {}
    order = conv_gather[4][:1, :1]
    for n, shard in (('w_out', w_out[0]), ('w_up', w_up[0]), ('w_down', w_down[0])):
        shard = (shard + order).astype(BF16)
        gathers[n] = _push_start([shard], [(N_CHIPS,) + shard.shape], _route_gather, OTHER_CHIPS,
                                 name="gather_start_" + n)
        order = gathers[n][4][:1, :1]

    def weight_in(after):
        send_sems, recv_sems, srcs, lands, _ = in_gather
        (own,), (got,) = _push_wait(send_sems, recv_sems, srcs, lands, after, _route_gather_half_wait, OTHER_CHIPS,
                                    name="gather_wait_w_in")
        got, _ = _forward_halves(got)
        full_in_t = lax.dynamic_update_slice(got, own[None], (chip, 0, 0)).reshape(IN_PROJ_WIDTH, D_MODEL)
        w_in_dt = jnp.pad(full_in_t[MAIN_WIDTH:], ((0, LANES - SSD_HEADS), (0, 0)))
        return full_in_t, w_in_dt, ()

    def weight(name, after):
        send_sems, recv_sems, srcs, lands, _ = gathers[name]
        (own,), (got,) = _push_wait(send_sems, recv_sems, srcs, lands, after, _route_gather_wait, OTHER_CHIPS,
                                    name="gather_wait_" + name)
        whole = lax.dynamic_update_slice(got, own[None], (chip, 0, 0))
        return whole if name == 'w_up' else whole.reshape(-1, D_MODEL)

    reductions, exchanging = {}, {}

    def flush(after):
        tokens = []
        for prev in list(exchanging):
            reductions[prev], token = _grad_scatter_start(exchanging.pop(prev), pos, after)
            tokens.append(token)
        return tokens

    def grad_ready(name, value):
        if name is None:
            return flush(value)
        if name == 'w_in':
            main, dtp = value
            value = lax.dynamic_update_slice(main, dtp[:SSD_HEADS], (MAIN_WIDTH, 0))
        g4 = value if value.ndim == 3 else value.reshape(N_CHIPS, -1, value.shape[1])
        tokens = flush(g4)
        exchanging[name], token = _grad_exchange_start(g4, name, cols=(name == 'w_in'))
        return tokens + [token]

    small = {
        'norm_mix': norm_mix, 'sinks': sinks, 'attn_out_norm': attn_out_norm,
        'ssd_conv_b': ssd_conv_b, 'dt_bias': dt_bias, 'a_log': a_log, 'ssd_d': ssd_d, 'ssd_norm': ssd_norm,
        'norm_ffn': norm_ffn, 'ffn_conv_b': ffn_conv_b, 'norm_final': norm_final,
    }
    loss, dx, g = _local_step(x[0], loss_target[0], small,
                              _StepHooks((gathers['w_down'][4],), weight_in, conv_weights, weight, grad_ready))

    small_names = [n for n in WEIGHTS if n not in BIG]
    small_g = [loss[:, :1]] + [g[n] for n in small_names]
    small_shapes = [(1, 1)] + [tuple(a.shape) for a in small_g[1:]]
    packed = _pack(small_g)
    spread = _push_start([packed], [(8,) + packed.shape], _route_to_all, ALL_OTHERS, name="allreduce_start")
    grads, deltas, new_m, new_v = {}, {}, {}, {}
    after = spread[4]
    shares = {}
    for n in ('w_down', 'w_up', 'w_out'):
        shares[n], after = _grad_sum_and_share(reductions[n], pos, after)
    for n in ('w_down', 'w_up', 'w_out', 'w_in'):
        if n == 'w_out':
            shares['w_in'], after = _grad_sum_and_share(reductions['w_in'], pos, after)
        mine, theirs = _grad_share_wait(shares[n], after)
        if n == 'w_in':
            outs = _adamw_halves(w_in_t, mine, theirs, m_in_t, v_in_t, pos, name="adamw_" + n, cols=True)
            after = outs[1]
            outs = [jnp.transpose(o) for o in outs]
        else:
            outs = _adamw_halves(w[n][0], mine, theirs, m[n][0], v[n][0], pos, name="adamw_" + n)
            after = outs[1]
        grads[n], deltas[n], new_m[n], new_v[n] = [o[None] for o in outs]
    (own,), (landed,) = _push_wait(spread[0], spread[1], spread[2], spread[3], after, _route_to_all_wait, ALL_OTHERS,
                                   name="allreduce_wait")
    landed = lax.dynamic_update_slice(landed, own[None], (4 * xi + 2 * yi + ci, 0, 0))
    red = _unpack(_sum_devices(landed), small_shapes)
    loss_out = red[0].reshape(())
    gsm = dict(zip(small_names, red[1:]))
    gsm['ssd_conv_w'] = lax.dynamic_slice(gsm['ssd_conv_w'], (0, chip * ssd_conv_w.shape[2]),
                                          (SSD_CONV, ssd_conv_w.shape[2]))
    gsm['ffn_conv_w'] = lax.dynamic_slice(gsm['ffn_conv_w'], (0, chip * ffn_conv_w.shape[2]),
                                          (FFN_CONV, ffn_conv_w.shape[2]))

    shapes = [tuple(w[n].shape) for n in small_names]
    gp = _pack([gsm[n] for n in small_names])
    d, m2, v2 = _adamw(_pack([w[n] for n in small_names]), gp, _pack([m[n] for n in small_names]),
                       _pack([v[n] for n in small_names]), name="adamw_small")
    for n, gg, dd, mm, vv in zip(small_names, _unpack(gp, shapes), _unpack(d, shapes), _unpack(m2, shapes),
                                 _unpack(v2, shapes)):
        grads[n], deltas[n], new_m[n], new_v[n] = gg, dd, mm, vv

    return (loss_out, dx[None], *[grads[n] for n in WEIGHTS], *[deltas[n] for n in WEIGHTS],
            *[new_m[n] for n in WEIGHTS], *[new_v[n] for n in WEIGHTS])
```

```python
import functools

import jax
import jax.numpy as jnp
from jax import lax
from jax.experimental import pallas as pl
from jax.experimental.pallas import tpu as pltpu

F32 = jnp.float32
BF16 = jnp.bfloat16

D_MODEL = 2048
N_Q_HEADS = 32
N_KV_HEADS = 8
HEAD_DIM = 64
WINDOW = 128
ATTN_BLOCK = 128
ROT_DIM = 16
ROPE_THETA = 500000.0
SSD_HEADS = 32
SSD_HEAD_DIM = 64
SSD_INNER = 2048
SSD_GROUPS = 8
SSD_STATE = 128
SSD_CONV = 4
SSD_CHUNK = 128
ATTN_WIDTH = 2048
KV_WIDTH = 512
BC_WIDTH = 1024
CONV_CH = 4096
IN_PROJ_WIDTH = 9248
MAIN_WIDTH = 9216
D_FF = 5632
FFN_CONV = 3
EPS = 1e-6
O_Q, O_K, O_V, O_Z, O_XBC, O_DT = 0, 2048, 2560, 3072, 5120, 9216

ADAM_LR = 0.001
ADAM_B1 = 0.9
ADAM_B2 = 0.999
ADAM_EPS = 1e-08
ADAM_WD = 0.01
ADAM_STEP = 10

N_CHIPS = 4
NEG = -1e30
LANES = 128
VMEM_LIMIT = 48 * 1024 * 1024
MESH = pl.DeviceIdType.MESH
HBM_SPEC = pl.BlockSpec(memory_space=pltpu.HBM)
TOKEN = jax.ShapeDtypeStruct((8, LANES), F32)

WEIGHTS = ['norm_mix', 'w_in', 'sinks', 'attn_out_norm', 'ssd_conv_w', 'ssd_conv_b', 'dt_bias', 'a_log', 'ssd_d',
           'ssd_norm', 'w_out', 'norm_ffn', 'w_up', 'ffn_conv_w', 'ffn_conv_b', 'w_down', 'norm_final']
BIG = ['w_in', 'w_out', 'w_up', 'w_down']


def _cp(sem=None, vmem=VMEM_LIMIT):
    kw = {'vmem_limit_bytes': vmem}
    if sem is not None:
        kw['dimension_semantics'] = sem
    return pltpu.CompilerParams(**kw)


def _tile(n, pref):
    if n <= pref:
        return n
    t = (pref // LANES) * LANES
    while t > LANES and n % t:
        t -= LANES
    assert n % t == 0, (n, pref)
    return t


def _rows(n, pref):
    t = min(n, pref)
    while n % t:
        t -= 8
    if 4 * t < pref:
        t = pref
        while n % t:
            t += 8
    return t


def _iota(shape, dim):
    return lax.broadcasted_iota(jnp.int32, shape, dim)


def _dot(a, b, mode='nn'):
    dn = {'nn': (((1,), (0,)), ((), ())), 'nt': (((1,), (1,)), ((), ())), 'tn': (((0,), (0,)), ((), ()))}[mode]
    return lax.dot_general(a.astype(BF16), b.astype(BF16), dn, preferred_element_type=F32)


def _dot_exact(a, b):
    return lax.dot_general(a, b, (((1,), (0,)), ((), ())), precision=lax.Precision.HIGHEST,
                           preferred_element_type=F32)


def _sigmoid(x):
    return 1.0 / (1.0 + jnp.exp(-x))


def _softplus(x):
    return jnp.maximum(x, 0.0) + jnp.log(1.0 + jnp.exp(-jnp.abs(x)))


def _matmul(a, b, *, mode, name, out_dtype=F32, add=None, deps=(), tm=1024, tn=1024, tk=2048,
            a_halves=False, b_halves=False, b_owner=False, owner_major=False, n_limit=None, k_limit=None,
            m_rows=None):
    ash, bsh = (a.shape[1:] if a_halves else a.shape), (b.shape[1:] if (b_halves or b_owner) else b.shape)
    if mode == 'nn':
        (m, k), (k2, n) = ash, bsh
    elif mode == 'nt':
        (m, k), (n, k2) = ash, bsh
    else:
        (k, m), (k2, n) = ash, bsh
    if n_limit is not None:
        assert mode == 'nt' and n_limit <= n
        n = n_limit
    if k_limit is not None:
        assert mode == 'nn' and k_limit <= k2
        k2 = k_limit
    if a_halves:
        assert mode == 'nt'
        k = 2 * k
    if b_halves:
        assert mode == 'tn'
        n = 2 * n
    if b_owner:
        assert mode in ('nn', 'nt')
        if mode == 'nn':
            n = 4 * n
        else:
            k2 = 4 * k2
    assert k == k2, (a.shape, b.shape, mode)
    tm = _tile(m, tm)
    tn = _tile(n // 4 if (owner_major or (b_owner and mode == 'nn')) else (n // 2 if b_halves else n), tn)
    tk = _tile(k // 4 if (b_owner and mode == 'nt') else (k // 2 if a_halves else k), tk)
    nk = k // tk
    has_add = add is not None
    assert not (has_add and owner_major)

    def body(*refs):
        a_ref, b_ref = refs[:2]
        add_ref = refs[2] if has_add else None

        def finish(r, o_ref):
            if has_add:
                r = r + add_ref[...].astype(F32)
            o_ref[...] = r.astype(out_dtype)

        if nk == 1:
            finish(_dot(a_ref[...], b_ref[...], mode), refs[-1])
            return
        o_ref, acc = refs[-2:]
        kk = pl.program_id(2)

        @pl.when(kk == 0)
        def _():
            acc[...] = _dot(a_ref[...], b_ref[...], mode)

        @pl.when((kk > 0) & (kk < nk - 1))
        def _():
            acc[...] += _dot(a_ref[...], b_ref[...], mode)

        @pl.when(kk == nk - 1)
        def _():
            finish(acc[...] + _dot(a_ref[...], b_ref[...], mode), o_ref)

    if mode == 'tn':
        a_spec = pl.BlockSpec((tk, tm), lambda i, j, kk: (kk, i))
    elif a_halves:
        nkh = nk // 2
        a_spec = pl.BlockSpec((None, tm, tk), lambda i, j, kk: (kk // nkh, i, kk % nkh))
    else:
        a_spec = pl.BlockSpec((tm, tk), lambda i, j, kk: (i, kk))
    if mode == 'nt' and b_owner:
        nkq = nk // 4
        b_spec = pl.BlockSpec((None, tn, tk), lambda i, j, kk: (kk // nkq, j, kk % nkq))
    elif mode == 'nt':
        b_spec = pl.BlockSpec((tn, tk), lambda i, j, kk: (j, kk))
    elif b_owner:
        njq = (n // 4) // tn
        b_spec = pl.BlockSpec((None, tk, tn), lambda i, j, kk: (j // njq, kk, j % njq))
    elif b_halves:
        njh = (n // 2) // tn
        b_spec = pl.BlockSpec((None, tk, tn), lambda i, j, kk: (j // njh, kk, j % njh))
    else:
        b_spec = pl.BlockSpec((tk, tn), lambda i, j, kk: (kk, j))
    if owner_major:
        njo = (n // 4) // tn
        o_spec = pl.BlockSpec((None, tm, tn), lambda i, j, kk: (j // njo, i, j % njo))
        out_shape = jax.ShapeDtypeStruct((N_CHIPS, m, n // 4), out_dtype)
    else:
        o_spec = pl.BlockSpec((tm, tn), lambda i, j, kk: (i, j))
        out_shape = jax.ShapeDtypeStruct((m if m_rows is None else m_rows, n), out_dtype)
    dep_spec = pl.BlockSpec((8, LANES), lambda i, j, kk: (0, 0))
    in_specs = [a_spec, b_spec] + ([pl.BlockSpec((tm, tn), lambda i, j, kk: (i, j))] if has_add else [])
    in_specs += [dep_spec] * len(deps)
    args = (a, b) + ((add,) if has_add else ()) + tuple(deps)
    return pl.pallas_call(
        body, name=name, grid=(m // tm, n // tn, nk), in_specs=in_specs, out_specs=o_spec, out_shape=out_shape,
        scratch_shapes=[pltpu.VMEM((tm, tn), F32)] if nk > 1 else [],
        compiler_params=_cp(("parallel", "parallel", "arbitrary")))(*args)


def _rmsnorm_fwd(x, g, name, deps=()):
    t, d = x.shape
    tb = _rows(t, 256)

    def body(x_ref, g_ref, *rest):
        o_ref = rest[-1]
        xv = x_ref[...]
        r = lax.rsqrt(jnp.mean(xv * xv, axis=-1, keepdims=True) + EPS)
        o_ref[...] = (xv * r * g_ref[...]).astype(BF16)

    dep_spec = pl.BlockSpec((8, LANES), lambda i: (0, 0))
    return pl.pallas_call(
        body, name=name, grid=(t // tb,),
        in_specs=[pl.BlockSpec((tb, d), lambda i: (i, 0)), pl.BlockSpec((1, d), lambda i: (0, 0))]
        + [dep_spec] * len(deps),
        out_specs=pl.BlockSpec((tb, d), lambda i: (i, 0)), out_shape=jax.ShapeDtypeStruct((t, d), BF16),
        compiler_params=_cp(("parallel",)))(x, g, *deps)


def _rmsnorm_bwd(x, g, dy, res, name, deps=()):
    t, d = x.shape
    tb = _rows(t, 256)

    def body(x_ref, g_ref, dy_ref, res_ref, *rest):
        dx_ref, dx16_ref, dg_ref = rest[-3:]
        i = pl.program_id(0)
        xv = x_ref[...]
        dyv = dy_ref[...].astype(F32)
        r = lax.rsqrt(jnp.mean(xv * xv, axis=-1, keepdims=True) + EPS)
        u = dyv * g_ref[...]
        dx = r * u - xv * (r * r * r * jnp.mean(u * xv, axis=-1, keepdims=True)) + res_ref[...]
        dx_ref[...] = dx
        dx16_ref[...] = dx.astype(BF16)
        part = jnp.sum(dyv * xv * r, axis=0, keepdims=True)

        @pl.when(i == 0)
        def _():
            dg_ref[...] = part

        @pl.when(i > 0)
        def _():
            dg_ref[...] += part

    row = pl.BlockSpec((tb, d), lambda i: (i, 0))
    vec = pl.BlockSpec((1, d), lambda i: (0, 0))
    return pl.pallas_call(
        body, name=name, grid=(t // tb,),
        in_specs=[row, vec, row, row] + [pl.BlockSpec((8, LANES), lambda i: (0, 0))] * len(deps),
        out_specs=[row, row, vec],
        out_shape=[jax.ShapeDtypeStruct((t, d), F32), jax.ShapeDtypeStruct((t, d), BF16),
                   jax.ShapeDtypeStruct((1, d), F32)],
        compiler_params=_cp(("arbitrary",)))(x, g, dy, res, *deps)


def _final_loss(h, g, tgt):
    t, d = h.shape
    tb = _rows(t, 256)

    def body(h_ref, g_ref, t_ref, loss_ref, dh_ref, dh16_ref, dg_ref):
        i = pl.program_id(0)
        hv = h_ref[...]
        gv = g_ref[...]
        r = lax.rsqrt(jnp.mean(hv * hv, axis=-1, keepdims=True) + EPS)
        y = hv * r * gv
        diff = y - t_ref[...]
        lpart = jnp.sum(jnp.sum(diff * diff, axis=1, keepdims=True), axis=0, keepdims=True) * (0.5 / d)
        dy = diff * (1.0 / d)
        u = dy * gv
        dh = r * u - hv * (r * r * r * jnp.mean(u * hv, axis=-1, keepdims=True))
        dh_ref[...] = dh
        dh16_ref[...] = dh.astype(BF16)
        gpart = jnp.sum(dy * hv * r, axis=0, keepdims=True)
        lrow = jnp.broadcast_to(lpart, (1, LANES))

        @pl.when(i == 0)
        def _():
            loss_ref[...] = lrow
            dg_ref[...] = gpart

        @pl.when(i > 0)
        def _():
            loss_ref[...] += lrow
            dg_ref[...] += gpart

    row = pl.BlockSpec((tb, d), lambda i: (i, 0))
    vec = pl.BlockSpec((1, d), lambda i: (0, 0))
    return pl.pallas_call(
        body, name="final_loss", grid=(t // tb,), in_specs=[row, vec, row],
        out_specs=[pl.BlockSpec((1, LANES), lambda i: (0, 0)), row, row, vec],
        out_shape=[jax.ShapeDtypeStruct((1, LANES), F32), jax.ShapeDtypeStruct((t, d), F32),
                   jax.ShapeDtypeStruct((t, d), BF16), jax.ShapeDtypeStruct((1, d), F32)],
        compiler_params=_cp(("arbitrary",)))(h, g, tgt)


def _rope_tables(t):
    pos = jnp.arange(t, dtype=F32)
    inv = 1.0 / (ROPE_THETA ** (jnp.arange(0, ROT_DIM, 2, dtype=F32) / ROT_DIM))
    ang = pos[:, None] * inv[None, :]
    cos, sin = jnp.cos(ang), jnp.sin(ang)
    half = ROT_DIM // 2
    rest = HEAD_DIM - ROT_DIM
    c = jnp.concatenate([cos, cos, jnp.ones((t, rest), F32)], axis=1)
    s1 = jnp.concatenate([-sin, jnp.zeros((t, half + rest), F32)], axis=1)
    s2 = jnp.concatenate([jnp.zeros((t, half), F32), sin, jnp.zeros((t, rest), F32)], axis=1)
    return jnp.concatenate([jnp.tile(v, (1, LANES // HEAD_DIM)) for v in (c, s1, s2)], axis=1)


def _split_tables(tab):
    return tab[:, :LANES], tab[:, LANES:2 * LANES], tab[:, 2 * LANES:]


def _rope(x, c, s1, s2):
    half = ROT_DIM // 2
    return x * c + pltpu.roll(x, LANES - half, 1) * s1 + pltpu.roll(x, half, 1) * s2


def _rope_t(g, c, s1, s2):
    half = ROT_DIM // 2
    return g * c + pltpu.roll(g * s1, half, 1) + pltpu.roll(g * s2, LANES - half, 1)


def _band_masks(i, heads):
    n = heads * ATTN_BLOCK
    q = jnp.bitwise_and(_iota((n, ATTN_BLOCK), 0), ATTN_BLOCK - 1)
    j = _iota((n, ATTN_BLOCK), 1)
    upper = j > q
    return upper, upper & (j < jnp.where(i > 0, 0, ATTN_BLOCK))


def _fold_band(full, upper):
    return jnp.where(upper, full[:, :ATTN_BLOCK], full[:, ATTN_BLOCK:])


def _unfold_band(band, upper):
    return jnp.concatenate([jnp.where(upper, band, 0.0), jnp.where(upper, 0.0, band)], axis=1)


def _half_masks():
    lane = _iota((1, LANES), 1)
    return [(lane < HEAD_DIM).astype(F32), (lane >= HEAD_DIM).astype(F32)]


def _stack_heads(blocks, hm, j):
    pieces = []
    for r in range(4):
        qb, half = (4 * j + r) // 2, (4 * j + r) % 2
        piece = blocks[qb] * hm[half]
        if half != j:
            piece = pltpu.roll(piece, HEAD_DIM, 1)
        pieces.append(piece)
    return jnp.concatenate(pieces, axis=0)


def _unstack_heads(stacked, j):
    out = []
    for qb in (2 * j, 2 * j + 1):
        acc = None
        for half in range(2):
            r = 2 * qb + half - 4 * j
            piece = stacked[r * ATTN_BLOCK:(r + 1) * ATTN_BLOCK]
            if half != j:
                piece = pltpu.roll(piece, HEAD_DIM, 1)
            acc = piece if acc is None else acc + piece
        out.append((qb, acc))
    return out


def _sink_column(sink_ref, base):
    return jnp.concatenate([jnp.full((ATTN_BLOCK, 1), sink_ref[base + r], F32) for r in range(4)], axis=0)


def _attn_specs(nb_clamp):
    blk = ATTN_BLOCK
    kb, vb = O_K // LANES, O_V // LANES

    def cur(i):
        return jnp.minimum(i, nb_clamp)

    def prev(i):
        return jnp.maximum(jnp.minimum(i, nb_clamp + 1) - 1, 0)

    q = pl.BlockSpec((blk, 512), lambda p, i: (cur(i), p))
    kc = pl.BlockSpec((blk, LANES), lambda p, i: (cur(i), kb + p))
    kp = pl.BlockSpec((blk, LANES), lambda p, i: (prev(i), kb + p))
    vc = pl.BlockSpec((blk, LANES), lambda p, i: (cur(i), vb + p))
    vp = pl.BlockSpec((blk, LANES), lambda p, i: (prev(i), vb + p))
    tc = pl.BlockSpec((blk, 3 * LANES), lambda p, i: (cur(i), 0))
    tp = pl.BlockSpec((blk, 3 * LANES), lambda p, i: (prev(i), 0))
    return q, kc, kp, vc, vp, tc, tp


def _attn_fwd(proj, sinks, tables):
    t = proj.shape[0]
    nb = t // ATTN_BLOCK
    scale = HEAD_DIM ** -0.5

    def body(sink_ref, q_ref, kc_ref, kp_ref, vc_ref, vp_ref, tc_ref, tp_ref, o_ref):
        p = pl.program_id(0)
        i = pl.program_id(1)
        cc, s1c, s2c = _split_tables(tc_ref[...])
        kband = jnp.concatenate([_rope(kp_ref[...], *_split_tables(tp_ref[...])),
                                 _rope(kc_ref[...], cc, s1c, s2c)], axis=0).astype(BF16)
        vband = jnp.concatenate([vp_ref[...], vc_ref[...]], axis=0)
        hm = _half_masks()
        vsel = [(vband * hm[j]).astype(BF16) for j in range(2)]
        upper, dropped = _band_masks(i, 1)
        qr = [_rope(q_ref[:, qb * LANES:(qb + 1) * LANES], cc, s1c, s2c) for qb in range(4)]

        def scores(hh):
            qb, half, j = hh // 2, hh % 2, hh // 4
            qs = qr[qb] * hm[half]
            if half != j:
                qs = pltpu.roll(qs, HEAD_DIM, 1)
            return _dot(qs, kband, 'nt')

        ahead = scores(0)
        acc = None
        for hh in range(8):
            qb, half, j = hh // 2, hh % 2, hh // 4
            raw = ahead
            if hh + 1 < 8:
                ahead = scores(hh + 1)
            s = jnp.where(dropped, NEG, _fold_band(raw, upper) * scale)
            sink = sink_ref[p * 8 + hh]
            m = jnp.maximum(jnp.max(s, axis=1, keepdims=True), sink)
            pe = jnp.exp(s - m)
            den = jnp.sum(pe, axis=1, keepdims=True) + jnp.exp(sink - m)
            o = _dot(_unfold_band(pe / den, upper), vsel[j])
            if half != j:
                o = pltpu.roll(o, HEAD_DIM, 1)
            acc = o if half == 0 else acc + o
            if half == 1:
                o_ref[:, qb * LANES:(qb + 1) * LANES] = acc

    q, kc, kp, vc, vp, tc, tp = _attn_specs(nb - 1)
    smem = pl.BlockSpec(memory_space=pltpu.SMEM)
    return pl.pallas_call(
        body, name="attn_fwd", grid=(4, nb),
        in_specs=[smem, q, kc, kp, vc, vp, tc, tp],
        out_specs=pl.BlockSpec((ATTN_BLOCK, 512), lambda p, i: (i, p)),
        out_shape=jax.ShapeDtypeStruct((t, ATTN_WIDTH), F32),
        compiler_params=_cp(("parallel", "arbitrary")))(sinks, proj, proj, proj, proj, proj, tables, tables)


def _attn_bwd(proj, sinks, tables, dout):
    t = proj.shape[0]
    nb = t // ATTN_BLOCK
    scale = HEAD_DIM ** -0.5

    def body(sink_ref, q_ref, kc_ref, kp_ref, vc_ref, vp_ref, tc_ref, tp_ref,
             do_ref, dq_ref, dk_ref, dv_ref, ds_ref, carry_k, carry_v):
        p = pl.program_id(0)
        i = pl.program_id(1)
        ptab = _split_tables(tp_ref[...])

        @pl.when(i == 0)
        def _():
            carry_k[...] = jnp.zeros_like(carry_k)
            carry_v[...] = jnp.zeros_like(carry_v)
            ds_ref[...] = jnp.zeros_like(ds_ref)

        @pl.when(i < nb)
        def _():
            cc, s1c, s2c = _split_tables(tc_ref[...])
            kband = jnp.concatenate([_rope(kp_ref[...], *ptab), _rope(kc_ref[...], cc, s1c, s2c)], axis=0)
            vband = jnp.concatenate([vp_ref[...], vc_ref[...]], axis=0)
            hm = _half_masks()
            kband16 = kband.astype(BF16)
            vband16 = vband.astype(BF16)
            upper, dropped = _band_masks(i, 4)
            dkb = jnp.zeros((2 * ATTN_BLOCK, LANES), F32)
            dvb = jnp.zeros((2 * ATTN_BLOCK, LANES), F32)
            row8 = _iota((8, LANES), 0)
            dsink = jnp.zeros((8, LANES), F32)
            qr = [_rope(q_ref[:, qb * LANES:(qb + 1) * LANES], cc, s1c, s2c) for qb in range(4)]
            dob = [do_ref[:, qb * LANES:(qb + 1) * LANES] for qb in range(4)]
            for j in range(2):
                qst = _stack_heads(qr, hm, j).astype(BF16)
                dost = _stack_heads(dob, hm, j).astype(BF16)
                s = jnp.where(dropped, NEG, _fold_band(_dot(qst, kband16, 'nt'), upper) * scale)
                sink = _sink_column(sink_ref, p * 8 + 4 * j)
                m = jnp.maximum(jnp.max(s, axis=1, keepdims=True), sink)
                pe = jnp.exp(s - m)
                psink = jnp.exp(sink - m)
                den = jnp.sum(pe, axis=1, keepdims=True) + psink
                pr = pe / den
                dvb = dvb + _dot(_unfold_band(pr, upper).T, dost)
                dp = _fold_band(_dot(dost, vband16, 'nt'), upper)
                delta = jnp.sum(pr * dp, axis=1, keepdims=True)
                dsc = _unfold_band(pr * (dp - delta) * scale, upper)
                dsk = psink / den * delta
                for r in range(4):
                    part = jnp.sum(dsk[r * ATTN_BLOCK:(r + 1) * ATTN_BLOCK])
                    dsink = dsink + jnp.where(row8 == 4 * j + r, -part, 0.0)
                for qb, dqb in _unstack_heads(_dot(dsc, kband * hm[j]), j):
                    dq_ref[:, qb * LANES:(qb + 1) * LANES] = _rope_t(dqb, cc, s1c, s2c).astype(BF16)
                dkb = dkb + _dot(dsc.T, qst)
            ds_ref[0] += dsink
            dk_ref[...] = _rope_t(carry_k[...] + dkb[:ATTN_BLOCK], *ptab).astype(BF16)
            dv_ref[...] = (carry_v[...] + dvb[:ATTN_BLOCK]).astype(BF16)
            carry_k[...] = dkb[ATTN_BLOCK:]
            carry_v[...] = dvb[ATTN_BLOCK:]

        @pl.when(i == nb)
        def _():
            dk_ref[...] = _rope_t(carry_k[...], *ptab).astype(BF16)
            dv_ref[...] = carry_v[...].astype(BF16)

    q, kc, kp, vc, vp, tc, tp = _attn_specs(nb - 1)
    smem = pl.BlockSpec(memory_space=pltpu.SMEM)
    qblk = pl.BlockSpec((ATTN_BLOCK, 512), lambda p, i: (jnp.minimum(i, nb - 1), p))
    kvout = pl.BlockSpec((ATTN_BLOCK, LANES), lambda p, i: (jnp.maximum(i - 1, 0), p))
    return pl.pallas_call(
        body, name="attn_bwd", grid=(4, nb + 1),
        in_specs=[smem, q, kc, kp, vc, vp, tc, tp, qblk],
        out_specs=[qblk, kvout, kvout, pl.BlockSpec((1, 8, LANES), lambda p, i: (p, 0, 0))],
        out_shape=[jax.ShapeDtypeStruct((t, ATTN_WIDTH), BF16), jax.ShapeDtypeStruct((t, KV_WIDTH), BF16),
                   jax.ShapeDtypeStruct((t, KV_WIDTH), BF16), jax.ShapeDtypeStruct((4, 8, LANES), F32)],
        scratch_shapes=[pltpu.VMEM((ATTN_BLOCK, LANES), F32), pltpu.VMEM((ATTN_BLOCK, LANES), F32)],
        compiler_params=_cp(("parallel", "arbitrary")))(sinks, proj, proj, proj, proj, proj, tables, tables, dout)


def _shift_rows(x, prev8, j):
    n, c = x.shape
    r = pltpu.roll(x.reshape(n // 8, 8, c), j, 1)
    before = pltpu.roll(prev8, j, 0)[None]
    if n > 8:
        before = jnp.concatenate([before, r[:-1]], axis=0)
    return jnp.where(_iota((1, 8, 1), 1) < j, before, r).reshape(n, c)


def _shift_rows_up(x, next8, j):
    n, c = x.shape
    r = pltpu.roll(x.reshape(n // 8, 8, c), 8 - j, 1)
    after = pltpu.roll(next8, 8 - j, 0)[None]
    if n > 8:
        after = jnp.concatenate([r[1:], after], axis=0)
    return jnp.where(_iota((1, 8, 1), 1) >= 8 - j, after, r).reshape(n, c)


def _conv_apply(x, prev8, w, b, taps):
    u = b + x * w[taps - 1:taps]
    for j in range(1, taps):
        u = u + _shift_rows(x, prev8, j) * w[taps - 1 - j:taps - j]
    return u


def _conv_grads(du, du_next8, x, w, taps):
    dx = du * w[taps - 1:taps]
    rowk = _iota((taps, 1), 0)
    dw = jnp.where(rowk == taps - 1, jnp.sum(du * x, axis=0, keepdims=True), 0.0)
    for j in range(1, taps):
        ahead = _shift_rows_up(du, du_next8, j)
        dx = dx + ahead * w[taps - 1 - j:taps - j]
        dw = dw + jnp.where(rowk == taps - 1 - j, jnp.sum(ahead * x, axis=0, keepdims=True), 0.0)
    return dx, dw, jnp.sum(du, axis=0, keepdims=True)


def _conv_specs(tb, tc, col0, t):
    c0 = col0 // tc
    cur = pl.BlockSpec((tb, tc), lambda j, i: (i, c0 + j))
    prev = pl.BlockSpec((8, tc), lambda j, i: (jnp.maximum(i * (tb // 8) - 1, 0), c0 + j))
    nxt = pl.BlockSpec((8, tc), lambda j, i: (jnp.minimum((i + 1) * (tb // 8), t // 8 - 1), c0 + j))
    return cur, prev, nxt


def _conv_silu_fwd(x, w, b, *, col0, width, name):
    t = x.shape[0]
    taps = w.shape[0]
    tb, tc = _rows(t, 512), _tile(width, 1024)
    assert col0 % tc == 0

    def body(x_ref, xp_ref, w_ref, b_ref, o_ref, u_ref):
        i = pl.program_id(1)
        prev8 = jnp.where(i > 0, xp_ref[...], 0.0)
        u = _conv_apply(x_ref[...], prev8, w_ref[...], b_ref[...], taps)
        u_ref[...] = u
        o_ref[...] = u * _sigmoid(u)

    cur, prev, _ = _conv_specs(tb, tc, col0, t)
    par = pl.BlockSpec((taps, tc), lambda j, i: (0, j))
    bias = pl.BlockSpec((1, tc), lambda j, i: (0, j))
    out = pl.BlockSpec((tb, tc), lambda j, i: (i, j))
    shp = jax.ShapeDtypeStruct((t, width), F32)
    return pl.pallas_call(
        body, name=name, grid=(width // tc, t // tb), in_specs=[cur, prev, par, bias], out_specs=[out, out],
        out_shape=[shp, shp], compiler_params=_cp(("parallel", "parallel")))(x, x, w, b)


def _dsilu(u):
    sg = _sigmoid(u)
    return sg * (1.0 + u * (1.0 - sg))


def _ssd_conv_bwd(x, w, dxs, dbm, dcm, *, col0, name):
    t = x.shape[0]
    taps = w.shape[0]
    tb, tc = _rows(t, 512), BC_WIDTH
    nrow, ncol = t // tb, CONV_CH // tc
    c0 = col0 // tc

    def body(x_ref, w_ref, xs_ref, xsn_ref, bm_ref, bmn_ref, cm_ref, cmn_ref, dx_ref, dw_ref, db_ref):
        i = pl.program_id(0)
        j = pl.program_id(1)

        def run(du_ref, dun_ref):
            next8 = jnp.where(i < nrow - 1, dun_ref[...], 0.0)
            dx, dwv, dbv = _conv_grads(du_ref[...], next8, x_ref[...], w_ref[...], taps)
            dx_ref[...] = dx.astype(BF16)

            @pl.when(i == 0)
            def _():
                dw_ref[j] = dwv
                db_ref[j] = dbv

            @pl.when(i > 0)
            def _():
                dw_ref[j] += dwv
                db_ref[j] += dbv

        pl.when(j < 2)(lambda: run(xs_ref, xsn_ref))
        pl.when(j == 2)(lambda: run(bm_ref, bmn_ref))
        pl.when(j == 3)(lambda: run(cm_ref, cmn_ref))

    def nxt_row(i):
        return jnp.minimum((i + 1) * (tb // 8), t // 8 - 1)

    xs_col = lambda j: jnp.minimum(j, SSD_INNER // tc - 1)
    in_specs = [pl.BlockSpec((tb, tc), lambda i, j: (i, c0 + j)), pl.BlockSpec((taps, tc), lambda i, j: (0, j)),
                pl.BlockSpec((tb, tc), lambda i, j: (i, xs_col(j))),
                pl.BlockSpec((8, tc), lambda i, j: (nxt_row(i), xs_col(j))),
                pl.BlockSpec((tb, tc), lambda i, j: (i, 0)), pl.BlockSpec((8, tc), lambda i, j: (nxt_row(i), 0)),
                pl.BlockSpec((tb, tc), lambda i, j: (i, 0)), pl.BlockSpec((8, tc), lambda i, j: (nxt_row(i), 0))]
    dx, dw, db = pl.pallas_call(
        body, name=name, grid=(nrow, ncol), in_specs=in_specs,
        out_specs=[pl.BlockSpec((tb, tc), lambda i, j: (i, c0 + j)),
                   pl.BlockSpec((ncol, taps, tc), lambda i, j: (0, 0, 0)),
                   pl.BlockSpec((ncol, 1, tc), lambda i, j: (0, 0, 0))],
        out_shape=[jax.ShapeDtypeStruct((t, MAIN_WIDTH), BF16), jax.ShapeDtypeStruct((ncol, taps, tc), F32),
                   jax.ShapeDtypeStruct((ncol, 1, tc), F32)],
        compiler_params=_cp(("arbitrary", "arbitrary")))(x, w, dxs, dxs, dbm, dbm, dcm, dcm)
    return dx, dw.transpose(1, 0, 2).reshape(taps, CONV_CH), db.transpose(1, 0, 2).reshape(1, CONV_CH)


def _ffn_specs(tb, tc, t):
    nc = D_FF // tc

    def cur(half):
        return pl.BlockSpec((tb, tc), lambda j, i: (i, half * nc + j))

    def prev(half):
        return pl.BlockSpec((8, tc), lambda j, i: (jnp.maximum(i * (tb // 8) - 1, 0), half * nc + j))

    def nxt(half):
        return pl.BlockSpec((8, tc), lambda j, i: (jnp.minimum((i + 1) * (tb // 8), t // 8 - 1), half * nc + j))

    def par(rows, half):
        return pl.BlockSpec((rows, tc), lambda j, i: (0, half * nc + j))

    return cur, prev, nxt, par


def _ffn_act_fwd(u0, w, b):
    t = u0.shape[0]
    tb, tc = _rows(t, 512), _tile(D_FF, 1408)
    cur, prev, _, par = _ffn_specs(tb, tc, t)

    def body(g_ref, gp_ref, v_ref, vp_ref, wg_ref, wv_ref, bg_ref, bv_ref, o_ref, u_ref):
        i = pl.program_id(1)
        ug = _conv_apply(g_ref[...], jnp.where(i > 0, gp_ref[...], 0.0), wg_ref[...], bg_ref[...], FFN_CONV)
        uv = _conv_apply(v_ref[...], jnp.where(i > 0, vp_ref[...], 0.0), wv_ref[...], bv_ref[...], FFN_CONV)
        o_ref[...] = (ug * _sigmoid(ug) * uv).astype(BF16)
        u_ref[0] = ug
        u_ref[1] = uv

    return pl.pallas_call(
        body, name="ffn_act_fwd", grid=(D_FF // tc, t // tb),
        in_specs=[cur(0), prev(0), cur(1), prev(1), par(FFN_CONV, 0), par(FFN_CONV, 1), par(1, 0), par(1, 1)],
        out_specs=[pl.BlockSpec((tb, tc), lambda j, i: (i, j)), pl.BlockSpec((2, tb, tc), lambda j, i: (0, i, j))],
        out_shape=[jax.ShapeDtypeStruct((t, D_FF), BF16), jax.ShapeDtypeStruct((2, t, D_FF), F32)],
        compiler_params=_cp(("parallel", "parallel")))(u0, u0, u0, u0, w, w, b, b)


def _ffn_act_bwd(u0, u, w, da):
    t = u0.shape[0]
    tb, tc = _rows(t, 256), _tile(D_FF, 1408)
    nrow = t // tb
    taps = FFN_CONV
    cur, _, _, par = _ffn_specs(tb, tc, t)

    def dact(ug, uv, dav):
        sg = _sigmoid(ug)
        return dav * uv * (sg * (1.0 + ug * (1.0 - sg))), dav * ug * sg

    def body(g_ref, v_ref, u_ref, un_ref, wg_ref, wv_ref, da_ref, dan_ref, dx_ref, dw_ref, db_ref):
        i = pl.program_id(1)
        dug, duv = dact(u_ref[0], u_ref[1], da_ref[...].astype(F32))
        dan = jnp.where(i < nrow - 1, dan_ref[...].astype(F32)[:8], 0.0)
        dugn, duvn = dact(un_ref[0], un_ref[1], dan)
        dxg, dwg, dbg = _conv_grads(dug, dugn, g_ref[...], wg_ref[...], taps)
        dxv, dwv, dbv = _conv_grads(duv, duvn, v_ref[...], wv_ref[...], taps)
        dx_ref[0] = dxg.astype(BF16)
        dx_ref[1] = dxv.astype(BF16)

        @pl.when(i == 0)
        def _():
            dw_ref[0] = dwg
            dw_ref[1] = dwv
            db_ref[0] = dbg
            db_ref[1] = dbv

        @pl.when(i > 0)
        def _():
            dw_ref[0] += dwg
            dw_ref[1] += dwv
            db_ref[0] += dbg
            db_ref[1] += dbv

    both = pl.BlockSpec((2, tb, tc), lambda j, i: (0, i, j))
    both_nxt = pl.BlockSpec((2, 8, tc), lambda j, i: (0, jnp.minimum((i + 1) * (tb // 8), t // 8 - 1), j))
    da_cur = pl.BlockSpec((tb, tc), lambda j, i: (i, j))
    da_nxt = pl.BlockSpec((16, tc), lambda j, i: (jnp.minimum((i + 1) * (tb // 16), t // 16 - 1), j))
    return pl.pallas_call(
        body, name="ffn_act_bwd", grid=(D_FF // tc, nrow),
        in_specs=[cur(0), cur(1), both, both_nxt, par(taps, 0), par(taps, 1), da_cur, da_nxt],
        out_specs=[both, pl.BlockSpec((2, taps, tc), lambda j, i: (0, 0, j)),
                   pl.BlockSpec((2, 1, tc), lambda j, i: (0, 0, j))],
        out_shape=[jax.ShapeDtypeStruct((2, t, D_FF), BF16), jax.ShapeDtypeStruct((2, taps, D_FF), F32),
                   jax.ShapeDtypeStruct((2, 1, D_FF), F32)],
        compiler_params=_cp(("parallel", "arbitrary")))(u0, u0, u, u, w, w, da, da)


def _head_masks():
    lane = _iota((1, 4 * SSD_HEAD_DIM), 1)
    return [((lane >= r * SSD_HEAD_DIM) & (lane < (r + 1) * SSD_HEAD_DIM)).astype(F32) for r in range(4)]


def _segsum(v):
    first = _iota((1, LANES), 1) < SSD_HEAD_DIM
    halves = []
    for k in range(2):
        vh = v[:, k * LANES:(k + 1) * LANES]
        both = jnp.sum(vh, axis=1, keepdims=True)
        one = jnp.sum(jnp.where(first, vh, 0.0), axis=1, keepdims=True)
        halves.append(jnp.where(first, one, both - one))
    return jnp.concatenate(halves, axis=1)


def _ssd_common(raw_e, prow, rawr4, bcol, acol):
    n = SSD_CHUNK
    dt_e = _softplus(raw_e + prow[0:1, :])
    a_e = -jnp.exp(prow[1:2, :])
    d_e = prow[2:3, :]
    tril = (_iota((n, n), 0) >= _iota((n, n), 1)).astype(F32)
    acs_e = _dot_exact(tril, dt_e * a_e)
    last_e = acs_e[n - 1:n, :]
    dtr4 = _softplus(rawr4 + bcol)
    triu = (_iota((n, n), 0) <= _iota((n, n), 1)).astype(F32)
    acs_r4 = _dot_exact(dtr4 * (-jnp.exp(acol)), triu)
    return dt_e, a_e, d_e, acs_e, last_e, acs_r4


def _decay_matrix(acs_e, acs_r4, r):
    n = SSD_CHUNK
    col = acs_e[:, r * SSD_HEAD_DIM:r * SSD_HEAD_DIM + 1]
    seg = col - acs_r4[r:r + 1, :]
    causal = _iota((n, n), 0) >= _iota((n, n), 1)
    return jnp.exp(jnp.where(causal, seg, NEG))


SSD_STEP_CHUNKS = 4
SSD_ROWS = SSD_STEP_CHUNKS * SSD_CHUNK


def _ssd_specs(t, rev):
    nb = t // SSD_ROWS
    xb, bb, cb = 0, SSD_INNER // SSD_STATE, (SSD_INNER + BC_WIDTH) // SSD_STATE

    def ch(c):
        return (nb - 1 - c) if rev else c

    x = pl.BlockSpec((SSD_ROWS, 256), lambda g, c: (ch(c), xb + g))
    bm = pl.BlockSpec((SSD_ROWS, SSD_STATE), lambda g, c: (ch(c), bb + g))
    cm = pl.BlockSpec((SSD_ROWS, SSD_STATE), lambda g, c: (ch(c), cb + g))
    dtc = pl.BlockSpec((1, SSD_ROWS, 256), lambda g, c: (g, ch(c), 0))
    dtr = pl.BlockSpec((1, 4, SSD_ROWS), lambda g, c: (g, 0, ch(c)))
    prow = pl.BlockSpec((1, 3, 256), lambda g, c: (g, 0, 0))
    pcol = pl.BlockSpec((1, 4, 1), lambda g, c: (g, 0, 0))
    st = pl.BlockSpec((1, SSD_STEP_CHUNKS, SSD_STATE, 256), lambda g, c: (g, ch(c), 0, 0))
    return x, bm, cm, dtc, dtr, prow, pcol, st, ch


def _ssd_params(dt_raw, dt_bias, a_log, ssd_d):
    t = dt_raw.shape[0]
    by_group = dt_raw.reshape(t, SSD_GROUPS, 4)
    dtc = jnp.repeat(by_group, SSD_HEAD_DIM, axis=2).transpose(1, 0, 2)
    dtr = by_group.transpose(1, 2, 0)
    prow = jnp.repeat(jnp.stack([dt_bias.reshape(SSD_GROUPS, 4), a_log.reshape(SSD_GROUPS, 4),
                                 ssd_d.reshape(SSD_GROUPS, 4)], axis=1), SSD_HEAD_DIM, axis=2)
    bcol = dt_bias.reshape(SSD_GROUPS, 4, 1)
    acol = a_log.reshape(SSD_GROUPS, 4, 1)
    return dtc, dtr, prow, bcol, acol


def _ssd_fwd(xbc, params):
    t = xbc.shape[0]
    nc = t // SSD_CHUNK
    dtc, dtr, prow, bcol, acol = params

    def body(x_ref, b_ref, c_ref, dtc_ref, dtr_ref, prow_ref, bcol_ref, acol_ref, y_ref, st_ref, s_scr):
        c = pl.program_id(1)

        @pl.when(c == 0)
        def _():
            s_scr[...] = jnp.zeros_like(s_scr)

        masks = _head_masks()
        s = s_scr[...]
        for k in range(SSD_STEP_CHUNKS):
            rows = slice(k * SSD_CHUNK, (k + 1) * SSD_CHUNK)
            dt_e, a_e, d_e, acs_e, last_e, acs_r4 = _ssd_common(
                dtc_ref[0, rows], prow_ref[0], dtr_ref[0][:, rows], bcol_ref[0], acol_ref[0])
            xv = x_ref[rows]
            bm, cm = b_ref[rows], c_ref[rows]
            st_ref[0, k] = s
            xdt = xv * dt_e
            cb = _dot(cm, bm, 'nt')
            y = _dot(cm, s) * jnp.exp(acs_e) + xv * d_e
            for r in range(4):
                mr = cb * _decay_matrix(acs_e, acs_r4, r)
                y = y + _dot(mr, xdt * masks[r])
            y_ref[rows] = y
            w = xdt * jnp.exp(last_e - acs_e)
            s = s * jnp.exp(last_e) + _dot(bm.T, w)
        s_scr[...] = s

    x, bm, cm, dtcs, dtrs, prs, pcs, st, _ = _ssd_specs(t, False)
    return pl.pallas_call(
        body, name="ssd_fwd", grid=(SSD_GROUPS, t // SSD_ROWS), in_specs=[x, bm, cm, dtcs, dtrs, prs, pcs, pcs],
        out_specs=[pl.BlockSpec((SSD_ROWS, 256), lambda g, c: (c, g)), st],
        out_shape=[jax.ShapeDtypeStruct((t, SSD_INNER), F32),
                   jax.ShapeDtypeStruct((SSD_GROUPS, nc, SSD_STATE, 256), F32)],
        scratch_shapes=[pltpu.VMEM((SSD_STATE, 256), F32)],
        compiler_params=_cp(("parallel", "arbitrary")))(xbc, xbc, xbc, dtc, dtr, prow, bcol, acol)


def _ssd_bwd(xbc, pre, params, states, dy):
    t = xbc.shape[0]
    nc = t // SSD_CHUNK
    n = SSD_CHUNK
    dtc, dtr, prow, bcol, acol = params

    def body(x_ref, b_ref, c_ref, ux_ref, ub_ref, uc_ref, dtc_ref, dtr_ref, prow_ref, bcol_ref, acol_ref, st_ref,
             dy_ref, dx_ref, db_ref, dc_ref, ddt_ref, dp_ref, ds_scr):
        c = pl.program_id(1)

        @pl.when(c == 0)
        def _():
            ds_scr[...] = jnp.zeros_like(ds_scr)
            dp_ref[...] = jnp.zeros_like(dp_ref)

        masks = _head_masks()
        ds = ds_scr[...]
        for k in reversed(range(SSD_STEP_CHUNKS)):
            rows = slice(k * SSD_CHUNK, (k + 1) * SSD_CHUNK)
            raw_e = dtc_ref[0, rows]
            prw = prow_ref[0]
            dt_e, a_e, d_e, acs_e, last_e, acs_r4 = _ssd_common(raw_e, prw, dtr_ref[0][:, rows], bcol_ref[0], acol_ref[0])
            xv = x_ref[rows]
            bm, cm = b_ref[rows], c_ref[rows]
            s = st_ref[0, k]
            dyv = dy_ref[rows]
            e_e = jnp.exp(acs_e)
            dec_e = jnp.exp(last_e - acs_e)
            cd_e = jnp.exp(last_e)
            xdt = xv * dt_e
            w = xdt * dec_e
            b16, c16, s16, ds16 = bm.astype(BF16), cm.astype(BF16), s.astype(BF16), ds.astype(BF16)
            cb = _dot(c16, b16, 'nt')
            yoff_raw = _dot(c16, s16)
            dye = dyv * e_e
            dye16 = dye.astype(BF16)
            dcm = _dot(dye16, s16, 'nt')
            ds_prev = ds * cd_e + _dot(cm.T, dye16)
            dacs_e = _segsum(dyv * yoff_raw) * e_e
            dw = _dot(b16, ds16)
            dbm = _dot(w, ds16, 'nt')
            tdec = _segsum(dw * xdt) * dec_e
            dacs_e = dacs_e - tdec
            dlast_e = jnp.sum(tdec, axis=0, keepdims=True)
            dxdt = dw * dec_e
            dlast_e = dlast_e + _segsum(jnp.sum(ds * s, axis=0, keepdims=True)) * cd_e
            dcb = jnp.zeros((n, n), F32)
            for r in range(4):
                lm = _decay_matrix(acs_e, acs_r4, r)
                mr = cb * lm
                dyr16 = (dyv * masks[r]).astype(BF16)
                dm = _dot(dyr16, xdt * masks[r], 'nt')
                dcb = dcb + dm * lm
                dseg = dm * mr
                dcol = jnp.sum(dseg, axis=1, keepdims=True) - jnp.sum(dseg.T, axis=1, keepdims=True)
                dacs_e = dacs_e + dcol * masks[r]
                dxdt = dxdt + _dot(mr.T, dyr16)
            dcm = dcm + _dot(dcb, b16)
            dbm = dbm + _dot(dcb.T, c16)
            dacs_e = dacs_e + jnp.where(_iota((n, 1), 0) == n - 1, dlast_e, 0.0)
            triu = (_iota((n, n), 0) <= _iota((n, n), 1)).astype(F32)
            ddta_e = _dot_exact(triu, dacs_e)
            ddt_e = ddta_e * a_e + _segsum(dxdt * xv)
            dx_ref[rows] = (dxdt * dt_e + dyv * d_e) * _dsilu(ux_ref[rows])
            db_ref[rows] = dbm * _dsilu(ub_ref[rows])
            dc_ref[rows] = dcm * _dsilu(uc_ref[rows])
            draw_e = ddt_e * _sigmoid(raw_e + prw[0:1, :])
            draw_t = draw_e.T
            ddt_ref[0, :, rows] = jnp.concatenate([draw_t[r * SSD_HEAD_DIM:r * SSD_HEAD_DIM + 1] for r in range(4)], axis=0)
            dbias = jnp.sum(draw_e, axis=0, keepdims=True)
            dalog = jnp.sum(ddta_e * dt_e, axis=0, keepdims=True) * a_e
            dd = _segsum(jnp.sum(dyv * xv, axis=0, keepdims=True))
            row3 = _iota((3, 1), 0)
            dp_ref[0] += (jnp.where(row3 == 0, dbias, 0.0) + jnp.where(row3 == 1, dalog, 0.0)
                          + jnp.where(row3 == 2, dd, 0.0))
            ds = ds_prev
        ds_scr[...] = ds


    x, bm, cm, dtcs, dtrs, prs, pcs, st, ch = _ssd_specs(t, True)
    yblk = pl.BlockSpec((SSD_ROWS, 256), lambda g, c: (ch(c), g))
    nblk = pl.BlockSpec((SSD_ROWS, SSD_STATE), lambda g, c: (ch(c), g))
    return pl.pallas_call(
        body, name="ssd_bwd", grid=(SSD_GROUPS, t // SSD_ROWS),
        in_specs=[x, bm, cm, x, bm, cm, dtcs, dtrs, prs, pcs, pcs, st, yblk],
        out_specs=[yblk, nblk, nblk, dtrs, prs],
        out_shape=[jax.ShapeDtypeStruct((t, SSD_INNER), F32), jax.ShapeDtypeStruct((t, BC_WIDTH), F32),
                   jax.ShapeDtypeStruct((t, BC_WIDTH), F32), jax.ShapeDtypeStruct((SSD_GROUPS, 4, t), F32),
                   jax.ShapeDtypeStruct((SSD_GROUPS, 3, 256), F32)],
        scratch_shapes=[pltpu.VMEM((SSD_STATE, 256), F32)],
        compiler_params=_cp(("parallel", "arbitrary")))(xbc, xbc, xbc, pre, pre, pre, dtc, dtr, prow, bcol, acol,
                                                         states, dy)


GROUP_W = SSD_INNER // SSD_GROUPS


def _mix_specs(tb):
    row = pl.BlockSpec((tb, 2048), lambda i: (i, 0))
    zlo = pl.BlockSpec((tb, 1024), lambda i: (i, O_Z // 1024))
    zhi = pl.BlockSpec((tb, 1024), lambda i: (i, O_Z // 1024 + 1))
    vec = pl.BlockSpec((1, 2048), lambda i: (0, 0))
    return row, zlo, zhi, vec


def _mix_fwd(attn, y, proj, g_attn, g_ssd):
    t = attn.shape[0]
    tb = _rows(t, 256)

    def body(a_ref, y_ref, zlo_ref, zhi_ref, ga_ref, gs_ref, o_ref):
        av = a_ref[...]
        r = lax.rsqrt(jnp.mean(av * av, axis=-1, keepdims=True) + EPS)
        o_ref[:, :ATTN_WIDTH] = (av * r * ga_ref[...]).astype(BF16)
        for g in range(SSD_GROUPS):
            lo, hi = g * GROUP_W, (g + 1) * GROUP_W
            zref = zlo_ref if g < 4 else zhi_ref
            z = zref[:, lo % 1024:lo % 1024 + GROUP_W]
            yg = y_ref[:, lo:hi] * (z * _sigmoid(z))
            rg = lax.rsqrt(jnp.mean(yg * yg, axis=-1, keepdims=True) + EPS)
            o_ref[:, ATTN_WIDTH + lo:ATTN_WIDTH + hi] = (yg * rg * gs_ref[:, lo:hi]).astype(BF16)

    row, zlo, zhi, vec = _mix_specs(tb)
    return pl.pallas_call(
        body, name="mix_fwd", grid=(t // tb,), in_specs=[row, row, zlo, zhi, vec, vec],
        out_specs=pl.BlockSpec((tb, 4096), lambda i: (i, 0)), out_shape=jax.ShapeDtypeStruct((t, 4096), BF16),
        compiler_params=_cp(("parallel",)))(attn, y, proj, proj, g_attn, g_ssd)


def _mix_bwd(dmix, attn, y, proj, g_attn, g_ssd):
    t = attn.shape[0]
    tb = _rows(t, 256)

    def body(dm_ref, a_ref, y_ref, zlo_ref, zhi_ref, ga_ref, gs_ref, da_ref, dy_ref, dz_ref, dga_ref, dgs_ref):
        i = pl.program_id(0)
        av = a_ref[...]
        dn = dm_ref[:, :ATTN_WIDTH].astype(F32)
        r = lax.rsqrt(jnp.mean(av * av, axis=-1, keepdims=True) + EPS)
        u = dn * ga_ref[...]
        da_ref[...] = r * u - av * (r * r * r * jnp.mean(u * av, axis=-1, keepdims=True))
        dga = jnp.sum(dn * av * r, axis=0, keepdims=True)

        @pl.when(i == 0)
        def _():
            dga_ref[...] = dga

        @pl.when(i > 0)
        def _():
            dga_ref[...] += dga

        for g in range(SSD_GROUPS):
            lo, hi = g * GROUP_W, (g + 1) * GROUP_W
            zref = zlo_ref if g < 4 else zhi_ref
            z = zref[:, lo % 1024:lo % 1024 + GROUP_W]
            yv = y_ref[:, lo:hi]
            sg = _sigmoid(z)
            sz = z * sg
            yg = yv * sz
            rg = lax.rsqrt(jnp.mean(yg * yg, axis=-1, keepdims=True) + EPS)
            do = dm_ref[:, ATTN_WIDTH + lo:ATTN_WIDTH + hi].astype(F32)
            ug = do * gs_ref[:, lo:hi]
            dyg = rg * ug - yg * (rg * rg * rg * jnp.mean(ug * yg, axis=-1, keepdims=True))
            dy_ref[:, lo:hi] = dyg * sz
            dz_ref[:, lo:hi] = (dyg * yv * (sg * (1.0 + z * (1.0 - sg)))).astype(BF16)
            dgs = jnp.sum(do * yg * rg, axis=0, keepdims=True)

            @pl.when(i == 0)
            def _():
                dgs_ref[:, lo:hi] = dgs

            @pl.when(i > 0)
            def _():
                dgs_ref[:, lo:hi] += dgs

    row, zlo, zhi, vec = _mix_specs(tb)
    return pl.pallas_call(
        body, name="mix_bwd", grid=(t // tb,),
        in_specs=[pl.BlockSpec((tb, 4096), lambda i: (i, 0)), row, row, zlo, zhi, vec, vec],
        out_specs=[row, row, row, vec, vec],
        out_shape=[jax.ShapeDtypeStruct((t, 2048), F32), jax.ShapeDtypeStruct((t, 2048), F32),
                   jax.ShapeDtypeStruct((t, 2048), BF16), jax.ShapeDtypeStruct((1, 2048), F32),
                   jax.ShapeDtypeStruct((1, 2048), F32)],
        compiler_params=_cp(("arbitrary",)))(dmix, attn, y, proj, proj, g_attn, g_ssd)


def _adamw(w, g, m, v, name):
    r, c = w.shape
    tb = _rows(r, 256)
    c1 = 1.0 - ADAM_B1 ** ADAM_STEP
    c2 = 1.0 - ADAM_B2 ** ADAM_STEP

    def body(w_ref, g_ref, m_ref, v_ref, d_ref, m2_ref, v2_ref):
        gv = g_ref[...]
        m2 = ADAM_B1 * m_ref[...] + (1.0 - ADAM_B1) * gv
        v2 = ADAM_B2 * v_ref[...] + (1.0 - ADAM_B2) * (gv * gv)
        d_ref[...] = -ADAM_LR * ((m2 / c1) / (jnp.sqrt(v2 / c2) + ADAM_EPS) + ADAM_WD * w_ref[...])
        m2_ref[...] = m2
        v2_ref[...] = v2

    blk = pl.BlockSpec((tb, c), lambda i: (i, 0))
    shp = jax.ShapeDtypeStruct((r, c), F32)
    return pl.pallas_call(body, name=name, grid=(r // tb,), in_specs=[blk] * 4, out_specs=[blk] * 3,
                          out_shape=[shp] * 3, compiler_params=_cp(("parallel",)))(w, g, m, v)


def _adamw_halves(w, mine, theirs, m, v, pos, name, cols=False):
    r, c = w.shape
    h = r if cols else r // 2
    tb = _rows(h, 128)
    nh = h // tb
    c1 = 1.0 - ADAM_B1 ** ADAM_STEP
    c2 = 1.0 - ADAM_B2 ** ADAM_STEP

    def body(pos_ref, w_ref, a_ref, b_ref, m_ref, v_ref, g_ref, d_ref, m2_ref, v2_ref):
        which = pl.program_id(1) if cols else pl.program_id(0) // nh
        gv = jnp.where(which == pos_ref[0], a_ref[...], b_ref[...])
        m2 = ADAM_B1 * m_ref[...] + (1.0 - ADAM_B1) * gv
        v2 = ADAM_B2 * v_ref[...] + (1.0 - ADAM_B2) * (gv * gv)
        g_ref[...] = gv
        d_ref[...] = -ADAM_LR * ((m2 / c1) / (jnp.sqrt(v2 / c2) + ADAM_EPS) + ADAM_WD * w_ref[...])
        m2_ref[...] = m2
        v2_ref[...] = v2

    if cols:
        full = pl.BlockSpec((tb, c // 2), lambda i, j, pref: (i, j))
        mine_spec = theirs_spec = pl.BlockSpec((tb, c // 2), lambda i, j, pref: (i, 0))
        grid = (nh, 2)
    else:
        full = pl.BlockSpec((tb, c), lambda i, pref: (i, 0))
        mine_spec = pl.BlockSpec((tb, c), lambda i, pref: (jnp.where(i // nh == pref[0], i % nh,
                                                                     jnp.where(pref[0] == 0, nh - 1, 0)), 0))
        theirs_spec = pl.BlockSpec((tb, c), lambda i, pref: (jnp.where(i // nh != pref[0], i % nh,
                                                                       jnp.where(pref[0] == 0, 0, nh - 1)), 0))
        grid = (r // tb,)
    shp = jax.ShapeDtypeStruct((r, c), F32)
    grid_spec = pltpu.PrefetchScalarGridSpec(num_scalar_prefetch=1, grid=grid,
                                             in_specs=[full, mine_spec, theirs_spec, full, full],
                                             out_specs=[full] * 4)
    return pl.pallas_call(body, name=name, grid_spec=grid_spec, out_shape=[shp] * 4,
                          compiler_params=_cp(("parallel",) * len(grid)))(pos, w, mine, theirs, m, v)


def _sum_own_half(g4, recv, pos, name, cols=False):
    _, r, c = g4.shape
    h, c = (r, c // 2) if cols else (r // 2, c)
    tb = _rows(h, 128)
    nh = h // tb

    def slot(j, pref):
        return (pref[1] + 1 + j) % N_CHIPS

    if cols:
        own = lambda j, i, pref: (slot(j, pref), i, pref[0])
    else:
        own = lambda j, i, pref: (slot(j, pref), pref[0] * nh + i, 0)
    same = lambda j, i, pref: (slot(j, pref), i, 0)

    def body(pos_ref, a_ref, b_ref, o_ref):
        o_ref[...] = (a_ref[...] + b_ref[...]).astype(BF16)

    grid_spec = pltpu.PrefetchScalarGridSpec(
        num_scalar_prefetch=1, grid=(N_CHIPS - 1, nh),
        in_specs=[pl.BlockSpec((1, tb, c), own), pl.BlockSpec((1, tb, c), same)],
        out_specs=pl.BlockSpec((1, tb, c), same))
    return pl.pallas_call(body, name=name, grid_spec=grid_spec,
                          out_shape=jax.ShapeDtypeStruct((N_CHIPS, h, c), BF16),
                          compiler_params=_cp(("parallel", "parallel")))(pos, g4, recv)


def _sum_chips(g4, recv, parts, pos, name, cols=False):
    _, r, c = g4.shape
    h, c = (r, c // 2) if cols else (r // 2, c)
    tb = _rows(h, 128)
    nh = h // tb
    own = (lambda i, pref: (pref[1], i, pref[0])) if cols else (lambda i, pref: (pref[1], pref[0] * nh + i, 0))

    def body(pos_ref, a_ref, b_ref, p_ref, o_ref):
        own = a_ref[0] + b_ref[0]
        o_ref[...] = ((own + p_ref[0].astype(F32)) + p_ref[1].astype(F32)) + p_ref[2].astype(F32)

    grid_spec = pltpu.PrefetchScalarGridSpec(
        num_scalar_prefetch=1, grid=(nh,),
        in_specs=[pl.BlockSpec((1, tb, c), own),
                  pl.BlockSpec((1, tb, c), lambda i, pref: (pref[1], i, 0)),
                  pl.BlockSpec((3, tb, c), lambda i, pref: (0, i, 0))],
        out_specs=pl.BlockSpec((tb, c), lambda i, pref: (i, 0)))
    return pl.pallas_call(body, name=name, grid_spec=grid_spec, out_shape=jax.ShapeDtypeStruct((h, c), F32),
                          compiler_params=_cp(("parallel",)))(pos, g4, recv, parts)


def _me():
    return lax.axis_index("x"), lax.axis_index("y"), lax.axis_index("c")


def _flip(v, bit):
    return (1 - v) if bit else v


CHIP_FLIPS = [(1, 0), (0, 1), (1, 1)]


def _forward_halves(gathered):
    def body(g_ref, o_ref, token, send_sems, recv_sems):
        x, y, c = _me()
        h = g_ref.shape[2] // 2
        cps = []
        for k, (fx, fy) in enumerate(CHIP_FLIPS):
            peer_chip = 2 * _flip(x, fx) + _flip(y, fy)
            mine = o_ref.at[peer_chip, :, pl.ds(c * h, h)]
            cp = pltpu.make_async_remote_copy(src_ref=mine, dst_ref=mine, send_sem=send_sems.at[k],
                                              recv_sem=recv_sems.at[k], device_id=(x, y, 1 - c), device_id_type=MESH)
            cp.start()
            cps.append(cp)
        for k, (fx, fy) in enumerate(CHIP_FLIPS):
            peer_chip = 2 * _flip(x, fx) + _flip(y, fy)
            theirs = o_ref.at[peer_chip, :, pl.ds((1 - c) * h, h)]
            pltpu.make_async_remote_copy(src_ref=theirs, dst_ref=theirs, send_sem=send_sems.at[k],
                                         recv_sem=recv_sems.at[k], device_id=(x, y, 1 - c),
                                         device_id_type=MESH).wait_recv()
        for cp in cps:
            cp.wait_send()
        token[...] = jnp.zeros_like(token)

    return pl.pallas_call(
        body, name="gather_forward_w_in", in_specs=[HBM_SPEC],
        out_specs=[HBM_SPEC, pl.BlockSpec(memory_space=pltpu.VMEM)],
        out_shape=[jax.ShapeDtypeStruct(gathered.shape, gathered.dtype), TOKEN],
        scratch_shapes=[pltpu.SemaphoreType.DMA((3,)), pltpu.SemaphoreType.DMA((3,))],
        input_output_aliases={0: 0},
        compiler_params=pltpu.CompilerParams(has_side_effects=True))(gathered)


SEM_SPEC = pl.BlockSpec(memory_space=pltpu.SEMAPHORE)
ANY_SPEC = pl.BlockSpec(memory_space=pl.ANY)
DATAFLOW = pltpu.SideEffectType.DATAFLOW_SIDE_EFFECTING


def _in_hbm(a):
    return pltpu.with_memory_space_constraint(a, pltpu.HBM)


def _push_start(srcs, land_shapes, route, peers, name):
    n, npeer = len(srcs), len(peers)
    lands = [lax.empty(shp, s.dtype) for shp, s in zip(land_shapes, srcs)]

    def body(*refs):
        ins, lnd = refs[:n], refs[n:2 * n]
        send_sems, recv_sems = refs[2 * n], refs[2 * n + 1]
        token = refs[-1]
        x, y, c = _me()
        for t in range(n):
            for k, (fx, fy, fc) in enumerate(peers):
                src, dst = route(ins[t], lnd[t], k, x, y, c)
                pltpu.make_async_remote_copy(
                    src_ref=src, dst_ref=dst, send_sem=send_sems.at[npeer * t + k],
                    recv_sem=recv_sems.at[npeer * t + k],
                    device_id=(_flip(x, fx), _flip(y, fy), _flip(c, fc)), device_id_type=MESH).start()
        token[...] = jnp.zeros_like(token)

    bufs = [_in_hbm(a) for a in list(srcs) + lands]
    outs = pl.pallas_call(
        body, name=name,
        out_shape=(pltpu.SemaphoreType.DMA((npeer * n,)), pltpu.SemaphoreType.DMA((npeer * n,)),
                   *[pltpu.HBM(b.shape, b.dtype) for b in bufs], TOKEN),
        in_specs=[HBM_SPEC] * (2 * n),
        out_specs=(SEM_SPEC, SEM_SPEC, *[HBM_SPEC] * (2 * n), pl.BlockSpec(memory_space=pltpu.VMEM)),
        input_output_aliases={i: 2 + i for i in range(2 * n)},
        compiler_params=pltpu.CompilerParams(has_side_effects=DATAFLOW))(*bufs)
    return outs[0], outs[1], list(outs[2:2 + n]), list(outs[2 + n:2 + 2 * n]), outs[-1]


def _push_wait(send_sems, recv_sems, srcs, lands, after, route, peers, name):
    n, npeer = len(srcs), len(peers)

    def body(*refs):
        ins, lnd = refs[:n], refs[n:2 * n]
        ssem, rsem = refs[2 * n], refs[2 * n + 1]
        x, y, c = _me()
        for t in range(n):
            for k, (fx, fy, fc) in enumerate(peers):
                src, dst = route(ins[t], lnd[t], k, x, y, c)
                cp = pltpu.make_async_remote_copy(
                    src_ref=src, dst_ref=dst, send_sem=ssem.at[npeer * t + k], recv_sem=rsem.at[npeer * t + k],
                    device_id=(_flip(x, fx), _flip(y, fy), _flip(c, fc)), device_id_type=MESH)
                cp.wait_send()
                cp.wait_recv()

    bufs = list(srcs) + list(lands)
    outs = pl.pallas_call(
        body, name=name, out_shape=tuple(pltpu.HBM(b.shape, b.dtype) for b in bufs),
        in_specs=[HBM_SPEC] * (2 * n) + [SEM_SPEC, SEM_SPEC, ANY_SPEC], out_specs=tuple([HBM_SPEC] * (2 * n)),
        input_output_aliases={i: i for i in range(2 * n)},
        compiler_params=pltpu.CompilerParams(has_side_effects=DATAFLOW))(*bufs, send_sems, recv_sems, after)
    return list(outs[:n]), list(outs[n:])


OTHER_CHIPS = [(fx, fy, 0) for fx, fy in CHIP_FLIPS]
SIBLING = [(0, 0, 1)]


def _route_gather(src, land, k, x, y, c):
    return src, land.at[2 * x + y]


def _route_gather_half(src, land, k, x, y, c):
    h = src.shape[1] // 2
    return src.at[:, pl.ds(c * h, h)], land.at[2 * x + y, :, pl.ds(c * h, h)]


def _route_gather_half_wait(src, land, k, x, y, c):
    fx, fy = CHIP_FLIPS[k]
    h = src.shape[1] // 2
    return src.at[:, pl.ds(c * h, h)], land.at[2 * _flip(x, fx) + _flip(y, fy), :, pl.ds(c * h, h)]


def _route_gather_wait(src, land, k, x, y, c):
    fx, fy = CHIP_FLIPS[k]
    return src, land.at[2 * _flip(x, fx) + _flip(y, fy)]


def _route_scatter(src, land, k, x, y, c):
    fx, fy = CHIP_FLIPS[k]
    return src.at[2 * _flip(x, fx) + _flip(y, fy)], land.at[k]


def _route_exchange(src, land, k, x, y, c):
    h = land.shape[1]
    return src.at[:, pl.ds((1 - c) * h, h)], land


def _route_whole(src, land, k, x, y, c):
    return src, land


def _route_exchange_cols(src, land, k, x, y, c):
    h = land.shape[2]
    return src.at[:, :, pl.ds((1 - c) * h, h)], land


ALL_OTHERS = [((k >> 2) & 1, (k >> 1) & 1, k & 1) for k in range(1, 8)]


def _route_to_all(src, land, k, x, y, c):
    return src, land.at[4 * x + 2 * y + c]


def _route_to_all_wait(src, land, k, x, y, c):
    fx, fy, fc = ALL_OTHERS[k]
    return src, land.at[4 * _flip(x, fx) + 2 * _flip(y, fy) + _flip(c, fc)]


def _sum_devices(parts):
    def body(p_ref, o_ref):
        acc = p_ref[0]
        for d in range(1, 8):
            acc = acc + p_ref[d]
        o_ref[...] = acc

    vm = pl.BlockSpec(memory_space=pltpu.VMEM)
    return pl.pallas_call(body, name="allreduce_sum", in_specs=[vm], out_specs=vm,
                          out_shape=jax.ShapeDtypeStruct(parts.shape[1:], F32),
                          compiler_params=pltpu.CompilerParams(vmem_limit_bytes=VMEM_LIMIT))(parts)


def _grad_exchange_start(g4, tag, cols=False):
    land = (N_CHIPS, g4.shape[1], g4.shape[2] // 2) if cols else (N_CHIPS, g4.shape[1] // 2, g4.shape[2])
    route = _route_exchange_cols if cols else _route_exchange
    send_sems, recv_sems, srcs, lands, token = _push_start(
        [g4], [land], route, SIBLING, name="grad_exchange_start_" + tag)
    return (send_sems, recv_sems, srcs, lands, tag, cols), token


def _grad_scatter_start(state, pos, after):
    send_sems, recv_sems, srcs, lands, tag, cols = state
    route = _route_exchange_cols if cols else _route_exchange
    (g4,), (recv,) = _push_wait(send_sems, recv_sems, srcs, lands, after, route, SIBLING,
                                name="grad_exchange_wait_" + tag)
    return _grad_pair_scatter(g4, recv, pos, tag, cols)


def _grad_pair_scatter(g4, recv, pos, tag, cols=False):
    p16 = _sum_own_half(g4, recv, pos, name="grad_sum_pair_" + tag, cols=cols)
    send_sems, recv_sems, srcs, lands, token = _push_start(
        [p16], [(3,) + p16.shape[1:]], _route_scatter, OTHER_CHIPS, name="grad_scatter_start_" + tag)
    return (g4, recv, send_sems, recv_sems, srcs, lands, tag, cols), token


def _grad_sum_and_share(state, pos, after):
    g4, recv, send_sems, recv_sems, srcs, lands, tag, cols = state
    parts = _push_wait(send_sems, recv_sems, srcs, lands, after, _route_scatter, OTHER_CHIPS,
                       name="grad_scatter_wait_" + tag)[1][0]
    mine = _sum_chips(g4, recv, parts, pos, name="grad_sum_chips_" + tag, cols=cols)
    send_sems, recv_sems, srcs, lands, token = _push_start(
        [mine], [mine.shape], _route_whole, SIBLING, name="grad_share_start_" + tag)
    return (send_sems, recv_sems, srcs, lands, tag), token


def _grad_share_wait(state, after):
    send_sems, recv_sems, srcs, lands, tag = state
    (mine,), (theirs,) = _push_wait(send_sems, recv_sems, srcs, lands, after, _route_whole, SIBLING,
                                    name="grad_share_wait_" + tag)
    return mine, theirs


def _local_step(x, tgt, p, hooks):
    t = x.shape[0]
    tables = _rope_tables(t)
    sinks = p['sinks'].reshape(N_Q_HEADS)

    def told(name, value):
        return tuple(hooks.grad_ready(name, value))

    xn = _rmsnorm_fwd(x, p['norm_mix'], "norm_mix_fwd", deps=hooks.first_deps)
    w_in_t, w_in_dt, in_deps = hooks.weight_in(xn)
    proj = _matmul(xn, w_in_t, mode='nt', name="in_proj", n_limit=MAIN_WIDTH, deps=in_deps)
    dt_raw = _matmul(xn, w_in_dt, mode='nt', name="in_proj_dt")[:, :SSD_HEADS]
    ssd_conv_w, ffn_conv_w = hooks.conv_weights(proj)
    p = dict(p, ssd_conv_w=ssd_conv_w, ffn_conv_w=ffn_conv_w)
    attn = _attn_fwd(proj, sinks, tables)
    conv_b = p['ssd_conv_b']
    xbc, xbc_pre = _conv_silu_fwd(proj, p['ssd_conv_w'], conv_b, col0=O_XBC, width=CONV_CH, name="ssd_conv_fwd")
    sp = _ssd_params(dt_raw, p['dt_bias'].reshape(-1), p['a_log'].reshape(-1), p['ssd_d'].reshape(-1))
    y, states = _ssd_fwd(xbc, sp)
    mix = _mix_fwd(attn, y, proj, p['attn_out_norm'], p['ssd_norm'])
    w_out = hooks.weight('w_out', mix)
    h1 = _matmul(mix, w_out, mode='nn', name="out_proj", add=x)
    hn = _rmsnorm_fwd(h1, p['norm_ffn'], "norm_ffn_fwd")
    w_up = hooks.weight('w_up', hn)
    u0 = _matmul(hn, w_up, mode='nn', name="ffn_up", b_owner=True, tn=1408)
    a, u = _ffn_act_fwd(u0, p['ffn_conv_w'], p['ffn_conv_b'])
    w_down = hooks.weight('w_down', a)
    h2 = _matmul(a, w_down, mode='nn', name="ffn_down", add=h1, tk=2816)
    loss, dh2, dh2_16, g_norm_final = _final_loss(h2, p['norm_final'].reshape(1, D_MODEL), tgt)

    g = {}
    da = _matmul(dh2_16, w_down, mode='nt', name="ffn_down_dx", out_dtype=BF16, tn=1408)
    g['w_down'] = _matmul(a, dh2_16, mode='tn', name="ffn_down_dw", tm=1408)
    dep = told('w_down', g['w_down'])
    du0, dcw, dcb = _ffn_act_bwd(u0, u, p['ffn_conv_w'], da)
    g['ffn_conv_w'] = dcw.transpose(1, 0, 2).reshape(FFN_CONV, 2 * D_FF)
    g['ffn_conv_b'] = dcb.transpose(1, 0, 2).reshape(1, 2 * D_FF)
    g['w_up'] = _matmul(hn, du0, mode='tn', name="ffn_up_dw", deps=dep, b_halves=True, owner_major=True,
                        tn=1408)
    dep = told('w_up', g['w_up'])
    dhn = _matmul(du0, w_up, mode='nt', name="ffn_up_dx", out_dtype=BF16, deps=dep, a_halves=True,
                  b_owner=True, tk=2816)
    dh1, dh1_16, g['norm_ffn'] = _rmsnorm_bwd(h1, p['norm_ffn'], dhn, dh2, "norm_ffn_bwd")

    g['w_out'] = _matmul(mix, dh1_16, mode='tn', name="out_proj_dw")
    dep = told('w_out', g['w_out'])
    dmix = _matmul(dh1_16, w_out, mode='nt', name="out_proj_dx", out_dtype=BF16, deps=dep)
    dattn, dy, dz, g['attn_out_norm'], g['ssd_norm'] = _mix_bwd(dmix, attn, y, proj, p['attn_out_norm'],
                                                                p['ssd_norm'])
    dq, dk, dv, dsink = _attn_bwd(proj, sinks, tables, dattn)
    g['sinks'] = dsink[:, :, 0].reshape(1, N_Q_HEADS)
    dxs, dbm, dcm, ddt8, dpar = _ssd_bwd(xbc, xbc_pre, sp, states, dy)
    dpar = dpar[:, :, ::SSD_HEAD_DIM]
    g['dt_bias'] = dpar[:, 0, :].reshape(1, SSD_HEADS)
    g['a_log'] = dpar[:, 1, :].reshape(1, SSD_HEADS)
    g['ssd_d'] = dpar[:, 2, :].reshape(1, SSD_HEADS)
    dproj, g['ssd_conv_w'], g['ssd_conv_b'] = _ssd_conv_bwd(proj, p['ssd_conv_w'], dxs, dbm, dcm, col0=O_XBC,
                                                            name="ssd_conv_bwd")
    for piece, col in ((dq, O_Q), (dk, O_K), (dv, O_V), (dz, O_Z)):
        dproj = lax.dynamic_update_slice(dproj, piece, (0, col))
    ddt = ddt8.transpose(2, 0, 1).reshape(t, SSD_HEADS)
    ddt_pad = jnp.pad(ddt, ((0, 0), (0, LANES - SSD_HEADS))).astype(BF16)
    g['w_in'] = (_matmul(dproj, xn, mode='tn', name="in_proj_dw", m_rows=IN_PROJ_WIDTH),
                 _matmul(ddt_pad, xn, mode='tn', name="in_proj_dt_dw"))
    dep = told('w_in', g['w_in'])
    dxn_dt = _matmul(ddt_pad, w_in_dt, mode='nn', name="in_proj_dt_dx", deps=dep)
    dxn = _matmul(dproj, w_in_t, mode='nn', name="in_proj_dx", out_dtype=BF16, add=dxn_dt, k_limit=MAIN_WIDTH,
                  tk=2304)
    dep = told(None, dxn)
    dx, _, g['norm_mix'] = _rmsnorm_bwd(x, p['norm_mix'], dxn, dh1, "norm_mix_bwd", deps=dep)
    g['norm_final'] = g_norm_final
    return loss, dx, g


def _pack(arrs):
    flat = jnp.concatenate([a.reshape(-1) for a in arrs])
    n = flat.shape[0]
    rows = -(-n // LANES)
    rows = -(-rows // 8) * 8
    return jnp.pad(flat, (0, rows * LANES - n)).reshape(rows, LANES)


def _unpack(packed, shapes):
    flat = packed.reshape(-1)
    out, off = [], 0
    for s in shapes:
        n = 1
        for d in s:
            n *= d
        out.append(flat[off:off + n].reshape(s))
        off += n
    return out


class _StepHooks:
    def __init__(self, first_deps, weight_in, conv_weights, weight, grad_ready):
        self.first_deps = first_deps
        self.weight_in = weight_in
        self.conv_weights = conv_weights
        self.weight = weight
        self.grad_ready = grad_ready


def kernel(x, norm_mix, w_in, sinks, attn_out_norm, ssd_conv_w, ssd_conv_b, dt_bias, a_log, ssd_d, ssd_norm, w_out, norm_ffn, w_up, ffn_conv_w, ffn_conv_b, w_down, norm_final, loss_target, m_norm_mix, m_w_in, m_sinks, m_attn_out_norm, m_ssd_conv_w, m_ssd_conv_b, m_dt_bias, m_a_log, m_ssd_d, m_ssd_norm, m_w_out, m_norm_ffn, m_w_up, m_ffn_conv_w, m_ffn_conv_b, m_w_down, m_norm_final, v_norm_mix, v_w_in, v_sinks, v_attn_out_norm, v_ssd_conv_w, v_ssd_conv_b, v_dt_bias, v_a_log, v_ssd_d, v_ssd_norm, v_w_out, v_norm_ffn, v_w_up, v_ffn_conv_w, v_ffn_conv_b, v_w_down, v_norm_final):
    args = dict(locals())
    w = {n: args[n] for n in WEIGHTS}
    m = {n: args['m_' + n] for n in WEIGHTS}
    v = {n: args['v_' + n] for n in WEIGHTS}
    xi, yi, ci = _me()
    chip = 2 * xi + yi
    pos = jnp.stack([ci, chip]).astype(jnp.int32)

    conv_shard = _pack([ssd_conv_w[0], ffn_conv_w[0]])
    conv_gather = _push_start([conv_shard], [(N_CHIPS,) + conv_shard.shape], _route_gather, OTHER_CHIPS,
                              name="gather_start_conv")

    def conv_weights(after):
        send_sems, recv_sems, srcs, lands, _ = conv_gather
        (own,), (got,) = _push_wait(send_sems, recv_sems, srcs, lands, after, _route_gather_wait, OTHER_CHIPS,
                                    name="gather_wait_conv")
        whole = lax.dynamic_update_slice(got, own[None], (chip, 0, 0))
        per_chip = [_unpack(whole[j], [ssd_conv_w.shape[1:], ffn_conv_w.shape[1:]]) for j in range(N_CHIPS)]
        return (jnp.concatenate([pc[0] for pc in per_chip], axis=1),
                jnp.concatenate([pc[1] for pc in per_chip], axis=1))

    w_in_t, m_in_t, v_in_t = (jnp.transpose(a[0]) for a in (w_in, m_w_in, v_w_in))
    in_shard = (w_in_t + conv_gather[4][:1, :1]).astype(BF16)
    in_gather = _push_start([in_shard], [(N_CHIPS,) + in_shard.shape], _route_gather_half, OTHER_CHIPS,
                            name="gather_start_w_in")
    gathers = {}
    order = in_gather[4][:1, :1]
    for n, shard in (('w_out', w_out[0]), ('w_up', w_up[0]), ('w_down', w_down[0])):
        shard = (shard + order).astype(BF16)
        gathers[n] = _push_start([shard], [(N_CHIPS,) + shard.shape], _route_gather, OTHER_CHIPS,
                                 name="gather_start_" + n)
        order = gathers[n][4][:1, :1]

    def weight_in(after):
        send_sems, recv_sems, srcs, lands, _ = in_gather
        (own,), (got,) = _push_wait(send_sems, recv_sems, srcs, lands, after, _route_gather_half_wait, OTHER_CHIPS,
                                    name="gather_wait_w_in")
        got, _ = _forward_halves(got)
        full_in_t = lax.dynamic_update_slice(got, own[None], (chip, 0, 0)).reshape(IN_PROJ_WIDTH, D_MODEL)
        w_in_dt = jnp.pad(full_in_t[MAIN_WIDTH:], ((0, LANES - SSD_HEADS), (0, 0)))
        return full_in_t, w_in_dt, ()

    def weight(name, after):
        send_sems, recv_sems, srcs, lands, _ = gathers[name]
        (own,), (got,) = _push_wait(send_sems, recv_sems, srcs, lands, after, _route_gather_wait, OTHER_CHIPS,
                                    name="gather_wait_" + name)
        whole = lax.dynamic_update_slice(got, own[None], (chip, 0, 0))
        return whole if name == 'w_up' else whole.reshape(-1, D_MODEL)

    reductions, exchanging = {}, {}

    def flush(after):
        tokens = []
        for prev in list(exchanging):
            reductions[prev], token = _grad_scatter_start(exchanging.pop(prev), pos, after)
            tokens.append(token)
        return tokens

    def grad_ready(name, value):
        if name is None:
            return flush(value)
        if name == 'w_in':
            main, dtp = value
            value = lax.dynamic_update_slice(main, dtp[:SSD_HEADS], (MAIN_WIDTH, 0))
        g4 = value if value.ndim == 3 else value.reshape(N_CHIPS, -1, value.shape[1])
        tokens = flush(g4)
        exchanging[name], token = _grad_exchange_start(g4, name, cols=(name == 'w_in'))
        return tokens + [token]

    small = {
        'norm_mix': norm_mix, 'sinks': sinks, 'attn_out_norm': attn_out_norm,
        'ssd_conv_b': ssd_conv_b, 'dt_bias': dt_bias, 'a_log': a_log, 'ssd_d': ssd_d, 'ssd_norm': ssd_norm,
        'norm_ffn': norm_ffn, 'ffn_conv_b': ffn_conv_b, 'norm_final': norm_final,
    }
    loss, dx, g = _local_step(x[0], loss_target[0], small,
                              _StepHooks((gathers['w_down'][4],), weight_in, conv_weights, weight, grad_ready))

    small_names = [n for n in WEIGHTS if n not in BIG]
    small_g = [loss[:, :1]] + [g[n] for n in small_names]
    small_shapes = [(1, 1)] + [tuple(a.shape) for a in small_g[1:]]
    packed = _pack(small_g)
    spread = _push_start([packed], [(8,) + packed.shape], _route_to_all, ALL_OTHERS, name="allreduce_start")
    grads, deltas, new_m, new_v = {}, {}, {}, {}
    after = spread[4]
    shares = {}
    for n in ('w_down', 'w_up', 'w_out'):
        shares[n], after = _grad_sum_and_share(reductions[n], pos, after)
    for n in ('w_down', 'w_up', 'w_out', 'w_in'):
        if n == 'w_out':
            shares['w_in'], after = _grad_sum_and_share(reductions['w_in'], pos, after)
        mine, theirs = _grad_share_wait(shares[n], after)
        if n == 'w_in':
            outs = _adamw_halves(w_in_t, mine, theirs, m_in_t, v_in_t, pos, name="adamw_" + n, cols=True)
            after = outs[1]
            outs = [jnp.transpose(o) for o in outs]
        else:
            outs = _adamw_halves(w[n][0], mine, theirs, m[n][0], v[n][0], pos, name="adamw_" + n)
            after = outs[1]
        grads[n], deltas[n], new_m[n], new_v[n] = [o[None] for o in outs]
    (own,), (landed,) = _push_wait(spread[0], spread[1], spread[2], spread[3], after, _route_to_all_wait, ALL_OTHERS,
                                   name="allreduce_wait")
    landed = lax.dynamic_update_slice(landed, own[None], (4 * xi + 2 * yi + ci, 0, 0))
    red = _unpack(_sum_devices(landed), small_shapes)
    loss_out = red[0].reshape(())
    gsm = dict(zip(small_names, red[1:]))
    gsm['ssd_conv_w'] = lax.dynamic_slice(gsm['ssd_conv_w'], (0, chip * ssd_conv_w.shape[2]),
                                          (SSD_CONV, ssd_conv_w.shape[2]))
    gsm['ffn_conv_w'] = lax.dynamic_slice(gsm['ffn_conv_w'], (0, chip * ffn_conv_w.shape[2]),
                                          (FFN_CONV, ffn_conv_w.shape[2]))

    shapes = [tuple(w[n].shape) for n in small_names]
    gp = _pack([gsm[n] for n in small_names])
    d, m2, v2 = _adamw(_pack([w[n] for n in small_names]), gp, _pack([m[n] for n in small_names]),
                       _pack([v[n] for n in small_names]), name="adamw_small")
    for n, gg, dd, mm, vv in zip(small_names, _unpack(gp, shapes), _unpack(d, shapes), _unpack(m2, shapes),
                                 _unpack(v2, shapes)):
        grads[n], deltas[n], new_m[n], new_v[n] = gg, dd, mm, vv

    return (loss_out, dx[None], *[grads[n] for n in WEIGHTS], *[deltas[n] for n in WEIGHTS],
            *[new_m[n] for n in WEIGHTS], *[new_v[n] for n in WEIGHTS])
```

```python
import functools

import jax
import jax.numpy as jnp
from jax import lax
from jax.experimental import pallas as pl
from jax.experimental.pallas import tpu as pltpu

F32 = jnp.float32
BF16 = jnp.bfloat16

D_MODEL = 2048
N_Q_HEADS = 32
N_KV_HEADS = 8
HEAD_DIM = 64
WINDOW = 128
ATTN_BLOCK = 128
ROT_DIM = 16
ROPE_THETA = 500000.0
SSD_HEADS = 32
SSD_HEAD_DIM = 64
SSD_INNER = 2048
SSD_GROUPS = 8
SSD_STATE = 128
SSD_CONV = 4
SSD_CHUNK = 128
ATTN_WIDTH = 2048
KV_WIDTH = 512
BC_WIDTH = 1024
CONV_CH = 4096
IN_PROJ_WIDTH = 9248
MAIN_WIDTH = 9216
D_FF = 5632
FFN_CONV = 3
EPS = 1e-6
O_Q, O_K, O_V, O_Z, O_XBC, O_DT = 0, 2048, 2560, 3072, 5120, 9216

ADAM_LR = 0.001
ADAM_B1 = 0.9
ADAM_B2 = 0.999
ADAM_EPS = 1e-08
ADAM_WD = 0.01
ADAM_STEP = 10

N_CHIPS = 4
NEG = -1e30
LANES = 128
VMEM_LIMIT = 48 * 1024 * 1024
MESH = pl.DeviceIdType.MESH
HBM_SPEC = pl.BlockSpec(memory_space=pltpu.HBM)
TOKEN = jax.ShapeDtypeStruct((8, LANES), F32)

WEIGHTS = ['norm_mix', 'w_in', 'sinks', 'attn_out_norm', 'ssd_conv_w', 'ssd_conv_b', 'dt_bias', 'a_log', 'ssd_d',
           'ssd_norm', 'w_out', 'norm_ffn', 'w_up', 'ffn_conv_w', 'ffn_conv_b', 'w_down', 'norm_final']
BIG = ['w_in', 'w_out', 'w_up', 'w_down']


def _cp(sem=None, vmem=VMEM_LIMIT):
    kw = {'vmem_limit_bytes': vmem}
    if sem is not None:
        kw['dimension_semantics'] = sem
    return pltpu.CompilerParams(**kw)


def _tile(n, pref):
    if n <= pref:
        return n
    t = (pref // LANES) * LANES
    while t > LANES and n % t:
        t -= LANES
    assert n % t == 0, (n, pref)
    return t


def _rows(n, pref):
    t = min(n, pref)
    while n % t:
        t -= 8
    if 4 * t < pref:
        t = pref
        while n % t:
            t += 8
    return t


def _iota(shape, dim):
    return lax.broadcasted_iota(jnp.int32, shape, dim)


def _dot(a, b, mode='nn'):
    dn = {'nn': (((1,), (0,)), ((), ())), 'nt': (((1,), (1,)), ((), ())), 'tn': (((0,), (0,)), ((), ()))}[mode]
    return lax.dot_general(a.astype(BF16), b.astype(BF16), dn, preferred_element_type=F32)


def _dot_exact(a, b):
    return lax.dot_general(a, b, (((1,), (0,)), ((), ())), precision=lax.Precision.HIGHEST,
                           preferred_element_type=F32)


def _sigmoid(x):
    return 1.0 / (1.0 + jnp.exp(-x))


def _softplus(x):
    return jnp.maximum(x, 0.0) + jnp.log(1.0 + jnp.exp(-jnp.abs(x)))


def _matmul(a, b, *, mode, name, out_dtype=F32, add=None, deps=(), tm=1024, tn=1024, tk=2048,
            a_halves=False, b_halves=False, b_owner=False, owner_major=False, n_limit=None, k_limit=None,
            m_rows=None):
    ash, bsh = (a.shape[1:] if a_halves else a.shape), (b.shape[1:] if (b_halves or b_owner) else b.shape)
    if mode == 'nn':
        (m, k), (k2, n) = ash, bsh
    elif mode == 'nt':
        (m, k), (n, k2) = ash, bsh
    else:
        (k, m), (k2, n) = ash, bsh
    if n_limit is not None:
        assert mode == 'nt' and n_limit <= n
        n = n_limit
    if k_limit is not None:
        assert mode == 'nn' and k_limit <= k2
        k2 = k_limit
    if a_halves:
        assert mode == 'nt'
        k = 2 * k
    if b_halves:
        assert mode == 'tn'
        n = 2 * n
    if b_owner:
        assert mode in ('nn', 'nt')
        if mode == 'nn':
            n = 4 * n
        else:
            k2 = 4 * k2
    assert k == k2, (a.shape, b.shape, mode)
    tm = _tile(m, tm)
    tn = _tile(n // 4 if (owner_major or (b_owner and mode == 'nn')) else (n // 2 if b_halves else n), tn)
    tk = _tile(k // 4 if (b_owner and mode == 'nt') else (k // 2 if a_halves else k), tk)
    nk = k // tk
    has_add = add is not None
    assert not (has_add and owner_major)

    def body(*refs):
        a_ref, b_ref = refs[:2]
        add_ref = refs[2] if has_add else None

        def finish(r, o_ref):
            if has_add:
                r = r + add_ref[...].astype(F32)
            o_ref[...] = r.astype(out_dtype)

        if nk == 1:
            finish(_dot(a_ref[...], b_ref[...], mode), refs[-1])
            return
        o_ref, acc = refs[-2:]
        kk = pl.program_id(2)

        @pl.when(kk == 0)
        def _():
            acc[...] = _dot(a_ref[...], b_ref[...], mode)

        @pl.when((kk > 0) & (kk < nk - 1))
        def _():
            acc[...] += _dot(a_ref[...], b_ref[...], mode)

        @pl.when(kk == nk - 1)
        def _():
            finish(acc[...] + _dot(a_ref[...], b_ref[...], mode), o_ref)

    if mode == 'tn':
        a_spec = pl.BlockSpec((tk, tm), lambda i, j, kk: (kk, i))
    elif a_halves:
        nkh = nk // 2
        a_spec = pl.BlockSpec((None, tm, tk), lambda i, j, kk: (kk // nkh, i, kk % nkh))
    else:
        a_spec = pl.BlockSpec((tm, tk), lambda i, j, kk: (i, kk))
    if mode == 'nt' and b_owner:
        nkq = nk // 4
        b_spec = pl.BlockSpec((None, tn, tk), lambda i, j, kk: (kk // nkq, j, kk % nkq))
    elif mode == 'nt':
        b_spec = pl.BlockSpec((tn, tk), lambda i, j, kk: (j, kk))
    elif b_owner:
        njq = (n // 4) // tn
        b_spec = pl.BlockSpec((None, tk, tn), lambda i, j, kk: (j // njq, kk, j % njq))
    elif b_halves:
        njh = (n // 2) // tn
        b_spec = pl.BlockSpec((None, tk, tn), lambda i, j, kk: (j // njh, kk, j % njh))
    else:
        b_spec = pl.BlockSpec((tk, tn), lambda i, j, kk: (kk, j))
    if owner_major:
        njo = (n // 4) // tn
        o_spec = pl.BlockSpec((None, tm, tn), lambda i, j, kk: (j // njo, i, j % njo))
        out_shape = jax.ShapeDtypeStruct((N_CHIPS, m, n // 4), out_dtype)
    else:
        o_spec = pl.BlockSpec((tm, tn), lambda i, j, kk: (i, j))
        out_shape = jax.ShapeDtypeStruct((m if m_rows is None else m_rows, n), out_dtype)
    dep_spec = pl.BlockSpec((8, LANES), lambda i, j, kk: (0, 0))
    in_specs = [a_spec, b_spec] + ([pl.BlockSpec((tm, tn), lambda i, j, kk: (i, j))] if has_add else [])
    in_specs += [dep_spec] * len(deps)
    args = (a, b) + ((add,) if has_add else ()) + tuple(deps)
    return pl.pallas_call(
        body, name=name, grid=(m // tm, n // tn, nk), in_specs=in_specs, out_specs=o_spec, out_shape=out_shape,
        scratch_shapes=[pltpu.VMEM((tm, tn), F32)] if nk > 1 else [],
        compiler_params=_cp(("parallel", "parallel", "arbitrary")))(*args)


def _rmsnorm_fwd(x, g, name, deps=()):
    t, d = x.shape
    tb = _rows(t, 256)

    def body(x_ref, g_ref, *rest):
        o_ref = rest[-1]
        xv = x_ref[...]
        r = lax.rsqrt(jnp.mean(xv * xv, axis=-1, keepdims=True) + EPS)
        o_ref[...] = (xv * r * g_ref[...]).astype(BF16)

    dep_spec = pl.BlockSpec((8, LANES), lambda i: (0, 0))
    return pl.pallas_call(
        body, name=name, grid=(t // tb,),
        in_specs=[pl.BlockSpec((tb, d), lambda i: (i, 0)), pl.BlockSpec((1, d), lambda i: (0, 0))]
        + [dep_spec] * len(deps),
        out_specs=pl.BlockSpec((tb, d), lambda i: (i, 0)), out_shape=jax.ShapeDtypeStruct((t, d), BF16),
        compiler_params=_cp(("parallel",)))(x, g, *deps)


def _rmsnorm_bwd(x, g, dy, res, name, deps=()):
    t, d = x.shape
    tb = _rows(t, 256)

    def body(x_ref, g_ref, dy_ref, res_ref, *rest):
        dx_ref, dx16_ref, dg_ref = rest[-3:]
        i = pl.program_id(0)
        xv = x_ref[...]
        dyv = dy_ref[...].astype(F32)
        r = lax.rsqrt(jnp.mean(xv * xv, axis=-1, keepdims=True) + EPS)
        u = dyv * g_ref[...]
        dx = r * u - xv * (r * r * r * jnp.mean(u * xv, axis=-1, keepdims=True)) + res_ref[...]
        dx_ref[...] = dx
        dx16_ref[...] = dx.astype(BF16)
        part = jnp.sum(dyv * xv * r, axis=0, keepdims=True)

        @pl.when(i == 0)
        def _():
            dg_ref[...] = part

        @pl.when(i > 0)
        def _():
            dg_ref[...] += part

    row = pl.BlockSpec((tb, d), lambda i: (i, 0))
    vec = pl.BlockSpec((1, d), lambda i: (0, 0))
    return pl.pallas_call(
        body, name=name, grid=(t // tb,),
        in_specs=[row, vec, row, row] + [pl.BlockSpec((8, LANES), lambda i: (0, 0))] * len(deps),
        out_specs=[row, row, vec],
        out_shape=[jax.ShapeDtypeStruct((t, d), F32), jax.ShapeDtypeStruct((t, d), BF16),
                   jax.ShapeDtypeStruct((1, d), F32)],
        compiler_params=_cp(("arbitrary",)))(x, g, dy, res, *deps)


def _final_loss(h, g, tgt):
    t, d = h.shape
    tb = _rows(t, 256)

    def body(h_ref, g_ref, t_ref, loss_ref, dh_ref, dh16_ref, dg_ref):
        i = pl.program_id(0)
        hv = h_ref[...]
        gv = g_ref[...]
        r = lax.rsqrt(jnp.mean(hv * hv, axis=-1, keepdims=True) + EPS)
        y = hv * r * gv
        diff = y - t_ref[...]
        lpart = jnp.sum(jnp.sum(diff * diff, axis=1, keepdims=True), axis=0, keepdims=True) * (0.5 / d)
        dy = diff * (1.0 / d)
        u = dy * gv
        dh = r * u - hv * (r * r * r * jnp.mean(u * hv, axis=-1, keepdims=True))
        dh_ref[...] = dh
        dh16_ref[...] = dh.astype(BF16)
        gpart = jnp.sum(dy * hv * r, axis=0, keepdims=True)
        lrow = jnp.broadcast_to(lpart, (1, LANES))

        @pl.when(i == 0)
        def _():
            loss_ref[...] = lrow
            dg_ref[...] = gpart

        @pl.when(i > 0)
        def _():
            loss_ref[...] += lrow
            dg_ref[...] += gpart

    row = pl.BlockSpec((tb, d), lambda i: (i, 0))
    vec = pl.BlockSpec((1, d), lambda i: (0, 0))
    return pl.pallas_call(
        body, name="final_loss", grid=(t // tb,), in_specs=[row, vec, row],
        out_specs=[pl.BlockSpec((1, LANES), lambda i: (0, 0)), row, row, vec],
        out_shape=[jax.ShapeDtypeStruct((1, LANES), F32), jax.ShapeDtypeStruct((t, d), F32),
                   jax.ShapeDtypeStruct((t, d), BF16), jax.ShapeDtypeStruct((1, d), F32)],
        compiler_params=_cp(("arbitrary",)))(h, g, tgt)


def _rope_tables(t):
    pos = jnp.arange(t, dtype=F32)
    inv = 1.0 / (ROPE_THETA ** (jnp.arange(0, ROT_DIM, 2, dtype=F32) / ROT_DIM))
    ang = pos[:, None] * inv[None, :]
    cos, sin = jnp.cos(ang), jnp.sin(ang)
    half = ROT_DIM // 2
    rest = HEAD_DIM - ROT_DIM
    c = jnp.concatenate([cos, cos, jnp.ones((t, rest), F32)], axis=1)
    s1 = jnp.concatenate([-sin, jnp.zeros((t, half + rest), F32)], axis=1)
    s2 = jnp.concatenate([jnp.zeros((t, half), F32), sin, jnp.zeros((t, rest), F32)], axis=1)
    return jnp.concatenate([jnp.tile(v, (1, LANES // HEAD_DIM)) for v in (c, s1, s2)], axis=1)


def _split_tables(tab):
    return tab[:, :LANES], tab[:, LANES:2 * LANES], tab[:, 2 * LANES:]


def _rope(x, c, s1, s2):
    half = ROT_DIM // 2
    return x * c + pltpu.roll(x, LANES - half, 1) * s1 + pltpu.roll(x, half, 1) * s2


def _rope_t(g, c, s1, s2):
    half = ROT_DIM // 2
    return g * c + pltpu.roll(g * s1, half, 1) + pltpu.roll(g * s2, LANES - half, 1)


def _band_masks(i, heads):
    n = heads * ATTN_BLOCK
    q = jnp.bitwise_and(_iota((n, ATTN_BLOCK), 0), ATTN_BLOCK - 1)
    j = _iota((n, ATTN_BLOCK), 1)
    upper = j > q
    return upper, upper & (j < jnp.where(i > 0, 0, ATTN_BLOCK))


def _fold_band(full, upper):
    return jnp.where(upper, full[:, :ATTN_BLOCK], full[:, ATTN_BLOCK:])


def _unfold_band(band, upper):
    return jnp.concatenate([jnp.where(upper, band, 0.0), jnp.where(upper, 0.0, band)], axis=1)


def _half_masks():
    lane = _iota((1, LANES), 1)
    return [(lane < HEAD_DIM).astype(F32), (lane >= HEAD_DIM).astype(F32)]


def _stack_heads(blocks, hm, j):
    pieces = []
    for r in range(4):
        qb, half = (4 * j + r) // 2, (4 * j + r) % 2
        piece = blocks[qb] * hm[half]
        if half != j:
            piece = pltpu.roll(piece, HEAD_DIM, 1)
        pieces.append(piece)
    return jnp.concatenate(pieces, axis=0)


def _unstack_heads(stacked, j):
    out = []
    for qb in (2 * j, 2 * j + 1):
        acc = None
        for half in range(2):
            r = 2 * qb + half - 4 * j
            piece = stacked[r * ATTN_BLOCK:(r + 1) * ATTN_BLOCK]
            if half != j:
                piece = pltpu.roll(piece, HEAD_DIM, 1)
            acc = piece if acc is None else acc + piece
        out.append((qb, acc))
    return out


def _sink_column(sink_ref, base):
    return jnp.concatenate([jnp.full((ATTN_BLOCK, 1), sink_ref[base + r], F32) for r in range(4)], axis=0)


def _attn_specs(nb_clamp):
    blk = ATTN_BLOCK
    kb, vb = O_K // LANES, O_V // LANES

    def cur(i):
        return jnp.minimum(i, nb_clamp)

    def prev(i):
        return jnp.maximum(jnp.minimum(i, nb_clamp + 1) - 1, 0)

    q = pl.BlockSpec((blk, 512), lambda p, i: (cur(i), p))
    kc = pl.BlockSpec((blk, LANES), lambda p, i: (cur(i), kb + p))
    kp = pl.BlockSpec((blk, LANES), lambda p, i: (prev(i), kb + p))
    vc = pl.BlockSpec((blk, LANES), lambda p, i: (cur(i), vb + p))
    vp = pl.BlockSpec((blk, LANES), lambda p, i: (prev(i), vb + p))
    tc = pl.BlockSpec((blk, 3 * LANES), lambda p, i: (cur(i), 0))
    tp = pl.BlockSpec((blk, 3 * LANES), lambda p, i: (prev(i), 0))
    return q, kc, kp, vc, vp, tc, tp


def _attn_fwd(proj, sinks, tables):
    t = proj.shape[0]
    nb = t // ATTN_BLOCK
    scale = HEAD_DIM ** -0.5

    def body(sink_ref, q_ref, kc_ref, kp_ref, vc_ref, vp_ref, tc_ref, tp_ref, o_ref):
        p = pl.program_id(0)
        i = pl.program_id(1)
        cc, s1c, s2c = _split_tables(tc_ref[...])
        kband = jnp.concatenate([_rope(kp_ref[...], *_split_tables(tp_ref[...])),
                                 _rope(kc_ref[...], cc, s1c, s2c)], axis=0).astype(BF16)
        vband = jnp.concatenate([vp_ref[...], vc_ref[...]], axis=0)
        hm = _half_masks()
        vsel = [(vband * hm[j]).astype(BF16) for j in range(2)]
        upper, dropped = _band_masks(i, 1)
        qr = [_rope(q_ref[:, qb * LANES:(qb + 1) * LANES], cc, s1c, s2c) for qb in range(4)]

        def scores(hh):
            qb, half, j = hh // 2, hh % 2, hh // 4
            qs = qr[qb] * hm[half]
            if half != j:
                qs = pltpu.roll(qs, HEAD_DIM, 1)
            return _dot(qs, kband, 'nt')

        ahead = scores(0)
        acc = None
        for hh in range(8):
            qb, half, j = hh // 2, hh % 2, hh // 4
            raw = ahead
            if hh + 1 < 8:
                ahead = scores(hh + 1)
            s = jnp.where(dropped, NEG, _fold_band(raw, upper) * scale)
            sink = sink_ref[p * 8 + hh]
            m = jnp.maximum(jnp.max(s, axis=1, keepdims=True), sink)
            pe = jnp.exp(s - m)
            den = jnp.sum(pe, axis=1, keepdims=True) + jnp.exp(sink - m)
            o = _dot(_unfold_band(pe / den, upper), vsel[j])
            if half != j:
                o = pltpu.roll(o, HEAD_DIM, 1)
            acc = o if half == 0 else acc + o
            if half == 1:
                o_ref[:, qb * LANES:(qb + 1) * LANES] = acc

    q, kc, kp, vc, vp, tc, tp = _attn_specs(nb - 1)
    smem = pl.BlockSpec(memory_space=pltpu.SMEM)
    return pl.pallas_call(
        body, name="attn_fwd", grid=(4, nb),
        in_specs=[smem, q, kc, kp, vc, vp, tc, tp],
        out_specs=pl.BlockSpec((ATTN_BLOCK, 512), lambda p, i: (i, p)),
        out_shape=jax.ShapeDtypeStruct((t, ATTN_WIDTH), F32),
        compiler_params=_cp(("parallel", "arbitrary")))(sinks, proj, proj, proj, proj, proj, tables, tables)


def _attn_bwd(proj, sinks, tables, dout):
    t = proj.shape[0]
    nb = t // ATTN_BLOCK
    scale = HEAD_DIM ** -0.5

    def body(sink_ref, q_ref, kc_ref, kp_ref, vc_ref, vp_ref, tc_ref, tp_ref,
             do_ref, dq_ref, dk_ref, dv_ref, ds_ref, carry_k, carry_v):
        p = pl.program_id(0)
        i = pl.program_id(1)
        ptab = _split_tables(tp_ref[...])

        @pl.when(i == 0)
        def _():
            carry_k[...] = jnp.zeros_like(carry_k)
            carry_v[...] = jnp.zeros_like(carry_v)
            ds_ref[...] = jnp.zeros_like(ds_ref)

        @pl.when(i < nb)
        def _():
            cc, s1c, s2c = _split_tables(tc_ref[...])
            kband = jnp.concatenate([_rope(kp_ref[...], *ptab), _rope(kc_ref[...], cc, s1c, s2c)], axis=0)
            vband = jnp.concatenate([vp_ref[...], vc_ref[...]], axis=0)
            hm = _half_masks()
            kband16 = kband.astype(BF16)
            vband16 = vband.astype(BF16)
            upper, dropped = _band_masks(i, 4)
            dkb = jnp.zeros((2 * ATTN_BLOCK, LANES), F32)
            dvb = jnp.zeros((2 * ATTN_BLOCK, LANES), F32)
            row8 = _iota((8, LANES), 0)
            dsink = jnp.zeros((8, LANES), F32)
            qr = [_rope(q_ref[:, qb * LANES:(qb + 1) * LANES], cc, s1c, s2c) for qb in range(4)]
            dob = [do_ref[:, qb * LANES:(qb + 1) * LANES] for qb in range(4)]
            for j in range(2):
                qst = _stack_heads(qr, hm, j).astype(BF16)
                dost = _stack_heads(dob, hm, j).astype(BF16)
                s = jnp.where(dropped, NEG, _fold_band(_dot(qst, kband16, 'nt'), upper) * scale)
                sink = _sink_column(sink_ref, p * 8 + 4 * j)
                m = jnp.maximum(jnp.max(s, axis=1, keepdims=True), sink)
                pe = jnp.exp(s - m)
                psink = jnp.exp(sink - m)
                den = jnp.sum(pe, axis=1, keepdims=True) + psink
                pr = pe / den
                dvb = dvb + _dot(_unfold_band(pr, upper).T, dost)
                dp = _fold_band(_dot(dost, vband16, 'nt'), upper)
                delta = jnp.sum(pr * dp, axis=1, keepdims=True)
                dsc = _unfold_band(pr * (dp - delta) * scale, upper)
                dsk = psink / den * delta
                for r in range(4):
                    part = jnp.sum(dsk[r * ATTN_BLOCK:(r + 1) * ATTN_BLOCK])
                    dsink = dsink + jnp.where(row8 == 4 * j + r, -part, 0.0)
                for qb, dqb in _unstack_heads(_dot(dsc, kband * hm[j]), j):
                    dq_ref[:, qb * LANES:(qb + 1) * LANES] = _rope_t(dqb, cc, s1c, s2c).astype(BF16)
                dkb = dkb + _dot(dsc.T, qst)
            ds_ref[0] += dsink
            dk_ref[...] = _rope_t(carry_k[...] + dkb[:ATTN_BLOCK], *ptab).astype(BF16)
            dv_ref[...] = (carry_v[...] + dvb[:ATTN_BLOCK]).astype(BF16)
            carry_k[...] = dkb[ATTN_BLOCK:]
            carry_v[...] = dvb[ATTN_BLOCK:]

        @pl.when(i == nb)
        def _():
            dk_ref[...] = _rope_t(carry_k[...], *ptab).astype(BF16)
            dv_ref[...] = carry_v[...].astype(BF16)

    q, kc, kp, vc, vp, tc, tp = _attn_specs(nb - 1)
    smem = pl.BlockSpec(memory_space=pltpu.SMEM)
    qblk = pl.BlockSpec((ATTN_BLOCK, 512), lambda p, i: (jnp.minimum(i, nb - 1), p))
    kvout = pl.BlockSpec((ATTN_BLOCK, LANES), lambda p, i: (jnp.maximum(i - 1, 0), p))
    return pl.pallas_call(
        body, name="attn_bwd", grid=(4, nb + 1),
        in_specs=[smem, q, kc, kp, vc, vp, tc, tp, qblk],
        out_specs=[qblk, kvout, kvout, pl.BlockSpec((1, 8, LANES), lambda p, i: (p, 0, 0))],
        out_shape=[jax.ShapeDtypeStruct((t, MAIN_WIDTH), BF16), jax.ShapeDtypeStruct((t, KV_WIDTH), BF16),
                   jax.ShapeDtypeStruct((t, KV_WIDTH), BF16), jax.ShapeDtypeStruct((4, 8, LANES), F32)],
        scratch_shapes=[pltpu.VMEM((ATTN_BLOCK, LANES), F32), pltpu.VMEM((ATTN_BLOCK, LANES), F32)],
        compiler_params=_cp(("parallel", "arbitrary")))(sinks, proj, proj, proj, proj, proj, tables, tables, dout)


def _shift_rows(x, prev8, j):
    n, c = x.shape
    r = pltpu.roll(x.reshape(n // 8, 8, c), j, 1)
    before = pltpu.roll(prev8, j, 0)[None]
    if n > 8:
        before = jnp.concatenate([before, r[:-1]], axis=0)
    return jnp.where(_iota((1, 8, 1), 1) < j, before, r).reshape(n, c)


def _shift_rows_up(x, next8, j):
    n, c = x.shape
    r = pltpu.roll(x.reshape(n // 8, 8, c), 8 - j, 1)
    after = pltpu.roll(next8, 8 - j, 0)[None]
    if n > 8:
        after = jnp.concatenate([r[1:], after], axis=0)
    return jnp.where(_iota((1, 8, 1), 1) >= 8 - j, after, r).reshape(n, c)


def _conv_apply(x, prev8, w, b, taps):
    u = b + x * w[taps - 1:taps]
    for j in range(1, taps):
        u = u + _shift_rows(x, prev8, j) * w[taps - 1 - j:taps - j]
    return u


def _conv_grads(du, du_next8, x, w, taps):
    dx = du * w[taps - 1:taps]
    rowk = _iota((taps, 1), 0)
    dw = jnp.where(rowk == taps - 1, jnp.sum(du * x, axis=0, keepdims=True), 0.0)
    for j in range(1, taps):
        ahead = _shift_rows_up(du, du_next8, j)
        dx = dx + ahead * w[taps - 1 - j:taps - j]
        dw = dw + jnp.where(rowk == taps - 1 - j, jnp.sum(ahead * x, axis=0, keepdims=True), 0.0)
    return dx, dw, jnp.sum(du, axis=0, keepdims=True)


def _conv_specs(tb, tc, col0, t):
    c0 = col0 // tc
    cur = pl.BlockSpec((tb, tc), lambda j, i: (i, c0 + j))
    prev = pl.BlockSpec((8, tc), lambda j, i: (jnp.maximum(i * (tb // 8) - 1, 0), c0 + j))
    nxt = pl.BlockSpec((8, tc), lambda j, i: (jnp.minimum((i + 1) * (tb // 8), t // 8 - 1), c0 + j))
    return cur, prev, nxt


def _conv_silu_fwd(x, w, b, *, col0, width, name):
    t = x.shape[0]
    taps = w.shape[0]
    tb, tc = _rows(t, 512), _tile(width, 1024)
    assert col0 % tc == 0

    def body(x_ref, xp_ref, w_ref, b_ref, o_ref, u_ref):
        i = pl.program_id(1)
        prev8 = jnp.where(i > 0, xp_ref[...], 0.0)
        u = _conv_apply(x_ref[...], prev8, w_ref[...], b_ref[...], taps)
        u_ref[...] = u
        o_ref[...] = u * _sigmoid(u)

    cur, prev, _ = _conv_specs(tb, tc, col0, t)
    par = pl.BlockSpec((taps, tc), lambda j, i: (0, j))
    bias = pl.BlockSpec((1, tc), lambda j, i: (0, j))
    out = pl.BlockSpec((tb, tc), lambda j, i: (i, j))
    shp = jax.ShapeDtypeStruct((t, width), F32)
    return pl.pallas_call(
        body, name=name, grid=(width // tc, t // tb), in_specs=[cur, prev, par, bias], out_specs=[out, out],
        out_shape=[shp, shp], compiler_params=_cp(("parallel", "parallel")))(x, x, w, b)


def _dsilu(u):
    sg = _sigmoid(u)
    return sg * (1.0 + u * (1.0 - sg))


def _ssd_conv_bwd(x, w, dxs, dbm, dcm, base, *, col0, name):
    t = x.shape[0]
    taps = w.shape[0]
    tb, tc = _rows(t, 512), BC_WIDTH
    nrow, ncol = t // tb, CONV_CH // tc
    c0 = col0 // tc

    def body(x_ref, w_ref, xs_ref, xsn_ref, bm_ref, bmn_ref, cm_ref, cmn_ref, base_ref, dx_ref, dw_ref, db_ref):
        i = pl.program_id(0)
        j = pl.program_id(1)

        def run(du_ref, dun_ref):
            next8 = jnp.where(i < nrow - 1, dun_ref[...], 0.0)
            dx, dwv, dbv = _conv_grads(du_ref[...], next8, x_ref[...], w_ref[...], taps)
            dx_ref[...] = dx.astype(BF16)

            @pl.when(i == 0)
            def _():
                dw_ref[j] = dwv
                db_ref[j] = dbv

            @pl.when(i > 0)
            def _():
                dw_ref[j] += dwv
                db_ref[j] += dbv

        pl.when(j < 2)(lambda: run(xs_ref, xsn_ref))
        pl.when(j == 2)(lambda: run(bm_ref, bmn_ref))
        pl.when(j == 3)(lambda: run(cm_ref, cmn_ref))

    def nxt_row(i):
        return jnp.minimum((i + 1) * (tb // 8), t // 8 - 1)

    xs_col = lambda j: jnp.minimum(j, SSD_INNER // tc - 1)
    in_specs = [pl.BlockSpec((tb, tc), lambda i, j: (i, c0 + j)), pl.BlockSpec((taps, tc), lambda i, j: (0, j)),
                pl.BlockSpec((tb, tc), lambda i, j: (i, xs_col(j))),
                pl.BlockSpec((8, tc), lambda i, j: (nxt_row(i), xs_col(j))),
                pl.BlockSpec((tb, tc), lambda i, j: (i, 0)), pl.BlockSpec((8, tc), lambda i, j: (nxt_row(i), 0)),
                pl.BlockSpec((tb, tc), lambda i, j: (i, 0)), pl.BlockSpec((8, tc), lambda i, j: (nxt_row(i), 0)),
                pl.BlockSpec(memory_space=pl.ANY)]
    dx, dw, db = pl.pallas_call(
        body, name=name, grid=(nrow, ncol), in_specs=in_specs, input_output_aliases={8: 0},
        out_specs=[pl.BlockSpec((tb, tc), lambda i, j: (i, c0 + j)),
                   pl.BlockSpec((ncol, taps, tc), lambda i, j: (0, 0, 0)),
                   pl.BlockSpec((ncol, 1, tc), lambda i, j: (0, 0, 0))],
        out_shape=[jax.ShapeDtypeStruct((t, MAIN_WIDTH), BF16), jax.ShapeDtypeStruct((ncol, taps, tc), F32),
                   jax.ShapeDtypeStruct((ncol, 1, tc), F32)],
        compiler_params=_cp(("arbitrary", "arbitrary")))(x, w, dxs, dxs, dbm, dbm, dcm, dcm, base)
    return dx, dw.transpose(1, 0, 2).reshape(taps, CONV_CH), db.transpose(1, 0, 2).reshape(1, CONV_CH)


def _ffn_specs(tb, tc, t):
    nc = D_FF // tc

    def cur(half):
        return pl.BlockSpec((tb, tc), lambda j, i: (i, half * nc + j))

    def prev(half):
        return pl.BlockSpec((8, tc), lambda j, i: (jnp.maximum(i * (tb // 8) - 1, 0), half * nc + j))

    def nxt(half):
        return pl.BlockSpec((8, tc), lambda j, i: (jnp.minimum((i + 1) * (tb // 8), t // 8 - 1), half * nc + j))

    def par(rows, half):
        return pl.BlockSpec((rows, tc), lambda j, i: (0, half * nc + j))

    return cur, prev, nxt, par


def _ffn_act_fwd(u0, w, b):
    t = u0.shape[0]
    tb, tc = _rows(t, 512), _tile(D_FF, 1408)
    cur, prev, _, par = _ffn_specs(tb, tc, t)

    def body(g_ref, gp_ref, v_ref, vp_ref, wg_ref, wv_ref, bg_ref, bv_ref, o_ref, u_ref):
        i = pl.program_id(1)
        ug = _conv_apply(g_ref[...], jnp.where(i > 0, gp_ref[...], 0.0), wg_ref[...], bg_ref[...], FFN_CONV)
        uv = _conv_apply(v_ref[...], jnp.where(i > 0, vp_ref[...], 0.0), wv_ref[...], bv_ref[...], FFN_CONV)
        o_ref[...] = (ug * _sigmoid(ug) * uv).astype(BF16)
        u_ref[0] = ug
        u_ref[1] = uv

    return pl.pallas_call(
        body, name="ffn_act_fwd", grid=(D_FF // tc, t // tb),
        in_specs=[cur(0), prev(0), cur(1), prev(1), par(FFN_CONV, 0), par(FFN_CONV, 1), par(1, 0), par(1, 1)],
        out_specs=[pl.BlockSpec((tb, tc), lambda j, i: (i, j)), pl.BlockSpec((2, tb, tc), lambda j, i: (0, i, j))],
        out_shape=[jax.ShapeDtypeStruct((t, D_FF), BF16), jax.ShapeDtypeStruct((2, t, D_FF), F32)],
        compiler_params=_cp(("parallel", "parallel")))(u0, u0, u0, u0, w, w, b, b)


def _ffn_act_bwd(u0, u, w, da):
    t = u0.shape[0]
    tb, tc = _rows(t, 256), _tile(D_FF, 1408)
    nrow = t // tb
    taps = FFN_CONV
    cur, _, _, par = _ffn_specs(tb, tc, t)

    def dact(ug, uv, dav):
        sg = _sigmoid(ug)
        return dav * uv * (sg * (1.0 + ug * (1.0 - sg))), dav * ug * sg

    def body(g_ref, v_ref, u_ref, un_ref, wg_ref, wv_ref, da_ref, dan_ref, dx_ref, dw_ref, db_ref):
        i = pl.program_id(1)
        dug, duv = dact(u_ref[0], u_ref[1], da_ref[...].astype(F32))
        dan = jnp.where(i < nrow - 1, dan_ref[...].astype(F32)[:8], 0.0)
        dugn, duvn = dact(un_ref[0], un_ref[1], dan)
        dxg, dwg, dbg = _conv_grads(dug, dugn, g_ref[...], wg_ref[...], taps)
        dxv, dwv, dbv = _conv_grads(duv, duvn, v_ref[...], wv_ref[...], taps)
        dx_ref[0] = dxg.astype(BF16)
        dx_ref[1] = dxv.astype(BF16)

        @pl.when(i == 0)
        def _():
            dw_ref[0] = dwg
            dw_ref[1] = dwv
            db_ref[0] = dbg
            db_ref[1] = dbv

        @pl.when(i > 0)
        def _():
            dw_ref[0] += dwg
            dw_ref[1] += dwv
            db_ref[0] += dbg
            db_ref[1] += dbv

    both = pl.BlockSpec((2, tb, tc), lambda j, i: (0, i, j))
    both_nxt = pl.BlockSpec((2, 8, tc), lambda j, i: (0, jnp.minimum((i + 1) * (tb // 8), t // 8 - 1), j))
    da_cur = pl.BlockSpec((tb, tc), lambda j, i: (i, j))
    da_nxt = pl.BlockSpec((16, tc), lambda j, i: (jnp.minimum((i + 1) * (tb // 16), t // 16 - 1), j))
    return pl.pallas_call(
        body, name="ffn_act_bwd", grid=(D_FF // tc, nrow),
        in_specs=[cur(0), cur(1), both, both_nxt, par(taps, 0), par(taps, 1), da_cur, da_nxt],
        out_specs=[both, pl.BlockSpec((2, taps, tc), lambda j, i: (0, 0, j)),
                   pl.BlockSpec((2, 1, tc), lambda j, i: (0, 0, j))],
        out_shape=[jax.ShapeDtypeStruct((2, t, D_FF), BF16), jax.ShapeDtypeStruct((2, taps, D_FF), F32),
                   jax.ShapeDtypeStruct((2, 1, D_FF), F32)],
        compiler_params=_cp(("parallel", "arbitrary")))(u0, u0, u, u, w, w, da, da)


def _head_masks():
    lane = _iota((1, 4 * SSD_HEAD_DIM), 1)
    return [((lane >= r * SSD_HEAD_DIM) & (lane < (r + 1) * SSD_HEAD_DIM)).astype(F32) for r in range(4)]


def _segsum(v):
    first = _iota((1, LANES), 1) < SSD_HEAD_DIM
    halves = []
    for k in range(2):
        vh = v[:, k * LANES:(k + 1) * LANES]
        both = jnp.sum(vh, axis=1, keepdims=True)
        one = jnp.sum(jnp.where(first, vh, 0.0), axis=1, keepdims=True)
        halves.append(jnp.where(first, one, both - one))
    return jnp.concatenate(halves, axis=1)


def _ssd_common(raw_e, prow, rawr4, bcol, acol):
    n = SSD_CHUNK
    dt_e = _softplus(raw_e + prow[0:1, :])
    a_e = -jnp.exp(prow[1:2, :])
    d_e = prow[2:3, :]
    tril = (_iota((n, n), 0) >= _iota((n, n), 1)).astype(F32)
    acs_e = _dot_exact(tril, dt_e * a_e)
    last_e = acs_e[n - 1:n, :]
    dtr4 = _softplus(rawr4 + bcol)
    triu = (_iota((n, n), 0) <= _iota((n, n), 1)).astype(F32)
    acs_r4 = _dot_exact(dtr4 * (-jnp.exp(acol)), triu)
    return dt_e, a_e, d_e, acs_e, last_e, acs_r4


def _decay_matrix(acs_e, acs_r4, r):
    n = SSD_CHUNK
    col = acs_e[:, r * SSD_HEAD_DIM:r * SSD_HEAD_DIM + 1]
    seg = col - acs_r4[r:r + 1, :]
    causal = _iota((n, n), 0) >= _iota((n, n), 1)
    return jnp.exp(jnp.where(causal, seg, NEG))


SSD_STEP_CHUNKS = 4
SSD_ROWS = SSD_STEP_CHUNKS * SSD_CHUNK


def _ssd_specs(t, rev):
    nb = t // SSD_ROWS
    xb, bb, cb = 0, SSD_INNER // SSD_STATE, (SSD_INNER + BC_WIDTH) // SSD_STATE

    def ch(c):
        return (nb - 1 - c) if rev else c

    x = pl.BlockSpec((SSD_ROWS, 256), lambda g, c: (ch(c), xb + g))
    bm = pl.BlockSpec((SSD_ROWS, SSD_STATE), lambda g, c: (ch(c), bb + g))
    cm = pl.BlockSpec((SSD_ROWS, SSD_STATE), lambda g, c: (ch(c), cb + g))
    dtc = pl.BlockSpec((1, SSD_ROWS, 256), lambda g, c: (g, ch(c), 0))
    dtr = pl.BlockSpec((1, 4, SSD_ROWS), lambda g, c: (g, 0, ch(c)))
    prow = pl.BlockSpec((1, 3, 256), lambda g, c: (g, 0, 0))
    pcol = pl.BlockSpec((1, 4, 1), lambda g, c: (g, 0, 0))
    st = pl.BlockSpec((1, SSD_STEP_CHUNKS, SSD_STATE, 256), lambda g, c: (g, ch(c), 0, 0))
    return x, bm, cm, dtc, dtr, prow, pcol, st, ch


def _ssd_params(dt_raw, dt_bias, a_log, ssd_d):
    t = dt_raw.shape[0]
    by_group = dt_raw.reshape(t, SSD_GROUPS, 4)
    dtc = jnp.repeat(by_group, SSD_HEAD_DIM, axis=2).transpose(1, 0, 2)
    dtr = by_group.transpose(1, 2, 0)
    prow = jnp.repeat(jnp.stack([dt_bias.reshape(SSD_GROUPS, 4), a_log.reshape(SSD_GROUPS, 4),
                                 ssd_d.reshape(SSD_GROUPS, 4)], axis=1), SSD_HEAD_DIM, axis=2)
    bcol = dt_bias.reshape(SSD_GROUPS, 4, 1)
    acol = a_log.reshape(SSD_GROUPS, 4, 1)
    return dtc, dtr, prow, bcol, acol


def _ssd_fwd(xbc, params):
    t = xbc.shape[0]
    nc = t // SSD_CHUNK
    dtc, dtr, prow, bcol, acol = params

    def body(x_ref, b_ref, c_ref, dtc_ref, dtr_ref, prow_ref, bcol_ref, acol_ref, y_ref, st_ref, s_scr):
        c = pl.program_id(1)

        @pl.when(c == 0)
        def _():
            s_scr[...] = jnp.zeros_like(s_scr)

        masks = _head_masks()
        s = s_scr[...]
        for k in range(SSD_STEP_CHUNKS):
            rows = slice(k * SSD_CHUNK, (k + 1) * SSD_CHUNK)
            dt_e, a_e, d_e, acs_e, last_e, acs_r4 = _ssd_common(
                dtc_ref[0, rows], prow_ref[0], dtr_ref[0][:, rows], bcol_ref[0], acol_ref[0])
            xv = x_ref[rows]
            bm, cm = b_ref[rows], c_ref[rows]
            st_ref[0, k] = s
            xdt = xv * dt_e
            cb = _dot(cm, bm, 'nt')
            y = _dot(cm, s) * jnp.exp(acs_e) + xv * d_e
            for r in range(4):
                mr = cb * _decay_matrix(acs_e, acs_r4, r)
                y = y + _dot(mr, xdt * masks[r])
            y_ref[rows] = y
            w = xdt * jnp.exp(last_e - acs_e)
            s = s * jnp.exp(last_e) + _dot(bm.T, w)
        s_scr[...] = s

    x, bm, cm, dtcs, dtrs, prs, pcs, st, _ = _ssd_specs(t, False)
    return pl.pallas_call(
        body, name="ssd_fwd", grid=(SSD_GROUPS, t // SSD_ROWS), in_specs=[x, bm, cm, dtcs, dtrs, prs, pcs, pcs],
        out_specs=[pl.BlockSpec((SSD_ROWS, 256), lambda g, c: (c, g)), st],
        out_shape=[jax.ShapeDtypeStruct((t, SSD_INNER), F32),
                   jax.ShapeDtypeStruct((SSD_GROUPS, nc, SSD_STATE, 256), F32)],
        scratch_shapes=[pltpu.VMEM((SSD_STATE, 256), F32)],
        compiler_params=_cp(("parallel", "arbitrary")))(xbc, xbc, xbc, dtc, dtr, prow, bcol, acol)


def _ssd_bwd(xbc, pre, params, states, dy):
    t = xbc.shape[0]
    nc = t // SSD_CHUNK
    n = SSD_CHUNK
    dtc, dtr, prow, bcol, acol = params

    def body(x_ref, b_ref, c_ref, ux_ref, ub_ref, uc_ref, dtc_ref, dtr_ref, prow_ref, bcol_ref, acol_ref, st_ref,
             dy_ref, dx_ref, db_ref, dc_ref, ddt_ref, dp_ref, ds_scr):
        c = pl.program_id(1)

        @pl.when(c == 0)
        def _():
            ds_scr[...] = jnp.zeros_like(ds_scr)
            dp_ref[...] = jnp.zeros_like(dp_ref)

        masks = _head_masks()
        ds = ds_scr[...]
        for k in reversed(range(SSD_STEP_CHUNKS)):
            rows = slice(k * SSD_CHUNK, (k + 1) * SSD_CHUNK)
            raw_e = dtc_ref[0, rows]
            prw = prow_ref[0]
            dt_e, a_e, d_e, acs_e, last_e, acs_r4 = _ssd_common(raw_e, prw, dtr_ref[0][:, rows], bcol_ref[0], acol_ref[0])
            xv = x_ref[rows]
            bm, cm = b_ref[rows], c_ref[rows]
            s = st_ref[0, k]
            dyv = dy_ref[rows]
            e_e = jnp.exp(acs_e)
            dec_e = jnp.exp(last_e - acs_e)
            cd_e = jnp.exp(last_e)
            xdt = xv * dt_e
            w = xdt * dec_e
            b16, c16, s16, ds16 = bm.astype(BF16), cm.astype(BF16), s.astype(BF16), ds.astype(BF16)
            cb = _dot(c16, b16, 'nt')
            yoff_raw = _dot(c16, s16)
            dye = dyv * e_e
            dye16 = dye.astype(BF16)
            dcm = _dot(dye16, s16, 'nt')
            ds_prev = ds * cd_e + _dot(cm.T, dye16)
            dacs_e = _segsum(dyv * yoff_raw) * e_e
            dw = _dot(b16, ds16)
            dbm = _dot(w, ds16, 'nt')
            tdec = _segsum(dw * xdt) * dec_e
            dacs_e = dacs_e - tdec
            dlast_e = jnp.sum(tdec, axis=0, keepdims=True)
            dxdt = dw * dec_e
            dlast_e = dlast_e + _segsum(jnp.sum(ds * s, axis=0, keepdims=True)) * cd_e
            dcb = jnp.zeros((n, n), F32)
            for r in range(4):
                lm = _decay_matrix(acs_e, acs_r4, r)
                mr = cb * lm
                dyr16 = (dyv * masks[r]).astype(BF16)
                dm = _dot(dyr16, xdt * masks[r], 'nt')
                dcb = dcb + dm * lm
                dseg = dm * mr
                dcol = jnp.sum(dseg, axis=1, keepdims=True) - jnp.sum(dseg.T, axis=1, keepdims=True)
                dacs_e = dacs_e + dcol * masks[r]
                dxdt = dxdt + _dot(mr.T, dyr16)
            dcm = dcm + _dot(dcb, b16)
            dbm = dbm + _dot(dcb.T, c16)
            dacs_e = dacs_e + jnp.where(_iota((n, 1), 0) == n - 1, dlast_e, 0.0)
            triu = (_iota((n, n), 0) <= _iota((n, n), 1)).astype(F32)
            ddta_e = _dot_exact(triu, dacs_e)
            ddt_e = ddta_e * a_e + _segsum(dxdt * xv)
            dx_ref[rows] = (dxdt * dt_e + dyv * d_e) * _dsilu(ux_ref[rows])
            db_ref[rows] = dbm * _dsilu(ub_ref[rows])
            dc_ref[rows] = dcm * _dsilu(uc_ref[rows])
            draw_e = ddt_e * _sigmoid(raw_e + prw[0:1, :])
            draw_t = draw_e.T
            ddt_ref[0, :, rows] = jnp.concatenate([draw_t[r * SSD_HEAD_DIM:r * SSD_HEAD_DIM + 1] for r in range(4)], axis=0)
            dbias = jnp.sum(draw_e, axis=0, keepdims=True)
            dalog = jnp.sum(ddta_e * dt_e, axis=0, keepdims=True) * a_e
            dd = _segsum(jnp.sum(dyv * xv, axis=0, keepdims=True))
            row3 = _iota((3, 1), 0)
            dp_ref[0] += (jnp.where(row3 == 0, dbias, 0.0) + jnp.where(row3 == 1, dalog, 0.0)
                          + jnp.where(row3 == 2, dd, 0.0))
            ds = ds_prev
        ds_scr[...] = ds


    x, bm, cm, dtcs, dtrs, prs, pcs, st, ch = _ssd_specs(t, True)
    yblk = pl.BlockSpec((SSD_ROWS, 256), lambda g, c: (ch(c), g))
    nblk = pl.BlockSpec((SSD_ROWS, SSD_STATE), lambda g, c: (ch(c), g))
    return pl.pallas_call(
        body, name="ssd_bwd", grid=(SSD_GROUPS, t // SSD_ROWS),
        in_specs=[x, bm, cm, x, bm, cm, dtcs, dtrs, prs, pcs, pcs, st, yblk],
        out_specs=[yblk, nblk, nblk, dtrs, prs],
        out_shape=[jax.ShapeDtypeStruct((t, SSD_INNER), F32), jax.ShapeDtypeStruct((t, BC_WIDTH), F32),
                   jax.ShapeDtypeStruct((t, BC_WIDTH), F32), jax.ShapeDtypeStruct((SSD_GROUPS, 4, t), F32),
                   jax.ShapeDtypeStruct((SSD_GROUPS, 3, 256), F32)],
        scratch_shapes=[pltpu.VMEM((SSD_STATE, 256), F32)],
        compiler_params=_cp(("parallel", "arbitrary")))(xbc, xbc, xbc, pre, pre, pre, dtc, dtr, prow, bcol, acol,
                                                         states, dy)


GROUP_W = SSD_INNER // SSD_GROUPS


def _mix_specs(tb):
    row = pl.BlockSpec((tb, 2048), lambda i: (i, 0))
    zlo = pl.BlockSpec((tb, 1024), lambda i: (i, O_Z // 1024))
    zhi = pl.BlockSpec((tb, 1024), lambda i: (i, O_Z // 1024 + 1))
    vec = pl.BlockSpec((1, 2048), lambda i: (0, 0))
    return row, zlo, zhi, vec


def _mix_fwd(attn, y, proj, g_attn, g_ssd):
    t = attn.shape[0]
    tb = _rows(t, 256)

    def body(a_ref, y_ref, zlo_ref, zhi_ref, ga_ref, gs_ref, o_ref):
        av = a_ref[...]
        r = lax.rsqrt(jnp.mean(av * av, axis=-1, keepdims=True) + EPS)
        o_ref[:, :ATTN_WIDTH] = (av * r * ga_ref[...]).astype(BF16)
        for g in range(SSD_GROUPS):
            lo, hi = g * GROUP_W, (g + 1) * GROUP_W
            zref = zlo_ref if g < 4 else zhi_ref
            z = zref[:, lo % 1024:lo % 1024 + GROUP_W]
            yg = y_ref[:, lo:hi] * (z * _sigmoid(z))
            rg = lax.rsqrt(jnp.mean(yg * yg, axis=-1, keepdims=True) + EPS)
            o_ref[:, ATTN_WIDTH + lo:ATTN_WIDTH + hi] = (yg * rg * gs_ref[:, lo:hi]).astype(BF16)

    row, zlo, zhi, vec = _mix_specs(tb)
    return pl.pallas_call(
        body, name="mix_fwd", grid=(t // tb,), in_specs=[row, row, zlo, zhi, vec, vec],
        out_specs=pl.BlockSpec((tb, 4096), lambda i: (i, 0)), out_shape=jax.ShapeDtypeStruct((t, 4096), BF16),
        compiler_params=_cp(("parallel",)))(attn, y, proj, proj, g_attn, g_ssd)


def _mix_bwd(dmix, attn, y, proj, g_attn, g_ssd):
    t = attn.shape[0]
    tb = _rows(t, 256)

    def body(dm_ref, a_ref, y_ref, zlo_ref, zhi_ref, ga_ref, gs_ref, da_ref, dy_ref, dz_ref, dga_ref, dgs_ref):
        i = pl.program_id(0)
        av = a_ref[...]
        dn = dm_ref[:, :ATTN_WIDTH].astype(F32)
        r = lax.rsqrt(jnp.mean(av * av, axis=-1, keepdims=True) + EPS)
        u = dn * ga_ref[...]
        da_ref[...] = r * u - av * (r * r * r * jnp.mean(u * av, axis=-1, keepdims=True))
        dga = jnp.sum(dn * av * r, axis=0, keepdims=True)

        @pl.when(i == 0)
        def _():
            dga_ref[...] = dga

        @pl.when(i > 0)
        def _():
            dga_ref[...] += dga

        for g in range(SSD_GROUPS):
            lo, hi = g * GROUP_W, (g + 1) * GROUP_W
            zref = zlo_ref if g < 4 else zhi_ref
            z = zref[:, lo % 1024:lo % 1024 + GROUP_W]
            yv = y_ref[:, lo:hi]
            sg = _sigmoid(z)
            sz = z * sg
            yg = yv * sz
            rg = lax.rsqrt(jnp.mean(yg * yg, axis=-1, keepdims=True) + EPS)
            do = dm_ref[:, ATTN_WIDTH + lo:ATTN_WIDTH + hi].astype(F32)
            ug = do * gs_ref[:, lo:hi]
            dyg = rg * ug - yg * (rg * rg * rg * jnp.mean(ug * yg, axis=-1, keepdims=True))
            dy_ref[:, lo:hi] = dyg * sz
            dz_ref[:, lo:hi] = (dyg * yv * (sg * (1.0 + z * (1.0 - sg)))).astype(BF16)
            dgs = jnp.sum(do * yg * rg, axis=0, keepdims=True)

            @pl.when(i == 0)
            def _():
                dgs_ref[:, lo:hi] = dgs

            @pl.when(i > 0)
            def _():
                dgs_ref[:, lo:hi] += dgs

    row, zlo, zhi, vec = _mix_specs(tb)
    return pl.pallas_call(
        body, name="mix_bwd", grid=(t // tb,),
        in_specs=[pl.BlockSpec((tb, 4096), lambda i: (i, 0)), row, row, zlo, zhi, vec, vec],
        out_specs=[row, row, row, vec, vec],
        out_shape=[jax.ShapeDtypeStruct((t, 2048), F32), jax.ShapeDtypeStruct((t, 2048), F32),
                   jax.ShapeDtypeStruct((t, 2048), BF16), jax.ShapeDtypeStruct((1, 2048), F32),
                   jax.ShapeDtypeStruct((1, 2048), F32)],
        compiler_params=_cp(("arbitrary",)))(dmix, attn, y, proj, proj, g_attn, g_ssd)


def _adamw(w, g, m, v, name):
    r, c = w.shape
    tb = _rows(r, 256)
    c1 = 1.0 - ADAM_B1 ** ADAM_STEP
    c2 = 1.0 - ADAM_B2 ** ADAM_STEP

    def body(w_ref, g_ref, m_ref, v_ref, d_ref, m2_ref, v2_ref):
        gv = g_ref[...]
        m2 = ADAM_B1 * m_ref[...] + (1.0 - ADAM_B1) * gv
        v2 = ADAM_B2 * v_ref[...] + (1.0 - ADAM_B2) * (gv * gv)
        d_ref[...] = -ADAM_LR * ((m2 / c1) / (jnp.sqrt(v2 / c2) + ADAM_EPS) + ADAM_WD * w_ref[...])
        m2_ref[...] = m2
        v2_ref[...] = v2

    blk = pl.BlockSpec((tb, c), lambda i: (i, 0))
    shp = jax.ShapeDtypeStruct((r, c), F32)
    return pl.pallas_call(body, name=name, grid=(r // tb,), in_specs=[blk] * 4, out_specs=[blk] * 3,
                          out_shape=[shp] * 3, compiler_params=_cp(("parallel",)))(w, g, m, v)


def _adamw_halves(w, mine, theirs, m, v, pos, name, cols=False):
    r, c = w.shape
    h = r if cols else r // 2
    tb = _rows(h, 128)
    nh = h // tb
    c1 = 1.0 - ADAM_B1 ** ADAM_STEP
    c2 = 1.0 - ADAM_B2 ** ADAM_STEP

    def body(pos_ref, w_ref, a_ref, b_ref, m_ref, v_ref, g_ref, d_ref, m2_ref, v2_ref):
        which = pl.program_id(1) if cols else pl.program_id(0) // nh
        gv = jnp.where(which == pos_ref[0], a_ref[...], b_ref[...])
        m2 = ADAM_B1 * m_ref[...] + (1.0 - ADAM_B1) * gv
        v2 = ADAM_B2 * v_ref[...] + (1.0 - ADAM_B2) * (gv * gv)
        g_ref[...] = gv
        d_ref[...] = -ADAM_LR * ((m2 / c1) / (jnp.sqrt(v2 / c2) + ADAM_EPS) + ADAM_WD * w_ref[...])
        m2_ref[...] = m2
        v2_ref[...] = v2

    if cols:
        full = pl.BlockSpec((tb, c // 2), lambda i, j, pref: (i, j))
        mine_spec = theirs_spec = pl.BlockSpec((tb, c // 2), lambda i, j, pref: (i, 0))
        grid = (nh, 2)
    else:
        full = pl.BlockSpec((tb, c), lambda i, pref: (i, 0))
        mine_spec = pl.BlockSpec((tb, c), lambda i, pref: (jnp.where(i // nh == pref[0], i % nh,
                                                                     jnp.where(pref[0] == 0, nh - 1, 0)), 0))
        theirs_spec = pl.BlockSpec((tb, c), lambda i, pref: (jnp.where(i // nh != pref[0], i % nh,
                                                                       jnp.where(pref[0] == 0, 0, nh - 1)), 0))
        grid = (r // tb,)
    shp = jax.ShapeDtypeStruct((r, c), F32)
    grid_spec = pltpu.PrefetchScalarGridSpec(num_scalar_prefetch=1, grid=grid,
                                             in_specs=[full, mine_spec, theirs_spec, full, full],
                                             out_specs=[full] * 4)
    return pl.pallas_call(body, name=name, grid_spec=grid_spec, out_shape=[shp] * 4,
                          compiler_params=_cp(("parallel",) * len(grid)))(pos, w, mine, theirs, m, v)


def _sum_own_half(g4, recv, pos, name, cols=False):
    _, r, c = g4.shape
    h, c = (r, c // 2) if cols else (r // 2, c)
    tb = _rows(h, 128)
    nh = h // tb

    def slot(j, pref):
        return (pref[1] + 1 + j) % N_CHIPS

    if cols:
        own = lambda j, i, pref: (slot(j, pref), i, pref[0])
    else:
        own = lambda j, i, pref: (slot(j, pref), pref[0] * nh + i, 0)
    same = lambda j, i, pref: (slot(j, pref), i, 0)

    def body(pos_ref, a_ref, b_ref, o_ref):
        o_ref[...] = (a_ref[...] + b_ref[...]).astype(BF16)

    grid_spec = pltpu.PrefetchScalarGridSpec(
        num_scalar_prefetch=1, grid=(N_CHIPS - 1, nh),
        in_specs=[pl.BlockSpec((1, tb, c), own), pl.BlockSpec((1, tb, c), same)],
        out_specs=pl.BlockSpec((1, tb, c), same))
    return pl.pallas_call(body, name=name, grid_spec=grid_spec,
                          out_shape=jax.ShapeDtypeStruct((N_CHIPS, h, c), BF16),
                          compiler_params=_cp(("parallel", "parallel")))(pos, g4, recv)


def _sum_chips(g4, recv, parts, pos, name, cols=False):
    _, r, c = g4.shape
    h, c = (r, c // 2) if cols else (r // 2, c)
    tb = _rows(h, 128)
    nh = h // tb
    own = (lambda i, pref: (pref[1], i, pref[0])) if cols else (lambda i, pref: (pref[1], pref[0] * nh + i, 0))

    def body(pos_ref, a_ref, b_ref, p_ref, o_ref):
        own = a_ref[0] + b_ref[0]
        o_ref[...] = ((own + p_ref[0].astype(F32)) + p_ref[1].astype(F32)) + p_ref[2].astype(F32)

    grid_spec = pltpu.PrefetchScalarGridSpec(
        num_scalar_prefetch=1, grid=(nh,),
        in_specs=[pl.BlockSpec((1, tb, c), own),
                  pl.BlockSpec((1, tb, c), lambda i, pref: (pref[1], i, 0)),
                  pl.BlockSpec((3, tb, c), lambda i, pref: (0, i, 0))],
        out_specs=pl.BlockSpec((tb, c), lambda i, pref: (i, 0)))
    return pl.pallas_call(body, name=name, grid_spec=grid_spec, out_shape=jax.ShapeDtypeStruct((h, c), F32),
                          compiler_params=_cp(("parallel",)))(pos, g4, recv, parts)


def _me():
    return lax.axis_index("x"), lax.axis_index("y"), lax.axis_index("c")


def _flip(v, bit):
    return (1 - v) if bit else v


CHIP_FLIPS = [(1, 0), (0, 1), (1, 1)]


def _forward_halves(gathered):
    def body(g_ref, o_ref, token, send_sems, recv_sems):
        x, y, c = _me()
        h = g_ref.shape[2] // 2
        cps = []
        for k, (fx, fy) in enumerate(CHIP_FLIPS):
            peer_chip = 2 * _flip(x, fx) + _flip(y, fy)
            mine = o_ref.at[peer_chip, :, pl.ds(c * h, h)]
            cp = pltpu.make_async_remote_copy(src_ref=mine, dst_ref=mine, send_sem=send_sems.at[k],
                                              recv_sem=recv_sems.at[k], device_id=(x, y, 1 - c), device_id_type=MESH)
            cp.start()
            cps.append(cp)
        for k, (fx, fy) in enumerate(CHIP_FLIPS):
            peer_chip = 2 * _flip(x, fx) + _flip(y, fy)
            theirs = o_ref.at[peer_chip, :, pl.ds((1 - c) * h, h)]
            pltpu.make_async_remote_copy(src_ref=theirs, dst_ref=theirs, send_sem=send_sems.at[k],
                                         recv_sem=recv_sems.at[k], device_id=(x, y, 1 - c),
                                         device_id_type=MESH).wait_recv()
        for cp in cps:
            cp.wait_send()
        token[...] = jnp.zeros_like(token)

    return pl.pallas_call(
        body, name="gather_forward_w_in", in_specs=[HBM_SPEC],
        out_specs=[HBM_SPEC, pl.BlockSpec(memory_space=pltpu.VMEM)],
        out_shape=[jax.ShapeDtypeStruct(gathered.shape, gathered.dtype), TOKEN],
        scratch_shapes=[pltpu.SemaphoreType.DMA((3,)), pltpu.SemaphoreType.DMA((3,))],
        input_output_aliases={0: 0},
        compiler_params=pltpu.CompilerParams(has_side_effects=True))(gathered)


SEM_SPEC = pl.BlockSpec(memory_space=pltpu.SEMAPHORE)
ANY_SPEC = pl.BlockSpec(memory_space=pl.ANY)
DATAFLOW = pltpu.SideEffectType.DATAFLOW_SIDE_EFFECTING


def _in_hbm(a):
    return pltpu.with_memory_space_constraint(a, pltpu.HBM)


def _push_start(srcs, land_shapes, route, peers, name):
    n, npeer = len(srcs), len(peers)
    lands = [lax.empty(shp, s.dtype) for shp, s in zip(land_shapes, srcs)]

    def body(*refs):
        ins, lnd = refs[:n], refs[n:2 * n]
        send_sems, recv_sems = refs[2 * n], refs[2 * n + 1]
        token = refs[-1]
        x, y, c = _me()
        for t in range(n):
            for k, (fx, fy, fc) in enumerate(peers):
                src, dst = route(ins[t], lnd[t], k, x, y, c)
                pltpu.make_async_remote_copy(
                    src_ref=src, dst_ref=dst, send_sem=send_sems.at[npeer * t + k],
                    recv_sem=recv_sems.at[npeer * t + k],
                    device_id=(_flip(x, fx), _flip(y, fy), _flip(c, fc)), device_id_type=MESH).start()
        token[...] = jnp.zeros_like(token)

    bufs = [_in_hbm(a) for a in list(srcs) + lands]
    outs = pl.pallas_call(
        body, name=name,
        out_shape=(pltpu.SemaphoreType.DMA((npeer * n,)), pltpu.SemaphoreType.DMA((npeer * n,)),
                   *[pltpu.HBM(b.shape, b.dtype) for b in bufs], TOKEN),
        in_specs=[HBM_SPEC] * (2 * n),
        out_specs=(SEM_SPEC, SEM_SPEC, *[HBM_SPEC] * (2 * n), pl.BlockSpec(memory_space=pltpu.VMEM)),
        input_output_aliases={i: 2 + i for i in range(2 * n)},
        compiler_params=pltpu.CompilerParams(has_side_effects=DATAFLOW))(*bufs)
    return outs[0], outs[1], list(outs[2:2 + n]), list(outs[2 + n:2 + 2 * n]), outs[-1]


def _push_wait(send_sems, recv_sems, srcs, lands, after, route, peers, name):
    n, npeer = len(srcs), len(peers)

    def body(*refs):
        ins, lnd = refs[:n], refs[n:2 * n]
        ssem, rsem = refs[2 * n], refs[2 * n + 1]
        x, y, c = _me()
        for t in range(n):
            for k, (fx, fy, fc) in enumerate(peers):
                src, dst = route(ins[t], lnd[t], k, x, y, c)
                cp = pltpu.make_async_remote_copy(
                    src_ref=src, dst_ref=dst, send_sem=ssem.at[npeer * t + k], recv_sem=rsem.at[npeer * t + k],
                    device_id=(_flip(x, fx), _flip(y, fy), _flip(c, fc)), device_id_type=MESH)
                cp.wait_send()
                cp.wait_recv()

    bufs = list(srcs) + list(lands)
    outs = pl.pallas_call(
        body, name=name, out_shape=tuple(pltpu.HBM(b.shape, b.dtype) for b in bufs),
        in_specs=[HBM_SPEC] * (2 * n) + [SEM_SPEC, SEM_SPEC, ANY_SPEC], out_specs=tuple([HBM_SPEC] * (2 * n)),
        input_output_aliases={i: i for i in range(2 * n)},
        compiler_params=pltpu.CompilerParams(has_side_effects=DATAFLOW))(*bufs, send_sems, recv_sems, after)
    return list(outs[:n]), list(outs[n:])


OTHER_CHIPS = [(fx, fy, 0) for fx, fy in CHIP_FLIPS]
SIBLING = [(0, 0, 1)]


def _route_gather(src, land, k, x, y, c):
    return src, land.at[2 * x + y]


def _route_gather_half(src, land, k, x, y, c):
    h = src.shape[1] // 2
    return src.at[:, pl.ds(c * h, h)], land.at[2 * x + y, :, pl.ds(c * h, h)]


def _route_gather_half_wait(src, land, k, x, y, c):
    fx, fy = CHIP_FLIPS[k]
    h = src.shape[1] // 2
    return src.at[:, pl.ds(c * h, h)], land.at[2 * _flip(x, fx) + _flip(y, fy), :, pl.ds(c * h, h)]


def _route_gather_wait(src, land, k, x, y, c):
    fx, fy = CHIP_FLIPS[k]
    return src, land.at[2 * _flip(x, fx) + _flip(y, fy)]


def _route_scatter(src, land, k, x, y, c):
    fx, fy = CHIP_FLIPS[k]
    return src.at[2 * _flip(x, fx) + _flip(y, fy)], land.at[k]


def _route_exchange(src, land, k, x, y, c):
    h = land.shape[1]
    return src.at[:, pl.ds((1 - c) * h, h)], land


def _route_whole(src, land, k, x, y, c):
    return src, land


def _route_exchange_cols(src, land, k, x, y, c):
    h = land.shape[2]
    return src.at[:, :, pl.ds((1 - c) * h, h)], land


ALL_OTHERS = [((k >> 2) & 1, (k >> 1) & 1, k & 1) for k in range(1, 8)]


def _route_to_all(src, land, k, x, y, c):
    return src, land.at[4 * x + 2 * y + c]


def _route_to_all_wait(src, land, k, x, y, c):
    fx, fy, fc = ALL_OTHERS[k]
    return src, land.at[4 * _flip(x, fx) + 2 * _flip(y, fy) + _flip(c, fc)]


def _sum_devices(parts):
    def body(p_ref, o_ref):
        acc = p_ref[0]
        for d in range(1, 8):
            acc = acc + p_ref[d]
        o_ref[...] = acc

    vm = pl.BlockSpec(memory_space=pltpu.VMEM)
    return pl.pallas_call(body, name="allreduce_sum", in_specs=[vm], out_specs=vm,
                          out_shape=jax.ShapeDtypeStruct(parts.shape[1:], F32),
                          compiler_params=pltpu.CompilerParams(vmem_limit_bytes=VMEM_LIMIT))(parts)


def _grad_exchange_start(g4, tag, cols=False):
    land = (N_CHIPS, g4.shape[1], g4.shape[2] // 2) if cols else (N_CHIPS, g4.shape[1] // 2, g4.shape[2])
    route = _route_exchange_cols if cols else _route_exchange
    send_sems, recv_sems, srcs, lands, token = _push_start(
        [g4], [land], route, SIBLING, name="grad_exchange_start_" + tag)
    return (send_sems, recv_sems, srcs, lands, tag, cols), token


def _grad_scatter_start(state, pos, after):
    send_sems, recv_sems, srcs, lands, tag, cols = state
    route = _route_exchange_cols if cols else _route_exchange
    (g4,), (recv,) = _push_wait(send_sems, recv_sems, srcs, lands, after, route, SIBLING,
                                name="grad_exchange_wait_" + tag)
    return _grad_pair_scatter(g4, recv, pos, tag, cols)


def _grad_pair_scatter(g4, recv, pos, tag, cols=False):
    p16 = _sum_own_half(g4, recv, pos, name="grad_sum_pair_" + tag, cols=cols)
    send_sems, recv_sems, srcs, lands, token = _push_start(
        [p16], [(3,) + p16.shape[1:]], _route_scatter, OTHER_CHIPS, name="grad_scatter_start_" + tag)
    return (g4, recv, send_sems, recv_sems, srcs, lands, tag, cols), token


def _grad_sum_and_share(state, pos, after):
    g4, recv, send_sems, recv_sems, srcs, lands, tag, cols = state
    parts = _push_wait(send_sems, recv_sems, srcs, lands, after, _route_scatter, OTHER_CHIPS,
                       name="grad_scatter_wait_" + tag)[1][0]
    mine = _sum_chips(g4, recv, parts, pos, name="grad_sum_chips_" + tag, cols=cols)
    send_sems, recv_sems, srcs, lands, token = _push_start(
        [mine], [mine.shape], _route_whole, SIBLING, name="grad_share_start_" + tag)
    return (send_sems, recv_sems, srcs, lands, tag), token


def _grad_share_wait(state, after):
    send_sems, recv_sems, srcs, lands, tag = state
    (mine,), (theirs,) = _push_wait(send_sems, recv_sems, srcs, lands, after, _route_whole, SIBLING,
                                    name="grad_share_wait_" + tag)
    return mine, theirs


def _local_step(x, tgt, p, hooks):
    t = x.shape[0]
    tables = _rope_tables(t)
    sinks = p['sinks'].reshape(N_Q_HEADS)

    def told(name, value):
        return tuple(hooks.grad_ready(name, value))

    xn = _rmsnorm_fwd(x, p['norm_mix'], "norm_mix_fwd", deps=hooks.first_deps)
    w_in_t, w_in_dt, in_deps = hooks.weight_in(xn)
    proj = _matmul(xn, w_in_t, mode='nt', name="in_proj", n_limit=MAIN_WIDTH, deps=in_deps)
    dt_raw = _matmul(xn, w_in_dt, mode='nt', name="in_proj_dt")[:, :SSD_HEADS]
    ssd_conv_w, ffn_conv_w = hooks.conv_weights(proj)
    p = dict(p, ssd_conv_w=ssd_conv_w, ffn_conv_w=ffn_conv_w)
    attn = _attn_fwd(proj, sinks, tables)
    conv_b = p['ssd_conv_b']
    xbc, xbc_pre = _conv_silu_fwd(proj, p['ssd_conv_w'], conv_b, col0=O_XBC, width=CONV_CH, name="ssd_conv_fwd")
    sp = _ssd_params(dt_raw, p['dt_bias'].reshape(-1), p['a_log'].reshape(-1), p['ssd_d'].reshape(-1))
    y, states = _ssd_fwd(xbc, sp)
    mix = _mix_fwd(attn, y, proj, p['attn_out_norm'], p['ssd_norm'])
    w_out = hooks.weight('w_out', mix)
    h1 = _matmul(mix, w_out, mode='nn', name="out_proj", add=x)
    hn = _rmsnorm_fwd(h1, p['norm_ffn'], "norm_ffn_fwd")
    w_up = hooks.weight('w_up', hn)
    u0 = _matmul(hn, w_up, mode='nn', name="ffn_up", b_owner=True, tn=1408)
    a, u = _ffn_act_fwd(u0, p['ffn_conv_w'], p['ffn_conv_b'])
    w_down = hooks.weight('w_down', a)
    h2 = _matmul(a, w_down, mode='nn', name="ffn_down", add=h1, tk=2816)
    loss, dh2, dh2_16, g_norm_final = _final_loss(h2, p['norm_final'].reshape(1, D_MODEL), tgt)

    g = {}
    da = _matmul(dh2_16, w_down, mode='nt', name="ffn_down_dx", out_dtype=BF16, tn=1408)
    g['w_down'] = _matmul(a, dh2_16, mode='tn', name="ffn_down_dw", tm=1408)
    dep = told('w_down', g['w_down'])
    du0, dcw, dcb = _ffn_act_bwd(u0, u, p['ffn_conv_w'], da)
    g['ffn_conv_w'] = dcw.transpose(1, 0, 2).reshape(FFN_CONV, 2 * D_FF)
    g['ffn_conv_b'] = dcb.transpose(1, 0, 2).reshape(1, 2 * D_FF)
    g['w_up'] = _matmul(hn, du0, mode='tn', name="ffn_up_dw", deps=dep, b_halves=True, owner_major=True,
                        tn=1408)
    dep = told('w_up', g['w_up'])
    dhn = _matmul(du0, w_up, mode='nt', name="ffn_up_dx", out_dtype=BF16, deps=dep, a_halves=True,
                  b_owner=True, tk=2816)
    dh1, dh1_16, g['norm_ffn'] = _rmsnorm_bwd(h1, p['norm_ffn'], dhn, dh2, "norm_ffn_bwd")

    g['w_out'] = _matmul(mix, dh1_16, mode='tn', name="out_proj_dw")
    dep = told('w_out', g['w_out'])
    dmix = _matmul(dh1_16, w_out, mode='nt', name="out_proj_dx", out_dtype=BF16, deps=dep)
    dattn, dy, dz, g['attn_out_norm'], g['ssd_norm'] = _mix_bwd(dmix, attn, y, proj, p['attn_out_norm'],
                                                                p['ssd_norm'])
    dq, dk, dv, dsink = _attn_bwd(proj, sinks, tables, dattn)
    g['sinks'] = dsink[:, :, 0].reshape(1, N_Q_HEADS)
    dxs, dbm, dcm, ddt8, dpar = _ssd_bwd(xbc, xbc_pre, sp, states, dy)
    dpar = dpar[:, :, ::SSD_HEAD_DIM]
    g['dt_bias'] = dpar[:, 0, :].reshape(1, SSD_HEADS)
    g['a_log'] = dpar[:, 1, :].reshape(1, SSD_HEADS)
    g['ssd_d'] = dpar[:, 2, :].reshape(1, SSD_HEADS)
    dproj, g['ssd_conv_w'], g['ssd_conv_b'] = _ssd_conv_bwd(proj, p['ssd_conv_w'], dxs, dbm, dcm, dq, col0=O_XBC,
                                                            name="ssd_conv_bwd")
    for piece, col in ((dk, O_K), (dv, O_V), (dz, O_Z)):
        dproj = lax.dynamic_update_slice(dproj, piece, (0, col))
    ddt = ddt8.transpose(2, 0, 1).reshape(t, SSD_HEADS)
    ddt_pad = jnp.pad(ddt, ((0, 0), (0, LANES - SSD_HEADS))).astype(BF16)
    g['w_in'] = (_matmul(dproj, xn, mode='tn', name="in_proj_dw", m_rows=IN_PROJ_WIDTH),
                 _matmul(ddt_pad, xn, mode='tn', name="in_proj_dt_dw"))
    dep = told('w_in', g['w_in'])
    dxn_dt = _matmul(ddt_pad, w_in_dt, mode='nn', name="in_proj_dt_dx", deps=dep)
    dxn = _matmul(dproj, w_in_t, mode='nn', name="in_proj_dx", out_dtype=BF16, add=dxn_dt, k_limit=MAIN_WIDTH,
                  tk=2304)
    dep = told(None, dxn)
    dx, _, g['norm_mix'] = _rmsnorm_bwd(x, p['norm_mix'], dxn, dh1, "norm_mix_bwd", deps=dep)
    g['norm_final'] = g_norm_final
    return loss, dx, g


def _pack(arrs):
    flat = jnp.concatenate([a.reshape(-1) for a in arrs])
    n = flat.shape[0]
    rows = -(-n // LANES)
    rows = -(-rows // 8) * 8
    return jnp.pad(flat, (0, rows * LANES - n)).reshape(rows, LANES)


def _unpack(packed, shapes):
    flat = packed.reshape(-1)
    out, off = [], 0
    for s in shapes:
        n = 1
        for d in s:
            n *= d
        out.append(flat[off:off + n].reshape(s))
        off += n
    return out


class _StepHooks:
    def __init__(self, first_deps, weight_in, conv_weights, weight, grad_ready):
        self.first_deps = first_deps
        self.weight_in = weight_in
        self.conv_weights = conv_weights
        self.weight = weight
        self.grad_ready = grad_ready


def kernel(x, norm_mix, w_in, sinks, attn_out_norm, ssd_conv_w, ssd_conv_b, dt_bias, a_log, ssd_d, ssd_norm, w_out, norm_ffn, w_up, ffn_conv_w, ffn_conv_b, w_down, norm_final, loss_target, m_norm_mix, m_w_in, m_sinks, m_attn_out_norm, m_ssd_conv_w, m_ssd_conv_b, m_dt_bias, m_a_log, m_ssd_d, m_ssd_norm, m_w_out, m_norm_ffn, m_w_up, m_ffn_conv_w, m_ffn_conv_b, m_w_down, m_norm_final, v_norm_mix, v_w_in, v_sinks, v_attn_out_norm, v_ssd_conv_w, v_ssd_conv_b, v_dt_bias, v_a_log, v_ssd_d, v_ssd_norm, v_w_out, v_norm_ffn, v_w_up, v_ffn_conv_w, v_ffn_conv_b, v_w_down, v_norm_final):
    args = dict(locals())
    w = {n: args[n] for n in WEIGHTS}
    m = {n: args['m_' + n] for n in WEIGHTS}
    v = {n: args['v_' + n] for n in WEIGHTS}
    xi, yi, ci = _me()
    chip = 2 * xi + yi
    pos = jnp.stack([ci, chip]).astype(jnp.int32)

    conv_shard = _pack([ssd_conv_w[0], ffn_conv_w[0]])
    conv_gather = _push_start([conv_shard], [(N_CHIPS,) + conv_shard.shape], _route_gather, OTHER_CHIPS,
                              name="gather_start_conv")

    def conv_weights(after):
        send_sems, recv_sems, srcs, lands, _ = conv_gather
        (own,), (got,) = _push_wait(send_sems, recv_sems, srcs, lands, after, _route_gather_wait, OTHER_CHIPS,
                                    name="gather_wait_conv")
        whole = lax.dynamic_update_slice(got, own[None], (chip, 0, 0))
        per_chip = [_unpack(whole[j], [ssd_conv_w.shape[1:], ffn_conv_w.shape[1:]]) for j in range(N_CHIPS)]
        return (jnp.concatenate([pc[0] for pc in per_chip], axis=1),
                jnp.concatenate([pc[1] for pc in per_chip], axis=1))

    w_in_t, m_in_t, v_in_t = (jnp.transpose(a[0]) for a in (w_in, m_w_in, v_w_in))
    in_shard = (w_in_t + conv_gather[4][:1, :1]).astype(BF16)
    in_gather = _push_start([in_shard], [(N_CHIPS,) + in_shard.shape], _route_gather_half, OTHER_CHIPS,
                            name="gather_start_w_in")
    gathers = {}
    order = in_gather[4][:1, :1]
    for n, shard in (('w_out', w_out[0]), ('w_up', w_up[0]), ('w_down', w_down[0])):
        shard = (shard + order).astype(BF16)
        gathers[n] = _push_start([shard], [(N_CHIPS,) + shard.shape], _route_gather, OTHER_CHIPS,
                                 name="gather_start_" + n)
        order = gathers[n][4][:1, :1]

    def weight_in(after):
        send_sems, recv_sems, srcs, lands, _ = in_gather
        (own,), (got,) = _push_wait(send_sems, recv_sems, srcs, lands, after, _route_gather_half_wait, OTHER_CHIPS,
                                    name="gather_wait_w_in")
        got, _ = _forward_halves(got)
        full_in_t = lax.dynamic_update_slice(got, own[None], (chip, 0, 0)).reshape(IN_PROJ_WIDTH, D_MODEL)
        w_in_dt = jnp.pad(full_in_t[MAIN_WIDTH:], ((0, LANES - SSD_HEADS), (0, 0)))
        return full_in_t, w_in_dt, ()

    def weight(name, after):
        send_sems, recv_sems, srcs, lands, _ = gathers[name]
        (own,), (got,) = _push_wait(send_sems, recv_sems, srcs, lands, after, _route_gather_wait, OTHER_CHIPS,
                                    name="gather_wait_" + name)
        whole = lax.dynamic_update_slice(got, own[None], (chip, 0, 0))
        return whole if name == 'w_up' else whole.reshape(-1, D_MODEL)

    reductions, exchanging = {}, {}

    def flush(after):
        tokens = []
        for prev in list(exchanging):
            reductions[prev], token = _grad_scatter_start(exchanging.pop(prev), pos, after)
            tokens.append(token)
        return tokens

    def grad_ready(name, value):
        if name is None:
            return flush(value)
        if name == 'w_in':
            main, dtp = value
            value = lax.dynamic_update_slice(main, dtp[:SSD_HEADS], (MAIN_WIDTH, 0))
        g4 = value if value.ndim == 3 else value.reshape(N_CHIPS, -1, value.shape[1])
        tokens = flush(g4)
        exchanging[name], token = _grad_exchange_start(g4, name, cols=(name == 'w_in'))
        return tokens + [token]

    small = {
        'norm_mix': norm_mix, 'sinks': sinks, 'attn_out_norm': attn_out_norm,
        'ssd_conv_b': ssd_conv_b, 'dt_bias': dt_bias, 'a_log': a_log, 'ssd_d': ssd_d, 'ssd_norm': ssd_norm,
        'norm_ffn': norm_ffn, 'ffn_conv_b': ffn_conv_b, 'norm_final': norm_final,
    }
    loss, dx, g = _local_step(x[0], loss_target[0], small,
                              _StepHooks((gathers['w_down'][4],), weight_in, conv_weights, weight, grad_ready))

    small_names = [n for n in WEIGHTS if n not in BIG]
    small_g = [loss[:, :1]] + [g[n] for n in small_names]
    small_shapes = [(1, 1)] + [tuple(a.shape) for a in small_g[1:]]
    packed = _pack(small_g)
    spread = _push_start([packed], [(8,) + packed.shape], _route_to_all, ALL_OTHERS, name="allreduce_start")
    grads, deltas, new_m, new_v = {}, {}, {}, {}
    after = spread[4]
    shares = {}
    for n in ('w_down', 'w_up', 'w_out'):
        shares[n], after = _grad_sum_and_share(reductions[n], pos, after)
    for n in ('w_down', 'w_up', 'w_out', 'w_in'):
        if n == 'w_out':
            shares['w_in'], after = _grad_sum_and_share(reductions['w_in'], pos, after)
        mine, theirs = _grad_share_wait(shares[n], after)
        if n == 'w_in':
            outs = _adamw_halves(w_in_t, mine, theirs, m_in_t, v_in_t, pos, name="adamw_" + n, cols=True)
            after = outs[1]
            outs = [jnp.transpose(o) for o in outs]
        else:
            outs = _adamw_halves(w[n][0], mine, theirs, m[n][0], v[n][0], pos, name="adamw_" + n)
            after = outs[1]
        grads[n], deltas[n], new_m[n], new_v[n] = [o[None] for o in outs]
    (own,), (landed,) = _push_wait(spread[0], spread[1], spread[2], spread[3], after, _route_to_all_wait, ALL_OTHERS,
                                   name="allreduce_wait")
    landed = lax.dynamic_update_slice(landed, own[None], (4 * xi + 2 * yi + ci, 0, 0))
    red = _unpack(_sum_devices(landed), small_shapes)
    loss_out = red[0].reshape(())
    gsm = dict(zip(small_names, red[1:]))
    gsm['ssd_conv_w'] = lax.dynamic_slice(gsm['ssd_conv_w'], (0, chip * ssd_conv_w.shape[2]),
                                          (SSD_CONV, ssd_conv_w.shape[2]))
    gsm['ffn_conv_w'] = lax.dynamic_slice(gsm['ffn_conv_w'], (0, chip * ffn_conv_w.shape[2]),
                                          (FFN_CONV, ffn_conv_w.shape[2]))

    shapes = [tuple(w[n].shape) for n in small_names]
    gp = _pack([gsm[n] for n in small_names])
    d, m2, v2 = _adamw(_pack([w[n] for n in small_names]), gp, _pack([m[n] for n in small_names]),
                       _pack([v[n] for n in small_names]), name="adamw_small")
    for n, gg, dd, mm, vv in zip(small_names, _unpack(gp, shapes), _unpack(d, shapes), _unpack(m2, shapes),
                                 _unpack(v2, shapes)):
        grads[n], deltas[n], new_m[n], new_v[n] = gg, dd, mm, vv

    return (loss_out, dx[None], *[grads[n] for n in WEIGHTS], *[deltas[n] for n in WEIGHTS],
            *[new_m[n] for n in WEIGHTS], *[new_v[n] for n in WEIGHTS])
```

```python
import functools

import jax
import jax.numpy as jnp
from jax import lax
from jax.experimental import pallas as pl
from jax.experimental.pallas import tpu as pltpu

F32 = jnp.float32
BF16 = jnp.bfloat16

D_MODEL = 2048
N_Q_HEADS = 32
N_KV_HEADS = 8
HEAD_DIM = 64
WINDOW = 128
ATTN_BLOCK = 128
ROT_DIM = 16
ROPE_THETA = 500000.0
SSD_HEADS = 32
SSD_HEAD_DIM = 64
SSD_INNER = 2048
SSD_GROUPS = 8
SSD_STATE = 128
SSD_CONV = 4
SSD_CHUNK = 128
ATTN_WIDTH = 2048
KV_WIDTH = 512
BC_WIDTH = 1024
CONV_CH = 4096
IN_PROJ_WIDTH = 9248
MAIN_WIDTH = 9216
D_FF = 5632
FFN_CONV = 3
EPS = 1e-6
O_Q, O_K, O_V, O_Z, O_XBC, O_DT = 0, 2048, 2560, 3072, 5120, 9216

ADAM_LR = 0.001
ADAM_B1 = 0.9
ADAM_B2 = 0.999
ADAM_EPS = 1e-08
ADAM_WD = 0.01
ADAM_STEP = 10

N_CHIPS = 4
NEG = -1e30
LANES = 128
VMEM_LIMIT = 48 * 1024 * 1024
MESH = pl.DeviceIdType.MESH
HBM_SPEC = pl.BlockSpec(memory_space=pltpu.HBM)
TOKEN = jax.ShapeDtypeStruct((8, LANES), F32)

WEIGHTS = ['norm_mix', 'w_in', 'sinks', 'attn_out_norm', 'ssd_conv_w', 'ssd_conv_b', 'dt_bias', 'a_log', 'ssd_d',
           'ssd_norm', 'w_out', 'norm_ffn', 'w_up', 'ffn_conv_w', 'ffn_conv_b', 'w_down', 'norm_final']
BIG = ['w_in', 'w_out', 'w_up', 'w_down']


def _cp(sem=None, vmem=VMEM_LIMIT):
    kw = {'vmem_limit_bytes': vmem}
    if sem is not None:
        kw['dimension_semantics'] = sem
    return pltpu.CompilerParams(**kw)


def _tile(n, pref):
    if n <= pref:
        return n
    t = (pref // LANES) * LANES
    while t > LANES and n % t:
        t -= LANES
    assert n % t == 0, (n, pref)
    return t


def _rows(n, pref):
    t = min(n, pref)
    while n % t:
        t -= 8
    if 4 * t < pref:
        t = pref
        while n % t:
            t += 8
    return t


def _iota(shape, dim):
    return lax.broadcasted_iota(jnp.int32, shape, dim)


def _dot(a, b, mode='nn'):
    dn = {'nn': (((1,), (0,)), ((), ())), 'nt': (((1,), (1,)), ((), ())), 'tn': (((0,), (0,)), ((), ()))}[mode]
    return lax.dot_general(a.astype(BF16), b.astype(BF16), dn, preferred_element_type=F32)


def _dot_exact(a, b):
    return lax.dot_general(a, b, (((1,), (0,)), ((), ())), precision=lax.Precision.HIGHEST,
                           preferred_element_type=F32)


def _sigmoid(x):
    return 1.0 / (1.0 + jnp.exp(-x))


def _softplus(x):
    return jnp.maximum(x, 0.0) + jnp.log(1.0 + jnp.exp(-jnp.abs(x)))


def _matmul(a, b, *, mode, name, out_dtype=F32, add=None, deps=(), tm=1024, tn=1024, tk=2048,
            a_halves=False, b_halves=False, b_owner=False, owner_major=False, n_limit=None, k_limit=None,
            m_rows=None, extra=None):
    ash, bsh = (a.shape[1:] if a_halves else a.shape), (b.shape[1:] if (b_halves or b_owner) else b.shape)
    if mode == 'nn':
        (m, k), (k2, n) = ash, bsh
    elif mode == 'nt':
        (m, k), (n, k2) = ash, bsh
    else:
        (k, m), (k2, n) = ash, bsh
    if n_limit is not None:
        assert mode == 'nt' and n_limit <= n
        n = n_limit
    if k_limit is not None:
        assert mode == 'nn' and k_limit <= k2
        k2 = k_limit
    if a_halves:
        assert mode == 'nt'
        k = 2 * k
    if b_halves:
        assert mode == 'tn'
        n = 2 * n
    if b_owner:
        assert mode in ('nn', 'nt')
        if mode == 'nn':
            n = 4 * n
        else:
            k2 = 4 * k2
    assert k == k2, (a.shape, b.shape, mode)
    tm = _tile(m, tm)
    tn = _tile(n // 4 if (owner_major or (b_owner and mode == 'nn')) else (n // 2 if b_halves else n), tn)
    tk = _tile(k // 4 if (b_owner and mode == 'nt') else (k // 2 if a_halves else k), tk)
    nk = k // tk
    has_add = add is not None
    assert not (has_add and owner_major)
    has_extra = extra is not None
    assert not has_extra or (mode == 'nn' and not owner_major)

    def body(*refs):
        a_ref, b_ref = refs[:2]
        add_ref = refs[2] if has_add else None

        def finish(r, o_ref):
            if has_add:
                r = r + add_ref[...].astype(F32)
            if has_extra:
                a2_ref, b2_ref = refs[2 + has_add:4 + has_add]
                r = r + _dot(a2_ref[...], b2_ref[...], 'nn')
            o_ref[...] = r.astype(out_dtype)

        if nk == 1:
            finish(_dot(a_ref[...], b_ref[...], mode), refs[-1])
            return
        o_ref, acc = refs[-2:]
        kk = pl.program_id(2)

        @pl.when(kk == 0)
        def _():
            acc[...] = _dot(a_ref[...], b_ref[...], mode)

        @pl.when((kk > 0) & (kk < nk - 1))
        def _():
            acc[...] += _dot(a_ref[...], b_ref[...], mode)

        @pl.when(kk == nk - 1)
        def _():
            finish(acc[...] + _dot(a_ref[...], b_ref[...], mode), o_ref)

    if mode == 'tn':
        a_spec = pl.BlockSpec((tk, tm), lambda i, j, kk: (kk, i))
    elif a_halves:
        nkh = nk // 2
        a_spec = pl.BlockSpec((None, tm, tk), lambda i, j, kk: (kk // nkh, i, kk % nkh))
    else:
        a_spec = pl.BlockSpec((tm, tk), lambda i, j, kk: (i, kk))
    if mode == 'nt' and b_owner:
        nkq = nk // 4
        b_spec = pl.BlockSpec((None, tn, tk), lambda i, j, kk: (kk // nkq, j, kk % nkq))
    elif mode == 'nt':
        b_spec = pl.BlockSpec((tn, tk), lambda i, j, kk: (j, kk))
    elif b_owner:
        njq = (n // 4) // tn
        b_spec = pl.BlockSpec((None, tk, tn), lambda i, j, kk: (j // njq, kk, j % njq))
    elif b_halves:
        njh = (n // 2) // tn
        b_spec = pl.BlockSpec((None, tk, tn), lambda i, j, kk: (j // njh, kk, j % njh))
    else:
        b_spec = pl.BlockSpec((tk, tn), lambda i, j, kk: (kk, j))
    if owner_major:
        njo = (n // 4) // tn
        o_spec = pl.BlockSpec((None, tm, tn), lambda i, j, kk: (j // njo, i, j % njo))
        out_shape = jax.ShapeDtypeStruct((N_CHIPS, m, n // 4), out_dtype)
    else:
        o_spec = pl.BlockSpec((tm, tn), lambda i, j, kk: (i, j))
        out_shape = jax.ShapeDtypeStruct((m if m_rows is None else m_rows, n), out_dtype)
    dep_spec = pl.BlockSpec((8, LANES), lambda i, j, kk: (0, 0))
    in_specs = [a_spec, b_spec] + ([pl.BlockSpec((tm, tn), lambda i, j, kk: (i, j))] if has_add else [])
    if has_extra:
        k_extra = extra[0].shape[1]
        in_specs += [pl.BlockSpec((tm, k_extra), lambda i, j, kk: (i, 0)),
                     pl.BlockSpec((k_extra, tn), lambda i, j, kk: (0, j))]
    in_specs += [dep_spec] * len(deps)
    args = (a, b) + ((add,) if has_add else ()) + (tuple(extra) if has_extra else ()) + tuple(deps)
    return pl.pallas_call(
        body, name=name, grid=(m // tm, n // tn, nk), in_specs=in_specs, out_specs=o_spec, out_shape=out_shape,
        scratch_shapes=[pltpu.VMEM((tm, tn), F32)] if nk > 1 else [],
        compiler_params=_cp(("parallel", "parallel", "arbitrary")))(*args)


def _rmsnorm_fwd(x, g, name, deps=()):
    t, d = x.shape
    tb = _rows(t, 256)

    def body(x_ref, g_ref, *rest):
        o_ref = rest[-1]
        xv = x_ref[...]
        r = lax.rsqrt(jnp.mean(xv * xv, axis=-1, keepdims=True) + EPS)
        o_ref[...] = (xv * r * g_ref[...]).astype(BF16)

    dep_spec = pl.BlockSpec((8, LANES), lambda i: (0, 0))
    return pl.pallas_call(
        body, name=name, grid=(t // tb,),
        in_specs=[pl.BlockSpec((tb, d), lambda i: (i, 0)), pl.BlockSpec((1, d), lambda i: (0, 0))]
        + [dep_spec] * len(deps),
        out_specs=pl.BlockSpec((tb, d), lambda i: (i, 0)), out_shape=jax.ShapeDtypeStruct((t, d), BF16),
        compiler_params=_cp(("parallel",)))(x, g, *deps)


def _rmsnorm_bwd(x, g, dy, res, name, deps=()):
    t, d = x.shape
    tb = _rows(t, 256)

    def body(x_ref, g_ref, dy_ref, res_ref, *rest):
        dx_ref, dx16_ref, dg_ref = rest[-3:]
        i = pl.program_id(0)
        xv = x_ref[...]
        dyv = dy_ref[...].astype(F32)
        r = lax.rsqrt(jnp.mean(xv * xv, axis=-1, keepdims=True) + EPS)
        u = dyv * g_ref[...]
        dx = r * u - xv * (r * r * r * jnp.mean(u * xv, axis=-1, keepdims=True)) + res_ref[...]
        dx_ref[...] = dx
        dx16_ref[...] = dx.astype(BF16)
        part = jnp.sum(dyv * xv * r, axis=0, keepdims=True)

        @pl.when(i == 0)
        def _():
            dg_ref[...] = part

        @pl.when(i > 0)
        def _():
            dg_ref[...] += part

    row = pl.BlockSpec((tb, d), lambda i: (i, 0))
    vec = pl.BlockSpec((1, d), lambda i: (0, 0))
    return pl.pallas_call(
        body, name=name, grid=(t // tb,),
        in_specs=[row, vec, row, row] + [pl.BlockSpec((8, LANES), lambda i: (0, 0))] * len(deps),
        out_specs=[row, row, vec],
        out_shape=[jax.ShapeDtypeStruct((t, d), F32), jax.ShapeDtypeStruct((t, d), BF16),
                   jax.ShapeDtypeStruct((1, d), F32)],
        compiler_params=_cp(("arbitrary",)))(x, g, dy, res, *deps)


def _final_loss(h, g, tgt):
    t, d = h.shape
    tb = _rows(t, 256)

    def body(h_ref, g_ref, t_ref, loss_ref, dh_ref, dh16_ref, dg_ref):
        i = pl.program_id(0)
        hv = h_ref[...]
        gv = g_ref[...]
        r = lax.rsqrt(jnp.mean(hv * hv, axis=-1, keepdims=True) + EPS)
        y = hv * r * gv
        diff = y - t_ref[...]
        lpart = jnp.sum(jnp.sum(diff * diff, axis=1, keepdims=True), axis=0, keepdims=True) * (0.5 / d)
        dy = diff * (1.0 / d)
        u = dy * gv
        dh = r * u - hv * (r * r * r * jnp.mean(u * hv, axis=-1, keepdims=True))
        dh_ref[...] = dh
        dh16_ref[...] = dh.astype(BF16)
        gpart = jnp.sum(dy * hv * r, axis=0, keepdims=True)
        lrow = jnp.broadcast_to(lpart, (1, LANES))

        @pl.when(i == 0)
        def _():
            loss_ref[...] = lrow
            dg_ref[...] = gpart

        @pl.when(i > 0)
        def _():
            loss_ref[...] += lrow
            dg_ref[...] += gpart

    row = pl.BlockSpec((tb, d), lambda i: (i, 0))
    vec = pl.BlockSpec((1, d), lambda i: (0, 0))
    return pl.pallas_call(
        body, name="final_loss", grid=(t // tb,), in_specs=[row, vec, row],
        out_specs=[pl.BlockSpec((1, LANES), lambda i: (0, 0)), row, row, vec],
        out_shape=[jax.ShapeDtypeStruct((1, LANES), F32), jax.ShapeDtypeStruct((t, d), F32),
                   jax.ShapeDtypeStruct((t, d), BF16), jax.ShapeDtypeStruct((1, d), F32)],
        compiler_params=_cp(("arbitrary",)))(h, g, tgt)


def _rope_tables(t):
    pos = jnp.arange(t, dtype=F32)
    inv = 1.0 / (ROPE_THETA ** (jnp.arange(0, ROT_DIM, 2, dtype=F32) / ROT_DIM))
    ang = pos[:, None] * inv[None, :]
    cos, sin = jnp.cos(ang), jnp.sin(ang)
    half = ROT_DIM // 2
    rest = HEAD_DIM - ROT_DIM
    c = jnp.concatenate([cos, cos, jnp.ones((t, rest), F32)], axis=1)
    s1 = jnp.concatenate([-sin, jnp.zeros((t, half + rest), F32)], axis=1)
    s2 = jnp.concatenate([jnp.zeros((t, half), F32), sin, jnp.zeros((t, rest), F32)], axis=1)
    return jnp.concatenate([jnp.tile(v, (1, LANES // HEAD_DIM)) for v in (c, s1, s2)], axis=1)


def _split_tables(tab):
    return tab[:, :LANES], tab[:, LANES:2 * LANES], tab[:, 2 * LANES:]


def _rope(x, c, s1, s2):
    half = ROT_DIM // 2
    return x * c + pltpu.roll(x, LANES - half, 1) * s1 + pltpu.roll(x, half, 1) * s2


def _rope_t(g, c, s1, s2):
    half = ROT_DIM // 2
    return g * c + pltpu.roll(g * s1, half, 1) + pltpu.roll(g * s2, LANES - half, 1)


def _band_masks(i, heads):
    n = heads * ATTN_BLOCK
    q = jnp.bitwise_and(_iota((n, ATTN_BLOCK), 0), ATTN_BLOCK - 1)
    j = _iota((n, ATTN_BLOCK), 1)
    upper = j > q
    return upper, upper & (j < jnp.where(i > 0, 0, ATTN_BLOCK))


def _fold_band(full, upper):
    return jnp.where(upper, full[:, :ATTN_BLOCK], full[:, ATTN_BLOCK:])


def _unfold_band(band, upper):
    return jnp.concatenate([jnp.where(upper, band, 0.0), jnp.where(upper, 0.0, band)], axis=1)


def _half_masks():
    lane = _iota((1, LANES), 1)
    return [(lane < HEAD_DIM).astype(F32), (lane >= HEAD_DIM).astype(F32)]


def _stack_heads(blocks, hm, j):
    pieces = []
    for r in range(4):
        qb, half = (4 * j + r) // 2, (4 * j + r) % 2
        piece = blocks[qb] * hm[half]
        if half != j:
            piece = pltpu.roll(piece, HEAD_DIM, 1)
        pieces.append(piece)
    return jnp.concatenate(pieces, axis=0)


def _unstack_heads(stacked, j):
    out = []
    for qb in (2 * j, 2 * j + 1):
        acc = None
        for half in range(2):
            r = 2 * qb + half - 4 * j
            piece = stacked[r * ATTN_BLOCK:(r + 1) * ATTN_BLOCK]
            if half != j:
                piece = pltpu.roll(piece, HEAD_DIM, 1)
            acc = piece if acc is None else acc + piece
        out.append((qb, acc))
    return out


def _sink_column(sink_ref, base):
    return jnp.concatenate([jnp.full((ATTN_BLOCK, 1), sink_ref[base + r], F32) for r in range(4)], axis=0)


def _attn_specs(nb_clamp):
    blk = ATTN_BLOCK
    kb, vb = O_K // LANES, O_V // LANES

    def cur(i):
        return jnp.minimum(i, nb_clamp)

    def prev(i):
        return jnp.maximum(jnp.minimum(i, nb_clamp + 1) - 1, 0)

    q = pl.BlockSpec((blk, 512), lambda p, i: (cur(i), p))
    kc = pl.BlockSpec((blk, LANES), lambda p, i: (cur(i), kb + p))
    kp = pl.BlockSpec((blk, LANES), lambda p, i: (prev(i), kb + p))
    vc = pl.BlockSpec((blk, LANES), lambda p, i: (cur(i), vb + p))
    vp = pl.BlockSpec((blk, LANES), lambda p, i: (prev(i), vb + p))
    tc = pl.BlockSpec((blk, 3 * LANES), lambda p, i: (cur(i), 0))
    tp = pl.BlockSpec((blk, 3 * LANES), lambda p, i: (prev(i), 0))
    return q, kc, kp, vc, vp, tc, tp


def _attn_fwd(proj, sinks, tables):
    t = proj.shape[0]
    nb = t // ATTN_BLOCK
    scale = HEAD_DIM ** -0.5

    def body(sink_ref, q_ref, kc_ref, kp_ref, vc_ref, vp_ref, tc_ref, tp_ref, o_ref):
        p = pl.program_id(0)
        i = pl.program_id(1)
        cc, s1c, s2c = _split_tables(tc_ref[...])
        kband = jnp.concatenate([_rope(kp_ref[...], *_split_tables(tp_ref[...])),
                                 _rope(kc_ref[...], cc, s1c, s2c)], axis=0).astype(BF16)
        vband = jnp.concatenate([vp_ref[...], vc_ref[...]], axis=0)
        hm = _half_masks()
        vsel = [(vband * hm[j]).astype(BF16) for j in range(2)]
        upper, dropped = _band_masks(i, 1)
        qr = [_rope(q_ref[:, qb * LANES:(qb + 1) * LANES], cc, s1c, s2c) for qb in range(4)]

        def scores(hh):
            qb, half, j = hh // 2, hh % 2, hh // 4
            qs = qr[qb] * hm[half]
            if half != j:
                qs = pltpu.roll(qs, HEAD_DIM, 1)
            return _dot(qs, kband, 'nt')

        ahead = scores(0)
        acc = None
        for hh in range(8):
            qb, half, j = hh // 2, hh % 2, hh // 4
            raw = ahead
            if hh + 1 < 8:
                ahead = scores(hh + 1)
            s = jnp.where(dropped, NEG, _fold_band(raw, upper) * scale)
            sink = sink_ref[p * 8 + hh]
            m = jnp.maximum(jnp.max(s, axis=1, keepdims=True), sink)
            pe = jnp.exp(s - m)
            den = jnp.sum(pe, axis=1, keepdims=True) + jnp.exp(sink - m)
            o = _dot(_unfold_band(pe / den, upper), vsel[j])
            if half != j:
                o = pltpu.roll(o, HEAD_DIM, 1)
            acc = o if half == 0 else acc + o
            if half == 1:
                o_ref[:, qb * LANES:(qb + 1) * LANES] = acc

    q, kc, kp, vc, vp, tc, tp = _attn_specs(nb - 1)
    smem = pl.BlockSpec(memory_space=pltpu.SMEM)
    return pl.pallas_call(
        body, name="attn_fwd", grid=(4, nb),
        in_specs=[smem, q, kc, kp, vc, vp, tc, tp],
        out_specs=pl.BlockSpec((ATTN_BLOCK, 512), lambda p, i: (i, p)),
        out_shape=jax.ShapeDtypeStruct((t, ATTN_WIDTH), F32),
        compiler_params=_cp(("parallel", "arbitrary")))(sinks, proj, proj, proj, proj, proj, tables, tables)


def _attn_bwd(proj, sinks, tables, dout):
    t = proj.shape[0]
    nb = t // ATTN_BLOCK
    scale = HEAD_DIM ** -0.5

    def body(sink_ref, q_ref, kc_ref, kp_ref, vc_ref, vp_ref, tc_ref, tp_ref,
             do_ref, dq_ref, dk_ref, dv_ref, ds_ref, carry_k, carry_v):
        p = pl.program_id(0)
        i = pl.program_id(1)
        ptab = _split_tables(tp_ref[...])

        @pl.when(i == 0)
        def _():
            carry_k[...] = jnp.zeros_like(carry_k)
            carry_v[...] = jnp.zeros_like(carry_v)
            ds_ref[...] = jnp.zeros_like(ds_ref)

        @pl.when(i < nb)
        def _():
            cc, s1c, s2c = _split_tables(tc_ref[...])
            kband = jnp.concatenate([_rope(kp_ref[...], *ptab), _rope(kc_ref[...], cc, s1c, s2c)], axis=0)
            vband = jnp.concatenate([vp_ref[...], vc_ref[...]], axis=0)
            hm = _half_masks()
            kband16 = kband.astype(BF16)
            vband16 = vband.astype(BF16)
            upper, dropped = _band_masks(i, 4)
            dkb = jnp.zeros((2 * ATTN_BLOCK, LANES), F32)
            dvb = jnp.zeros((2 * ATTN_BLOCK, LANES), F32)
            row8 = _iota((8, LANES), 0)
            dsink = jnp.zeros((8, LANES), F32)
            qr = [_rope(q_ref[:, qb * LANES:(qb + 1) * LANES], cc, s1c, s2c) for qb in range(4)]
            dob = [do_ref[:, qb * LANES:(qb + 1) * LANES] for qb in range(4)]
            for j in range(2):
                qst = _stack_heads(qr, hm, j).astype(BF16)
                dost = _stack_heads(dob, hm, j).astype(BF16)
                s = jnp.where(dropped, NEG, _fold_band(_dot(qst, kband16, 'nt'), upper) * scale)
                sink = _sink_column(sink_ref, p * 8 + 4 * j)
                m = jnp.maximum(jnp.max(s, axis=1, keepdims=True), sink)
                pe = jnp.exp(s - m)
                psink = jnp.exp(sink - m)
                den = jnp.sum(pe, axis=1, keepdims=True) + psink
                pr = pe / den
                dvb = dvb + _dot(_unfold_band(pr, upper).T, dost)
                dp = _fold_band(_dot(dost, vband16, 'nt'), upper)
                delta = jnp.sum(pr * dp, axis=1, keepdims=True)
                dsc = _unfold_band(pr * (dp - delta) * scale, upper)
                dsk = psink / den * delta
                for r in range(4):
                    part = jnp.sum(dsk[r * ATTN_BLOCK:(r + 1) * ATTN_BLOCK])
                    dsink = dsink + jnp.where(row8 == 4 * j + r, -part, 0.0)
                for qb, dqb in _unstack_heads(_dot(dsc, kband * hm[j]), j):
                    dq_ref[:, qb * LANES:(qb + 1) * LANES] = _rope_t(dqb, cc, s1c, s2c).astype(BF16)
                dkb = dkb + _dot(dsc.T, qst)
            ds_ref[0] += dsink
            dk_ref[...] = _rope_t(carry_k[...] + dkb[:ATTN_BLOCK], *ptab).astype(BF16)
            dv_ref[...] = (carry_v[...] + dvb[:ATTN_BLOCK]).astype(BF16)
            carry_k[...] = dkb[ATTN_BLOCK:]
            carry_v[...] = dvb[ATTN_BLOCK:]

        @pl.when(i == nb)
        def _():
            dk_ref[...] = _rope_t(carry_k[...], *ptab).astype(BF16)
            dv_ref[...] = carry_v[...].astype(BF16)

    q, kc, kp, vc, vp, tc, tp = _attn_specs(nb - 1)
    smem = pl.BlockSpec(memory_space=pltpu.SMEM)
    qblk = pl.BlockSpec((ATTN_BLOCK, 512), lambda p, i: (jnp.minimum(i, nb - 1), p))
    kvout = pl.BlockSpec((ATTN_BLOCK, LANES), lambda p, i: (jnp.maximum(i - 1, 0), p))
    return pl.pallas_call(
        body, name="attn_bwd", grid=(4, nb + 1),
        in_specs=[smem, q, kc, kp, vc, vp, tc, tp, qblk],
        out_specs=[qblk, kvout, kvout, pl.BlockSpec((1, 8, LANES), lambda p, i: (p, 0, 0))],
        out_shape=[jax.ShapeDtypeStruct((t, MAIN_WIDTH), BF16), jax.ShapeDtypeStruct((t, KV_WIDTH), BF16),
                   jax.ShapeDtypeStruct((t, KV_WIDTH), BF16), jax.ShapeDtypeStruct((4, 8, LANES), F32)],
        scratch_shapes=[pltpu.VMEM((ATTN_BLOCK, LANES), F32), pltpu.VMEM((ATTN_BLOCK, LANES), F32)],
        compiler_params=_cp(("parallel", "arbitrary")))(sinks, proj, proj, proj, proj, proj, tables, tables, dout)


def _shift_rows(x, prev8, j):
    n, c = x.shape
    r = pltpu.roll(x.reshape(n // 8, 8, c), j, 1)
    before = pltpu.roll(prev8, j, 0)[None]
    if n > 8:
        before = jnp.concatenate([before, r[:-1]], axis=0)
    return jnp.where(_iota((1, 8, 1), 1) < j, before, r).reshape(n, c)


def _shift_rows_up(x, next8, j):
    n, c = x.shape
    r = pltpu.roll(x.reshape(n // 8, 8, c), 8 - j, 1)
    after = pltpu.roll(next8, 8 - j, 0)[None]
    if n > 8:
        after = jnp.concatenate([r[1:], after], axis=0)
    return jnp.where(_iota((1, 8, 1), 1) >= 8 - j, after, r).reshape(n, c)


def _conv_apply(x, prev8, w, b, taps):
    u = b + x * w[taps - 1:taps]
    for j in range(1, taps):
        u = u + _shift_rows(x, prev8, j) * w[taps - 1 - j:taps - j]
    return u


def _conv_grads(du, du_next8, x, w, taps):
    dx = du * w[taps - 1:taps]
    rowk = _iota((taps, 1), 0)
    dw = jnp.where(rowk == taps - 1, jnp.sum(du * x, axis=0, keepdims=True), 0.0)
    for j in range(1, taps):
        ahead = _shift_rows_up(du, du_next8, j)
        dx = dx + ahead * w[taps - 1 - j:taps - j]
        dw = dw + jnp.where(rowk == taps - 1 - j, jnp.sum(ahead * x, axis=0, keepdims=True), 0.0)
    return dx, dw, jnp.sum(du, axis=0, keepdims=True)


def _conv_specs(tb, tc, col0, t):
    c0 = col0 // tc
    cur = pl.BlockSpec((tb, tc), lambda j, i: (i, c0 + j))
    prev = pl.BlockSpec((8, tc), lambda j, i: (jnp.maximum(i * (tb // 8) - 1, 0), c0 + j))
    nxt = pl.BlockSpec((8, tc), lambda j, i: (jnp.minimum((i + 1) * (tb // 8), t // 8 - 1), c0 + j))
    return cur, prev, nxt


def _conv_silu_fwd(x, w, b, *, col0, width, name):
    t = x.shape[0]
    taps = w.shape[0]
    tb, tc = _rows(t, 512), _tile(width, 1024)
    assert col0 % tc == 0

    def body(x_ref, xp_ref, w_ref, b_ref, o_ref, u_ref):
        i = pl.program_id(1)
        prev8 = jnp.where(i > 0, xp_ref[...], 0.0)
        u = _conv_apply(x_ref[...], prev8, w_ref[...], b_ref[...], taps)
        u_ref[...] = u
        o_ref[...] = u * _sigmoid(u)

    cur, prev, _ = _conv_specs(tb, tc, col0, t)
    par = pl.BlockSpec((taps, tc), lambda j, i: (0, j))
    bias = pl.BlockSpec((1, tc), lambda j, i: (0, j))
    out = pl.BlockSpec((tb, tc), lambda j, i: (i, j))
    shp = jax.ShapeDtypeStruct((t, width), F32)
    return pl.pallas_call(
        body, name=name, grid=(width // tc, t // tb), in_specs=[cur, prev, par, bias], out_specs=[out, out],
        out_shape=[shp, shp], compiler_params=_cp(("parallel", "parallel")))(x, x, w, b)


def _dsilu(u):
    sg = _sigmoid(u)
    return sg * (1.0 + u * (1.0 - sg))


def _ssd_conv_bwd(x, w, dxs, dbm, dcm, base, *, col0, name):
    t = x.shape[0]
    taps = w.shape[0]
    tb, tc = _rows(t, 512), BC_WIDTH
    nrow, ncol = t // tb, CONV_CH // tc
    c0 = col0 // tc

    def body(x_ref, w_ref, xs_ref, xsn_ref, bm_ref, bmn_ref, cm_ref, cmn_ref, base_ref, dx_ref, dw_ref, db_ref):
        i = pl.program_id(0)
        j = pl.program_id(1)

        def run(du_ref, dun_ref):
            next8 = jnp.where(i < nrow - 1, dun_ref[...], 0.0)
            dx, dwv, dbv = _conv_grads(du_ref[...], next8, x_ref[...], w_ref[...], taps)
            dx_ref[...] = dx.astype(BF16)

            @pl.when(i == 0)
            def _():
                dw_ref[j] = dwv
                db_ref[j] = dbv

            @pl.when(i > 0)
            def _():
                dw_ref[j] += dwv
                db_ref[j] += dbv

        pl.when(j < 2)(lambda: run(xs_ref, xsn_ref))
        pl.when(j == 2)(lambda: run(bm_ref, bmn_ref))
        pl.when(j == 3)(lambda: run(cm_ref, cmn_ref))

    def nxt_row(i):
        return jnp.minimum((i + 1) * (tb // 8), t // 8 - 1)

    xs_col = lambda j: jnp.minimum(j, SSD_INNER // tc - 1)
    in_specs = [pl.BlockSpec((tb, tc), lambda i, j: (i, c0 + j)), pl.BlockSpec((taps, tc), lambda i, j: (0, j)),
                pl.BlockSpec((tb, tc), lambda i, j: (i, xs_col(j))),
                pl.BlockSpec((8, tc), lambda i, j: (nxt_row(i), xs_col(j))),
                pl.BlockSpec((tb, tc), lambda i, j: (i, 0)), pl.BlockSpec((8, tc), lambda i, j: (nxt_row(i), 0)),
                pl.BlockSpec((tb, tc), lambda i, j: (i, 0)), pl.BlockSpec((8, tc), lambda i, j: (nxt_row(i), 0)),
                pl.BlockSpec(memory_space=pl.ANY)]
    dx, dw, db = pl.pallas_call(
        body, name=name, grid=(nrow, ncol), in_specs=in_specs, input_output_aliases={8: 0},
        out_specs=[pl.BlockSpec((tb, tc), lambda i, j: (i, c0 + j)),
                   pl.BlockSpec((ncol, taps, tc), lambda i, j: (0, 0, 0)),
                   pl.BlockSpec((ncol, 1, tc), lambda i, j: (0, 0, 0))],
        out_shape=[jax.ShapeDtypeStruct((t, MAIN_WIDTH), BF16), jax.ShapeDtypeStruct((ncol, taps, tc), F32),
                   jax.ShapeDtypeStruct((ncol, 1, tc), F32)],
        compiler_params=_cp(("arbitrary", "arbitrary")))(x, w, dxs, dxs, dbm, dbm, dcm, dcm, base)
    return dx, dw.transpose(1, 0, 2).reshape(taps, CONV_CH), db.transpose(1, 0, 2).reshape(1, CONV_CH)


def _ffn_specs(tb, tc, t):
    nc = D_FF // tc

    def cur(half):
        return pl.BlockSpec((tb, tc), lambda j, i: (i, half * nc + j))

    def prev(half):
        return pl.BlockSpec((8, tc), lambda j, i: (jnp.maximum(i * (tb // 8) - 1, 0), half * nc + j))

    def nxt(half):
        return pl.BlockSpec((8, tc), lambda j, i: (jnp.minimum((i + 1) * (tb // 8), t // 8 - 1), half * nc + j))

    def par(rows, half):
        return pl.BlockSpec((rows, tc), lambda j, i: (0, half * nc + j))

    return cur, prev, nxt, par


def _ffn_act_fwd(u0, w, b):
    t = u0.shape[0]
    tb, tc = _rows(t, 512), _tile(D_FF, 1408)
    cur, prev, _, par = _ffn_specs(tb, tc, t)

    def body(g_ref, gp_ref, v_ref, vp_ref, wg_ref, wv_ref, bg_ref, bv_ref, o_ref, u_ref):
        i = pl.program_id(1)
        ug = _conv_apply(g_ref[...], jnp.where(i > 0, gp_ref[...], 0.0), wg_ref[...], bg_ref[...], FFN_CONV)
        uv = _conv_apply(v_ref[...], jnp.where(i > 0, vp_ref[...], 0.0), wv_ref[...], bv_ref[...], FFN_CONV)
        o_ref[...] = (ug * _sigmoid(ug) * uv).astype(BF16)
        u_ref[0] = ug
        u_ref[1] = uv

    return pl.pallas_call(
        body, name="ffn_act_fwd", grid=(D_FF // tc, t // tb),
        in_specs=[cur(0), prev(0), cur(1), prev(1), par(FFN_CONV, 0), par(FFN_CONV, 1), par(1, 0), par(1, 1)],
        out_specs=[pl.BlockSpec((tb, tc), lambda j, i: (i, j)), pl.BlockSpec((2, tb, tc), lambda j, i: (0, i, j))],
        out_shape=[jax.ShapeDtypeStruct((t, D_FF), BF16), jax.ShapeDtypeStruct((2, t, D_FF), F32)],
        compiler_params=_cp(("parallel", "parallel")))(u0, u0, u0, u0, w, w, b, b)


def _ffn_act_bwd(u0, u, w, da):
    t = u0.shape[0]
    tb, tc = _rows(t, 256), _tile(D_FF, 1408)
    nrow = t // tb
    taps = FFN_CONV
    cur, _, _, par = _ffn_specs(tb, tc, t)

    def dact(ug, uv, dav):
        sg = _sigmoid(ug)
        return dav * uv * (sg * (1.0 + ug * (1.0 - sg))), dav * ug * sg

    def body(g_ref, v_ref, u_ref, un_ref, wg_ref, wv_ref, da_ref, dan_ref, dx_ref, dw_ref, db_ref):
        i = pl.program_id(1)
        dug, duv = dact(u_ref[0], u_ref[1], da_ref[...].astype(F32))
        dan = jnp.where(i < nrow - 1, dan_ref[...].astype(F32)[:8], 0.0)
        dugn, duvn = dact(un_ref[0], un_ref[1], dan)
        dxg, dwg, dbg = _conv_grads(dug, dugn, g_ref[...], wg_ref[...], taps)
        dxv, dwv, dbv = _conv_grads(duv, duvn, v_ref[...], wv_ref[...], taps)
        dx_ref[0] = dxg.astype(BF16)
        dx_ref[1] = dxv.astype(BF16)

        @pl.when(i == 0)
        def _():
            dw_ref[0] = dwg
            dw_ref[1] = dwv
            db_ref[0] = dbg
            db_ref[1] = dbv

        @pl.when(i > 0)
        def _():
            dw_ref[0] += dwg
            dw_ref[1] += dwv
            db_ref[0] += dbg
            db_ref[1] += dbv

    both = pl.BlockSpec((2, tb, tc), lambda j, i: (0, i, j))
    both_nxt = pl.BlockSpec((2, 8, tc), lambda j, i: (0, jnp.minimum((i + 1) * (tb // 8), t // 8 - 1), j))
    da_cur = pl.BlockSpec((tb, tc), lambda j, i: (i, j))
    da_nxt = pl.BlockSpec((16, tc), lambda j, i: (jnp.minimum((i + 1) * (tb // 16), t // 16 - 1), j))
    return pl.pallas_call(
        body, name="ffn_act_bwd", grid=(D_FF // tc, nrow),
        in_specs=[cur(0), cur(1), both, both_nxt, par(taps, 0), par(taps, 1), da_cur, da_nxt],
        out_specs=[both, pl.BlockSpec((2, taps, tc), lambda j, i: (0, 0, j)),
                   pl.BlockSpec((2, 1, tc), lambda j, i: (0, 0, j))],
        out_shape=[jax.ShapeDtypeStruct((2, t, D_FF), BF16), jax.ShapeDtypeStruct((2, taps, D_FF), F32),
                   jax.ShapeDtypeStruct((2, 1, D_FF), F32)],
        compiler_params=_cp(("parallel", "arbitrary")))(u0, u0, u, u, w, w, da, da)


def _head_masks():
    lane = _iota((1, 4 * SSD_HEAD_DIM), 1)
    return [((lane >= r * SSD_HEAD_DIM) & (lane < (r + 1) * SSD_HEAD_DIM)).astype(F32) for r in range(4)]


def _segsum(v):
    first = _iota((1, LANES), 1) < SSD_HEAD_DIM
    halves = []
    for k in range(2):
        vh = v[:, k * LANES:(k + 1) * LANES]
        both = jnp.sum(vh, axis=1, keepdims=True)
        one = jnp.sum(jnp.where(first, vh, 0.0), axis=1, keepdims=True)
        halves.append(jnp.where(first, one, both - one))
    return jnp.concatenate(halves, axis=1)


def _ssd_common(raw_e, prow, rawr4, bcol, acol):
    n = SSD_CHUNK
    dt_e = _softplus(raw_e + prow[0:1, :])
    a_e = -jnp.exp(prow[1:2, :])
    d_e = prow[2:3, :]
    tril = (_iota((n, n), 0) >= _iota((n, n), 1)).astype(F32)
    acs_e = _dot_exact(tril, dt_e * a_e)
    last_e = acs_e[n - 1:n, :]
    dtr4 = _softplus(rawr4 + bcol)
    triu = (_iota((n, n), 0) <= _iota((n, n), 1)).astype(F32)
    acs_r4 = _dot_exact(dtr4 * (-jnp.exp(acol)), triu)
    return dt_e, a_e, d_e, acs_e, last_e, acs_r4


def _decay_matrix(acs_e, acs_r4, r):
    n = SSD_CHUNK
    col = acs_e[:, r * SSD_HEAD_DIM:r * SSD_HEAD_DIM + 1]
    seg = col - acs_r4[r:r + 1, :]
    causal = _iota((n, n), 0) >= _iota((n, n), 1)
    return jnp.exp(jnp.where(causal, seg, NEG))


SSD_STEP_CHUNKS = 4
SSD_ROWS = SSD_STEP_CHUNKS * SSD_CHUNK


def _ssd_specs(t, rev):
    nb = t // SSD_ROWS
    xb, bb, cb = 0, SSD_INNER // SSD_STATE, (SSD_INNER + BC_WIDTH) // SSD_STATE

    def ch(c):
        return (nb - 1 - c) if rev else c

    x = pl.BlockSpec((SSD_ROWS, 256), lambda g, c: (ch(c), xb + g))
    bm = pl.BlockSpec((SSD_ROWS, SSD_STATE), lambda g, c: (ch(c), bb + g))
    cm = pl.BlockSpec((SSD_ROWS, SSD_STATE), lambda g, c: (ch(c), cb + g))
    dtc = pl.BlockSpec((1, SSD_ROWS, 256), lambda g, c: (g, ch(c), 0))
    dtr = pl.BlockSpec((1, 4, SSD_ROWS), lambda g, c: (g, 0, ch(c)))
    prow = pl.BlockSpec((1, 3, 256), lambda g, c: (g, 0, 0))
    pcol = pl.BlockSpec((1, 4, 1), lambda g, c: (g, 0, 0))
    st = pl.BlockSpec((1, SSD_STEP_CHUNKS, SSD_STATE, 256), lambda g, c: (g, ch(c), 0, 0))
    return x, bm, cm, dtc, dtr, prow, pcol, st, ch


def _ssd_params(dt_raw, dt_bias, a_log, ssd_d):
    t = dt_raw.shape[0]
    by_group = dt_raw.reshape(t, SSD_GROUPS, 4)
    dtc = jnp.repeat(by_group, SSD_HEAD_DIM, axis=2).transpose(1, 0, 2)
    dtr = by_group.transpose(1, 2, 0)
    prow = jnp.repeat(jnp.stack([dt_bias.reshape(SSD_GROUPS, 4), a_log.reshape(SSD_GROUPS, 4),
                                 ssd_d.reshape(SSD_GROUPS, 4)], axis=1), SSD_HEAD_DIM, axis=2)
    bcol = dt_bias.reshape(SSD_GROUPS, 4, 1)
    acol = a_log.reshape(SSD_GROUPS, 4, 1)
    return dtc, dtr, prow, bcol, acol


def _ssd_fwd(xbc, params):
    t = xbc.shape[0]
    nc = t // SSD_CHUNK
    dtc, dtr, prow, bcol, acol = params

    def body(x_ref, b_ref, c_ref, dtc_ref, dtr_ref, prow_ref, bcol_ref, acol_ref, y_ref, st_ref, s_scr):
        c = pl.program_id(1)

        @pl.when(c == 0)
        def _():
            s_scr[...] = jnp.zeros_like(s_scr)

        masks = _head_masks()
        s = s_scr[...]
        for k in range(SSD_STEP_CHUNKS):
            rows = slice(k * SSD_CHUNK, (k + 1) * SSD_CHUNK)
            dt_e, a_e, d_e, acs_e, last_e, acs_r4 = _ssd_common(
                dtc_ref[0, rows], prow_ref[0], dtr_ref[0][:, rows], bcol_ref[0], acol_ref[0])
            xv = x_ref[rows]
            bm, cm = b_ref[rows], c_ref[rows]
            st_ref[0, k] = s
            xdt = xv * dt_e
            cb = _dot(cm, bm, 'nt')
            y = _dot(cm, s) * jnp.exp(acs_e) + xv * d_e
            for r in range(4):
                mr = cb * _decay_matrix(acs_e, acs_r4, r)
                y = y + _dot(mr, xdt * masks[r])
            y_ref[rows] = y
            w = xdt * jnp.exp(last_e - acs_e)
            s = s * jnp.exp(last_e) + _dot(bm.T, w)
        s_scr[...] = s

    x, bm, cm, dtcs, dtrs, prs, pcs, st, _ = _ssd_specs(t, False)
    return pl.pallas_call(
        body, name="ssd_fwd", grid=(SSD_GROUPS, t // SSD_ROWS), in_specs=[x, bm, cm, dtcs, dtrs, prs, pcs, pcs],
        out_specs=[pl.BlockSpec((SSD_ROWS, 256), lambda g, c: (c, g)), st],
        out_shape=[jax.ShapeDtypeStruct((t, SSD_INNER), F32),
                   jax.ShapeDtypeStruct((SSD_GROUPS, nc, SSD_STATE, 256), F32)],
        scratch_shapes=[pltpu.VMEM((SSD_STATE, 256), F32)],
        compiler_params=_cp(("parallel", "arbitrary")))(xbc, xbc, xbc, dtc, dtr, prow, bcol, acol)


def _ssd_bwd(xbc, pre, params, states, dy):
    t = xbc.shape[0]
    nc = t // SSD_CHUNK
    n = SSD_CHUNK
    dtc, dtr, prow, bcol, acol = params

    def body(x_ref, b_ref, c_ref, ux_ref, ub_ref, uc_ref, dtc_ref, dtr_ref, prow_ref, bcol_ref, acol_ref, st_ref,
             dy_ref, dx_ref, db_ref, dc_ref, ddt_ref, dp_ref, ds_scr):
        c = pl.program_id(1)

        @pl.when(c == 0)
        def _():
            ds_scr[...] = jnp.zeros_like(ds_scr)
            dp_ref[...] = jnp.zeros_like(dp_ref)

        masks = _head_masks()
        ds = ds_scr[...]
        for k in reversed(range(SSD_STEP_CHUNKS)):
            rows = slice(k * SSD_CHUNK, (k + 1) * SSD_CHUNK)
            raw_e = dtc_ref[0, rows]
            prw = prow_ref[0]
            dt_e, a_e, d_e, acs_e, last_e, acs_r4 = _ssd_common(raw_e, prw, dtr_ref[0][:, rows], bcol_ref[0], acol_ref[0])
            xv = x_ref[rows]
            bm, cm = b_ref[rows], c_ref[rows]
            s = st_ref[0, k]
            dyv = dy_ref[rows]
            e_e = jnp.exp(acs_e)
            dec_e = jnp.exp(last_e - acs_e)
            cd_e = jnp.exp(last_e)
            xdt = xv * dt_e
            w = xdt * dec_e
            b16, c16, s16, ds16 = bm.astype(BF16), cm.astype(BF16), s.astype(BF16), ds.astype(BF16)
            cb = _dot(c16, b16, 'nt')
            yoff_raw = _dot(c16, s16)
            dye = dyv * e_e
            dye16 = dye.astype(BF16)
            dcm = _dot(dye16, s16, 'nt')
            ds_prev = ds * cd_e + _dot(cm.T, dye16)
            dacs_e = _segsum(dyv * yoff_raw) * e_e
            dw = _dot(b16, ds16)
            dbm = _dot(w, ds16, 'nt')
            tdec = _segsum(dw * xdt) * dec_e
            dacs_e = dacs_e - tdec
            dlast_e = jnp.sum(tdec, axis=0, keepdims=True)
            dxdt = dw * dec_e
            dlast_e = dlast_e + _segsum(jnp.sum(ds * s, axis=0, keepdims=True)) * cd_e
            dcb = jnp.zeros((n, n), F32)
            for r in range(4):
                lm = _decay_matrix(acs_e, acs_r4, r)
                mr = cb * lm
                dyr16 = (dyv * masks[r]).astype(BF16)
                dm = _dot(dyr16, xdt * masks[r], 'nt')
                dcb = dcb + dm * lm
                dseg = dm * mr
                dcol = jnp.sum(dseg, axis=1, keepdims=True) - jnp.sum(dseg.T, axis=1, keepdims=True)
                dacs_e = dacs_e + dcol * masks[r]
                dxdt = dxdt + _dot(mr.T, dyr16)
            dcm = dcm + _dot(dcb, b16)
            dbm = dbm + _dot(dcb.T, c16)
            dacs_e = dacs_e + jnp.where(_iota((n, 1), 0) == n - 1, dlast_e, 0.0)
            triu = (_iota((n, n), 0) <= _iota((n, n), 1)).astype(F32)
            ddta_e = _dot_exact(triu, dacs_e)
            ddt_e = ddta_e * a_e + _segsum(dxdt * xv)
            dx_ref[rows] = (dxdt * dt_e + dyv * d_e) * _dsilu(ux_ref[rows])
            db_ref[rows] = dbm * _dsilu(ub_ref[rows])
            dc_ref[rows] = dcm * _dsilu(uc_ref[rows])
            draw_e = ddt_e * _sigmoid(raw_e + prw[0:1, :])
            draw_t = draw_e.T
            ddt_ref[0, :, rows] = jnp.concatenate([draw_t[r * SSD_HEAD_DIM:r * SSD_HEAD_DIM + 1] for r in range(4)], axis=0)
            dbias = jnp.sum(draw_e, axis=0, keepdims=True)
            dalog = jnp.sum(ddta_e * dt_e, axis=0, keepdims=True) * a_e
            dd = _segsum(jnp.sum(dyv * xv, axis=0, keepdims=True))
            row3 = _iota((3, 1), 0)
            dp_ref[0] += (jnp.where(row3 == 0, dbias, 0.0) + jnp.where(row3 == 1, dalog, 0.0)
                          + jnp.where(row3 == 2, dd, 0.0))
            ds = ds_prev
        ds_scr[...] = ds


    x, bm, cm, dtcs, dtrs, prs, pcs, st, ch = _ssd_specs(t, True)
    yblk = pl.BlockSpec((SSD_ROWS, 256), lambda g, c: (ch(c), g))
    nblk = pl.BlockSpec((SSD_ROWS, SSD_STATE), lambda g, c: (ch(c), g))
    return pl.pallas_call(
        body, name="ssd_bwd", grid=(SSD_GROUPS, t // SSD_ROWS),
        in_specs=[x, bm, cm, x, bm, cm, dtcs, dtrs, prs, pcs, pcs, st, yblk],
        out_specs=[yblk, nblk, nblk, dtrs, prs],
        out_shape=[jax.ShapeDtypeStruct((t, SSD_INNER), F32), jax.ShapeDtypeStruct((t, BC_WIDTH), F32),
                   jax.ShapeDtypeStruct((t, BC_WIDTH), F32), jax.ShapeDtypeStruct((SSD_GROUPS, 4, t), F32),
                   jax.ShapeDtypeStruct((SSD_GROUPS, 3, 256), F32)],
        scratch_shapes=[pltpu.VMEM((SSD_STATE, 256), F32)],
        compiler_params=_cp(("parallel", "arbitrary")))(xbc, xbc, xbc, pre, pre, pre, dtc, dtr, prow, bcol, acol,
                                                         states, dy)


GROUP_W = SSD_INNER // SSD_GROUPS


def _mix_specs(tb):
    row = pl.BlockSpec((tb, 2048), lambda i: (i, 0))
    zlo = pl.BlockSpec((tb, 1024), lambda i: (i, O_Z // 1024))
    zhi = pl.BlockSpec((tb, 1024), lambda i: (i, O_Z // 1024 + 1))
    vec = pl.BlockSpec((1, 2048), lambda i: (0, 0))
    return row, zlo, zhi, vec


def _mix_fwd(attn, y, proj, g_attn, g_ssd):
    t = attn.shape[0]
    tb = _rows(t, 256)

    def body(a_ref, y_ref, zlo_ref, zhi_ref, ga_ref, gs_ref, o_ref):
        av = a_ref[...]
        r = lax.rsqrt(jnp.mean(av * av, axis=-1, keepdims=True) + EPS)
        o_ref[:, :ATTN_WIDTH] = (av * r * ga_ref[...]).astype(BF16)
        for g in range(SSD_GROUPS):
            lo, hi = g * GROUP_W, (g + 1) * GROUP_W
            zref = zlo_ref if g < 4 else zhi_ref
            z = zref[:, lo % 1024:lo % 1024 + GROUP_W]
            yg = y_ref[:, lo:hi] * (z * _sigmoid(z))
            rg = lax.rsqrt(jnp.mean(yg * yg, axis=-1, keepdims=True) + EPS)
            o_ref[:, ATTN_WIDTH + lo:ATTN_WIDTH + hi] = (yg * rg * gs_ref[:, lo:hi]).astype(BF16)

    row, zlo, zhi, vec = _mix_specs(tb)
    return pl.pallas_call(
        body, name="mix_fwd", grid=(t // tb,), in_specs=[row, row, zlo, zhi, vec, vec],
        out_specs=pl.BlockSpec((tb, 4096), lambda i: (i, 0)), out_shape=jax.ShapeDtypeStruct((t, 4096), BF16),
        compiler_params=_cp(("parallel",)))(attn, y, proj, proj, g_attn, g_ssd)


def _mix_bwd(dmix, attn, y, proj, g_attn, g_ssd):
    t = attn.shape[0]
    tb = _rows(t, 256)

    def body(dm_ref, a_ref, y_ref, zlo_ref, zhi_ref, ga_ref, gs_ref, da_ref, dy_ref, dz_ref, dga_ref, dgs_ref):
        i = pl.program_id(0)
        av = a_ref[...]
        dn = dm_ref[:, :ATTN_WIDTH].astype(F32)
        r = lax.rsqrt(jnp.mean(av * av, axis=-1, keepdims=True) + EPS)
        u = dn * ga_ref[...]
        da_ref[...] = r * u - av * (r * r * r * jnp.mean(u * av, axis=-1, keepdims=True))
        dga = jnp.sum(dn * av * r, axis=0, keepdims=True)

        @pl.when(i == 0)
        def _():
            dga_ref[...] = dga

        @pl.when(i > 0)
        def _():
            dga_ref[...] += dga

        for g in range(SSD_GROUPS):
            lo, hi = g * GROUP_W, (g + 1) * GROUP_W
            zref = zlo_ref if g < 4 else zhi_ref
            z = zref[:, lo % 1024:lo % 1024 + GROUP_W]
            yv = y_ref[:, lo:hi]
            sg = _sigmoid(z)
            sz = z * sg
            yg = yv * sz
            rg = lax.rsqrt(jnp.mean(yg * yg, axis=-1, keepdims=True) + EPS)
            do = dm_ref[:, ATTN_WIDTH + lo:ATTN_WIDTH + hi].astype(F32)
            ug = do * gs_ref[:, lo:hi]
            dyg = rg * ug - yg * (rg * rg * rg * jnp.mean(ug * yg, axis=-1, keepdims=True))
            dy_ref[:, lo:hi] = dyg * sz
            dz_ref[:, lo:hi] = (dyg * yv * (sg * (1.0 + z * (1.0 - sg)))).astype(BF16)
            dgs = jnp.sum(do * yg * rg, axis=0, keepdims=True)

            @pl.when(i == 0)
            def _():
                dgs_ref[:, lo:hi] = dgs

            @pl.when(i > 0)
            def _():
                dgs_ref[:, lo:hi] += dgs

    row, zlo, zhi, vec = _mix_specs(tb)
    return pl.pallas_call(
        body, name="mix_bwd", grid=(t // tb,),
        in_specs=[pl.BlockSpec((tb, 4096), lambda i: (i, 0)), row, row, zlo, zhi, vec, vec],
        out_specs=[row, row, row, vec, vec],
        out_shape=[jax.ShapeDtypeStruct((t, 2048), F32), jax.ShapeDtypeStruct((t, 2048), F32),
                   jax.ShapeDtypeStruct((t, 2048), BF16), jax.ShapeDtypeStruct((1, 2048), F32),
                   jax.ShapeDtypeStruct((1, 2048), F32)],
        compiler_params=_cp(("arbitrary",)))(dmix, attn, y, proj, proj, g_attn, g_ssd)


def _adamw(w, g, m, v, name):
    r, c = w.shape
    tb = _rows(r, 256)
    c1 = 1.0 - ADAM_B1 ** ADAM_STEP
    c2 = 1.0 - ADAM_B2 ** ADAM_STEP

    def body(w_ref, g_ref, m_ref, v_ref, d_ref, m2_ref, v2_ref):
        gv = g_ref[...]
        m2 = ADAM_B1 * m_ref[...] + (1.0 - ADAM_B1) * gv
        v2 = ADAM_B2 * v_ref[...] + (1.0 - ADAM_B2) * (gv * gv)
        d_ref[...] = -ADAM_LR * ((m2 / c1) / (jnp.sqrt(v2 / c2) + ADAM_EPS) + ADAM_WD * w_ref[...])
        m2_ref[...] = m2
        v2_ref[...] = v2

    blk = pl.BlockSpec((tb, c), lambda i: (i, 0))
    shp = jax.ShapeDtypeStruct((r, c), F32)
    return pl.pallas_call(body, name=name, grid=(r // tb,), in_specs=[blk] * 4, out_specs=[blk] * 3,
                          out_shape=[shp] * 3, compiler_params=_cp(("parallel",)))(w, g, m, v)


def _adamw_halves(w, mine, theirs, m, v, pos, name, cols=False):
    r, c = w.shape
    h = r if cols else r // 2
    tb = _rows(h, 128)
    nh = h // tb
    c1 = 1.0 - ADAM_B1 ** ADAM_STEP
    c2 = 1.0 - ADAM_B2 ** ADAM_STEP

    def body(pos_ref, w_ref, a_ref, b_ref, m_ref, v_ref, g_ref, d_ref, m2_ref, v2_ref):
        which = pl.program_id(1) if cols else pl.program_id(0) // nh
        gv = jnp.where(which == pos_ref[0], a_ref[...], b_ref[...])
        m2 = ADAM_B1 * m_ref[...] + (1.0 - ADAM_B1) * gv
        v2 = ADAM_B2 * v_ref[...] + (1.0 - ADAM_B2) * (gv * gv)
        g_ref[...] = gv
        d_ref[...] = -ADAM_LR * ((m2 / c1) / (jnp.sqrt(v2 / c2) + ADAM_EPS) + ADAM_WD * w_ref[...])
        m2_ref[...] = m2
        v2_ref[...] = v2

    if cols:
        full = pl.BlockSpec((tb, c // 2), lambda i, j, pref: (i, j))
        mine_spec = theirs_spec = pl.BlockSpec((tb, c // 2), lambda i, j, pref: (i, 0))
        grid = (nh, 2)
    else:
        full = pl.BlockSpec((tb, c), lambda i, pref: (i, 0))
        mine_spec = pl.BlockSpec((tb, c), lambda i, pref: (jnp.where(i // nh == pref[0], i % nh,
                                                                     jnp.where(pref[0] == 0, nh - 1, 0)), 0))
        theirs_spec = pl.BlockSpec((tb, c), lambda i, pref: (jnp.where(i // nh != pref[0], i % nh,
                                                                       jnp.where(pref[0] == 0, 0, nh - 1)), 0))
        grid = (r // tb,)
    shp = jax.ShapeDtypeStruct((r, c), F32)
    grid_spec = pltpu.PrefetchScalarGridSpec(num_scalar_prefetch=1, grid=grid,
                                             in_specs=[full, mine_spec, theirs_spec, full, full],
                                             out_specs=[full] * 4)
    return pl.pallas_call(body, name=name, grid_spec=grid_spec, out_shape=[shp] * 4,
                          compiler_params=_cp(("parallel",) * len(grid)))(pos, w, mine, theirs, m, v)


def _sum_own_half(g4, recv, pos, name, cols=False):
    _, r, c = g4.shape
    h, c = (r, c // 2) if cols else (r // 2, c)
    tb = _rows(h, 128)
    nh = h // tb

    def slot(j, pref):
        return (pref[1] + 1 + j) % N_CHIPS

    if cols:
        own = lambda j, i, pref: (slot(j, pref), i, pref[0])
    else:
        own = lambda j, i, pref: (slot(j, pref), pref[0] * nh + i, 0)
    same = lambda j, i, pref: (slot(j, pref), i, 0)

    def body(pos_ref, a_ref, b_ref, o_ref):
        o_ref[...] = (a_ref[...] + b_ref[...]).astype(BF16)

    grid_spec = pltpu.PrefetchScalarGridSpec(
        num_scalar_prefetch=1, grid=(N_CHIPS - 1, nh),
        in_specs=[pl.BlockSpec((1, tb, c), own), pl.BlockSpec((1, tb, c), same)],
        out_specs=pl.BlockSpec((1, tb, c), same))
    return pl.pallas_call(body, name=name, grid_spec=grid_spec,
                          out_shape=jax.ShapeDtypeStruct((N_CHIPS, h, c), BF16),
                          compiler_params=_cp(("parallel", "parallel")))(pos, g4, recv)


def _sum_chips(g4, recv, parts, pos, name, cols=False):
    _, r, c = g4.shape
    h, c = (r, c // 2) if cols else (r // 2, c)
    tb = _rows(h, 128)
    nh = h // tb
    own = (lambda i, pref: (pref[1], i, pref[0])) if cols else (lambda i, pref: (pref[1], pref[0] * nh + i, 0))

    def body(pos_ref, a_ref, b_ref, p_ref, o_ref):
        own = a_ref[0] + b_ref[0]
        o_ref[...] = ((own + p_ref[0].astype(F32)) + p_ref[1].astype(F32)) + p_ref[2].astype(F32)

    grid_spec = pltpu.PrefetchScalarGridSpec(
        num_scalar_prefetch=1, grid=(nh,),
        in_specs=[pl.BlockSpec((1, tb, c), own),
                  pl.BlockSpec((1, tb, c), lambda i, pref: (pref[1], i, 0)),
                  pl.BlockSpec((3, tb, c), lambda i, pref: (0, i, 0))],
        out_specs=pl.BlockSpec((tb, c), lambda i, pref: (i, 0)))
    return pl.pallas_call(body, name=name, grid_spec=grid_spec, out_shape=jax.ShapeDtypeStruct((h, c), F32),
                          compiler_params=_cp(("parallel",)))(pos, g4, recv, parts)


def _me():
    return lax.axis_index("x"), lax.axis_index("y"), lax.axis_index("c")


def _flip(v, bit):
    return (1 - v) if bit else v


CHIP_FLIPS = [(1, 0), (0, 1), (1, 1)]


def _forward_halves(gathered):
    def body(g_ref, o_ref, token, send_sems, recv_sems):
        x, y, c = _me()
        h = g_ref.shape[2] // 2
        cps = []
        for k, (fx, fy) in enumerate(CHIP_FLIPS):
            peer_chip = 2 * _flip(x, fx) + _flip(y, fy)
            mine = o_ref.at[peer_chip, :, pl.ds(c * h, h)]
            cp = pltpu.make_async_remote_copy(src_ref=mine, dst_ref=mine, send_sem=send_sems.at[k],
                                              recv_sem=recv_sems.at[k], device_id=(x, y, 1 - c), device_id_type=MESH)
            cp.start()
            cps.append(cp)
        for k, (fx, fy) in enumerate(CHIP_FLIPS):
            peer_chip = 2 * _flip(x, fx) + _flip(y, fy)
            theirs = o_ref.at[peer_chip, :, pl.ds((1 - c) * h, h)]
            pltpu.make_async_remote_copy(src_ref=theirs, dst_ref=theirs, send_sem=send_sems.at[k],
                                         recv_sem=recv_sems.at[k], device_id=(x, y, 1 - c),
                                         device_id_type=MESH).wait_recv()
        for cp in cps:
            cp.wait_send()
        token[...] = jnp.zeros_like(token)

    return pl.pallas_call(
        body, name="gather_forward_w_in", in_specs=[HBM_SPEC],
        out_specs=[HBM_SPEC, pl.BlockSpec(memory_space=pltpu.VMEM)],
        out_shape=[jax.ShapeDtypeStruct(gathered.shape, gathered.dtype), TOKEN],
        scratch_shapes=[pltpu.SemaphoreType.DMA((3,)), pltpu.SemaphoreType.DMA((3,))],
        input_output_aliases={0: 0},
        compiler_params=pltpu.CompilerParams(has_side_effects=True))(gathered)


SEM_SPEC = pl.BlockSpec(memory_space=pltpu.SEMAPHORE)
ANY_SPEC = pl.BlockSpec(memory_space=pl.ANY)
DATAFLOW = pltpu.SideEffectType.DATAFLOW_SIDE_EFFECTING


def _in_hbm(a):
    return pltpu.with_memory_space_constraint(a, pltpu.HBM)


def _push_start(srcs, land_shapes, route, peers, name):
    n, npeer = len(srcs), len(peers)
    lands = [lax.empty(shp, s.dtype) for shp, s in zip(land_shapes, srcs)]

    def body(*refs):
        ins, lnd = refs[:n], refs[n:2 * n]
        send_sems, recv_sems = refs[2 * n], refs[2 * n + 1]
        token = refs[-1]
        x, y, c = _me()
        for t in range(n):
            for k, (fx, fy, fc) in enumerate(peers):
                src, dst = route(ins[t], lnd[t], k, x, y, c)
                pltpu.make_async_remote_copy(
                    src_ref=src, dst_ref=dst, send_sem=send_sems.at[npeer * t + k],
                    recv_sem=recv_sems.at[npeer * t + k],
                    device_id=(_flip(x, fx), _flip(y, fy), _flip(c, fc)), device_id_type=MESH).start()
        token[...] = jnp.zeros_like(token)

    bufs = [_in_hbm(a) for a in list(srcs) + lands]
    outs = pl.pallas_call(
        body, name=name,
        out_shape=(pltpu.SemaphoreType.DMA((npeer * n,)), pltpu.SemaphoreType.DMA((npeer * n,)),
                   *[pltpu.HBM(b.shape, b.dtype) for b in bufs], TOKEN),
        in_specs=[HBM_SPEC] * (2 * n),
        out_specs=(SEM_SPEC, SEM_SPEC, *[HBM_SPEC] * (2 * n), pl.BlockSpec(memory_space=pltpu.VMEM)),
        input_output_aliases={i: 2 + i for i in range(2 * n)},
        compiler_params=pltpu.CompilerParams(has_side_effects=DATAFLOW))(*bufs)
    return outs[0], outs[1], list(outs[2:2 + n]), list(outs[2 + n:2 + 2 * n]), outs[-1]


def _push_wait(send_sems, recv_sems, srcs, lands, after, route, peers, name):
    n, npeer = len(srcs), len(peers)

    def body(*refs):
        ins, lnd = refs[:n], refs[n:2 * n]
        ssem, rsem = refs[2 * n], refs[2 * n + 1]
        x, y, c = _me()
        for t in range(n):
            for k, (fx, fy, fc) in enumerate(peers):
                src, dst = route(ins[t], lnd[t], k, x, y, c)
                cp = pltpu.make_async_remote_copy(
                    src_ref=src, dst_ref=dst, send_sem=ssem.at[npeer * t + k], recv_sem=rsem.at[npeer * t + k],
                    device_id=(_flip(x, fx), _flip(y, fy), _flip(c, fc)), device_id_type=MESH)
                cp.wait_send()
                cp.wait_recv()

    bufs = list(srcs) + list(lands)
    outs = pl.pallas_call(
        body, name=name, out_shape=tuple(pltpu.HBM(b.shape, b.dtype) for b in bufs),
        in_specs=[HBM_SPEC] * (2 * n) + [SEM_SPEC, SEM_SPEC, ANY_SPEC], out_specs=tuple([HBM_SPEC] * (2 * n)),
        input_output_aliases={i: i for i in range(2 * n)},
        compiler_params=pltpu.CompilerParams(has_side_effects=DATAFLOW))(*bufs, send_sems, recv_sems, after)
    return list(outs[:n]), list(outs[n:])


OTHER_CHIPS = [(fx, fy, 0) for fx, fy in CHIP_FLIPS]
SIBLING = [(0, 0, 1)]


def _route_gather(src, land, k, x, y, c):
    return src, land.at[2 * x + y]


def _route_gather_half(src, land, k, x, y, c):
    h = src.shape[1] // 2
    return src.at[:, pl.ds(c * h, h)], land.at[2 * x + y, :, pl.ds(c * h, h)]


def _route_gather_half_wait(src, land, k, x, y, c):
    fx, fy = CHIP_FLIPS[k]
    h = src.shape[1] // 2
    return src.at[:, pl.ds(c * h, h)], land.at[2 * _flip(x, fx) + _flip(y, fy), :, pl.ds(c * h, h)]


def _route_gather_wait(src, land, k, x, y, c):
    fx, fy = CHIP_FLIPS[k]
    return src, land.at[2 * _flip(x, fx) + _flip(y, fy)]


def _route_scatter(src, land, k, x, y, c):
    fx, fy = CHIP_FLIPS[k]
    return src.at[2 * _flip(x, fx) + _flip(y, fy)], land.at[k]


def _route_exchange(src, land, k, x, y, c):
    h = land.shape[1]
    return src.at[:, pl.ds((1 - c) * h, h)], land


def _route_whole(src, land, k, x, y, c):
    return src, land


def _route_exchange_cols(src, land, k, x, y, c):
    h = land.shape[2]
    return src.at[:, :, pl.ds((1 - c) * h, h)], land


ALL_OTHERS = [((k >> 2) & 1, (k >> 1) & 1, k & 1) for k in range(1, 8)]


def _route_to_all(src, land, k, x, y, c):
    return src, land.at[4 * x + 2 * y + c]


def _route_to_all_wait(src, land, k, x, y, c):
    fx, fy, fc = ALL_OTHERS[k]
    return src, land.at[4 * _flip(x, fx) + 2 * _flip(y, fy) + _flip(c, fc)]


def _sum_devices(parts):
    def body(p_ref, o_ref):
        acc = p_ref[0]
        for d in range(1, 8):
            acc = acc + p_ref[d]
        o_ref[...] = acc

    vm = pl.BlockSpec(memory_space=pltpu.VMEM)
    return pl.pallas_call(body, name="allreduce_sum", in_specs=[vm], out_specs=vm,
                          out_shape=jax.ShapeDtypeStruct(parts.shape[1:], F32),
                          compiler_params=pltpu.CompilerParams(vmem_limit_bytes=VMEM_LIMIT))(parts)


def _grad_exchange_start(g4, tag, cols=False):
    land = (N_CHIPS, g4.shape[1], g4.shape[2] // 2) if cols else (N_CHIPS, g4.shape[1] // 2, g4.shape[2])
    route = _route_exchange_cols if cols else _route_exchange
    send_sems, recv_sems, srcs, lands, token = _push_start(
        [g4], [land], route, SIBLING, name="grad_exchange_start_" + tag)
    return (send_sems, recv_sems, srcs, lands, tag, cols), token


def _grad_scatter_start(state, pos, after):
    send_sems, recv_sems, srcs, lands, tag, cols = state
    route = _route_exchange_cols if cols else _route_exchange
    (g4,), (recv,) = _push_wait(send_sems, recv_sems, srcs, lands, after, route, SIBLING,
                                name="grad_exchange_wait_" + tag)
    return _grad_pair_scatter(g4, recv, pos, tag, cols)


def _grad_pair_scatter(g4, recv, pos, tag, cols=False):
    p16 = _sum_own_half(g4, recv, pos, name="grad_sum_pair_" + tag, cols=cols)
    send_sems, recv_sems, srcs, lands, token = _push_start(
        [p16], [(3,) + p16.shape[1:]], _route_scatter, OTHER_CHIPS, name="grad_scatter_start_" + tag)
    return (g4, recv, send_sems, recv_sems, srcs, lands, tag, cols), token


def _grad_sum_and_share(state, pos, after):
    g4, recv, send_sems, recv_sems, srcs, lands, tag, cols = state
    parts = _push_wait(send_sems, recv_sems, srcs, lands, after, _route_scatter, OTHER_CHIPS,
                       name="grad_scatter_wait_" + tag)[1][0]
    mine = _sum_chips(g4, recv, parts, pos, name="grad_sum_chips_" + tag, cols=cols)
    send_sems, recv_sems, srcs, lands, token = _push_start(
        [mine], [mine.shape], _route_whole, SIBLING, name="grad_share_start_" + tag)
    return (send_sems, recv_sems, srcs, lands, tag), token


def _grad_share_wait(state, after):
    send_sems, recv_sems, srcs, lands, tag = state
    (mine,), (theirs,) = _push_wait(send_sems, recv_sems, srcs, lands, after, _route_whole, SIBLING,
                                    name="grad_share_wait_" + tag)
    return mine, theirs


def _local_step(x, tgt, p, hooks):
    t = x.shape[0]
    tables = _rope_tables(t)
    sinks = p['sinks'].reshape(N_Q_HEADS)

    def told(name, value):
        return tuple(hooks.grad_ready(name, value))

    xn = _rmsnorm_fwd(x, p['norm_mix'], "norm_mix_fwd", deps=hooks.first_deps)
    w_in_t, w_in_dt, in_deps = hooks.weight_in(xn)
    proj = _matmul(xn, w_in_t, mode='nt', name="in_proj", n_limit=MAIN_WIDTH, deps=in_deps)
    dt_raw = _matmul(xn, w_in_dt, mode='nt', name="in_proj_dt")[:, :SSD_HEADS]
    ssd_conv_w, ffn_conv_w = hooks.conv_weights(proj)
    p = dict(p, ssd_conv_w=ssd_conv_w, ffn_conv_w=ffn_conv_w)
    attn = _attn_fwd(proj, sinks, tables)
    conv_b = p['ssd_conv_b']
    xbc, xbc_pre = _conv_silu_fwd(proj, p['ssd_conv_w'], conv_b, col0=O_XBC, width=CONV_CH, name="ssd_conv_fwd")
    sp = _ssd_params(dt_raw, p['dt_bias'].reshape(-1), p['a_log'].reshape(-1), p['ssd_d'].reshape(-1))
    y, states = _ssd_fwd(xbc, sp)
    mix = _mix_fwd(attn, y, proj, p['attn_out_norm'], p['ssd_norm'])
    w_out = hooks.weight('w_out', mix)
    h1 = _matmul(mix, w_out, mode='nn', name="out_proj", add=x)
    hn = _rmsnorm_fwd(h1, p['norm_ffn'], "norm_ffn_fwd")
    w_up = hooks.weight('w_up', hn)
    u0 = _matmul(hn, w_up, mode='nn', name="ffn_up", b_owner=True, tn=1408)
    a, u = _ffn_act_fwd(u0, p['ffn_conv_w'], p['ffn_conv_b'])
    w_down = hooks.weight('w_down', a)
    h2 = _matmul(a, w_down, mode='nn', name="ffn_down", add=h1, tk=2816)
    loss, dh2, dh2_16, g_norm_final = _final_loss(h2, p['norm_final'].reshape(1, D_MODEL), tgt)

    g = {}
    da = _matmul(dh2_16, w_down, mode='nt', name="ffn_down_dx", out_dtype=BF16, tn=1408)
    g['w_down'] = _matmul(a, dh2_16, mode='tn', name="ffn_down_dw", tm=1408)
    dep = told('w_down', g['w_down'])
    du0, dcw, dcb = _ffn_act_bwd(u0, u, p['ffn_conv_w'], da)
    g['ffn_conv_w'] = dcw.transpose(1, 0, 2).reshape(FFN_CONV, 2 * D_FF)
    g['ffn_conv_b'] = dcb.transpose(1, 0, 2).reshape(1, 2 * D_FF)
    g['w_up'] = _matmul(hn, du0, mode='tn', name="ffn_up_dw", deps=dep, b_halves=True, owner_major=True,
                        tn=1408)
    dep = told('w_up', g['w_up'])
    dhn = _matmul(du0, w_up, mode='nt', name="ffn_up_dx", out_dtype=BF16, deps=dep, a_halves=True,
                  b_owner=True, tk=2816)
    dh1, dh1_16, g['norm_ffn'] = _rmsnorm_bwd(h1, p['norm_ffn'], dhn, dh2, "norm_ffn_bwd")

    g['w_out'] = _matmul(mix, dh1_16, mode='tn', name="out_proj_dw")
    dep = told('w_out', g['w_out'])
    dmix = _matmul(dh1_16, w_out, mode='nt', name="out_proj_dx", out_dtype=BF16, deps=dep)
    dattn, dy, dz, g['attn_out_norm'], g['ssd_norm'] = _mix_bwd(dmix, attn, y, proj, p['attn_out_norm'],
                                                                p['ssd_norm'])
    dq, dk, dv, dsink = _attn_bwd(proj, sinks, tables, dattn)
    g['sinks'] = dsink[:, :, 0].reshape(1, N_Q_HEADS)
    dxs, dbm, dcm, ddt8, dpar = _ssd_bwd(xbc, xbc_pre, sp, states, dy)
    dpar = dpar[:, :, ::SSD_HEAD_DIM]
    g['dt_bias'] = dpar[:, 0, :].reshape(1, SSD_HEADS)
    g['a_log'] = dpar[:, 1, :].reshape(1, SSD_HEADS)
    g['ssd_d'] = dpar[:, 2, :].reshape(1, SSD_HEADS)
    dproj, g['ssd_conv_w'], g['ssd_conv_b'] = _ssd_conv_bwd(proj, p['ssd_conv_w'], dxs, dbm, dcm, dq, col0=O_XBC,
                                                            name="ssd_conv_bwd")
    for piece, col in ((dk, O_K), (dv, O_V), (dz, O_Z)):
        dproj = lax.dynamic_update_slice(dproj, piece, (0, col))
    ddt = ddt8.transpose(2, 0, 1).reshape(t, SSD_HEADS)
    ddt_pad = jnp.pad(ddt, ((0, 0), (0, LANES - SSD_HEADS))).astype(BF16)
    g['w_in'] = (_matmul(dproj, xn, mode='tn', name="in_proj_dw", m_rows=IN_PROJ_WIDTH),
                 _matmul(ddt_pad, xn, mode='tn', name="in_proj_dt_dw"))
    dep = told('w_in', g['w_in'])
    dxn = _matmul(dproj, w_in_t, mode='nn', name="in_proj_dx", out_dtype=BF16, extra=(ddt_pad, w_in_dt),
                  k_limit=MAIN_WIDTH, tk=2304, deps=dep)
    dep = told(None, dxn)
    dx, _, g['norm_mix'] = _rmsnorm_bwd(x, p['norm_mix'], dxn, dh1, "norm_mix_bwd", deps=dep)
    g['norm_final'] = g_norm_final
    return loss, dx, g


def _pack(arrs):
    flat = jnp.concatenate([a.reshape(-1) for a in arrs])
    n = flat.shape[0]
    rows = -(-n // LANES)
    rows = -(-rows // 8) * 8
    return jnp.pad(flat, (0, rows * LANES - n)).reshape(rows, LANES)


def _unpack(packed, shapes):
    flat = packed.reshape(-1)
    out, off = [], 0
    for s in shapes:
        n = 1
        for d in s:
            n *= d
        out.append(flat[off:off + n].reshape(s))
        off += n
    return out


class _StepHooks:
    def __init__(self, first_deps, weight_in, conv_weights, weight, grad_ready):
        self.first_deps = first_deps
        self.weight_in = weight_in
        self.conv_weights = conv_weights
        self.weight = weight
        self.grad_ready = grad_ready


def kernel(x, norm_mix, w_in, sinks, attn_out_norm, ssd_conv_w, ssd_conv_b, dt_bias, a_log, ssd_d, ssd_norm, w_out, norm_ffn, w_up, ffn_conv_w, ffn_conv_b, w_down, norm_final, loss_target, m_norm_mix, m_w_in, m_sinks, m_attn_out_norm, m_ssd_conv_w, m_ssd_conv_b, m_dt_bias, m_a_log, m_ssd_d, m_ssd_norm, m_w_out, m_norm_ffn, m_w_up, m_ffn_conv_w, m_ffn_conv_b, m_w_down, m_norm_final, v_norm_mix, v_w_in, v_sinks, v_attn_out_norm, v_ssd_conv_w, v_ssd_conv_b, v_dt_bias, v_a_log, v_ssd_d, v_ssd_norm, v_w_out, v_norm_ffn, v_w_up, v_ffn_conv_w, v_ffn_conv_b, v_w_down, v_norm_final):
    args = dict(locals())
    w = {n: args[n] for n in WEIGHTS}
    m = {n: args['m_' + n] for n in WEIGHTS}
    v = {n: args['v_' + n] for n in WEIGHTS}
    xi, yi, ci = _me()
    chip = 2 * xi + yi
    pos = jnp.stack([ci, chip]).astype(jnp.int32)

    conv_shard = _pack([ssd_conv_w[0], ffn_conv_w[0]])
    conv_gather = _push_start([conv_shard], [(N_CHIPS,) + conv_shard.shape], _route_gather, OTHER_CHIPS,
                              name="gather_start_conv")

    def conv_weights(after):
        send_sems, recv_sems, srcs, lands, _ = conv_gather
        (own,), (got,) = _push_wait(send_sems, recv_sems, srcs, lands, after, _route_gather_wait, OTHER_CHIPS,
                                    name="gather_wait_conv")
        whole = lax.dynamic_update_slice(got, own[None], (chip, 0, 0))
        per_chip = [_unpack(whole[j], [ssd_conv_w.shape[1:], ffn_conv_w.shape[1:]]) for j in range(N_CHIPS)]
        return (jnp.concatenate([pc[0] for pc in per_chip], axis=1),
                jnp.concatenate([pc[1] for pc in per_chip], axis=1))

    w_in_t, m_in_t, v_in_t = (jnp.transpose(a[0]) for a in (w_in, m_w_in, v_w_in))
    in_shard = (w_in_t + conv_gather[4][:1, :1]).astype(BF16)
    in_gather = _push_start([in_shard], [(N_CHIPS,) + in_shard.shape], _route_gather_half, OTHER_CHIPS,
                            name="gather_start_w_in")
    gathers = {}
    order = in_gather[4][:1, :1]
    for n, shard in (('w_out', w_out[0]), ('w_up', w_up[0]), ('w_down', w_down[0])):
        shard = (shard + order).astype(BF16)
        gathers[n] = _push_start([shard], [(N_CHIPS,) + shard.shape], _route_gather, OTHER_CHIPS,
                                 name="gather_start_" + n)
        order = gathers[n][4][:1, :1]

    def weight_in(after):
        send_sems, recv_sems, srcs, lands, _ = in_gather
        (own,), (got,) = _push_wait(send_sems, recv_sems, srcs, lands, after, _route_gather_half_wait, OTHER_CHIPS,
                                    name="gather_wait_w_in")
        got, _ = _forward_halves(got)
        full_in_t = lax.dynamic_update_slice(got, own[None], (chip, 0, 0)).reshape(IN_PROJ_WIDTH, D_MODEL)
        w_in_dt = jnp.pad(full_in_t[MAIN_WIDTH:], ((0, LANES - SSD_HEADS), (0, 0)))
        return full_in_t, w_in_dt, ()

    def weight(name, after):
        send_sems, recv_sems, srcs, lands, _ = gathers[name]
        (own,), (got,) = _push_wait(send_sems, recv_sems, srcs, lands, after, _route_gather_wait, OTHER_CHIPS,
                                    name="gather_wait_" + name)
        whole = lax.dynamic_update_slice(got, own[None], (chip, 0, 0))
        return whole if name == 'w_up' else whole.reshape(-1, D_MODEL)

    reductions, exchanging = {}, {}

    def flush(after):
        tokens = []
        for prev in list(exchanging):
            reductions[prev], token = _grad_scatter_start(exchanging.pop(prev), pos, after)
            tokens.append(token)
        return tokens

    def grad_ready(name, value):
        if name is None:
            return flush(value)
        if name == 'w_in':
            main, dtp = value
            value = lax.dynamic_update_slice(main, dtp[:SSD_HEADS], (MAIN_WIDTH, 0))
        g4 = value if value.ndim == 3 else value.reshape(N_CHIPS, -1, value.shape[1])
        tokens = flush(g4)
        exchanging[name], token = _grad_exchange_start(g4, name, cols=(name == 'w_in'))
        return tokens + [token]

    small = {
        'norm_mix': norm_mix, 'sinks': sinks, 'attn_out_norm': attn_out_norm,
        'ssd_conv_b': ssd_conv_b, 'dt_bias': dt_bias, 'a_log': a_log, 'ssd_d': ssd_d, 'ssd_norm': ssd_norm,
        'norm_ffn': norm_ffn, 'ffn_conv_b': ffn_conv_b, 'norm_final': norm_final,
    }
    loss, dx, g = _local_step(x[0], loss_target[0], small,
                              _StepHooks((gathers['w_down'][4],), weight_in, conv_weights, weight, grad_ready))

    small_names = [n for n in WEIGHTS if n not in BIG]
    small_g = [loss[:, :1]] + [g[n] for n in small_names]
    small_shapes = [(1, 1)] + [tuple(a.shape) for a in small_g[1:]]
    packed = _pack(small_g)
    spread = _push_start([packed], [(8,) + packed.shape], _route_to_all, ALL_OTHERS, name="allreduce_start")
    grads, deltas, new_m, new_v = {}, {}, {}, {}
    after = spread[4]
    shares = {}
    for n in ('w_down', 'w_up', 'w_out'):
        shares[n], after = _grad_sum_and_share(reductions[n], pos, after)
    for n in ('w_down', 'w_up', 'w_out', 'w_in'):
        if n == 'w_out':
            shares['w_in'], after = _grad_sum_and_share(reductions['w_in'], pos, after)
        mine, theirs = _grad_share_wait(shares[n], after)
        if n == 'w_in':
            outs = _adamw_halves(w_in_t, mine, theirs, m_in_t, v_in_t, pos, name="adamw_" + n, cols=True)
            after = outs[1]
            outs = [jnp.transpose(o) for o in outs]
        else:
            outs = _adamw_halves(w[n][0], mine, theirs, m[n][0], v[n][0], pos, name="adamw_" + n)
            after = outs[1]
        grads[n], deltas[n], new_m[n], new_v[n] = [o[None] for o in outs]
    (own,), (landed,) = _push_wait(spread[0], spread[1], spread[2], spread[3], after, _route_to_all_wait, ALL_OTHERS,
                                   name="allreduce_wait")
    landed = lax.dynamic_update_slice(landed, own[None], (4 * xi + 2 * yi + ci, 0, 0))
    red = _unpack(_sum_devices(landed), small_shapes)
    loss_out = red[0].reshape(())
    gsm = dict(zip(small_names, red[1:]))
    gsm['ssd_conv_w'] = lax.dynamic_slice(gsm['ssd_conv_w'], (0, chip * ssd_conv_w.shape[2]),
                                          (SSD_CONV, ssd_conv_w.shape[2]))
    gsm['ffn_conv_w'] = lax.dynamic_slice(gsm['ffn_conv_w'], (0, chip * ffn_conv_w.shape[2]),
                                          (FFN_CONV, ffn_conv_w.shape[2]))

    shapes = [tuple(w[n].shape) for n in small_names]
    gp = _pack([gsm[n] for n in small_names])
    d, m2, v2 = _adamw(_pack([w[n] for n in small_names]), gp, _pack([m[n] for n in small_names]),
                       _pack([v[n] for n in small_names]), name="adamw_small")
    for n, gg, dd, mm, vv in zip(small_names, _unpack(gp, shapes), _unpack(d, shapes), _unpack(m2, shapes),
                                 _unpack(v2, shapes)):
        grads[n], deltas[n], new_m[n], new_v[n] = gg, dd, mm, vv

    return (loss_out, dx[None], *[grads[n] for n in WEIGHTS], *[deltas[n] for n in WEIGHTS],
            *[new_m[n] for n in WEIGHTS], *[new_v[n] for n in WEIGHTS])
```
